```python
import jax, jax.numpy as jnp
from jax import lax
import numpy as np

D_MODEL = 1024
BATCH = 8
SEQ = 8192
DEPTH = 1

N_MEM = 256
BLOCK = 128
EPS = 1e-6
NEG = -1e30
ROPE_THETA = 500000.0

DIL_GROUPS = ((128, 1), (512, 4), (2048, 16))
N_DIL_GROUPS = 3
A_HEADS = 4
A_HEAD_DIM = 128
A_WIDTH = A_HEADS * A_HEAD_DIM
A_QKV = N_DIL_GROUPS * A_WIDTH
ROT_DIM = A_HEAD_DIM // 4

B_HEADS = 8
B_HEAD_DIM = 64
B_WIDTH = B_HEADS * B_HEAD_DIM

M_HEADS = 4
M_HEAD_DIM = 128
M_WIDTH = M_HEADS * M_HEAD_DIM

N_BRANCH = 3
IN_SIZES = (A_QKV, A_QKV, A_QKV, A_WIDTH,
            B_WIDTH, B_WIDTH, B_WIDTH, B_HEADS, B_WIDTH,
            M_WIDTH, M_WIDTH,
            N_BRANCH * D_MODEL)
IN_COLS = 3 * A_QKV + A_WIDTH + 4 * B_WIDTH + B_HEADS + 2 * M_WIDTH + N_BRANCH * D_MODEL

kernel_name = "hybrid_dilated_fox_memory_gated_block"


def rmsnorm(x, g):
    x32 = x.astype(jnp.float32)
    y = x32 * lax.rsqrt(jnp.mean(x32 * x32, axis=-1, keepdims=True) + EPS)
    return (y * g.astype(jnp.float32)).astype(x.dtype)


def rope_partial(x, pos):
    half = ROT_DIM // 2
    inv = ROPE_THETA ** (-jnp.arange(half, dtype=jnp.float32) / half)
    ang = pos.astype(jnp.float32)[..., None] * inv
    cos = jnp.cos(ang)[:, :, None, :]
    sin = jnp.sin(ang)[:, :, None, :]
    xr = x[..., :ROT_DIM].astype(jnp.float32)
    x1, x2 = xr[..., :half], xr[..., half:]
    rot = jnp.concatenate([x1 * cos - x2 * sin, x2 * cos + x1 * sin], axis=-1).astype(x.dtype)
    return jnp.concatenate([rot, x[..., ROT_DIM:]], axis=-1)


def dilated_window_attention(q, k, v, window, dilation):
    bsz, seq, nh, hd = q.shape
    L = seq // dilation
    steps = window // dilation
    Lp = -(-L // BLOCK) * BLOCK
    nb = Lp // BLOCK

    def to_classes(t):
        t = t.reshape(bsz, L, dilation, nh, hd).transpose(0, 2, 3, 1, 4)
        return jnp.pad(t, ((0, 0), (0, 0), (0, 0), (0, Lp - L), (0, 0)))

    def band(t):
        t = jnp.pad(t, ((0, 0), (0, 0), (0, 0), (BLOCK, 0), (0, 0)))
        t = t.reshape(bsz, dilation, nh, nb + 1, BLOCK, hd)
        return jnp.concatenate([t[:, :, :, :-1], t[:, :, :, 1:]], axis=4)

    qb = to_classes(q).reshape(bsz, dilation, nh, nb, BLOCK, hd)
    kb = band(to_classes(k))
    vb = band(to_classes(v))
    s = jnp.einsum('brhnqd,brhnkd->brhnqk', qb, kb).astype(jnp.float32) * (hd ** -0.5)
    qi = jnp.arange(BLOCK)[:, None]
    ki = jnp.arange(2 * BLOCK)[None, :]
    rel = BLOCK + qi - ki
    blk = jnp.arange(nb)[:, None, None]
    valid = (rel >= 0) & (rel <= steps) & (blk * BLOCK + ki - BLOCK >= 0)
    s = jnp.where(valid, s, NEG)
    m = jnp.max(s, axis=-1, keepdims=True)
    p = jnp.exp(s - m)
    den = jnp.sum(p, axis=-1)
    o = jnp.einsum('brhnqk,brhnkd->brhnqd', p.astype(v.dtype), vb).astype(jnp.float32) / den[..., None]
    lse = m[..., 0] + jnp.log(den)
    o = o.reshape(bsz, dilation, nh, Lp, hd)[:, :, :, :L].transpose(0, 3, 1, 2, 4).reshape(bsz, seq, nh, hd)
    lse = lse.reshape(bsz, dilation, nh, Lp)[..., :L].transpose(0, 3, 1, 2).reshape(bsz, seq, nh)
    return o, lse


def forgetting_attention(q, k, v, log_f):
    bsz, seq, nh, hd = q.shape
    nb = seq // BLOCK
    c = jnp.cumsum(log_f, axis=1).transpose(0, 2, 1)
    kt = k.transpose(0, 2, 1, 3)
    vt = v.transpose(0, 2, 1, 3)
    qb = q.transpose(0, 2, 1, 3).reshape(bsz, nh, nb, BLOCK, hd).transpose(2, 0, 1, 3, 4)
    cb = c.reshape(bsz, nh, nb, BLOCK).transpose(2, 0, 1, 3)
    kpos = jnp.arange(seq)

    def one_block(args):
        n, qn, cn = args
        s = jnp.einsum('bhqd,bhkd->bhqk', qn, kt).astype(jnp.float32) * (hd ** -0.5)
        s = s + cn[..., :, None] - c[:, :, None, :]
        qpos = n * BLOCK + jnp.arange(BLOCK)
        s = jnp.where(kpos[None, :] <= qpos[:, None], s, NEG)
        p = jax.nn.softmax(s, axis=-1)
        return jnp.einsum('bhqk,bhkd->bhqd', p.astype(vt.dtype), vt)

    o = lax.map(one_block, (jnp.arange(nb), qb, cb))
    return o.transpose(1, 0, 3, 2, 4).reshape(bsz, seq, nh, hd)


def memory_attention(q, mem_k, mem_v):
    s = jnp.einsum('bshd,bnhd->bhsn', q, mem_k).astype(jnp.float32) * (q.shape[-1] ** -0.5)
    p = jax.nn.softmax(s, axis=-1)
    return jnp.einsum('bhsn,bnhd->bshd', p.astype(mem_v.dtype), mem_v)


def _fwd_setup_inputs(seed: int = 0) -> dict:
    key = jax.random.key(seed)
    ks = jax.random.split(key, 16)
    f32 = jnp.float32
    x = jax.random.normal(ks[0], (BATCH, SEQ, D_MODEL), f32)
    mem = jax.random.normal(ks[1], (BATCH, N_MEM, D_MODEL), f32)
    start = jax.random.randint(ks[2], (BATCH, 1), 0, 4096, dtype=jnp.int32)
    positions = start + jnp.arange(SEQ, dtype=jnp.int32)[None, :]
    norm_pre_g = 1.0 + 0.05 * jax.random.normal(ks[3], (DEPTH, D_MODEL), f32)
    norm_post_g = 1.0 + 0.05 * jax.random.normal(ks[4], (DEPTH, D_MODEL), f32)
    norm_mem_g = 1.0 + 0.05 * jax.random.normal(ks[5], (DEPTH, D_MODEL), f32)
    w_in = jax.random.normal(ks[6], (DEPTH, D_MODEL, IN_COLS), f32) * D_MODEL ** -0.5
    b_forget = 3.0 + 0.5 * jax.random.normal(ks[7], (DEPTH, B_HEADS), f32)
    b_merge = 0.01 * jax.random.normal(ks[8], (DEPTH, N_BRANCH * D_MODEL), f32)
    w_mem_kv = jax.random.normal(ks[9], (DEPTH, D_MODEL, 2 * M_WIDTH), f32) * D_MODEL ** -0.5
    w_branch_a = jax.random.normal(ks[10], (DEPTH, A_WIDTH, D_MODEL), f32) * A_WIDTH ** -0.5
    w_branch_b = jax.random.normal(ks[11], (DEPTH, B_WIDTH, D_MODEL), f32) * B_WIDTH ** -0.5
    w_branch_m = jax.random.normal(ks[12], (DEPTH, M_WIDTH, D_MODEL), f32) * M_WIDTH ** -0.5
    w_out = jax.random.normal(ks[13], (DEPTH, D_MODEL, D_MODEL), f32) * D_MODEL ** -0.5
    return {"x": x, "mem": mem, "positions": positions,
            "norm_pre_g": norm_pre_g, "norm_post_g": norm_post_g, "norm_mem_g": norm_mem_g,
            "w_in": w_in, "b_forget": b_forget, "b_merge": b_merge, "w_mem_kv": w_mem_kv,
            "w_branch_a": w_branch_a, "w_branch_b": w_branch_b, "w_branch_m": w_branch_m,
            "w_out": w_out}


def _fwd_reference(x, mem, positions, norm_pre_g, norm_post_g, norm_mem_g, w_in, b_forget, b_merge,
              w_mem_kv, w_branch_a, w_branch_b, w_branch_m, w_out):
    bsz, seq, _ = x.shape
    split_at = [int(i) for i in np.cumsum(IN_SIZES)[:-1]]
    for layer in range(DEPTH):
        h = rmsnorm(x, norm_pre_g[layer])
        u = jnp.einsum('bsd,de->bse', h, w_in[layer])
        (qa, ka, va, za, qb, kb, vb, fb, zb, qm, zm, gl) = jnp.split(u, split_at, axis=-1)

        nh_a = N_DIL_GROUPS * A_HEADS
        qa = rope_partial(qa.reshape(bsz, seq, nh_a, A_HEAD_DIM), positions)
        ka = rope_partial(ka.reshape(bsz, seq, nh_a, A_HEAD_DIM), positions)
        va = va.reshape(bsz, seq, nh_a, A_HEAD_DIM)
        outs, lses = [], []
        for g, (window, dilation) in enumerate(DIL_GROUPS):
            sl = slice(g * A_HEADS, (g + 1) * A_HEADS)
            o_g, lse_g = dilated_window_attention(qa[:, :, sl], ka[:, :, sl], va[:, :, sl], window, dilation)
            outs.append(o_g)
            lses.append(lse_g)
        wgt = jax.nn.softmax(jnp.stack(lses, axis=0), axis=0)
        y_a = jnp.sum(wgt[..., None] * jnp.stack(outs, axis=0), axis=0)
        y_a = y_a.astype(x.dtype).reshape(bsz, seq, A_WIDTH) * jax.nn.silu(za)

        log_f = jax.nn.log_sigmoid((fb + b_forget[layer]).astype(jnp.float32))
        y_b = forgetting_attention(qb.reshape(bsz, seq, B_HEADS, B_HEAD_DIM),
                                   kb.reshape(bsz, seq, B_HEADS, B_HEAD_DIM),
                                   vb.reshape(bsz, seq, B_HEADS, B_HEAD_DIM), log_f)
        y_b = y_b.reshape(bsz, seq, B_WIDTH) * jax.nn.silu(zb)

        mkv = jnp.einsum('bnd,de->bne', rmsnorm(mem, norm_mem_g[layer]), w_mem_kv[layer])
        mk, mv = jnp.split(mkv, 2, axis=-1)
        y_m = memory_attention(qm.reshape(bsz, seq, M_HEADS, M_HEAD_DIM),
                               mk.reshape(bsz, N_MEM, M_HEADS, M_HEAD_DIM),
                               mv.reshape(bsz, N_MEM, M_HEADS, M_HEAD_DIM))
        y_m = y_m.reshape(bsz, seq, M_WIDTH) * jax.nn.silu(zm)

        gates = jax.nn.sigmoid(gl + b_merge[layer]).reshape(bsz, seq, N_BRANCH, D_MODEL)
        merged = (gates[:, :, 0] * jnp.einsum('bse,ed->bsd', y_a, w_branch_a[layer])
                  + gates[:, :, 1] * jnp.einsum('bse,ed->bsd', y_b, w_branch_b[layer])
                  + gates[:, :, 2] * jnp.einsum('bse,ed->bsd', y_m, w_branch_m[layer]))
        out = jnp.einsum('bsd,de->bse', merged, w_out[layer])
        x = x + rmsnorm(out, norm_post_g[layer])
    return x


import jax as _jax
import jax.numpy as _jnp

TWIN_FORMAT = 'train_step'
FWD_PARAMS = ['x', 'mem', 'positions', 'norm_pre_g', 'norm_post_g', 'norm_mem_g', 'w_in', 'b_forget', 'b_merge', 'w_mem_kv', 'w_branch_a', 'w_branch_b', 'w_branch_m', 'w_out']
TWIN_WEIGHTS = ['norm_pre_g', 'norm_post_g', 'norm_mem_g', 'w_in', 'b_forget', 'b_merge', 'w_mem_kv', 'w_branch_a', 'w_branch_b', 'w_branch_m', 'w_out']
TWIN_DIFF_INPUT = 'x'
TWIN_INPUTS = ['x', 'mem', 'positions', 'norm_pre_g', 'norm_post_g', 'norm_mem_g', 'w_in', 'b_forget', 'b_merge', 'w_mem_kv', 'w_branch_a', 'w_branch_b', 'w_branch_m', 'w_out', 'loss_target', 'm_norm_pre_g', 'm_norm_post_g', 'm_norm_mem_g', 'm_w_in', 'm_b_forget', 'm_b_merge', 'm_w_mem_kv', 'm_w_branch_a', 'm_w_branch_b', 'm_w_branch_m', 'm_w_out', 'v_norm_pre_g', 'v_norm_post_g', 'v_norm_mem_g', 'v_w_in', 'v_b_forget', 'v_b_merge', 'v_w_mem_kv', 'v_w_branch_a', 'v_w_branch_b', 'v_w_branch_m', 'v_w_out']
TWIN_OUTPUTS = ['loss', 'grad_x', 'grad_norm_pre_g', 'grad_norm_post_g', 'grad_norm_mem_g', 'grad_w_in', 'grad_b_forget', 'grad_b_merge', 'grad_w_mem_kv', 'grad_w_branch_a', 'grad_w_branch_b', 'grad_w_branch_m', 'grad_w_out', 'delta_norm_pre_g', 'delta_norm_post_g', 'delta_norm_mem_g', 'delta_w_in', 'delta_b_forget', 'delta_b_merge', 'delta_w_mem_kv', 'delta_w_branch_a', 'delta_w_branch_b', 'delta_w_branch_m', 'delta_w_out', 'new_m_norm_pre_g', 'new_m_norm_post_g', 'new_m_norm_mem_g', 'new_m_w_in', 'new_m_b_forget', 'new_m_b_merge', 'new_m_w_mem_kv', 'new_m_w_branch_a', 'new_m_w_branch_b', 'new_m_w_branch_m', 'new_m_w_out', 'new_v_norm_pre_g', 'new_v_norm_post_g', 'new_v_norm_mem_g', 'new_v_w_in', 'new_v_b_forget', 'new_v_b_merge', 'new_v_w_mem_kv', 'new_v_w_branch_a', 'new_v_w_branch_b', 'new_v_w_branch_m', 'new_v_w_out']
TWIN_LEAF_KINDS = {'loss': 'loss', 'grad_x': 'grad_x', 'grad_norm_pre_g': 'grad_w', 'grad_norm_post_g': 'grad_w', 'grad_norm_mem_g': 'grad_w', 'grad_w_in': 'grad_w', 'grad_b_forget': 'grad_w', 'grad_b_merge': 'grad_w', 'grad_w_mem_kv': 'grad_w', 'grad_w_branch_a': 'grad_w', 'grad_w_branch_b': 'grad_w', 'grad_w_branch_m': 'grad_w', 'grad_w_out': 'grad_w', 'delta_norm_pre_g': 'delta_w', 'delta_norm_post_g': 'delta_w', 'delta_norm_mem_g': 'delta_w', 'delta_w_in': 'delta_w', 'delta_b_forget': 'delta_w', 'delta_b_merge': 'delta_w', 'delta_w_mem_kv': 'delta_w', 'delta_w_branch_a': 'delta_w', 'delta_w_branch_b': 'delta_w', 'delta_w_branch_m': 'delta_w', 'delta_w_out': 'delta_w', 'new_m_norm_pre_g': 'new_m', 'new_m_norm_post_g': 'new_m', 'new_m_norm_mem_g': 'new_m', 'new_m_w_in': 'new_m', 'new_m_b_forget': 'new_m', 'new_m_b_merge': 'new_m', 'new_m_w_mem_kv': 'new_m', 'new_m_w_branch_a': 'new_m', 'new_m_w_branch_b': 'new_m', 'new_m_w_branch_m': 'new_m', 'new_m_w_out': 'new_m', 'new_v_norm_pre_g': 'new_v', 'new_v_norm_post_g': 'new_v', 'new_v_norm_mem_g': 'new_v', 'new_v_w_in': 'new_v', 'new_v_b_forget': 'new_v', 'new_v_b_merge': 'new_v', 'new_v_w_mem_kv': 'new_v', 'new_v_w_branch_a': 'new_v', 'new_v_w_branch_b': 'new_v', 'new_v_w_branch_m': 'new_v', 'new_v_w_out': 'new_v'}


def _forward(args):
    return _fwd_reference(*[args[k] for k in FWD_PARAMS])


def _output_shape():
    out = _jax.eval_shape(lambda: _forward(_fwd_setup_inputs(0)))
    return out.shape, out.dtype

N_MICROBATCH = 1
ADAM_LR = 0.001
ADAM_B1 = 0.9
ADAM_B2 = 0.999
ADAM_EPS = 1e-08
ADAM_WD = 0.01
ADAM_STEP = 10
PER_EXAMPLE_BATCH_AXIS = {'x': 0, 'mem': 0, 'positions': 0, 'loss_target': 0}
SHARED_INPUTS = []
_WEIGHT_DTYPES = {'norm_pre_g': _jnp.float32, 'norm_post_g': _jnp.float32, 'norm_mem_g': _jnp.float32, 'w_in': _jnp.float32, 'b_forget': _jnp.float32, 'b_merge': _jnp.float32, 'w_mem_kv': _jnp.float32, 'w_branch_a': _jnp.float32, 'w_branch_b': _jnp.float32, 'w_branch_m': _jnp.float32, 'w_out': _jnp.float32}
MOMENT_SCALE = {'norm_pre_g': 7.339919e-01, 'norm_post_g': 6.386230e+01, 'norm_mem_g': 1.829470e-01, 'w_in': 2.043542e-01, 'b_forget': 3.967482e+00, 'b_merge': 9.444482e-02, 'w_mem_kv': 1.806060e-01, 'w_branch_a': 1.491596e-01, 'w_branch_b': 3.654743e-01, 'w_branch_m': 1.402015e-01, 'w_out': 3.878346e-01}


def _to_microbatches(a, axis):
    t = _jnp.moveaxis(a, axis, 0)
    t = t.reshape((N_MICROBATCH, t.shape[0] // N_MICROBATCH) + t.shape[1:])
    return _jnp.moveaxis(t, 1, axis + 1)


def setup_inputs(seed: int = 0) -> dict:
    inp = _fwd_setup_inputs(seed)
    key = _jax.random.fold_in(_jax.random.key(seed), 7919)
    shape, _ = _output_shape()
    out = dict(inp)
    out["loss_target"] = _jax.random.normal(_jax.random.fold_in(key, 0), shape, _jnp.float32)
    for i, name in enumerate(TWIN_WEIGHTS):
        w = inp[name].astype(_jnp.float32)
        if MOMENT_SCALE is None:
            s = _jnp.sqrt(_jnp.mean(_jnp.square(w)) + 1e-30)
        else:
            s = MOMENT_SCALE[name]
        km, kv = _jax.random.split(_jax.random.fold_in(key, i + 1))
        out[name] = w
        out["m_" + name] = s * _jax.random.normal(km, w.shape, _jnp.float32)
        out["v_" + name] = (s * s) * _jax.random.uniform(kv, w.shape, _jnp.float32, 0.5, 1.5)
    if N_MICROBATCH > 1:
        for name, axis in PER_EXAMPLE_BATCH_AXIS.items():
            out[name] = _to_microbatches(out[name], axis)
    return {'x': out['x'], 'mem': out['mem'], 'positions': out['positions'], 'norm_pre_g': out['norm_pre_g'], 'norm_post_g': out['norm_post_g'], 'norm_mem_g': out['norm_mem_g'], 'w_in': out['w_in'], 'b_forget': out['b_forget'], 'b_merge': out['b_merge'], 'w_mem_kv': out['w_mem_kv'], 'w_branch_a': out['w_branch_a'], 'w_branch_b': out['w_branch_b'], 'w_branch_m': out['w_branch_m'], 'w_out': out['w_out'], 'loss_target': out['loss_target'], 'm_norm_pre_g': out['m_norm_pre_g'], 'm_norm_post_g': out['m_norm_post_g'], 'm_norm_mem_g': out['m_norm_mem_g'], 'm_w_in': out['m_w_in'], 'm_b_forget': out['m_b_forget'], 'm_b_merge': out['m_b_merge'], 'm_w_mem_kv': out['m_w_mem_kv'], 'm_w_branch_a': out['m_w_branch_a'], 'm_w_branch_b': out['m_w_branch_b'], 'm_w_branch_m': out['m_w_branch_m'], 'm_w_out': out['m_w_out'], 'v_norm_pre_g': out['v_norm_pre_g'], 'v_norm_post_g': out['v_norm_post_g'], 'v_norm_mem_g': out['v_norm_mem_g'], 'v_w_in': out['v_w_in'], 'v_b_forget': out['v_b_forget'], 'v_b_merge': out['v_b_merge'], 'v_w_mem_kv': out['v_w_mem_kv'], 'v_w_branch_a': out['v_w_branch_a'], 'v_w_branch_b': out['v_w_branch_b'], 'v_w_branch_m': out['v_w_branch_m'], 'v_w_out': out['v_w_out']}


def _loss(weights, diff, rest, loss_target):
    with _jax.named_scope("forward"):
        args = {**rest, TWIN_DIFF_INPUT: diff, **{k: w.astype(_WEIGHT_DTYPES[k]) for k, w in weights.items()}}
        y = _forward(args)
    with _jax.named_scope("loss_head"):
        err = _jnp.square(y.astype(_jnp.float32) - loss_target)
        return 0.5 * _jnp.sum(_jnp.mean(err, axis=-1)) if err.ndim else 0.5 * err


def _adamw(w, g, m, v):
    m = ADAM_B1 * m + (1.0 - ADAM_B1) * g
    v = ADAM_B2 * v + (1.0 - ADAM_B2) * _jnp.square(g)
    m_hat = m / (1.0 - ADAM_B1 ** ADAM_STEP)
    v_hat = v / (1.0 - ADAM_B2 ** ADAM_STEP)
    delta = -ADAM_LR * (m_hat / (_jnp.sqrt(v_hat) + ADAM_EPS) + ADAM_WD * w)
    return delta, m, v


def reference(x, mem, positions, norm_pre_g, norm_post_g, norm_mem_g, w_in, b_forget, b_merge, w_mem_kv, w_branch_a, w_branch_b, w_branch_m, w_out, loss_target, m_norm_pre_g, m_norm_post_g, m_norm_mem_g, m_w_in, m_b_forget, m_b_merge, m_w_mem_kv, m_w_branch_a, m_w_branch_b, m_w_branch_m, m_w_out, v_norm_pre_g, v_norm_post_g, v_norm_mem_g, v_w_in, v_b_forget, v_b_merge, v_w_mem_kv, v_w_branch_a, v_w_branch_b, v_w_branch_m, v_w_out):
    given = dict(x=x, mem=mem, positions=positions, norm_pre_g=norm_pre_g, norm_post_g=norm_post_g, norm_mem_g=norm_mem_g, w_in=w_in, b_forget=b_forget, b_merge=b_merge, w_mem_kv=w_mem_kv, w_branch_a=w_branch_a, w_branch_b=w_branch_b, w_branch_m=w_branch_m, w_out=w_out, loss_target=loss_target, m_norm_pre_g=m_norm_pre_g, m_norm_post_g=m_norm_post_g, m_norm_mem_g=m_norm_mem_g, m_w_in=m_w_in, m_b_forget=m_b_forget, m_b_merge=m_b_merge, m_w_mem_kv=m_w_mem_kv, m_w_branch_a=m_w_branch_a, m_w_branch_b=m_w_branch_b, m_w_branch_m=m_w_branch_m, m_w_out=m_w_out, v_norm_pre_g=v_norm_pre_g, v_norm_post_g=v_norm_post_g, v_norm_mem_g=v_norm_mem_g, v_w_in=v_w_in, v_b_forget=v_b_forget, v_b_merge=v_b_merge, v_w_mem_kv=v_w_mem_kv, v_w_branch_a=v_w_branch_a, v_w_branch_b=v_w_branch_b, v_w_branch_m=v_w_branch_m, v_w_out=v_w_out)
    weights = {n: given[n] for n in TWIN_WEIGHTS}
    shared = {n: given[n] for n in SHARED_INPUTS}
    per_example = {n: given[n] for n in ['x', 'mem', 'positions']}
    grad_fn = _jax.value_and_grad(_loss, argnums=(0, 1))

    def one_microbatch(ex, loss_target):
        ex = dict(ex)
        diff = ex.pop(TWIN_DIFF_INPUT)
        return grad_fn(weights, diff, {**shared, **ex}, loss_target)

    if N_MICROBATCH == 1:
        loss, (grad_w, grad_x) = one_microbatch(per_example, given["loss_target"])
    else:
        def body(carry, xs):
            loss_sum, grad_sum = carry
            l_k, (gw_k, gx_k) = one_microbatch(xs[0], xs[1])
            with _jax.named_scope("update"):
                return (loss_sum + l_k, _jax.tree.map(_jnp.add, grad_sum, gw_k)), gx_k

        init = (_jnp.zeros((), _jnp.float32), _jax.tree.map(_jnp.zeros_like, weights))
        (loss, grad_w), grad_x = _jax.lax.scan(body, init, (per_example, given["loss_target"]))
    with _jax.named_scope("update"):
        delta_w, new_m, new_v = {}, {}, {}
        for n in TWIN_WEIGHTS:
            delta_w[n], new_m[n], new_v[n] = _adamw(weights[n], grad_w[n], given["m_" + n], given["v_" + n])
    return (loss, grad_x, *[grad_w[n] for n in TWIN_WEIGHTS], *[delta_w[n] for n in TWIN_WEIGHTS],
            *[new_m[n] for n in TWIN_WEIGHTS], *[new_v[n] for n in TWIN_WEIGHTS])
```

```python
import functools

import jax
import jax.numpy as jnp
from jax import lax
from jax.experimental import pallas as pl
from jax.experimental.pallas import tpu as pltpu

F32 = jnp.float32
BF16 = jnp.bfloat16
MESH = pl.DeviceIdType.MESH

D_MODEL = 1024
N_MEM = 256
EPS = 1e-6
NEG = -1e30
ROPE_THETA = 500000.0
ROT_DIM = 32
HD = 128
A_GROUP = 512
DILATIONS = (1, 4, 16)
BAND = 128
B_HEADS = 8
B_HD = 64
N_CHIPS = 4
N_DEV = 8

C_QA, C_KA, C_VA, C_ZA = 0, 1536, 3072, 4608
C_QB, C_KB, C_VB, C_ZB = 5120, 5632, 6144, 6656
C_QM, C_ZM, C_GL = 7168, 7680, 8192
N_MAIN = 11264
FB_ORIG = 6656
IN_COLS = 11272
SHARD_COLS = IN_COLS // N_CHIPS

ADAM_LR, ADAM_B1, ADAM_B2, ADAM_EPS, ADAM_WD, ADAM_STEP = 0.001, 0.9, 0.999, 1e-08, 0.01, 10

VMEM_LIMIT_V7X = 56 * 1024 * 1024

NT = (((1,), (1,)), ((), ()))
NN = (((1,), (0,)), ((), ()))
TN = (((0,), (0,)), ((), ()))


def _params(sem):
    return pltpu.CompilerParams(dimension_semantics=sem, vmem_limit_bytes=VMEM_LIMIT_V7X)


def _dot(a, b, dn=NN):
    return lax.dot_general(a, b, dn, preferred_element_type=F32)


def _sig(z):
    return 1.0 / (1.0 + jnp.exp(-z))


def _rows(name, fn, row_ins, bc_ins, outs, reds=(), tm=512):
    arrs, specs = [], []
    for r in row_ins:
        arr, w, cb = r if isinstance(r, tuple) else (r, r.shape[1], 0)
        arrs.append(arr)
        specs.append(pl.BlockSpec((tm, w), functools.partial(lambda i, cb: (i, cb), cb=cb)))
    for b in bc_ins:
        arrs.append(b)
        specs.append(pl.BlockSpec(b.shape, lambda i: (0, 0)))
    s = arrs[0].shape[0]
    tm = min(tm, s)
    n_in, n_out = len(arrs), len(outs)

    def body(*refs):
        vals = fn(*[r[...] for r in refs[:n_in]])
        if not isinstance(vals, (tuple, list)):
            vals = (vals,)
        for r, v in zip(refs[n_in:n_in + n_out], vals[:n_out]):
            r[...] = v.astype(r.dtype)
        if reds:
            red_refs = refs[n_in + n_out:]

            @pl.when(pl.program_id(0) == 0)
            def _():
                for r in red_refs:
                    r[...] = jnp.zeros_like(r)

            for r, v in zip(red_refs, vals[n_out:]):
                r[...] += v

    out_shape = [jax.ShapeDtypeStruct((s, c), dt) for c, dt in outs]
    out_shape += [jax.ShapeDtypeStruct((1, c), F32) for c in reds]
    out_specs = [pl.BlockSpec((tm, c), lambda i: (i, 0)) for c, _ in outs]
    out_specs += [pl.BlockSpec((1, c), lambda i: (0, 0)) for c in reds]
    res = pl.pallas_call(
        body, name=name, grid=(s // tm,), in_specs=specs, out_specs=out_specs, out_shape=out_shape,
        compiler_params=_params(("arbitrary",) if reds else ("parallel",)),
    )(*arrs)
    return res


def _mm(name, a, b, mode, out_dtype, tm=1024, tn=1024, tk=1024):
    if mode == "nn":
        (m, k), (_, n) = a.shape, b.shape
    elif mode == "nt":
        (m, k), (n, _) = a.shape, b.shape
    else:
        (k, m), (_, n) = a.shape, b.shape
    tm, tn, tk = min(tm, m), min(tn, n), min(tk, k)
    nk = k // tk
    dn = {"nn": NN, "nt": NT, "tn": TN}[mode]

    def body(a_ref, b_ref, o_ref, *acc):
        part = _dot(a_ref[...].astype(BF16), b_ref[...].astype(BF16), dn)
        if nk == 1:
            o_ref[...] = part.astype(o_ref.dtype)
        else:
            kk = pl.program_id(2)

            @pl.when(kk == 0)
            def _():
                acc[0][...] = part

            @pl.when(kk > 0)
            def _():
                acc[0][...] += part

            @pl.when(kk == nk - 1)
            def _():
                o_ref[...] = acc[0][...].astype(o_ref.dtype)

    a_spec = (pl.BlockSpec((tk, tm), lambda i, j, kk: (kk, i)) if mode == "tn"
              else pl.BlockSpec((tm, tk), lambda i, j, kk: (i, kk)))
    b_spec = (pl.BlockSpec((tn, tk), lambda i, j, kk: (j, kk)) if mode == "nt"
              else pl.BlockSpec((tk, tn), lambda i, j, kk: (kk, j)))
    return pl.pallas_call(
        body, name=name, grid=(m // tm, n // tn, nk), in_specs=[a_spec, b_spec],
        out_specs=pl.BlockSpec((tm, tn), lambda i, j, kk: (i, j)),
        out_shape=jax.ShapeDtypeStruct((m, n), out_dtype),
        scratch_shapes=[pltpu.VMEM((tm, tn), F32)] if nk > 1 else [],
        compiler_params=_params(("parallel", "parallel", "arbitrary")),
    )(a, b)


def _rms_fwd(name, x, g):
    def fn(xv, gv):
        r = lax.rsqrt(jnp.mean(xv * xv, axis=-1, keepdims=True) + EPS)
        return (xv * r * gv,)
    return _rows(name, fn, [x], [g], [(x.shape[1], BF16)], tm=min(512, x.shape[0]))[0]


def _rope_tables(pos, inv):
    ang = pos.astype(F32) * inv
    lane = lax.broadcasted_iota(jnp.int32, ang.shape, 1)
    c = jnp.where(lane < ROT_DIM, jnp.cos(ang), 1.0)
    sn = jnp.sin(ang)
    sg = jnp.where(lane < ROT_DIM // 2, -sn, jnp.where(lane < ROT_DIM, sn, 0.0))
    return c, sg, lane


def _rope_apply(x, c, sg, lane):
    outs = []
    for h in range(x.shape[1] // HD):
        xh = x[:, h * HD:(h + 1) * HD].astype(F32)
        swap = jnp.where(lane < ROT_DIM // 2, pltpu.roll(xh, HD - ROT_DIM // 2, 1),
                         pltpu.roll(xh, ROT_DIM // 2, 1))
        outs.append(xh * c + swap * sg)
    return jnp.concatenate(outs, axis=1)


def _rope_fwd(u, pos, inv):
    def fn(q, k, p, iv):
        c, sg, lane = _rope_tables(p, iv)
        return _rope_apply(q, c, sg, lane), _rope_apply(k, c, sg, lane)
    return _rows("rope_fwd", fn, [(u, 1536, 0), (u, 1536, 1), pos], [inv],
                 [(1536, BF16), (1536, BF16)], tm=256)


def _rope_bwd(dq, dk, pos, inv):
    def fn(q, k, p, iv):
        c, sg, lane = _rope_tables(p, iv)
        return _rope_apply(q, c, -sg, lane), _rope_apply(k, c, -sg, lane)
    return _rows("rope_bwd", fn, [dq, dk, pos], [inv], [(1536, BF16), (1536, BF16)], tm=256)


def _lane_pack(cols, like):
    lane = lax.broadcasted_iota(jnp.int32, like, 1)
    out = jnp.zeros(like, F32)
    for h, cvec in enumerate(cols):
        out = jnp.where(lane == h, cvec, out)
    return out


def _band_specs(l, d, tq):
    nsb = tq // BAND
    nblk = l // BAND
    cur = pl.BlockSpec((tq, A_GROUP), lambda r, i: (i, r))
    prev = pl.BlockSpec((BAND, A_GROUP), lambda r, i: (jnp.maximum(i * nsb - 1, 0), r))
    nxt = pl.BlockSpec((BAND, A_GROUP), lambda r, i: (jnp.minimum((i + 1) * nsb, nblk - 1), r))
    st_cur = pl.BlockSpec((tq, HD), lambda r, i: (i, r))
    st_nxt = pl.BlockSpec((BAND, HD), lambda r, i: (jnp.minimum((i + 1) * nsb, nblk - 1), r))
    return nsb, cur, prev, nxt, st_cur, st_nxt


def _band_mask_q(i, first_tile):
    qr = lax.broadcasted_iota(jnp.int32, (BAND, 2 * BAND), 0)
    kc = lax.broadcasted_iota(jnp.int32, (BAND, 2 * BAND), 1)
    in_prev = (kc < BAND) & (kc >= qr)
    in_cur = (kc >= BAND) & (kc - BAND <= qr)
    if i == 0:
        in_prev = in_prev & jnp.logical_not(first_tile)
    return in_prev | in_cur


def _band_mask_k(j, nsb, last_tile):
    qr = lax.broadcasted_iota(jnp.int32, (2 * BAND, BAND), 0)
    kc = lax.broadcasted_iota(jnp.int32, (2 * BAND, BAND), 1)
    same = (qr < BAND) & (kc <= qr)
    nxt = (qr >= BAND) & (kc >= qr - BAND)
    if j == nsb - 1:
        nxt = nxt & jnp.logical_not(last_tile)
    return same | nxt


def _band_fwd(name, q, k, v, d):
    l = q.shape[0]
    tq = min(512, l)
    nsb, cur, prev, _, st_cur, _ = _band_specs(l, d, tq)
    scale = HD ** -0.5

    def body(q_ref, kc_ref, kp_ref, vc_ref, vp_ref, o_ref, lse_ref):
        first = pl.program_id(1) == 0
        for i in range(nsb):
            lses = []
            mask = _band_mask_q(i, first)
            for h in range(4):
                cs = slice(h * HD, (h + 1) * HD)
                qv = q_ref[i * BAND:(i + 1) * BAND, cs]
                if i == 0:
                    kk = jnp.concatenate([kp_ref[:, cs], kc_ref[0:BAND, cs]], axis=0)
                    vv = jnp.concatenate([vp_ref[:, cs], vc_ref[0:BAND, cs]], axis=0)
                else:
                    kk = kc_ref[(i - 1) * BAND:(i + 1) * BAND, cs]
                    vv = vc_ref[(i - 1) * BAND:(i + 1) * BAND, cs]
                s = jnp.where(mask, _dot(qv, kk, NT) * scale, NEG)
                m = jnp.max(s, axis=-1, keepdims=True)
                p = jnp.exp(s - m)
                den = jnp.sum(p, axis=-1, keepdims=True)
                o_ref[i * BAND:(i + 1) * BAND, cs] = _dot(p.astype(BF16), vv) / den
                lses.append(m + jnp.log(den))
            lse_ref[i * BAND:(i + 1) * BAND, :] = _lane_pack(lses, (BAND, HD))

    return pl.pallas_call(
        body, name=name, grid=(d, l // tq), in_specs=[cur, cur, prev, cur, prev],
        out_specs=[cur, st_cur],
        out_shape=[jax.ShapeDtypeStruct((l, d * A_GROUP), F32), jax.ShapeDtypeStruct((l, d * HD), F32)],
        compiler_params=_params(("parallel", "parallel")),
    )(q, k, k, v, v)


def _band_dq(name, q, k, v, dy, lse, delta, d):
    l = q.shape[0]
    tq = min(512, l)
    nsb, cur, prev, _, st_cur, _ = _band_specs(l, d, tq)
    scale = HD ** -0.5

    def body(q_ref, kc_ref, kp_ref, vc_ref, vp_ref, dy_ref, lse_ref, dl_ref, dq_ref):
        first = pl.program_id(1) == 0
        for i in range(nsb):
            mask = _band_mask_q(i, first)
            rs = slice(i * BAND, (i + 1) * BAND)
            for h in range(4):
                cs = slice(h * HD, (h + 1) * HD)
                if i == 0:
                    kk = jnp.concatenate([kp_ref[:, cs], kc_ref[0:BAND, cs]], axis=0)
                    vv = jnp.concatenate([vp_ref[:, cs], vc_ref[0:BAND, cs]], axis=0)
                else:
                    kk = kc_ref[(i - 1) * BAND:(i + 1) * BAND, cs]
                    vv = vc_ref[(i - 1) * BAND:(i + 1) * BAND, cs]
                s = jnp.where(mask, _dot(q_ref[rs, cs], kk, NT) * scale, NEG)
                p = jnp.exp(s - lse_ref[rs, h:h + 1])
                dp = _dot(dy_ref[rs, cs], vv, NT)
                ds = p * (dp - dl_ref[rs, h:h + 1])
                dq_ref[rs, cs] = (_dot(ds.astype(BF16), kk) * scale).astype(dq_ref.dtype)

    return pl.pallas_call(
        body, name=name, grid=(d, l // tq),
        in_specs=[cur, cur, prev, cur, prev, cur, st_cur, st_cur], out_specs=cur,
        out_shape=jax.ShapeDtypeStruct((l, d * A_GROUP), BF16),
        compiler_params=_params(("parallel", "parallel")),
    )(q, k, k, v, v, dy, lse, delta)


def _band_dkv(name, q, k, v, dy, lse, delta, d):
    l = q.shape[0]
    tq = min(512, l)
    nsb, cur, _, nxt, st_cur, st_nxt = _band_specs(l, d, tq)
    scale = HD ** -0.5
    ntile = l // tq

    def body(k_ref, v_ref, qc_ref, qn_ref, dyc_ref, dyn_ref, lc_ref, ln_ref, dc_ref, dn_ref,
             dk_ref, dv_ref):
        last = pl.program_id(1) == ntile - 1

        def win(c_ref, n_ref, j, cs):
            if j == nsb - 1:
                return jnp.concatenate([c_ref[j * BAND:(j + 1) * BAND, cs], n_ref[:, cs]], axis=0)
            return c_ref[j * BAND:(j + 2) * BAND, cs]

        for j in range(nsb):
            mask = _band_mask_k(j, nsb, last)
            rs = slice(j * BAND, (j + 1) * BAND)
            for h in range(4):
                cs = slice(h * HD, (h + 1) * HD)
                hs = slice(h, h + 1)
                qw = win(qc_ref, qn_ref, j, cs)
                dyw = win(dyc_ref, dyn_ref, j, cs)
                s = jnp.where(mask, _dot(qw, k_ref[rs, cs], NT) * scale, NEG)
                p = jnp.exp(s - win(lc_ref, ln_ref, j, hs))
                dp = _dot(dyw, v_ref[rs, cs], NT)
                ds = p * (dp - win(dc_ref, dn_ref, j, hs))
                dv_ref[rs, cs] = _dot(p.astype(BF16), dyw, TN).astype(dv_ref.dtype)
                dk_ref[rs, cs] = (_dot(ds.astype(BF16), qw, TN) * scale).astype(dk_ref.dtype)

    shp = jax.ShapeDtypeStruct((l, d * A_GROUP), BF16)
    return pl.pallas_call(
        body, name=name, grid=(d, ntile),
        in_specs=[cur, cur, cur, nxt, cur, nxt, st_cur, st_nxt, st_cur, st_nxt],
        out_specs=[cur, cur], out_shape=[shp, shp],
        compiler_params=_params(("parallel", "parallel")),
    )(k, v, q, q, dy, dy, lse, lse, delta, delta)


def _split3(x):
    hi = x.astype(BF16)
    r1 = x - hi.astype(F32)
    mid = r1.astype(BF16)
    lo = (r1 - mid.astype(F32)).astype(BF16)
    return hi, mid, lo


def _fox_prep(z, b):
    h, s = z.shape
    blk = min(512, s)

    def body(z_ref, b_ref, c_ref):
        r = lax.broadcasted_iota(jnp.int32, (blk, blk), 0)
        cidx = lax.broadcasted_iota(jnp.int32, (blk, blk), 1)
        tri = (r <= cidx).astype(BF16)
        carry = jnp.zeros((h, 1), F32)
        for t in range(s // blk):
            zz = z_ref[:, t * blk:(t + 1) * blk] + b_ref[...]
            lf = jnp.minimum(zz, 0.0) - jnp.log(1.0 + jnp.exp(-jnp.abs(zz)))
            hi, mid, lo = _split3(lf)
            cs = _dot(hi, tri) + _dot(mid, tri) + _dot(lo, tri) + carry
            c_ref[:, t * blk:(t + 1) * blk] = cs
            carry = cs[:, blk - 1:blk]

    return pl.pallas_call(body, name="fox_prep", out_shape=jax.ShapeDtypeStruct((h, s), F32))(z, b)


def _fox_prep_bwd(dck, dcq, z, b):
    h, s = z.shape
    blk = min(512, s)

    def body(dck_ref, dcq_ref, z_ref, b_ref, dz_ref, db_ref):
        r = lax.broadcasted_iota(jnp.int32, (blk, blk), 0)
        cidx = lax.broadcasted_iota(jnp.int32, (blk, blk), 1)
        tri = (r >= cidx).astype(BF16)
        carry = jnp.zeros((h, 1), F32)
        tot = jnp.zeros((h, 1), F32)
        for t in reversed(range(s // blk)):
            hi, mid, lo = _split3(dck_ref[:, t * blk:(t + 1) * blk] + dcq_ref[:, t * blk:(t + 1) * blk])
            rc = _dot(hi, tri) + _dot(mid, tri) + _dot(lo, tri) + carry
            carry = rc[:, 0:1]
            zz = z_ref[:, t * blk:(t + 1) * blk] + b_ref[...]
            dz = rc * _sig(-zz)
            dz_ref[:, t * blk:(t + 1) * blk] = dz
            tot = tot + jnp.sum(dz, axis=-1, keepdims=True)
        db_ref[...] = tot

    return pl.pallas_call(
        body, name="fox_prep_bwd",
        out_shape=[jax.ShapeDtypeStruct((h, s), F32), jax.ShapeDtypeStruct((h, 1), F32)])(dck, dcq, z, b)


def _fox_fwd(q, k, v, crow, t):
    h, s, dh = q.shape
    nt = s // t
    scale = dh ** -0.5

    def body(q_ref, k_ref, v_ref, c_ref, o_ref, lse_ref):
        i = pl.program_id(1)
        qv = q_ref[...]
        row = lax.broadcasted_iota(jnp.int32, (t, t), 0)
        col = lax.broadcasted_iota(jnp.int32, (t, t), 1)

        def tile(j, carry, diag):
            m, lsum, acc = carry
            j0 = pl.multiple_of(j * t, t)
            s_ = _dot(qv, k_ref[pl.ds(j0, t), :], NT) * scale - c_ref[j]
            if diag:
                s_ = jnp.where(row >= col, s_, NEG)
            m2 = jnp.maximum(m, jnp.max(s_, axis=-1, keepdims=True))
            a = jnp.exp(m - m2)
            p = jnp.exp(s_ - m2)
            return (m2, a * lsum + jnp.sum(p, axis=-1, keepdims=True),
                    a * acc + _dot(p.astype(BF16), v_ref[pl.ds(j0, t), :]))

        init = (jnp.full((t, 1), NEG, F32), jnp.zeros((t, 1), F32), jnp.zeros((t, dh), F32))
        carry = lax.fori_loop(0, i, lambda j, c: tile(j, c, False), init)
        m, lsum, acc = tile(i, carry, True)
        o_ref[...] = (acc / lsum).astype(o_ref.dtype)
        lse_ref[...] = m + jnp.log(lsum)

    return pl.pallas_call(
        body, name="fox_fwd", grid=(h, nt),
        in_specs=[pl.BlockSpec((None, t, dh), lambda hh, i: (hh, i, 0)),
                  pl.BlockSpec((None, s, dh), lambda hh, i: (hh, 0, 0)),
                  pl.BlockSpec((None, s, dh), lambda hh, i: (hh, 0, 0)),
                  pl.BlockSpec((None, nt, 1, t), lambda hh, i: (hh, 0, 0, 0))],
        out_specs=[pl.BlockSpec((None, t, dh), lambda hh, i: (hh, i, 0)),
                   pl.BlockSpec((None, t, 1), lambda hh, i: (hh, i, 0))],
        out_shape=[jax.ShapeDtypeStruct((h, s, dh), BF16), jax.ShapeDtypeStruct((h, s, 1), F32)],
        compiler_params=_params(("parallel", "parallel")),
    )(q, k, v, crow)


def _fox_bwd(q, k, v, do, ccol, lse_row, delta_row, t):
    h, s, dh = q.shape
    nt = s // t
    scale = dh ** -0.5

    def body(q_ref, do_ref, lse_ref, dl_ref, k_ref, v_ref, c_ref, dq_ref, dk_ref, dv_ref, dc_ref, dcq_ref):
        j = pl.program_id(1)

        @pl.when(j == 0)
        def _():
            dq_ref[...] = jnp.zeros_like(dq_ref)
            dcq_ref[...] = jnp.zeros_like(dcq_ref)

        kk, vv, cc = k_ref[...], v_ref[...], c_ref[...]
        krow = lax.broadcasted_iota(jnp.int32, (t, t), 0)
        qcol = lax.broadcasted_iota(jnp.int32, (t, t), 1)

        def tile(i, carry, diag):
            dk, dv, dc = carry
            i0 = pl.multiple_of(i * t, t)
            qi, doi = q_ref[pl.ds(i0, t), :], do_ref[pl.ds(i0, t), :]
            st = _dot(kk, qi, NT) * scale - cc - lse_ref[i]
            if diag:
                st = jnp.where(krow <= qcol, st, NEG)
            pt = jnp.exp(st)
            dst = pt * (_dot(vv, doi, NT) - dl_ref[i])
            dsb = dst.astype(BF16)
            dq_ref[pl.ds(i0, t), :] += _dot(dsb, kk, TN) * scale
            dcq_ref[i] += jnp.sum(dst, axis=0, keepdims=True)
            return (dk + _dot(dsb, qi), dv + _dot(pt.astype(BF16), doi),
                    dc - jnp.sum(dst, axis=-1, keepdims=True))

        init = (jnp.zeros((t, dh), F32), jnp.zeros((t, dh), F32), jnp.zeros((t, 1), F32))
        carry = tile(j, init, True)
        dk, dv, dc = lax.fori_loop(j + 1, nt, lambda i, c: tile(i, c, False), carry)
        dk_ref[...] = (dk * scale).astype(dk_ref.dtype)
        dv_ref[...] = dv.astype(dv_ref.dtype)
        dc_ref[...] = dc

    full = pl.BlockSpec((None, s, dh), lambda hh, j: (hh, 0, 0))
    rowst = pl.BlockSpec((None, nt, 1, t), lambda hh, j: (hh, 0, 0, 0))
    tl = pl.BlockSpec((None, t, dh), lambda hh, j: (hh, j, 0))
    col = pl.BlockSpec((None, t, 1), lambda hh, j: (hh, j, 0))
    return pl.pallas_call(
        body, name="fox_bwd", grid=(h, nt),
        in_specs=[full, full, rowst, rowst, tl, tl, col],
        out_specs=[full, tl, tl, col, rowst],
        out_shape=[jax.ShapeDtypeStruct((h, s, dh), F32), jax.ShapeDtypeStruct((h, s, dh), BF16),
                   jax.ShapeDtypeStruct((h, s, dh), BF16), jax.ShapeDtypeStruct((h, s, 1), F32),
                   jax.ShapeDtypeStruct((h, nt, 1, t), F32)],
        compiler_params=_params(("parallel", "arbitrary")),
    )(q, do, lse_row, delta_row, k, v, ccol)


def _mem_fwd(u, mkv, tq=512):
    s = u.shape[0]
    scale = HD ** -0.5

    def body(q_ref, mk_ref, mv_ref, o_ref, lse_ref):
        lses = []
        for h in range(4):
            cs = slice(h * HD, (h + 1) * HD)
            sc = _dot(q_ref[:, cs], mk_ref[:, cs], NT) * scale
            m = jnp.max(sc, axis=-1, keepdims=True)
            p = jnp.exp(sc - m)
            den = jnp.sum(p, axis=-1, keepdims=True)
            o_ref[:, cs] = (_dot(p.astype(BF16), mv_ref[:, cs]) / den).astype(o_ref.dtype)
            lses.append(m + jnp.log(den))
        lse_ref[...] = _lane_pack(lses, (tq, HD))

    return pl.pallas_call(
        body, name="mem_fwd", grid=(s // tq,),
        in_specs=[pl.BlockSpec((tq, 512), lambda i: (i, C_QM // 512)),
                  pl.BlockSpec((N_MEM, 512), lambda i: (0, 0)),
                  pl.BlockSpec((N_MEM, 512), lambda i: (0, 1))],
        out_specs=[pl.BlockSpec((tq, 512), lambda i: (i, 0)), pl.BlockSpec((tq, HD), lambda i: (i, 0))],
        out_shape=[jax.ShapeDtypeStruct((s, 512), BF16), jax.ShapeDtypeStruct((s, HD), F32)],
        compiler_params=_params(("parallel",)),
    )(u, mkv, mkv)


def _mem_bwd(u, mkv, o, do, lse, tq=512):
    s = u.shape[0]
    scale = HD ** -0.5

    def body(q_ref, mk_ref, mv_ref, o_ref, do_ref, lse_ref, dq_ref, dmk_ref, dmv_ref):
        @pl.when(pl.program_id(0) == 0)
        def _():
            dmk_ref[...] = jnp.zeros_like(dmk_ref)
            dmv_ref[...] = jnp.zeros_like(dmv_ref)

        for h in range(4):
            cs = slice(h * HD, (h + 1) * HD)
            qv, dov = q_ref[:, cs], do_ref[:, cs]
            sc = _dot(qv, mk_ref[:, cs], NT) * scale
            p = jnp.exp(sc - lse_ref[:, h:h + 1])
            delta = jnp.sum(dov.astype(F32) * o_ref[:, cs].astype(F32), axis=-1, keepdims=True)
            ds = p * (_dot(dov, mv_ref[:, cs], NT) - delta)
            dsb = ds.astype(BF16)
            dq_ref[:, cs] = (_dot(dsb, mk_ref[:, cs]) * scale).astype(dq_ref.dtype)
            dmk_ref[:, cs] += _dot(dsb, qv, TN) * scale
            dmv_ref[:, cs] += _dot(p.astype(BF16), dov, TN)

    row = pl.BlockSpec((tq, 512), lambda i: (i, 0))
    acc = pl.BlockSpec((N_MEM, 512), lambda i: (0, 0))
    return pl.pallas_call(
        body, name="mem_bwd", grid=(s // tq,),
        in_specs=[pl.BlockSpec((tq, 512), lambda i: (i, C_QM // 512)),
                  pl.BlockSpec((N_MEM, 512), lambda i: (0, 0)),
                  pl.BlockSpec((N_MEM, 512), lambda i: (0, 1)),
                  row, row, pl.BlockSpec((tq, HD), lambda i: (i, 0))],
        out_specs=[row, acc, acc],
        out_shape=[jax.ShapeDtypeStruct((s, 512), BF16), jax.ShapeDtypeStruct((N_MEM, 512), F32),
                   jax.ShapeDtypeStruct((N_MEM, 512), F32)],
        compiler_params=_params(("arbitrary",)),
    )(u, mkv, mkv, o, do, lse)


def _heads_major(a, col0):
    s = a.shape[0]
    return a[:, col0:col0 + 512].reshape(s, B_HEADS, B_HD).transpose(1, 0, 2)


def _token_major(a):
    h, s, dh = a.shape
    return a.transpose(1, 0, 2).reshape(s, h * dh)


def _class_view(a, d):
    s, c = a.shape
    return a.reshape(s // d, d * c)


def _local_step(x, mem, pos, target, g_pre, g_post, g_mem, w_main, w_fb, b_forget, b_merge,
                w_mem_kv, w_ba, w_bb, w_bm, w_out):
    s = x.shape[0]
    t_fox = min(512, s)
    nt = s // t_fox
    half = ROT_DIM // 2
    inv = ROPE_THETA ** (-jnp.arange(half, dtype=F32) / half)
    inv128 = jnp.concatenate([inv, inv, jnp.zeros((HD - ROT_DIM,), F32)]).reshape(1, HD)

    h = _rms_fwd("norm_pre", x, g_pre)
    u = _mm("proj_in", h, w_main, "nn", BF16, tn=512)
    ufb = _mm("proj_fb", h, w_fb, "nn", F32)
    memn = _rms_fwd("norm_mem", mem, g_mem)
    mkv = _mm("proj_mem", memn, w_mem_kv, "nn", BF16)

    q_rot, k_rot = _rope_fwd(u, pos, inv128)
    os_, lses = [], []
    views = []
    for g, d in enumerate(DILATIONS):
        qv = _class_view(q_rot[:, g * A_GROUP:(g + 1) * A_GROUP], d)
        kv = _class_view(k_rot[:, g * A_GROUP:(g + 1) * A_GROUP], d)
        vv = _class_view(u[:, C_VA + g * A_GROUP:C_VA + (g + 1) * A_GROUP], d)
        views.append((qv, kv, vv))
        o_g, lse_g = _band_fwd("band_fwd%d" % g, qv, kv, vv, d)
        os_.append(o_g.reshape(s, A_GROUP))
        lses.append(lse_g.reshape(s, HD))

    def merge_a(o1, o2, o3, l1, l2, l3, za):
        ys, tots = [], []
        for hh in range(4):
            cs, hs = slice(hh * HD, (hh + 1) * HD), slice(hh, hh + 1)
            mx = jnp.maximum(jnp.maximum(l1[:, hs], l2[:, hs]), l3[:, hs])
            e1, e2, e3 = jnp.exp(l1[:, hs] - mx), jnp.exp(l2[:, hs] - mx), jnp.exp(l3[:, hs] - mx)
            den = e1 + e2 + e3
            ys.append((e1 * o1[:, cs] + e2 * o2[:, cs] + e3 * o3[:, cs]) / den)
            tots.append(mx + jnp.log(den))
        y = jnp.concatenate(ys, axis=1)
        zf = za.astype(F32)
        return y, y * (zf * _sig(zf)), _lane_pack(tots, l1.shape)

    y_a, yg_a, lse_a = _rows("merge_a", merge_a, os_ + lses + [(u, 512, C_ZA // 512)], [],
                             [(512, BF16), (512, BF16), (HD, F32)])

    zrow = ufb[:, :B_HEADS].T
    c = _fox_prep(zrow, b_forget.reshape(B_HEADS, 1))
    crow = c.reshape(B_HEADS, nt, 1, t_fox)
    qb, kb, vb = _heads_major(u, C_QB), _heads_major(u, C_KB), _heads_major(u, C_VB)
    ob, lse_b = _fox_fwd(qb, kb, vb, crow, t_fox)
    y_b = _token_major(ob)

    y_m, lse_m = _mem_fwd(u, mkv)

    def gate(y, z):
        zf = z.astype(F32)
        return (y.astype(F32) * (zf * _sig(zf)),)

    yg_b = _rows("gate_b", gate, [y_b, (u, 512, C_ZB // 512)], [], [(512, BF16)])[0]
    yg_m = _rows("gate_m", gate, [y_m, (u, 512, C_ZM // 512)], [], [(512, BF16)])[0]

    br_a = _mm("branch_a", yg_a, w_ba, "nn", BF16)
    br_b = _mm("branch_b", yg_b, w_bb, "nn", BF16)
    br_m = _mm("branch_m", yg_m, w_bm, "nn", BF16)
    gl = [(u, 1024, C_GL // 1024 + i) for i in range(3)]
    bm3 = b_merge.reshape(3, D_MODEL)

    def merge(g0, g1, g2, b0, b1, b2, bm):
        tot = 0.0
        for i, (gv, bv) in enumerate(((g0, b0), (g1, b1), (g2, b2))):
            tot = tot + _sig(gv.astype(F32) + bm[i:i + 1, :]) * bv.astype(F32)
        return (tot,)

    merged = _rows("merge_gates", merge, gl + [br_a, br_b, br_m], [bm3], [(D_MODEL, BF16)])[0]
    out = _mm("proj_out", merged, w_out, "nn", F32)

    def tail(xv, ov, tv, gv):
        r = lax.rsqrt(jnp.mean(ov * ov, axis=-1, keepdims=True) + EPS)
        n = ov * r
        err = xv + n * gv - tv
        dy = err * (1.0 / D_MODEL)
        dn = dy * gv
        dout = r * (dn - n * jnp.mean(dn * n, axis=-1, keepdims=True))
        return (dy, dout, jnp.sum(0.5 * err * err * (1.0 / D_MODEL), axis=0, keepdims=True),
                jnp.sum(dy * n, axis=0, keepdims=True))

    dy, dout, loss_lanes, g_post_grad = _rows(
        "tail", tail, [x, out, target], [g_post], [(D_MODEL, F32), (D_MODEL, BF16)],
        reds=[D_MODEL, D_MODEL], tm=256)

    dmerged = _mm("d_merged", dout, w_out, "nt", BF16)
    gw_out = _mm("g_w_out", merged, dout, "tn", F32)

    def merge_bwd(dm, g0, g1, g2, b0, b1, b2, bm):
        dmf = dm.astype(F32)
        dbs, dgs, sums = [], [], []
        for i, (gv, bv) in enumerate(((g0, b0), (g1, b1), (g2, b2))):
            sg = _sig(gv.astype(F32) + bm[i:i + 1, :])
            dbs.append(dmf * sg)
            dg = dmf * bv.astype(F32) * sg * (1.0 - sg)
            dgs.append(dg)
            sums.append(jnp.sum(dg, axis=0, keepdims=True))
        return tuple(dbs + dgs + sums)

    res = _rows("merge_bwd", merge_bwd, [dmerged] + gl + [br_a, br_b, br_m], [bm3],
                [(D_MODEL, BF16)] * 6, reds=[D_MODEL] * 3, tm=256)
    dbr, dgl, g_bmerge = res[0:3], res[3:6], jnp.concatenate(res[6:9], axis=1)

    dyg, gw_branch = [], []
    for nm, dbv, wv, ygv in (("a", dbr[0], w_ba, yg_a), ("b", dbr[1], w_bb, yg_b), ("m", dbr[2], w_bm, yg_m)):
        dyg.append(_mm("d_yg_" + nm, dbv, wv, "nt", BF16))
        gw_branch.append(_mm("g_w_branch_" + nm, ygv, dbv, "tn", F32))

    def gate_bwd(dg, y, z):
        dgf, yf, zf = dg.astype(F32), y.astype(F32), z.astype(F32)
        sg = _sig(zf)
        return dgf * (zf * sg), dgf * yf * (sg * (1.0 + zf * (1.0 - sg)))

    def gate_bwd_a(dg, y, z):
        dyv, dz = gate_bwd(dg, y, z)
        prod = dyv * y.astype(F32)
        dl = [jnp.sum(prod[:, hh * HD:(hh + 1) * HD], axis=-1, keepdims=True) for hh in range(4)]
        return dyv, dz, _lane_pack(dl, (dg.shape[0], HD))

    dy_a, dz_a, delta_a = _rows("gate_bwd_a", gate_bwd_a, [dyg[0], y_a, (u, 512, C_ZA // 512)], [],
                                [(512, BF16), (512, BF16), (HD, F32)])
    dy_b, dz_b = _rows("gate_bwd_b", gate_bwd, [dyg[1], y_b, (u, 512, C_ZB // 512)], [],
                       [(512, BF16), (512, BF16)])
    dy_m, dz_m = _rows("gate_bwd_m", gate_bwd, [dyg[2], y_m, (u, 512, C_ZM // 512)], [],
                       [(512, BF16), (512, BF16)])

    dq_m, dmk, dmv = _mem_bwd(u, mkv, y_m, dy_m, lse_m)
    dmkv = jnp.concatenate([dmk, dmv], axis=1)
    gw_mem_kv = _mm("g_w_mem_kv", memn, dmkv, "tn", F32)
    dmemn = _mm("d_memn", dmkv, w_mem_kv, "nt", F32)

    def mem_gain_grad(mv, dv):
        r = lax.rsqrt(jnp.mean(mv * mv, axis=-1, keepdims=True) + EPS)
        return (jnp.sum(dv * mv * r, axis=0, keepdims=True),)

    g_mem_grad = _rows("g_norm_mem", mem_gain_grad, [mem, dmemn], [], [], reds=[D_MODEL], tm=N_MEM)[0]

    dob = _heads_major(dy_b, 0)

    def fox_delta(a, b):
        return (jnp.sum(a.astype(F32) * b.astype(F32), axis=-1, keepdims=True),)

    delta_b = _rows("fox_delta", fox_delta, [dob.reshape(B_HEADS * s, B_HD), ob.reshape(B_HEADS * s, B_HD)],
                    [], [(1, F32)], tm=min(2048, s))[0]
    dqb, dkb, dvb, dck, dcq = _fox_bwd(qb, kb, vb, dob, c.reshape(B_HEADS, s, 1),
                                 lse_b.reshape(B_HEADS, nt, 1, t_fox),
                                 delta_b.reshape(B_HEADS, nt, 1, t_fox), t_fox)
    dzrow, g_bforget = _fox_prep_bwd(dck.reshape(B_HEADS, s), dcq.reshape(B_HEADS, s), zrow,
                                     b_forget.reshape(B_HEADS, 1))
    dfb = jnp.zeros((s, HD), BF16).at[:, :B_HEADS].set(dzrow.T.astype(BF16))

    dqs, dks, dvs = [], [], []
    for g, d in enumerate(DILATIONS):
        qv, kv, vv = views[g]
        dyv, lv, dlv = _class_view(dy_a, d), _class_view(lse_a, d), _class_view(delta_a, d)
        dqs.append(_band_dq("band_dq%d" % g, qv, kv, vv, dyv, lv, dlv, d).reshape(s, A_GROUP))
        dk_g, dv_g = _band_dkv("band_dkv%d" % g, qv, kv, vv, dyv, lv, dlv, d)
        dks.append(dk_g.reshape(s, A_GROUP))
        dvs.append(dv_g.reshape(s, A_GROUP))
    dqa, dka = _rope_bwd(jnp.concatenate(dqs, axis=1), jnp.concatenate(dks, axis=1), pos, inv128)

    du = jnp.concatenate(
        [dqa, dka] + dvs + [dz_a, _token_major(dqb).astype(BF16), _token_major(dkb), _token_major(dvb),
                            dz_b, dq_m, dz_m] + list(dgl), axis=1)

    gw_main = _mm("g_w_main", h, du, "tn", F32)
    gw_fb = _mm("g_w_fb", h, dfb, "tn", F32)
    dh_main = _mm("d_h", du, w_main, "nt", F32, tk=1024)
    dh_fb = _mm("d_h_fb", dfb, w_fb, "nt", F32)

    def pre_bwd(xv, d1, d2, dyv, gv):
        r = lax.rsqrt(jnp.mean(xv * xv, axis=-1, keepdims=True) + EPS)
        n = xv * r
        dhv = d1 + d2
        dn = dhv * gv
        dx = r * (dn - n * jnp.mean(dn * n, axis=-1, keepdims=True))
        return dyv + dx, jnp.sum(dhv * n, axis=0, keepdims=True)

    grad_x, g_pre_grad = _rows("norm_pre_bwd", pre_bwd, [x, dh_main, dh_fb, dy], [g_pre],
                               [(D_MODEL, F32)], reds=[D_MODEL], tm=256)

    gw_in = jnp.concatenate([gw_main[:, :FB_ORIG], gw_fb[:, :B_HEADS], gw_main[:, FB_ORIG:]], axis=1)
    grads = dict(norm_pre_g=g_pre_grad, norm_post_g=g_post_grad, norm_mem_g=g_mem_grad, w_in=gw_in,
                 b_forget=g_bforget.reshape(1, B_HEADS), b_merge=g_bmerge, w_mem_kv=gw_mem_kv,
                 w_branch_a=gw_branch[0], w_branch_b=gw_branch[1], w_branch_m=gw_branch[2], w_out=gw_out)
    return loss_lanes, grad_x, grads


HBM_SPEC = pl.BlockSpec(memory_space=pltpu.HBM)


def _place():
    x, y, c = lax.axis_index("x"), lax.axis_index("y"), lax.axis_index("c")
    chips = [(1 - x, y), (x, 1 - y), (1 - x, 1 - y)]
    return x, y, c, 2 * x + y, chips


def _gather_weights(packed):
    _, r, w = packed.shape

    def body(src, out, send_sems, recv_sems, local_sem):
        x, y, c, p, chips = _place()
        me, sib = (x, y, c), (x, y, 1 - c)

        def cp(k, chip, half, to, src_ref=None):
            dst = out.at[chip, half]
            return pltpu.make_async_remote_copy(
                src_ref=dst if src_ref is None else src_ref, dst_ref=dst, send_sem=send_sems.at[k],
                recv_sem=recv_sems.at[k], device_id=to, device_id_type=MESH)

        mine = pltpu.make_async_copy(src, out.at[p], local_sem)
        mine.start()
        first = [cp(j, p, c, (cx, cy, c), src_ref=src.at[c]) for j, (cx, cy) in enumerate(chips)]
        for f in first:
            f.start()
        passed = []
        for j, (cx, cy) in enumerate(chips):
            cp(j, 2 * cx + cy, c, me).wait_recv()
            fw = cp(3 + j, 2 * cx + cy, c, sib)
            fw.start()
            passed.append(fw)
        for j, (cx, cy) in enumerate(chips):
            cp(3 + j, 2 * cx + cy, 1 - c, me).wait_recv()
        for f in first + passed:
            f.wait_send()
        mine.wait()

    return pl.pallas_call(
        body, name="gather_weights", in_specs=[HBM_SPEC], out_specs=HBM_SPEC,
        out_shape=jax.ShapeDtypeStruct((N_CHIPS, 2, r, w), packed.dtype),
        scratch_shapes=[pltpu.SemaphoreType.DMA((6,)), pltpu.SemaphoreType.DMA((6,)), pltpu.SemaphoreType.DMA],
    )(packed)


def _swap_with_sibling(theirs):
    def body(src, out, send_sems, recv_sems):
        x, y, c, _, _ = _place()
        cps = [pltpu.make_async_remote_copy(
            src_ref=src.at[q], dst_ref=out.at[q], send_sem=send_sems.at[q], recv_sem=recv_sems.at[q],
            device_id=(x, y, 1 - c), device_id_type=MESH) for q in range(N_CHIPS)]
        for cpy in cps:
            cpy.start()
        for cpy in cps:
            cpy.wait()

    return pl.pallas_call(
        body, name="swap_with_sibling", in_specs=[HBM_SPEC], out_specs=HBM_SPEC,
        out_shape=jax.ShapeDtypeStruct(theirs.shape, theirs.dtype),
        scratch_shapes=[pltpu.SemaphoreType.DMA((N_CHIPS,)), pltpu.SemaphoreType.DMA((N_CHIPS,))],
    )(theirs)


def _scatter_to_owners(t):
    def body(src, out, send_sems, recv_sems, local_sem):
        x, y, c, p, chips = _place()
        mine = pltpu.make_async_copy(src.at[p], out.at[p], local_sem)
        mine.start()
        sends = []
        for j, (cx, cy) in enumerate(chips):
            cpy = pltpu.make_async_remote_copy(
                src_ref=src.at[2 * cx + cy], dst_ref=out.at[p], send_sem=send_sems.at[j],
                recv_sem=recv_sems.at[j], device_id=(cx, cy, c), device_id_type=MESH)
            cpy.start()
            sends.append(cpy)
        for j, (cx, cy) in enumerate(chips):
            pltpu.make_async_remote_copy(
                src_ref=src.at[2 * cx + cy], dst_ref=out.at[2 * cx + cy], send_sem=send_sems.at[j],
                recv_sem=recv_sems.at[j], device_id=(cx, cy, c), device_id_type=MESH).wait_recv()
        for cpy in sends:
            cpy.wait_send()
        mine.wait()

    return pl.pallas_call(
        body, name="scatter_to_owners", in_specs=[HBM_SPEC], out_specs=HBM_SPEC,
        out_shape=jax.ShapeDtypeStruct(t.shape, t.dtype),
        scratch_shapes=[pltpu.SemaphoreType.DMA((3,)), pltpu.SemaphoreType.DMA((3,)), pltpu.SemaphoreType.DMA],
    )(t)


def _share_with_sibling(f):
    def body(src, out, send_sem, recv_sem, local_sem):
        x, y, c, _, _ = _place()
        mine = pltpu.make_async_copy(src, out.at[c], local_sem)
        mine.start()
        cpy = pltpu.make_async_remote_copy(
            src_ref=src, dst_ref=out.at[c], send_sem=send_sem, recv_sem=recv_sem,
            device_id=(x, y, 1 - c), device_id_type=MESH)
        cpy.start()
        pltpu.make_async_remote_copy(
            src_ref=src, dst_ref=out.at[1 - c], send_sem=send_sem, recv_sem=recv_sem,
            device_id=(x, y, 1 - c), device_id_type=MESH).wait_recv()
        cpy.wait_send()
        mine.wait()

    return pl.pallas_call(
        body, name="share_with_sibling", in_specs=[HBM_SPEC], out_specs=HBM_SPEC,
        out_shape=jax.ShapeDtypeStruct((2,) + f.shape, f.dtype),
        scratch_shapes=[pltpu.SemaphoreType.DMA, pltpu.SemaphoreType.DMA, pltpu.SemaphoreType.DMA],
    )(f)


def _sum_small(v):
    def body(v_ref, out_ref, buf, send_sems, recv_sems):
        x, y, c, _, _ = _place()
        me = 4 * x + 2 * y + c
        buf[me] = v_ref[...]
        flips = [(dx, dy, dc) for dx in (0, 1) for dy in (0, 1) for dc in (0, 1)][1:]
        sends = []
        for k, (dx, dy, dc) in enumerate(flips):
            cpy = pltpu.make_async_remote_copy(
                src_ref=v_ref, dst_ref=buf.at[me], send_sem=send_sems.at[k], recv_sem=recv_sems.at[k],
                device_id=((x + dx) % 2, (y + dy) % 2, (c + dc) % 2), device_id_type=MESH)
            cpy.start()
            sends.append(cpy)
        for k, (dx, dy, dc) in enumerate(flips):
            px, py, pc = (x + dx) % 2, (y + dy) % 2, (c + dc) % 2
            pltpu.make_async_remote_copy(
                src_ref=v_ref, dst_ref=buf.at[4 * px + 2 * py + pc], send_sem=send_sems.at[k],
                recv_sem=recv_sems.at[k], device_id=(px, py, pc), device_id_type=MESH).wait_recv()
        for cpy in sends:
            cpy.wait_send()
        tot = buf[0]
        for i in range(1, N_DEV):
            tot = tot + buf[i]
        out_ref[...] = tot

    return pl.pallas_call(
        body, name="sum_small", out_shape=jax.ShapeDtypeStruct(v.shape, v.dtype),
        in_specs=[pl.BlockSpec(memory_space=pltpu.VMEM)], out_specs=pl.BlockSpec(memory_space=pltpu.VMEM),
        scratch_shapes=[pltpu.VMEM((N_DEV,) + v.shape, v.dtype), pltpu.SemaphoreType.DMA((N_DEV - 1,)),
                        pltpu.SemaphoreType.DMA((N_DEV - 1,))],
    )(v)


def _add_slabs(name, terms):
    arr0 = terms[0][0]
    n = arr0.shape[0] if terms[0][1] is None else 1
    _, r, w = arr0.shape
    tw = 256
    specs = []
    for _, slab in terms:
        if slab is None:
            specs.append(pl.BlockSpec((None, r, tw), lambda i, j: (i, 0, j)))
        else:
            specs.append(pl.BlockSpec((None, r, tw), functools.partial(lambda i, j, sl: (sl, 0, j), sl=slab)))

    def body(*refs):
        tot = refs[0][...]
        for rf in refs[1:-1]:
            tot = tot + rf[...]
        refs[-1][...] = tot

    return pl.pallas_call(
        body, name=name, grid=(n, w // tw), in_specs=specs,
        out_specs=pl.BlockSpec((None, r, tw), lambda i, j: (i, 0, j)),
        out_shape=jax.ShapeDtypeStruct((n, r, w), arr0.dtype),
        compiler_params=_params(("parallel", "parallel")),
    )(*[a for a, _ in terms])


def _adamw(name, w, g, m, v, tm):
    def fn(wv, gv, mv, vv):
        m2 = ADAM_B1 * mv + (1.0 - ADAM_B1) * gv
        v2 = ADAM_B2 * vv + (1.0 - ADAM_B2) * (gv * gv)
        m_hat = m2 / (1.0 - ADAM_B1 ** ADAM_STEP)
        v_hat = v2 / (1.0 - ADAM_B2 ** ADAM_STEP)
        return -ADAM_LR * (m_hat / (jnp.sqrt(v_hat) + ADAM_EPS) + ADAM_WD * wv), m2, v2
    c = w.shape[1]
    return _rows(name, fn, [w, g, m, v], [], [(c, F32)] * 3, tm=tm)


N_IN, N_KV, N_BR, N_OUT = D_MODEL * SHARD_COLS, 256 * 1024, 512 * 256, 256 * 1024
PACK_W = 1024
PACK_ROWS = (N_IN + N_KV + 3 * N_BR + N_OUT) // 2 // PACK_W


def _small_pack(pre, post, memg, bforget, bmerge):
    pad = jnp.zeros((1, D_MODEL - B_HEADS), F32)
    return jnp.concatenate([pre, post, memg, bmerge.reshape(3, D_MODEL),
                            jnp.concatenate([bforget, pad], axis=1), jnp.zeros((1, D_MODEL), F32)], axis=0)


def _small_unpack(s8):
    return dict(norm_pre_g=s8[0:1], norm_post_g=s8[1:2], norm_mem_g=s8[2:3],
                b_merge=s8[3:6].reshape(1, 3 * D_MODEL), b_forget=s8[6:7, :B_HEADS])


WEIGHTS = ("norm_pre_g", "norm_post_g", "norm_mem_g", "w_in", "b_forget", "b_merge", "w_mem_kv",
           "w_branch_a", "w_branch_b", "w_branch_m", "w_out")
SMALL = ("norm_pre_g", "norm_post_g", "norm_mem_g", "b_forget", "b_merge")


def kernel(x, mem, positions, norm_pre_g, norm_post_g, norm_mem_g, w_in, b_forget, b_merge, w_mem_kv, w_branch_a, w_branch_b, w_branch_m, w_out, loss_target, m_norm_pre_g, m_norm_post_g, m_norm_mem_g, m_w_in, m_b_forget, m_b_merge, m_w_mem_kv, m_w_branch_a, m_w_branch_b, m_w_branch_m, m_w_out, v_norm_pre_g, v_norm_post_g, v_norm_mem_g, v_w_in, v_b_forget, v_b_merge, v_w_mem_kv, v_w_branch_a, v_w_branch_b, v_w_branch_m, v_w_out):
    w = dict(norm_pre_g=norm_pre_g, norm_post_g=norm_post_g, norm_mem_g=norm_mem_g, w_in=w_in[0],
             b_forget=b_forget, b_merge=b_merge, w_mem_kv=w_mem_kv[0], w_branch_a=w_branch_a[0],
             w_branch_b=w_branch_b[0], w_branch_m=w_branch_m[0], w_out=w_out[0])
    mo = dict(norm_pre_g=m_norm_pre_g, norm_post_g=m_norm_post_g, norm_mem_g=m_norm_mem_g, w_in=m_w_in[0],
              b_forget=m_b_forget, b_merge=m_b_merge, w_mem_kv=m_w_mem_kv[0], w_branch_a=m_w_branch_a[0],
              w_branch_b=m_w_branch_b[0], w_branch_m=m_w_branch_m[0], w_out=m_w_out[0])
    vo = dict(norm_pre_g=v_norm_pre_g, norm_post_g=v_norm_post_g, norm_mem_g=v_norm_mem_g, w_in=v_w_in[0],
              b_forget=v_b_forget, b_merge=v_b_merge, w_mem_kv=v_w_mem_kv[0], w_branch_a=v_w_branch_a[0],
              w_branch_b=v_w_branch_b[0], w_branch_m=v_w_branch_m[0], w_out=v_w_out[0])
    s = x.shape[1]
    c = lax.axis_index("c")

    packed = jnp.concatenate([w[n].reshape(-1) for n in ("w_in", "w_mem_kv", "w_branch_a", "w_branch_b",
                                                         "w_branch_m", "w_out")]).astype(BF16)
    allw = _gather_weights(packed.reshape(2, PACK_ROWS, PACK_W)).reshape(N_CHIPS, -1)
    o = 0
    w_in_f = allw[:, o:o + N_IN].reshape(N_CHIPS, D_MODEL, SHARD_COLS).transpose(1, 0, 2).reshape(D_MODEL, IN_COLS)
    o += N_IN
    w_kv_f = allw[:, o:o + N_KV].reshape(D_MODEL, D_MODEL)
    o += N_KV
    w_br_f = []
    for _ in range(3):
        w_br_f.append(allw[:, o:o + N_BR].reshape(N_CHIPS, 512, 256).transpose(1, 0, 2).reshape(512, D_MODEL))
        o += N_BR
    w_out_f = allw[:, o:o + N_OUT].reshape(D_MODEL, D_MODEL)
    w_main = jnp.concatenate([w_in_f[:, :FB_ORIG], w_in_f[:, FB_ORIG + B_HEADS:]], axis=1)
    w_fb = jnp.concatenate([w_in_f[:, FB_ORIG:FB_ORIG + B_HEADS], jnp.zeros((D_MODEL, HD - B_HEADS), BF16)], axis=1)

    loss_lanes, grad_x, g = _local_step(
        x[0], mem[0], positions.reshape(s, 1), loss_target[0], norm_pre_g, norm_post_g, norm_mem_g,
        w_main, w_fb, b_forget, b_merge, w_kv_f, w_br_f[0], w_br_f[1], w_br_f[2], w_out_f)
    loss = lax.psum(jnp.sum(loss_lanes), ("x", "y", "c"))

    def by_rows(a, rows_half):
        return a.reshape(N_CHIPS, 2, rows_half * a.shape[1])

    def by_cols(a, cols):
        r2 = a.shape[0] // 2
        return a.reshape(2, r2, N_CHIPS, cols).transpose(2, 0, 1, 3).reshape(N_CHIPS, 2, r2 * cols)

    g4 = jnp.concatenate([by_cols(g["w_in"], SHARD_COLS), by_rows(g["w_mem_kv"], 128),
                          by_cols(g["w_branch_a"], 256), by_cols(g["w_branch_b"], 256),
                          by_cols(g["w_branch_m"], 256), by_rows(g["w_out"], 128)], axis=2)
    g4 = g4.reshape(N_CHIPS, 2, PACK_ROWS, PACK_W)
    mine = lax.dynamic_index_in_dim(g4, c, axis=1, keepdims=False)
    theirs = lax.dynamic_index_in_dim(g4, 1 - c, axis=1, keepdims=False)
    pair = _add_slabs("add_pair", [(mine, None), (_swap_with_sibling(theirs), None)])
    landed = _scatter_to_owners(pair)
    half = _add_slabs("add_chips", [(landed, q) for q in range(N_CHIPS)])
    red = _share_with_sibling(half[0]).reshape(2, -1)
    o = 0
    gs = {}
    for n, size, shape in (("w_in", N_IN, (D_MODEL, SHARD_COLS)), ("w_mem_kv", N_KV, (256, D_MODEL)),
                           ("w_branch_a", N_BR, (512, 256)), ("w_branch_b", N_BR, (512, 256)),
                           ("w_branch_m", N_BR, (512, 256)), ("w_out", N_OUT, (256, D_MODEL))):
        gs[n] = red[:, o:o + size // 2].reshape(shape)
        o += size // 2
    gs.update(_small_unpack(_sum_small(_small_pack(
        g["norm_pre_g"], g["norm_post_g"], g["norm_mem_g"], g["b_forget"], g["b_merge"]))))

    delta, new_m, new_v = {}, {}, {}
    for n, tm in (("w_in", 128), ("w_mem_kv", 256), ("w_branch_a", 512), ("w_branch_b", 512),
                  ("w_branch_m", 512), ("w_out", 256)):
        d_, m_, v_ = _adamw("adamw_" + n, w[n], gs[n], mo[n], vo[n], tm)
        delta[n], new_m[n], new_v[n] = d_[None], m_[None], v_[None]
        gs[n] = gs[n][None]
    packs = [_small_pack(*[t[n] for n in ("norm_pre_g", "norm_post_g", "norm_mem_g", "b_forget", "b_merge")])
             for t in (w, gs, mo, vo)]
    for res, store in zip(_adamw("adamw_small", *packs, 8), (delta, new_m, new_v)):
        store.update(_small_unpack(res))

    return (loss, grad_x[None], *[gs[n] for n in WEIGHTS], *[delta[n] for n in WEIGHTS],
            *[new_m[n] for n in WEIGHTS], *[new_v[n] for n in WEIGHTS])
```

```python
import functools

import jax
import jax.numpy as jnp
from jax import lax
from jax.experimental import pallas as pl
from jax.experimental.pallas import tpu as pltpu

F32 = jnp.float32
BF16 = jnp.bfloat16
MESH = pl.DeviceIdType.MESH

D_MODEL = 1024
N_MEM = 256
EPS = 1e-6
NEG = -1e30
ROPE_THETA = 500000.0
ROT_DIM = 32
HD = 128
A_GROUP = 512
DILATIONS = (1, 4, 16)
BAND = 128
B_HEADS = 8
B_HD = 64
N_CHIPS = 4
N_DEV = 8

C_QA, C_KA, C_VA, C_ZA = 0, 1536, 3072, 4608
C_QB, C_KB, C_VB, C_ZB = 5120, 5632, 6144, 6656
C_QM, C_ZM, C_GL = 7168, 7680, 8192
N_MAIN = 11264
FB_ORIG = 6656
IN_COLS = 11272
SHARD_COLS = IN_COLS // N_CHIPS

ADAM_LR, ADAM_B1, ADAM_B2, ADAM_EPS, ADAM_WD, ADAM_STEP = 0.001, 0.9, 0.999, 1e-08, 0.01, 10

VMEM_LIMIT_V7X = 56 * 1024 * 1024

NT = (((1,), (1,)), ((), ()))
NN = (((1,), (0,)), ((), ()))
TN = (((0,), (0,)), ((), ()))


def _params(sem):
    return pltpu.CompilerParams(dimension_semantics=sem, vmem_limit_bytes=VMEM_LIMIT_V7X)


def _dot(a, b, dn=NN):
    return lax.dot_general(a, b, dn, preferred_element_type=F32)


def _sig(z):
    return 1.0 / (1.0 + jnp.exp(-z))


def _rows(name, fn, row_ins, bc_ins, outs, reds=(), tm=512):
    arrs, specs = [], []
    for r in row_ins:
        arr, w, cb = r if isinstance(r, tuple) else (r, r.shape[1], 0)
        arrs.append(arr)
        specs.append(pl.BlockSpec((tm, w), functools.partial(lambda i, cb: (i, cb), cb=cb)))
    for b in bc_ins:
        arrs.append(b)
        specs.append(pl.BlockSpec(b.shape, lambda i: (0, 0)))
    s = arrs[0].shape[0]
    tm = min(tm, s)
    n_in, n_out = len(arrs), len(outs)

    def body(*refs):
        vals = fn(*[r[...] for r in refs[:n_in]])
        if not isinstance(vals, (tuple, list)):
            vals = (vals,)
        for r, v in zip(refs[n_in:n_in + n_out], vals[:n_out]):
            r[...] = v.astype(r.dtype)
        if reds:
            red_refs = refs[n_in + n_out:]

            @pl.when(pl.program_id(0) == 0)
            def _():
                for r in red_refs:
                    r[...] = jnp.zeros_like(r)

            for r, v in zip(red_refs, vals[n_out:]):
                r[...] += v

    out_shape = [jax.ShapeDtypeStruct((s, c), dt) for c, dt in outs]
    out_shape += [jax.ShapeDtypeStruct((1, c), F32) for c in reds]
    out_specs = [pl.BlockSpec((tm, c), lambda i: (i, 0)) for c, _ in outs]
    out_specs += [pl.BlockSpec((1, c), lambda i: (0, 0)) for c in reds]
    res = pl.pallas_call(
        body, name=name, grid=(s // tm,), in_specs=specs, out_specs=out_specs, out_shape=out_shape,
        compiler_params=_params(("arbitrary",) if reds else ("parallel",)),
    )(*arrs)
    return res


def _mm(name, a, b, mode, out_dtype, tm=1024, tn=1024, tk=1024):
    if mode == "nn":
        (m, k), (_, n) = a.shape, b.shape
    elif mode == "nt":
        (m, k), (n, _) = a.shape, b.shape
    else:
        (k, m), (_, n) = a.shape, b.shape
    tm, tn, tk = min(tm, m), min(tn, n), min(tk, k)
    nk = k // tk
    dn = {"nn": NN, "nt": NT, "tn": TN}[mode]

    def body(a_ref, b_ref, o_ref, *acc):
        part = _dot(a_ref[...].astype(BF16), b_ref[...].astype(BF16), dn)
        if nk == 1:
            o_ref[...] = part.astype(o_ref.dtype)
        else:
            kk = pl.program_id(2)

            @pl.when(kk == 0)
            def _():
                acc[0][...] = part

            @pl.when(kk > 0)
            def _():
                acc[0][...] += part

            @pl.when(kk == nk - 1)
            def _():
                o_ref[...] = acc[0][...].astype(o_ref.dtype)

    a_spec = (pl.BlockSpec((tk, tm), lambda i, j, kk: (kk, i)) if mode == "tn"
              else pl.BlockSpec((tm, tk), lambda i, j, kk: (i, kk)))
    b_spec = (pl.BlockSpec((tn, tk), lambda i, j, kk: (j, kk)) if mode == "nt"
              else pl.BlockSpec((tk, tn), lambda i, j, kk: (kk, j)))
    return pl.pallas_call(
        body, name=name, grid=(m // tm, n // tn, nk), in_specs=[a_spec, b_spec],
        out_specs=pl.BlockSpec((tm, tn), lambda i, j, kk: (i, j)),
        out_shape=jax.ShapeDtypeStruct((m, n), out_dtype),
        scratch_shapes=[pltpu.VMEM((tm, tn), F32)] if nk > 1 else [],
        compiler_params=_params(("parallel", "parallel", "arbitrary")),
    )(a, b)


def _rms_fwd(name, x, g):
    def fn(xv, gv):
        r = lax.rsqrt(jnp.mean(xv * xv, axis=-1, keepdims=True) + EPS)
        return (xv * r * gv,)
    return _rows(name, fn, [x], [g], [(x.shape[1], BF16)], tm=min(512, x.shape[0]))[0]


def _rope_tables(pos, inv):
    ang = pos.astype(F32) * inv
    lane = lax.broadcasted_iota(jnp.int32, ang.shape, 1)
    c = jnp.where(lane < ROT_DIM, jnp.cos(ang), 1.0)
    sn = jnp.sin(ang)
    sg = jnp.where(lane < ROT_DIM // 2, -sn, jnp.where(lane < ROT_DIM, sn, 0.0))
    return c, sg, lane


def _rope_apply(x, c, sg, lane):
    outs = []
    for h in range(x.shape[1] // HD):
        xh = x[:, h * HD:(h + 1) * HD].astype(F32)
        swap = jnp.where(lane < ROT_DIM // 2, pltpu.roll(xh, HD - ROT_DIM // 2, 1),
                         pltpu.roll(xh, ROT_DIM // 2, 1))
        outs.append(xh * c + swap * sg)
    return jnp.concatenate(outs, axis=1)


def _rope_fwd(u, pos, inv):
    def fn(q, k, p, iv):
        c, sg, lane = _rope_tables(p, iv)
        return _rope_apply(q, c, sg, lane), _rope_apply(k, c, sg, lane)
    return _rows("rope_fwd", fn, [(u, 1536, 0), (u, 1536, 1), pos], [inv],
                 [(1536, BF16), (1536, BF16)], tm=256)


def _rope_bwd(dq, dk, pos, inv):
    def fn(q, k, p, iv):
        c, sg, lane = _rope_tables(p, iv)
        return _rope_apply(q, c, -sg, lane), _rope_apply(k, c, -sg, lane)
    return _rows("rope_bwd", fn, [dq, dk, pos], [inv], [(1536, BF16), (1536, BF16)], tm=256)


def _lane_pack(cols, like):
    lane = lax.broadcasted_iota(jnp.int32, like, 1)
    out = jnp.zeros(like, F32)
    for h, cvec in enumerate(cols):
        out = jnp.where(lane == h, cvec, out)
    return out


def _band_specs(l, d, tq):
    nsb = tq // BAND
    nblk = l // BAND
    cur = pl.BlockSpec((tq, A_GROUP), lambda r, i: (i, r))
    prev = pl.BlockSpec((BAND, A_GROUP), lambda r, i: (jnp.maximum(i * nsb - 1, 0), r))
    nxt = pl.BlockSpec((BAND, A_GROUP), lambda r, i: (jnp.minimum((i + 1) * nsb, nblk - 1), r))
    st_cur = pl.BlockSpec((tq, HD), lambda r, i: (i, r))
    st_nxt = pl.BlockSpec((BAND, HD), lambda r, i: (jnp.minimum((i + 1) * nsb, nblk - 1), r))
    return nsb, cur, prev, nxt, st_cur, st_nxt


def _band_mask_q(i, first_tile):
    qr = lax.broadcasted_iota(jnp.int32, (BAND, 2 * BAND), 0)
    kc = lax.broadcasted_iota(jnp.int32, (BAND, 2 * BAND), 1)
    in_prev = (kc < BAND) & (kc >= qr)
    in_cur = (kc >= BAND) & (kc - BAND <= qr)
    if i == 0:
        in_prev = in_prev & jnp.logical_not(first_tile)
    return in_prev | in_cur


def _band_mask_k(j, nsb, last_tile):
    qr = lax.broadcasted_iota(jnp.int32, (2 * BAND, BAND), 0)
    kc = lax.broadcasted_iota(jnp.int32, (2 * BAND, BAND), 1)
    same = (qr < BAND) & (kc <= qr)
    nxt = (qr >= BAND) & (kc >= qr - BAND)
    if j == nsb - 1:
        nxt = nxt & jnp.logical_not(last_tile)
    return same | nxt


def _band_fwd(name, q, k, v, d):
    l = q.shape[0]
    tq = min(512, l)
    nsb, cur, prev, _, st_cur, _ = _band_specs(l, d, tq)
    scale = HD ** -0.5

    def body(q_ref, kc_ref, kp_ref, vc_ref, vp_ref, o_ref, lse_ref):
        first = pl.program_id(1) == 0
        for i in range(nsb):
            lses = []
            mask = _band_mask_q(i, first)
            for h in range(4):
                cs = slice(h * HD, (h + 1) * HD)
                qv = q_ref[i * BAND:(i + 1) * BAND, cs]
                if i == 0:
                    kk = jnp.concatenate([kp_ref[:, cs], kc_ref[0:BAND, cs]], axis=0)
                    vv = jnp.concatenate([vp_ref[:, cs], vc_ref[0:BAND, cs]], axis=0)
                else:
                    kk = kc_ref[(i - 1) * BAND:(i + 1) * BAND, cs]
                    vv = vc_ref[(i - 1) * BAND:(i + 1) * BAND, cs]
                s = jnp.where(mask, _dot(qv, kk, NT) * scale, NEG)
                m = jnp.max(s, axis=-1, keepdims=True)
                p = jnp.exp(s - m)
                den = jnp.sum(p, axis=-1, keepdims=True)
                o_ref[i * BAND:(i + 1) * BAND, cs] = _dot(p.astype(BF16), vv) / den
                lses.append(m + jnp.log(den))
            lse_ref[i * BAND:(i + 1) * BAND, :] = _lane_pack(lses, (BAND, HD))

    return pl.pallas_call(
        body, name=name, grid=(d, l // tq), in_specs=[cur, cur, prev, cur, prev],
        out_specs=[cur, st_cur],
        out_shape=[jax.ShapeDtypeStruct((l, d * A_GROUP), F32), jax.ShapeDtypeStruct((l, d * HD), F32)],
        compiler_params=_params(("parallel", "parallel")),
    )(q, k, k, v, v)


def _band_dq(name, q, k, v, dy, lse, delta, d):
    l = q.shape[0]
    tq = min(512, l)
    nsb, cur, prev, _, st_cur, _ = _band_specs(l, d, tq)
    scale = HD ** -0.5

    def body(q_ref, kc_ref, kp_ref, vc_ref, vp_ref, dy_ref, lse_ref, dl_ref, dq_ref):
        first = pl.program_id(1) == 0
        for i in range(nsb):
            mask = _band_mask_q(i, first)
            rs = slice(i * BAND, (i + 1) * BAND)
            for h in range(4):
                cs = slice(h * HD, (h + 1) * HD)
                if i == 0:
                    kk = jnp.concatenate([kp_ref[:, cs], kc_ref[0:BAND, cs]], axis=0)
                    vv = jnp.concatenate([vp_ref[:, cs], vc_ref[0:BAND, cs]], axis=0)
                else:
                    kk = kc_ref[(i - 1) * BAND:(i + 1) * BAND, cs]
                    vv = vc_ref[(i - 1) * BAND:(i + 1) * BAND, cs]
                s = jnp.where(mask, _dot(q_ref[rs, cs], kk, NT) * scale, NEG)
                p = jnp.exp(s - lse_ref[rs, h:h + 1])
                dp = _dot(dy_ref[rs, cs], vv, NT)
                ds = p * (dp - dl_ref[rs, h:h + 1])
                dq_ref[rs, cs] = (_dot(ds.astype(BF16), kk) * scale).astype(dq_ref.dtype)

    return pl.pallas_call(
        body, name=name, grid=(d, l // tq),
        in_specs=[cur, cur, prev, cur, prev, cur, st_cur, st_cur], out_specs=cur,
        out_shape=jax.ShapeDtypeStruct((l, d * A_GROUP), BF16),
        compiler_params=_params(("parallel", "parallel")),
    )(q, k, k, v, v, dy, lse, delta)


def _band_dkv(name, q, k, v, dy, lse, delta, d):
    l = q.shape[0]
    tq = min(512, l)
    nsb, cur, _, nxt, st_cur, st_nxt = _band_specs(l, d, tq)
    scale = HD ** -0.5
    ntile = l // tq

    def body(k_ref, v_ref, qc_ref, qn_ref, dyc_ref, dyn_ref, lc_ref, ln_ref, dc_ref, dn_ref,
             dk_ref, dv_ref):
        last = pl.program_id(1) == ntile - 1

        def win(c_ref, n_ref, j, cs):
            if j == nsb - 1:
                return jnp.concatenate([c_ref[j * BAND:(j + 1) * BAND, cs], n_ref[:, cs]], axis=0)
            return c_ref[j * BAND:(j + 2) * BAND, cs]

        for j in range(nsb):
            mask = _band_mask_k(j, nsb, last)
            rs = slice(j * BAND, (j + 1) * BAND)
            for h in range(4):
                cs = slice(h * HD, (h + 1) * HD)
                hs = slice(h, h + 1)
                qw = win(qc_ref, qn_ref, j, cs)
                dyw = win(dyc_ref, dyn_ref, j, cs)
                s = jnp.where(mask, _dot(qw, k_ref[rs, cs], NT) * scale, NEG)
                p = jnp.exp(s - win(lc_ref, ln_ref, j, hs))
                dp = _dot(dyw, v_ref[rs, cs], NT)
                ds = p * (dp - win(dc_ref, dn_ref, j, hs))
                dv_ref[rs, cs] = _dot(p.astype(BF16), dyw, TN).astype(dv_ref.dtype)
                dk_ref[rs, cs] = (_dot(ds.astype(BF16), qw, TN) * scale).astype(dk_ref.dtype)

    shp = jax.ShapeDtypeStruct((l, d * A_GROUP), BF16)
    return pl.pallas_call(
        body, name=name, grid=(d, ntile),
        in_specs=[cur, cur, cur, nxt, cur, nxt, st_cur, st_nxt, st_cur, st_nxt],
        out_specs=[cur, cur], out_shape=[shp, shp],
        compiler_params=_params(("parallel", "parallel")),
    )(k, v, q, q, dy, dy, lse, lse, delta, delta)


def _split3(x):
    hi = x.astype(BF16)
    r1 = x - hi.astype(F32)
    mid = r1.astype(BF16)
    lo = (r1 - mid.astype(F32)).astype(BF16)
    return hi, mid, lo


def _fox_prep(z, b):
    h, s = z.shape
    blk = min(512, s)

    def body(z_ref, b_ref, c_ref):
        r = lax.broadcasted_iota(jnp.int32, (blk, blk), 0)
        cidx = lax.broadcasted_iota(jnp.int32, (blk, blk), 1)
        tri = (r <= cidx).astype(BF16)
        carry = jnp.zeros((h, 1), F32)
        for t in range(s // blk):
            zz = z_ref[:, t * blk:(t + 1) * blk] + b_ref[...]
            lf = jnp.minimum(zz, 0.0) - jnp.log(1.0 + jnp.exp(-jnp.abs(zz)))
            hi, mid, lo = _split3(lf)
            cs = _dot(hi, tri) + _dot(mid, tri) + _dot(lo, tri) + carry
            c_ref[:, t * blk:(t + 1) * blk] = cs
            carry = cs[:, blk - 1:blk]

    return pl.pallas_call(body, name="fox_prep", out_shape=jax.ShapeDtypeStruct((h, s), F32))(z, b)


def _fox_prep_bwd(dck, dcq, z, b):
    h, s = z.shape
    blk = min(512, s)

    def body(dck_ref, dcq_ref, z_ref, b_ref, dz_ref, db_ref):
        r = lax.broadcasted_iota(jnp.int32, (blk, blk), 0)
        cidx = lax.broadcasted_iota(jnp.int32, (blk, blk), 1)
        tri = (r >= cidx).astype(BF16)
        carry = jnp.zeros((h, 1), F32)
        tot = jnp.zeros((h, 1), F32)
        for t in reversed(range(s // blk)):
            hi, mid, lo = _split3(dck_ref[:, t * blk:(t + 1) * blk] + dcq_ref[:, t * blk:(t + 1) * blk])
            rc = _dot(hi, tri) + _dot(mid, tri) + _dot(lo, tri) + carry
            carry = rc[:, 0:1]
            zz = z_ref[:, t * blk:(t + 1) * blk] + b_ref[...]
            dz = rc * _sig(-zz)
            dz_ref[:, t * blk:(t + 1) * blk] = dz
            tot = tot + jnp.sum(dz, axis=-1, keepdims=True)
        db_ref[...] = tot

    return pl.pallas_call(
        body, name="fox_prep_bwd",
        out_shape=[jax.ShapeDtypeStruct((h, s), F32), jax.ShapeDtypeStruct((h, 1), F32)])(dck, dcq, z, b)


def _fox_fwd(q, k, v, crow, t):
    h, s, dh = q.shape
    nt = s // t
    scale = dh ** -0.5

    def body(q_ref, k_ref, v_ref, c_ref, o_ref, lse_ref):
        i = pl.program_id(1)
        qv = q_ref[...]
        row = lax.broadcasted_iota(jnp.int32, (t, t), 0)
        col = lax.broadcasted_iota(jnp.int32, (t, t), 1)

        def tile(j, carry, diag):
            m, lsum, acc = carry
            j0 = pl.multiple_of(j * t, t)
            s_ = _dot(qv, k_ref[pl.ds(j0, t), :], NT) * scale - c_ref[j]
            if diag:
                s_ = jnp.where(row >= col, s_, NEG)
            m2 = jnp.maximum(m, jnp.max(s_, axis=-1, keepdims=True))
            a = jnp.exp(m - m2)
            p = jnp.exp(s_ - m2)
            return (m2, a * lsum + jnp.sum(p, axis=-1, keepdims=True),
                    a * acc + _dot(p.astype(BF16), v_ref[pl.ds(j0, t), :]))

        init = (jnp.full((t, 1), NEG, F32), jnp.zeros((t, 1), F32), jnp.zeros((t, dh), F32))
        carry = lax.fori_loop(0, i, lambda j, c: tile(j, c, False), init)
        m, lsum, acc = tile(i, carry, True)
        o_ref[...] = (acc / lsum).astype(o_ref.dtype)
        lse_ref[...] = m + jnp.log(lsum)

    return pl.pallas_call(
        body, name="fox_fwd", grid=(h, nt),
        in_specs=[pl.BlockSpec((None, t, dh), lambda hh, i: (hh, i, 0)),
                  pl.BlockSpec((None, s, dh), lambda hh, i: (hh, 0, 0)),
                  pl.BlockSpec((None, s, dh), lambda hh, i: (hh, 0, 0)),
                  pl.BlockSpec((None, nt, 1, t), lambda hh, i: (hh, 0, 0, 0))],
        out_specs=[pl.BlockSpec((None, t, dh), lambda hh, i: (hh, i, 0)),
                   pl.BlockSpec((None, t, 1), lambda hh, i: (hh, i, 0))],
        out_shape=[jax.ShapeDtypeStruct((h, s, dh), BF16), jax.ShapeDtypeStruct((h, s, 1), F32)],
        compiler_params=_params(("parallel", "parallel")),
    )(q, k, v, crow)


def _fox_bwd(q, k, v, do, ccol, lse_row, delta_row, t):
    h, s, dh = q.shape
    nt = s // t
    scale = dh ** -0.5

    def body(q_ref, do_ref, lse_ref, dl_ref, k_ref, v_ref, c_ref, dq_ref, dk_ref, dv_ref, dc_ref, dcq_ref):
        j = pl.program_id(1)

        @pl.when(j == 0)
        def _():
            dq_ref[...] = jnp.zeros_like(dq_ref)
            dcq_ref[...] = jnp.zeros_like(dcq_ref)

        kk, vv, cc = k_ref[...], v_ref[...], c_ref[...]
        krow = lax.broadcasted_iota(jnp.int32, (t, t), 0)
        qcol = lax.broadcasted_iota(jnp.int32, (t, t), 1)

        def tile(i, carry, diag):
            dk, dv, dc = carry
            i0 = pl.multiple_of(i * t, t)
            qi, doi = q_ref[pl.ds(i0, t), :], do_ref[pl.ds(i0, t), :]
            st = _dot(kk, qi, NT) * scale - cc - lse_ref[i]
            if diag:
                st = jnp.where(krow <= qcol, st, NEG)
            pt = jnp.exp(st)
            dst = pt * (_dot(vv, doi, NT) - dl_ref[i])
            dsb = dst.astype(BF16)
            dq_ref[pl.ds(i0, t), :] += _dot(dsb, kk, TN) * scale
            dcq_ref[i] += jnp.sum(dst, axis=0, keepdims=True)
            return (dk + _dot(dsb, qi), dv + _dot(pt.astype(BF16), doi),
                    dc - jnp.sum(dst, axis=-1, keepdims=True))

        init = (jnp.zeros((t, dh), F32), jnp.zeros((t, dh), F32), jnp.zeros((t, 1), F32))
        carry = tile(j, init, True)
        dk, dv, dc = lax.fori_loop(j + 1, nt, lambda i, c: tile(i, c, False), carry)
        dk_ref[...] = (dk * scale).astype(dk_ref.dtype)
        dv_ref[...] = dv.astype(dv_ref.dtype)
        dc_ref[...] = dc

    full = pl.BlockSpec((None, s, dh), lambda hh, j: (hh, 0, 0))
    rowst = pl.BlockSpec((None, nt, 1, t), lambda hh, j: (hh, 0, 0, 0))
    tl = pl.BlockSpec((None, t, dh), lambda hh, j: (hh, j, 0))
    col = pl.BlockSpec((None, t, 1), lambda hh, j: (hh, j, 0))
    return pl.pallas_call(
        body, name="fox_bwd", grid=(h, nt),
        in_specs=[full, full, rowst, rowst, tl, tl, col],
        out_specs=[full, tl, tl, col, rowst],
        out_shape=[jax.ShapeDtypeStruct((h, s, dh), F32), jax.ShapeDtypeStruct((h, s, dh), BF16),
                   jax.ShapeDtypeStruct((h, s, dh), BF16), jax.ShapeDtypeStruct((h, s, 1), F32),
                   jax.ShapeDtypeStruct((h, nt, 1, t), F32)],
        compiler_params=_params(("parallel", "arbitrary")),
    )(q, do, lse_row, delta_row, k, v, ccol)


def _mem_fwd(u, mkv, tq=512):
    s = u.shape[0]
    scale = HD ** -0.5

    def body(q_ref, mk_ref, mv_ref, o_ref, lse_ref):
        lses = []
        for h in range(4):
            cs = slice(h * HD, (h + 1) * HD)
            sc = _dot(q_ref[:, cs], mk_ref[:, cs], NT) * scale
            m = jnp.max(sc, axis=-1, keepdims=True)
            p = jnp.exp(sc - m)
            den = jnp.sum(p, axis=-1, keepdims=True)
            o_ref[:, cs] = (_dot(p.astype(BF16), mv_ref[:, cs]) / den).astype(o_ref.dtype)
            lses.append(m + jnp.log(den))
        lse_ref[...] = _lane_pack(lses, (tq, HD))

    return pl.pallas_call(
        body, name="mem_fwd", grid=(s // tq,),
        in_specs=[pl.BlockSpec((tq, 512), lambda i: (i, C_QM // 512)),
                  pl.BlockSpec((N_MEM, 512), lambda i: (0, 0)),
                  pl.BlockSpec((N_MEM, 512), lambda i: (0, 1))],
        out_specs=[pl.BlockSpec((tq, 512), lambda i: (i, 0)), pl.BlockSpec((tq, HD), lambda i: (i, 0))],
        out_shape=[jax.ShapeDtypeStruct((s, 512), BF16), jax.ShapeDtypeStruct((s, HD), F32)],
        compiler_params=_params(("parallel",)),
    )(u, mkv, mkv)


def _mem_bwd(u, mkv, o, do, lse, tq=512):
    s = u.shape[0]
    scale = HD ** -0.5

    def body(q_ref, mk_ref, mv_ref, o_ref, do_ref, lse_ref, dq_ref, dmk_ref, dmv_ref):
        @pl.when(pl.program_id(0) == 0)
        def _():
            dmk_ref[...] = jnp.zeros_like(dmk_ref)
            dmv_ref[...] = jnp.zeros_like(dmv_ref)

        for h in range(4):
            cs = slice(h * HD, (h + 1) * HD)
            qv, dov = q_ref[:, cs], do_ref[:, cs]
            sc = _dot(qv, mk_ref[:, cs], NT) * scale
            p = jnp.exp(sc - lse_ref[:, h:h + 1])
            delta = jnp.sum(dov.astype(F32) * o_ref[:, cs].astype(F32), axis=-1, keepdims=True)
            ds = p * (_dot(dov, mv_ref[:, cs], NT) - delta)
            dsb = ds.astype(BF16)
            dq_ref[:, cs] = (_dot(dsb, mk_ref[:, cs]) * scale).astype(dq_ref.dtype)
            dmk_ref[:, cs] += _dot(dsb, qv, TN) * scale
            dmv_ref[:, cs] += _dot(p.astype(BF16), dov, TN)

    row = pl.BlockSpec((tq, 512), lambda i: (i, 0))
    acc = pl.BlockSpec((N_MEM, 512), lambda i: (0, 0))
    return pl.pallas_call(
        body, name="mem_bwd", grid=(s // tq,),
        in_specs=[pl.BlockSpec((tq, 512), lambda i: (i, C_QM // 512)),
                  pl.BlockSpec((N_MEM, 512), lambda i: (0, 0)),
                  pl.BlockSpec((N_MEM, 512), lambda i: (0, 1)),
                  row, row, pl.BlockSpec((tq, HD), lambda i: (i, 0))],
        out_specs=[row, acc, acc],
        out_shape=[jax.ShapeDtypeStruct((s, 512), BF16), jax.ShapeDtypeStruct((N_MEM, 512), F32),
                   jax.ShapeDtypeStruct((N_MEM, 512), F32)],
        compiler_params=_params(("arbitrary",)),
    )(u, mkv, mkv, o, do, lse)


def _heads_major(a, col0):
    s = a.shape[0]
    return a[:, col0:col0 + 512].reshape(s, B_HEADS, B_HD).transpose(1, 0, 2)


def _token_major(a):
    h, s, dh = a.shape
    return a.transpose(1, 0, 2).reshape(s, h * dh)


def _class_view(a, d):
    s, c = a.shape
    return a.reshape(s // d, d * c)


def _local_step(x, mem, pos, target, g_pre, g_post, g_mem, w_main, w_fb, b_forget, b_merge,
                w_mem_kv, w_ba, w_bb, w_bm, w_out):
    s = x.shape[0]
    t_fox = min(512, s)
    nt = s // t_fox
    half = ROT_DIM // 2
    inv = ROPE_THETA ** (-jnp.arange(half, dtype=F32) / half)
    inv128 = jnp.concatenate([inv, inv, jnp.zeros((HD - ROT_DIM,), F32)]).reshape(1, HD)

    h = _rms_fwd("norm_pre", x, g_pre)
    u = _mm("proj_in", h, w_main, "nn", BF16, tn=512)
    ufb = _mm("proj_fb", h, w_fb, "nn", F32)
    memn = _rms_fwd("norm_mem", mem, g_mem)
    mkv = _mm("proj_mem", memn, w_mem_kv, "nn", BF16)

    q_rot, k_rot = _rope_fwd(u, pos, inv128)
    os_, lses = [], []
    views = []
    for g, d in enumerate(DILATIONS):
        qv = _class_view(q_rot[:, g * A_GROUP:(g + 1) * A_GROUP], d)
        kv = _class_view(k_rot[:, g * A_GROUP:(g + 1) * A_GROUP], d)
        vv = _class_view(u[:, C_VA + g * A_GROUP:C_VA + (g + 1) * A_GROUP], d)
        views.append((qv, kv, vv))
        o_g, lse_g = _band_fwd("band_fwd%d" % g, qv, kv, vv, d)
        os_.append(o_g.reshape(s, A_GROUP))
        lses.append(lse_g.reshape(s, HD))

    def merge_a(o1, o2, o3, l1, l2, l3, za):
        ys, tots = [], []
        for hh in range(4):
            cs, hs = slice(hh * HD, (hh + 1) * HD), slice(hh, hh + 1)
            mx = jnp.maximum(jnp.maximum(l1[:, hs], l2[:, hs]), l3[:, hs])
            e1, e2, e3 = jnp.exp(l1[:, hs] - mx), jnp.exp(l2[:, hs] - mx), jnp.exp(l3[:, hs] - mx)
            den = e1 + e2 + e3
            ys.append((e1 * o1[:, cs] + e2 * o2[:, cs] + e3 * o3[:, cs]) / den)
            tots.append(mx + jnp.log(den))
        y = jnp.concatenate(ys, axis=1)
        zf = za.astype(F32)
        return y, y * (zf * _sig(zf)), _lane_pack(tots, l1.shape)

    y_a, yg_a, lse_a = _rows("merge_a", merge_a, os_ + lses + [(u, 512, C_ZA // 512)], [],
                             [(512, BF16), (512, BF16), (HD, F32)])

    zrow = ufb[:, :B_HEADS].T
    c = _fox_prep(zrow, b_forget.reshape(B_HEADS, 1))
    crow = c.reshape(B_HEADS, nt, 1, t_fox)
    qb, kb, vb = _heads_major(u, C_QB), _heads_major(u, C_KB), _heads_major(u, C_VB)
    ob, lse_b = _fox_fwd(qb, kb, vb, crow, t_fox)
    y_b = _token_major(ob)

    y_m, lse_m = _mem_fwd(u, mkv)

    def gate(y, z):
        zf = z.astype(F32)
        return (y.astype(F32) * (zf * _sig(zf)),)

    yg_b = _rows("gate_b", gate, [y_b, (u, 512, C_ZB // 512)], [], [(512, BF16)])[0]
    yg_m = _rows("gate_m", gate, [y_m, (u, 512, C_ZM // 512)], [], [(512, BF16)])[0]

    br_a = _mm("branch_a", yg_a, w_ba, "nn", BF16)
    br_b = _mm("branch_b", yg_b, w_bb, "nn", BF16)
    br_m = _mm("branch_m", yg_m, w_bm, "nn", BF16)
    gl = [(u, 1024, C_GL // 1024 + i) for i in range(3)]
    bm3 = b_merge.reshape(3, D_MODEL)

    def merge(g0, g1, g2, b0, b1, b2, bm):
        tot = 0.0
        for i, (gv, bv) in enumerate(((g0, b0), (g1, b1), (g2, b2))):
            tot = tot + _sig(gv.astype(F32) + bm[i:i + 1, :]) * bv.astype(F32)
        return (tot,)

    merged = _rows("merge_gates", merge, gl + [br_a, br_b, br_m], [bm3], [(D_MODEL, BF16)])[0]
    out = _mm("proj_out", merged, w_out, "nn", F32)

    def tail(xv, ov, tv, gv):
        r = lax.rsqrt(jnp.mean(ov * ov, axis=-1, keepdims=True) + EPS)
        n = ov * r
        err = xv + n * gv - tv
        dy = err * (1.0 / D_MODEL)
        dn = dy * gv
        dout = r * (dn - n * jnp.mean(dn * n, axis=-1, keepdims=True))
        return (dy, dout, jnp.sum(0.5 * err * err * (1.0 / D_MODEL), axis=0, keepdims=True),
                jnp.sum(dy * n, axis=0, keepdims=True))

    dy, dout, loss_lanes, g_post_grad = _rows(
        "tail", tail, [x, out, target], [g_post], [(D_MODEL, F32), (D_MODEL, BF16)],
        reds=[D_MODEL, D_MODEL], tm=256)

    dmerged = _mm("d_merged", dout, w_out, "nt", BF16)
    gw_out = _mm("g_w_out", merged, dout, "tn", F32)

    def merge_bwd(dm, g0, g1, g2, b0, b1, b2, bm):
        dmf = dm.astype(F32)
        dbs, dgs, sums = [], [], []
        for i, (gv, bv) in enumerate(((g0, b0), (g1, b1), (g2, b2))):
            sg = _sig(gv.astype(F32) + bm[i:i + 1, :])
            dbs.append(dmf * sg)
            dg = dmf * bv.astype(F32) * sg * (1.0 - sg)
            dgs.append(dg)
            sums.append(jnp.sum(dg, axis=0, keepdims=True))
        return tuple(dbs + dgs + sums)

    res = _rows("merge_bwd", merge_bwd, [dmerged] + gl + [br_a, br_b, br_m], [bm3],
                [(D_MODEL, BF16)] * 6, reds=[D_MODEL] * 3, tm=256)
    dbr, dgl, g_bmerge = res[0:3], res[3:6], jnp.concatenate(res[6:9], axis=1)

    dyg, gw_branch = [], []
    for nm, dbv, wv, ygv in (("a", dbr[0], w_ba, yg_a), ("b", dbr[1], w_bb, yg_b), ("m", dbr[2], w_bm, yg_m)):
        dyg.append(_mm("d_yg_" + nm, dbv, wv, "nt", BF16))
        gw_branch.append(_mm("g_w_branch_" + nm, ygv, dbv, "tn", F32))

    def gate_bwd(dg, y, z):
        dgf, yf, zf = dg.astype(F32), y.astype(F32), z.astype(F32)
        sg = _sig(zf)
        return dgf * (zf * sg), dgf * yf * (sg * (1.0 + zf * (1.0 - sg)))

    def gate_bwd_a(dg, y, z):
        dyv, dz = gate_bwd(dg, y, z)
        prod = dyv * y.astype(F32)
        dl = [jnp.sum(prod[:, hh * HD:(hh + 1) * HD], axis=-1, keepdims=True) for hh in range(4)]
        return dyv, dz, _lane_pack(dl, (dg.shape[0], HD))

    dy_a, dz_a, delta_a = _rows("gate_bwd_a", gate_bwd_a, [dyg[0], y_a, (u, 512, C_ZA // 512)], [],
                                [(512, BF16), (512, BF16), (HD, F32)])
    dy_b, dz_b = _rows("gate_bwd_b", gate_bwd, [dyg[1], y_b, (u, 512, C_ZB // 512)], [],
                       [(512, BF16), (512, BF16)])
    dy_m, dz_m = _rows("gate_bwd_m", gate_bwd, [dyg[2], y_m, (u, 512, C_ZM // 512)], [],
                       [(512, BF16), (512, BF16)])

    dq_m, dmk, dmv = _mem_bwd(u, mkv, y_m, dy_m, lse_m)
    dmkv = jnp.concatenate([dmk, dmv], axis=1)
    gw_mem_kv = _mm("g_w_mem_kv", memn, dmkv, "tn", F32)
    dmemn = _mm("d_memn", dmkv, w_mem_kv, "nt", F32)

    def mem_gain_grad(mv, dv):
        r = lax.rsqrt(jnp.mean(mv * mv, axis=-1, keepdims=True) + EPS)
        return (jnp.sum(dv * mv * r, axis=0, keepdims=True),)

    g_mem_grad = _rows("g_norm_mem", mem_gain_grad, [mem, dmemn], [], [], reds=[D_MODEL], tm=N_MEM)[0]

    dob = _heads_major(dy_b, 0)

    def fox_delta(a, b):
        return (jnp.sum(a.astype(F32) * b.astype(F32), axis=-1, keepdims=True),)

    delta_b = _rows("fox_delta", fox_delta, [dob.reshape(B_HEADS * s, B_HD), ob.reshape(B_HEADS * s, B_HD)],
                    [], [(1, F32)], tm=min(2048, s))[0]
    dqb, dkb, dvb, dck, dcq = _fox_bwd(qb, kb, vb, dob, c.reshape(B_HEADS, s, 1),
                                 lse_b.reshape(B_HEADS, nt, 1, t_fox),
                                 delta_b.reshape(B_HEADS, nt, 1, t_fox), t_fox)
    dzrow, g_bforget = _fox_prep_bwd(dck.reshape(B_HEADS, s), dcq.reshape(B_HEADS, s), zrow,
                                     b_forget.reshape(B_HEADS, 1))
    dfb = jnp.zeros((s, HD), BF16).at[:, :B_HEADS].set(dzrow.T.astype(BF16))

    dqs, dks, dvs = [], [], []
    for g, d in enumerate(DILATIONS):
        qv, kv, vv = views[g]
        dyv, lv, dlv = _class_view(dy_a, d), _class_view(lse_a, d), _class_view(delta_a, d)
        dqs.append(_band_dq("band_dq%d" % g, qv, kv, vv, dyv, lv, dlv, d).reshape(s, A_GROUP))
        dk_g, dv_g = _band_dkv("band_dkv%d" % g, qv, kv, vv, dyv, lv, dlv, d)
        dks.append(dk_g.reshape(s, A_GROUP))
        dvs.append(dv_g.reshape(s, A_GROUP))
    dqa, dka = _rope_bwd(jnp.concatenate(dqs, axis=1), jnp.concatenate(dks, axis=1), pos, inv128)

    du = jnp.concatenate(
        [dqa, dka] + dvs + [dz_a, _token_major(dqb).astype(BF16), _token_major(dkb), _token_major(dvb),
                            dz_b, dq_m, dz_m] + list(dgl), axis=1)

    gw_main = _mm("g_w_main", h, du, "tn", F32)
    gw_fb = _mm("g_w_fb", h, dfb, "tn", F32)
    dh_main = _mm("d_h", du, w_main, "nt", F32, tk=1024)
    dh_fb = _mm("d_h_fb", dfb, w_fb, "nt", F32)

    def pre_bwd(xv, d1, d2, dyv, gv):
        r = lax.rsqrt(jnp.mean(xv * xv, axis=-1, keepdims=True) + EPS)
        n = xv * r
        dhv = d1 + d2
        dn = dhv * gv
        dx = r * (dn - n * jnp.mean(dn * n, axis=-1, keepdims=True))
        return dyv + dx, jnp.sum(dhv * n, axis=0, keepdims=True)

    grad_x, g_pre_grad = _rows("norm_pre_bwd", pre_bwd, [x, dh_main, dh_fb, dy], [g_pre],
                               [(D_MODEL, F32)], reds=[D_MODEL], tm=256)

    gw_in = jnp.concatenate([gw_main[:, :FB_ORIG], gw_fb[:, :B_HEADS], gw_main[:, FB_ORIG:]], axis=1)
    grads = dict(norm_pre_g=g_pre_grad, norm_post_g=g_post_grad, norm_mem_g=g_mem_grad, w_in=gw_in,
                 b_forget=g_bforget.reshape(1, B_HEADS), b_merge=g_bmerge, w_mem_kv=gw_mem_kv,
                 w_branch_a=gw_branch[0], w_branch_b=gw_branch[1], w_branch_m=gw_branch[2], w_out=gw_out)
    return loss_lanes, grad_x, grads


HBM_SPEC = pl.BlockSpec(memory_space=pltpu.HBM)


def _place():
    x, y, c = lax.axis_index("x"), lax.axis_index("y"), lax.axis_index("c")
    chips = [(1 - x, y), (x, 1 - y), (1 - x, 1 - y)]
    return x, y, c, 2 * x + y, chips


def _gather_weights(parts):
    n = len(parts)

    def body(*refs):
        srcs, outs = refs[:n], refs[n:2 * n]
        send_sems, recv_sems, local_sems = refs[2 * n:]
        x, y, c, p, chips = _place()
        me, sib = (x, y, c), (x, y, 1 - c)

        def cp(i, k, chip, half, to, src_ref=None):
            dst = outs[i].at[chip, half]
            return pltpu.make_async_remote_copy(
                src_ref=dst if src_ref is None else src_ref, dst_ref=dst, send_sem=send_sems.at[i, k],
                recv_sem=recv_sems.at[i, k], device_id=to, device_id_type=MESH)

        local = [pltpu.make_async_copy(srcs[i], outs[i].at[p], local_sems.at[i]) for i in range(n)]
        for cpy in local:
            cpy.start()
        first = [cp(i, j, p, c, (cx, cy, c), src_ref=srcs[i].at[c])
                 for j, (cx, cy) in enumerate(chips) for i in range(n)]
        for f in first:
            f.start()
        passed = []
        for j, (cx, cy) in enumerate(chips):
            for i in range(n):
                cp(i, j, 2 * cx + cy, c, me).wait_recv()
                fw = cp(i, 3 + j, 2 * cx + cy, c, sib)
                fw.start()
                passed.append(fw)
        for j, (cx, cy) in enumerate(chips):
            for i in range(n):
                cp(i, 3 + j, 2 * cx + cy, 1 - c, me).wait_recv()
        for f in first + passed:
            f.wait_send()
        for cpy in local:
            cpy.wait()

    return pl.pallas_call(
        body, name="gather_weights", in_specs=[HBM_SPEC] * n, out_specs=[HBM_SPEC] * n,
        out_shape=[jax.ShapeDtypeStruct((N_CHIPS,) + a.shape, a.dtype) for a in parts],
        scratch_shapes=[pltpu.SemaphoreType.DMA((n, 6)), pltpu.SemaphoreType.DMA((n, 6)),
                        pltpu.SemaphoreType.DMA((n,))],
    )(*parts)


def _swap_with_sibling(parts):
    n = len(parts)

    def body(*refs):
        srcs, outs = refs[:n], refs[n:2 * n]
        send_sems, recv_sems = refs[2 * n:]
        x, y, c, _, _ = _place()
        cps = [pltpu.make_async_remote_copy(
            src_ref=srcs[i].at[q], dst_ref=outs[i].at[q], send_sem=send_sems.at[i, q],
            recv_sem=recv_sems.at[i, q], device_id=(x, y, 1 - c), device_id_type=MESH)
            for q in range(N_CHIPS) for i in range(n)]
        for cpy in cps:
            cpy.start()
        for cpy in cps:
            cpy.wait()

    return pl.pallas_call(
        body, name="swap_with_sibling", in_specs=[HBM_SPEC] * n, out_specs=[HBM_SPEC] * n,
        out_shape=[jax.ShapeDtypeStruct(a.shape, a.dtype) for a in parts],
        scratch_shapes=[pltpu.SemaphoreType.DMA((n, N_CHIPS)), pltpu.SemaphoreType.DMA((n, N_CHIPS))],
    )(*parts)


def _scatter_to_owners(parts):
    n = len(parts)

    def body(*refs):
        srcs, outs = refs[:n], refs[n:2 * n]
        send_sems, recv_sems, local_sems = refs[2 * n:]
        x, y, c, p, chips = _place()
        local = [pltpu.make_async_copy(srcs[i].at[p], outs[i].at[p], local_sems.at[i]) for i in range(n)]
        for cpy in local:
            cpy.start()
        sends = []
        for j, (cx, cy) in enumerate(chips):
            for i in range(n):
                cpy = pltpu.make_async_remote_copy(
                    src_ref=srcs[i].at[2 * cx + cy], dst_ref=outs[i].at[p], send_sem=send_sems.at[i, j],
                    recv_sem=recv_sems.at[i, j], device_id=(cx, cy, c), device_id_type=MESH)
                cpy.start()
                sends.append(cpy)
        for j, (cx, cy) in enumerate(chips):
            for i in range(n):
                pltpu.make_async_remote_copy(
                    src_ref=srcs[i].at[2 * cx + cy], dst_ref=outs[i].at[2 * cx + cy],
                    send_sem=send_sems.at[i, j], recv_sem=recv_sems.at[i, j],
                    device_id=(cx, cy, c), device_id_type=MESH).wait_recv()
        for cpy in sends:
            cpy.wait_send()
        for cpy in local:
            cpy.wait()

    return pl.pallas_call(
        body, name="scatter_to_owners", in_specs=[HBM_SPEC] * n, out_specs=[HBM_SPEC] * n,
        out_shape=[jax.ShapeDtypeStruct(a.shape, a.dtype) for a in parts],
        scratch_shapes=[pltpu.SemaphoreType.DMA((n, 3)), pltpu.SemaphoreType.DMA((n, 3)),
                        pltpu.SemaphoreType.DMA((n,))],
    )(*parts)


def _share_with_sibling(parts):
    n = len(parts)

    def body(*refs):
        srcs, outs = refs[:n], refs[n:2 * n]
        send_sems, recv_sems, local_sems = refs[2 * n:]
        x, y, c, _, _ = _place()
        local = [pltpu.make_async_copy(srcs[i].at[0], outs[i].at[c], local_sems.at[i]) for i in range(n)]
        for cpy in local:
            cpy.start()
        sends = [pltpu.make_async_remote_copy(
            src_ref=srcs[i].at[0], dst_ref=outs[i].at[c], send_sem=send_sems.at[i], recv_sem=recv_sems.at[i],
            device_id=(x, y, 1 - c), device_id_type=MESH) for i in range(n)]
        for cpy in sends:
            cpy.start()
        for i in range(n):
            pltpu.make_async_remote_copy(
                src_ref=srcs[i].at[0], dst_ref=outs[i].at[1 - c], send_sem=send_sems.at[i],
                recv_sem=recv_sems.at[i], device_id=(x, y, 1 - c), device_id_type=MESH).wait_recv()
        for cpy in sends:
            cpy.wait_send()
        for cpy in local:
            cpy.wait()

    return pl.pallas_call(
        body, name="share_with_sibling", in_specs=[HBM_SPEC] * n, out_specs=[HBM_SPEC] * n,
        out_shape=[jax.ShapeDtypeStruct((2,) + a.shape[1:], a.dtype) for a in parts],
        scratch_shapes=[pltpu.SemaphoreType.DMA((n,)), pltpu.SemaphoreType.DMA((n,)),
                        pltpu.SemaphoreType.DMA((n,))],
    )(*parts)


def _sum_small(v):
    def body(v_ref, out_ref, buf, send_sems, recv_sems):
        x, y, c, _, _ = _place()
        me = 4 * x + 2 * y + c
        buf[me] = v_ref[...]
        flips = [(dx, dy, dc) for dx in (0, 1) for dy in (0, 1) for dc in (0, 1)][1:]
        sends = []
        for k, (dx, dy, dc) in enumerate(flips):
            cpy = pltpu.make_async_remote_copy(
                src_ref=v_ref, dst_ref=buf.at[me], send_sem=send_sems.at[k], recv_sem=recv_sems.at[k],
                device_id=((x + dx) % 2, (y + dy) % 2, (c + dc) % 2), device_id_type=MESH)
            cpy.start()
            sends.append(cpy)
        for k, (dx, dy, dc) in enumerate(flips):
            px, py, pc = (x + dx) % 2, (y + dy) % 2, (c + dc) % 2
            pltpu.make_async_remote_copy(
                src_ref=v_ref, dst_ref=buf.at[4 * px + 2 * py + pc], send_sem=send_sems.at[k],
                recv_sem=recv_sems.at[k], device_id=(px, py, pc), device_id_type=MESH).wait_recv()
        for cpy in sends:
            cpy.wait_send()
        tot = buf[0]
        for i in range(1, N_DEV):
            tot = tot + buf[i]
        out_ref[...] = tot

    return pl.pallas_call(
        body, name="sum_small", out_shape=jax.ShapeDtypeStruct(v.shape, v.dtype),
        in_specs=[pl.BlockSpec(memory_space=pltpu.VMEM)], out_specs=pl.BlockSpec(memory_space=pltpu.VMEM),
        scratch_shapes=[pltpu.VMEM((N_DEV,) + v.shape, v.dtype), pltpu.SemaphoreType.DMA((N_DEV - 1,)),
                        pltpu.SemaphoreType.DMA((N_DEV - 1,))],
    )(v)


def _add_slabs(name, terms):
    arr0 = terms[0][0]
    n = arr0.shape[0] if terms[0][1] is None else 1
    _, r, w = arr0.shape
    tr = 64
    specs = []
    for _, slab in terms:
        if slab is None:
            specs.append(pl.BlockSpec((None, tr, w), lambda i, j: (i, j, 0)))
        else:
            specs.append(pl.BlockSpec((None, tr, w), functools.partial(lambda i, j, sl: (sl, j, 0), sl=slab)))

    def body(*refs):
        tot = refs[0][...]
        for rf in refs[1:-1]:
            tot = tot + rf[...]
        refs[-1][...] = tot

    return pl.pallas_call(
        body, name=name, grid=(n, r // tr), in_specs=specs,
        out_specs=pl.BlockSpec((None, tr, w), lambda i, j: (i, j, 0)),
        out_shape=jax.ShapeDtypeStruct((n, r, w), arr0.dtype),
        compiler_params=_params(("parallel", "parallel")),
    )(*[a for a, _ in terms])


def _adamw(name, w, g, m, v, tm):
    def fn(wv, gv, mv, vv):
        m2 = ADAM_B1 * mv + (1.0 - ADAM_B1) * gv
        v2 = ADAM_B2 * vv + (1.0 - ADAM_B2) * (gv * gv)
        m_hat = m2 / (1.0 - ADAM_B1 ** ADAM_STEP)
        v_hat = v2 / (1.0 - ADAM_B2 ** ADAM_STEP)
        return -ADAM_LR * (m_hat / (jnp.sqrt(v_hat) + ADAM_EPS) + ADAM_WD * wv), m2, v2
    c = w.shape[1]
    return _rows(name, fn, [w, g, m, v], [], [(c, F32)] * 3, tm=tm)


REST_ROWS = 256 + 3 * 128 + 256
REST_SPLITS = (("w_mem_kv", 0, 256), ("w_branch_a", 256, 128), ("w_branch_b", 384, 128),
               ("w_branch_m", 512, 128), ("w_out", 640, 256))


def _rest_pack(t):
    return jnp.concatenate([t[n].reshape(rows, D_MODEL) for n, _, rows in REST_SPLITS], axis=0)


def _rest_unpack(a, shapes):
    return {n: a[r0:r0 + rows].reshape(shapes[n]) for n, r0, rows in REST_SPLITS}


def _small_pack(pre, post, memg, bforget, bmerge):
    pad = jnp.zeros((1, D_MODEL - B_HEADS), F32)
    return jnp.concatenate([pre, post, memg, bmerge.reshape(3, D_MODEL),
                            jnp.concatenate([bforget, pad], axis=1), jnp.zeros((1, D_MODEL), F32)], axis=0)


def _small_unpack(s8):
    return dict(norm_pre_g=s8[0:1], norm_post_g=s8[1:2], norm_mem_g=s8[2:3],
                b_merge=s8[3:6].reshape(1, 3 * D_MODEL), b_forget=s8[6:7, :B_HEADS])


WEIGHTS = ("norm_pre_g", "norm_post_g", "norm_mem_g", "w_in", "b_forget", "b_merge", "w_mem_kv",
           "w_branch_a", "w_branch_b", "w_branch_m", "w_out")
SMALL = ("norm_pre_g", "norm_post_g", "norm_mem_g", "b_forget", "b_merge")


def kernel(x, mem, positions, norm_pre_g, norm_post_g, norm_mem_g, w_in, b_forget, b_merge, w_mem_kv, w_branch_a, w_branch_b, w_branch_m, w_out, loss_target, m_norm_pre_g, m_norm_post_g, m_norm_mem_g, m_w_in, m_b_forget, m_b_merge, m_w_mem_kv, m_w_branch_a, m_w_branch_b, m_w_branch_m, m_w_out, v_norm_pre_g, v_norm_post_g, v_norm_mem_g, v_w_in, v_b_forget, v_b_merge, v_w_mem_kv, v_w_branch_a, v_w_branch_b, v_w_branch_m, v_w_out):
    w = dict(norm_pre_g=norm_pre_g, norm_post_g=norm_post_g, norm_mem_g=norm_mem_g, w_in=w_in[0],
             b_forget=b_forget, b_merge=b_merge, w_mem_kv=w_mem_kv[0], w_branch_a=w_branch_a[0],
             w_branch_b=w_branch_b[0], w_branch_m=w_branch_m[0], w_out=w_out[0])
    mo = dict(norm_pre_g=m_norm_pre_g, norm_post_g=m_norm_post_g, norm_mem_g=m_norm_mem_g, w_in=m_w_in[0],
              b_forget=m_b_forget, b_merge=m_b_merge, w_mem_kv=m_w_mem_kv[0], w_branch_a=m_w_branch_a[0],
              w_branch_b=m_w_branch_b[0], w_branch_m=m_w_branch_m[0], w_out=m_w_out[0])
    vo = dict(norm_pre_g=v_norm_pre_g, norm_post_g=v_norm_post_g, norm_mem_g=v_norm_mem_g, w_in=v_w_in[0],
              b_forget=v_b_forget, b_merge=v_b_merge, w_mem_kv=v_w_mem_kv[0], w_branch_a=v_w_branch_a[0],
              w_branch_b=v_w_branch_b[0], w_branch_m=v_w_branch_m[0], w_out=v_w_out[0])
    s = x.shape[1]
    c = lax.axis_index("c")

    all_in, all_rest = _gather_weights([w["w_in"].astype(BF16).reshape(2, D_MODEL // 2, SHARD_COLS),
                                        _rest_pack(w).astype(BF16).reshape(2, REST_ROWS // 2, D_MODEL)])
    all_in = all_in.reshape(N_CHIPS, D_MODEL, SHARD_COLS)
    w_in_f = jnp.concatenate([all_in[p] for p in range(N_CHIPS)], axis=1)
    all_rest = all_rest.reshape(N_CHIPS, REST_ROWS, D_MODEL)
    w_kv_f = all_rest[:, 0:256].reshape(D_MODEL, D_MODEL)
    w_br_f = [all_rest[:, 256 + 128 * i:384 + 128 * i].reshape(N_CHIPS, 512, 256).transpose(1, 0, 2)
              .reshape(512, D_MODEL) for i in range(3)]
    w_out_f = all_rest[:, 640:896].reshape(D_MODEL, D_MODEL)
    w_main = jnp.concatenate([w_in_f[:, :FB_ORIG], w_in_f[:, FB_ORIG + B_HEADS:]], axis=1)
    w_fb = jnp.concatenate([w_in_f[:, FB_ORIG:FB_ORIG + B_HEADS], jnp.zeros((D_MODEL, HD - B_HEADS), BF16)], axis=1)

    loss_lanes, grad_x, g = _local_step(
        x[0], mem[0], positions.reshape(s, 1), loss_target[0], norm_pre_g, norm_post_g, norm_mem_g,
        w_main, w_fb, b_forget, b_merge, w_kv_f, w_br_f[0], w_br_f[1], w_br_f[2], w_out_f)
    loss = lax.psum(jnp.sum(loss_lanes), ("x", "y", "c"))

    def per_chip(name, p):
        a = g[name]
        if name in ("w_mem_kv", "w_out"):
            return a[256 * p:256 * (p + 1)]
        return a[:, 256 * p:256 * (p + 1)]

    in4 = jnp.stack([g["w_in"][:, SHARD_COLS * p:SHARD_COLS * (p + 1)] for p in range(N_CHIPS)])
    rest4 = jnp.stack([_rest_pack({n: per_chip(n, p) for n, _, _ in REST_SPLITS}) for p in range(N_CHIPS)])
    halves = [in4.reshape(N_CHIPS, 2, D_MODEL // 2, SHARD_COLS),
              rest4.reshape(N_CHIPS, 2, REST_ROWS // 2, D_MODEL)]
    mine = [lax.dynamic_index_in_dim(a, c, axis=1, keepdims=False) for a in halves]
    theirs = [lax.dynamic_index_in_dim(a, 1 - c, axis=1, keepdims=False) for a in halves]
    got = _swap_with_sibling(theirs)
    pair = [_add_slabs("add_pair_%d" % i, [(mine[i], None), (got[i], None)]) for i in range(2)]
    landed = _scatter_to_owners(pair)
    half = [_add_slabs("add_chips_%d" % i, [(landed[i], q) for q in range(N_CHIPS)]) for i in range(2)]
    red_in, red_rest = _share_with_sibling(half)
    gs = {"w_in": red_in.reshape(D_MODEL, SHARD_COLS)}
    gs.update(_rest_unpack(red_rest.reshape(REST_ROWS, D_MODEL), {n: w[n].shape for n, _, _ in REST_SPLITS}))
    gs.update(_small_unpack(_sum_small(_small_pack(
        g["norm_pre_g"], g["norm_post_g"], g["norm_mem_g"], g["b_forget"], g["b_merge"]))))

    delta, new_m, new_v = {}, {}, {}
    for n, tm in (("w_in", 128), ("w_mem_kv", 256), ("w_branch_a", 512), ("w_branch_b", 512),
                  ("w_branch_m", 512), ("w_out", 256)):
        d_, m_, v_ = _adamw("adamw_" + n, w[n], gs[n], mo[n], vo[n], tm)
        delta[n], new_m[n], new_v[n] = d_[None], m_[None], v_[None]
        gs[n] = gs[n][None]
    packs = [_small_pack(*[t[n] for n in ("norm_pre_g", "norm_post_g", "norm_mem_g", "b_forget", "b_merge")])
             for t in (w, gs, mo, vo)]
    for res, store in zip(_adamw("adamw_small", *packs, 8), (delta, new_m, new_v)):
        store.update(_small_unpack(res))

    return (loss, grad_x[None], *[gs[n] for n in WEIGHTS], *[delta[n] for n in WEIGHTS],
            *[new_m[n] for n in WEIGHTS], *[new_v[n] for n in WEIGHTS])
```

```python
import functools

import jax
import jax.numpy as jnp
from jax import lax
from jax.experimental import pallas as pl
from jax.experimental.pallas import tpu as pltpu

F32 = jnp.float32
BF16 = jnp.bfloat16
MESH = pl.DeviceIdType.MESH

D_MODEL = 1024
N_MEM = 256
EPS = 1e-6
NEG = -1e30
ROPE_THETA = 500000.0
ROT_DIM = 32
HD = 128
A_GROUP = 512
DILATIONS = (1, 4, 16)
BAND = 128
B_HEADS = 8
B_HD = 64
N_CHIPS = 4
N_DEV = 8

C_QA, C_KA, C_VA, C_ZA = 0, 1536, 3072, 4608
C_QB, C_KB, C_VB, C_ZB = 5120, 5632, 6144, 6656
C_QM, C_ZM, C_GL = 7168, 7680, 8192
N_MAIN = 11264
FB_ORIG = 6656
IN_COLS = 11272
SHARD_COLS = IN_COLS // N_CHIPS

ADAM_LR, ADAM_B1, ADAM_B2, ADAM_EPS, ADAM_WD, ADAM_STEP = 0.001, 0.9, 0.999, 1e-08, 0.01, 10

VMEM_LIMIT_V7X = 56 * 1024 * 1024

NT = (((1,), (1,)), ((), ()))
NN = (((1,), (0,)), ((), ()))
TN = (((0,), (0,)), ((), ()))


def _params(sem):
    return pltpu.CompilerParams(dimension_semantics=sem, vmem_limit_bytes=VMEM_LIMIT_V7X)


def _dot(a, b, dn=NN):
    return lax.dot_general(a, b, dn, preferred_element_type=F32)


def _sig(z):
    return 1.0 / (1.0 + jnp.exp(-z))


def _rows(name, fn, row_ins, bc_ins, outs, reds=(), tm=512):
    arrs, specs = [], []
    for r in row_ins:
        arr, w, cb = r if isinstance(r, tuple) else (r, r.shape[1], 0)
        arrs.append(arr)
        specs.append(pl.BlockSpec((tm, w), functools.partial(lambda i, cb: (i, cb), cb=cb)))
    for b in bc_ins:
        arrs.append(b)
        specs.append(pl.BlockSpec(b.shape, lambda i: (0, 0)))
    s = arrs[0].shape[0]
    tm = min(tm, s)
    n_in, n_out = len(arrs), len(outs)

    def body(*refs):
        vals = fn(*[r[...] for r in refs[:n_in]])
        if not isinstance(vals, (tuple, list)):
            vals = (vals,)
        for r, v in zip(refs[n_in:n_in + n_out], vals[:n_out]):
            r[...] = v.astype(r.dtype)
        if reds:
            red_refs = refs[n_in + n_out:]

            @pl.when(pl.program_id(0) == 0)
            def _():
                for r in red_refs:
                    r[...] = jnp.zeros_like(r)

            for r, v in zip(red_refs, vals[n_out:]):
                r[...] += v

    out_shape = [jax.ShapeDtypeStruct((s, c), dt) for c, dt in outs]
    out_shape += [jax.ShapeDtypeStruct((1, c), F32) for c in reds]
    out_specs = [pl.BlockSpec((tm, c), lambda i: (i, 0)) for c, _ in outs]
    out_specs += [pl.BlockSpec((1, c), lambda i: (0, 0)) for c in reds]
    res = pl.pallas_call(
        body, name=name, grid=(s // tm,), in_specs=specs, out_specs=out_specs, out_shape=out_shape,
        compiler_params=_params(("arbitrary",) if reds else ("parallel",)),
    )(*arrs)
    return res


def _mm(name, a, b, mode, out_dtype, tm=1024, tn=1024, tk=1024):
    if mode == "nn":
        (m, k), (_, n) = a.shape, b.shape
    elif mode == "nt":
        (m, k), (n, _) = a.shape, b.shape
    else:
        (k, m), (_, n) = a.shape, b.shape
    tm, tn, tk = min(tm, m), min(tn, n), min(tk, k)
    nk = k // tk
    dn = {"nn": NN, "nt": NT, "tn": TN}[mode]

    def body(a_ref, b_ref, o_ref, *acc):
        part = _dot(a_ref[...].astype(BF16), b_ref[...].astype(BF16), dn)
        if nk == 1:
            o_ref[...] = part.astype(o_ref.dtype)
        else:
            kk = pl.program_id(2)

            @pl.when(kk == 0)
            def _():
                acc[0][...] = part

            @pl.when(kk > 0)
            def _():
                acc[0][...] += part

            @pl.when(kk == nk - 1)
            def _():
                o_ref[...] = acc[0][...].astype(o_ref.dtype)

    a_spec = (pl.BlockSpec((tk, tm), lambda i, j, kk: (kk, i)) if mode == "tn"
              else pl.BlockSpec((tm, tk), lambda i, j, kk: (i, kk)))
    b_spec = (pl.BlockSpec((tn, tk), lambda i, j, kk: (j, kk)) if mode == "nt"
              else pl.BlockSpec((tk, tn), lambda i, j, kk: (kk, j)))
    return pl.pallas_call(
        body, name=name, grid=(m // tm, n // tn, nk), in_specs=[a_spec, b_spec],
        out_specs=pl.BlockSpec((tm, tn), lambda i, j, kk: (i, j)),
        out_shape=jax.ShapeDtypeStruct((m, n), out_dtype),
        scratch_shapes=[pltpu.VMEM((tm, tn), F32)] if nk > 1 else [],
        compiler_params=_params(("parallel", "parallel", "arbitrary")),
    )(a, b)


def _rms_fwd(name, x, g):
    def fn(xv, gv):
        r = lax.rsqrt(jnp.mean(xv * xv, axis=-1, keepdims=True) + EPS)
        return (xv * r * gv,)
    return _rows(name, fn, [x], [g], [(x.shape[1], BF16)], tm=min(512, x.shape[0]))[0]


def _rope_tables(pos, inv):
    ang = pos.astype(F32) * inv
    lane = lax.broadcasted_iota(jnp.int32, ang.shape, 1)
    c = jnp.where(lane < ROT_DIM, jnp.cos(ang), 1.0)
    sn = jnp.sin(ang)
    sg = jnp.where(lane < ROT_DIM // 2, -sn, jnp.where(lane < ROT_DIM, sn, 0.0))
    return c, sg, lane


def _rope_apply(x, c, sg, lane):
    outs = []
    for h in range(x.shape[1] // HD):
        xh = x[:, h * HD:(h + 1) * HD].astype(F32)
        swap = jnp.where(lane < ROT_DIM // 2, pltpu.roll(xh, HD - ROT_DIM // 2, 1),
                         pltpu.roll(xh, ROT_DIM // 2, 1))
        outs.append(xh * c + swap * sg)
    return jnp.concatenate(outs, axis=1)


def _rope_fwd(u, pos, inv):
    def fn(q, k, p, iv):
        c, sg, lane = _rope_tables(p, iv)
        return _rope_apply(q, c, sg, lane), _rope_apply(k, c, sg, lane)
    return _rows("rope_fwd", fn, [(u, 1536, 0), (u, 1536, 1), pos], [inv],
                 [(1536, BF16), (1536, BF16)], tm=256)


def _rope_bwd(dq, dk, pos, inv):
    def fn(q, k, p, iv):
        c, sg, lane = _rope_tables(p, iv)
        return _rope_apply(q, c, -sg, lane), _rope_apply(k, c, -sg, lane)
    return _rows("rope_bwd", fn, [dq, dk, pos], [inv], [(1536, BF16), (1536, BF16)], tm=256)


def _lane_pack(cols, like):
    lane = lax.broadcasted_iota(jnp.int32, like, 1)
    out = jnp.zeros(like, F32)
    for h, cvec in enumerate(cols):
        out = jnp.where(lane == h, cvec, out)
    return out


def _band_specs(l, d, tq):
    nsb = tq // BAND
    nblk = l // BAND
    cur = pl.BlockSpec((tq, A_GROUP), lambda r, i: (i, r))
    prev = pl.BlockSpec((BAND, A_GROUP), lambda r, i: (jnp.maximum(i * nsb - 1, 0), r))
    nxt = pl.BlockSpec((BAND, A_GROUP), lambda r, i: (jnp.minimum((i + 1) * nsb, nblk - 1), r))
    st_cur = pl.BlockSpec((tq, HD), lambda r, i: (i, r))
    st_nxt = pl.BlockSpec((BAND, HD), lambda r, i: (jnp.minimum((i + 1) * nsb, nblk - 1), r))
    return nsb, cur, prev, nxt, st_cur, st_nxt


def _band_mask_q(i, first_tile):
    qr = lax.broadcasted_iota(jnp.int32, (BAND, 2 * BAND), 0)
    kc = lax.broadcasted_iota(jnp.int32, (BAND, 2 * BAND), 1)
    in_prev = (kc < BAND) & (kc >= qr)
    in_cur = (kc >= BAND) & (kc - BAND <= qr)
    if i == 0:
        in_prev = in_prev & jnp.logical_not(first_tile)
    return in_prev | in_cur


def _band_mask_k(j, nsb, last_tile):
    qr = lax.broadcasted_iota(jnp.int32, (2 * BAND, BAND), 0)
    kc = lax.broadcasted_iota(jnp.int32, (2 * BAND, BAND), 1)
    same = (qr < BAND) & (kc <= qr)
    nxt = (qr >= BAND) & (kc >= qr - BAND)
    if j == nsb - 1:
        nxt = nxt & jnp.logical_not(last_tile)
    return same | nxt


def _band_fwd(name, q, k, v, d):
    l = q.shape[0]
    tq = min(512, l)
    nsb, cur, prev, _, st_cur, _ = _band_specs(l, d, tq)
    scale = HD ** -0.5

    def body(q_ref, kc_ref, kp_ref, vc_ref, vp_ref, o_ref, lse_ref):
        first = pl.program_id(1) == 0
        for i in range(nsb):
            lses = []
            mask = _band_mask_q(i, first)
            for h in range(4):
                cs = slice(h * HD, (h + 1) * HD)
                qv = q_ref[i * BAND:(i + 1) * BAND, cs]
                if i == 0:
                    kk = jnp.concatenate([kp_ref[:, cs], kc_ref[0:BAND, cs]], axis=0)
                    vv = jnp.concatenate([vp_ref[:, cs], vc_ref[0:BAND, cs]], axis=0)
                else:
                    kk = kc_ref[(i - 1) * BAND:(i + 1) * BAND, cs]
                    vv = vc_ref[(i - 1) * BAND:(i + 1) * BAND, cs]
                s = jnp.where(mask, _dot(qv, kk, NT) * scale, NEG)
                m = jnp.max(s, axis=-1, keepdims=True)
                p = jnp.exp(s - m)
                den = jnp.sum(p, axis=-1, keepdims=True)
                o_ref[i * BAND:(i + 1) * BAND, cs] = _dot(p.astype(BF16), vv) / den
                lses.append(m + jnp.log(den))
            lse_ref[i * BAND:(i + 1) * BAND, :] = _lane_pack(lses, (BAND, HD))

    return pl.pallas_call(
        body, name=name, grid=(d, l // tq), in_specs=[cur, cur, prev, cur, prev],
        out_specs=[cur, st_cur],
        out_shape=[jax.ShapeDtypeStruct((l, d * A_GROUP), F32), jax.ShapeDtypeStruct((l, d * HD), F32)],
        compiler_params=_params(("parallel", "parallel")),
    )(q, k, k, v, v)


def _band_dq(name, q, k, v, dy, lse, delta, d):
    l = q.shape[0]
    tq = min(512, l)
    nsb, cur, prev, _, st_cur, _ = _band_specs(l, d, tq)
    scale = HD ** -0.5

    def body(q_ref, kc_ref, kp_ref, vc_ref, vp_ref, dy_ref, lse_ref, dl_ref, dq_ref):
        first = pl.program_id(1) == 0
        for i in range(nsb):
            mask = _band_mask_q(i, first)
            rs = slice(i * BAND, (i + 1) * BAND)
            for h in range(4):
                cs = slice(h * HD, (h + 1) * HD)
                if i == 0:
                    kk = jnp.concatenate([kp_ref[:, cs], kc_ref[0:BAND, cs]], axis=0)
                    vv = jnp.concatenate([vp_ref[:, cs], vc_ref[0:BAND, cs]], axis=0)
                else:
                    kk = kc_ref[(i - 1) * BAND:(i + 1) * BAND, cs]
                    vv = vc_ref[(i - 1) * BAND:(i + 1) * BAND, cs]
                s = jnp.where(mask, _dot(q_ref[rs, cs], kk, NT) * scale, NEG)
                p = jnp.exp(s - lse_ref[rs, h:h + 1])
                dp = _dot(dy_ref[rs, cs], vv, NT)
                ds = p * (dp - dl_ref[rs, h:h + 1])
                dq_ref[rs, cs] = (_dot(ds.astype(BF16), kk) * scale).astype(dq_ref.dtype)

    return pl.pallas_call(
        body, name=name, grid=(d, l // tq),
        in_specs=[cur, cur, prev, cur, prev, cur, st_cur, st_cur], out_specs=cur,
        out_shape=jax.ShapeDtypeStruct((l, d * A_GROUP), BF16),
        compiler_params=_params(("parallel", "parallel")),
    )(q, k, k, v, v, dy, lse, delta)


def _band_dkv(name, q, k, v, dy, lse, delta, d):
    l = q.shape[0]
    tq = min(512, l)
    nsb, cur, _, nxt, st_cur, st_nxt = _band_specs(l, d, tq)
    scale = HD ** -0.5
    ntile = l // tq

    def body(k_ref, v_ref, qc_ref, qn_ref, dyc_ref, dyn_ref, lc_ref, ln_ref, dc_ref, dn_ref,
             dk_ref, dv_ref):
        last = pl.program_id(1) == ntile - 1

        def win(c_ref, n_ref, j, cs):
            if j == nsb - 1:
                return jnp.concatenate([c_ref[j * BAND:(j + 1) * BAND, cs], n_ref[:, cs]], axis=0)
            return c_ref[j * BAND:(j + 2) * BAND, cs]

        for j in range(nsb):
            mask = _band_mask_k(j, nsb, last)
            rs = slice(j * BAND, (j + 1) * BAND)
            for h in range(4):
                cs = slice(h * HD, (h + 1) * HD)
                hs = slice(h, h + 1)
                qw = win(qc_ref, qn_ref, j, cs)
                dyw = win(dyc_ref, dyn_ref, j, cs)
                s = jnp.where(mask, _dot(qw, k_ref[rs, cs], NT) * scale, NEG)
                p = jnp.exp(s - win(lc_ref, ln_ref, j, hs))
                dp = _dot(dyw, v_ref[rs, cs], NT)
                ds = p * (dp - win(dc_ref, dn_ref, j, hs))
                dv_ref[rs, cs] = _dot(p.astype(BF16), dyw, TN).astype(dv_ref.dtype)
                dk_ref[rs, cs] = (_dot(ds.astype(BF16), qw, TN) * scale).astype(dk_ref.dtype)

    shp = jax.ShapeDtypeStruct((l, d * A_GROUP), BF16)
    return pl.pallas_call(
        body, name=name, grid=(d, ntile),
        in_specs=[cur, cur, cur, nxt, cur, nxt, st_cur, st_nxt, st_cur, st_nxt],
        out_specs=[cur, cur], out_shape=[shp, shp],
        compiler_params=_params(("parallel", "parallel")),
    )(k, v, q, q, dy, dy, lse, lse, delta, delta)


def _split3(x):
    hi = x.astype(BF16)
    r1 = x - hi.astype(F32)
    mid = r1.astype(BF16)
    lo = (r1 - mid.astype(F32)).astype(BF16)
    return hi, mid, lo


def _fox_prep(z, b):
    h, s = z.shape
    blk = min(512, s)

    def body(z_ref, b_ref, c_ref):
        r = lax.broadcasted_iota(jnp.int32, (blk, blk), 0)
        cidx = lax.broadcasted_iota(jnp.int32, (blk, blk), 1)
        tri = (r <= cidx).astype(BF16)
        carry = jnp.zeros((h, 1), F32)
        for t in range(s // blk):
            zz = z_ref[:, t * blk:(t + 1) * blk] + b_ref[...]
            lf = jnp.minimum(zz, 0.0) - jnp.log(1.0 + jnp.exp(-jnp.abs(zz)))
            hi, mid, lo = _split3(lf)
            cs = _dot(hi, tri) + _dot(mid, tri) + _dot(lo, tri) + carry
            c_ref[:, t * blk:(t + 1) * blk] = cs
            carry = cs[:, blk - 1:blk]

    return pl.pallas_call(body, name="fox_prep", out_shape=jax.ShapeDtypeStruct((h, s), F32))(z, b)


def _fox_prep_bwd(dck, dcq, z, b):
    h, s = z.shape
    blk = min(512, s)

    def body(dck_ref, dcq_ref, z_ref, b_ref, dz_ref, db_ref):
        r = lax.broadcasted_iota(jnp.int32, (blk, blk), 0)
        cidx = lax.broadcasted_iota(jnp.int32, (blk, blk), 1)
        tri = (r >= cidx).astype(BF16)
        carry = jnp.zeros((h, 1), F32)
        tot = jnp.zeros((h, 1), F32)
        for t in reversed(range(s // blk)):
            hi, mid, lo = _split3(dcq_ref[:, t * blk:(t + 1) * blk] - dck_ref[:, t * blk:(t + 1) * blk])
            rc = _dot(hi, tri) + _dot(mid, tri) + _dot(lo, tri) + carry
            carry = rc[:, 0:1]
            zz = z_ref[:, t * blk:(t + 1) * blk] + b_ref[...]
            dz = rc * _sig(-zz)
            dz_ref[:, t * blk:(t + 1) * blk] = dz
            tot = tot + jnp.sum(dz, axis=-1, keepdims=True)
        db_ref[...] = tot

    return pl.pallas_call(
        body, name="fox_prep_bwd",
        out_shape=[jax.ShapeDtypeStruct((h, s), F32), jax.ShapeDtypeStruct((h, 1), F32)])(dck, dcq, z, b)


FOX_W = 128
FOX_A = B_HD
FOX_B = B_HD + 3


def _fox_aug(name, x, stat, muls, ones_lo, stat_lo):
    def fn(xv, st=None):
        lane = lax.broadcasted_iota(jnp.int32, xv.shape, 1)
        aux = jnp.zeros(xv.shape, F32)
        if ones_lo is not None:
            aux = jnp.where((lane >= ones_lo) & (lane < ones_lo + 3), 1.0, aux)
        if st is not None:
            neg = -st
            hi = neg.astype(BF16).astype(F32)
            mid = (neg - hi).astype(BF16).astype(F32)
            lo = neg - hi - mid
            aux = jnp.where(lane == stat_lo, hi, jnp.where(lane == stat_lo + 1, mid,
                                                           jnp.where(lane == stat_lo + 2, lo, aux)))
        xf = xv.astype(F32)
        return tuple(jnp.where(lane < B_HD, xf * mul, aux) for mul in muls)

    ins = [x] if stat is None else [x, stat]
    return _rows(name, fn, ins, [], [(FOX_W, BF16)] * len(muls), tm=1024)


def _fox_fwd(qf, kb, vt4, t):
    h, s, w = qf.shape
    nt = s // t

    def body(q_ref, k_ref, v_ref, o_ref, lse_ref):
        i = pl.program_id(1)
        qv = q_ref[...]
        krow = lax.broadcasted_iota(jnp.int32, (t, t), 0)
        qcol = lax.broadcasted_iota(jnp.int32, (t, t), 1)

        def tile(j, carry, diag):
            m, acc = carry
            j0 = pl.multiple_of(j * t, t)
            st = _dot(k_ref[pl.ds(j0, t), :], qv, NT)
            if diag:
                st = jnp.where(krow <= qcol, st, NEG)
            m2 = jnp.maximum(m, jnp.max(st, axis=0, keepdims=True))
            p = jnp.exp(st - m2).astype(BF16)
            return m2, jnp.exp(m - m2) * acc + _dot(v_ref[j], p)

        init = (jnp.full((1, t), NEG, F32), jnp.zeros((w, t), F32))
        carry = lax.fori_loop(0, i, lambda j, c: tile(j, c, False), init)
        m, acc = tile(i, carry, True)
        den = acc[B_HD:B_HD + 1, :]
        o_ref[...] = (acc[0:B_HD, :] / den).astype(o_ref.dtype)
        lse_ref[...] = m + jnp.log(den)

    return pl.pallas_call(
        body, name="fox_fwd", grid=(h, nt),
        in_specs=[pl.BlockSpec((None, t, w), lambda hh, i: (hh, i, 0)),
                  pl.BlockSpec((None, s, w), lambda hh, i: (hh, 0, 0)),
                  pl.BlockSpec((None, nt, w, t), lambda hh, i: (hh, 0, 0, 0))],
        out_specs=[pl.BlockSpec((None, B_HD, t), lambda hh, i: (hh, 0, i)),
                   pl.BlockSpec((None, 1, t), lambda hh, i: (hh, 0, i))],
        out_shape=[jax.ShapeDtypeStruct((h, B_HD, s), BF16), jax.ShapeDtypeStruct((h, 1, s), F32)],
        compiler_params=_params(("parallel", "parallel")),
    )(qf, kb, vt4)


def _fox_bwd(qb, dob, kb, kst4, vb, t):
    h, s, w = qb.shape
    nt = s // t

    def body(q_ref, do_ref, k_ref, kt_ref, v_ref, dqt_ref, dk_ref, dv_ref):
        j = pl.program_id(1)

        @pl.when(j == 0)
        def _():
            dqt_ref[...] = jnp.zeros_like(dqt_ref)

        kk, kt, vv = k_ref[...], kt_ref[...], v_ref[...]
        krow = lax.broadcasted_iota(jnp.int32, (t, t), 0)
        qcol = lax.broadcasted_iota(jnp.int32, (t, t), 1)

        def tile(i, carry, diag):
            dk, dv = carry
            i0 = pl.multiple_of(i * t, t)
            qi, doi = q_ref[pl.ds(i0, t), :], do_ref[pl.ds(i0, t), :]
            st = _dot(kk, qi, NT)
            if diag:
                st = jnp.where(krow <= qcol, st, NEG)
            pt = jnp.exp(st)
            dsb = (pt * _dot(vv, doi, NT)).astype(BF16)
            dqt_ref[i] += _dot(kt, dsb)
            return dk + _dot(dsb, qi), dv + _dot(pt.astype(BF16), doi)

        init = (jnp.zeros((t, w), F32), jnp.zeros((t, w), F32))
        carry = tile(j, init, True)
        dk, dv = lax.fori_loop(j + 1, nt, lambda i, c: tile(i, c, False), carry)
        dk_ref[...] = dk
        dv_ref[...] = dv

    full = pl.BlockSpec((None, s, w), lambda hh, j: (hh, 0, 0))
    tl = pl.BlockSpec((None, t, w), lambda hh, j: (hh, j, 0))
    return pl.pallas_call(
        body, name="fox_bwd", grid=(h, nt),
        in_specs=[full, full, tl, pl.BlockSpec((None, None, w, t), lambda hh, j: (hh, j, 0, 0)), tl],
        out_specs=[pl.BlockSpec((None, nt, w, t), lambda hh, j: (hh, 0, 0, 0)), tl, tl],
        out_shape=[jax.ShapeDtypeStruct((h, nt, w, t), F32), jax.ShapeDtypeStruct((h, s, w), F32),
                   jax.ShapeDtypeStruct((h, s, w), F32)],
        compiler_params=_params(("parallel", "arbitrary")),
    )(qb, dob, kb, kst4, vb)


def _mem_fwd(u, mkv, tq=512):
    s = u.shape[0]
    scale = HD ** -0.5

    def body(q_ref, mk_ref, mv_ref, o_ref, lse_ref):
        lses = []
        for h in range(4):
            cs = slice(h * HD, (h + 1) * HD)
            sc = _dot(q_ref[:, cs], mk_ref[:, cs], NT) * scale
            m = jnp.max(sc, axis=-1, keepdims=True)
            p = jnp.exp(sc - m)
            den = jnp.sum(p, axis=-1, keepdims=True)
            o_ref[:, cs] = (_dot(p.astype(BF16), mv_ref[:, cs]) / den).astype(o_ref.dtype)
            lses.append(m + jnp.log(den))
        lse_ref[...] = _lane_pack(lses, (tq, HD))

    return pl.pallas_call(
        body, name="mem_fwd", grid=(s // tq,),
        in_specs=[pl.BlockSpec((tq, 512), lambda i: (i, C_QM // 512)),
                  pl.BlockSpec((N_MEM, 512), lambda i: (0, 0)),
                  pl.BlockSpec((N_MEM, 512), lambda i: (0, 1))],
        out_specs=[pl.BlockSpec((tq, 512), lambda i: (i, 0)), pl.BlockSpec((tq, HD), lambda i: (i, 0))],
        out_shape=[jax.ShapeDtypeStruct((s, 512), BF16), jax.ShapeDtypeStruct((s, HD), F32)],
        compiler_params=_params(("parallel",)),
    )(u, mkv, mkv)


def _mem_bwd(u, mkv, o, do, lse, tq=512):
    s = u.shape[0]
    scale = HD ** -0.5

    def body(q_ref, mk_ref, mv_ref, o_ref, do_ref, lse_ref, dq_ref, dmk_ref, dmv_ref):
        @pl.when(pl.program_id(0) == 0)
        def _():
            dmk_ref[...] = jnp.zeros_like(dmk_ref)
            dmv_ref[...] = jnp.zeros_like(dmv_ref)

        for h in range(4):
            cs = slice(h * HD, (h + 1) * HD)
            qv, dov = q_ref[:, cs], do_ref[:, cs]
            sc = _dot(qv, mk_ref[:, cs], NT) * scale
            p = jnp.exp(sc - lse_ref[:, h:h + 1])
            delta = jnp.sum(dov.astype(F32) * o_ref[:, cs].astype(F32), axis=-1, keepdims=True)
            ds = p * (_dot(dov, mv_ref[:, cs], NT) - delta)
            dsb = ds.astype(BF16)
            dq_ref[:, cs] = (_dot(dsb, mk_ref[:, cs]) * scale).astype(dq_ref.dtype)
            dmk_ref[:, cs] += _dot(dsb, qv, TN) * scale
            dmv_ref[:, cs] += _dot(p.astype(BF16), dov, TN)

    row = pl.BlockSpec((tq, 512), lambda i: (i, 0))
    acc = pl.BlockSpec((N_MEM, 512), lambda i: (0, 0))
    return pl.pallas_call(
        body, name="mem_bwd", grid=(s // tq,),
        in_specs=[pl.BlockSpec((tq, 512), lambda i: (i, C_QM // 512)),
                  pl.BlockSpec((N_MEM, 512), lambda i: (0, 0)),
                  pl.BlockSpec((N_MEM, 512), lambda i: (0, 1)),
                  row, row, pl.BlockSpec((tq, HD), lambda i: (i, 0))],
        out_specs=[row, acc, acc],
        out_shape=[jax.ShapeDtypeStruct((s, 512), BF16), jax.ShapeDtypeStruct((N_MEM, 512), F32),
                   jax.ShapeDtypeStruct((N_MEM, 512), F32)],
        compiler_params=_params(("arbitrary",)),
    )(u, mkv, mkv, o, do, lse)


def _heads_major(a, col0):
    s = a.shape[0]
    return a[:, col0:col0 + 512].reshape(s, B_HEADS, B_HD).transpose(1, 0, 2)


def _token_major(a):
    h, s, dh = a.shape
    return a.transpose(1, 0, 2).reshape(s, h * dh)


def _class_view(a, d):
    s, c = a.shape
    return a.reshape(s // d, d * c)


def _local_step(x, mem, pos, target, g_pre, g_post, g_mem, w_main, w_fb, b_forget, b_merge,
                w_mem_kv, w_ba, w_bb, w_bm, w_out):
    s = x.shape[0]
    t_fox = min(512, s)
    nt = s // t_fox
    half = ROT_DIM // 2
    inv = ROPE_THETA ** (-jnp.arange(half, dtype=F32) / half)
    inv128 = jnp.concatenate([inv, inv, jnp.zeros((HD - ROT_DIM,), F32)]).reshape(1, HD)

    h = _rms_fwd("norm_pre", x, g_pre)
    u = _mm("proj_in", h, w_main, "nn", BF16, tn=512)
    ufb = _mm("proj_fb", h, w_fb, "nn", F32)
    memn = _rms_fwd("norm_mem", mem, g_mem)
    mkv = _mm("proj_mem", memn, w_mem_kv, "nn", BF16)

    q_rot, k_rot = _rope_fwd(u, pos, inv128)
    os_, lses = [], []
    views = []
    for g, d in enumerate(DILATIONS):
        qv = _class_view(q_rot[:, g * A_GROUP:(g + 1) * A_GROUP], d)
        kv = _class_view(k_rot[:, g * A_GROUP:(g + 1) * A_GROUP], d)
        vv = _class_view(u[:, C_VA + g * A_GROUP:C_VA + (g + 1) * A_GROUP], d)
        views.append((qv, kv, vv))
        o_g, lse_g = _band_fwd("band_fwd%d" % g, qv, kv, vv, d)
        os_.append(o_g.reshape(s, A_GROUP))
        lses.append(lse_g.reshape(s, HD))

    def merge_a(o1, o2, o3, l1, l2, l3, za):
        ys, tots = [], []
        for hh in range(4):
            cs, hs = slice(hh * HD, (hh + 1) * HD), slice(hh, hh + 1)
            mx = jnp.maximum(jnp.maximum(l1[:, hs], l2[:, hs]), l3[:, hs])
            e1, e2, e3 = jnp.exp(l1[:, hs] - mx), jnp.exp(l2[:, hs] - mx), jnp.exp(l3[:, hs] - mx)
            den = e1 + e2 + e3
            ys.append((e1 * o1[:, cs] + e2 * o2[:, cs] + e3 * o3[:, cs]) / den)
            tots.append(mx + jnp.log(den))
        y = jnp.concatenate(ys, axis=1)
        zf = za.astype(F32)
        return y, y * (zf * _sig(zf)), _lane_pack(tots, l1.shape)

    y_a, yg_a, lse_a = _rows("merge_a", merge_a, os_ + lses + [(u, 512, C_ZA // 512)], [],
                             [(512, BF16), (512, BF16), (HD, F32)])

    zrow = ufb[:, :B_HEADS].T
    c = _fox_prep(zrow, b_forget.reshape(B_HEADS, 1))
    n_hs = B_HEADS * s

    def wide(a):
        return jnp.pad(a.reshape(n_hs, B_HD), ((0, 0), (0, FOX_W - B_HD)))

    def tiles_t(a):
        return a.reshape(B_HEADS, nt, t_fox, FOX_W).transpose(0, 1, 3, 2)

    q_w, k_w, v_w = wide(_heads_major(u, C_QB)), wide(_heads_major(u, C_KB)), wide(_heads_major(u, C_VB))
    c_col = c.reshape(n_hs, 1)
    qf = _fox_aug("fox_aug_qf", q_w, None, (B_HD ** -0.5,), FOX_A, None)[0].reshape(B_HEADS, s, FOX_W)
    kb, ks = _fox_aug("fox_aug_k", k_w, c_col, (1.0, B_HD ** -0.5), FOX_B, FOX_A)
    kb = kb.reshape(B_HEADS, s, FOX_W)
    vb = _fox_aug("fox_aug_v", v_w, None, (1.0,), FOX_A, None)[0]
    ot, lse_b = _fox_fwd(qf, kb, tiles_t(vb), t_fox)
    vb = vb.reshape(B_HEADS, s, FOX_W)
    y_b = ot.transpose(2, 0, 1).reshape(s, B_HEADS * B_HD)

    y_m, lse_m = _mem_fwd(u, mkv)

    def gate(y, z):
        zf = z.astype(F32)
        return (y.astype(F32) * (zf * _sig(zf)),)

    yg_b = _rows("gate_b", gate, [y_b, (u, 512, C_ZB // 512)], [], [(512, BF16)])[0]
    yg_m = _rows("gate_m", gate, [y_m, (u, 512, C_ZM // 512)], [], [(512, BF16)])[0]

    br_a = _mm("branch_a", yg_a, w_ba, "nn", BF16)
    br_b = _mm("branch_b", yg_b, w_bb, "nn", BF16)
    br_m = _mm("branch_m", yg_m, w_bm, "nn", BF16)
    gl = [(u, 1024, C_GL // 1024 + i) for i in range(3)]
    bm3 = b_merge.reshape(3, D_MODEL)

    def merge(g0, g1, g2, b0, b1, b2, bm):
        tot = 0.0
        for i, (gv, bv) in enumerate(((g0, b0), (g1, b1), (g2, b2))):
            tot = tot + _sig(gv.astype(F32) + bm[i:i + 1, :]) * bv.astype(F32)
        return (tot,)

    merged = _rows("merge_gates", merge, gl + [br_a, br_b, br_m], [bm3], [(D_MODEL, BF16)])[0]
    out = _mm("proj_out", merged, w_out, "nn", F32)

    def tail(xv, ov, tv, gv):
        r = lax.rsqrt(jnp.mean(ov * ov, axis=-1, keepdims=True) + EPS)
        n = ov * r
        err = xv + n * gv - tv
        dy = err * (1.0 / D_MODEL)
        dn = dy * gv
        dout = r * (dn - n * jnp.mean(dn * n, axis=-1, keepdims=True))
        return (dy, dout, jnp.sum(0.5 * err * err * (1.0 / D_MODEL), axis=0, keepdims=True),
                jnp.sum(dy * n, axis=0, keepdims=True))

    dy, dout, loss_lanes, g_post_grad = _rows(
        "tail", tail, [x, out, target], [g_post], [(D_MODEL, F32), (D_MODEL, BF16)],
        reds=[D_MODEL, D_MODEL], tm=256)

    dmerged = _mm("d_merged", dout, w_out, "nt", BF16)
    gw_out = _mm("g_w_out", merged, dout, "tn", F32)

    def merge_bwd(dm, g0, g1, g2, b0, b1, b2, bm):
        dmf = dm.astype(F32)
        dbs, dgs, sums = [], [], []
        for i, (gv, bv) in enumerate(((g0, b0), (g1, b1), (g2, b2))):
            sg = _sig(gv.astype(F32) + bm[i:i + 1, :])
            dbs.append(dmf * sg)
            dg = dmf * bv.astype(F32) * sg * (1.0 - sg)
            dgs.append(dg)
            sums.append(jnp.sum(dg, axis=0, keepdims=True))
        return tuple(dbs + dgs + sums)

    res = _rows("merge_bwd", merge_bwd, [dmerged] + gl + [br_a, br_b, br_m], [bm3],
                [(D_MODEL, BF16)] * 6, reds=[D_MODEL] * 3, tm=256)
    dbr, dgl, g_bmerge = res[0:3], res[3:6], jnp.concatenate(res[6:9], axis=1)

    dyg, gw_branch = [], []
    for nm, dbv, wv, ygv in (("a", dbr[0], w_ba, yg_a), ("b", dbr[1], w_bb, yg_b), ("m", dbr[2], w_bm, yg_m)):
        dyg.append(_mm("d_yg_" + nm, dbv, wv, "nt", BF16))
        gw_branch.append(_mm("g_w_branch_" + nm, ygv, dbv, "tn", F32))

    def gate_bwd(dg, y, z):
        dgf, yf, zf = dg.astype(F32), y.astype(F32), z.astype(F32)
        sg = _sig(zf)
        return dgf * (zf * sg), dgf * yf * (sg * (1.0 + zf * (1.0 - sg)))

    def gate_bwd_a(dg, y, z):
        dyv, dz = gate_bwd(dg, y, z)
        prod = dyv * y.astype(F32)
        dl = [jnp.sum(prod[:, hh * HD:(hh + 1) * HD], axis=-1, keepdims=True) for hh in range(4)]
        return dyv, dz, _lane_pack(dl, (dg.shape[0], HD))

    dy_a, dz_a, delta_a = _rows("gate_bwd_a", gate_bwd_a, [dyg[0], y_a, (u, 512, C_ZA // 512)], [],
                                [(512, BF16), (512, BF16), (HD, F32)])
    dy_b, dz_b = _rows("gate_bwd_b", gate_bwd, [dyg[1], y_b, (u, 512, C_ZB // 512)], [],
                       [(512, BF16), (512, BF16)])
    dy_m, dz_m = _rows("gate_bwd_m", gate_bwd, [dyg[2], y_m, (u, 512, C_ZM // 512)], [],
                       [(512, BF16), (512, BF16)])

    dq_m, dmk, dmv = _mem_bwd(u, mkv, y_m, dy_m, lse_m)
    dmkv = jnp.concatenate([dmk, dmv], axis=1)
    gw_mem_kv = _mm("g_w_mem_kv", memn, dmkv, "tn", F32)
    dmemn = _mm("d_memn", dmkv, w_mem_kv, "nt", F32)

    def mem_gain_grad(mv, dv):
        r = lax.rsqrt(jnp.mean(mv * mv, axis=-1, keepdims=True) + EPS)
        return (jnp.sum(dv * mv * r, axis=0, keepdims=True),)

    g_mem_grad = _rows("g_norm_mem", mem_gain_grad, [mem, dmemn], [], [], reds=[D_MODEL], tm=N_MEM)[0]

    dob = _heads_major(dy_b, 0)

    def fox_delta(a, b):
        return (jnp.sum(a.astype(F32) * b.astype(F32), axis=-1, keepdims=True),)

    ob = ot.transpose(0, 2, 1).reshape(n_hs, B_HD)
    delta_b = _rows("fox_delta", fox_delta, [dob.reshape(n_hs, B_HD), ob], [], [(1, F32)], tm=min(2048, s))[0]
    qb = _fox_aug("fox_aug_qb", q_w, lse_b.reshape(n_hs, 1), (B_HD ** -0.5,), FOX_A, FOX_B)[0]
    dow = _fox_aug("fox_aug_do", wide(dob), delta_b, (1.0,), None, FOX_A)[0]
    dqt, dkw, dvw = _fox_bwd(qb.reshape(B_HEADS, s, FOX_W), dow.reshape(B_HEADS, s, FOX_W), kb, tiles_t(ks), vb,
                             t_fox)
    dqb = dqt[:, :, :B_HD, :].transpose(0, 1, 3, 2).reshape(B_HEADS, s, B_HD)
    dkb, dvb = dkw[:, :, :B_HD].astype(BF16), dvw[:, :, :B_HD].astype(BF16)
    dzrow, g_bforget = _fox_prep_bwd(dkw[:, :, FOX_A], dqt[:, :, FOX_B, :].reshape(B_HEADS, s), zrow,
                                     b_forget.reshape(B_HEADS, 1))
    dfb = jnp.zeros((s, HD), BF16).at[:, :B_HEADS].set(dzrow.T.astype(BF16))

    dqs, dks, dvs = [], [], []
    for g, d in enumerate(DILATIONS):
        qv, kv, vv = views[g]
        dyv, lv, dlv = _class_view(dy_a, d), _class_view(lse_a, d), _class_view(delta_a, d)
        dqs.append(_band_dq("band_dq%d" % g, qv, kv, vv, dyv, lv, dlv, d).reshape(s, A_GROUP))
        dk_g, dv_g = _band_dkv("band_dkv%d" % g, qv, kv, vv, dyv, lv, dlv, d)
        dks.append(dk_g.reshape(s, A_GROUP))
        dvs.append(dv_g.reshape(s, A_GROUP))
    dqa, dka = _rope_bwd(jnp.concatenate(dqs, axis=1), jnp.concatenate(dks, axis=1), pos, inv128)

    du = jnp.concatenate(
        [dqa, dka] + dvs + [dz_a, _token_major(dqb).astype(BF16), _token_major(dkb), _token_major(dvb),
                            dz_b, dq_m, dz_m] + list(dgl), axis=1)

    gw_main = _mm("g_w_main", h, du, "tn", F32)
    gw_fb = _mm("g_w_fb", h, dfb, "tn", F32)
    dh_main = _mm("d_h", du, w_main, "nt", F32, tk=1024)
    dh_fb = _mm("d_h_fb", dfb, w_fb, "nt", F32)

    def pre_bwd(xv, d1, d2, dyv, gv):
        r = lax.rsqrt(jnp.mean(xv * xv, axis=-1, keepdims=True) + EPS)
        n = xv * r
        dhv = d1 + d2
        dn = dhv * gv
        dx = r * (dn - n * jnp.mean(dn * n, axis=-1, keepdims=True))
        return dyv + dx, jnp.sum(dhv * n, axis=0, keepdims=True)

    grad_x, g_pre_grad = _rows("norm_pre_bwd", pre_bwd, [x, dh_main, dh_fb, dy], [g_pre],
                               [(D_MODEL, F32)], reds=[D_MODEL], tm=256)

    gw_in = jnp.concatenate([gw_main[:, :FB_ORIG], gw_fb[:, :B_HEADS], gw_main[:, FB_ORIG:]], axis=1)
    grads = dict(norm_pre_g=g_pre_grad, norm_post_g=g_post_grad, norm_mem_g=g_mem_grad, w_in=gw_in,
                 b_forget=g_bforget.reshape(1, B_HEADS), b_merge=g_bmerge, w_mem_kv=gw_mem_kv,
                 w_branch_a=gw_branch[0], w_branch_b=gw_branch[1], w_branch_m=gw_branch[2], w_out=gw_out)
    return loss_lanes, grad_x, grads


HBM_SPEC = pl.BlockSpec(memory_space=pltpu.HBM)


def _place():
    x, y, c = lax.axis_index("x"), lax.axis_index("y"), lax.axis_index("c")
    chips = [(1 - x, y), (x, 1 - y), (1 - x, 1 - y)]
    return x, y, c, 2 * x + y, chips


N_CHUNKS = 4


def _units(parts, row_axis):
    units = []
    for i, a in enumerate(parts):
        ch = a.shape[row_axis] // N_CHUNKS
        units += [(i, pl.ds(k * ch, ch)) for k in range(N_CHUNKS)]
    return units


def _gather_weights(parts):
    n = len(parts)
    units = _units(parts, 1)
    nu = len(units)

    def body(*refs):
        srcs, outs = refs[:n], refs[n:2 * n]
        send_sems, recv_sems, local_sems = refs[2 * n:]
        x, y, c, p, chips = _place()
        me, sib = (x, y, c), (x, y, 1 - c)

        def cp(u, k, chip, half, to, from_src=False):
            i, rs = units[u]
            dst = outs[i].at[chip, half, rs]
            return pltpu.make_async_remote_copy(
                src_ref=srcs[i].at[half, rs] if from_src else dst, dst_ref=dst, send_sem=send_sems.at[u, k],
                recv_sem=recv_sems.at[u, k], device_id=to, device_id_type=MESH)

        local = [pltpu.make_async_copy(srcs[i].at[hf, rs], outs[i].at[p, hf, rs], local_sems.at[hf, u])
                 for hf in range(2) for u, (i, rs) in enumerate(units)]
        for cpy in local:
            cpy.start()
        first = [cp(u, j, p, c, (cx, cy, c), from_src=True)
                 for u in range(nu) for j, (cx, cy) in enumerate(chips)]
        for f in first:
            f.start()
        passed = []
        for u in range(nu):
            for j, (cx, cy) in enumerate(chips):
                cp(u, j, 2 * cx + cy, c, me).wait_recv()
                fw = cp(u, 3 + j, 2 * cx + cy, c, sib)
                fw.start()
                passed.append(fw)
        for u in range(nu):
            for j, (cx, cy) in enumerate(chips):
                cp(u, 3 + j, 2 * cx + cy, 1 - c, me).wait_recv()
        for f in first + passed:
            f.wait_send()
        for cpy in local:
            cpy.wait()

    return pl.pallas_call(
        body, name="gather_weights", in_specs=[HBM_SPEC] * n, out_specs=[HBM_SPEC] * n,
        out_shape=[jax.ShapeDtypeStruct((N_CHIPS,) + a.shape, a.dtype) for a in parts],
        scratch_shapes=[pltpu.SemaphoreType.DMA((nu, 6)), pltpu.SemaphoreType.DMA((nu, 6)),
                        pltpu.SemaphoreType.DMA((2, nu))],
    )(*parts)


def _swap_with_sibling(parts):
    n = len(parts)
    units = _units(parts, 1)

    def body(*refs):
        srcs, outs = refs[:n], refs[n:2 * n]
        send_sems, recv_sems = refs[2 * n:]
        x, y, c, _, _ = _place()
        cps = [pltpu.make_async_remote_copy(
            src_ref=srcs[i].at[q, rs], dst_ref=outs[i].at[q, rs], send_sem=send_sems.at[u, q],
            recv_sem=recv_sems.at[u, q], device_id=(x, y, 1 - c), device_id_type=MESH)
            for q in range(N_CHIPS) for u, (i, rs) in enumerate(units)]
        for cpy in cps:
            cpy.start()
        for cpy in cps:
            cpy.wait()

    return pl.pallas_call(
        body, name="swap_with_sibling", in_specs=[HBM_SPEC] * n, out_specs=[HBM_SPEC] * n,
        out_shape=[jax.ShapeDtypeStruct(a.shape, a.dtype) for a in parts],
        scratch_shapes=[pltpu.SemaphoreType.DMA((len(units), N_CHIPS)),
                        pltpu.SemaphoreType.DMA((len(units), N_CHIPS))],
    )(*parts)


def _scatter_to_owners(parts):
    n = len(parts)
    units = _units(parts, 1)

    def body(*refs):
        srcs, outs = refs[:n], refs[n:2 * n]
        send_sems, recv_sems, local_sems = refs[2 * n:]
        x, y, c, p, chips = _place()
        local = [pltpu.make_async_copy(srcs[i].at[p, rs], outs[i].at[p, rs], local_sems.at[u])
                 for u, (i, rs) in enumerate(units)]
        for cpy in local:
            cpy.start()
        sends = []
        for u, (i, rs) in enumerate(units):
            for j, (cx, cy) in enumerate(chips):
                cpy = pltpu.make_async_remote_copy(
                    src_ref=srcs[i].at[2 * cx + cy, rs], dst_ref=outs[i].at[p, rs], send_sem=send_sems.at[u, j],
                    recv_sem=recv_sems.at[u, j], device_id=(cx, cy, c), device_id_type=MESH)
                cpy.start()
                sends.append(cpy)
        for u, (i, rs) in enumerate(units):
            for j, (cx, cy) in enumerate(chips):
                pltpu.make_async_remote_copy(
                    src_ref=srcs[i].at[2 * cx + cy, rs], dst_ref=outs[i].at[2 * cx + cy, rs],
                    send_sem=send_sems.at[u, j], recv_sem=recv_sems.at[u, j],
                    device_id=(cx, cy, c), device_id_type=MESH).wait_recv()
        for cpy in sends:
            cpy.wait_send()
        for cpy in local:
            cpy.wait()

    return pl.pallas_call(
        body, name="scatter_to_owners", in_specs=[HBM_SPEC] * n, out_specs=[HBM_SPEC] * n,
        out_shape=[jax.ShapeDtypeStruct(a.shape, a.dtype) for a in parts],
        scratch_shapes=[pltpu.SemaphoreType.DMA((len(units), 3)), pltpu.SemaphoreType.DMA((len(units), 3)),
                        pltpu.SemaphoreType.DMA((len(units),))],
    )(*parts)


def _share_with_sibling(parts):
    n = len(parts)
    units = _units(parts, 1)

    def body(*refs):
        srcs, outs = refs[:n], refs[n:2 * n]
        send_sems, recv_sems, local_sems = refs[2 * n:]
        x, y, c, _, _ = _place()
        local = [pltpu.make_async_copy(srcs[i].at[0, rs], outs[i].at[c, rs], local_sems.at[u])
                 for u, (i, rs) in enumerate(units)]
        for cpy in local:
            cpy.start()
        sends = [pltpu.make_async_remote_copy(
            src_ref=srcs[i].at[0, rs], dst_ref=outs[i].at[c, rs], send_sem=send_sems.at[u],
            recv_sem=recv_sems.at[u], device_id=(x, y, 1 - c), device_id_type=MESH)
            for u, (i, rs) in enumerate(units)]
        for cpy in sends:
            cpy.start()
        for u, (i, rs) in enumerate(units):
            pltpu.make_async_remote_copy(
                src_ref=srcs[i].at[0, rs], dst_ref=outs[i].at[1 - c, rs], send_sem=send_sems.at[u],
                recv_sem=recv_sems.at[u], device_id=(x, y, 1 - c), device_id_type=MESH).wait_recv()
        for cpy in sends:
            cpy.wait_send()
        for cpy in local:
            cpy.wait()

    return pl.pallas_call(
        body, name="share_with_sibling", in_specs=[HBM_SPEC] * n, out_specs=[HBM_SPEC] * n,
        out_shape=[jax.ShapeDtypeStruct((2,) + a.shape[1:], a.dtype) for a in parts],
        scratch_shapes=[pltpu.SemaphoreType.DMA((len(units),)), pltpu.SemaphoreType.DMA((len(units),)),
                        pltpu.SemaphoreType.DMA((len(units),))],
    )(*parts)


def _sum_small(v):
    def body(v_ref, out_ref, buf, send_sems, recv_sems):
        x, y, c, _, _ = _place()
        me = 4 * x + 2 * y + c
        buf[me] = v_ref[...]
        flips = [(dx, dy, dc) for dx in (0, 1) for dy in (0, 1) for dc in (0, 1)][1:]
        sends = []
        for k, (dx, dy, dc) in enumerate(flips):
            cpy = pltpu.make_async_remote_copy(
                src_ref=v_ref, dst_ref=buf.at[me], send_sem=send_sems.at[k], recv_sem=recv_sems.at[k],
                device_id=((x + dx) % 2, (y + dy) % 2, (c + dc) % 2), device_id_type=MESH)
            cpy.start()
            sends.append(cpy)
        for k, (dx, dy, dc) in enumerate(flips):
            px, py, pc = (x + dx) % 2, (y + dy) % 2, (c + dc) % 2
            pltpu.make_async_remote_copy(
                src_ref=v_ref, dst_ref=buf.at[4 * px + 2 * py + pc], send_sem=send_sems.at[k],
                recv_sem=recv_sems.at[k], device_id=(px, py, pc), device_id_type=MESH).wait_recv()
        for cpy in sends:
            cpy.wait_send()
        tot = buf[0]
        for i in range(1, N_DEV):
            tot = tot + buf[i]
        out_ref[...] = tot

    return pl.pallas_call(
        body, name="sum_small", out_shape=jax.ShapeDtypeStruct(v.shape, v.dtype),
        in_specs=[pl.BlockSpec(memory_space=pltpu.VMEM)], out_specs=pl.BlockSpec(memory_space=pltpu.VMEM),
        scratch_shapes=[pltpu.VMEM((N_DEV,) + v.shape, v.dtype), pltpu.SemaphoreType.DMA((N_DEV - 1,)),
                        pltpu.SemaphoreType.DMA((N_DEV - 1,))],
    )(v)


def _add_slabs(name, terms):
    arr0 = terms[0][0]
    n = arr0.shape[0] if terms[0][1] is None else 1
    _, r, w = arr0.shape
    tr = 64
    specs = []
    for _, slab in terms:
        if slab is None:
            specs.append(pl.BlockSpec((None, tr, w), lambda i, j: (i, j, 0)))
        else:
            specs.append(pl.BlockSpec((None, tr, w), functools.partial(lambda i, j, sl: (sl, j, 0), sl=slab)))

    def body(*refs):
        tot = refs[0][...]
        for rf in refs[1:-1]:
            tot = tot + rf[...]
        refs[-1][...] = tot

    return pl.pallas_call(
        body, name=name, grid=(n, r // tr), in_specs=specs,
        out_specs=pl.BlockSpec((None, tr, w), lambda i, j: (i, j, 0)),
        out_shape=jax.ShapeDtypeStruct((n, r, w), arr0.dtype),
        compiler_params=_params(("parallel", "parallel")),
    )(*[a for a, _ in terms])


def _adamw(name, w, g, m, v, tm):
    def fn(wv, gv, mv, vv):
        m2 = ADAM_B1 * mv + (1.0 - ADAM_B1) * gv
        v2 = ADAM_B2 * vv + (1.0 - ADAM_B2) * (gv * gv)
        m_hat = m2 / (1.0 - ADAM_B1 ** ADAM_STEP)
        v_hat = v2 / (1.0 - ADAM_B2 ** ADAM_STEP)
        return -ADAM_LR * (m_hat / (jnp.sqrt(v_hat) + ADAM_EPS) + ADAM_WD * wv), m2, v2
    c = w.shape[1]
    return _rows(name, fn, [w, g, m, v], [], [(c, F32)] * 3, tm=tm)


REST_ROWS = 256 + 3 * 128 + 256
REST_SPLITS = (("w_mem_kv", 0, 256), ("w_branch_a", 256, 128), ("w_branch_b", 384, 128),
               ("w_branch_m", 512, 128), ("w_out", 640, 256))


def _rest_pack(t):
    return jnp.concatenate([t[n].reshape(rows, D_MODEL) for n, _, rows in REST_SPLITS], axis=0)


def _rest_unpack(a, shapes):
    return {n: a[r0:r0 + rows].reshape(shapes[n]) for n, r0, rows in REST_SPLITS}


def _small_pack(pre, post, memg, bforget, bmerge):
    pad = jnp.zeros((1, D_MODEL - B_HEADS), F32)
    return jnp.concatenate([pre, post, memg, bmerge.reshape(3, D_MODEL),
                            jnp.concatenate([bforget, pad], axis=1), jnp.zeros((1, D_MODEL), F32)], axis=0)


def _small_unpack(s8):
    return dict(norm_pre_g=s8[0:1], norm_post_g=s8[1:2], norm_mem_g=s8[2:3],
                b_merge=s8[3:6].reshape(1, 3 * D_MODEL), b_forget=s8[6:7, :B_HEADS])


WEIGHTS = ("norm_pre_g", "norm_post_g", "norm_mem_g", "w_in", "b_forget", "b_merge", "w_mem_kv",
           "w_branch_a", "w_branch_b", "w_branch_m", "w_out")
SMALL = ("norm_pre_g", "norm_post_g", "norm_mem_g", "b_forget", "b_merge")


def kernel(x, mem, positions, norm_pre_g, norm_post_g, norm_mem_g, w_in, b_forget, b_merge, w_mem_kv, w_branch_a, w_branch_b, w_branch_m, w_out, loss_target, m_norm_pre_g, m_norm_post_g, m_norm_mem_g, m_w_in, m_b_forget, m_b_merge, m_w_mem_kv, m_w_branch_a, m_w_branch_b, m_w_branch_m, m_w_out, v_norm_pre_g, v_norm_post_g, v_norm_mem_g, v_w_in, v_b_forget, v_b_merge, v_w_mem_kv, v_w_branch_a, v_w_branch_b, v_w_branch_m, v_w_out):
    w = dict(norm_pre_g=norm_pre_g, norm_post_g=norm_post_g, norm_mem_g=norm_mem_g, w_in=w_in[0],
             b_forget=b_forget, b_merge=b_merge, w_mem_kv=w_mem_kv[0], w_branch_a=w_branch_a[0],
             w_branch_b=w_branch_b[0], w_branch_m=w_branch_m[0], w_out=w_out[0])
    mo = dict(norm_pre_g=m_norm_pre_g, norm_post_g=m_norm_post_g, norm_mem_g=m_norm_mem_g, w_in=m_w_in[0],
              b_forget=m_b_forget, b_merge=m_b_merge, w_mem_kv=m_w_mem_kv[0], w_branch_a=m_w_branch_a[0],
              w_branch_b=m_w_branch_b[0], w_branch_m=m_w_branch_m[0], w_out=m_w_out[0])
    vo = dict(norm_pre_g=v_norm_pre_g, norm_post_g=v_norm_post_g, norm_mem_g=v_norm_mem_g, w_in=v_w_in[0],
              b_forget=v_b_forget, b_merge=v_b_merge, w_mem_kv=v_w_mem_kv[0], w_branch_a=v_w_branch_a[0],
              w_branch_b=v_w_branch_b[0], w_branch_m=v_w_branch_m[0], w_out=v_w_out[0])
    s = x.shape[1]
    c = lax.axis_index("c")

    all_in, all_rest = _gather_weights([w["w_in"].astype(BF16).reshape(2, D_MODEL // 2, SHARD_COLS),
                                        _rest_pack(w).astype(BF16).reshape(2, REST_ROWS // 2, D_MODEL)])
    all_in = all_in.reshape(N_CHIPS, D_MODEL, SHARD_COLS)
    w_in_f = jnp.concatenate([all_in[p] for p in range(N_CHIPS)], axis=1)
    all_rest = all_rest.reshape(N_CHIPS, REST_ROWS, D_MODEL)
    w_kv_f = all_rest[:, 0:256].reshape(D_MODEL, D_MODEL)
    w_br_f = [all_rest[:, 256 + 128 * i:384 + 128 * i].reshape(N_CHIPS, 512, 256).transpose(1, 0, 2)
              .reshape(512, D_MODEL) for i in range(3)]
    w_out_f = all_rest[:, 640:896].reshape(D_MODEL, D_MODEL)
    w_main = jnp.concatenate([w_in_f[:, :FB_ORIG], w_in_f[:, FB_ORIG + B_HEADS:]], axis=1)
    w_fb = jnp.concatenate([w_in_f[:, FB_ORIG:FB_ORIG + B_HEADS], jnp.zeros((D_MODEL, HD - B_HEADS), BF16)], axis=1)

    loss_lanes, grad_x, g = _local_step(
        x[0], mem[0], positions.reshape(s, 1), loss_target[0], norm_pre_g, norm_post_g, norm_mem_g,
        w_main, w_fb, b_forget, b_merge, w_kv_f, w_br_f[0], w_br_f[1], w_br_f[2], w_out_f)
    loss = lax.psum(jnp.sum(loss_lanes), ("x", "y", "c"))

    def per_chip(name, p):
        a = g[name]
        if name in ("w_mem_kv", "w_out"):
            return a[256 * p:256 * (p + 1)]
        return a[:, 256 * p:256 * (p + 1)]

    in4 = jnp.stack([g["w_in"][:, SHARD_COLS * p:SHARD_COLS * (p + 1)] for p in range(N_CHIPS)])
    rest4 = jnp.stack([_rest_pack({n: per_chip(n, p) for n, _, _ in REST_SPLITS}) for p in range(N_CHIPS)])
    halves = [in4.reshape(N_CHIPS, 2, D_MODEL // 2, SHARD_COLS),
              rest4.reshape(N_CHIPS, 2, REST_ROWS // 2, D_MODEL)]
    mine = [lax.dynamic_index_in_dim(a, c, axis=1, keepdims=False) for a in halves]
    theirs = [lax.dynamic_index_in_dim(a, 1 - c, axis=1, keepdims=False) for a in halves]
    got = _swap_with_sibling(theirs)
    pair = [_add_slabs("add_pair_%d" % i, [(mine[i], None), (got[i], None)]) for i in range(2)]
    landed = _scatter_to_owners(pair)
    half = [_add_slabs("add_chips_%d" % i, [(landed[i], q) for q in range(N_CHIPS)]) for i in range(2)]
    red_in, red_rest = _share_with_sibling(half)
    gs = {"w_in": red_in.reshape(D_MODEL, SHARD_COLS)}
    gs.update(_rest_unpack(red_rest.reshape(REST_ROWS, D_MODEL), {n: w[n].shape for n, _, _ in REST_SPLITS}))
    gs.update(_small_unpack(_sum_small(_small_pack(
        g["norm_pre_g"], g["norm_post_g"], g["norm_mem_g"], g["b_forget"], g["b_merge"]))))

    delta, new_m, new_v = {}, {}, {}
    for n, tm in (("w_in", 128), ("w_mem_kv", 256), ("w_branch_a", 512), ("w_branch_b", 512),
                  ("w_branch_m", 512), ("w_out", 256)):
        d_, m_, v_ = _adamw("adamw_" + n, w[n], gs[n], mo[n], vo[n], tm)
        delta[n], new_m[n], new_v[n] = d_[None], m_[None], v_[None]
        gs[n] = gs[n][None]
    packs = [_small_pack(*[t[n] for n in ("norm_pre_g", "norm_post_g", "norm_mem_g", "b_forget", "b_merge")])
             for t in (w, gs, mo, vo)]
    for res, store in zip(_adamw("adamw_small", *packs, 8), (delta, new_m, new_v)):
        store.update(_small_unpack(res))

    return (loss, grad_x[None], *[gs[n] for n in WEIGHTS], *[delta[n] for n in WEIGHTS],
            *[new_m[n] for n in WEIGHTS], *[new_v[n] for n in WEIGHTS])
```

```python
import functools

import jax
import jax.numpy as jnp
from jax import lax
from jax.experimental import pallas as pl
from jax.experimental.pallas import tpu as pltpu

F32 = jnp.float32
BF16 = jnp.bfloat16
MESH = pl.DeviceIdType.MESH

D_MODEL = 1024
N_MEM = 256
EPS = 1e-6
NEG = -1e30
ROPE_THETA = 500000.0
ROT_DIM = 32
HD = 128
A_GROUP = 512
DILATIONS = (1, 4, 16)
BAND = 128
B_HEADS = 8
B_HD = 64
N_CHIPS = 4
N_DEV = 8

C_QA, C_KA, C_VA, C_ZA = 0, 1536, 3072, 4608
C_QB, C_KB, C_VB, C_ZB = 5120, 5632, 6144, 6656
C_QM, C_ZM, C_GL = 7168, 7680, 8192
N_MAIN = 11264
FB_ORIG = 6656
IN_COLS = 11272
SHARD_COLS = IN_COLS // N_CHIPS

ADAM_LR, ADAM_B1, ADAM_B2, ADAM_EPS, ADAM_WD, ADAM_STEP = 0.001, 0.9, 0.999, 1e-08, 0.01, 10

VMEM_LIMIT_V7X = 56 * 1024 * 1024

NT = (((1,), (1,)), ((), ()))
NN = (((1,), (0,)), ((), ()))
TN = (((0,), (0,)), ((), ()))


def _params(sem):
    return pltpu.CompilerParams(dimension_semantics=sem, vmem_limit_bytes=VMEM_LIMIT_V7X)


def _dot(a, b, dn=NN):
    return lax.dot_general(a, b, dn, preferred_element_type=F32)


def _sig(z):
    return 1.0 / (1.0 + jnp.exp(-z))


def _rows(name, fn, row_ins, bc_ins, outs, reds=(), tm=512):
    arrs, specs = [], []
    for r in row_ins:
        arr, w, cb = r if isinstance(r, tuple) else (r, r.shape[1], 0)
        arrs.append(arr)
        specs.append(pl.BlockSpec((tm, w), functools.partial(lambda i, cb: (i, cb), cb=cb)))
    for b in bc_ins:
        arrs.append(b)
        specs.append(pl.BlockSpec(b.shape, lambda i: (0, 0)))
    s = arrs[0].shape[0]
    tm = min(tm, s)
    n_in, n_out = len(arrs), len(outs)

    def body(*refs):
        vals = fn(*[r[...] for r in refs[:n_in]])
        if not isinstance(vals, (tuple, list)):
            vals = (vals,)
        for r, v in zip(refs[n_in:n_in + n_out], vals[:n_out]):
            r[...] = v.astype(r.dtype)
        if reds:
            red_refs = refs[n_in + n_out:]

            @pl.when(pl.program_id(0) == 0)
            def _():
                for r in red_refs:
                    r[...] = jnp.zeros_like(r)

            for r, v in zip(red_refs, vals[n_out:]):
                r[...] += v

    out_shape = [jax.ShapeDtypeStruct((s, c), dt) for c, dt in outs]
    out_shape += [jax.ShapeDtypeStruct((1, c), F32) for c in reds]
    out_specs = [pl.BlockSpec((tm, c), lambda i: (i, 0)) for c, _ in outs]
    out_specs += [pl.BlockSpec((1, c), lambda i: (0, 0)) for c in reds]
    res = pl.pallas_call(
        body, name=name, grid=(s // tm,), in_specs=specs, out_specs=out_specs, out_shape=out_shape,
        compiler_params=_params(("arbitrary",) if reds else ("parallel",)),
    )(*arrs)
    return res


def _mm(name, a, b, mode, out_dtype, tm=1024, tn=1024, tk=1024):
    if mode == "nn":
        (m, k), (_, n) = a.shape, b.shape
    elif mode == "nt":
        (m, k), (n, _) = a.shape, b.shape
    else:
        (k, m), (_, n) = a.shape, b.shape
    tm, tn, tk = min(tm, m), min(tn, n), min(tk, k)
    nk = k // tk
    dn = {"nn": NN, "nt": NT, "tn": TN}[mode]

    def body(a_ref, b_ref, o_ref, *acc):
        part = _dot(a_ref[...].astype(BF16), b_ref[...].astype(BF16), dn)
        if nk == 1:
            o_ref[...] = part.astype(o_ref.dtype)
        else:
            kk = pl.program_id(2)

            @pl.when(kk == 0)
            def _():
                acc[0][...] = part

            @pl.when(kk > 0)
            def _():
                acc[0][...] += part

            @pl.when(kk == nk - 1)
            def _():
                o_ref[...] = acc[0][...].astype(o_ref.dtype)

    a_spec = (pl.BlockSpec((tk, tm), lambda i, j, kk: (kk, i)) if mode == "tn"
              else pl.BlockSpec((tm, tk), lambda i, j, kk: (i, kk)))
    b_spec = (pl.BlockSpec((tn, tk), lambda i, j, kk: (j, kk)) if mode == "nt"
              else pl.BlockSpec((tk, tn), lambda i, j, kk: (kk, j)))
    return pl.pallas_call(
        body, name=name, grid=(m // tm, n // tn, nk), in_specs=[a_spec, b_spec],
        out_specs=pl.BlockSpec((tm, tn), lambda i, j, kk: (i, j)),
        out_shape=jax.ShapeDtypeStruct((m, n), out_dtype),
        scratch_shapes=[pltpu.VMEM((tm, tn), F32)] if nk > 1 else [],
        compiler_params=_params(("parallel", "parallel", "arbitrary")),
    )(a, b)


def _rms_fwd(name, x, g):
    def fn(xv, gv):
        r = lax.rsqrt(jnp.mean(xv * xv, axis=-1, keepdims=True) + EPS)
        return (xv * r * gv,)
    return _rows(name, fn, [x], [g], [(x.shape[1], BF16)], tm=min(512, x.shape[0]))[0]


def _rope_tables(pos, inv):
    ang = pos.astype(F32) * inv
    lane = lax.broadcasted_iota(jnp.int32, ang.shape, 1)
    c = jnp.where(lane < ROT_DIM, jnp.cos(ang), 1.0)
    sn = jnp.sin(ang)
    sg = jnp.where(lane < ROT_DIM // 2, -sn, jnp.where(lane < ROT_DIM, sn, 0.0))
    return c, sg, lane


def _rope_apply(x, c, sg, lane):
    outs = []
    for h in range(x.shape[1] // HD):
        xh = x[:, h * HD:(h + 1) * HD].astype(F32)
        swap = jnp.where(lane < ROT_DIM // 2, pltpu.roll(xh, HD - ROT_DIM // 2, 1),
                         pltpu.roll(xh, ROT_DIM // 2, 1))
        outs.append(xh * c + swap * sg)
    return jnp.concatenate(outs, axis=1)


def _rope_fwd(u, pos, inv):
    def fn(q, k, p, iv):
        c, sg, lane = _rope_tables(p, iv)
        return _rope_apply(q, c, sg, lane), _rope_apply(k, c, sg, lane)
    return _rows("rope_fwd", fn, [(u, 1536, 0), (u, 1536, 1), pos], [inv],
                 [(1536, BF16), (1536, BF16)], tm=256)


def _rope_bwd(dq, dk, pos, inv):
    def fn(q, k, p, iv):
        c, sg, lane = _rope_tables(p, iv)
        return _rope_apply(q, c, -sg, lane), _rope_apply(k, c, -sg, lane)
    return _rows("rope_bwd", fn, [dq, dk, pos], [inv], [(1536, BF16), (1536, BF16)], tm=256)


def _lane_pack(cols, like):
    lane = lax.broadcasted_iota(jnp.int32, like, 1)
    out = jnp.zeros(like, F32)
    for h, cvec in enumerate(cols):
        out = jnp.where(lane == h, cvec, out)
    return out


def _band_specs(l, d, tq):
    nsb = tq // BAND
    nblk = l // BAND
    cur = pl.BlockSpec((tq, A_GROUP), lambda r, i: (i, r))
    prev = pl.BlockSpec((BAND, A_GROUP), lambda r, i: (jnp.maximum(i * nsb - 1, 0), r))
    nxt = pl.BlockSpec((BAND, A_GROUP), lambda r, i: (jnp.minimum((i + 1) * nsb, nblk - 1), r))
    st_cur = pl.BlockSpec((tq, HD), lambda r, i: (i, r))
    st_nxt = pl.BlockSpec((BAND, HD), lambda r, i: (jnp.minimum((i + 1) * nsb, nblk - 1), r))
    return nsb, cur, prev, nxt, st_cur, st_nxt


def _band_mask_q(i, first_tile):
    qr = lax.broadcasted_iota(jnp.int32, (BAND, 2 * BAND), 0)
    kc = lax.broadcasted_iota(jnp.int32, (BAND, 2 * BAND), 1)
    in_prev = (kc < BAND) & (kc >= qr)
    in_cur = (kc >= BAND) & (kc - BAND <= qr)
    if i == 0:
        in_prev = in_prev & jnp.logical_not(first_tile)
    return in_prev | in_cur


def _band_mask_k(j, nsb, last_tile):
    qr = lax.broadcasted_iota(jnp.int32, (2 * BAND, BAND), 0)
    kc = lax.broadcasted_iota(jnp.int32, (2 * BAND, BAND), 1)
    same = (qr < BAND) & (kc <= qr)
    nxt = (qr >= BAND) & (kc >= qr - BAND)
    if j == nsb - 1:
        nxt = nxt & jnp.logical_not(last_tile)
    return same | nxt


def _band_fwd(name, q, k, v, d):
    l = q.shape[0]
    tq = min(512, l)
    nsb, cur, prev, _, st_cur, _ = _band_specs(l, d, tq)
    scale = HD ** -0.5

    def body(q_ref, kc_ref, kp_ref, vc_ref, vp_ref, o_ref, lse_ref):
        first = pl.program_id(1) == 0
        for i in range(nsb):
            lses = []
            mask = _band_mask_q(i, first)
            for h in range(4):
                cs = slice(h * HD, (h + 1) * HD)
                qv = q_ref[i * BAND:(i + 1) * BAND, cs]
                if i == 0:
                    kk = jnp.concatenate([kp_ref[:, cs], kc_ref[0:BAND, cs]], axis=0)
                    vv = jnp.concatenate([vp_ref[:, cs], vc_ref[0:BAND, cs]], axis=0)
                else:
                    kk = kc_ref[(i - 1) * BAND:(i + 1) * BAND, cs]
                    vv = vc_ref[(i - 1) * BAND:(i + 1) * BAND, cs]
                s = jnp.where(mask, _dot(qv, kk, NT) * scale, NEG)
                m = jnp.max(s, axis=-1, keepdims=True)
                p = jnp.exp(s - m)
                den = jnp.sum(p, axis=-1, keepdims=True)
                o_ref[i * BAND:(i + 1) * BAND, cs] = _dot(p.astype(BF16), vv) / den
                lses.append(m + jnp.log(den))
            lse_ref[i * BAND:(i + 1) * BAND, :] = _lane_pack(lses, (BAND, HD))

    return pl.pallas_call(
        body, name=name, grid=(d, l // tq), in_specs=[cur, cur, prev, cur, prev],
        out_specs=[cur, st_cur],
        out_shape=[jax.ShapeDtypeStruct((l, d * A_GROUP), F32), jax.ShapeDtypeStruct((l, d * HD), F32)],
        compiler_params=_params(("parallel", "parallel")),
    )(q, k, k, v, v)


def _band_dq(name, q, k, v, dy, lse, delta, d):
    l = q.shape[0]
    tq = min(512, l)
    nsb, cur, prev, _, st_cur, _ = _band_specs(l, d, tq)
    scale = HD ** -0.5

    def body(q_ref, kc_ref, kp_ref, vc_ref, vp_ref, dy_ref, lse_ref, dl_ref, dq_ref):
        first = pl.program_id(1) == 0
        for i in range(nsb):
            mask = _band_mask_q(i, first)
            rs = slice(i * BAND, (i + 1) * BAND)
            for h in range(4):
                cs = slice(h * HD, (h + 1) * HD)
                if i == 0:
                    kk = jnp.concatenate([kp_ref[:, cs], kc_ref[0:BAND, cs]], axis=0)
                    vv = jnp.concatenate([vp_ref[:, cs], vc_ref[0:BAND, cs]], axis=0)
                else:
                    kk = kc_ref[(i - 1) * BAND:(i + 1) * BAND, cs]
                    vv = vc_ref[(i - 1) * BAND:(i + 1) * BAND, cs]
                s = jnp.where(mask, _dot(q_ref[rs, cs], kk, NT) * scale, NEG)
                p = jnp.exp(s - lse_ref[rs, h:h + 1])
                dp = _dot(dy_ref[rs, cs], vv, NT)
                ds = p * (dp - dl_ref[rs, h:h + 1])
                dq_ref[rs, cs] = (_dot(ds.astype(BF16), kk) * scale).astype(dq_ref.dtype)

    return pl.pallas_call(
        body, name=name, grid=(d, l // tq),
        in_specs=[cur, cur, prev, cur, prev, cur, st_cur, st_cur], out_specs=cur,
        out_shape=jax.ShapeDtypeStruct((l, d * A_GROUP), BF16),
        compiler_params=_params(("parallel", "parallel")),
    )(q, k, k, v, v, dy, lse, delta)


def _band_dkv(name, q, k, v, dy, lse, delta, d):
    l = q.shape[0]
    tq = min(512, l)
    nsb, cur, _, nxt, st_cur, st_nxt = _band_specs(l, d, tq)
    scale = HD ** -0.5
    ntile = l // tq

    def body(k_ref, v_ref, qc_ref, qn_ref, dyc_ref, dyn_ref, lc_ref, ln_ref, dc_ref, dn_ref,
             dk_ref, dv_ref):
        last = pl.program_id(1) == ntile - 1

        def win(c_ref, n_ref, j, cs):
            if j == nsb - 1:
                return jnp.concatenate([c_ref[j * BAND:(j + 1) * BAND, cs], n_ref[:, cs]], axis=0)
            return c_ref[j * BAND:(j + 2) * BAND, cs]

        for j in range(nsb):
            mask = _band_mask_k(j, nsb, last)
            rs = slice(j * BAND, (j + 1) * BAND)
            for h in range(4):
                cs = slice(h * HD, (h + 1) * HD)
                hs = slice(h, h + 1)
                qw = win(qc_ref, qn_ref, j, cs)
                dyw = win(dyc_ref, dyn_ref, j, cs)
                s = jnp.where(mask, _dot(qw, k_ref[rs, cs], NT) * scale, NEG)
                p = jnp.exp(s - win(lc_ref, ln_ref, j, hs))
                dp = _dot(dyw, v_ref[rs, cs], NT)
                ds = p * (dp - win(dc_ref, dn_ref, j, hs))
                dv_ref[rs, cs] = _dot(p.astype(BF16), dyw, TN).astype(dv_ref.dtype)
                dk_ref[rs, cs] = (_dot(ds.astype(BF16), qw, TN) * scale).astype(dk_ref.dtype)

    shp = jax.ShapeDtypeStruct((l, d * A_GROUP), BF16)
    return pl.pallas_call(
        body, name=name, grid=(d, ntile),
        in_specs=[cur, cur, cur, nxt, cur, nxt, st_cur, st_nxt, st_cur, st_nxt],
        out_specs=[cur, cur], out_shape=[shp, shp],
        compiler_params=_params(("parallel", "parallel")),
    )(k, v, q, q, dy, dy, lse, lse, delta, delta)


def _split3(x):
    hi = x.astype(BF16)
    r1 = x - hi.astype(F32)
    mid = r1.astype(BF16)
    lo = (r1 - mid.astype(F32)).astype(BF16)
    return hi, mid, lo


def _fox_prep(z, b):
    h, s = z.shape
    blk = min(512, s)

    def body(z_ref, b_ref, c_ref):
        r = lax.broadcasted_iota(jnp.int32, (blk, blk), 0)
        cidx = lax.broadcasted_iota(jnp.int32, (blk, blk), 1)
        tri = (r <= cidx).astype(BF16)
        carry = jnp.zeros((h, 1), F32)
        for t in range(s // blk):
            zz = z_ref[:, t * blk:(t + 1) * blk] + b_ref[...]
            lf = jnp.minimum(zz, 0.0) - jnp.log(1.0 + jnp.exp(-jnp.abs(zz)))
            hi, mid, lo = _split3(lf)
            cs = _dot(hi, tri) + _dot(mid, tri) + _dot(lo, tri) + carry
            c_ref[:, t * blk:(t + 1) * blk] = cs
            carry = cs[:, blk - 1:blk]

    return pl.pallas_call(body, name="fox_prep", out_shape=jax.ShapeDtypeStruct((h, s), F32))(z, b)


def _fox_prep_bwd(dck, dcq, z, b):
    h, s = z.shape
    blk = min(512, s)

    def body(dck_ref, dcq_ref, z_ref, b_ref, dz_ref, db_ref):
        r = lax.broadcasted_iota(jnp.int32, (blk, blk), 0)
        cidx = lax.broadcasted_iota(jnp.int32, (blk, blk), 1)
        tri = (r >= cidx).astype(BF16)
        carry = jnp.zeros((h, 1), F32)
        tot = jnp.zeros((h, 1), F32)
        for t in reversed(range(s // blk)):
            hi, mid, lo = _split3(dcq_ref[:, t * blk:(t + 1) * blk] - dck_ref[:, t * blk:(t + 1) * blk])
            rc = _dot(hi, tri) + _dot(mid, tri) + _dot(lo, tri) + carry
            carry = rc[:, 0:1]
            zz = z_ref[:, t * blk:(t + 1) * blk] + b_ref[...]
            dz = rc * _sig(-zz)
            dz_ref[:, t * blk:(t + 1) * blk] = dz
            tot = tot + jnp.sum(dz, axis=-1, keepdims=True)
        db_ref[...] = tot

    return pl.pallas_call(
        body, name="fox_prep_bwd",
        out_shape=[jax.ShapeDtypeStruct((h, s), F32), jax.ShapeDtypeStruct((h, 1), F32)])(dck, dcq, z, b)


FOX_W = 128
FOX_C = B_HD
FOX_ONE = B_HD + 3
FOX_SUB = 256
FOX_SUB_FWD = 128
FOX_HEADS_PER_STEP = 2


def _fox_aug_k(k, c_col):
    def fn(kv, cv):
        lane = lax.broadcasted_iota(jnp.int32, kv.shape, 1)
        neg = cv * (-(B_HD ** -0.5))
        hi = neg.astype(BF16).astype(F32)
        mid = (neg - hi).astype(BF16).astype(F32)
        lo = neg - hi - mid
        aux = jnp.where(lane == FOX_C, hi, jnp.where(lane == FOX_C + 1, mid, jnp.where(lane == FOX_C + 2, lo, 0.0)))
        kb = jnp.where(lane < B_HD, kv.astype(F32) * (B_HD ** -0.5), aux)
        return kb, jnp.where(lane == FOX_ONE, 1.0, kb)

    return _rows("fox_aug_k", fn, [k, c_col], [], [(FOX_W, BF16)] * 2, tm=2048)


def _fox_fwd(qf, kb, vt4, t):
    h, s, w = qf.shape
    nt = s // t
    sub = FOX_SUB_FWD
    nsub = t // sub
    nh = FOX_HEADS_PER_STEP

    def body(q_ref, k_ref, v_ref, o_ref, lse_ref):
        i = pl.program_id(1)
        krow = lax.broadcasted_iota(jnp.int32, (sub, t), 0)
        qcol = lax.broadcasted_iota(jnp.int32, (sub, t), 1)

        def tile(j, carry, diag):
            out = []
            for hh in range(nh):
                m, acc = carry[hh]
                qv, vj = q_ref[hh], v_ref[hh, j]
                sts = [_dot(k_ref[hh, pl.ds(pl.multiple_of(j * t + b * sub, sub), sub), :], qv, NT)
                       for b in range(nsub)]
                for b in range(nsub):
                    st = sts[b]
                    if diag:
                        st = jnp.where(krow + b * sub <= qcol, st, NEG)
                    m2 = jnp.maximum(m, jnp.max(st, axis=0, keepdims=True))
                    p = jnp.exp(st - m2).astype(BF16)
                    acc = jnp.exp(m - m2) * acc + _dot(vj[:, b * sub:(b + 1) * sub], p)
                    m = m2
                out.append((m, acc))
            return tuple(out)

        init = tuple((jnp.full((1, t), NEG, F32), jnp.zeros((w, t), F32)) for _ in range(nh))
        carry = lax.fori_loop(0, i, lambda j, c: tile(j, c, False), init)
        for hh, (m, acc) in enumerate(tile(i, carry, True)):
            den = acc[B_HD:B_HD + 1, :]
            o_ref[hh] = (acc[0:B_HD, :] / den).astype(o_ref.dtype)
            lse_ref[hh] = m + jnp.log(den)

    return pl.pallas_call(
        body, name="fox_fwd", grid=(h // nh, nt),
        in_specs=[pl.BlockSpec((nh, t, w), lambda hh, i: (hh, i, 0)),
                  pl.BlockSpec((nh, s, w), lambda hh, i: (hh, 0, 0)),
                  pl.BlockSpec((nh, nt, w, t), lambda hh, i: (hh, 0, 0, 0))],
        out_specs=[pl.BlockSpec((nh, B_HD, t), lambda hh, i: (hh, 0, i)),
                   pl.BlockSpec((nh, 1, t), lambda hh, i: (hh, 0, i))],
        out_shape=[jax.ShapeDtypeStruct((h, B_HD, s), BF16), jax.ShapeDtypeStruct((h, 1, s), F32)],
        compiler_params=_params(("parallel", "parallel")),
    )(qf, kb, vt4)


def _fox_bwd(qf, dow, lse_row, delta_row, kb, kst4, vb, t):
    h, s, w = qf.shape
    nt = s // t
    nsub = t // FOX_SUB

    def body(q_ref, do_ref, lse_ref, dl_ref, k_ref, kt_ref, v_ref, dqt_ref, dk_ref, dv_ref, dk_acc, dv_acc):
        j = pl.program_id(1)

        @pl.when(j == 0)
        def _():
            dqt_ref[...] = jnp.zeros_like(dqt_ref)

        dk_acc[...] = jnp.zeros_like(dk_acc)
        dv_acc[...] = jnp.zeros_like(dv_acc)
        kt = kt_ref[...]
        krow = lax.broadcasted_iota(jnp.int32, (FOX_SUB, t), 0)
        qcol = lax.broadcasted_iota(jnp.int32, (FOX_SUB, t), 1)

        def tile(i, diag):
            i0 = pl.multiple_of(i * t, t)
            qi, doi = q_ref[pl.ds(i0, t), :], do_ref[pl.ds(i0, t), :]
            lse, dl = lse_ref[i], dl_ref[i]
            subs = [slice(b * FOX_SUB, (b + 1) * FOX_SUB) for b in range(nsub)]
            sts = [_dot(k_ref[rs, :], qi, NT) for rs in subs]
            dps = [_dot(v_ref[rs, :], doi, NT) for rs in subs]
            dq = None
            for b, rs in enumerate(subs):
                st = sts[b] - lse
                if diag:
                    st = jnp.where(krow + b * FOX_SUB <= qcol, st, NEG)
                pt = jnp.exp(st)
                dsb = (pt * (dps[b] - dl)).astype(BF16)
                dv_acc[rs, :] += _dot(pt.astype(BF16), doi)
                dk_acc[rs, :] += _dot(dsb, qi)
                part = _dot(kt[:, rs], dsb)
                dq = part if dq is None else dq + part
            dqt_ref[i] += dq

        def step(i, carry):
            tile(i, False)
            return carry

        tile(j, True)
        lax.fori_loop(j + 1, nt, step, 0)
        dk_ref[...] = dk_acc[...] * (B_HD ** -0.5)
        dv_ref[...] = dv_acc[...]

    full = pl.BlockSpec((None, s, w), lambda hh, j: (hh, 0, 0))
    rowst = pl.BlockSpec((None, nt, 1, t), lambda hh, j: (hh, 0, 0, 0))
    tl = pl.BlockSpec((None, t, w), lambda hh, j: (hh, j, 0))
    return pl.pallas_call(
        body, name="fox_bwd", grid=(h, nt),
        in_specs=[full, full, rowst, rowst, tl, pl.BlockSpec((None, None, w, t), lambda hh, j: (hh, j, 0, 0)), tl],
        out_specs=[pl.BlockSpec((None, nt, w, t), lambda hh, j: (hh, 0, 0, 0)), tl, tl],
        out_shape=[jax.ShapeDtypeStruct((h, nt, w, t), F32), jax.ShapeDtypeStruct((h, s, w), F32),
                   jax.ShapeDtypeStruct((h, s, w), F32)],
        scratch_shapes=[pltpu.VMEM((t, w), F32), pltpu.VMEM((t, w), F32)],
        compiler_params=_params(("parallel", "arbitrary")),
    )(qf, dow, lse_row, delta_row, kb, kst4, vb)


def _mem_fwd(u, mkv, tq=512):
    s = u.shape[0]
    scale = HD ** -0.5

    def body(q_ref, mk_ref, mv_ref, o_ref, lse_ref):
        lses = []
        for h in range(4):
            cs = slice(h * HD, (h + 1) * HD)
            sc = _dot(q_ref[:, cs], mk_ref[:, cs], NT) * scale
            m = jnp.max(sc, axis=-1, keepdims=True)
            p = jnp.exp(sc - m)
            den = jnp.sum(p, axis=-1, keepdims=True)
            o_ref[:, cs] = (_dot(p.astype(BF16), mv_ref[:, cs]) / den).astype(o_ref.dtype)
            lses.append(m + jnp.log(den))
        lse_ref[...] = _lane_pack(lses, (tq, HD))

    return pl.pallas_call(
        body, name="mem_fwd", grid=(s // tq,),
        in_specs=[pl.BlockSpec((tq, 512), lambda i: (i, C_QM // 512)),
                  pl.BlockSpec((N_MEM, 512), lambda i: (0, 0)),
                  pl.BlockSpec((N_MEM, 512), lambda i: (0, 1))],
        out_specs=[pl.BlockSpec((tq, 512), lambda i: (i, 0)), pl.BlockSpec((tq, HD), lambda i: (i, 0))],
        out_shape=[jax.ShapeDtypeStruct((s, 512), BF16), jax.ShapeDtypeStruct((s, HD), F32)],
        compiler_params=_params(("parallel",)),
    )(u, mkv, mkv)


def _mem_bwd(u, mkv, o, do, lse, tq=512):
    s = u.shape[0]
    scale = HD ** -0.5

    def body(q_ref, mk_ref, mv_ref, o_ref, do_ref, lse_ref, dq_ref, dmk_ref, dmv_ref):
        @pl.when(pl.program_id(0) == 0)
        def _():
            dmk_ref[...] = jnp.zeros_like(dmk_ref)
            dmv_ref[...] = jnp.zeros_like(dmv_ref)

        for h in range(4):
            cs = slice(h * HD, (h + 1) * HD)
            qv, dov = q_ref[:, cs], do_ref[:, cs]
            sc = _dot(qv, mk_ref[:, cs], NT) * scale
            p = jnp.exp(sc - lse_ref[:, h:h + 1])
            delta = jnp.sum(dov.astype(F32) * o_ref[:, cs].astype(F32), axis=-1, keepdims=True)
            ds = p * (_dot(dov, mv_ref[:, cs], NT) - delta)
            dsb = ds.astype(BF16)
            dq_ref[:, cs] = (_dot(dsb, mk_ref[:, cs]) * scale).astype(dq_ref.dtype)
            dmk_ref[:, cs] += _dot(dsb, qv, TN) * scale
            dmv_ref[:, cs] += _dot(p.astype(BF16), dov, TN)

    row = pl.BlockSpec((tq, 512), lambda i: (i, 0))
    acc = pl.BlockSpec((N_MEM, 512), lambda i: (0, 0))
    return pl.pallas_call(
        body, name="mem_bwd", grid=(s // tq,),
        in_specs=[pl.BlockSpec((tq, 512), lambda i: (i, C_QM // 512)),
                  pl.BlockSpec((N_MEM, 512), lambda i: (0, 0)),
                  pl.BlockSpec((N_MEM, 512), lambda i: (0, 1)),
                  row, row, pl.BlockSpec((tq, HD), lambda i: (i, 0))],
        out_specs=[row, acc, acc],
        out_shape=[jax.ShapeDtypeStruct((s, 512), BF16), jax.ShapeDtypeStruct((N_MEM, 512), F32),
                   jax.ShapeDtypeStruct((N_MEM, 512), F32)],
        compiler_params=_params(("arbitrary",)),
    )(u, mkv, mkv, o, do, lse)


def _heads_major(a, col0):
    s = a.shape[0]
    return a[:, col0:col0 + 512].reshape(s, B_HEADS, B_HD).transpose(1, 0, 2)


def _token_major(a):
    h, s, dh = a.shape
    return a.transpose(1, 0, 2).reshape(s, h * dh)


def _class_view(a, d):
    s, c = a.shape
    return a.reshape(s // d, d * c)


def _local_step(x, mem, pos, target, g_pre, g_post, g_mem, w_main, w_fb, b_forget, b_merge,
                w_mem_kv, w_ba, w_bb, w_bm, w_out):
    s = x.shape[0]
    t_fox = min(512, s)
    nt = s // t_fox
    half = ROT_DIM // 2
    inv = ROPE_THETA ** (-jnp.arange(half, dtype=F32) / half)
    inv128 = jnp.concatenate([inv, inv, jnp.zeros((HD - ROT_DIM,), F32)]).reshape(1, HD)

    h = _rms_fwd("norm_pre", x, g_pre)
    u = _mm("proj_in", h, w_main, "nn", BF16, tn=512)
    ufb = _mm("proj_fb", h, w_fb, "nn", F32)
    memn = _rms_fwd("norm_mem", mem, g_mem)
    mkv = _mm("proj_mem", memn, w_mem_kv, "nn", BF16)

    q_rot, k_rot = _rope_fwd(u, pos, inv128)
    os_, lses = [], []
    views = []
    for g, d in enumerate(DILATIONS):
        qv = _class_view(q_rot[:, g * A_GROUP:(g + 1) * A_GROUP], d)
        kv = _class_view(k_rot[:, g * A_GROUP:(g + 1) * A_GROUP], d)
        vv = _class_view(u[:, C_VA + g * A_GROUP:C_VA + (g + 1) * A_GROUP], d)
        views.append((qv, kv, vv))
        o_g, lse_g = _band_fwd("band_fwd%d" % g, qv, kv, vv, d)
        os_.append(o_g.reshape(s, A_GROUP))
        lses.append(lse_g.reshape(s, HD))

    def merge_a(o1, o2, o3, l1, l2, l3, za):
        ys, tots = [], []
        for hh in range(4):
            cs, hs = slice(hh * HD, (hh + 1) * HD), slice(hh, hh + 1)
            mx = jnp.maximum(jnp.maximum(l1[:, hs], l2[:, hs]), l3[:, hs])
            e1, e2, e3 = jnp.exp(l1[:, hs] - mx), jnp.exp(l2[:, hs] - mx), jnp.exp(l3[:, hs] - mx)
            den = e1 + e2 + e3
            ys.append((e1 * o1[:, cs] + e2 * o2[:, cs] + e3 * o3[:, cs]) / den)
            tots.append(mx + jnp.log(den))
        y = jnp.concatenate(ys, axis=1)
        zf = za.astype(F32)
        return y, y * (zf * _sig(zf)), _lane_pack(tots, l1.shape)

    y_a, yg_a, lse_a = _rows("merge_a", merge_a, os_ + lses + [(u, 512, C_ZA // 512)], [],
                             [(512, BF16), (512, BF16), (HD, F32)])

    zrow = ufb[:, :B_HEADS].T
    c = _fox_prep(zrow, b_forget.reshape(B_HEADS, 1))
    n_hs = B_HEADS * s

    def wide(a, fill):
        return jnp.pad(a, ((0, 0), (0, 0), (0, FOX_W - B_HD)), constant_values=fill)

    def tiles_t(a):
        return a.reshape(B_HEADS, nt, t_fox, FOX_W).transpose(0, 1, 3, 2)

    qf = wide(_heads_major(u, C_QB), B_HD ** 0.5)
    vb = wide(_heads_major(u, C_VB), 1.0)
    kb, ks = _fox_aug_k(wide(_heads_major(u, C_KB), 0.0).reshape(n_hs, FOX_W), c.reshape(n_hs, 1))
    kb, ks = kb.reshape(B_HEADS, s, FOX_W), ks.reshape(B_HEADS, s, FOX_W)
    ot, lse_b = _fox_fwd(qf, kb, tiles_t(vb), t_fox)
    y_b = ot.transpose(2, 0, 1).reshape(s, B_HEADS * B_HD)

    y_m, lse_m = _mem_fwd(u, mkv)

    def gate(y, z):
        zf = z.astype(F32)
        return (y.astype(F32) * (zf * _sig(zf)),)

    yg_b = _rows("gate_b", gate, [y_b, (u, 512, C_ZB // 512)], [], [(512, BF16)])[0]
    yg_m = _rows("gate_m", gate, [y_m, (u, 512, C_ZM // 512)], [], [(512, BF16)])[0]

    br_a = _mm("branch_a", yg_a, w_ba, "nn", BF16)
    br_b = _mm("branch_b", yg_b, w_bb, "nn", BF16)
    br_m = _mm("branch_m", yg_m, w_bm, "nn", BF16)
    gl = [(u, 1024, C_GL // 1024 + i) for i in range(3)]
    bm3 = b_merge.reshape(3, D_MODEL)

    def merge(g0, g1, g2, b0, b1, b2, bm):
        tot = 0.0
        for i, (gv, bv) in enumerate(((g0, b0), (g1, b1), (g2, b2))):
            tot = tot + _sig(gv.astype(F32) + bm[i:i + 1, :]) * bv.astype(F32)
        return (tot,)

    merged = _rows("merge_gates", merge, gl + [br_a, br_b, br_m], [bm3], [(D_MODEL, BF16)])[0]
    out = _mm("proj_out", merged, w_out, "nn", F32)

    def tail(xv, ov, tv, gv):
        r = lax.rsqrt(jnp.mean(ov * ov, axis=-1, keepdims=True) + EPS)
        n = ov * r
        err = xv + n * gv - tv
        dy = err * (1.0 / D_MODEL)
        dn = dy * gv
        dout = r * (dn - n * jnp.mean(dn * n, axis=-1, keepdims=True))
        return (dy, dout, jnp.sum(0.5 * err * err * (1.0 / D_MODEL), axis=0, keepdims=True),
                jnp.sum(dy * n, axis=0, keepdims=True))

    dy, dout, loss_lanes, g_post_grad = _rows(
        "tail", tail, [x, out, target], [g_post], [(D_MODEL, F32), (D_MODEL, BF16)],
        reds=[D_MODEL, D_MODEL], tm=256)

    dmerged = _mm("d_merged", dout, w_out, "nt", BF16)
    gw_out = _mm("g_w_out", merged, dout, "tn", F32)

    def merge_bwd(dm, g0, g1, g2, b0, b1, b2, bm):
        dmf = dm.astype(F32)
        dbs, dgs, sums = [], [], []
        for i, (gv, bv) in enumerate(((g0, b0), (g1, b1), (g2, b2))):
            sg = _sig(gv.astype(F32) + bm[i:i + 1, :])
            dbs.append(dmf * sg)
            dg = dmf * bv.astype(F32) * sg * (1.0 - sg)
            dgs.append(dg)
            sums.append(jnp.sum(dg, axis=0, keepdims=True))
        return tuple(dbs + dgs + sums)

    res = _rows("merge_bwd", merge_bwd, [dmerged] + gl + [br_a, br_b, br_m], [bm3],
                [(D_MODEL, BF16)] * 6, reds=[D_MODEL] * 3, tm=256)
    dbr, dgl, g_bmerge = res[0:3], res[3:6], jnp.concatenate(res[6:9], axis=1)

    dyg, gw_branch = [], []
    for nm, dbv, wv, ygv in (("a", dbr[0], w_ba, yg_a), ("b", dbr[1], w_bb, yg_b), ("m", dbr[2], w_bm, yg_m)):
        dyg.append(_mm("d_yg_" + nm, dbv, wv, "nt", BF16))
        gw_branch.append(_mm("g_w_branch_" + nm, ygv, dbv, "tn", F32))

    def gate_bwd(dg, y, z):
        dgf, yf, zf = dg.astype(F32), y.astype(F32), z.astype(F32)
        sg = _sig(zf)
        return dgf * (zf * sg), dgf * yf * (sg * (1.0 + zf * (1.0 - sg)))

    def gate_bwd_a(dg, y, z):
        dyv, dz = gate_bwd(dg, y, z)
        prod = dyv * y.astype(F32)
        dl = [jnp.sum(prod[:, hh * HD:(hh + 1) * HD], axis=-1, keepdims=True) for hh in range(4)]
        return dyv, dz, _lane_pack(dl, (dg.shape[0], HD))

    dy_a, dz_a, delta_a = _rows("gate_bwd_a", gate_bwd_a, [dyg[0], y_a, (u, 512, C_ZA // 512)], [],
                                [(512, BF16), (512, BF16), (HD, F32)])
    dy_b, dz_b = _rows("gate_bwd_b", gate_bwd, [dyg[1], y_b, (u, 512, C_ZB // 512)], [],
                       [(512, BF16), (512, BF16)])
    dy_m, dz_m = _rows("gate_bwd_m", gate_bwd, [dyg[2], y_m, (u, 512, C_ZM // 512)], [],
                       [(512, BF16), (512, BF16)])

    dq_m, dmk, dmv = _mem_bwd(u, mkv, y_m, dy_m, lse_m)
    dmkv = jnp.concatenate([dmk, dmv], axis=1)
    gw_mem_kv = _mm("g_w_mem_kv", memn, dmkv, "tn", F32)
    dmemn = _mm("d_memn", dmkv, w_mem_kv, "nt", F32)

    def mem_gain_grad(mv, dv):
        r = lax.rsqrt(jnp.mean(mv * mv, axis=-1, keepdims=True) + EPS)
        return (jnp.sum(dv * mv * r, axis=0, keepdims=True),)

    g_mem_grad = _rows("g_norm_mem", mem_gain_grad, [mem, dmemn], [], [], reds=[D_MODEL], tm=N_MEM)[0]

    dob = _heads_major(dy_b, 0)

    def fox_delta(a, b):
        return (jnp.sum(a.astype(F32) * b.astype(F32), axis=-1, keepdims=True),)

    ob = ot.transpose(0, 2, 1).reshape(n_hs, B_HD)
    delta_b = _rows("fox_delta", fox_delta, [dob.reshape(n_hs, B_HD), ob], [], [(1, F32)], tm=min(2048, s))[0]
    dqt, dkw, dvw = _fox_bwd(qf, wide(dob, 0.0), lse_b.reshape(B_HEADS, nt, 1, t_fox),
                             delta_b.reshape(B_HEADS, nt, 1, t_fox), kb, tiles_t(ks), vb, t_fox)
    dqb = dqt[:, :, :B_HD, :].transpose(0, 1, 3, 2).reshape(B_HEADS, s, B_HD)
    dkb, dvb = dkw[:, :, :B_HD].astype(BF16), dvw[:, :, :B_HD].astype(BF16)
    dzrow, g_bforget = _fox_prep_bwd(dkw[:, :, B_HD], dqt[:, :, FOX_ONE, :].reshape(B_HEADS, s), zrow,
                                     b_forget.reshape(B_HEADS, 1))
    dfb = jnp.zeros((s, HD), BF16).at[:, :B_HEADS].set(dzrow.T.astype(BF16))

    dqs, dks, dvs = [], [], []
    for g, d in enumerate(DILATIONS):
        qv, kv, vv = views[g]
        dyv, lv, dlv = _class_view(dy_a, d), _class_view(lse_a, d), _class_view(delta_a, d)
        dqs.append(_band_dq("band_dq%d" % g, qv, kv, vv, dyv, lv, dlv, d).reshape(s, A_GROUP))
        dk_g, dv_g = _band_dkv("band_dkv%d" % g, qv, kv, vv, dyv, lv, dlv, d)
        dks.append(dk_g.reshape(s, A_GROUP))
        dvs.append(dv_g.reshape(s, A_GROUP))
    dqa, dka = _rope_bwd(jnp.concatenate(dqs, axis=1), jnp.concatenate(dks, axis=1), pos, inv128)

    du = jnp.concatenate(
        [dqa, dka] + dvs + [dz_a, _token_major(dqb).astype(BF16), _token_major(dkb), _token_major(dvb),
                            dz_b, dq_m, dz_m] + list(dgl), axis=1)

    gw_main = _mm("g_w_main", h, du, "tn", F32)
    gw_fb = _mm("g_w_fb", h, dfb, "tn", F32)
    dh_main = _mm("d_h", du, w_main, "nt", F32, tk=1024)
    dh_fb = _mm("d_h_fb", dfb, w_fb, "nt", F32)

    def pre_bwd(xv, d1, d2, dyv, gv):
        r = lax.rsqrt(jnp.mean(xv * xv, axis=-1, keepdims=True) + EPS)
        n = xv * r
        dhv = d1 + d2
        dn = dhv * gv
        dx = r * (dn - n * jnp.mean(dn * n, axis=-1, keepdims=True))
        return dyv + dx, jnp.sum(dhv * n, axis=0, keepdims=True)

    grad_x, g_pre_grad = _rows("norm_pre_bwd", pre_bwd, [x, dh_main, dh_fb, dy], [g_pre],
                               [(D_MODEL, F32)], reds=[D_MODEL], tm=256)

    gw_in = jnp.concatenate([gw_main[:, :FB_ORIG], gw_fb[:, :B_HEADS], gw_main[:, FB_ORIG:]], axis=1)
    grads = dict(norm_pre_g=g_pre_grad, norm_post_g=g_post_grad, norm_mem_g=g_mem_grad, w_in=gw_in,
                 b_forget=g_bforget.reshape(1, B_HEADS), b_merge=g_bmerge, w_mem_kv=gw_mem_kv,
                 w_branch_a=gw_branch[0], w_branch_b=gw_branch[1], w_branch_m=gw_branch[2], w_out=gw_out)
    return loss_lanes, grad_x, grads


HBM_SPEC = pl.BlockSpec(memory_space=pltpu.HBM)


def _place():
    x, y, c = lax.axis_index("x"), lax.axis_index("y"), lax.axis_index("c")
    chips = [(1 - x, y), (x, 1 - y), (1 - x, 1 - y)]
    return x, y, c, 2 * x + y, chips


N_CHUNKS = 4


def _units(parts, row_axis):
    units = []
    for i, a in enumerate(parts):
        ch = a.shape[row_axis] // N_CHUNKS
        units += [(i, pl.ds(k * ch, ch)) for k in range(N_CHUNKS)]
    return units


def _gather_weights(parts):
    n = len(parts)
    units = _units(parts, 1)
    nu = len(units)

    def body(*refs):
        srcs, outs = refs[:n], refs[n:2 * n]
        send_sems, recv_sems = refs[2 * n:]
        x, y, c, p, chips = _place()
        me, sib = (x, y, c), (x, y, 1 - c)

        def cp(u, k, chip, half, to, from_src=False):
            i, rs = units[u]
            dst = outs[i].at[chip, half, rs]
            return pltpu.make_async_remote_copy(
                src_ref=srcs[i].at[half, rs] if from_src else dst, dst_ref=dst, send_sem=send_sems.at[u, k],
                recv_sem=recv_sems.at[u, k], device_id=to, device_id_type=MESH)

        first = [cp(u, j, p, c, (cx, cy, c), from_src=True)
                 for u in range(nu) for j, (cx, cy) in enumerate(chips)]
        for f in first:
            f.start()
        passed = []
        for u in range(nu):
            for j, (cx, cy) in enumerate(chips):
                cp(u, j, 2 * cx + cy, c, me).wait_recv()
                fw = cp(u, 3 + j, 2 * cx + cy, c, sib)
                fw.start()
                passed.append(fw)
        for u in range(nu):
            for j, (cx, cy) in enumerate(chips):
                cp(u, 3 + j, 2 * cx + cy, 1 - c, me).wait_recv()
        for f in first + passed:
            f.wait_send()

    return pl.pallas_call(
        body, name="gather_weights", in_specs=[HBM_SPEC] * n, out_specs=[HBM_SPEC] * n,
        out_shape=[jax.ShapeDtypeStruct((N_CHIPS,) + a.shape, a.dtype) for a in parts],
        scratch_shapes=[pltpu.SemaphoreType.DMA((nu, 6)), pltpu.SemaphoreType.DMA((nu, 6))],
    )(*parts)


def _swap_with_sibling(parts):
    n = len(parts)
    units = _units(parts, 1)

    def body(*refs):
        srcs, outs = refs[:n], refs[n:2 * n]
        send_sems, recv_sems = refs[2 * n:]
        x, y, c, _, _ = _place()
        cps = [pltpu.make_async_remote_copy(
            src_ref=srcs[i].at[q, rs], dst_ref=outs[i].at[q, rs], send_sem=send_sems.at[u, q],
            recv_sem=recv_sems.at[u, q], device_id=(x, y, 1 - c), device_id_type=MESH)
            for q in range(N_CHIPS) for u, (i, rs) in enumerate(units)]
        for cpy in cps:
            cpy.start()
        for cpy in cps:
            cpy.wait()

    return pl.pallas_call(
        body, name="swap_with_sibling", in_specs=[HBM_SPEC] * n, out_specs=[HBM_SPEC] * n,
        out_shape=[jax.ShapeDtypeStruct(a.shape, a.dtype) for a in parts],
        scratch_shapes=[pltpu.SemaphoreType.DMA((len(units), N_CHIPS)),
                        pltpu.SemaphoreType.DMA((len(units), N_CHIPS))],
    )(*parts)


def _scatter_to_owners(parts):
    n = len(parts)
    units = _units(parts, 1)

    def body(*refs):
        srcs, outs = refs[:n], refs[n:2 * n]
        send_sems, recv_sems = refs[2 * n:]
        x, y, c, p, chips = _place()
        sends = []
        for u, (i, rs) in enumerate(units):
            for j, (cx, cy) in enumerate(chips):
                cpy = pltpu.make_async_remote_copy(
                    src_ref=srcs[i].at[2 * cx + cy, rs], dst_ref=outs[i].at[p, rs], send_sem=send_sems.at[u, j],
                    recv_sem=recv_sems.at[u, j], device_id=(cx, cy, c), device_id_type=MESH)
                cpy.start()
                sends.append(cpy)
        for u, (i, rs) in enumerate(units):
            for j, (cx, cy) in enumerate(chips):
                pltpu.make_async_remote_copy(
                    src_ref=srcs[i].at[2 * cx + cy, rs], dst_ref=outs[i].at[2 * cx + cy, rs],
                    send_sem=send_sems.at[u, j], recv_sem=recv_sems.at[u, j],
                    device_id=(cx, cy, c), device_id_type=MESH).wait_recv()
        for cpy in sends:
            cpy.wait_send()

    return pl.pallas_call(
        body, name="scatter_to_owners", in_specs=[HBM_SPEC] * n, out_specs=[HBM_SPEC] * n,
        out_shape=[jax.ShapeDtypeStruct(a.shape, a.dtype) for a in parts],
        scratch_shapes=[pltpu.SemaphoreType.DMA((len(units), 3)), pltpu.SemaphoreType.DMA((len(units), 3))],
    )(*parts)


def _share_with_sibling(parts):
    n = len(parts)
    units = _units(parts, 1)

    def body(*refs):
        srcs, outs = refs[:n], refs[n:2 * n]
        send_sems, recv_sems = refs[2 * n:]
        x, y, c, _, _ = _place()
        sends = [pltpu.make_async_remote_copy(
            src_ref=srcs[i].at[0, rs], dst_ref=outs[i].at[c, rs], send_sem=send_sems.at[u],
            recv_sem=recv_sems.at[u], device_id=(x, y, 1 - c), device_id_type=MESH)
            for u, (i, rs) in enumerate(units)]
        for cpy in sends:
            cpy.start()
        for u, (i, rs) in enumerate(units):
            pltpu.make_async_remote_copy(
                src_ref=srcs[i].at[0, rs], dst_ref=outs[i].at[1 - c, rs], send_sem=send_sems.at[u],
                recv_sem=recv_sems.at[u], device_id=(x, y, 1 - c), device_id_type=MESH).wait_recv()
        for cpy in sends:
            cpy.wait_send()

    return pl.pallas_call(
        body, name="share_with_sibling", in_specs=[HBM_SPEC] * n, out_specs=[HBM_SPEC] * n,
        out_shape=[jax.ShapeDtypeStruct((2,) + a.shape[1:], a.dtype) for a in parts],
        scratch_shapes=[pltpu.SemaphoreType.DMA((len(units),)), pltpu.SemaphoreType.DMA((len(units),))],
    )(*parts)


def _sum_small(v):
    def body(v_ref, out_ref, buf, send_sems, recv_sems):
        x, y, c, _, _ = _place()
        me = 4 * x + 2 * y + c
        buf[me] = v_ref[...]
        flips = [(dx, dy, dc) for dx in (0, 1) for dy in (0, 1) for dc in (0, 1)][1:]
        sends = []
        for k, (dx, dy, dc) in enumerate(flips):
            cpy = pltpu.make_async_remote_copy(
                src_ref=v_ref, dst_ref=buf.at[me], send_sem=send_sems.at[k], recv_sem=recv_sems.at[k],
                device_id=((x + dx) % 2, (y + dy) % 2, (c + dc) % 2), device_id_type=MESH)
            cpy.start()
            sends.append(cpy)
        for k, (dx, dy, dc) in enumerate(flips):
            px, py, pc = (x + dx) % 2, (y + dy) % 2, (c + dc) % 2
            pltpu.make_async_remote_copy(
                src_ref=v_ref, dst_ref=buf.at[4 * px + 2 * py + pc], send_sem=send_sems.at[k],
                recv_sem=recv_sems.at[k], device_id=(px, py, pc), device_id_type=MESH).wait_recv()
        for cpy in sends:
            cpy.wait_send()
        tot = buf[0]
        for i in range(1, N_DEV):
            tot = tot + buf[i]
        out_ref[...] = tot

    return pl.pallas_call(
        body, name="sum_small", out_shape=jax.ShapeDtypeStruct(v.shape, v.dtype),
        in_specs=[pl.BlockSpec(memory_space=pltpu.VMEM)], out_specs=pl.BlockSpec(memory_space=pltpu.VMEM),
        scratch_shapes=[pltpu.VMEM((N_DEV,) + v.shape, v.dtype), pltpu.SemaphoreType.DMA((N_DEV - 1,)),
                        pltpu.SemaphoreType.DMA((N_DEV - 1,))],
    )(v)


def _add_slabs(name, terms):
    arr0 = terms[0][0]
    n = arr0.shape[0] if terms[0][1] is None else 1
    _, r, w = arr0.shape
    tr = 64
    specs = []
    for _, slab in terms:
        if slab is None:
            specs.append(pl.BlockSpec((None, tr, w), lambda i, j: (i, j, 0)))
        else:
            specs.append(pl.BlockSpec((None, tr, w), functools.partial(lambda i, j, sl: (sl, j, 0), sl=slab)))

    def body(*refs):
        tot = refs[0][...]
        for rf in refs[1:-1]:
            tot = tot + rf[...]
        refs[-1][...] = tot

    return pl.pallas_call(
        body, name=name, grid=(n, r // tr), in_specs=specs,
        out_specs=pl.BlockSpec((None, tr, w), lambda i, j: (i, j, 0)),
        out_shape=jax.ShapeDtypeStruct((n, r, w), arr0.dtype),
        compiler_params=_params(("parallel", "parallel")),
    )(*[a for a, _ in terms])


def _adamw(name, w, g, m, v, tm):
    def fn(wv, gv, mv, vv):
        m2 = ADAM_B1 * mv + (1.0 - ADAM_B1) * gv
        v2 = ADAM_B2 * vv + (1.0 - ADAM_B2) * (gv * gv)
        m_hat = m2 / (1.0 - ADAM_B1 ** ADAM_STEP)
        v_hat = v2 / (1.0 - ADAM_B2 ** ADAM_STEP)
        return -ADAM_LR * (m_hat / (jnp.sqrt(v_hat) + ADAM_EPS) + ADAM_WD * wv), m2, v2
    c = w.shape[1]
    return _rows(name, fn, [w, g, m, v], [], [(c, F32)] * 3, tm=tm)


REST_ROWS = 256 + 3 * 128 + 256
REST_SPLITS = (("w_mem_kv", 0, 256), ("w_branch_a", 256, 128), ("w_branch_b", 384, 128),
               ("w_branch_m", 512, 128), ("w_out", 640, 256))


def _rest_pack(t):
    return jnp.concatenate([t[n].reshape(rows, D_MODEL) for n, _, rows in REST_SPLITS], axis=0)


def _rest_unpack(a, shapes):
    return {n: a[r0:r0 + rows].reshape(shapes[n]) for n, r0, rows in REST_SPLITS}


def _small_pack(pre, post, memg, bforget, bmerge):
    pad = jnp.zeros((1, D_MODEL - B_HEADS), F32)
    return jnp.concatenate([pre, post, memg, bmerge.reshape(3, D_MODEL),
                            jnp.concatenate([bforget, pad], axis=1), jnp.zeros((1, D_MODEL), F32)], axis=0)


def _small_unpack(s8):
    return dict(norm_pre_g=s8[0:1], norm_post_g=s8[1:2], norm_mem_g=s8[2:3],
                b_merge=s8[3:6].reshape(1, 3 * D_MODEL), b_forget=s8[6:7, :B_HEADS])


WEIGHTS = ("norm_pre_g", "norm_post_g", "norm_mem_g", "w_in", "b_forget", "b_merge", "w_mem_kv",
           "w_branch_a", "w_branch_b", "w_branch_m", "w_out")
SMALL = ("norm_pre_g", "norm_post_g", "norm_mem_g", "b_forget", "b_merge")


def kernel(x, mem, positions, norm_pre_g, norm_post_g, norm_mem_g, w_in, b_forget, b_merge, w_mem_kv, w_branch_a, w_branch_b, w_branch_m, w_out, loss_target, m_norm_pre_g, m_norm_post_g, m_norm_mem_g, m_w_in, m_b_forget, m_b_merge, m_w_mem_kv, m_w_branch_a, m_w_branch_b, m_w_branch_m, m_w_out, v_norm_pre_g, v_norm_post_g, v_norm_mem_g, v_w_in, v_b_forget, v_b_merge, v_w_mem_kv, v_w_branch_a, v_w_branch_b, v_w_branch_m, v_w_out):
    w = dict(norm_pre_g=norm_pre_g, norm_post_g=norm_post_g, norm_mem_g=norm_mem_g, w_in=w_in[0],
             b_forget=b_forget, b_merge=b_merge, w_mem_kv=w_mem_kv[0], w_branch_a=w_branch_a[0],
             w_branch_b=w_branch_b[0], w_branch_m=w_branch_m[0], w_out=w_out[0])
    mo = dict(norm_pre_g=m_norm_pre_g, norm_post_g=m_norm_post_g, norm_mem_g=m_norm_mem_g, w_in=m_w_in[0],
              b_forget=m_b_forget, b_merge=m_b_merge, w_mem_kv=m_w_mem_kv[0], w_branch_a=m_w_branch_a[0],
              w_branch_b=m_w_branch_b[0], w_branch_m=m_w_branch_m[0], w_out=m_w_out[0])
    vo = dict(norm_pre_g=v_norm_pre_g, norm_post_g=v_norm_post_g, norm_mem_g=v_norm_mem_g, w_in=v_w_in[0],
              b_forget=v_b_forget, b_merge=v_b_merge, w_mem_kv=v_w_mem_kv[0], w_branch_a=v_w_branch_a[0],
              w_branch_b=v_w_branch_b[0], w_branch_m=v_w_branch_m[0], w_out=v_w_out[0])
    s = x.shape[1]
    c = lax.axis_index("c")

    chip = 2 * lax.axis_index("x") + lax.axis_index("y")

    def put(whole, own, slot):
        return lax.dynamic_update_index_in_dim(whole, own.astype(whole.dtype), slot, 0)

    own_w = [w["w_in"].astype(BF16).reshape(2, D_MODEL // 2, SHARD_COLS),
             _rest_pack(w).astype(BF16).reshape(2, REST_ROWS // 2, D_MODEL)]
    all_in, all_rest = [put(a, o, chip) for a, o in zip(_gather_weights(own_w), own_w)]
    all_in = all_in.reshape(N_CHIPS, D_MODEL, SHARD_COLS)
    w_in_f = jnp.concatenate([all_in[p] for p in range(N_CHIPS)], axis=1)
    all_rest = all_rest.reshape(N_CHIPS, REST_ROWS, D_MODEL)
    w_kv_f = all_rest[:, 0:256].reshape(D_MODEL, D_MODEL)
    w_br_f = [all_rest[:, 256 + 128 * i:384 + 128 * i].reshape(N_CHIPS, 512, 256).transpose(1, 0, 2)
              .reshape(512, D_MODEL) for i in range(3)]
    w_out_f = all_rest[:, 640:896].reshape(D_MODEL, D_MODEL)
    w_main = jnp.concatenate([w_in_f[:, :FB_ORIG], w_in_f[:, FB_ORIG + B_HEADS:]], axis=1)
    w_fb = jnp.concatenate([w_in_f[:, FB_ORIG:FB_ORIG + B_HEADS], jnp.zeros((D_MODEL, HD - B_HEADS), BF16)], axis=1)

    loss_lanes, grad_x, g = _local_step(
        x[0], mem[0], positions.reshape(s, 1), loss_target[0], norm_pre_g, norm_post_g, norm_mem_g,
        w_main, w_fb, b_forget, b_merge, w_kv_f, w_br_f[0], w_br_f[1], w_br_f[2], w_out_f)
    loss = lax.psum(jnp.sum(loss_lanes), ("x", "y", "c"))

    def per_chip(name, p):
        a = g[name]
        if name in ("w_mem_kv", "w_out"):
            return a[256 * p:256 * (p + 1)]
        return a[:, 256 * p:256 * (p + 1)]

    in4 = jnp.stack([g["w_in"][:, SHARD_COLS * p:SHARD_COLS * (p + 1)] for p in range(N_CHIPS)])
    rest4 = jnp.stack([_rest_pack({n: per_chip(n, p) for n, _, _ in REST_SPLITS}) for p in range(N_CHIPS)])
    halves = [in4.reshape(N_CHIPS, 2, D_MODEL // 2, SHARD_COLS),
              rest4.reshape(N_CHIPS, 2, REST_ROWS // 2, D_MODEL)]
    mine = [lax.dynamic_index_in_dim(a, c, axis=1, keepdims=False) for a in halves]
    theirs = [lax.dynamic_index_in_dim(a, 1 - c, axis=1, keepdims=False) for a in halves]
    got = _swap_with_sibling(theirs)
    pair = [_add_slabs("add_pair_%d" % i, [(mine[i], None), (got[i], None)]) for i in range(2)]
    landed = [put(a, lax.dynamic_index_in_dim(o, chip, 0, keepdims=False), chip)
              for a, o in zip(_scatter_to_owners(pair), pair)]
    half = [_add_slabs("add_chips_%d" % i, [(landed[i], q) for q in range(N_CHIPS)]) for i in range(2)]
    red_in, red_rest = [put(a, o[0], c) for a, o in zip(_share_with_sibling(half), half)]
    gs = {"w_in": red_in.reshape(D_MODEL, SHARD_COLS)}
    gs.update(_rest_unpack(red_rest.reshape(REST_ROWS, D_MODEL), {n: w[n].shape for n, _, _ in REST_SPLITS}))
    gs.update(_small_unpack(_sum_small(_small_pack(
        g["norm_pre_g"], g["norm_post_g"], g["norm_mem_g"], g["b_forget"], g["b_merge"]))))

    delta, new_m, new_v = {}, {}, {}
    for n, tm in (("w_in", 128), ("w_mem_kv", 256), ("w_branch_a", 512), ("w_branch_b", 512),
                  ("w_branch_m", 512), ("w_out", 256)):
        d_, m_, v_ = _adamw("adamw_" + n, w[n], gs[n], mo[n], vo[n], tm)
        delta[n], new_m[n], new_v[n] = d_[None], m_[None], v_[None]
        gs[n] = gs[n][None]
    packs = [_small_pack(*[t[n] for n in ("norm_pre_g", "norm_post_g", "norm_mem_g", "b_forget", "b_merge")])
             for t in (w, gs, mo, vo)]
    for res, store in zip(_adamw("adamw_small", *packs, 8), (delta, new_m, new_v)):
        store.update(_small_unpack(res))

    return (loss, grad_x[None], *[gs[n] for n in WEIGHTS], *[delta[n] for n in WEIGHTS],
            *[new_m[n] for n in WEIGHTS], *[new_v[n] for n in WEIGHTS])
```

```python
import functools

import jax
import jax.numpy as jnp
from jax import lax
from jax.experimental import pallas as pl
from jax.experimental.pallas import tpu as pltpu

F32 = jnp.float32
BF16 = jnp.bfloat16
MESH = pl.DeviceIdType.MESH

D_MODEL = 1024
N_MEM = 256
EPS = 1e-6
NEG = -1e30
ROPE_THETA = 500000.0
ROT_DIM = 32
HD = 128
A_GROUP = 512
DILATIONS = (1, 4, 16)
BAND = 128
B_HEADS = 8
B_HD = 64
N_CHIPS = 4
N_DEV = 8

C_QA, C_KA, C_VA, C_ZA = 0, 1536, 3072, 4608
C_QB, C_KB, C_VB, C_ZB = 5120, 5632, 6144, 6656
C_QM, C_ZM, C_GL = 7168, 7680, 8192
N_MAIN = 11264
FB_ORIG = 6656
IN_COLS = 11272
SHARD_COLS = IN_COLS // N_CHIPS

ADAM_LR, ADAM_B1, ADAM_B2, ADAM_EPS, ADAM_WD, ADAM_STEP = 0.001, 0.9, 0.999, 1e-08, 0.01, 10

VMEM_LIMIT_V7X = 56 * 1024 * 1024

NT = (((1,), (1,)), ((), ()))
NN = (((1,), (0,)), ((), ()))
TN = (((0,), (0,)), ((), ()))


def _params(sem):
    return pltpu.CompilerParams(dimension_semantics=sem, vmem_limit_bytes=VMEM_LIMIT_V7X)


def _dot(a, b, dn=NN):
    return lax.dot_general(a, b, dn, preferred_element_type=F32)


def _sig(z):
    return 1.0 / (1.0 + jnp.exp(-z))


def _rows(name, fn, row_ins, bc_ins, outs, reds=(), tm=512):
    arrs, specs = [], []
    for r in row_ins:
        arr, w, cb = r if isinstance(r, tuple) else (r, r.shape[1], 0)
        arrs.append(arr)
        specs.append(pl.BlockSpec((tm, w), functools.partial(lambda i, cb: (i, cb), cb=cb)))
    for b in bc_ins:
        arrs.append(b)
        specs.append(pl.BlockSpec(b.shape, lambda i: (0, 0)))
    s = arrs[0].shape[0]
    tm = min(tm, s)
    n_in, n_out = len(arrs), len(outs)

    def body(*refs):
        vals = fn(*[r[...] for r in refs[:n_in]])
        if not isinstance(vals, (tuple, list)):
            vals = (vals,)
        for r, v in zip(refs[n_in:n_in + n_out], vals[:n_out]):
            r[...] = v.astype(r.dtype)
        if reds:
            red_refs = refs[n_in + n_out:]

            @pl.when(pl.program_id(0) == 0)
            def _():
                for r in red_refs:
                    r[...] = jnp.zeros_like(r)

            for r, v in zip(red_refs, vals[n_out:]):
                r[...] += v

    out_shape = [jax.ShapeDtypeStruct((s, c), dt) for c, dt in outs]
    out_shape += [jax.ShapeDtypeStruct((1, c), F32) for c in reds]
    out_specs = [pl.BlockSpec((tm, c), lambda i: (i, 0)) for c, _ in outs]
    out_specs += [pl.BlockSpec((1, c), lambda i: (0, 0)) for c in reds]
    res = pl.pallas_call(
        body, name=name, grid=(s // tm,), in_specs=specs, out_specs=out_specs, out_shape=out_shape,
        compiler_params=_params(("arbitrary",) if reds else ("parallel",)),
    )(*arrs)
    return res


def _mm(name, a, b, mode, out_dtype, tm=1024, tn=1024, tk=1024):
    if mode == "nn":
        (m, k), (_, n) = a.shape, b.shape
    elif mode == "nt":
        (m, k), (n, _) = a.shape, b.shape
    else:
        (k, m), (_, n) = a.shape, b.shape
    tm, tn, tk = min(tm, m), min(tn, n), min(tk, k)
    nk = k // tk
    dn = {"nn": NN, "nt": NT, "tn": TN}[mode]

    def body(a_ref, b_ref, o_ref, *acc):
        part = _dot(a_ref[...].astype(BF16), b_ref[...].astype(BF16), dn)
        if nk == 1:
            o_ref[...] = part.astype(o_ref.dtype)
        else:
            kk = pl.program_id(2)

            @pl.when(kk == 0)
            def _():
                acc[0][...] = part

            @pl.when(kk > 0)
            def _():
                acc[0][...] += part

            @pl.when(kk == nk - 1)
            def _():
                o_ref[...] = acc[0][...].astype(o_ref.dtype)

    a_spec = (pl.BlockSpec((tk, tm), lambda i, j, kk: (kk, i)) if mode == "tn"
              else pl.BlockSpec((tm, tk), lambda i, j, kk: (i, kk)))
    b_spec = (pl.BlockSpec((tn, tk), lambda i, j, kk: (j, kk)) if mode == "nt"
              else pl.BlockSpec((tk, tn), lambda i, j, kk: (kk, j)))
    return pl.pallas_call(
        body, name=name, grid=(m // tm, n // tn, nk), in_specs=[a_spec, b_spec],
        out_specs=pl.BlockSpec((tm, tn), lambda i, j, kk: (i, j)),
        out_shape=jax.ShapeDtypeStruct((m, n), out_dtype),
        scratch_shapes=[pltpu.VMEM((tm, tn), F32)] if nk > 1 else [],
        compiler_params=_params(("parallel", "parallel", "arbitrary")),
    )(a, b)


def _rms_fwd(name, x, g):
    def fn(xv, gv):
        r = lax.rsqrt(jnp.mean(xv * xv, axis=-1, keepdims=True) + EPS)
        return (xv * r * gv,)
    return _rows(name, fn, [x], [g], [(x.shape[1], BF16)], tm=min(512, x.shape[0]))[0]


def _rope_tables(pos, inv):
    ang = pos.astype(F32) * inv
    lane = lax.broadcasted_iota(jnp.int32, ang.shape, 1)
    c = jnp.where(lane < ROT_DIM, jnp.cos(ang), 1.0)
    sn = jnp.sin(ang)
    sg = jnp.where(lane < ROT_DIM // 2, -sn, jnp.where(lane < ROT_DIM, sn, 0.0))
    return c, sg, lane


def _rope_apply(x, c, sg, lane):
    outs = []
    for h in range(x.shape[1] // HD):
        xh = x[:, h * HD:(h + 1) * HD].astype(F32)
        swap = jnp.where(lane < ROT_DIM // 2, pltpu.roll(xh, HD - ROT_DIM // 2, 1),
                         pltpu.roll(xh, ROT_DIM // 2, 1))
        outs.append(xh * c + swap * sg)
    return jnp.concatenate(outs, axis=1)


def _rope_fwd(u, pos, inv):
    def fn(q, k, p, iv):
        c, sg, lane = _rope_tables(p, iv)
        return _rope_apply(q, c, sg, lane), _rope_apply(k, c, sg, lane)
    return _rows("rope_fwd", fn, [(u, 1536, 0), (u, 1536, 1), pos], [inv],
                 [(1536, BF16), (1536, BF16)], tm=256)


def _rope_bwd(dq, dk, pos, inv):
    def fn(q, k, p, iv):
        c, sg, lane = _rope_tables(p, iv)
        return _rope_apply(q, c, -sg, lane), _rope_apply(k, c, -sg, lane)
    return _rows("rope_bwd", fn, [dq, dk, pos], [inv], [(1536, BF16), (1536, BF16)], tm=256)


def _lane_pack(cols, like):
    lane = lax.broadcasted_iota(jnp.int32, like, 1)
    out = jnp.zeros(like, F32)
    for h, cvec in enumerate(cols):
        out = jnp.where(lane == h, cvec, out)
    return out


def _band_specs(l, d, tq):
    nsb = tq // BAND
    nblk = l // BAND
    cur = pl.BlockSpec((tq, A_GROUP), lambda r, i: (i, r))
    prev = pl.BlockSpec((BAND, A_GROUP), lambda r, i: (jnp.maximum(i * nsb - 1, 0), r))
    nxt = pl.BlockSpec((BAND, A_GROUP), lambda r, i: (jnp.minimum((i + 1) * nsb, nblk - 1), r))
    st_cur = pl.BlockSpec((tq, HD), lambda r, i: (i, r))
    st_nxt = pl.BlockSpec((BAND, HD), lambda r, i: (jnp.minimum((i + 1) * nsb, nblk - 1), r))
    return nsb, cur, prev, nxt, st_cur, st_nxt


def _band_mask_q(i, first_tile):
    qr = lax.broadcasted_iota(jnp.int32, (BAND, 2 * BAND), 0)
    kc = lax.broadcasted_iota(jnp.int32, (BAND, 2 * BAND), 1)
    in_prev = (kc < BAND) & (kc >= qr)
    in_cur = (kc >= BAND) & (kc - BAND <= qr)
    if i == 0:
        in_prev = in_prev & jnp.logical_not(first_tile)
    return in_prev | in_cur


def _band_mask_k(j, nsb, last_tile):
    qr = lax.broadcasted_iota(jnp.int32, (2 * BAND, BAND), 0)
    kc = lax.broadcasted_iota(jnp.int32, (2 * BAND, BAND), 1)
    same = (qr < BAND) & (kc <= qr)
    nxt = (qr >= BAND) & (kc >= qr - BAND)
    if j == nsb - 1:
        nxt = nxt & jnp.logical_not(last_tile)
    return same | nxt


def _band_fwd(name, q, k, v, d):
    l = q.shape[0]
    tq = min(512, l)
    nsb, cur, prev, _, st_cur, _ = _band_specs(l, d, tq)
    scale = HD ** -0.5

    def body(q_ref, kc_ref, kp_ref, vc_ref, vp_ref, o_ref, lse_ref):
        first = pl.program_id(1) == 0
        for i in range(nsb):
            lses = []
            mask = _band_mask_q(i, first)
            for h in range(4):
                cs = slice(h * HD, (h + 1) * HD)
                qv = q_ref[i * BAND:(i + 1) * BAND, cs]
                if i == 0:
                    kk = jnp.concatenate([kp_ref[:, cs], kc_ref[0:BAND, cs]], axis=0)
                    vv = jnp.concatenate([vp_ref[:, cs], vc_ref[0:BAND, cs]], axis=0)
                else:
                    kk = kc_ref[(i - 1) * BAND:(i + 1) * BAND, cs]
                    vv = vc_ref[(i - 1) * BAND:(i + 1) * BAND, cs]
                s = jnp.where(mask, _dot(qv, kk, NT) * scale, NEG)
                m = jnp.max(s, axis=-1, keepdims=True)
                p = jnp.exp(s - m)
                den = jnp.sum(p, axis=-1, keepdims=True)
                o_ref[i * BAND:(i + 1) * BAND, cs] = _dot(p.astype(BF16), vv) / den
                lses.append(m + jnp.log(den))
            lse_ref[i * BAND:(i + 1) * BAND, :] = _lane_pack(lses, (BAND, HD))

    return pl.pallas_call(
        body, name=name, grid=(d, l // tq), in_specs=[cur, cur, prev, cur, prev],
        out_specs=[cur, st_cur],
        out_shape=[jax.ShapeDtypeStruct((l, d * A_GROUP), F32), jax.ShapeDtypeStruct((l, d * HD), F32)],
        compiler_params=_params(("parallel", "parallel")),
    )(q, k, k, v, v)


def _band_dq(name, q, k, v, dy, lse, delta, d):
    l = q.shape[0]
    tq = min(512, l)
    nsb, cur, prev, _, st_cur, _ = _band_specs(l, d, tq)
    scale = HD ** -0.5

    def body(q_ref, kc_ref, kp_ref, vc_ref, vp_ref, dy_ref, lse_ref, dl_ref, dq_ref):
        first = pl.program_id(1) == 0
        for i in range(nsb):
            mask = _band_mask_q(i, first)
            rs = slice(i * BAND, (i + 1) * BAND)
            for h in range(4):
                cs = slice(h * HD, (h + 1) * HD)
                if i == 0:
                    kk = jnp.concatenate([kp_ref[:, cs], kc_ref[0:BAND, cs]], axis=0)
                    vv = jnp.concatenate([vp_ref[:, cs], vc_ref[0:BAND, cs]], axis=0)
                else:
                    kk = kc_ref[(i - 1) * BAND:(i + 1) * BAND, cs]
                    vv = vc_ref[(i - 1) * BAND:(i + 1) * BAND, cs]
                s = jnp.where(mask, _dot(q_ref[rs, cs], kk, NT) * scale, NEG)
                p = jnp.exp(s - lse_ref[rs, h:h + 1])
                dp = _dot(dy_ref[rs, cs], vv, NT)
                ds = p * (dp - dl_ref[rs, h:h + 1])
                dq_ref[rs, cs] = (_dot(ds.astype(BF16), kk) * scale).astype(dq_ref.dtype)

    return pl.pallas_call(
        body, name=name, grid=(d, l // tq),
        in_specs=[cur, cur, prev, cur, prev, cur, st_cur, st_cur], out_specs=cur,
        out_shape=jax.ShapeDtypeStruct((l, d * A_GROUP), BF16),
        compiler_params=_params(("parallel", "parallel")),
    )(q, k, k, v, v, dy, lse, delta)


def _band_dkv(name, q, k, v, dy, lse, delta, d):
    l = q.shape[0]
    tq = min(512, l)
    nsb, cur, _, nxt, st_cur, st_nxt = _band_specs(l, d, tq)
    scale = HD ** -0.5
    ntile = l // tq

    def body(k_ref, v_ref, qc_ref, qn_ref, dyc_ref, dyn_ref, lc_ref, ln_ref, dc_ref, dn_ref,
             dk_ref, dv_ref):
        last = pl.program_id(1) == ntile - 1

        def win(c_ref, n_ref, j, cs):
            if j == nsb - 1:
                return jnp.concatenate([c_ref[j * BAND:(j + 1) * BAND, cs], n_ref[:, cs]], axis=0)
            return c_ref[j * BAND:(j + 2) * BAND, cs]

        for j in range(nsb):
            mask = _band_mask_k(j, nsb, last)
            rs = slice(j * BAND, (j + 1) * BAND)
            for h in range(4):
                cs = slice(h * HD, (h + 1) * HD)
                hs = slice(h, h + 1)
                qw = win(qc_ref, qn_ref, j, cs)
                dyw = win(dyc_ref, dyn_ref, j, cs)
                s = jnp.where(mask, _dot(qw, k_ref[rs, cs], NT) * scale, NEG)
                p = jnp.exp(s - win(lc_ref, ln_ref, j, hs))
                dp = _dot(dyw, v_ref[rs, cs], NT)
                ds = p * (dp - win(dc_ref, dn_ref, j, hs))
                dv_ref[rs, cs] = _dot(p.astype(BF16), dyw, TN).astype(dv_ref.dtype)
                dk_ref[rs, cs] = (_dot(ds.astype(BF16), qw, TN) * scale).astype(dk_ref.dtype)

    shp = jax.ShapeDtypeStruct((l, d * A_GROUP), BF16)
    return pl.pallas_call(
        body, name=name, grid=(d, ntile),
        in_specs=[cur, cur, cur, nxt, cur, nxt, st_cur, st_nxt, st_cur, st_nxt],
        out_specs=[cur, cur], out_shape=[shp, shp],
        compiler_params=_params(("parallel", "parallel")),
    )(k, v, q, q, dy, dy, lse, lse, delta, delta)


def _split3(x):
    hi = x.astype(BF16)
    r1 = x - hi.astype(F32)
    mid = r1.astype(BF16)
    lo = (r1 - mid.astype(F32)).astype(BF16)
    return hi, mid, lo


def _fox_prep(z, b):
    h, s = z.shape
    blk = min(512, s)

    def body(z_ref, b_ref, c_ref):
        r = lax.broadcasted_iota(jnp.int32, (blk, blk), 0)
        cidx = lax.broadcasted_iota(jnp.int32, (blk, blk), 1)
        tri = (r <= cidx).astype(BF16)
        carry = jnp.zeros((h, 1), F32)
        for t in range(s // blk):
            zz = z_ref[:, t * blk:(t + 1) * blk] + b_ref[...]
            lf = jnp.minimum(zz, 0.0) - jnp.log(1.0 + jnp.exp(-jnp.abs(zz)))
            hi, mid, lo = _split3(lf)
            cs = _dot(hi, tri) + _dot(mid, tri) + _dot(lo, tri) + carry
            c_ref[:, t * blk:(t + 1) * blk] = cs
            carry = cs[:, blk - 1:blk]

    return pl.pallas_call(body, name="fox_prep", out_shape=jax.ShapeDtypeStruct((h, s), F32))(z, b)


def _fox_prep_bwd(dck, dcq, z, b):
    h, s = z.shape
    blk = min(512, s)

    def body(dck_ref, dcq_ref, z_ref, b_ref, dz_ref, db_ref):
        r = lax.broadcasted_iota(jnp.int32, (blk, blk), 0)
        cidx = lax.broadcasted_iota(jnp.int32, (blk, blk), 1)
        tri = (r >= cidx).astype(BF16)
        carry = jnp.zeros((h, 1), F32)
        tot = jnp.zeros((h, 1), F32)
        for t in reversed(range(s // blk)):
            hi, mid, lo = _split3(dcq_ref[:, t * blk:(t + 1) * blk] - dck_ref[:, t * blk:(t + 1) * blk])
            rc = _dot(hi, tri) + _dot(mid, tri) + _dot(lo, tri) + carry
            carry = rc[:, 0:1]
            zz = z_ref[:, t * blk:(t + 1) * blk] + b_ref[...]
            dz = rc * _sig(-zz)
            dz_ref[:, t * blk:(t + 1) * blk] = dz
            tot = tot + jnp.sum(dz, axis=-1, keepdims=True)
        db_ref[...] = tot

    return pl.pallas_call(
        body, name="fox_prep_bwd",
        out_shape=[jax.ShapeDtypeStruct((h, s), F32), jax.ShapeDtypeStruct((h, 1), F32)])(dck, dcq, z, b)


FOX_W = 128
FOX_C = B_HD
FOX_ONE = B_HD + 3
FOX_SUB = 256
FOX_SUB_FWD = 128
FOX_HEADS_PER_STEP = 2


def _fox_aug_k(k, c_col):
    def fn(kv, cv):
        lane = lax.broadcasted_iota(jnp.int32, kv.shape, 1)
        neg = cv * (-(B_HD ** -0.5))
        hi = neg.astype(BF16).astype(F32)
        mid = (neg - hi).astype(BF16).astype(F32)
        lo = neg - hi - mid
        aux = jnp.where(lane == FOX_C, hi, jnp.where(lane == FOX_C + 1, mid, jnp.where(lane == FOX_C + 2, lo, 0.0)))
        kb = jnp.where(lane < B_HD, kv.astype(F32) * (B_HD ** -0.5), aux)
        return kb, jnp.where(lane == FOX_ONE, 1.0, kb)

    return _rows("fox_aug_k", fn, [k, c_col], [], [(FOX_W, BF16)] * 2, tm=2048)


def _fox_fwd(qf, kb, vt4, t):
    h, s, w = qf.shape
    nt = s // t
    sub = FOX_SUB_FWD
    nsub = t // sub
    nh = FOX_HEADS_PER_STEP

    def body(q_ref, k_ref, v_ref, o_ref, lse_ref):
        i = pl.program_id(1)
        krow = lax.broadcasted_iota(jnp.int32, (sub, t), 0)
        qcol = lax.broadcasted_iota(jnp.int32, (sub, t), 1)

        def tile(j, carry, diag):
            out = []
            for hh in range(nh):
                m, acc = carry[hh]
                qv, vj = q_ref[hh], v_ref[hh, j]
                sts = [_dot(k_ref[hh, pl.ds(pl.multiple_of(j * t + b * sub, sub), sub), :], qv, NT)
                       for b in range(nsub)]
                for b in range(nsub):
                    st = sts[b]
                    if diag:
                        st = jnp.where(krow + b * sub <= qcol, st, NEG)
                    m2 = jnp.maximum(m, jnp.max(st, axis=0, keepdims=True))
                    p = jnp.exp(st - m2).astype(BF16)
                    acc = jnp.exp(m - m2) * acc + _dot(vj[:, b * sub:(b + 1) * sub], p)
                    m = m2
                out.append((m, acc))
            return tuple(out)

        init = tuple((jnp.full((1, t), NEG, F32), jnp.zeros((w, t), F32)) for _ in range(nh))
        carry = lax.fori_loop(0, i, lambda j, c: tile(j, c, False), init)
        for hh, (m, acc) in enumerate(tile(i, carry, True)):
            den = acc[B_HD:B_HD + 1, :]
            o_ref[hh] = (acc[0:B_HD, :] / den).astype(o_ref.dtype)
            lse_ref[hh] = m + jnp.log(den)

    return pl.pallas_call(
        body, name="fox_fwd", grid=(h // nh, nt),
        in_specs=[pl.BlockSpec((nh, t, w), lambda hh, i: (hh, i, 0)),
                  pl.BlockSpec((nh, s, w), lambda hh, i: (hh, 0, 0)),
                  pl.BlockSpec((nh, nt, w, t), lambda hh, i: (hh, 0, 0, 0))],
        out_specs=[pl.BlockSpec((nh, B_HD, t), lambda hh, i: (hh, 0, i)),
                   pl.BlockSpec((nh, 1, t), lambda hh, i: (hh, 0, i))],
        out_shape=[jax.ShapeDtypeStruct((h, B_HD, s), BF16), jax.ShapeDtypeStruct((h, 1, s), F32)],
        compiler_params=_params(("parallel", "parallel")),
    )(qf, kb, vt4)


def _fox_bwd(qf, dow, lse_row, delta_row, kb, kst4, vb, t):
    h, s, w = qf.shape
    nt = s // t
    nsub = t // FOX_SUB
    nh = FOX_HEADS_PER_STEP

    def body(q_ref, do_ref, lse_ref, dl_ref, k_ref, kt_ref, v_ref, dqt_ref, dk_ref, dv_ref, dk_acc, dv_acc):
        j = pl.program_id(1)

        @pl.when(j == 0)
        def _():
            dqt_ref[...] = jnp.zeros_like(dqt_ref)

        dk_acc[...] = jnp.zeros_like(dk_acc)
        dv_acc[...] = jnp.zeros_like(dv_acc)
        krow = lax.broadcasted_iota(jnp.int32, (FOX_SUB, t), 0)
        qcol = lax.broadcasted_iota(jnp.int32, (FOX_SUB, t), 1)
        subs = [slice(b * FOX_SUB, (b + 1) * FOX_SUB) for b in range(nsub)]

        def tile(i, diag):
            i0 = pl.multiple_of(i * t, t)
            for hh in range(nh):
                qi, doi = q_ref[hh, pl.ds(i0, t), :], do_ref[hh, pl.ds(i0, t), :]
                lse, dl = lse_ref[hh, i], dl_ref[hh, i]
                sts = [_dot(k_ref[hh, rs, :], qi, NT) for rs in subs]
                dps = [_dot(v_ref[hh, rs, :], doi, NT) for rs in subs]
                dq = None
                for b, rs in enumerate(subs):
                    st = sts[b] - lse
                    if diag:
                        st = jnp.where(krow + b * FOX_SUB <= qcol, st, NEG)
                    pt = jnp.exp(st)
                    dsb = (pt * (dps[b] - dl)).astype(BF16)
                    dv_acc[hh, rs, :] += _dot(pt.astype(BF16), doi)
                    dk_acc[hh, rs, :] += _dot(dsb, qi)
                    part = _dot(kt_ref[hh, :, rs], dsb)
                    dq = part if dq is None else dq + part
                dqt_ref[hh, i] += dq

        def step(i, carry):
            tile(i, False)
            return carry

        tile(j, True)
        lax.fori_loop(j + 1, nt, step, 0)
        dk_ref[...] = dk_acc[...] * (B_HD ** -0.5)
        dv_ref[...] = dv_acc[...]

    full = pl.BlockSpec((nh, s, w), lambda hh, j: (hh, 0, 0))
    rowst = pl.BlockSpec((nh, nt, 1, t), lambda hh, j: (hh, 0, 0, 0))
    tl = pl.BlockSpec((nh, t, w), lambda hh, j: (hh, j, 0))
    return pl.pallas_call(
        body, name="fox_bwd", grid=(h // nh, nt),
        in_specs=[full, full, rowst, rowst, tl, pl.BlockSpec((nh, None, w, t), lambda hh, j: (hh, j, 0, 0)), tl],
        out_specs=[pl.BlockSpec((nh, nt, w, t), lambda hh, j: (hh, 0, 0, 0)), tl, tl],
        out_shape=[jax.ShapeDtypeStruct((h, nt, w, t), F32), jax.ShapeDtypeStruct((h, s, w), F32),
                   jax.ShapeDtypeStruct((h, s, w), F32)],
        scratch_shapes=[pltpu.VMEM((nh, t, w), F32), pltpu.VMEM((nh, t, w), F32)],
        compiler_params=_params(("parallel", "arbitrary")),
    )(qf, dow, lse_row, delta_row, kb, kst4, vb)


def _mem_fwd(u, mkv, tq=512):
    s = u.shape[0]
    scale = HD ** -0.5

    def body(q_ref, mk_ref, mv_ref, o_ref, lse_ref):
        lses = []
        for h in range(4):
            cs = slice(h * HD, (h + 1) * HD)
            sc = _dot(q_ref[:, cs], mk_ref[:, cs], NT) * scale
            m = jnp.max(sc, axis=-1, keepdims=True)
            p = jnp.exp(sc - m)
            den = jnp.sum(p, axis=-1, keepdims=True)
            o_ref[:, cs] = (_dot(p.astype(BF16), mv_ref[:, cs]) / den).astype(o_ref.dtype)
            lses.append(m + jnp.log(den))
        lse_ref[...] = _lane_pack(lses, (tq, HD))

    return pl.pallas_call(
        body, name="mem_fwd", grid=(s // tq,),
        in_specs=[pl.BlockSpec((tq, 512), lambda i: (i, C_QM // 512)),
                  pl.BlockSpec((N_MEM, 512), lambda i: (0, 0)),
                  pl.BlockSpec((N_MEM, 512), lambda i: (0, 1))],
        out_specs=[pl.BlockSpec((tq, 512), lambda i: (i, 0)), pl.BlockSpec((tq, HD), lambda i: (i, 0))],
        out_shape=[jax.ShapeDtypeStruct((s, 512), BF16), jax.ShapeDtypeStruct((s, HD), F32)],
        compiler_params=_params(("parallel",)),
    )(u, mkv, mkv)


def _mem_bwd(u, mkv, o, do, lse, tq=512):
    s = u.shape[0]
    scale = HD ** -0.5

    def body(q_ref, mk_ref, mv_ref, o_ref, do_ref, lse_ref, dq_ref, dmk_ref, dmv_ref):
        @pl.when(pl.program_id(0) == 0)
        def _():
            dmk_ref[...] = jnp.zeros_like(dmk_ref)
            dmv_ref[...] = jnp.zeros_like(dmv_ref)

        for h in range(4):
            cs = slice(h * HD, (h + 1) * HD)
            qv, dov = q_ref[:, cs], do_ref[:, cs]
            sc = _dot(qv, mk_ref[:, cs], NT) * scale
            p = jnp.exp(sc - lse_ref[:, h:h + 1])
            delta = jnp.sum(dov.astype(F32) * o_ref[:, cs].astype(F32), axis=-1, keepdims=True)
            ds = p * (_dot(dov, mv_ref[:, cs], NT) - delta)
            dsb = ds.astype(BF16)
            dq_ref[:, cs] = (_dot(dsb, mk_ref[:, cs]) * scale).astype(dq_ref.dtype)
            dmk_ref[:, cs] += _dot(dsb, qv, TN) * scale
            dmv_ref[:, cs] += _dot(p.astype(BF16), dov, TN)

    row = pl.BlockSpec((tq, 512), lambda i: (i, 0))
    acc = pl.BlockSpec((N_MEM, 512), lambda i: (0, 0))
    return pl.pallas_call(
        body, name="mem_bwd", grid=(s // tq,),
        in_specs=[pl.BlockSpec((tq, 512), lambda i: (i, C_QM // 512)),
                  pl.BlockSpec((N_MEM, 512), lambda i: (0, 0)),
                  pl.BlockSpec((N_MEM, 512), lambda i: (0, 1)),
                  row, row, pl.BlockSpec((tq, HD), lambda i: (i, 0))],
        out_specs=[row, acc, acc],
        out_shape=[jax.ShapeDtypeStruct((s, 512), BF16), jax.ShapeDtypeStruct((N_MEM, 512), F32),
                   jax.ShapeDtypeStruct((N_MEM, 512), F32)],
        compiler_params=_params(("arbitrary",)),
    )(u, mkv, mkv, o, do, lse)


def _heads_major(a, col0):
    s = a.shape[0]
    return a[:, col0:col0 + 512].reshape(s, B_HEADS, B_HD).transpose(1, 0, 2)


def _token_major(a):
    h, s, dh = a.shape
    return a.transpose(1, 0, 2).reshape(s, h * dh)


def _class_view(a, d):
    s, c = a.shape
    return a.reshape(s // d, d * c)


def _local_step(x, mem, pos, target, g_pre, g_post, g_mem, w_main, w_fb, b_forget, b_merge,
                w_mem_kv, w_ba, w_bb, w_bm, w_out):
    s = x.shape[0]
    t_fox = min(512, s)
    nt = s // t_fox
    half = ROT_DIM // 2
    inv = ROPE_THETA ** (-jnp.arange(half, dtype=F32) / half)
    inv128 = jnp.concatenate([inv, inv, jnp.zeros((HD - ROT_DIM,), F32)]).reshape(1, HD)

    h = _rms_fwd("norm_pre", x, g_pre)
    u = _mm("proj_in", h, w_main, "nn", BF16)
    ufb = _mm("proj_fb", h, w_fb, "nn", F32)
    memn = _rms_fwd("norm_mem", mem, g_mem)
    mkv = _mm("proj_mem", memn, w_mem_kv, "nn", BF16)

    q_rot, k_rot = _rope_fwd(u, pos, inv128)
    os_, lses = [], []
    views = []
    for g, d in enumerate(DILATIONS):
        qv = _class_view(q_rot[:, g * A_GROUP:(g + 1) * A_GROUP], d)
        kv = _class_view(k_rot[:, g * A_GROUP:(g + 1) * A_GROUP], d)
        vv = _class_view(u[:, C_VA + g * A_GROUP:C_VA + (g + 1) * A_GROUP], d)
        views.append((qv, kv, vv))
        o_g, lse_g = _band_fwd("band_fwd%d" % g, qv, kv, vv, d)
        os_.append(o_g.reshape(s, A_GROUP))
        lses.append(lse_g.reshape(s, HD))

    def merge_a(o1, o2, o3, l1, l2, l3, za):
        ys, tots = [], []
        for hh in range(4):
            cs, hs = slice(hh * HD, (hh + 1) * HD), slice(hh, hh + 1)
            mx = jnp.maximum(jnp.maximum(l1[:, hs], l2[:, hs]), l3[:, hs])
            e1, e2, e3 = jnp.exp(l1[:, hs] - mx), jnp.exp(l2[:, hs] - mx), jnp.exp(l3[:, hs] - mx)
            den = e1 + e2 + e3
            ys.append((e1 * o1[:, cs] + e2 * o2[:, cs] + e3 * o3[:, cs]) / den)
            tots.append(mx + jnp.log(den))
        y = jnp.concatenate(ys, axis=1)
        zf = za.astype(F32)
        return y, y * (zf * _sig(zf)), _lane_pack(tots, l1.shape)

    y_a, yg_a, lse_a = _rows("merge_a", merge_a, os_ + lses + [(u, 512, C_ZA // 512)], [],
                             [(512, BF16), (512, BF16), (HD, F32)])

    zrow = ufb[:, :B_HEADS].T
    c = _fox_prep(zrow, b_forget.reshape(B_HEADS, 1))
    n_hs = B_HEADS * s

    def wide(a, fill):
        return jnp.pad(a, ((0, 0), (0, 0), (0, FOX_W - B_HD)), constant_values=fill)

    def tiles_t(a):
        return a.reshape(B_HEADS, nt, t_fox, FOX_W).transpose(0, 1, 3, 2)

    qf = wide(_heads_major(u, C_QB), B_HD ** 0.5)
    vb = wide(_heads_major(u, C_VB), 1.0)
    kb, ks = _fox_aug_k(wide(_heads_major(u, C_KB), 0.0).reshape(n_hs, FOX_W), c.reshape(n_hs, 1))
    kb, ks = kb.reshape(B_HEADS, s, FOX_W), ks.reshape(B_HEADS, s, FOX_W)
    ot, lse_b = _fox_fwd(qf, kb, tiles_t(vb), t_fox)
    y_b = ot.transpose(2, 0, 1).reshape(s, B_HEADS * B_HD)

    y_m, lse_m = _mem_fwd(u, mkv)

    def gate(y, z):
        zf = z.astype(F32)
        return (y.astype(F32) * (zf * _sig(zf)),)

    yg_b = _rows("gate_b", gate, [y_b, (u, 512, C_ZB // 512)], [], [(512, BF16)])[0]
    yg_m = _rows("gate_m", gate, [y_m, (u, 512, C_ZM // 512)], [], [(512, BF16)])[0]

    br_a = _mm("branch_a", yg_a, w_ba, "nn", BF16)
    br_b = _mm("branch_b", yg_b, w_bb, "nn", BF16)
    br_m = _mm("branch_m", yg_m, w_bm, "nn", BF16)
    gl = [(u, 1024, C_GL // 1024 + i) for i in range(3)]
    bm3 = b_merge.reshape(3, D_MODEL)

    def merge(g0, g1, g2, b0, b1, b2, bm):
        tot = 0.0
        for i, (gv, bv) in enumerate(((g0, b0), (g1, b1), (g2, b2))):
            tot = tot + _sig(gv.astype(F32) + bm[i:i + 1, :]) * bv.astype(F32)
        return (tot,)

    merged = _rows("merge_gates", merge, gl + [br_a, br_b, br_m], [bm3], [(D_MODEL, BF16)])[0]
    out = _mm("proj_out", merged, w_out, "nn", F32)

    def tail(xv, ov, tv, gv):
        r = lax.rsqrt(jnp.mean(ov * ov, axis=-1, keepdims=True) + EPS)
        n = ov * r
        err = xv + n * gv - tv
        dy = err * (1.0 / D_MODEL)
        dn = dy * gv
        dout = r * (dn - n * jnp.mean(dn * n, axis=-1, keepdims=True))
        return (dy, dout, jnp.sum(0.5 * err * err * (1.0 / D_MODEL), axis=0, keepdims=True),
                jnp.sum(dy * n, axis=0, keepdims=True))

    dy, dout, loss_lanes, g_post_grad = _rows(
        "tail", tail, [x, out, target], [g_post], [(D_MODEL, F32), (D_MODEL, BF16)],
        reds=[D_MODEL, D_MODEL], tm=256)

    dmerged = _mm("d_merged", dout, w_out, "nt", BF16)
    gw_out = _mm("g_w_out", merged, dout, "tn", F32)

    def merge_bwd(dm, g0, g1, g2, b0, b1, b2, bm):
        dmf = dm.astype(F32)
        dbs, dgs, sums = [], [], []
        for i, (gv, bv) in enumerate(((g0, b0), (g1, b1), (g2, b2))):
            sg = _sig(gv.astype(F32) + bm[i:i + 1, :])
            dbs.append(dmf * sg)
            dg = dmf * bv.astype(F32) * sg * (1.0 - sg)
            dgs.append(dg)
            sums.append(jnp.sum(dg, axis=0, keepdims=True))
        return tuple(dbs + dgs + sums)

    res = _rows("merge_bwd", merge_bwd, [dmerged] + gl + [br_a, br_b, br_m], [bm3],
                [(D_MODEL, BF16)] * 6, reds=[D_MODEL] * 3, tm=256)
    dbr, dgl, g_bmerge = res[0:3], res[3:6], jnp.concatenate(res[6:9], axis=1)

    dyg, gw_branch = [], []
    for nm, dbv, wv, ygv in (("a", dbr[0], w_ba, yg_a), ("b", dbr[1], w_bb, yg_b), ("m", dbr[2], w_bm, yg_m)):
        dyg.append(_mm("d_yg_" + nm, dbv, wv, "nt", BF16))
        gw_branch.append(_mm("g_w_branch_" + nm, ygv, dbv, "tn", F32))

    def gate_bwd(dg, y, z):
        dgf, yf, zf = dg.astype(F32), y.astype(F32), z.astype(F32)
        sg = _sig(zf)
        return dgf * (zf * sg), dgf * yf * (sg * (1.0 + zf * (1.0 - sg)))

    def gate_bwd_a(dg, y, z):
        dyv, dz = gate_bwd(dg, y, z)
        prod = dyv * y.astype(F32)
        dl = [jnp.sum(prod[:, hh * HD:(hh + 1) * HD], axis=-1, keepdims=True) for hh in range(4)]
        return dyv, dz, _lane_pack(dl, (dg.shape[0], HD))

    dy_a, dz_a, delta_a = _rows("gate_bwd_a", gate_bwd_a, [dyg[0], y_a, (u, 512, C_ZA // 512)], [],
                                [(512, BF16), (512, BF16), (HD, F32)])
    dy_b, dz_b = _rows("gate_bwd_b", gate_bwd, [dyg[1], y_b, (u, 512, C_ZB // 512)], [],
                       [(512, BF16), (512, BF16)])
    dy_m, dz_m = _rows("gate_bwd_m", gate_bwd, [dyg[2], y_m, (u, 512, C_ZM // 512)], [],
                       [(512, BF16), (512, BF16)])

    dq_m, dmk, dmv = _mem_bwd(u, mkv, y_m, dy_m, lse_m)
    dmkv = jnp.concatenate([dmk, dmv], axis=1)
    gw_mem_kv = _mm("g_w_mem_kv", memn, dmkv, "tn", F32)
    dmemn = _mm("d_memn", dmkv, w_mem_kv, "nt", F32)

    def mem_gain_grad(mv, dv):
        r = lax.rsqrt(jnp.mean(mv * mv, axis=-1, keepdims=True) + EPS)
        return (jnp.sum(dv * mv * r, axis=0, keepdims=True),)

    g_mem_grad = _rows("g_norm_mem", mem_gain_grad, [mem, dmemn], [], [], reds=[D_MODEL], tm=N_MEM)[0]

    dob = _heads_major(dy_b, 0)

    def fox_delta(a, b):
        return (jnp.sum(a.astype(F32) * b.astype(F32), axis=-1, keepdims=True),)

    ob = ot.transpose(0, 2, 1).reshape(n_hs, B_HD)
    delta_b = _rows("fox_delta", fox_delta, [dob.reshape(n_hs, B_HD), ob], [], [(1, F32)], tm=min(2048, s))[0]
    dqt, dkw, dvw = _fox_bwd(qf, wide(dob, 0.0), lse_b.reshape(B_HEADS, nt, 1, t_fox),
                             delta_b.reshape(B_HEADS, nt, 1, t_fox), kb, tiles_t(ks), vb, t_fox)
    dqb = dqt[:, :, :B_HD, :].transpose(0, 1, 3, 2).reshape(B_HEADS, s, B_HD)
    dkb, dvb = dkw[:, :, :B_HD].astype(BF16), dvw[:, :, :B_HD].astype(BF16)
    dzrow, g_bforget = _fox_prep_bwd(dkw[:, :, B_HD], dqt[:, :, FOX_ONE, :].reshape(B_HEADS, s), zrow,
                                     b_forget.reshape(B_HEADS, 1))
    dfb = jnp.zeros((s, HD), BF16).at[:, :B_HEADS].set(dzrow.T.astype(BF16))

    dqs, dks, dvs = [], [], []
    for g, d in enumerate(DILATIONS):
        qv, kv, vv = views[g]
        dyv, lv, dlv = _class_view(dy_a, d), _class_view(lse_a, d), _class_view(delta_a, d)
        dqs.append(_band_dq("band_dq%d" % g, qv, kv, vv, dyv, lv, dlv, d).reshape(s, A_GROUP))
        dk_g, dv_g = _band_dkv("band_dkv%d" % g, qv, kv, vv, dyv, lv, dlv, d)
        dks.append(dk_g.reshape(s, A_GROUP))
        dvs.append(dv_g.reshape(s, A_GROUP))
    dqa, dka = _rope_bwd(jnp.concatenate(dqs, axis=1), jnp.concatenate(dks, axis=1), pos, inv128)

    du = jnp.concatenate(
        [dqa, dka] + dvs + [dz_a, _token_major(dqb).astype(BF16), _token_major(dkb), _token_major(dvb),
                            dz_b, dq_m, dz_m] + list(dgl), axis=1)

    gw_main = _mm("g_w_main", h, du, "tn", F32)
    gw_fb = _mm("g_w_fb", h, dfb, "tn", F32)
    dh_main = _mm("d_h", du, w_main, "nt", F32, tk=1024)
    dh_fb = _mm("d_h_fb", dfb, w_fb, "nt", F32)

    def pre_bwd(xv, d1, d2, dyv, gv):
        r = lax.rsqrt(jnp.mean(xv * xv, axis=-1, keepdims=True) + EPS)
        n = xv * r
        dhv = d1 + d2
        dn = dhv * gv
        dx = r * (dn - n * jnp.mean(dn * n, axis=-1, keepdims=True))
        return dyv + dx, jnp.sum(dhv * n, axis=0, keepdims=True)

    grad_x, g_pre_grad = _rows("norm_pre_bwd", pre_bwd, [x, dh_main, dh_fb, dy], [g_pre],
                               [(D_MODEL, F32)], reds=[D_MODEL], tm=256)

    gw_in = jnp.concatenate([gw_main[:, :FB_ORIG], gw_fb[:, :B_HEADS], gw_main[:, FB_ORIG:]], axis=1)
    grads = dict(norm_pre_g=g_pre_grad, norm_post_g=g_post_grad, norm_mem_g=g_mem_grad, w_in=gw_in,
                 b_forget=g_bforget.reshape(1, B_HEADS), b_merge=g_bmerge, w_mem_kv=gw_mem_kv,
                 w_branch_a=gw_branch[0], w_branch_b=gw_branch[1], w_branch_m=gw_branch[2], w_out=gw_out)
    return loss_lanes, grad_x, grads


HBM_SPEC = pl.BlockSpec(memory_space=pltpu.HBM)


def _place():
    x, y, c = lax.axis_index("x"), lax.axis_index("y"), lax.axis_index("c")
    chips = [(1 - x, y), (x, 1 - y), (1 - x, 1 - y)]
    return x, y, c, 2 * x + y, chips


N_CHUNKS = 4


def _units(parts, row_axis):
    units = []
    for i, a in enumerate(parts):
        ch = a.shape[row_axis] // N_CHUNKS
        units += [(i, pl.ds(k * ch, ch)) for k in range(N_CHUNKS)]
    return units


def _gather_weights(parts):
    n = len(parts)
    units = _units(parts, 1)
    nu = len(units)

    def body(*refs):
        srcs, outs = refs[:n], refs[n:2 * n]
        send_sems, recv_sems = refs[2 * n:]
        x, y, c, p, chips = _place()
        me, sib = (x, y, c), (x, y, 1 - c)

        def cp(u, k, chip, half, to, from_src=False):
            i, rs = units[u]
            dst = outs[i].at[chip, half, rs]
            return pltpu.make_async_remote_copy(
                src_ref=srcs[i].at[half, rs] if from_src else dst, dst_ref=dst, send_sem=send_sems.at[u, k],
                recv_sem=recv_sems.at[u, k], device_id=to, device_id_type=MESH)

        first = [cp(u, j, p, c, (cx, cy, c), from_src=True)
                 for u in range(nu) for j, (cx, cy) in enumerate(chips)]
        for f in first:
            f.start()
        passed = []
        for u in range(nu):
            for j, (cx, cy) in enumerate(chips):
                cp(u, j, 2 * cx + cy, c, me).wait_recv()
                fw = cp(u, 3 + j, 2 * cx + cy, c, sib)
                fw.start()
                passed.append(fw)
        for u in range(nu):
            for j, (cx, cy) in enumerate(chips):
                cp(u, 3 + j, 2 * cx + cy, 1 - c, me).wait_recv()
        for f in first + passed:
            f.wait_send()

    return pl.pallas_call(
        body, name="gather_weights", in_specs=[HBM_SPEC] * n, out_specs=[HBM_SPEC] * n,
        out_shape=[jax.ShapeDtypeStruct((N_CHIPS,) + a.shape, a.dtype) for a in parts],
        scratch_shapes=[pltpu.SemaphoreType.DMA((nu, 6)), pltpu.SemaphoreType.DMA((nu, 6))],
    )(*parts)


def _swap_with_sibling(parts):
    n = len(parts)
    units = _units(parts, 1)

    def body(*refs):
        srcs, outs = refs[:n], refs[n:2 * n]
        send_sems, recv_sems = refs[2 * n:]
        x, y, c, _, _ = _place()
        cps = [pltpu.make_async_remote_copy(
            src_ref=srcs[i].at[q, rs], dst_ref=outs[i].at[q, rs], send_sem=send_sems.at[u, q],
            recv_sem=recv_sems.at[u, q], device_id=(x, y, 1 - c), device_id_type=MESH)
            for q in range(N_CHIPS) for u, (i, rs) in enumerate(units)]
        for cpy in cps:
            cpy.start()
        for cpy in cps:
            cpy.wait()

    return pl.pallas_call(
        body, name="swap_with_sibling", in_specs=[HBM_SPEC] * n, out_specs=[HBM_SPEC] * n,
        out_shape=[jax.ShapeDtypeStruct(a.shape, a.dtype) for a in parts],
        scratch_shapes=[pltpu.SemaphoreType.DMA((len(units), N_CHIPS)),
                        pltpu.SemaphoreType.DMA((len(units), N_CHIPS))],
    )(*parts)


def _scatter_to_owners(parts):
    n = len(parts)
    units = _units(parts, 1)

    def body(*refs):
        srcs, outs = refs[:n], refs[n:2 * n]
        send_sems, recv_sems = refs[2 * n:]
        x, y, c, p, chips = _place()
        sends = []
        for u, (i, rs) in enumerate(units):
            for j, (cx, cy) in enumerate(chips):
                cpy = pltpu.make_async_remote_copy(
                    src_ref=srcs[i].at[2 * cx + cy, rs], dst_ref=outs[i].at[p, rs], send_sem=send_sems.at[u, j],
                    recv_sem=recv_sems.at[u, j], device_id=(cx, cy, c), device_id_type=MESH)
                cpy.start()
                sends.append(cpy)
        for u, (i, rs) in enumerate(units):
            for j, (cx, cy) in enumerate(chips):
                pltpu.make_async_remote_copy(
                    src_ref=srcs[i].at[2 * cx + cy, rs], dst_ref=outs[i].at[2 * cx + cy, rs],
                    send_sem=send_sems.at[u, j], recv_sem=recv_sems.at[u, j],
                    device_id=(cx, cy, c), device_id_type=MESH).wait_recv()
        for cpy in sends:
            cpy.wait_send()

    return pl.pallas_call(
        body, name="scatter_to_owners", in_specs=[HBM_SPEC] * n, out_specs=[HBM_SPEC] * n,
        out_shape=[jax.ShapeDtypeStruct(a.shape, a.dtype) for a in parts],
        scratch_shapes=[pltpu.SemaphoreType.DMA((len(units), 3)), pltpu.SemaphoreType.DMA((len(units), 3))],
    )(*parts)


def _share_with_sibling(parts):
    n = len(parts)
    units = _units(parts, 1)

    def body(*refs):
        srcs, outs = refs[:n], refs[n:2 * n]
        send_sems, recv_sems = refs[2 * n:]
        x, y, c, _, _ = _place()
        sends = [pltpu.make_async_remote_copy(
            src_ref=srcs[i].at[0, rs], dst_ref=outs[i].at[c, rs], send_sem=send_sems.at[u],
            recv_sem=recv_sems.at[u], device_id=(x, y, 1 - c), device_id_type=MESH)
            for u, (i, rs) in enumerate(units)]
        for cpy in sends:
            cpy.start()
        for u, (i, rs) in enumerate(units):
            pltpu.make_async_remote_copy(
                src_ref=srcs[i].at[0, rs], dst_ref=outs[i].at[1 - c, rs], send_sem=send_sems.at[u],
                recv_sem=recv_sems.at[u], device_id=(x, y, 1 - c), device_id_type=MESH).wait_recv()
        for cpy in sends:
            cpy.wait_send()

    return pl.pallas_call(
        body, name="share_with_sibling", in_specs=[HBM_SPEC] * n, out_specs=[HBM_SPEC] * n,
        out_shape=[jax.ShapeDtypeStruct((2,) + a.shape[1:], a.dtype) for a in parts],
        scratch_shapes=[pltpu.SemaphoreType.DMA((len(units),)), pltpu.SemaphoreType.DMA((len(units),))],
    )(*parts)


def _sum_small(v):
    def body(v_ref, out_ref, buf, send_sems, recv_sems):
        x, y, c, _, _ = _place()
        me = 4 * x + 2 * y + c
        buf[me] = v_ref[...]
        flips = [(dx, dy, dc) for dx in (0, 1) for dy in (0, 1) for dc in (0, 1)][1:]
        sends = []
        for k, (dx, dy, dc) in enumerate(flips):
            cpy = pltpu.make_async_remote_copy(
                src_ref=v_ref, dst_ref=buf.at[me], send_sem=send_sems.at[k], recv_sem=recv_sems.at[k],
                device_id=((x + dx) % 2, (y + dy) % 2, (c + dc) % 2), device_id_type=MESH)
            cpy.start()
            sends.append(cpy)
        for k, (dx, dy, dc) in enumerate(flips):
            px, py, pc = (x + dx) % 2, (y + dy) % 2, (c + dc) % 2
            pltpu.make_async_remote_copy(
                src_ref=v_ref, dst_ref=buf.at[4 * px + 2 * py + pc], send_sem=send_sems.at[k],
                recv_sem=recv_sems.at[k], device_id=(px, py, pc), device_id_type=MESH).wait_recv()
        for cpy in sends:
            cpy.wait_send()
        tot = buf[0]
        for i in range(1, N_DEV):
            tot = tot + buf[i]
        out_ref[...] = tot

    return pl.pallas_call(
        body, name="sum_small", out_shape=jax.ShapeDtypeStruct(v.shape, v.dtype),
        in_specs=[pl.BlockSpec(memory_space=pltpu.VMEM)], out_specs=pl.BlockSpec(memory_space=pltpu.VMEM),
        scratch_shapes=[pltpu.VMEM((N_DEV,) + v.shape, v.dtype), pltpu.SemaphoreType.DMA((N_DEV - 1,)),
                        pltpu.SemaphoreType.DMA((N_DEV - 1,))],
    )(v)


def _add_slabs(name, terms, out_dtype):
    arr0 = terms[0][0]
    n = arr0.shape[0] if terms[0][1] is None else 1
    _, r, w = arr0.shape
    tr = 64
    specs = []
    for _, slab in terms:
        if slab is None:
            specs.append(pl.BlockSpec((None, tr, w), lambda i, j: (i, j, 0)))
        else:
            specs.append(pl.BlockSpec((None, tr, w), functools.partial(lambda i, j, sl: (sl, j, 0), sl=slab)))

    def body(*refs):
        tot = refs[0][...].astype(F32)
        for rf in refs[1:-1]:
            tot = tot + rf[...].astype(F32)
        refs[-1][...] = tot.astype(out_dtype)

    return pl.pallas_call(
        body, name=name, grid=(n, r // tr), in_specs=specs,
        out_specs=pl.BlockSpec((None, tr, w), lambda i, j: (i, j, 0)),
        out_shape=jax.ShapeDtypeStruct((n, r, w), out_dtype),
        compiler_params=_params(("parallel", "parallel")),
    )(*[a for a, _ in terms])


def _adamw(name, w, g, m, v, tm):
    def fn(wv, gv, mv, vv):
        m2 = ADAM_B1 * mv + (1.0 - ADAM_B1) * gv
        v2 = ADAM_B2 * vv + (1.0 - ADAM_B2) * (gv * gv)
        m_hat = m2 / (1.0 - ADAM_B1 ** ADAM_STEP)
        v_hat = v2 / (1.0 - ADAM_B2 ** ADAM_STEP)
        return -ADAM_LR * (m_hat / (jnp.sqrt(v_hat) + ADAM_EPS) + ADAM_WD * wv), m2, v2
    c = w.shape[1]
    return _rows(name, fn, [w, g, m, v], [], [(c, F32)] * 3, tm=tm)


REST_ROWS = 256 + 3 * 128 + 256
REST_SPLITS = (("w_mem_kv", 0, 256), ("w_branch_a", 256, 128), ("w_branch_b", 384, 128),
               ("w_branch_m", 512, 128), ("w_out", 640, 256))


def _rest_pack(t):
    return jnp.concatenate([t[n].reshape(rows, D_MODEL) for n, _, rows in REST_SPLITS], axis=0)


def _rest_unpack(a, shapes):
    return {n: a[r0:r0 + rows].reshape(shapes[n]) for n, r0, rows in REST_SPLITS}


def _small_pack(pre, post, memg, bforget, bmerge):
    pad = jnp.zeros((1, D_MODEL - B_HEADS), F32)
    return jnp.concatenate([pre, post, memg, bmerge.reshape(3, D_MODEL),
                            jnp.concatenate([bforget, pad], axis=1), jnp.zeros((1, D_MODEL), F32)], axis=0)


def _small_unpack(s8):
    return dict(norm_pre_g=s8[0:1], norm_post_g=s8[1:2], norm_mem_g=s8[2:3],
                b_merge=s8[3:6].reshape(1, 3 * D_MODEL), b_forget=s8[6:7, :B_HEADS])


WEIGHTS = ("norm_pre_g", "norm_post_g", "norm_mem_g", "w_in", "b_forget", "b_merge", "w_mem_kv",
           "w_branch_a", "w_branch_b", "w_branch_m", "w_out")
SMALL = ("norm_pre_g", "norm_post_g", "norm_mem_g", "b_forget", "b_merge")


def kernel(x, mem, positions, norm_pre_g, norm_post_g, norm_mem_g, w_in, b_forget, b_merge, w_mem_kv, w_branch_a, w_branch_b, w_branch_m, w_out, loss_target, m_norm_pre_g, m_norm_post_g, m_norm_mem_g, m_w_in, m_b_forget, m_b_merge, m_w_mem_kv, m_w_branch_a, m_w_branch_b, m_w_branch_m, m_w_out, v_norm_pre_g, v_norm_post_g, v_norm_mem_g, v_w_in, v_b_forget, v_b_merge, v_w_mem_kv, v_w_branch_a, v_w_branch_b, v_w_branch_m, v_w_out):
    w = dict(norm_pre_g=norm_pre_g, norm_post_g=norm_post_g, norm_mem_g=norm_mem_g, w_in=w_in[0],
             b_forget=b_forget, b_merge=b_merge, w_mem_kv=w_mem_kv[0], w_branch_a=w_branch_a[0],
             w_branch_b=w_branch_b[0], w_branch_m=w_branch_m[0], w_out=w_out[0])
    mo = dict(norm_pre_g=m_norm_pre_g, norm_post_g=m_norm_post_g, norm_mem_g=m_norm_mem_g, w_in=m_w_in[0],
              b_forget=m_b_forget, b_merge=m_b_merge, w_mem_kv=m_w_mem_kv[0], w_branch_a=m_w_branch_a[0],
              w_branch_b=m_w_branch_b[0], w_branch_m=m_w_branch_m[0], w_out=m_w_out[0])
    vo = dict(norm_pre_g=v_norm_pre_g, norm_post_g=v_norm_post_g, norm_mem_g=v_norm_mem_g, w_in=v_w_in[0],
              b_forget=v_b_forget, b_merge=v_b_merge, w_mem_kv=v_w_mem_kv[0], w_branch_a=v_w_branch_a[0],
              w_branch_b=v_w_branch_b[0], w_branch_m=v_w_branch_m[0], w_out=v_w_out[0])
    s = x.shape[1]
    c = lax.axis_index("c")

    chip = 2 * lax.axis_index("x") + lax.axis_index("y")

    def put(whole, own, slot):
        return lax.dynamic_update_index_in_dim(whole, own.astype(whole.dtype), slot, 0)

    own_w = [w["w_in"].astype(BF16).reshape(2, D_MODEL // 2, SHARD_COLS),
             _rest_pack(w).astype(BF16).reshape(2, REST_ROWS // 2, D_MODEL)]
    all_in, all_rest = [put(a, o, chip) for a, o in zip(_gather_weights(own_w), own_w)]
    all_in = all_in.reshape(N_CHIPS, D_MODEL, SHARD_COLS)
    w_in_f = jnp.concatenate([all_in[p] for p in range(N_CHIPS)], axis=1)
    all_rest = all_rest.reshape(N_CHIPS, REST_ROWS, D_MODEL)
    w_kv_f = all_rest[:, 0:256].reshape(D_MODEL, D_MODEL)
    w_br_f = [all_rest[:, 256 + 128 * i:384 + 128 * i].reshape(N_CHIPS, 512, 256).transpose(1, 0, 2)
              .reshape(512, D_MODEL) for i in range(3)]
    w_out_f = all_rest[:, 640:896].reshape(D_MODEL, D_MODEL)
    w_main = jnp.concatenate([w_in_f[:, :FB_ORIG], w_in_f[:, FB_ORIG + B_HEADS:]], axis=1)
    w_fb = jnp.concatenate([w_in_f[:, FB_ORIG:FB_ORIG + B_HEADS], jnp.zeros((D_MODEL, HD - B_HEADS), BF16)], axis=1)

    loss_lanes, grad_x, g = _local_step(
        x[0], mem[0], positions.reshape(s, 1), loss_target[0], norm_pre_g, norm_post_g, norm_mem_g,
        w_main, w_fb, b_forget, b_merge, w_kv_f, w_br_f[0], w_br_f[1], w_br_f[2], w_out_f)
    loss = lax.psum(jnp.sum(loss_lanes), ("x", "y", "c"))

    def per_chip(name, p):
        a = g[name]
        if name in ("w_mem_kv", "w_out"):
            return a[256 * p:256 * (p + 1)]
        return a[:, 256 * p:256 * (p + 1)]

    in4 = jnp.stack([g["w_in"][:, SHARD_COLS * p:SHARD_COLS * (p + 1)] for p in range(N_CHIPS)])
    rest4 = jnp.stack([_rest_pack({n: per_chip(n, p) for n, _, _ in REST_SPLITS}) for p in range(N_CHIPS)])
    halves = [in4.reshape(N_CHIPS, 2, D_MODEL // 2, SHARD_COLS),
              rest4.reshape(N_CHIPS, 2, REST_ROWS // 2, D_MODEL)]
    mine = [lax.dynamic_index_in_dim(a, c, axis=1, keepdims=False) for a in halves]
    theirs = [lax.dynamic_index_in_dim(a, 1 - c, axis=1, keepdims=False) for a in halves]
    got = _swap_with_sibling(theirs)
    pair = [_add_slabs("add_pair_%d" % i, [(mine[i], None), (got[i], None)], BF16) for i in range(2)]
    landed = [put(a, lax.dynamic_index_in_dim(o, chip, 0, keepdims=False), chip)
              for a, o in zip(_scatter_to_owners(pair), pair)]
    half = [_add_slabs("add_chips_%d" % i, [(landed[i], q) for q in range(N_CHIPS)], F32) for i in range(2)]
    red_in, red_rest = [put(a, o[0], c) for a, o in zip(_share_with_sibling(half), half)]
    gs = {"w_in": red_in.reshape(D_MODEL, SHARD_COLS)}
    gs.update(_rest_unpack(red_rest.reshape(REST_ROWS, D_MODEL), {n: w[n].shape for n, _, _ in REST_SPLITS}))
    gs.update(_small_unpack(_sum_small(_small_pack(
        g["norm_pre_g"], g["norm_post_g"], g["norm_mem_g"], g["b_forget"], g["b_merge"]))))

    delta, new_m, new_v = {}, {}, {}
    for n, tm in (("w_in", 128), ("w_mem_kv", 256), ("w_branch_a", 512), ("w_branch_b", 512),
                  ("w_branch_m", 512), ("w_out", 256)):
        d_, m_, v_ = _adamw("adamw_" + n, w[n], gs[n], mo[n], vo[n], tm)
        delta[n], new_m[n], new_v[n] = d_[None], m_[None], v_[None]
        gs[n] = gs[n][None]
    packs = [_small_pack(*[t[n] for n in ("norm_pre_g", "norm_post_g", "norm_mem_g", "b_forget", "b_merge")])
             for t in (w, gs, mo, vo)]
    for res, store in zip(_adamw("adamw_small", *packs, 8), (delta, new_m, new_v)):
        store.update(_small_unpack(res))

    return (loss, grad_x[None], *[gs[n] for n in WEIGHTS], *[delta[n] for n in WEIGHTS],
            *[new_m[n] for n in WEIGHTS], *[new_v[n] for n in WEIGHTS])
```

```python
import functools

import jax
import jax.numpy as jnp
from jax import lax
from jax.experimental import pallas as pl
from jax.experimental.pallas import tpu as pltpu

F32 = jnp.float32
BF16 = jnp.bfloat16
MESH = pl.DeviceIdType.MESH

D_MODEL = 1024
N_MEM = 256
EPS = 1e-6
NEG = -1e30
ROPE_THETA = 500000.0
ROT_DIM = 32
HD = 128
A_GROUP = 512
DILATIONS = (1, 4, 16)
BAND = 128
B_HEADS = 8
B_HD = 64
N_CHIPS = 4
N_DEV = 8

C_QA, C_KA, C_VA, C_ZA = 0, 1536, 3072, 4608
C_QB, C_KB, C_VB, C_ZB = 5120, 5632, 6144, 6656
C_QM, C_ZM, C_GL = 7168, 7680, 8192
N_MAIN = 11264
FB_ORIG = 6656
IN_COLS = 11272
SHARD_COLS = IN_COLS // N_CHIPS

ADAM_LR, ADAM_B1, ADAM_B2, ADAM_EPS, ADAM_WD, ADAM_STEP = 0.001, 0.9, 0.999, 1e-08, 0.01, 10

VMEM_LIMIT_V7X = 56 * 1024 * 1024

NT = (((1,), (1,)), ((), ()))
NN = (((1,), (0,)), ((), ()))
TN = (((0,), (0,)), ((), ()))


def _params(sem):
    return pltpu.CompilerParams(dimension_semantics=sem, vmem_limit_bytes=VMEM_LIMIT_V7X)


def _dot(a, b, dn=NN):
    return lax.dot_general(a, b, dn, preferred_element_type=F32)


def _sig(z):
    return 1.0 / (1.0 + jnp.exp(-z))


def _rows(name, fn, row_ins, bc_ins, outs, reds=(), tm=512, scratch=()):
    arrs, specs = [], []
    s = None
    for r in row_ins:
        arr, w, cb, d = (tuple(r) + (1,))[:4] if isinstance(r, tuple) else (r, r.shape[1], 0, 1)
        s = arr.shape[0] * d if s is None else s
        arrs.append(arr)
        specs.append((w, cb, d))
    tm = min(tm, s)
    specs = [pl.BlockSpec((tm // d, w), functools.partial(lambda i, cb: (i, cb), cb=cb)) for w, cb, d in specs]
    for b in bc_ins:
        arrs.append(b)
        specs.append(pl.BlockSpec(b.shape, lambda i: (0, 0)))
    outs = [(tuple(o) + (1,))[:3] for o in outs]
    n_in, n_out = len(arrs), len(outs)

    def body(*refs):
        n_ref = n_in + n_out + len(reds)
        vals = fn(*[r[...] for r in refs[:n_in]], *refs[n_ref:])
        if not isinstance(vals, (tuple, list)):
            vals = (vals,)
        for r, v in zip(refs[n_in:n_in + n_out], vals[:n_out]):
            r[...] = v.astype(r.dtype)
        if reds:
            red_refs = refs[n_in + n_out:n_ref]

            @pl.when(pl.program_id(0) == 0)
            def _():
                for r in red_refs:
                    r[...] = jnp.zeros_like(r)

            for r, v in zip(red_refs, vals[n_out:]):
                r[...] += v

    out_shape = [jax.ShapeDtypeStruct((s // d, c), dt) for c, dt, d in outs]
    out_shape += [jax.ShapeDtypeStruct((1, c), F32) for c in reds]
    out_specs = [pl.BlockSpec((tm // d, c), lambda i: (i, 0)) for c, _, d in outs]
    out_specs += [pl.BlockSpec((1, c), lambda i: (0, 0)) for c in reds]
    res = pl.pallas_call(
        body, name=name, grid=(s // tm,), in_specs=specs, out_specs=out_specs, out_shape=out_shape,
        scratch_shapes=list(scratch),
        compiler_params=_params(("arbitrary",) if reds else ("parallel",)),
    )(*arrs)
    return res


def _to_class(x, scr, d):
    if d == 1:
        return x.astype(F32)
    tm, c = x.shape
    for g in range(c // 128):
        scr[g][...] = x[:, g * 128:(g + 1) * 128].astype(F32)
    return jnp.concatenate([scr[g][pl.ds(r, tm // d, stride=d), :] for r in range(d) for g in range(c // 128)],
                           axis=1)


def _from_class(x, scr, d):
    if d == 1:
        return x.astype(F32)
    n, dc = x.shape
    c = dc // d
    for r in range(d):
        for g in range(c // 128):
            scr[g][pl.ds(r, n, stride=d), :] = x[:, r * c + g * 128:r * c + (g + 1) * 128].astype(F32)
    return jnp.concatenate([scr[g][...] for g in range(c // 128)], axis=1)


def _mm(name, a, b, mode, out_dtype, tm=1024, tn=1024, tk=1024):
    if mode == "nn":
        (m, k), (_, n) = a.shape, b.shape
    elif mode == "nt":
        (m, k), (n, _) = a.shape, b.shape
    else:
        (k, m), (_, n) = a.shape, b.shape
    tm, tn, tk = min(tm, m), min(tn, n), min(tk, k)
    nk = k // tk
    dn = {"nn": NN, "nt": NT, "tn": TN}[mode]

    def body(a_ref, b_ref, o_ref, *acc):
        part = _dot(a_ref[...].astype(BF16), b_ref[...].astype(BF16), dn)
        if nk == 1:
            o_ref[...] = part.astype(o_ref.dtype)
        else:
            kk = pl.program_id(2)

            @pl.when(kk == 0)
            def _():
                acc[0][...] = part

            @pl.when(kk > 0)
            def _():
                acc[0][...] += part

            @pl.when(kk == nk - 1)
            def _():
                o_ref[...] = acc[0][...].astype(o_ref.dtype)

    a_spec = (pl.BlockSpec((tk, tm), lambda i, j, kk: (kk, i)) if mode == "tn"
              else pl.BlockSpec((tm, tk), lambda i, j, kk: (i, kk)))
    b_spec = (pl.BlockSpec((tn, tk), lambda i, j, kk: (j, kk)) if mode == "nt"
              else pl.BlockSpec((tk, tn), lambda i, j, kk: (kk, j)))
    return pl.pallas_call(
        body, name=name, grid=(m // tm, n // tn, nk), in_specs=[a_spec, b_spec],
        out_specs=pl.BlockSpec((tm, tn), lambda i, j, kk: (i, j)),
        out_shape=jax.ShapeDtypeStruct((m, n), out_dtype),
        scratch_shapes=[pltpu.VMEM((tm, tn), F32)] if nk > 1 else [],
        compiler_params=_params(("parallel", "parallel", "arbitrary")),
    )(a, b)


def _rms_fwd(name, x, g):
    def fn(xv, gv):
        r = lax.rsqrt(jnp.mean(xv * xv, axis=-1, keepdims=True) + EPS)
        return (xv * r * gv,)
    return _rows(name, fn, [x], [g], [(x.shape[1], BF16)], tm=min(512, x.shape[0]))[0]


def _rope_tables(pos, inv):
    ang = pos.astype(F32) * inv
    lane = lax.broadcasted_iota(jnp.int32, ang.shape, 1)
    c = jnp.where(lane < ROT_DIM, jnp.cos(ang), 1.0)
    sn = jnp.sin(ang)
    sg = jnp.where(lane < ROT_DIM // 2, -sn, jnp.where(lane < ROT_DIM, sn, 0.0))
    return c, sg, lane


def _rope_apply(x, c, sg, lane):
    outs = []
    for h in range(x.shape[1] // HD):
        xh = x[:, h * HD:(h + 1) * HD].astype(F32)
        swap = jnp.where(lane < ROT_DIM // 2, pltpu.roll(xh, HD - ROT_DIM // 2, 1),
                         pltpu.roll(xh, ROT_DIM // 2, 1))
        outs.append(xh * c + swap * sg)
    return jnp.concatenate(outs, axis=1)


ROPE_TM = 256


def _class_scratch(tm):
    return [pltpu.VMEM((tm, 128), F32) for _ in range(A_GROUP // 128)]


def _rope_fwd(u, pos, inv):
    def fn(q, k, v, p, iv, *scr):
        c, sg, lane = _rope_tables(p, iv)
        qr, kr = _rope_apply(q, c, sg, lane), _rope_apply(k, c, sg, lane)
        outs = []
        for g, d in enumerate(DILATIONS):
            gs = slice(g * A_GROUP, (g + 1) * A_GROUP)
            outs += [_to_class(qr[:, gs], scr, d), _to_class(kr[:, gs], scr, d), _to_class(v[:, gs], scr, d)]
        return tuple(outs)

    outs = [(d * A_GROUP, BF16, d) for d in DILATIONS for _ in range(3)]
    return _rows("rope_fwd", fn, [(u, 1536, 0), (u, 1536, 1), (u, 1536, 2), pos], [inv], outs, tm=ROPE_TM,
                 scratch=_class_scratch(ROPE_TM))


def _rope_bwd(dqs, dks, dvs, pos, inv):
    def fn(*args):
        grads, p, iv, scr = args[:9], args[9], args[10], args[11:]
        c, sg, lane = _rope_tables(p, iv)
        tok = [jnp.concatenate([_from_class(grads[3 * k + g], scr, d) for g, d in enumerate(DILATIONS)], axis=1)
               for k in range(3)]
        return _rope_apply(tok[0], c, -sg, lane), _rope_apply(tok[1], c, -sg, lane), tok[2]

    ins = [(a, a.shape[1], 0, d) for grp in (dqs, dks, dvs) for a, d in zip(grp, DILATIONS)]
    return _rows("rope_bwd", fn, ins + [pos], [inv], [(1536, BF16)] * 3, tm=ROPE_TM,
                 scratch=_class_scratch(ROPE_TM))


def _lane_pack(cols, like):
    lane = lax.broadcasted_iota(jnp.int32, like, 1)
    out = jnp.zeros(like, F32)
    for h, cvec in enumerate(cols):
        out = jnp.where(lane == h, cvec, out)
    return out


def _band_specs(l, d, tq):
    nsb = tq // BAND
    nblk = l // BAND
    cur = pl.BlockSpec((tq, A_GROUP), lambda r, i: (i, r))
    prev = pl.BlockSpec((BAND, A_GROUP), lambda r, i: (jnp.maximum(i * nsb - 1, 0), r))
    nxt = pl.BlockSpec((BAND, A_GROUP), lambda r, i: (jnp.minimum((i + 1) * nsb, nblk - 1), r))
    st_cur = pl.BlockSpec((tq, HD), lambda r, i: (i, r))
    st_nxt = pl.BlockSpec((BAND, HD), lambda r, i: (jnp.minimum((i + 1) * nsb, nblk - 1), r))
    return nsb, cur, prev, nxt, st_cur, st_nxt


def _band_mask_q(i, first_tile):
    qr = lax.broadcasted_iota(jnp.int32, (BAND, 2 * BAND), 0)
    kc = lax.broadcasted_iota(jnp.int32, (BAND, 2 * BAND), 1)
    in_prev = (kc < BAND) & (kc >= qr)
    in_cur = (kc >= BAND) & (kc - BAND <= qr)
    if i == 0:
        in_prev = in_prev & jnp.logical_not(first_tile)
    return in_prev | in_cur


def _band_mask_k(j, nsb, last_tile):
    qr = lax.broadcasted_iota(jnp.int32, (2 * BAND, BAND), 0)
    kc = lax.broadcasted_iota(jnp.int32, (2 * BAND, BAND), 1)
    same = (qr < BAND) & (kc <= qr)
    nxt = (qr >= BAND) & (kc >= qr - BAND)
    if j == nsb - 1:
        nxt = nxt & jnp.logical_not(last_tile)
    return same | nxt


def _band_fwd(name, q, k, v, d):
    l = q.shape[0]
    tq = min(512, l)
    nsb, cur, prev, _, st_cur, _ = _band_specs(l, d, tq)
    scale = HD ** -0.5

    def body(q_ref, kc_ref, kp_ref, vc_ref, vp_ref, o_ref, lse_ref):
        first = pl.program_id(1) == 0
        for i in range(nsb):
            lses = []
            mask = _band_mask_q(i, first)
            for h in range(4):
                cs = slice(h * HD, (h + 1) * HD)
                qv = q_ref[i * BAND:(i + 1) * BAND, cs]
                if i == 0:
                    kk = jnp.concatenate([kp_ref[:, cs], kc_ref[0:BAND, cs]], axis=0)
                    vv = jnp.concatenate([vp_ref[:, cs], vc_ref[0:BAND, cs]], axis=0)
                else:
                    kk = kc_ref[(i - 1) * BAND:(i + 1) * BAND, cs]
                    vv = vc_ref[(i - 1) * BAND:(i + 1) * BAND, cs]
                s = jnp.where(mask, _dot(qv, kk, NT) * scale, NEG)
                m = jnp.max(s, axis=-1, keepdims=True)
                p = jnp.exp(s - m)
                den = jnp.sum(p, axis=-1, keepdims=True)
                o_ref[i * BAND:(i + 1) * BAND, cs] = _dot(p.astype(BF16), vv) / den
                lses.append(m + jnp.log(den))
            lse_ref[i * BAND:(i + 1) * BAND, :] = _lane_pack(lses, (BAND, HD))

    return pl.pallas_call(
        body, name=name, grid=(d, l // tq), in_specs=[cur, cur, prev, cur, prev],
        out_specs=[cur, st_cur],
        out_shape=[jax.ShapeDtypeStruct((l, d * A_GROUP), F32), jax.ShapeDtypeStruct((l, d * HD), F32)],
        compiler_params=_params(("parallel", "parallel")),
    )(q, k, k, v, v)


def _band_dq(name, q, k, v, dy, lse, delta, d):
    l = q.shape[0]
    tq = min(512, l)
    nsb, cur, prev, _, st_cur, _ = _band_specs(l, d, tq)
    scale = HD ** -0.5

    def body(q_ref, kc_ref, kp_ref, vc_ref, vp_ref, dy_ref, lse_ref, dl_ref, dq_ref):
        first = pl.program_id(1) == 0
        for i in range(nsb):
            mask = _band_mask_q(i, first)
            rs = slice(i * BAND, (i + 1) * BAND)
            for h in range(4):
                cs = slice(h * HD, (h + 1) * HD)
                if i == 0:
                    kk = jnp.concatenate([kp_ref[:, cs], kc_ref[0:BAND, cs]], axis=0)
                    vv = jnp.concatenate([vp_ref[:, cs], vc_ref[0:BAND, cs]], axis=0)
                else:
                    kk = kc_ref[(i - 1) * BAND:(i + 1) * BAND, cs]
                    vv = vc_ref[(i - 1) * BAND:(i + 1) * BAND, cs]
                s = jnp.where(mask, _dot(q_ref[rs, cs], kk, NT) * scale, NEG)
                p = jnp.exp(s - lse_ref[rs, h:h + 1])
                dp = _dot(dy_ref[rs, cs], vv, NT)
                ds = p * (dp - dl_ref[rs, h:h + 1])
                dq_ref[rs, cs] = (_dot(ds.astype(BF16), kk) * scale).astype(dq_ref.dtype)

    return pl.pallas_call(
        body, name=name, grid=(d, l // tq),
        in_specs=[cur, cur, prev, cur, prev, cur, st_cur, st_cur], out_specs=cur,
        out_shape=jax.ShapeDtypeStruct((l, d * A_GROUP), BF16),
        compiler_params=_params(("parallel", "parallel")),
    )(q, k, k, v, v, dy, lse, delta)


def _band_dkv(name, q, k, v, dy, lse, delta, d):
    l = q.shape[0]
    tq = min(512, l)
    nsb, cur, _, nxt, st_cur, st_nxt = _band_specs(l, d, tq)
    scale = HD ** -0.5
    ntile = l // tq

    def body(k_ref, v_ref, qc_ref, qn_ref, dyc_ref, dyn_ref, lc_ref, ln_ref, dc_ref, dn_ref,
             dk_ref, dv_ref):
        last = pl.program_id(1) == ntile - 1

        def win(c_ref, n_ref, j, cs):
            if j == nsb - 1:
                return jnp.concatenate([c_ref[j * BAND:(j + 1) * BAND, cs], n_ref[:, cs]], axis=0)
            return c_ref[j * BAND:(j + 2) * BAND, cs]

        for j in range(nsb):
            mask = _band_mask_k(j, nsb, last)
            rs = slice(j * BAND, (j + 1) * BAND)
            for h in range(4):
                cs = slice(h * HD, (h + 1) * HD)
                hs = slice(h, h + 1)
                qw = win(qc_ref, qn_ref, j, cs)
                dyw = win(dyc_ref, dyn_ref, j, cs)
                s = jnp.where(mask, _dot(qw, k_ref[rs, cs], NT) * scale, NEG)
                p = jnp.exp(s - win(lc_ref, ln_ref, j, hs))
                dp = _dot(dyw, v_ref[rs, cs], NT)
                ds = p * (dp - win(dc_ref, dn_ref, j, hs))
                dv_ref[rs, cs] = _dot(p.astype(BF16), dyw, TN).astype(dv_ref.dtype)
                dk_ref[rs, cs] = (_dot(ds.astype(BF16), qw, TN) * scale).astype(dk_ref.dtype)

    shp = jax.ShapeDtypeStruct((l, d * A_GROUP), BF16)
    return pl.pallas_call(
        body, name=name, grid=(d, ntile),
        in_specs=[cur, cur, cur, nxt, cur, nxt, st_cur, st_nxt, st_cur, st_nxt],
        out_specs=[cur, cur], out_shape=[shp, shp],
        compiler_params=_params(("parallel", "parallel")),
    )(k, v, q, q, dy, dy, lse, lse, delta, delta)


def _split3(x):
    hi = x.astype(BF16)
    r1 = x - hi.astype(F32)
    mid = r1.astype(BF16)
    lo = (r1 - mid.astype(F32)).astype(BF16)
    return hi, mid, lo


def _fox_prep(z, b):
    h, s = z.shape
    blk = min(512, s)

    def body(z_ref, b_ref, c_ref):
        r = lax.broadcasted_iota(jnp.int32, (blk, blk), 0)
        cidx = lax.broadcasted_iota(jnp.int32, (blk, blk), 1)
        tri = (r <= cidx).astype(BF16)
        carry = jnp.zeros((h, 1), F32)
        for t in range(s // blk):
            zz = z_ref[:, t * blk:(t + 1) * blk] + b_ref[...]
            lf = jnp.minimum(zz, 0.0) - jnp.log(1.0 + jnp.exp(-jnp.abs(zz)))
            hi, mid, lo = _split3(lf)
            cs = _dot(hi, tri) + _dot(mid, tri) + _dot(lo, tri) + carry
            c_ref[:, t * blk:(t + 1) * blk] = cs
            carry = cs[:, blk - 1:blk]

    return pl.pallas_call(body, name="fox_prep", out_shape=jax.ShapeDtypeStruct((h, s), F32))(z, b)


def _fox_prep_bwd(dck, dcq, z, b):
    h, s = z.shape
    blk = min(512, s)

    def body(dck_ref, dcq_ref, z_ref, b_ref, dz_ref, db_ref):
        r = lax.broadcasted_iota(jnp.int32, (blk, blk), 0)
        cidx = lax.broadcasted_iota(jnp.int32, (blk, blk), 1)
        tri = (r >= cidx).astype(BF16)
        carry = jnp.zeros((h, 1), F32)
        tot = jnp.zeros((h, 1), F32)
        for t in reversed(range(s // blk)):
            hi, mid, lo = _split3(dcq_ref[:, t * blk:(t + 1) * blk] - dck_ref[:, t * blk:(t + 1) * blk])
            rc = _dot(hi, tri) + _dot(mid, tri) + _dot(lo, tri) + carry
            carry = rc[:, 0:1]
            zz = z_ref[:, t * blk:(t + 1) * blk] + b_ref[...]
            dz = rc * _sig(-zz)
            dz_ref[:, t * blk:(t + 1) * blk] = dz
            tot = tot + jnp.sum(dz, axis=-1, keepdims=True)
        db_ref[...] = tot

    return pl.pallas_call(
        body, name="fox_prep_bwd",
        out_shape=[jax.ShapeDtypeStruct((h, s), F32), jax.ShapeDtypeStruct((h, 1), F32)])(dck, dcq, z, b)


FOX_W = 128
FOX_C = B_HD
FOX_ONE = B_HD + 3
FOX_SUB = 256
FOX_SUB_FWD = 128
FOX_HEADS_PER_STEP = 2


def _fox_aug_k(k, c_col):
    def fn(kv, cv):
        lane = lax.broadcasted_iota(jnp.int32, kv.shape, 1)
        neg = cv * (-(B_HD ** -0.5))
        hi = neg.astype(BF16).astype(F32)
        mid = (neg - hi).astype(BF16).astype(F32)
        lo = neg - hi - mid
        aux = jnp.where(lane == FOX_C, hi, jnp.where(lane == FOX_C + 1, mid, jnp.where(lane == FOX_C + 2, lo, 0.0)))
        kb = jnp.where(lane < B_HD, kv.astype(F32) * (B_HD ** -0.5), aux)
        return kb, jnp.where(lane == FOX_ONE, 1.0, kb)

    return _rows("fox_aug_k", fn, [k, c_col], [], [(FOX_W, BF16)] * 2, tm=2048)


def _fox_fwd(qf, kb, vt4, t):
    h, s, w = qf.shape
    nt = s // t
    sub = FOX_SUB_FWD
    nsub = t // sub
    nh = FOX_HEADS_PER_STEP

    def body(q_ref, k_ref, v_ref, o_ref, lse_ref):
        i = pl.program_id(1)
        krow = lax.broadcasted_iota(jnp.int32, (sub, t), 0)
        qcol = lax.broadcasted_iota(jnp.int32, (sub, t), 1)

        def tile(j, carry, diag):
            out = []
            for hh in range(nh):
                m, acc = carry[hh]
                qv, vj = q_ref[hh], v_ref[hh, j]
                sts = [_dot(k_ref[hh, pl.ds(pl.multiple_of(j * t + b * sub, sub), sub), :], qv, NT)
                       for b in range(nsub)]
                for b in range(nsub):
                    st = sts[b]
                    if diag:
                        st = jnp.where(krow + b * sub <= qcol, st, NEG)
                    m2 = jnp.maximum(m, jnp.max(st, axis=0, keepdims=True))
                    p = jnp.exp(st - m2).astype(BF16)
                    acc = jnp.exp(m - m2) * acc + _dot(vj[:, b * sub:(b + 1) * sub], p)
                    m = m2
                out.append((m, acc))
            return tuple(out)

        init = tuple((jnp.full((1, t), NEG, F32), jnp.zeros((w, t), F32)) for _ in range(nh))
        carry = lax.fori_loop(0, i, lambda j, c: tile(j, c, False), init)
        for hh, (m, acc) in enumerate(tile(i, carry, True)):
            den = acc[B_HD:B_HD + 1, :]
            o_ref[hh] = (acc[0:B_HD, :] / den).astype(o_ref.dtype)
            lse_ref[hh] = m + jnp.log(den)

    return pl.pallas_call(
        body, name="fox_fwd", grid=(h // nh, nt),
        in_specs=[pl.BlockSpec((nh, t, w), lambda hh, i: (hh, i, 0)),
                  pl.BlockSpec((nh, s, w), lambda hh, i: (hh, 0, 0)),
                  pl.BlockSpec((nh, nt, w, t), lambda hh, i: (hh, 0, 0, 0))],
        out_specs=[pl.BlockSpec((nh, B_HD, t), lambda hh, i: (hh, 0, i)),
                   pl.BlockSpec((nh, 1, t), lambda hh, i: (hh, 0, i))],
        out_shape=[jax.ShapeDtypeStruct((h, B_HD, s), BF16), jax.ShapeDtypeStruct((h, 1, s), F32)],
        compiler_params=_params(("parallel", "parallel")),
    )(qf, kb, vt4)


def _fox_bwd(qf, dow, lse_row, delta_row, kb, kst4, vb, t):
    h, s, w = qf.shape
    nt = s // t
    nsub = t // FOX_SUB
    nh = FOX_HEADS_PER_STEP

    def body(q_ref, do_ref, lse_ref, dl_ref, k_ref, kt_ref, v_ref, dqt_ref, dk_ref, dv_ref, dk_acc, dv_acc):
        j = pl.program_id(1)

        @pl.when(j == 0)
        def _():
            dqt_ref[...] = jnp.zeros_like(dqt_ref)

        dk_acc[...] = jnp.zeros_like(dk_acc)
        dv_acc[...] = jnp.zeros_like(dv_acc)
        krow = lax.broadcasted_iota(jnp.int32, (FOX_SUB, t), 0)
        qcol = lax.broadcasted_iota(jnp.int32, (FOX_SUB, t), 1)
        subs = [slice(b * FOX_SUB, (b + 1) * FOX_SUB) for b in range(nsub)]

        def tile(i, diag):
            i0 = pl.multiple_of(i * t, t)
            for hh in range(nh):
                qi, doi = q_ref[hh, pl.ds(i0, t), :], do_ref[hh, pl.ds(i0, t), :]
                lse, dl = lse_ref[hh, i], dl_ref[hh, i]
                sts = [_dot(k_ref[hh, rs, :], qi, NT) for rs in subs]
                dps = [_dot(v_ref[hh, rs, :], doi, NT) for rs in subs]
                dq = None
                for b, rs in enumerate(subs):
                    st = sts[b] - lse
                    if diag:
                        st = jnp.where(krow + b * FOX_SUB <= qcol, st, NEG)
                    pt = jnp.exp(st)
                    dsb = (pt * (dps[b] - dl)).astype(BF16)
                    dv_acc[hh, rs, :] += _dot(pt.astype(BF16), doi)
                    dk_acc[hh, rs, :] += _dot(dsb, qi)
                    part = _dot(kt_ref[hh, :, rs], dsb)
                    dq = part if dq is None else dq + part
                dqt_ref[hh, i] += dq

        def step(i, carry):
            tile(i, False)
            return carry

        tile(j, True)
        lax.fori_loop(j + 1, nt, step, 0)
        dk_ref[...] = dk_acc[...] * (B_HD ** -0.5)
        dv_ref[...] = dv_acc[...]

    full = pl.BlockSpec((nh, s, w), lambda hh, j: (hh, 0, 0))
    rowst = pl.BlockSpec((nh, nt, 1, t), lambda hh, j: (hh, 0, 0, 0))
    tl = pl.BlockSpec((nh, t, w), lambda hh, j: (hh, j, 0))
    return pl.pallas_call(
        body, name="fox_bwd", grid=(h // nh, nt),
        in_specs=[full, full, rowst, rowst, tl, pl.BlockSpec((nh, None, w, t), lambda hh, j: (hh, j, 0, 0)), tl],
        out_specs=[pl.BlockSpec((nh, nt, w, t), lambda hh, j: (hh, 0, 0, 0)), tl, tl],
        out_shape=[jax.ShapeDtypeStruct((h, nt, w, t), F32), jax.ShapeDtypeStruct((h, s, w), F32),
                   jax.ShapeDtypeStruct((h, s, w), F32)],
        scratch_shapes=[pltpu.VMEM((nh, t, w), F32), pltpu.VMEM((nh, t, w), F32)],
        compiler_params=_params(("parallel", "arbitrary")),
    )(qf, dow, lse_row, delta_row, kb, kst4, vb)


def _mem_fwd(u, mkv, tq=512):
    s = u.shape[0]
    scale = HD ** -0.5

    def body(q_ref, mk_ref, mv_ref, o_ref, lse_ref):
        lses = []
        for h in range(4):
            cs = slice(h * HD, (h + 1) * HD)
            sc = _dot(q_ref[:, cs], mk_ref[:, cs], NT) * scale
            m = jnp.max(sc, axis=-1, keepdims=True)
            p = jnp.exp(sc - m)
            den = jnp.sum(p, axis=-1, keepdims=True)
            o_ref[:, cs] = (_dot(p.astype(BF16), mv_ref[:, cs]) / den).astype(o_ref.dtype)
            lses.append(m + jnp.log(den))
        lse_ref[...] = _lane_pack(lses, (tq, HD))

    return pl.pallas_call(
        body, name="mem_fwd", grid=(s // tq,),
        in_specs=[pl.BlockSpec((tq, 512), lambda i: (i, C_QM // 512)),
                  pl.BlockSpec((N_MEM, 512), lambda i: (0, 0)),
                  pl.BlockSpec((N_MEM, 512), lambda i: (0, 1))],
        out_specs=[pl.BlockSpec((tq, 512), lambda i: (i, 0)), pl.BlockSpec((tq, HD), lambda i: (i, 0))],
        out_shape=[jax.ShapeDtypeStruct((s, 512), BF16), jax.ShapeDtypeStruct((s, HD), F32)],
        compiler_params=_params(("parallel",)),
    )(u, mkv, mkv)


def _mem_bwd(u, mkv, o, do, lse, tq=512):
    s = u.shape[0]
    scale = HD ** -0.5

    def body(q_ref, mk_ref, mv_ref, o_ref, do_ref, lse_ref, dq_ref, dmk_ref, dmv_ref):
        @pl.when(pl.program_id(0) == 0)
        def _():
            dmk_ref[...] = jnp.zeros_like(dmk_ref)
            dmv_ref[...] = jnp.zeros_like(dmv_ref)

        for h in range(4):
            cs = slice(h * HD, (h + 1) * HD)
            qv, dov = q_ref[:, cs], do_ref[:, cs]
            sc = _dot(qv, mk_ref[:, cs], NT) * scale
            p = jnp.exp(sc - lse_ref[:, h:h + 1])
            delta = jnp.sum(dov.astype(F32) * o_ref[:, cs].astype(F32), axis=-1, keepdims=True)
            ds = p * (_dot(dov, mv_ref[:, cs], NT) - delta)
            dsb = ds.astype(BF16)
            dq_ref[:, cs] = (_dot(dsb, mk_ref[:, cs]) * scale).astype(dq_ref.dtype)
            dmk_ref[:, cs] += _dot(dsb, qv, TN) * scale
            dmv_ref[:, cs] += _dot(p.astype(BF16), dov, TN)

    row = pl.BlockSpec((tq, 512), lambda i: (i, 0))
    acc = pl.BlockSpec((N_MEM, 512), lambda i: (0, 0))
    return pl.pallas_call(
        body, name="mem_bwd", grid=(s // tq,),
        in_specs=[pl.BlockSpec((tq, 512), lambda i: (i, C_QM // 512)),
                  pl.BlockSpec((N_MEM, 512), lambda i: (0, 0)),
                  pl.BlockSpec((N_MEM, 512), lambda i: (0, 1)),
                  row, row, pl.BlockSpec((tq, HD), lambda i: (i, 0))],
        out_specs=[row, acc, acc],
        out_shape=[jax.ShapeDtypeStruct((s, 512), BF16), jax.ShapeDtypeStruct((N_MEM, 512), F32),
                   jax.ShapeDtypeStruct((N_MEM, 512), F32)],
        compiler_params=_params(("arbitrary",)),
    )(u, mkv, mkv, o, do, lse)


def _heads_major(a, col0):
    s = a.shape[0]
    return a[:, col0:col0 + 512].reshape(s, B_HEADS, B_HD).transpose(1, 0, 2)


def _token_major(a):
    h, s, dh = a.shape
    return a.transpose(1, 0, 2).reshape(s, h * dh)


def _class_view(a, d):
    s, c = a.shape
    return a.reshape(s // d, d * c)


def _local_step(x, mem, pos, target, g_pre, g_post, g_mem, w_main, w_fb, b_forget, b_merge,
                w_mem_kv, w_ba, w_bb, w_bm, w_out):
    s = x.shape[0]
    t_fox = min(512, s)
    nt = s // t_fox
    half = ROT_DIM // 2
    inv = ROPE_THETA ** (-jnp.arange(half, dtype=F32) / half)
    inv128 = jnp.concatenate([inv, inv, jnp.zeros((HD - ROT_DIM,), F32)]).reshape(1, HD)

    h = _rms_fwd("norm_pre", x, g_pre)
    u = _mm("proj_in", h, w_main, "nn", BF16)
    ufb = _mm("proj_fb", h, w_fb, "nn", F32)
    memn = _rms_fwd("norm_mem", mem, g_mem)
    mkv = _mm("proj_mem", memn, w_mem_kv, "nn", BF16)

    qkv = _rope_fwd(u, pos, inv128)
    views = [tuple(qkv[3 * g:3 * g + 3]) for g in range(3)]
    os_, lses = [], []
    for g, d in enumerate(DILATIONS):
        o_g, lse_g = _band_fwd("band_fwd%d" % g, *views[g], d)
        os_.append((o_g, d * A_GROUP, 0, d))
        lses.append((lse_g, d * HD, 0, d))

    def merge_a(o1, o2, o3, l1, l2, l3, za, *scr):
        o1, o2, o3 = [_from_class(o, scr, d) for o, d in zip((o1, o2, o3), DILATIONS)]
        l1, l2, l3 = [_from_class(lv, scr, d) for lv, d in zip((l1, l2, l3), DILATIONS)]
        ys, tots = [], []
        for hh in range(4):
            cs, hs = slice(hh * HD, (hh + 1) * HD), slice(hh, hh + 1)
            mx = jnp.maximum(jnp.maximum(l1[:, hs], l2[:, hs]), l3[:, hs])
            e1, e2, e3 = jnp.exp(l1[:, hs] - mx), jnp.exp(l2[:, hs] - mx), jnp.exp(l3[:, hs] - mx)
            den = e1 + e2 + e3
            ys.append((e1 * o1[:, cs] + e2 * o2[:, cs] + e3 * o3[:, cs]) / den)
            tots.append(mx + jnp.log(den))
        y = jnp.concatenate(ys, axis=1)
        zf = za.astype(F32)
        tot = _lane_pack(tots, l1.shape)
        return (y, y * (zf * _sig(zf))) + tuple(_to_class(tot, scr, d) for d in DILATIONS)

    res = _rows("merge_a", merge_a, os_ + lses + [(u, 512, C_ZA // 512)], [],
                [(512, BF16), (512, BF16)] + [(d * HD, F32, d) for d in DILATIONS], tm=ROPE_TM,
                scratch=_class_scratch(ROPE_TM))
    y_a, yg_a, lse_a = res[0], res[1], res[2:5]

    zrow = ufb[:, :B_HEADS].T
    c = _fox_prep(zrow, b_forget.reshape(B_HEADS, 1))
    n_hs = B_HEADS * s

    def wide(a, fill):
        return jnp.pad(a, ((0, 0), (0, 0), (0, FOX_W - B_HD)), constant_values=fill)

    def tiles_t(a):
        return a.reshape(B_HEADS, nt, t_fox, FOX_W).transpose(0, 1, 3, 2)

    qf = wide(_heads_major(u, C_QB), B_HD ** 0.5)
    vb = wide(_heads_major(u, C_VB), 1.0)
    kb, ks = _fox_aug_k(wide(_heads_major(u, C_KB), 0.0).reshape(n_hs, FOX_W), c.reshape(n_hs, 1))
    kb, ks = kb.reshape(B_HEADS, s, FOX_W), ks.reshape(B_HEADS, s, FOX_W)
    ot, lse_b = _fox_fwd(qf, kb, tiles_t(vb), t_fox)
    y_b = ot.transpose(2, 0, 1).reshape(s, B_HEADS * B_HD)

    y_m, lse_m = _mem_fwd(u, mkv)

    def gate(y, z):
        zf = z.astype(F32)
        return (y.astype(F32) * (zf * _sig(zf)),)

    yg_b = _rows("gate_b", gate, [y_b, (u, 512, C_ZB // 512)], [], [(512, BF16)])[0]
    yg_m = _rows("gate_m", gate, [y_m, (u, 512, C_ZM // 512)], [], [(512, BF16)])[0]

    br_a = _mm("branch_a", yg_a, w_ba, "nn", BF16)
    br_b = _mm("branch_b", yg_b, w_bb, "nn", BF16)
    br_m = _mm("branch_m", yg_m, w_bm, "nn", BF16)
    gl = [(u, 1024, C_GL // 1024 + i) for i in range(3)]
    bm3 = b_merge.reshape(3, D_MODEL)

    def merge(g0, g1, g2, b0, b1, b2, bm):
        tot = 0.0
        for i, (gv, bv) in enumerate(((g0, b0), (g1, b1), (g2, b2))):
            tot = tot + _sig(gv.astype(F32) + bm[i:i + 1, :]) * bv.astype(F32)
        return (tot,)

    merged = _rows("merge_gates", merge, gl + [br_a, br_b, br_m], [bm3], [(D_MODEL, BF16)])[0]
    out = _mm("proj_out", merged, w_out, "nn", F32)

    def tail(xv, ov, tv, gv):
        r = lax.rsqrt(jnp.mean(ov * ov, axis=-1, keepdims=True) + EPS)
        n = ov * r
        err = xv + n * gv - tv
        dy = err * (1.0 / D_MODEL)
        dn = dy * gv
        dout = r * (dn - n * jnp.mean(dn * n, axis=-1, keepdims=True))
        return (dy, dout, jnp.sum(0.5 * err * err * (1.0 / D_MODEL), axis=0, keepdims=True),
                jnp.sum(dy * n, axis=0, keepdims=True))

    dy, dout, loss_lanes, g_post_grad = _rows(
        "tail", tail, [x, out, target], [g_post], [(D_MODEL, F32), (D_MODEL, BF16)],
        reds=[D_MODEL, D_MODEL], tm=256)

    dmerged = _mm("d_merged", dout, w_out, "nt", BF16)
    gw_out = _mm("g_w_out", merged, dout, "tn", F32)

    def merge_bwd(dm, g0, g1, g2, b0, b1, b2, bm):
        dmf = dm.astype(F32)
        dbs, dgs, sums = [], [], []
        for i, (gv, bv) in enumerate(((g0, b0), (g1, b1), (g2, b2))):
            sg = _sig(gv.astype(F32) + bm[i:i + 1, :])
            dbs.append(dmf * sg)
            dg = dmf * bv.astype(F32) * sg * (1.0 - sg)
            dgs.append(dg)
            sums.append(jnp.sum(dg, axis=0, keepdims=True))
        return tuple(dbs + dgs + sums)

    res = _rows("merge_bwd", merge_bwd, [dmerged] + gl + [br_a, br_b, br_m], [bm3],
                [(D_MODEL, BF16)] * 6, reds=[D_MODEL] * 3, tm=256)
    dbr, dgl, g_bmerge = res[0:3], res[3:6], jnp.concatenate(res[6:9], axis=1)

    dyg, gw_branch = [], []
    for nm, dbv, wv, ygv in (("a", dbr[0], w_ba, yg_a), ("b", dbr[1], w_bb, yg_b), ("m", dbr[2], w_bm, yg_m)):
        dyg.append(_mm("d_yg_" + nm, dbv, wv, "nt", BF16))
        gw_branch.append(_mm("g_w_branch_" + nm, ygv, dbv, "tn", F32))

    def gate_bwd(dg, y, z):
        dgf, yf, zf = dg.astype(F32), y.astype(F32), z.astype(F32)
        sg = _sig(zf)
        return dgf * (zf * sg), dgf * yf * (sg * (1.0 + zf * (1.0 - sg)))

    def gate_bwd_a(dg, y, z, *scr):
        dyv, dz = gate_bwd(dg, y, z)
        prod = dyv * y.astype(F32)
        dl = [jnp.sum(prod[:, hh * HD:(hh + 1) * HD], axis=-1, keepdims=True) for hh in range(4)]
        delta = _lane_pack(dl, (dg.shape[0], HD))
        return ((dz,) + tuple(_to_class(dyv, scr, d) for d in DILATIONS)
                + tuple(_to_class(delta, scr, d) for d in DILATIONS))

    res = _rows("gate_bwd_a", gate_bwd_a, [dyg[0], y_a, (u, 512, C_ZA // 512)], [],
                [(512, BF16)] + [(d * A_GROUP, BF16, d) for d in DILATIONS] + [(d * HD, F32, d) for d in DILATIONS],
                tm=ROPE_TM, scratch=_class_scratch(ROPE_TM))
    dz_a, dy_a, delta_a = res[0], res[1:4], res[4:7]
    dy_b, dz_b = _rows("gate_bwd_b", gate_bwd, [dyg[1], y_b, (u, 512, C_ZB // 512)], [],
                       [(512, BF16), (512, BF16)])
    dy_m, dz_m = _rows("gate_bwd_m", gate_bwd, [dyg[2], y_m, (u, 512, C_ZM // 512)], [],
                       [(512, BF16), (512, BF16)])

    dq_m, dmk, dmv = _mem_bwd(u, mkv, y_m, dy_m, lse_m)
    dmkv = jnp.concatenate([dmk, dmv], axis=1)
    gw_mem_kv = _mm("g_w_mem_kv", memn, dmkv, "tn", F32)
    dmemn = _mm("d_memn", dmkv, w_mem_kv, "nt", F32)

    def mem_gain_grad(mv, dv):
        r = lax.rsqrt(jnp.mean(mv * mv, axis=-1, keepdims=True) + EPS)
        return (jnp.sum(dv * mv * r, axis=0, keepdims=True),)

    g_mem_grad = _rows("g_norm_mem", mem_gain_grad, [mem, dmemn], [], [], reds=[D_MODEL], tm=N_MEM)[0]

    dob = _heads_major(dy_b, 0)

    def fox_delta(a, b):
        return (jnp.sum(a.astype(F32) * b.astype(F32), axis=-1, keepdims=True),)

    ob = ot.transpose(0, 2, 1).reshape(n_hs, B_HD)
    delta_b = _rows("fox_delta", fox_delta, [dob.reshape(n_hs, B_HD), ob], [], [(1, F32)], tm=min(2048, s))[0]
    dqt, dkw, dvw = _fox_bwd(qf, wide(dob, 0.0), lse_b.reshape(B_HEADS, nt, 1, t_fox),
                             delta_b.reshape(B_HEADS, nt, 1, t_fox), kb, tiles_t(ks), vb, t_fox)
    dqb = dqt[:, :, :B_HD, :].transpose(0, 1, 3, 2).reshape(B_HEADS, s, B_HD)
    dkb, dvb = dkw[:, :, :B_HD].astype(BF16), dvw[:, :, :B_HD].astype(BF16)
    dzrow, g_bforget = _fox_prep_bwd(dkw[:, :, B_HD], dqt[:, :, FOX_ONE, :].reshape(B_HEADS, s), zrow,
                                     b_forget.reshape(B_HEADS, 1))
    dfb = jnp.zeros((s, HD), BF16).at[:, :B_HEADS].set(dzrow.T.astype(BF16))

    dqs, dks, dvs = [], [], []
    for g, d in enumerate(DILATIONS):
        qv, kv, vv = views[g]
        dqs.append(_band_dq("band_dq%d" % g, qv, kv, vv, dy_a[g], lse_a[g], delta_a[g], d))
        dk_g, dv_g = _band_dkv("band_dkv%d" % g, qv, kv, vv, dy_a[g], lse_a[g], delta_a[g], d)
        dks.append(dk_g)
        dvs.append(dv_g)
    dqa, dka, dva = _rope_bwd(dqs, dks, dvs, pos, inv128)

    du = jnp.concatenate(
        [dqa, dka, dva, dz_a, _token_major(dqb).astype(BF16), _token_major(dkb), _token_major(dvb),
                            dz_b, dq_m, dz_m] + list(dgl), axis=1)

    gw_main = _mm("g_w_main", h.T, du, "nn", F32, tk=2048)
    gw_fb = _mm("g_w_fb", h, dfb, "tn", F32)
    dh_main = _mm("d_h", du, w_main, "nt", F32, tk=2816)
    dh_fb = _mm("d_h_fb", dfb, w_fb, "nt", F32)

    def pre_bwd(xv, d1, d2, dyv, gv):
        r = lax.rsqrt(jnp.mean(xv * xv, axis=-1, keepdims=True) + EPS)
        n = xv * r
        dhv = d1 + d2
        dn = dhv * gv
        dx = r * (dn - n * jnp.mean(dn * n, axis=-1, keepdims=True))
        return dyv + dx, jnp.sum(dhv * n, axis=0, keepdims=True)

    grad_x, g_pre_grad = _rows("norm_pre_bwd", pre_bwd, [x, dh_main, dh_fb, dy], [g_pre],
                               [(D_MODEL, F32)], reds=[D_MODEL], tm=256)

    gw_in = jnp.concatenate([gw_main[:, :FB_ORIG], gw_fb[:, :B_HEADS], gw_main[:, FB_ORIG:]], axis=1)
    grads = dict(norm_pre_g=g_pre_grad, norm_post_g=g_post_grad, norm_mem_g=g_mem_grad, w_in=gw_in,
                 b_forget=g_bforget.reshape(1, B_HEADS), b_merge=g_bmerge, w_mem_kv=gw_mem_kv,
                 w_branch_a=gw_branch[0], w_branch_b=gw_branch[1], w_branch_m=gw_branch[2], w_out=gw_out)
    return loss_lanes, grad_x, grads


HBM_SPEC = pl.BlockSpec(memory_space=pltpu.HBM)


def _place():
    x, y, c = lax.axis_index("x"), lax.axis_index("y"), lax.axis_index("c")
    chips = [(1 - x, y), (x, 1 - y), (1 - x, 1 - y)]
    return x, y, c, 2 * x + y, chips


N_CHUNKS = 4


def _units(parts, row_axis):
    units = []
    for i, a in enumerate(parts):
        ch = a.shape[row_axis] // N_CHUNKS
        units += [(i, pl.ds(k * ch, ch)) for k in range(N_CHUNKS)]
    return units


def _gather_weights(parts):
    n = len(parts)
    units = _units(parts, 1)
    nu = len(units)

    def body(*refs):
        srcs, outs = refs[:n], refs[n:2 * n]
        send_sems, recv_sems = refs[2 * n:]
        x, y, c, p, chips = _place()
        me, sib = (x, y, c), (x, y, 1 - c)

        def cp(u, k, chip, half, to, from_src=False):
            i, rs = units[u]
            dst = outs[i].at[chip, half, rs]
            return pltpu.make_async_remote_copy(
                src_ref=srcs[i].at[half, rs] if from_src else dst, dst_ref=dst, send_sem=send_sems.at[u, k],
                recv_sem=recv_sems.at[u, k], device_id=to, device_id_type=MESH)

        first = [cp(u, j, p, c, (cx, cy, c), from_src=True)
                 for u in range(nu) for j, (cx, cy) in enumerate(chips)]
        for f in first:
            f.start()
        passed = []
        for u in range(nu):
            for j, (cx, cy) in enumerate(chips):
                cp(u, j, 2 * cx + cy, c, me).wait_recv()
                fw = cp(u, 3 + j, 2 * cx + cy, c, sib)
                fw.start()
                passed.append(fw)
        for u in range(nu):
            for j, (cx, cy) in enumerate(chips):
                cp(u, 3 + j, 2 * cx + cy, 1 - c, me).wait_recv()
        for f in first + passed:
            f.wait_send()

    return pl.pallas_call(
        body, name="gather_weights", in_specs=[HBM_SPEC] * n, out_specs=[HBM_SPEC] * n,
        out_shape=[jax.ShapeDtypeStruct((N_CHIPS,) + a.shape, a.dtype) for a in parts],
        scratch_shapes=[pltpu.SemaphoreType.DMA((nu, 6)), pltpu.SemaphoreType.DMA((nu, 6))],
    )(*parts)


def _swap_with_sibling(parts):
    n = len(parts)
    units = _units(parts, 1)

    def body(*refs):
        srcs, outs = refs[:n], refs[n:2 * n]
        send_sems, recv_sems = refs[2 * n:]
        x, y, c, _, _ = _place()
        cps = [pltpu.make_async_remote_copy(
            src_ref=srcs[i].at[q, rs], dst_ref=outs[i].at[q, rs], send_sem=send_sems.at[u, q],
            recv_sem=recv_sems.at[u, q], device_id=(x, y, 1 - c), device_id_type=MESH)
            for q in range(N_CHIPS) for u, (i, rs) in enumerate(units)]
        for cpy in cps:
            cpy.start()
        for cpy in cps:
            cpy.wait()

    return pl.pallas_call(
        body, name="swap_with_sibling", in_specs=[HBM_SPEC] * n, out_specs=[HBM_SPEC] * n,
        out_shape=[jax.ShapeDtypeStruct(a.shape, a.dtype) for a in parts],
        scratch_shapes=[pltpu.SemaphoreType.DMA((len(units), N_CHIPS)),
                        pltpu.SemaphoreType.DMA((len(units), N_CHIPS))],
    )(*parts)


def _scatter_to_owners(parts):
    n = len(parts)
    units = _units(parts, 1)

    def body(*refs):
        srcs, outs = refs[:n], refs[n:2 * n]
        send_sems, recv_sems = refs[2 * n:]
        x, y, c, p, chips = _place()
        sends = []
        for u, (i, rs) in enumerate(units):
            for j, (cx, cy) in enumerate(chips):
                cpy = pltpu.make_async_remote_copy(
                    src_ref=srcs[i].at[2 * cx + cy, rs], dst_ref=outs[i].at[p, rs], send_sem=send_sems.at[u, j],
                    recv_sem=recv_sems.at[u, j], device_id=(cx, cy, c), device_id_type=MESH)
                cpy.start()
                sends.append(cpy)
        for u, (i, rs) in enumerate(units):
            for j, (cx, cy) in enumerate(chips):
                pltpu.make_async_remote_copy(
                    src_ref=srcs[i].at[2 * cx + cy, rs], dst_ref=outs[i].at[2 * cx + cy, rs],
                    send_sem=send_sems.at[u, j], recv_sem=recv_sems.at[u, j],
                    device_id=(cx, cy, c), device_id_type=MESH).wait_recv()
        for cpy in sends:
            cpy.wait_send()

    return pl.pallas_call(
        body, name="scatter_to_owners", in_specs=[HBM_SPEC] * n, out_specs=[HBM_SPEC] * n,
        out_shape=[jax.ShapeDtypeStruct(a.shape, a.dtype) for a in parts],
        scratch_shapes=[pltpu.SemaphoreType.DMA((len(units), 3)), pltpu.SemaphoreType.DMA((len(units), 3))],
    )(*parts)


def _share_with_sibling(parts):
    n = len(parts)
    units = _units(parts, 1)

    def body(*refs):
        srcs, outs = refs[:n], refs[n:2 * n]
        send_sems, recv_sems = refs[2 * n:]
        x, y, c, _, _ = _place()
        sends = [pltpu.make_async_remote_copy(
            src_ref=srcs[i].at[0, rs], dst_ref=outs[i].at[c, rs], send_sem=send_sems.at[u],
            recv_sem=recv_sems.at[u], device_id=(x, y, 1 - c), device_id_type=MESH)
            for u, (i, rs) in enumerate(units)]
        for cpy in sends:
            cpy.start()
        for u, (i, rs) in enumerate(units):
            pltpu.make_async_remote_copy(
                src_ref=srcs[i].at[0, rs], dst_ref=outs[i].at[1 - c, rs], send_sem=send_sems.at[u],
                recv_sem=recv_sems.at[u], device_id=(x, y, 1 - c), device_id_type=MESH).wait_recv()
        for cpy in sends:
            cpy.wait_send()

    return pl.pallas_call(
        body, name="share_with_sibling", in_specs=[HBM_SPEC] * n, out_specs=[HBM_SPEC] * n,
        out_shape=[jax.ShapeDtypeStruct((2,) + a.shape[1:], a.dtype) for a in parts],
        scratch_shapes=[pltpu.SemaphoreType.DMA((len(units),)), pltpu.SemaphoreType.DMA((len(units),))],
    )(*parts)


def _sum_small(v):
    def body(v_ref, out_ref, buf, send_sems, recv_sems):
        x, y, c, _, _ = _place()
        me = 4 * x + 2 * y + c
        buf[me] = v_ref[...]
        flips = [(dx, dy, dc) for dx in (0, 1) for dy in (0, 1) for dc in (0, 1)][1:]
        sends = []
        for k, (dx, dy, dc) in enumerate(flips):
            cpy = pltpu.make_async_remote_copy(
                src_ref=v_ref, dst_ref=buf.at[me], send_sem=send_sems.at[k], recv_sem=recv_sems.at[k],
                device_id=((x + dx) % 2, (y + dy) % 2, (c + dc) % 2), device_id_type=MESH)
            cpy.start()
            sends.append(cpy)
        for k, (dx, dy, dc) in enumerate(flips):
            px, py, pc = (x + dx) % 2, (y + dy) % 2, (c + dc) % 2
            pltpu.make_async_remote_copy(
                src_ref=v_ref, dst_ref=buf.at[4 * px + 2 * py + pc], send_sem=send_sems.at[k],
                recv_sem=recv_sems.at[k], device_id=(px, py, pc), device_id_type=MESH).wait_recv()
        for cpy in sends:
            cpy.wait_send()
        tot = buf[0]
        for i in range(1, N_DEV):
            tot = tot + buf[i]
        out_ref[...] = tot

    return pl.pallas_call(
        body, name="sum_small", out_shape=jax.ShapeDtypeStruct(v.shape, v.dtype),
        in_specs=[pl.BlockSpec(memory_space=pltpu.VMEM)], out_specs=pl.BlockSpec(memory_space=pltpu.VMEM),
        scratch_shapes=[pltpu.VMEM((N_DEV,) + v.shape, v.dtype), pltpu.SemaphoreType.DMA((N_DEV - 1,)),
                        pltpu.SemaphoreType.DMA((N_DEV - 1,))],
    )(v)


def _add_slabs(name, terms, out_dtype):
    arr0 = terms[0][0]
    n = arr0.shape[0] if terms[0][1] is None else 1
    _, r, w = arr0.shape
    tr = 64
    specs = []
    for _, slab in terms:
        if slab is None:
            specs.append(pl.BlockSpec((None, tr, w), lambda i, j: (i, j, 0)))
        else:
            specs.append(pl.BlockSpec((None, tr, w), functools.partial(lambda i, j, sl: (sl, j, 0), sl=slab)))

    def body(*refs):
        tot = refs[0][...].astype(F32)
        for rf in refs[1:-1]:
            tot = tot + rf[...].astype(F32)
        refs[-1][...] = tot.astype(out_dtype)

    return pl.pallas_call(
        body, name=name, grid=(n, r // tr), in_specs=specs,
        out_specs=pl.BlockSpec((None, tr, w), lambda i, j: (i, j, 0)),
        out_shape=jax.ShapeDtypeStruct((n, r, w), out_dtype),
        compiler_params=_params(("parallel", "parallel")),
    )(*[a for a, _ in terms])


def _adamw(name, w, g, m, v, tm):
    def fn(wv, gv, mv, vv):
        m2 = ADAM_B1 * mv + (1.0 - ADAM_B1) * gv
        v2 = ADAM_B2 * vv + (1.0 - ADAM_B2) * (gv * gv)
        m_hat = m2 / (1.0 - ADAM_B1 ** ADAM_STEP)
        v_hat = v2 / (1.0 - ADAM_B2 ** ADAM_STEP)
        return -ADAM_LR * (m_hat / (jnp.sqrt(v_hat) + ADAM_EPS) + ADAM_WD * wv), m2, v2
    c = w.shape[1]
    return _rows(name, fn, [w, g, m, v], [], [(c, F32)] * 3, tm=tm)


REST_ROWS = 256 + 3 * 128 + 256
REST_SPLITS = (("w_mem_kv", 0, 256), ("w_branch_a", 256, 128), ("w_branch_b", 384, 128),
               ("w_branch_m", 512, 128), ("w_out", 640, 256))


def _rest_pack(t):
    return jnp.concatenate([t[n].reshape(rows, D_MODEL) for n, _, rows in REST_SPLITS], axis=0)


def _rest_unpack(a, shapes):
    return {n: a[r0:r0 + rows].reshape(shapes[n]) for n, r0, rows in REST_SPLITS}


def _small_pack(pre, post, memg, bforget, bmerge):
    pad = jnp.zeros((1, D_MODEL - B_HEADS), F32)
    return jnp.concatenate([pre, post, memg, bmerge.reshape(3, D_MODEL),
                            jnp.concatenate([bforget, pad], axis=1), jnp.zeros((1, D_MODEL), F32)], axis=0)


def _small_unpack(s8):
    return dict(norm_pre_g=s8[0:1], norm_post_g=s8[1:2], norm_mem_g=s8[2:3],
                b_merge=s8[3:6].reshape(1, 3 * D_MODEL), b_forget=s8[6:7, :B_HEADS])


WEIGHTS = ("norm_pre_g", "norm_post_g", "norm_mem_g", "w_in", "b_forget", "b_merge", "w_mem_kv",
           "w_branch_a", "w_branch_b", "w_branch_m", "w_out")
SMALL = ("norm_pre_g", "norm_post_g", "norm_mem_g", "b_forget", "b_merge")


def kernel(x, mem, positions, norm_pre_g, norm_post_g, norm_mem_g, w_in, b_forget, b_merge, w_mem_kv, w_branch_a, w_branch_b, w_branch_m, w_out, loss_target, m_norm_pre_g, m_norm_post_g, m_norm_mem_g, m_w_in, m_b_forget, m_b_merge, m_w_mem_kv, m_w_branch_a, m_w_branch_b, m_w_branch_m, m_w_out, v_norm_pre_g, v_norm_post_g, v_norm_mem_g, v_w_in, v_b_forget, v_b_merge, v_w_mem_kv, v_w_branch_a, v_w_branch_b, v_w_branch_m, v_w_out):
    w = dict(norm_pre_g=norm_pre_g, norm_post_g=norm_post_g, norm_mem_g=norm_mem_g, w_in=w_in[0],
             b_forget=b_forget, b_merge=b_merge, w_mem_kv=w_mem_kv[0], w_branch_a=w_branch_a[0],
             w_branch_b=w_branch_b[0], w_branch_m=w_branch_m[0], w_out=w_out[0])
    mo = dict(norm_pre_g=m_norm_pre_g, norm_post_g=m_norm_post_g, norm_mem_g=m_norm_mem_g, w_in=m_w_in[0],
              b_forget=m_b_forget, b_merge=m_b_merge, w_mem_kv=m_w_mem_kv[0], w_branch_a=m_w_branch_a[0],
              w_branch_b=m_w_branch_b[0], w_branch_m=m_w_branch_m[0], w_out=m_w_out[0])
    vo = dict(norm_pre_g=v_norm_pre_g, norm_post_g=v_norm_post_g, norm_mem_g=v_norm_mem_g, w_in=v_w_in[0],
              b_forget=v_b_forget, b_merge=v_b_merge, w_mem_kv=v_w_mem_kv[0], w_branch_a=v_w_branch_a[0],
              w_branch_b=v_w_branch_b[0], w_branch_m=v_w_branch_m[0], w_out=v_w_out[0])
    s = x.shape[1]
    c = lax.axis_index("c")

    chip = 2 * lax.axis_index("x") + lax.axis_index("y")

    def put(whole, own, slot):
        return lax.dynamic_update_index_in_dim(whole, own.astype(whole.dtype), slot, 0)

    own_w = [w["w_in"].astype(BF16).reshape(2, D_MODEL // 2, SHARD_COLS),
             _rest_pack(w).astype(BF16).reshape(2, REST_ROWS // 2, D_MODEL)]
    all_in, all_rest = [put(a, o, chip) for a, o in zip(_gather_weights(own_w), own_w)]
    all_in = all_in.reshape(N_CHIPS, D_MODEL, SHARD_COLS)
    w_in_f = jnp.concatenate([all_in[p] for p in range(N_CHIPS)], axis=1)
    all_rest = all_rest.reshape(N_CHIPS, REST_ROWS, D_MODEL)
    w_kv_f = all_rest[:, 0:256].reshape(D_MODEL, D_MODEL)
    w_br_f = [all_rest[:, 256 + 128 * i:384 + 128 * i].reshape(N_CHIPS, 512, 256).transpose(1, 0, 2)
              .reshape(512, D_MODEL) for i in range(3)]
    w_out_f = all_rest[:, 640:896].reshape(D_MODEL, D_MODEL)
    w_main = jnp.concatenate([w_in_f[:, :FB_ORIG], w_in_f[:, FB_ORIG + B_HEADS:]], axis=1)
    w_fb = jnp.concatenate([w_in_f[:, FB_ORIG:FB_ORIG + B_HEADS], jnp.zeros((D_MODEL, HD - B_HEADS), BF16)], axis=1)

    loss_lanes, grad_x, g = _local_step(
        x[0], mem[0], positions.reshape(s, 1), loss_target[0], norm_pre_g, norm_post_g, norm_mem_g,
        w_main, w_fb, b_forget, b_merge, w_kv_f, w_br_f[0], w_br_f[1], w_br_f[2], w_out_f)
    loss = lax.psum(jnp.sum(loss_lanes), ("x", "y", "c"))

    def per_chip(name, p):
        a = g[name]
        if name in ("w_mem_kv", "w_out"):
            return a[256 * p:256 * (p + 1)]
        return a[:, 256 * p:256 * (p + 1)]

    in4 = jnp.stack([g["w_in"][:, SHARD_COLS * p:SHARD_COLS * (p + 1)] for p in range(N_CHIPS)])
    rest4 = jnp.stack([_rest_pack({n: per_chip(n, p) for n, _, _ in REST_SPLITS}) for p in range(N_CHIPS)])
    halves = [in4.reshape(N_CHIPS, 2, D_MODEL // 2, SHARD_COLS),
              rest4.reshape(N_CHIPS, 2, REST_ROWS // 2, D_MODEL)]
    mine = [lax.dynamic_index_in_dim(a, c, axis=1, keepdims=False) for a in halves]
    theirs = [lax.dynamic_index_in_dim(a, 1 - c, axis=1, keepdims=False) for a in halves]
    got = _swap_with_sibling(theirs)
    pair = [_add_slabs("add_pair_%d" % i, [(mine[i], None), (got[i], None)], BF16) for i in range(2)]
    landed = [put(a, lax.dynamic_index_in_dim(o, chip, 0, keepdims=False), chip)
              for a, o in zip(_scatter_to_owners(pair), pair)]
    half = [_add_slabs("add_chips_%d" % i, [(landed[i], q) for q in range(N_CHIPS)], F32) for i in range(2)]
    red_in, red_rest = [put(a, o[0], c) for a, o in zip(_share_with_sibling(half), half)]
    gs = {"w_in": red_in.reshape(D_MODEL, SHARD_COLS)}
    gs.update(_rest_unpack(red_rest.reshape(REST_ROWS, D_MODEL), {n: w[n].shape for n, _, _ in REST_SPLITS}))
    gs.update(_small_unpack(_sum_small(_small_pack(
        g["norm_pre_g"], g["norm_post_g"], g["norm_mem_g"], g["b_forget"], g["b_merge"]))))

    delta, new_m, new_v = {}, {}, {}
    for n, tm in (("w_in", 128), ("w_mem_kv", 256), ("w_branch_a", 512), ("w_branch_b", 512),
                  ("w_branch_m", 512), ("w_out", 256)):
        d_, m_, v_ = _adamw("adamw_" + n, w[n], gs[n], mo[n], vo[n], tm)
        delta[n], new_m[n], new_v[n] = d_[None], m_[None], v_[None]
        gs[n] = gs[n][None]
    packs = [_small_pack(*[t[n] for n in ("norm_pre_g", "norm_post_g", "norm_mem_g", "b_forget", "b_merge")])
             for t in (w, gs, mo, vo)]
    for res, store in zip(_adamw("adamw_small", *packs, 8), (delta, new_m, new_v)):
        store.update(_small_unpack(res))

    return (loss, grad_x[None], *[gs[n] for n in WEIGHTS], *[delta[n] for n in WEIGHTS],
            *[new_m[n] for n in WEIGHTS], *[new_v[n] for n in WEIGHTS])
```

```python
import functools

import jax
import jax.numpy as jnp
from jax import lax
from jax.experimental import pallas as pl
from jax.experimental.pallas import tpu as pltpu

F32 = jnp.float32
BF16 = jnp.bfloat16
MESH = pl.DeviceIdType.MESH

D_MODEL = 1024
N_MEM = 256
EPS = 1e-6
NEG = -1e30
ROPE_THETA = 500000.0
ROT_DIM = 32
HD = 128
A_GROUP = 512
DILATIONS = (1, 4, 16)
BAND = 128
B_HEADS = 8
B_HD = 64
N_CHIPS = 4
N_DEV = 8

C_QA, C_KA, C_VA, C_ZA = 0, 1536, 3072, 4608
C_QB, C_KB, C_VB, C_ZB = 5120, 5632, 6144, 6656
C_QM, C_ZM, C_GL = 7168, 7680, 8192
N_MAIN = 11264
FB_ORIG = 6656
IN_COLS = 11272
SHARD_COLS = IN_COLS // N_CHIPS

ADAM_LR, ADAM_B1, ADAM_B2, ADAM_EPS, ADAM_WD, ADAM_STEP = 0.001, 0.9, 0.999, 1e-08, 0.01, 10

VMEM_LIMIT_V7X = 56 * 1024 * 1024

NT = (((1,), (1,)), ((), ()))
NN = (((1,), (0,)), ((), ()))
TN = (((0,), (0,)), ((), ()))


def _params(sem):
    return pltpu.CompilerParams(dimension_semantics=sem, vmem_limit_bytes=VMEM_LIMIT_V7X)


def _dot(a, b, dn=NN):
    return lax.dot_general(a, b, dn, preferred_element_type=F32)


def _sig(z):
    return 1.0 / (1.0 + jnp.exp(-z))


def _rows(name, fn, row_ins, bc_ins, outs, reds=(), tm=512, scratch=()):
    arrs, specs = [], []
    s = None
    for r in row_ins:
        arr, w, cb, d = (tuple(r) + (1,))[:4] if isinstance(r, tuple) else (r, r.shape[1], 0, 1)
        s = arr.shape[0] * d if s is None else s
        arrs.append(arr)
        specs.append((w, cb, d))
    tm = min(tm, s)
    specs = [pl.BlockSpec((tm // d, w), functools.partial(lambda i, cb: (i, cb), cb=cb)) for w, cb, d in specs]
    for b in bc_ins:
        arrs.append(b)
        specs.append(pl.BlockSpec(b.shape, lambda i: (0, 0)))
    outs = [(tuple(o) + (1,))[:3] for o in outs]
    n_in, n_out = len(arrs), len(outs)

    def body(*refs):
        n_ref = n_in + n_out + len(reds)
        vals = fn(*[r[...] for r in refs[:n_in]], *refs[n_ref:])
        if not isinstance(vals, (tuple, list)):
            vals = (vals,)
        for r, v in zip(refs[n_in:n_in + n_out], vals[:n_out]):
            r[...] = v.astype(r.dtype)
        if reds:
            red_refs = refs[n_in + n_out:n_ref]

            @pl.when(pl.program_id(0) == 0)
            def _():
                for r in red_refs:
                    r[...] = jnp.zeros_like(r)

            for r, v in zip(red_refs, vals[n_out:]):
                r[...] += v

    out_shape = [jax.ShapeDtypeStruct((s // d, c), dt) for c, dt, d in outs]
    out_shape += [jax.ShapeDtypeStruct((1, c), F32) for c in reds]
    out_specs = [pl.BlockSpec((tm // d, c), lambda i: (i, 0)) for c, _, d in outs]
    out_specs += [pl.BlockSpec((1, c), lambda i: (0, 0)) for c in reds]
    res = pl.pallas_call(
        body, name=name, grid=(s // tm,), in_specs=specs, out_specs=out_specs, out_shape=out_shape,
        scratch_shapes=list(scratch),
        compiler_params=_params(("arbitrary",) if reds else ("parallel",)),
    )(*arrs)
    return res


def _to_class(x, scr, d):
    if d == 1:
        return x.astype(F32)
    tm, c = x.shape
    for g in range(c // 128):
        scr[g][...] = x[:, g * 128:(g + 1) * 128].astype(F32)
    return jnp.concatenate([scr[g][pl.ds(r, tm // d, stride=d), :] for r in range(d) for g in range(c // 128)],
                           axis=1)


def _from_class(x, scr, d):
    if d == 1:
        return x.astype(F32)
    n, dc = x.shape
    c = dc // d
    for r in range(d):
        for g in range(c // 128):
            scr[g][pl.ds(r, n, stride=d), :] = x[:, r * c + g * 128:r * c + (g + 1) * 128].astype(F32)
    return jnp.concatenate([scr[g][...] for g in range(c // 128)], axis=1)


def _mm(name, a, b, mode, out_dtype, tm=1024, tn=1024, tk=1024):
    if mode == "nn":
        (m, k), (_, n) = a.shape, b.shape
    elif mode == "nt":
        (m, k), (n, _) = a.shape, b.shape
    else:
        (k, m), (_, n) = a.shape, b.shape
    tm, tn, tk = min(tm, m), min(tn, n), min(tk, k)
    nk = k // tk
    dn = {"nn": NN, "nt": NT, "tn": TN}[mode]

    def body(a_ref, b_ref, o_ref, *acc):
        part = _dot(a_ref[...].astype(BF16), b_ref[...].astype(BF16), dn)
        if nk == 1:
            o_ref[...] = part.astype(o_ref.dtype)
        else:
            kk = pl.program_id(2)

            @pl.when(kk == 0)
            def _():
                acc[0][...] = part

            @pl.when(kk > 0)
            def _():
                acc[0][...] += part

            @pl.when(kk == nk - 1)
            def _():
                o_ref[...] = acc[0][...].astype(o_ref.dtype)

    a_spec = (pl.BlockSpec((tk, tm), lambda i, j, kk: (kk, i)) if mode == "tn"
              else pl.BlockSpec((tm, tk), lambda i, j, kk: (i, kk)))
    b_spec = (pl.BlockSpec((tn, tk), lambda i, j, kk: (j, kk)) if mode == "nt"
              else pl.BlockSpec((tk, tn), lambda i, j, kk: (kk, j)))
    return pl.pallas_call(
        body, name=name, grid=(m // tm, n // tn, nk), in_specs=[a_spec, b_spec],
        out_specs=pl.BlockSpec((tm, tn), lambda i, j, kk: (i, j)),
        out_shape=jax.ShapeDtypeStruct((m, n), out_dtype),
        scratch_shapes=[pltpu.VMEM((tm, tn), F32)] if nk > 1 else [],
        compiler_params=_params(("parallel", "parallel", "arbitrary")),
    )(a, b)


def _rms_fwd(name, x, g):
    def fn(xv, gv):
        r = lax.rsqrt(jnp.mean(xv * xv, axis=-1, keepdims=True) + EPS)
        return (xv * r * gv,)
    return _rows(name, fn, [x], [g], [(x.shape[1], BF16)], tm=min(512, x.shape[0]))[0]


def _rope_tables(pos, inv):
    ang = pos.astype(F32) * inv
    lane = lax.broadcasted_iota(jnp.int32, ang.shape, 1)
    c = jnp.where(lane < ROT_DIM, jnp.cos(ang), 1.0)
    sn = jnp.sin(ang)
    sg = jnp.where(lane < ROT_DIM // 2, -sn, jnp.where(lane < ROT_DIM, sn, 0.0))
    return c, sg, lane


def _rope_apply(x, c, sg, lane):
    outs = []
    for h in range(x.shape[1] // HD):
        xh = x[:, h * HD:(h + 1) * HD].astype(F32)
        swap = jnp.where(lane < ROT_DIM // 2, pltpu.roll(xh, HD - ROT_DIM // 2, 1),
                         pltpu.roll(xh, ROT_DIM // 2, 1))
        outs.append(xh * c + swap * sg)
    return jnp.concatenate(outs, axis=1)


ROPE_TM = 256


def _class_scratch(tm):
    return [pltpu.VMEM((tm, 128), F32) for _ in range(A_GROUP // 128)]


def _rope_fwd(u, pos, inv):
    def fn(q, k, v, p, iv, *scr):
        c, sg, lane = _rope_tables(p, iv)
        qr, kr = _rope_apply(q, c, sg, lane), _rope_apply(k, c, sg, lane)
        outs = []
        for g, d in enumerate(DILATIONS):
            gs = slice(g * A_GROUP, (g + 1) * A_GROUP)
            outs += [_to_class(qr[:, gs], scr, d), _to_class(kr[:, gs], scr, d), _to_class(v[:, gs], scr, d)]
        return tuple(outs)

    outs = [(d * A_GROUP, BF16, d) for d in DILATIONS for _ in range(3)]
    return _rows("rope_fwd", fn, [(u, 1536, 0), (u, 1536, 1), (u, 1536, 2), pos], [inv], outs, tm=ROPE_TM,
                 scratch=_class_scratch(ROPE_TM))


def _rope_bwd(dqs, dks, dvs, pos, inv):
    def fn(*args):
        grads, p, iv, scr = args[:9], args[9], args[10], args[11:]
        c, sg, lane = _rope_tables(p, iv)
        tok = [jnp.concatenate([_from_class(grads[3 * k + g], scr, d) for g, d in enumerate(DILATIONS)], axis=1)
               for k in range(3)]
        return _rope_apply(tok[0], c, -sg, lane), _rope_apply(tok[1], c, -sg, lane), tok[2]

    ins = [(a, a.shape[1], 0, d) for grp in (dqs, dks, dvs) for a, d in zip(grp, DILATIONS)]
    return _rows("rope_bwd", fn, ins + [pos], [inv], [(1536, BF16)] * 3, tm=ROPE_TM,
                 scratch=_class_scratch(ROPE_TM))


def _lane_pack(cols, like):
    lane = lax.broadcasted_iota(jnp.int32, like, 1)
    out = jnp.zeros(like, F32)
    for h, cvec in enumerate(cols):
        out = jnp.where(lane == h, cvec, out)
    return out


def _band_specs(l, d, tq):
    nsb = tq // BAND
    nblk = l // BAND
    cur = pl.BlockSpec((tq, A_GROUP), lambda r, i: (i, r))
    prev = pl.BlockSpec((BAND, A_GROUP), lambda r, i: (jnp.maximum(i * nsb - 1, 0), r))
    nxt = pl.BlockSpec((BAND, A_GROUP), lambda r, i: (jnp.minimum((i + 1) * nsb, nblk - 1), r))
    st_cur = pl.BlockSpec((tq, HD), lambda r, i: (i, r))
    st_nxt = pl.BlockSpec((BAND, HD), lambda r, i: (jnp.minimum((i + 1) * nsb, nblk - 1), r))
    return nsb, cur, prev, nxt, st_cur, st_nxt


def _band_mask_q(i, first_tile):
    qr = lax.broadcasted_iota(jnp.int32, (BAND, 2 * BAND), 0)
    kc = lax.broadcasted_iota(jnp.int32, (BAND, 2 * BAND), 1)
    in_prev = (kc < BAND) & (kc >= qr)
    in_cur = (kc >= BAND) & (kc - BAND <= qr)
    if i == 0:
        in_prev = in_prev & jnp.logical_not(first_tile)
    return in_prev | in_cur


def _band_mask_k(j, nsb, last_tile):
    qr = lax.broadcasted_iota(jnp.int32, (2 * BAND, BAND), 0)
    kc = lax.broadcasted_iota(jnp.int32, (2 * BAND, BAND), 1)
    same = (qr < BAND) & (kc <= qr)
    nxt = (qr >= BAND) & (kc >= qr - BAND)
    if j == nsb - 1:
        nxt = nxt & jnp.logical_not(last_tile)
    return same | nxt


def _band_fwd(name, q, k, v, d):
    l = q.shape[0]
    tq = min(512, l)
    nsb, cur, prev, _, st_cur, _ = _band_specs(l, d, tq)
    scale = HD ** -0.5

    def body(q_ref, kc_ref, kp_ref, vc_ref, vp_ref, o_ref, lse_ref):
        first = pl.program_id(1) == 0
        for i in range(nsb):
            lses = []
            mask = _band_mask_q(i, first)
            for h in range(4):
                cs = slice(h * HD, (h + 1) * HD)
                qv = q_ref[i * BAND:(i + 1) * BAND, cs]
                if i == 0:
                    kk = jnp.concatenate([kp_ref[:, cs], kc_ref[0:BAND, cs]], axis=0)
                    vv = jnp.concatenate([vp_ref[:, cs], vc_ref[0:BAND, cs]], axis=0)
                else:
                    kk = kc_ref[(i - 1) * BAND:(i + 1) * BAND, cs]
                    vv = vc_ref[(i - 1) * BAND:(i + 1) * BAND, cs]
                s = jnp.where(mask, _dot(qv, kk, NT) * scale, NEG)
                m = jnp.max(s, axis=-1, keepdims=True)
                p = jnp.exp(s - m)
                den = jnp.sum(p, axis=-1, keepdims=True)
                o_ref[i * BAND:(i + 1) * BAND, cs] = _dot(p.astype(BF16), vv) / den
                lses.append(m + jnp.log(den))
            lse_ref[i * BAND:(i + 1) * BAND, :] = _lane_pack(lses, (BAND, HD))

    return pl.pallas_call(
        body, name=name, grid=(d, l // tq), in_specs=[cur, cur, prev, cur, prev],
        out_specs=[cur, st_cur],
        out_shape=[jax.ShapeDtypeStruct((l, d * A_GROUP), F32), jax.ShapeDtypeStruct((l, d * HD), F32)],
        compiler_params=_params(("parallel", "parallel")),
    )(q, k, k, v, v)


def _band_dq(name, q, k, v, dy, lse, delta, d):
    l = q.shape[0]
    tq = min(512, l)
    nsb, cur, prev, _, st_cur, _ = _band_specs(l, d, tq)
    scale = HD ** -0.5

    def body(q_ref, kc_ref, kp_ref, vc_ref, vp_ref, dy_ref, lse_ref, dl_ref, dq_ref):
        first = pl.program_id(1) == 0
        for i in range(nsb):
            mask = _band_mask_q(i, first)
            rs = slice(i * BAND, (i + 1) * BAND)
            for h in range(4):
                cs = slice(h * HD, (h + 1) * HD)
                if i == 0:
                    kk = jnp.concatenate([kp_ref[:, cs], kc_ref[0:BAND, cs]], axis=0)
                    vv = jnp.concatenate([vp_ref[:, cs], vc_ref[0:BAND, cs]], axis=0)
                else:
                    kk = kc_ref[(i - 1) * BAND:(i + 1) * BAND, cs]
                    vv = vc_ref[(i - 1) * BAND:(i + 1) * BAND, cs]
                s = jnp.where(mask, _dot(q_ref[rs, cs], kk, NT) * scale, NEG)
                p = jnp.exp(s - lse_ref[rs, h:h + 1])
                dp = _dot(dy_ref[rs, cs], vv, NT)
                ds = p * (dp - dl_ref[rs, h:h + 1])
                dq_ref[rs, cs] = (_dot(ds.astype(BF16), kk) * scale).astype(dq_ref.dtype)

    return pl.pallas_call(
        body, name=name, grid=(d, l // tq),
        in_specs=[cur, cur, prev, cur, prev, cur, st_cur, st_cur], out_specs=cur,
        out_shape=jax.ShapeDtypeStruct((l, d * A_GROUP), BF16),
        compiler_params=_params(("parallel", "parallel")),
    )(q, k, k, v, v, dy, lse, delta)


def _band_dkv(name, q, k, v, dy, lse, delta, d):
    l = q.shape[0]
    tq = min(512, l)
    nsb, cur, _, nxt, st_cur, st_nxt = _band_specs(l, d, tq)
    scale = HD ** -0.5
    ntile = l // tq

    def body(k_ref, v_ref, qc_ref, qn_ref, dyc_ref, dyn_ref, lc_ref, ln_ref, dc_ref, dn_ref,
             dk_ref, dv_ref):
        last = pl.program_id(1) == ntile - 1

        def win(c_ref, n_ref, j, cs):
            if j == nsb - 1:
                return jnp.concatenate([c_ref[j * BAND:(j + 1) * BAND, cs], n_ref[:, cs]], axis=0)
            return c_ref[j * BAND:(j + 2) * BAND, cs]

        for j in range(nsb):
            mask = _band_mask_k(j, nsb, last)
            rs = slice(j * BAND, (j + 1) * BAND)
            for h in range(4):
                cs = slice(h * HD, (h + 1) * HD)
                hs = slice(h, h + 1)
                qw = win(qc_ref, qn_ref, j, cs)
                dyw = win(dyc_ref, dyn_ref, j, cs)
                s = jnp.where(mask, _dot(qw, k_ref[rs, cs], NT) * scale, NEG)
                p = jnp.exp(s - win(lc_ref, ln_ref, j, hs))
                dp = _dot(dyw, v_ref[rs, cs], NT)
                ds = p * (dp - win(dc_ref, dn_ref, j, hs))
                dv_ref[rs, cs] = _dot(p.astype(BF16), dyw, TN).astype(dv_ref.dtype)
                dk_ref[rs, cs] = (_dot(ds.astype(BF16), qw, TN) * scale).astype(dk_ref.dtype)

    shp = jax.ShapeDtypeStruct((l, d * A_GROUP), BF16)
    return pl.pallas_call(
        body, name=name, grid=(d, ntile),
        in_specs=[cur, cur, cur, nxt, cur, nxt, st_cur, st_nxt, st_cur, st_nxt],
        out_specs=[cur, cur], out_shape=[shp, shp],
        compiler_params=_params(("parallel", "parallel")),
    )(k, v, q, q, dy, dy, lse, lse, delta, delta)


def _split3(x):
    hi = x.astype(BF16)
    r1 = x - hi.astype(F32)
    mid = r1.astype(BF16)
    lo = (r1 - mid.astype(F32)).astype(BF16)
    return hi, mid, lo


def _fox_prep(z, b):
    h, s = z.shape
    blk = min(512, s)

    def body(z_ref, b_ref, c_ref):
        r = lax.broadcasted_iota(jnp.int32, (blk, blk), 0)
        cidx = lax.broadcasted_iota(jnp.int32, (blk, blk), 1)
        tri = (r <= cidx).astype(BF16)
        carry = jnp.zeros((h, 1), F32)
        for t in range(s // blk):
            zz = z_ref[:, t * blk:(t + 1) * blk] + b_ref[...]
            lf = jnp.minimum(zz, 0.0) - jnp.log(1.0 + jnp.exp(-jnp.abs(zz)))
            hi, mid, lo = _split3(lf)
            cs = _dot(hi, tri) + _dot(mid, tri) + _dot(lo, tri) + carry
            c_ref[:, t * blk:(t + 1) * blk] = cs
            carry = cs[:, blk - 1:blk]

    return pl.pallas_call(body, name="fox_prep", out_shape=jax.ShapeDtypeStruct((h, s), F32))(z, b)


def _fox_prep_bwd(dck, dcq, z, b):
    h, s = z.shape
    blk = min(512, s)

    def body(dck_ref, dcq_ref, z_ref, b_ref, dz_ref, db_ref):
        r = lax.broadcasted_iota(jnp.int32, (blk, blk), 0)
        cidx = lax.broadcasted_iota(jnp.int32, (blk, blk), 1)
        tri = (r >= cidx).astype(BF16)
        carry = jnp.zeros((h, 1), F32)
        tot = jnp.zeros((h, 1), F32)
        for t in reversed(range(s // blk)):
            hi, mid, lo = _split3(dcq_ref[:, t * blk:(t + 1) * blk] - dck_ref[:, t * blk:(t + 1) * blk])
            rc = _dot(hi, tri) + _dot(mid, tri) + _dot(lo, tri) + carry
            carry = rc[:, 0:1]
            zz = z_ref[:, t * blk:(t + 1) * blk] + b_ref[...]
            dz = rc * _sig(-zz)
            dz_ref[:, t * blk:(t + 1) * blk] = dz
            tot = tot + jnp.sum(dz, axis=-1, keepdims=True)
        db_ref[...] = tot

    return pl.pallas_call(
        body, name="fox_prep_bwd",
        out_shape=[jax.ShapeDtypeStruct((h, s), F32), jax.ShapeDtypeStruct((h, 1), F32)])(dck, dcq, z, b)


FOX_W = 128
FOX_C = B_HD
FOX_ONE = B_HD + 3
FOX_SUB = 256
FOX_SUB_FWD = 128
FOX_HEADS_PER_STEP = 2


def _fox_aug_k(k, c_col):
    def fn(kv, cv):
        lane = lax.broadcasted_iota(jnp.int32, kv.shape, 1)
        neg = cv * (-(B_HD ** -0.5))
        hi = neg.astype(BF16).astype(F32)
        mid = (neg - hi).astype(BF16).astype(F32)
        lo = neg - hi - mid
        aux = jnp.where(lane == FOX_C, hi, jnp.where(lane == FOX_C + 1, mid, jnp.where(lane == FOX_C + 2, lo, 0.0)))
        kb = jnp.where(lane < B_HD, kv.astype(F32) * (B_HD ** -0.5), aux)
        return kb, jnp.where(lane == FOX_ONE, 1.0, kb)

    return _rows("fox_aug_k", fn, [k, c_col], [], [(FOX_W, BF16)] * 2, tm=2048)


FOX_DEAD = -110.0


def _fox_norm2(x, t):
    h, s, w = x.shape

    def body(x_ref, o_ref):
        xv = x_ref[...].astype(F32)
        lane = lax.broadcasted_iota(jnp.int32, xv.shape, 1)
        n2 = jnp.sum(jnp.where(lane < B_HD, xv * xv, 0.0), axis=-1, keepdims=True)
        o_ref[...] = jnp.broadcast_to(jnp.max(n2, axis=0, keepdims=True), o_ref.shape)

    return pl.pallas_call(
        body, name="fox_norm2", grid=(h, s // t),
        in_specs=[pl.BlockSpec((None, t, w), lambda hh, i: (hh, i, 0))],
        out_specs=pl.BlockSpec((None, None, 8, 128), lambda hh, i: (hh, i, 0, 0)),
        out_shape=jax.ShapeDtypeStruct((h, s // t, 8, 128), F32),
        compiler_params=_params(("parallel", "parallel")),
    )(x)


def _fox_bounds(qf, kb, c, t):
    qn = jnp.sqrt(jnp.max(_fox_norm2(qf, t)[:, :, 0, 0], axis=1))
    kn = jnp.sqrt(jnp.max(_fox_norm2(kb, t)[:, :, 0, 0], axis=1))
    return jnp.concatenate([c[:, ::t], c[:, t - 1::t], (2.0 * qn * kn)[:, None]], axis=1)


SMEM_SPEC = pl.BlockSpec(memory_space=pltpu.SMEM)


def _fox_fwd(qf, kb, vt4, bounds, t):
    h, s, w = qf.shape
    nt = s // t
    sub = FOX_SUB_FWD
    nsub = t // sub
    nh = FOX_HEADS_PER_STEP

    def body(b_ref, q_ref, k_ref, v_ref, o_ref, lse_ref):
        i = pl.program_id(1)
        krow = lax.broadcasted_iota(jnp.int32, (sub, t), 0)
        qcol = lax.broadcasted_iota(jnp.int32, (sub, t), 1)

        def dead_before(hh):
            head = pl.program_id(0) * nh + hh
            top = b_ref[head, 2 * nt] + b_ref[head, i]
            return lax.fori_loop(
                0, i, lambda jj, n: n + (top - b_ref[head, nt + jj] < FOX_DEAD).astype(jnp.int32), 0)

        j_lo = functools.reduce(jnp.minimum, [dead_before(hh) for hh in range(nh)])

        def tile(j, carry, diag):
            out = []
            for hh in range(nh):
                m, acc = carry[hh]
                qv, vj = q_ref[hh], v_ref[hh, j]
                sts = [_dot(k_ref[hh, pl.ds(pl.multiple_of(j * t + b * sub, sub), sub), :], qv, NT)
                       for b in range(nsub)]
                for b in range(nsub):
                    st = sts[b]
                    if diag:
                        st = jnp.where(krow + b * sub <= qcol, st, NEG)
                    m2 = jnp.maximum(m, jnp.max(st, axis=0, keepdims=True))
                    p = jnp.exp(st - m2).astype(BF16)
                    acc = jnp.exp(m - m2) * acc + _dot(vj[:, b * sub:(b + 1) * sub], p)
                    m = m2
                out.append((m, acc))
            return tuple(out)

        init = tuple((jnp.full((1, t), NEG, F32), jnp.zeros((w, t), F32)) for _ in range(nh))
        carry = lax.fori_loop(j_lo, i, lambda j, c: tile(j, c, False), init)
        for hh, (m, acc) in enumerate(tile(i, carry, True)):
            den = acc[B_HD:B_HD + 1, :]
            o_ref[hh] = (acc[0:B_HD, :] / den).astype(o_ref.dtype)
            lse_ref[hh] = m + jnp.log(den)

    return pl.pallas_call(
        body, name="fox_fwd", grid=(h // nh, nt),
        in_specs=[SMEM_SPEC,
                  pl.BlockSpec((nh, t, w), lambda hh, i: (hh, i, 0)),
                  pl.BlockSpec((nh, s, w), lambda hh, i: (hh, 0, 0)),
                  pl.BlockSpec((nh, nt, w, t), lambda hh, i: (hh, 0, 0, 0))],
        out_specs=[pl.BlockSpec((nh, B_HD, t), lambda hh, i: (hh, 0, i)),
                   pl.BlockSpec((nh, 1, t), lambda hh, i: (hh, 0, i))],
        out_shape=[jax.ShapeDtypeStruct((h, B_HD, s), BF16), jax.ShapeDtypeStruct((h, 1, s), F32)],
        compiler_params=_params(("parallel", "parallel")),
    )(bounds, qf, kb, vt4)


def _fox_bwd(qf, dow, lse_row, delta_row, kb, kst4, vb, bounds, t):
    h, s, w = qf.shape
    nt = s // t
    nsub = t // FOX_SUB
    nh = FOX_HEADS_PER_STEP

    def body(b_ref, q_ref, do_ref, lse_ref, dl_ref, k_ref, kt_ref, v_ref, dqt_ref, dk_ref, dv_ref, dk_acc, dv_acc):
        j = pl.program_id(1)

        def alive_after(hh):
            head = pl.program_id(0) * nh + hh
            top = b_ref[head, 2 * nt] - b_ref[head, nt + j]
            return lax.fori_loop(
                j + 1, nt, lambda ii, n: n + (top + b_ref[head, ii] >= FOX_DEAD).astype(jnp.int32), 0)

        i_hi = j + 1 + functools.reduce(jnp.maximum, [alive_after(hh) for hh in range(nh)])

        @pl.when(j == 0)
        def _():
            dqt_ref[...] = jnp.zeros_like(dqt_ref)

        dk_acc[...] = jnp.zeros_like(dk_acc)
        dv_acc[...] = jnp.zeros_like(dv_acc)
        krow = lax.broadcasted_iota(jnp.int32, (FOX_SUB, t), 0)
        qcol = lax.broadcasted_iota(jnp.int32, (FOX_SUB, t), 1)
        subs = [slice(b * FOX_SUB, (b + 1) * FOX_SUB) for b in range(nsub)]

        def tile(i, diag):
            i0 = pl.multiple_of(i * t, t)
            for hh in range(nh):
                qi, doi = q_ref[hh, pl.ds(i0, t), :], do_ref[hh, pl.ds(i0, t), :]
                lse, dl = lse_ref[hh, i], dl_ref[hh, i]
                sts = [_dot(k_ref[hh, rs, :], qi, NT) for rs in subs]
                dps = [_dot(v_ref[hh, rs, :], doi, NT) for rs in subs]
                dq = None
                for b, rs in enumerate(subs):
                    st = sts[b] - lse
                    if diag:
                        st = jnp.where(krow + b * FOX_SUB <= qcol, st, NEG)
                    pt = jnp.exp(st)
                    dsb = (pt * (dps[b] - dl)).astype(BF16)
                    dv_acc[hh, rs, :] += _dot(pt.astype(BF16), doi)
                    dk_acc[hh, rs, :] += _dot(dsb, qi)
                    part = _dot(kt_ref[hh, :, rs], dsb)
                    dq = part if dq is None else dq + part
                dqt_ref[hh, i] += dq

        def step(i, carry):
            tile(i, False)
            return carry

        tile(j, True)
        lax.fori_loop(j + 1, i_hi, step, 0)
        dk_ref[...] = dk_acc[...] * (B_HD ** -0.5)
        dv_ref[...] = dv_acc[...]

    full = pl.BlockSpec((nh, s, w), lambda hh, j: (hh, 0, 0))
    rowst = pl.BlockSpec((nh, nt, 1, t), lambda hh, j: (hh, 0, 0, 0))
    tl = pl.BlockSpec((nh, t, w), lambda hh, j: (hh, j, 0))
    return pl.pallas_call(
        body, name="fox_bwd", grid=(h // nh, nt),
        in_specs=[SMEM_SPEC, full, full, rowst, rowst, tl,
                  pl.BlockSpec((nh, None, w, t), lambda hh, j: (hh, j, 0, 0)), tl],
        out_specs=[pl.BlockSpec((nh, nt, w, t), lambda hh, j: (hh, 0, 0, 0)), tl, tl],
        out_shape=[jax.ShapeDtypeStruct((h, nt, w, t), F32), jax.ShapeDtypeStruct((h, s, w), F32),
                   jax.ShapeDtypeStruct((h, s, w), F32)],
        scratch_shapes=[pltpu.VMEM((nh, t, w), F32), pltpu.VMEM((nh, t, w), F32)],
        compiler_params=_params(("parallel", "arbitrary")),
    )(bounds, qf, dow, lse_row, delta_row, kb, kst4, vb)


def _mem_fwd(u, mkv, tq=512):
    s = u.shape[0]
    scale = HD ** -0.5

    def body(q_ref, mk_ref, mv_ref, o_ref, lse_ref):
        lses = []
        for h in range(4):
            cs = slice(h * HD, (h + 1) * HD)
            sc = _dot(q_ref[:, cs], mk_ref[:, cs], NT) * scale
            m = jnp.max(sc, axis=-1, keepdims=True)
            p = jnp.exp(sc - m)
            den = jnp.sum(p, axis=-1, keepdims=True)
            o_ref[:, cs] = (_dot(p.astype(BF16), mv_ref[:, cs]) / den).astype(o_ref.dtype)
            lses.append(m + jnp.log(den))
        lse_ref[...] = _lane_pack(lses, (tq, HD))

    return pl.pallas_call(
        body, name="mem_fwd", grid=(s // tq,),
        in_specs=[pl.BlockSpec((tq, 512), lambda i: (i, C_QM // 512)),
                  pl.BlockSpec((N_MEM, 512), lambda i: (0, 0)),
                  pl.BlockSpec((N_MEM, 512), lambda i: (0, 1))],
        out_specs=[pl.BlockSpec((tq, 512), lambda i: (i, 0)), pl.BlockSpec((tq, HD), lambda i: (i, 0))],
        out_shape=[jax.ShapeDtypeStruct((s, 512), BF16), jax.ShapeDtypeStruct((s, HD), F32)],
        compiler_params=_params(("parallel",)),
    )(u, mkv, mkv)


def _mem_bwd(u, mkv, o, do, lse, tq=512):
    s = u.shape[0]
    scale = HD ** -0.5

    def body(q_ref, mk_ref, mv_ref, o_ref, do_ref, lse_ref, dq_ref, dmk_ref, dmv_ref):
        @pl.when(pl.program_id(0) == 0)
        def _():
            dmk_ref[...] = jnp.zeros_like(dmk_ref)
            dmv_ref[...] = jnp.zeros_like(dmv_ref)

        for h in range(4):
            cs = slice(h * HD, (h + 1) * HD)
            qv, dov = q_ref[:, cs], do_ref[:, cs]
            sc = _dot(qv, mk_ref[:, cs], NT) * scale
            p = jnp.exp(sc - lse_ref[:, h:h + 1])
            delta = jnp.sum(dov.astype(F32) * o_ref[:, cs].astype(F32), axis=-1, keepdims=True)
            ds = p * (_dot(dov, mv_ref[:, cs], NT) - delta)
            dsb = ds.astype(BF16)
            dq_ref[:, cs] = (_dot(dsb, mk_ref[:, cs]) * scale).astype(dq_ref.dtype)
            dmk_ref[:, cs] += _dot(dsb, qv, TN) * scale
            dmv_ref[:, cs] += _dot(p.astype(BF16), dov, TN)

    row = pl.BlockSpec((tq, 512), lambda i: (i, 0))
    acc = pl.BlockSpec((N_MEM, 512), lambda i: (0, 0))
    return pl.pallas_call(
        body, name="mem_bwd", grid=(s // tq,),
        in_specs=[pl.BlockSpec((tq, 512), lambda i: (i, C_QM // 512)),
                  pl.BlockSpec((N_MEM, 512), lambda i: (0, 0)),
                  pl.BlockSpec((N_MEM, 512), lambda i: (0, 1)),
                  row, row, pl.BlockSpec((tq, HD), lambda i: (i, 0))],
        out_specs=[row, acc, acc],
        out_shape=[jax.ShapeDtypeStruct((s, 512), BF16), jax.ShapeDtypeStruct((N_MEM, 512), F32),
                   jax.ShapeDtypeStruct((N_MEM, 512), F32)],
        compiler_params=_params(("arbitrary",)),
    )(u, mkv, mkv, o, do, lse)


def _heads_major(a, col0):
    s = a.shape[0]
    return a[:, col0:col0 + 512].reshape(s, B_HEADS, B_HD).transpose(1, 0, 2)


def _token_major(a):
    h, s, dh = a.shape
    return a.transpose(1, 0, 2).reshape(s, h * dh)


def _class_view(a, d):
    s, c = a.shape
    return a.reshape(s // d, d * c)


def _local_step(x, mem, pos, target, g_pre, g_post, g_mem, w_main, w_fb, b_forget, b_merge,
                w_mem_kv, w_ba, w_bb, w_bm, w_out):
    s = x.shape[0]
    t_fox = min(512, s)
    nt = s // t_fox
    half = ROT_DIM // 2
    inv = ROPE_THETA ** (-jnp.arange(half, dtype=F32) / half)
    inv128 = jnp.concatenate([inv, inv, jnp.zeros((HD - ROT_DIM,), F32)]).reshape(1, HD)

    h = _rms_fwd("norm_pre", x, g_pre)
    u = _mm("proj_in", h, w_main, "nn", BF16)
    ufb = _mm("proj_fb", h, w_fb, "nn", F32)
    memn = _rms_fwd("norm_mem", mem, g_mem)
    mkv = _mm("proj_mem", memn, w_mem_kv, "nn", BF16)

    qkv = _rope_fwd(u, pos, inv128)
    views = [tuple(qkv[3 * g:3 * g + 3]) for g in range(3)]
    os_, lses = [], []
    for g, d in enumerate(DILATIONS):
        o_g, lse_g = _band_fwd("band_fwd%d" % g, *views[g], d)
        os_.append((o_g, d * A_GROUP, 0, d))
        lses.append((lse_g, d * HD, 0, d))

    def merge_a(o1, o2, o3, l1, l2, l3, za, *scr):
        o1, o2, o3 = [_from_class(o, scr, d) for o, d in zip((o1, o2, o3), DILATIONS)]
        l1, l2, l3 = [_from_class(lv, scr, d) for lv, d in zip((l1, l2, l3), DILATIONS)]
        ys, tots = [], []
        for hh in range(4):
            cs, hs = slice(hh * HD, (hh + 1) * HD), slice(hh, hh + 1)
            mx = jnp.maximum(jnp.maximum(l1[:, hs], l2[:, hs]), l3[:, hs])
            e1, e2, e3 = jnp.exp(l1[:, hs] - mx), jnp.exp(l2[:, hs] - mx), jnp.exp(l3[:, hs] - mx)
            den = e1 + e2 + e3
            ys.append((e1 * o1[:, cs] + e2 * o2[:, cs] + e3 * o3[:, cs]) / den)
            tots.append(mx + jnp.log(den))
        y = jnp.concatenate(ys, axis=1)
        zf = za.astype(F32)
        tot = _lane_pack(tots, l1.shape)
        return (y, y * (zf * _sig(zf))) + tuple(_to_class(tot, scr, d) for d in DILATIONS)

    res = _rows("merge_a", merge_a, os_ + lses + [(u, 512, C_ZA // 512)], [],
                [(512, BF16), (512, BF16)] + [(d * HD, F32, d) for d in DILATIONS], tm=ROPE_TM,
                scratch=_class_scratch(ROPE_TM))
    y_a, yg_a, lse_a = res[0], res[1], res[2:5]

    zrow = ufb[:, :B_HEADS].T
    c = _fox_prep(zrow, b_forget.reshape(B_HEADS, 1))
    n_hs = B_HEADS * s

    def wide(a, fill):
        return jnp.pad(a, ((0, 0), (0, 0), (0, FOX_W - B_HD)), constant_values=fill)

    def tiles_t(a):
        return a.reshape(B_HEADS, nt, t_fox, FOX_W).transpose(0, 1, 3, 2)

    qf = wide(_heads_major(u, C_QB), B_HD ** 0.5)
    vb = wide(_heads_major(u, C_VB), 1.0)
    kb, ks = _fox_aug_k(wide(_heads_major(u, C_KB), 0.0).reshape(n_hs, FOX_W), c.reshape(n_hs, 1))
    kb, ks = kb.reshape(B_HEADS, s, FOX_W), ks.reshape(B_HEADS, s, FOX_W)
    bounds = _fox_bounds(qf, kb, c, t_fox)
    ot, lse_b = _fox_fwd(qf, kb, tiles_t(vb), bounds, t_fox)
    y_b = ot.transpose(2, 0, 1).reshape(s, B_HEADS * B_HD)

    y_m, lse_m = _mem_fwd(u, mkv)

    def gate(y, z):
        zf = z.astype(F32)
        return (y.astype(F32) * (zf * _sig(zf)),)

    yg_b = _rows("gate_b", gate, [y_b, (u, 512, C_ZB // 512)], [], [(512, BF16)])[0]
    yg_m = _rows("gate_m", gate, [y_m, (u, 512, C_ZM // 512)], [], [(512, BF16)])[0]

    br_a = _mm("branch_a", yg_a, w_ba, "nn", BF16)
    br_b = _mm("branch_b", yg_b, w_bb, "nn", BF16)
    br_m = _mm("branch_m", yg_m, w_bm, "nn", BF16)
    gl = [(u, 1024, C_GL // 1024 + i) for i in range(3)]
    bm3 = b_merge.reshape(3, D_MODEL)

    def merge(g0, g1, g2, b0, b1, b2, bm):
        tot = 0.0
        for i, (gv, bv) in enumerate(((g0, b0), (g1, b1), (g2, b2))):
            tot = tot + _sig(gv.astype(F32) + bm[i:i + 1, :]) * bv.astype(F32)
        return (tot,)

    merged = _rows("merge_gates", merge, gl + [br_a, br_b, br_m], [bm3], [(D_MODEL, BF16)])[0]
    out = _mm("proj_out", merged, w_out, "nn", F32)

    def tail(xv, ov, tv, gv):
        r = lax.rsqrt(jnp.mean(ov * ov, axis=-1, keepdims=True) + EPS)
        n = ov * r
        err = xv + n * gv - tv
        dy = err * (1.0 / D_MODEL)
        dn = dy * gv
        dout = r * (dn - n * jnp.mean(dn * n, axis=-1, keepdims=True))
        return (dy, dout, jnp.sum(0.5 * err * err * (1.0 / D_MODEL), axis=0, keepdims=True),
                jnp.sum(dy * n, axis=0, keepdims=True))

    dy, dout, loss_lanes, g_post_grad = _rows(
        "tail", tail, [x, out, target], [g_post], [(D_MODEL, F32), (D_MODEL, BF16)],
        reds=[D_MODEL, D_MODEL], tm=256)

    dmerged = _mm("d_merged", dout, w_out, "nt", BF16)
    gw_out = _mm("g_w_out", merged, dout, "tn", F32)

    def merge_bwd(dm, g0, g1, g2, b0, b1, b2, bm):
        dmf = dm.astype(F32)
        dbs, dgs, sums = [], [], []
        for i, (gv, bv) in enumerate(((g0, b0), (g1, b1), (g2, b2))):
            sg = _sig(gv.astype(F32) + bm[i:i + 1, :])
            dbs.append(dmf * sg)
            dg = dmf * bv.astype(F32) * sg * (1.0 - sg)
            dgs.append(dg)
            sums.append(jnp.sum(dg, axis=0, keepdims=True))
        return tuple(dbs + dgs + sums)

    res = _rows("merge_bwd", merge_bwd, [dmerged] + gl + [br_a, br_b, br_m], [bm3],
                [(D_MODEL, BF16)] * 6, reds=[D_MODEL] * 3, tm=256)
    dbr, dgl, g_bmerge = res[0:3], res[3:6], jnp.concatenate(res[6:9], axis=1)

    dyg, gw_branch = [], []
    for nm, dbv, wv, ygv in (("a", dbr[0], w_ba, yg_a), ("b", dbr[1], w_bb, yg_b), ("m", dbr[2], w_bm, yg_m)):
        dyg.append(_mm("d_yg_" + nm, dbv, wv, "nt", BF16))
        gw_branch.append(_mm("g_w_branch_" + nm, ygv, dbv, "tn", F32))

    def gate_bwd(dg, y, z):
        dgf, yf, zf = dg.astype(F32), y.astype(F32), z.astype(F32)
        sg = _sig(zf)
        return dgf * (zf * sg), dgf * yf * (sg * (1.0 + zf * (1.0 - sg)))

    def gate_bwd_a(dg, y, z, *scr):
        dyv, dz = gate_bwd(dg, y, z)
        prod = dyv * y.astype(F32)
        dl = [jnp.sum(prod[:, hh * HD:(hh + 1) * HD], axis=-1, keepdims=True) for hh in range(4)]
        delta = _lane_pack(dl, (dg.shape[0], HD))
        return ((dz,) + tuple(_to_class(dyv, scr, d) for d in DILATIONS)
                + tuple(_to_class(delta, scr, d) for d in DILATIONS))

    res = _rows("gate_bwd_a", gate_bwd_a, [dyg[0], y_a, (u, 512, C_ZA // 512)], [],
                [(512, BF16)] + [(d * A_GROUP, BF16, d) for d in DILATIONS] + [(d * HD, F32, d) for d in DILATIONS],
                tm=ROPE_TM, scratch=_class_scratch(ROPE_TM))
    dz_a, dy_a, delta_a = res[0], res[1:4], res[4:7]
    dy_b, dz_b = _rows("gate_bwd_b", gate_bwd, [dyg[1], y_b, (u, 512, C_ZB // 512)], [],
                       [(512, BF16), (512, BF16)])
    dy_m, dz_m = _rows("gate_bwd_m", gate_bwd, [dyg[2], y_m, (u, 512, C_ZM // 512)], [],
                       [(512, BF16), (512, BF16)])

    dq_m, dmk, dmv = _mem_bwd(u, mkv, y_m, dy_m, lse_m)
    dmkv = jnp.concatenate([dmk, dmv], axis=1)
    gw_mem_kv = _mm("g_w_mem_kv", memn, dmkv, "tn", F32)
    dmemn = _mm("d_memn", dmkv, w_mem_kv, "nt", F32)

    def mem_gain_grad(mv, dv):
        r = lax.rsqrt(jnp.mean(mv * mv, axis=-1, keepdims=True) + EPS)
        return (jnp.sum(dv * mv * r, axis=0, keepdims=True),)

    g_mem_grad = _rows("g_norm_mem", mem_gain_grad, [mem, dmemn], [], [], reds=[D_MODEL], tm=N_MEM)[0]

    dob = _heads_major(dy_b, 0)

    def fox_delta(a, b):
        return (jnp.sum(a.astype(F32) * b.astype(F32), axis=-1, keepdims=True),)

    ob = ot.transpose(0, 2, 1).reshape(n_hs, B_HD)
    delta_b = _rows("fox_delta", fox_delta, [dob.reshape(n_hs, B_HD), ob], [], [(1, F32)], tm=min(2048, s))[0]
    dqt, dkw, dvw = _fox_bwd(qf, wide(dob, 0.0), lse_b.reshape(B_HEADS, nt, 1, t_fox),
                             delta_b.reshape(B_HEADS, nt, 1, t_fox), kb, tiles_t(ks), vb, bounds, t_fox)
    dqb = dqt[:, :, :B_HD, :].transpose(0, 1, 3, 2).reshape(B_HEADS, s, B_HD)
    dkb, dvb = dkw[:, :, :B_HD].astype(BF16), dvw[:, :, :B_HD].astype(BF16)
    dzrow, g_bforget = _fox_prep_bwd(dkw[:, :, B_HD], dqt[:, :, FOX_ONE, :].reshape(B_HEADS, s), zrow,
                                     b_forget.reshape(B_HEADS, 1))
    dfb = jnp.zeros((s, HD), BF16).at[:, :B_HEADS].set(dzrow.T.astype(BF16))

    dqs, dks, dvs = [], [], []
    for g, d in enumerate(DILATIONS):
        qv, kv, vv = views[g]
        dqs.append(_band_dq("band_dq%d" % g, qv, kv, vv, dy_a[g], lse_a[g], delta_a[g], d))
        dk_g, dv_g = _band_dkv("band_dkv%d" % g, qv, kv, vv, dy_a[g], lse_a[g], delta_a[g], d)
        dks.append(dk_g)
        dvs.append(dv_g)
    dqa, dka, dva = _rope_bwd(dqs, dks, dvs, pos, inv128)

    du = jnp.concatenate(
        [dqa, dka, dva, dz_a, _token_major(dqb).astype(BF16), _token_major(dkb), _token_major(dvb),
                            dz_b, dq_m, dz_m] + list(dgl), axis=1)

    gw_main = _mm("g_w_main", h.T, du, "nn", F32, tk=2048)
    gw_fb = _mm("g_w_fb", h, dfb, "tn", F32)
    dh_main = _mm("d_h", du, w_main, "nt", F32, tk=2816)
    dh_fb = _mm("d_h_fb", dfb, w_fb, "nt", F32)

    def pre_bwd(xv, d1, d2, dyv, gv):
        r = lax.rsqrt(jnp.mean(xv * xv, axis=-1, keepdims=True) + EPS)
        n = xv * r
        dhv = d1 + d2
        dn = dhv * gv
        dx = r * (dn - n * jnp.mean(dn * n, axis=-1, keepdims=True))
        return dyv + dx, jnp.sum(dhv * n, axis=0, keepdims=True)

    grad_x, g_pre_grad = _rows("norm_pre_bwd", pre_bwd, [x, dh_main, dh_fb, dy], [g_pre],
                               [(D_MODEL, F32)], reds=[D_MODEL], tm=256)

    gw_in = jnp.concatenate([gw_main[:, :FB_ORIG], gw_fb[:, :B_HEADS], gw_main[:, FB_ORIG:]], axis=1)
    grads = dict(norm_pre_g=g_pre_grad, norm_post_g=g_post_grad, norm_mem_g=g_mem_grad, w_in=gw_in,
                 b_forget=g_bforget.reshape(1, B_HEADS), b_merge=g_bmerge, w_mem_kv=gw_mem_kv,
                 w_branch_a=gw_branch[0], w_branch_b=gw_branch[1], w_branch_m=gw_branch[2], w_out=gw_out)
    return loss_lanes, grad_x, grads


HBM_SPEC = pl.BlockSpec(memory_space=pltpu.HBM)


def _place():
    x, y, c = lax.axis_index("x"), lax.axis_index("y"), lax.axis_index("c")
    chips = [(1 - x, y), (x, 1 - y), (1 - x, 1 - y)]
    return x, y, c, 2 * x + y, chips


N_CHUNKS = 4


def _units(parts, row_axis):
    units = []
    for i, a in enumerate(parts):
        ch = a.shape[row_axis] // N_CHUNKS
        units += [(i, pl.ds(k * ch, ch)) for k in range(N_CHUNKS)]
    return units


def _gather_weights(parts):
    n = len(parts)
    units = _units(parts, 1)
    nu = len(units)

    def body(*refs):
        srcs, outs = refs[:n], refs[n:2 * n]
        send_sems, recv_sems = refs[2 * n:]
        x, y, c, p, chips = _place()
        me, sib = (x, y, c), (x, y, 1 - c)

        def cp(u, k, chip, half, to, from_src=False):
            i, rs = units[u]
            dst = outs[i].at[chip, half, rs]
            return pltpu.make_async_remote_copy(
                src_ref=srcs[i].at[half, rs] if from_src else dst, dst_ref=dst, send_sem=send_sems.at[u, k],
                recv_sem=recv_sems.at[u, k], device_id=to, device_id_type=MESH)

        first = [cp(u, j, p, c, (cx, cy, c), from_src=True)
                 for u in range(nu) for j, (cx, cy) in enumerate(chips)]
        for f in first:
            f.start()
        passed = []
        for u in range(nu):
            for j, (cx, cy) in enumerate(chips):
                cp(u, j, 2 * cx + cy, c, me).wait_recv()
                fw = cp(u, 3 + j, 2 * cx + cy, c, sib)
                fw.start()
                passed.append(fw)
        for u in range(nu):
            for j, (cx, cy) in enumerate(chips):
                cp(u, 3 + j, 2 * cx + cy, 1 - c, me).wait_recv()
        for f in first + passed:
            f.wait_send()

    return pl.pallas_call(
        body, name="gather_weights", in_specs=[HBM_SPEC] * n, out_specs=[HBM_SPEC] * n,
        out_shape=[jax.ShapeDtypeStruct((N_CHIPS,) + a.shape, a.dtype) for a in parts],
        scratch_shapes=[pltpu.SemaphoreType.DMA((nu, 6)), pltpu.SemaphoreType.DMA((nu, 6))],
    )(*parts)


def _swap_with_sibling(parts):
    n = len(parts)
    units = _units(parts, 1)

    def body(*refs):
        srcs, outs = refs[:n], refs[n:2 * n]
        send_sems, recv_sems = refs[2 * n:]
        x, y, c, _, _ = _place()
        cps = [pltpu.make_async_remote_copy(
            src_ref=srcs[i].at[q, rs], dst_ref=outs[i].at[q, rs], send_sem=send_sems.at[u, q],
            recv_sem=recv_sems.at[u, q], device_id=(x, y, 1 - c), device_id_type=MESH)
            for q in range(N_CHIPS) for u, (i, rs) in enumerate(units)]
        for cpy in cps:
            cpy.start()
        for cpy in cps:
            cpy.wait()

    return pl.pallas_call(
        body, name="swap_with_sibling", in_specs=[HBM_SPEC] * n, out_specs=[HBM_SPEC] * n,
        out_shape=[jax.ShapeDtypeStruct(a.shape, a.dtype) for a in parts],
        scratch_shapes=[pltpu.SemaphoreType.DMA((len(units), N_CHIPS)),
                        pltpu.SemaphoreType.DMA((len(units), N_CHIPS))],
    )(*parts)


def _scatter_to_owners(parts):
    n = len(parts)
    units = _units(parts, 1)

    def body(*refs):
        srcs, outs = refs[:n], refs[n:2 * n]
        send_sems, recv_sems = refs[2 * n:]
        x, y, c, p, chips = _place()
        sends = []
        for u, (i, rs) in enumerate(units):
            for j, (cx, cy) in enumerate(chips):
                cpy = pltpu.make_async_remote_copy(
                    src_ref=srcs[i].at[2 * cx + cy, rs], dst_ref=outs[i].at[p, rs], send_sem=send_sems.at[u, j],
                    recv_sem=recv_sems.at[u, j], device_id=(cx, cy, c), device_id_type=MESH)
                cpy.start()
                sends.append(cpy)
        for u, (i, rs) in enumerate(units):
            for j, (cx, cy) in enumerate(chips):
                pltpu.make_async_remote_copy(
                    src_ref=srcs[i].at[2 * cx + cy, rs], dst_ref=outs[i].at[2 * cx + cy, rs],
                    send_sem=send_sems.at[u, j], recv_sem=recv_sems.at[u, j],
                    device_id=(cx, cy, c), device_id_type=MESH).wait_recv()
        for cpy in sends:
            cpy.wait_send()

    return pl.pallas_call(
        body, name="scatter_to_owners", in_specs=[HBM_SPEC] * n, out_specs=[HBM_SPEC] * n,
        out_shape=[jax.ShapeDtypeStruct(a.shape, a.dtype) for a in parts],
        scratch_shapes=[pltpu.SemaphoreType.DMA((len(units), 3)), pltpu.SemaphoreType.DMA((len(units), 3))],
    )(*parts)


def _share_with_sibling(parts):
    n = len(parts)
    units = _units(parts, 1)

    def body(*refs):
        srcs, outs = refs[:n], refs[n:2 * n]
        send_sems, recv_sems = refs[2 * n:]
        x, y, c, _, _ = _place()
        sends = [pltpu.make_async_remote_copy(
            src_ref=srcs[i].at[0, rs], dst_ref=outs[i].at[c, rs], send_sem=send_sems.at[u],
            recv_sem=recv_sems.at[u], device_id=(x, y, 1 - c), device_id_type=MESH)
            for u, (i, rs) in enumerate(units)]
        for cpy in sends:
            cpy.start()
        for u, (i, rs) in enumerate(units):
            pltpu.make_async_remote_copy(
                src_ref=srcs[i].at[0, rs], dst_ref=outs[i].at[1 - c, rs], send_sem=send_sems.at[u],
                recv_sem=recv_sems.at[u], device_id=(x, y, 1 - c), device_id_type=MESH).wait_recv()
        for cpy in sends:
            cpy.wait_send()

    return pl.pallas_call(
        body, name="share_with_sibling", in_specs=[HBM_SPEC] * n, out_specs=[HBM_SPEC] * n,
        out_shape=[jax.ShapeDtypeStruct((2,) + a.shape[1:], a.dtype) for a in parts],
        scratch_shapes=[pltpu.SemaphoreType.DMA((len(units),)), pltpu.SemaphoreType.DMA((len(units),))],
    )(*parts)


def _sum_small(v):
    def body(v_ref, out_ref, buf, send_sems, recv_sems):
        x, y, c, _, _ = _place()
        me = 4 * x + 2 * y + c
        buf[me] = v_ref[...]
        flips = [(dx, dy, dc) for dx in (0, 1) for dy in (0, 1) for dc in (0, 1)][1:]
        sends = []
        for k, (dx, dy, dc) in enumerate(flips):
            cpy = pltpu.make_async_remote_copy(
                src_ref=v_ref, dst_ref=buf.at[me], send_sem=send_sems.at[k], recv_sem=recv_sems.at[k],
                device_id=((x + dx) % 2, (y + dy) % 2, (c + dc) % 2), device_id_type=MESH)
            cpy.start()
            sends.append(cpy)
        for k, (dx, dy, dc) in enumerate(flips):
            px, py, pc = (x + dx) % 2, (y + dy) % 2, (c + dc) % 2
            pltpu.make_async_remote_copy(
                src_ref=v_ref, dst_ref=buf.at[4 * px + 2 * py + pc], send_sem=send_sems.at[k],
                recv_sem=recv_sems.at[k], device_id=(px, py, pc), device_id_type=MESH).wait_recv()
        for cpy in sends:
            cpy.wait_send()
        tot = buf[0]
        for i in range(1, N_DEV):
            tot = tot + buf[i]
        out_ref[...] = tot

    return pl.pallas_call(
        body, name="sum_small", out_shape=jax.ShapeDtypeStruct(v.shape, v.dtype),
        in_specs=[pl.BlockSpec(memory_space=pltpu.VMEM)], out_specs=pl.BlockSpec(memory_space=pltpu.VMEM),
        scratch_shapes=[pltpu.VMEM((N_DEV,) + v.shape, v.dtype), pltpu.SemaphoreType.DMA((N_DEV - 1,)),
                        pltpu.SemaphoreType.DMA((N_DEV - 1,))],
    )(v)


def _add_slabs(name, terms, out_dtype):
    arr0 = terms[0][0]
    n = arr0.shape[0] if terms[0][1] is None else 1
    _, r, w = arr0.shape
    tr = 64
    specs = []
    for _, slab in terms:
        if slab is None:
            specs.append(pl.BlockSpec((None, tr, w), lambda i, j: (i, j, 0)))
        else:
            specs.append(pl.BlockSpec((None, tr, w), functools.partial(lambda i, j, sl: (sl, j, 0), sl=slab)))

    def body(*refs):
        tot = refs[0][...].astype(F32)
        for rf in refs[1:-1]:
            tot = tot + rf[...].astype(F32)
        refs[-1][...] = tot.astype(out_dtype)

    return pl.pallas_call(
        body, name=name, grid=(n, r // tr), in_specs=specs,
        out_specs=pl.BlockSpec((None, tr, w), lambda i, j: (i, j, 0)),
        out_shape=jax.ShapeDtypeStruct((n, r, w), out_dtype),
        compiler_params=_params(("parallel", "parallel")),
    )(*[a for a, _ in terms])


def _adamw(name, w, g, m, v, tm):
    def fn(wv, gv, mv, vv):
        m2 = ADAM_B1 * mv + (1.0 - ADAM_B1) * gv
        v2 = ADAM_B2 * vv + (1.0 - ADAM_B2) * (gv * gv)
        m_hat = m2 / (1.0 - ADAM_B1 ** ADAM_STEP)
        v_hat = v2 / (1.0 - ADAM_B2 ** ADAM_STEP)
        return -ADAM_LR * (m_hat / (jnp.sqrt(v_hat) + ADAM_EPS) + ADAM_WD * wv), m2, v2
    c = w.shape[1]
    return _rows(name, fn, [w, g, m, v], [], [(c, F32)] * 3, tm=tm)


REST_ROWS = 256 + 3 * 128 + 256
REST_SPLITS = (("w_mem_kv", 0, 256), ("w_branch_a", 256, 128), ("w_branch_b", 384, 128),
               ("w_branch_m", 512, 128), ("w_out", 640, 256))


def _rest_pack(t):
    return jnp.concatenate([t[n].reshape(rows, D_MODEL) for n, _, rows in REST_SPLITS], axis=0)


def _rest_unpack(a, shapes):
    return {n: a[r0:r0 + rows].reshape(shapes[n]) for n, r0, rows in REST_SPLITS}


def _small_pack(pre, post, memg, bforget, bmerge):
    pad = jnp.zeros((1, D_MODEL - B_HEADS), F32)
    return jnp.concatenate([pre, post, memg, bmerge.reshape(3, D_MODEL),
                            jnp.concatenate([bforget, pad], axis=1), jnp.zeros((1, D_MODEL), F32)], axis=0)


def _small_unpack(s8):
    return dict(norm_pre_g=s8[0:1], norm_post_g=s8[1:2], norm_mem_g=s8[2:3],
                b_merge=s8[3:6].reshape(1, 3 * D_MODEL), b_forget=s8[6:7, :B_HEADS])


WEIGHTS = ("norm_pre_g", "norm_post_g", "norm_mem_g", "w_in", "b_forget", "b_merge", "w_mem_kv",
           "w_branch_a", "w_branch_b", "w_branch_m", "w_out")
SMALL = ("norm_pre_g", "norm_post_g", "norm_mem_g", "b_forget", "b_merge")


def kernel(x, mem, positions, norm_pre_g, norm_post_g, norm_mem_g, w_in, b_forget, b_merge, w_mem_kv, w_branch_a, w_branch_b, w_branch_m, w_out, loss_target, m_norm_pre_g, m_norm_post_g, m_norm_mem_g, m_w_in, m_b_forget, m_b_merge, m_w_mem_kv, m_w_branch_a, m_w_branch_b, m_w_branch_m, m_w_out, v_norm_pre_g, v_norm_post_g, v_norm_mem_g, v_w_in, v_b_forget, v_b_merge, v_w_mem_kv, v_w_branch_a, v_w_branch_b, v_w_branch_m, v_w_out):
    w = dict(norm_pre_g=norm_pre_g, norm_post_g=norm_post_g, norm_mem_g=norm_mem_g, w_in=w_in[0],
             b_forget=b_forget, b_merge=b_merge, w_mem_kv=w_mem_kv[0], w_branch_a=w_branch_a[0],
             w_branch_b=w_branch_b[0], w_branch_m=w_branch_m[0], w_out=w_out[0])
    mo = dict(norm_pre_g=m_norm_pre_g, norm_post_g=m_norm_post_g, norm_mem_g=m_norm_mem_g, w_in=m_w_in[0],
              b_forget=m_b_forget, b_merge=m_b_merge, w_mem_kv=m_w_mem_kv[0], w_branch_a=m_w_branch_a[0],
              w_branch_b=m_w_branch_b[0], w_branch_m=m_w_branch_m[0], w_out=m_w_out[0])
    vo = dict(norm_pre_g=v_norm_pre_g, norm_post_g=v_norm_post_g, norm_mem_g=v_norm_mem_g, w_in=v_w_in[0],
              b_forget=v_b_forget, b_merge=v_b_merge, w_mem_kv=v_w_mem_kv[0], w_branch_a=v_w_branch_a[0],
              w_branch_b=v_w_branch_b[0], w_branch_m=v_w_branch_m[0], w_out=v_w_out[0])
    s = x.shape[1]
    c = lax.axis_index("c")

    chip = 2 * lax.axis_index("x") + lax.axis_index("y")

    def put(whole, own, slot):
        return lax.dynamic_update_index_in_dim(whole, own.astype(whole.dtype), slot, 0)

    own_w = [w["w_in"].astype(BF16).reshape(2, D_MODEL // 2, SHARD_COLS),
             _rest_pack(w).astype(BF16).reshape(2, REST_ROWS // 2, D_MODEL)]
    all_in, all_rest = [put(a, o, chip) for a, o in zip(_gather_weights(own_w), own_w)]
    all_in = all_in.reshape(N_CHIPS, D_MODEL, SHARD_COLS)
    w_in_f = jnp.concatenate([all_in[p] for p in range(N_CHIPS)], axis=1)
    all_rest = all_rest.reshape(N_CHIPS, REST_ROWS, D_MODEL)
    w_kv_f = all_rest[:, 0:256].reshape(D_MODEL, D_MODEL)
    w_br_f = [all_rest[:, 256 + 128 * i:384 + 128 * i].reshape(N_CHIPS, 512, 256).transpose(1, 0, 2)
              .reshape(512, D_MODEL) for i in range(3)]
    w_out_f = all_rest[:, 640:896].reshape(D_MODEL, D_MODEL)
    w_main = jnp.concatenate([w_in_f[:, :FB_ORIG], w_in_f[:, FB_ORIG + B_HEADS:]], axis=1)
    w_fb = jnp.concatenate([w_in_f[:, FB_ORIG:FB_ORIG + B_HEADS], jnp.zeros((D_MODEL, HD - B_HEADS), BF16)], axis=1)

    loss_lanes, grad_x, g = _local_step(
        x[0], mem[0], positions.reshape(s, 1), loss_target[0], norm_pre_g, norm_post_g, norm_mem_g,
        w_main, w_fb, b_forget, b_merge, w_kv_f, w_br_f[0], w_br_f[1], w_br_f[2], w_out_f)
    loss = lax.psum(jnp.sum(loss_lanes), ("x", "y", "c"))

    def per_chip(name, p):
        a = g[name]
        if name in ("w_mem_kv", "w_out"):
            return a[256 * p:256 * (p + 1)]
        return a[:, 256 * p:256 * (p + 1)]

    in4 = jnp.stack([g["w_in"][:, SHARD_COLS * p:SHARD_COLS * (p + 1)] for p in range(N_CHIPS)])
    rest4 = jnp.stack([_rest_pack({n: per_chip(n, p) for n, _, _ in REST_SPLITS}) for p in range(N_CHIPS)])
    halves = [in4.reshape(N_CHIPS, 2, D_MODEL // 2, SHARD_COLS),
              rest4.reshape(N_CHIPS, 2, REST_ROWS // 2, D_MODEL)]
    mine = [lax.dynamic_index_in_dim(a, c, axis=1, keepdims=False) for a in halves]
    theirs = [lax.dynamic_index_in_dim(a, 1 - c, axis=1, keepdims=False) for a in halves]
    got = _swap_with_sibling(theirs)
    pair = [_add_slabs("add_pair_%d" % i, [(mine[i], None), (got[i], None)], BF16) for i in range(2)]
    landed = [put(a, lax.dynamic_index_in_dim(o, chip, 0, keepdims=False), chip)
              for a, o in zip(_scatter_to_owners(pair), pair)]
    half = [_add_slabs("add_chips_%d" % i, [(landed[i], q) for q in range(N_CHIPS)], F32) for i in range(2)]
    red_in, red_rest = [put(a, o[0], c) for a, o in zip(_share_with_sibling(half), half)]
    gs = {"w_in": red_in.reshape(D_MODEL, SHARD_COLS)}
    gs.update(_rest_unpack(red_rest.reshape(REST_ROWS, D_MODEL), {n: w[n].shape for n, _, _ in REST_SPLITS}))
    gs.update(_small_unpack(_sum_small(_small_pack(
        g["norm_pre_g"], g["norm_post_g"], g["norm_mem_g"], g["b_forget"], g["b_merge"]))))

    delta, new_m, new_v = {}, {}, {}
    for n, tm in (("w_in", 128), ("w_mem_kv", 256), ("w_branch_a", 512), ("w_branch_b", 512),
                  ("w_branch_m", 512), ("w_out", 256)):
        d_, m_, v_ = _adamw("adamw_" + n, w[n], gs[n], mo[n], vo[n], tm)
        delta[n], new_m[n], new_v[n] = d_[None], m_[None], v_[None]
        gs[n] = gs[n][None]
    packs = [_small_pack(*[t[n] for n in ("norm_pre_g", "norm_post_g", "norm_mem_g", "b_forget", "b_merge")])
             for t in (w, gs, mo, vo)]
    for res, store in zip(_adamw("adamw_small", *packs, 8), (delta, new_m, new_v)):
        store.update(_small_unpack(res))

    return (loss, grad_x[None], *[gs[n] for n in WEIGHTS], *[delta[n] for n in WEIGHTS],
            *[new_m[n] for n in WEIGHTS], *[new_v[n] for n in WEIGHTS])
```

```python
import functools

import jax
import jax.numpy as jnp
from jax import lax
from jax.experimental import pallas as pl
from jax.experimental.pallas import tpu as pltpu

F32 = jnp.float32
BF16 = jnp.bfloat16
MESH = pl.DeviceIdType.MESH

D_MODEL = 1024
N_MEM = 256
EPS = 1e-6
NEG = -1e30
ROPE_THETA = 500000.0
ROT_DIM = 32
HD = 128
A_GROUP = 512
DILATIONS = (1, 4, 16)
BAND = 128
B_HEADS = 8
B_HD = 64
N_CHIPS = 4
N_DEV = 8

C_QA, C_KA, C_VA, C_ZA = 0, 1536, 3072, 4608
C_QB, C_KB, C_VB, C_ZB = 5120, 5632, 6144, 6656
C_QM, C_ZM, C_GL = 7168, 7680, 8192
N_MAIN = 11264
FB_ORIG = 6656
IN_COLS = 11272
SHARD_COLS = IN_COLS // N_CHIPS

ADAM_LR, ADAM_B1, ADAM_B2, ADAM_EPS, ADAM_WD, ADAM_STEP = 0.001, 0.9, 0.999, 1e-08, 0.01, 10

VMEM_LIMIT_V7X = 56 * 1024 * 1024

NT = (((1,), (1,)), ((), ()))
NN = (((1,), (0,)), ((), ()))
TN = (((0,), (0,)), ((), ()))


def _params(sem):
    return pltpu.CompilerParams(dimension_semantics=sem, vmem_limit_bytes=VMEM_LIMIT_V7X)


def _dot(a, b, dn=NN):
    return lax.dot_general(a, b, dn, preferred_element_type=F32)


def _sig(z):
    return 1.0 / (1.0 + jnp.exp(-z))


def _rows(name, fn, row_ins, bc_ins, outs, reds=(), tm=512, scratch=()):
    arrs, specs = [], []
    s = None
    for r in row_ins:
        arr, w, cb, d = (tuple(r) + (1,))[:4] if isinstance(r, tuple) else (r, r.shape[1], 0, 1)
        s = arr.shape[0] * d if s is None else s
        arrs.append(arr)
        specs.append((w, cb, d))
    tm = min(tm, s)
    specs = [pl.BlockSpec((tm // d, w), functools.partial(lambda i, cb: (i, cb), cb=cb)) for w, cb, d in specs]
    for b in bc_ins:
        arrs.append(b)
        specs.append(pl.BlockSpec(b.shape, lambda i: (0, 0)))
    outs = [(tuple(o) + (1,))[:3] for o in outs]
    n_in, n_out = len(arrs), len(outs)

    def body(*refs):
        n_ref = n_in + n_out + len(reds)
        vals = fn(*[r[...] for r in refs[:n_in]], *refs[n_ref:])
        if not isinstance(vals, (tuple, list)):
            vals = (vals,)
        for r, v in zip(refs[n_in:n_in + n_out], vals[:n_out]):
            r[...] = v.astype(r.dtype)
        if reds:
            red_refs = refs[n_in + n_out:n_ref]

            @pl.when(pl.program_id(0) == 0)
            def _():
                for r in red_refs:
                    r[...] = jnp.zeros_like(r)

            for r, v in zip(red_refs, vals[n_out:]):
                r[...] += v

    out_shape = [jax.ShapeDtypeStruct((s // d, c), dt) for c, dt, d in outs]
    out_shape += [jax.ShapeDtypeStruct((1, c), F32) for c in reds]
    out_specs = [pl.BlockSpec((tm // d, c), lambda i: (i, 0)) for c, _, d in outs]
    out_specs += [pl.BlockSpec((1, c), lambda i: (0, 0)) for c in reds]
    res = pl.pallas_call(
        body, name=name, grid=(s // tm,), in_specs=specs, out_specs=out_specs, out_shape=out_shape,
        scratch_shapes=list(scratch),
        compiler_params=_params(("arbitrary",) if reds else ("parallel",)),
    )(*arrs)
    return res


def _to_class(x, scr, d):
    if d == 1:
        return x.astype(F32)
    tm, c = x.shape
    for g in range(c // 128):
        scr[g][...] = x[:, g * 128:(g + 1) * 128].astype(F32)
    return jnp.concatenate([scr[g][pl.ds(r, tm // d, stride=d), :] for r in range(d) for g in range(c // 128)],
                           axis=1)


def _from_class(x, scr, d):
    if d == 1:
        return x.astype(F32)
    n, dc = x.shape
    c = dc // d
    for r in range(d):
        for g in range(c // 128):
            scr[g][pl.ds(r, n, stride=d), :] = x[:, r * c + g * 128:r * c + (g + 1) * 128].astype(F32)
    return jnp.concatenate([scr[g][...] for g in range(c // 128)], axis=1)


def _mm(name, a, b, mode, out_dtype, tm=1024, tn=1024, tk=1024, side=None):
    if mode == "nn":
        (m, k), (_, n) = a.shape, b.shape
    elif mode == "nt":
        (m, k), (n, _) = a.shape, b.shape
    else:
        (k, m), (_, n) = a.shape, b.shape
    tm, tn, tk = min(tm, m), min(tn, n), min(tk, k)
    nk = k // tk
    grid = (m // tm, n // tn, nk)
    dn = {"nn": NN, "nt": NT, "tn": TN}[mode]
    n_si = len(side["ins"]) if side else 0
    n_so = len(side["outs"]) if side else 0
    n_acc = 1 if nk > 1 else 0

    def body(*refs):
        a_ref, b_ref = refs[:2]
        side_in, o_ref = refs[2:2 + n_si], refs[2 + n_si]
        side_out = refs[3 + n_si:3 + n_si + n_so]
        acc = refs[3 + n_si + n_so:3 + n_si + n_so + n_acc]
        side_scratch = refs[3 + n_si + n_so + n_acc:]
        step = (pl.program_id(0) * grid[1] + pl.program_id(1)) * grid[2] + pl.program_id(2)
        if side:
            @pl.when(step == 0)
            def _():
                side["start"](side_in, side_out, side_scratch)

        part = _dot(a_ref[...].astype(BF16), b_ref[...].astype(BF16), dn)
        if nk == 1:
            o_ref[...] = part.astype(o_ref.dtype)
        else:
            kk = pl.program_id(2)

            @pl.when(kk == 0)
            def _():
                acc[0][...] = part

            @pl.when(kk > 0)
            def _():
                acc[0][...] += part

            @pl.when(kk == nk - 1)
            def _():
                o_ref[...] = acc[0][...].astype(o_ref.dtype)

        if side:
            @pl.when(step == grid[0] * grid[1] * grid[2] - 1)
            def _():
                side["wait"](side_in, side_out, side_scratch)

    a_spec = (pl.BlockSpec((tk, tm), lambda i, j, kk: (kk, i)) if mode == "tn"
              else pl.BlockSpec((tm, tk), lambda i, j, kk: (i, kk)))
    b_spec = (pl.BlockSpec((tn, tk), lambda i, j, kk: (j, kk)) if mode == "nt"
              else pl.BlockSpec((tk, tn), lambda i, j, kk: (kk, j)))
    o_spec = pl.BlockSpec((tm, tn), lambda i, j, kk: (i, j))
    o_shape = jax.ShapeDtypeStruct((m, n), out_dtype)
    acc_scratch = [pltpu.VMEM((tm, tn), F32)] * n_acc
    if not side:
        return pl.pallas_call(
            body, name=name, grid=grid, in_specs=[a_spec, b_spec], out_specs=o_spec, out_shape=o_shape,
            scratch_shapes=acc_scratch, compiler_params=_params(("parallel", "parallel", "arbitrary")),
        )(a, b)
    return pl.pallas_call(
        body, name=name, grid=grid, in_specs=[a_spec, b_spec] + [HBM_SPEC] * n_si,
        out_specs=[o_spec] + [HBM_SPEC] * n_so, out_shape=[o_shape] + side["outs"],
        scratch_shapes=acc_scratch + side["scratch"],
        compiler_params=_params(("arbitrary", "arbitrary", "arbitrary")),
    )(a, b, *side["ins"])


def _rms_fwd(name, x, g):
    def fn(xv, gv):
        r = lax.rsqrt(jnp.mean(xv * xv, axis=-1, keepdims=True) + EPS)
        return (xv * r * gv,)
    return _rows(name, fn, [x], [g], [(x.shape[1], BF16)], tm=min(512, x.shape[0]))[0]


def _rope_tables(pos, inv):
    ang = pos.astype(F32) * inv
    lane = lax.broadcasted_iota(jnp.int32, ang.shape, 1)
    c = jnp.where(lane < ROT_DIM, jnp.cos(ang), 1.0)
    sn = jnp.sin(ang)
    sg = jnp.where(lane < ROT_DIM // 2, -sn, jnp.where(lane < ROT_DIM, sn, 0.0))
    return c, sg, lane


def _rope_apply(x, c, sg, lane):
    outs = []
    for h in range(x.shape[1] // HD):
        xh = x[:, h * HD:(h + 1) * HD].astype(F32)
        swap = jnp.where(lane < ROT_DIM // 2, pltpu.roll(xh, HD - ROT_DIM // 2, 1),
                         pltpu.roll(xh, ROT_DIM // 2, 1))
        outs.append(xh * c + swap * sg)
    return jnp.concatenate(outs, axis=1)


ROPE_TM = 256


def _class_scratch(tm):
    return [pltpu.VMEM((tm, 128), F32) for _ in range(A_GROUP // 128)]


def _rope_fwd(u, pos, inv):
    def fn(q, k, v, p, iv, *scr):
        c, sg, lane = _rope_tables(p, iv)
        qr, kr = _rope_apply(q, c, sg, lane), _rope_apply(k, c, sg, lane)
        outs = []
        for g, d in enumerate(DILATIONS):
            gs = slice(g * A_GROUP, (g + 1) * A_GROUP)
            outs += [_to_class(qr[:, gs], scr, d), _to_class(kr[:, gs], scr, d), _to_class(v[:, gs], scr, d)]
        return tuple(outs)

    outs = [(d * A_GROUP, BF16, d) for d in DILATIONS for _ in range(3)]
    return _rows("rope_fwd", fn, [(u, 1536, 0), (u, 1536, 1), (u, 1536, 2), pos], [inv], outs, tm=ROPE_TM,
                 scratch=_class_scratch(ROPE_TM))


def _rope_bwd(dqs, dks, dvs, pos, inv):
    def fn(*args):
        grads, p, iv, scr = args[:9], args[9], args[10], args[11:]
        c, sg, lane = _rope_tables(p, iv)
        tok = [jnp.concatenate([_from_class(grads[3 * k + g], scr, d) for g, d in enumerate(DILATIONS)], axis=1)
               for k in range(3)]
        return _rope_apply(tok[0], c, -sg, lane), _rope_apply(tok[1], c, -sg, lane), tok[2]

    ins = [(a, a.shape[1], 0, d) for grp in (dqs, dks, dvs) for a, d in zip(grp, DILATIONS)]
    return _rows("rope_bwd", fn, ins + [pos], [inv], [(1536, BF16)] * 3, tm=ROPE_TM,
                 scratch=_class_scratch(ROPE_TM))


def _lane_pack(cols, like):
    lane = lax.broadcasted_iota(jnp.int32, like, 1)
    out = jnp.zeros(like, F32)
    for h, cvec in enumerate(cols):
        out = jnp.where(lane == h, cvec, out)
    return out


def _band_specs(l, d, tq):
    nsb = tq // BAND
    nblk = l // BAND
    cur = pl.BlockSpec((tq, A_GROUP), lambda r, i: (i, r))
    prev = pl.BlockSpec((BAND, A_GROUP), lambda r, i: (jnp.maximum(i * nsb - 1, 0), r))
    nxt = pl.BlockSpec((BAND, A_GROUP), lambda r, i: (jnp.minimum((i + 1) * nsb, nblk - 1), r))
    st_cur = pl.BlockSpec((tq, HD), lambda r, i: (i, r))
    st_nxt = pl.BlockSpec((BAND, HD), lambda r, i: (jnp.minimum((i + 1) * nsb, nblk - 1), r))
    return nsb, cur, prev, nxt, st_cur, st_nxt


def _band_mask_q(i, first_tile):
    qr = lax.broadcasted_iota(jnp.int32, (BAND, 2 * BAND), 0)
    kc = lax.broadcasted_iota(jnp.int32, (BAND, 2 * BAND), 1)
    in_prev = (kc < BAND) & (kc >= qr)
    in_cur = (kc >= BAND) & (kc - BAND <= qr)
    if i == 0:
        in_prev = in_prev & jnp.logical_not(first_tile)
    return in_prev | in_cur


def _band_mask_k(j, nsb, last_tile):
    qr = lax.broadcasted_iota(jnp.int32, (2 * BAND, BAND), 0)
    kc = lax.broadcasted_iota(jnp.int32, (2 * BAND, BAND), 1)
    same = (qr < BAND) & (kc <= qr)
    nxt = (qr >= BAND) & (kc >= qr - BAND)
    if j == nsb - 1:
        nxt = nxt & jnp.logical_not(last_tile)
    return same | nxt


def _band_fwd(name, q, k, v, d):
    l = q.shape[0]
    tq = min(512, l)
    nsb, cur, prev, _, st_cur, _ = _band_specs(l, d, tq)
    scale = HD ** -0.5

    def body(q_ref, kc_ref, kp_ref, vc_ref, vp_ref, o_ref, lse_ref):
        first = pl.program_id(1) == 0
        for i in range(nsb):
            lses = []
            mask = _band_mask_q(i, first)
            for h in range(4):
                cs = slice(h * HD, (h + 1) * HD)
                qv = q_ref[i * BAND:(i + 1) * BAND, cs]
                if i == 0:
                    kk = jnp.concatenate([kp_ref[:, cs], kc_ref[0:BAND, cs]], axis=0)
                    vv = jnp.concatenate([vp_ref[:, cs], vc_ref[0:BAND, cs]], axis=0)
                else:
                    kk = kc_ref[(i - 1) * BAND:(i + 1) * BAND, cs]
                    vv = vc_ref[(i - 1) * BAND:(i + 1) * BAND, cs]
                s = jnp.where(mask, _dot(qv, kk, NT) * scale, NEG)
                m = jnp.max(s, axis=-1, keepdims=True)
                p = jnp.exp(s - m)
                den = jnp.sum(p, axis=-1, keepdims=True)
                o_ref[i * BAND:(i + 1) * BAND, cs] = _dot(p.astype(BF16), vv) / den
                lses.append(m + jnp.log(den))
            lse_ref[i * BAND:(i + 1) * BAND, :] = _lane_pack(lses, (BAND, HD))

    return pl.pallas_call(
        body, name=name, grid=(d, l // tq), in_specs=[cur, cur, prev, cur, prev],
        out_specs=[cur, st_cur],
        out_shape=[jax.ShapeDtypeStruct((l, d * A_GROUP), F32), jax.ShapeDtypeStruct((l, d * HD), F32)],
        compiler_params=_params(("parallel", "parallel")),
    )(q, k, k, v, v)


def _band_dq(name, q, k, v, dy, lse, delta, d):
    l = q.shape[0]
    tq = min(512, l)
    nsb, cur, prev, _, st_cur, _ = _band_specs(l, d, tq)
    scale = HD ** -0.5

    def body(q_ref, kc_ref, kp_ref, vc_ref, vp_ref, dy_ref, lse_ref, dl_ref, dq_ref):
        first = pl.program_id(1) == 0
        for i in range(nsb):
            mask = _band_mask_q(i, first)
            rs = slice(i * BAND, (i + 1) * BAND)
            for h in range(4):
                cs = slice(h * HD, (h + 1) * HD)
                if i == 0:
                    kk = jnp.concatenate([kp_ref[:, cs], kc_ref[0:BAND, cs]], axis=0)
                    vv = jnp.concatenate([vp_ref[:, cs], vc_ref[0:BAND, cs]], axis=0)
                else:
                    kk = kc_ref[(i - 1) * BAND:(i + 1) * BAND, cs]
                    vv = vc_ref[(i - 1) * BAND:(i + 1) * BAND, cs]
                s = jnp.where(mask, _dot(q_ref[rs, cs], kk, NT) * scale, NEG)
                p = jnp.exp(s - lse_ref[rs, h:h + 1])
                dp = _dot(dy_ref[rs, cs], vv, NT)
                ds = p * (dp - dl_ref[rs, h:h + 1])
                dq_ref[rs, cs] = (_dot(ds.astype(BF16), kk) * scale).astype(dq_ref.dtype)

    return pl.pallas_call(
        body, name=name, grid=(d, l // tq),
        in_specs=[cur, cur, prev, cur, prev, cur, st_cur, st_cur], out_specs=cur,
        out_shape=jax.ShapeDtypeStruct((l, d * A_GROUP), BF16),
        compiler_params=_params(("parallel", "parallel")),
    )(q, k, k, v, v, dy, lse, delta)


def _band_dkv(name, q, k, v, dy, lse, delta, d):
    l = q.shape[0]
    tq = min(512, l)
    nsb, cur, _, nxt, st_cur, st_nxt = _band_specs(l, d, tq)
    scale = HD ** -0.5
    ntile = l // tq

    def body(k_ref, v_ref, qc_ref, qn_ref, dyc_ref, dyn_ref, lc_ref, ln_ref, dc_ref, dn_ref,
             dk_ref, dv_ref):
        last = pl.program_id(1) == ntile - 1

        def win(c_ref, n_ref, j, cs):
            if j == nsb - 1:
                return jnp.concatenate([c_ref[j * BAND:(j + 1) * BAND, cs], n_ref[:, cs]], axis=0)
            return c_ref[j * BAND:(j + 2) * BAND, cs]

        for j in range(nsb):
            mask = _band_mask_k(j, nsb, last)
            rs = slice(j * BAND, (j + 1) * BAND)
            for h in range(4):
                cs = slice(h * HD, (h + 1) * HD)
                hs = slice(h, h + 1)
                qw = win(qc_ref, qn_ref, j, cs)
                dyw = win(dyc_ref, dyn_ref, j, cs)
                s = jnp.where(mask, _dot(qw, k_ref[rs, cs], NT) * scale, NEG)
                p = jnp.exp(s - win(lc_ref, ln_ref, j, hs))
                dp = _dot(dyw, v_ref[rs, cs], NT)
                ds = p * (dp - win(dc_ref, dn_ref, j, hs))
                dv_ref[rs, cs] = _dot(p.astype(BF16), dyw, TN).astype(dv_ref.dtype)
                dk_ref[rs, cs] = (_dot(ds.astype(BF16), qw, TN) * scale).astype(dk_ref.dtype)

    shp = jax.ShapeDtypeStruct((l, d * A_GROUP), BF16)
    return pl.pallas_call(
        body, name=name, grid=(d, ntile),
        in_specs=[cur, cur, cur, nxt, cur, nxt, st_cur, st_nxt, st_cur, st_nxt],
        out_specs=[cur, cur], out_shape=[shp, shp],
        compiler_params=_params(("parallel", "parallel")),
    )(k, v, q, q, dy, dy, lse, lse, delta, delta)


def _split3(x):
    hi = x.astype(BF16)
    r1 = x - hi.astype(F32)
    mid = r1.astype(BF16)
    lo = (r1 - mid.astype(F32)).astype(BF16)
    return hi, mid, lo


def _fox_prep(z, b):
    h, s = z.shape
    blk = min(512, s)

    def body(z_ref, b_ref, c_ref):
        r = lax.broadcasted_iota(jnp.int32, (blk, blk), 0)
        cidx = lax.broadcasted_iota(jnp.int32, (blk, blk), 1)
        tri = (r <= cidx).astype(BF16)
        carry = jnp.zeros((h, 1), F32)
        for t in range(s // blk):
            zz = z_ref[:, t * blk:(t + 1) * blk] + b_ref[...]
            lf = jnp.minimum(zz, 0.0) - jnp.log(1.0 + jnp.exp(-jnp.abs(zz)))
            hi, mid, lo = _split3(lf)
            cs = _dot(hi, tri) + _dot(mid, tri) + _dot(lo, tri) + carry
            c_ref[:, t * blk:(t + 1) * blk] = cs
            carry = cs[:, blk - 1:blk]

    return pl.pallas_call(body, name="fox_prep", out_shape=jax.ShapeDtypeStruct((h, s), F32))(z, b)


def _fox_prep_bwd(dck, dcq, z, b):
    h, s = z.shape
    blk = min(512, s)

    def body(dck_ref, dcq_ref, z_ref, b_ref, dz_ref, db_ref):
        r = lax.broadcasted_iota(jnp.int32, (blk, blk), 0)
        cidx = lax.broadcasted_iota(jnp.int32, (blk, blk), 1)
        tri = (r >= cidx).astype(BF16)
        carry = jnp.zeros((h, 1), F32)
        tot = jnp.zeros((h, 1), F32)
        for t in reversed(range(s // blk)):
            hi, mid, lo = _split3(dcq_ref[:, t * blk:(t + 1) * blk] - dck_ref[:, t * blk:(t + 1) * blk])
            rc = _dot(hi, tri) + _dot(mid, tri) + _dot(lo, tri) + carry
            carry = rc[:, 0:1]
            zz = z_ref[:, t * blk:(t + 1) * blk] + b_ref[...]
            dz = rc * _sig(-zz)
            dz_ref[:, t * blk:(t + 1) * blk] = dz
            tot = tot + jnp.sum(dz, axis=-1, keepdims=True)
        db_ref[...] = tot

    return pl.pallas_call(
        body, name="fox_prep_bwd",
        out_shape=[jax.ShapeDtypeStruct((h, s), F32), jax.ShapeDtypeStruct((h, 1), F32)])(dck, dcq, z, b)


FOX_W = 128
FOX_C = B_HD
FOX_ONE = B_HD + 3
FOX_SUB = 256
FOX_SUB_FWD = 128
FOX_HEADS_PER_STEP = 2


def _fox_aug_k(k, c_col):
    def fn(kv, cv):
        lane = lax.broadcasted_iota(jnp.int32, kv.shape, 1)
        neg = cv * (-(B_HD ** -0.5))
        hi = neg.astype(BF16).astype(F32)
        mid = (neg - hi).astype(BF16).astype(F32)
        lo = neg - hi - mid
        aux = jnp.where(lane == FOX_C, hi, jnp.where(lane == FOX_C + 1, mid, jnp.where(lane == FOX_C + 2, lo, 0.0)))
        kb = jnp.where(lane < B_HD, kv.astype(F32) * (B_HD ** -0.5), aux)
        return kb, jnp.where(lane == FOX_ONE, 1.0, kb)

    return _rows("fox_aug_k", fn, [k, c_col], [], [(FOX_W, BF16)] * 2, tm=2048)


FOX_DEAD = -110.0


def _fox_norm2(x, t):
    h, s, w = x.shape

    def body(x_ref, o_ref):
        xv = x_ref[...].astype(F32)
        lane = lax.broadcasted_iota(jnp.int32, xv.shape, 1)
        n2 = jnp.sum(jnp.where(lane < B_HD, xv * xv, 0.0), axis=-1, keepdims=True)
        o_ref[...] = jnp.broadcast_to(jnp.max(n2, axis=0, keepdims=True), o_ref.shape)

    return pl.pallas_call(
        body, name="fox_norm2", grid=(h, s // t),
        in_specs=[pl.BlockSpec((None, t, w), lambda hh, i: (hh, i, 0))],
        out_specs=pl.BlockSpec((None, None, 8, 128), lambda hh, i: (hh, i, 0, 0)),
        out_shape=jax.ShapeDtypeStruct((h, s // t, 8, 128), F32),
        compiler_params=_params(("parallel", "parallel")),
    )(x)


def _fox_bounds(qf, kb, c, t):
    qn = jnp.sqrt(jnp.max(_fox_norm2(qf, t)[:, :, 0, 0], axis=1))
    kn = jnp.sqrt(jnp.max(_fox_norm2(kb, t)[:, :, 0, 0], axis=1))
    return jnp.concatenate([c[:, ::t], c[:, t - 1::t], (2.0 * qn * kn)[:, None]], axis=1)


SMEM_SPEC = pl.BlockSpec(memory_space=pltpu.SMEM)


def _fox_fwd(qf, kb, vt4, bounds, t):
    h, s, w = qf.shape
    nt = s // t
    sub = FOX_SUB_FWD
    nsub = t // sub
    nh = FOX_HEADS_PER_STEP

    def body(b_ref, q_ref, k_ref, v_ref, o_ref, lse_ref):
        i = pl.program_id(1)
        krow = lax.broadcasted_iota(jnp.int32, (sub, t), 0)
        qcol = lax.broadcasted_iota(jnp.int32, (sub, t), 1)

        def dead_before(hh):
            head = pl.program_id(0) * nh + hh
            top = b_ref[head, 2 * nt] + b_ref[head, i]
            return lax.fori_loop(
                0, i, lambda jj, n: n + (top - b_ref[head, nt + jj] < FOX_DEAD).astype(jnp.int32), 0)

        j_lo = functools.reduce(jnp.minimum, [dead_before(hh) for hh in range(nh)])

        def tile(j, carry, diag):
            out = []
            for hh in range(nh):
                m, acc = carry[hh]
                qv, vj = q_ref[hh], v_ref[hh, j]
                sts = [_dot(k_ref[hh, pl.ds(pl.multiple_of(j * t + b * sub, sub), sub), :], qv, NT)
                       for b in range(nsub)]
                for b in range(nsub):
                    st = sts[b]
                    if diag:
                        st = jnp.where(krow + b * sub <= qcol, st, NEG)
                    m2 = jnp.maximum(m, jnp.max(st, axis=0, keepdims=True))
                    p = jnp.exp(st - m2).astype(BF16)
                    acc = jnp.exp(m - m2) * acc + _dot(vj[:, b * sub:(b + 1) * sub], p)
                    m = m2
                out.append((m, acc))
            return tuple(out)

        init = tuple((jnp.full((1, t), NEG, F32), jnp.zeros((w, t), F32)) for _ in range(nh))
        carry = lax.fori_loop(j_lo, i, lambda j, c: tile(j, c, False), init)
        for hh, (m, acc) in enumerate(tile(i, carry, True)):
            den = acc[B_HD:B_HD + 1, :]
            o_ref[hh] = (acc[0:B_HD, :] / den).astype(o_ref.dtype)
            lse_ref[hh] = m + jnp.log(den)

    return pl.pallas_call(
        body, name="fox_fwd", grid=(h // nh, nt),
        in_specs=[SMEM_SPEC,
                  pl.BlockSpec((nh, t, w), lambda hh, i: (hh, i, 0)),
                  pl.BlockSpec((nh, s, w), lambda hh, i: (hh, 0, 0)),
                  pl.BlockSpec((nh, nt, w, t), lambda hh, i: (hh, 0, 0, 0))],
        out_specs=[pl.BlockSpec((nh, B_HD, t), lambda hh, i: (hh, 0, i)),
                   pl.BlockSpec((nh, 1, t), lambda hh, i: (hh, 0, i))],
        out_shape=[jax.ShapeDtypeStruct((h, B_HD, s), BF16), jax.ShapeDtypeStruct((h, 1, s), F32)],
        compiler_params=_params(("parallel", "parallel")),
    )(bounds, qf, kb, vt4)


def _fox_bwd(qf, dow, lse_row, delta_row, kb, kst4, vb, bounds, t):
    h, s, w = qf.shape
    nt = s // t
    nsub = t // FOX_SUB
    nh = FOX_HEADS_PER_STEP

    def body(b_ref, q_ref, do_ref, lse_ref, dl_ref, k_ref, kt_ref, v_ref, dqt_ref, dk_ref, dv_ref, dk_acc, dv_acc):
        j = pl.program_id(1)

        def alive_after(hh):
            head = pl.program_id(0) * nh + hh
            top = b_ref[head, 2 * nt] - b_ref[head, nt + j]
            return lax.fori_loop(
                j + 1, nt, lambda ii, n: n + (top + b_ref[head, ii] >= FOX_DEAD).astype(jnp.int32), 0)

        i_hi = j + 1 + functools.reduce(jnp.maximum, [alive_after(hh) for hh in range(nh)])

        @pl.when(j == 0)
        def _():
            dqt_ref[...] = jnp.zeros_like(dqt_ref)

        dk_acc[...] = jnp.zeros_like(dk_acc)
        dv_acc[...] = jnp.zeros_like(dv_acc)
        krow = lax.broadcasted_iota(jnp.int32, (FOX_SUB, t), 0)
        qcol = lax.broadcasted_iota(jnp.int32, (FOX_SUB, t), 1)
        subs = [slice(b * FOX_SUB, (b + 1) * FOX_SUB) for b in range(nsub)]

        def tile(i, diag):
            i0 = pl.multiple_of(i * t, t)
            for hh in range(nh):
                qi, doi = q_ref[hh, pl.ds(i0, t), :], do_ref[hh, pl.ds(i0, t), :]
                lse, dl = lse_ref[hh, i], dl_ref[hh, i]
                sts = [_dot(k_ref[hh, rs, :], qi, NT) for rs in subs]
                dps = [_dot(v_ref[hh, rs, :], doi, NT) for rs in subs]
                dq = None
                for b, rs in enumerate(subs):
                    st = sts[b] - lse
                    if diag:
                        st = jnp.where(krow + b * FOX_SUB <= qcol, st, NEG)
                    pt = jnp.exp(st)
                    dsb = (pt * (dps[b] - dl)).astype(BF16)
                    dv_acc[hh, rs, :] += _dot(pt.astype(BF16), doi)
                    dk_acc[hh, rs, :] += _dot(dsb, qi)
                    part = _dot(kt_ref[hh, :, rs], dsb)
                    dq = part if dq is None else dq + part
                dqt_ref[hh, i] += dq

        def step(i, carry):
            tile(i, False)
            return carry

        tile(j, True)
        lax.fori_loop(j + 1, i_hi, step, 0)
        dk_ref[...] = dk_acc[...] * (B_HD ** -0.5)
        dv_ref[...] = dv_acc[...]

    full = pl.BlockSpec((nh, s, w), lambda hh, j: (hh, 0, 0))
    rowst = pl.BlockSpec((nh, nt, 1, t), lambda hh, j: (hh, 0, 0, 0))
    tl = pl.BlockSpec((nh, t, w), lambda hh, j: (hh, j, 0))
    return pl.pallas_call(
        body, name="fox_bwd", grid=(h // nh, nt),
        in_specs=[SMEM_SPEC, full, full, rowst, rowst, tl,
                  pl.BlockSpec((nh, None, w, t), lambda hh, j: (hh, j, 0, 0)), tl],
        out_specs=[pl.BlockSpec((nh, nt, w, t), lambda hh, j: (hh, 0, 0, 0)), tl, tl],
        out_shape=[jax.ShapeDtypeStruct((h, nt, w, t), F32), jax.ShapeDtypeStruct((h, s, w), F32),
                   jax.ShapeDtypeStruct((h, s, w), F32)],
        scratch_shapes=[pltpu.VMEM((nh, t, w), F32), pltpu.VMEM((nh, t, w), F32)],
        compiler_params=_params(("parallel", "arbitrary")),
    )(bounds, qf, dow, lse_row, delta_row, kb, kst4, vb)


def _mem_fwd(u, mkv, tq=512):
    s = u.shape[0]
    scale = HD ** -0.5

    def body(q_ref, mk_ref, mv_ref, o_ref, lse_ref):
        lses = []
        for h in range(4):
            cs = slice(h * HD, (h + 1) * HD)
            sc = _dot(q_ref[:, cs], mk_ref[:, cs], NT) * scale
            m = jnp.max(sc, axis=-1, keepdims=True)
            p = jnp.exp(sc - m)
            den = jnp.sum(p, axis=-1, keepdims=True)
            o_ref[:, cs] = (_dot(p.astype(BF16), mv_ref[:, cs]) / den).astype(o_ref.dtype)
            lses.append(m + jnp.log(den))
        lse_ref[...] = _lane_pack(lses, (tq, HD))

    return pl.pallas_call(
        body, name="mem_fwd", grid=(s // tq,),
        in_specs=[pl.BlockSpec((tq, 512), lambda i: (i, C_QM // 512)),
                  pl.BlockSpec((N_MEM, 512), lambda i: (0, 0)),
                  pl.BlockSpec((N_MEM, 512), lambda i: (0, 1))],
        out_specs=[pl.BlockSpec((tq, 512), lambda i: (i, 0)), pl.BlockSpec((tq, HD), lambda i: (i, 0))],
        out_shape=[jax.ShapeDtypeStruct((s, 512), BF16), jax.ShapeDtypeStruct((s, HD), F32)],
        compiler_params=_params(("parallel",)),
    )(u, mkv, mkv)


def _mem_bwd(u, mkv, o, do, lse, tq=512):
    s = u.shape[0]
    scale = HD ** -0.5

    def body(q_ref, mk_ref, mv_ref, o_ref, do_ref, lse_ref, dq_ref, dmk_ref, dmv_ref):
        @pl.when(pl.program_id(0) == 0)
        def _():
            dmk_ref[...] = jnp.zeros_like(dmk_ref)
            dmv_ref[...] = jnp.zeros_like(dmv_ref)

        for h in range(4):
            cs = slice(h * HD, (h + 1) * HD)
            qv, dov = q_ref[:, cs], do_ref[:, cs]
            sc = _dot(qv, mk_ref[:, cs], NT) * scale
            p = jnp.exp(sc - lse_ref[:, h:h + 1])
            delta = jnp.sum(dov.astype(F32) * o_ref[:, cs].astype(F32), axis=-1, keepdims=True)
            ds = p * (_dot(dov, mv_ref[:, cs], NT) - delta)
            dsb = ds.astype(BF16)
            dq_ref[:, cs] = (_dot(dsb, mk_ref[:, cs]) * scale).astype(dq_ref.dtype)
            dmk_ref[:, cs] += _dot(dsb, qv, TN) * scale
            dmv_ref[:, cs] += _dot(p.astype(BF16), dov, TN)

    row = pl.BlockSpec((tq, 512), lambda i: (i, 0))
    acc = pl.BlockSpec((N_MEM, 512), lambda i: (0, 0))
    return pl.pallas_call(
        body, name="mem_bwd", grid=(s // tq,),
        in_specs=[pl.BlockSpec((tq, 512), lambda i: (i, C_QM // 512)),
                  pl.BlockSpec((N_MEM, 512), lambda i: (0, 0)),
                  pl.BlockSpec((N_MEM, 512), lambda i: (0, 1)),
                  row, row, pl.BlockSpec((tq, HD), lambda i: (i, 0))],
        out_specs=[row, acc, acc],
        out_shape=[jax.ShapeDtypeStruct((s, 512), BF16), jax.ShapeDtypeStruct((N_MEM, 512), F32),
                   jax.ShapeDtypeStruct((N_MEM, 512), F32)],
        compiler_params=_params(("arbitrary",)),
    )(u, mkv, mkv, o, do, lse)


def _heads_major(a, col0):
    s = a.shape[0]
    return a[:, col0:col0 + 512].reshape(s, B_HEADS, B_HD).transpose(1, 0, 2)


def _token_major(a):
    h, s, dh = a.shape
    return a.transpose(1, 0, 2).reshape(s, h * dh)


def _class_view(a, d):
    s, c = a.shape
    return a.reshape(s // d, d * c)


def _local_step(x, mem, pos, target, g_pre, g_post, g_mem, w_main, w_fb, b_forget, b_merge,
                w_mem_kv, w_ba, w_bb, w_bm, w_out, exchange=None):
    s = x.shape[0]
    t_fox = min(512, s)
    nt = s // t_fox
    half = ROT_DIM // 2
    inv = ROPE_THETA ** (-jnp.arange(half, dtype=F32) / half)
    inv128 = jnp.concatenate([inv, inv, jnp.zeros((HD - ROT_DIM,), F32)]).reshape(1, HD)

    h = _rms_fwd("norm_pre", x, g_pre)
    u = _mm("proj_in", h, w_main, "nn", BF16)
    ufb = _mm("proj_fb", h, w_fb, "nn", F32)
    memn = _rms_fwd("norm_mem", mem, g_mem)
    mkv = _mm("proj_mem", memn, w_mem_kv, "nn", BF16)

    qkv = _rope_fwd(u, pos, inv128)
    views = [tuple(qkv[3 * g:3 * g + 3]) for g in range(3)]
    os_, lses = [], []
    for g, d in enumerate(DILATIONS):
        o_g, lse_g = _band_fwd("band_fwd%d" % g, *views[g], d)
        os_.append((o_g, d * A_GROUP, 0, d))
        lses.append((lse_g, d * HD, 0, d))

    def merge_a(o1, o2, o3, l1, l2, l3, za, *scr):
        o1, o2, o3 = [_from_class(o, scr, d) for o, d in zip((o1, o2, o3), DILATIONS)]
        l1, l2, l3 = [_from_class(lv, scr, d) for lv, d in zip((l1, l2, l3), DILATIONS)]
        ys, tots = [], []
        for hh in range(4):
            cs, hs = slice(hh * HD, (hh + 1) * HD), slice(hh, hh + 1)
            mx = jnp.maximum(jnp.maximum(l1[:, hs], l2[:, hs]), l3[:, hs])
            e1, e2, e3 = jnp.exp(l1[:, hs] - mx), jnp.exp(l2[:, hs] - mx), jnp.exp(l3[:, hs] - mx)
            den = e1 + e2 + e3
            ys.append((e1 * o1[:, cs] + e2 * o2[:, cs] + e3 * o3[:, cs]) / den)
            tots.append(mx + jnp.log(den))
        y = jnp.concatenate(ys, axis=1)
        zf = za.astype(F32)
        tot = _lane_pack(tots, l1.shape)
        return (y, y * (zf * _sig(zf))) + tuple(_to_class(tot, scr, d) for d in DILATIONS)

    res = _rows("merge_a", merge_a, os_ + lses + [(u, 512, C_ZA // 512)], [],
                [(512, BF16), (512, BF16)] + [(d * HD, F32, d) for d in DILATIONS], tm=ROPE_TM,
                scratch=_class_scratch(ROPE_TM))
    y_a, yg_a, lse_a = res[0], res[1], res[2:5]

    zrow = ufb[:, :B_HEADS].T
    c = _fox_prep(zrow, b_forget.reshape(B_HEADS, 1))
    n_hs = B_HEADS * s

    def wide(a, fill):
        return jnp.pad(a, ((0, 0), (0, 0), (0, FOX_W - B_HD)), constant_values=fill)

    def tiles_t(a):
        return a.reshape(B_HEADS, nt, t_fox, FOX_W).transpose(0, 1, 3, 2)

    qf = wide(_heads_major(u, C_QB), B_HD ** 0.5)
    vb = wide(_heads_major(u, C_VB), 1.0)
    kb, ks = _fox_aug_k(wide(_heads_major(u, C_KB), 0.0).reshape(n_hs, FOX_W), c.reshape(n_hs, 1))
    kb, ks = kb.reshape(B_HEADS, s, FOX_W), ks.reshape(B_HEADS, s, FOX_W)
    bounds = _fox_bounds(qf, kb, c, t_fox)
    ot, lse_b = _fox_fwd(qf, kb, tiles_t(vb), bounds, t_fox)
    y_b = ot.transpose(2, 0, 1).reshape(s, B_HEADS * B_HD)

    y_m, lse_m = _mem_fwd(u, mkv)

    def gate(y, z):
        zf = z.astype(F32)
        return (y.astype(F32) * (zf * _sig(zf)),)

    yg_b = _rows("gate_b", gate, [y_b, (u, 512, C_ZB // 512)], [], [(512, BF16)])[0]
    yg_m = _rows("gate_m", gate, [y_m, (u, 512, C_ZM // 512)], [], [(512, BF16)])[0]

    br_a = _mm("branch_a", yg_a, w_ba, "nn", BF16)
    br_b = _mm("branch_b", yg_b, w_bb, "nn", BF16)
    br_m = _mm("branch_m", yg_m, w_bm, "nn", BF16)
    gl = [(u, 1024, C_GL // 1024 + i) for i in range(3)]
    bm3 = b_merge.reshape(3, D_MODEL)

    def merge(g0, g1, g2, b0, b1, b2, bm):
        tot = 0.0
        for i, (gv, bv) in enumerate(((g0, b0), (g1, b1), (g2, b2))):
            tot = tot + _sig(gv.astype(F32) + bm[i:i + 1, :]) * bv.astype(F32)
        return (tot,)

    merged = _rows("merge_gates", merge, gl + [br_a, br_b, br_m], [bm3], [(D_MODEL, BF16)])[0]
    out = _mm("proj_out", merged, w_out, "nn", F32)

    def tail(xv, ov, tv, gv):
        r = lax.rsqrt(jnp.mean(ov * ov, axis=-1, keepdims=True) + EPS)
        n = ov * r
        err = xv + n * gv - tv
        dy = err * (1.0 / D_MODEL)
        dn = dy * gv
        dout = r * (dn - n * jnp.mean(dn * n, axis=-1, keepdims=True))
        return (dy, dout, jnp.sum(0.5 * err * err * (1.0 / D_MODEL), axis=0, keepdims=True),
                jnp.sum(dy * n, axis=0, keepdims=True))

    dy, dout, loss_lanes, g_post_grad = _rows(
        "tail", tail, [x, out, target], [g_post], [(D_MODEL, F32), (D_MODEL, BF16)],
        reds=[D_MODEL, D_MODEL], tm=256)

    dmerged = _mm("d_merged", dout, w_out, "nt", BF16)
    gw_out = _mm("g_w_out", merged, dout, "tn", F32)

    def merge_bwd(dm, g0, g1, g2, b0, b1, b2, bm):
        dmf = dm.astype(F32)
        dbs, dgs, sums = [], [], []
        for i, (gv, bv) in enumerate(((g0, b0), (g1, b1), (g2, b2))):
            sg = _sig(gv.astype(F32) + bm[i:i + 1, :])
            dbs.append(dmf * sg)
            dg = dmf * bv.astype(F32) * sg * (1.0 - sg)
            dgs.append(dg)
            sums.append(jnp.sum(dg, axis=0, keepdims=True))
        return tuple(dbs + dgs + sums)

    res = _rows("merge_bwd", merge_bwd, [dmerged] + gl + [br_a, br_b, br_m], [bm3],
                [(D_MODEL, BF16)] * 6, reds=[D_MODEL] * 3, tm=256)
    dbr, dgl, g_bmerge = res[0:3], res[3:6], jnp.concatenate(res[6:9], axis=1)

    dyg, gw_branch = [], []
    for nm, dbv, wv, ygv in (("a", dbr[0], w_ba, yg_a), ("b", dbr[1], w_bb, yg_b), ("m", dbr[2], w_bm, yg_m)):
        dyg.append(_mm("d_yg_" + nm, dbv, wv, "nt", BF16))
        gw_branch.append(_mm("g_w_branch_" + nm, ygv, dbv, "tn", F32))

    def gate_bwd(dg, y, z):
        dgf, yf, zf = dg.astype(F32), y.astype(F32), z.astype(F32)
        sg = _sig(zf)
        return dgf * (zf * sg), dgf * yf * (sg * (1.0 + zf * (1.0 - sg)))

    def gate_bwd_a(dg, y, z, *scr):
        dyv, dz = gate_bwd(dg, y, z)
        prod = dyv * y.astype(F32)
        dl = [jnp.sum(prod[:, hh * HD:(hh + 1) * HD], axis=-1, keepdims=True) for hh in range(4)]
        delta = _lane_pack(dl, (dg.shape[0], HD))
        return ((dz,) + tuple(_to_class(dyv, scr, d) for d in DILATIONS)
                + tuple(_to_class(delta, scr, d) for d in DILATIONS))

    res = _rows("gate_bwd_a", gate_bwd_a, [dyg[0], y_a, (u, 512, C_ZA // 512)], [],
                [(512, BF16)] + [(d * A_GROUP, BF16, d) for d in DILATIONS] + [(d * HD, F32, d) for d in DILATIONS],
                tm=ROPE_TM, scratch=_class_scratch(ROPE_TM))
    dz_a, dy_a, delta_a = res[0], res[1:4], res[4:7]
    dy_b, dz_b = _rows("gate_bwd_b", gate_bwd, [dyg[1], y_b, (u, 512, C_ZB // 512)], [],
                       [(512, BF16), (512, BF16)])
    dy_m, dz_m = _rows("gate_bwd_m", gate_bwd, [dyg[2], y_m, (u, 512, C_ZM // 512)], [],
                       [(512, BF16), (512, BF16)])

    dq_m, dmk, dmv = _mem_bwd(u, mkv, y_m, dy_m, lse_m)
    dmkv = jnp.concatenate([dmk, dmv], axis=1)
    gw_mem_kv = _mm("g_w_mem_kv", memn, dmkv, "tn", F32)
    dmemn = _mm("d_memn", dmkv, w_mem_kv, "nt", F32)

    def mem_gain_grad(mv, dv):
        r = lax.rsqrt(jnp.mean(mv * mv, axis=-1, keepdims=True) + EPS)
        return (jnp.sum(dv * mv * r, axis=0, keepdims=True),)

    g_mem_grad = _rows("g_norm_mem", mem_gain_grad, [mem, dmemn], [], [], reds=[D_MODEL], tm=N_MEM)[0]

    dob = _heads_major(dy_b, 0)

    def fox_delta(a, b):
        return (jnp.sum(a.astype(F32) * b.astype(F32), axis=-1, keepdims=True),)

    ob = ot.transpose(0, 2, 1).reshape(n_hs, B_HD)
    delta_b = _rows("fox_delta", fox_delta, [dob.reshape(n_hs, B_HD), ob], [], [(1, F32)], tm=min(2048, s))[0]
    dqt, dkw, dvw = _fox_bwd(qf, wide(dob, 0.0), lse_b.reshape(B_HEADS, nt, 1, t_fox),
                             delta_b.reshape(B_HEADS, nt, 1, t_fox), kb, tiles_t(ks), vb, bounds, t_fox)
    dqb = dqt[:, :, :B_HD, :].transpose(0, 1, 3, 2).reshape(B_HEADS, s, B_HD)
    dkb, dvb = dkw[:, :, :B_HD].astype(BF16), dvw[:, :, :B_HD].astype(BF16)
    dzrow, g_bforget = _fox_prep_bwd(dkw[:, :, B_HD], dqt[:, :, FOX_ONE, :].reshape(B_HEADS, s), zrow,
                                     b_forget.reshape(B_HEADS, 1))
    dfb = jnp.zeros((s, HD), BF16).at[:, :B_HEADS].set(dzrow.T.astype(BF16))

    dqs, dks, dvs = [], [], []
    for g, d in enumerate(DILATIONS):
        qv, kv, vv = views[g]
        dqs.append(_band_dq("band_dq%d" % g, qv, kv, vv, dy_a[g], lse_a[g], delta_a[g], d))
        dk_g, dv_g = _band_dkv("band_dkv%d" % g, qv, kv, vv, dy_a[g], lse_a[g], delta_a[g], d)
        dks.append(dk_g)
        dvs.append(dv_g)
    dqa, dka, dva = _rope_bwd(dqs, dks, dvs, pos, inv128)

    du = jnp.concatenate(
        [dqa, dka, dva, dz_a, _token_major(dqb).astype(BF16), _token_major(dkb), _token_major(dvb),
                            dz_b, dq_m, dz_m] + list(dgl), axis=1)

    gw_main = _mm("g_w_main", h.T, du, "nn", F32, tk=2048)
    gw_fb = _mm("g_w_fb", h, dfb, "tn", F32)
    gw_in = jnp.concatenate([gw_main[:, :FB_ORIG], gw_fb[:, :B_HEADS], gw_main[:, FB_ORIG:]], axis=1)
    grads = dict(norm_post_g=g_post_grad, norm_mem_g=g_mem_grad, w_in=gw_in,
                 b_forget=g_bforget.reshape(1, B_HEADS), b_merge=g_bmerge, w_mem_kv=gw_mem_kv,
                 w_branch_a=gw_branch[0], w_branch_b=gw_branch[1], w_branch_m=gw_branch[2], w_out=gw_out)
    side = exchange(grads) if exchange else None
    dh_main = _mm("d_h", du, w_main, "nt", F32, tk=2816, side=side)
    landed = None
    if side:
        dh_main, landed = dh_main[0], dh_main[1:]
    dh_fb = _mm("d_h_fb", dfb, w_fb, "nt", F32)

    def pre_bwd(xv, d1, d2, dyv, gv):
        r = lax.rsqrt(jnp.mean(xv * xv, axis=-1, keepdims=True) + EPS)
        n = xv * r
        dhv = d1 + d2
        dn = dhv * gv
        dx = r * (dn - n * jnp.mean(dn * n, axis=-1, keepdims=True))
        return dyv + dx, jnp.sum(dhv * n, axis=0, keepdims=True)

    grad_x, g_pre_grad = _rows("norm_pre_bwd", pre_bwd, [x, dh_main, dh_fb, dy], [g_pre],
                               [(D_MODEL, F32)], reds=[D_MODEL], tm=256)

    grads["norm_pre_g"] = g_pre_grad
    return loss_lanes, grad_x, grads, landed


HBM_SPEC = pl.BlockSpec(memory_space=pltpu.HBM)


def _place():
    x, y, c = lax.axis_index("x"), lax.axis_index("y"), lax.axis_index("c")
    chips = [(1 - x, y), (x, 1 - y), (1 - x, 1 - y)]
    return x, y, c, 2 * x + y, chips


N_CHUNKS = 4


def _units(parts, row_axis):
    units = []
    for i, a in enumerate(parts):
        ch = a.shape[row_axis] // N_CHUNKS
        units += [(i, pl.ds(k * ch, ch)) for k in range(N_CHUNKS)]
    return units


def _gather_weights(parts):
    n = len(parts)
    units = _units(parts, 1)
    nu = len(units)

    def body(*refs):
        srcs, outs = refs[:n], refs[n:2 * n]
        send_sems, recv_sems = refs[2 * n:]
        x, y, c, p, chips = _place()
        me, sib = (x, y, c), (x, y, 1 - c)

        def cp(u, k, chip, half, to, from_src=False):
            i, rs = units[u]
            dst = outs[i].at[chip, half, rs]
            return pltpu.make_async_remote_copy(
                src_ref=srcs[i].at[half, rs] if from_src else dst, dst_ref=dst, send_sem=send_sems.at[u, k],
                recv_sem=recv_sems.at[u, k], device_id=to, device_id_type=MESH)

        first = [cp(u, j, p, c, (cx, cy, c), from_src=True)
                 for u in range(nu) for j, (cx, cy) in enumerate(chips)]
        for f in first:
            f.start()
        passed = []
        for u in range(nu):
            for j, (cx, cy) in enumerate(chips):
                cp(u, j, 2 * cx + cy, c, me).wait_recv()
                fw = cp(u, 3 + j, 2 * cx + cy, c, sib)
                fw.start()
                passed.append(fw)
        for u in range(nu):
            for j, (cx, cy) in enumerate(chips):
                cp(u, 3 + j, 2 * cx + cy, 1 - c, me).wait_recv()
        for f in first + passed:
            f.wait_send()

    return pl.pallas_call(
        body, name="gather_weights", in_specs=[HBM_SPEC] * n, out_specs=[HBM_SPEC] * n,
        out_shape=[jax.ShapeDtypeStruct((N_CHIPS,) + a.shape, a.dtype) for a in parts],
        scratch_shapes=[pltpu.SemaphoreType.DMA((nu, 6)), pltpu.SemaphoreType.DMA((nu, 6))],
    )(*parts)


def _swap_with_sibling(parts):
    n = len(parts)
    units = _units(parts, 1)

    def body(*refs):
        srcs, outs = refs[:n], refs[n:2 * n]
        send_sems, recv_sems = refs[2 * n:]
        x, y, c, _, _ = _place()
        cps = [pltpu.make_async_remote_copy(
            src_ref=srcs[i].at[q, rs], dst_ref=outs[i].at[q, rs], send_sem=send_sems.at[u, q],
            recv_sem=recv_sems.at[u, q], device_id=(x, y, 1 - c), device_id_type=MESH)
            for q in range(N_CHIPS) for u, (i, rs) in enumerate(units)]
        for cpy in cps:
            cpy.start()
        for cpy in cps:
            cpy.wait()

    return pl.pallas_call(
        body, name="swap_with_sibling", in_specs=[HBM_SPEC] * n, out_specs=[HBM_SPEC] * n,
        out_shape=[jax.ShapeDtypeStruct(a.shape, a.dtype) for a in parts],
        scratch_shapes=[pltpu.SemaphoreType.DMA((len(units), N_CHIPS)),
                        pltpu.SemaphoreType.DMA((len(units), N_CHIPS))],
    )(*parts)


def _scatter_to_owners(parts):
    n = len(parts)
    units = _units(parts, 1)

    def copies(srcs, outs, send_sems, recv_sems, incoming):
        x, y, c, p, chips = _place()
        return [pltpu.make_async_remote_copy(
            src_ref=srcs[i].at[2 * cx + cy, rs], dst_ref=outs[i].at[(2 * cx + cy) if incoming else p, rs],
            send_sem=send_sems.at[u, j], recv_sem=recv_sems.at[u, j], device_id=(cx, cy, c), device_id_type=MESH)
            for u, (i, rs) in enumerate(units) for j, (cx, cy) in enumerate(chips)]

    def start(ins, outs, scratch):
        for cpy in copies(ins, outs, *scratch, incoming=False):
            cpy.start()

    def wait(ins, outs, scratch):
        for cpy in copies(ins, outs, *scratch, incoming=True):
            cpy.wait_recv()
        for cpy in copies(ins, outs, *scratch, incoming=False):
            cpy.wait_send()

    return dict(ins=list(parts), outs=[jax.ShapeDtypeStruct(a.shape, a.dtype) for a in parts],
                scratch=[pltpu.SemaphoreType.DMA((len(units), 3)), pltpu.SemaphoreType.DMA((len(units), 3))],
                start=start, wait=wait)


def _share_with_sibling(parts):
    n = len(parts)
    units = _units(parts, 1)

    def body(*refs):
        srcs, outs = refs[:n], refs[n:2 * n]
        send_sems, recv_sems = refs[2 * n:]
        x, y, c, _, _ = _place()
        sends = [pltpu.make_async_remote_copy(
            src_ref=srcs[i].at[0, rs], dst_ref=outs[i].at[c, rs], send_sem=send_sems.at[u],
            recv_sem=recv_sems.at[u], device_id=(x, y, 1 - c), device_id_type=MESH)
            for u, (i, rs) in enumerate(units)]
        for cpy in sends:
            cpy.start()
        for u, (i, rs) in enumerate(units):
            pltpu.make_async_remote_copy(
                src_ref=srcs[i].at[0, rs], dst_ref=outs[i].at[1 - c, rs], send_sem=send_sems.at[u],
                recv_sem=recv_sems.at[u], device_id=(x, y, 1 - c), device_id_type=MESH).wait_recv()
        for cpy in sends:
            cpy.wait_send()

    return pl.pallas_call(
        body, name="share_with_sibling", in_specs=[HBM_SPEC] * n, out_specs=[HBM_SPEC] * n,
        out_shape=[jax.ShapeDtypeStruct((2,) + a.shape[1:], a.dtype) for a in parts],
        scratch_shapes=[pltpu.SemaphoreType.DMA((len(units),)), pltpu.SemaphoreType.DMA((len(units),))],
    )(*parts)


def _sum_small(v):
    def body(v_ref, out_ref, buf, send_sems, recv_sems):
        x, y, c, _, _ = _place()
        me = 4 * x + 2 * y + c
        buf[me] = v_ref[...]
        flips = [(dx, dy, dc) for dx in (0, 1) for dy in (0, 1) for dc in (0, 1)][1:]
        sends = []
        for k, (dx, dy, dc) in enumerate(flips):
            cpy = pltpu.make_async_remote_copy(
                src_ref=v_ref, dst_ref=buf.at[me], send_sem=send_sems.at[k], recv_sem=recv_sems.at[k],
                device_id=((x + dx) % 2, (y + dy) % 2, (c + dc) % 2), device_id_type=MESH)
            cpy.start()
            sends.append(cpy)
        for k, (dx, dy, dc) in enumerate(flips):
            px, py, pc = (x + dx) % 2, (y + dy) % 2, (c + dc) % 2
            pltpu.make_async_remote_copy(
                src_ref=v_ref, dst_ref=buf.at[4 * px + 2 * py + pc], send_sem=send_sems.at[k],
                recv_sem=recv_sems.at[k], device_id=(px, py, pc), device_id_type=MESH).wait_recv()
        for cpy in sends:
            cpy.wait_send()
        tot = buf[0]
        for i in range(1, N_DEV):
            tot = tot + buf[i]
        out_ref[...] = tot

    return pl.pallas_call(
        body, name="sum_small", out_shape=jax.ShapeDtypeStruct(v.shape, v.dtype),
        in_specs=[pl.BlockSpec(memory_space=pltpu.VMEM)], out_specs=pl.BlockSpec(memory_space=pltpu.VMEM),
        scratch_shapes=[pltpu.VMEM((N_DEV,) + v.shape, v.dtype), pltpu.SemaphoreType.DMA((N_DEV - 1,)),
                        pltpu.SemaphoreType.DMA((N_DEV - 1,))],
    )(v)


def _add_slabs(name, terms, out_dtype):
    arr0 = terms[0][0]
    n = arr0.shape[0] if terms[0][1] is None else 1
    _, r, w = arr0.shape
    tr = 64
    specs = []
    for _, slab in terms:
        if slab is None:
            specs.append(pl.BlockSpec((None, tr, w), lambda i, j: (i, j, 0)))
        else:
            specs.append(pl.BlockSpec((None, tr, w), functools.partial(lambda i, j, sl: (sl, j, 0), sl=slab)))

    def body(*refs):
        tot = refs[0][...].astype(F32)
        for rf in refs[1:-1]:
            tot = tot + rf[...].astype(F32)
        refs[-1][...] = tot.astype(out_dtype)

    return pl.pallas_call(
        body, name=name, grid=(n, r // tr), in_specs=specs,
        out_specs=pl.BlockSpec((None, tr, w), lambda i, j: (i, j, 0)),
        out_shape=jax.ShapeDtypeStruct((n, r, w), out_dtype),
        compiler_params=_params(("parallel", "parallel")),
    )(*[a for a, _ in terms])


def _adamw(name, w, g, m, v, tm):
    def fn(wv, gv, mv, vv):
        m2 = ADAM_B1 * mv + (1.0 - ADAM_B1) * gv
        v2 = ADAM_B2 * vv + (1.0 - ADAM_B2) * (gv * gv)
        m_hat = m2 / (1.0 - ADAM_B1 ** ADAM_STEP)
        v_hat = v2 / (1.0 - ADAM_B2 ** ADAM_STEP)
        return -ADAM_LR * (m_hat / (jnp.sqrt(v_hat) + ADAM_EPS) + ADAM_WD * wv), m2, v2
    c = w.shape[1]
    return _rows(name, fn, [w, g, m, v], [], [(c, F32)] * 3, tm=tm)


REST_ROWS = 256 + 3 * 128 + 256
REST_SPLITS = (("w_mem_kv", 0, 256), ("w_branch_a", 256, 128), ("w_branch_b", 384, 128),
               ("w_branch_m", 512, 128), ("w_out", 640, 256))


def _rest_pack(t):
    return jnp.concatenate([t[n].reshape(rows, D_MODEL) for n, _, rows in REST_SPLITS], axis=0)


def _rest_unpack(a, shapes):
    return {n: a[r0:r0 + rows].reshape(shapes[n]) for n, r0, rows in REST_SPLITS}


def _small_pack(pre, post, memg, bforget, bmerge):
    pad = jnp.zeros((1, D_MODEL - B_HEADS), F32)
    return jnp.concatenate([pre, post, memg, bmerge.reshape(3, D_MODEL),
                            jnp.concatenate([bforget, pad], axis=1), jnp.zeros((1, D_MODEL), F32)], axis=0)


def _small_unpack(s8):
    return dict(norm_pre_g=s8[0:1], norm_post_g=s8[1:2], norm_mem_g=s8[2:3],
                b_merge=s8[3:6].reshape(1, 3 * D_MODEL), b_forget=s8[6:7, :B_HEADS])


WEIGHTS = ("norm_pre_g", "norm_post_g", "norm_mem_g", "w_in", "b_forget", "b_merge", "w_mem_kv",
           "w_branch_a", "w_branch_b", "w_branch_m", "w_out")
SMALL = ("norm_pre_g", "norm_post_g", "norm_mem_g", "b_forget", "b_merge")


def kernel(x, mem, positions, norm_pre_g, norm_post_g, norm_mem_g, w_in, b_forget, b_merge, w_mem_kv, w_branch_a, w_branch_b, w_branch_m, w_out, loss_target, m_norm_pre_g, m_norm_post_g, m_norm_mem_g, m_w_in, m_b_forget, m_b_merge, m_w_mem_kv, m_w_branch_a, m_w_branch_b, m_w_branch_m, m_w_out, v_norm_pre_g, v_norm_post_g, v_norm_mem_g, v_w_in, v_b_forget, v_b_merge, v_w_mem_kv, v_w_branch_a, v_w_branch_b, v_w_branch_m, v_w_out):
    w = dict(norm_pre_g=norm_pre_g, norm_post_g=norm_post_g, norm_mem_g=norm_mem_g, w_in=w_in[0],
             b_forget=b_forget, b_merge=b_merge, w_mem_kv=w_mem_kv[0], w_branch_a=w_branch_a[0],
             w_branch_b=w_branch_b[0], w_branch_m=w_branch_m[0], w_out=w_out[0])
    mo = dict(norm_pre_g=m_norm_pre_g, norm_post_g=m_norm_post_g, norm_mem_g=m_norm_mem_g, w_in=m_w_in[0],
              b_forget=m_b_forget, b_merge=m_b_merge, w_mem_kv=m_w_mem_kv[0], w_branch_a=m_w_branch_a[0],
              w_branch_b=m_w_branch_b[0], w_branch_m=m_w_branch_m[0], w_out=m_w_out[0])
    vo = dict(norm_pre_g=v_norm_pre_g, norm_post_g=v_norm_post_g, norm_mem_g=v_norm_mem_g, w_in=v_w_in[0],
              b_forget=v_b_forget, b_merge=v_b_merge, w_mem_kv=v_w_mem_kv[0], w_branch_a=v_w_branch_a[0],
              w_branch_b=v_w_branch_b[0], w_branch_m=v_w_branch_m[0], w_out=v_w_out[0])
    s = x.shape[1]
    c = lax.axis_index("c")

    chip = 2 * lax.axis_index("x") + lax.axis_index("y")

    def put(whole, own, slot):
        return lax.dynamic_update_index_in_dim(whole, own.astype(whole.dtype), slot, 0)

    own_w = [w["w_in"].astype(BF16).reshape(2, D_MODEL // 2, SHARD_COLS),
             _rest_pack(w).astype(BF16).reshape(2, REST_ROWS // 2, D_MODEL)]
    all_in, all_rest = [put(a, o, chip) for a, o in zip(_gather_weights(own_w), own_w)]
    all_in = all_in.reshape(N_CHIPS, D_MODEL, SHARD_COLS)
    w_in_f = jnp.concatenate([all_in[p] for p in range(N_CHIPS)], axis=1)
    all_rest = all_rest.reshape(N_CHIPS, REST_ROWS, D_MODEL)
    w_kv_f = all_rest[:, 0:256].reshape(D_MODEL, D_MODEL)
    w_br_f = [all_rest[:, 256 + 128 * i:384 + 128 * i].reshape(N_CHIPS, 512, 256).transpose(1, 0, 2)
              .reshape(512, D_MODEL) for i in range(3)]
    w_out_f = all_rest[:, 640:896].reshape(D_MODEL, D_MODEL)
    w_main = jnp.concatenate([w_in_f[:, :FB_ORIG], w_in_f[:, FB_ORIG + B_HEADS:]], axis=1)
    w_fb = jnp.concatenate([w_in_f[:, FB_ORIG:FB_ORIG + B_HEADS], jnp.zeros((D_MODEL, HD - B_HEADS), BF16)], axis=1)

    pair = []

    def exchange(g):
        def per_chip(name, p):
            a = g[name]
            if name in ("w_mem_kv", "w_out"):
                return a[256 * p:256 * (p + 1)]
            return a[:, 256 * p:256 * (p + 1)]

        in4 = jnp.stack([g["w_in"][:, SHARD_COLS * p:SHARD_COLS * (p + 1)] for p in range(N_CHIPS)])
        rest4 = jnp.stack([_rest_pack({n: per_chip(n, p) for n, _, _ in REST_SPLITS}) for p in range(N_CHIPS)])
        halves = [in4.reshape(N_CHIPS, 2, D_MODEL // 2, SHARD_COLS),
                  rest4.reshape(N_CHIPS, 2, REST_ROWS // 2, D_MODEL)]
        mine = [lax.dynamic_index_in_dim(a, c, axis=1, keepdims=False) for a in halves]
        theirs = [lax.dynamic_index_in_dim(a, 1 - c, axis=1, keepdims=False) for a in halves]
        got = _swap_with_sibling(theirs)
        pair.extend(_add_slabs("add_pair_%d" % i, [(mine[i], None), (got[i], None)], BF16) for i in range(2))
        return _scatter_to_owners(pair)

    loss_lanes, grad_x, g, landed = _local_step(
        x[0], mem[0], positions.reshape(s, 1), loss_target[0], norm_pre_g, norm_post_g, norm_mem_g,
        w_main, w_fb, b_forget, b_merge, w_kv_f, w_br_f[0], w_br_f[1], w_br_f[2], w_out_f, exchange)
    loss = lax.psum(jnp.sum(loss_lanes), ("x", "y", "c"))
    landed = [put(a, lax.dynamic_index_in_dim(o, chip, 0, keepdims=False), chip) for a, o in zip(landed, pair)]
    half = [_add_slabs("add_chips_%d" % i, [(landed[i], q) for q in range(N_CHIPS)], F32) for i in range(2)]
    red_in, red_rest = [put(a, o[0], c) for a, o in zip(_share_with_sibling(half), half)]
    gs = {"w_in": red_in.reshape(D_MODEL, SHARD_COLS)}
    gs.update(_rest_unpack(red_rest.reshape(REST_ROWS, D_MODEL), {n: w[n].shape for n, _, _ in REST_SPLITS}))
    gs.update(_small_unpack(_sum_small(_small_pack(
        g["norm_pre_g"], g["norm_post_g"], g["norm_mem_g"], g["b_forget"], g["b_merge"]))))

    delta, new_m, new_v = {}, {}, {}
    for n, tm in (("w_in", 128), ("w_mem_kv", 256), ("w_branch_a", 512), ("w_branch_b", 512),
                  ("w_branch_m", 512), ("w_out", 256)):
        d_, m_, v_ = _adamw("adamw_" + n, w[n], gs[n], mo[n], vo[n], tm)
        delta[n], new_m[n], new_v[n] = d_[None], m_[None], v_[None]
        gs[n] = gs[n][None]
    packs = [_small_pack(*[t[n] for n in ("norm_pre_g", "norm_post_g", "norm_mem_g", "b_forget", "b_merge")])
             for t in (w, gs, mo, vo)]
    for res, store in zip(_adamw("adamw_small", *packs, 8), (delta, new_m, new_v)):
        store.update(_small_unpack(res))

    return (loss, grad_x[None], *[gs[n] for n in WEIGHTS], *[delta[n] for n in WEIGHTS],
            *[new_m[n] for n in WEIGHTS], *[new_v[n] for n in WEIGHTS])
```

```python
import functools

import jax
import jax.numpy as jnp
from jax import lax
from jax.experimental import pallas as pl
from jax.experimental.pallas import tpu as pltpu

F32 = jnp.float32
BF16 = jnp.bfloat16
MESH = pl.DeviceIdType.MESH

D_MODEL = 1024
N_MEM = 256
EPS = 1e-6
NEG = -1e30
ROPE_THETA = 500000.0
ROT_DIM = 32
HD = 128
A_GROUP = 512
DILATIONS = (1, 4, 16)
BAND = 128
B_HEADS = 8
B_HD = 64
N_CHIPS = 4
N_DEV = 8

C_QA, C_KA, C_VA, C_ZA = 0, 1536, 3072, 4608
C_QB, C_KB, C_VB, C_ZB = 5120, 5632, 6144, 6656
C_QM, C_ZM, C_GL = 7168, 7680, 8192
N_MAIN = 11264
FB_ORIG = 6656
IN_COLS = 11272
SHARD_COLS = IN_COLS // N_CHIPS

ADAM_LR, ADAM_B1, ADAM_B2, ADAM_EPS, ADAM_WD, ADAM_STEP = 0.001, 0.9, 0.999, 1e-08, 0.01, 10

VMEM_LIMIT_V7X = 56 * 1024 * 1024

NT = (((1,), (1,)), ((), ()))
NN = (((1,), (0,)), ((), ()))
TN = (((0,), (0,)), ((), ()))


def _params(sem):
    return pltpu.CompilerParams(dimension_semantics=sem, vmem_limit_bytes=VMEM_LIMIT_V7X)


def _dot(a, b, dn=NN):
    return lax.dot_general(a, b, dn, preferred_element_type=F32)


def _sig(z):
    return 1.0 / (1.0 + jnp.exp(-z))


def _rows(name, fn, row_ins, bc_ins, outs, reds=(), tm=512, scratch=()):
    arrs, specs = [], []
    s = None
    for r in row_ins:
        arr, w, cb, d = (tuple(r) + (1,))[:4] if isinstance(r, tuple) else (r, r.shape[1], 0, 1)
        s = arr.shape[0] * d if s is None else s
        arrs.append(arr)
        specs.append((w, cb, d))
    tm = min(tm, s)
    specs = [pl.BlockSpec((tm // d, w), functools.partial(lambda i, cb: (i, cb), cb=cb)) for w, cb, d in specs]
    for b in bc_ins:
        arrs.append(b)
        specs.append(pl.BlockSpec(b.shape, lambda i: (0, 0)))
    outs = [(tuple(o) + (1,))[:3] for o in outs]
    n_in, n_out = len(arrs), len(outs)

    def body(*refs):
        n_ref = n_in + n_out + len(reds)
        vals = fn(*[r[...] for r in refs[:n_in]], *refs[n_ref:])
        if not isinstance(vals, (tuple, list)):
            vals = (vals,)
        for r, v in zip(refs[n_in:n_in + n_out], vals[:n_out]):
            r[...] = v.astype(r.dtype)
        if reds:
            red_refs = refs[n_in + n_out:n_ref]

            @pl.when(pl.program_id(0) == 0)
            def _():
                for r in red_refs:
                    r[...] = jnp.zeros_like(r)

            for r, v in zip(red_refs, vals[n_out:]):
                r[...] += v

    out_shape = [jax.ShapeDtypeStruct((s // d, c), dt) for c, dt, d in outs]
    out_shape += [jax.ShapeDtypeStruct((1, c), F32) for c in reds]
    out_specs = [pl.BlockSpec((tm // d, c), lambda i: (i, 0)) for c, _, d in outs]
    out_specs += [pl.BlockSpec((1, c), lambda i: (0, 0)) for c in reds]
    res = pl.pallas_call(
        body, name=name, grid=(s // tm,), in_specs=specs, out_specs=out_specs, out_shape=out_shape,
        scratch_shapes=list(scratch),
        compiler_params=_params(("arbitrary",) if reds else ("parallel",)),
    )(*arrs)
    return res


def _to_class(x, scr, d):
    if d == 1:
        return x.astype(F32)
    tm, c = x.shape
    for g in range(c // 128):
        scr[g][...] = x[:, g * 128:(g + 1) * 128].astype(F32)
    return jnp.concatenate([scr[g][pl.ds(r, tm // d, stride=d), :] for r in range(d) for g in range(c // 128)],
                           axis=1)


def _from_class(x, scr, d):
    if d == 1:
        return x.astype(F32)
    n, dc = x.shape
    c = dc // d
    for r in range(d):
        for g in range(c // 128):
            scr[g][pl.ds(r, n, stride=d), :] = x[:, r * c + g * 128:r * c + (g + 1) * 128].astype(F32)
    return jnp.concatenate([scr[g][...] for g in range(c // 128)], axis=1)


def _mm(name, a, b, mode, out_dtype, tm=1024, tn=1024, tk=1024, side=None):
    if mode == "nn":
        (m, k), (_, n) = a.shape, b.shape
    elif mode == "nt":
        (m, k), (n, _) = a.shape, b.shape
    else:
        (k, m), (_, n) = a.shape, b.shape
    tm, tn, tk = min(tm, m), min(tn, n), min(tk, k)
    nk = k // tk
    grid = (m // tm, n // tn, nk)
    dn = {"nn": NN, "nt": NT, "tn": TN}[mode]
    n_si = len(side["ins"]) if side else 0
    n_so = len(side["outs"]) if side else 0
    n_acc = 1 if nk > 1 else 0

    def body(*refs):
        a_ref, b_ref = refs[:2]
        side_in, o_ref = refs[2:2 + n_si], refs[2 + n_si]
        side_out = refs[3 + n_si:3 + n_si + n_so]
        acc = refs[3 + n_si + n_so:3 + n_si + n_so + n_acc]
        side_scratch = refs[3 + n_si + n_so + n_acc:]
        step = (pl.program_id(0) * grid[1] + pl.program_id(1)) * grid[2] + pl.program_id(2)
        if side:
            @pl.when(step == 0)
            def _():
                side["start"](side_in, side_out, side_scratch)

        part = _dot(a_ref[...].astype(BF16), b_ref[...].astype(BF16), dn)
        if nk == 1:
            o_ref[...] = part.astype(o_ref.dtype)
        else:
            kk = pl.program_id(2)

            @pl.when(kk == 0)
            def _():
                acc[0][...] = part

            @pl.when(kk > 0)
            def _():
                acc[0][...] += part

            @pl.when(kk == nk - 1)
            def _():
                o_ref[...] = acc[0][...].astype(o_ref.dtype)

        if side:
            @pl.when(step == grid[0] * grid[1] * grid[2] - 1)
            def _():
                side["wait"](side_in, side_out, side_scratch)

    a_spec = (pl.BlockSpec((tk, tm), lambda i, j, kk: (kk, i)) if mode == "tn"
              else pl.BlockSpec((tm, tk), lambda i, j, kk: (i, kk)))
    b_spec = (pl.BlockSpec((tn, tk), lambda i, j, kk: (j, kk)) if mode == "nt"
              else pl.BlockSpec((tk, tn), lambda i, j, kk: (kk, j)))
    o_spec = pl.BlockSpec((tm, tn), lambda i, j, kk: (i, j))
    o_shape = jax.ShapeDtypeStruct((m, n), out_dtype)
    acc_scratch = [pltpu.VMEM((tm, tn), F32)] * n_acc
    if not side:
        return pl.pallas_call(
            body, name=name, grid=grid, in_specs=[a_spec, b_spec], out_specs=o_spec, out_shape=o_shape,
            scratch_shapes=acc_scratch, compiler_params=_params(("parallel", "parallel", "arbitrary")),
        )(a, b)
    return pl.pallas_call(
        body, name=name, grid=grid, in_specs=[a_spec, b_spec] + [HBM_SPEC] * n_si,
        out_specs=[o_spec] + [HBM_SPEC] * n_so, out_shape=[o_shape] + side["outs"],
        scratch_shapes=acc_scratch + side["scratch"],
        compiler_params=_params(("arbitrary", "arbitrary", "arbitrary")),
    )(a, b, *side["ins"])


def _rms_fwd(name, x, g):
    def fn(xv, gv):
        r = lax.rsqrt(jnp.mean(xv * xv, axis=-1, keepdims=True) + EPS)
        return (xv * r * gv,)
    return _rows(name, fn, [x], [g], [(x.shape[1], BF16)], tm=min(512, x.shape[0]))[0]


def _rope_tables(pos, inv):
    ang = pos.astype(F32) * inv
    lane = lax.broadcasted_iota(jnp.int32, ang.shape, 1)
    c = jnp.where(lane < ROT_DIM, jnp.cos(ang), 1.0)
    sn = jnp.sin(ang)
    sg = jnp.where(lane < ROT_DIM // 2, -sn, jnp.where(lane < ROT_DIM, sn, 0.0))
    return c, sg, lane


def _rope_apply(x, c, sg, lane):
    outs = []
    for h in range(x.shape[1] // HD):
        xh = x[:, h * HD:(h + 1) * HD].astype(F32)
        swap = jnp.where(lane < ROT_DIM // 2, pltpu.roll(xh, HD - ROT_DIM // 2, 1),
                         pltpu.roll(xh, ROT_DIM // 2, 1))
        outs.append(xh * c + swap * sg)
    return jnp.concatenate(outs, axis=1)


ROPE_TM = 256


def _class_scratch(tm):
    return [pltpu.VMEM((tm, 128), F32) for _ in range(A_GROUP // 128)]


def _rope_fwd(u, pos, inv):
    def fn(q, k, v, p, iv, *scr):
        c, sg, lane = _rope_tables(p, iv)
        qr, kr = _rope_apply(q, c, sg, lane), _rope_apply(k, c, sg, lane)
        outs = []
        for g, d in enumerate(DILATIONS):
            gs = slice(g * A_GROUP, (g + 1) * A_GROUP)
            outs += [_to_class(qr[:, gs], scr, d), _to_class(kr[:, gs], scr, d), _to_class(v[:, gs], scr, d)]
        return tuple(outs)

    outs = [(d * A_GROUP, BF16, d) for d in DILATIONS for _ in range(3)]
    return _rows("rope_fwd", fn, [(u, 1536, 0), (u, 1536, 1), (u, 1536, 2), pos], [inv], outs, tm=ROPE_TM,
                 scratch=_class_scratch(ROPE_TM))


def _rope_bwd(dqs, dks, dvs, pos, inv):
    def fn(*args):
        grads, p, iv, scr = args[:9], args[9], args[10], args[11:]
        c, sg, lane = _rope_tables(p, iv)
        tok = [jnp.concatenate([_from_class(grads[3 * k + g], scr, d) for g, d in enumerate(DILATIONS)], axis=1)
               for k in range(3)]
        return _rope_apply(tok[0], c, -sg, lane), _rope_apply(tok[1], c, -sg, lane), tok[2]

    ins = [(a, a.shape[1], 0, d) for grp in (dqs, dks, dvs) for a, d in zip(grp, DILATIONS)]
    return _rows("rope_bwd", fn, ins + [pos], [inv], [(1536, BF16)] * 3, tm=ROPE_TM,
                 scratch=_class_scratch(ROPE_TM))


def _lane_pack(cols, like):
    lane = lax.broadcasted_iota(jnp.int32, like, 1)
    out = jnp.zeros(like, F32)
    for h, cvec in enumerate(cols):
        out = jnp.where(lane == h, cvec, out)
    return out


def _band_specs(l, d, tq):
    nsb = tq // BAND
    nblk = l // BAND
    cur = pl.BlockSpec((tq, A_GROUP), lambda r, i: (i, r))
    prev = pl.BlockSpec((BAND, A_GROUP), lambda r, i: (jnp.maximum(i * nsb - 1, 0), r))
    nxt = pl.BlockSpec((BAND, A_GROUP), lambda r, i: (jnp.minimum((i + 1) * nsb, nblk - 1), r))
    st_cur = pl.BlockSpec((tq, HD), lambda r, i: (i, r))
    st_nxt = pl.BlockSpec((BAND, HD), lambda r, i: (jnp.minimum((i + 1) * nsb, nblk - 1), r))
    return nsb, cur, prev, nxt, st_cur, st_nxt


def _band_mask_q(i, first_tile):
    qr = lax.broadcasted_iota(jnp.int32, (BAND, 2 * BAND), 0)
    kc = lax.broadcasted_iota(jnp.int32, (BAND, 2 * BAND), 1)
    in_prev = (kc < BAND) & (kc >= qr)
    in_cur = (kc >= BAND) & (kc - BAND <= qr)
    if i == 0:
        in_prev = in_prev & jnp.logical_not(first_tile)
    return in_prev | in_cur


def _band_mask_k(j, nsb, last_tile):
    qr = lax.broadcasted_iota(jnp.int32, (2 * BAND, BAND), 0)
    kc = lax.broadcasted_iota(jnp.int32, (2 * BAND, BAND), 1)
    same = (qr < BAND) & (kc <= qr)
    nxt = (qr >= BAND) & (kc >= qr - BAND)
    if j == nsb - 1:
        nxt = nxt & jnp.logical_not(last_tile)
    return same | nxt


def _band_fwd(name, q, k, v, d):
    l = q.shape[0]
    tq = min(512, l)
    nsb, cur, prev, _, st_cur, _ = _band_specs(l, d, tq)
    scale = HD ** -0.5

    def body(q_ref, kc_ref, kp_ref, vc_ref, vp_ref, o_ref, lse_ref):
        first = pl.program_id(1) == 0
        for i in range(nsb):
            lses = []
            mask = _band_mask_q(i, first)
            for h in range(4):
                cs = slice(h * HD, (h + 1) * HD)
                qv = q_ref[i * BAND:(i + 1) * BAND, cs]
                if i == 0:
                    kk = jnp.concatenate([kp_ref[:, cs], kc_ref[0:BAND, cs]], axis=0)
                    vv = jnp.concatenate([vp_ref[:, cs], vc_ref[0:BAND, cs]], axis=0)
                else:
                    kk = kc_ref[(i - 1) * BAND:(i + 1) * BAND, cs]
                    vv = vc_ref[(i - 1) * BAND:(i + 1) * BAND, cs]
                s = jnp.where(mask, _dot(qv, kk, NT) * scale, NEG)
                m = jnp.max(s, axis=-1, keepdims=True)
                p = jnp.exp(s - m)
                den = jnp.sum(p, axis=-1, keepdims=True)
                o_ref[i * BAND:(i + 1) * BAND, cs] = _dot(p.astype(BF16), vv) / den
                lses.append(m + jnp.log(den))
            lse_ref[i * BAND:(i + 1) * BAND, :] = _lane_pack(lses, (BAND, HD))

    return pl.pallas_call(
        body, name=name, grid=(d, l // tq), in_specs=[cur, cur, prev, cur, prev],
        out_specs=[cur, st_cur],
        out_shape=[jax.ShapeDtypeStruct((l, d * A_GROUP), F32), jax.ShapeDtypeStruct((l, d * HD), F32)],
        compiler_params=_params(("parallel", "parallel")),
    )(q, k, k, v, v)


def _band_dq(name, q, k, v, dy, lse, delta, d):
    l = q.shape[0]
    tq = min(512, l)
    nsb, cur, prev, _, st_cur, _ = _band_specs(l, d, tq)
    scale = HD ** -0.5

    def body(q_ref, kc_ref, kp_ref, vc_ref, vp_ref, dy_ref, lse_ref, dl_ref, dq_ref):
        first = pl.program_id(1) == 0
        for i in range(nsb):
            mask = _band_mask_q(i, first)
            rs = slice(i * BAND, (i + 1) * BAND)
            for h in range(4):
                cs = slice(h * HD, (h + 1) * HD)
                if i == 0:
                    kk = jnp.concatenate([kp_ref[:, cs], kc_ref[0:BAND, cs]], axis=0)
                    vv = jnp.concatenate([vp_ref[:, cs], vc_ref[0:BAND, cs]], axis=0)
                else:
                    kk = kc_ref[(i - 1) * BAND:(i + 1) * BAND, cs]
                    vv = vc_ref[(i - 1) * BAND:(i + 1) * BAND, cs]
                s = jnp.where(mask, _dot(q_ref[rs, cs], kk, NT) * scale, NEG)
                p = jnp.exp(s - lse_ref[rs, h:h + 1])
                dp = _dot(dy_ref[rs, cs], vv, NT)
                ds = p * (dp - dl_ref[rs, h:h + 1])
                dq_ref[rs, cs] = (_dot(ds.astype(BF16), kk) * scale).astype(dq_ref.dtype)

    return pl.pallas_call(
        body, name=name, grid=(d, l // tq),
        in_specs=[cur, cur, prev, cur, prev, cur, st_cur, st_cur], out_specs=cur,
        out_shape=jax.ShapeDtypeStruct((l, d * A_GROUP), BF16),
        compiler_params=_params(("parallel", "parallel")),
    )(q, k, k, v, v, dy, lse, delta)


def _band_dkv(name, q, k, v, dy, lse, delta, d):
    l = q.shape[0]
    tq = min(512, l)
    nsb, cur, _, nxt, st_cur, st_nxt = _band_specs(l, d, tq)
    scale = HD ** -0.5
    ntile = l // tq

    def body(k_ref, v_ref, qc_ref, qn_ref, dyc_ref, dyn_ref, lc_ref, ln_ref, dc_ref, dn_ref,
             dk_ref, dv_ref):
        last = pl.program_id(1) == ntile - 1

        def win(c_ref, n_ref, j, cs):
            if j == nsb - 1:
                return jnp.concatenate([c_ref[j * BAND:(j + 1) * BAND, cs], n_ref[:, cs]], axis=0)
            return c_ref[j * BAND:(j + 2) * BAND, cs]

        for j in range(nsb):
            mask = _band_mask_k(j, nsb, last)
            rs = slice(j * BAND, (j + 1) * BAND)
            for h in range(4):
                cs = slice(h * HD, (h + 1) * HD)
                hs = slice(h, h + 1)
                qw = win(qc_ref, qn_ref, j, cs)
                dyw = win(dyc_ref, dyn_ref, j, cs)
                s = jnp.where(mask, _dot(qw, k_ref[rs, cs], NT) * scale, NEG)
                p = jnp.exp(s - win(lc_ref, ln_ref, j, hs))
                dp = _dot(dyw, v_ref[rs, cs], NT)
                ds = p * (dp - win(dc_ref, dn_ref, j, hs))
                dv_ref[rs, cs] = _dot(p.astype(BF16), dyw, TN).astype(dv_ref.dtype)
                dk_ref[rs, cs] = (_dot(ds.astype(BF16), qw, TN) * scale).astype(dk_ref.dtype)

    shp = jax.ShapeDtypeStruct((l, d * A_GROUP), BF16)
    return pl.pallas_call(
        body, name=name, grid=(d, ntile),
        in_specs=[cur, cur, cur, nxt, cur, nxt, st_cur, st_nxt, st_cur, st_nxt],
        out_specs=[cur, cur], out_shape=[shp, shp],
        compiler_params=_params(("parallel", "parallel")),
    )(k, v, q, q, dy, dy, lse, lse, delta, delta)


def _split3(x):
    hi = x.astype(BF16)
    r1 = x - hi.astype(F32)
    mid = r1.astype(BF16)
    lo = (r1 - mid.astype(F32)).astype(BF16)
    return hi, mid, lo


def _fox_prep(z, b):
    h, s = z.shape
    blk = min(512, s)

    def body(z_ref, b_ref, c_ref):
        r = lax.broadcasted_iota(jnp.int32, (blk, blk), 0)
        cidx = lax.broadcasted_iota(jnp.int32, (blk, blk), 1)
        tri = (r <= cidx).astype(BF16)
        carry = jnp.zeros((h, 1), F32)
        for t in range(s // blk):
            zz = z_ref[:, t * blk:(t + 1) * blk] + b_ref[...]
            lf = jnp.minimum(zz, 0.0) - jnp.log(1.0 + jnp.exp(-jnp.abs(zz)))
            hi, mid, lo = _split3(lf)
            cs = _dot(hi, tri) + _dot(mid, tri) + _dot(lo, tri) + carry
            c_ref[:, t * blk:(t + 1) * blk] = cs
            carry = cs[:, blk - 1:blk]

    return pl.pallas_call(body, name="fox_prep", out_shape=jax.ShapeDtypeStruct((h, s), F32))(z, b)


def _fox_prep_bwd(dck, dcq, z, b):
    h, s = z.shape
    blk = min(512, s)

    def body(dck_ref, dcq_ref, z_ref, b_ref, dz_ref, db_ref):
        r = lax.broadcasted_iota(jnp.int32, (blk, blk), 0)
        cidx = lax.broadcasted_iota(jnp.int32, (blk, blk), 1)
        tri = (r >= cidx).astype(BF16)
        carry = jnp.zeros((h, 1), F32)
        tot = jnp.zeros((h, 1), F32)
        for t in reversed(range(s // blk)):
            hi, mid, lo = _split3(dcq_ref[:, t * blk:(t + 1) * blk] - dck_ref[:, t * blk:(t + 1) * blk])
            rc = _dot(hi, tri) + _dot(mid, tri) + _dot(lo, tri) + carry
            carry = rc[:, 0:1]
            zz = z_ref[:, t * blk:(t + 1) * blk] + b_ref[...]
            dz = rc * _sig(-zz)
            dz_ref[:, t * blk:(t + 1) * blk] = dz
            tot = tot + jnp.sum(dz, axis=-1, keepdims=True)
        db_ref[...] = tot

    return pl.pallas_call(
        body, name="fox_prep_bwd",
        out_shape=[jax.ShapeDtypeStruct((h, s), F32), jax.ShapeDtypeStruct((h, 1), F32)])(dck, dcq, z, b)


FOX_W = 128
FOX_C = B_HD
FOX_ONE = B_HD + 3
FOX_SUB = 256
FOX_SUB_FWD = 128
FOX_HEADS_PER_STEP = 2


def _fox_aug_k(k, c_col):
    def fn(kv, cv):
        lane = lax.broadcasted_iota(jnp.int32, kv.shape, 1)
        neg = cv * (-(B_HD ** -0.5))
        hi = neg.astype(BF16).astype(F32)
        mid = (neg - hi).astype(BF16).astype(F32)
        lo = neg - hi - mid
        aux = jnp.where(lane == FOX_C, hi, jnp.where(lane == FOX_C + 1, mid, jnp.where(lane == FOX_C + 2, lo, 0.0)))
        kb = jnp.where(lane < B_HD, kv.astype(F32) * (B_HD ** -0.5), aux)
        return kb, jnp.where(lane == FOX_ONE, 1.0, kb)

    return _rows("fox_aug_k", fn, [k, c_col], [], [(FOX_W, BF16)] * 2, tm=2048)


FOX_DEAD = -110.0


def _fox_norm2(qf, kb):
    h, s, w = qf.shape
    tm = min(2048, s)

    def body(q_ref, k_ref, qo_ref, ko_ref):
        row = lax.broadcasted_iota(jnp.int32, (w, w), 0)
        ones = (row < B_HD).astype(BF16)
        for x_ref, o_ref in ((q_ref, qo_ref), (k_ref, ko_ref)):
            xv = x_ref[...].astype(F32)
            n2 = _dot((xv * xv).astype(BF16), ones)
            o_ref[...] = jnp.broadcast_to(jnp.max(n2, axis=0, keepdims=True)[:, :1], o_ref.shape)

    spec = pl.BlockSpec((None, tm, w), lambda hh, i: (hh, i, 0))
    ospec = pl.BlockSpec((None, None, 8, 128), lambda hh, i: (hh, i, 0, 0))
    shp = jax.ShapeDtypeStruct((h, s // tm, 8, 128), F32)
    return pl.pallas_call(
        body, name="fox_norm2", grid=(h, s // tm), in_specs=[spec, spec], out_specs=[ospec, ospec],
        out_shape=[shp, shp], compiler_params=_params(("parallel", "parallel")),
    )(qf, kb)


def _fox_bounds(qf, kb, c, t):
    q2, k2 = _fox_norm2(qf, kb)
    g = 2.0 * jnp.sqrt(1.02 * jnp.max(q2[:, :, 0, 0], axis=1) * 1.02 * jnp.max(k2[:, :, 0, 0], axis=1))
    return jnp.concatenate([c[:, ::t], c[:, t - 1::t], g[:, None]], axis=1)


SMEM_SPEC = pl.BlockSpec(memory_space=pltpu.SMEM)


def _fox_fwd(qf, kb, vt4, bounds, t):
    h, s, w = qf.shape
    nt = s // t
    sub = FOX_SUB_FWD
    nsub = t // sub
    nh = FOX_HEADS_PER_STEP

    def body(b_ref, q_ref, k_ref, v_ref, o_ref, lse_ref):
        i = pl.program_id(1)
        krow = lax.broadcasted_iota(jnp.int32, (sub, t), 0)
        qcol = lax.broadcasted_iota(jnp.int32, (sub, t), 1)

        def dead_before(hh):
            head = pl.program_id(0) * nh + hh
            top = b_ref[head, 2 * nt] + b_ref[head, i]
            return lax.fori_loop(
                0, i, lambda jj, n: n + (top - b_ref[head, nt + jj] < FOX_DEAD).astype(jnp.int32), 0)

        j_lo = functools.reduce(jnp.minimum, [dead_before(hh) for hh in range(nh)])

        def tile(j, carry, diag):
            out = []
            for hh in range(nh):
                m, acc = carry[hh]
                qv, vj = q_ref[hh], v_ref[hh, j]
                sts = [_dot(k_ref[hh, pl.ds(pl.multiple_of(j * t + b * sub, sub), sub), :], qv, NT)
                       for b in range(nsub)]
                for b in range(nsub):
                    st = sts[b]
                    if diag:
                        st = jnp.where(krow + b * sub <= qcol, st, NEG)
                    m2 = jnp.maximum(m, jnp.max(st, axis=0, keepdims=True))
                    p = jnp.exp(st - m2).astype(BF16)
                    acc = jnp.exp(m - m2) * acc + _dot(vj[:, b * sub:(b + 1) * sub], p)
                    m = m2
                out.append((m, acc))
            return tuple(out)

        init = tuple((jnp.full((1, t), NEG, F32), jnp.zeros((w, t), F32)) for _ in range(nh))
        carry = lax.fori_loop(j_lo, i, lambda j, c: tile(j, c, False), init)
        for hh, (m, acc) in enumerate(tile(i, carry, True)):
            den = acc[B_HD:B_HD + 1, :]
            o_ref[hh] = (acc[0:B_HD, :] / den).astype(o_ref.dtype)
            lse_ref[hh] = m + jnp.log(den)

    return pl.pallas_call(
        body, name="fox_fwd", grid=(h // nh, nt),
        in_specs=[SMEM_SPEC,
                  pl.BlockSpec((nh, t, w), lambda hh, i: (hh, i, 0)),
                  pl.BlockSpec((nh, s, w), lambda hh, i: (hh, 0, 0)),
                  pl.BlockSpec((nh, nt, w, t), lambda hh, i: (hh, 0, 0, 0))],
        out_specs=[pl.BlockSpec((nh, B_HD, t), lambda hh, i: (hh, 0, i)),
                   pl.BlockSpec((nh, 1, t), lambda hh, i: (hh, 0, i))],
        out_shape=[jax.ShapeDtypeStruct((h, B_HD, s), BF16), jax.ShapeDtypeStruct((h, 1, s), F32)],
        compiler_params=_params(("parallel", "parallel")),
    )(bounds, qf, kb, vt4)


def _fox_bwd(qf, dow, lse_row, delta_row, kb, kst4, vb, bounds, t):
    h, s, w = qf.shape
    nt = s // t
    nsub = t // FOX_SUB
    nh = FOX_HEADS_PER_STEP

    def body(b_ref, q_ref, do_ref, lse_ref, dl_ref, k_ref, kt_ref, v_ref, dqt_ref, dk_ref, dv_ref, dk_acc, dv_acc):
        j = pl.program_id(1)

        def alive_after(hh):
            head = pl.program_id(0) * nh + hh
            top = b_ref[head, 2 * nt] - b_ref[head, nt + j]
            return lax.fori_loop(
                j + 1, nt, lambda ii, n: n + (top + b_ref[head, ii] >= FOX_DEAD).astype(jnp.int32), 0)

        i_hi = j + 1 + functools.reduce(jnp.maximum, [alive_after(hh) for hh in range(nh)])

        @pl.when(j == 0)
        def _():
            dqt_ref[...] = jnp.zeros_like(dqt_ref)

        dk_acc[...] = jnp.zeros_like(dk_acc)
        dv_acc[...] = jnp.zeros_like(dv_acc)
        krow = lax.broadcasted_iota(jnp.int32, (FOX_SUB, t), 0)
        qcol = lax.broadcasted_iota(jnp.int32, (FOX_SUB, t), 1)
        subs = [slice(b * FOX_SUB, (b + 1) * FOX_SUB) for b in range(nsub)]

        def tile(i, diag):
            i0 = pl.multiple_of(i * t, t)
            for hh in range(nh):
                qi, doi = q_ref[hh, pl.ds(i0, t), :], do_ref[hh, pl.ds(i0, t), :]
                lse, dl = lse_ref[hh, i], dl_ref[hh, i]
                sts = [_dot(k_ref[hh, rs, :], qi, NT) for rs in subs]
                dps = [_dot(v_ref[hh, rs, :], doi, NT) for rs in subs]
                dq = None
                for b, rs in enumerate(subs):
                    st = sts[b] - lse
                    if diag:
                        st = jnp.where(krow + b * FOX_SUB <= qcol, st, NEG)
                    pt = jnp.exp(st)
                    dsb = (pt * (dps[b] - dl)).astype(BF16)
                    dv_acc[hh, rs, :] += _dot(pt.astype(BF16), doi)
                    dk_acc[hh, rs, :] += _dot(dsb, qi)
                    part = _dot(kt_ref[hh, :, rs], dsb)
                    dq = part if dq is None else dq + part
                dqt_ref[hh, i] += dq

        def step(i, carry):
            tile(i, False)
            return carry

        tile(j, True)
        lax.fori_loop(j + 1, i_hi, step, 0)
        dk_ref[...] = dk_acc[...] * (B_HD ** -0.5)
        dv_ref[...] = dv_acc[...]

    full = pl.BlockSpec((nh, s, w), lambda hh, j: (hh, 0, 0))
    rowst = pl.BlockSpec((nh, nt, 1, t), lambda hh, j: (hh, 0, 0, 0))
    tl = pl.BlockSpec((nh, t, w), lambda hh, j: (hh, j, 0))
    return pl.pallas_call(
        body, name="fox_bwd", grid=(h // nh, nt),
        in_specs=[SMEM_SPEC, full, full, rowst, rowst, tl,
                  pl.BlockSpec((nh, None, w, t), lambda hh, j: (hh, j, 0, 0)), tl],
        out_specs=[pl.BlockSpec((nh, nt, w, t), lambda hh, j: (hh, 0, 0, 0)), tl, tl],
        out_shape=[jax.ShapeDtypeStruct((h, nt, w, t), F32), jax.ShapeDtypeStruct((h, s, w), F32),
                   jax.ShapeDtypeStruct((h, s, w), F32)],
        scratch_shapes=[pltpu.VMEM((nh, t, w), F32), pltpu.VMEM((nh, t, w), F32)],
        compiler_params=_params(("parallel", "arbitrary")),
    )(bounds, qf, dow, lse_row, delta_row, kb, kst4, vb)


def _mem_fwd(u, mkv, tq=512):
    s = u.shape[0]
    scale = HD ** -0.5

    def body(q_ref, mk_ref, mv_ref, o_ref, lse_ref):
        lses = []
        for h in range(4):
            cs = slice(h * HD, (h + 1) * HD)
            sc = _dot(q_ref[:, cs], mk_ref[:, cs], NT) * scale
            m = jnp.max(sc, axis=-1, keepdims=True)
            p = jnp.exp(sc - m)
            den = jnp.sum(p, axis=-1, keepdims=True)
            o_ref[:, cs] = (_dot(p.astype(BF16), mv_ref[:, cs]) / den).astype(o_ref.dtype)
            lses.append(m + jnp.log(den))
        lse_ref[...] = _lane_pack(lses, (tq, HD))

    return pl.pallas_call(
        body, name="mem_fwd", grid=(s // tq,),
        in_specs=[pl.BlockSpec((tq, 512), lambda i: (i, C_QM // 512)),
                  pl.BlockSpec((N_MEM, 512), lambda i: (0, 0)),
                  pl.BlockSpec((N_MEM, 512), lambda i: (0, 1))],
        out_specs=[pl.BlockSpec((tq, 512), lambda i: (i, 0)), pl.BlockSpec((tq, HD), lambda i: (i, 0))],
        out_shape=[jax.ShapeDtypeStruct((s, 512), BF16), jax.ShapeDtypeStruct((s, HD), F32)],
        compiler_params=_params(("parallel",)),
    )(u, mkv, mkv)


def _mem_bwd(u, mkv, o, do, lse, tq=512):
    s = u.shape[0]
    scale = HD ** -0.5

    def body(q_ref, mk_ref, mv_ref, o_ref, do_ref, lse_ref, dq_ref, dmk_ref, dmv_ref):
        @pl.when(pl.program_id(0) == 0)
        def _():
            dmk_ref[...] = jnp.zeros_like(dmk_ref)
            dmv_ref[...] = jnp.zeros_like(dmv_ref)

        for h in range(4):
            cs = slice(h * HD, (h + 1) * HD)
            qv, dov = q_ref[:, cs], do_ref[:, cs]
            sc = _dot(qv, mk_ref[:, cs], NT) * scale
            p = jnp.exp(sc - lse_ref[:, h:h + 1])
            delta = jnp.sum(dov.astype(F32) * o_ref[:, cs].astype(F32), axis=-1, keepdims=True)
            ds = p * (_dot(dov, mv_ref[:, cs], NT) - delta)
            dsb = ds.astype(BF16)
            dq_ref[:, cs] = (_dot(dsb, mk_ref[:, cs]) * scale).astype(dq_ref.dtype)
            dmk_ref[:, cs] += _dot(dsb, qv, TN) * scale
            dmv_ref[:, cs] += _dot(p.astype(BF16), dov, TN)

    row = pl.BlockSpec((tq, 512), lambda i: (i, 0))
    acc = pl.BlockSpec((N_MEM, 512), lambda i: (0, 0))
    return pl.pallas_call(
        body, name="mem_bwd", grid=(s // tq,),
        in_specs=[pl.BlockSpec((tq, 512), lambda i: (i, C_QM // 512)),
                  pl.BlockSpec((N_MEM, 512), lambda i: (0, 0)),
                  pl.BlockSpec((N_MEM, 512), lambda i: (0, 1)),
                  row, row, pl.BlockSpec((tq, HD), lambda i: (i, 0))],
        out_specs=[row, acc, acc],
        out_shape=[jax.ShapeDtypeStruct((s, 512), BF16), jax.ShapeDtypeStruct((N_MEM, 512), F32),
                   jax.ShapeDtypeStruct((N_MEM, 512), F32)],
        compiler_params=_params(("arbitrary",)),
    )(u, mkv, mkv, o, do, lse)


def _heads_major(a, col0):
    s = a.shape[0]
    return a[:, col0:col0 + 512].reshape(s, B_HEADS, B_HD).transpose(1, 0, 2)


def _token_major(a):
    h, s, dh = a.shape
    return a.transpose(1, 0, 2).reshape(s, h * dh)


def _class_view(a, d):
    s, c = a.shape
    return a.reshape(s // d, d * c)


def _local_step(x, mem, pos, target, g_pre, g_post, g_mem, w_main, w_fb, b_forget, b_merge,
                w_mem_kv, w_ba, w_bb, w_bm, w_out, exchange=None):
    s = x.shape[0]
    t_fox = min(512, s)
    nt = s // t_fox
    half = ROT_DIM // 2
    inv = ROPE_THETA ** (-jnp.arange(half, dtype=F32) / half)
    inv128 = jnp.concatenate([inv, inv, jnp.zeros((HD - ROT_DIM,), F32)]).reshape(1, HD)

    h = _rms_fwd("norm_pre", x, g_pre)
    u = _mm("proj_in", h, w_main, "nn", BF16)
    ufb = _mm("proj_fb", h, w_fb, "nn", F32)
    memn = _rms_fwd("norm_mem", mem, g_mem)
    mkv = _mm("proj_mem", memn, w_mem_kv, "nn", BF16)

    qkv = _rope_fwd(u, pos, inv128)
    views = [tuple(qkv[3 * g:3 * g + 3]) for g in range(3)]
    os_, lses = [], []
    for g, d in enumerate(DILATIONS):
        o_g, lse_g = _band_fwd("band_fwd%d" % g, *views[g], d)
        os_.append((o_g, d * A_GROUP, 0, d))
        lses.append((lse_g, d * HD, 0, d))

    def merge_a(o1, o2, o3, l1, l2, l3, za, *scr):
        o1, o2, o3 = [_from_class(o, scr, d) for o, d in zip((o1, o2, o3), DILATIONS)]
        l1, l2, l3 = [_from_class(lv, scr, d) for lv, d in zip((l1, l2, l3), DILATIONS)]
        ys, tots = [], []
        for hh in range(4):
            cs, hs = slice(hh * HD, (hh + 1) * HD), slice(hh, hh + 1)
            mx = jnp.maximum(jnp.maximum(l1[:, hs], l2[:, hs]), l3[:, hs])
            e1, e2, e3 = jnp.exp(l1[:, hs] - mx), jnp.exp(l2[:, hs] - mx), jnp.exp(l3[:, hs] - mx)
            den = e1 + e2 + e3
            ys.append((e1 * o1[:, cs] + e2 * o2[:, cs] + e3 * o3[:, cs]) / den)
            tots.append(mx + jnp.log(den))
        y = jnp.concatenate(ys, axis=1)
        zf = za.astype(F32)
        tot = _lane_pack(tots, l1.shape)
        return (y, y * (zf * _sig(zf))) + tuple(_to_class(tot, scr, d) for d in DILATIONS)

    res = _rows("merge_a", merge_a, os_ + lses + [(u, 512, C_ZA // 512)], [],
                [(512, BF16), (512, BF16)] + [(d * HD, F32, d) for d in DILATIONS], tm=ROPE_TM,
                scratch=_class_scratch(ROPE_TM))
    y_a, yg_a, lse_a = res[0], res[1], res[2:5]

    zrow = ufb[:, :B_HEADS].T
    c = _fox_prep(zrow, b_forget.reshape(B_HEADS, 1))
    n_hs = B_HEADS * s

    def wide(a, fill):
        return jnp.pad(a, ((0, 0), (0, 0), (0, FOX_W - B_HD)), constant_values=fill)

    def tiles_t(a):
        return a.reshape(B_HEADS, nt, t_fox, FOX_W).transpose(0, 1, 3, 2)

    qf = wide(_heads_major(u, C_QB), B_HD ** 0.5)
    vb = wide(_heads_major(u, C_VB), 1.0)
    kb, ks = _fox_aug_k(wide(_heads_major(u, C_KB), 0.0).reshape(n_hs, FOX_W), c.reshape(n_hs, 1))
    kb, ks = kb.reshape(B_HEADS, s, FOX_W), ks.reshape(B_HEADS, s, FOX_W)
    bounds = _fox_bounds(qf, kb, c, t_fox)
    ot, lse_b = _fox_fwd(qf, kb, tiles_t(vb), bounds, t_fox)
    y_b = ot.transpose(2, 0, 1).reshape(s, B_HEADS * B_HD)

    y_m, lse_m = _mem_fwd(u, mkv)

    def gate(y, z):
        zf = z.astype(F32)
        return (y.astype(F32) * (zf * _sig(zf)),)

    yg_b = _rows("gate_b", gate, [y_b, (u, 512, C_ZB // 512)], [], [(512, BF16)])[0]
    yg_m = _rows("gate_m", gate, [y_m, (u, 512, C_ZM // 512)], [], [(512, BF16)])[0]

    br_a = _mm("branch_a", yg_a, w_ba, "nn", BF16)
    br_b = _mm("branch_b", yg_b, w_bb, "nn", BF16)
    br_m = _mm("branch_m", yg_m, w_bm, "nn", BF16)
    gl = [(u, 1024, C_GL // 1024 + i) for i in range(3)]
    bm3 = b_merge.reshape(3, D_MODEL)

    def merge(g0, g1, g2, b0, b1, b2, bm):
        tot = 0.0
        for i, (gv, bv) in enumerate(((g0, b0), (g1, b1), (g2, b2))):
            tot = tot + _sig(gv.astype(F32) + bm[i:i + 1, :]) * bv.astype(F32)
        return (tot,)

    merged = _rows("merge_gates", merge, gl + [br_a, br_b, br_m], [bm3], [(D_MODEL, BF16)])[0]
    out = _mm("proj_out", merged, w_out, "nn", F32)

    def tail(xv, ov, tv, gv):
        r = lax.rsqrt(jnp.mean(ov * ov, axis=-1, keepdims=True) + EPS)
        n = ov * r
        err = xv + n * gv - tv
        dy = err * (1.0 / D_MODEL)
        dn = dy * gv
        dout = r * (dn - n * jnp.mean(dn * n, axis=-1, keepdims=True))
        return (dy, dout, jnp.sum(0.5 * err * err * (1.0 / D_MODEL), axis=0, keepdims=True),
                jnp.sum(dy * n, axis=0, keepdims=True))

    dy, dout, loss_lanes, g_post_grad = _rows(
        "tail", tail, [x, out, target], [g_post], [(D_MODEL, F32), (D_MODEL, BF16)],
        reds=[D_MODEL, D_MODEL], tm=256)

    dmerged = _mm("d_merged", dout, w_out, "nt", BF16)
    gw_out = _mm("g_w_out", merged, dout, "tn", F32)

    def merge_bwd(dm, g0, g1, g2, b0, b1, b2, bm):
        dmf = dm.astype(F32)
        dbs, dgs, sums = [], [], []
        for i, (gv, bv) in enumerate(((g0, b0), (g1, b1), (g2, b2))):
            sg = _sig(gv.astype(F32) + bm[i:i + 1, :])
            dbs.append(dmf * sg)
            dg = dmf * bv.astype(F32) * sg * (1.0 - sg)
            dgs.append(dg)
            sums.append(jnp.sum(dg, axis=0, keepdims=True))
        return tuple(dbs + dgs + sums)

    res = _rows("merge_bwd", merge_bwd, [dmerged] + gl + [br_a, br_b, br_m], [bm3],
                [(D_MODEL, BF16)] * 6, reds=[D_MODEL] * 3, tm=256)
    dbr, dgl, g_bmerge = res[0:3], res[3:6], jnp.concatenate(res[6:9], axis=1)

    dyg, gw_branch = [], []
    for nm, dbv, wv, ygv in (("a", dbr[0], w_ba, yg_a), ("b", dbr[1], w_bb, yg_b), ("m", dbr[2], w_bm, yg_m)):
        dyg.append(_mm("d_yg_" + nm, dbv, wv, "nt", BF16))
        gw_branch.append(_mm("g_w_branch_" + nm, ygv, dbv, "tn", F32))

    def gate_bwd(dg, y, z):
        dgf, yf, zf = dg.astype(F32), y.astype(F32), z.astype(F32)
        sg = _sig(zf)
        return dgf * (zf * sg), dgf * yf * (sg * (1.0 + zf * (1.0 - sg)))

    def gate_bwd_a(dg, y, z, *scr):
        dyv, dz = gate_bwd(dg, y, z)
        prod = dyv * y.astype(F32)
        dl = [jnp.sum(prod[:, hh * HD:(hh + 1) * HD], axis=-1, keepdims=True) for hh in range(4)]
        delta = _lane_pack(dl, (dg.shape[0], HD))
        return ((dz,) + tuple(_to_class(dyv, scr, d) for d in DILATIONS)
                + tuple(_to_class(delta, scr, d) for d in DILATIONS))

    res = _rows("gate_bwd_a", gate_bwd_a, [dyg[0], y_a, (u, 512, C_ZA // 512)], [],
                [(512, BF16)] + [(d * A_GROUP, BF16, d) for d in DILATIONS] + [(d * HD, F32, d) for d in DILATIONS],
                tm=ROPE_TM, scratch=_class_scratch(ROPE_TM))
    dz_a, dy_a, delta_a = res[0], res[1:4], res[4:7]
    dy_b, dz_b = _rows("gate_bwd_b", gate_bwd, [dyg[1], y_b, (u, 512, C_ZB // 512)], [],
                       [(512, BF16), (512, BF16)])
    dy_m, dz_m = _rows("gate_bwd_m", gate_bwd, [dyg[2], y_m, (u, 512, C_ZM // 512)], [],
                       [(512, BF16), (512, BF16)])

    dq_m, dmk, dmv = _mem_bwd(u, mkv, y_m, dy_m, lse_m)
    dmkv = jnp.concatenate([dmk, dmv], axis=1)
    gw_mem_kv = _mm("g_w_mem_kv", memn, dmkv, "tn", F32)
    dmemn = _mm("d_memn", dmkv, w_mem_kv, "nt", F32)

    def mem_gain_grad(mv, dv):
        r = lax.rsqrt(jnp.mean(mv * mv, axis=-1, keepdims=True) + EPS)
        return (jnp.sum(dv * mv * r, axis=0, keepdims=True),)

    g_mem_grad = _rows("g_norm_mem", mem_gain_grad, [mem, dmemn], [], [], reds=[D_MODEL], tm=N_MEM)[0]

    dob = _heads_major(dy_b, 0)

    def fox_delta(a, b):
        return (jnp.sum(a.astype(F32) * b.astype(F32), axis=-1, keepdims=True),)

    ob = ot.transpose(0, 2, 1).reshape(n_hs, B_HD)
    delta_b = _rows("fox_delta", fox_delta, [dob.reshape(n_hs, B_HD), ob], [], [(1, F32)], tm=min(2048, s))[0]
    dqt, dkw, dvw = _fox_bwd(qf, wide(dob, 0.0), lse_b.reshape(B_HEADS, nt, 1, t_fox),
                             delta_b.reshape(B_HEADS, nt, 1, t_fox), kb, tiles_t(ks), vb, bounds, t_fox)
    dqb = dqt[:, :, :B_HD, :].transpose(0, 1, 3, 2).reshape(B_HEADS, s, B_HD)
    dkb, dvb = dkw[:, :, :B_HD].astype(BF16), dvw[:, :, :B_HD].astype(BF16)
    dzrow, g_bforget = _fox_prep_bwd(dkw[:, :, B_HD], dqt[:, :, FOX_ONE, :].reshape(B_HEADS, s), zrow,
                                     b_forget.reshape(B_HEADS, 1))
    dfb = jnp.zeros((s, HD), BF16).at[:, :B_HEADS].set(dzrow.T.astype(BF16))

    dqs, dks, dvs = [], [], []
    for g, d in enumerate(DILATIONS):
        qv, kv, vv = views[g]
        dqs.append(_band_dq("band_dq%d" % g, qv, kv, vv, dy_a[g], lse_a[g], delta_a[g], d))
        dk_g, dv_g = _band_dkv("band_dkv%d" % g, qv, kv, vv, dy_a[g], lse_a[g], delta_a[g], d)
        dks.append(dk_g)
        dvs.append(dv_g)
    dqa, dka, dva = _rope_bwd(dqs, dks, dvs, pos, inv128)

    du = jnp.concatenate(
        [dqa, dka, dva, dz_a, _token_major(dqb).astype(BF16), _token_major(dkb), _token_major(dvb),
                            dz_b, dq_m, dz_m] + list(dgl), axis=1)

    gw_main = _mm("g_w_main", h.T, du, "nn", F32, tk=2048)
    gw_fb = _mm("g_w_fb", h, dfb, "tn", F32)
    gw_in = jnp.concatenate([gw_main[:, :FB_ORIG], gw_fb[:, :B_HEADS], gw_main[:, FB_ORIG:]], axis=1)
    grads = dict(norm_post_g=g_post_grad, norm_mem_g=g_mem_grad, w_in=gw_in,
                 b_forget=g_bforget.reshape(1, B_HEADS), b_merge=g_bmerge, w_mem_kv=gw_mem_kv,
                 w_branch_a=gw_branch[0], w_branch_b=gw_branch[1], w_branch_m=gw_branch[2], w_out=gw_out)
    side = exchange(grads) if exchange else None
    dh_main = _mm("d_h", du, w_main, "nt", F32, tk=2816, side=side)
    landed = None
    if side:
        dh_main, landed = dh_main[0], dh_main[1:]
    dh_fb = _mm("d_h_fb", dfb, w_fb, "nt", F32)

    def pre_bwd(xv, d1, d2, dyv, gv):
        r = lax.rsqrt(jnp.mean(xv * xv, axis=-1, keepdims=True) + EPS)
        n = xv * r
        dhv = d1 + d2
        dn = dhv * gv
        dx = r * (dn - n * jnp.mean(dn * n, axis=-1, keepdims=True))
        return dyv + dx, jnp.sum(dhv * n, axis=0, keepdims=True)

    grad_x, g_pre_grad = _rows("norm_pre_bwd", pre_bwd, [x, dh_main, dh_fb, dy], [g_pre],
                               [(D_MODEL, F32)], reds=[D_MODEL], tm=256)

    grads["norm_pre_g"] = g_pre_grad
    return loss_lanes, grad_x, grads, landed


HBM_SPEC = pl.BlockSpec(memory_space=pltpu.HBM)


def _place():
    x, y, c = lax.axis_index("x"), lax.axis_index("y"), lax.axis_index("c")
    chips = [(1 - x, y), (x, 1 - y), (1 - x, 1 - y)]
    return x, y, c, 2 * x + y, chips


N_CHUNKS = 4


def _units(parts, row_axis):
    units = []
    for i, a in enumerate(parts):
        ch = a.shape[row_axis] // N_CHUNKS
        units += [(i, pl.ds(k * ch, ch)) for k in range(N_CHUNKS)]
    return units


def _gather_weights(parts):
    n = len(parts)
    units = _units(parts, 1)
    nu = len(units)

    def body(*refs):
        srcs, outs = refs[:n], refs[n:2 * n]
        send_sems, recv_sems = refs[2 * n:]
        x, y, c, p, chips = _place()
        me, sib = (x, y, c), (x, y, 1 - c)

        def cp(u, k, chip, half, to, from_src=False):
            i, rs = units[u]
            dst = outs[i].at[chip, half, rs]
            return pltpu.make_async_remote_copy(
                src_ref=srcs[i].at[half, rs] if from_src else dst, dst_ref=dst, send_sem=send_sems.at[u, k],
                recv_sem=recv_sems.at[u, k], device_id=to, device_id_type=MESH)

        first = [cp(u, j, p, c, (cx, cy, c), from_src=True)
                 for u in range(nu) for j, (cx, cy) in enumerate(chips)]
        for f in first:
            f.start()
        passed = []
        for u in range(nu):
            for j, (cx, cy) in enumerate(chips):
                cp(u, j, 2 * cx + cy, c, me).wait_recv()
                fw = cp(u, 3 + j, 2 * cx + cy, c, sib)
                fw.start()
                passed.append(fw)
        for u in range(nu):
            for j, (cx, cy) in enumerate(chips):
                cp(u, 3 + j, 2 * cx + cy, 1 - c, me).wait_recv()
        for f in first + passed:
            f.wait_send()

    return pl.pallas_call(
        body, name="gather_weights", in_specs=[HBM_SPEC] * n, out_specs=[HBM_SPEC] * n,
        out_shape=[jax.ShapeDtypeStruct((N_CHIPS,) + a.shape, a.dtype) for a in parts],
        scratch_shapes=[pltpu.SemaphoreType.DMA((nu, 6)), pltpu.SemaphoreType.DMA((nu, 6))],
    )(*parts)


def _swap_with_sibling(parts):
    n = len(parts)
    units = _units(parts, 2)

    def body(*refs):
        srcs, outs = refs[:n], refs[n:2 * n]
        send_sems, recv_sems = refs[2 * n:]
        x, y, c, _, _ = _place()
        cps = [pltpu.make_async_remote_copy(
            src_ref=srcs[i].at[q, 1 - c, rs], dst_ref=outs[i].at[q, rs], send_sem=send_sems.at[u, q],
            recv_sem=recv_sems.at[u, q], device_id=(x, y, 1 - c), device_id_type=MESH)
            for q in range(N_CHIPS) for u, (i, rs) in enumerate(units)]
        for cpy in cps:
            cpy.start()
        for cpy in cps:
            cpy.wait()

    return pl.pallas_call(
        body, name="swap_with_sibling", in_specs=[HBM_SPEC] * n, out_specs=[HBM_SPEC] * n,
        out_shape=[jax.ShapeDtypeStruct(a.shape[:1] + a.shape[2:], a.dtype) for a in parts],
        scratch_shapes=[pltpu.SemaphoreType.DMA((len(units), N_CHIPS)),
                        pltpu.SemaphoreType.DMA((len(units), N_CHIPS))],
    )(*parts)


def _scatter_to_owners(parts):
    n = len(parts)
    units = _units(parts, 1)

    def copies(srcs, outs, send_sems, recv_sems, incoming):
        x, y, c, p, chips = _place()
        return [pltpu.make_async_remote_copy(
            src_ref=srcs[i].at[2 * cx + cy, rs], dst_ref=outs[i].at[(2 * cx + cy) if incoming else p, rs],
            send_sem=send_sems.at[u, j], recv_sem=recv_sems.at[u, j], device_id=(cx, cy, c), device_id_type=MESH)
            for u, (i, rs) in enumerate(units) for j, (cx, cy) in enumerate(chips)]

    def start(ins, outs, scratch):
        for cpy in copies(ins, outs, *scratch, incoming=False):
            cpy.start()

    def wait(ins, outs, scratch):
        for cpy in copies(ins, outs, *scratch, incoming=True):
            cpy.wait_recv()
        for cpy in copies(ins, outs, *scratch, incoming=False):
            cpy.wait_send()

    return dict(ins=list(parts), outs=[jax.ShapeDtypeStruct(a.shape, a.dtype) for a in parts],
                scratch=[pltpu.SemaphoreType.DMA((len(units), 3)), pltpu.SemaphoreType.DMA((len(units), 3))],
                start=start, wait=wait)


def _share_with_sibling(parts):
    n = len(parts)
    units = _units(parts, 1)

    def body(*refs):
        srcs, outs = refs[:n], refs[n:2 * n]
        send_sems, recv_sems = refs[2 * n:]
        x, y, c, _, _ = _place()
        sends = [pltpu.make_async_remote_copy(
            src_ref=srcs[i].at[0, rs], dst_ref=outs[i].at[c, rs], send_sem=send_sems.at[u],
            recv_sem=recv_sems.at[u], device_id=(x, y, 1 - c), device_id_type=MESH)
            for u, (i, rs) in enumerate(units)]
        for cpy in sends:
            cpy.start()
        for u, (i, rs) in enumerate(units):
            pltpu.make_async_remote_copy(
                src_ref=srcs[i].at[0, rs], dst_ref=outs[i].at[1 - c, rs], send_sem=send_sems.at[u],
                recv_sem=recv_sems.at[u], device_id=(x, y, 1 - c), device_id_type=MESH).wait_recv()
        for cpy in sends:
            cpy.wait_send()

    return pl.pallas_call(
        body, name="share_with_sibling", in_specs=[HBM_SPEC] * n, out_specs=[HBM_SPEC] * n,
        out_shape=[jax.ShapeDtypeStruct((2,) + a.shape[1:], a.dtype) for a in parts],
        scratch_shapes=[pltpu.SemaphoreType.DMA((len(units),)), pltpu.SemaphoreType.DMA((len(units),))],
    )(*parts)


def _sum_small(v):
    def body(v_ref, out_ref, buf, send_sems, recv_sems):
        x, y, c, _, _ = _place()
        me = 4 * x + 2 * y + c
        buf[me] = v_ref[...]
        flips = [(dx, dy, dc) for dx in (0, 1) for dy in (0, 1) for dc in (0, 1)][1:]
        sends = []
        for k, (dx, dy, dc) in enumerate(flips):
            cpy = pltpu.make_async_remote_copy(
                src_ref=v_ref, dst_ref=buf.at[me], send_sem=send_sems.at[k], recv_sem=recv_sems.at[k],
                device_id=((x + dx) % 2, (y + dy) % 2, (c + dc) % 2), device_id_type=MESH)
            cpy.start()
            sends.append(cpy)
        for k, (dx, dy, dc) in enumerate(flips):
            px, py, pc = (x + dx) % 2, (y + dy) % 2, (c + dc) % 2
            pltpu.make_async_remote_copy(
                src_ref=v_ref, dst_ref=buf.at[4 * px + 2 * py + pc], send_sem=send_sems.at[k],
                recv_sem=recv_sems.at[k], device_id=(px, py, pc), device_id_type=MESH).wait_recv()
        for cpy in sends:
            cpy.wait_send()
        tot = buf[0]
        for i in range(1, N_DEV):
            tot = tot + buf[i]
        out_ref[...] = tot

    return pl.pallas_call(
        body, name="sum_small", out_shape=jax.ShapeDtypeStruct(v.shape, v.dtype),
        in_specs=[pl.BlockSpec(memory_space=pltpu.VMEM)], out_specs=pl.BlockSpec(memory_space=pltpu.VMEM),
        scratch_shapes=[pltpu.VMEM((N_DEV,) + v.shape, v.dtype), pltpu.SemaphoreType.DMA((N_DEV - 1,)),
                        pltpu.SemaphoreType.DMA((N_DEV - 1,))],
    )(v)


def _add_slabs(name, terms, out_dtype):
    arr0 = terms[0][0]
    n = arr0.shape[0] if terms[0][1] is None else 1
    _, r, w = arr0.shape
    tr = 64
    specs = []
    for _, slab in terms:
        if slab is None:
            specs.append(pl.BlockSpec((None, tr, w), lambda i, j: (i, j, 0)))
        else:
            specs.append(pl.BlockSpec((None, tr, w), functools.partial(lambda i, j, sl: (sl, j, 0), sl=slab)))

    def body(*refs):
        tot = refs[0][...].astype(F32)
        for rf in refs[1:-1]:
            tot = tot + rf[...].astype(F32)
        refs[-1][...] = tot.astype(out_dtype)

    return pl.pallas_call(
        body, name=name, grid=(n, r // tr), in_specs=specs,
        out_specs=pl.BlockSpec((None, tr, w), lambda i, j: (i, j, 0)),
        out_shape=jax.ShapeDtypeStruct((n, r, w), out_dtype),
        compiler_params=_params(("parallel", "parallel")),
    )(*[a for a, _ in terms])


def _add_pair(name, halves, got, c):
    nq, _, r, w = halves.shape
    tr = 64

    def body(c_ref, a_ref, b_ref, o_ref):
        o_ref[...] = (a_ref[...] + b_ref[...]).astype(o_ref.dtype)

    grid_spec = pltpu.PrefetchScalarGridSpec(
        num_scalar_prefetch=1, grid=(nq, r // tr),
        in_specs=[pl.BlockSpec((None, None, tr, w), lambda i, j, c_ref: (i, c_ref[0], j, 0)),
                  pl.BlockSpec((None, tr, w), lambda i, j, c_ref: (i, j, 0))],
        out_specs=pl.BlockSpec((None, tr, w), lambda i, j, c_ref: (i, j, 0)))
    return pl.pallas_call(
        body, name=name, grid_spec=grid_spec, out_shape=jax.ShapeDtypeStruct((nq, r, w), BF16),
        compiler_params=_params(("parallel", "parallel")),
    )(jnp.reshape(c, (1,)).astype(jnp.int32), halves, got)


def _adamw(name, w, g, m, v, tm):
    def fn(wv, gv, mv, vv):
        m2 = ADAM_B1 * mv + (1.0 - ADAM_B1) * gv
        v2 = ADAM_B2 * vv + (1.0 - ADAM_B2) * (gv * gv)
        m_hat = m2 / (1.0 - ADAM_B1 ** ADAM_STEP)
        v_hat = v2 / (1.0 - ADAM_B2 ** ADAM_STEP)
        return -ADAM_LR * (m_hat / (jnp.sqrt(v_hat) + ADAM_EPS) + ADAM_WD * wv), m2, v2
    c = w.shape[1]
    return _rows(name, fn, [w, g, m, v], [], [(c, F32)] * 3, tm=tm)


REST_ROWS = 256 + 3 * 128 + 256
REST_SPLITS = (("w_mem_kv", 0, 256), ("w_branch_a", 256, 128), ("w_branch_b", 384, 128),
               ("w_branch_m", 512, 128), ("w_out", 640, 256))


def _rest_pack(t):
    return jnp.concatenate([t[n].reshape(rows, D_MODEL) for n, _, rows in REST_SPLITS], axis=0)


def _rest_unpack(a, shapes):
    return {n: a[r0:r0 + rows].reshape(shapes[n]) for n, r0, rows in REST_SPLITS}


def _small_pack(pre, post, memg, bforget, bmerge):
    pad = jnp.zeros((1, D_MODEL - B_HEADS), F32)
    return jnp.concatenate([pre, post, memg, bmerge.reshape(3, D_MODEL),
                            jnp.concatenate([bforget, pad], axis=1), jnp.zeros((1, D_MODEL), F32)], axis=0)


def _small_unpack(s8):
    return dict(norm_pre_g=s8[0:1], norm_post_g=s8[1:2], norm_mem_g=s8[2:3],
                b_merge=s8[3:6].reshape(1, 3 * D_MODEL), b_forget=s8[6:7, :B_HEADS])


WEIGHTS = ("norm_pre_g", "norm_post_g", "norm_mem_g", "w_in", "b_forget", "b_merge", "w_mem_kv",
           "w_branch_a", "w_branch_b", "w_branch_m", "w_out")
SMALL = ("norm_pre_g", "norm_post_g", "norm_mem_g", "b_forget", "b_merge")


def kernel(x, mem, positions, norm_pre_g, norm_post_g, norm_mem_g, w_in, b_forget, b_merge, w_mem_kv, w_branch_a, w_branch_b, w_branch_m, w_out, loss_target, m_norm_pre_g, m_norm_post_g, m_norm_mem_g, m_w_in, m_b_forget, m_b_merge, m_w_mem_kv, m_w_branch_a, m_w_branch_b, m_w_branch_m, m_w_out, v_norm_pre_g, v_norm_post_g, v_norm_mem_g, v_w_in, v_b_forget, v_b_merge, v_w_mem_kv, v_w_branch_a, v_w_branch_b, v_w_branch_m, v_w_out):
    w = dict(norm_pre_g=norm_pre_g, norm_post_g=norm_post_g, norm_mem_g=norm_mem_g, w_in=w_in[0],
             b_forget=b_forget, b_merge=b_merge, w_mem_kv=w_mem_kv[0], w_branch_a=w_branch_a[0],
             w_branch_b=w_branch_b[0], w_branch_m=w_branch_m[0], w_out=w_out[0])
    mo = dict(norm_pre_g=m_norm_pre_g, norm_post_g=m_norm_post_g, norm_mem_g=m_norm_mem_g, w_in=m_w_in[0],
              b_forget=m_b_forget, b_merge=m_b_merge, w_mem_kv=m_w_mem_kv[0], w_branch_a=m_w_branch_a[0],
              w_branch_b=m_w_branch_b[0], w_branch_m=m_w_branch_m[0], w_out=m_w_out[0])
    vo = dict(norm_pre_g=v_norm_pre_g, norm_post_g=v_norm_post_g, norm_mem_g=v_norm_mem_g, w_in=v_w_in[0],
              b_forget=v_b_forget, b_merge=v_b_merge, w_mem_kv=v_w_mem_kv[0], w_branch_a=v_w_branch_a[0],
              w_branch_b=v_w_branch_b[0], w_branch_m=v_w_branch_m[0], w_out=v_w_out[0])
    s = x.shape[1]
    c = lax.axis_index("c")

    chip = 2 * lax.axis_index("x") + lax.axis_index("y")

    def put(whole, own, slot):
        return lax.dynamic_update_index_in_dim(whole, own.astype(whole.dtype), slot, 0)

    own_w = [w["w_in"].astype(BF16).reshape(2, D_MODEL // 2, SHARD_COLS),
             _rest_pack(w).astype(BF16).reshape(2, REST_ROWS // 2, D_MODEL)]
    all_in, all_rest = [put(a, o, chip) for a, o in zip(_gather_weights(own_w), own_w)]
    all_in = all_in.reshape(N_CHIPS, D_MODEL, SHARD_COLS)
    w_in_f = jnp.concatenate([all_in[p] for p in range(N_CHIPS)], axis=1)
    all_rest = all_rest.reshape(N_CHIPS, REST_ROWS, D_MODEL)
    w_kv_f = all_rest[:, 0:256].reshape(D_MODEL, D_MODEL)
    w_br_f = [all_rest[:, 256 + 128 * i:384 + 128 * i].reshape(N_CHIPS, 512, 256).transpose(1, 0, 2)
              .reshape(512, D_MODEL) for i in range(3)]
    w_out_f = all_rest[:, 640:896].reshape(D_MODEL, D_MODEL)
    w_main = jnp.concatenate([w_in_f[:, :FB_ORIG], w_in_f[:, FB_ORIG + B_HEADS:]], axis=1)
    w_fb = jnp.concatenate([w_in_f[:, FB_ORIG:FB_ORIG + B_HEADS], jnp.zeros((D_MODEL, HD - B_HEADS), BF16)], axis=1)

    pair = []

    def exchange(g):
        def per_chip(name, p):
            a = g[name]
            if name in ("w_mem_kv", "w_out"):
                return a[256 * p:256 * (p + 1)]
            return a[:, 256 * p:256 * (p + 1)]

        in4 = jnp.stack([g["w_in"][:, SHARD_COLS * p:SHARD_COLS * (p + 1)] for p in range(N_CHIPS)])
        rest4 = jnp.stack([_rest_pack({n: per_chip(n, p) for n, _, _ in REST_SPLITS}) for p in range(N_CHIPS)])
        halves = [in4.reshape(N_CHIPS, 2, D_MODEL // 2, SHARD_COLS),
                  rest4.reshape(N_CHIPS, 2, REST_ROWS // 2, D_MODEL)]
        got = _swap_with_sibling(halves)
        pair.extend(_add_pair("add_pair_%d" % i, halves[i], got[i], c) for i in range(2))
        return _scatter_to_owners(pair)

    loss_lanes, grad_x, g, landed = _local_step(
        x[0], mem[0], positions.reshape(s, 1), loss_target[0], norm_pre_g, norm_post_g, norm_mem_g,
        w_main, w_fb, b_forget, b_merge, w_kv_f, w_br_f[0], w_br_f[1], w_br_f[2], w_out_f, exchange)
    loss = lax.psum(jnp.sum(loss_lanes), ("x", "y", "c"))
    landed = [put(a, lax.dynamic_index_in_dim(o, chip, 0, keepdims=False), chip) for a, o in zip(landed, pair)]
    half = [_add_slabs("add_chips_%d" % i, [(landed[i], q) for q in range(N_CHIPS)], F32) for i in range(2)]
    red_in, red_rest = [put(a, o[0], c) for a, o in zip(_share_with_sibling(half), half)]
    gs = {"w_in": red_in.reshape(D_MODEL, SHARD_COLS)}
    gs.update(_rest_unpack(red_rest.reshape(REST_ROWS, D_MODEL), {n: w[n].shape for n, _, _ in REST_SPLITS}))
    gs.update(_small_unpack(_sum_small(_small_pack(
        g["norm_pre_g"], g["norm_post_g"], g["norm_mem_g"], g["b_forget"], g["b_merge"]))))

    delta, new_m, new_v = {}, {}, {}
    for n, tm in (("w_in", 128), ("w_mem_kv", 256), ("w_branch_a", 512), ("w_branch_b", 512),
                  ("w_branch_m", 512), ("w_out", 256)):
        d_, m_, v_ = _adamw("adamw_" + n, w[n], gs[n], mo[n], vo[n], tm)
        delta[n], new_m[n], new_v[n] = d_[None], m_[None], v_[None]
        gs[n] = gs[n][None]
    packs = [_small_pack(*[t[n] for n in ("norm_pre_g", "norm_post_g", "norm_mem_g", "b_forget", "b_merge")])
             for t in (w, gs, mo, vo)]
    for res, store in zip(_adamw("adamw_small", *packs, 8), (delta, new_m, new_v)):
        store.update(_small_unpack(res))

    return (loss, grad_x[None], *[gs[n] for n in WEIGHTS], *[delta[n] for n in WEIGHTS],
            *[new_m[n] for n in WEIGHTS], *[new_v[n] for n in WEIGHTS])
```

```python
import functools

import jax
import jax.numpy as jnp
from jax import lax
from jax.experimental import pallas as pl
from jax.experimental.pallas import tpu as pltpu

F32 = jnp.float32
BF16 = jnp.bfloat16
MESH = pl.DeviceIdType.MESH

D_MODEL = 1024
N_MEM = 256
EPS = 1e-6
NEG = -1e30
ROPE_THETA = 500000.0
ROT_DIM = 32
HD = 128
A_GROUP = 512
DILATIONS = (1, 4, 16)
BAND = 128
B_HEADS = 8
B_HD = 64
N_CHIPS = 4
N_DEV = 8

C_QA, C_KA, C_VA, C_ZA = 0, 1536, 3072, 4608
C_QB, C_KB, C_VB, C_ZB = 5120, 5632, 6144, 6656
C_QM, C_ZM, C_GL = 7168, 7680, 8192
N_MAIN = 11264
FB_ORIG = 6656
IN_COLS = 11272
SHARD_COLS = IN_COLS // N_CHIPS

ADAM_LR, ADAM_B1, ADAM_B2, ADAM_EPS, ADAM_WD, ADAM_STEP = 0.001, 0.9, 0.999, 1e-08, 0.01, 10

VMEM_LIMIT_V7X = 56 * 1024 * 1024

NT = (((1,), (1,)), ((), ()))
NN = (((1,), (0,)), ((), ()))
TN = (((0,), (0,)), ((), ()))


def _params(sem):
    return pltpu.CompilerParams(dimension_semantics=sem, vmem_limit_bytes=VMEM_LIMIT_V7X)


def _dot(a, b, dn=NN):
    return lax.dot_general(a, b, dn, preferred_element_type=F32)


def _sig(z):
    return 1.0 / (1.0 + jnp.exp(-z))


def _rows(name, fn, row_ins, bc_ins, outs, reds=(), tm=512, scratch=()):
    arrs, specs = [], []
    s = None
    for r in row_ins:
        arr, w, cb, d = (tuple(r) + (1,))[:4] if isinstance(r, tuple) else (r, r.shape[1], 0, 1)
        s = arr.shape[0] * d if s is None else s
        arrs.append(arr)
        specs.append((w, cb, d))
    tm = min(tm, s)
    specs = [pl.BlockSpec((tm // d, w), functools.partial(lambda i, cb: (i, cb), cb=cb)) for w, cb, d in specs]
    for b in bc_ins:
        arrs.append(b)
        specs.append(pl.BlockSpec(b.shape, lambda i: (0, 0)))
    outs = [(tuple(o) + (1,))[:3] for o in outs]
    n_in, n_out = len(arrs), len(outs)

    def body(*refs):
        n_ref = n_in + n_out + len(reds)
        vals = fn(*[r[...] for r in refs[:n_in]], *refs[n_ref:])
        if not isinstance(vals, (tuple, list)):
            vals = (vals,)
        for r, v in zip(refs[n_in:n_in + n_out], vals[:n_out]):
            r[...] = v.astype(r.dtype)
        if reds:
            red_refs = refs[n_in + n_out:n_ref]

            @pl.when(pl.program_id(0) == 0)
            def _():
                for r in red_refs:
                    r[...] = jnp.zeros_like(r)

            for r, v in zip(red_refs, vals[n_out:]):
                r[...] += v

    out_shape = [jax.ShapeDtypeStruct((s // d, c), dt) for c, dt, d in outs]
    out_shape += [jax.ShapeDtypeStruct((1, c), F32) for c in reds]
    out_specs = [pl.BlockSpec((tm // d, c), lambda i: (i, 0)) for c, _, d in outs]
    out_specs += [pl.BlockSpec((1, c), lambda i: (0, 0)) for c in reds]
    res = pl.pallas_call(
        body, name=name, grid=(s // tm,), in_specs=specs, out_specs=out_specs, out_shape=out_shape,
        scratch_shapes=list(scratch),
        compiler_params=_params(("arbitrary",) if reds else ("parallel",)),
    )(*arrs)
    return res


def _to_class(x, scr, d):
    if d == 1:
        return x.astype(F32)
    tm, c = x.shape
    for g in range(c // 128):
        scr[g][...] = x[:, g * 128:(g + 1) * 128].astype(F32)
    return jnp.concatenate([scr[g][pl.ds(r, tm // d, stride=d), :] for r in range(d) for g in range(c // 128)],
                           axis=1)


def _from_class(x, scr, d):
    if d == 1:
        return x.astype(F32)
    n, dc = x.shape
    c = dc // d
    for r in range(d):
        for g in range(c // 128):
            scr[g][pl.ds(r, n, stride=d), :] = x[:, r * c + g * 128:r * c + (g + 1) * 128].astype(F32)
    return jnp.concatenate([scr[g][...] for g in range(c // 128)], axis=1)


def _mm(name, a, b, mode, out_dtype, tm=1024, tn=1024, tk=1024, side=None):
    if mode == "nn":
        (m, k), (_, n) = a.shape, b.shape
    elif mode == "nt":
        (m, k), (n, _) = a.shape, b.shape
    else:
        (k, m), (_, n) = a.shape, b.shape
    tm, tn, tk = min(tm, m), min(tn, n), min(tk, k)
    nk = k // tk
    grid = (m // tm, n // tn, nk)
    dn = {"nn": NN, "nt": NT, "tn": TN}[mode]
    n_si = len(side["ins"]) if side else 0
    n_so = len(side["outs"]) if side else 0
    n_acc = 1 if nk > 1 else 0

    def body(*refs):
        a_ref, b_ref = refs[:2]
        side_in, o_ref = refs[2:2 + n_si], refs[2 + n_si]
        side_out = refs[3 + n_si:3 + n_si + n_so]
        acc = refs[3 + n_si + n_so:3 + n_si + n_so + n_acc]
        side_scratch = refs[3 + n_si + n_so + n_acc:]
        step = (pl.program_id(0) * grid[1] + pl.program_id(1)) * grid[2] + pl.program_id(2)
        if side:
            @pl.when(step == 0)
            def _():
                side["start"](side_in, side_out, side_scratch)

        part = _dot(a_ref[...].astype(BF16), b_ref[...].astype(BF16), dn)
        if nk == 1:
            o_ref[...] = part.astype(o_ref.dtype)
        else:
            kk = pl.program_id(2)

            @pl.when(kk == 0)
            def _():
                acc[0][...] = part

            @pl.when(kk > 0)
            def _():
                acc[0][...] += part

            @pl.when(kk == nk - 1)
            def _():
                o_ref[...] = acc[0][...].astype(o_ref.dtype)

        if side:
            @pl.when(step == grid[0] * grid[1] * grid[2] - 1)
            def _():
                side["wait"](side_in, side_out, side_scratch)

    a_spec = (pl.BlockSpec((tk, tm), lambda i, j, kk: (kk, i)) if mode == "tn"
              else pl.BlockSpec((tm, tk), lambda i, j, kk: (i, kk)))
    b_spec = (pl.BlockSpec((tn, tk), lambda i, j, kk: (j, kk)) if mode == "nt"
              else pl.BlockSpec((tk, tn), lambda i, j, kk: (kk, j)))
    o_spec = pl.BlockSpec((tm, tn), lambda i, j, kk: (i, j))
    o_shape = jax.ShapeDtypeStruct((m, n), out_dtype)
    acc_scratch = [pltpu.VMEM((tm, tn), F32)] * n_acc
    if not side:
        return pl.pallas_call(
            body, name=name, grid=grid, in_specs=[a_spec, b_spec], out_specs=o_spec, out_shape=o_shape,
            scratch_shapes=acc_scratch, compiler_params=_params(("parallel", "parallel", "arbitrary")),
        )(a, b)
    return pl.pallas_call(
        body, name=name, grid=grid, in_specs=[a_spec, b_spec] + [HBM_SPEC] * n_si,
        out_specs=[o_spec] + [HBM_SPEC] * n_so, out_shape=[o_shape] + side["outs"],
        scratch_shapes=acc_scratch + side["scratch"],
        compiler_params=_params(("arbitrary", "arbitrary", "arbitrary")),
    )(a, b, *side["ins"])


def _rms_fwd(name, x, g):
    def fn(xv, gv):
        r = lax.rsqrt(jnp.mean(xv * xv, axis=-1, keepdims=True) + EPS)
        return (xv * r * gv,)
    return _rows(name, fn, [x], [g], [(x.shape[1], BF16)], tm=min(512, x.shape[0]))[0]


def _rope_tables(pos, inv):
    ang = pos.astype(F32) * inv
    lane = lax.broadcasted_iota(jnp.int32, ang.shape, 1)
    c = jnp.where(lane < ROT_DIM, jnp.cos(ang), 1.0)
    sn = jnp.sin(ang)
    sg = jnp.where(lane < ROT_DIM // 2, -sn, jnp.where(lane < ROT_DIM, sn, 0.0))
    return c, sg, lane


def _rope_apply(x, c, sg, lane):
    outs = []
    for h in range(x.shape[1] // HD):
        xh = x[:, h * HD:(h + 1) * HD].astype(F32)
        swap = jnp.where(lane < ROT_DIM // 2, pltpu.roll(xh, HD - ROT_DIM // 2, 1),
                         pltpu.roll(xh, ROT_DIM // 2, 1))
        outs.append(xh * c + swap * sg)
    return jnp.concatenate(outs, axis=1)


ROPE_TM = 256


def _class_scratch(tm):
    return [pltpu.VMEM((tm, 128), F32) for _ in range(A_GROUP // 128)]


def _rope_fwd(u, pos, inv):
    def fn(q, k, v, p, iv, *scr):
        c, sg, lane = _rope_tables(p, iv)
        qr, kr = _rope_apply(q, c, sg, lane), _rope_apply(k, c, sg, lane)
        outs = []
        for g, d in enumerate(DILATIONS):
            gs = slice(g * A_GROUP, (g + 1) * A_GROUP)
            outs += [_to_class(qr[:, gs], scr, d), _to_class(kr[:, gs], scr, d), _to_class(v[:, gs], scr, d)]
        return tuple(outs)

    outs = [(d * A_GROUP, BF16, d) for d in DILATIONS for _ in range(3)]
    return _rows("rope_fwd", fn, [(u, 1536, 0), (u, 1536, 1), (u, 1536, 2), pos], [inv], outs, tm=ROPE_TM,
                 scratch=_class_scratch(ROPE_TM))


def _rope_bwd(dqs, dks, dvs, pos, inv):
    def fn(*args):
        grads, p, iv, scr = args[:9], args[9], args[10], args[11:]
        c, sg, lane = _rope_tables(p, iv)
        tok = [jnp.concatenate([_from_class(grads[3 * k + g], scr, d) for g, d in enumerate(DILATIONS)], axis=1)
               for k in range(3)]
        return _rope_apply(tok[0], c, -sg, lane), _rope_apply(tok[1], c, -sg, lane), tok[2]

    ins = [(a, a.shape[1], 0, d) for grp in (dqs, dks, dvs) for a, d in zip(grp, DILATIONS)]
    return _rows("rope_bwd", fn, ins + [pos], [inv], [(1536, BF16)] * 3, tm=ROPE_TM,
                 scratch=_class_scratch(ROPE_TM))


def _lane_pack(cols, like):
    lane = lax.broadcasted_iota(jnp.int32, like, 1)
    out = jnp.zeros(like, F32)
    for h, cvec in enumerate(cols):
        out = jnp.where(lane == h, cvec, out)
    return out


def _band_specs(l, d, tq):
    nsb = tq // BAND
    nblk = l // BAND
    cur = pl.BlockSpec((tq, A_GROUP), lambda r, i: (i, r))
    prev = pl.BlockSpec((BAND, A_GROUP), lambda r, i: (jnp.maximum(i * nsb - 1, 0), r))
    nxt = pl.BlockSpec((BAND, A_GROUP), lambda r, i: (jnp.minimum((i + 1) * nsb, nblk - 1), r))
    st_cur = pl.BlockSpec((tq, HD), lambda r, i: (i, r))
    st_nxt = pl.BlockSpec((BAND, HD), lambda r, i: (jnp.minimum((i + 1) * nsb, nblk - 1), r))
    return nsb, cur, prev, nxt, st_cur, st_nxt


def _band_mask_q(i, first_tile):
    qr = lax.broadcasted_iota(jnp.int32, (BAND, 2 * BAND), 0)
    kc = lax.broadcasted_iota(jnp.int32, (BAND, 2 * BAND), 1)
    in_prev = (kc < BAND) & (kc >= qr)
    in_cur = (kc >= BAND) & (kc - BAND <= qr)
    if i == 0:
        in_prev = in_prev & jnp.logical_not(first_tile)
    return in_prev | in_cur


def _band_mask_k(j, nsb, last_tile):
    qr = lax.broadcasted_iota(jnp.int32, (2 * BAND, BAND), 0)
    kc = lax.broadcasted_iota(jnp.int32, (2 * BAND, BAND), 1)
    same = (qr < BAND) & (kc <= qr)
    nxt = (qr >= BAND) & (kc >= qr - BAND)
    if j == nsb - 1:
        nxt = nxt & jnp.logical_not(last_tile)
    return same | nxt


def _band_fwd(name, q, k, v, d):
    l = q.shape[0]
    tq = min(512, l)
    nsb, cur, prev, _, st_cur, _ = _band_specs(l, d, tq)
    scale = HD ** -0.5

    def body(q_ref, kc_ref, kp_ref, vc_ref, vp_ref, o_ref, lse_ref):
        first = pl.program_id(1) == 0
        for i in range(nsb):
            lses = []
            mask = _band_mask_q(i, first)
            for h in range(4):
                cs = slice(h * HD, (h + 1) * HD)
                qv = q_ref[i * BAND:(i + 1) * BAND, cs]
                if i == 0:
                    kk = jnp.concatenate([kp_ref[:, cs], kc_ref[0:BAND, cs]], axis=0)
                    vv = jnp.concatenate([vp_ref[:, cs], vc_ref[0:BAND, cs]], axis=0)
                else:
                    kk = kc_ref[(i - 1) * BAND:(i + 1) * BAND, cs]
                    vv = vc_ref[(i - 1) * BAND:(i + 1) * BAND, cs]
                s = jnp.where(mask, _dot(qv, kk, NT) * scale, NEG)
                m = jnp.max(s, axis=-1, keepdims=True)
                p = jnp.exp(s - m)
                den = jnp.sum(p, axis=-1, keepdims=True)
                o_ref[i * BAND:(i + 1) * BAND, cs] = _dot(p.astype(BF16), vv) / den
                lses.append(m + jnp.log(den))
            lse_ref[i * BAND:(i + 1) * BAND, :] = _lane_pack(lses, (BAND, HD))

    return pl.pallas_call(
        body, name=name, grid=(d, l // tq), in_specs=[cur, cur, prev, cur, prev],
        out_specs=[cur, st_cur],
        out_shape=[jax.ShapeDtypeStruct((l, d * A_GROUP), F32), jax.ShapeDtypeStruct((l, d * HD), F32)],
        compiler_params=_params(("parallel", "parallel")),
    )(q, k, k, v, v)


def _band_dq(name, q, k, v, dy, lse, delta, d):
    l = q.shape[0]
    tq = min(512, l)
    nsb, cur, prev, _, st_cur, _ = _band_specs(l, d, tq)
    scale = HD ** -0.5

    def body(q_ref, kc_ref, kp_ref, vc_ref, vp_ref, dy_ref, lse_ref, dl_ref, dq_ref):
        first = pl.program_id(1) == 0
        for i in range(nsb):
            mask = _band_mask_q(i, first)
            rs = slice(i * BAND, (i + 1) * BAND)
            for h in range(4):
                cs = slice(h * HD, (h + 1) * HD)
                if i == 0:
                    kk = jnp.concatenate([kp_ref[:, cs], kc_ref[0:BAND, cs]], axis=0)
                    vv = jnp.concatenate([vp_ref[:, cs], vc_ref[0:BAND, cs]], axis=0)
                else:
                    kk = kc_ref[(i - 1) * BAND:(i + 1) * BAND, cs]
                    vv = vc_ref[(i - 1) * BAND:(i + 1) * BAND, cs]
                s = jnp.where(mask, _dot(q_ref[rs, cs], kk, NT) * scale, NEG)
                p = jnp.exp(s - lse_ref[rs, h:h + 1])
                dp = _dot(dy_ref[rs, cs], vv, NT)
                ds = p * (dp - dl_ref[rs, h:h + 1])
                dq_ref[rs, cs] = (_dot(ds.astype(BF16), kk) * scale).astype(dq_ref.dtype)

    return pl.pallas_call(
        body, name=name, grid=(d, l // tq),
        in_specs=[cur, cur, prev, cur, prev, cur, st_cur, st_cur], out_specs=cur,
        out_shape=jax.ShapeDtypeStruct((l, d * A_GROUP), BF16),
        compiler_params=_params(("parallel", "parallel")),
    )(q, k, k, v, v, dy, lse, delta)


def _band_dkv(name, q, k, v, dy, lse, delta, d):
    l = q.shape[0]
    tq = min(512, l)
    nsb, cur, _, nxt, st_cur, st_nxt = _band_specs(l, d, tq)
    scale = HD ** -0.5
    ntile = l // tq

    def body(k_ref, v_ref, qc_ref, qn_ref, dyc_ref, dyn_ref, lc_ref, ln_ref, dc_ref, dn_ref,
             dk_ref, dv_ref):
        last = pl.program_id(1) == ntile - 1

        def win(c_ref, n_ref, j, cs):
            if j == nsb - 1:
                return jnp.concatenate([c_ref[j * BAND:(j + 1) * BAND, cs], n_ref[:, cs]], axis=0)
            return c_ref[j * BAND:(j + 2) * BAND, cs]

        for j in range(nsb):
            mask = _band_mask_k(j, nsb, last)
            rs = slice(j * BAND, (j + 1) * BAND)
            for h in range(4):
                cs = slice(h * HD, (h + 1) * HD)
                hs = slice(h, h + 1)
                qw = win(qc_ref, qn_ref, j, cs)
                dyw = win(dyc_ref, dyn_ref, j, cs)
                s = jnp.where(mask, _dot(qw, k_ref[rs, cs], NT) * scale, NEG)
                p = jnp.exp(s - win(lc_ref, ln_ref, j, hs))
                dp = _dot(dyw, v_ref[rs, cs], NT)
                ds = p * (dp - win(dc_ref, dn_ref, j, hs))
                dv_ref[rs, cs] = _dot(p.astype(BF16), dyw, TN).astype(dv_ref.dtype)
                dk_ref[rs, cs] = (_dot(ds.astype(BF16), qw, TN) * scale).astype(dk_ref.dtype)

    shp = jax.ShapeDtypeStruct((l, d * A_GROUP), BF16)
    return pl.pallas_call(
        body, name=name, grid=(d, ntile),
        in_specs=[cur, cur, cur, nxt, cur, nxt, st_cur, st_nxt, st_cur, st_nxt],
        out_specs=[cur, cur], out_shape=[shp, shp],
        compiler_params=_params(("parallel", "parallel")),
    )(k, v, q, q, dy, dy, lse, lse, delta, delta)


def _split3(x):
    hi = x.astype(BF16)
    r1 = x - hi.astype(F32)
    mid = r1.astype(BF16)
    lo = (r1 - mid.astype(F32)).astype(BF16)
    return hi, mid, lo


def _fox_prep(z, b):
    h, s = z.shape
    blk = min(512, s)

    def body(z_ref, b_ref, c_ref):
        r = lax.broadcasted_iota(jnp.int32, (blk, blk), 0)
        cidx = lax.broadcasted_iota(jnp.int32, (blk, blk), 1)
        tri = (r <= cidx).astype(BF16)
        carry = jnp.zeros((h, 1), F32)
        for t in range(s // blk):
            zz = z_ref[:, t * blk:(t + 1) * blk] + b_ref[...]
            lf = jnp.minimum(zz, 0.0) - jnp.log(1.0 + jnp.exp(-jnp.abs(zz)))
            hi, mid, lo = _split3(lf)
            cs = _dot(hi, tri) + _dot(mid, tri) + _dot(lo, tri) + carry
            c_ref[:, t * blk:(t + 1) * blk] = cs
            carry = cs[:, blk - 1:blk]

    return pl.pallas_call(body, name="fox_prep", out_shape=jax.ShapeDtypeStruct((h, s), F32))(z, b)


def _fox_prep_bwd(dc, z, b):
    h, s = z.shape
    blk = min(512, s)

    def body(dc_ref, z_ref, b_ref, dz_ref, db_ref):
        r = lax.broadcasted_iota(jnp.int32, (blk, blk), 0)
        cidx = lax.broadcasted_iota(jnp.int32, (blk, blk), 1)
        tri = (r >= cidx).astype(BF16)
        carry = jnp.zeros((h, 1), F32)
        tot = jnp.zeros((h, 1), F32)
        for t in reversed(range(s // blk)):
            hi, mid, lo = _split3(dc_ref[:, t * blk:(t + 1) * blk])
            rc = _dot(hi, tri) + _dot(mid, tri) + _dot(lo, tri) + carry
            carry = rc[:, 0:1]
            zz = z_ref[:, t * blk:(t + 1) * blk] + b_ref[...]
            dz = rc * _sig(-zz)
            dz_ref[:, t * blk:(t + 1) * blk] = dz
            tot = tot + jnp.sum(dz, axis=-1, keepdims=True)
        db_ref[...] = tot

    return pl.pallas_call(
        body, name="fox_prep_bwd",
        out_shape=[jax.ShapeDtypeStruct((h, s), F32), jax.ShapeDtypeStruct((h, 1), F32)])(dc, z, b)


FOX_W = 128
FOX_C = B_HD
FOX_ONE = B_HD + 3
FOX_SUB = 256
FOX_SUB_FWD = 128
FOX_HEADS_PER_STEP = 2


def _head_of_pair(x, hh):
    return x if hh == 0 else pltpu.roll(x, B_HD, 1)


def _fox_pack(u, c_col, t):
    s = u.shape[0]
    nt = s // t
    scale = B_HD ** -0.5

    def body(q_ref, k_ref, v_ref, c_ref, qf_ref, kb_ref, ks_ref, vb_ref, vt_ref):
        lane = lax.broadcasted_iota(jnp.int32, (t, FOX_W), 1)
        qv, kv, vv = [r[...].astype(F32) for r in (q_ref, k_ref, v_ref)]
        for hh in range(2):
            qf_ref[hh] = jnp.where(lane < B_HD, _head_of_pair(qv, hh), B_HD ** 0.5).astype(BF16)
            neg = c_ref[hh] * (-scale)
            hi = neg.astype(BF16).astype(F32)
            mid = (neg - hi).astype(BF16).astype(F32)
            lo = neg - hi - mid
            aux = jnp.where(lane == FOX_C, hi,
                            jnp.where(lane == FOX_C + 1, mid, jnp.where(lane == FOX_C + 2, lo, 0.0)))
            kb = jnp.where(lane < B_HD, _head_of_pair(kv, hh) * scale, aux)
            kb_ref[hh] = kb.astype(BF16)
            ks_ref[hh] = jnp.where(lane == FOX_ONE, 1.0, kb).T.astype(BF16)
            vb = jnp.where(lane < B_HD, _head_of_pair(vv, hh), 1.0)
            vb_ref[hh] = vb.astype(BF16)
            vt_ref[hh] = vb.T.astype(BF16)

    def tok(col0):
        return pl.BlockSpec((t, FOX_W), functools.partial(lambda hp, i, cb: (i, cb + hp), cb=col0 // FOX_W))

    rows = pl.BlockSpec((2, t, FOX_W), lambda hp, i: (hp, i, 0))
    tiles = pl.BlockSpec((2, None, FOX_W, t), lambda hp, i: (hp, i, 0, 0))
    hm = jax.ShapeDtypeStruct((B_HEADS, s, FOX_W), BF16)
    tt = jax.ShapeDtypeStruct((B_HEADS, nt, FOX_W, t), BF16)
    return pl.pallas_call(
        body, name="fox_pack", grid=(B_HEADS // 2, nt),
        in_specs=[tok(C_QB), tok(C_KB), tok(C_VB), pl.BlockSpec((2, t, 1), lambda hp, i: (hp, i, 0))],
        out_specs=[rows, rows, tiles, rows, tiles], out_shape=[hm, hm, tt, hm, tt],
        compiler_params=_params(("parallel", "parallel")),
    )(u, u, u, c_col)


def _fox_pack_bwd(dy, y, t):
    s = dy.shape[0]
    nt = s // t

    def body(do_ref, o_ref, dow_ref, dl_ref):
        lane = lax.broadcasted_iota(jnp.int32, (t, FOX_W), 1)
        lane8 = lax.broadcasted_iota(jnp.int32, (8, FOX_W), 1)
        dov = do_ref[...].astype(F32)
        parts = _split3(dov * o_ref[...].astype(F32))
        for hh in range(2):
            dow_ref[hh] = jnp.where(lane < B_HD, _head_of_pair(dov, hh), 0.0).astype(BF16)
            mask = ((lane8 >= hh * B_HD) & (lane8 < (hh + 1) * B_HD)).astype(BF16)
            row = _dot(mask, parts[0], NT) + _dot(mask, parts[1], NT) + _dot(mask, parts[2], NT)
            dl_ref[hh] = row[0:1, :]

    tok = pl.BlockSpec((t, FOX_W), lambda hp, i: (i, hp))
    return pl.pallas_call(
        body, name="fox_pack_bwd", grid=(B_HEADS // 2, nt), in_specs=[tok, tok],
        out_specs=[pl.BlockSpec((2, t, FOX_W), lambda hp, i: (hp, i, 0)),
                   pl.BlockSpec((2, None, 1, t), lambda hp, i: (hp, i, 0, 0))],
        out_shape=[jax.ShapeDtypeStruct((B_HEADS, s, FOX_W), BF16), jax.ShapeDtypeStruct((B_HEADS, nt, 1, t), F32)],
        compiler_params=_params(("parallel", "parallel")),
    )(dy, y)


def _fox_unpack(dqt, dkw, dvw, t):
    h, nt = dqt.shape[:2]
    s = nt * t

    def body(dq_ref, dk_ref, dv_ref, dqo_ref, dko_ref, dvo_ref, dc_ref):
        lane = lax.broadcasted_iota(jnp.int32, (t, FOX_W), 1)

        def join(a0, a1):
            return jnp.where(lane < B_HD, a0, pltpu.roll(a1, B_HD, 1))

        for hh in range(2):
            dc_ref[hh] = dq_ref[hh][FOX_ONE:FOX_ONE + 1, :] - dk_ref[hh].T[B_HD:B_HD + 1, :]
        dqo_ref[...] = join(dq_ref[0].T, dq_ref[1].T).astype(BF16)
        dko_ref[...] = join(dk_ref[0], dk_ref[1]).astype(BF16)
        dvo_ref[...] = join(dv_ref[0], dv_ref[1]).astype(BF16)

    tok = pl.BlockSpec((t, FOX_W), lambda hp, i: (i, hp))
    rows = pl.BlockSpec((2, t, FOX_W), lambda hp, i: (hp, i, 0))
    shp = jax.ShapeDtypeStruct((s, h * B_HD), BF16)
    return pl.pallas_call(
        body, name="fox_unpack", grid=(h // 2, nt),
        in_specs=[pl.BlockSpec((2, None, FOX_W, t), lambda hp, i: (hp, i, 0, 0)), rows, rows],
        out_specs=[tok, tok, tok, pl.BlockSpec((2, None, 1, t), lambda hp, i: (hp, i, 0, 0))],
        out_shape=[shp, shp, shp, jax.ShapeDtypeStruct((h, nt, 1, t), F32)],
        compiler_params=_params(("parallel", "parallel")),
    )(dqt, dkw, dvw)


FOX_DEAD = -110.0


def _fox_norm2(qf, kb):
    h, s, w = qf.shape
    tm = min(2048, s)

    def body(q_ref, k_ref, qo_ref, ko_ref):
        row = lax.broadcasted_iota(jnp.int32, (w, w), 0)
        ones = (row < B_HD).astype(BF16)
        for x_ref, o_ref in ((q_ref, qo_ref), (k_ref, ko_ref)):
            xv = x_ref[...].astype(F32)
            n2 = _dot((xv * xv).astype(BF16), ones)
            o_ref[...] = jnp.broadcast_to(jnp.max(n2, axis=0, keepdims=True)[:, :1], o_ref.shape)

    spec = pl.BlockSpec((None, tm, w), lambda hh, i: (hh, i, 0))
    ospec = pl.BlockSpec((None, None, 8, 128), lambda hh, i: (hh, i, 0, 0))
    shp = jax.ShapeDtypeStruct((h, s // tm, 8, 128), F32)
    return pl.pallas_call(
        body, name="fox_norm2", grid=(h, s // tm), in_specs=[spec, spec], out_specs=[ospec, ospec],
        out_shape=[shp, shp], compiler_params=_params(("parallel", "parallel")),
    )(qf, kb)


def _fox_bounds(qf, kb, c, t):
    q2, k2 = _fox_norm2(qf, kb)
    g = 2.0 * jnp.sqrt(1.02 * jnp.max(q2[:, :, 0, 0], axis=1) * 1.02 * jnp.max(k2[:, :, 0, 0], axis=1))
    return jnp.concatenate([c[:, ::t], c[:, t - 1::t], g[:, None]], axis=1)


SMEM_SPEC = pl.BlockSpec(memory_space=pltpu.SMEM)


def _fox_fwd(qf, kb, vt4, bounds, t):
    h, s, w = qf.shape
    nt = s // t
    sub = FOX_SUB_FWD
    nsub = t // sub
    nh = FOX_HEADS_PER_STEP

    def body(b_ref, q_ref, k_ref, v_ref, o_ref, lse_ref):
        i = pl.program_id(1)
        krow = lax.broadcasted_iota(jnp.int32, (sub, t), 0)
        qcol = lax.broadcasted_iota(jnp.int32, (sub, t), 1)

        def dead_before(hh):
            head = pl.program_id(0) * nh + hh
            top = b_ref[head, 2 * nt] + b_ref[head, i]
            return lax.fori_loop(
                0, i, lambda jj, n: n + (top - b_ref[head, nt + jj] < FOX_DEAD).astype(jnp.int32), 0)

        j_lo = functools.reduce(jnp.minimum, [dead_before(hh) for hh in range(nh)])

        def tile(j, carry, diag):
            out = []
            for hh in range(nh):
                m, acc = carry[hh]
                qv, vj = q_ref[hh], v_ref[hh, j]
                sts = [_dot(k_ref[hh, pl.ds(pl.multiple_of(j * t + b * sub, sub), sub), :], qv, NT)
                       for b in range(nsub)]
                for b in range(nsub):
                    st = sts[b]
                    if diag:
                        st = jnp.where(krow + b * sub <= qcol, st, NEG)
                    m2 = jnp.maximum(m, jnp.max(st, axis=0, keepdims=True))
                    p = jnp.exp(st - m2).astype(BF16)
                    acc = jnp.exp(m - m2) * acc + _dot(vj[:, b * sub:(b + 1) * sub], p)
                    m = m2
                out.append((m, acc))
            return tuple(out)

        init = tuple((jnp.full((1, t), NEG, F32), jnp.zeros((w, t), F32)) for _ in range(nh))
        carry = lax.fori_loop(j_lo, i, lambda j, c: tile(j, c, False), init)
        outs = []
        for hh, (m, acc) in enumerate(tile(i, carry, True)):
            den = acc[B_HD:B_HD + 1, :]
            outs.append(acc[0:B_HD, :] / den)
            lse_ref[hh] = m + jnp.log(den)
        o_ref[...] = jnp.concatenate(outs, axis=0).T.astype(o_ref.dtype)

    return pl.pallas_call(
        body, name="fox_fwd", grid=(h // nh, nt),
        in_specs=[SMEM_SPEC,
                  pl.BlockSpec((nh, t, w), lambda hh, i: (hh, i, 0)),
                  pl.BlockSpec((nh, s, w), lambda hh, i: (hh, 0, 0)),
                  pl.BlockSpec((nh, nt, w, t), lambda hh, i: (hh, 0, 0, 0))],
        out_specs=[pl.BlockSpec((t, nh * B_HD), lambda hh, i: (i, hh)),
                   pl.BlockSpec((nh, 1, t), lambda hh, i: (hh, 0, i))],
        out_shape=[jax.ShapeDtypeStruct((s, h * B_HD), BF16), jax.ShapeDtypeStruct((h, 1, s), F32)],
        compiler_params=_params(("parallel", "parallel")),
    )(bounds, qf, kb, vt4)


def _fox_bwd(qf, dow, lse_row, delta_row, kb, kst4, vb, bounds, t):
    h, s, w = qf.shape
    nt = s // t
    nsub = t // FOX_SUB
    nh = FOX_HEADS_PER_STEP

    def body(b_ref, q_ref, do_ref, lse_ref, dl_ref, k_ref, kt_ref, v_ref, dqt_ref, dk_ref, dv_ref, dk_acc, dv_acc):
        j = pl.program_id(1)

        def alive_after(hh):
            head = pl.program_id(0) * nh + hh
            top = b_ref[head, 2 * nt] - b_ref[head, nt + j]
            return lax.fori_loop(
                j + 1, nt, lambda ii, n: n + (top + b_ref[head, ii] >= FOX_DEAD).astype(jnp.int32), 0)

        i_hi = j + 1 + functools.reduce(jnp.maximum, [alive_after(hh) for hh in range(nh)])

        @pl.when(j == 0)
        def _():
            dqt_ref[...] = jnp.zeros_like(dqt_ref)

        dk_acc[...] = jnp.zeros_like(dk_acc)
        dv_acc[...] = jnp.zeros_like(dv_acc)
        krow = lax.broadcasted_iota(jnp.int32, (FOX_SUB, t), 0)
        qcol = lax.broadcasted_iota(jnp.int32, (FOX_SUB, t), 1)
        subs = [slice(b * FOX_SUB, (b + 1) * FOX_SUB) for b in range(nsub)]

        def tile(i, diag):
            i0 = pl.multiple_of(i * t, t)
            for hh in range(nh):
                qi, doi = q_ref[hh, pl.ds(i0, t), :], do_ref[hh, pl.ds(i0, t), :]
                lse, dl = lse_ref[hh, i], dl_ref[hh, i]
                sts = [_dot(k_ref[hh, rs, :], qi, NT) for rs in subs]
                dps = [_dot(v_ref[hh, rs, :], doi, NT) for rs in subs]
                dq = None
                for b, rs in enumerate(subs):
                    st = sts[b] - lse
                    if diag:
                        st = jnp.where(krow + b * FOX_SUB <= qcol, st, NEG)
                    pt = jnp.exp(st)
                    dsb = (pt * (dps[b] - dl)).astype(BF16)
                    dv_acc[hh, rs, :] += _dot(pt.astype(BF16), doi)
                    dk_acc[hh, rs, :] += _dot(dsb, qi)
                    part = _dot(kt_ref[hh, :, rs], dsb)
                    dq = part if dq is None else dq + part
                dqt_ref[hh, i] += dq

        def step(i, carry):
            tile(i, False)
            return carry

        tile(j, True)
        lax.fori_loop(j + 1, i_hi, step, 0)
        dk_ref[...] = dk_acc[...] * (B_HD ** -0.5)
        dv_ref[...] = dv_acc[...]

    full = pl.BlockSpec((nh, s, w), lambda hh, j: (hh, 0, 0))
    rowst = pl.BlockSpec((nh, nt, 1, t), lambda hh, j: (hh, 0, 0, 0))
    tl = pl.BlockSpec((nh, t, w), lambda hh, j: (hh, j, 0))
    return pl.pallas_call(
        body, name="fox_bwd", grid=(h // nh, nt),
        in_specs=[SMEM_SPEC, full, full, rowst, rowst, tl,
                  pl.BlockSpec((nh, None, w, t), lambda hh, j: (hh, j, 0, 0)), tl],
        out_specs=[pl.BlockSpec((nh, nt, w, t), lambda hh, j: (hh, 0, 0, 0)), tl, tl],
        out_shape=[jax.ShapeDtypeStruct((h, nt, w, t), F32), jax.ShapeDtypeStruct((h, s, w), F32),
                   jax.ShapeDtypeStruct((h, s, w), F32)],
        scratch_shapes=[pltpu.VMEM((nh, t, w), F32), pltpu.VMEM((nh, t, w), F32)],
        compiler_params=_params(("parallel", "arbitrary")),
    )(bounds, qf, dow, lse_row, delta_row, kb, kst4, vb)


def _mem_fwd(u, mkv, tq=512):
    s = u.shape[0]
    scale = HD ** -0.5

    def body(q_ref, mk_ref, mv_ref, o_ref, lse_ref):
        lses = []
        for h in range(4):
            cs = slice(h * HD, (h + 1) * HD)
            sc = _dot(q_ref[:, cs], mk_ref[:, cs], NT) * scale
            m = jnp.max(sc, axis=-1, keepdims=True)
            p = jnp.exp(sc - m)
            den = jnp.sum(p, axis=-1, keepdims=True)
            o_ref[:, cs] = (_dot(p.astype(BF16), mv_ref[:, cs]) / den).astype(o_ref.dtype)
            lses.append(m + jnp.log(den))
        lse_ref[...] = _lane_pack(lses, (tq, HD))

    return pl.pallas_call(
        body, name="mem_fwd", grid=(s // tq,),
        in_specs=[pl.BlockSpec((tq, 512), lambda i: (i, C_QM // 512)),
                  pl.BlockSpec((N_MEM, 512), lambda i: (0, 0)),
                  pl.BlockSpec((N_MEM, 512), lambda i: (0, 1))],
        out_specs=[pl.BlockSpec((tq, 512), lambda i: (i, 0)), pl.BlockSpec((tq, HD), lambda i: (i, 0))],
        out_shape=[jax.ShapeDtypeStruct((s, 512), BF16), jax.ShapeDtypeStruct((s, HD), F32)],
        compiler_params=_params(("parallel",)),
    )(u, mkv, mkv)


def _mem_bwd(u, mkv, o, do, lse, tq=512):
    s = u.shape[0]
    scale = HD ** -0.5

    def body(q_ref, mk_ref, mv_ref, o_ref, do_ref, lse_ref, dq_ref, dmk_ref, dmv_ref):
        @pl.when(pl.program_id(0) == 0)
        def _():
            dmk_ref[...] = jnp.zeros_like(dmk_ref)
            dmv_ref[...] = jnp.zeros_like(dmv_ref)

        for h in range(4):
            cs = slice(h * HD, (h + 1) * HD)
            qv, dov = q_ref[:, cs], do_ref[:, cs]
            sc = _dot(qv, mk_ref[:, cs], NT) * scale
            p = jnp.exp(sc - lse_ref[:, h:h + 1])
            delta = jnp.sum(dov.astype(F32) * o_ref[:, cs].astype(F32), axis=-1, keepdims=True)
            ds = p * (_dot(dov, mv_ref[:, cs], NT) - delta)
            dsb = ds.astype(BF16)
            dq_ref[:, cs] = (_dot(dsb, mk_ref[:, cs]) * scale).astype(dq_ref.dtype)
            dmk_ref[:, cs] += _dot(dsb, qv, TN) * scale
            dmv_ref[:, cs] += _dot(p.astype(BF16), dov, TN)

    row = pl.BlockSpec((tq, 512), lambda i: (i, 0))
    acc = pl.BlockSpec((N_MEM, 512), lambda i: (0, 0))
    return pl.pallas_call(
        body, name="mem_bwd", grid=(s // tq,),
        in_specs=[pl.BlockSpec((tq, 512), lambda i: (i, C_QM // 512)),
                  pl.BlockSpec((N_MEM, 512), lambda i: (0, 0)),
                  pl.BlockSpec((N_MEM, 512), lambda i: (0, 1)),
                  row, row, pl.BlockSpec((tq, HD), lambda i: (i, 0))],
        out_specs=[row, acc, acc],
        out_shape=[jax.ShapeDtypeStruct((s, 512), BF16), jax.ShapeDtypeStruct((N_MEM, 512), F32),
                   jax.ShapeDtypeStruct((N_MEM, 512), F32)],
        compiler_params=_params(("arbitrary",)),
    )(u, mkv, mkv, o, do, lse)


def _local_step(x, mem, pos, target, g_pre, g_post, g_mem, w_main, w_fb, b_forget, b_merge,
                w_mem_kv, w_ba, w_bb, w_bm, w_out, exchange=None):
    s = x.shape[0]
    t_fox = min(512, s)
    nt = s // t_fox
    half = ROT_DIM // 2
    inv = ROPE_THETA ** (-jnp.arange(half, dtype=F32) / half)
    inv128 = jnp.concatenate([inv, inv, jnp.zeros((HD - ROT_DIM,), F32)]).reshape(1, HD)

    h = _rms_fwd("norm_pre", x, g_pre)
    u = _mm("proj_in", h, w_main, "nn", BF16)
    ufb = _mm("proj_fb", h, w_fb, "nn", F32)
    memn = _rms_fwd("norm_mem", mem, g_mem)
    mkv = _mm("proj_mem", memn, w_mem_kv, "nn", BF16)

    qkv = _rope_fwd(u, pos, inv128)
    views = [tuple(qkv[3 * g:3 * g + 3]) for g in range(3)]
    os_, lses = [], []
    for g, d in enumerate(DILATIONS):
        o_g, lse_g = _band_fwd("band_fwd%d" % g, *views[g], d)
        os_.append((o_g, d * A_GROUP, 0, d))
        lses.append((lse_g, d * HD, 0, d))

    def merge_a(o1, o2, o3, l1, l2, l3, za, *scr):
        o1, o2, o3 = [_from_class(o, scr, d) for o, d in zip((o1, o2, o3), DILATIONS)]
        l1, l2, l3 = [_from_class(lv, scr, d) for lv, d in zip((l1, l2, l3), DILATIONS)]
        ys, tots = [], []
        for hh in range(4):
            cs, hs = slice(hh * HD, (hh + 1) * HD), slice(hh, hh + 1)
            mx = jnp.maximum(jnp.maximum(l1[:, hs], l2[:, hs]), l3[:, hs])
            e1, e2, e3 = jnp.exp(l1[:, hs] - mx), jnp.exp(l2[:, hs] - mx), jnp.exp(l3[:, hs] - mx)
            den = e1 + e2 + e3
            ys.append((e1 * o1[:, cs] + e2 * o2[:, cs] + e3 * o3[:, cs]) / den)
            tots.append(mx + jnp.log(den))
        y = jnp.concatenate(ys, axis=1)
        zf = za.astype(F32)
        tot = _lane_pack(tots, l1.shape)
        return (y, y * (zf * _sig(zf))) + tuple(_to_class(tot, scr, d) for d in DILATIONS)

    res = _rows("merge_a", merge_a, os_ + lses + [(u, 512, C_ZA // 512)], [],
                [(512, BF16), (512, BF16)] + [(d * HD, F32, d) for d in DILATIONS], tm=ROPE_TM,
                scratch=_class_scratch(ROPE_TM))
    y_a, yg_a, lse_a = res[0], res[1], res[2:5]

    zrow = ufb[:, :B_HEADS].T
    c = _fox_prep(zrow, b_forget.reshape(B_HEADS, 1))
    qf, kb, kst4, vb, vt4 = _fox_pack(u, c.reshape(B_HEADS, s, 1), t_fox)
    bounds = _fox_bounds(qf, kb, c, t_fox)
    y_b, lse_b = _fox_fwd(qf, kb, vt4, bounds, t_fox)

    y_m, lse_m = _mem_fwd(u, mkv)

    def gate(y, z):
        zf = z.astype(F32)
        return (y.astype(F32) * (zf * _sig(zf)),)

    yg_b = _rows("gate_b", gate, [y_b, (u, 512, C_ZB // 512)], [], [(512, BF16)])[0]
    yg_m = _rows("gate_m", gate, [y_m, (u, 512, C_ZM // 512)], [], [(512, BF16)])[0]

    br_a = _mm("branch_a", yg_a, w_ba, "nn", BF16)
    br_b = _mm("branch_b", yg_b, w_bb, "nn", BF16)
    br_m = _mm("branch_m", yg_m, w_bm, "nn", BF16)
    gl = [(u, 1024, C_GL // 1024 + i) for i in range(3)]
    bm3 = b_merge.reshape(3, D_MODEL)

    def merge(g0, g1, g2, b0, b1, b2, bm):
        tot = 0.0
        for i, (gv, bv) in enumerate(((g0, b0), (g1, b1), (g2, b2))):
            tot = tot + _sig(gv.astype(F32) + bm[i:i + 1, :]) * bv.astype(F32)
        return (tot,)

    merged = _rows("merge_gates", merge, gl + [br_a, br_b, br_m], [bm3], [(D_MODEL, BF16)])[0]
    out = _mm("proj_out", merged, w_out, "nn", F32)

    def tail(xv, ov, tv, gv):
        r = lax.rsqrt(jnp.mean(ov * ov, axis=-1, keepdims=True) + EPS)
        n = ov * r
        err = xv + n * gv - tv
        dy = err * (1.0 / D_MODEL)
        dn = dy * gv
        dout = r * (dn - n * jnp.mean(dn * n, axis=-1, keepdims=True))
        return (dy, dout, jnp.sum(0.5 * err * err * (1.0 / D_MODEL), axis=0, keepdims=True),
                jnp.sum(dy * n, axis=0, keepdims=True))

    dy, dout, loss_lanes, g_post_grad = _rows(
        "tail", tail, [x, out, target], [g_post], [(D_MODEL, F32), (D_MODEL, BF16)],
        reds=[D_MODEL, D_MODEL], tm=256)

    dmerged = _mm("d_merged", dout, w_out, "nt", BF16)
    gw_out = _mm("g_w_out", merged, dout, "tn", F32)

    def merge_bwd(dm, g0, g1, g2, b0, b1, b2, bm):
        dmf = dm.astype(F32)
        dbs, dgs, sums = [], [], []
        for i, (gv, bv) in enumerate(((g0, b0), (g1, b1), (g2, b2))):
            sg = _sig(gv.astype(F32) + bm[i:i + 1, :])
            dbs.append(dmf * sg)
            dg = dmf * bv.astype(F32) * sg * (1.0 - sg)
            dgs.append(dg)
            sums.append(jnp.sum(dg, axis=0, keepdims=True))
        return tuple(dbs + dgs + sums)

    res = _rows("merge_bwd", merge_bwd, [dmerged] + gl + [br_a, br_b, br_m], [bm3],
                [(D_MODEL, BF16)] * 6, reds=[D_MODEL] * 3, tm=256)
    dbr, dgl, g_bmerge = res[0:3], res[3:6], jnp.concatenate(res[6:9], axis=1)

    dyg, gw_branch = [], []
    for nm, dbv, wv, ygv in (("a", dbr[0], w_ba, yg_a), ("b", dbr[1], w_bb, yg_b), ("m", dbr[2], w_bm, yg_m)):
        dyg.append(_mm("d_yg_" + nm, dbv, wv, "nt", BF16))
        gw_branch.append(_mm("g_w_branch_" + nm, ygv, dbv, "tn", F32))

    def gate_bwd(dg, y, z):
        dgf, yf, zf = dg.astype(F32), y.astype(F32), z.astype(F32)
        sg = _sig(zf)
        return dgf * (zf * sg), dgf * yf * (sg * (1.0 + zf * (1.0 - sg)))

    def gate_bwd_a(dg, y, z, *scr):
        dyv, dz = gate_bwd(dg, y, z)
        prod = dyv * y.astype(F32)
        dl = [jnp.sum(prod[:, hh * HD:(hh + 1) * HD], axis=-1, keepdims=True) for hh in range(4)]
        delta = _lane_pack(dl, (dg.shape[0], HD))
        return ((dz,) + tuple(_to_class(dyv, scr, d) for d in DILATIONS)
                + tuple(_to_class(delta, scr, d) for d in DILATIONS))

    res = _rows("gate_bwd_a", gate_bwd_a, [dyg[0], y_a, (u, 512, C_ZA // 512)], [],
                [(512, BF16)] + [(d * A_GROUP, BF16, d) for d in DILATIONS] + [(d * HD, F32, d) for d in DILATIONS],
                tm=ROPE_TM, scratch=_class_scratch(ROPE_TM))
    dz_a, dy_a, delta_a = res[0], res[1:4], res[4:7]
    dy_b, dz_b = _rows("gate_bwd_b", gate_bwd, [dyg[1], y_b, (u, 512, C_ZB // 512)], [],
                       [(512, BF16), (512, BF16)])
    dy_m, dz_m = _rows("gate_bwd_m", gate_bwd, [dyg[2], y_m, (u, 512, C_ZM // 512)], [],
                       [(512, BF16), (512, BF16)])

    dq_m, dmk, dmv = _mem_bwd(u, mkv, y_m, dy_m, lse_m)
    dmkv = jnp.concatenate([dmk, dmv], axis=1)
    gw_mem_kv = _mm("g_w_mem_kv", memn, dmkv, "tn", F32)
    dmemn = _mm("d_memn", dmkv, w_mem_kv, "nt", F32)

    def mem_gain_grad(mv, dv):
        r = lax.rsqrt(jnp.mean(mv * mv, axis=-1, keepdims=True) + EPS)
        return (jnp.sum(dv * mv * r, axis=0, keepdims=True),)

    g_mem_grad = _rows("g_norm_mem", mem_gain_grad, [mem, dmemn], [], [], reds=[D_MODEL], tm=N_MEM)[0]

    dow, delta_b = _fox_pack_bwd(dy_b, y_b, t_fox)
    dqt, dkw, dvw = _fox_bwd(qf, dow, lse_b.reshape(B_HEADS, nt, 1, t_fox), delta_b, kb, kst4, vb, bounds, t_fox)
    dqb, dkb, dvb, dc = _fox_unpack(dqt, dkw, dvw, t_fox)
    dzrow, g_bforget = _fox_prep_bwd(dc.reshape(B_HEADS, s), zrow, b_forget.reshape(B_HEADS, 1))
    dfb = jnp.zeros((s, HD), BF16).at[:, :B_HEADS].set(dzrow.T.astype(BF16))

    dqs, dks, dvs = [], [], []
    for g, d in enumerate(DILATIONS):
        qv, kv, vv = views[g]
        dqs.append(_band_dq("band_dq%d" % g, qv, kv, vv, dy_a[g], lse_a[g], delta_a[g], d))
        dk_g, dv_g = _band_dkv("band_dkv%d" % g, qv, kv, vv, dy_a[g], lse_a[g], delta_a[g], d)
        dks.append(dk_g)
        dvs.append(dv_g)
    dqa, dka, dva = _rope_bwd(dqs, dks, dvs, pos, inv128)

    du = jnp.concatenate(
        [dqa, dka, dva, dz_a, dqb, dkb, dvb,
                            dz_b, dq_m, dz_m] + list(dgl), axis=1)

    gw_main = _mm("g_w_main", h.T, du, "nn", F32, tk=2048)
    gw_fb = _mm("g_w_fb", h, dfb, "tn", F32)
    gw_in = jnp.concatenate([gw_main[:, :FB_ORIG], gw_fb[:, :B_HEADS], gw_main[:, FB_ORIG:]], axis=1)
    grads = dict(norm_post_g=g_post_grad, norm_mem_g=g_mem_grad, w_in=gw_in,
                 b_forget=g_bforget.reshape(1, B_HEADS), b_merge=g_bmerge, w_mem_kv=gw_mem_kv,
                 w_branch_a=gw_branch[0], w_branch_b=gw_branch[1], w_branch_m=gw_branch[2], w_out=gw_out)
    side = exchange(grads) if exchange else None
    dh_main = _mm("d_h", du, w_main, "nt", F32, tk=2816, side=side)
    landed = None
    if side:
        dh_main, landed = dh_main[0], dh_main[1:]
    dh_fb = _mm("d_h_fb", dfb, w_fb, "nt", F32)

    def pre_bwd(xv, d1, d2, dyv, gv):
        r = lax.rsqrt(jnp.mean(xv * xv, axis=-1, keepdims=True) + EPS)
        n = xv * r
        dhv = d1 + d2
        dn = dhv * gv
        dx = r * (dn - n * jnp.mean(dn * n, axis=-1, keepdims=True))
        return dyv + dx, jnp.sum(dhv * n, axis=0, keepdims=True)

    grad_x, g_pre_grad = _rows("norm_pre_bwd", pre_bwd, [x, dh_main, dh_fb, dy], [g_pre],
                               [(D_MODEL, F32)], reds=[D_MODEL], tm=256)

    grads["norm_pre_g"] = g_pre_grad
    return loss_lanes, grad_x, grads, landed


HBM_SPEC = pl.BlockSpec(memory_space=pltpu.HBM)


def _place():
    x, y, c = lax.axis_index("x"), lax.axis_index("y"), lax.axis_index("c")
    chips = [(1 - x, y), (x, 1 - y), (1 - x, 1 - y)]
    return x, y, c, 2 * x + y, chips


N_CHUNKS = 4


def _units(parts, row_axis):
    units = []
    for i, a in enumerate(parts):
        ch = a.shape[row_axis] // N_CHUNKS
        units += [(i, pl.ds(k * ch, ch)) for k in range(N_CHUNKS)]
    return units


def _gather_weights(parts):
    n = len(parts)
    units = _units(parts, 1)
    nu = len(units)

    def body(*refs):
        srcs, outs = refs[:n], refs[n:2 * n]
        send_sems, recv_sems = refs[2 * n:]
        x, y, c, p, chips = _place()
        me, sib = (x, y, c), (x, y, 1 - c)

        def cp(u, k, chip, half, to, from_src=False):
            i, rs = units[u]
            dst = outs[i].at[chip, half, rs]
            return pltpu.make_async_remote_copy(
                src_ref=srcs[i].at[half, rs] if from_src else dst, dst_ref=dst, send_sem=send_sems.at[u, k],
                recv_sem=recv_sems.at[u, k], device_id=to, device_id_type=MESH)

        first = [cp(u, j, p, c, (cx, cy, c), from_src=True)
                 for u in range(nu) for j, (cx, cy) in enumerate(chips)]
        for f in first:
            f.start()
        passed = []
        for u in range(nu):
            for j, (cx, cy) in enumerate(chips):
                cp(u, j, 2 * cx + cy, c, me).wait_recv()
                fw = cp(u, 3 + j, 2 * cx + cy, c, sib)
                fw.start()
                passed.append(fw)
        for u in range(nu):
            for j, (cx, cy) in enumerate(chips):
                cp(u, 3 + j, 2 * cx + cy, 1 - c, me).wait_recv()
        for f in first + passed:
            f.wait_send()

    return pl.pallas_call(
        body, name="gather_weights", in_specs=[HBM_SPEC] * n, out_specs=[HBM_SPEC] * n,
        out_shape=[jax.ShapeDtypeStruct((N_CHIPS,) + a.shape, a.dtype) for a in parts],
        scratch_shapes=[pltpu.SemaphoreType.DMA((nu, 6)), pltpu.SemaphoreType.DMA((nu, 6))],
    )(*parts)


def _swap_with_sibling(parts):
    n = len(parts)
    units = _units(parts, 2)

    def body(*refs):
        srcs, outs = refs[:n], refs[n:2 * n]
        send_sems, recv_sems = refs[2 * n:]
        x, y, c, _, _ = _place()
        cps = [pltpu.make_async_remote_copy(
            src_ref=srcs[i].at[q, 1 - c, rs], dst_ref=outs[i].at[q, rs], send_sem=send_sems.at[u, q],
            recv_sem=recv_sems.at[u, q], device_id=(x, y, 1 - c), device_id_type=MESH)
            for q in range(N_CHIPS) for u, (i, rs) in enumerate(units)]
        for cpy in cps:
            cpy.start()
        for cpy in cps:
            cpy.wait()

    return pl.pallas_call(
        body, name="swap_with_sibling", in_specs=[HBM_SPEC] * n, out_specs=[HBM_SPEC] * n,
        out_shape=[jax.ShapeDtypeStruct(a.shape[:1] + a.shape[2:], a.dtype) for a in parts],
        scratch_shapes=[pltpu.SemaphoreType.DMA((len(units), N_CHIPS)),
                        pltpu.SemaphoreType.DMA((len(units), N_CHIPS))],
    )(*parts)


def _scatter_to_owners(parts):
    n = len(parts)
    units = _units(parts, 1)

    def copies(srcs, outs, send_sems, recv_sems, incoming):
        x, y, c, p, chips = _place()
        return [pltpu.make_async_remote_copy(
            src_ref=srcs[i].at[2 * cx + cy, rs], dst_ref=outs[i].at[(2 * cx + cy) if incoming else p, rs],
            send_sem=send_sems.at[u, j], recv_sem=recv_sems.at[u, j], device_id=(cx, cy, c), device_id_type=MESH)
            for u, (i, rs) in enumerate(units) for j, (cx, cy) in enumerate(chips)]

    def start(ins, outs, scratch):
        for cpy in copies(ins, outs, *scratch, incoming=False):
            cpy.start()

    def wait(ins, outs, scratch):
        for cpy in copies(ins, outs, *scratch, incoming=True):
            cpy.wait_recv()
        for cpy in copies(ins, outs, *scratch, incoming=False):
            cpy.wait_send()

    return dict(ins=list(parts), outs=[jax.ShapeDtypeStruct(a.shape, a.dtype) for a in parts],
                scratch=[pltpu.SemaphoreType.DMA((len(units), 3)), pltpu.SemaphoreType.DMA((len(units), 3))],
                start=start, wait=wait)


def _share_with_sibling(parts):
    n = len(parts)
    units = _units(parts, 1)

    def body(*refs):
        srcs, outs = refs[:n], refs[n:2 * n]
        send_sems, recv_sems = refs[2 * n:]
        x, y, c, _, _ = _place()
        sends = [pltpu.make_async_remote_copy(
            src_ref=srcs[i].at[0, rs], dst_ref=outs[i].at[c, rs], send_sem=send_sems.at[u],
            recv_sem=recv_sems.at[u], device_id=(x, y, 1 - c), device_id_type=MESH)
            for u, (i, rs) in enumerate(units)]
        for cpy in sends:
            cpy.start()
        for u, (i, rs) in enumerate(units):
            pltpu.make_async_remote_copy(
                src_ref=srcs[i].at[0, rs], dst_ref=outs[i].at[1 - c, rs], send_sem=send_sems.at[u],
                recv_sem=recv_sems.at[u], device_id=(x, y, 1 - c), device_id_type=MESH).wait_recv()
        for cpy in sends:
            cpy.wait_send()

    return pl.pallas_call(
        body, name="share_with_sibling", in_specs=[HBM_SPEC] * n, out_specs=[HBM_SPEC] * n,
        out_shape=[jax.ShapeDtypeStruct((2,) + a.shape[1:], a.dtype) for a in parts],
        scratch_shapes=[pltpu.SemaphoreType.DMA((len(units),)), pltpu.SemaphoreType.DMA((len(units),))],
    )(*parts)


def _sum_small(v):
    def body(v_ref, out_ref, buf, send_sems, recv_sems):
        x, y, c, _, _ = _place()
        me = 4 * x + 2 * y + c
        buf[me] = v_ref[...]
        flips = [(dx, dy, dc) for dx in (0, 1) for dy in (0, 1) for dc in (0, 1)][1:]
        sends = []
        for k, (dx, dy, dc) in enumerate(flips):
            cpy = pltpu.make_async_remote_copy(
                src_ref=v_ref, dst_ref=buf.at[me], send_sem=send_sems.at[k], recv_sem=recv_sems.at[k],
                device_id=((x + dx) % 2, (y + dy) % 2, (c + dc) % 2), device_id_type=MESH)
            cpy.start()
            sends.append(cpy)
        for k, (dx, dy, dc) in enumerate(flips):
            px, py, pc = (x + dx) % 2, (y + dy) % 2, (c + dc) % 2
            pltpu.make_async_remote_copy(
                src_ref=v_ref, dst_ref=buf.at[4 * px + 2 * py + pc], send_sem=send_sems.at[k],
                recv_sem=recv_sems.at[k], device_id=(px, py, pc), device_id_type=MESH).wait_recv()
        for cpy in sends:
            cpy.wait_send()
        tot = buf[0]
        for i in range(1, N_DEV):
            tot = tot + buf[i]
        out_ref[...] = tot

    return pl.pallas_call(
        body, name="sum_small", out_shape=jax.ShapeDtypeStruct(v.shape, v.dtype),
        in_specs=[pl.BlockSpec(memory_space=pltpu.VMEM)], out_specs=pl.BlockSpec(memory_space=pltpu.VMEM),
        scratch_shapes=[pltpu.VMEM((N_DEV,) + v.shape, v.dtype), pltpu.SemaphoreType.DMA((N_DEV - 1,)),
                        pltpu.SemaphoreType.DMA((N_DEV - 1,))],
    )(v)


def _add_slabs(name, terms, out_dtype):
    arr0 = terms[0][0]
    n = arr0.shape[0] if terms[0][1] is None else 1
    _, r, w = arr0.shape
    tr = 64
    specs = []
    for _, slab in terms:
        if slab is None:
            specs.append(pl.BlockSpec((None, tr, w), lambda i, j: (i, j, 0)))
        else:
            specs.append(pl.BlockSpec((None, tr, w), functools.partial(lambda i, j, sl: (sl, j, 0), sl=slab)))

    def body(*refs):
        tot = refs[0][...].astype(F32)
        for rf in refs[1:-1]:
            tot = tot + rf[...].astype(F32)
        refs[-1][...] = tot.astype(out_dtype)

    return pl.pallas_call(
        body, name=name, grid=(n, r // tr), in_specs=specs,
        out_specs=pl.BlockSpec((None, tr, w), lambda i, j: (i, j, 0)),
        out_shape=jax.ShapeDtypeStruct((n, r, w), out_dtype),
        compiler_params=_params(("parallel", "parallel")),
    )(*[a for a, _ in terms])


def _add_pair(name, halves, got, c):
    nq, _, r, w = halves.shape
    tr = 64

    def body(c_ref, a_ref, b_ref, o_ref):
        o_ref[...] = (a_ref[...] + b_ref[...]).astype(o_ref.dtype)

    grid_spec = pltpu.PrefetchScalarGridSpec(
        num_scalar_prefetch=1, grid=(nq, r // tr),
        in_specs=[pl.BlockSpec((None, None, tr, w), lambda i, j, c_ref: (i, c_ref[0], j, 0)),
                  pl.BlockSpec((None, tr, w), lambda i, j, c_ref: (i, j, 0))],
        out_specs=pl.BlockSpec((None, tr, w), lambda i, j, c_ref: (i, j, 0)))
    return pl.pallas_call(
        body, name=name, grid_spec=grid_spec, out_shape=jax.ShapeDtypeStruct((nq, r, w), BF16),
        compiler_params=_params(("parallel", "parallel")),
    )(jnp.reshape(c, (1,)).astype(jnp.int32), halves, got)


def _adamw(name, w, g, m, v, tm):
    def fn(wv, gv, mv, vv):
        m2 = ADAM_B1 * mv + (1.0 - ADAM_B1) * gv
        v2 = ADAM_B2 * vv + (1.0 - ADAM_B2) * (gv * gv)
        m_hat = m2 / (1.0 - ADAM_B1 ** ADAM_STEP)
        v_hat = v2 / (1.0 - ADAM_B2 ** ADAM_STEP)
        return -ADAM_LR * (m_hat / (jnp.sqrt(v_hat) + ADAM_EPS) + ADAM_WD * wv), m2, v2
    c = w.shape[1]
    return _rows(name, fn, [w, g, m, v], [], [(c, F32)] * 3, tm=tm)


REST_ROWS = 256 + 3 * 128 + 256
REST_SPLITS = (("w_mem_kv", 0, 256), ("w_branch_a", 256, 128), ("w_branch_b", 384, 128),
               ("w_branch_m", 512, 128), ("w_out", 640, 256))


def _rest_pack(t):
    return jnp.concatenate([t[n].reshape(rows, D_MODEL) for n, _, rows in REST_SPLITS], axis=0)


def _rest_unpack(a, shapes):
    return {n: a[r0:r0 + rows].reshape(shapes[n]) for n, r0, rows in REST_SPLITS}


def _small_pack(pre, post, memg, bforget, bmerge):
    pad = jnp.zeros((1, D_MODEL - B_HEADS), F32)
    return jnp.concatenate([pre, post, memg, bmerge.reshape(3, D_MODEL),
                            jnp.concatenate([bforget, pad], axis=1), jnp.zeros((1, D_MODEL), F32)], axis=0)


def _small_unpack(s8):
    return dict(norm_pre_g=s8[0:1], norm_post_g=s8[1:2], norm_mem_g=s8[2:3],
                b_merge=s8[3:6].reshape(1, 3 * D_MODEL), b_forget=s8[6:7, :B_HEADS])


WEIGHTS = ("norm_pre_g", "norm_post_g", "norm_mem_g", "w_in", "b_forget", "b_merge", "w_mem_kv",
           "w_branch_a", "w_branch_b", "w_branch_m", "w_out")
SMALL = ("norm_pre_g", "norm_post_g", "norm_mem_g", "b_forget", "b_merge")


def kernel(x, mem, positions, norm_pre_g, norm_post_g, norm_mem_g, w_in, b_forget, b_merge, w_mem_kv, w_branch_a, w_branch_b, w_branch_m, w_out, loss_target, m_norm_pre_g, m_norm_post_g, m_norm_mem_g, m_w_in, m_b_forget, m_b_merge, m_w_mem_kv, m_w_branch_a, m_w_branch_b, m_w_branch_m, m_w_out, v_norm_pre_g, v_norm_post_g, v_norm_mem_g, v_w_in, v_b_forget, v_b_merge, v_w_mem_kv, v_w_branch_a, v_w_branch_b, v_w_branch_m, v_w_out):
    w = dict(norm_pre_g=norm_pre_g, norm_post_g=norm_post_g, norm_mem_g=norm_mem_g, w_in=w_in[0],
             b_forget=b_forget, b_merge=b_merge, w_mem_kv=w_mem_kv[0], w_branch_a=w_branch_a[0],
             w_branch_b=w_branch_b[0], w_branch_m=w_branch_m[0], w_out=w_out[0])
    mo = dict(norm_pre_g=m_norm_pre_g, norm_post_g=m_norm_post_g, norm_mem_g=m_norm_mem_g, w_in=m_w_in[0],
              b_forget=m_b_forget, b_merge=m_b_merge, w_mem_kv=m_w_mem_kv[0], w_branch_a=m_w_branch_a[0],
              w_branch_b=m_w_branch_b[0], w_branch_m=m_w_branch_m[0], w_out=m_w_out[0])
    vo = dict(norm_pre_g=v_norm_pre_g, norm_post_g=v_norm_post_g, norm_mem_g=v_norm_mem_g, w_in=v_w_in[0],
              b_forget=v_b_forget, b_merge=v_b_merge, w_mem_kv=v_w_mem_kv[0], w_branch_a=v_w_branch_a[0],
              w_branch_b=v_w_branch_b[0], w_branch_m=v_w_branch_m[0], w_out=v_w_out[0])
    s = x.shape[1]
    c = lax.axis_index("c")

    chip = 2 * lax.axis_index("x") + lax.axis_index("y")

    def put(whole, own, slot):
        return lax.dynamic_update_index_in_dim(whole, own.astype(whole.dtype), slot, 0)

    own_w = [w["w_in"].astype(BF16).reshape(2, D_MODEL // 2, SHARD_COLS),
             _rest_pack(w).astype(BF16).reshape(2, REST_ROWS // 2, D_MODEL)]
    all_in, all_rest = [put(a, o, chip) for a, o in zip(_gather_weights(own_w), own_w)]
    all_in = all_in.reshape(N_CHIPS, D_MODEL, SHARD_COLS)
    w_in_f = jnp.concatenate([all_in[p] for p in range(N_CHIPS)], axis=1)
    all_rest = all_rest.reshape(N_CHIPS, REST_ROWS, D_MODEL)
    w_kv_f = all_rest[:, 0:256].reshape(D_MODEL, D_MODEL)
    w_br_f = [all_rest[:, 256 + 128 * i:384 + 128 * i].reshape(N_CHIPS, 512, 256).transpose(1, 0, 2)
              .reshape(512, D_MODEL) for i in range(3)]
    w_out_f = all_rest[:, 640:896].reshape(D_MODEL, D_MODEL)
    w_main = jnp.concatenate([w_in_f[:, :FB_ORIG], w_in_f[:, FB_ORIG + B_HEADS:]], axis=1)
    w_fb = jnp.concatenate([w_in_f[:, FB_ORIG:FB_ORIG + B_HEADS], jnp.zeros((D_MODEL, HD - B_HEADS), BF16)], axis=1)

    pair = []

    def exchange(g):
        def per_chip(name, p):
            a = g[name]
            if name in ("w_mem_kv", "w_out"):
                return a[256 * p:256 * (p + 1)]
            return a[:, 256 * p:256 * (p + 1)]

        in4 = jnp.stack([g["w_in"][:, SHARD_COLS * p:SHARD_COLS * (p + 1)] for p in range(N_CHIPS)])
        rest4 = jnp.stack([_rest_pack({n: per_chip(n, p) for n, _, _ in REST_SPLITS}) for p in range(N_CHIPS)])
        halves = [in4.reshape(N_CHIPS, 2, D_MODEL // 2, SHARD_COLS),
                  rest4.reshape(N_CHIPS, 2, REST_ROWS // 2, D_MODEL)]
        got = _swap_with_sibling(halves)
        pair.extend(_add_pair("add_pair_%d" % i, halves[i], got[i], c) for i in range(2))
        return _scatter_to_owners(pair)

    loss_lanes, grad_x, g, landed = _local_step(
        x[0], mem[0], positions.reshape(s, 1), loss_target[0], norm_pre_g, norm_post_g, norm_mem_g,
        w_main, w_fb, b_forget, b_merge, w_kv_f, w_br_f[0], w_br_f[1], w_br_f[2], w_out_f, exchange)
    loss = lax.psum(jnp.sum(loss_lanes), ("x", "y", "c"))
    landed = [put(a, lax.dynamic_index_in_dim(o, chip, 0, keepdims=False), chip) for a, o in zip(landed, pair)]
    half = [_add_slabs("add_chips_%d" % i, [(landed[i], q) for q in range(N_CHIPS)], F32) for i in range(2)]
    red_in, red_rest = [put(a, o[0], c) for a, o in zip(_share_with_sibling(half), half)]
    gs = {"w_in": red_in.reshape(D_MODEL, SHARD_COLS)}
    gs.update(_rest_unpack(red_rest.reshape(REST_ROWS, D_MODEL), {n: w[n].shape for n, _, _ in REST_SPLITS}))
    gs.update(_small_unpack(_sum_small(_small_pack(
        g["norm_pre_g"], g["norm_post_g"], g["norm_mem_g"], g["b_forget"], g["b_merge"]))))

    delta, new_m, new_v = {}, {}, {}
    for n, tm in (("w_in", 128), ("w_mem_kv", 256), ("w_branch_a", 512), ("w_branch_b", 512),
                  ("w_branch_m", 512), ("w_out", 256)):
        d_, m_, v_ = _adamw("adamw_" + n, w[n], gs[n], mo[n], vo[n], tm)
        delta[n], new_m[n], new_v[n] = d_[None], m_[None], v_[None]
        gs[n] = gs[n][None]
    packs = [_small_pack(*[t[n] for n in ("norm_pre_g", "norm_post_g", "norm_mem_g", "b_forget", "b_merge")])
             for t in (w, gs, mo, vo)]
    for res, store in zip(_adamw("adamw_small", *packs, 8), (delta, new_m, new_v)):
        store.update(_small_unpack(res))

    return (loss, grad_x[None], *[gs[n] for n in WEIGHTS], *[delta[n] for n in WEIGHTS],
            *[new_m[n] for n in WEIGHTS], *[new_v[n] for n in WEIGHTS])
```

```python
import functools

import jax
import jax.numpy as jnp
from jax import lax
from jax.experimental import pallas as pl
from jax.experimental.pallas import tpu as pltpu

F32 = jnp.float32
BF16 = jnp.bfloat16
MESH = pl.DeviceIdType.MESH

D_MODEL = 1024
N_MEM = 256
EPS = 1e-6
NEG = -1e30
ROPE_THETA = 500000.0
ROT_DIM = 32
HD = 128
A_GROUP = 512
DILATIONS = (1, 4, 16)
BAND = 128
B_HEADS = 8
B_HD = 64
N_CHIPS = 4
N_DEV = 8

C_QA, C_KA, C_VA, C_ZA = 0, 1536, 3072, 4608
C_QB, C_KB, C_VB, C_ZB = 5120, 5632, 6144, 6656
C_QM, C_ZM, C_GL = 7168, 7680, 8192
N_MAIN = 11264
FB_ORIG = 6656
IN_COLS = 11272
SHARD_COLS = IN_COLS // N_CHIPS

ADAM_LR, ADAM_B1, ADAM_B2, ADAM_EPS, ADAM_WD, ADAM_STEP = 0.001, 0.9, 0.999, 1e-08, 0.01, 10

VMEM_LIMIT_V7X = 56 * 1024 * 1024

NT = (((1,), (1,)), ((), ()))
NN = (((1,), (0,)), ((), ()))
TN = (((0,), (0,)), ((), ()))


def _params(sem):
    return pltpu.CompilerParams(dimension_semantics=sem, vmem_limit_bytes=VMEM_LIMIT_V7X)


def _dot(a, b, dn=NN):
    return lax.dot_general(a, b, dn, preferred_element_type=F32)


def _sig(z):
    return 1.0 / (1.0 + jnp.exp(-z))


def _rows(name, fn, row_ins, bc_ins, outs, reds=(), tm=512, scratch=()):
    arrs, specs = [], []
    s = None
    for r in row_ins:
        arr, w, cb, d = (tuple(r) + (1,))[:4] if isinstance(r, tuple) else (r, r.shape[1], 0, 1)
        s = arr.shape[0] * d if s is None else s
        arrs.append(arr)
        specs.append((w, cb, d))
    tm = min(tm, s)
    specs = [pl.BlockSpec((tm // d, w), functools.partial(lambda i, cb: (i, cb), cb=cb)) for w, cb, d in specs]
    for b in bc_ins:
        arrs.append(b)
        specs.append(pl.BlockSpec(b.shape, lambda i: (0, 0)))
    outs = [(tuple(o) + (1,))[:3] for o in outs]
    n_in, n_out = len(arrs), len(outs)

    def body(*refs):
        n_ref = n_in + n_out + len(reds)
        vals = fn(*[r[...] for r in refs[:n_in]], *refs[n_ref:])
        if not isinstance(vals, (tuple, list)):
            vals = (vals,)
        for r, v in zip(refs[n_in:n_in + n_out], vals[:n_out]):
            r[...] = v.astype(r.dtype)
        if reds:
            red_refs = refs[n_in + n_out:n_ref]

            @pl.when(pl.program_id(0) == 0)
            def _():
                for r in red_refs:
                    r[...] = jnp.zeros_like(r)

            for r, v in zip(red_refs, vals[n_out:]):
                r[...] += v

    out_shape = [jax.ShapeDtypeStruct((s // d, c), dt) for c, dt, d in outs]
    out_shape += [jax.ShapeDtypeStruct((1, c), F32) for c in reds]
    out_specs = [pl.BlockSpec((tm // d, c), lambda i: (i, 0)) for c, _, d in outs]
    out_specs += [pl.BlockSpec((1, c), lambda i: (0, 0)) for c in reds]
    res = pl.pallas_call(
        body, name=name, grid=(s // tm,), in_specs=specs, out_specs=out_specs, out_shape=out_shape,
        scratch_shapes=list(scratch),
        compiler_params=_params(("arbitrary",) if reds else ("parallel",)),
    )(*arrs)
    return res


def _to_class(x, scr, d):
    if d == 1:
        return x.astype(F32)
    tm, c = x.shape
    for g in range(c // 128):
        scr[g][...] = x[:, g * 128:(g + 1) * 128].astype(F32)
    return jnp.concatenate([scr[g][pl.ds(r, tm // d, stride=d), :] for r in range(d) for g in range(c // 128)],
                           axis=1)


def _from_class(x, scr, d):
    if d == 1:
        return x.astype(F32)
    n, dc = x.shape
    c = dc // d
    for r in range(d):
        for g in range(c // 128):
            scr[g][pl.ds(r, n, stride=d), :] = x[:, r * c + g * 128:r * c + (g + 1) * 128].astype(F32)
    return jnp.concatenate([scr[g][...] for g in range(c // 128)], axis=1)


def _mm(name, a, b, mode, out_dtype, tm=1024, tn=1024, tk=1024, side=None):
    if mode == "nn":
        (m, k), (_, n) = a.shape, b.shape
    elif mode == "nt":
        (m, k), (n, _) = a.shape, b.shape
    else:
        (k, m), (_, n) = a.shape, b.shape
    tm, tn, tk = min(tm, m), min(tn, n), min(tk, k)
    nk = k // tk
    grid = (m // tm, n // tn, nk)
    dn = {"nn": NN, "nt": NT, "tn": TN}[mode]
    n_si = len(side["ins"]) if side else 0
    n_so = len(side["outs"]) if side else 0
    n_acc = 1 if nk > 1 else 0

    def body(*refs):
        a_ref, b_ref = refs[:2]
        side_in, o_ref = refs[2:2 + n_si], refs[2 + n_si]
        side_out = refs[3 + n_si:3 + n_si + n_so]
        acc = refs[3 + n_si + n_so:3 + n_si + n_so + n_acc]
        side_scratch = refs[3 + n_si + n_so + n_acc:]
        step = (pl.program_id(0) * grid[1] + pl.program_id(1)) * grid[2] + pl.program_id(2)
        if side:
            @pl.when(step == 0)
            def _():
                side["start"](side_in, side_out, side_scratch)

        part = _dot(a_ref[...].astype(BF16), b_ref[...].astype(BF16), dn)
        if nk == 1:
            o_ref[...] = part.astype(o_ref.dtype)
        else:
            kk = pl.program_id(2)

            @pl.when(kk == 0)
            def _():
                acc[0][...] = part

            @pl.when(kk > 0)
            def _():
                acc[0][...] += part

            @pl.when(kk == nk - 1)
            def _():
                o_ref[...] = acc[0][...].astype(o_ref.dtype)

        if side:
            @pl.when(step == grid[0] * grid[1] * grid[2] - 1)
            def _():
                side["wait"](side_in, side_out, side_scratch)

    a_spec = (pl.BlockSpec((tk, tm), lambda i, j, kk: (kk, i)) if mode == "tn"
              else pl.BlockSpec((tm, tk), lambda i, j, kk: (i, kk)))
    b_spec = (pl.BlockSpec((tn, tk), lambda i, j, kk: (j, kk)) if mode == "nt"
              else pl.BlockSpec((tk, tn), lambda i, j, kk: (kk, j)))
    o_spec = pl.BlockSpec((tm, tn), lambda i, j, kk: (i, j))
    o_shape = jax.ShapeDtypeStruct((m, n), out_dtype)
    acc_scratch = [pltpu.VMEM((tm, tn), F32)] * n_acc
    if not side:
        return pl.pallas_call(
            body, name=name, grid=grid, in_specs=[a_spec, b_spec], out_specs=o_spec, out_shape=o_shape,
            scratch_shapes=acc_scratch, compiler_params=_params(("parallel", "parallel", "arbitrary")),
        )(a, b)
    return pl.pallas_call(
        body, name=name, grid=grid, in_specs=[a_spec, b_spec] + [HBM_SPEC] * n_si,
        out_specs=[o_spec] + [HBM_SPEC] * n_so, out_shape=[o_shape] + side["outs"],
        scratch_shapes=acc_scratch + side["scratch"],
        compiler_params=_params(("arbitrary", "arbitrary", "arbitrary")),
    )(a, b, *side["ins"])


def _rms_fwd(name, x, g):
    def fn(xv, gv):
        r = lax.rsqrt(jnp.mean(xv * xv, axis=-1, keepdims=True) + EPS)
        return (xv * r * gv,)
    return _rows(name, fn, [x], [g], [(x.shape[1], BF16)], tm=min(512, x.shape[0]))[0]


def _rope_tables(pos, inv):
    ang = pos.astype(F32) * inv
    lane = lax.broadcasted_iota(jnp.int32, ang.shape, 1)
    c = jnp.where(lane < ROT_DIM, jnp.cos(ang), 1.0)
    sn = jnp.sin(ang)
    sg = jnp.where(lane < ROT_DIM // 2, -sn, jnp.where(lane < ROT_DIM, sn, 0.0))
    return c, sg, lane


def _rope_apply(x, c, sg, lane):
    outs = []
    for h in range(x.shape[1] // HD):
        xh = x[:, h * HD:(h + 1) * HD].astype(F32)
        swap = jnp.where(lane < ROT_DIM // 2, pltpu.roll(xh, HD - ROT_DIM // 2, 1),
                         pltpu.roll(xh, ROT_DIM // 2, 1))
        outs.append(xh * c + swap * sg)
    return jnp.concatenate(outs, axis=1)


ROPE_TM = 256


def _class_scratch(tm):
    return [pltpu.VMEM((tm, 128), F32) for _ in range(A_GROUP // 128)]


def _rope_fwd(u, pos, inv):
    def fn(q, k, v, p, iv, *scr):
        c, sg, lane = _rope_tables(p, iv)
        qr, kr = _rope_apply(q, c, sg, lane), _rope_apply(k, c, sg, lane)
        outs = []
        for g, d in enumerate(DILATIONS):
            gs = slice(g * A_GROUP, (g + 1) * A_GROUP)
            outs += [_to_class(qr[:, gs], scr, d), _to_class(kr[:, gs], scr, d), _to_class(v[:, gs], scr, d)]
        return tuple(outs)

    outs = [(d * A_GROUP, BF16, d) for d in DILATIONS for _ in range(3)]
    return _rows("rope_fwd", fn, [(u, 1536, 0), (u, 1536, 1), (u, 1536, 2), pos], [inv], outs, tm=ROPE_TM,
                 scratch=_class_scratch(ROPE_TM))


def _rope_bwd(dqs, dks, dvs, pos, inv):
    def fn(*args):
        grads, p, iv, scr = args[:9], args[9], args[10], args[11:]
        c, sg, lane = _rope_tables(p, iv)
        tok = [jnp.concatenate([_from_class(grads[3 * k + g], scr, d) for g, d in enumerate(DILATIONS)], axis=1)
               for k in range(3)]
        return _rope_apply(tok[0], c, -sg, lane), _rope_apply(tok[1], c, -sg, lane), tok[2]

    ins = [(a, a.shape[1], 0, d) for grp in (dqs, dks, dvs) for a, d in zip(grp, DILATIONS)]
    return _rows("rope_bwd", fn, ins + [pos], [inv], [(1536, BF16)] * 3, tm=ROPE_TM,
                 scratch=_class_scratch(ROPE_TM))


def _lane_pack(cols, like):
    lane = lax.broadcasted_iota(jnp.int32, like, 1)
    out = jnp.zeros(like, F32)
    for h, cvec in enumerate(cols):
        out = jnp.where(lane == h, cvec, out)
    return out


def _band_specs(l, d, tq):
    nsb = tq // BAND
    nblk = l // BAND
    cur = pl.BlockSpec((tq, A_GROUP), lambda r, i: (i, r))
    prev = pl.BlockSpec((BAND, A_GROUP), lambda r, i: (jnp.maximum(i * nsb - 1, 0), r))
    nxt = pl.BlockSpec((BAND, A_GROUP), lambda r, i: (jnp.minimum((i + 1) * nsb, nblk - 1), r))
    st_cur = pl.BlockSpec((tq, HD), lambda r, i: (i, r))
    st_nxt = pl.BlockSpec((BAND, HD), lambda r, i: (jnp.minimum((i + 1) * nsb, nblk - 1), r))
    return nsb, cur, prev, nxt, st_cur, st_nxt


def _band_mask_q(i, first_tile):
    qr = lax.broadcasted_iota(jnp.int32, (BAND, 2 * BAND), 0)
    kc = lax.broadcasted_iota(jnp.int32, (BAND, 2 * BAND), 1)
    in_prev = (kc < BAND) & (kc >= qr)
    in_cur = (kc >= BAND) & (kc - BAND <= qr)
    if i == 0:
        in_prev = in_prev & jnp.logical_not(first_tile)
    return in_prev | in_cur


def _band_mask_k(j, nsb, last_tile):
    kc = lax.broadcasted_iota(jnp.int32, (BAND, 2 * BAND), 0)
    qr = lax.broadcasted_iota(jnp.int32, (BAND, 2 * BAND), 1)
    same = (qr < BAND) & (kc <= qr)
    nxt = (qr >= BAND) & (kc >= qr - BAND)
    if j == nsb - 1:
        nxt = nxt & jnp.logical_not(last_tile)
    return same | nxt


def _band_fwd(name, q, k, v, d):
    l = q.shape[0]
    tq = min(512, l)
    nsb, cur, prev, _, st_cur, _ = _band_specs(l, d, tq)
    scale = HD ** -0.5

    def body(q_ref, kc_ref, kp_ref, vc_ref, vp_ref, o_ref, lse_ref):
        first = pl.program_id(1) == 0
        for i in range(nsb):
            lses = []
            mask = _band_mask_q(i, first)
            for h in range(4):
                cs = slice(h * HD, (h + 1) * HD)
                qv = q_ref[i * BAND:(i + 1) * BAND, cs]
                if i == 0:
                    kk = jnp.concatenate([kp_ref[:, cs], kc_ref[0:BAND, cs]], axis=0)
                    vv = jnp.concatenate([vp_ref[:, cs], vc_ref[0:BAND, cs]], axis=0)
                else:
                    kk = kc_ref[(i - 1) * BAND:(i + 1) * BAND, cs]
                    vv = vc_ref[(i - 1) * BAND:(i + 1) * BAND, cs]
                s = jnp.where(mask, _dot(qv, kk, NT) * scale, NEG)
                m = jnp.max(s, axis=-1, keepdims=True)
                p = jnp.exp(s - m)
                den = jnp.sum(p, axis=-1, keepdims=True)
                o_ref[i * BAND:(i + 1) * BAND, cs] = _dot(p.astype(BF16), vv) / den
                lses.append(m + jnp.log(den))
            lse_ref[i * BAND:(i + 1) * BAND, :] = _lane_pack(lses, (BAND, HD))

    return pl.pallas_call(
        body, name=name, grid=(d, l // tq), in_specs=[cur, cur, prev, cur, prev],
        out_specs=[cur, st_cur],
        out_shape=[jax.ShapeDtypeStruct((l, d * A_GROUP), F32), jax.ShapeDtypeStruct((l, d * HD), F32)],
        compiler_params=_params(("parallel", "parallel")),
    )(q, k, k, v, v)


def _band_dq(name, q, k, v, dy, lse, delta, d):
    l = q.shape[0]
    tq = min(512, l)
    nsb, cur, prev, _, st_cur, _ = _band_specs(l, d, tq)
    scale = HD ** -0.5

    def body(q_ref, kc_ref, kp_ref, vc_ref, vp_ref, dy_ref, lse_ref, dl_ref, dq_ref):
        first = pl.program_id(1) == 0
        for i in range(nsb):
            mask = _band_mask_q(i, first)
            rs = slice(i * BAND, (i + 1) * BAND)
            for h in range(4):
                cs = slice(h * HD, (h + 1) * HD)
                if i == 0:
                    kk = jnp.concatenate([kp_ref[:, cs], kc_ref[0:BAND, cs]], axis=0)
                    vv = jnp.concatenate([vp_ref[:, cs], vc_ref[0:BAND, cs]], axis=0)
                else:
                    kk = kc_ref[(i - 1) * BAND:(i + 1) * BAND, cs]
                    vv = vc_ref[(i - 1) * BAND:(i + 1) * BAND, cs]
                s = jnp.where(mask, _dot(q_ref[rs, cs], kk, NT) * scale, NEG)
                p = jnp.exp(s - lse_ref[rs, h:h + 1])
                dp = _dot(dy_ref[rs, cs], vv, NT)
                ds = p * (dp - dl_ref[rs, h:h + 1])
                dq_ref[rs, cs] = (_dot(ds.astype(BF16), kk) * scale).astype(dq_ref.dtype)

    return pl.pallas_call(
        body, name=name, grid=(d, l // tq),
        in_specs=[cur, cur, prev, cur, prev, cur, st_cur, st_cur], out_specs=cur,
        out_shape=jax.ShapeDtypeStruct((l, d * A_GROUP), BF16),
        compiler_params=_params(("parallel", "parallel")),
    )(q, k, k, v, v, dy, lse, delta)


def _band_dkv(name, q, k, v, dy, lse, delta, d):
    l = q.shape[0]
    tq = min(512, l)
    nsb, cur, _, nxt, st_cur, st_nxt = _band_specs(l, d, tq)
    scale = HD ** -0.5
    ntile = l // tq

    def body(k_ref, v_ref, qc_ref, qn_ref, dyc_ref, dyn_ref, lc_ref, ln_ref, dc_ref, dn_ref,
             dk_ref, dv_ref):
        last = pl.program_id(1) == ntile - 1

        def win(c_ref, n_ref, j, cs):
            if j == nsb - 1:
                return jnp.concatenate([c_ref[j * BAND:(j + 1) * BAND, cs], n_ref[:, cs]], axis=0)
            return c_ref[j * BAND:(j + 2) * BAND, cs]

        allh = slice(0, HD)
        for j in range(nsb):
            mask = _band_mask_k(j, nsb, last)
            rs = slice(j * BAND, (j + 1) * BAND)
            lse_t = win(lc_ref, ln_ref, j, allh).T
            delta_t = win(dc_ref, dn_ref, j, allh).T
            for h in range(4):
                cs = slice(h * HD, (h + 1) * HD)
                qw = win(qc_ref, qn_ref, j, cs)
                dyw = win(dyc_ref, dyn_ref, j, cs)
                st = jnp.where(mask, _dot(k_ref[rs, cs], qw, NT) * scale, NEG)
                pt = jnp.exp(st - lse_t[h:h + 1, :])
                dst = pt * (_dot(v_ref[rs, cs], dyw, NT) - delta_t[h:h + 1, :])
                dv_ref[rs, cs] = _dot(pt.astype(BF16), dyw).astype(dv_ref.dtype)
                dk_ref[rs, cs] = (_dot(dst.astype(BF16), qw) * scale).astype(dk_ref.dtype)

    shp = jax.ShapeDtypeStruct((l, d * A_GROUP), BF16)
    return pl.pallas_call(
        body, name=name, grid=(d, ntile),
        in_specs=[cur, cur, cur, nxt, cur, nxt, st_cur, st_nxt, st_cur, st_nxt],
        out_specs=[cur, cur], out_shape=[shp, shp],
        compiler_params=_params(("parallel", "parallel")),
    )(k, v, q, q, dy, dy, lse, lse, delta, delta)


def _split3(x):
    hi = x.astype(BF16)
    r1 = x - hi.astype(F32)
    mid = r1.astype(BF16)
    lo = (r1 - mid.astype(F32)).astype(BF16)
    return hi, mid, lo


def _fox_prep(z, b):
    h, s = z.shape
    blk = min(512, s)

    def body(z_ref, b_ref, c_ref):
        r = lax.broadcasted_iota(jnp.int32, (blk, blk), 0)
        cidx = lax.broadcasted_iota(jnp.int32, (blk, blk), 1)
        tri = (r <= cidx).astype(BF16)
        carry = jnp.zeros((h, 1), F32)
        for t in range(s // blk):
            zz = z_ref[:, t * blk:(t + 1) * blk] + b_ref[...]
            lf = jnp.minimum(zz, 0.0) - jnp.log(1.0 + jnp.exp(-jnp.abs(zz)))
            hi, mid, lo = _split3(lf)
            cs = _dot(hi, tri) + _dot(mid, tri) + _dot(lo, tri) + carry
            c_ref[:, t * blk:(t + 1) * blk] = cs
            carry = cs[:, blk - 1:blk]

    return pl.pallas_call(body, name="fox_prep", out_shape=jax.ShapeDtypeStruct((h, s), F32))(z, b)


def _fox_prep_bwd(dc, z, b):
    h, s = z.shape
    blk = min(512, s)

    def body(dc_ref, z_ref, b_ref, dz_ref, db_ref):
        r = lax.broadcasted_iota(jnp.int32, (blk, blk), 0)
        cidx = lax.broadcasted_iota(jnp.int32, (blk, blk), 1)
        tri = (r >= cidx).astype(BF16)
        carry = jnp.zeros((h, 1), F32)
        tot = jnp.zeros((h, 1), F32)
        for t in reversed(range(s // blk)):
            hi, mid, lo = _split3(dc_ref[:, t * blk:(t + 1) * blk])
            rc = _dot(hi, tri) + _dot(mid, tri) + _dot(lo, tri) + carry
            carry = rc[:, 0:1]
            zz = z_ref[:, t * blk:(t + 1) * blk] + b_ref[...]
            dz = rc * _sig(-zz)
            dz_ref[:, t * blk:(t + 1) * blk] = dz
            tot = tot + jnp.sum(dz, axis=-1, keepdims=True)
        db_ref[...] = tot

    return pl.pallas_call(
        body, name="fox_prep_bwd",
        out_shape=[jax.ShapeDtypeStruct((h, s), F32), jax.ShapeDtypeStruct((h, 1), F32)])(dc, z, b)


FOX_W = 128
FOX_C = B_HD
FOX_ONE = B_HD + 3
FOX_SUB = 256
FOX_SUB_FWD = 128
FOX_HEADS_PER_STEP = 2


def _head_of_pair(x, hh):
    return x if hh == 0 else pltpu.roll(x, B_HD, 1)


def _fox_pack(u, c_col, t):
    s = u.shape[0]
    nt = s // t
    scale = B_HD ** -0.5

    def body(q_ref, k_ref, v_ref, c_ref, qf_ref, kb_ref, ks_ref, vb_ref, vt_ref):
        lane = lax.broadcasted_iota(jnp.int32, (t, FOX_W), 1)
        qv, kv, vv = [r[...].astype(F32) for r in (q_ref, k_ref, v_ref)]
        for hh in range(2):
            qf_ref[hh] = jnp.where(lane < B_HD, _head_of_pair(qv, hh), B_HD ** 0.5).astype(BF16)
            neg = c_ref[hh] * (-scale)
            hi = neg.astype(BF16).astype(F32)
            mid = (neg - hi).astype(BF16).astype(F32)
            lo = neg - hi - mid
            aux = jnp.where(lane == FOX_C, hi,
                            jnp.where(lane == FOX_C + 1, mid, jnp.where(lane == FOX_C + 2, lo, 0.0)))
            kb = jnp.where(lane < B_HD, _head_of_pair(kv, hh) * scale, aux)
            kb_ref[hh] = kb.astype(BF16)
            ks_ref[hh] = jnp.where(lane == FOX_ONE, 1.0, kb).T.astype(BF16)
            vb = jnp.where(lane < B_HD, _head_of_pair(vv, hh), 1.0)
            vb_ref[hh] = vb.astype(BF16)
            vt_ref[hh] = vb.T.astype(BF16)

    def tok(col0):
        return pl.BlockSpec((t, FOX_W), functools.partial(lambda hp, i, cb: (i, cb + hp), cb=col0 // FOX_W))

    rows = pl.BlockSpec((2, t, FOX_W), lambda hp, i: (hp, i, 0))
    tiles = pl.BlockSpec((2, None, FOX_W, t), lambda hp, i: (hp, i, 0, 0))
    hm = jax.ShapeDtypeStruct((B_HEADS, s, FOX_W), BF16)
    tt = jax.ShapeDtypeStruct((B_HEADS, nt, FOX_W, t), BF16)
    return pl.pallas_call(
        body, name="fox_pack", grid=(B_HEADS // 2, nt),
        in_specs=[tok(C_QB), tok(C_KB), tok(C_VB), pl.BlockSpec((2, t, 1), lambda hp, i: (hp, i, 0))],
        out_specs=[rows, rows, tiles, rows, tiles], out_shape=[hm, hm, tt, hm, tt],
        compiler_params=_params(("parallel", "parallel")),
    )(u, u, u, c_col)


def _fox_pack_bwd(dy, y, t):
    s = dy.shape[0]
    nt = s // t

    def body(do_ref, o_ref, dow_ref, dl_ref):
        lane = lax.broadcasted_iota(jnp.int32, (t, FOX_W), 1)
        lane8 = lax.broadcasted_iota(jnp.int32, (8, FOX_W), 1)
        dov = do_ref[...].astype(F32)
        parts = _split3(dov * o_ref[...].astype(F32))
        for hh in range(2):
            dow_ref[hh] = jnp.where(lane < B_HD, _head_of_pair(dov, hh), 0.0).astype(BF16)
            mask = ((lane8 >= hh * B_HD) & (lane8 < (hh + 1) * B_HD)).astype(BF16)
            row = _dot(mask, parts[0], NT) + _dot(mask, parts[1], NT) + _dot(mask, parts[2], NT)
            dl_ref[hh] = row[0:1, :]

    tok = pl.BlockSpec((t, FOX_W), lambda hp, i: (i, hp))
    return pl.pallas_call(
        body, name="fox_pack_bwd", grid=(B_HEADS // 2, nt), in_specs=[tok, tok],
        out_specs=[pl.BlockSpec((2, t, FOX_W), lambda hp, i: (hp, i, 0)),
                   pl.BlockSpec((2, None, 1, t), lambda hp, i: (hp, i, 0, 0))],
        out_shape=[jax.ShapeDtypeStruct((B_HEADS, s, FOX_W), BF16), jax.ShapeDtypeStruct((B_HEADS, nt, 1, t), F32)],
        compiler_params=_params(("parallel", "parallel")),
    )(dy, y)


def _fox_unpack(dqt, dkw, dvw, t):
    h, nt = dqt.shape[:2]
    s = nt * t

    def body(dq_ref, dk_ref, dv_ref, dqo_ref, dko_ref, dvo_ref, dc_ref):
        lane = lax.broadcasted_iota(jnp.int32, (t, FOX_W), 1)

        def join(a0, a1):
            return jnp.where(lane < B_HD, a0, pltpu.roll(a1, B_HD, 1))

        for hh in range(2):
            dc_ref[hh] = dq_ref[hh][FOX_ONE:FOX_ONE + 1, :] - dk_ref[hh].T[B_HD:B_HD + 1, :]
        dqo_ref[...] = join(dq_ref[0].T, dq_ref[1].T).astype(BF16)
        dko_ref[...] = join(dk_ref[0], dk_ref[1]).astype(BF16)
        dvo_ref[...] = join(dv_ref[0], dv_ref[1]).astype(BF16)

    tok = pl.BlockSpec((t, FOX_W), lambda hp, i: (i, hp))
    rows = pl.BlockSpec((2, t, FOX_W), lambda hp, i: (hp, i, 0))
    shp = jax.ShapeDtypeStruct((s, h * B_HD), BF16)
    return pl.pallas_call(
        body, name="fox_unpack", grid=(h // 2, nt),
        in_specs=[pl.BlockSpec((2, None, FOX_W, t), lambda hp, i: (hp, i, 0, 0)), rows, rows],
        out_specs=[tok, tok, tok, pl.BlockSpec((2, None, 1, t), lambda hp, i: (hp, i, 0, 0))],
        out_shape=[shp, shp, shp, jax.ShapeDtypeStruct((h, nt, 1, t), F32)],
        compiler_params=_params(("parallel", "parallel")),
    )(dqt, dkw, dvw)


FOX_DEAD = -110.0


def _fox_norm2(qf, kb):
    h, s, w = qf.shape
    tm = min(2048, s)

    def body(q_ref, k_ref, qo_ref, ko_ref):
        row = lax.broadcasted_iota(jnp.int32, (w, w), 0)
        ones = (row < B_HD).astype(BF16)
        for x_ref, o_ref in ((q_ref, qo_ref), (k_ref, ko_ref)):
            xv = x_ref[...].astype(F32)
            n2 = _dot((xv * xv).astype(BF16), ones)
            o_ref[...] = jnp.broadcast_to(jnp.max(n2, axis=0, keepdims=True)[:, :1], o_ref.shape)

    spec = pl.BlockSpec((None, tm, w), lambda hh, i: (hh, i, 0))
    ospec = pl.BlockSpec((None, None, 8, 128), lambda hh, i: (hh, i, 0, 0))
    shp = jax.ShapeDtypeStruct((h, s // tm, 8, 128), F32)
    return pl.pallas_call(
        body, name="fox_norm2", grid=(h, s // tm), in_specs=[spec, spec], out_specs=[ospec, ospec],
        out_shape=[shp, shp], compiler_params=_params(("parallel", "parallel")),
    )(qf, kb)


def _fox_bounds(qf, kb, c, t):
    q2, k2 = _fox_norm2(qf, kb)
    g = 2.0 * jnp.sqrt(1.02 * jnp.max(q2[:, :, 0, 0], axis=1) * 1.02 * jnp.max(k2[:, :, 0, 0], axis=1))
    return jnp.concatenate([c[:, ::t], c[:, t - 1::t], g[:, None]], axis=1)


SMEM_SPEC = pl.BlockSpec(memory_space=pltpu.SMEM)


def _fox_fwd(qf, kb, vt4, bounds, t):
    h, s, w = qf.shape
    nt = s // t
    sub = FOX_SUB_FWD
    nsub = t // sub
    nh = FOX_HEADS_PER_STEP

    def body(b_ref, q_ref, k_ref, v_ref, o_ref, lse_ref):
        i = pl.program_id(1)
        krow = lax.broadcasted_iota(jnp.int32, (sub, t), 0)
        qcol = lax.broadcasted_iota(jnp.int32, (sub, t), 1)

        def dead_before(hh):
            head = pl.program_id(0) * nh + hh
            top = b_ref[head, 2 * nt] + b_ref[head, i]
            return lax.fori_loop(
                0, i, lambda jj, n: n + (top - b_ref[head, nt + jj] < FOX_DEAD).astype(jnp.int32), 0)

        j_lo = functools.reduce(jnp.minimum, [dead_before(hh) for hh in range(nh)])

        def tile(j, carry, diag):
            out = []
            for hh in range(nh):
                m, acc = carry[hh]
                qv, vj = q_ref[hh], v_ref[hh, j]
                sts = [_dot(k_ref[hh, pl.ds(pl.multiple_of(j * t + b * sub, sub), sub), :], qv, NT)
                       for b in range(nsub)]
                for b in range(nsub):
                    st = sts[b]
                    if diag:
                        st = jnp.where(krow + b * sub <= qcol, st, NEG)
                    m2 = jnp.maximum(m, jnp.max(st, axis=0, keepdims=True))
                    p = jnp.exp(st - m2).astype(BF16)
                    acc = jnp.exp(m - m2) * acc + _dot(vj[:, b * sub:(b + 1) * sub], p)
                    m = m2
                out.append((m, acc))
            return tuple(out)

        init = tuple((jnp.full((1, t), NEG, F32), jnp.zeros((w, t), F32)) for _ in range(nh))
        carry = lax.fori_loop(j_lo, i, lambda j, c: tile(j, c, False), init)
        outs = []
        for hh, (m, acc) in enumerate(tile(i, carry, True)):
            den = acc[B_HD:B_HD + 1, :]
            outs.append(acc[0:B_HD, :] / den)
            lse_ref[hh] = m + jnp.log(den)
        o_ref[...] = jnp.concatenate(outs, axis=0).T.astype(o_ref.dtype)

    return pl.pallas_call(
        body, name="fox_fwd", grid=(h // nh, nt),
        in_specs=[SMEM_SPEC,
                  pl.BlockSpec((nh, t, w), lambda hh, i: (hh, i, 0)),
                  pl.BlockSpec((nh, s, w), lambda hh, i: (hh, 0, 0)),
                  pl.BlockSpec((nh, nt, w, t), lambda hh, i: (hh, 0, 0, 0))],
        out_specs=[pl.BlockSpec((t, nh * B_HD), lambda hh, i: (i, hh)),
                   pl.BlockSpec((nh, 1, t), lambda hh, i: (hh, 0, i))],
        out_shape=[jax.ShapeDtypeStruct((s, h * B_HD), BF16), jax.ShapeDtypeStruct((h, 1, s), F32)],
        compiler_params=_params(("parallel", "parallel")),
    )(bounds, qf, kb, vt4)


def _fox_bwd(qf, dow, lse_row, delta_row, kb, kst4, vb, bounds, t):
    h, s, w = qf.shape
    nt = s // t
    nsub = t // FOX_SUB
    nh = FOX_HEADS_PER_STEP

    def body(b_ref, q_ref, do_ref, lse_ref, dl_ref, k_ref, kt_ref, v_ref, dqt_ref, dk_ref, dv_ref, dk_acc, dv_acc):
        j = pl.program_id(1)

        def alive_after(hh):
            head = pl.program_id(0) * nh + hh
            top = b_ref[head, 2 * nt] - b_ref[head, nt + j]
            return lax.fori_loop(
                j + 1, nt, lambda ii, n: n + (top + b_ref[head, ii] >= FOX_DEAD).astype(jnp.int32), 0)

        i_hi = j + 1 + functools.reduce(jnp.maximum, [alive_after(hh) for hh in range(nh)])

        @pl.when(j == 0)
        def _():
            dqt_ref[...] = jnp.zeros_like(dqt_ref)

        dk_acc[...] = jnp.zeros_like(dk_acc)
        dv_acc[...] = jnp.zeros_like(dv_acc)
        krow = lax.broadcasted_iota(jnp.int32, (FOX_SUB, t), 0)
        qcol = lax.broadcasted_iota(jnp.int32, (FOX_SUB, t), 1)
        subs = [slice(b * FOX_SUB, (b + 1) * FOX_SUB) for b in range(nsub)]

        def tile(i, diag):
            i0 = pl.multiple_of(i * t, t)
            for hh in range(nh):
                qi, doi = q_ref[hh, pl.ds(i0, t), :], do_ref[hh, pl.ds(i0, t), :]
                lse, dl = lse_ref[hh, i], dl_ref[hh, i]
                sts = [_dot(k_ref[hh, rs, :], qi, NT) for rs in subs]
                dps = [_dot(v_ref[hh, rs, :], doi, NT) for rs in subs]
                dq = None
                for b, rs in enumerate(subs):
                    st = sts[b] - lse
                    if diag:
                        st = jnp.where(krow + b * FOX_SUB <= qcol, st, NEG)
                    pt = jnp.exp(st)
                    dsb = (pt * (dps[b] - dl)).astype(BF16)
                    dv_acc[hh, rs, :] += _dot(pt.astype(BF16), doi)
                    dk_acc[hh, rs, :] += _dot(dsb, qi)
                    part = _dot(kt_ref[hh, :, rs], dsb)
                    dq = part if dq is None else dq + part
                dqt_ref[hh, i] += dq

        def step(i, carry):
            tile(i, False)
            return carry

        tile(j, True)
        lax.fori_loop(j + 1, i_hi, step, 0)
        dk_ref[...] = dk_acc[...] * (B_HD ** -0.5)
        dv_ref[...] = dv_acc[...]

    full = pl.BlockSpec((nh, s, w), lambda hh, j: (hh, 0, 0))
    rowst = pl.BlockSpec((nh, nt, 1, t), lambda hh, j: (hh, 0, 0, 0))
    tl = pl.BlockSpec((nh, t, w), lambda hh, j: (hh, j, 0))
    return pl.pallas_call(
        body, name="fox_bwd", grid=(h // nh, nt),
        in_specs=[SMEM_SPEC, full, full, rowst, rowst, tl,
                  pl.BlockSpec((nh, None, w, t), lambda hh, j: (hh, j, 0, 0)), tl],
        out_specs=[pl.BlockSpec((nh, nt, w, t), lambda hh, j: (hh, 0, 0, 0)), tl, tl],
        out_shape=[jax.ShapeDtypeStruct((h, nt, w, t), F32), jax.ShapeDtypeStruct((h, s, w), F32),
                   jax.ShapeDtypeStruct((h, s, w), F32)],
        scratch_shapes=[pltpu.VMEM((nh, t, w), F32), pltpu.VMEM((nh, t, w), F32)],
        compiler_params=_params(("parallel", "arbitrary")),
    )(bounds, qf, dow, lse_row, delta_row, kb, kst4, vb)


def _mem_fwd(u, mkv, tq=512):
    s = u.shape[0]
    scale = HD ** -0.5

    def body(q_ref, mk_ref, mv_ref, o_ref, lse_ref):
        lses = []
        for h in range(4):
            cs = slice(h * HD, (h + 1) * HD)
            sc = _dot(q_ref[:, cs], mk_ref[:, cs], NT) * scale
            m = jnp.max(sc, axis=-1, keepdims=True)
            p = jnp.exp(sc - m)
            den = jnp.sum(p, axis=-1, keepdims=True)
            o_ref[:, cs] = (_dot(p.astype(BF16), mv_ref[:, cs]) / den).astype(o_ref.dtype)
            lses.append(m + jnp.log(den))
        lse_ref[...] = _lane_pack(lses, (tq, HD))

    return pl.pallas_call(
        body, name="mem_fwd", grid=(s // tq,),
        in_specs=[pl.BlockSpec((tq, 512), lambda i: (i, C_QM // 512)),
                  pl.BlockSpec((N_MEM, 512), lambda i: (0, 0)),
                  pl.BlockSpec((N_MEM, 512), lambda i: (0, 1))],
        out_specs=[pl.BlockSpec((tq, 512), lambda i: (i, 0)), pl.BlockSpec((tq, HD), lambda i: (i, 0))],
        out_shape=[jax.ShapeDtypeStruct((s, 512), BF16), jax.ShapeDtypeStruct((s, HD), F32)],
        compiler_params=_params(("parallel",)),
    )(u, mkv, mkv)


def _mem_bwd(u, mkv, o, do, lse, tq=512):
    s = u.shape[0]
    scale = HD ** -0.5

    def body(q_ref, mk_ref, mv_ref, o_ref, do_ref, lse_ref, dq_ref, dmk_ref, dmv_ref):
        @pl.when(pl.program_id(0) == 0)
        def _():
            dmk_ref[...] = jnp.zeros_like(dmk_ref)
            dmv_ref[...] = jnp.zeros_like(dmv_ref)

        for h in range(4):
            cs = slice(h * HD, (h + 1) * HD)
            qv, dov = q_ref[:, cs], do_ref[:, cs]
            sc = _dot(qv, mk_ref[:, cs], NT) * scale
            p = jnp.exp(sc - lse_ref[:, h:h + 1])
            delta = jnp.sum(dov.astype(F32) * o_ref[:, cs].astype(F32), axis=-1, keepdims=True)
            ds = p * (_dot(dov, mv_ref[:, cs], NT) - delta)
            dsb = ds.astype(BF16)
            dq_ref[:, cs] = (_dot(dsb, mk_ref[:, cs]) * scale).astype(dq_ref.dtype)
            dmk_ref[:, cs] += _dot(dsb, qv, TN) * scale
            dmv_ref[:, cs] += _dot(p.astype(BF16), dov, TN)

    row = pl.BlockSpec((tq, 512), lambda i: (i, 0))
    acc = pl.BlockSpec((N_MEM, 512), lambda i: (0, 0))
    return pl.pallas_call(
        body, name="mem_bwd", grid=(s // tq,),
        in_specs=[pl.BlockSpec((tq, 512), lambda i: (i, C_QM // 512)),
                  pl.BlockSpec((N_MEM, 512), lambda i: (0, 0)),
                  pl.BlockSpec((N_MEM, 512), lambda i: (0, 1)),
                  row, row, pl.BlockSpec((tq, HD), lambda i: (i, 0))],
        out_specs=[row, acc, acc],
        out_shape=[jax.ShapeDtypeStruct((s, 512), BF16), jax.ShapeDtypeStruct((N_MEM, 512), F32),
                   jax.ShapeDtypeStruct((N_MEM, 512), F32)],
        compiler_params=_params(("arbitrary",)),
    )(u, mkv, mkv, o, do, lse)


def _local_step(x, mem, pos, target, g_pre, g_post, g_mem, w_main, w_fb, b_forget, b_merge,
                w_mem_kv, w_ba, w_bb, w_bm, w_out, exchange=None):
    s = x.shape[0]
    t_fox = min(512, s)
    nt = s // t_fox
    half = ROT_DIM // 2
    inv = ROPE_THETA ** (-jnp.arange(half, dtype=F32) / half)
    inv128 = jnp.concatenate([inv, inv, jnp.zeros((HD - ROT_DIM,), F32)]).reshape(1, HD)

    h = _rms_fwd("norm_pre", x, g_pre)
    u = _mm("proj_in", h, w_main, "nn", BF16)
    ufb = _mm("proj_fb", h, w_fb, "nn", F32)
    memn = _rms_fwd("norm_mem", mem, g_mem)
    mkv = _mm("proj_mem", memn, w_mem_kv, "nn", BF16)

    qkv = _rope_fwd(u, pos, inv128)
    views = [tuple(qkv[3 * g:3 * g + 3]) for g in range(3)]
    os_, lses = [], []
    for g, d in enumerate(DILATIONS):
        o_g, lse_g = _band_fwd("band_fwd%d" % g, *views[g], d)
        os_.append((o_g, d * A_GROUP, 0, d))
        lses.append((lse_g, d * HD, 0, d))

    def merge_a(o1, o2, o3, l1, l2, l3, za, *scr):
        o1, o2, o3 = [_from_class(o, scr, d) for o, d in zip((o1, o2, o3), DILATIONS)]
        l1, l2, l3 = [_from_class(lv, scr, d) for lv, d in zip((l1, l2, l3), DILATIONS)]
        ys, tots = [], []
        for hh in range(4):
            cs, hs = slice(hh * HD, (hh + 1) * HD), slice(hh, hh + 1)
            mx = jnp.maximum(jnp.maximum(l1[:, hs], l2[:, hs]), l3[:, hs])
            e1, e2, e3 = jnp.exp(l1[:, hs] - mx), jnp.exp(l2[:, hs] - mx), jnp.exp(l3[:, hs] - mx)
            den = e1 + e2 + e3
            ys.append((e1 * o1[:, cs] + e2 * o2[:, cs] + e3 * o3[:, cs]) / den)
            tots.append(mx + jnp.log(den))
        y = jnp.concatenate(ys, axis=1)
        zf = za.astype(F32)
        tot = _lane_pack(tots, l1.shape)
        return (y, y * (zf * _sig(zf))) + tuple(_to_class(tot, scr, d) for d in DILATIONS)

    res = _rows("merge_a", merge_a, os_ + lses + [(u, 512, C_ZA // 512)], [],
                [(512, BF16), (512, BF16)] + [(d * HD, F32, d) for d in DILATIONS], tm=ROPE_TM,
                scratch=_class_scratch(ROPE_TM))
    y_a, yg_a, lse_a = res[0], res[1], res[2:5]

    zrow = ufb[:, :B_HEADS].T
    c = _fox_prep(zrow, b_forget.reshape(B_HEADS, 1))
    qf, kb, kst4, vb, vt4 = _fox_pack(u, c.reshape(B_HEADS, s, 1), t_fox)
    bounds = _fox_bounds(qf, kb, c, t_fox)
    y_b, lse_b = _fox_fwd(qf, kb, vt4, bounds, t_fox)

    y_m, lse_m = _mem_fwd(u, mkv)

    def gate(y, z):
        zf = z.astype(F32)
        return (y.astype(F32) * (zf * _sig(zf)),)

    yg_b = _rows("gate_b", gate, [y_b, (u, 512, C_ZB // 512)], [], [(512, BF16)])[0]
    yg_m = _rows("gate_m", gate, [y_m, (u, 512, C_ZM // 512)], [], [(512, BF16)])[0]

    br_a = _mm("branch_a", yg_a, w_ba, "nn", BF16)
    br_b = _mm("branch_b", yg_b, w_bb, "nn", BF16)
    br_m = _mm("branch_m", yg_m, w_bm, "nn", BF16)
    gl = [(u, 1024, C_GL // 1024 + i) for i in range(3)]
    bm3 = b_merge.reshape(3, D_MODEL)

    def merge(g0, g1, g2, b0, b1, b2, bm):
        tot = 0.0
        for i, (gv, bv) in enumerate(((g0, b0), (g1, b1), (g2, b2))):
            tot = tot + _sig(gv.astype(F32) + bm[i:i + 1, :]) * bv.astype(F32)
        return (tot,)

    merged = _rows("merge_gates", merge, gl + [br_a, br_b, br_m], [bm3], [(D_MODEL, BF16)])[0]
    out = _mm("proj_out", merged, w_out, "nn", F32)

    def tail(xv, ov, tv, gv):
        r = lax.rsqrt(jnp.mean(ov * ov, axis=-1, keepdims=True) + EPS)
        n = ov * r
        err = xv + n * gv - tv
        dy = err * (1.0 / D_MODEL)
        dn = dy * gv
        dout = r * (dn - n * jnp.mean(dn * n, axis=-1, keepdims=True))
        return (dy, dout, jnp.sum(0.5 * err * err * (1.0 / D_MODEL), axis=0, keepdims=True),
                jnp.sum(dy * n, axis=0, keepdims=True))

    dy, dout, loss_lanes, g_post_grad = _rows(
        "tail", tail, [x, out, target], [g_post], [(D_MODEL, F32), (D_MODEL, BF16)],
        reds=[D_MODEL, D_MODEL], tm=256)

    dmerged = _mm("d_merged", dout, w_out, "nt", BF16)
    gw_out = _mm("g_w_out", merged, dout, "tn", F32)

    def merge_bwd(dm, g0, g1, g2, b0, b1, b2, bm):
        dmf = dm.astype(F32)
        dbs, dgs, sums = [], [], []
        for i, (gv, bv) in enumerate(((g0, b0), (g1, b1), (g2, b2))):
            sg = _sig(gv.astype(F32) + bm[i:i + 1, :])
            dbs.append(dmf * sg)
            dg = dmf * bv.astype(F32) * sg * (1.0 - sg)
            dgs.append(dg)
            sums.append(jnp.sum(dg, axis=0, keepdims=True))
        return tuple(dbs + dgs + sums)

    res = _rows("merge_bwd", merge_bwd, [dmerged] + gl + [br_a, br_b, br_m], [bm3],
                [(D_MODEL, BF16)] * 6, reds=[D_MODEL] * 3, tm=256)
    dbr, dgl, g_bmerge = res[0:3], res[3:6], jnp.concatenate(res[6:9], axis=1)

    dyg, gw_branch = [], []
    for nm, dbv, wv, ygv in (("a", dbr[0], w_ba, yg_a), ("b", dbr[1], w_bb, yg_b), ("m", dbr[2], w_bm, yg_m)):
        dyg.append(_mm("d_yg_" + nm, dbv, wv, "nt", BF16))
        gw_branch.append(_mm("g_w_branch_" + nm, ygv, dbv, "tn", F32))

    def gate_bwd(dg, y, z):
        dgf, yf, zf = dg.astype(F32), y.astype(F32), z.astype(F32)
        sg = _sig(zf)
        return dgf * (zf * sg), dgf * yf * (sg * (1.0 + zf * (1.0 - sg)))

    def gate_bwd_a(dg, y, z, *scr):
        dyv, dz = gate_bwd(dg, y, z)
        prod = dyv * y.astype(F32)
        dl = [jnp.sum(prod[:, hh * HD:(hh + 1) * HD], axis=-1, keepdims=True) for hh in range(4)]
        delta = _lane_pack(dl, (dg.shape[0], HD))
        return ((dz,) + tuple(_to_class(dyv, scr, d) for d in DILATIONS)
                + tuple(_to_class(delta, scr, d) for d in DILATIONS))

    res = _rows("gate_bwd_a", gate_bwd_a, [dyg[0], y_a, (u, 512, C_ZA // 512)], [],
                [(512, BF16)] + [(d * A_GROUP, BF16, d) for d in DILATIONS] + [(d * HD, F32, d) for d in DILATIONS],
                tm=ROPE_TM, scratch=_class_scratch(ROPE_TM))
    dz_a, dy_a, delta_a = res[0], res[1:4], res[4:7]
    dy_b, dz_b = _rows("gate_bwd_b", gate_bwd, [dyg[1], y_b, (u, 512, C_ZB // 512)], [],
                       [(512, BF16), (512, BF16)])
    dy_m, dz_m = _rows("gate_bwd_m", gate_bwd, [dyg[2], y_m, (u, 512, C_ZM // 512)], [],
                       [(512, BF16), (512, BF16)])

    dq_m, dmk, dmv = _mem_bwd(u, mkv, y_m, dy_m, lse_m)
    dmkv = jnp.concatenate([dmk, dmv], axis=1)
    gw_mem_kv = _mm("g_w_mem_kv", memn, dmkv, "tn", F32)
    dmemn = _mm("d_memn", dmkv, w_mem_kv, "nt", F32)

    def mem_gain_grad(mv, dv):
        r = lax.rsqrt(jnp.mean(mv * mv, axis=-1, keepdims=True) + EPS)
        return (jnp.sum(dv * mv * r, axis=0, keepdims=True),)

    g_mem_grad = _rows("g_norm_mem", mem_gain_grad, [mem, dmemn], [], [], reds=[D_MODEL], tm=N_MEM)[0]

    dow, delta_b = _fox_pack_bwd(dy_b, y_b, t_fox)
    dqt, dkw, dvw = _fox_bwd(qf, dow, lse_b.reshape(B_HEADS, nt, 1, t_fox), delta_b, kb, kst4, vb, bounds, t_fox)
    dqb, dkb, dvb, dc = _fox_unpack(dqt, dkw, dvw, t_fox)
    dzrow, g_bforget = _fox_prep_bwd(dc.reshape(B_HEADS, s), zrow, b_forget.reshape(B_HEADS, 1))
    dfb = jnp.zeros((s, HD), BF16).at[:, :B_HEADS].set(dzrow.T.astype(BF16))

    dqs, dks, dvs = [], [], []
    for g, d in enumerate(DILATIONS):
        qv, kv, vv = views[g]
        dqs.append(_band_dq("band_dq%d" % g, qv, kv, vv, dy_a[g], lse_a[g], delta_a[g], d))
        dk_g, dv_g = _band_dkv("band_dkv%d" % g, qv, kv, vv, dy_a[g], lse_a[g], delta_a[g], d)
        dks.append(dk_g)
        dvs.append(dv_g)
    dqa, dka, dva = _rope_bwd(dqs, dks, dvs, pos, inv128)

    du = jnp.concatenate(
        [dqa, dka, dva, dz_a, dqb, dkb, dvb,
                            dz_b, dq_m, dz_m] + list(dgl), axis=1)

    gw_main = _mm("g_w_main", h.T, du, "nn", F32, tk=2048)
    gw_fb = _mm("g_w_fb", h, dfb, "tn", F32)
    gw_in = jnp.concatenate([gw_main[:, :FB_ORIG], gw_fb[:, :B_HEADS], gw_main[:, FB_ORIG:]], axis=1)
    grads = dict(norm_post_g=g_post_grad, norm_mem_g=g_mem_grad, w_in=gw_in,
                 b_forget=g_bforget.reshape(1, B_HEADS), b_merge=g_bmerge, w_mem_kv=gw_mem_kv,
                 w_branch_a=gw_branch[0], w_branch_b=gw_branch[1], w_branch_m=gw_branch[2], w_out=gw_out)
    side = exchange(grads) if exchange else None
    dh_main = _mm("d_h", du, w_main, "nt", F32, tk=2816, side=side)
    landed = None
    if side:
        dh_main, landed = dh_main[0], dh_main[1:]
    dh_fb = _mm("d_h_fb", dfb, w_fb, "nt", F32)

    def pre_bwd(xv, d1, d2, dyv, gv):
        r = lax.rsqrt(jnp.mean(xv * xv, axis=-1, keepdims=True) + EPS)
        n = xv * r
        dhv = d1 + d2
        dn = dhv * gv
        dx = r * (dn - n * jnp.mean(dn * n, axis=-1, keepdims=True))
        return dyv + dx, jnp.sum(dhv * n, axis=0, keepdims=True)

    grad_x, g_pre_grad = _rows("norm_pre_bwd", pre_bwd, [x, dh_main, dh_fb, dy], [g_pre],
                               [(D_MODEL, F32)], reds=[D_MODEL], tm=256)

    grads["norm_pre_g"] = g_pre_grad
    return loss_lanes, grad_x, grads, landed


HBM_SPEC = pl.BlockSpec(memory_space=pltpu.HBM)


def _place():
    x, y, c = lax.axis_index("x"), lax.axis_index("y"), lax.axis_index("c")
    chips = [(1 - x, y), (x, 1 - y), (1 - x, 1 - y)]
    return x, y, c, 2 * x + y, chips


N_CHUNKS = 4


def _units(parts, row_axis):
    units = []
    for i, a in enumerate(parts):
        ch = a.shape[row_axis] // N_CHUNKS
        units += [(i, pl.ds(k * ch, ch)) for k in range(N_CHUNKS)]
    return units


def _gather_weights(parts):
    n = len(parts)
    units = _units(parts, 1)
    nu = len(units)
    via_y = [(u % N_CHUNKS) < N_CHUNKS // 2 for u in range(nu)]

    def body(*refs):
        srcs, outs = refs[:n], refs[n:2 * n]
        send_sems, recv_sems = refs[2 * n:]
        x, y, c, p, _ = _place()
        me, sib = (x, y, c), (x, y, 1 - c)
        xn, yn, dg = (1 - x, y), (x, 1 - y), (1 - x, 1 - y)

        def cp(u, k, chip, half, to, from_src=False):
            i, rs = units[u]
            dst = outs[i].at[2 * chip[0] + chip[1], half, rs]
            return pltpu.make_async_remote_copy(
                src_ref=srcs[i].at[half, rs] if from_src else dst, dst_ref=dst, send_sem=send_sems.at[u, k],
                recv_sem=recv_sems.at[u, k], device_id=to, device_id_type=MESH)

        sent = []

        def go(copy):
            copy.start()
            sent.append(copy)

        for u in range(nu):
            go(cp(u, 0, (x, y), c, (*xn, c), from_src=True))
            go(cp(u, 1, (x, y), c, (*yn, c), from_src=True))
        for u in range(nu):
            cp(u, 0, xn, c, me).wait_recv()
            go(cp(u, 4, xn, c, sib))
            if via_y[u]:
                go(cp(u, 2, xn, c, (*yn, c)))
            cp(u, 1, yn, c, me).wait_recv()
            go(cp(u, 5, yn, c, sib))
            if not via_y[u]:
                go(cp(u, 3, yn, c, (*xn, c)))
        for u in range(nu):
            cp(u, 2 if via_y[u] else 3, dg, c, me).wait_recv()
            go(cp(u, 6, dg, c, sib))
        for u in range(nu):
            for k, chip in ((4, xn), (5, yn), (6, dg)):
                cp(u, k, chip, 1 - c, me).wait_recv()
        for copy in sent:
            copy.wait_send()

    return pl.pallas_call(
        body, name="gather_weights", in_specs=[HBM_SPEC] * n, out_specs=[HBM_SPEC] * n,
        out_shape=[jax.ShapeDtypeStruct((N_CHIPS,) + a.shape, a.dtype) for a in parts],
        scratch_shapes=[pltpu.SemaphoreType.DMA((nu, 7)), pltpu.SemaphoreType.DMA((nu, 7))],
    )(*parts)


def _swap_with_sibling(parts):
    n = len(parts)
    units = _units(parts, 2)

    def body(*refs):
        srcs, outs = refs[:n], refs[n:2 * n]
        send_sems, recv_sems = refs[2 * n:]
        x, y, c, _, _ = _place()
        cps = [pltpu.make_async_remote_copy(
            src_ref=srcs[i].at[q, 1 - c, rs], dst_ref=outs[i].at[q, rs], send_sem=send_sems.at[u, q],
            recv_sem=recv_sems.at[u, q], device_id=(x, y, 1 - c), device_id_type=MESH)
            for q in range(N_CHIPS) for u, (i, rs) in enumerate(units)]
        for cpy in cps:
            cpy.start()
        for cpy in cps:
            cpy.wait()

    return pl.pallas_call(
        body, name="swap_with_sibling", in_specs=[HBM_SPEC] * n, out_specs=[HBM_SPEC] * n,
        out_shape=[jax.ShapeDtypeStruct(a.shape[:1] + a.shape[2:], a.dtype) for a in parts],
        scratch_shapes=[pltpu.SemaphoreType.DMA((len(units), N_CHIPS)),
                        pltpu.SemaphoreType.DMA((len(units), N_CHIPS))],
    )(*parts)


def _scatter_to_owners(parts):
    n = len(parts)
    units = _units(parts, 1)

    def copies(srcs, outs, send_sems, recv_sems, incoming):
        x, y, c, p, chips = _place()
        return [pltpu.make_async_remote_copy(
            src_ref=srcs[i].at[2 * cx + cy, rs], dst_ref=outs[i].at[(2 * cx + cy) if incoming else p, rs],
            send_sem=send_sems.at[u, j], recv_sem=recv_sems.at[u, j], device_id=(cx, cy, c), device_id_type=MESH)
            for u, (i, rs) in enumerate(units) for j, (cx, cy) in enumerate(chips)]

    def start(ins, outs, scratch):
        for cpy in copies(ins, outs, *scratch, incoming=False):
            cpy.start()

    def wait(ins, outs, scratch):
        for cpy in copies(ins, outs, *scratch, incoming=True):
            cpy.wait_recv()
        for cpy in copies(ins, outs, *scratch, incoming=False):
            cpy.wait_send()

    return dict(ins=list(parts), outs=[jax.ShapeDtypeStruct(a.shape, a.dtype) for a in parts],
                scratch=[pltpu.SemaphoreType.DMA((len(units), 3)), pltpu.SemaphoreType.DMA((len(units), 3))],
                start=start, wait=wait)


def _share_with_sibling(parts):
    n = len(parts)
    units = _units(parts, 1)

    def body(*refs):
        srcs, outs = refs[:n], refs[n:2 * n]
        send_sems, recv_sems = refs[2 * n:]
        x, y, c, _, _ = _place()
        sends = [pltpu.make_async_remote_copy(
            src_ref=srcs[i].at[0, rs], dst_ref=outs[i].at[c, rs], send_sem=send_sems.at[u],
            recv_sem=recv_sems.at[u], device_id=(x, y, 1 - c), device_id_type=MESH)
            for u, (i, rs) in enumerate(units)]
        for cpy in sends:
            cpy.start()
        for u, (i, rs) in enumerate(units):
            pltpu.make_async_remote_copy(
                src_ref=srcs[i].at[0, rs], dst_ref=outs[i].at[1 - c, rs], send_sem=send_sems.at[u],
                recv_sem=recv_sems.at[u], device_id=(x, y, 1 - c), device_id_type=MESH).wait_recv()
        for cpy in sends:
            cpy.wait_send()

    return pl.pallas_call(
        body, name="share_with_sibling", in_specs=[HBM_SPEC] * n, out_specs=[HBM_SPEC] * n,
        out_shape=[jax.ShapeDtypeStruct((2,) + a.shape[1:], a.dtype) for a in parts],
        scratch_shapes=[pltpu.SemaphoreType.DMA((len(units),)), pltpu.SemaphoreType.DMA((len(units),))],
    )(*parts)


def _sum_small(v):
    def body(v_ref, out_ref, buf, send_sems, recv_sems):
        x, y, c, _, _ = _place()
        me = 4 * x + 2 * y + c
        buf[me] = v_ref[...]
        flips = [(dx, dy, dc) for dx in (0, 1) for dy in (0, 1) for dc in (0, 1)][1:]
        sends = []
        for k, (dx, dy, dc) in enumerate(flips):
            cpy = pltpu.make_async_remote_copy(
                src_ref=v_ref, dst_ref=buf.at[me], send_sem=send_sems.at[k], recv_sem=recv_sems.at[k],
                device_id=((x + dx) % 2, (y + dy) % 2, (c + dc) % 2), device_id_type=MESH)
            cpy.start()
            sends.append(cpy)
        for k, (dx, dy, dc) in enumerate(flips):
            px, py, pc = (x + dx) % 2, (y + dy) % 2, (c + dc) % 2
            pltpu.make_async_remote_copy(
                src_ref=v_ref, dst_ref=buf.at[4 * px + 2 * py + pc], send_sem=send_sems.at[k],
                recv_sem=recv_sems.at[k], device_id=(px, py, pc), device_id_type=MESH).wait_recv()
        for cpy in sends:
            cpy.wait_send()
        tot = buf[0]
        for i in range(1, N_DEV):
            tot = tot + buf[i]
        out_ref[...] = tot

    return pl.pallas_call(
        body, name="sum_small", out_shape=jax.ShapeDtypeStruct(v.shape, v.dtype),
        in_specs=[pl.BlockSpec(memory_space=pltpu.VMEM)], out_specs=pl.BlockSpec(memory_space=pltpu.VMEM),
        scratch_shapes=[pltpu.VMEM((N_DEV,) + v.shape, v.dtype), pltpu.SemaphoreType.DMA((N_DEV - 1,)),
                        pltpu.SemaphoreType.DMA((N_DEV - 1,))],
    )(v)


def _add_slabs(name, terms, out_dtype):
    arr0 = terms[0][0]
    n = arr0.shape[0] if terms[0][1] is None else 1
    _, r, w = arr0.shape
    tr = 64
    specs = []
    for _, slab in terms:
        if slab is None:
            specs.append(pl.BlockSpec((None, tr, w), lambda i, j: (i, j, 0)))
        else:
            specs.append(pl.BlockSpec((None, tr, w), functools.partial(lambda i, j, sl: (sl, j, 0), sl=slab)))

    def body(*refs):
        tot = refs[0][...].astype(F32)
        for rf in refs[1:-1]:
            tot = tot + rf[...].astype(F32)
        refs[-1][...] = tot.astype(out_dtype)

    return pl.pallas_call(
        body, name=name, grid=(n, r // tr), in_specs=specs,
        out_specs=pl.BlockSpec((None, tr, w), lambda i, j: (i, j, 0)),
        out_shape=jax.ShapeDtypeStruct((n, r, w), out_dtype),
        compiler_params=_params(("parallel", "parallel")),
    )(*[a for a, _ in terms])


def _add_pair(name, halves, got, c):
    nq, _, r, w = halves.shape
    tr = 64

    def body(c_ref, a_ref, b_ref, o_ref):
        o_ref[...] = (a_ref[...] + b_ref[...]).astype(o_ref.dtype)

    grid_spec = pltpu.PrefetchScalarGridSpec(
        num_scalar_prefetch=1, grid=(nq, r // tr),
        in_specs=[pl.BlockSpec((None, None, tr, w), lambda i, j, c_ref: (i, c_ref[0], j, 0)),
                  pl.BlockSpec((None, tr, w), lambda i, j, c_ref: (i, j, 0))],
        out_specs=pl.BlockSpec((None, tr, w), lambda i, j, c_ref: (i, j, 0)))
    return pl.pallas_call(
        body, name=name, grid_spec=grid_spec, out_shape=jax.ShapeDtypeStruct((nq, r, w), BF16),
        compiler_params=_params(("parallel", "parallel")),
    )(jnp.reshape(c, (1,)).astype(jnp.int32), halves, got)


def _adamw(name, w, g, m, v, tm):
    def fn(wv, gv, mv, vv):
        m2 = ADAM_B1 * mv + (1.0 - ADAM_B1) * gv
        v2 = ADAM_B2 * vv + (1.0 - ADAM_B2) * (gv * gv)
        m_hat = m2 / (1.0 - ADAM_B1 ** ADAM_STEP)
        v_hat = v2 / (1.0 - ADAM_B2 ** ADAM_STEP)
        return -ADAM_LR * (m_hat / (jnp.sqrt(v_hat) + ADAM_EPS) + ADAM_WD * wv), m2, v2
    c = w.shape[1]
    return _rows(name, fn, [w, g, m, v], [], [(c, F32)] * 3, tm=tm)


REST_ROWS = 256 + 3 * 128 + 256
REST_SPLITS = (("w_mem_kv", 0, 256), ("w_branch_a", 256, 128), ("w_branch_b", 384, 128),
               ("w_branch_m", 512, 128), ("w_out", 640, 256))


def _rest_pack(t):
    return jnp.concatenate([t[n].reshape(rows, D_MODEL) for n, _, rows in REST_SPLITS], axis=0)


def _rest_unpack(a, shapes):
    return {n: a[r0:r0 + rows].reshape(shapes[n]) for n, r0, rows in REST_SPLITS}


def _small_pack(pre, post, memg, bforget, bmerge):
    pad = jnp.zeros((1, D_MODEL - B_HEADS), F32)
    return jnp.concatenate([pre, post, memg, bmerge.reshape(3, D_MODEL),
                            jnp.concatenate([bforget, pad], axis=1), jnp.zeros((1, D_MODEL), F32)], axis=0)


def _small_unpack(s8):
    return dict(norm_pre_g=s8[0:1], norm_post_g=s8[1:2], norm_mem_g=s8[2:3],
                b_merge=s8[3:6].reshape(1, 3 * D_MODEL), b_forget=s8[6:7, :B_HEADS])


WEIGHTS = ("norm_pre_g", "norm_post_g", "norm_mem_g", "w_in", "b_forget", "b_merge", "w_mem_kv",
           "w_branch_a", "w_branch_b", "w_branch_m", "w_out")
SMALL = ("norm_pre_g", "norm_post_g", "norm_mem_g", "b_forget", "b_merge")


def kernel(x, mem, positions, norm_pre_g, norm_post_g, norm_mem_g, w_in, b_forget, b_merge, w_mem_kv, w_branch_a, w_branch_b, w_branch_m, w_out, loss_target, m_norm_pre_g, m_norm_post_g, m_norm_mem_g, m_w_in, m_b_forget, m_b_merge, m_w_mem_kv, m_w_branch_a, m_w_branch_b, m_w_branch_m, m_w_out, v_norm_pre_g, v_norm_post_g, v_norm_mem_g, v_w_in, v_b_forget, v_b_merge, v_w_mem_kv, v_w_branch_a, v_w_branch_b, v_w_branch_m, v_w_out):
    w = dict(norm_pre_g=norm_pre_g, norm_post_g=norm_post_g, norm_mem_g=norm_mem_g, w_in=w_in[0],
             b_forget=b_forget, b_merge=b_merge, w_mem_kv=w_mem_kv[0], w_branch_a=w_branch_a[0],
             w_branch_b=w_branch_b[0], w_branch_m=w_branch_m[0], w_out=w_out[0])
    mo = dict(norm_pre_g=m_norm_pre_g, norm_post_g=m_norm_post_g, norm_mem_g=m_norm_mem_g, w_in=m_w_in[0],
              b_forget=m_b_forget, b_merge=m_b_merge, w_mem_kv=m_w_mem_kv[0], w_branch_a=m_w_branch_a[0],
              w_branch_b=m_w_branch_b[0], w_branch_m=m_w_branch_m[0], w_out=m_w_out[0])
    vo = dict(norm_pre_g=v_norm_pre_g, norm_post_g=v_norm_post_g, norm_mem_g=v_norm_mem_g, w_in=v_w_in[0],
              b_forget=v_b_forget, b_merge=v_b_merge, w_mem_kv=v_w_mem_kv[0], w_branch_a=v_w_branch_a[0],
              w_branch_b=v_w_branch_b[0], w_branch_m=v_w_branch_m[0], w_out=v_w_out[0])
    s = x.shape[1]
    c = lax.axis_index("c")

    chip = 2 * lax.axis_index("x") + lax.axis_index("y")

    def put(whole, own, slot):
        return lax.dynamic_update_index_in_dim(whole, own.astype(whole.dtype), slot, 0)

    own_w = [w["w_in"].astype(BF16).reshape(2, D_MODEL // 2, SHARD_COLS),
             _rest_pack(w).astype(BF16).reshape(2, REST_ROWS // 2, D_MODEL)]
    all_in, all_rest = _gather_weights(own_w)
    all_in = all_in.reshape(N_CHIPS, D_MODEL, SHARD_COLS)
    own_in, own_rest = own_w[0].reshape(D_MODEL, SHARD_COLS), own_w[1].reshape(REST_ROWS, D_MODEL)
    w_in_f = jnp.concatenate([jnp.where(chip == p, own_in, all_in[p]) for p in range(N_CHIPS)], axis=1)
    all_rest = all_rest.reshape(N_CHIPS, REST_ROWS, D_MODEL)
    all_rest = jnp.stack([jnp.where(chip == p, own_rest, all_rest[p]) for p in range(N_CHIPS)])
    w_kv_f = all_rest[:, 0:256].reshape(D_MODEL, D_MODEL)
    w_br_f = [all_rest[:, 256 + 128 * i:384 + 128 * i].reshape(N_CHIPS, 512, 256).transpose(1, 0, 2)
              .reshape(512, D_MODEL) for i in range(3)]
    w_out_f = all_rest[:, 640:896].reshape(D_MODEL, D_MODEL)
    w_main = jnp.concatenate([w_in_f[:, :FB_ORIG], w_in_f[:, FB_ORIG + B_HEADS:]], axis=1)
    w_fb = jnp.concatenate([w_in_f[:, FB_ORIG:FB_ORIG + B_HEADS], jnp.zeros((D_MODEL, HD - B_HEADS), BF16)], axis=1)

    pair = []

    def exchange(g):
        def per_chip(name, p):
            a = g[name]
            if name in ("w_mem_kv", "w_out"):
                return a[256 * p:256 * (p + 1)]
            return a[:, 256 * p:256 * (p + 1)]

        in4 = jnp.stack([g["w_in"][:, SHARD_COLS * p:SHARD_COLS * (p + 1)] for p in range(N_CHIPS)])
        rest4 = jnp.stack([_rest_pack({n: per_chip(n, p) for n, _, _ in REST_SPLITS}) for p in range(N_CHIPS)])
        halves = [in4.reshape(N_CHIPS, 2, D_MODEL // 2, SHARD_COLS),
                  rest4.reshape(N_CHIPS, 2, REST_ROWS // 2, D_MODEL)]
        got = _swap_with_sibling(halves)
        pair.extend(_add_pair("add_pair_%d" % i, halves[i], got[i], c) for i in range(2))
        return _scatter_to_owners(pair)

    loss_lanes, grad_x, g, landed = _local_step(
        x[0], mem[0], positions.reshape(s, 1), loss_target[0], norm_pre_g, norm_post_g, norm_mem_g,
        w_main, w_fb, b_forget, b_merge, w_kv_f, w_br_f[0], w_br_f[1], w_br_f[2], w_out_f, exchange)
    loss = lax.psum(jnp.sum(loss_lanes), ("x", "y", "c"))
    landed = [put(a, lax.dynamic_index_in_dim(o, chip, 0, keepdims=False), chip) for a, o in zip(landed, pair)]
    half = [_add_slabs("add_chips_%d" % i, [(landed[i], q) for q in range(N_CHIPS)], F32) for i in range(2)]
    red_in, red_rest = [put(a, o[0], c) for a, o in zip(_share_with_sibling(half), half)]
    gs = {"w_in": red_in.reshape(D_MODEL, SHARD_COLS)}
    gs.update(_rest_unpack(red_rest.reshape(REST_ROWS, D_MODEL), {n: w[n].shape for n, _, _ in REST_SPLITS}))
    gs.update(_small_unpack(_sum_small(_small_pack(
        g["norm_pre_g"], g["norm_post_g"], g["norm_mem_g"], g["b_forget"], g["b_merge"]))))

    delta, new_m, new_v = {}, {}, {}
    for n, tm in (("w_in", 128), ("w_mem_kv", 256), ("w_branch_a", 512), ("w_branch_b", 512),
                  ("w_branch_m", 512), ("w_out", 256)):
        d_, m_, v_ = _adamw("adamw_" + n, w[n], gs[n], mo[n], vo[n], tm)
        delta[n], new_m[n], new_v[n] = d_[None], m_[None], v_[None]
        gs[n] = gs[n][None]
    packs = [_small_pack(*[t[n] for n in ("norm_pre_g", "norm_post_g", "norm_mem_g", "b_forget", "b_merge")])
             for t in (w, gs, mo, vo)]
    for res, store in zip(_adamw("adamw_small", *packs, 8), (delta, new_m, new_v)):
        store.update(_small_unpack(res))

    return (loss, grad_x[None], *[gs[n] for n in WEIGHTS], *[delta[n] for n in WEIGHTS],
            *[new_m[n] for n in WEIGHTS], *[new_v[n] for n in WEIGHTS])
```

```python
import functools

import jax
import jax.numpy as jnp
from jax import lax
from jax.experimental import pallas as pl
from jax.experimental.pallas import tpu as pltpu

F32 = jnp.float32
BF16 = jnp.bfloat16
MESH = pl.DeviceIdType.MESH

D_MODEL = 1024
N_MEM = 256
EPS = 1e-6
NEG = -1e30
ROPE_THETA = 500000.0
ROT_DIM = 32
HD = 128
A_GROUP = 512
DILATIONS = (1, 4, 16)
BAND = 128
B_HEADS = 8
B_HD = 64
N_CHIPS = 4
N_DEV = 8

C_QA, C_KA, C_VA, C_ZA = 0, 1536, 3072, 4608
C_QB, C_KB, C_VB, C_ZB = 5120, 5632, 6144, 6656
C_QM, C_ZM, C_GL = 7168, 7680, 8192
N_MAIN = 11264
FB_ORIG = 6656
IN_COLS = 11272
SHARD_COLS = IN_COLS // N_CHIPS

ADAM_LR, ADAM_B1, ADAM_B2, ADAM_EPS, ADAM_WD, ADAM_STEP = 0.001, 0.9, 0.999, 1e-08, 0.01, 10

VMEM_LIMIT_V7X = 56 * 1024 * 1024

NT = (((1,), (1,)), ((), ()))
NN = (((1,), (0,)), ((), ()))
TN = (((0,), (0,)), ((), ()))


def _params(sem):
    return pltpu.CompilerParams(dimension_semantics=sem, vmem_limit_bytes=VMEM_LIMIT_V7X)


def _dot(a, b, dn=NN):
    return lax.dot_general(a, b, dn, preferred_element_type=F32)


def _sig(z):
    return 1.0 / (1.0 + jnp.exp(-z))


def _rows(name, fn, row_ins, bc_ins, outs, reds=(), tm=512, scratch=()):
    arrs, specs = [], []
    s = None
    for r in row_ins:
        arr, w, cb, d = (tuple(r) + (1,))[:4] if isinstance(r, tuple) else (r, r.shape[1], 0, 1)
        s = arr.shape[0] * d if s is None else s
        arrs.append(arr)
        specs.append((w, cb, d))
    tm = min(tm, s)
    specs = [pl.BlockSpec((tm // d, w), functools.partial(lambda i, cb: (i, cb), cb=cb)) for w, cb, d in specs]
    for b in bc_ins:
        arrs.append(b)
        specs.append(pl.BlockSpec(b.shape, lambda i: (0, 0)))
    outs = [(tuple(o) + (1,))[:3] for o in outs]
    n_in, n_out = len(arrs), len(outs)

    def body(*refs):
        n_ref = n_in + n_out + len(reds)
        vals = fn(*[r[...] for r in refs[:n_in]], *refs[n_ref:])
        if not isinstance(vals, (tuple, list)):
            vals = (vals,)
        for r, v in zip(refs[n_in:n_in + n_out], vals[:n_out]):
            r[...] = v.astype(r.dtype)
        if reds:
            red_refs = refs[n_in + n_out:n_ref]

            @pl.when(pl.program_id(0) == 0)
            def _():
                for r in red_refs:
                    r[...] = jnp.zeros_like(r)

            for r, v in zip(red_refs, vals[n_out:]):
                r[...] += v

    out_shape = [jax.ShapeDtypeStruct((s // d, c), dt) for c, dt, d in outs]
    out_shape += [jax.ShapeDtypeStruct((1, c), F32) for c in reds]
    out_specs = [pl.BlockSpec((tm // d, c), lambda i: (i, 0)) for c, _, d in outs]
    out_specs += [pl.BlockSpec((1, c), lambda i: (0, 0)) for c in reds]
    res = pl.pallas_call(
        body, name=name, grid=(s // tm,), in_specs=specs, out_specs=out_specs, out_shape=out_shape,
        scratch_shapes=list(scratch),
        compiler_params=_params(("arbitrary",) if reds else ("parallel",)),
    )(*arrs)
    return res


def _to_class(x, scr, d):
    if d == 1:
        return x.astype(F32)
    tm, c = x.shape
    for g in range(c // 128):
        scr[g][...] = x[:, g * 128:(g + 1) * 128].astype(F32)
    return jnp.concatenate([scr[g][pl.ds(r, tm // d, stride=d), :] for r in range(d) for g in range(c // 128)],
                           axis=1)


def _from_class(x, scr, d):
    if d == 1:
        return x.astype(F32)
    n, dc = x.shape
    c = dc // d
    for r in range(d):
        for g in range(c // 128):
            scr[g][pl.ds(r, n, stride=d), :] = x[:, r * c + g * 128:r * c + (g + 1) * 128].astype(F32)
    return jnp.concatenate([scr[g][...] for g in range(c // 128)], axis=1)


def _mm(name, a, b, mode, out_dtype, tm=1024, tn=1024, tk=1024, side=None):
    if mode == "nn":
        (m, k), (_, n) = a.shape, b.shape
    elif mode == "nt":
        (m, k), (n, _) = a.shape, b.shape
    else:
        (k, m), (_, n) = a.shape, b.shape
    tm, tn, tk = min(tm, m), min(tn, n), min(tk, k)
    nk = k // tk
    grid = (m // tm, n // tn, nk)
    dn = {"nn": NN, "nt": NT, "tn": TN}[mode]
    n_si = len(side["ins"]) if side else 0
    n_so = len(side["outs"]) if side else 0
    n_acc = 1 if nk > 1 else 0

    def body(*refs):
        a_ref, b_ref = refs[:2]
        side_in, o_ref = refs[2:2 + n_si], refs[2 + n_si]
        side_out = refs[3 + n_si:3 + n_si + n_so]
        acc = refs[3 + n_si + n_so:3 + n_si + n_so + n_acc]
        side_scratch = refs[3 + n_si + n_so + n_acc:]
        step = (pl.program_id(0) * grid[1] + pl.program_id(1)) * grid[2] + pl.program_id(2)
        if side:
            @pl.when(step == 0)
            def _():
                side["start"](side_in, side_out, side_scratch)

        part = _dot(a_ref[...].astype(BF16), b_ref[...].astype(BF16), dn)
        if nk == 1:
            o_ref[...] = part.astype(o_ref.dtype)
        else:
            kk = pl.program_id(2)

            @pl.when(kk == 0)
            def _():
                acc[0][...] = part

            @pl.when(kk > 0)
            def _():
                acc[0][...] += part

            @pl.when(kk == nk - 1)
            def _():
                o_ref[...] = acc[0][...].astype(o_ref.dtype)

        if side:
            @pl.when(step == grid[0] * grid[1] * grid[2] - 1)
            def _():
                side["wait"](side_in, side_out, side_scratch)

    a_spec = (pl.BlockSpec((tk, tm), lambda i, j, kk: (kk, i)) if mode == "tn"
              else pl.BlockSpec((tm, tk), lambda i, j, kk: (i, kk)))
    b_spec = (pl.BlockSpec((tn, tk), lambda i, j, kk: (j, kk)) if mode == "nt"
              else pl.BlockSpec((tk, tn), lambda i, j, kk: (kk, j)))
    o_spec = pl.BlockSpec((tm, tn), lambda i, j, kk: (i, j))
    o_shape = jax.ShapeDtypeStruct((m, n), out_dtype)
    acc_scratch = [pltpu.VMEM((tm, tn), F32)] * n_acc
    if not side:
        return pl.pallas_call(
            body, name=name, grid=grid, in_specs=[a_spec, b_spec], out_specs=o_spec, out_shape=o_shape,
            scratch_shapes=acc_scratch, compiler_params=_params(("parallel", "parallel", "arbitrary")),
        )(a, b)
    return pl.pallas_call(
        body, name=name, grid=grid, in_specs=[a_spec, b_spec] + [HBM_SPEC] * n_si,
        out_specs=[o_spec] + [HBM_SPEC] * n_so, out_shape=[o_shape] + side["outs"],
        scratch_shapes=acc_scratch + side["scratch"],
        compiler_params=_params(("arbitrary", "arbitrary", "arbitrary")),
    )(a, b, *side["ins"])


def _rms_fwd(name, x, g):
    def fn(xv, gv):
        r = lax.rsqrt(jnp.mean(xv * xv, axis=-1, keepdims=True) + EPS)
        return (xv * r * gv,)
    return _rows(name, fn, [x], [g], [(x.shape[1], BF16)], tm=min(512, x.shape[0]))[0]


def _rope_tables(pos, inv):
    ang = pos.astype(F32) * inv
    lane = lax.broadcasted_iota(jnp.int32, ang.shape, 1)
    c = jnp.where(lane < ROT_DIM, jnp.cos(ang), 1.0)
    sn = jnp.sin(ang)
    sg = jnp.where(lane < ROT_DIM // 2, -sn, jnp.where(lane < ROT_DIM, sn, 0.0))
    return c, sg, lane


def _rope_apply(x, c, sg, lane):
    outs = []
    for h in range(x.shape[1] // HD):
        xh = x[:, h * HD:(h + 1) * HD].astype(F32)
        swap = jnp.where(lane < ROT_DIM // 2, pltpu.roll(xh, HD - ROT_DIM // 2, 1),
                         pltpu.roll(xh, ROT_DIM // 2, 1))
        outs.append(xh * c + swap * sg)
    return jnp.concatenate(outs, axis=1)


ROPE_TM = 256


def _class_scratch(tm):
    return [pltpu.VMEM((tm, 128), F32) for _ in range(A_GROUP // 128)]


def _rope_fwd(u, pos, inv):
    def fn(q, k, v, p, iv, *scr):
        c, sg, lane = _rope_tables(p, iv)
        qr, kr = _rope_apply(q, c, sg, lane), _rope_apply(k, c, sg, lane)
        outs = []
        for g, d in enumerate(DILATIONS):
            gs = slice(g * A_GROUP, (g + 1) * A_GROUP)
            outs += [_to_class(qr[:, gs], scr, d), _to_class(kr[:, gs], scr, d), _to_class(v[:, gs], scr, d)]
        return tuple(outs)

    outs = [(d * A_GROUP, BF16, d) for d in DILATIONS for _ in range(3)]
    return _rows("rope_fwd", fn, [(u, 1536, 0), (u, 1536, 1), (u, 1536, 2), pos], [inv], outs, tm=ROPE_TM,
                 scratch=_class_scratch(ROPE_TM))


def _rope_bwd(dqs, dks, dvs, pos, inv):
    def fn(*args):
        grads, p, iv, scr = args[:9], args[9], args[10], args[11:]
        c, sg, lane = _rope_tables(p, iv)
        tok = [jnp.concatenate([_from_class(grads[3 * k + g], scr, d) for g, d in enumerate(DILATIONS)], axis=1)
               for k in range(3)]
        return _rope_apply(tok[0], c, -sg, lane), _rope_apply(tok[1], c, -sg, lane), tok[2]

    ins = [(a, a.shape[1], 0, d) for grp in (dqs, dks, dvs) for a, d in zip(grp, DILATIONS)]
    return _rows("rope_bwd", fn, ins + [pos], [inv], [(1536, BF16)] * 3, tm=ROPE_TM,
                 scratch=_class_scratch(ROPE_TM))


def _lane_pack(cols, like):
    lane = lax.broadcasted_iota(jnp.int32, like, 1)
    out = jnp.zeros(like, F32)
    for h, cvec in enumerate(cols):
        out = jnp.where(lane == h, cvec, out)
    return out


def _band_specs(l, d, tq):
    nsb = tq // BAND
    nblk = l // BAND
    cur = pl.BlockSpec((tq, A_GROUP), lambda r, i: (i, r))
    prev = pl.BlockSpec((BAND, A_GROUP), lambda r, i: (jnp.maximum(i * nsb - 1, 0), r))
    nxt = pl.BlockSpec((BAND, A_GROUP), lambda r, i: (jnp.minimum((i + 1) * nsb, nblk - 1), r))
    st_cur = pl.BlockSpec((tq, HD), lambda r, i: (i, r))
    st_nxt = pl.BlockSpec((BAND, HD), lambda r, i: (jnp.minimum((i + 1) * nsb, nblk - 1), r))
    return nsb, cur, prev, nxt, st_cur, st_nxt


def _band_mask_q(i, first_tile):
    qr = lax.broadcasted_iota(jnp.int32, (BAND, 2 * BAND), 0)
    kc = lax.broadcasted_iota(jnp.int32, (BAND, 2 * BAND), 1)
    in_prev = (kc < BAND) & (kc >= qr)
    in_cur = (kc >= BAND) & (kc - BAND <= qr)
    if i == 0:
        in_prev = in_prev & jnp.logical_not(first_tile)
    return in_prev | in_cur


def _band_mask_k(j, nsb, last_tile):
    kc = lax.broadcasted_iota(jnp.int32, (BAND, 2 * BAND), 0)
    qr = lax.broadcasted_iota(jnp.int32, (BAND, 2 * BAND), 1)
    same = (qr < BAND) & (kc <= qr)
    nxt = (qr >= BAND) & (kc >= qr - BAND)
    if j == nsb - 1:
        nxt = nxt & jnp.logical_not(last_tile)
    return same | nxt


def _band_fwd(name, q, k, v, d):
    l = q.shape[0]
    tq = min(512, l)
    nsb, cur, prev, _, st_cur, _ = _band_specs(l, d, tq)
    scale = HD ** -0.5

    def body(q_ref, kc_ref, kp_ref, vc_ref, vp_ref, o_ref, lse_ref):
        first = pl.program_id(1) == 0
        for i in range(nsb):
            lses = []
            mask = _band_mask_q(i, first)
            for h in range(4):
                cs = slice(h * HD, (h + 1) * HD)
                qv = q_ref[i * BAND:(i + 1) * BAND, cs]
                if i == 0:
                    kk = jnp.concatenate([kp_ref[:, cs], kc_ref[0:BAND, cs]], axis=0)
                    vv = jnp.concatenate([vp_ref[:, cs], vc_ref[0:BAND, cs]], axis=0)
                else:
                    kk = kc_ref[(i - 1) * BAND:(i + 1) * BAND, cs]
                    vv = vc_ref[(i - 1) * BAND:(i + 1) * BAND, cs]
                s = jnp.where(mask, _dot(qv, kk, NT) * scale, NEG)
                m = jnp.max(s, axis=-1, keepdims=True)
                p = jnp.exp(s - m)
                den = jnp.sum(p, axis=-1, keepdims=True)
                o_ref[i * BAND:(i + 1) * BAND, cs] = _dot(p.astype(BF16), vv) / den
                lses.append(m + jnp.log(den))
            lse_ref[i * BAND:(i + 1) * BAND, :] = _lane_pack(lses, (BAND, HD))

    return pl.pallas_call(
        body, name=name, grid=(d, l // tq), in_specs=[cur, cur, prev, cur, prev],
        out_specs=[cur, st_cur],
        out_shape=[jax.ShapeDtypeStruct((l, d * A_GROUP), F32), jax.ShapeDtypeStruct((l, d * HD), F32)],
        compiler_params=_params(("parallel", "parallel")),
    )(q, k, k, v, v)


def _band_dq(name, q, k, v, dy, lse, delta, d):
    l = q.shape[0]
    tq = min(512, l)
    nsb, cur, prev, _, st_cur, _ = _band_specs(l, d, tq)
    scale = HD ** -0.5

    def body(q_ref, kc_ref, kp_ref, vc_ref, vp_ref, dy_ref, lse_ref, dl_ref, dq_ref):
        first = pl.program_id(1) == 0
        for i in range(nsb):
            mask = _band_mask_q(i, first)
            rs = slice(i * BAND, (i + 1) * BAND)
            for h in range(4):
                cs = slice(h * HD, (h + 1) * HD)
                if i == 0:
                    kk = jnp.concatenate([kp_ref[:, cs], kc_ref[0:BAND, cs]], axis=0)
                    vv = jnp.concatenate([vp_ref[:, cs], vc_ref[0:BAND, cs]], axis=0)
                else:
                    kk = kc_ref[(i - 1) * BAND:(i + 1) * BAND, cs]
                    vv = vc_ref[(i - 1) * BAND:(i + 1) * BAND, cs]
                s = jnp.where(mask, _dot(q_ref[rs, cs], kk, NT) * scale, NEG)
                p = jnp.exp(s - lse_ref[rs, h:h + 1])
                dp = _dot(dy_ref[rs, cs], vv, NT)
                ds = p * (dp - dl_ref[rs, h:h + 1])
                dq_ref[rs, cs] = (_dot(ds.astype(BF16), kk) * scale).astype(dq_ref.dtype)

    return pl.pallas_call(
        body, name=name, grid=(d, l // tq),
        in_specs=[cur, cur, prev, cur, prev, cur, st_cur, st_cur], out_specs=cur,
        out_shape=jax.ShapeDtypeStruct((l, d * A_GROUP), BF16),
        compiler_params=_params(("parallel", "parallel")),
    )(q, k, k, v, v, dy, lse, delta)


def _band_dkv(name, q, k, v, dy, lse, delta, d):
    l = q.shape[0]
    tq = min(512, l)
    nsb, cur, _, nxt, st_cur, st_nxt = _band_specs(l, d, tq)
    scale = HD ** -0.5
    ntile = l // tq

    def body(k_ref, v_ref, qc_ref, qn_ref, dyc_ref, dyn_ref, lc_ref, ln_ref, dc_ref, dn_ref,
             dk_ref, dv_ref):
        last = pl.program_id(1) == ntile - 1

        def win(c_ref, n_ref, j, cs):
            if j == nsb - 1:
                return jnp.concatenate([c_ref[j * BAND:(j + 1) * BAND, cs], n_ref[:, cs]], axis=0)
            return c_ref[j * BAND:(j + 2) * BAND, cs]

        allh = slice(0, HD)
        for j in range(nsb):
            mask = _band_mask_k(j, nsb, last)
            rs = slice(j * BAND, (j + 1) * BAND)
            lse_t = win(lc_ref, ln_ref, j, allh).T
            delta_t = win(dc_ref, dn_ref, j, allh).T
            for h in range(4):
                cs = slice(h * HD, (h + 1) * HD)
                qw = win(qc_ref, qn_ref, j, cs)
                dyw = win(dyc_ref, dyn_ref, j, cs)
                st = jnp.where(mask, _dot(k_ref[rs, cs], qw, NT) * scale, NEG)
                pt = jnp.exp(st - lse_t[h:h + 1, :])
                dst = pt * (_dot(v_ref[rs, cs], dyw, NT) - delta_t[h:h + 1, :])
                dv_ref[rs, cs] = _dot(pt.astype(BF16), dyw).astype(dv_ref.dtype)
                dk_ref[rs, cs] = (_dot(dst.astype(BF16), qw) * scale).astype(dk_ref.dtype)

    shp = jax.ShapeDtypeStruct((l, d * A_GROUP), BF16)
    return pl.pallas_call(
        body, name=name, grid=(d, ntile),
        in_specs=[cur, cur, cur, nxt, cur, nxt, st_cur, st_nxt, st_cur, st_nxt],
        out_specs=[cur, cur], out_shape=[shp, shp],
        compiler_params=_params(("parallel", "parallel")),
    )(k, v, q, q, dy, dy, lse, lse, delta, delta)


def _split3(x):
    hi = x.astype(BF16)
    r1 = x - hi.astype(F32)
    mid = r1.astype(BF16)
    lo = (r1 - mid.astype(F32)).astype(BF16)
    return hi, mid, lo


def _fox_prep(z, b):
    h, s = z.shape
    blk = min(512, s)

    def body(z_ref, b_ref, c_ref):
        r = lax.broadcasted_iota(jnp.int32, (blk, blk), 0)
        cidx = lax.broadcasted_iota(jnp.int32, (blk, blk), 1)
        tri = (r <= cidx).astype(BF16)
        carry = jnp.zeros((h, 1), F32)
        for t in range(s // blk):
            zz = z_ref[:, t * blk:(t + 1) * blk] + b_ref[...]
            lf = jnp.minimum(zz, 0.0) - jnp.log(1.0 + jnp.exp(-jnp.abs(zz)))
            hi, mid, lo = _split3(lf)
            cs = _dot(hi, tri) + _dot(mid, tri) + _dot(lo, tri) + carry
            c_ref[:, t * blk:(t + 1) * blk] = cs
            carry = cs[:, blk - 1:blk]

    return pl.pallas_call(body, name="fox_prep", out_shape=jax.ShapeDtypeStruct((h, s), F32))(z, b)


def _fox_prep_bwd(dc, z, b):
    h, s = z.shape
    blk = min(512, s)

    def body(dc_ref, z_ref, b_ref, dz_ref, db_ref):
        r = lax.broadcasted_iota(jnp.int32, (blk, blk), 0)
        cidx = lax.broadcasted_iota(jnp.int32, (blk, blk), 1)
        tri = (r >= cidx).astype(BF16)
        carry = jnp.zeros((h, 1), F32)
        tot = jnp.zeros((h, 1), F32)
        for t in reversed(range(s // blk)):
            hi, mid, lo = _split3(dc_ref[:, t * blk:(t + 1) * blk])
            rc = _dot(hi, tri) + _dot(mid, tri) + _dot(lo, tri) + carry
            carry = rc[:, 0:1]
            zz = z_ref[:, t * blk:(t + 1) * blk] + b_ref[...]
            dz = rc * _sig(-zz)
            dz_ref[:, t * blk:(t + 1) * blk] = dz
            tot = tot + jnp.sum(dz, axis=-1, keepdims=True)
        db_ref[...] = tot

    return pl.pallas_call(
        body, name="fox_prep_bwd",
        out_shape=[jax.ShapeDtypeStruct((h, s), F32), jax.ShapeDtypeStruct((h, 1), F32)])(dc, z, b)


FOX_W = 128
FOX_C = B_HD
FOX_ONE = B_HD + 3
FOX_SUB = 256
FOX_SUB_FWD = 128
FOX_HEADS_PER_STEP = 2


def _head_of_pair(x, hh):
    return x if hh == 0 else pltpu.roll(x, B_HD, 1)


def _fox_pack(u, c_col, t):
    s = u.shape[0]
    nt = s // t
    scale = B_HD ** -0.5

    def body(q_ref, k_ref, v_ref, c_ref, qf_ref, kb_ref, ks_ref, vb_ref, vt_ref):
        lane = lax.broadcasted_iota(jnp.int32, (t, FOX_W), 1)
        qv, kv, vv = [r[...].astype(F32) for r in (q_ref, k_ref, v_ref)]
        for hh in range(2):
            qf_ref[hh] = jnp.where(lane < B_HD, _head_of_pair(qv, hh), B_HD ** 0.5).astype(BF16)
            neg = c_ref[hh] * (-scale)
            hi = neg.astype(BF16).astype(F32)
            mid = (neg - hi).astype(BF16).astype(F32)
            lo = neg - hi - mid
            aux = jnp.where(lane == FOX_C, hi,
                            jnp.where(lane == FOX_C + 1, mid, jnp.where(lane == FOX_C + 2, lo, 0.0)))
            kb = jnp.where(lane < B_HD, _head_of_pair(kv, hh) * scale, aux)
            kb_ref[hh] = kb.astype(BF16)
            ks_ref[hh] = jnp.where(lane == FOX_ONE, 1.0, kb).T.astype(BF16)
            vb = jnp.where(lane < B_HD, _head_of_pair(vv, hh), 1.0)
            vb_ref[hh] = vb.astype(BF16)
            vt_ref[hh] = vb.T.astype(BF16)

    def tok(col0):
        return pl.BlockSpec((t, FOX_W), functools.partial(lambda hp, i, cb: (i, cb + hp), cb=col0 // FOX_W))

    rows = pl.BlockSpec((2, t, FOX_W), lambda hp, i: (hp, i, 0))
    tiles = pl.BlockSpec((2, None, FOX_W, t), lambda hp, i: (hp, i, 0, 0))
    hm = jax.ShapeDtypeStruct((B_HEADS, s, FOX_W), BF16)
    tt = jax.ShapeDtypeStruct((B_HEADS, nt, FOX_W, t), BF16)
    return pl.pallas_call(
        body, name="fox_pack", grid=(B_HEADS // 2, nt),
        in_specs=[tok(C_QB), tok(C_KB), tok(C_VB), pl.BlockSpec((2, t, 1), lambda hp, i: (hp, i, 0))],
        out_specs=[rows, rows, tiles, rows, tiles], out_shape=[hm, hm, tt, hm, tt],
        compiler_params=_params(("parallel", "parallel")),
    )(u, u, u, c_col)


def _fox_pack_bwd(dy, y, t):
    s = dy.shape[0]
    nt = s // t

    def body(do_ref, o_ref, dow_ref, dl_ref):
        lane = lax.broadcasted_iota(jnp.int32, (t, FOX_W), 1)
        lane8 = lax.broadcasted_iota(jnp.int32, (8, FOX_W), 1)
        dov = do_ref[...].astype(F32)
        parts = _split3(dov * o_ref[...].astype(F32))
        for hh in range(2):
            dow_ref[hh] = jnp.where(lane < B_HD, _head_of_pair(dov, hh), 0.0).astype(BF16)
            mask = ((lane8 >= hh * B_HD) & (lane8 < (hh + 1) * B_HD)).astype(BF16)
            row = _dot(mask, parts[0], NT) + _dot(mask, parts[1], NT) + _dot(mask, parts[2], NT)
            dl_ref[hh] = row[0:1, :]

    tok = pl.BlockSpec((t, FOX_W), lambda hp, i: (i, hp))
    return pl.pallas_call(
        body, name="fox_pack_bwd", grid=(B_HEADS // 2, nt), in_specs=[tok, tok],
        out_specs=[pl.BlockSpec((2, t, FOX_W), lambda hp, i: (hp, i, 0)),
                   pl.BlockSpec((2, None, 1, t), lambda hp, i: (hp, i, 0, 0))],
        out_shape=[jax.ShapeDtypeStruct((B_HEADS, s, FOX_W), BF16), jax.ShapeDtypeStruct((B_HEADS, nt, 1, t), F32)],
        compiler_params=_params(("parallel", "parallel")),
    )(dy, y)


def _fox_unpack(dqt, dkw, dvw, t):
    h, nt = dqt.shape[:2]
    s = nt * t

    def body(dq_ref, dk_ref, dv_ref, dqo_ref, dko_ref, dvo_ref, dc_ref):
        lane = lax.broadcasted_iota(jnp.int32, (t, FOX_W), 1)

        def join(a0, a1):
            return jnp.where(lane < B_HD, a0, pltpu.roll(a1, B_HD, 1))

        for hh in range(2):
            dc_ref[hh] = dq_ref[hh][FOX_ONE:FOX_ONE + 1, :] - dk_ref[hh].T[B_HD:B_HD + 1, :]
        dqo_ref[...] = join(dq_ref[0].T, dq_ref[1].T).astype(BF16)
        dko_ref[...] = join(dk_ref[0], dk_ref[1]).astype(BF16)
        dvo_ref[...] = join(dv_ref[0], dv_ref[1]).astype(BF16)

    tok = pl.BlockSpec((t, FOX_W), lambda hp, i: (i, hp))
    rows = pl.BlockSpec((2, t, FOX_W), lambda hp, i: (hp, i, 0))
    shp = jax.ShapeDtypeStruct((s, h * B_HD), BF16)
    return pl.pallas_call(
        body, name="fox_unpack", grid=(h // 2, nt),
        in_specs=[pl.BlockSpec((2, None, FOX_W, t), lambda hp, i: (hp, i, 0, 0)), rows, rows],
        out_specs=[tok, tok, tok, pl.BlockSpec((2, None, 1, t), lambda hp, i: (hp, i, 0, 0))],
        out_shape=[shp, shp, shp, jax.ShapeDtypeStruct((h, nt, 1, t), F32)],
        compiler_params=_params(("parallel", "parallel")),
    )(dqt, dkw, dvw)


FOX_DEAD = -110.0


def _fox_norm2(qf, kb):
    h, s, w = qf.shape
    tm = min(2048, s)

    def body(q_ref, k_ref, qo_ref, ko_ref):
        row = lax.broadcasted_iota(jnp.int32, (w, w), 0)
        ones = (row < B_HD).astype(BF16)
        for x_ref, o_ref in ((q_ref, qo_ref), (k_ref, ko_ref)):
            xv = x_ref[...].astype(F32)
            n2 = _dot((xv * xv).astype(BF16), ones)
            o_ref[...] = jnp.broadcast_to(jnp.max(n2, axis=0, keepdims=True)[:, :1], o_ref.shape)

    spec = pl.BlockSpec((None, tm, w), lambda hh, i: (hh, i, 0))
    ospec = pl.BlockSpec((None, None, 8, 128), lambda hh, i: (hh, i, 0, 0))
    shp = jax.ShapeDtypeStruct((h, s // tm, 8, 128), F32)
    return pl.pallas_call(
        body, name="fox_norm2", grid=(h, s // tm), in_specs=[spec, spec], out_specs=[ospec, ospec],
        out_shape=[shp, shp], compiler_params=_params(("parallel", "parallel")),
    )(qf, kb)


def _fox_bounds(qf, kb, c, t):
    q2, k2 = _fox_norm2(qf, kb)
    g = 2.0 * jnp.sqrt(1.02 * jnp.max(q2[:, :, 0, 0], axis=1) * 1.02 * jnp.max(k2[:, :, 0, 0], axis=1))
    return jnp.concatenate([c[:, ::t], c[:, t - 1::t], g[:, None]], axis=1)


SMEM_SPEC = pl.BlockSpec(memory_space=pltpu.SMEM)


def _fox_fwd(qf, kb, vt4, bounds, t):
    h, s, w = qf.shape
    nt = s // t
    sub = FOX_SUB_FWD
    nsub = t // sub
    nh = FOX_HEADS_PER_STEP

    def body(b_ref, q_ref, k_ref, v_ref, o_ref, lse_ref):
        i = pl.program_id(1)
        krow = lax.broadcasted_iota(jnp.int32, (sub, t), 0)
        qcol = lax.broadcasted_iota(jnp.int32, (sub, t), 1)

        def dead_before(hh):
            head = pl.program_id(0) * nh + hh
            top = b_ref[head, 2 * nt] + b_ref[head, i]
            return lax.fori_loop(
                0, i, lambda jj, n: n + (top - b_ref[head, nt + jj] < FOX_DEAD).astype(jnp.int32), 0)

        j_lo = functools.reduce(jnp.minimum, [dead_before(hh) for hh in range(nh)])

        def tile(j, carry, diag):
            out = []
            for hh in range(nh):
                m, acc = carry[hh]
                qv, vj = q_ref[hh], v_ref[hh, j]
                los = [b * sub if diag else 0 for b in range(nsub)]
                sts = [_dot(k_ref[hh, pl.ds(pl.multiple_of(j * t + b * sub, sub), sub), :], qv[lo:, :], NT)
                       for b, lo in enumerate(los)]
                for b, lo in enumerate(los):
                    st = sts[b]
                    if diag:
                        st = jnp.where(krow[:, :t - lo] <= qcol[:, :t - lo], st, NEG)
                    m_old, acc_old = m[:, lo:], acc[:, lo:]
                    m2 = jnp.maximum(m_old, jnp.max(st, axis=0, keepdims=True))
                    p = jnp.exp(st - m2).astype(BF16)
                    acc2 = jnp.exp(m_old - m2) * acc_old + _dot(vj[:, b * sub:(b + 1) * sub], p)
                    m = m2 if lo == 0 else jnp.concatenate([m[:, :lo], m2], axis=1)
                    acc = acc2 if lo == 0 else jnp.concatenate([acc[:, :lo], acc2], axis=1)
                out.append((m, acc))
            return tuple(out)

        init = tuple((jnp.full((1, t), NEG, F32), jnp.zeros((w, t), F32)) for _ in range(nh))
        carry = lax.fori_loop(j_lo, i, lambda j, c: tile(j, c, False), init)
        outs = []
        for hh, (m, acc) in enumerate(tile(i, carry, True)):
            den = acc[B_HD:B_HD + 1, :]
            outs.append(acc[0:B_HD, :] / den)
            lse_ref[hh] = m + jnp.log(den)
        o_ref[...] = jnp.concatenate(outs, axis=0).T.astype(o_ref.dtype)

    return pl.pallas_call(
        body, name="fox_fwd", grid=(h // nh, nt),
        in_specs=[SMEM_SPEC,
                  pl.BlockSpec((nh, t, w), lambda hh, i: (hh, i, 0)),
                  pl.BlockSpec((nh, s, w), lambda hh, i: (hh, 0, 0)),
                  pl.BlockSpec((nh, nt, w, t), lambda hh, i: (hh, 0, 0, 0))],
        out_specs=[pl.BlockSpec((t, nh * B_HD), lambda hh, i: (i, hh)),
                   pl.BlockSpec((nh, 1, t), lambda hh, i: (hh, 0, i))],
        out_shape=[jax.ShapeDtypeStruct((s, h * B_HD), BF16), jax.ShapeDtypeStruct((h, 1, s), F32)],
        compiler_params=_params(("parallel", "parallel")),
    )(bounds, qf, kb, vt4)


def _fox_bwd(qf, dow, lse_row, delta_row, kb, kst4, vb, bounds, t):
    h, s, w = qf.shape
    nt = s // t
    nsub = t // FOX_SUB
    nh = FOX_HEADS_PER_STEP

    def body(b_ref, q_ref, do_ref, lse_ref, dl_ref, k_ref, kt_ref, v_ref, dqt_ref, dk_ref, dv_ref, dk_acc, dv_acc):
        j = pl.program_id(1)

        def alive_after(hh):
            head = pl.program_id(0) * nh + hh
            top = b_ref[head, 2 * nt] - b_ref[head, nt + j]
            return lax.fori_loop(
                j + 1, nt, lambda ii, n: n + (top + b_ref[head, ii] >= FOX_DEAD).astype(jnp.int32), 0)

        i_hi = j + 1 + functools.reduce(jnp.maximum, [alive_after(hh) for hh in range(nh)])

        @pl.when(j == 0)
        def _():
            dqt_ref[...] = jnp.zeros_like(dqt_ref)

        dk_acc[...] = jnp.zeros_like(dk_acc)
        dv_acc[...] = jnp.zeros_like(dv_acc)
        krow = lax.broadcasted_iota(jnp.int32, (FOX_SUB, t), 0)
        qcol = lax.broadcasted_iota(jnp.int32, (FOX_SUB, t), 1)
        subs = [slice(b * FOX_SUB, (b + 1) * FOX_SUB) for b in range(nsub)]

        def tile(i, diag):
            i0 = pl.multiple_of(i * t, t)
            for hh in range(nh):
                qi, doi = q_ref[hh, pl.ds(i0, t), :], do_ref[hh, pl.ds(i0, t), :]
                lse, dl = lse_ref[hh, i], dl_ref[hh, i]
                los = [b * FOX_SUB if diag else 0 for b in range(nsub)]
                sts = [_dot(k_ref[hh, rs, :], qi[lo:, :], NT) for rs, lo in zip(subs, los)]
                dps = [_dot(v_ref[hh, rs, :], doi[lo:, :], NT) for rs, lo in zip(subs, los)]
                dq = None
                for b, (rs, lo) in enumerate(zip(subs, los)):
                    st = sts[b] - lse[:, lo:]
                    if diag:
                        st = jnp.where(krow[:, :t - lo] <= qcol[:, :t - lo], st, NEG)
                    pt = jnp.exp(st)
                    dsb = (pt * (dps[b] - dl[:, lo:])).astype(BF16)
                    dv_acc[hh, rs, :] += _dot(pt.astype(BF16), doi[lo:, :])
                    dk_acc[hh, rs, :] += _dot(dsb, qi[lo:, :])
                    part = _dot(kt_ref[hh, :, rs], dsb)
                    if lo:
                        part = jnp.concatenate([jnp.zeros((w, lo), F32), part], axis=1)
                    dq = part if dq is None else dq + part
                dqt_ref[hh, i] += dq

        def step(i, carry):
            tile(i, False)
            return carry

        tile(j, True)
        lax.fori_loop(j + 1, i_hi, step, 0)
        dk_ref[...] = dk_acc[...] * (B_HD ** -0.5)
        dv_ref[...] = dv_acc[...]

    full = pl.BlockSpec((nh, s, w), lambda hh, j: (hh, 0, 0))
    rowst = pl.BlockSpec((nh, nt, 1, t), lambda hh, j: (hh, 0, 0, 0))
    tl = pl.BlockSpec((nh, t, w), lambda hh, j: (hh, j, 0))
    return pl.pallas_call(
        body, name="fox_bwd", grid=(h // nh, nt),
        in_specs=[SMEM_SPEC, full, full, rowst, rowst, tl,
                  pl.BlockSpec((nh, None, w, t), lambda hh, j: (hh, j, 0, 0)), tl],
        out_specs=[pl.BlockSpec((nh, nt, w, t), lambda hh, j: (hh, 0, 0, 0)), tl, tl],
        out_shape=[jax.ShapeDtypeStruct((h, nt, w, t), F32), jax.ShapeDtypeStruct((h, s, w), F32),
                   jax.ShapeDtypeStruct((h, s, w), F32)],
        scratch_shapes=[pltpu.VMEM((nh, t, w), F32), pltpu.VMEM((nh, t, w), F32)],
        compiler_params=_params(("parallel", "arbitrary")),
    )(bounds, qf, dow, lse_row, delta_row, kb, kst4, vb)


def _mem_fwd(u, mkv, tq=512):
    s = u.shape[0]
    scale = HD ** -0.5

    def body(q_ref, mk_ref, mv_ref, o_ref, lse_ref):
        lses = []
        for h in range(4):
            cs = slice(h * HD, (h + 1) * HD)
            sc = _dot(q_ref[:, cs], mk_ref[:, cs], NT) * scale
            m = jnp.max(sc, axis=-1, keepdims=True)
            p = jnp.exp(sc - m)
            den = jnp.sum(p, axis=-1, keepdims=True)
            o_ref[:, cs] = (_dot(p.astype(BF16), mv_ref[:, cs]) / den).astype(o_ref.dtype)
            lses.append(m + jnp.log(den))
        lse_ref[...] = _lane_pack(lses, (tq, HD))

    return pl.pallas_call(
        body, name="mem_fwd", grid=(s // tq,),
        in_specs=[pl.BlockSpec((tq, 512), lambda i: (i, C_QM // 512)),
                  pl.BlockSpec((N_MEM, 512), lambda i: (0, 0)),
                  pl.BlockSpec((N_MEM, 512), lambda i: (0, 1))],
        out_specs=[pl.BlockSpec((tq, 512), lambda i: (i, 0)), pl.BlockSpec((tq, HD), lambda i: (i, 0))],
        out_shape=[jax.ShapeDtypeStruct((s, 512), BF16), jax.ShapeDtypeStruct((s, HD), F32)],
        compiler_params=_params(("parallel",)),
    )(u, mkv, mkv)


def _mem_bwd(u, mkv, o, do, lse, tq=512):
    s = u.shape[0]
    scale = HD ** -0.5

    def body(q_ref, mk_ref, mv_ref, o_ref, do_ref, lse_ref, dq_ref, dmk_ref, dmv_ref):
        @pl.when(pl.program_id(0) == 0)
        def _():
            dmk_ref[...] = jnp.zeros_like(dmk_ref)
            dmv_ref[...] = jnp.zeros_like(dmv_ref)

        for h in range(4):
            cs = slice(h * HD, (h + 1) * HD)
            qv, dov = q_ref[:, cs], do_ref[:, cs]
            sc = _dot(qv, mk_ref[:, cs], NT) * scale
            p = jnp.exp(sc - lse_ref[:, h:h + 1])
            delta = jnp.sum(dov.astype(F32) * o_ref[:, cs].astype(F32), axis=-1, keepdims=True)
            ds = p * (_dot(dov, mv_ref[:, cs], NT) - delta)
            dsb = ds.astype(BF16)
            dq_ref[:, cs] = (_dot(dsb, mk_ref[:, cs]) * scale).astype(dq_ref.dtype)
            dmk_ref[:, cs] += _dot(dsb, qv, TN) * scale
            dmv_ref[:, cs] += _dot(p.astype(BF16), dov, TN)

    row = pl.BlockSpec((tq, 512), lambda i: (i, 0))
    acc = pl.BlockSpec((N_MEM, 512), lambda i: (0, 0))
    return pl.pallas_call(
        body, name="mem_bwd", grid=(s // tq,),
        in_specs=[pl.BlockSpec((tq, 512), lambda i: (i, C_QM // 512)),
                  pl.BlockSpec((N_MEM, 512), lambda i: (0, 0)),
                  pl.BlockSpec((N_MEM, 512), lambda i: (0, 1)),
                  row, row, pl.BlockSpec((tq, HD), lambda i: (i, 0))],
        out_specs=[row, acc, acc],
        out_shape=[jax.ShapeDtypeStruct((s, 512), BF16), jax.ShapeDtypeStruct((N_MEM, 512), F32),
                   jax.ShapeDtypeStruct((N_MEM, 512), F32)],
        compiler_params=_params(("arbitrary",)),
    )(u, mkv, mkv, o, do, lse)


def _local_step(x, mem, pos, target, g_pre, g_post, g_mem, w_main, w_fb, b_forget, b_merge,
                w_mem_kv, w_ba, w_bb, w_bm, w_out, exchange=None):
    s = x.shape[0]
    t_fox = min(512, s)
    nt = s // t_fox
    half = ROT_DIM // 2
    inv = ROPE_THETA ** (-jnp.arange(half, dtype=F32) / half)
    inv128 = jnp.concatenate([inv, inv, jnp.zeros((HD - ROT_DIM,), F32)]).reshape(1, HD)

    h = _rms_fwd("norm_pre", x, g_pre)
    u = _mm("proj_in", h, w_main, "nn", BF16, tm=2048)
    ufb = _mm("proj_fb", h, w_fb, "nn", F32)
    memn = _rms_fwd("norm_mem", mem, g_mem)
    mkv = _mm("proj_mem", memn, w_mem_kv, "nn", BF16)

    qkv = _rope_fwd(u, pos, inv128)
    views = [tuple(qkv[3 * g:3 * g + 3]) for g in range(3)]
    os_, lses = [], []
    for g, d in enumerate(DILATIONS):
        o_g, lse_g = _band_fwd("band_fwd%d" % g, *views[g], d)
        os_.append((o_g, d * A_GROUP, 0, d))
        lses.append((lse_g, d * HD, 0, d))

    def merge_a(o1, o2, o3, l1, l2, l3, za, *scr):
        o1, o2, o3 = [_from_class(o, scr, d) for o, d in zip((o1, o2, o3), DILATIONS)]
        l1, l2, l3 = [_from_class(lv, scr, d) for lv, d in zip((l1, l2, l3), DILATIONS)]
        ys, tots = [], []
        for hh in range(4):
            cs, hs = slice(hh * HD, (hh + 1) * HD), slice(hh, hh + 1)
            mx = jnp.maximum(jnp.maximum(l1[:, hs], l2[:, hs]), l3[:, hs])
            e1, e2, e3 = jnp.exp(l1[:, hs] - mx), jnp.exp(l2[:, hs] - mx), jnp.exp(l3[:, hs] - mx)
            den = e1 + e2 + e3
            ys.append((e1 * o1[:, cs] + e2 * o2[:, cs] + e3 * o3[:, cs]) / den)
            tots.append(mx + jnp.log(den))
        y = jnp.concatenate(ys, axis=1)
        zf = za.astype(F32)
        tot = _lane_pack(tots, l1.shape)
        return (y, y * (zf * _sig(zf))) + tuple(_to_class(tot, scr, d) for d in DILATIONS)

    res = _rows("merge_a", merge_a, os_ + lses + [(u, 512, C_ZA // 512)], [],
                [(512, BF16), (512, BF16)] + [(d * HD, F32, d) for d in DILATIONS], tm=ROPE_TM,
                scratch=_class_scratch(ROPE_TM))
    y_a, yg_a, lse_a = res[0], res[1], res[2:5]

    zrow = ufb[:, :B_HEADS].T
    c = _fox_prep(zrow, b_forget.reshape(B_HEADS, 1))
    qf, kb, kst4, vb, vt4 = _fox_pack(u, c.reshape(B_HEADS, s, 1), t_fox)
    bounds = _fox_bounds(qf, kb, c, t_fox)
    y_b, lse_b = _fox_fwd(qf, kb, vt4, bounds, t_fox)

    y_m, lse_m = _mem_fwd(u, mkv)

    def gate(y, z):
        zf = z.astype(F32)
        return (y.astype(F32) * (zf * _sig(zf)),)

    yg_b = _rows("gate_b", gate, [y_b, (u, 512, C_ZB // 512)], [], [(512, BF16)])[0]
    yg_m = _rows("gate_m", gate, [y_m, (u, 512, C_ZM // 512)], [], [(512, BF16)])[0]

    br_a = _mm("branch_a", yg_a, w_ba, "nn", BF16)
    br_b = _mm("branch_b", yg_b, w_bb, "nn", BF16)
    br_m = _mm("branch_m", yg_m, w_bm, "nn", BF16)
    gl = [(u, 1024, C_GL // 1024 + i) for i in range(3)]
    bm3 = b_merge.reshape(3, D_MODEL)

    def merge(g0, g1, g2, b0, b1, b2, bm):
        tot = 0.0
        for i, (gv, bv) in enumerate(((g0, b0), (g1, b1), (g2, b2))):
            tot = tot + _sig(gv.astype(F32) + bm[i:i + 1, :]) * bv.astype(F32)
        return (tot,)

    merged = _rows("merge_gates", merge, gl + [br_a, br_b, br_m], [bm3], [(D_MODEL, BF16)])[0]
    out = _mm("proj_out", merged, w_out, "nn", F32)

    def tail(xv, ov, tv, gv):
        r = lax.rsqrt(jnp.mean(ov * ov, axis=-1, keepdims=True) + EPS)
        n = ov * r
        err = xv + n * gv - tv
        dy = err * (1.0 / D_MODEL)
        dn = dy * gv
        dout = r * (dn - n * jnp.mean(dn * n, axis=-1, keepdims=True))
        return (dy, dout, jnp.sum(0.5 * err * err * (1.0 / D_MODEL), axis=0, keepdims=True),
                jnp.sum(dy * n, axis=0, keepdims=True))

    dy, dout, loss_lanes, g_post_grad = _rows(
        "tail", tail, [x, out, target], [g_post], [(D_MODEL, F32), (D_MODEL, BF16)],
        reds=[D_MODEL, D_MODEL], tm=256)

    dmerged = _mm("d_merged", dout, w_out, "nt", BF16)
    gw_out = _mm("g_w_out", merged, dout, "tn", F32)

    def merge_bwd(dm, g0, g1, g2, b0, b1, b2, bm):
        dmf = dm.astype(F32)
        dbs, dgs, sums = [], [], []
        for i, (gv, bv) in enumerate(((g0, b0), (g1, b1), (g2, b2))):
            sg = _sig(gv.astype(F32) + bm[i:i + 1, :])
            dbs.append(dmf * sg)
            dg = dmf * bv.astype(F32) * sg * (1.0 - sg)
            dgs.append(dg)
            sums.append(jnp.sum(dg, axis=0, keepdims=True))
        return tuple(dbs + dgs + sums)

    res = _rows("merge_bwd", merge_bwd, [dmerged] + gl + [br_a, br_b, br_m], [bm3],
                [(D_MODEL, BF16)] * 6, reds=[D_MODEL] * 3, tm=256)
    dbr, dgl, g_bmerge = res[0:3], res[3:6], jnp.concatenate(res[6:9], axis=1)

    dyg, gw_branch = [], []
    for nm, dbv, wv, ygv in (("a", dbr[0], w_ba, yg_a), ("b", dbr[1], w_bb, yg_b), ("m", dbr[2], w_bm, yg_m)):
        dyg.append(_mm("d_yg_" + nm, dbv, wv, "nt", BF16))
        gw_branch.append(_mm("g_w_branch_" + nm, ygv, dbv, "tn", F32))

    def gate_bwd(dg, y, z):
        dgf, yf, zf = dg.astype(F32), y.astype(F32), z.astype(F32)
        sg = _sig(zf)
        return dgf * (zf * sg), dgf * yf * (sg * (1.0 + zf * (1.0 - sg)))

    def gate_bwd_a(dg, y, z, *scr):
        dyv, dz = gate_bwd(dg, y, z)
        prod = dyv * y.astype(F32)
        dl = [jnp.sum(prod[:, hh * HD:(hh + 1) * HD], axis=-1, keepdims=True) for hh in range(4)]
        delta = _lane_pack(dl, (dg.shape[0], HD))
        return ((dz,) + tuple(_to_class(dyv, scr, d) for d in DILATIONS)
                + tuple(_to_class(delta, scr, d) for d in DILATIONS))

    res = _rows("gate_bwd_a", gate_bwd_a, [dyg[0], y_a, (u, 512, C_ZA // 512)], [],
                [(512, BF16)] + [(d * A_GROUP, BF16, d) for d in DILATIONS] + [(d * HD, F32, d) for d in DILATIONS],
                tm=ROPE_TM, scratch=_class_scratch(ROPE_TM))
    dz_a, dy_a, delta_a = res[0], res[1:4], res[4:7]
    dy_b, dz_b = _rows("gate_bwd_b", gate_bwd, [dyg[1], y_b, (u, 512, C_ZB // 512)], [],
                       [(512, BF16), (512, BF16)])
    dy_m, dz_m = _rows("gate_bwd_m", gate_bwd, [dyg[2], y_m, (u, 512, C_ZM // 512)], [],
                       [(512, BF16), (512, BF16)])

    dq_m, dmk, dmv = _mem_bwd(u, mkv, y_m, dy_m, lse_m)
    dmkv = jnp.concatenate([dmk, dmv], axis=1)
    gw_mem_kv = _mm("g_w_mem_kv", memn, dmkv, "tn", F32)
    dmemn = _mm("d_memn", dmkv, w_mem_kv, "nt", F32)

    def mem_gain_grad(mv, dv):
        r = lax.rsqrt(jnp.mean(mv * mv, axis=-1, keepdims=True) + EPS)
        return (jnp.sum(dv * mv * r, axis=0, keepdims=True),)

    g_mem_grad = _rows("g_norm_mem", mem_gain_grad, [mem, dmemn], [], [], reds=[D_MODEL], tm=N_MEM)[0]

    dow, delta_b = _fox_pack_bwd(dy_b, y_b, t_fox)
    dqt, dkw, dvw = _fox_bwd(qf, dow, lse_b.reshape(B_HEADS, nt, 1, t_fox), delta_b, kb, kst4, vb, bounds, t_fox)
    dqb, dkb, dvb, dc = _fox_unpack(dqt, dkw, dvw, t_fox)
    dzrow, g_bforget = _fox_prep_bwd(dc.reshape(B_HEADS, s), zrow, b_forget.reshape(B_HEADS, 1))
    dfb = jnp.zeros((s, HD), BF16).at[:, :B_HEADS].set(dzrow.T.astype(BF16))

    dqs, dks, dvs = [], [], []
    for g, d in enumerate(DILATIONS):
        qv, kv, vv = views[g]
        dqs.append(_band_dq("band_dq%d" % g, qv, kv, vv, dy_a[g], lse_a[g], delta_a[g], d))
        dk_g, dv_g = _band_dkv("band_dkv%d" % g, qv, kv, vv, dy_a[g], lse_a[g], delta_a[g], d)
        dks.append(dk_g)
        dvs.append(dv_g)
    dqa, dka, dva = _rope_bwd(dqs, dks, dvs, pos, inv128)

    du = jnp.concatenate(
        [dqa, dka, dva, dz_a, dqb, dkb, dvb,
                            dz_b, dq_m, dz_m] + list(dgl), axis=1)

    gw_main = _mm("g_w_main", h.T, du, "nn", F32, tk=2048)
    gw_fb = _mm("g_w_fb", h, dfb, "tn", F32)
    gw_in = jnp.concatenate([gw_main[:, :FB_ORIG], gw_fb[:, :B_HEADS], gw_main[:, FB_ORIG:]], axis=1)
    grads = dict(norm_post_g=g_post_grad, norm_mem_g=g_mem_grad, w_in=gw_in,
                 b_forget=g_bforget.reshape(1, B_HEADS), b_merge=g_bmerge, w_mem_kv=gw_mem_kv,
                 w_branch_a=gw_branch[0], w_branch_b=gw_branch[1], w_branch_m=gw_branch[2], w_out=gw_out)
    side = exchange(grads) if exchange else None
    dh_main = _mm("d_h", du, w_main, "nt", F32, tk=2816, side=side)
    landed = None
    if side:
        dh_main, landed = dh_main[0], dh_main[1:]
    dh_fb = _mm("d_h_fb", dfb, w_fb, "nt", F32)

    def pre_bwd(xv, d1, d2, dyv, gv):
        r = lax.rsqrt(jnp.mean(xv * xv, axis=-1, keepdims=True) + EPS)
        n = xv * r
        dhv = d1 + d2
        dn = dhv * gv
        dx = r * (dn - n * jnp.mean(dn * n, axis=-1, keepdims=True))
        return dyv + dx, jnp.sum(dhv * n, axis=0, keepdims=True)

    grad_x, g_pre_grad = _rows("norm_pre_bwd", pre_bwd, [x, dh_main, dh_fb, dy], [g_pre],
                               [(D_MODEL, F32)], reds=[D_MODEL], tm=256)

    grads["norm_pre_g"] = g_pre_grad
    return loss_lanes, grad_x, grads, landed


HBM_SPEC = pl.BlockSpec(memory_space=pltpu.HBM)


def _place():
    x, y, c = lax.axis_index("x"), lax.axis_index("y"), lax.axis_index("c")
    chips = [(1 - x, y), (x, 1 - y), (1 - x, 1 - y)]
    return x, y, c, 2 * x + y, chips


N_CHUNKS = 4


def _units(parts, row_axis):
    units = []
    for i, a in enumerate(parts):
        ch = a.shape[row_axis] // N_CHUNKS
        units += [(i, pl.ds(k * ch, ch)) for k in range(N_CHUNKS)]
    return units


def _gather_weights(parts):
    n = len(parts)
    units = _units(parts, 1)
    nu = len(units)
    via_y = [(u % N_CHUNKS) < N_CHUNKS // 2 for u in range(nu)]

    def body(*refs):
        srcs, outs = refs[:n], refs[n:2 * n]
        send_sems, recv_sems = refs[2 * n:]
        x, y, c, p, _ = _place()
        me, sib = (x, y, c), (x, y, 1 - c)
        xn, yn, dg = (1 - x, y), (x, 1 - y), (1 - x, 1 - y)

        def cp(u, k, chip, half, to, from_src=False):
            i, rs = units[u]
            dst = outs[i].at[2 * chip[0] + chip[1], half, rs]
            return pltpu.make_async_remote_copy(
                src_ref=srcs[i].at[half, rs] if from_src else dst, dst_ref=dst, send_sem=send_sems.at[u, k],
                recv_sem=recv_sems.at[u, k], device_id=to, device_id_type=MESH)

        sent = []

        def go(copy):
            copy.start()
            sent.append(copy)

        for u in range(nu):
            go(cp(u, 0, (x, y), c, (*xn, c), from_src=True))
            go(cp(u, 1, (x, y), c, (*yn, c), from_src=True))
        for u in range(nu):
            cp(u, 0, xn, c, me).wait_recv()
            go(cp(u, 4, xn, c, sib))
            if via_y[u]:
                go(cp(u, 2, xn, c, (*yn, c)))
            cp(u, 1, yn, c, me).wait_recv()
            go(cp(u, 5, yn, c, sib))
            if not via_y[u]:
                go(cp(u, 3, yn, c, (*xn, c)))
        for u in range(nu):
            cp(u, 2 if via_y[u] else 3, dg, c, me).wait_recv()
            go(cp(u, 6, dg, c, sib))
        for u in range(nu):
            for k, chip in ((4, xn), (5, yn), (6, dg)):
                cp(u, k, chip, 1 - c, me).wait_recv()
        for copy in sent:
            copy.wait_send()

    return pl.pallas_call(
        body, name="gather_weights", in_specs=[HBM_SPEC] * n, out_specs=[HBM_SPEC] * n,
        out_shape=[jax.ShapeDtypeStruct((N_CHIPS,) + a.shape, a.dtype) for a in parts],
        scratch_shapes=[pltpu.SemaphoreType.DMA((nu, 7)), pltpu.SemaphoreType.DMA((nu, 7))],
    )(*parts)


def _swap_with_sibling(parts):
    n = len(parts)
    units = _units(parts, 2)

    def body(*refs):
        srcs, outs = refs[:n], refs[n:2 * n]
        send_sems, recv_sems = refs[2 * n:]
        x, y, c, _, _ = _place()
        cps = [pltpu.make_async_remote_copy(
            src_ref=srcs[i].at[q, 1 - c, rs], dst_ref=outs[i].at[q, rs], send_sem=send_sems.at[u, q],
            recv_sem=recv_sems.at[u, q], device_id=(x, y, 1 - c), device_id_type=MESH)
            for q in range(N_CHIPS) for u, (i, rs) in enumerate(units)]
        for cpy in cps:
            cpy.start()
        for cpy in cps:
            cpy.wait()

    return pl.pallas_call(
        body, name="swap_with_sibling", in_specs=[HBM_SPEC] * n, out_specs=[HBM_SPEC] * n,
        out_shape=[jax.ShapeDtypeStruct(a.shape[:1] + a.shape[2:], a.dtype) for a in parts],
        scratch_shapes=[pltpu.SemaphoreType.DMA((len(units), N_CHIPS)),
                        pltpu.SemaphoreType.DMA((len(units), N_CHIPS))],
    )(*parts)


def _scatter_to_owners(parts):
    n = len(parts)
    units = _units(parts, 1)

    def copies(srcs, outs, send_sems, recv_sems, incoming):
        x, y, c, p, chips = _place()
        return [pltpu.make_async_remote_copy(
            src_ref=srcs[i].at[2 * cx + cy, rs], dst_ref=outs[i].at[(2 * cx + cy) if incoming else p, rs],
            send_sem=send_sems.at[u, j], recv_sem=recv_sems.at[u, j], device_id=(cx, cy, c), device_id_type=MESH)
            for u, (i, rs) in enumerate(units) for j, (cx, cy) in enumerate(chips)]

    def start(ins, outs, scratch):
        for cpy in copies(ins, outs, *scratch, incoming=False):
            cpy.start()

    def wait(ins, outs, scratch):
        for cpy in copies(ins, outs, *scratch, incoming=True):
            cpy.wait_recv()
        for cpy in copies(ins, outs, *scratch, incoming=False):
            cpy.wait_send()

    return dict(ins=list(parts), outs=[jax.ShapeDtypeStruct(a.shape, a.dtype) for a in parts],
                scratch=[pltpu.SemaphoreType.DMA((len(units), 3)), pltpu.SemaphoreType.DMA((len(units), 3))],
                start=start, wait=wait)


def _share_with_sibling(parts):
    n = len(parts)
    units = _units(parts, 1)

    def body(*refs):
        srcs, outs = refs[:n], refs[n:2 * n]
        send_sems, recv_sems = refs[2 * n:]
        x, y, c, _, _ = _place()
        sends = [pltpu.make_async_remote_copy(
            src_ref=srcs[i].at[0, rs], dst_ref=outs[i].at[c, rs], send_sem=send_sems.at[u],
            recv_sem=recv_sems.at[u], device_id=(x, y, 1 - c), device_id_type=MESH)
            for u, (i, rs) in enumerate(units)]
        for cpy in sends:
            cpy.start()
        for u, (i, rs) in enumerate(units):
            pltpu.make_async_remote_copy(
                src_ref=srcs[i].at[0, rs], dst_ref=outs[i].at[1 - c, rs], send_sem=send_sems.at[u],
                recv_sem=recv_sems.at[u], device_id=(x, y, 1 - c), device_id_type=MESH).wait_recv()
        for cpy in sends:
            cpy.wait_send()

    return pl.pallas_call(
        body, name="share_with_sibling", in_specs=[HBM_SPEC] * n, out_specs=[HBM_SPEC] * n,
        out_shape=[jax.ShapeDtypeStruct((2,) + a.shape[1:], a.dtype) for a in parts],
        scratch_shapes=[pltpu.SemaphoreType.DMA((len(units),)), pltpu.SemaphoreType.DMA((len(units),))],
    )(*parts)


def _sum_small(v):
    def body(v_ref, out_ref, buf, send_sems, recv_sems):
        x, y, c, _, _ = _place()
        me = 4 * x + 2 * y + c
        buf[me] = v_ref[...]
        flips = [(dx, dy, dc) for dx in (0, 1) for dy in (0, 1) for dc in (0, 1)][1:]
        sends = []
        for k, (dx, dy, dc) in enumerate(flips):
            cpy = pltpu.make_async_remote_copy(
                src_ref=v_ref, dst_ref=buf.at[me], send_sem=send_sems.at[k], recv_sem=recv_sems.at[k],
                device_id=((x + dx) % 2, (y + dy) % 2, (c + dc) % 2), device_id_type=MESH)
            cpy.start()
            sends.append(cpy)
        for k, (dx, dy, dc) in enumerate(flips):
            px, py, pc = (x + dx) % 2, (y + dy) % 2, (c + dc) % 2
            pltpu.make_async_remote_copy(
                src_ref=v_ref, dst_ref=buf.at[4 * px + 2 * py + pc], send_sem=send_sems.at[k],
                recv_sem=recv_sems.at[k], device_id=(px, py, pc), device_id_type=MESH).wait_recv()
        for cpy in sends:
            cpy.wait_send()
        tot = buf[0]
        for i in range(1, N_DEV):
            tot = tot + buf[i]
        out_ref[...] = tot

    return pl.pallas_call(
        body, name="sum_small", out_shape=jax.ShapeDtypeStruct(v.shape, v.dtype),
        in_specs=[pl.BlockSpec(memory_space=pltpu.VMEM)], out_specs=pl.BlockSpec(memory_space=pltpu.VMEM),
        scratch_shapes=[pltpu.VMEM((N_DEV,) + v.shape, v.dtype), pltpu.SemaphoreType.DMA((N_DEV - 1,)),
                        pltpu.SemaphoreType.DMA((N_DEV - 1,))],
    )(v)


def _add_chips(name, landed, pair, chip):
    nq, r, w = landed.shape
    tr = 64

    def body(chip_ref, *refs):
        own = refs[nq][...].astype(F32)
        tot = None
        for q in range(nq):
            term = jnp.where(chip_ref[0] == q, own, refs[q][...].astype(F32))
            tot = term if tot is None else tot + term
        refs[nq + 1][...] = tot

    specs = [pl.BlockSpec((None, tr, w), functools.partial(lambda j, chip_ref, q: (q, j, 0), q=q)) for q in range(nq)]
    specs.append(pl.BlockSpec((None, tr, w), lambda j, chip_ref: (chip_ref[0], j, 0)))
    grid_spec = pltpu.PrefetchScalarGridSpec(
        num_scalar_prefetch=1, grid=(r // tr,), in_specs=specs,
        out_specs=pl.BlockSpec((None, tr, w), lambda j, chip_ref: (0, j, 0)))
    return pl.pallas_call(
        body, name=name, grid_spec=grid_spec, out_shape=jax.ShapeDtypeStruct((1, r, w), F32),
        compiler_params=_params(("parallel",)),
    )(jnp.reshape(chip, (1,)).astype(jnp.int32), *([landed] * nq), pair)


def _add_pair(name, halves, got, c):
    nq, _, r, w = halves.shape
    tr = 64

    def body(c_ref, a_ref, b_ref, o_ref):
        o_ref[...] = (a_ref[...] + b_ref[...]).astype(o_ref.dtype)

    grid_spec = pltpu.PrefetchScalarGridSpec(
        num_scalar_prefetch=1, grid=(nq, r // tr),
        in_specs=[pl.BlockSpec((None, None, tr, w), lambda i, j, c_ref: (i, c_ref[0], j, 0)),
                  pl.BlockSpec((None, tr, w), lambda i, j, c_ref: (i, j, 0))],
        out_specs=pl.BlockSpec((None, tr, w), lambda i, j, c_ref: (i, j, 0)))
    return pl.pallas_call(
        body, name=name, grid_spec=grid_spec, out_shape=jax.ShapeDtypeStruct((nq, r, w), BF16),
        compiler_params=_params(("parallel", "parallel")),
    )(jnp.reshape(c, (1,)).astype(jnp.int32), halves, got)


def _adamw(name, w, g, m, v, tm):
    def fn(wv, gv, mv, vv):
        m2 = ADAM_B1 * mv + (1.0 - ADAM_B1) * gv
        v2 = ADAM_B2 * vv + (1.0 - ADAM_B2) * (gv * gv)
        m_hat = m2 / (1.0 - ADAM_B1 ** ADAM_STEP)
        v_hat = v2 / (1.0 - ADAM_B2 ** ADAM_STEP)
        return -ADAM_LR * (m_hat / (jnp.sqrt(v_hat) + ADAM_EPS) + ADAM_WD * wv), m2, v2
    c = w.shape[1]
    return _rows(name, fn, [w, g, m, v], [], [(c, F32)] * 3, tm=tm)


REST_ROWS = 256 + 3 * 128 + 256
REST_SPLITS = (("w_mem_kv", 0, 256), ("w_branch_a", 256, 128), ("w_branch_b", 384, 128),
               ("w_branch_m", 512, 128), ("w_out", 640, 256))


def _rest_pack(t):
    return jnp.concatenate([t[n].reshape(rows, D_MODEL) for n, _, rows in REST_SPLITS], axis=0)


def _rest_unpack(a, shapes):
    return {n: a[r0:r0 + rows].reshape(shapes[n]) for n, r0, rows in REST_SPLITS}


def _small_pack(pre, post, memg, bforget, bmerge):
    pad = jnp.zeros((1, D_MODEL - B_HEADS), F32)
    return jnp.concatenate([pre, post, memg, bmerge.reshape(3, D_MODEL),
                            jnp.concatenate([bforget, pad], axis=1), jnp.zeros((1, D_MODEL), F32)], axis=0)


def _small_unpack(s8):
    return dict(norm_pre_g=s8[0:1], norm_post_g=s8[1:2], norm_mem_g=s8[2:3],
                b_merge=s8[3:6].reshape(1, 3 * D_MODEL), b_forget=s8[6:7, :B_HEADS])


WEIGHTS = ("norm_pre_g", "norm_post_g", "norm_mem_g", "w_in", "b_forget", "b_merge", "w_mem_kv",
           "w_branch_a", "w_branch_b", "w_branch_m", "w_out")
SMALL = ("norm_pre_g", "norm_post_g", "norm_mem_g", "b_forget", "b_merge")


def kernel(x, mem, positions, norm_pre_g, norm_post_g, norm_mem_g, w_in, b_forget, b_merge, w_mem_kv, w_branch_a, w_branch_b, w_branch_m, w_out, loss_target, m_norm_pre_g, m_norm_post_g, m_norm_mem_g, m_w_in, m_b_forget, m_b_merge, m_w_mem_kv, m_w_branch_a, m_w_branch_b, m_w_branch_m, m_w_out, v_norm_pre_g, v_norm_post_g, v_norm_mem_g, v_w_in, v_b_forget, v_b_merge, v_w_mem_kv, v_w_branch_a, v_w_branch_b, v_w_branch_m, v_w_out):
    w = dict(norm_pre_g=norm_pre_g, norm_post_g=norm_post_g, norm_mem_g=norm_mem_g, w_in=w_in[0],
             b_forget=b_forget, b_merge=b_merge, w_mem_kv=w_mem_kv[0], w_branch_a=w_branch_a[0],
             w_branch_b=w_branch_b[0], w_branch_m=w_branch_m[0], w_out=w_out[0])
    mo = dict(norm_pre_g=m_norm_pre_g, norm_post_g=m_norm_post_g, norm_mem_g=m_norm_mem_g, w_in=m_w_in[0],
              b_forget=m_b_forget, b_merge=m_b_merge, w_mem_kv=m_w_mem_kv[0], w_branch_a=m_w_branch_a[0],
              w_branch_b=m_w_branch_b[0], w_branch_m=m_w_branch_m[0], w_out=m_w_out[0])
    vo = dict(norm_pre_g=v_norm_pre_g, norm_post_g=v_norm_post_g, norm_mem_g=v_norm_mem_g, w_in=v_w_in[0],
              b_forget=v_b_forget, b_merge=v_b_merge, w_mem_kv=v_w_mem_kv[0], w_branch_a=v_w_branch_a[0],
              w_branch_b=v_w_branch_b[0], w_branch_m=v_w_branch_m[0], w_out=v_w_out[0])
    s = x.shape[1]
    c = lax.axis_index("c")

    chip = 2 * lax.axis_index("x") + lax.axis_index("y")

    def put(whole, own, slot):
        return lax.dynamic_update_index_in_dim(whole, own.astype(whole.dtype), slot, 0)

    own_w = [w["w_in"].astype(BF16).reshape(2, D_MODEL // 2, SHARD_COLS),
             _rest_pack(w).astype(BF16).reshape(2, REST_ROWS // 2, D_MODEL)]
    all_in, all_rest = _gather_weights(own_w)
    all_in = all_in.reshape(N_CHIPS, D_MODEL, SHARD_COLS)
    own_in, own_rest = own_w[0].reshape(D_MODEL, SHARD_COLS), own_w[1].reshape(REST_ROWS, D_MODEL)
    w_in_f = jnp.concatenate([jnp.where(chip == p, own_in, all_in[p]) for p in range(N_CHIPS)], axis=1)
    all_rest = all_rest.reshape(N_CHIPS, REST_ROWS, D_MODEL)
    all_rest = jnp.stack([jnp.where(chip == p, own_rest, all_rest[p]) for p in range(N_CHIPS)])
    w_kv_f = all_rest[:, 0:256].reshape(D_MODEL, D_MODEL)
    w_br_f = [all_rest[:, 256 + 128 * i:384 + 128 * i].reshape(N_CHIPS, 512, 256).transpose(1, 0, 2)
              .reshape(512, D_MODEL) for i in range(3)]
    w_out_f = all_rest[:, 640:896].reshape(D_MODEL, D_MODEL)
    w_main = jnp.concatenate([w_in_f[:, :FB_ORIG], w_in_f[:, FB_ORIG + B_HEADS:]], axis=1)
    w_fb = jnp.concatenate([w_in_f[:, FB_ORIG:FB_ORIG + B_HEADS], jnp.zeros((D_MODEL, HD - B_HEADS), BF16)], axis=1)

    pair = []

    def exchange(g):
        def per_chip(name, p):
            a = g[name]
            if name in ("w_mem_kv", "w_out"):
                return a[256 * p:256 * (p + 1)]
            return a[:, 256 * p:256 * (p + 1)]

        in4 = jnp.stack([g["w_in"][:, SHARD_COLS * p:SHARD_COLS * (p + 1)] for p in range(N_CHIPS)])
        rest4 = jnp.stack([_rest_pack({n: per_chip(n, p) for n, _, _ in REST_SPLITS}) for p in range(N_CHIPS)])
        halves = [in4.reshape(N_CHIPS, 2, D_MODEL // 2, SHARD_COLS),
                  rest4.reshape(N_CHIPS, 2, REST_ROWS // 2, D_MODEL)]
        got = _swap_with_sibling(halves)
        pair.extend(_add_pair("add_pair_%d" % i, halves[i], got[i], c) for i in range(2))
        return _scatter_to_owners(pair)

    loss_lanes, grad_x, g, landed = _local_step(
        x[0], mem[0], positions.reshape(s, 1), loss_target[0], norm_pre_g, norm_post_g, norm_mem_g,
        w_main, w_fb, b_forget, b_merge, w_kv_f, w_br_f[0], w_br_f[1], w_br_f[2], w_out_f, exchange)
    loss = lax.psum(jnp.sum(loss_lanes), ("x", "y", "c"))
    half = [_add_chips("add_chips_%d" % i, landed[i], pair[i], chip) for i in range(2)]
    red_in, red_rest = [put(a, o[0], c) for a, o in zip(_share_with_sibling(half), half)]
    gs = {"w_in": red_in.reshape(D_MODEL, SHARD_COLS)}
    gs.update(_rest_unpack(red_rest.reshape(REST_ROWS, D_MODEL), {n: w[n].shape for n, _, _ in REST_SPLITS}))
    gs.update(_small_unpack(_sum_small(_small_pack(
        g["norm_pre_g"], g["norm_post_g"], g["norm_mem_g"], g["b_forget"], g["b_merge"]))))

    delta, new_m, new_v = {}, {}, {}
    for n, tm in (("w_in", 128), ("w_mem_kv", 256), ("w_branch_a", 512), ("w_branch_b", 512),
                  ("w_branch_m", 512), ("w_out", 256)):
        d_, m_, v_ = _adamw("adamw_" + n, w[n], gs[n], mo[n], vo[n], tm)
        delta[n], new_m[n], new_v[n] = d_[None], m_[None], v_[None]
        gs[n] = gs[n][None]
    packs = [_small_pack(*[t[n] for n in ("norm_pre_g", "norm_post_g", "norm_mem_g", "b_forget", "b_merge")])
             for t in (w, gs, mo, vo)]
    for res, store in zip(_adamw("adamw_small", *packs, 8), (delta, new_m, new_v)):
        store.update(_small_unpack(res))

    return (loss, grad_x[None], *[gs[n] for n in WEIGHTS], *[delta[n] for n in WEIGHTS],
            *[new_m[n] for n in WEIGHTS], *[new_v[n] for n in WEIGHTS])
```

```python
import functools

import jax
import jax.numpy as jnp
from jax import lax
from jax.experimental import pallas as pl
from jax.experimental.pallas import tpu as pltpu

F32 = jnp.float32
BF16 = jnp.bfloat16
MESH = pl.DeviceIdType.MESH

D_MODEL = 1024
N_MEM = 256
EPS = 1e-6
NEG = -1e30
ROPE_THETA = 500000.0
ROT_DIM = 32
HD = 128
A_GROUP = 512
DILATIONS = (1, 4, 16)
BAND = 128
B_HEADS = 8
B_HD = 64
N_CHIPS = 4
N_DEV = 8

C_QA, C_KA, C_VA, C_ZA = 0, 1536, 3072, 4608
C_QB, C_KB, C_VB, C_ZB = 5120, 5632, 6144, 6656
C_QM, C_ZM, C_GL = 7168, 7680, 8192
N_MAIN = 11264
FB_ORIG = 6656
IN_COLS = 11272
SHARD_COLS = IN_COLS // N_CHIPS

ADAM_LR, ADAM_B1, ADAM_B2, ADAM_EPS, ADAM_WD, ADAM_STEP = 0.001, 0.9, 0.999, 1e-08, 0.01, 10

VMEM_LIMIT_V7X = 56 * 1024 * 1024

NT = (((1,), (1,)), ((), ()))
NN = (((1,), (0,)), ((), ()))
TN = (((0,), (0,)), ((), ()))


def _params(sem):
    return pltpu.CompilerParams(dimension_semantics=sem, vmem_limit_bytes=VMEM_LIMIT_V7X)


def _dot(a, b, dn=NN):
    return lax.dot_general(a, b, dn, preferred_element_type=F32)


def _sig(z):
    return 1.0 / (1.0 + jnp.exp(-z))


def _rows(name, fn, row_ins, bc_ins, outs, reds=(), tm=512, scratch=()):
    arrs, specs = [], []
    s = None
    for r in row_ins:
        arr, w, cb, d = (tuple(r) + (1,))[:4] if isinstance(r, tuple) else (r, r.shape[1], 0, 1)
        s = arr.shape[0] * d if s is None else s
        arrs.append(arr)
        specs.append((w, cb, d))
    tm = min(tm, s)
    specs = [pl.BlockSpec((tm // d, w), functools.partial(lambda i, cb: (i, cb), cb=cb)) for w, cb, d in specs]
    for b in bc_ins:
        arrs.append(b)
        specs.append(pl.BlockSpec(b.shape, lambda i: (0, 0)))
    outs = [(tuple(o) + (1,))[:3] for o in outs]
    n_in, n_out = len(arrs), len(outs)

    def body(*refs):
        n_ref = n_in + n_out + len(reds)
        vals = fn(*[r[...] for r in refs[:n_in]], *refs[n_ref:])
        if not isinstance(vals, (tuple, list)):
            vals = (vals,)
        for r, v in zip(refs[n_in:n_in + n_out], vals[:n_out]):
            r[...] = v.astype(r.dtype)
        if reds:
            red_refs = refs[n_in + n_out:n_ref]

            @pl.when(pl.program_id(0) == 0)
            def _():
                for r in red_refs:
                    r[...] = jnp.zeros_like(r)

            for r, v in zip(red_refs, vals[n_out:]):
                r[...] += v

    out_shape = [jax.ShapeDtypeStruct((s // d, c), dt) for c, dt, d in outs]
    out_shape += [jax.ShapeDtypeStruct((1, c), F32) for c in reds]
    out_specs = [pl.BlockSpec((tm // d, c), lambda i: (i, 0)) for c, _, d in outs]
    out_specs += [pl.BlockSpec((1, c), lambda i: (0, 0)) for c in reds]
    res = pl.pallas_call(
        body, name=name, grid=(s // tm,), in_specs=specs, out_specs=out_specs, out_shape=out_shape,
        scratch_shapes=list(scratch),
        compiler_params=_params(("arbitrary",) if reds else ("parallel",)),
    )(*arrs)
    return res


def _to_class(x, scr, d):
    if d == 1:
        return x.astype(F32)
    tm, c = x.shape
    for g in range(c // 128):
        scr[g][...] = x[:, g * 128:(g + 1) * 128].astype(F32)
    return jnp.concatenate([scr[g][pl.ds(r, tm // d, stride=d), :] for r in range(d) for g in range(c // 128)],
                           axis=1)


def _from_class(x, scr, d):
    if d == 1:
        return x.astype(F32)
    n, dc = x.shape
    c = dc // d
    for r in range(d):
        for g in range(c // 128):
            scr[g][pl.ds(r, n, stride=d), :] = x[:, r * c + g * 128:r * c + (g + 1) * 128].astype(F32)
    return jnp.concatenate([scr[g][...] for g in range(c // 128)], axis=1)


def _mm(name, a, b, mode, out_dtype, tm=1024, tn=1024, tk=1024, side=None):
    if mode == "nn":
        (m, k), (_, n) = a.shape, b.shape
    elif mode == "nt":
        (m, k), (n, _) = a.shape, b.shape
    else:
        (k, m), (_, n) = a.shape, b.shape
    tm, tn, tk = min(tm, m), min(tn, n), min(tk, k)
    nk = k // tk
    grid = (m // tm, n // tn, nk)
    dn = {"nn": NN, "nt": NT, "tn": TN}[mode]
    n_si = len(side["ins"]) if side else 0
    n_so = len(side["outs"]) if side else 0
    n_acc = 1 if nk > 1 else 0

    def body(*refs):
        a_ref, b_ref = refs[:2]
        side_in, o_ref = refs[2:2 + n_si], refs[2 + n_si]
        side_out = refs[3 + n_si:3 + n_si + n_so]
        acc = refs[3 + n_si + n_so:3 + n_si + n_so + n_acc]
        side_scratch = refs[3 + n_si + n_so + n_acc:]
        step = (pl.program_id(0) * grid[1] + pl.program_id(1)) * grid[2] + pl.program_id(2)
        if side:
            @pl.when(step == 0)
            def _():
                side["start"](side_in, side_out, side_scratch)

        part = _dot(a_ref[...].astype(BF16), b_ref[...].astype(BF16), dn)
        if nk == 1:
            o_ref[...] = part.astype(o_ref.dtype)
        else:
            kk = pl.program_id(2)

            @pl.when(kk == 0)
            def _():
                acc[0][...] = part

            @pl.when(kk > 0)
            def _():
                acc[0][...] += part

            @pl.when(kk == nk - 1)
            def _():
                o_ref[...] = acc[0][...].astype(o_ref.dtype)

        if side:
            @pl.when(step == grid[0] * grid[1] * grid[2] - 1)
            def _():
                side["wait"](side_in, side_out, side_scratch)

    a_spec = (pl.BlockSpec((tk, tm), lambda i, j, kk: (kk, i)) if mode == "tn"
              else pl.BlockSpec((tm, tk), lambda i, j, kk: (i, kk)))
    b_spec = (pl.BlockSpec((tn, tk), lambda i, j, kk: (j, kk)) if mode == "nt"
              else pl.BlockSpec((tk, tn), lambda i, j, kk: (kk, j)))
    o_spec = pl.BlockSpec((tm, tn), lambda i, j, kk: (i, j))
    o_shape = jax.ShapeDtypeStruct((m, n), out_dtype)
    acc_scratch = [pltpu.VMEM((tm, tn), F32)] * n_acc
    if not side:
        return pl.pallas_call(
            body, name=name, grid=grid, in_specs=[a_spec, b_spec], out_specs=o_spec, out_shape=o_shape,
            scratch_shapes=acc_scratch, compiler_params=_params(("parallel", "parallel", "arbitrary")),
        )(a, b)
    return pl.pallas_call(
        body, name=name, grid=grid, in_specs=[a_spec, b_spec] + [HBM_SPEC] * n_si,
        out_specs=[o_spec] + [HBM_SPEC] * n_so, out_shape=[o_shape] + side["outs"],
        scratch_shapes=acc_scratch + side["scratch"],
        compiler_params=_params(("arbitrary", "arbitrary", "arbitrary")),
    )(a, b, *side["ins"])


def _rms_fwd(name, x, g):
    def fn(xv, gv):
        r = lax.rsqrt(jnp.mean(xv * xv, axis=-1, keepdims=True) + EPS)
        return (xv * r * gv,)
    return _rows(name, fn, [x], [g], [(x.shape[1], BF16)], tm=min(512, x.shape[0]))[0]


def _rope_tables(pos, inv):
    ang = pos.astype(F32) * inv
    lane = lax.broadcasted_iota(jnp.int32, ang.shape, 1)
    c = jnp.where(lane < ROT_DIM, jnp.cos(ang), 1.0)
    sn = jnp.sin(ang)
    sg = jnp.where(lane < ROT_DIM // 2, -sn, jnp.where(lane < ROT_DIM, sn, 0.0))
    return c, sg, lane


def _rope_apply(x, c, sg, lane):
    outs = []
    for h in range(x.shape[1] // HD):
        xh = x[:, h * HD:(h + 1) * HD].astype(F32)
        swap = jnp.where(lane < ROT_DIM // 2, pltpu.roll(xh, HD - ROT_DIM // 2, 1),
                         pltpu.roll(xh, ROT_DIM // 2, 1))
        outs.append(xh * c + swap * sg)
    return jnp.concatenate(outs, axis=1)


ROPE_TM = 256


def _class_scratch(tm):
    return [pltpu.VMEM((tm, 128), F32) for _ in range(A_GROUP // 128)]


def _rope_fwd(u, pos, inv):
    def fn(q, k, v, p, iv, *scr):
        c, sg, lane = _rope_tables(p, iv)
        qr, kr = _rope_apply(q, c, sg, lane), _rope_apply(k, c, sg, lane)
        outs = []
        for g, d in enumerate(DILATIONS):
            gs = slice(g * A_GROUP, (g + 1) * A_GROUP)
            outs += [_to_class(qr[:, gs], scr, d), _to_class(kr[:, gs], scr, d), _to_class(v[:, gs], scr, d)]
        return tuple(outs)

    outs = [(d * A_GROUP, BF16, d) for d in DILATIONS for _ in range(3)]
    return _rows("rope_fwd", fn, [(u, 1536, 0), (u, 1536, 1), (u, 1536, 2), pos], [inv], outs, tm=ROPE_TM,
                 scratch=_class_scratch(ROPE_TM))


def _rope_bwd(dqs, dks, dvs, pos, inv):
    def fn(*args):
        grads, p, iv, scr = args[:9], args[9], args[10], args[11:]
        c, sg, lane = _rope_tables(p, iv)
        tok = [jnp.concatenate([_from_class(grads[3 * k + g], scr, d) for g, d in enumerate(DILATIONS)], axis=1)
               for k in range(3)]
        return _rope_apply(tok[0], c, -sg, lane), _rope_apply(tok[1], c, -sg, lane), tok[2]

    ins = [(a, a.shape[1], 0, d) for grp in (dqs, dks, dvs) for a, d in zip(grp, DILATIONS)]
    return _rows("rope_bwd", fn, ins + [pos], [inv], [(1536, BF16)] * 3, tm=ROPE_TM,
                 scratch=_class_scratch(ROPE_TM))


def _lane_pack(cols, like):
    lane = lax.broadcasted_iota(jnp.int32, like, 1)
    out = jnp.zeros(like, F32)
    for h, cvec in enumerate(cols):
        out = jnp.where(lane == h, cvec, out)
    return out


def _band_specs(l, d, tq):
    nsb = tq // BAND
    nblk = l // BAND
    cur = pl.BlockSpec((tq, A_GROUP), lambda r, i: (i, r))
    prev = pl.BlockSpec((BAND, A_GROUP), lambda r, i: (jnp.maximum(i * nsb - 1, 0), r))
    nxt = pl.BlockSpec((BAND, A_GROUP), lambda r, i: (jnp.minimum((i + 1) * nsb, nblk - 1), r))
    st_cur = pl.BlockSpec((tq, HD), lambda r, i: (i, r))
    st_nxt = pl.BlockSpec((BAND, HD), lambda r, i: (jnp.minimum((i + 1) * nsb, nblk - 1), r))
    return nsb, cur, prev, nxt, st_cur, st_nxt


def _band_mask_q(i, first_tile):
    qr = lax.broadcasted_iota(jnp.int32, (BAND, 2 * BAND), 0)
    kc = lax.broadcasted_iota(jnp.int32, (BAND, 2 * BAND), 1)
    in_prev = (kc < BAND) & (kc >= qr)
    in_cur = (kc >= BAND) & (kc - BAND <= qr)
    if i == 0:
        in_prev = in_prev & jnp.logical_not(first_tile)
    return in_prev | in_cur


def _band_mask_k(j, nsb, last_tile):
    kc = lax.broadcasted_iota(jnp.int32, (BAND, 2 * BAND), 0)
    qr = lax.broadcasted_iota(jnp.int32, (BAND, 2 * BAND), 1)
    same = (qr < BAND) & (kc <= qr)
    nxt = (qr >= BAND) & (kc >= qr - BAND)
    if j == nsb - 1:
        nxt = nxt & jnp.logical_not(last_tile)
    return same | nxt


def _band_fwd(name, q, k, v, d):
    l = q.shape[0]
    tq = min(512, l)
    nsb, cur, prev, _, st_cur, _ = _band_specs(l, d, tq)
    scale = HD ** -0.5

    def body(q_ref, kc_ref, kp_ref, vc_ref, vp_ref, o_ref, lse_ref):
        first = pl.program_id(1) == 0
        for i in range(nsb):
            lses = []
            mask = _band_mask_q(i, first)
            for h in range(4):
                cs = slice(h * HD, (h + 1) * HD)
                qv = q_ref[i * BAND:(i + 1) * BAND, cs]
                if i == 0:
                    kk = jnp.concatenate([kp_ref[:, cs], kc_ref[0:BAND, cs]], axis=0)
                    vv = jnp.concatenate([vp_ref[:, cs], vc_ref[0:BAND, cs]], axis=0)
                else:
                    kk = kc_ref[(i - 1) * BAND:(i + 1) * BAND, cs]
                    vv = vc_ref[(i - 1) * BAND:(i + 1) * BAND, cs]
                s = jnp.where(mask, _dot(qv, kk, NT) * scale, NEG)
                m = jnp.max(s, axis=-1, keepdims=True)
                p = jnp.exp(s - m)
                den = jnp.sum(p, axis=-1, keepdims=True)
                o_ref[i * BAND:(i + 1) * BAND, cs] = _dot(p.astype(BF16), vv) / den
                lses.append(m + jnp.log(den))
            lse_ref[i * BAND:(i + 1) * BAND, :] = _lane_pack(lses, (BAND, HD))

    return pl.pallas_call(
        body, name=name, grid=(d, l // tq), in_specs=[cur, cur, prev, cur, prev],
        out_specs=[cur, st_cur],
        out_shape=[jax.ShapeDtypeStruct((l, d * A_GROUP), F32), jax.ShapeDtypeStruct((l, d * HD), F32)],
        compiler_params=_params(("parallel", "parallel")),
    )(q, k, k, v, v)


def _band_dq(name, q, k, v, dy, lse, delta, d):
    l = q.shape[0]
    tq = min(512, l)
    nsb, cur, prev, _, st_cur, _ = _band_specs(l, d, tq)
    scale = HD ** -0.5

    def body(q_ref, kc_ref, kp_ref, vc_ref, vp_ref, dy_ref, lse_ref, dl_ref, dq_ref):
        first = pl.program_id(1) == 0
        for i in range(nsb):
            mask = _band_mask_q(i, first)
            rs = slice(i * BAND, (i + 1) * BAND)
            for h in range(4):
                cs = slice(h * HD, (h + 1) * HD)
                if i == 0:
                    kk = jnp.concatenate([kp_ref[:, cs], kc_ref[0:BAND, cs]], axis=0)
                    vv = jnp.concatenate([vp_ref[:, cs], vc_ref[0:BAND, cs]], axis=0)
                else:
                    kk = kc_ref[(i - 1) * BAND:(i + 1) * BAND, cs]
                    vv = vc_ref[(i - 1) * BAND:(i + 1) * BAND, cs]
                s = jnp.where(mask, _dot(q_ref[rs, cs], kk, NT) * scale, NEG)
                p = jnp.exp(s - lse_ref[rs, h:h + 1])
                dp = _dot(dy_ref[rs, cs], vv, NT)
                ds = p * (dp - dl_ref[rs, h:h + 1])
                dq_ref[rs, cs] = (_dot(ds.astype(BF16), kk) * scale).astype(dq_ref.dtype)

    return pl.pallas_call(
        body, name=name, grid=(d, l // tq),
        in_specs=[cur, cur, prev, cur, prev, cur, st_cur, st_cur], out_specs=cur,
        out_shape=jax.ShapeDtypeStruct((l, d * A_GROUP), BF16),
        compiler_params=_params(("parallel", "parallel")),
    )(q, k, k, v, v, dy, lse, delta)


def _band_dkv(name, q, k, v, dy, lse, delta, d):
    l = q.shape[0]
    tq = min(512, l)
    nsb, cur, _, nxt, st_cur, st_nxt = _band_specs(l, d, tq)
    scale = HD ** -0.5
    ntile = l // tq

    def body(k_ref, v_ref, qc_ref, qn_ref, dyc_ref, dyn_ref, lc_ref, ln_ref, dc_ref, dn_ref,
             dk_ref, dv_ref):
        last = pl.program_id(1) == ntile - 1

        def win(c_ref, n_ref, j, cs):
            if j == nsb - 1:
                return jnp.concatenate([c_ref[j * BAND:(j + 1) * BAND, cs], n_ref[:, cs]], axis=0)
            return c_ref[j * BAND:(j + 2) * BAND, cs]

        allh = slice(0, HD)
        for j in range(nsb):
            mask = _band_mask_k(j, nsb, last)
            rs = slice(j * BAND, (j + 1) * BAND)
            lse_t = win(lc_ref, ln_ref, j, allh).T
            delta_t = win(dc_ref, dn_ref, j, allh).T
            for h in range(4):
                cs = slice(h * HD, (h + 1) * HD)
                qw = win(qc_ref, qn_ref, j, cs)
                dyw = win(dyc_ref, dyn_ref, j, cs)
                st = jnp.where(mask, _dot(k_ref[rs, cs], qw, NT) * scale, NEG)
                pt = jnp.exp(st - lse_t[h:h + 1, :])
                dst = pt * (_dot(v_ref[rs, cs], dyw, NT) - delta_t[h:h + 1, :])
                dv_ref[rs, cs] = _dot(pt.astype(BF16), dyw).astype(dv_ref.dtype)
                dk_ref[rs, cs] = (_dot(dst.astype(BF16), qw) * scale).astype(dk_ref.dtype)

    shp = jax.ShapeDtypeStruct((l, d * A_GROUP), BF16)
    return pl.pallas_call(
        body, name=name, grid=(d, ntile),
        in_specs=[cur, cur, cur, nxt, cur, nxt, st_cur, st_nxt, st_cur, st_nxt],
        out_specs=[cur, cur], out_shape=[shp, shp],
        compiler_params=_params(("parallel", "parallel")),
    )(k, v, q, q, dy, dy, lse, lse, delta, delta)


def _split3(x):
    hi = x.astype(BF16)
    r1 = x - hi.astype(F32)
    mid = r1.astype(BF16)
    lo = (r1 - mid.astype(F32)).astype(BF16)
    return hi, mid, lo


def _fox_prep(z, b):
    h, s = z.shape
    blk = min(512, s)

    def body(z_ref, b_ref, c_ref):
        r = lax.broadcasted_iota(jnp.int32, (blk, blk), 0)
        cidx = lax.broadcasted_iota(jnp.int32, (blk, blk), 1)
        tri = (r <= cidx).astype(BF16)
        carry = jnp.zeros((h, 1), F32)
        for t in range(s // blk):
            zz = z_ref[:, t * blk:(t + 1) * blk] + b_ref[...]
            lf = jnp.minimum(zz, 0.0) - jnp.log(1.0 + jnp.exp(-jnp.abs(zz)))
            hi, mid, lo = _split3(lf)
            cs = _dot(hi, tri) + _dot(mid, tri) + _dot(lo, tri) + carry
            c_ref[:, t * blk:(t + 1) * blk] = cs
            carry = cs[:, blk - 1:blk]

    return pl.pallas_call(body, name="fox_prep", out_shape=jax.ShapeDtypeStruct((h, s), F32))(z, b)


def _fox_prep_bwd(dc, z, b):
    h, s = z.shape
    blk = min(512, s)

    def body(dc_ref, z_ref, b_ref, dz_ref, db_ref):
        r = lax.broadcasted_iota(jnp.int32, (blk, blk), 0)
        cidx = lax.broadcasted_iota(jnp.int32, (blk, blk), 1)
        tri = (r >= cidx).astype(BF16)
        carry = jnp.zeros((h, 1), F32)
        tot = jnp.zeros((h, 1), F32)
        for t in reversed(range(s // blk)):
            hi, mid, lo = _split3(dc_ref[:, t * blk:(t + 1) * blk])
            rc = _dot(hi, tri) + _dot(mid, tri) + _dot(lo, tri) + carry
            carry = rc[:, 0:1]
            zz = z_ref[:, t * blk:(t + 1) * blk] + b_ref[...]
            dz = rc * _sig(-zz)
            dz_ref[:, t * blk:(t + 1) * blk] = dz
            tot = tot + jnp.sum(dz, axis=-1, keepdims=True)
        db_ref[...] = tot

    return pl.pallas_call(
        body, name="fox_prep_bwd",
        out_shape=[jax.ShapeDtypeStruct((h, s), F32), jax.ShapeDtypeStruct((h, 1), F32)])(dc, z, b)


FOX_W = 128
FOX_C = B_HD
FOX_ONE = B_HD + 3
FOX_SUB = 256
FOX_SUB_FWD = 128
FOX_HEADS_PER_STEP = 2


def _head_of_pair(x, hh):
    return x if hh == 0 else pltpu.roll(x, B_HD, 1)


def _fox_pack(u, c_col, t):
    s = u.shape[0]
    nt = s // t
    scale = B_HD ** -0.5

    def body(q_ref, k_ref, v_ref, c_ref, qf_ref, kb_ref, ks_ref, vb_ref, vt_ref):
        lane = lax.broadcasted_iota(jnp.int32, (t, FOX_W), 1)
        qv, kv, vv = [r[...].astype(F32) for r in (q_ref, k_ref, v_ref)]
        for hh in range(2):
            qf_ref[hh] = jnp.where(lane < B_HD, _head_of_pair(qv, hh), B_HD ** 0.5).astype(BF16)
            neg = c_ref[hh] * (-scale)
            hi = neg.astype(BF16).astype(F32)
            mid = (neg - hi).astype(BF16).astype(F32)
            lo = neg - hi - mid
            aux = jnp.where(lane == FOX_C, hi,
                            jnp.where(lane == FOX_C + 1, mid, jnp.where(lane == FOX_C + 2, lo, 0.0)))
            kb = jnp.where(lane < B_HD, _head_of_pair(kv, hh) * scale, aux)
            kb_ref[hh] = kb.astype(BF16)
            ks_ref[hh] = jnp.where(lane == FOX_ONE, 1.0, kb).T.astype(BF16)
            vb = jnp.where(lane < B_HD, _head_of_pair(vv, hh), 1.0)
            vb_ref[hh] = vb.astype(BF16)
            vt_ref[hh] = vb.T.astype(BF16)

    def tok(col0):
        return pl.BlockSpec((t, FOX_W), functools.partial(lambda hp, i, cb: (i, cb + hp), cb=col0 // FOX_W))

    rows = pl.BlockSpec((2, t, FOX_W), lambda hp, i: (hp, i, 0))
    tiles = pl.BlockSpec((2, None, FOX_W, t), lambda hp, i: (hp, i, 0, 0))
    hm = jax.ShapeDtypeStruct((B_HEADS, s, FOX_W), BF16)
    tt = jax.ShapeDtypeStruct((B_HEADS, nt, FOX_W, t), BF16)
    return pl.pallas_call(
        body, name="fox_pack", grid=(B_HEADS // 2, nt),
        in_specs=[tok(C_QB), tok(C_KB), tok(C_VB), pl.BlockSpec((2, t, 1), lambda hp, i: (hp, i, 0))],
        out_specs=[rows, rows, tiles, rows, tiles], out_shape=[hm, hm, tt, hm, tt],
        compiler_params=_params(("parallel", "parallel")),
    )(u, u, u, c_col)


def _fox_pack_bwd(dy, y, t):
    s = dy.shape[0]
    nt = s // t

    def body(do_ref, o_ref, dow_ref, dl_ref):
        lane = lax.broadcasted_iota(jnp.int32, (t, FOX_W), 1)
        lane8 = lax.broadcasted_iota(jnp.int32, (8, FOX_W), 1)
        dov = do_ref[...].astype(F32)
        parts = _split3(dov * o_ref[...].astype(F32))
        for hh in range(2):
            dow_ref[hh] = jnp.where(lane < B_HD, _head_of_pair(dov, hh), 0.0).astype(BF16)
            mask = ((lane8 >= hh * B_HD) & (lane8 < (hh + 1) * B_HD)).astype(BF16)
            row = _dot(mask, parts[0], NT) + _dot(mask, parts[1], NT) + _dot(mask, parts[2], NT)
            dl_ref[hh] = row[0:1, :]

    tok = pl.BlockSpec((t, FOX_W), lambda hp, i: (i, hp))
    return pl.pallas_call(
        body, name="fox_pack_bwd", grid=(B_HEADS // 2, nt), in_specs=[tok, tok],
        out_specs=[pl.BlockSpec((2, t, FOX_W), lambda hp, i: (hp, i, 0)),
                   pl.BlockSpec((2, None, 1, t), lambda hp, i: (hp, i, 0, 0))],
        out_shape=[jax.ShapeDtypeStruct((B_HEADS, s, FOX_W), BF16), jax.ShapeDtypeStruct((B_HEADS, nt, 1, t), F32)],
        compiler_params=_params(("parallel", "parallel")),
    )(dy, y)


def _fox_unpack(dqt, dkw, dvw, t):
    h, nt = dqt.shape[:2]
    s = nt * t

    def body(dq_ref, dk_ref, dv_ref, dqo_ref, dko_ref, dvo_ref, dc_ref):
        lane = lax.broadcasted_iota(jnp.int32, (t, FOX_W), 1)

        def join(a0, a1):
            return jnp.where(lane < B_HD, a0, pltpu.roll(a1, B_HD, 1))

        for hh in range(2):
            dc_ref[hh] = dq_ref[hh][FOX_ONE:FOX_ONE + 1, :] - dk_ref[hh].T[B_HD:B_HD + 1, :]
        dqo_ref[...] = join(dq_ref[0].T, dq_ref[1].T).astype(BF16)
        dko_ref[...] = join(dk_ref[0], dk_ref[1]).astype(BF16)
        dvo_ref[...] = join(dv_ref[0], dv_ref[1]).astype(BF16)

    tok = pl.BlockSpec((t, FOX_W), lambda hp, i: (i, hp))
    rows = pl.BlockSpec((2, t, FOX_W), lambda hp, i: (hp, i, 0))
    shp = jax.ShapeDtypeStruct((s, h * B_HD), BF16)
    return pl.pallas_call(
        body, name="fox_unpack", grid=(h // 2, nt),
        in_specs=[pl.BlockSpec((2, None, FOX_W, t), lambda hp, i: (hp, i, 0, 0)), rows, rows],
        out_specs=[tok, tok, tok, pl.BlockSpec((2, None, 1, t), lambda hp, i: (hp, i, 0, 0))],
        out_shape=[shp, shp, shp, jax.ShapeDtypeStruct((h, nt, 1, t), F32)],
        compiler_params=_params(("parallel", "parallel")),
    )(dqt, dkw, dvw)


FOX_DEAD = -110.0


def _fox_norm2(qf, kb):
    h, s, w = qf.shape
    tm = min(2048, s)

    def body(q_ref, k_ref, qo_ref, ko_ref):
        row = lax.broadcasted_iota(jnp.int32, (w, w), 0)
        ones = (row < B_HD).astype(BF16)
        for x_ref, o_ref in ((q_ref, qo_ref), (k_ref, ko_ref)):
            xv = x_ref[...].astype(F32)
            n2 = _dot((xv * xv).astype(BF16), ones)
            o_ref[...] = jnp.broadcast_to(jnp.max(n2, axis=0, keepdims=True)[:, :1], o_ref.shape)

    spec = pl.BlockSpec((None, tm, w), lambda hh, i: (hh, i, 0))
    ospec = pl.BlockSpec((None, None, 8, 128), lambda hh, i: (hh, i, 0, 0))
    shp = jax.ShapeDtypeStruct((h, s // tm, 8, 128), F32)
    return pl.pallas_call(
        body, name="fox_norm2", grid=(h, s // tm), in_specs=[spec, spec], out_specs=[ospec, ospec],
        out_shape=[shp, shp], compiler_params=_params(("parallel", "parallel")),
    )(qf, kb)


def _fox_bounds(qf, kb, c, t):
    q2, k2 = _fox_norm2(qf, kb)
    g = 2.0 * jnp.sqrt(1.02 * jnp.max(q2[:, :, 0, 0], axis=1) * 1.02 * jnp.max(k2[:, :, 0, 0], axis=1))
    return jnp.concatenate([c[:, ::t], c[:, t - 1::t], g[:, None]], axis=1)


SMEM_SPEC = pl.BlockSpec(memory_space=pltpu.SMEM)


def _fox_fwd(qf, kb, vt4, bounds, t):
    h, s, w = qf.shape
    nt = s // t
    sub = FOX_SUB_FWD
    nsub = t // sub
    nh = FOX_HEADS_PER_STEP

    def body(b_ref, q_ref, k_ref, v_ref, o_ref, lse_ref):
        i = pl.program_id(1)
        krow = lax.broadcasted_iota(jnp.int32, (sub, t), 0)
        qcol = lax.broadcasted_iota(jnp.int32, (sub, t), 1)

        def dead_before(hh):
            head = pl.program_id(0) * nh + hh
            top = b_ref[head, 2 * nt] + b_ref[head, i]
            return lax.fori_loop(
                0, i, lambda jj, n: n + (top - b_ref[head, nt + jj] < FOX_DEAD).astype(jnp.int32), 0)

        j_lo = functools.reduce(jnp.minimum, [dead_before(hh) for hh in range(nh)])

        def tile(j, carry, diag):
            out = []
            for hh in range(nh):
                m, acc = carry[hh]
                qv, vj = q_ref[hh], v_ref[hh, j]
                los = [b * sub if diag else 0 for b in range(nsub)]
                sts = [_dot(k_ref[hh, pl.ds(pl.multiple_of(j * t + b * sub, sub), sub), :], qv[lo:, :], NT)
                       for b, lo in enumerate(los)]
                for b, lo in enumerate(los):
                    st = sts[b]
                    if diag:
                        st = jnp.where(krow[:, :t - lo] <= qcol[:, :t - lo], st, NEG)
                    m_old, acc_old = m[:, lo:], acc[:, lo:]
                    m2 = jnp.maximum(m_old, jnp.max(st, axis=0, keepdims=True))
                    p = jnp.exp(st - m2).astype(BF16)
                    acc2 = jnp.exp(m_old - m2) * acc_old + _dot(vj[:, b * sub:(b + 1) * sub], p)
                    m = m2 if lo == 0 else jnp.concatenate([m[:, :lo], m2], axis=1)
                    acc = acc2 if lo == 0 else jnp.concatenate([acc[:, :lo], acc2], axis=1)
                out.append((m, acc))
            return tuple(out)

        init = tuple((jnp.full((1, t), NEG, F32), jnp.zeros((w, t), F32)) for _ in range(nh))
        carry = lax.fori_loop(j_lo, i, lambda j, c: tile(j, c, False), init)
        outs = []
        for hh, (m, acc) in enumerate(tile(i, carry, True)):
            den = acc[B_HD:B_HD + 1, :]
            outs.append(acc[0:B_HD, :] / den)
            lse_ref[hh] = m + jnp.log(den)
        o_ref[...] = jnp.concatenate(outs, axis=0).T.astype(o_ref.dtype)

    return pl.pallas_call(
        body, name="fox_fwd", grid=(h // nh, nt),
        in_specs=[SMEM_SPEC,
                  pl.BlockSpec((nh, t, w), lambda hh, i: (hh, i, 0)),
                  pl.BlockSpec((nh, s, w), lambda hh, i: (hh, 0, 0)),
                  pl.BlockSpec((nh, nt, w, t), lambda hh, i: (hh, 0, 0, 0))],
        out_specs=[pl.BlockSpec((t, nh * B_HD), lambda hh, i: (i, hh)),
                   pl.BlockSpec((nh, 1, t), lambda hh, i: (hh, 0, i))],
        out_shape=[jax.ShapeDtypeStruct((s, h * B_HD), BF16), jax.ShapeDtypeStruct((h, 1, s), F32)],
        compiler_params=_params(("parallel", "parallel")),
    )(bounds, qf, kb, vt4)


def _fox_bwd(qf, dow, lse_row, delta_row, kb, kst4, vb, bounds, t):
    h, s, w = qf.shape
    nt = s // t
    nsub = t // FOX_SUB
    nh = FOX_HEADS_PER_STEP

    def body(b_ref, q_ref, do_ref, lse_ref, dl_ref, k_ref, kt_ref, v_ref, dqt_ref, dk_ref, dv_ref, dk_acc, dv_acc):
        j = pl.program_id(1)

        def alive_after(hh):
            head = pl.program_id(0) * nh + hh
            top = b_ref[head, 2 * nt] - b_ref[head, nt + j]
            return lax.fori_loop(
                j + 1, nt, lambda ii, n: n + (top + b_ref[head, ii] >= FOX_DEAD).astype(jnp.int32), 0)

        i_hi = j + 1 + functools.reduce(jnp.maximum, [alive_after(hh) for hh in range(nh)])

        @pl.when(j == 0)
        def _():
            dqt_ref[...] = jnp.zeros_like(dqt_ref)

        dk_acc[...] = jnp.zeros_like(dk_acc)
        dv_acc[...] = jnp.zeros_like(dv_acc)
        krow = lax.broadcasted_iota(jnp.int32, (FOX_SUB, t), 0)
        qcol = lax.broadcasted_iota(jnp.int32, (FOX_SUB, t), 1)
        subs = [slice(b * FOX_SUB, (b + 1) * FOX_SUB) for b in range(nsub)]

        def tile(i, diag):
            i0 = pl.multiple_of(i * t, t)
            for hh in range(nh):
                qi, doi = q_ref[hh, pl.ds(i0, t), :], do_ref[hh, pl.ds(i0, t), :]
                lse, dl = lse_ref[hh, i], dl_ref[hh, i]
                los = [b * FOX_SUB if diag else 0 for b in range(nsub)]
                sts = [_dot(k_ref[hh, rs, :], qi[lo:, :], NT) for rs, lo in zip(subs, los)]
                dps = [_dot(v_ref[hh, rs, :], doi[lo:, :], NT) for rs, lo in zip(subs, los)]
                dq = None
                for b, (rs, lo) in enumerate(zip(subs, los)):
                    st = sts[b] - lse[:, lo:]
                    if diag:
                        st = jnp.where(krow[:, :t - lo] <= qcol[:, :t - lo], st, NEG)
                    pt = jnp.exp(st)
                    dsb = (pt * (dps[b] - dl[:, lo:])).astype(BF16)
                    dv_acc[hh, rs, :] += _dot(pt.astype(BF16), doi[lo:, :])
                    dk_acc[hh, rs, :] += _dot(dsb, qi[lo:, :])
                    part = _dot(kt_ref[hh, :, rs], dsb)
                    if lo:
                        part = jnp.concatenate([jnp.zeros((w, lo), F32), part], axis=1)
                    dq = part if dq is None else dq + part
                dqt_ref[hh, i] += dq

        def step(i, carry):
            tile(i, False)
            return carry

        tile(j, True)
        lax.fori_loop(j + 1, i_hi, step, 0)
        dk_ref[...] = dk_acc[...] * (B_HD ** -0.5)
        dv_ref[...] = dv_acc[...]

    full = pl.BlockSpec((nh, s, w), lambda hh, j: (hh, 0, 0))
    rowst = pl.BlockSpec((nh, nt, 1, t), lambda hh, j: (hh, 0, 0, 0))
    tl = pl.BlockSpec((nh, t, w), lambda hh, j: (hh, j, 0))
    return pl.pallas_call(
        body, name="fox_bwd", grid=(h // nh, nt),
        in_specs=[SMEM_SPEC, full, full, rowst, rowst, tl,
                  pl.BlockSpec((nh, None, w, t), lambda hh, j: (hh, j, 0, 0)), tl],
        out_specs=[pl.BlockSpec((nh, nt, w, t), lambda hh, j: (hh, 0, 0, 0)), tl, tl],
        out_shape=[jax.ShapeDtypeStruct((h, nt, w, t), F32), jax.ShapeDtypeStruct((h, s, w), F32),
                   jax.ShapeDtypeStruct((h, s, w), F32)],
        scratch_shapes=[pltpu.VMEM((nh, t, w), F32), pltpu.VMEM((nh, t, w), F32)],
        compiler_params=_params(("parallel", "arbitrary")),
    )(bounds, qf, dow, lse_row, delta_row, kb, kst4, vb)


def _mem_fwd(u, mkv, tq=512):
    s = u.shape[0]
    scale = HD ** -0.5

    def body(q_ref, mk_ref, mv_ref, o_ref, lse_ref):
        lses = []
        for h in range(4):
            cs = slice(h * HD, (h + 1) * HD)
            sc = _dot(q_ref[:, cs], mk_ref[:, cs], NT) * scale
            m = jnp.max(sc, axis=-1, keepdims=True)
            p = jnp.exp(sc - m)
            den = jnp.sum(p, axis=-1, keepdims=True)
            o_ref[:, cs] = (_dot(p.astype(BF16), mv_ref[:, cs]) / den).astype(o_ref.dtype)
            lses.append(m + jnp.log(den))
        lse_ref[...] = _lane_pack(lses, (tq, HD))

    return pl.pallas_call(
        body, name="mem_fwd", grid=(s // tq,),
        in_specs=[pl.BlockSpec((tq, 512), lambda i: (i, C_QM // 512)),
                  pl.BlockSpec((N_MEM, 512), lambda i: (0, 0)),
                  pl.BlockSpec((N_MEM, 512), lambda i: (0, 1))],
        out_specs=[pl.BlockSpec((tq, 512), lambda i: (i, 0)), pl.BlockSpec((tq, HD), lambda i: (i, 0))],
        out_shape=[jax.ShapeDtypeStruct((s, 512), BF16), jax.ShapeDtypeStruct((s, HD), F32)],
        compiler_params=_params(("parallel",)),
    )(u, mkv, mkv)


def _mem_bwd(u, mkv, o, do, lse, tq=512):
    s = u.shape[0]
    scale = HD ** -0.5

    def body(q_ref, mk_ref, mv_ref, o_ref, do_ref, lse_ref, dq_ref, dmk_ref, dmv_ref):
        @pl.when(pl.program_id(0) == 0)
        def _():
            dmk_ref[...] = jnp.zeros_like(dmk_ref)
            dmv_ref[...] = jnp.zeros_like(dmv_ref)

        for h in range(4):
            cs = slice(h * HD, (h + 1) * HD)
            qv, dov = q_ref[:, cs], do_ref[:, cs]
            sc = _dot(qv, mk_ref[:, cs], NT) * scale
            p = jnp.exp(sc - lse_ref[:, h:h + 1])
            delta = jnp.sum(dov.astype(F32) * o_ref[:, cs].astype(F32), axis=-1, keepdims=True)
            ds = p * (_dot(dov, mv_ref[:, cs], NT) - delta)
            dsb = ds.astype(BF16)
            dq_ref[:, cs] = (_dot(dsb, mk_ref[:, cs]) * scale).astype(dq_ref.dtype)
            dmk_ref[:, cs] += _dot(dsb, qv, TN) * scale
            dmv_ref[:, cs] += _dot(p.astype(BF16), dov, TN)

    row = pl.BlockSpec((tq, 512), lambda i: (i, 0))
    acc = pl.BlockSpec((N_MEM, 512), lambda i: (0, 0))
    return pl.pallas_call(
        body, name="mem_bwd", grid=(s // tq,),
        in_specs=[pl.BlockSpec((tq, 512), lambda i: (i, C_QM // 512)),
                  pl.BlockSpec((N_MEM, 512), lambda i: (0, 0)),
                  pl.BlockSpec((N_MEM, 512), lambda i: (0, 1)),
                  row, row, pl.BlockSpec((tq, HD), lambda i: (i, 0))],
        out_specs=[row, acc, acc],
        out_shape=[jax.ShapeDtypeStruct((s, 512), BF16), jax.ShapeDtypeStruct((N_MEM, 512), F32),
                   jax.ShapeDtypeStruct((N_MEM, 512), F32)],
        compiler_params=_params(("arbitrary",)),
    )(u, mkv, mkv, o, do, lse)


def _local_step(x, mem, pos, target, g_pre, g_post, g_mem, w_main, w_fb, b_forget, b_merge,
                w_mem_kv, w_ba, w_bb, w_bm, w_out, exchange=None):
    s = x.shape[0]
    t_fox = min(512, s)
    nt = s // t_fox
    half = ROT_DIM // 2
    inv = ROPE_THETA ** (-jnp.arange(half, dtype=F32) / half)
    inv128 = jnp.concatenate([inv, inv, jnp.zeros((HD - ROT_DIM,), F32)]).reshape(1, HD)

    h = _rms_fwd("norm_pre", x, g_pre)
    u = _mm("proj_in", h, w_main, "nn", BF16, tm=4096)
    ufb = _mm("proj_fb", h, w_fb, "nn", F32)
    memn = _rms_fwd("norm_mem", mem, g_mem)
    mkv = _mm("proj_mem", memn, w_mem_kv, "nn", BF16)

    qkv = _rope_fwd(u, pos, inv128)
    views = [tuple(qkv[3 * g:3 * g + 3]) for g in range(3)]
    os_, lses = [], []
    for g, d in enumerate(DILATIONS):
        o_g, lse_g = _band_fwd("band_fwd%d" % g, *views[g], d)
        os_.append((o_g, d * A_GROUP, 0, d))
        lses.append((lse_g, d * HD, 0, d))

    def merge_a(o1, o2, o3, l1, l2, l3, za, *scr):
        o1, o2, o3 = [_from_class(o, scr, d) for o, d in zip((o1, o2, o3), DILATIONS)]
        l1, l2, l3 = [_from_class(lv, scr, d) for lv, d in zip((l1, l2, l3), DILATIONS)]
        ys, tots = [], []
        for hh in range(4):
            cs, hs = slice(hh * HD, (hh + 1) * HD), slice(hh, hh + 1)
            mx = jnp.maximum(jnp.maximum(l1[:, hs], l2[:, hs]), l3[:, hs])
            e1, e2, e3 = jnp.exp(l1[:, hs] - mx), jnp.exp(l2[:, hs] - mx), jnp.exp(l3[:, hs] - mx)
            den = e1 + e2 + e3
            ys.append((e1 * o1[:, cs] + e2 * o2[:, cs] + e3 * o3[:, cs]) / den)
            tots.append(mx + jnp.log(den))
        y = jnp.concatenate(ys, axis=1)
        zf = za.astype(F32)
        tot = _lane_pack(tots, l1.shape)
        return (y, y * (zf * _sig(zf))) + tuple(_to_class(tot, scr, d) for d in DILATIONS)

    res = _rows("merge_a", merge_a, os_ + lses + [(u, 512, C_ZA // 512)], [],
                [(512, BF16), (512, BF16)] + [(d * HD, F32, d) for d in DILATIONS], tm=ROPE_TM,
                scratch=_class_scratch(ROPE_TM))
    y_a, yg_a, lse_a = res[0], res[1], res[2:5]

    zrow = ufb[:, :B_HEADS].T
    c = _fox_prep(zrow, b_forget.reshape(B_HEADS, 1))
    qf, kb, kst4, vb, vt4 = _fox_pack(u, c.reshape(B_HEADS, s, 1), t_fox)
    bounds = _fox_bounds(qf, kb, c, t_fox)
    y_b, lse_b = _fox_fwd(qf, kb, vt4, bounds, t_fox)

    y_m, lse_m = _mem_fwd(u, mkv)

    def gate(y, z):
        zf = z.astype(F32)
        return (y.astype(F32) * (zf * _sig(zf)),)

    yg_b = _rows("gate_b", gate, [y_b, (u, 512, C_ZB // 512)], [], [(512, BF16)])[0]
    yg_m = _rows("gate_m", gate, [y_m, (u, 512, C_ZM // 512)], [], [(512, BF16)])[0]

    br_a = _mm("branch_a", yg_a, w_ba, "nn", BF16)
    br_b = _mm("branch_b", yg_b, w_bb, "nn", BF16)
    br_m = _mm("branch_m", yg_m, w_bm, "nn", BF16)
    gl = [(u, 1024, C_GL // 1024 + i) for i in range(3)]
    bm3 = b_merge.reshape(3, D_MODEL)

    def merge(g0, g1, g2, b0, b1, b2, bm):
        tot = 0.0
        for i, (gv, bv) in enumerate(((g0, b0), (g1, b1), (g2, b2))):
            tot = tot + _sig(gv.astype(F32) + bm[i:i + 1, :]) * bv.astype(F32)
        return (tot,)

    merged = _rows("merge_gates", merge, gl + [br_a, br_b, br_m], [bm3], [(D_MODEL, BF16)])[0]
    out = _mm("proj_out", merged, w_out, "nn", F32)

    def tail(xv, ov, tv, gv):
        r = lax.rsqrt(jnp.mean(ov * ov, axis=-1, keepdims=True) + EPS)
        n = ov * r
        err = xv + n * gv - tv
        dy = err * (1.0 / D_MODEL)
        dn = dy * gv
        dout = r * (dn - n * jnp.mean(dn * n, axis=-1, keepdims=True))
        return (dy, dout, jnp.sum(0.5 * err * err * (1.0 / D_MODEL), axis=0, keepdims=True),
                jnp.sum(dy * n, axis=0, keepdims=True))

    dy, dout, loss_lanes, g_post_grad = _rows(
        "tail", tail, [x, out, target], [g_post], [(D_MODEL, F32), (D_MODEL, BF16)],
        reds=[D_MODEL, D_MODEL], tm=256)

    dmerged = _mm("d_merged", dout, w_out, "nt", BF16)
    gw_out = _mm("g_w_out", merged, dout, "tn", F32)

    def merge_bwd(dm, g0, g1, g2, b0, b1, b2, bm):
        dmf = dm.astype(F32)
        dbs, dgs, sums = [], [], []
        for i, (gv, bv) in enumerate(((g0, b0), (g1, b1), (g2, b2))):
            sg = _sig(gv.astype(F32) + bm[i:i + 1, :])
            dbs.append(dmf * sg)
            dg = dmf * bv.astype(F32) * sg * (1.0 - sg)
            dgs.append(dg)
            sums.append(jnp.sum(dg, axis=0, keepdims=True))
        return tuple(dbs + dgs + sums)

    res = _rows("merge_bwd", merge_bwd, [dmerged] + gl + [br_a, br_b, br_m], [bm3],
                [(D_MODEL, BF16)] * 6, reds=[D_MODEL] * 3, tm=256)
    dbr, dgl, g_bmerge = res[0:3], res[3:6], jnp.concatenate(res[6:9], axis=1)

    dyg, gw_branch = [], []
    for nm, dbv, wv, ygv in (("a", dbr[0], w_ba, yg_a), ("b", dbr[1], w_bb, yg_b), ("m", dbr[2], w_bm, yg_m)):
        dyg.append(_mm("d_yg_" + nm, dbv, wv, "nt", BF16))
        gw_branch.append(_mm("g_w_branch_" + nm, ygv, dbv, "tn", F32))

    def gate_bwd(dg, y, z):
        dgf, yf, zf = dg.astype(F32), y.astype(F32), z.astype(F32)
        sg = _sig(zf)
        return dgf * (zf * sg), dgf * yf * (sg * (1.0 + zf * (1.0 - sg)))

    def gate_bwd_a(dg, y, z, *scr):
        dyv, dz = gate_bwd(dg, y, z)
        prod = dyv * y.astype(F32)
        dl = [jnp.sum(prod[:, hh * HD:(hh + 1) * HD], axis=-1, keepdims=True) for hh in range(4)]
        delta = _lane_pack(dl, (dg.shape[0], HD))
        return ((dz,) + tuple(_to_class(dyv, scr, d) for d in DILATIONS)
                + tuple(_to_class(delta, scr, d) for d in DILATIONS))

    res = _rows("gate_bwd_a", gate_bwd_a, [dyg[0], y_a, (u, 512, C_ZA // 512)], [],
                [(512, BF16)] + [(d * A_GROUP, BF16, d) for d in DILATIONS] + [(d * HD, F32, d) for d in DILATIONS],
                tm=ROPE_TM, scratch=_class_scratch(ROPE_TM))
    dz_a, dy_a, delta_a = res[0], res[1:4], res[4:7]
    dy_b, dz_b = _rows("gate_bwd_b", gate_bwd, [dyg[1], y_b, (u, 512, C_ZB // 512)], [],
                       [(512, BF16), (512, BF16)])
    dy_m, dz_m = _rows("gate_bwd_m", gate_bwd, [dyg[2], y_m, (u, 512, C_ZM // 512)], [],
                       [(512, BF16), (512, BF16)])

    dq_m, dmk, dmv = _mem_bwd(u, mkv, y_m, dy_m, lse_m)
    dmkv = jnp.concatenate([dmk, dmv], axis=1)
    gw_mem_kv = _mm("g_w_mem_kv", memn, dmkv, "tn", F32)
    dmemn = _mm("d_memn", dmkv, w_mem_kv, "nt", F32)

    def mem_gain_grad(mv, dv):
        r = lax.rsqrt(jnp.mean(mv * mv, axis=-1, keepdims=True) + EPS)
        return (jnp.sum(dv * mv * r, axis=0, keepdims=True),)

    g_mem_grad = _rows("g_norm_mem", mem_gain_grad, [mem, dmemn], [], [], reds=[D_MODEL], tm=N_MEM)[0]

    dow, delta_b = _fox_pack_bwd(dy_b, y_b, t_fox)
    dqt, dkw, dvw = _fox_bwd(qf, dow, lse_b.reshape(B_HEADS, nt, 1, t_fox), delta_b, kb, kst4, vb, bounds, t_fox)
    dqb, dkb, dvb, dc = _fox_unpack(dqt, dkw, dvw, t_fox)
    dzrow, g_bforget = _fox_prep_bwd(dc.reshape(B_HEADS, s), zrow, b_forget.reshape(B_HEADS, 1))
    dfb = jnp.zeros((s, HD), BF16).at[:, :B_HEADS].set(dzrow.T.astype(BF16))

    dqs, dks, dvs = [], [], []
    for g, d in enumerate(DILATIONS):
        qv, kv, vv = views[g]
        dqs.append(_band_dq("band_dq%d" % g, qv, kv, vv, dy_a[g], lse_a[g], delta_a[g], d))
        dk_g, dv_g = _band_dkv("band_dkv%d" % g, qv, kv, vv, dy_a[g], lse_a[g], delta_a[g], d)
        dks.append(dk_g)
        dvs.append(dv_g)
    dqa, dka, dva = _rope_bwd(dqs, dks, dvs, pos, inv128)

    du = jnp.concatenate(
        [dqa, dka, dva, dz_a, dqb, dkb, dvb,
                            dz_b, dq_m, dz_m] + list(dgl), axis=1)

    gw_main = _mm("g_w_main", h.T, du, "nn", F32, tn=1408, tk=2048)
    gw_fb = _mm("g_w_fb", h, dfb, "tn", F32)
    gw_in = jnp.concatenate([gw_main[:, :FB_ORIG], gw_fb[:, :B_HEADS], gw_main[:, FB_ORIG:]], axis=1)
    grads = dict(norm_post_g=g_post_grad, norm_mem_g=g_mem_grad, w_in=gw_in,
                 b_forget=g_bforget.reshape(1, B_HEADS), b_merge=g_bmerge, w_mem_kv=gw_mem_kv,
                 w_branch_a=gw_branch[0], w_branch_b=gw_branch[1], w_branch_m=gw_branch[2], w_out=gw_out)
    side = exchange(grads) if exchange else None
    dh_main = _mm("d_h", du, w_main, "nt", F32, tm=2048, tk=1408, side=side)
    landed = None
    if side:
        dh_main, landed = dh_main[0], dh_main[1:]
    dh_fb = _mm("d_h_fb", dfb, w_fb, "nt", F32)

    def pre_bwd(xv, d1, d2, dyv, gv):
        r = lax.rsqrt(jnp.mean(xv * xv, axis=-1, keepdims=True) + EPS)
        n = xv * r
        dhv = d1 + d2
        dn = dhv * gv
        dx = r * (dn - n * jnp.mean(dn * n, axis=-1, keepdims=True))
        return dyv + dx, jnp.sum(dhv * n, axis=0, keepdims=True)

    grad_x, g_pre_grad = _rows("norm_pre_bwd", pre_bwd, [x, dh_main, dh_fb, dy], [g_pre],
                               [(D_MODEL, F32)], reds=[D_MODEL], tm=256)

    grads["norm_pre_g"] = g_pre_grad
    return loss_lanes, grad_x, grads, landed


HBM_SPEC = pl.BlockSpec(memory_space=pltpu.HBM)


def _place():
    x, y, c = lax.axis_index("x"), lax.axis_index("y"), lax.axis_index("c")
    chips = [(1 - x, y), (x, 1 - y), (1 - x, 1 - y)]
    return x, y, c, 2 * x + y, chips


N_CHUNKS = 4


def _units(parts, row_axis):
    units = []
    for i, a in enumerate(parts):
        ch = a.shape[row_axis] // N_CHUNKS
        units += [(i, pl.ds(k * ch, ch)) for k in range(N_CHUNKS)]
    return units


def _gather_weights(parts):
    n = len(parts)
    units = _units(parts, 1)
    nu = len(units)
    via_y = [(u % N_CHUNKS) < N_CHUNKS // 2 for u in range(nu)]

    def body(*refs):
        srcs, outs = refs[:n], refs[n:2 * n]
        send_sems, recv_sems = refs[2 * n:]
        x, y, c, p, _ = _place()
        me, sib = (x, y, c), (x, y, 1 - c)
        xn, yn, dg = (1 - x, y), (x, 1 - y), (1 - x, 1 - y)

        def cp(u, k, chip, half, to, from_src=False):
            i, rs = units[u]
            dst = outs[i].at[2 * chip[0] + chip[1], half, rs]
            return pltpu.make_async_remote_copy(
                src_ref=srcs[i].at[half, rs] if from_src else dst, dst_ref=dst, send_sem=send_sems.at[u, k],
                recv_sem=recv_sems.at[u, k], device_id=to, device_id_type=MESH)

        sent = []

        def go(copy):
            copy.start()
            sent.append(copy)

        for u in range(nu):
            go(cp(u, 0, (x, y), c, (*xn, c), from_src=True))
            go(cp(u, 1, (x, y), c, (*yn, c), from_src=True))
        for u in range(nu):
            cp(u, 0, xn, c, me).wait_recv()
            go(cp(u, 4, xn, c, sib))
            if via_y[u]:
                go(cp(u, 2, xn, c, (*yn, c)))
            cp(u, 1, yn, c, me).wait_recv()
            go(cp(u, 5, yn, c, sib))
            if not via_y[u]:
                go(cp(u, 3, yn, c, (*xn, c)))
        for u in range(nu):
            cp(u, 2 if via_y[u] else 3, dg, c, me).wait_recv()
            go(cp(u, 6, dg, c, sib))
        for u in range(nu):
            for k, chip in ((4, xn), (5, yn), (6, dg)):
                cp(u, k, chip, 1 - c, me).wait_recv()
        for copy in sent:
            copy.wait_send()

    return pl.pallas_call(
        body, name="gather_weights", in_specs=[HBM_SPEC] * n, out_specs=[HBM_SPEC] * n,
        out_shape=[jax.ShapeDtypeStruct((N_CHIPS,) + a.shape, a.dtype) for a in parts],
        scratch_shapes=[pltpu.SemaphoreType.DMA((nu, 7)), pltpu.SemaphoreType.DMA((nu, 7))],
    )(*parts)


def _swap_with_sibling(parts):
    n = len(parts)
    units = _units(parts, 2)

    def body(*refs):
        srcs, outs = refs[:n], refs[n:2 * n]
        send_sems, recv_sems = refs[2 * n:]
        x, y, c, _, _ = _place()
        cps = [pltpu.make_async_remote_copy(
            src_ref=srcs[i].at[q, 1 - c, rs], dst_ref=outs[i].at[q, rs], send_sem=send_sems.at[u, q],
            recv_sem=recv_sems.at[u, q], device_id=(x, y, 1 - c), device_id_type=MESH)
            for q in range(N_CHIPS) for u, (i, rs) in enumerate(units)]
        for cpy in cps:
            cpy.start()
        for cpy in cps:
            cpy.wait()

    return pl.pallas_call(
        body, name="swap_with_sibling", in_specs=[HBM_SPEC] * n, out_specs=[HBM_SPEC] * n,
        out_shape=[jax.ShapeDtypeStruct(a.shape[:1] + a.shape[2:], a.dtype) for a in parts],
        scratch_shapes=[pltpu.SemaphoreType.DMA((len(units), N_CHIPS)),
                        pltpu.SemaphoreType.DMA((len(units), N_CHIPS))],
    )(*parts)


def _scatter_to_owners(parts):
    n = len(parts)
    units = _units(parts, 1)

    def copies(srcs, outs, send_sems, recv_sems, incoming):
        x, y, c, p, chips = _place()
        return [pltpu.make_async_remote_copy(
            src_ref=srcs[i].at[2 * cx + cy, rs], dst_ref=outs[i].at[(2 * cx + cy) if incoming else p, rs],
            send_sem=send_sems.at[u, j], recv_sem=recv_sems.at[u, j], device_id=(cx, cy, c), device_id_type=MESH)
            for u, (i, rs) in enumerate(units) for j, (cx, cy) in enumerate(chips)]

    def start(ins, outs, scratch):
        for cpy in copies(ins, outs, *scratch, incoming=False):
            cpy.start()

    def wait(ins, outs, scratch):
        for cpy in copies(ins, outs, *scratch, incoming=True):
            cpy.wait_recv()
        for cpy in copies(ins, outs, *scratch, incoming=False):
            cpy.wait_send()

    return dict(ins=list(parts), outs=[jax.ShapeDtypeStruct(a.shape, a.dtype) for a in parts],
                scratch=[pltpu.SemaphoreType.DMA((len(units), 3)), pltpu.SemaphoreType.DMA((len(units), 3))],
                start=start, wait=wait)


def _share_with_sibling(parts):
    n = len(parts)
    units = _units(parts, 1)

    def body(*refs):
        srcs, outs = refs[:n], refs[n:2 * n]
        send_sems, recv_sems = refs[2 * n:]
        x, y, c, _, _ = _place()
        sends = [pltpu.make_async_remote_copy(
            src_ref=srcs[i].at[0, rs], dst_ref=outs[i].at[c, rs], send_sem=send_sems.at[u],
            recv_sem=recv_sems.at[u], device_id=(x, y, 1 - c), device_id_type=MESH)
            for u, (i, rs) in enumerate(units)]
        for cpy in sends:
            cpy.start()
        for u, (i, rs) in enumerate(units):
            pltpu.make_async_remote_copy(
                src_ref=srcs[i].at[0, rs], dst_ref=outs[i].at[1 - c, rs], send_sem=send_sems.at[u],
                recv_sem=recv_sems.at[u], device_id=(x, y, 1 - c), device_id_type=MESH).wait_recv()
        for cpy in sends:
            cpy.wait_send()

    return pl.pallas_call(
        body, name="share_with_sibling", in_specs=[HBM_SPEC] * n, out_specs=[HBM_SPEC] * n,
        out_shape=[jax.ShapeDtypeStruct((2,) + a.shape[1:], a.dtype) for a in parts],
        scratch_shapes=[pltpu.SemaphoreType.DMA((len(units),)), pltpu.SemaphoreType.DMA((len(units),))],
    )(*parts)


def _sum_small(v):
    def body(v_ref, out_ref, buf, send_sems, recv_sems):
        x, y, c, _, _ = _place()
        me = 4 * x + 2 * y + c
        buf[me] = v_ref[...]
        flips = [(dx, dy, dc) for dx in (0, 1) for dy in (0, 1) for dc in (0, 1)][1:]
        sends = []
        for k, (dx, dy, dc) in enumerate(flips):
            cpy = pltpu.make_async_remote_copy(
                src_ref=v_ref, dst_ref=buf.at[me], send_sem=send_sems.at[k], recv_sem=recv_sems.at[k],
                device_id=((x + dx) % 2, (y + dy) % 2, (c + dc) % 2), device_id_type=MESH)
            cpy.start()
            sends.append(cpy)
        for k, (dx, dy, dc) in enumerate(flips):
            px, py, pc = (x + dx) % 2, (y + dy) % 2, (c + dc) % 2
            pltpu.make_async_remote_copy(
                src_ref=v_ref, dst_ref=buf.at[4 * px + 2 * py + pc], send_sem=send_sems.at[k],
                recv_sem=recv_sems.at[k], device_id=(px, py, pc), device_id_type=MESH).wait_recv()
        for cpy in sends:
            cpy.wait_send()
        tot = buf[0]
        for i in range(1, N_DEV):
            tot = tot + buf[i]
        out_ref[...] = tot

    return pl.pallas_call(
        body, name="sum_small", out_shape=jax.ShapeDtypeStruct(v.shape, v.dtype),
        in_specs=[pl.BlockSpec(memory_space=pltpu.VMEM)], out_specs=pl.BlockSpec(memory_space=pltpu.VMEM),
        scratch_shapes=[pltpu.VMEM((N_DEV,) + v.shape, v.dtype), pltpu.SemaphoreType.DMA((N_DEV - 1,)),
                        pltpu.SemaphoreType.DMA((N_DEV - 1,))],
    )(v)


def _add_chips(name, landed, pair, chip):
    nq, r, w = landed.shape
    tr = 64

    def body(chip_ref, *refs):
        own = refs[nq][...].astype(F32)
        tot = None
        for q in range(nq):
            term = jnp.where(chip_ref[0] == q, own, refs[q][...].astype(F32))
            tot = term if tot is None else tot + term
        refs[nq + 1][...] = tot

    specs = [pl.BlockSpec((None, tr, w), functools.partial(lambda j, chip_ref, q: (q, j, 0), q=q)) for q in range(nq)]
    specs.append(pl.BlockSpec((None, tr, w), lambda j, chip_ref: (chip_ref[0], j, 0)))
    grid_spec = pltpu.PrefetchScalarGridSpec(
        num_scalar_prefetch=1, grid=(r // tr,), in_specs=specs,
        out_specs=pl.BlockSpec((None, tr, w), lambda j, chip_ref: (0, j, 0)))
    return pl.pallas_call(
        body, name=name, grid_spec=grid_spec, out_shape=jax.ShapeDtypeStruct((1, r, w), F32),
        compiler_params=_params(("parallel",)),
    )(jnp.reshape(chip, (1,)).astype(jnp.int32), *([landed] * nq), pair)


def _add_pair(name, halves, got, c):
    nq, _, r, w = halves.shape
    tr = 64

    def body(c_ref, a_ref, b_ref, o_ref):
        o_ref[...] = (a_ref[...] + b_ref[...]).astype(o_ref.dtype)

    grid_spec = pltpu.PrefetchScalarGridSpec(
        num_scalar_prefetch=1, grid=(nq, r // tr),
        in_specs=[pl.BlockSpec((None, None, tr, w), lambda i, j, c_ref: (i, c_ref[0], j, 0)),
                  pl.BlockSpec((None, tr, w), lambda i, j, c_ref: (i, j, 0))],
        out_specs=pl.BlockSpec((None, tr, w), lambda i, j, c_ref: (i, j, 0)))
    return pl.pallas_call(
        body, name=name, grid_spec=grid_spec, out_shape=jax.ShapeDtypeStruct((nq, r, w), BF16),
        compiler_params=_params(("parallel", "parallel")),
    )(jnp.reshape(c, (1,)).astype(jnp.int32), halves, got)


def _adamw(name, w, g, m, v, tm):
    def fn(wv, gv, mv, vv):
        m2 = ADAM_B1 * mv + (1.0 - ADAM_B1) * gv
        v2 = ADAM_B2 * vv + (1.0 - ADAM_B2) * (gv * gv)
        m_hat = m2 / (1.0 - ADAM_B1 ** ADAM_STEP)
        v_hat = v2 / (1.0 - ADAM_B2 ** ADAM_STEP)
        return -ADAM_LR * (m_hat / (jnp.sqrt(v_hat) + ADAM_EPS) + ADAM_WD * wv), m2, v2
    c = w.shape[1]
    return _rows(name, fn, [w, g, m, v], [], [(c, F32)] * 3, tm=tm)


REST_ROWS = 256 + 3 * 128 + 256
REST_SPLITS = (("w_mem_kv", 0, 256), ("w_branch_a", 256, 128), ("w_branch_b", 384, 128),
               ("w_branch_m", 512, 128), ("w_out", 640, 256))


def _rest_pack(t):
    return jnp.concatenate([t[n].reshape(rows, D_MODEL) for n, _, rows in REST_SPLITS], axis=0)


def _rest_unpack(a, shapes):
    return {n: a[r0:r0 + rows].reshape(shapes[n]) for n, r0, rows in REST_SPLITS}


def _small_pack(pre, post, memg, bforget, bmerge):
    pad = jnp.zeros((1, D_MODEL - B_HEADS), F32)
    return jnp.concatenate([pre, post, memg, bmerge.reshape(3, D_MODEL),
                            jnp.concatenate([bforget, pad], axis=1), jnp.zeros((1, D_MODEL), F32)], axis=0)


def _small_unpack(s8):
    return dict(norm_pre_g=s8[0:1], norm_post_g=s8[1:2], norm_mem_g=s8[2:3],
                b_merge=s8[3:6].reshape(1, 3 * D_MODEL), b_forget=s8[6:7, :B_HEADS])


WEIGHTS = ("norm_pre_g", "norm_post_g", "norm_mem_g", "w_in", "b_forget", "b_merge", "w_mem_kv",
           "w_branch_a", "w_branch_b", "w_branch_m", "w_out")
SMALL = ("norm_pre_g", "norm_post_g", "norm_mem_g", "b_forget", "b_merge")


def kernel(x, mem, positions, norm_pre_g, norm_post_g, norm_mem_g, w_in, b_forget, b_merge, w_mem_kv, w_branch_a, w_branch_b, w_branch_m, w_out, loss_target, m_norm_pre_g, m_norm_post_g, m_norm_mem_g, m_w_in, m_b_forget, m_b_merge, m_w_mem_kv, m_w_branch_a, m_w_branch_b, m_w_branch_m, m_w_out, v_norm_pre_g, v_norm_post_g, v_norm_mem_g, v_w_in, v_b_forget, v_b_merge, v_w_mem_kv, v_w_branch_a, v_w_branch_b, v_w_branch_m, v_w_out):
    w = dict(norm_pre_g=norm_pre_g, norm_post_g=norm_post_g, norm_mem_g=norm_mem_g, w_in=w_in[0],
             b_forget=b_forget, b_merge=b_merge, w_mem_kv=w_mem_kv[0], w_branch_a=w_branch_a[0],
             w_branch_b=w_branch_b[0], w_branch_m=w_branch_m[0], w_out=w_out[0])
    mo = dict(norm_pre_g=m_norm_pre_g, norm_post_g=m_norm_post_g, norm_mem_g=m_norm_mem_g, w_in=m_w_in[0],
              b_forget=m_b_forget, b_merge=m_b_merge, w_mem_kv=m_w_mem_kv[0], w_branch_a=m_w_branch_a[0],
              w_branch_b=m_w_branch_b[0], w_branch_m=m_w_branch_m[0], w_out=m_w_out[0])
    vo = dict(norm_pre_g=v_norm_pre_g, norm_post_g=v_norm_post_g, norm_mem_g=v_norm_mem_g, w_in=v_w_in[0],
              b_forget=v_b_forget, b_merge=v_b_merge, w_mem_kv=v_w_mem_kv[0], w_branch_a=v_w_branch_a[0],
              w_branch_b=v_w_branch_b[0], w_branch_m=v_w_branch_m[0], w_out=v_w_out[0])
    s = x.shape[1]
    c = lax.axis_index("c")

    chip = 2 * lax.axis_index("x") + lax.axis_index("y")

    def put(whole, own, slot):
        return lax.dynamic_update_index_in_dim(whole, own.astype(whole.dtype), slot, 0)

    own_w = [w["w_in"].astype(BF16).reshape(2, D_MODEL // 2, SHARD_COLS),
             _rest_pack(w).astype(BF16).reshape(2, REST_ROWS // 2, D_MODEL)]
    all_in, all_rest = _gather_weights(own_w)
    all_in = all_in.reshape(N_CHIPS, D_MODEL, SHARD_COLS)
    own_in, own_rest = own_w[0].reshape(D_MODEL, SHARD_COLS), own_w[1].reshape(REST_ROWS, D_MODEL)
    w_in_f = jnp.concatenate([jnp.where(chip == p, own_in, all_in[p]) for p in range(N_CHIPS)], axis=1)
    all_rest = all_rest.reshape(N_CHIPS, REST_ROWS, D_MODEL)
    all_rest = jnp.stack([jnp.where(chip == p, own_rest, all_rest[p]) for p in range(N_CHIPS)])
    w_kv_f = all_rest[:, 0:256].reshape(D_MODEL, D_MODEL)
    w_br_f = [all_rest[:, 256 + 128 * i:384 + 128 * i].reshape(N_CHIPS, 512, 256).transpose(1, 0, 2)
              .reshape(512, D_MODEL) for i in range(3)]
    w_out_f = all_rest[:, 640:896].reshape(D_MODEL, D_MODEL)
    w_main = jnp.concatenate([w_in_f[:, :FB_ORIG], w_in_f[:, FB_ORIG + B_HEADS:]], axis=1)
    w_fb = jnp.concatenate([w_in_f[:, FB_ORIG:FB_ORIG + B_HEADS], jnp.zeros((D_MODEL, HD - B_HEADS), BF16)], axis=1)

    pair = []

    def exchange(g):
        def per_chip(name, p):
            a = g[name]
            if name in ("w_mem_kv", "w_out"):
                return a[256 * p:256 * (p + 1)]
            return a[:, 256 * p:256 * (p + 1)]

        in4 = jnp.stack([g["w_in"][:, SHARD_COLS * p:SHARD_COLS * (p + 1)] for p in range(N_CHIPS)])
        rest4 = jnp.stack([_rest_pack({n: per_chip(n, p) for n, _, _ in REST_SPLITS}) for p in range(N_CHIPS)])
        halves = [in4.reshape(N_CHIPS, 2, D_MODEL // 2, SHARD_COLS),
                  rest4.reshape(N_CHIPS, 2, REST_ROWS // 2, D_MODEL)]
        got = _swap_with_sibling(halves)
        pair.extend(_add_pair("add_pair_%d" % i, halves[i], got[i], c) for i in range(2))
        return _scatter_to_owners(pair)

    loss_lanes, grad_x, g, landed = _local_step(
        x[0], mem[0], positions.reshape(s, 1), loss_target[0], norm_pre_g, norm_post_g, norm_mem_g,
        w_main, w_fb, b_forget, b_merge, w_kv_f, w_br_f[0], w_br_f[1], w_br_f[2], w_out_f, exchange)
    loss = lax.psum(jnp.sum(loss_lanes), ("x", "y", "c"))
    half = [_add_chips("add_chips_%d" % i, landed[i], pair[i], chip) for i in range(2)]
    red_in, red_rest = [put(a, o[0], c) for a, o in zip(_share_with_sibling(half), half)]
    gs = {"w_in": red_in.reshape(D_MODEL, SHARD_COLS)}
    gs.update(_rest_unpack(red_rest.reshape(REST_ROWS, D_MODEL), {n: w[n].shape for n, _, _ in REST_SPLITS}))
    gs.update(_small_unpack(_sum_small(_small_pack(
        g["norm_pre_g"], g["norm_post_g"], g["norm_mem_g"], g["b_forget"], g["b_merge"]))))

    delta, new_m, new_v = {}, {}, {}
    for n, tm in (("w_in", 128), ("w_mem_kv", 256), ("w_branch_a", 512), ("w_branch_b", 512),
                  ("w_branch_m", 512), ("w_out", 256)):
        d_, m_, v_ = _adamw("adamw_" + n, w[n], gs[n], mo[n], vo[n], tm)
        delta[n], new_m[n], new_v[n] = d_[None], m_[None], v_[None]
        gs[n] = gs[n][None]
    packs = [_small_pack(*[t[n] for n in ("norm_pre_g", "norm_post_g", "norm_mem_g", "b_forget", "b_merge")])
             for t in (w, gs, mo, vo)]
    for res, store in zip(_adamw("adamw_small", *packs, 8), (delta, new_m, new_v)):
        store.update(_small_unpack(res))

    return (loss, grad_x[None], *[gs[n] for n in WEIGHTS], *[delta[n] for n in WEIGHTS],
            *[new_m[n] for n in WEIGHTS], *[new_v[n] for n in WEIGHTS])
```

```python
import functools

import jax
import jax.numpy as jnp
from jax import lax
from jax.experimental import pallas as pl
from jax.experimental.pallas import tpu as pltpu

F32 = jnp.float32
BF16 = jnp.bfloat16
MESH = pl.DeviceIdType.MESH

D_MODEL = 1024
N_MEM = 256
EPS = 1e-6
NEG = -1e30
ROPE_THETA = 500000.0
ROT_DIM = 32
HD = 128
A_GROUP = 512
DILATIONS = (1, 4, 16)
BAND = 128
B_HEADS = 8
B_HD = 64
N_CHIPS = 4
N_DEV = 8

C_QA, C_KA, C_VA, C_ZA = 0, 1536, 3072, 4608
C_QB, C_KB, C_VB, C_ZB = 5120, 5632, 6144, 6656
C_QM, C_ZM, C_GL = 7168, 7680, 8192
FB_ORIG = 6656
IN_COLS = 11272
SHARD_COLS = IN_COLS // N_CHIPS

ADAM_LR, ADAM_B1, ADAM_B2, ADAM_EPS, ADAM_WD, ADAM_STEP = 0.001, 0.9, 0.999, 1e-08, 0.01, 10

VMEM_LIMIT_V7X = 56 * 1024 * 1024

NT = (((1,), (1,)), ((), ()))
NN = (((1,), (0,)), ((), ()))
TN = (((0,), (0,)), ((), ()))


def _params(sem):
    return pltpu.CompilerParams(dimension_semantics=sem, vmem_limit_bytes=VMEM_LIMIT_V7X)


def _dot(a, b, dn=NN):
    return lax.dot_general(a, b, dn, preferred_element_type=F32)


def _sig(z):
    return 1.0 / (1.0 + jnp.exp(-z))


def _rows(name, fn, row_ins, bc_ins, outs, reds=(), tm=512, scratch=()):
    arrs, specs = [], []
    s = None
    for r in row_ins:
        arr, w, cb, d = (tuple(r) + (1,))[:4] if isinstance(r, tuple) else (r, r.shape[1], 0, 1)
        s = arr.shape[0] * d if s is None else s
        arrs.append(arr)
        specs.append((w, cb, d))
    tm = min(tm, s)
    specs = [pl.BlockSpec((tm // d, w), functools.partial(lambda i, cb: (i, cb), cb=cb)) for w, cb, d in specs]
    for b in bc_ins:
        arrs.append(b)
        specs.append(pl.BlockSpec(b.shape, lambda i: (0, 0)))
    outs = [(tuple(o) + (1,))[:3] for o in outs]
    n_in, n_out = len(arrs), len(outs)

    def body(*refs):
        n_ref = n_in + n_out + len(reds)
        vals = fn(*[r[...] for r in refs[:n_in]], *refs[n_ref:])
        if not isinstance(vals, (tuple, list)):
            vals = (vals,)
        for r, v in zip(refs[n_in:n_in + n_out], vals[:n_out]):
            r[...] = v.astype(r.dtype)
        if reds:
            red_refs = refs[n_in + n_out:n_ref]

            @pl.when(pl.program_id(0) == 0)
            def _():
                for r in red_refs:
                    r[...] = jnp.zeros_like(r)

            for r, v in zip(red_refs, vals[n_out:]):
                r[...] += v

    out_shape = [jax.ShapeDtypeStruct((s // d, c), dt) for c, dt, d in outs]
    out_shape += [jax.ShapeDtypeStruct((1, c), F32) for c in reds]
    out_specs = [pl.BlockSpec((tm // d, c), lambda i: (i, 0)) for c, _, d in outs]
    out_specs += [pl.BlockSpec((1, c), lambda i: (0, 0)) for c in reds]
    res = pl.pallas_call(
        body, name=name, grid=(s // tm,), in_specs=specs, out_specs=out_specs, out_shape=out_shape,
        scratch_shapes=list(scratch),
        compiler_params=_params(("arbitrary",) if reds else ("parallel",)),
    )(*arrs)
    return res


def _to_class(x, scr, d):
    if d == 1:
        return x.astype(F32)
    tm, c = x.shape
    for g in range(c // 128):
        scr[g][...] = x[:, g * 128:(g + 1) * 128].astype(F32)
    return jnp.concatenate([scr[g][pl.ds(r, tm // d, stride=d), :] for r in range(d) for g in range(c // 128)],
                           axis=1)


def _from_class(x, scr, d):
    if d == 1:
        return x.astype(F32)
    n, dc = x.shape
    c = dc // d
    for r in range(d):
        for g in range(c // 128):
            scr[g][pl.ds(r, n, stride=d), :] = x[:, r * c + g * 128:r * c + (g + 1) * 128].astype(F32)
    return jnp.concatenate([scr[g][...] for g in range(c // 128)], axis=1)


def _mm(name, a, b, mode, out_dtype, tm=1024, tn=1024, tk=1024, side=None):
    if mode == "nn":
        (m, k), (_, n) = a.shape, b.shape
    elif mode == "nt":
        (m, k), (n, _) = a.shape, b.shape
    else:
        (k, m), (_, n) = a.shape, b.shape
    tm, tn, tk = min(tm, m), min(tn, n), min(tk, k)
    nk = k // tk
    grid = (m // tm, n // tn, nk)
    dn = {"nn": NN, "nt": NT, "tn": TN}[mode]
    n_si = len(side["ins"]) if side else 0
    n_so = len(side["outs"]) if side else 0
    n_acc = 1 if nk > 1 else 0

    def body(*refs):
        a_ref, b_ref = refs[:2]
        side_in, o_ref = refs[2:2 + n_si], refs[2 + n_si]
        side_out = refs[3 + n_si:3 + n_si + n_so]
        acc = refs[3 + n_si + n_so:3 + n_si + n_so + n_acc]
        side_scratch = refs[3 + n_si + n_so + n_acc:]
        step = (pl.program_id(0) * grid[1] + pl.program_id(1)) * grid[2] + pl.program_id(2)
        if side:
            @pl.when(step == 0)
            def _():
                side["start"](side_in, side_out, side_scratch)

        part = _dot(a_ref[...].astype(BF16), b_ref[...].astype(BF16), dn)
        if nk == 1:
            o_ref[...] = part.astype(o_ref.dtype)
        else:
            kk = pl.program_id(2)

            @pl.when(kk == 0)
            def _():
                acc[0][...] = part

            @pl.when(kk > 0)
            def _():
                acc[0][...] += part

            @pl.when(kk == nk - 1)
            def _():
                o_ref[...] = acc[0][...].astype(o_ref.dtype)

        if side:
            @pl.when(step == grid[0] * grid[1] * grid[2] - 1)
            def _():
                side["wait"](side_in, side_out, side_scratch)

    a_spec = (pl.BlockSpec((tk, tm), lambda i, j, kk: (kk, i)) if mode == "tn"
              else pl.BlockSpec((tm, tk), lambda i, j, kk: (i, kk)))
    b_spec = (pl.BlockSpec((tn, tk), lambda i, j, kk: (j, kk)) if mode == "nt"
              else pl.BlockSpec((tk, tn), lambda i, j, kk: (kk, j)))
    o_spec = pl.BlockSpec((tm, tn), lambda i, j, kk: (i, j))
    o_shape = jax.ShapeDtypeStruct((m, n), out_dtype)
    acc_scratch = [pltpu.VMEM((tm, tn), F32)] * n_acc
    if not side:
        return pl.pallas_call(
            body, name=name, grid=grid, in_specs=[a_spec, b_spec], out_specs=o_spec, out_shape=o_shape,
            scratch_shapes=acc_scratch, compiler_params=_params(("parallel", "parallel", "arbitrary")),
        )(a, b)
    return pl.pallas_call(
        body, name=name, grid=grid, in_specs=[a_spec, b_spec] + [HBM_SPEC] * n_si,
        out_specs=[o_spec] + [HBM_SPEC] * n_so, out_shape=[o_shape] + side["outs"],
        scratch_shapes=acc_scratch + side["scratch"],
        compiler_params=_params(("arbitrary", "arbitrary", "arbitrary")),
    )(a, b, *side["ins"])


def _rms_fwd(name, x, g):
    def fn(xv, gv):
        r = lax.rsqrt(jnp.mean(xv * xv, axis=-1, keepdims=True) + EPS)
        return (xv * r * gv,)
    return _rows(name, fn, [x], [g], [(x.shape[1], BF16)], tm=min(512, x.shape[0]))[0]


def _rope_tables(pos, inv):
    ang = pos.astype(F32) * inv
    lane = lax.broadcasted_iota(jnp.int32, ang.shape, 1)
    c = jnp.where(lane < ROT_DIM, jnp.cos(ang), 1.0)
    sn = jnp.sin(ang)
    sg = jnp.where(lane < ROT_DIM // 2, -sn, jnp.where(lane < ROT_DIM, sn, 0.0))
    return c, sg, lane


def _rope_apply(x, c, sg, lane):
    outs = []
    for h in range(x.shape[1] // HD):
        xh = x[:, h * HD:(h + 1) * HD].astype(F32)
        swap = jnp.where(lane < ROT_DIM // 2, pltpu.roll(xh, HD - ROT_DIM // 2, 1),
                         pltpu.roll(xh, ROT_DIM // 2, 1))
        outs.append(xh * c + swap * sg)
    return jnp.concatenate(outs, axis=1)


ROPE_TM = 256


def _class_scratch(tm):
    return [pltpu.VMEM((tm, 128), F32) for _ in range(A_GROUP // 128)]


def _rope_fwd(u, pos, inv):
    def fn(q, k, v, p, iv, *scr):
        c, sg, lane = _rope_tables(p, iv)
        qr, kr = _rope_apply(q, c, sg, lane), _rope_apply(k, c, sg, lane)
        outs = []
        for g, d in enumerate(DILATIONS):
            gs = slice(g * A_GROUP, (g + 1) * A_GROUP)
            outs += [_to_class(qr[:, gs], scr, d), _to_class(kr[:, gs], scr, d), _to_class(v[:, gs], scr, d)]
        return tuple(outs)

    outs = [(d * A_GROUP, BF16, d) for d in DILATIONS for _ in range(3)]
    qkv = [(u, 3 * A_GROUP, c0 // (3 * A_GROUP)) for c0 in (C_QA, C_KA, C_VA)]
    return _rows("rope_fwd", fn, qkv + [pos], [inv], outs, tm=ROPE_TM,
                 scratch=_class_scratch(ROPE_TM))


def _rope_bwd(dqs, dks, dvs, pos, inv):
    def fn(*args):
        grads, p, iv, scr = args[:9], args[9], args[10], args[11:]
        c, sg, lane = _rope_tables(p, iv)
        tok = [jnp.concatenate([_from_class(grads[3 * k + g], scr, d) for g, d in enumerate(DILATIONS)], axis=1)
               for k in range(3)]
        return _rope_apply(tok[0], c, -sg, lane), _rope_apply(tok[1], c, -sg, lane), tok[2]

    ins = [(a, a.shape[1], 0, d) for grp in (dqs, dks, dvs) for a, d in zip(grp, DILATIONS)]
    return _rows("rope_bwd", fn, ins + [pos], [inv], [(1536, BF16)] * 3, tm=ROPE_TM,
                 scratch=_class_scratch(ROPE_TM))


def _lane_pack(cols, like):
    lane = lax.broadcasted_iota(jnp.int32, like, 1)
    out = jnp.zeros(like, F32)
    for h, cvec in enumerate(cols):
        out = jnp.where(lane == h, cvec, out)
    return out


def _band_specs(l, d, tq):
    nsb = tq // BAND
    nblk = l // BAND
    cur = pl.BlockSpec((tq, A_GROUP), lambda r, i: (i, r))
    prev = pl.BlockSpec((BAND, A_GROUP), lambda r, i: (jnp.maximum(i * nsb - 1, 0), r))
    nxt = pl.BlockSpec((BAND, A_GROUP), lambda r, i: (jnp.minimum((i + 1) * nsb, nblk - 1), r))
    st_cur = pl.BlockSpec((tq, HD), lambda r, i: (i, r))
    st_nxt = pl.BlockSpec((BAND, HD), lambda r, i: (jnp.minimum((i + 1) * nsb, nblk - 1), r))
    return nsb, cur, prev, nxt, st_cur, st_nxt


def _band_mask_q(i, first_tile):
    qr = lax.broadcasted_iota(jnp.int32, (BAND, 2 * BAND), 0)
    kc = lax.broadcasted_iota(jnp.int32, (BAND, 2 * BAND), 1)
    in_prev = (kc < BAND) & (kc >= qr)
    in_cur = (kc >= BAND) & (kc - BAND <= qr)
    if i == 0:
        in_prev = in_prev & jnp.logical_not(first_tile)
    return in_prev | in_cur


def _band_mask_k(j, nsb, last_tile):
    kc = lax.broadcasted_iota(jnp.int32, (BAND, 2 * BAND), 0)
    qr = lax.broadcasted_iota(jnp.int32, (BAND, 2 * BAND), 1)
    same = (qr < BAND) & (kc <= qr)
    nxt = (qr >= BAND) & (kc >= qr - BAND)
    if j == nsb - 1:
        nxt = nxt & jnp.logical_not(last_tile)
    return same | nxt


def _band_fwd(name, q, k, v, d):
    l = q.shape[0]
    tq = min(512, l)
    nsb, cur, prev, _, st_cur, _ = _band_specs(l, d, tq)
    scale = HD ** -0.5

    def body(q_ref, kc_ref, kp_ref, vc_ref, vp_ref, o_ref, lse_ref):
        first = pl.program_id(1) == 0
        for i in range(nsb):
            lses = []
            mask = _band_mask_q(i, first)
            for h in range(4):
                cs = slice(h * HD, (h + 1) * HD)
                qv = q_ref[i * BAND:(i + 1) * BAND, cs]
                if i == 0:
                    kk = jnp.concatenate([kp_ref[:, cs], kc_ref[0:BAND, cs]], axis=0)
                    vv = jnp.concatenate([vp_ref[:, cs], vc_ref[0:BAND, cs]], axis=0)
                else:
                    kk = kc_ref[(i - 1) * BAND:(i + 1) * BAND, cs]
                    vv = vc_ref[(i - 1) * BAND:(i + 1) * BAND, cs]
                s = jnp.where(mask, _dot(qv, kk, NT) * scale, NEG)
                m = jnp.max(s, axis=-1, keepdims=True)
                p = jnp.exp(s - m)
                den = jnp.sum(p, axis=-1, keepdims=True)
                o_ref[i * BAND:(i + 1) * BAND, cs] = _dot(p.astype(BF16), vv) / den
                lses.append(m + jnp.log(den))
            lse_ref[i * BAND:(i + 1) * BAND, :] = _lane_pack(lses, (BAND, HD))

    return pl.pallas_call(
        body, name=name, grid=(d, l // tq), in_specs=[cur, cur, prev, cur, prev],
        out_specs=[cur, st_cur],
        out_shape=[jax.ShapeDtypeStruct((l, d * A_GROUP), F32), jax.ShapeDtypeStruct((l, d * HD), F32)],
        compiler_params=_params(("parallel", "parallel")),
    )(q, k, k, v, v)


def _band_dq(name, q, k, v, dy, lse, delta, d):
    l = q.shape[0]
    tq = min(512, l)
    nsb, cur, prev, _, st_cur, _ = _band_specs(l, d, tq)
    scale = HD ** -0.5

    def body(q_ref, kc_ref, kp_ref, vc_ref, vp_ref, dy_ref, lse_ref, dl_ref, dq_ref):
        first = pl.program_id(1) == 0
        for i in range(nsb):
            mask = _band_mask_q(i, first)
            rs = slice(i * BAND, (i + 1) * BAND)
            for h in range(4):
                cs = slice(h * HD, (h + 1) * HD)
                if i == 0:
                    kk = jnp.concatenate([kp_ref[:, cs], kc_ref[0:BAND, cs]], axis=0)
                    vv = jnp.concatenate([vp_ref[:, cs], vc_ref[0:BAND, cs]], axis=0)
                else:
                    kk = kc_ref[(i - 1) * BAND:(i + 1) * BAND, cs]
                    vv = vc_ref[(i - 1) * BAND:(i + 1) * BAND, cs]
                s = jnp.where(mask, _dot(q_ref[rs, cs], kk, NT) * scale, NEG)
                p = jnp.exp(s - lse_ref[rs, h:h + 1])
                dp = _dot(dy_ref[rs, cs], vv, NT)
                ds = p * (dp - dl_ref[rs, h:h + 1])
                dq_ref[rs, cs] = (_dot(ds.astype(BF16), kk) * scale).astype(dq_ref.dtype)

    return pl.pallas_call(
        body, name=name, grid=(d, l // tq),
        in_specs=[cur, cur, prev, cur, prev, cur, st_cur, st_cur], out_specs=cur,
        out_shape=jax.ShapeDtypeStruct((l, d * A_GROUP), BF16),
        compiler_params=_params(("parallel", "parallel")),
    )(q, k, k, v, v, dy, lse, delta)


def _band_dkv(name, q, k, v, dy, lse, delta, d):
    l = q.shape[0]
    tq = min(512, l)
    nsb, cur, _, nxt, st_cur, st_nxt = _band_specs(l, d, tq)
    scale = HD ** -0.5
    ntile = l // tq

    def body(k_ref, v_ref, qc_ref, qn_ref, dyc_ref, dyn_ref, lc_ref, ln_ref, dc_ref, dn_ref,
             dk_ref, dv_ref):
        last = pl.program_id(1) == ntile - 1

        def win(c_ref, n_ref, j, cs):
            if j == nsb - 1:
                return jnp.concatenate([c_ref[j * BAND:(j + 1) * BAND, cs], n_ref[:, cs]], axis=0)
            return c_ref[j * BAND:(j + 2) * BAND, cs]

        allh = slice(0, HD)
        for j in range(nsb):
            mask = _band_mask_k(j, nsb, last)
            rs = slice(j * BAND, (j + 1) * BAND)
            lse_t = win(lc_ref, ln_ref, j, allh).T
            delta_t = win(dc_ref, dn_ref, j, allh).T
            for h in range(4):
                cs = slice(h * HD, (h + 1) * HD)
                qw = win(qc_ref, qn_ref, j, cs)
                dyw = win(dyc_ref, dyn_ref, j, cs)
                st = jnp.where(mask, _dot(k_ref[rs, cs], qw, NT) * scale, NEG)
                pt = jnp.exp(st - lse_t[h:h + 1, :])
                dst = pt * (_dot(v_ref[rs, cs], dyw, NT) - delta_t[h:h + 1, :])
                dv_ref[rs, cs] = _dot(pt.astype(BF16), dyw).astype(dv_ref.dtype)
                dk_ref[rs, cs] = (_dot(dst.astype(BF16), qw) * scale).astype(dk_ref.dtype)

    shp = jax.ShapeDtypeStruct((l, d * A_GROUP), BF16)
    return pl.pallas_call(
        body, name=name, grid=(d, ntile),
        in_specs=[cur, cur, cur, nxt, cur, nxt, st_cur, st_nxt, st_cur, st_nxt],
        out_specs=[cur, cur], out_shape=[shp, shp],
        compiler_params=_params(("parallel", "parallel")),
    )(k, v, q, q, dy, dy, lse, lse, delta, delta)


def _split3(x):
    hi = x.astype(BF16)
    r1 = x - hi.astype(F32)
    mid = r1.astype(BF16)
    lo = (r1 - mid.astype(F32)).astype(BF16)
    return hi, mid, lo


def _fox_prep(z, b):
    h, s = z.shape
    blk = min(512, s)

    def body(z_ref, b_ref, c_ref):
        r = lax.broadcasted_iota(jnp.int32, (blk, blk), 0)
        cidx = lax.broadcasted_iota(jnp.int32, (blk, blk), 1)
        tri = (r <= cidx).astype(BF16)
        carry = jnp.zeros((h, 1), F32)
        for t in range(s // blk):
            zz = z_ref[:, t * blk:(t + 1) * blk] + b_ref[...]
            lf = jnp.minimum(zz, 0.0) - jnp.log(1.0 + jnp.exp(-jnp.abs(zz)))
            hi, mid, lo = _split3(lf)
            cs = _dot(hi, tri) + _dot(mid, tri) + _dot(lo, tri) + carry
            c_ref[:, t * blk:(t + 1) * blk] = cs
            carry = cs[:, blk - 1:blk]

    return pl.pallas_call(body, name="fox_prep", out_shape=jax.ShapeDtypeStruct((h, s), F32))(z, b)


def _fox_prep_bwd(dc, z, b):
    h, s = z.shape
    blk = min(512, s)

    def body(dc_ref, z_ref, b_ref, dz_ref, db_ref):
        r = lax.broadcasted_iota(jnp.int32, (blk, blk), 0)
        cidx = lax.broadcasted_iota(jnp.int32, (blk, blk), 1)
        tri = (r >= cidx).astype(BF16)
        carry = jnp.zeros((h, 1), F32)
        tot = jnp.zeros((h, 1), F32)
        for t in reversed(range(s // blk)):
            hi, mid, lo = _split3(dc_ref[:, t * blk:(t + 1) * blk])
            rc = _dot(hi, tri) + _dot(mid, tri) + _dot(lo, tri) + carry
            carry = rc[:, 0:1]
            zz = z_ref[:, t * blk:(t + 1) * blk] + b_ref[...]
            dz = rc * _sig(-zz)
            dz_ref[:, t * blk:(t + 1) * blk] = dz
            tot = tot + jnp.sum(dz, axis=-1, keepdims=True)
        db_ref[...] = tot

    return pl.pallas_call(
        body, name="fox_prep_bwd",
        out_shape=[jax.ShapeDtypeStruct((h, s), F32), jax.ShapeDtypeStruct((h, 1), F32)])(dc, z, b)


FOX_W = 128
FOX_C = B_HD
FOX_ONE = B_HD + 3
FOX_SUB = 256
FOX_SUB_FWD = 128
FOX_HEADS_PER_STEP = 2


def _head_of_pair(x, hh):
    return x if hh == 0 else pltpu.roll(x, B_HD, 1)


def _fox_pack(u, c_col, t):
    s = u.shape[0]
    nt = s // t
    scale = B_HD ** -0.5

    def body(q_ref, k_ref, v_ref, c_ref, qf_ref, kb_ref, ks_ref, vb_ref, vt_ref):
        lane = lax.broadcasted_iota(jnp.int32, (t, FOX_W), 1)
        qv, kv, vv = [r[...].astype(F32) for r in (q_ref, k_ref, v_ref)]
        for hh in range(2):
            qf_ref[hh] = jnp.where(lane < B_HD, _head_of_pair(qv, hh), B_HD ** 0.5).astype(BF16)
            neg = c_ref[hh] * (-scale)
            hi = neg.astype(BF16).astype(F32)
            mid = (neg - hi).astype(BF16).astype(F32)
            lo = neg - hi - mid
            aux = jnp.where(lane == FOX_C, hi,
                            jnp.where(lane == FOX_C + 1, mid, jnp.where(lane == FOX_C + 2, lo, 0.0)))
            kb = jnp.where(lane < B_HD, _head_of_pair(kv, hh) * scale, aux)
            kb_ref[hh] = kb.astype(BF16)
            ks_ref[hh] = jnp.where(lane == FOX_ONE, 1.0, kb).T.astype(BF16)
            vb = jnp.where(lane < B_HD, _head_of_pair(vv, hh), 1.0)
            vb_ref[hh] = vb.astype(BF16)
            vt_ref[hh] = vb.T.astype(BF16)

    def tok(col0):
        return pl.BlockSpec((t, FOX_W), functools.partial(lambda hp, i, cb: (i, cb + hp), cb=col0 // FOX_W))

    rows = pl.BlockSpec((2, t, FOX_W), lambda hp, i: (hp, i, 0))
    tiles = pl.BlockSpec((2, None, FOX_W, t), lambda hp, i: (hp, i, 0, 0))
    hm = jax.ShapeDtypeStruct((B_HEADS, s, FOX_W), BF16)
    tt = jax.ShapeDtypeStruct((B_HEADS, nt, FOX_W, t), BF16)
    return pl.pallas_call(
        body, name="fox_pack", grid=(B_HEADS // 2, nt),
        in_specs=[tok(C_QB), tok(C_KB), tok(C_VB), pl.BlockSpec((2, t, 1), lambda hp, i: (hp, i, 0))],
        out_specs=[rows, rows, tiles, rows, tiles], out_shape=[hm, hm, tt, hm, tt],
        compiler_params=_params(("parallel", "parallel")),
    )(u, u, u, c_col)


def _fox_pack_bwd(dy, y, t):
    s = dy.shape[0]
    nt = s // t

    def body(do_ref, o_ref, dow_ref, dl_ref):
        lane = lax.broadcasted_iota(jnp.int32, (t, FOX_W), 1)
        lane8 = lax.broadcasted_iota(jnp.int32, (8, FOX_W), 1)
        dov = do_ref[...].astype(F32)
        parts = _split3(dov * o_ref[...].astype(F32))
        for hh in range(2):
            dow_ref[hh] = jnp.where(lane < B_HD, _head_of_pair(dov, hh), 0.0).astype(BF16)
            mask = ((lane8 >= hh * B_HD) & (lane8 < (hh + 1) * B_HD)).astype(BF16)
            row = _dot(mask, parts[0], NT) + _dot(mask, parts[1], NT) + _dot(mask, parts[2], NT)
            dl_ref[hh] = row[0:1, :]

    tok = pl.BlockSpec((t, FOX_W), lambda hp, i: (i, hp))
    return pl.pallas_call(
        body, name="fox_pack_bwd", grid=(B_HEADS // 2, nt), in_specs=[tok, tok],
        out_specs=[pl.BlockSpec((2, t, FOX_W), lambda hp, i: (hp, i, 0)),
                   pl.BlockSpec((2, None, 1, t), lambda hp, i: (hp, i, 0, 0))],
        out_shape=[jax.ShapeDtypeStruct((B_HEADS, s, FOX_W), BF16), jax.ShapeDtypeStruct((B_HEADS, nt, 1, t), F32)],
        compiler_params=_params(("parallel", "parallel")),
    )(dy, y)


def _fox_unpack(dqt, dkw, dvw, t):
    h, nt = dqt.shape[:2]
    s = nt * t

    def body(dq_ref, dk_ref, dv_ref, dqo_ref, dko_ref, dvo_ref, dc_ref):
        lane = lax.broadcasted_iota(jnp.int32, (t, FOX_W), 1)

        def join(a0, a1):
            return jnp.where(lane < B_HD, a0, pltpu.roll(a1, B_HD, 1))

        for hh in range(2):
            dc_ref[hh] = dq_ref[hh][FOX_ONE:FOX_ONE + 1, :] - dk_ref[hh].T[B_HD:B_HD + 1, :]
        dqo_ref[...] = join(dq_ref[0].T, dq_ref[1].T).astype(BF16)
        dko_ref[...] = join(dk_ref[0], dk_ref[1]).astype(BF16)
        dvo_ref[...] = join(dv_ref[0], dv_ref[1]).astype(BF16)

    tok = pl.BlockSpec((t, FOX_W), lambda hp, i: (i, hp))
    rows = pl.BlockSpec((2, t, FOX_W), lambda hp, i: (hp, i, 0))
    shp = jax.ShapeDtypeStruct((s, h * B_HD), BF16)
    return pl.pallas_call(
        body, name="fox_unpack", grid=(h // 2, nt),
        in_specs=[pl.BlockSpec((2, None, FOX_W, t), lambda hp, i: (hp, i, 0, 0)), rows, rows],
        out_specs=[tok, tok, tok, pl.BlockSpec((2, None, 1, t), lambda hp, i: (hp, i, 0, 0))],
        out_shape=[shp, shp, shp, jax.ShapeDtypeStruct((h, nt, 1, t), F32)],
        compiler_params=_params(("parallel", "parallel")),
    )(dqt, dkw, dvw)


FOX_DEAD = -110.0


def _fox_norm2(qf, kb):
    h, s, w = qf.shape
    tm = min(2048, s)

    def body(q_ref, k_ref, qo_ref, ko_ref):
        row = lax.broadcasted_iota(jnp.int32, (w, w), 0)
        ones = (row < B_HD).astype(BF16)
        for x_ref, o_ref in ((q_ref, qo_ref), (k_ref, ko_ref)):
            xv = x_ref[...].astype(F32)
            n2 = _dot((xv * xv).astype(BF16), ones)
            o_ref[...] = jnp.broadcast_to(jnp.max(n2, axis=0, keepdims=True)[:, :1], o_ref.shape)

    spec = pl.BlockSpec((None, tm, w), lambda hh, i: (hh, i, 0))
    ospec = pl.BlockSpec((None, None, 8, 128), lambda hh, i: (hh, i, 0, 0))
    shp = jax.ShapeDtypeStruct((h, s // tm, 8, 128), F32)
    return pl.pallas_call(
        body, name="fox_norm2", grid=(h, s // tm), in_specs=[spec, spec], out_specs=[ospec, ospec],
        out_shape=[shp, shp], compiler_params=_params(("parallel", "parallel")),
    )(qf, kb)


def _fox_bounds(qf, kb, c, t):
    q2, k2 = _fox_norm2(qf, kb)
    g = 2.0 * jnp.sqrt(1.02 * jnp.max(q2[:, :, 0, 0], axis=1) * 1.02 * jnp.max(k2[:, :, 0, 0], axis=1))
    return jnp.concatenate([c[:, ::t], c[:, t - 1::t], g[:, None]], axis=1)


SMEM_SPEC = pl.BlockSpec(memory_space=pltpu.SMEM)


def _fox_fwd(qf, kb, vt4, bounds, t):
    h, s, w = qf.shape
    nt = s // t
    sub = FOX_SUB_FWD
    nsub = t // sub
    nh = FOX_HEADS_PER_STEP

    def body(b_ref, q_ref, k_ref, v_ref, o_ref, lse_ref):
        i = pl.program_id(1)
        krow = lax.broadcasted_iota(jnp.int32, (sub, t), 0)
        qcol = lax.broadcasted_iota(jnp.int32, (sub, t), 1)

        def dead_before(hh):
            head = pl.program_id(0) * nh + hh
            top = b_ref[head, 2 * nt] + b_ref[head, i]
            return lax.fori_loop(
                0, i, lambda jj, n: n + (top - b_ref[head, nt + jj] < FOX_DEAD).astype(jnp.int32), 0)

        j_lo = functools.reduce(jnp.minimum, [dead_before(hh) for hh in range(nh)])

        def tile(j, carry, diag):
            out = []
            for hh in range(nh):
                m, acc = carry[hh]
                qv, vj = q_ref[hh], v_ref[hh, j]
                los = [b * sub if diag else 0 for b in range(nsub)]
                sts = [_dot(k_ref[hh, pl.ds(pl.multiple_of(j * t + b * sub, sub), sub), :], qv[lo:, :], NT)
                       for b, lo in enumerate(los)]
                for b, lo in enumerate(los):
                    st = sts[b]
                    if diag:
                        st = jnp.where(krow[:, :t - lo] <= qcol[:, :t - lo], st, NEG)
                    m_old, acc_old = m[:, lo:], acc[:, lo:]
                    m2 = jnp.maximum(m_old, jnp.max(st, axis=0, keepdims=True))
                    p = jnp.exp(st - m2).astype(BF16)
                    acc2 = jnp.exp(m_old - m2) * acc_old + _dot(vj[:, b * sub:(b + 1) * sub], p)
                    m = m2 if lo == 0 else jnp.concatenate([m[:, :lo], m2], axis=1)
                    acc = acc2 if lo == 0 else jnp.concatenate([acc[:, :lo], acc2], axis=1)
                out.append((m, acc))
            return tuple(out)

        init = tuple((jnp.full((1, t), NEG, F32), jnp.zeros((w, t), F32)) for _ in range(nh))
        carry = lax.fori_loop(j_lo, i, lambda j, c: tile(j, c, False), init)
        outs = []
        for hh, (m, acc) in enumerate(tile(i, carry, True)):
            den = acc[B_HD:B_HD + 1, :]
            outs.append(acc[0:B_HD, :] / den)
            lse_ref[hh] = m + jnp.log(den)
        o_ref[...] = jnp.concatenate(outs, axis=0).T.astype(o_ref.dtype)

    return pl.pallas_call(
        body, name="fox_fwd", grid=(h // nh, nt),
        in_specs=[SMEM_SPEC,
                  pl.BlockSpec((nh, t, w), lambda hh, i: (hh, i, 0)),
                  pl.BlockSpec((nh, s, w), lambda hh, i: (hh, 0, 0)),
                  pl.BlockSpec((nh, nt, w, t), lambda hh, i: (hh, 0, 0, 0))],
        out_specs=[pl.BlockSpec((t, nh * B_HD), lambda hh, i: (i, hh)),
                   pl.BlockSpec((nh, 1, t), lambda hh, i: (hh, 0, i))],
        out_shape=[jax.ShapeDtypeStruct((s, h * B_HD), BF16), jax.ShapeDtypeStruct((h, 1, s), F32)],
        compiler_params=_params(("parallel", "parallel")),
    )(bounds, qf, kb, vt4)


def _fox_bwd(qf, dow, lse_row, delta_row, kb, kst4, vb, bounds, t):
    h, s, w = qf.shape
    nt = s // t
    nsub = t // FOX_SUB
    nh = FOX_HEADS_PER_STEP

    def body(b_ref, q_ref, do_ref, lse_ref, dl_ref, k_ref, kt_ref, v_ref, dqt_ref, dk_ref, dv_ref, dk_acc, dv_acc):
        j = pl.program_id(1)

        def alive_after(hh):
            head = pl.program_id(0) * nh + hh
            top = b_ref[head, 2 * nt] - b_ref[head, nt + j]
            return lax.fori_loop(
                j + 1, nt, lambda ii, n: n + (top + b_ref[head, ii] >= FOX_DEAD).astype(jnp.int32), 0)

        i_hi = j + 1 + functools.reduce(jnp.maximum, [alive_after(hh) for hh in range(nh)])

        @pl.when(j == 0)
        def _():
            dqt_ref[...] = jnp.zeros_like(dqt_ref)

        dk_acc[...] = jnp.zeros_like(dk_acc)
        dv_acc[...] = jnp.zeros_like(dv_acc)
        krow = lax.broadcasted_iota(jnp.int32, (FOX_SUB, t), 0)
        qcol = lax.broadcasted_iota(jnp.int32, (FOX_SUB, t), 1)
        subs = [slice(b * FOX_SUB, (b + 1) * FOX_SUB) for b in range(nsub)]

        def tile(i, diag):
            i0 = pl.multiple_of(i * t, t)
            for hh in range(nh):
                qi, doi = q_ref[hh, pl.ds(i0, t), :], do_ref[hh, pl.ds(i0, t), :]
                lse, dl = lse_ref[hh, i], dl_ref[hh, i]
                los = [b * FOX_SUB if diag else 0 for b in range(nsub)]
                sts = [_dot(k_ref[hh, rs, :], qi[lo:, :], NT) for rs, lo in zip(subs, los)]
                dps = [_dot(v_ref[hh, rs, :], doi[lo:, :], NT) for rs, lo in zip(subs, los)]
                dq = None
                for b, (rs, lo) in enumerate(zip(subs, los)):
                    st = sts[b] - lse[:, lo:]
                    if diag:
                        st = jnp.where(krow[:, :t - lo] <= qcol[:, :t - lo], st, NEG)
                    pt = jnp.exp(st)
                    dsb = (pt * (dps[b] - dl[:, lo:])).astype(BF16)
                    dv_acc[hh, rs, :] += _dot(pt.astype(BF16), doi[lo:, :])
                    dk_acc[hh, rs, :] += _dot(dsb, qi[lo:, :])
                    part = _dot(kt_ref[hh, :, rs], dsb)
                    if lo:
                        part = jnp.concatenate([jnp.zeros((w, lo), F32), part], axis=1)
                    dq = part if dq is None else dq + part
                dqt_ref[hh, i] += dq

        def step(i, carry):
            tile(i, False)
            return carry

        tile(j, True)
        lax.fori_loop(j + 1, i_hi, step, 0)
        dk_ref[...] = dk_acc[...] * (B_HD ** -0.5)
        dv_ref[...] = dv_acc[...]

    full = pl.BlockSpec((nh, s, w), lambda hh, j: (hh, 0, 0))
    rowst = pl.BlockSpec((nh, nt, 1, t), lambda hh, j: (hh, 0, 0, 0))
    tl = pl.BlockSpec((nh, t, w), lambda hh, j: (hh, j, 0))
    return pl.pallas_call(
        body, name="fox_bwd", grid=(h // nh, nt),
        in_specs=[SMEM_SPEC, full, full, rowst, rowst, tl,
                  pl.BlockSpec((nh, None, w, t), lambda hh, j: (hh, j, 0, 0)), tl],
        out_specs=[pl.BlockSpec((nh, nt, w, t), lambda hh, j: (hh, 0, 0, 0)), tl, tl],
        out_shape=[jax.ShapeDtypeStruct((h, nt, w, t), F32), jax.ShapeDtypeStruct((h, s, w), F32),
                   jax.ShapeDtypeStruct((h, s, w), F32)],
        scratch_shapes=[pltpu.VMEM((nh, t, w), F32), pltpu.VMEM((nh, t, w), F32)],
        compiler_params=_params(("parallel", "arbitrary")),
    )(bounds, qf, dow, lse_row, delta_row, kb, kst4, vb)


def _mem_fwd(u, mkv, tq=512):
    s = u.shape[0]
    scale = HD ** -0.5

    def body(q_ref, mk_ref, mv_ref, o_ref, lse_ref):
        lses = []
        for h in range(4):
            cs = slice(h * HD, (h + 1) * HD)
            sc = _dot(q_ref[:, cs], mk_ref[:, cs], NT) * scale
            m = jnp.max(sc, axis=-1, keepdims=True)
            p = jnp.exp(sc - m)
            den = jnp.sum(p, axis=-1, keepdims=True)
            o_ref[:, cs] = (_dot(p.astype(BF16), mv_ref[:, cs]) / den).astype(o_ref.dtype)
            lses.append(m + jnp.log(den))
        lse_ref[...] = _lane_pack(lses, (tq, HD))

    return pl.pallas_call(
        body, name="mem_fwd", grid=(s // tq,),
        in_specs=[pl.BlockSpec((tq, 512), lambda i: (i, C_QM // 512)),
                  pl.BlockSpec((N_MEM, 512), lambda i: (0, 0)),
                  pl.BlockSpec((N_MEM, 512), lambda i: (0, 1))],
        out_specs=[pl.BlockSpec((tq, 512), lambda i: (i, 0)), pl.BlockSpec((tq, HD), lambda i: (i, 0))],
        out_shape=[jax.ShapeDtypeStruct((s, 512), BF16), jax.ShapeDtypeStruct((s, HD), F32)],
        compiler_params=_params(("parallel",)),
    )(u, mkv, mkv)


def _mem_bwd(u, mkv, o, do, lse, tq=512):
    s = u.shape[0]
    scale = HD ** -0.5

    def body(q_ref, mk_ref, mv_ref, o_ref, do_ref, lse_ref, dq_ref, dmk_ref, dmv_ref):
        @pl.when(pl.program_id(0) == 0)
        def _():
            dmk_ref[...] = jnp.zeros_like(dmk_ref)
            dmv_ref[...] = jnp.zeros_like(dmv_ref)

        for h in range(4):
            cs = slice(h * HD, (h + 1) * HD)
            qv, dov = q_ref[:, cs], do_ref[:, cs]
            sc = _dot(qv, mk_ref[:, cs], NT) * scale
            p = jnp.exp(sc - lse_ref[:, h:h + 1])
            delta = jnp.sum(dov.astype(F32) * o_ref[:, cs].astype(F32), axis=-1, keepdims=True)
            ds = p * (_dot(dov, mv_ref[:, cs], NT) - delta)
            dsb = ds.astype(BF16)
            dq_ref[:, cs] = (_dot(dsb, mk_ref[:, cs]) * scale).astype(dq_ref.dtype)
            dmk_ref[:, cs] += _dot(dsb, qv, TN) * scale
            dmv_ref[:, cs] += _dot(p.astype(BF16), dov, TN)

    row = pl.BlockSpec((tq, 512), lambda i: (i, 0))
    acc = pl.BlockSpec((N_MEM, 512), lambda i: (0, 0))
    return pl.pallas_call(
        body, name="mem_bwd", grid=(s // tq,),
        in_specs=[pl.BlockSpec((tq, 512), lambda i: (i, C_QM // 512)),
                  pl.BlockSpec((N_MEM, 512), lambda i: (0, 0)),
                  pl.BlockSpec((N_MEM, 512), lambda i: (0, 1)),
                  row, row, pl.BlockSpec((tq, HD), lambda i: (i, 0))],
        out_specs=[row, acc, acc],
        out_shape=[jax.ShapeDtypeStruct((s, 512), BF16), jax.ShapeDtypeStruct((N_MEM, 512), F32),
                   jax.ShapeDtypeStruct((N_MEM, 512), F32)],
        compiler_params=_params(("arbitrary",)),
    )(u, mkv, mkv, o, do, lse)


def _local_step(x, mem, pos, target, g_pre, g_post, g_mem, w_main, w_fb, b_forget, b_merge,
                w_mem_kv, w_ba, w_bb, w_bm, w_out, exchange=None):
    s = x.shape[0]
    t_fox = min(512, s)
    nt = s // t_fox
    half = ROT_DIM // 2
    inv = ROPE_THETA ** (-jnp.arange(half, dtype=F32) / half)
    inv128 = jnp.concatenate([inv, inv, jnp.zeros((HD - ROT_DIM,), F32)]).reshape(1, HD)

    h = _rms_fwd("norm_pre", x, g_pre)
    u = _mm("proj_in", h, w_main, "nn", BF16, tm=4096)
    ufb = _mm("proj_fb", h, w_fb, "nn", F32)
    memn = _rms_fwd("norm_mem", mem, g_mem)
    mkv = _mm("proj_mem", memn, w_mem_kv, "nn", BF16)

    qkv = _rope_fwd(u, pos, inv128)
    views = [tuple(qkv[3 * g:3 * g + 3]) for g in range(3)]
    os_, lses = [], []
    for g, d in enumerate(DILATIONS):
        o_g, lse_g = _band_fwd("band_fwd%d" % g, *views[g], d)
        os_.append((o_g, d * A_GROUP, 0, d))
        lses.append((lse_g, d * HD, 0, d))

    def merge_a(o1, o2, o3, l1, l2, l3, za, *scr):
        o1, o2, o3 = [_from_class(o, scr, d) for o, d in zip((o1, o2, o3), DILATIONS)]
        l1, l2, l3 = [_from_class(lv, scr, d) for lv, d in zip((l1, l2, l3), DILATIONS)]
        ys, tots = [], []
        for hh in range(4):
            cs, hs = slice(hh * HD, (hh + 1) * HD), slice(hh, hh + 1)
            mx = jnp.maximum(jnp.maximum(l1[:, hs], l2[:, hs]), l3[:, hs])
            e1, e2, e3 = jnp.exp(l1[:, hs] - mx), jnp.exp(l2[:, hs] - mx), jnp.exp(l3[:, hs] - mx)
            den = e1 + e2 + e3
            ys.append((e1 * o1[:, cs] + e2 * o2[:, cs] + e3 * o3[:, cs]) / den)
            tots.append(mx + jnp.log(den))
        y = jnp.concatenate(ys, axis=1)
        zf = za.astype(F32)
        tot = _lane_pack(tots, l1.shape)
        return (y, y * (zf * _sig(zf))) + tuple(_to_class(tot, scr, d) for d in DILATIONS)

    res = _rows("merge_a", merge_a, os_ + lses + [(u, 512, C_ZA // 512)], [],
                [(512, BF16), (512, BF16)] + [(d * HD, F32, d) for d in DILATIONS], tm=ROPE_TM,
                scratch=_class_scratch(ROPE_TM))
    y_a, yg_a, lse_a = res[0], res[1], res[2:5]

    zrow = ufb[:, :B_HEADS].T
    c = _fox_prep(zrow, b_forget.reshape(B_HEADS, 1))
    qf, kb, kst4, vb, vt4 = _fox_pack(u, c.reshape(B_HEADS, s, 1), t_fox)
    bounds = _fox_bounds(qf, kb, c, t_fox)
    y_b, lse_b = _fox_fwd(qf, kb, vt4, bounds, t_fox)

    y_m, lse_m = _mem_fwd(u, mkv)

    def gate(y, z):
        zf = z.astype(F32)
        return (y.astype(F32) * (zf * _sig(zf)),)

    yg_b = _rows("gate_b", gate, [y_b, (u, 512, C_ZB // 512)], [], [(512, BF16)])[0]
    yg_m = _rows("gate_m", gate, [y_m, (u, 512, C_ZM // 512)], [], [(512, BF16)])[0]

    br_a = _mm("branch_a", yg_a, w_ba, "nn", BF16)
    br_b = _mm("branch_b", yg_b, w_bb, "nn", BF16)
    br_m = _mm("branch_m", yg_m, w_bm, "nn", BF16)
    gl = [(u, 1024, C_GL // 1024 + i) for i in range(3)]
    bm3 = b_merge.reshape(3, D_MODEL)

    def merge(g0, g1, g2, b0, b1, b2, bm):
        tot = 0.0
        for i, (gv, bv) in enumerate(((g0, b0), (g1, b1), (g2, b2))):
            tot = tot + _sig(gv.astype(F32) + bm[i:i + 1, :]) * bv.astype(F32)
        return (tot,)

    merged = _rows("merge_gates", merge, gl + [br_a, br_b, br_m], [bm3], [(D_MODEL, BF16)])[0]
    out = _mm("proj_out", merged, w_out, "nn", F32)

    def tail(xv, ov, tv, gv):
        r = lax.rsqrt(jnp.mean(ov * ov, axis=-1, keepdims=True) + EPS)
        n = ov * r
        err = xv + n * gv - tv
        dy = err * (1.0 / D_MODEL)
        dn = dy * gv
        dout = r * (dn - n * jnp.mean(dn * n, axis=-1, keepdims=True))
        return (dy, dout, jnp.sum(0.5 * err * err * (1.0 / D_MODEL), axis=0, keepdims=True),
                jnp.sum(dy * n, axis=0, keepdims=True))

    dy, dout, loss_lanes, g_post_grad = _rows(
        "tail", tail, [x, out, target], [g_post], [(D_MODEL, F32), (D_MODEL, BF16)],
        reds=[D_MODEL, D_MODEL], tm=256)

    dmerged = _mm("d_merged", dout, w_out, "nt", BF16)
    gw_out = _mm("g_w_out", merged, dout, "tn", F32)

    def merge_bwd(dm, g0, g1, g2, b0, b1, b2, bm):
        dmf = dm.astype(F32)
        dbs, dgs, sums = [], [], []
        for i, (gv, bv) in enumerate(((g0, b0), (g1, b1), (g2, b2))):
            sg = _sig(gv.astype(F32) + bm[i:i + 1, :])
            dbs.append(dmf * sg)
            dg = dmf * bv.astype(F32) * sg * (1.0 - sg)
            dgs.append(dg)
            sums.append(jnp.sum(dg, axis=0, keepdims=True))
        return tuple(dbs + dgs + sums)

    res = _rows("merge_bwd", merge_bwd, [dmerged] + gl + [br_a, br_b, br_m], [bm3],
                [(D_MODEL, BF16)] * 6, reds=[D_MODEL] * 3, tm=256)
    dbr, dgl, g_bmerge = res[0:3], res[3:6], jnp.concatenate(res[6:9], axis=1)

    dyg, gw_branch = [], []
    for nm, dbv, wv, ygv in (("a", dbr[0], w_ba, yg_a), ("b", dbr[1], w_bb, yg_b), ("m", dbr[2], w_bm, yg_m)):
        dyg.append(_mm("d_yg_" + nm, dbv, wv, "nt", BF16))
        gw_branch.append(_mm("g_w_branch_" + nm, ygv, dbv, "tn", F32))

    def gate_bwd(dg, y, z):
        dgf, yf, zf = dg.astype(F32), y.astype(F32), z.astype(F32)
        sg = _sig(zf)
        return dgf * (zf * sg), dgf * yf * (sg * (1.0 + zf * (1.0 - sg)))

    def gate_bwd_a(dg, y, z, *scr):
        dyv, dz = gate_bwd(dg, y, z)
        prod = dyv * y.astype(F32)
        dl = [jnp.sum(prod[:, hh * HD:(hh + 1) * HD], axis=-1, keepdims=True) for hh in range(4)]
        delta = _lane_pack(dl, (dg.shape[0], HD))
        return ((dz,) + tuple(_to_class(dyv, scr, d) for d in DILATIONS)
                + tuple(_to_class(delta, scr, d) for d in DILATIONS))

    res = _rows("gate_bwd_a", gate_bwd_a, [dyg[0], y_a, (u, 512, C_ZA // 512)], [],
                [(512, BF16)] + [(d * A_GROUP, BF16, d) for d in DILATIONS] + [(d * HD, F32, d) for d in DILATIONS],
                tm=ROPE_TM, scratch=_class_scratch(ROPE_TM))
    dz_a, dy_a, delta_a = res[0], res[1:4], res[4:7]
    dy_b, dz_b = _rows("gate_bwd_b", gate_bwd, [dyg[1], y_b, (u, 512, C_ZB // 512)], [],
                       [(512, BF16), (512, BF16)])
    dy_m, dz_m = _rows("gate_bwd_m", gate_bwd, [dyg[2], y_m, (u, 512, C_ZM // 512)], [],
                       [(512, BF16), (512, BF16)])

    dq_m, dmk, dmv = _mem_bwd(u, mkv, y_m, dy_m, lse_m)
    dmkv = jnp.concatenate([dmk, dmv], axis=1)
    gw_mem_kv = _mm("g_w_mem_kv", memn, dmkv, "tn", F32)
    dmemn = _mm("d_memn", dmkv, w_mem_kv, "nt", F32)

    def mem_gain_grad(mv, dv):
        r = lax.rsqrt(jnp.mean(mv * mv, axis=-1, keepdims=True) + EPS)
        return (jnp.sum(dv * mv * r, axis=0, keepdims=True),)

    g_mem_grad = _rows("g_norm_mem", mem_gain_grad, [mem, dmemn], [], [], reds=[D_MODEL], tm=N_MEM)[0]

    dow, delta_b = _fox_pack_bwd(dy_b, y_b, t_fox)
    dqt, dkw, dvw = _fox_bwd(qf, dow, lse_b.reshape(B_HEADS, nt, 1, t_fox), delta_b, kb, kst4, vb, bounds, t_fox)
    dqb, dkb, dvb, dc = _fox_unpack(dqt, dkw, dvw, t_fox)
    dzrow, g_bforget = _fox_prep_bwd(dc.reshape(B_HEADS, s), zrow, b_forget.reshape(B_HEADS, 1))
    dfb = jnp.zeros((s, HD), BF16).at[:, :B_HEADS].set(dzrow.T.astype(BF16))

    dqs, dks, dvs = [], [], []
    for g, d in enumerate(DILATIONS):
        qv, kv, vv = views[g]
        dqs.append(_band_dq("band_dq%d" % g, qv, kv, vv, dy_a[g], lse_a[g], delta_a[g], d))
        dk_g, dv_g = _band_dkv("band_dkv%d" % g, qv, kv, vv, dy_a[g], lse_a[g], delta_a[g], d)
        dks.append(dk_g)
        dvs.append(dv_g)
    dqa, dka, dva = _rope_bwd(dqs, dks, dvs, pos, inv128)

    du = jnp.concatenate(
        [dqa, dka, dva, dz_a, dqb, dkb, dvb,
                            dz_b, dq_m, dz_m] + list(dgl), axis=1)

    gw_main = _mm("g_w_main", h.T, du, "nn", F32, tk=2048)
    gw_fb = _mm("g_w_fb", h, dfb, "tn", F32)
    gw_in = jnp.concatenate([gw_main[:, :FB_ORIG], gw_fb[:, :B_HEADS], gw_main[:, FB_ORIG:]], axis=1)
    grads = dict(norm_post_g=g_post_grad, norm_mem_g=g_mem_grad, w_in=gw_in,
                 b_forget=g_bforget.reshape(1, B_HEADS), b_merge=g_bmerge, w_mem_kv=gw_mem_kv,
                 w_branch_a=gw_branch[0], w_branch_b=gw_branch[1], w_branch_m=gw_branch[2], w_out=gw_out)
    side = exchange(grads) if exchange else None
    dh_main = _mm("d_h", du, w_main, "nt", F32, tk=2816, side=side)
    landed = None
    if side:
        dh_main, landed = dh_main[0], dh_main[1:]
    dh_fb = _mm("d_h_fb", dfb, w_fb, "nt", F32)

    def pre_bwd(xv, d1, d2, dyv, gv):
        r = lax.rsqrt(jnp.mean(xv * xv, axis=-1, keepdims=True) + EPS)
        n = xv * r
        dhv = d1 + d2
        dn = dhv * gv
        dx = r * (dn - n * jnp.mean(dn * n, axis=-1, keepdims=True))
        return dyv + dx, jnp.sum(dhv * n, axis=0, keepdims=True)

    grad_x, g_pre_grad = _rows("norm_pre_bwd", pre_bwd, [x, dh_main, dh_fb, dy], [g_pre],
                               [(D_MODEL, F32)], reds=[D_MODEL], tm=256)

    grads["norm_pre_g"] = g_pre_grad
    return loss_lanes, grad_x, grads, landed


HBM_SPEC = pl.BlockSpec(memory_space=pltpu.HBM)


def _place():
    x, y, c = lax.axis_index("x"), lax.axis_index("y"), lax.axis_index("c")
    chips = [(1 - x, y), (x, 1 - y), (1 - x, 1 - y)]
    return x, y, c, 2 * x + y, chips


N_CHUNKS = 4


def _units(parts, row_axis):
    units = []
    for i, a in enumerate(parts):
        ch = a.shape[row_axis] // N_CHUNKS
        units += [(i, pl.ds(k * ch, ch)) for k in range(N_CHUNKS)]
    return units


def _gather_weights(parts):
    n = len(parts)
    units = _units(parts, 1)
    nu = len(units)
    via_y = [(u % N_CHUNKS) < N_CHUNKS // 2 for u in range(nu)]

    def body(*refs):
        srcs, outs = refs[:n], refs[n:2 * n]
        send_sems, recv_sems = refs[2 * n:]
        x, y, c, p, _ = _place()
        me, sib = (x, y, c), (x, y, 1 - c)
        xn, yn, dg = (1 - x, y), (x, 1 - y), (1 - x, 1 - y)

        def cp(u, k, chip, half, to, from_src=False):
            i, rs = units[u]
            dst = outs[i].at[2 * chip[0] + chip[1], half, rs]
            return pltpu.make_async_remote_copy(
                src_ref=srcs[i].at[half, rs] if from_src else dst, dst_ref=dst, send_sem=send_sems.at[u, k],
                recv_sem=recv_sems.at[u, k], device_id=to, device_id_type=MESH)

        sent = []

        def go(copy):
            copy.start()
            sent.append(copy)

        for u in range(nu):
            go(cp(u, 0, (x, y), c, (*xn, c), from_src=True))
            go(cp(u, 1, (x, y), c, (*yn, c), from_src=True))
        for u in range(nu):
            cp(u, 0, xn, c, me).wait_recv()
            go(cp(u, 4, xn, c, sib))
            if via_y[u]:
                go(cp(u, 2, xn, c, (*yn, c)))
            cp(u, 1, yn, c, me).wait_recv()
            go(cp(u, 5, yn, c, sib))
            if not via_y[u]:
                go(cp(u, 3, yn, c, (*xn, c)))
        for u in range(nu):
            cp(u, 2 if via_y[u] else 3, dg, c, me).wait_recv()
            go(cp(u, 6, dg, c, sib))
        for u in range(nu):
            for k, chip in ((4, xn), (5, yn), (6, dg)):
                cp(u, k, chip, 1 - c, me).wait_recv()
        for copy in sent:
            copy.wait_send()

    return pl.pallas_call(
        body, name="gather_weights", in_specs=[HBM_SPEC] * n, out_specs=[HBM_SPEC] * n,
        out_shape=[jax.ShapeDtypeStruct((N_CHIPS,) + a.shape, a.dtype) for a in parts],
        scratch_shapes=[pltpu.SemaphoreType.DMA((nu, 7)), pltpu.SemaphoreType.DMA((nu, 7))],
    )(*parts)


def _swap_with_sibling(parts):
    n = len(parts)
    units = _units(parts, 2)

    def body(*refs):
        srcs, outs = refs[:n], refs[n:2 * n]
        send_sems, recv_sems = refs[2 * n:]
        x, y, c, _, _ = _place()
        cps = [pltpu.make_async_remote_copy(
            src_ref=srcs[i].at[q, 1 - c, rs], dst_ref=outs[i].at[q, rs], send_sem=send_sems.at[u, q],
            recv_sem=recv_sems.at[u, q], device_id=(x, y, 1 - c), device_id_type=MESH)
            for q in range(N_CHIPS) for u, (i, rs) in enumerate(units)]
        for cpy in cps:
            cpy.start()
        for cpy in cps:
            cpy.wait()

    return pl.pallas_call(
        body, name="swap_with_sibling", in_specs=[HBM_SPEC] * n, out_specs=[HBM_SPEC] * n,
        out_shape=[jax.ShapeDtypeStruct(a.shape[:1] + a.shape[2:], a.dtype) for a in parts],
        scratch_shapes=[pltpu.SemaphoreType.DMA((len(units), N_CHIPS)),
                        pltpu.SemaphoreType.DMA((len(units), N_CHIPS))],
    )(*parts)


def _scatter_to_owners(parts):
    n = len(parts)
    units = _units(parts, 1)

    def copies(srcs, outs, send_sems, recv_sems, incoming):
        x, y, c, p, chips = _place()
        return [pltpu.make_async_remote_copy(
            src_ref=srcs[i].at[2 * cx + cy, rs], dst_ref=outs[i].at[(2 * cx + cy) if incoming else p, rs],
            send_sem=send_sems.at[u, j], recv_sem=recv_sems.at[u, j], device_id=(cx, cy, c), device_id_type=MESH)
            for u, (i, rs) in enumerate(units) for j, (cx, cy) in enumerate(chips)]

    def start(ins, outs, scratch):
        for cpy in copies(ins, outs, *scratch, incoming=False):
            cpy.start()

    def wait(ins, outs, scratch):
        for cpy in copies(ins, outs, *scratch, incoming=True):
            cpy.wait_recv()
        for cpy in copies(ins, outs, *scratch, incoming=False):
            cpy.wait_send()

    return dict(ins=list(parts), outs=[jax.ShapeDtypeStruct(a.shape, a.dtype) for a in parts],
                scratch=[pltpu.SemaphoreType.DMA((len(units), 3)), pltpu.SemaphoreType.DMA((len(units), 3))],
                start=start, wait=wait)


def _share_with_sibling(parts):
    n = len(parts)
    units = _units(parts, 1)

    def body(*refs):
        srcs, outs = refs[:n], refs[n:2 * n]
        send_sems, recv_sems = refs[2 * n:]
        x, y, c, _, _ = _place()
        sends = [pltpu.make_async_remote_copy(
            src_ref=srcs[i].at[0, rs], dst_ref=outs[i].at[c, rs], send_sem=send_sems.at[u],
            recv_sem=recv_sems.at[u], device_id=(x, y, 1 - c), device_id_type=MESH)
            for u, (i, rs) in enumerate(units)]
        for cpy in sends:
            cpy.start()
        for u, (i, rs) in enumerate(units):
            pltpu.make_async_remote_copy(
                src_ref=srcs[i].at[0, rs], dst_ref=outs[i].at[1 - c, rs], send_sem=send_sems.at[u],
                recv_sem=recv_sems.at[u], device_id=(x, y, 1 - c), device_id_type=MESH).wait_recv()
        for cpy in sends:
            cpy.wait_send()

    return pl.pallas_call(
        body, name="share_with_sibling", in_specs=[HBM_SPEC] * n, out_specs=[HBM_SPEC] * n,
        out_shape=[jax.ShapeDtypeStruct((2,) + a.shape[1:], a.dtype) for a in parts],
        scratch_shapes=[pltpu.SemaphoreType.DMA((len(units),)), pltpu.SemaphoreType.DMA((len(units),))],
    )(*parts)


def _sum_small(v):
    def body(v_ref, out_ref, buf, send_sems, recv_sems):
        x, y, c, _, _ = _place()
        me = 4 * x + 2 * y + c
        buf[me] = v_ref[...]
        flips = [(dx, dy, dc) for dx in (0, 1) for dy in (0, 1) for dc in (0, 1)][1:]
        sends = []
        for k, (dx, dy, dc) in enumerate(flips):
            cpy = pltpu.make_async_remote_copy(
                src_ref=v_ref, dst_ref=buf.at[me], send_sem=send_sems.at[k], recv_sem=recv_sems.at[k],
                device_id=((x + dx) % 2, (y + dy) % 2, (c + dc) % 2), device_id_type=MESH)
            cpy.start()
            sends.append(cpy)
        for k, (dx, dy, dc) in enumerate(flips):
            px, py, pc = (x + dx) % 2, (y + dy) % 2, (c + dc) % 2
            pltpu.make_async_remote_copy(
                src_ref=v_ref, dst_ref=buf.at[4 * px + 2 * py + pc], send_sem=send_sems.at[k],
                recv_sem=recv_sems.at[k], device_id=(px, py, pc), device_id_type=MESH).wait_recv()
        for cpy in sends:
            cpy.wait_send()
        tot = buf[0]
        for i in range(1, N_DEV):
            tot = tot + buf[i]
        out_ref[...] = tot

    return pl.pallas_call(
        body, name="sum_small", out_shape=jax.ShapeDtypeStruct(v.shape, v.dtype),
        in_specs=[pl.BlockSpec(memory_space=pltpu.VMEM)], out_specs=pl.BlockSpec(memory_space=pltpu.VMEM),
        scratch_shapes=[pltpu.VMEM((N_DEV,) + v.shape, v.dtype), pltpu.SemaphoreType.DMA((N_DEV - 1,)),
                        pltpu.SemaphoreType.DMA((N_DEV - 1,))],
    )(v)


def _add_chips(name, landed, pair, chip):
    nq, r, w = landed.shape
    tr = 64

    def body(chip_ref, *refs):
        own = refs[nq][...].astype(F32)
        tot = None
        for q in range(nq):
            term = jnp.where(chip_ref[0] == q, own, refs[q][...].astype(F32))
            tot = term if tot is None else tot + term
        refs[nq + 1][...] = tot

    specs = [pl.BlockSpec((None, tr, w), functools.partial(lambda j, chip_ref, q: (q, j, 0), q=q)) for q in range(nq)]
    specs.append(pl.BlockSpec((None, tr, w), lambda j, chip_ref: (chip_ref[0], j, 0)))
    grid_spec = pltpu.PrefetchScalarGridSpec(
        num_scalar_prefetch=1, grid=(r // tr,), in_specs=specs,
        out_specs=pl.BlockSpec((None, tr, w), lambda j, chip_ref: (0, j, 0)))
    return pl.pallas_call(
        body, name=name, grid_spec=grid_spec, out_shape=jax.ShapeDtypeStruct((1, r, w), F32),
        compiler_params=_params(("parallel",)),
    )(jnp.reshape(chip, (1,)).astype(jnp.int32), *([landed] * nq), pair)


def _add_pair(name, halves, got, c):
    nq, _, r, w = halves.shape
    tr = 64

    def body(c_ref, a_ref, b_ref, o_ref):
        o_ref[...] = (a_ref[...] + b_ref[...]).astype(o_ref.dtype)

    grid_spec = pltpu.PrefetchScalarGridSpec(
        num_scalar_prefetch=1, grid=(nq, r // tr),
        in_specs=[pl.BlockSpec((None, None, tr, w), lambda i, j, c_ref: (i, c_ref[0], j, 0)),
                  pl.BlockSpec((None, tr, w), lambda i, j, c_ref: (i, j, 0))],
        out_specs=pl.BlockSpec((None, tr, w), lambda i, j, c_ref: (i, j, 0)))
    return pl.pallas_call(
        body, name=name, grid_spec=grid_spec, out_shape=jax.ShapeDtypeStruct((nq, r, w), BF16),
        compiler_params=_params(("parallel", "parallel")),
    )(jnp.reshape(c, (1,)).astype(jnp.int32), halves, got)


def _adamw(name, w, g, m, v, tm):
    def fn(wv, gv, mv, vv):
        m2 = ADAM_B1 * mv + (1.0 - ADAM_B1) * gv
        v2 = ADAM_B2 * vv + (1.0 - ADAM_B2) * (gv * gv)
        m_hat = m2 / (1.0 - ADAM_B1 ** ADAM_STEP)
        v_hat = v2 / (1.0 - ADAM_B2 ** ADAM_STEP)
        return -ADAM_LR * (m_hat / (jnp.sqrt(v_hat) + ADAM_EPS) + ADAM_WD * wv), m2, v2
    c = w.shape[1]
    return _rows(name, fn, [w, g, m, v], [], [(c, F32)] * 3, tm=tm)


REST_ROWS = 256 + 3 * 128 + 256
REST_SPLITS = (("w_mem_kv", 0, 256), ("w_branch_a", 256, 128), ("w_branch_b", 384, 128),
               ("w_branch_m", 512, 128), ("w_out", 640, 256))


def _rest_pack(t):
    return jnp.concatenate([t[n].reshape(rows, D_MODEL) for n, _, rows in REST_SPLITS], axis=0)


def _rest_unpack(a, shapes):
    return {n: a[r0:r0 + rows].reshape(shapes[n]) for n, r0, rows in REST_SPLITS}


def _small_pack(pre, post, memg, bforget, bmerge):
    pad = jnp.zeros((1, D_MODEL - B_HEADS), F32)
    return jnp.concatenate([pre, post, memg, bmerge.reshape(3, D_MODEL),
                            jnp.concatenate([bforget, pad], axis=1), jnp.zeros((1, D_MODEL), F32)], axis=0)


def _small_unpack(s8):
    return dict(norm_pre_g=s8[0:1], norm_post_g=s8[1:2], norm_mem_g=s8[2:3],
                b_merge=s8[3:6].reshape(1, 3 * D_MODEL), b_forget=s8[6:7, :B_HEADS])


WEIGHTS = ("norm_pre_g", "norm_post_g", "norm_mem_g", "w_in", "b_forget", "b_merge", "w_mem_kv",
           "w_branch_a", "w_branch_b", "w_branch_m", "w_out")
SMALL = ("norm_pre_g", "norm_post_g", "norm_mem_g", "b_forget", "b_merge")


def kernel(x, mem, positions, norm_pre_g, norm_post_g, norm_mem_g, w_in, b_forget, b_merge, w_mem_kv, w_branch_a, w_branch_b, w_branch_m, w_out, loss_target, m_norm_pre_g, m_norm_post_g, m_norm_mem_g, m_w_in, m_b_forget, m_b_merge, m_w_mem_kv, m_w_branch_a, m_w_branch_b, m_w_branch_m, m_w_out, v_norm_pre_g, v_norm_post_g, v_norm_mem_g, v_w_in, v_b_forget, v_b_merge, v_w_mem_kv, v_w_branch_a, v_w_branch_b, v_w_branch_m, v_w_out):
    w = dict(norm_pre_g=norm_pre_g, norm_post_g=norm_post_g, norm_mem_g=norm_mem_g, w_in=w_in[0],
             b_forget=b_forget, b_merge=b_merge, w_mem_kv=w_mem_kv[0], w_branch_a=w_branch_a[0],
             w_branch_b=w_branch_b[0], w_branch_m=w_branch_m[0], w_out=w_out[0])
    mo = dict(norm_pre_g=m_norm_pre_g, norm_post_g=m_norm_post_g, norm_mem_g=m_norm_mem_g, w_in=m_w_in[0],
              b_forget=m_b_forget, b_merge=m_b_merge, w_mem_kv=m_w_mem_kv[0], w_branch_a=m_w_branch_a[0],
              w_branch_b=m_w_branch_b[0], w_branch_m=m_w_branch_m[0], w_out=m_w_out[0])
    vo = dict(norm_pre_g=v_norm_pre_g, norm_post_g=v_norm_post_g, norm_mem_g=v_norm_mem_g, w_in=v_w_in[0],
              b_forget=v_b_forget, b_merge=v_b_merge, w_mem_kv=v_w_mem_kv[0], w_branch_a=v_w_branch_a[0],
              w_branch_b=v_w_branch_b[0], w_branch_m=v_w_branch_m[0], w_out=v_w_out[0])
    s = x.shape[1]
    c = lax.axis_index("c")

    chip = 2 * lax.axis_index("x") + lax.axis_index("y")

    def put(whole, own, slot):
        return lax.dynamic_update_index_in_dim(whole, own.astype(whole.dtype), slot, 0)

    own_w = [w["w_in"].astype(BF16).reshape(2, D_MODEL // 2, SHARD_COLS),
             _rest_pack(w).astype(BF16).reshape(2, REST_ROWS // 2, D_MODEL)]
    all_in, all_rest = _gather_weights(own_w)
    all_in = all_in.reshape(N_CHIPS, D_MODEL, SHARD_COLS)
    own_in, own_rest = own_w[0].reshape(D_MODEL, SHARD_COLS), own_w[1].reshape(REST_ROWS, D_MODEL)
    w_in_f = jnp.concatenate([jnp.where(chip == p, own_in, all_in[p]) for p in range(N_CHIPS)], axis=1)
    all_rest = all_rest.reshape(N_CHIPS, REST_ROWS, D_MODEL)
    all_rest = jnp.stack([jnp.where(chip == p, own_rest, all_rest[p]) for p in range(N_CHIPS)])
    w_kv_f = all_rest[:, 0:256].reshape(D_MODEL, D_MODEL)
    w_br_f = [all_rest[:, 256 + 128 * i:384 + 128 * i].reshape(N_CHIPS, 512, 256).transpose(1, 0, 2)
              .reshape(512, D_MODEL) for i in range(3)]
    w_out_f = all_rest[:, 640:896].reshape(D_MODEL, D_MODEL)
    w_main = jnp.concatenate([w_in_f[:, :FB_ORIG], w_in_f[:, FB_ORIG + B_HEADS:]], axis=1)
    w_fb = jnp.concatenate([w_in_f[:, FB_ORIG:FB_ORIG + B_HEADS], jnp.zeros((D_MODEL, HD - B_HEADS), BF16)], axis=1)

    pair = []

    def exchange(g):
        def per_chip(name, p):
            a = g[name]
            if name in ("w_mem_kv", "w_out"):
                return a[256 * p:256 * (p + 1)]
            return a[:, 256 * p:256 * (p + 1)]

        in4 = jnp.stack([g["w_in"][:, SHARD_COLS * p:SHARD_COLS * (p + 1)] for p in range(N_CHIPS)])
        rest4 = jnp.stack([_rest_pack({n: per_chip(n, p) for n, _, _ in REST_SPLITS}) for p in range(N_CHIPS)])
        halves = [in4.reshape(N_CHIPS, 2, D_MODEL // 2, SHARD_COLS),
                  rest4.reshape(N_CHIPS, 2, REST_ROWS // 2, D_MODEL)]
        got = _swap_with_sibling(halves)
        pair.extend(_add_pair("add_pair_%d" % i, halves[i], got[i], c) for i in range(2))
        return _scatter_to_owners(pair)

    loss_lanes, grad_x, g, landed = _local_step(
        x[0], mem[0], positions.reshape(s, 1), loss_target[0], norm_pre_g, norm_post_g, norm_mem_g,
        w_main, w_fb, b_forget, b_merge, w_kv_f, w_br_f[0], w_br_f[1], w_br_f[2], w_out_f, exchange)
    loss = lax.psum(jnp.sum(loss_lanes), ("x", "y", "c"))
    half = [_add_chips("add_chips_%d" % i, landed[i], pair[i], chip) for i in range(2)]
    red_in, red_rest = [put(a, o[0], c) for a, o in zip(_share_with_sibling(half), half)]
    gs = {"w_in": red_in.reshape(D_MODEL, SHARD_COLS)}
    gs.update(_rest_unpack(red_rest.reshape(REST_ROWS, D_MODEL), {n: w[n].shape for n, _, _ in REST_SPLITS}))
    gs.update(_small_unpack(_sum_small(_small_pack(
        g["norm_pre_g"], g["norm_post_g"], g["norm_mem_g"], g["b_forget"], g["b_merge"]))))

    delta, new_m, new_v = {}, {}, {}
    for n, tm in (("w_in", 128), ("w_mem_kv", 256), ("w_branch_a", 512), ("w_branch_b", 512),
                  ("w_branch_m", 512), ("w_out", 256)):
        d_, m_, v_ = _adamw("adamw_" + n, w[n], gs[n], mo[n], vo[n], tm)
        delta[n], new_m[n], new_v[n] = d_[None], m_[None], v_[None]
        gs[n] = gs[n][None]
    packs = [_small_pack(*[t[n] for n in SMALL])
             for t in (w, gs, mo, vo)]
    for res, store in zip(_adamw("adamw_small", *packs, 8), (delta, new_m, new_v)):
        store.update(_small_unpack(res))

    return (loss, grad_x[None], *[gs[n] for n in WEIGHTS], *[delta[n] for n in WEIGHTS],
            *[new_m[n] for n in WEIGHTS], *[new_v[n] for n in WEIGHTS])
```

```python
import functools

import jax
import jax.numpy as jnp
from jax import lax
from jax.experimental import pallas as pl
from jax.experimental.pallas import tpu as pltpu

F32 = jnp.float32
BF16 = jnp.bfloat16
MESH = pl.DeviceIdType.MESH

D_MODEL = 1024
N_MEM = 256
EPS = 1e-6
NEG = -1e30
ROPE_THETA = 500000.0
ROT_DIM = 32
HD = 128
A_GROUP = 512
DILATIONS = (1, 4, 16)
BAND = 128
B_HEADS = 8
B_HD = 64
N_CHIPS = 4
N_DEV = 8

C_QA, C_KA, C_VA, C_ZA = 0, 1536, 3072, 4608
C_QB, C_KB, C_VB, C_ZB = 5120, 5632, 6144, 6656
C_QM, C_ZM, C_GL = 7168, 7680, 8192
FB_ORIG = 6656
IN_COLS = 11272
SHARD_COLS = IN_COLS // N_CHIPS

ADAM_LR, ADAM_B1, ADAM_B2, ADAM_EPS, ADAM_WD, ADAM_STEP = 0.001, 0.9, 0.999, 1e-08, 0.01, 10

VMEM_LIMIT_V7X = 56 * 1024 * 1024

NT = (((1,), (1,)), ((), ()))
NN = (((1,), (0,)), ((), ()))
TN = (((0,), (0,)), ((), ()))


def _params(sem):
    return pltpu.CompilerParams(dimension_semantics=sem, vmem_limit_bytes=VMEM_LIMIT_V7X)


def _dot(a, b, dn=NN):
    return lax.dot_general(a, b, dn, preferred_element_type=F32)


def _sig(z):
    return 1.0 / (1.0 + jnp.exp(-z))


def _rows(name, fn, row_ins, bc_ins, outs, reds=(), tm=512, scratch=(), into=None):
    arrs, specs = [], []
    s = None
    for r in row_ins:
        arr, w, cb, d = (tuple(r) + (1,))[:4] if isinstance(r, tuple) else (r, r.shape[1], 0, 1)
        s = arr.shape[0] * d if s is None else s
        arrs.append(arr)
        specs.append((w, cb, d))
    tm = min(tm, s)
    specs = [pl.BlockSpec((tm // d, w), functools.partial(lambda i, cb: (i, cb), cb=cb)) for w, cb, d in specs]
    for b in bc_ins:
        arrs.append(b)
        specs.append(pl.BlockSpec(b.shape, lambda i: (0, 0)))
    outs = [(tuple(o) + (1,))[:3] for o in outs]
    n_in, n_out = len(arrs), len(outs)
    o0 = n_in + (0 if into is None else 1)

    def body(*refs):
        n_ref = o0 + n_out + len(reds)
        vals = fn(*[r[...] for r in refs[:n_in]], *refs[n_ref:])
        if not isinstance(vals, (tuple, list)):
            vals = (vals,)
        for r, v in zip(refs[o0:o0 + n_out], vals[:n_out]):
            r[...] = v.astype(r.dtype)
        if reds:
            red_refs = refs[o0 + n_out:n_ref]

            @pl.when(pl.program_id(0) == 0)
            def _():
                for r in red_refs:
                    r[...] = jnp.zeros_like(r)

            for r, v in zip(red_refs, vals[n_out:]):
                r[...] += v

    out_shape = [jax.ShapeDtypeStruct((s // d, c), dt) for c, dt, d in outs]
    out_shape += [jax.ShapeDtypeStruct((1, c), F32) for c in reds]
    out_specs = [pl.BlockSpec((tm // d, c), lambda i: (i, 0)) for c, _, d in outs]
    out_specs += [pl.BlockSpec((1, c), lambda i: (0, 0)) for c in reds]
    aliases = {}
    if into is not None:
        whole, k, cb = into
        out_shape[k] = jax.ShapeDtypeStruct(whole.shape, whole.dtype)
        out_specs[k] = pl.BlockSpec((tm, outs[k][0]), functools.partial(lambda i, cb: (i, cb), cb=cb))
        aliases = {n_in: k}
        arrs.append(whole)
        specs.append(pl.BlockSpec(memory_space=pl.ANY))
    res = pl.pallas_call(
        body, name=name, grid=(s // tm,), in_specs=specs, out_specs=out_specs, out_shape=out_shape,
        scratch_shapes=list(scratch), input_output_aliases=aliases,
        compiler_params=_params(("arbitrary",) if reds else ("parallel",)),
    )(*arrs)
    return res


def _to_class(x, scr, d):
    if d == 1:
        return x.astype(F32)
    tm, c = x.shape
    for g in range(c // 128):
        scr[g][...] = x[:, g * 128:(g + 1) * 128].astype(F32)
    return jnp.concatenate([scr[g][pl.ds(r, tm // d, stride=d), :] for r in range(d) for g in range(c // 128)],
                           axis=1)


def _from_class(x, scr, d):
    if d == 1:
        return x.astype(F32)
    n, dc = x.shape
    c = dc // d
    for r in range(d):
        for g in range(c // 128):
            scr[g][pl.ds(r, n, stride=d), :] = x[:, r * c + g * 128:r * c + (g + 1) * 128].astype(F32)
    return jnp.concatenate([scr[g][...] for g in range(c // 128)], axis=1)


def _mm(name, a, b, mode, out_dtype, tm=1024, tn=1024, tk=1024, side=None):
    if mode == "nn":
        (m, k), (_, n) = a.shape, b.shape
    elif mode == "nt":
        (m, k), (n, _) = a.shape, b.shape
    else:
        (k, m), (_, n) = a.shape, b.shape
    tm, tn, tk = min(tm, m), min(tn, n), min(tk, k)
    nk = k // tk
    grid = (m // tm, n // tn, nk)
    dn = {"nn": NN, "nt": NT, "tn": TN}[mode]
    n_si = len(side["ins"]) if side else 0
    n_so = len(side["outs"]) if side else 0
    n_acc = 1 if nk > 1 else 0

    def body(*refs):
        a_ref, b_ref = refs[:2]
        side_in, o_ref = refs[2:2 + n_si], refs[2 + n_si]
        side_out = refs[3 + n_si:3 + n_si + n_so]
        acc = refs[3 + n_si + n_so:3 + n_si + n_so + n_acc]
        side_scratch = refs[3 + n_si + n_so + n_acc:]
        step = (pl.program_id(0) * grid[1] + pl.program_id(1)) * grid[2] + pl.program_id(2)
        if side:
            @pl.when(step == 0)
            def _():
                side["start"](side_in, side_out, side_scratch)

        part = _dot(a_ref[...].astype(BF16), b_ref[...].astype(BF16), dn)
        if nk == 1:
            o_ref[...] = part.astype(o_ref.dtype)
        else:
            kk = pl.program_id(2)

            @pl.when(kk == 0)
            def _():
                acc[0][...] = part

            @pl.when(kk > 0)
            def _():
                acc[0][...] += part

            @pl.when(kk == nk - 1)
            def _():
                o_ref[...] = acc[0][...].astype(o_ref.dtype)

        if side:
            @pl.when(step == grid[0] * grid[1] * grid[2] - 1)
            def _():
                side["wait"](side_in, side_out, side_scratch)

    a_spec = (pl.BlockSpec((tk, tm), lambda i, j, kk: (kk, i)) if mode == "tn"
              else pl.BlockSpec((tm, tk), lambda i, j, kk: (i, kk)))
    b_spec = (pl.BlockSpec((tn, tk), lambda i, j, kk: (j, kk)) if mode == "nt"
              else pl.BlockSpec((tk, tn), lambda i, j, kk: (kk, j)))
    o_spec = pl.BlockSpec((tm, tn), lambda i, j, kk: (i, j))
    o_shape = jax.ShapeDtypeStruct((m, n), out_dtype)
    acc_scratch = [pltpu.VMEM((tm, tn), F32)] * n_acc
    if not side:
        return pl.pallas_call(
            body, name=name, grid=grid, in_specs=[a_spec, b_spec], out_specs=o_spec, out_shape=o_shape,
            scratch_shapes=acc_scratch, compiler_params=_params(("parallel", "parallel", "arbitrary")),
        )(a, b)
    return pl.pallas_call(
        body, name=name, grid=grid, in_specs=[a_spec, b_spec] + [HBM_SPEC] * n_si,
        out_specs=[o_spec] + [HBM_SPEC] * n_so, out_shape=[o_shape] + side["outs"],
        scratch_shapes=acc_scratch + side["scratch"],
        compiler_params=_params(("arbitrary", "arbitrary", "arbitrary")),
    )(a, b, *side["ins"])


def _rms_fwd(name, x, g):
    def fn(xv, gv):
        r = lax.rsqrt(jnp.mean(xv * xv, axis=-1, keepdims=True) + EPS)
        return (xv * r * gv,)
    return _rows(name, fn, [x], [g], [(x.shape[1], BF16)], tm=min(512, x.shape[0]))[0]


def _rope_tables(pos, inv):
    ang = pos.astype(F32) * inv
    lane = lax.broadcasted_iota(jnp.int32, ang.shape, 1)
    c = jnp.where(lane < ROT_DIM, jnp.cos(ang), 1.0)
    sn = jnp.sin(ang)
    sg = jnp.where(lane < ROT_DIM // 2, -sn, jnp.where(lane < ROT_DIM, sn, 0.0))
    return c, sg, lane


def _rope_apply(x, c, sg, lane):
    outs = []
    for h in range(x.shape[1] // HD):
        xh = x[:, h * HD:(h + 1) * HD].astype(F32)
        swap = jnp.where(lane < ROT_DIM // 2, pltpu.roll(xh, HD - ROT_DIM // 2, 1),
                         pltpu.roll(xh, ROT_DIM // 2, 1))
        outs.append(xh * c + swap * sg)
    return jnp.concatenate(outs, axis=1)


ROPE_TM = 256


def _class_scratch(tm):
    return [pltpu.VMEM((tm, 128), F32) for _ in range(A_GROUP // 128)]


def _rope_fwd(u, pos, inv):
    def fn(q, k, v, p, iv, *scr):
        c, sg, lane = _rope_tables(p, iv)
        qr, kr = _rope_apply(q, c, sg, lane), _rope_apply(k, c, sg, lane)
        outs = []
        for g, d in enumerate(DILATIONS):
            gs = slice(g * A_GROUP, (g + 1) * A_GROUP)
            outs += [_to_class(qr[:, gs], scr, d), _to_class(kr[:, gs], scr, d), _to_class(v[:, gs], scr, d)]
        return tuple(outs)

    outs = [(d * A_GROUP, BF16, d) for d in DILATIONS for _ in range(3)]
    qkv = [(u, 3 * A_GROUP, c0 // (3 * A_GROUP)) for c0 in (C_QA, C_KA, C_VA)]
    return _rows("rope_fwd", fn, qkv + [pos], [inv], outs, tm=ROPE_TM,
                 scratch=_class_scratch(ROPE_TM))


def _rope_bwd(dqs, dks, dvs, pos, inv, du):
    s = du.shape[0]
    tm = ROPE_TM
    n_g = len(DILATIONS)

    def body(*refs):
        grads, p_ref, iv_ref = refs[:3 * n_g], refs[3 * n_g], refs[3 * n_g + 1]
        o_ref, scr = refs[3 * n_g + 3], refs[3 * n_g + 4:]
        part = pl.program_id(0)
        for k in range(3):
            @pl.when(part == k)
            def _(k=k):
                tok = jnp.concatenate([_from_class(grads[n_g * k + g][...], scr, d)
                                       for g, d in enumerate(DILATIONS)], axis=1)
                if k < 2:
                    c, sg, lane = _rope_tables(p_ref[...], iv_ref[...])
                    tok = _rope_apply(tok, c, -sg, lane)
                o_ref[...] = tok.astype(o_ref.dtype)

    in_specs = [pl.BlockSpec((tm // d, d * A_GROUP),
                             functools.partial(lambda p, i, k: (jnp.where(p == k, i, 0), 0), k=k))
                for k in range(3) for d in DILATIONS]
    in_specs += [pl.BlockSpec((tm, 1), lambda p, i: (i, 0)), pl.BlockSpec(inv.shape, lambda p, i: (0, 0)),
                 pl.BlockSpec(memory_space=pl.ANY)]
    return pl.pallas_call(
        body, name="rope_bwd", grid=(3, s // tm), in_specs=in_specs,
        out_specs=pl.BlockSpec((tm, 3 * A_GROUP), lambda p, i: (i, p)),
        out_shape=jax.ShapeDtypeStruct(du.shape, du.dtype), scratch_shapes=_class_scratch(tm),
        input_output_aliases={3 * n_g + 2: 0},
        compiler_params=_params(("arbitrary", "arbitrary")),
    )(*dqs, *dks, *dvs, pos, inv, du)


def _lane_pack(cols, like):
    lane = lax.broadcasted_iota(jnp.int32, like, 1)
    out = jnp.zeros(like, F32)
    for h, cvec in enumerate(cols):
        out = jnp.where(lane == h, cvec, out)
    return out


def _band_specs(l, d, tq):
    nsb = tq // BAND
    nblk = l // BAND
    cur = pl.BlockSpec((tq, A_GROUP), lambda r, i: (i, r))
    prev = pl.BlockSpec((BAND, A_GROUP), lambda r, i: (jnp.maximum(i * nsb - 1, 0), r))
    nxt = pl.BlockSpec((BAND, A_GROUP), lambda r, i: (jnp.minimum((i + 1) * nsb, nblk - 1), r))
    st_cur = pl.BlockSpec((tq, HD), lambda r, i: (i, r))
    st_nxt = pl.BlockSpec((BAND, HD), lambda r, i: (jnp.minimum((i + 1) * nsb, nblk - 1), r))
    return nsb, cur, prev, nxt, st_cur, st_nxt


def _band_mask_q(i, first_tile):
    qr = lax.broadcasted_iota(jnp.int32, (BAND, 2 * BAND), 0)
    kc = lax.broadcasted_iota(jnp.int32, (BAND, 2 * BAND), 1)
    in_prev = (kc < BAND) & (kc >= qr)
    in_cur = (kc >= BAND) & (kc - BAND <= qr)
    if i == 0:
        in_prev = in_prev & jnp.logical_not(first_tile)
    return in_prev | in_cur


def _band_mask_k(j, nsb, last_tile):
    kc = lax.broadcasted_iota(jnp.int32, (BAND, 2 * BAND), 0)
    qr = lax.broadcasted_iota(jnp.int32, (BAND, 2 * BAND), 1)
    same = (qr < BAND) & (kc <= qr)
    nxt = (qr >= BAND) & (kc >= qr - BAND)
    if j == nsb - 1:
        nxt = nxt & jnp.logical_not(last_tile)
    return same | nxt


def _band_fwd(name, q, k, v, d):
    l = q.shape[0]
    tq = min(512, l)
    nsb, cur, prev, _, st_cur, _ = _band_specs(l, d, tq)
    scale = HD ** -0.5

    def body(q_ref, kc_ref, kp_ref, vc_ref, vp_ref, o_ref, lse_ref):
        first = pl.program_id(1) == 0
        for i in range(nsb):
            lses = []
            mask = _band_mask_q(i, first)
            for h in range(4):
                cs = slice(h * HD, (h + 1) * HD)
                qv = q_ref[i * BAND:(i + 1) * BAND, cs]
                if i == 0:
                    kk = jnp.concatenate([kp_ref[:, cs], kc_ref[0:BAND, cs]], axis=0)
                    vv = jnp.concatenate([vp_ref[:, cs], vc_ref[0:BAND, cs]], axis=0)
                else:
                    kk = kc_ref[(i - 1) * BAND:(i + 1) * BAND, cs]
                    vv = vc_ref[(i - 1) * BAND:(i + 1) * BAND, cs]
                s = jnp.where(mask, _dot(qv, kk, NT) * scale, NEG)
                m = jnp.max(s, axis=-1, keepdims=True)
                p = jnp.exp(s - m)
                den = jnp.sum(p, axis=-1, keepdims=True)
                o_ref[i * BAND:(i + 1) * BAND, cs] = _dot(p.astype(BF16), vv) / den
                lses.append(m + jnp.log(den))
            lse_ref[i * BAND:(i + 1) * BAND, :] = _lane_pack(lses, (BAND, HD))

    return pl.pallas_call(
        body, name=name, grid=(d, l // tq), in_specs=[cur, cur, prev, cur, prev],
        out_specs=[cur, st_cur],
        out_shape=[jax.ShapeDtypeStruct((l, d * A_GROUP), F32), jax.ShapeDtypeStruct((l, d * HD), F32)],
        compiler_params=_params(("parallel", "parallel")),
    )(q, k, k, v, v)


def _band_dq(name, q, k, v, dy, lse, delta, d):
    l = q.shape[0]
    tq = min(512, l)
    nsb, cur, prev, _, st_cur, _ = _band_specs(l, d, tq)
    scale = HD ** -0.5

    def body(q_ref, kc_ref, kp_ref, vc_ref, vp_ref, dy_ref, lse_ref, dl_ref, dq_ref):
        first = pl.program_id(1) == 0
        for i in range(nsb):
            mask = _band_mask_q(i, first)
            rs = slice(i * BAND, (i + 1) * BAND)
            for h in range(4):
                cs = slice(h * HD, (h + 1) * HD)
                if i == 0:
                    kk = jnp.concatenate([kp_ref[:, cs], kc_ref[0:BAND, cs]], axis=0)
                    vv = jnp.concatenate([vp_ref[:, cs], vc_ref[0:BAND, cs]], axis=0)
                else:
                    kk = kc_ref[(i - 1) * BAND:(i + 1) * BAND, cs]
                    vv = vc_ref[(i - 1) * BAND:(i + 1) * BAND, cs]
                s = jnp.where(mask, _dot(q_ref[rs, cs], kk, NT) * scale, NEG)
                p = jnp.exp(s - lse_ref[rs, h:h + 1])
                dp = _dot(dy_ref[rs, cs], vv, NT)
                ds = p * (dp - dl_ref[rs, h:h + 1])
                dq_ref[rs, cs] = (_dot(ds.astype(BF16), kk) * scale).astype(dq_ref.dtype)

    return pl.pallas_call(
        body, name=name, grid=(d, l // tq),
        in_specs=[cur, cur, prev, cur, prev, cur, st_cur, st_cur], out_specs=cur,
        out_shape=jax.ShapeDtypeStruct((l, d * A_GROUP), BF16),
        compiler_params=_params(("parallel", "parallel")),
    )(q, k, k, v, v, dy, lse, delta)


def _band_dkv(name, q, k, v, dy, lse, delta, d):
    l = q.shape[0]
    tq = min(512, l)
    nsb, cur, _, nxt, st_cur, st_nxt = _band_specs(l, d, tq)
    scale = HD ** -0.5
    ntile = l // tq

    def body(k_ref, v_ref, qc_ref, qn_ref, dyc_ref, dyn_ref, lc_ref, ln_ref, dc_ref, dn_ref,
             dk_ref, dv_ref):
        last = pl.program_id(1) == ntile - 1

        def win(c_ref, n_ref, j, cs):
            if j == nsb - 1:
                return jnp.concatenate([c_ref[j * BAND:(j + 1) * BAND, cs], n_ref[:, cs]], axis=0)
            return c_ref[j * BAND:(j + 2) * BAND, cs]

        allh = slice(0, HD)
        for j in range(nsb):
            mask = _band_mask_k(j, nsb, last)
            rs = slice(j * BAND, (j + 1) * BAND)
            lse_t = win(lc_ref, ln_ref, j, allh).T
            delta_t = win(dc_ref, dn_ref, j, allh).T
            for h in range(4):
                cs = slice(h * HD, (h + 1) * HD)
                qw = win(qc_ref, qn_ref, j, cs)
                dyw = win(dyc_ref, dyn_ref, j, cs)
                st = jnp.where(mask, _dot(k_ref[rs, cs], qw, NT) * scale, NEG)
                pt = jnp.exp(st - lse_t[h:h + 1, :])
                dst = pt * (_dot(v_ref[rs, cs], dyw, NT) - delta_t[h:h + 1, :])
                dv_ref[rs, cs] = _dot(pt.astype(BF16), dyw).astype(dv_ref.dtype)
                dk_ref[rs, cs] = (_dot(dst.astype(BF16), qw) * scale).astype(dk_ref.dtype)

    shp = jax.ShapeDtypeStruct((l, d * A_GROUP), BF16)
    return pl.pallas_call(
        body, name=name, grid=(d, ntile),
        in_specs=[cur, cur, cur, nxt, cur, nxt, st_cur, st_nxt, st_cur, st_nxt],
        out_specs=[cur, cur], out_shape=[shp, shp],
        compiler_params=_params(("parallel", "parallel")),
    )(k, v, q, q, dy, dy, lse, lse, delta, delta)


def _split3(x):
    hi = x.astype(BF16)
    r1 = x - hi.astype(F32)
    mid = r1.astype(BF16)
    lo = (r1 - mid.astype(F32)).astype(BF16)
    return hi, mid, lo


def _fox_prep(z, b):
    h, s = z.shape
    blk = min(512, s)

    def body(z_ref, b_ref, c_ref):
        r = lax.broadcasted_iota(jnp.int32, (blk, blk), 0)
        cidx = lax.broadcasted_iota(jnp.int32, (blk, blk), 1)
        tri = (r <= cidx).astype(BF16)
        carry = jnp.zeros((h, 1), F32)
        for t in range(s // blk):
            zz = z_ref[:, t * blk:(t + 1) * blk] + b_ref[...]
            lf = jnp.minimum(zz, 0.0) - jnp.log(1.0 + jnp.exp(-jnp.abs(zz)))
            hi, mid, lo = _split3(lf)
            cs = _dot(hi, tri) + _dot(mid, tri) + _dot(lo, tri) + carry
            c_ref[:, t * blk:(t + 1) * blk] = cs
            carry = cs[:, blk - 1:blk]

    return pl.pallas_call(body, name="fox_prep", out_shape=jax.ShapeDtypeStruct((h, s), F32))(z, b)


def _fox_prep_bwd(dc, z, b):
    h, s = z.shape
    blk = min(512, s)

    def body(dc_ref, z_ref, b_ref, dz_ref, db_ref):
        r = lax.broadcasted_iota(jnp.int32, (blk, blk), 0)
        cidx = lax.broadcasted_iota(jnp.int32, (blk, blk), 1)
        tri = (r >= cidx).astype(BF16)
        carry = jnp.zeros((h, 1), F32)
        tot = jnp.zeros((h, 1), F32)
        for t in reversed(range(s // blk)):
            hi, mid, lo = _split3(dc_ref[:, t * blk:(t + 1) * blk])
            rc = _dot(hi, tri) + _dot(mid, tri) + _dot(lo, tri) + carry
            carry = rc[:, 0:1]
            zz = z_ref[:, t * blk:(t + 1) * blk] + b_ref[...]
            dz = rc * _sig(-zz)
            dz_ref[:, t * blk:(t + 1) * blk] = dz
            tot = tot + jnp.sum(dz, axis=-1, keepdims=True)
        db_ref[...] = tot

    return pl.pallas_call(
        body, name="fox_prep_bwd",
        out_shape=[jax.ShapeDtypeStruct((h, s), F32), jax.ShapeDtypeStruct((h, 1), F32)])(dc, z, b)


FOX_W = 128
FOX_C = B_HD
FOX_ONE = B_HD + 3
FOX_SUB = 256
FOX_SUB_FWD = 128
FOX_HEADS_PER_STEP = 2


def _head_of_pair(x, hh):
    return x if hh == 0 else pltpu.roll(x, B_HD, 1)


def _fox_pack(u, c_col, t):
    s = u.shape[0]
    nt = s // t
    scale = B_HD ** -0.5

    def body(q_ref, k_ref, v_ref, c_ref, qf_ref, kb_ref, ks_ref, vb_ref, vt_ref):
        lane = lax.broadcasted_iota(jnp.int32, (t, FOX_W), 1)
        qv, kv, vv = [r[...].astype(F32) for r in (q_ref, k_ref, v_ref)]
        for hh in range(2):
            qf_ref[hh] = jnp.where(lane < B_HD, _head_of_pair(qv, hh), B_HD ** 0.5).astype(BF16)
            neg = c_ref[hh] * (-scale)
            hi = neg.astype(BF16).astype(F32)
            mid = (neg - hi).astype(BF16).astype(F32)
            lo = neg - hi - mid
            aux = jnp.where(lane == FOX_C, hi,
                            jnp.where(lane == FOX_C + 1, mid, jnp.where(lane == FOX_C + 2, lo, 0.0)))
            kb = jnp.where(lane < B_HD, _head_of_pair(kv, hh) * scale, aux)
            kb_ref[hh] = kb.astype(BF16)
            ks_ref[hh] = jnp.where(lane == FOX_ONE, 1.0, kb).T.astype(BF16)
            vb = jnp.where(lane < B_HD, _head_of_pair(vv, hh), 1.0)
            vb_ref[hh] = vb.astype(BF16)
            vt_ref[hh] = vb.T.astype(BF16)

    def tok(col0):
        return pl.BlockSpec((t, FOX_W), functools.partial(lambda hp, i, cb: (i, cb + hp), cb=col0 // FOX_W))

    rows = pl.BlockSpec((2, t, FOX_W), lambda hp, i: (hp, i, 0))
    tiles = pl.BlockSpec((2, None, FOX_W, t), lambda hp, i: (hp, i, 0, 0))
    hm = jax.ShapeDtypeStruct((B_HEADS, s, FOX_W), BF16)
    tt = jax.ShapeDtypeStruct((B_HEADS, nt, FOX_W, t), BF16)
    return pl.pallas_call(
        body, name="fox_pack", grid=(B_HEADS // 2, nt),
        in_specs=[tok(C_QB), tok(C_KB), tok(C_VB), pl.BlockSpec((2, t, 1), lambda hp, i: (hp, i, 0))],
        out_specs=[rows, rows, tiles, rows, tiles], out_shape=[hm, hm, tt, hm, tt],
        compiler_params=_params(("parallel", "parallel")),
    )(u, u, u, c_col)


def _fox_pack_bwd(dy, y, t):
    s = dy.shape[0]
    nt = s // t

    def body(do_ref, o_ref, dow_ref, dl_ref):
        lane = lax.broadcasted_iota(jnp.int32, (t, FOX_W), 1)
        lane8 = lax.broadcasted_iota(jnp.int32, (8, FOX_W), 1)
        dov = do_ref[...].astype(F32)
        parts = _split3(dov * o_ref[...].astype(F32))
        for hh in range(2):
            dow_ref[hh] = jnp.where(lane < B_HD, _head_of_pair(dov, hh), 0.0).astype(BF16)
            mask = ((lane8 >= hh * B_HD) & (lane8 < (hh + 1) * B_HD)).astype(BF16)
            row = _dot(mask, parts[0], NT) + _dot(mask, parts[1], NT) + _dot(mask, parts[2], NT)
            dl_ref[hh] = row[0:1, :]

    tok = pl.BlockSpec((t, FOX_W), lambda hp, i: (i, hp))
    return pl.pallas_call(
        body, name="fox_pack_bwd", grid=(B_HEADS // 2, nt), in_specs=[tok, tok],
        out_specs=[pl.BlockSpec((2, t, FOX_W), lambda hp, i: (hp, i, 0)),
                   pl.BlockSpec((2, None, 1, t), lambda hp, i: (hp, i, 0, 0))],
        out_shape=[jax.ShapeDtypeStruct((B_HEADS, s, FOX_W), BF16), jax.ShapeDtypeStruct((B_HEADS, nt, 1, t), F32)],
        compiler_params=_params(("parallel", "parallel")),
    )(dy, y)


def _fox_unpack(dqt, dkw, dvw, t):
    h, nt = dqt.shape[:2]
    s = nt * t

    def body(dq_ref, dk_ref, dv_ref, dqo_ref, dko_ref, dvo_ref, dc_ref):
        lane = lax.broadcasted_iota(jnp.int32, (t, FOX_W), 1)

        def join(a0, a1):
            return jnp.where(lane < B_HD, a0, pltpu.roll(a1, B_HD, 1))

        for hh in range(2):
            dc_ref[hh] = dq_ref[hh][FOX_ONE:FOX_ONE + 1, :] - dk_ref[hh].T[B_HD:B_HD + 1, :]
        dqo_ref[...] = join(dq_ref[0].T, dq_ref[1].T).astype(BF16)
        dko_ref[...] = join(dk_ref[0], dk_ref[1]).astype(BF16)
        dvo_ref[...] = join(dv_ref[0], dv_ref[1]).astype(BF16)

    tok = pl.BlockSpec((t, FOX_W), lambda hp, i: (i, hp))
    rows = pl.BlockSpec((2, t, FOX_W), lambda hp, i: (hp, i, 0))
    shp = jax.ShapeDtypeStruct((s, h * B_HD), BF16)
    return pl.pallas_call(
        body, name="fox_unpack", grid=(h // 2, nt),
        in_specs=[pl.BlockSpec((2, None, FOX_W, t), lambda hp, i: (hp, i, 0, 0)), rows, rows],
        out_specs=[tok, tok, tok, pl.BlockSpec((2, None, 1, t), lambda hp, i: (hp, i, 0, 0))],
        out_shape=[shp, shp, shp, jax.ShapeDtypeStruct((h, nt, 1, t), F32)],
        compiler_params=_params(("parallel", "parallel")),
    )(dqt, dkw, dvw)


FOX_DEAD = -110.0


def _fox_norm2(qf, kb):
    h, s, w = qf.shape
    tm = min(2048, s)

    def body(q_ref, k_ref, qo_ref, ko_ref):
        row = lax.broadcasted_iota(jnp.int32, (w, w), 0)
        ones = (row < B_HD).astype(BF16)
        for x_ref, o_ref in ((q_ref, qo_ref), (k_ref, ko_ref)):
            xv = x_ref[...].astype(F32)
            n2 = _dot((xv * xv).astype(BF16), ones)
            o_ref[...] = jnp.broadcast_to(jnp.max(n2, axis=0, keepdims=True)[:, :1], o_ref.shape)

    spec = pl.BlockSpec((None, tm, w), lambda hh, i: (hh, i, 0))
    ospec = pl.BlockSpec((None, None, 8, 128), lambda hh, i: (hh, i, 0, 0))
    shp = jax.ShapeDtypeStruct((h, s // tm, 8, 128), F32)
    return pl.pallas_call(
        body, name="fox_norm2", grid=(h, s // tm), in_specs=[spec, spec], out_specs=[ospec, ospec],
        out_shape=[shp, shp], compiler_params=_params(("parallel", "parallel")),
    )(qf, kb)


def _fox_bounds(qf, kb, c, t):
    q2, k2 = _fox_norm2(qf, kb)
    g = 2.0 * jnp.sqrt(1.02 * jnp.max(q2[:, :, 0, 0], axis=1) * 1.02 * jnp.max(k2[:, :, 0, 0], axis=1))
    return jnp.concatenate([c[:, ::t], c[:, t - 1::t], g[:, None]], axis=1)


SMEM_SPEC = pl.BlockSpec(memory_space=pltpu.SMEM)


def _fox_fwd(qf, kb, vt4, bounds, t):
    h, s, w = qf.shape
    nt = s // t
    sub = FOX_SUB_FWD
    nsub = t // sub
    nh = FOX_HEADS_PER_STEP

    def body(b_ref, q_ref, k_ref, v_ref, o_ref, lse_ref):
        i = pl.program_id(1)
        krow = lax.broadcasted_iota(jnp.int32, (sub, t), 0)
        qcol = lax.broadcasted_iota(jnp.int32, (sub, t), 1)

        def dead_before(hh):
            head = pl.program_id(0) * nh + hh
            top = b_ref[head, 2 * nt] + b_ref[head, i]
            return lax.fori_loop(
                0, i, lambda jj, n: n + (top - b_ref[head, nt + jj] < FOX_DEAD).astype(jnp.int32), 0)

        j_lo = functools.reduce(jnp.minimum, [dead_before(hh) for hh in range(nh)])

        def tile(j, carry, diag):
            out = []
            for hh in range(nh):
                m, acc = carry[hh]
                qv, vj = q_ref[hh], v_ref[hh, j]
                los = [b * sub if diag else 0 for b in range(nsub)]
                sts = [_dot(k_ref[hh, pl.ds(pl.multiple_of(j * t + b * sub, sub), sub), :], qv[lo:, :], NT)
                       for b, lo in enumerate(los)]
                for b, lo in enumerate(los):
                    st = sts[b]
                    if diag:
                        st = jnp.where(krow[:, :t - lo] <= qcol[:, :t - lo], st, NEG)
                    m_old, acc_old = m[:, lo:], acc[:, lo:]
                    m2 = jnp.maximum(m_old, jnp.max(st, axis=0, keepdims=True))
                    p = jnp.exp(st - m2).astype(BF16)
                    acc2 = jnp.exp(m_old - m2) * acc_old + _dot(vj[:, b * sub:(b + 1) * sub], p)
                    m = m2 if lo == 0 else jnp.concatenate([m[:, :lo], m2], axis=1)
                    acc = acc2 if lo == 0 else jnp.concatenate([acc[:, :lo], acc2], axis=1)
                out.append((m, acc))
            return tuple(out)

        init = tuple((jnp.full((1, t), NEG, F32), jnp.zeros((w, t), F32)) for _ in range(nh))
        carry = lax.fori_loop(j_lo, i, lambda j, c: tile(j, c, False), init)
        outs = []
        for hh, (m, acc) in enumerate(tile(i, carry, True)):
            den = acc[B_HD:B_HD + 1, :]
            outs.append(acc[0:B_HD, :] / den)
            lse_ref[hh] = m + jnp.log(den)
        o_ref[...] = jnp.concatenate(outs, axis=0).T.astype(o_ref.dtype)

    return pl.pallas_call(
        body, name="fox_fwd", grid=(h // nh, nt),
        in_specs=[SMEM_SPEC,
                  pl.BlockSpec((nh, t, w), lambda hh, i: (hh, i, 0)),
                  pl.BlockSpec((nh, s, w), lambda hh, i: (hh, 0, 0)),
                  pl.BlockSpec((nh, nt, w, t), lambda hh, i: (hh, 0, 0, 0))],
        out_specs=[pl.BlockSpec((t, nh * B_HD), lambda hh, i: (i, hh)),
                   pl.BlockSpec((nh, 1, t), lambda hh, i: (hh, 0, i))],
        out_shape=[jax.ShapeDtypeStruct((s, h * B_HD), BF16), jax.ShapeDtypeStruct((h, 1, s), F32)],
        compiler_params=_params(("parallel", "parallel")),
    )(bounds, qf, kb, vt4)


def _fox_bwd(qf, dow, lse_row, delta_row, kb, kst4, vb, bounds, t):
    h, s, w = qf.shape
    nt = s // t
    nsub = t // FOX_SUB
    nh = FOX_HEADS_PER_STEP

    def body(b_ref, q_ref, do_ref, lse_ref, dl_ref, k_ref, kt_ref, v_ref, dqt_ref, dk_ref, dv_ref, dk_acc, dv_acc):
        j = pl.program_id(1)

        def alive_after(hh):
            head = pl.program_id(0) * nh + hh
            top = b_ref[head, 2 * nt] - b_ref[head, nt + j]
            return lax.fori_loop(
                j + 1, nt, lambda ii, n: n + (top + b_ref[head, ii] >= FOX_DEAD).astype(jnp.int32), 0)

        i_hi = j + 1 + functools.reduce(jnp.maximum, [alive_after(hh) for hh in range(nh)])

        @pl.when(j == 0)
        def _():
            dqt_ref[...] = jnp.zeros_like(dqt_ref)

        dk_acc[...] = jnp.zeros_like(dk_acc)
        dv_acc[...] = jnp.zeros_like(dv_acc)
        krow = lax.broadcasted_iota(jnp.int32, (FOX_SUB, t), 0)
        qcol = lax.broadcasted_iota(jnp.int32, (FOX_SUB, t), 1)
        subs = [slice(b * FOX_SUB, (b + 1) * FOX_SUB) for b in range(nsub)]

        def tile(i, diag):
            i0 = pl.multiple_of(i * t, t)
            for hh in range(nh):
                qi, doi = q_ref[hh, pl.ds(i0, t), :], do_ref[hh, pl.ds(i0, t), :]
                lse, dl = lse_ref[hh, i], dl_ref[hh, i]
                los = [b * FOX_SUB if diag else 0 for b in range(nsub)]
                sts = [_dot(k_ref[hh, rs, :], qi[lo:, :], NT) for rs, lo in zip(subs, los)]
                dps = [_dot(v_ref[hh, rs, :], doi[lo:, :], NT) for rs, lo in zip(subs, los)]
                dq = None
                for b, (rs, lo) in enumerate(zip(subs, los)):
                    st = sts[b] - lse[:, lo:]
                    if diag:
                        st = jnp.where(krow[:, :t - lo] <= qcol[:, :t - lo], st, NEG)
                    pt = jnp.exp(st)
                    dsb = (pt * (dps[b] - dl[:, lo:])).astype(BF16)
                    dv_acc[hh, rs, :] += _dot(pt.astype(BF16), doi[lo:, :])
                    dk_acc[hh, rs, :] += _dot(dsb, qi[lo:, :])
                    part = _dot(kt_ref[hh, :, rs], dsb)
                    if lo:
                        part = jnp.concatenate([jnp.zeros((w, lo), F32), part], axis=1)
                    dq = part if dq is None else dq + part
                dqt_ref[hh, i] += dq

        def step(i, carry):
            tile(i, False)
            return carry

        tile(j, True)
        lax.fori_loop(j + 1, i_hi, step, 0)
        dk_ref[...] = dk_acc[...] * (B_HD ** -0.5)
        dv_ref[...] = dv_acc[...]

    full = pl.BlockSpec((nh, s, w), lambda hh, j: (hh, 0, 0))
    rowst = pl.BlockSpec((nh, nt, 1, t), lambda hh, j: (hh, 0, 0, 0))
    tl = pl.BlockSpec((nh, t, w), lambda hh, j: (hh, j, 0))
    return pl.pallas_call(
        body, name="fox_bwd", grid=(h // nh, nt),
        in_specs=[SMEM_SPEC, full, full, rowst, rowst, tl,
                  pl.BlockSpec((nh, None, w, t), lambda hh, j: (hh, j, 0, 0)), tl],
        out_specs=[pl.BlockSpec((nh, nt, w, t), lambda hh, j: (hh, 0, 0, 0)), tl, tl],
        out_shape=[jax.ShapeDtypeStruct((h, nt, w, t), F32), jax.ShapeDtypeStruct((h, s, w), F32),
                   jax.ShapeDtypeStruct((h, s, w), F32)],
        scratch_shapes=[pltpu.VMEM((nh, t, w), F32), pltpu.VMEM((nh, t, w), F32)],
        compiler_params=_params(("parallel", "arbitrary")),
    )(bounds, qf, dow, lse_row, delta_row, kb, kst4, vb)


def _mem_fwd(u, mkv, tq=512):
    s = u.shape[0]
    scale = HD ** -0.5

    def body(q_ref, mk_ref, mv_ref, o_ref, lse_ref):
        lses = []
        for h in range(4):
            cs = slice(h * HD, (h + 1) * HD)
            sc = _dot(q_ref[:, cs], mk_ref[:, cs], NT) * scale
            m = jnp.max(sc, axis=-1, keepdims=True)
            p = jnp.exp(sc - m)
            den = jnp.sum(p, axis=-1, keepdims=True)
            o_ref[:, cs] = (_dot(p.astype(BF16), mv_ref[:, cs]) / den).astype(o_ref.dtype)
            lses.append(m + jnp.log(den))
        lse_ref[...] = _lane_pack(lses, (tq, HD))

    return pl.pallas_call(
        body, name="mem_fwd", grid=(s // tq,),
        in_specs=[pl.BlockSpec((tq, 512), lambda i: (i, C_QM // 512)),
                  pl.BlockSpec((N_MEM, 512), lambda i: (0, 0)),
                  pl.BlockSpec((N_MEM, 512), lambda i: (0, 1))],
        out_specs=[pl.BlockSpec((tq, 512), lambda i: (i, 0)), pl.BlockSpec((tq, HD), lambda i: (i, 0))],
        out_shape=[jax.ShapeDtypeStruct((s, 512), BF16), jax.ShapeDtypeStruct((s, HD), F32)],
        compiler_params=_params(("parallel",)),
    )(u, mkv, mkv)


def _mem_bwd(u, mkv, o, do, lse, du, tq=512):
    s = u.shape[0]
    scale = HD ** -0.5

    def body(q_ref, mk_ref, mv_ref, o_ref, do_ref, lse_ref, _, dq_ref, dmk_ref, dmv_ref):
        @pl.when(pl.program_id(0) == 0)
        def _():
            dmk_ref[...] = jnp.zeros_like(dmk_ref)
            dmv_ref[...] = jnp.zeros_like(dmv_ref)

        for h in range(4):
            cs = slice(h * HD, (h + 1) * HD)
            qv, dov = q_ref[:, cs], do_ref[:, cs]
            sc = _dot(qv, mk_ref[:, cs], NT) * scale
            p = jnp.exp(sc - lse_ref[:, h:h + 1])
            delta = jnp.sum(dov.astype(F32) * o_ref[:, cs].astype(F32), axis=-1, keepdims=True)
            ds = p * (_dot(dov, mv_ref[:, cs], NT) - delta)
            dsb = ds.astype(BF16)
            dq_ref[:, cs] = (_dot(dsb, mk_ref[:, cs]) * scale).astype(dq_ref.dtype)
            dmk_ref[:, cs] += _dot(dsb, qv, TN) * scale
            dmv_ref[:, cs] += _dot(p.astype(BF16), dov, TN)

    row = pl.BlockSpec((tq, 512), lambda i: (i, 0))
    acc = pl.BlockSpec((N_MEM, 512), lambda i: (0, 0))
    return pl.pallas_call(
        body, name="mem_bwd", grid=(s // tq,),
        in_specs=[pl.BlockSpec((tq, 512), lambda i: (i, C_QM // 512)),
                  pl.BlockSpec((N_MEM, 512), lambda i: (0, 0)),
                  pl.BlockSpec((N_MEM, 512), lambda i: (0, 1)),
                  row, row, pl.BlockSpec((tq, HD), lambda i: (i, 0)), pl.BlockSpec(memory_space=pl.ANY)],
        out_specs=[pl.BlockSpec((tq, 512), lambda i: (i, C_QM // 512)), acc, acc],
        out_shape=[jax.ShapeDtypeStruct(du.shape, du.dtype), jax.ShapeDtypeStruct((N_MEM, 512), F32),
                   jax.ShapeDtypeStruct((N_MEM, 512), F32)],
        input_output_aliases={6: 0},
        compiler_params=_params(("arbitrary",)),
    )(u, mkv, mkv, o, do, lse, du)


def _local_step(x, mem, pos, target, g_pre, g_post, g_mem, w_main, w_fb, b_forget, b_merge,
                w_mem_kv, w_ba, w_bb, w_bm, w_out, exchange=None):
    s = x.shape[0]
    t_fox = min(512, s)
    nt = s // t_fox
    half = ROT_DIM // 2
    inv = ROPE_THETA ** (-jnp.arange(half, dtype=F32) / half)
    inv128 = jnp.concatenate([inv, inv, jnp.zeros((HD - ROT_DIM,), F32)]).reshape(1, HD)

    h = _rms_fwd("norm_pre", x, g_pre)
    u = _mm("proj_in", h, w_main, "nn", BF16, tm=4096)
    ufb = _mm("proj_fb", h, w_fb, "nn", F32)
    memn = _rms_fwd("norm_mem", mem, g_mem)
    mkv = _mm("proj_mem", memn, w_mem_kv, "nn", BF16)

    qkv = _rope_fwd(u, pos, inv128)
    views = [tuple(qkv[3 * g:3 * g + 3]) for g in range(3)]
    os_, lses = [], []
    for g, d in enumerate(DILATIONS):
        o_g, lse_g = _band_fwd("band_fwd%d" % g, *views[g], d)
        os_.append((o_g, d * A_GROUP, 0, d))
        lses.append((lse_g, d * HD, 0, d))

    def merge_a(o1, o2, o3, l1, l2, l3, za, *scr):
        o1, o2, o3 = [_from_class(o, scr, d) for o, d in zip((o1, o2, o3), DILATIONS)]
        l1, l2, l3 = [_from_class(lv, scr, d) for lv, d in zip((l1, l2, l3), DILATIONS)]
        ys, tots = [], []
        for hh in range(4):
            cs, hs = slice(hh * HD, (hh + 1) * HD), slice(hh, hh + 1)
            mx = jnp.maximum(jnp.maximum(l1[:, hs], l2[:, hs]), l3[:, hs])
            e1, e2, e3 = jnp.exp(l1[:, hs] - mx), jnp.exp(l2[:, hs] - mx), jnp.exp(l3[:, hs] - mx)
            den = e1 + e2 + e3
            ys.append((e1 * o1[:, cs] + e2 * o2[:, cs] + e3 * o3[:, cs]) / den)
            tots.append(mx + jnp.log(den))
        y = jnp.concatenate(ys, axis=1)
        zf = za.astype(F32)
        tot = _lane_pack(tots, l1.shape)
        return (y, y * (zf * _sig(zf))) + tuple(_to_class(tot, scr, d) for d in DILATIONS)

    res = _rows("merge_a", merge_a, os_ + lses + [(u, 512, C_ZA // 512)], [],
                [(512, BF16), (512, BF16)] + [(d * HD, F32, d) for d in DILATIONS], tm=ROPE_TM,
                scratch=_class_scratch(ROPE_TM))
    y_a, yg_a, lse_a = res[0], res[1], res[2:5]

    zrow = ufb[:, :B_HEADS].T
    c = _fox_prep(zrow, b_forget.reshape(B_HEADS, 1))
    qf, kb, kst4, vb, vt4 = _fox_pack(u, c.reshape(B_HEADS, s, 1), t_fox)
    bounds = _fox_bounds(qf, kb, c, t_fox)
    y_b, lse_b = _fox_fwd(qf, kb, vt4, bounds, t_fox)

    y_m, lse_m = _mem_fwd(u, mkv)

    def gate(y, z):
        zf = z.astype(F32)
        return (y.astype(F32) * (zf * _sig(zf)),)

    yg_b = _rows("gate_b", gate, [y_b, (u, 512, C_ZB // 512)], [], [(512, BF16)])[0]
    yg_m = _rows("gate_m", gate, [y_m, (u, 512, C_ZM // 512)], [], [(512, BF16)])[0]

    br_a = _mm("branch_a", yg_a, w_ba, "nn", BF16)
    br_b = _mm("branch_b", yg_b, w_bb, "nn", BF16)
    br_m = _mm("branch_m", yg_m, w_bm, "nn", BF16)
    gl = [(u, 1024, C_GL // 1024 + i) for i in range(3)]
    bm3 = b_merge.reshape(3, D_MODEL)

    def merge(g0, g1, g2, b0, b1, b2, bm):
        tot = 0.0
        for i, (gv, bv) in enumerate(((g0, b0), (g1, b1), (g2, b2))):
            tot = tot + _sig(gv.astype(F32) + bm[i:i + 1, :]) * bv.astype(F32)
        return (tot,)

    merged = _rows("merge_gates", merge, gl + [br_a, br_b, br_m], [bm3], [(D_MODEL, BF16)])[0]
    out = _mm("proj_out", merged, w_out, "nn", F32)

    def tail(xv, ov, tv, gv):
        r = lax.rsqrt(jnp.mean(ov * ov, axis=-1, keepdims=True) + EPS)
        n = ov * r
        err = xv + n * gv - tv
        dy = err * (1.0 / D_MODEL)
        dn = dy * gv
        dout = r * (dn - n * jnp.mean(dn * n, axis=-1, keepdims=True))
        return (dy, dout, jnp.sum(0.5 * err * err * (1.0 / D_MODEL), axis=0, keepdims=True),
                jnp.sum(dy * n, axis=0, keepdims=True))

    dy, dout, loss_lanes, g_post_grad = _rows(
        "tail", tail, [x, out, target], [g_post], [(D_MODEL, F32), (D_MODEL, BF16)],
        reds=[D_MODEL, D_MODEL], tm=256)

    dmerged = _mm("d_merged", dout, w_out, "nt", BF16)
    gw_out = _mm("g_w_out", merged, dout, "tn", F32)

    def merge_bwd(dm, g0, g1, g2, b0, b1, b2, bm):
        dmf = dm.astype(F32)
        dbs, dgs, sums = [], [], []
        for i, (gv, bv) in enumerate(((g0, b0), (g1, b1), (g2, b2))):
            sg = _sig(gv.astype(F32) + bm[i:i + 1, :])
            dbs.append(dmf * sg)
            dg = dmf * bv.astype(F32) * sg * (1.0 - sg)
            dgs.append(dg)
            sums.append(jnp.sum(dg, axis=0, keepdims=True))
        return tuple(dbs + dgs + sums)

    res = _rows("merge_bwd", merge_bwd, [dmerged] + gl + [br_a, br_b, br_m], [bm3],
                [(D_MODEL, BF16)] * 6, reds=[D_MODEL] * 3, tm=256)
    dbr, dgl, g_bmerge = res[0:3], res[3:6], jnp.concatenate(res[6:9], axis=1)

    dyg, gw_branch = [], []
    for nm, dbv, wv, ygv in (("a", dbr[0], w_ba, yg_a), ("b", dbr[1], w_bb, yg_b), ("m", dbr[2], w_bm, yg_m)):
        dyg.append(_mm("d_yg_" + nm, dbv, wv, "nt", BF16))
        gw_branch.append(_mm("g_w_branch_" + nm, ygv, dbv, "tn", F32))

    def gate_bwd(dg, y, z):
        dgf, yf, zf = dg.astype(F32), y.astype(F32), z.astype(F32)
        sg = _sig(zf)
        return dgf * (zf * sg), dgf * yf * (sg * (1.0 + zf * (1.0 - sg)))

    def gate_bwd_a(dg, y, z, *scr):
        dyv, dz = gate_bwd(dg, y, z)
        prod = dyv * y.astype(F32)
        dl = [jnp.sum(prod[:, hh * HD:(hh + 1) * HD], axis=-1, keepdims=True) for hh in range(4)]
        delta = _lane_pack(dl, (dg.shape[0], HD))
        return ((dz,) + tuple(_to_class(dyv, scr, d) for d in DILATIONS)
                + tuple(_to_class(delta, scr, d) for d in DILATIONS))

    du = lax.empty(u.shape, BF16)
    res = _rows("gate_bwd_a", gate_bwd_a, [dyg[0], y_a, (u, 512, C_ZA // 512)], [],
                [(512, BF16)] + [(d * A_GROUP, BF16, d) for d in DILATIONS] + [(d * HD, F32, d) for d in DILATIONS],
                tm=ROPE_TM, scratch=_class_scratch(ROPE_TM), into=(du, 0, C_ZA // 512))
    du, dy_a, delta_a = res[0], res[1:4], res[4:7]
    dy_b, du = _rows("gate_bwd_b", gate_bwd, [dyg[1], y_b, (u, 512, C_ZB // 512)], [],
                     [(512, BF16), (512, BF16)], into=(du, 1, C_ZB // 512))
    dy_m, du = _rows("gate_bwd_m", gate_bwd, [dyg[2], y_m, (u, 512, C_ZM // 512)], [],
                     [(512, BF16), (512, BF16)], into=(du, 1, C_ZM // 512))

    du, dmk, dmv = _mem_bwd(u, mkv, y_m, dy_m, lse_m, du)
    dmkv = jnp.concatenate([dmk, dmv], axis=1)
    gw_mem_kv = _mm("g_w_mem_kv", memn, dmkv, "tn", F32)
    dmemn = _mm("d_memn", dmkv, w_mem_kv, "nt", F32)

    def mem_gain_grad(mv, dv):
        r = lax.rsqrt(jnp.mean(mv * mv, axis=-1, keepdims=True) + EPS)
        return (jnp.sum(dv * mv * r, axis=0, keepdims=True),)

    g_mem_grad = _rows("g_norm_mem", mem_gain_grad, [mem, dmemn], [], [], reds=[D_MODEL], tm=N_MEM)[0]

    dow, delta_b = _fox_pack_bwd(dy_b, y_b, t_fox)
    dqt, dkw, dvw = _fox_bwd(qf, dow, lse_b.reshape(B_HEADS, nt, 1, t_fox), delta_b, kb, kst4, vb, bounds, t_fox)
    dqb, dkb, dvb, dc = _fox_unpack(dqt, dkw, dvw, t_fox)
    dzrow, g_bforget = _fox_prep_bwd(dc.reshape(B_HEADS, s), zrow, b_forget.reshape(B_HEADS, 1))
    dfb = jnp.zeros((s, HD), BF16).at[:, :B_HEADS].set(dzrow.T.astype(BF16))

    dqs, dks, dvs = [], [], []
    for g, d in enumerate(DILATIONS):
        qv, kv, vv = views[g]
        dqs.append(_band_dq("band_dq%d" % g, qv, kv, vv, dy_a[g], lse_a[g], delta_a[g], d))
        dk_g, dv_g = _band_dkv("band_dkv%d" % g, qv, kv, vv, dy_a[g], lse_a[g], delta_a[g], d)
        dks.append(dk_g)
        dvs.append(dv_g)
    for col, piece in ((C_QB, dqb), (C_KB, dkb), (C_VB, dvb), (C_GL, dgl[0]), (C_GL + D_MODEL, dgl[1]),
                       (C_GL + 2 * D_MODEL, dgl[2])):
        du = lax.dynamic_update_slice(du, piece, (0, col))
    du = _rope_bwd(dqs, dks, dvs, pos, inv128, du)

    gw_main = _mm("g_w_main", h.T, du, "nn", F32, tk=2048)
    gw_fb = _mm("g_w_fb", h, dfb, "tn", F32)
    gw_in = jnp.concatenate([gw_main[:, :FB_ORIG], gw_fb[:, :B_HEADS], gw_main[:, FB_ORIG:]], axis=1)
    grads = dict(norm_post_g=g_post_grad, norm_mem_g=g_mem_grad, w_in=gw_in,
                 b_forget=g_bforget.reshape(1, B_HEADS), b_merge=g_bmerge, w_mem_kv=gw_mem_kv,
                 w_branch_a=gw_branch[0], w_branch_b=gw_branch[1], w_branch_m=gw_branch[2], w_out=gw_out)
    side = exchange(grads) if exchange else None
    dh_main = _mm("d_h", du, w_main, "nt", F32, tk=2816, side=side)
    landed = None
    if side:
        dh_main, landed = dh_main[0], dh_main[1:]
    dh_fb = _mm("d_h_fb", dfb, w_fb, "nt", F32)

    def pre_bwd(xv, d1, d2, dyv, gv):
        r = lax.rsqrt(jnp.mean(xv * xv, axis=-1, keepdims=True) + EPS)
        n = xv * r
        dhv = d1 + d2
        dn = dhv * gv
        dx = r * (dn - n * jnp.mean(dn * n, axis=-1, keepdims=True))
        return dyv + dx, jnp.sum(dhv * n, axis=0, keepdims=True)

    grad_x, g_pre_grad = _rows("norm_pre_bwd", pre_bwd, [x, dh_main, dh_fb, dy], [g_pre],
                               [(D_MODEL, F32)], reds=[D_MODEL], tm=256)

    grads["norm_pre_g"] = g_pre_grad
    return loss_lanes, grad_x, grads, landed


HBM_SPEC = pl.BlockSpec(memory_space=pltpu.HBM)


def _place():
    x, y, c = lax.axis_index("x"), lax.axis_index("y"), lax.axis_index("c")
    chips = [(1 - x, y), (x, 1 - y), (1 - x, 1 - y)]
    return x, y, c, 2 * x + y, chips


N_CHUNKS = 4


def _units(parts, row_axis):
    units = []
    for i, a in enumerate(parts):
        ch = a.shape[row_axis] // N_CHUNKS
        units += [(i, pl.ds(k * ch, ch)) for k in range(N_CHUNKS)]
    return units


def _gather_weights(parts):
    n = len(parts)
    units = _units(parts, 1)
    nu = len(units)
    via_y = [(u % N_CHUNKS) < N_CHUNKS // 2 for u in range(nu)]

    def body(*refs):
        srcs, outs = refs[:n], refs[n:2 * n]
        send_sems, recv_sems = refs[2 * n:]
        x, y, c, p, _ = _place()
        me, sib = (x, y, c), (x, y, 1 - c)
        xn, yn, dg = (1 - x, y), (x, 1 - y), (1 - x, 1 - y)

        def cp(u, k, chip, half, to, from_src=False):
            i, rs = units[u]
            dst = outs[i].at[2 * chip[0] + chip[1], half, rs]
            return pltpu.make_async_remote_copy(
                src_ref=srcs[i].at[half, rs] if from_src else dst, dst_ref=dst, send_sem=send_sems.at[u, k],
                recv_sem=recv_sems.at[u, k], device_id=to, device_id_type=MESH)

        sent = []

        def go(copy):
            copy.start()
            sent.append(copy)

        for u in range(nu):
            go(cp(u, 0, (x, y), c, (*xn, c), from_src=True))
            go(cp(u, 1, (x, y), c, (*yn, c), from_src=True))
        for u in range(nu):
            cp(u, 0, xn, c, me).wait_recv()
            go(cp(u, 4, xn, c, sib))
            if via_y[u]:
                go(cp(u, 2, xn, c, (*yn, c)))
            cp(u, 1, yn, c, me).wait_recv()
            go(cp(u, 5, yn, c, sib))
            if not via_y[u]:
                go(cp(u, 3, yn, c, (*xn, c)))
        for u in range(nu):
            cp(u, 2 if via_y[u] else 3, dg, c, me).wait_recv()
            go(cp(u, 6, dg, c, sib))
        for u in range(nu):
            for k, chip in ((4, xn), (5, yn), (6, dg)):
                cp(u, k, chip, 1 - c, me).wait_recv()
        for copy in sent:
            copy.wait_send()

    return pl.pallas_call(
        body, name="gather_weights", in_specs=[HBM_SPEC] * n, out_specs=[HBM_SPEC] * n,
        out_shape=[jax.ShapeDtypeStruct((N_CHIPS,) + a.shape, a.dtype) for a in parts],
        scratch_shapes=[pltpu.SemaphoreType.DMA((nu, 7)), pltpu.SemaphoreType.DMA((nu, 7))],
    )(*parts)


def _swap_with_sibling(parts):
    n = len(parts)
    units = _units(parts, 2)

    def body(*refs):
        srcs, outs = refs[:n], refs[n:2 * n]
        send_sems, recv_sems = refs[2 * n:]
        x, y, c, _, _ = _place()
        cps = [pltpu.make_async_remote_copy(
            src_ref=srcs[i].at[q, 1 - c, rs], dst_ref=outs[i].at[q, rs], send_sem=send_sems.at[u, q],
            recv_sem=recv_sems.at[u, q], device_id=(x, y, 1 - c), device_id_type=MESH)
            for q in range(N_CHIPS) for u, (i, rs) in enumerate(units)]
        for cpy in cps:
            cpy.start()
        for cpy in cps:
            cpy.wait()

    return pl.pallas_call(
        body, name="swap_with_sibling", in_specs=[HBM_SPEC] * n, out_specs=[HBM_SPEC] * n,
        out_shape=[jax.ShapeDtypeStruct(a.shape[:1] + a.shape[2:], a.dtype) for a in parts],
        scratch_shapes=[pltpu.SemaphoreType.DMA((len(units), N_CHIPS)),
                        pltpu.SemaphoreType.DMA((len(units), N_CHIPS))],
    )(*parts)


def _scatter_to_owners(parts):
    n = len(parts)
    units = _units(parts, 1)

    def copies(srcs, outs, send_sems, recv_sems, incoming):
        x, y, c, p, chips = _place()
        return [pltpu.make_async_remote_copy(
            src_ref=srcs[i].at[2 * cx + cy, rs], dst_ref=outs[i].at[(2 * cx + cy) if incoming else p, rs],
            send_sem=send_sems.at[u, j], recv_sem=recv_sems.at[u, j], device_id=(cx, cy, c), device_id_type=MESH)
            for u, (i, rs) in enumerate(units) for j, (cx, cy) in enumerate(chips)]

    def start(ins, outs, scratch):
        for cpy in copies(ins, outs, *scratch, incoming=False):
            cpy.start()

    def wait(ins, outs, scratch):
        for cpy in copies(ins, outs, *scratch, incoming=True):
            cpy.wait_recv()
        for cpy in copies(ins, outs, *scratch, incoming=False):
            cpy.wait_send()

    return dict(ins=list(parts), outs=[jax.ShapeDtypeStruct(a.shape, a.dtype) for a in parts],
                scratch=[pltpu.SemaphoreType.DMA((len(units), 3)), pltpu.SemaphoreType.DMA((len(units), 3))],
                start=start, wait=wait)


def _share_with_sibling(parts):
    n = len(parts)
    units = _units(parts, 1)

    def body(*refs):
        srcs, outs = refs[:n], refs[n:2 * n]
        send_sems, recv_sems = refs[2 * n:]
        x, y, c, _, _ = _place()
        sends = [pltpu.make_async_remote_copy(
            src_ref=srcs[i].at[0, rs], dst_ref=outs[i].at[c, rs], send_sem=send_sems.at[u],
            recv_sem=recv_sems.at[u], device_id=(x, y, 1 - c), device_id_type=MESH)
            for u, (i, rs) in enumerate(units)]
        for cpy in sends:
            cpy.start()
        for u, (i, rs) in enumerate(units):
            pltpu.make_async_remote_copy(
                src_ref=srcs[i].at[0, rs], dst_ref=outs[i].at[1 - c, rs], send_sem=send_sems.at[u],
                recv_sem=recv_sems.at[u], device_id=(x, y, 1 - c), device_id_type=MESH).wait_recv()
        for cpy in sends:
            cpy.wait_send()

    return pl.pallas_call(
        body, name="share_with_sibling", in_specs=[HBM_SPEC] * n, out_specs=[HBM_SPEC] * n,
        out_shape=[jax.ShapeDtypeStruct((2,) + a.shape[1:], a.dtype) for a in parts],
        scratch_shapes=[pltpu.SemaphoreType.DMA((len(units),)), pltpu.SemaphoreType.DMA((len(units),))],
    )(*parts)


def _sum_small(v):
    def body(v_ref, out_ref, buf, send_sems, recv_sems):
        x, y, c, _, _ = _place()
        me = 4 * x + 2 * y + c
        buf[me] = v_ref[...]
        flips = [(dx, dy, dc) for dx in (0, 1) for dy in (0, 1) for dc in (0, 1)][1:]
        sends = []
        for k, (dx, dy, dc) in enumerate(flips):
            cpy = pltpu.make_async_remote_copy(
                src_ref=v_ref, dst_ref=buf.at[me], send_sem=send_sems.at[k], recv_sem=recv_sems.at[k],
                device_id=((x + dx) % 2, (y + dy) % 2, (c + dc) % 2), device_id_type=MESH)
            cpy.start()
            sends.append(cpy)
        for k, (dx, dy, dc) in enumerate(flips):
            px, py, pc = (x + dx) % 2, (y + dy) % 2, (c + dc) % 2
            pltpu.make_async_remote_copy(
                src_ref=v_ref, dst_ref=buf.at[4 * px + 2 * py + pc], send_sem=send_sems.at[k],
                recv_sem=recv_sems.at[k], device_id=(px, py, pc), device_id_type=MESH).wait_recv()
        for cpy in sends:
            cpy.wait_send()
        tot = buf[0]
        for i in range(1, N_DEV):
            tot = tot + buf[i]
        out_ref[...] = tot

    return pl.pallas_call(
        body, name="sum_small", out_shape=jax.ShapeDtypeStruct(v.shape, v.dtype),
        in_specs=[pl.BlockSpec(memory_space=pltpu.VMEM)], out_specs=pl.BlockSpec(memory_space=pltpu.VMEM),
        scratch_shapes=[pltpu.VMEM((N_DEV,) + v.shape, v.dtype), pltpu.SemaphoreType.DMA((N_DEV - 1,)),
                        pltpu.SemaphoreType.DMA((N_DEV - 1,))],
    )(v)


def _add_chips(name, landed, pair, chip):
    nq, r, w = landed.shape
    tr = 64

    def body(chip_ref, *refs):
        own = refs[nq][...].astype(F32)
        tot = None
        for q in range(nq):
            term = jnp.where(chip_ref[0] == q, own, refs[q][...].astype(F32))
            tot = term if tot is None else tot + term
        refs[nq + 1][...] = tot

    specs = [pl.BlockSpec((None, tr, w), functools.partial(lambda j, chip_ref, q: (q, j, 0), q=q)) for q in range(nq)]
    specs.append(pl.BlockSpec((None, tr, w), lambda j, chip_ref: (chip_ref[0], j, 0)))
    grid_spec = pltpu.PrefetchScalarGridSpec(
        num_scalar_prefetch=1, grid=(r // tr,), in_specs=specs,
        out_specs=pl.BlockSpec((None, tr, w), lambda j, chip_ref: (0, j, 0)))
    return pl.pallas_call(
        body, name=name, grid_spec=grid_spec, out_shape=jax.ShapeDtypeStruct((1, r, w), F32),
        compiler_params=_params(("parallel",)),
    )(jnp.reshape(chip, (1,)).astype(jnp.int32), *([landed] * nq), pair)


def _add_pair(name, halves, got, c):
    nq, _, r, w = halves.shape
    tr = 64

    def body(c_ref, a_ref, b_ref, o_ref):
        o_ref[...] = (a_ref[...] + b_ref[...]).astype(o_ref.dtype)

    grid_spec = pltpu.PrefetchScalarGridSpec(
        num_scalar_prefetch=1, grid=(nq, r // tr),
        in_specs=[pl.BlockSpec((None, None, tr, w), lambda i, j, c_ref: (i, c_ref[0], j, 0)),
                  pl.BlockSpec((None, tr, w), lambda i, j, c_ref: (i, j, 0))],
        out_specs=pl.BlockSpec((None, tr, w), lambda i, j, c_ref: (i, j, 0)))
    return pl.pallas_call(
        body, name=name, grid_spec=grid_spec, out_shape=jax.ShapeDtypeStruct((nq, r, w), BF16),
        compiler_params=_params(("parallel", "parallel")),
    )(jnp.reshape(c, (1,)).astype(jnp.int32), halves, got)


def _adamw(name, w, g, m, v, tm):
    def fn(wv, gv, mv, vv):
        m2 = ADAM_B1 * mv + (1.0 - ADAM_B1) * gv
        v2 = ADAM_B2 * vv + (1.0 - ADAM_B2) * (gv * gv)
        m_hat = m2 / (1.0 - ADAM_B1 ** ADAM_STEP)
        v_hat = v2 / (1.0 - ADAM_B2 ** ADAM_STEP)
        return -ADAM_LR * (m_hat / (jnp.sqrt(v_hat) + ADAM_EPS) + ADAM_WD * wv), m2, v2
    c = w.shape[1]
    return _rows(name, fn, [w, g, m, v], [], [(c, F32)] * 3, tm=tm)


REST_ROWS = 256 + 3 * 128 + 256
REST_SPLITS = (("w_mem_kv", 0, 256), ("w_branch_a", 256, 128), ("w_branch_b", 384, 128),
               ("w_branch_m", 512, 128), ("w_out", 640, 256))


def _rest_pack(t):
    return jnp.concatenate([t[n].reshape(rows, D_MODEL) for n, _, rows in REST_SPLITS], axis=0)


def _rest_unpack(a, shapes):
    return {n: a[r0:r0 + rows].reshape(shapes[n]) for n, r0, rows in REST_SPLITS}


def _small_pack(pre, post, memg, bforget, bmerge):
    pad = jnp.zeros((1, D_MODEL - B_HEADS), F32)
    return jnp.concatenate([pre, post, memg, bmerge.reshape(3, D_MODEL),
                            jnp.concatenate([bforget, pad], axis=1), jnp.zeros((1, D_MODEL), F32)], axis=0)


def _small_unpack(s8):
    return dict(norm_pre_g=s8[0:1], norm_post_g=s8[1:2], norm_mem_g=s8[2:3],
                b_merge=s8[3:6].reshape(1, 3 * D_MODEL), b_forget=s8[6:7, :B_HEADS])


WEIGHTS = ("norm_pre_g", "norm_post_g", "norm_mem_g", "w_in", "b_forget", "b_merge", "w_mem_kv",
           "w_branch_a", "w_branch_b", "w_branch_m", "w_out")
SMALL = ("norm_pre_g", "norm_post_g", "norm_mem_g", "b_forget", "b_merge")


def kernel(x, mem, positions, norm_pre_g, norm_post_g, norm_mem_g, w_in, b_forget, b_merge, w_mem_kv, w_branch_a, w_branch_b, w_branch_m, w_out, loss_target, m_norm_pre_g, m_norm_post_g, m_norm_mem_g, m_w_in, m_b_forget, m_b_merge, m_w_mem_kv, m_w_branch_a, m_w_branch_b, m_w_branch_m, m_w_out, v_norm_pre_g, v_norm_post_g, v_norm_mem_g, v_w_in, v_b_forget, v_b_merge, v_w_mem_kv, v_w_branch_a, v_w_branch_b, v_w_branch_m, v_w_out):
    w = dict(norm_pre_g=norm_pre_g, norm_post_g=norm_post_g, norm_mem_g=norm_mem_g, w_in=w_in[0],
             b_forget=b_forget, b_merge=b_merge, w_mem_kv=w_mem_kv[0], w_branch_a=w_branch_a[0],
             w_branch_b=w_branch_b[0], w_branch_m=w_branch_m[0], w_out=w_out[0])
    mo = dict(norm_pre_g=m_norm_pre_g, norm_post_g=m_norm_post_g, norm_mem_g=m_norm_mem_g, w_in=m_w_in[0],
              b_forget=m_b_forget, b_merge=m_b_merge, w_mem_kv=m_w_mem_kv[0], w_branch_a=m_w_branch_a[0],
              w_branch_b=m_w_branch_b[0], w_branch_m=m_w_branch_m[0], w_out=m_w_out[0])
    vo = dict(norm_pre_g=v_norm_pre_g, norm_post_g=v_norm_post_g, norm_mem_g=v_norm_mem_g, w_in=v_w_in[0],
              b_forget=v_b_forget, b_merge=v_b_merge, w_mem_kv=v_w_mem_kv[0], w_branch_a=v_w_branch_a[0],
              w_branch_b=v_w_branch_b[0], w_branch_m=v_w_branch_m[0], w_out=v_w_out[0])
    s = x.shape[1]
    c = lax.axis_index("c")

    chip = 2 * lax.axis_index("x") + lax.axis_index("y")

    def put(whole, own, slot):
        return lax.dynamic_update_index_in_dim(whole, own.astype(whole.dtype), slot, 0)

    own_w = [w["w_in"].astype(BF16).reshape(2, D_MODEL // 2, SHARD_COLS),
             _rest_pack(w).astype(BF16).reshape(2, REST_ROWS // 2, D_MODEL)]
    all_in, all_rest = _gather_weights(own_w)
    all_in = all_in.reshape(N_CHIPS, D_MODEL, SHARD_COLS)
    own_in, own_rest = own_w[0].reshape(D_MODEL, SHARD_COLS), own_w[1].reshape(REST_ROWS, D_MODEL)
    w_in_f = jnp.concatenate([jnp.where(chip == p, own_in, all_in[p]) for p in range(N_CHIPS)], axis=1)
    all_rest = all_rest.reshape(N_CHIPS, REST_ROWS, D_MODEL)
    all_rest = jnp.stack([jnp.where(chip == p, own_rest, all_rest[p]) for p in range(N_CHIPS)])
    w_kv_f = all_rest[:, 0:256].reshape(D_MODEL, D_MODEL)
    w_br_f = [all_rest[:, 256 + 128 * i:384 + 128 * i].reshape(N_CHIPS, 512, 256).transpose(1, 0, 2)
              .reshape(512, D_MODEL) for i in range(3)]
    w_out_f = all_rest[:, 640:896].reshape(D_MODEL, D_MODEL)
    w_main = jnp.concatenate([w_in_f[:, :FB_ORIG], w_in_f[:, FB_ORIG + B_HEADS:]], axis=1)
    w_fb = jnp.concatenate([w_in_f[:, FB_ORIG:FB_ORIG + B_HEADS], jnp.zeros((D_MODEL, HD - B_HEADS), BF16)], axis=1)

    pair = []

    def exchange(g):
        def per_chip(name, p):
            a = g[name]
            if name in ("w_mem_kv", "w_out"):
                return a[256 * p:256 * (p + 1)]
            return a[:, 256 * p:256 * (p + 1)]

        in4 = jnp.stack([g["w_in"][:, SHARD_COLS * p:SHARD_COLS * (p + 1)] for p in range(N_CHIPS)])
        rest4 = jnp.stack([_rest_pack({n: per_chip(n, p) for n, _, _ in REST_SPLITS}) for p in range(N_CHIPS)])
        halves = [in4.reshape(N_CHIPS, 2, D_MODEL // 2, SHARD_COLS),
                  rest4.reshape(N_CHIPS, 2, REST_ROWS // 2, D_MODEL)]
        got = _swap_with_sibling(halves)
        pair.extend(_add_pair("add_pair_%d" % i, halves[i], got[i], c) for i in range(2))
        return _scatter_to_owners(pair)

    loss_lanes, grad_x, g, landed = _local_step(
        x[0], mem[0], positions.reshape(s, 1), loss_target[0], norm_pre_g, norm_post_g, norm_mem_g,
        w_main, w_fb, b_forget, b_merge, w_kv_f, w_br_f[0], w_br_f[1], w_br_f[2], w_out_f, exchange)
    loss = lax.psum(jnp.sum(loss_lanes), ("x", "y", "c"))
    half = [_add_chips("add_chips_%d" % i, landed[i], pair[i], chip) for i in range(2)]
    red_in, red_rest = [put(a, o[0], c) for a, o in zip(_share_with_sibling(half), half)]
    gs = {"w_in": red_in.reshape(D_MODEL, SHARD_COLS)}
    gs.update(_rest_unpack(red_rest.reshape(REST_ROWS, D_MODEL), {n: w[n].shape for n, _, _ in REST_SPLITS}))
    gs.update(_small_unpack(_sum_small(_small_pack(
        g["norm_pre_g"], g["norm_post_g"], g["norm_mem_g"], g["b_forget"], g["b_merge"]))))

    delta, new_m, new_v = {}, {}, {}
    for n, tm in (("w_in", 128), ("w_mem_kv", 256), ("w_branch_a", 512), ("w_branch_b", 512),
                  ("w_branch_m", 512), ("w_out", 256)):
        d_, m_, v_ = _adamw("adamw_" + n, w[n], gs[n], mo[n], vo[n], tm)
        delta[n], new_m[n], new_v[n] = d_[None], m_[None], v_[None]
        gs[n] = gs[n][None]
    packs = [_small_pack(*[t[n] for n in SMALL])
             for t in (w, gs, mo, vo)]
    for res, store in zip(_adamw("adamw_small", *packs, 8), (delta, new_m, new_v)):
        store.update(_small_unpack(res))

    return (loss, grad_x[None], *[gs[n] for n in WEIGHTS], *[delta[n] for n in WEIGHTS],
            *[new_m[n] for n in WEIGHTS], *[new_v[n] for n in WEIGHTS])
```

```python
import functools

import jax
import jax.numpy as jnp
from jax import lax
from jax.experimental import pallas as pl
from jax.experimental.pallas import tpu as pltpu

F32 = jnp.float32
BF16 = jnp.bfloat16
MESH = pl.DeviceIdType.MESH

D_MODEL = 1024
N_MEM = 256
EPS = 1e-6
NEG = -1e30
ROPE_THETA = 500000.0
ROT_DIM = 32
HD = 128
A_GROUP = 512
DILATIONS = (1, 4, 16)
BAND = 128
B_HEADS = 8
B_HD = 64
N_CHIPS = 4
N_DEV = 8

C_QA, C_KA, C_VA, C_ZA = 0, 1536, 3072, 4608
C_QB, C_KB, C_VB, C_ZB = 5120, 5632, 6144, 6656
C_QM, C_ZM, C_GL = 7168, 7680, 8192
FB_ORIG = 6656
IN_COLS = 11272
SHARD_COLS = IN_COLS // N_CHIPS

ADAM_LR, ADAM_B1, ADAM_B2, ADAM_EPS, ADAM_WD, ADAM_STEP = 0.001, 0.9, 0.999, 1e-08, 0.01, 10

VMEM_LIMIT_V7X = 56 * 1024 * 1024

NT = (((1,), (1,)), ((), ()))
NN = (((1,), (0,)), ((), ()))
TN = (((0,), (0,)), ((), ()))


def _params(sem):
    return pltpu.CompilerParams(dimension_semantics=sem, vmem_limit_bytes=VMEM_LIMIT_V7X)


def _dot(a, b, dn=NN):
    return lax.dot_general(a, b, dn, preferred_element_type=F32)


def _sig(z):
    return 1.0 / (1.0 + jnp.exp(-z))


def _rows(name, fn, row_ins, bc_ins, outs, reds=(), tm=512, scratch=(), into=None):
    arrs, specs = [], []
    s = None
    for r in row_ins:
        arr, w, cb, d = (tuple(r) + (1,))[:4] if isinstance(r, tuple) else (r, r.shape[1], 0, 1)
        s = arr.shape[0] * d if s is None else s
        arrs.append(arr)
        specs.append((w, cb, d))
    tm = min(tm, s)
    specs = [pl.BlockSpec((tm // d, w), functools.partial(lambda i, cb: (i, cb), cb=cb)) for w, cb, d in specs]
    for b in bc_ins:
        arrs.append(b)
        specs.append(pl.BlockSpec(b.shape, lambda i: (0, 0)))
    outs = [(tuple(o) + (1,))[:3] for o in outs]
    n_in, n_out = len(arrs), len(outs)
    o0 = n_in + (0 if into is None else 1)

    def body(*refs):
        n_ref = o0 + n_out + len(reds)
        vals = fn(*[r[...] for r in refs[:n_in]], *refs[n_ref:])
        if not isinstance(vals, (tuple, list)):
            vals = (vals,)
        for r, v in zip(refs[o0:o0 + n_out], vals[:n_out]):
            r[...] = v.astype(r.dtype)
        if reds:
            red_refs = refs[o0 + n_out:n_ref]

            @pl.when(pl.program_id(0) == 0)
            def _():
                for r in red_refs:
                    r[...] = jnp.zeros_like(r)

            for r, v in zip(red_refs, vals[n_out:]):
                r[...] += v

    out_shape = [jax.ShapeDtypeStruct((s // d, c), dt) for c, dt, d in outs]
    out_shape += [jax.ShapeDtypeStruct((1, c), F32) for c in reds]
    out_specs = [pl.BlockSpec((tm // d, c), lambda i: (i, 0)) for c, _, d in outs]
    out_specs += [pl.BlockSpec((1, c), lambda i: (0, 0)) for c in reds]
    aliases = {}
    if into is not None:
        whole, k, cb = into
        out_shape[k] = jax.ShapeDtypeStruct(whole.shape, whole.dtype)
        out_specs[k] = pl.BlockSpec((tm, outs[k][0]), functools.partial(lambda i, cb: (i, cb), cb=cb))
        aliases = {n_in: k}
        arrs.append(whole)
        specs.append(pl.BlockSpec(memory_space=pl.ANY))
    res = pl.pallas_call(
        body, name=name, grid=(s // tm,), in_specs=specs, out_specs=out_specs, out_shape=out_shape,
        scratch_shapes=list(scratch), input_output_aliases=aliases,
        compiler_params=_params(("arbitrary",) if reds else ("parallel",)),
    )(*arrs)
    return res


def _to_class(x, scr, d):
    if d == 1:
        return x.astype(F32)
    tm, c = x.shape
    for g in range(c // 128):
        scr[g][...] = x[:, g * 128:(g + 1) * 128].astype(F32)
    return jnp.concatenate([scr[g][pl.ds(r, tm // d, stride=d), :] for r in range(d) for g in range(c // 128)],
                           axis=1)


def _from_class(x, scr, d):
    if d == 1:
        return x.astype(F32)
    n, dc = x.shape
    c = dc // d
    for r in range(d):
        for g in range(c // 128):
            scr[g][pl.ds(r, n, stride=d), :] = x[:, r * c + g * 128:r * c + (g + 1) * 128].astype(F32)
    return jnp.concatenate([scr[g][...] for g in range(c // 128)], axis=1)


def _mm(name, a, b, mode, out_dtype, tm=1024, tn=1024, tk=1024, side=None):
    if mode == "nn":
        (m, k), (_, n) = a.shape, b.shape
    elif mode == "nt":
        (m, k), (n, _) = a.shape, b.shape
    else:
        (k, m), (_, n) = a.shape, b.shape
    tm, tn, tk = min(tm, m), min(tn, n), min(tk, k)
    nk = k // tk
    grid = (m // tm, n // tn, nk)
    dn = {"nn": NN, "nt": NT, "tn": TN}[mode]
    n_si = len(side["ins"]) if side else 0
    n_so = len(side["outs"]) if side else 0
    n_acc = 1 if nk > 1 else 0

    def body(*refs):
        a_ref, b_ref = refs[:2]
        side_in, o_ref = refs[2:2 + n_si], refs[2 + n_si]
        side_out = refs[3 + n_si:3 + n_si + n_so]
        acc = refs[3 + n_si + n_so:3 + n_si + n_so + n_acc]
        side_scratch = refs[3 + n_si + n_so + n_acc:]
        step = (pl.program_id(0) * grid[1] + pl.program_id(1)) * grid[2] + pl.program_id(2)
        if side:
            @pl.when(step == 0)
            def _():
                side["start"](side_in, side_out, side_scratch)

        part = _dot(a_ref[...].astype(BF16), b_ref[...].astype(BF16), dn)
        if nk == 1:
            o_ref[...] = part.astype(o_ref.dtype)
        else:
            kk = pl.program_id(2)

            @pl.when(kk == 0)
            def _():
                acc[0][...] = part

            @pl.when(kk > 0)
            def _():
                acc[0][...] += part

            @pl.when(kk == nk - 1)
            def _():
                o_ref[...] = acc[0][...].astype(o_ref.dtype)

        if side:
            @pl.when(step == grid[0] * grid[1] * grid[2] - 1)
            def _():
                side["wait"](side_in, side_out, side_scratch)

    a_spec = (pl.BlockSpec((tk, tm), lambda i, j, kk: (kk, i)) if mode == "tn"
              else pl.BlockSpec((tm, tk), lambda i, j, kk: (i, kk)))
    b_spec = (pl.BlockSpec((tn, tk), lambda i, j, kk: (j, kk)) if mode == "nt"
              else pl.BlockSpec((tk, tn), lambda i, j, kk: (kk, j)))
    o_spec = pl.BlockSpec((tm, tn), lambda i, j, kk: (i, j))
    o_shape = jax.ShapeDtypeStruct((m, n), out_dtype)
    acc_scratch = [pltpu.VMEM((tm, tn), F32)] * n_acc
    if not side:
        return pl.pallas_call(
            body, name=name, grid=grid, in_specs=[a_spec, b_spec], out_specs=o_spec, out_shape=o_shape,
            scratch_shapes=acc_scratch, compiler_params=_params(("parallel", "parallel", "arbitrary")),
        )(a, b)
    return pl.pallas_call(
        body, name=name, grid=grid, in_specs=[a_spec, b_spec] + [HBM_SPEC] * n_si,
        out_specs=[o_spec] + [HBM_SPEC] * n_so, out_shape=[o_shape] + side["outs"],
        scratch_shapes=acc_scratch + side["scratch"],
        compiler_params=_params(("arbitrary", "arbitrary", "arbitrary")),
    )(a, b, *side["ins"])


def _rms_fwd(name, x, g):
    def fn(xv, gv):
        r = lax.rsqrt(jnp.mean(xv * xv, axis=-1, keepdims=True) + EPS)
        return (xv * r * gv,)
    return _rows(name, fn, [x], [g], [(x.shape[1], BF16)], tm=min(512, x.shape[0]))[0]


def _rope_tables(pos, inv):
    ang = pos.astype(F32) * inv
    lane = lax.broadcasted_iota(jnp.int32, ang.shape, 1)
    c = jnp.where(lane < ROT_DIM, jnp.cos(ang), 1.0)
    sn = jnp.sin(ang)
    sg = jnp.where(lane < ROT_DIM // 2, -sn, jnp.where(lane < ROT_DIM, sn, 0.0))
    return c, sg, lane


def _rope_apply(x, c, sg, lane):
    outs = []
    for h in range(x.shape[1] // HD):
        xh = x[:, h * HD:(h + 1) * HD].astype(F32)
        swap = jnp.where(lane < ROT_DIM // 2, pltpu.roll(xh, HD - ROT_DIM // 2, 1),
                         pltpu.roll(xh, ROT_DIM // 2, 1))
        outs.append(xh * c + swap * sg)
    return jnp.concatenate(outs, axis=1)


ROPE_TM = 256


def _class_scratch(tm):
    return [pltpu.VMEM((tm, 128), F32) for _ in range(A_GROUP // 128)]


def _rope_fwd(u, pos, inv):
    def fn(q, k, v, p, iv, *scr):
        c, sg, lane = _rope_tables(p, iv)
        qr, kr = _rope_apply(q, c, sg, lane), _rope_apply(k, c, sg, lane)
        outs = []
        for g, d in enumerate(DILATIONS):
            gs = slice(g * A_GROUP, (g + 1) * A_GROUP)
            outs += [_to_class(qr[:, gs], scr, d), _to_class(kr[:, gs], scr, d), _to_class(v[:, gs], scr, d)]
        return tuple(outs)

    outs = [(d * A_GROUP, BF16, d) for d in DILATIONS for _ in range(3)]
    qkv = [(u, 3 * A_GROUP, c0 // (3 * A_GROUP)) for c0 in (C_QA, C_KA, C_VA)]
    return _rows("rope_fwd", fn, qkv + [pos], [inv], outs, tm=ROPE_TM,
                 scratch=_class_scratch(ROPE_TM))


def _rope_bwd(dqs, dks, dvs, pos, inv, du):
    s = du.shape[0]
    tm = min(2 * ROPE_TM, s)
    n_g = len(DILATIONS)

    def body(*refs):
        grads, p_ref, iv_ref = refs[:3 * n_g], refs[3 * n_g], refs[3 * n_g + 1]
        o_ref, scr = refs[3 * n_g + 3], refs[3 * n_g + 4:]
        part = pl.program_id(0)
        for k in range(3):
            @pl.when(part == k)
            def _(k=k):
                tok = jnp.concatenate([_from_class(grads[n_g * k + g][...], scr, d)
                                       for g, d in enumerate(DILATIONS)], axis=1)
                if k < 2:
                    c, sg, lane = _rope_tables(p_ref[...], iv_ref[...])
                    tok = _rope_apply(tok, c, -sg, lane)
                o_ref[...] = tok.astype(o_ref.dtype)

    in_specs = [pl.BlockSpec((tm // d, d * A_GROUP),
                             functools.partial(lambda p, i, k: (jnp.where(p == k, i, 0), 0), k=k))
                for k in range(3) for d in DILATIONS]
    in_specs += [pl.BlockSpec((tm, 1), lambda p, i: (i, 0)), pl.BlockSpec(inv.shape, lambda p, i: (0, 0)),
                 pl.BlockSpec(memory_space=pl.ANY)]
    return pl.pallas_call(
        body, name="rope_bwd", grid=(3, s // tm), in_specs=in_specs,
        out_specs=pl.BlockSpec((tm, 3 * A_GROUP), lambda p, i: (i, p)),
        out_shape=jax.ShapeDtypeStruct(du.shape, du.dtype), scratch_shapes=_class_scratch(tm),
        input_output_aliases={3 * n_g + 2: 0},
        compiler_params=_params(("arbitrary", "arbitrary")),
    )(*dqs, *dks, *dvs, pos, inv, du)


def _lane_pack(cols, like):
    lane = lax.broadcasted_iota(jnp.int32, like, 1)
    out = jnp.zeros(like, F32)
    for h, cvec in enumerate(cols):
        out = jnp.where(lane == h, cvec, out)
    return out


def _band_specs(l, d, tq):
    nsb = tq // BAND
    nblk = l // BAND
    cur = pl.BlockSpec((tq, A_GROUP), lambda r, i: (i, r))
    prev = pl.BlockSpec((BAND, A_GROUP), lambda r, i: (jnp.maximum(i * nsb - 1, 0), r))
    nxt = pl.BlockSpec((BAND, A_GROUP), lambda r, i: (jnp.minimum((i + 1) * nsb, nblk - 1), r))
    st_cur = pl.BlockSpec((tq, HD), lambda r, i: (i, r))
    st_nxt = pl.BlockSpec((BAND, HD), lambda r, i: (jnp.minimum((i + 1) * nsb, nblk - 1), r))
    return nsb, cur, prev, nxt, st_cur, st_nxt


def _band_mask_q(i, first_tile):
    qr = lax.broadcasted_iota(jnp.int32, (BAND, 2 * BAND), 0)
    kc = lax.broadcasted_iota(jnp.int32, (BAND, 2 * BAND), 1)
    in_prev = (kc < BAND) & (kc >= qr)
    in_cur = (kc >= BAND) & (kc - BAND <= qr)
    if i == 0:
        in_prev = in_prev & jnp.logical_not(first_tile)
    return in_prev | in_cur


def _band_mask_k(j, nsb, last_tile):
    kc = lax.broadcasted_iota(jnp.int32, (BAND, 2 * BAND), 0)
    qr = lax.broadcasted_iota(jnp.int32, (BAND, 2 * BAND), 1)
    same = (qr < BAND) & (kc <= qr)
    nxt = (qr >= BAND) & (kc >= qr - BAND)
    if j == nsb - 1:
        nxt = nxt & jnp.logical_not(last_tile)
    return same | nxt


def _band_fwd(name, q, k, v, d):
    l = q.shape[0]
    tq = min(512, l)
    nsb, cur, prev, _, st_cur, _ = _band_specs(l, d, tq)
    scale = HD ** -0.5

    def body(q_ref, kc_ref, kp_ref, vc_ref, vp_ref, o_ref, lse_ref):
        first = pl.program_id(1) == 0
        for i in range(nsb):
            lses = []
            mask = _band_mask_q(i, first)
            for h in range(4):
                cs = slice(h * HD, (h + 1) * HD)
                qv = q_ref[i * BAND:(i + 1) * BAND, cs]
                if i == 0:
                    kk = jnp.concatenate([kp_ref[:, cs], kc_ref[0:BAND, cs]], axis=0)
                    vv = jnp.concatenate([vp_ref[:, cs], vc_ref[0:BAND, cs]], axis=0)
                else:
                    kk = kc_ref[(i - 1) * BAND:(i + 1) * BAND, cs]
                    vv = vc_ref[(i - 1) * BAND:(i + 1) * BAND, cs]
                s = jnp.where(mask, _dot(qv, kk, NT) * scale, NEG)
                m = jnp.max(s, axis=-1, keepdims=True)
                p = jnp.exp(s - m)
                den = jnp.sum(p, axis=-1, keepdims=True)
                o_ref[i * BAND:(i + 1) * BAND, cs] = _dot(p.astype(BF16), vv) / den
                lses.append(m + jnp.log(den))
            lse_ref[i * BAND:(i + 1) * BAND, :] = _lane_pack(lses, (BAND, HD))

    return pl.pallas_call(
        body, name=name, grid=(d, l // tq), in_specs=[cur, cur, prev, cur, prev],
        out_specs=[cur, st_cur],
        out_shape=[jax.ShapeDtypeStruct((l, d * A_GROUP), F32), jax.ShapeDtypeStruct((l, d * HD), F32)],
        compiler_params=_params(("parallel", "parallel")),
    )(q, k, k, v, v)


def _band_dq(name, q, k, v, dy, lse, delta, d):
    l = q.shape[0]
    tq = min(512, l)
    nsb, cur, prev, _, st_cur, _ = _band_specs(l, d, tq)
    scale = HD ** -0.5

    def body(q_ref, kc_ref, kp_ref, vc_ref, vp_ref, dy_ref, lse_ref, dl_ref, dq_ref):
        first = pl.program_id(1) == 0
        for i in range(nsb):
            mask = _band_mask_q(i, first)
            rs = slice(i * BAND, (i + 1) * BAND)
            for h in range(4):
                cs = slice(h * HD, (h + 1) * HD)
                if i == 0:
                    kk = jnp.concatenate([kp_ref[:, cs], kc_ref[0:BAND, cs]], axis=0)
                    vv = jnp.concatenate([vp_ref[:, cs], vc_ref[0:BAND, cs]], axis=0)
                else:
                    kk = kc_ref[(i - 1) * BAND:(i + 1) * BAND, cs]
                    vv = vc_ref[(i - 1) * BAND:(i + 1) * BAND, cs]
                s = jnp.where(mask, _dot(q_ref[rs, cs], kk, NT) * scale, NEG)
                p = jnp.exp(s - lse_ref[rs, h:h + 1])
                dp = _dot(dy_ref[rs, cs], vv, NT)
                ds = p * (dp - dl_ref[rs, h:h + 1])
                dq_ref[rs, cs] = (_dot(ds.astype(BF16), kk) * scale).astype(dq_ref.dtype)

    return pl.pallas_call(
        body, name=name, grid=(d, l // tq),
        in_specs=[cur, cur, prev, cur, prev, cur, st_cur, st_cur], out_specs=cur,
        out_shape=jax.ShapeDtypeStruct((l, d * A_GROUP), BF16),
        compiler_params=_params(("parallel", "parallel")),
    )(q, k, k, v, v, dy, lse, delta)


def _band_dkv(name, q, k, v, dy, lse, delta, d):
    l = q.shape[0]
    tq = min(512, l)
    nsb, cur, _, nxt, st_cur, st_nxt = _band_specs(l, d, tq)
    scale = HD ** -0.5
    ntile = l // tq

    def body(k_ref, v_ref, qc_ref, qn_ref, dyc_ref, dyn_ref, lc_ref, ln_ref, dc_ref, dn_ref,
             dk_ref, dv_ref):
        last = pl.program_id(1) == ntile - 1

        def win(c_ref, n_ref, j, cs):
            if j == nsb - 1:
                return jnp.concatenate([c_ref[j * BAND:(j + 1) * BAND, cs], n_ref[:, cs]], axis=0)
            return c_ref[j * BAND:(j + 2) * BAND, cs]

        allh = slice(0, HD)
        for j in range(nsb):
            mask = _band_mask_k(j, nsb, last)
            rs = slice(j * BAND, (j + 1) * BAND)
            lse_t = win(lc_ref, ln_ref, j, allh).T
            delta_t = win(dc_ref, dn_ref, j, allh).T
            for h in range(4):
                cs = slice(h * HD, (h + 1) * HD)
                qw = win(qc_ref, qn_ref, j, cs)
                dyw = win(dyc_ref, dyn_ref, j, cs)
                st = jnp.where(mask, _dot(k_ref[rs, cs], qw, NT) * scale, NEG)
                pt = jnp.exp(st - lse_t[h:h + 1, :])
                dst = pt * (_dot(v_ref[rs, cs], dyw, NT) - delta_t[h:h + 1, :])
                dv_ref[rs, cs] = _dot(pt.astype(BF16), dyw).astype(dv_ref.dtype)
                dk_ref[rs, cs] = (_dot(dst.astype(BF16), qw) * scale).astype(dk_ref.dtype)

    shp = jax.ShapeDtypeStruct((l, d * A_GROUP), BF16)
    return pl.pallas_call(
        body, name=name, grid=(d, ntile),
        in_specs=[cur, cur, cur, nxt, cur, nxt, st_cur, st_nxt, st_cur, st_nxt],
        out_specs=[cur, cur], out_shape=[shp, shp],
        compiler_params=_params(("parallel", "parallel")),
    )(k, v, q, q, dy, dy, lse, lse, delta, delta)


def _split3(x):
    hi = x.astype(BF16)
    r1 = x - hi.astype(F32)
    mid = r1.astype(BF16)
    lo = (r1 - mid.astype(F32)).astype(BF16)
    return hi, mid, lo


def _fox_prep(z, b):
    h, s = z.shape
    blk = min(512, s)

    def body(z_ref, b_ref, c_ref):
        r = lax.broadcasted_iota(jnp.int32, (blk, blk), 0)
        cidx = lax.broadcasted_iota(jnp.int32, (blk, blk), 1)
        tri = (r <= cidx).astype(BF16)
        carry = jnp.zeros((h, 1), F32)
        for t in range(s // blk):
            zz = z_ref[:, t * blk:(t + 1) * blk] + b_ref[...]
            lf = jnp.minimum(zz, 0.0) - jnp.log(1.0 + jnp.exp(-jnp.abs(zz)))
            hi, mid, lo = _split3(lf)
            cs = _dot(hi, tri) + _dot(mid, tri) + _dot(lo, tri) + carry
            c_ref[:, t * blk:(t + 1) * blk] = cs
            carry = cs[:, blk - 1:blk]

    return pl.pallas_call(body, name="fox_prep", out_shape=jax.ShapeDtypeStruct((h, s), F32))(z, b)


def _fox_prep_bwd(dc, z, b):
    h, s = z.shape
    blk = min(512, s)

    def body(dc_ref, z_ref, b_ref, dz_ref, db_ref):
        r = lax.broadcasted_iota(jnp.int32, (blk, blk), 0)
        cidx = lax.broadcasted_iota(jnp.int32, (blk, blk), 1)
        tri = (r >= cidx).astype(BF16)
        carry = jnp.zeros((h, 1), F32)
        tot = jnp.zeros((h, 1), F32)
        for t in reversed(range(s // blk)):
            hi, mid, lo = _split3(dc_ref[:, t * blk:(t + 1) * blk])
            rc = _dot(hi, tri) + _dot(mid, tri) + _dot(lo, tri) + carry
            carry = rc[:, 0:1]
            zz = z_ref[:, t * blk:(t + 1) * blk] + b_ref[...]
            dz = rc * _sig(-zz)
            dz_ref[:, t * blk:(t + 1) * blk] = dz
            tot = tot + jnp.sum(dz, axis=-1, keepdims=True)
        db_ref[...] = tot

    return pl.pallas_call(
        body, name="fox_prep_bwd",
        out_shape=[jax.ShapeDtypeStruct((h, s), F32), jax.ShapeDtypeStruct((h, 1), F32)])(dc, z, b)


FOX_W = 128
FOX_C = B_HD
FOX_ONE = B_HD + 3
FOX_SUB = 256
FOX_SUB_FWD = 128
FOX_HEADS_PER_STEP = 2


def _head_of_pair(x, hh):
    return x if hh == 0 else pltpu.roll(x, B_HD, 1)


def _fox_pack(u, c_col, t):
    s = u.shape[0]
    nt = s // t
    scale = B_HD ** -0.5

    def body(q_ref, k_ref, v_ref, c_ref, qf_ref, kb_ref, ks_ref, vb_ref, vt_ref):
        lane = lax.broadcasted_iota(jnp.int32, (t, FOX_W), 1)
        qv, kv, vv = [r[...].astype(F32) for r in (q_ref, k_ref, v_ref)]
        for hh in range(2):
            qf_ref[hh] = jnp.where(lane < B_HD, _head_of_pair(qv, hh), B_HD ** 0.5).astype(BF16)
            neg = c_ref[hh] * (-scale)
            hi = neg.astype(BF16).astype(F32)
            mid = (neg - hi).astype(BF16).astype(F32)
            lo = neg - hi - mid
            aux = jnp.where(lane == FOX_C, hi,
                            jnp.where(lane == FOX_C + 1, mid, jnp.where(lane == FOX_C + 2, lo, 0.0)))
            kb = jnp.where(lane < B_HD, _head_of_pair(kv, hh) * scale, aux)
            kb_ref[hh] = kb.astype(BF16)
            ks_ref[hh] = jnp.where(lane == FOX_ONE, 1.0, kb).T.astype(BF16)
            vb = jnp.where(lane < B_HD, _head_of_pair(vv, hh), 1.0)
            vb_ref[hh] = vb.astype(BF16)
            vt_ref[hh] = vb.T.astype(BF16)

    def tok(col0):
        return pl.BlockSpec((t, FOX_W), functools.partial(lambda hp, i, cb: (i, cb + hp), cb=col0 // FOX_W))

    rows = pl.BlockSpec((2, t, FOX_W), lambda hp, i: (hp, i, 0))
    tiles = pl.BlockSpec((2, None, FOX_W, t), lambda hp, i: (hp, i, 0, 0))
    hm = jax.ShapeDtypeStruct((B_HEADS, s, FOX_W), BF16)
    tt = jax.ShapeDtypeStruct((B_HEADS, nt, FOX_W, t), BF16)
    return pl.pallas_call(
        body, name="fox_pack", grid=(B_HEADS // 2, nt),
        in_specs=[tok(C_QB), tok(C_KB), tok(C_VB), pl.BlockSpec((2, t, 1), lambda hp, i: (hp, i, 0))],
        out_specs=[rows, rows, tiles, rows, tiles], out_shape=[hm, hm, tt, hm, tt],
        compiler_params=_params(("parallel", "parallel")),
    )(u, u, u, c_col)


def _fox_pack_bwd(dy, y, t):
    s = dy.shape[0]
    nt = s // t

    def body(do_ref, o_ref, dow_ref, dl_ref):
        lane = lax.broadcasted_iota(jnp.int32, (t, FOX_W), 1)
        lane8 = lax.broadcasted_iota(jnp.int32, (8, FOX_W), 1)
        dov = do_ref[...].astype(F32)
        parts = _split3(dov * o_ref[...].astype(F32))
        for hh in range(2):
            dow_ref[hh] = jnp.where(lane < B_HD, _head_of_pair(dov, hh), 0.0).astype(BF16)
            mask = ((lane8 >= hh * B_HD) & (lane8 < (hh + 1) * B_HD)).astype(BF16)
            row = _dot(mask, parts[0], NT) + _dot(mask, parts[1], NT) + _dot(mask, parts[2], NT)
            dl_ref[hh] = row[0:1, :]

    tok = pl.BlockSpec((t, FOX_W), lambda hp, i: (i, hp))
    return pl.pallas_call(
        body, name="fox_pack_bwd", grid=(B_HEADS // 2, nt), in_specs=[tok, tok],
        out_specs=[pl.BlockSpec((2, t, FOX_W), lambda hp, i: (hp, i, 0)),
                   pl.BlockSpec((2, None, 1, t), lambda hp, i: (hp, i, 0, 0))],
        out_shape=[jax.ShapeDtypeStruct((B_HEADS, s, FOX_W), BF16), jax.ShapeDtypeStruct((B_HEADS, nt, 1, t), F32)],
        compiler_params=_params(("parallel", "parallel")),
    )(dy, y)


def _fox_unpack(dqt, dkw, dvw, t):
    h, nt = dqt.shape[:2]
    s = nt * t

    def body(dq_ref, dk_ref, dv_ref, dqo_ref, dko_ref, dvo_ref, dc_ref):
        lane = lax.broadcasted_iota(jnp.int32, (t, FOX_W), 1)

        def join(a0, a1):
            return jnp.where(lane < B_HD, a0, pltpu.roll(a1, B_HD, 1))

        for hh in range(2):
            dc_ref[hh] = dq_ref[hh][FOX_ONE:FOX_ONE + 1, :] - dk_ref[hh].T[B_HD:B_HD + 1, :]
        dqo_ref[...] = join(dq_ref[0].T, dq_ref[1].T).astype(BF16)
        dko_ref[...] = join(dk_ref[0], dk_ref[1]).astype(BF16)
        dvo_ref[...] = join(dv_ref[0], dv_ref[1]).astype(BF16)

    tok = pl.BlockSpec((t, FOX_W), lambda hp, i: (i, hp))
    rows = pl.BlockSpec((2, t, FOX_W), lambda hp, i: (hp, i, 0))
    shp = jax.ShapeDtypeStruct((s, h * B_HD), BF16)
    return pl.pallas_call(
        body, name="fox_unpack", grid=(h // 2, nt),
        in_specs=[pl.BlockSpec((2, None, FOX_W, t), lambda hp, i: (hp, i, 0, 0)), rows, rows],
        out_specs=[tok, tok, tok, pl.BlockSpec((2, None, 1, t), lambda hp, i: (hp, i, 0, 0))],
        out_shape=[shp, shp, shp, jax.ShapeDtypeStruct((h, nt, 1, t), F32)],
        compiler_params=_params(("parallel", "parallel")),
    )(dqt, dkw, dvw)


FOX_DEAD = -110.0


def _fox_norm2(qf, kb):
    h, s, w = qf.shape
    tm = min(2048, s)

    def body(q_ref, k_ref, qo_ref, ko_ref):
        row = lax.broadcasted_iota(jnp.int32, (w, w), 0)
        ones = (row < B_HD).astype(BF16)
        for x_ref, o_ref in ((q_ref, qo_ref), (k_ref, ko_ref)):
            xv = x_ref[...].astype(F32)
            n2 = _dot((xv * xv).astype(BF16), ones)
            o_ref[...] = jnp.broadcast_to(jnp.max(n2, axis=0, keepdims=True)[:, :1], o_ref.shape)

    spec = pl.BlockSpec((None, tm, w), lambda hh, i: (hh, i, 0))
    ospec = pl.BlockSpec((None, None, 8, 128), lambda hh, i: (hh, i, 0, 0))
    shp = jax.ShapeDtypeStruct((h, s // tm, 8, 128), F32)
    return pl.pallas_call(
        body, name="fox_norm2", grid=(h, s // tm), in_specs=[spec, spec], out_specs=[ospec, ospec],
        out_shape=[shp, shp], compiler_params=_params(("parallel", "parallel")),
    )(qf, kb)


def _fox_bounds(qf, kb, c, t):
    q2, k2 = _fox_norm2(qf, kb)
    g = 2.0 * jnp.sqrt(1.02 * jnp.max(q2[:, :, 0, 0], axis=1) * 1.02 * jnp.max(k2[:, :, 0, 0], axis=1))
    return jnp.concatenate([c[:, ::t], c[:, t - 1::t], g[:, None]], axis=1)


SMEM_SPEC = pl.BlockSpec(memory_space=pltpu.SMEM)


def _fox_fwd(qf, kb, vt4, bounds, t):
    h, s, w = qf.shape
    nt = s // t
    sub = FOX_SUB_FWD
    nsub = t // sub
    nh = FOX_HEADS_PER_STEP

    def body(b_ref, q_ref, k_ref, v_ref, o_ref, lse_ref):
        i = pl.program_id(1)
        krow = lax.broadcasted_iota(jnp.int32, (sub, t), 0)
        qcol = lax.broadcasted_iota(jnp.int32, (sub, t), 1)

        def dead_before(hh):
            head = pl.program_id(0) * nh + hh
            top = b_ref[head, 2 * nt] + b_ref[head, i]
            return lax.fori_loop(
                0, i, lambda jj, n: n + (top - b_ref[head, nt + jj] < FOX_DEAD).astype(jnp.int32), 0)

        j_lo = functools.reduce(jnp.minimum, [dead_before(hh) for hh in range(nh)])

        def tile(j, carry, diag):
            out = []
            for hh in range(nh):
                m, acc = carry[hh]
                qv, vj = q_ref[hh], v_ref[hh, j]
                los = [b * sub if diag else 0 for b in range(nsub)]
                sts = [_dot(k_ref[hh, pl.ds(pl.multiple_of(j * t + b * sub, sub), sub), :], qv[lo:, :], NT)
                       for b, lo in enumerate(los)]
                for b, lo in enumerate(los):
                    st = sts[b]
                    if diag:
                        st = jnp.where(krow[:, :t - lo] <= qcol[:, :t - lo], st, NEG)
                    m_old, acc_old = m[:, lo:], acc[:, lo:]
                    m2 = jnp.maximum(m_old, jnp.max(st, axis=0, keepdims=True))
                    p = jnp.exp(st - m2).astype(BF16)
                    acc2 = jnp.exp(m_old - m2) * acc_old + _dot(vj[:, b * sub:(b + 1) * sub], p)
                    m = m2 if lo == 0 else jnp.concatenate([m[:, :lo], m2], axis=1)
                    acc = acc2 if lo == 0 else jnp.concatenate([acc[:, :lo], acc2], axis=1)
                out.append((m, acc))
            return tuple(out)

        init = tuple((jnp.full((1, t), NEG, F32), jnp.zeros((w, t), F32)) for _ in range(nh))
        carry = lax.fori_loop(j_lo, i, lambda j, c: tile(j, c, False), init)
        outs = []
        for hh, (m, acc) in enumerate(tile(i, carry, True)):
            den = acc[B_HD:B_HD + 1, :]
            outs.append(acc[0:B_HD, :] / den)
            lse_ref[hh] = m + jnp.log(den)
        o_ref[...] = jnp.concatenate(outs, axis=0).T.astype(o_ref.dtype)

    return pl.pallas_call(
        body, name="fox_fwd", grid=(h // nh, nt),
        in_specs=[SMEM_SPEC,
                  pl.BlockSpec((nh, t, w), lambda hh, i: (hh, i, 0)),
                  pl.BlockSpec((nh, s, w), lambda hh, i: (hh, 0, 0)),
                  pl.BlockSpec((nh, nt, w, t), lambda hh, i: (hh, 0, 0, 0))],
        out_specs=[pl.BlockSpec((t, nh * B_HD), lambda hh, i: (i, hh)),
                   pl.BlockSpec((nh, 1, t), lambda hh, i: (hh, 0, i))],
        out_shape=[jax.ShapeDtypeStruct((s, h * B_HD), BF16), jax.ShapeDtypeStruct((h, 1, s), F32)],
        compiler_params=_params(("parallel", "parallel")),
    )(bounds, qf, kb, vt4)


def _fox_bwd(qf, dow, lse_row, delta_row, kb, kst4, vb, bounds, t):
    h, s, w = qf.shape
    nt = s // t
    nsub = t // FOX_SUB
    nh = FOX_HEADS_PER_STEP

    def body(b_ref, q_ref, do_ref, lse_ref, dl_ref, k_ref, kt_ref, v_ref, dqt_ref, dk_ref, dv_ref, dk_acc, dv_acc):
        j = pl.program_id(1)

        def alive_after(hh):
            head = pl.program_id(0) * nh + hh
            top = b_ref[head, 2 * nt] - b_ref[head, nt + j]
            return lax.fori_loop(
                j + 1, nt, lambda ii, n: n + (top + b_ref[head, ii] >= FOX_DEAD).astype(jnp.int32), 0)

        i_hi = j + 1 + functools.reduce(jnp.maximum, [alive_after(hh) for hh in range(nh)])

        @pl.when(j == 0)
        def _():
            dqt_ref[...] = jnp.zeros_like(dqt_ref)

        dk_acc[...] = jnp.zeros_like(dk_acc)
        dv_acc[...] = jnp.zeros_like(dv_acc)
        krow = lax.broadcasted_iota(jnp.int32, (FOX_SUB, t), 0)
        qcol = lax.broadcasted_iota(jnp.int32, (FOX_SUB, t), 1)
        subs = [slice(b * FOX_SUB, (b + 1) * FOX_SUB) for b in range(nsub)]

        def tile(i, diag):
            i0 = pl.multiple_of(i * t, t)
            for hh in range(nh):
                qi, doi = q_ref[hh, pl.ds(i0, t), :], do_ref[hh, pl.ds(i0, t), :]
                lse, dl = lse_ref[hh, i], dl_ref[hh, i]
                los = [b * FOX_SUB if diag else 0 for b in range(nsub)]
                sts = [_dot(k_ref[hh, rs, :], qi[lo:, :], NT) for rs, lo in zip(subs, los)]
                dps = [_dot(v_ref[hh, rs, :], doi[lo:, :], NT) for rs, lo in zip(subs, los)]
                dq = None
                for b, (rs, lo) in enumerate(zip(subs, los)):
                    st = sts[b] - lse[:, lo:]
                    if diag:
                        st = jnp.where(krow[:, :t - lo] <= qcol[:, :t - lo], st, NEG)
                    pt = jnp.exp(st)
                    dsb = (pt * (dps[b] - dl[:, lo:])).astype(BF16)
                    dv_acc[hh, rs, :] += _dot(pt.astype(BF16), doi[lo:, :])
                    dk_acc[hh, rs, :] += _dot(dsb, qi[lo:, :])
                    part = _dot(kt_ref[hh, :, rs], dsb)
                    if lo:
                        part = jnp.concatenate([jnp.zeros((w, lo), F32), part], axis=1)
                    dq = part if dq is None else dq + part
                dqt_ref[hh, i] += dq

        def step(i, carry):
            tile(i, False)
            return carry

        tile(j, True)
        lax.fori_loop(j + 1, i_hi, step, 0)
        dk_ref[...] = dk_acc[...] * (B_HD ** -0.5)
        dv_ref[...] = dv_acc[...]

    full = pl.BlockSpec((nh, s, w), lambda hh, j: (hh, 0, 0))
    rowst = pl.BlockSpec((nh, nt, 1, t), lambda hh, j: (hh, 0, 0, 0))
    tl = pl.BlockSpec((nh, t, w), lambda hh, j: (hh, j, 0))
    return pl.pallas_call(
        body, name="fox_bwd", grid=(h // nh, nt),
        in_specs=[SMEM_SPEC, full, full, rowst, rowst, tl,
                  pl.BlockSpec((nh, None, w, t), lambda hh, j: (hh, j, 0, 0)), tl],
        out_specs=[pl.BlockSpec((nh, nt, w, t), lambda hh, j: (hh, 0, 0, 0)), tl, tl],
        out_shape=[jax.ShapeDtypeStruct((h, nt, w, t), F32), jax.ShapeDtypeStruct((h, s, w), F32),
                   jax.ShapeDtypeStruct((h, s, w), F32)],
        scratch_shapes=[pltpu.VMEM((nh, t, w), F32), pltpu.VMEM((nh, t, w), F32)],
        compiler_params=_params(("parallel", "arbitrary")),
    )(bounds, qf, dow, lse_row, delta_row, kb, kst4, vb)


def _mem_fwd(u, mkv, tq=512):
    s = u.shape[0]
    scale = HD ** -0.5

    def body(q_ref, mk_ref, mv_ref, o_ref, lse_ref):
        lses = []
        for h in range(4):
            cs = slice(h * HD, (h + 1) * HD)
            sc = _dot(q_ref[:, cs], mk_ref[:, cs], NT) * scale
            m = jnp.max(sc, axis=-1, keepdims=True)
            p = jnp.exp(sc - m)
            den = jnp.sum(p, axis=-1, keepdims=True)
            o_ref[:, cs] = (_dot(p.astype(BF16), mv_ref[:, cs]) / den).astype(o_ref.dtype)
            lses.append(m + jnp.log(den))
        lse_ref[...] = _lane_pack(lses, (tq, HD))

    return pl.pallas_call(
        body, name="mem_fwd", grid=(s // tq,),
        in_specs=[pl.BlockSpec((tq, 512), lambda i: (i, C_QM // 512)),
                  pl.BlockSpec((N_MEM, 512), lambda i: (0, 0)),
                  pl.BlockSpec((N_MEM, 512), lambda i: (0, 1))],
        out_specs=[pl.BlockSpec((tq, 512), lambda i: (i, 0)), pl.BlockSpec((tq, HD), lambda i: (i, 0))],
        out_shape=[jax.ShapeDtypeStruct((s, 512), BF16), jax.ShapeDtypeStruct((s, HD), F32)],
        compiler_params=_params(("parallel",)),
    )(u, mkv, mkv)


def _mem_bwd(u, mkv, o, do, lse, du, tq=512):
    s = u.shape[0]
    scale = HD ** -0.5

    def body(q_ref, mk_ref, mv_ref, o_ref, do_ref, lse_ref, _, dq_ref, dmk_ref, dmv_ref):
        @pl.when(pl.program_id(0) == 0)
        def _():
            dmk_ref[...] = jnp.zeros_like(dmk_ref)
            dmv_ref[...] = jnp.zeros_like(dmv_ref)

        for h in range(4):
            cs = slice(h * HD, (h + 1) * HD)
            qv, dov = q_ref[:, cs], do_ref[:, cs]
            sc = _dot(qv, mk_ref[:, cs], NT) * scale
            p = jnp.exp(sc - lse_ref[:, h:h + 1])
            delta = jnp.sum(dov.astype(F32) * o_ref[:, cs].astype(F32), axis=-1, keepdims=True)
            ds = p * (_dot(dov, mv_ref[:, cs], NT) - delta)
            dsb = ds.astype(BF16)
            dq_ref[:, cs] = (_dot(dsb, mk_ref[:, cs]) * scale).astype(dq_ref.dtype)
            dmk_ref[:, cs] += _dot(dsb, qv, TN) * scale
            dmv_ref[:, cs] += _dot(p.astype(BF16), dov, TN)

    row = pl.BlockSpec((tq, 512), lambda i: (i, 0))
    acc = pl.BlockSpec((N_MEM, 512), lambda i: (0, 0))
    return pl.pallas_call(
        body, name="mem_bwd", grid=(s // tq,),
        in_specs=[pl.BlockSpec((tq, 512), lambda i: (i, C_QM // 512)),
                  pl.BlockSpec((N_MEM, 512), lambda i: (0, 0)),
                  pl.BlockSpec((N_MEM, 512), lambda i: (0, 1)),
                  row, row, pl.BlockSpec((tq, HD), lambda i: (i, 0)), pl.BlockSpec(memory_space=pl.ANY)],
        out_specs=[pl.BlockSpec((tq, 512), lambda i: (i, C_QM // 512)), acc, acc],
        out_shape=[jax.ShapeDtypeStruct(du.shape, du.dtype), jax.ShapeDtypeStruct((N_MEM, 512), F32),
                   jax.ShapeDtypeStruct((N_MEM, 512), F32)],
        input_output_aliases={6: 0},
        compiler_params=_params(("arbitrary",)),
    )(u, mkv, mkv, o, do, lse, du)


FB_CHIP = FB_ORIG // SHARD_COLS
FB_AT = FB_ORIG - FB_CHIP * SHARD_COLS


def _chip_slabs(main, fb):
    cuts = [SHARD_COLS * p - (B_HEADS if p > FB_CHIP else 0) for p in range(N_CHIPS + 1)]
    slabs = [main[:, a:b] for a, b in zip(cuts[:-1], cuts[1:])]
    own = slabs[FB_CHIP]
    slabs[FB_CHIP] = jnp.concatenate([own[:, :FB_AT], fb, own[:, FB_AT:]], axis=1)
    return slabs


def _split_forget(slabs):
    own = slabs[FB_CHIP]
    parts = list(slabs[:FB_CHIP]) + [own[:, :FB_AT], own[:, FB_AT + B_HEADS:]] + list(slabs[FB_CHIP + 1:])
    return jnp.concatenate(parts, axis=1), own[:, FB_AT:FB_AT + B_HEADS]


def _local_step(x, mem, pos, target, g_pre, g_post, g_mem, w_main, w_fb, b_forget, b_merge,
                w_mem_kv, w_ba, w_bb, w_bm, w_out, exchange=None):
    s = x.shape[0]
    t_fox = min(512, s)
    nt = s // t_fox
    half = ROT_DIM // 2
    inv = ROPE_THETA ** (-jnp.arange(half, dtype=F32) / half)
    inv128 = jnp.concatenate([inv, inv, jnp.zeros((HD - ROT_DIM,), F32)]).reshape(1, HD)

    h = _rms_fwd("norm_pre", x, g_pre)
    u = _mm("proj_in", h, w_main, "nn", BF16, tm=4096)
    ufb = _mm("proj_fb", h, w_fb, "nn", F32)
    memn = _rms_fwd("norm_mem", mem, g_mem)
    mkv = _mm("proj_mem", memn, w_mem_kv, "nn", BF16)

    qkv = _rope_fwd(u, pos, inv128)
    views = [tuple(qkv[3 * g:3 * g + 3]) for g in range(3)]
    os_, lses = [], []
    for g, d in enumerate(DILATIONS):
        o_g, lse_g = _band_fwd("band_fwd%d" % g, *views[g], d)
        os_.append((o_g, d * A_GROUP, 0, d))
        lses.append((lse_g, d * HD, 0, d))

    def merge_a(o1, o2, o3, l1, l2, l3, za, *scr):
        o1, o2, o3 = [_from_class(o, scr, d) for o, d in zip((o1, o2, o3), DILATIONS)]
        l1, l2, l3 = [_from_class(lv, scr, d) for lv, d in zip((l1, l2, l3), DILATIONS)]
        ys, tots = [], []
        for hh in range(4):
            cs, hs = slice(hh * HD, (hh + 1) * HD), slice(hh, hh + 1)
            mx = jnp.maximum(jnp.maximum(l1[:, hs], l2[:, hs]), l3[:, hs])
            e1, e2, e3 = jnp.exp(l1[:, hs] - mx), jnp.exp(l2[:, hs] - mx), jnp.exp(l3[:, hs] - mx)
            den = e1 + e2 + e3
            ys.append((e1 * o1[:, cs] + e2 * o2[:, cs] + e3 * o3[:, cs]) / den)
            tots.append(mx + jnp.log(den))
        y = jnp.concatenate(ys, axis=1)
        zf = za.astype(F32)
        tot = _lane_pack(tots, l1.shape)
        return (y, y * (zf * _sig(zf))) + tuple(_to_class(tot, scr, d) for d in DILATIONS)

    res = _rows("merge_a", merge_a, os_ + lses + [(u, 512, C_ZA // 512)], [],
                [(512, BF16), (512, BF16)] + [(d * HD, F32, d) for d in DILATIONS], tm=ROPE_TM,
                scratch=_class_scratch(ROPE_TM))
    y_a, yg_a, lse_a = res[0], res[1], res[2:5]

    zrow = ufb[:, :B_HEADS].T
    c = _fox_prep(zrow, b_forget.reshape(B_HEADS, 1))
    qf, kb, kst4, vb, vt4 = _fox_pack(u, c.reshape(B_HEADS, s, 1), t_fox)
    bounds = _fox_bounds(qf, kb, c, t_fox)
    y_b, lse_b = _fox_fwd(qf, kb, vt4, bounds, t_fox)

    y_m, lse_m = _mem_fwd(u, mkv)

    def gate(y, z):
        zf = z.astype(F32)
        return (y.astype(F32) * (zf * _sig(zf)),)

    yg_b = _rows("gate_b", gate, [y_b, (u, 512, C_ZB // 512)], [], [(512, BF16)])[0]
    yg_m = _rows("gate_m", gate, [y_m, (u, 512, C_ZM // 512)], [], [(512, BF16)])[0]

    br_a = _mm("branch_a", yg_a, w_ba, "nn", BF16)
    br_b = _mm("branch_b", yg_b, w_bb, "nn", BF16)
    br_m = _mm("branch_m", yg_m, w_bm, "nn", BF16)
    gl = [(u, 1024, C_GL // 1024 + i) for i in range(3)]
    bm3 = b_merge.reshape(3, D_MODEL)

    def merge(g0, g1, g2, b0, b1, b2, bm):
        tot = 0.0
        for i, (gv, bv) in enumerate(((g0, b0), (g1, b1), (g2, b2))):
            tot = tot + _sig(gv.astype(F32) + bm[i:i + 1, :]) * bv.astype(F32)
        return (tot,)

    merged = _rows("merge_gates", merge, gl + [br_a, br_b, br_m], [bm3], [(D_MODEL, BF16)])[0]
    out = _mm("proj_out", merged, w_out, "nn", F32)

    def tail(xv, ov, tv, gv):
        r = lax.rsqrt(jnp.mean(ov * ov, axis=-1, keepdims=True) + EPS)
        n = ov * r
        err = xv + n * gv - tv
        dy = err * (1.0 / D_MODEL)
        dn = dy * gv
        dout = r * (dn - n * jnp.mean(dn * n, axis=-1, keepdims=True))
        return (dy, dout, jnp.sum(0.5 * err * err * (1.0 / D_MODEL), axis=0, keepdims=True),
                jnp.sum(dy * n, axis=0, keepdims=True))

    dy, dout, loss_lanes, g_post_grad = _rows(
        "tail", tail, [x, out, target], [g_post], [(D_MODEL, F32), (D_MODEL, BF16)],
        reds=[D_MODEL, D_MODEL], tm=256)

    dmerged = _mm("d_merged", dout, w_out, "nt", BF16)
    gw_out = _mm("g_w_out", merged, dout, "tn", F32)

    def merge_bwd(dm, g0, g1, g2, b0, b1, b2, bm):
        dmf = dm.astype(F32)
        dbs, dgs, sums = [], [], []
        for i, (gv, bv) in enumerate(((g0, b0), (g1, b1), (g2, b2))):
            sg = _sig(gv.astype(F32) + bm[i:i + 1, :])
            dbs.append(dmf * sg)
            dg = dmf * bv.astype(F32) * sg * (1.0 - sg)
            dgs.append(dg)
            sums.append(jnp.sum(dg, axis=0, keepdims=True))
        return tuple(dbs + dgs + sums)

    res = _rows("merge_bwd", merge_bwd, [dmerged] + gl + [br_a, br_b, br_m], [bm3],
                [(D_MODEL, BF16)] * 6, reds=[D_MODEL] * 3, tm=256)
    dbr, dgl, g_bmerge = res[0:3], res[3:6], jnp.concatenate(res[6:9], axis=1)

    dyg, gw_branch = [], []
    for nm, dbv, wv, ygv in (("a", dbr[0], w_ba, yg_a), ("b", dbr[1], w_bb, yg_b), ("m", dbr[2], w_bm, yg_m)):
        dyg.append(_mm("d_yg_" + nm, dbv, wv, "nt", BF16))
        gw_branch.append(_mm("g_w_branch_" + nm, ygv, dbv, "tn", F32))

    def gate_bwd(dg, y, z):
        dgf, yf, zf = dg.astype(F32), y.astype(F32), z.astype(F32)
        sg = _sig(zf)
        return dgf * (zf * sg), dgf * yf * (sg * (1.0 + zf * (1.0 - sg)))

    def gate_bwd_a(dg, y, z, *scr):
        dyv, dz = gate_bwd(dg, y, z)
        prod = dyv * y.astype(F32)
        dl = [jnp.sum(prod[:, hh * HD:(hh + 1) * HD], axis=-1, keepdims=True) for hh in range(4)]
        delta = _lane_pack(dl, (dg.shape[0], HD))
        return ((dz,) + tuple(_to_class(dyv, scr, d) for d in DILATIONS)
                + tuple(_to_class(delta, scr, d) for d in DILATIONS))

    du = lax.empty(u.shape, BF16)
    res = _rows("gate_bwd_a", gate_bwd_a, [dyg[0], y_a, (u, 512, C_ZA // 512)], [],
                [(512, BF16)] + [(d * A_GROUP, BF16, d) for d in DILATIONS] + [(d * HD, F32, d) for d in DILATIONS],
                tm=ROPE_TM, scratch=_class_scratch(ROPE_TM), into=(du, 0, C_ZA // 512))
    du, dy_a, delta_a = res[0], res[1:4], res[4:7]
    dy_b, du = _rows("gate_bwd_b", gate_bwd, [dyg[1], y_b, (u, 512, C_ZB // 512)], [],
                     [(512, BF16), (512, BF16)], into=(du, 1, C_ZB // 512))
    dy_m, du = _rows("gate_bwd_m", gate_bwd, [dyg[2], y_m, (u, 512, C_ZM // 512)], [],
                     [(512, BF16), (512, BF16)], into=(du, 1, C_ZM // 512))

    du, dmk, dmv = _mem_bwd(u, mkv, y_m, dy_m, lse_m, du)
    dmkv = jnp.concatenate([dmk, dmv], axis=1)
    gw_mem_kv = _mm("g_w_mem_kv", memn, dmkv, "tn", F32)
    dmemn = _mm("d_memn", dmkv, w_mem_kv, "nt", F32)

    def mem_gain_grad(mv, dv):
        r = lax.rsqrt(jnp.mean(mv * mv, axis=-1, keepdims=True) + EPS)
        return (jnp.sum(dv * mv * r, axis=0, keepdims=True),)

    g_mem_grad = _rows("g_norm_mem", mem_gain_grad, [mem, dmemn], [], [], reds=[D_MODEL], tm=N_MEM)[0]

    dow, delta_b = _fox_pack_bwd(dy_b, y_b, t_fox)
    dqt, dkw, dvw = _fox_bwd(qf, dow, lse_b.reshape(B_HEADS, nt, 1, t_fox), delta_b, kb, kst4, vb, bounds, t_fox)
    dqb, dkb, dvb, dc = _fox_unpack(dqt, dkw, dvw, t_fox)
    dzrow, g_bforget = _fox_prep_bwd(dc.reshape(B_HEADS, s), zrow, b_forget.reshape(B_HEADS, 1))
    dfb = jnp.zeros((s, HD), BF16).at[:, :B_HEADS].set(dzrow.T.astype(BF16))

    dqs, dks, dvs = [], [], []
    for g, d in enumerate(DILATIONS):
        qv, kv, vv = views[g]
        dqs.append(_band_dq("band_dq%d" % g, qv, kv, vv, dy_a[g], lse_a[g], delta_a[g], d))
        dk_g, dv_g = _band_dkv("band_dkv%d" % g, qv, kv, vv, dy_a[g], lse_a[g], delta_a[g], d)
        dks.append(dk_g)
        dvs.append(dv_g)
    for col, piece in ((C_QB, dqb), (C_KB, dkb), (C_VB, dvb), (C_GL, dgl[0]), (C_GL + D_MODEL, dgl[1]),
                       (C_GL + 2 * D_MODEL, dgl[2])):
        du = lax.dynamic_update_slice(du, piece, (0, col))
    du = _rope_bwd(dqs, dks, dvs, pos, inv128, du)

    gw_main = _mm("g_w_main", h.T, du, "nn", F32, tk=2048)
    gw_fb = _mm("g_w_fb", h, dfb, "tn", F32)
    grads = dict(norm_post_g=g_post_grad, norm_mem_g=g_mem_grad, w_in=_chip_slabs(gw_main, gw_fb[:, :B_HEADS]),
                 b_forget=g_bforget.reshape(1, B_HEADS), b_merge=g_bmerge, w_mem_kv=gw_mem_kv,
                 w_branch_a=gw_branch[0], w_branch_b=gw_branch[1], w_branch_m=gw_branch[2], w_out=gw_out)
    side = exchange(grads) if exchange else None
    dh_main = _mm("d_h", du, w_main, "nt", F32, tk=2816, side=side)
    landed = None
    if side:
        dh_main, landed = dh_main[0], dh_main[1:]
    dh_fb = _mm("d_h_fb", dfb, w_fb, "nt", F32)

    def pre_bwd(xv, d1, d2, dyv, gv):
        r = lax.rsqrt(jnp.mean(xv * xv, axis=-1, keepdims=True) + EPS)
        n = xv * r
        dhv = d1 + d2
        dn = dhv * gv
        dx = r * (dn - n * jnp.mean(dn * n, axis=-1, keepdims=True))
        return dyv + dx, jnp.sum(dhv * n, axis=0, keepdims=True)

    grad_x, g_pre_grad = _rows("norm_pre_bwd", pre_bwd, [x, dh_main, dh_fb, dy], [g_pre],
                               [(D_MODEL, F32)], reds=[D_MODEL], tm=256)

    grads["norm_pre_g"] = g_pre_grad
    return loss_lanes, grad_x, grads, landed


HBM_SPEC = pl.BlockSpec(memory_space=pltpu.HBM)


def _place():
    x, y, c = lax.axis_index("x"), lax.axis_index("y"), lax.axis_index("c")
    chips = [(1 - x, y), (x, 1 - y), (1 - x, 1 - y)]
    return x, y, c, 2 * x + y, chips


N_CHUNKS = 4


def _units(parts, row_axis):
    units = []
    for i, a in enumerate(parts):
        ch = a.shape[row_axis] // N_CHUNKS
        units += [(i, pl.ds(k * ch, ch)) for k in range(N_CHUNKS)]
    return units


def _gather_weights(parts):
    n = len(parts)
    units = _units(parts, 1)
    nu = len(units)
    via_y = [(u % N_CHUNKS) < N_CHUNKS // 2 for u in range(nu)]

    def body(*refs):
        srcs, outs = refs[:n], refs[n:2 * n]
        send_sems, recv_sems = refs[2 * n:]
        x, y, c, p, _ = _place()
        me, sib = (x, y, c), (x, y, 1 - c)
        xn, yn, dg = (1 - x, y), (x, 1 - y), (1 - x, 1 - y)

        def cp(u, k, chip, half, to, from_src=False):
            i, rs = units[u]
            dst = outs[i].at[2 * chip[0] + chip[1], half, rs]
            return pltpu.make_async_remote_copy(
                src_ref=srcs[i].at[half, rs] if from_src else dst, dst_ref=dst, send_sem=send_sems.at[u, k],
                recv_sem=recv_sems.at[u, k], device_id=to, device_id_type=MESH)

        sent = []

        def go(copy):
            copy.start()
            sent.append(copy)

        for u in range(nu):
            go(cp(u, 0, (x, y), c, (*xn, c), from_src=True))
            go(cp(u, 1, (x, y), c, (*yn, c), from_src=True))
        for u in range(nu):
            cp(u, 0, xn, c, me).wait_recv()
            go(cp(u, 4, xn, c, sib))
            if via_y[u]:
                go(cp(u, 2, xn, c, (*yn, c)))
            cp(u, 1, yn, c, me).wait_recv()
            go(cp(u, 5, yn, c, sib))
            if not via_y[u]:
                go(cp(u, 3, yn, c, (*xn, c)))
        for u in range(nu):
            cp(u, 2 if via_y[u] else 3, dg, c, me).wait_recv()
            go(cp(u, 6, dg, c, sib))
        for u in range(nu):
            for k, chip in ((4, xn), (5, yn), (6, dg)):
                cp(u, k, chip, 1 - c, me).wait_recv()
        for copy in sent:
            copy.wait_send()

    return pl.pallas_call(
        body, name="gather_weights", in_specs=[HBM_SPEC] * n, out_specs=[HBM_SPEC] * n,
        out_shape=[jax.ShapeDtypeStruct((N_CHIPS,) + a.shape, a.dtype) for a in parts],
        scratch_shapes=[pltpu.SemaphoreType.DMA((nu, 7)), pltpu.SemaphoreType.DMA((nu, 7))],
    )(*parts)


def _swap_with_sibling(parts):
    n = len(parts)
    units = _units(parts, 2)

    def body(*refs):
        srcs, outs = refs[:n], refs[n:2 * n]
        send_sems, recv_sems = refs[2 * n:]
        x, y, c, _, _ = _place()
        cps = [pltpu.make_async_remote_copy(
            src_ref=srcs[i].at[q, 1 - c, rs], dst_ref=outs[i].at[q, rs], send_sem=send_sems.at[u, q],
            recv_sem=recv_sems.at[u, q], device_id=(x, y, 1 - c), device_id_type=MESH)
            for q in range(N_CHIPS) for u, (i, rs) in enumerate(units)]
        for cpy in cps:
            cpy.start()
        for cpy in cps:
            cpy.wait()

    return pl.pallas_call(
        body, name="swap_with_sibling", in_specs=[HBM_SPEC] * n, out_specs=[HBM_SPEC] * n,
        out_shape=[jax.ShapeDtypeStruct(a.shape[:1] + a.shape[2:], a.dtype) for a in parts],
        scratch_shapes=[pltpu.SemaphoreType.DMA((len(units), N_CHIPS)),
                        pltpu.SemaphoreType.DMA((len(units), N_CHIPS))],
    )(*parts)


def _scatter_to_owners(parts):
    n = len(parts)
    units = _units(parts, 1)

    def copies(srcs, outs, send_sems, recv_sems, incoming):
        x, y, c, p, chips = _place()
        return [pltpu.make_async_remote_copy(
            src_ref=srcs[i].at[2 * cx + cy, rs], dst_ref=outs[i].at[(2 * cx + cy) if incoming else p, rs],
            send_sem=send_sems.at[u, j], recv_sem=recv_sems.at[u, j], device_id=(cx, cy, c), device_id_type=MESH)
            for u, (i, rs) in enumerate(units) for j, (cx, cy) in enumerate(chips)]

    def start(ins, outs, scratch):
        for cpy in copies(ins, outs, *scratch, incoming=False):
            cpy.start()

    def wait(ins, outs, scratch):
        for cpy in copies(ins, outs, *scratch, incoming=True):
            cpy.wait_recv()
        for cpy in copies(ins, outs, *scratch, incoming=False):
            cpy.wait_send()

    return dict(ins=list(parts), outs=[jax.ShapeDtypeStruct(a.shape, a.dtype) for a in parts],
                scratch=[pltpu.SemaphoreType.DMA((len(units), 3)), pltpu.SemaphoreType.DMA((len(units), 3))],
                start=start, wait=wait)


def _share_with_sibling(parts):
    n = len(parts)
    units = _units(parts, 1)

    def body(*refs):
        srcs, outs = refs[:n], refs[n:2 * n]
        send_sems, recv_sems = refs[2 * n:]
        x, y, c, _, _ = _place()
        sends = [pltpu.make_async_remote_copy(
            src_ref=srcs[i].at[0, rs], dst_ref=outs[i].at[c, rs], send_sem=send_sems.at[u],
            recv_sem=recv_sems.at[u], device_id=(x, y, 1 - c), device_id_type=MESH)
            for u, (i, rs) in enumerate(units)]
        for cpy in sends:
            cpy.start()
        for u, (i, rs) in enumerate(units):
            pltpu.make_async_remote_copy(
                src_ref=srcs[i].at[0, rs], dst_ref=outs[i].at[1 - c, rs], send_sem=send_sems.at[u],
                recv_sem=recv_sems.at[u], device_id=(x, y, 1 - c), device_id_type=MESH).wait_recv()
        for cpy in sends:
            cpy.wait_send()

    return pl.pallas_call(
        body, name="share_with_sibling", in_specs=[HBM_SPEC] * n, out_specs=[HBM_SPEC] * n,
        out_shape=[jax.ShapeDtypeStruct((2,) + a.shape[1:], a.dtype) for a in parts],
        scratch_shapes=[pltpu.SemaphoreType.DMA((len(units),)), pltpu.SemaphoreType.DMA((len(units),))],
    )(*parts)


def _sum_small(v):
    def body(v_ref, out_ref, buf, send_sems, recv_sems):
        x, y, c, _, _ = _place()
        me = 4 * x + 2 * y + c
        buf[me] = v_ref[...]
        flips = [(dx, dy, dc) for dx in (0, 1) for dy in (0, 1) for dc in (0, 1)][1:]
        sends = []
        for k, (dx, dy, dc) in enumerate(flips):
            cpy = pltpu.make_async_remote_copy(
                src_ref=v_ref, dst_ref=buf.at[me], send_sem=send_sems.at[k], recv_sem=recv_sems.at[k],
                device_id=((x + dx) % 2, (y + dy) % 2, (c + dc) % 2), device_id_type=MESH)
            cpy.start()
            sends.append(cpy)
        for k, (dx, dy, dc) in enumerate(flips):
            px, py, pc = (x + dx) % 2, (y + dy) % 2, (c + dc) % 2
            pltpu.make_async_remote_copy(
                src_ref=v_ref, dst_ref=buf.at[4 * px + 2 * py + pc], send_sem=send_sems.at[k],
                recv_sem=recv_sems.at[k], device_id=(px, py, pc), device_id_type=MESH).wait_recv()
        for cpy in sends:
            cpy.wait_send()
        tot = buf[0]
        for i in range(1, N_DEV):
            tot = tot + buf[i]
        out_ref[...] = tot

    return pl.pallas_call(
        body, name="sum_small", out_shape=jax.ShapeDtypeStruct(v.shape, v.dtype),
        in_specs=[pl.BlockSpec(memory_space=pltpu.VMEM)], out_specs=pl.BlockSpec(memory_space=pltpu.VMEM),
        scratch_shapes=[pltpu.VMEM((N_DEV,) + v.shape, v.dtype), pltpu.SemaphoreType.DMA((N_DEV - 1,)),
                        pltpu.SemaphoreType.DMA((N_DEV - 1,))],
    )(v)


def _add_chips(name, landed, pair, chip):
    nq, r, w = landed.shape
    tr = 64

    def body(chip_ref, *refs):
        own = refs[nq][...].astype(F32)
        tot = None
        for q in range(nq):
            term = jnp.where(chip_ref[0] == q, own, refs[q][...].astype(F32))
            tot = term if tot is None else tot + term
        refs[nq + 1][...] = tot

    specs = [pl.BlockSpec((None, tr, w), functools.partial(lambda j, chip_ref, q: (q, j, 0), q=q)) for q in range(nq)]
    specs.append(pl.BlockSpec((None, tr, w), lambda j, chip_ref: (chip_ref[0], j, 0)))
    grid_spec = pltpu.PrefetchScalarGridSpec(
        num_scalar_prefetch=1, grid=(r // tr,), in_specs=specs,
        out_specs=pl.BlockSpec((None, tr, w), lambda j, chip_ref: (0, j, 0)))
    return pl.pallas_call(
        body, name=name, grid_spec=grid_spec, out_shape=jax.ShapeDtypeStruct((1, r, w), F32),
        compiler_params=_params(("parallel",)),
    )(jnp.reshape(chip, (1,)).astype(jnp.int32), *([landed] * nq), pair)


def _add_pair(name, halves, got, c):
    nq, _, r, w = halves.shape
    tr = 64

    def body(c_ref, a_ref, b_ref, o_ref):
        o_ref[...] = (a_ref[...] + b_ref[...]).astype(o_ref.dtype)

    grid_spec = pltpu.PrefetchScalarGridSpec(
        num_scalar_prefetch=1, grid=(nq, r // tr),
        in_specs=[pl.BlockSpec((None, None, tr, w), lambda i, j, c_ref: (i, c_ref[0], j, 0)),
                  pl.BlockSpec((None, tr, w), lambda i, j, c_ref: (i, j, 0))],
        out_specs=pl.BlockSpec((None, tr, w), lambda i, j, c_ref: (i, j, 0)))
    return pl.pallas_call(
        body, name=name, grid_spec=grid_spec, out_shape=jax.ShapeDtypeStruct((nq, r, w), BF16),
        compiler_params=_params(("parallel", "parallel")),
    )(jnp.reshape(c, (1,)).astype(jnp.int32), halves, got)


def _adamw(name, w, g, m, v, tm):
    def fn(wv, gv, mv, vv):
        m2 = ADAM_B1 * mv + (1.0 - ADAM_B1) * gv
        v2 = ADAM_B2 * vv + (1.0 - ADAM_B2) * (gv * gv)
        m_hat = m2 / (1.0 - ADAM_B1 ** ADAM_STEP)
        v_hat = v2 / (1.0 - ADAM_B2 ** ADAM_STEP)
        return -ADAM_LR * (m_hat / (jnp.sqrt(v_hat) + ADAM_EPS) + ADAM_WD * wv), m2, v2
    c = w.shape[1]
    return _rows(name, fn, [w, g, m, v], [], [(c, F32)] * 3, tm=tm)


REST_ROWS = 256 + 3 * 128 + 256
REST_SPLITS = (("w_mem_kv", 0, 256), ("w_branch_a", 256, 128), ("w_branch_b", 384, 128),
               ("w_branch_m", 512, 128), ("w_out", 640, 256))


def _rest_pack(t):
    return jnp.concatenate([t[n].reshape(rows, D_MODEL) for n, _, rows in REST_SPLITS], axis=0)


def _rest_unpack(a, shapes):
    return {n: a[r0:r0 + rows].reshape(shapes[n]) for n, r0, rows in REST_SPLITS}


def _small_pack(pre, post, memg, bforget, bmerge):
    pad = jnp.zeros((1, D_MODEL - B_HEADS), F32)
    return jnp.concatenate([pre, post, memg, bmerge.reshape(3, D_MODEL),
                            jnp.concatenate([bforget, pad], axis=1), jnp.zeros((1, D_MODEL), F32)], axis=0)


def _small_unpack(s8):
    return dict(norm_pre_g=s8[0:1], norm_post_g=s8[1:2], norm_mem_g=s8[2:3],
                b_merge=s8[3:6].reshape(1, 3 * D_MODEL), b_forget=s8[6:7, :B_HEADS])


WEIGHTS = ("norm_pre_g", "norm_post_g", "norm_mem_g", "w_in", "b_forget", "b_merge", "w_mem_kv",
           "w_branch_a", "w_branch_b", "w_branch_m", "w_out")
SMALL = ("norm_pre_g", "norm_post_g", "norm_mem_g", "b_forget", "b_merge")


def kernel(x, mem, positions, norm_pre_g, norm_post_g, norm_mem_g, w_in, b_forget, b_merge, w_mem_kv, w_branch_a, w_branch_b, w_branch_m, w_out, loss_target, m_norm_pre_g, m_norm_post_g, m_norm_mem_g, m_w_in, m_b_forget, m_b_merge, m_w_mem_kv, m_w_branch_a, m_w_branch_b, m_w_branch_m, m_w_out, v_norm_pre_g, v_norm_post_g, v_norm_mem_g, v_w_in, v_b_forget, v_b_merge, v_w_mem_kv, v_w_branch_a, v_w_branch_b, v_w_branch_m, v_w_out):
    w = dict(norm_pre_g=norm_pre_g, norm_post_g=norm_post_g, norm_mem_g=norm_mem_g, w_in=w_in[0],
             b_forget=b_forget, b_merge=b_merge, w_mem_kv=w_mem_kv[0], w_branch_a=w_branch_a[0],
             w_branch_b=w_branch_b[0], w_branch_m=w_branch_m[0], w_out=w_out[0])
    mo = dict(norm_pre_g=m_norm_pre_g, norm_post_g=m_norm_post_g, norm_mem_g=m_norm_mem_g, w_in=m_w_in[0],
              b_forget=m_b_forget, b_merge=m_b_merge, w_mem_kv=m_w_mem_kv[0], w_branch_a=m_w_branch_a[0],
              w_branch_b=m_w_branch_b[0], w_branch_m=m_w_branch_m[0], w_out=m_w_out[0])
    vo = dict(norm_pre_g=v_norm_pre_g, norm_post_g=v_norm_post_g, norm_mem_g=v_norm_mem_g, w_in=v_w_in[0],
              b_forget=v_b_forget, b_merge=v_b_merge, w_mem_kv=v_w_mem_kv[0], w_branch_a=v_w_branch_a[0],
              w_branch_b=v_w_branch_b[0], w_branch_m=v_w_branch_m[0], w_out=v_w_out[0])
    s = x.shape[1]
    c = lax.axis_index("c")

    chip = 2 * lax.axis_index("x") + lax.axis_index("y")

    def put(whole, own, slot):
        return lax.dynamic_update_index_in_dim(whole, own.astype(whole.dtype), slot, 0)

    own_w = [w["w_in"].astype(BF16).reshape(2, D_MODEL // 2, SHARD_COLS),
             _rest_pack(w).astype(BF16).reshape(2, REST_ROWS // 2, D_MODEL)]
    all_in, all_rest = _gather_weights(own_w)
    all_in = all_in.reshape(N_CHIPS, D_MODEL, SHARD_COLS)
    own_in, own_rest = own_w[0].reshape(D_MODEL, SHARD_COLS), own_w[1].reshape(REST_ROWS, D_MODEL)
    w_main, w_fb = _split_forget([jnp.where(chip == p, own_in, all_in[p]) for p in range(N_CHIPS)])
    w_fb = jnp.concatenate([w_fb, jnp.zeros((D_MODEL, HD - B_HEADS), BF16)], axis=1)
    all_rest = all_rest.reshape(N_CHIPS, REST_ROWS, D_MODEL)
    all_rest = jnp.stack([jnp.where(chip == p, own_rest, all_rest[p]) for p in range(N_CHIPS)])
    w_kv_f = all_rest[:, 0:256].reshape(D_MODEL, D_MODEL)
    w_br_f = [all_rest[:, 256 + 128 * i:384 + 128 * i].reshape(N_CHIPS, 512, 256).transpose(1, 0, 2)
              .reshape(512, D_MODEL) for i in range(3)]
    w_out_f = all_rest[:, 640:896].reshape(D_MODEL, D_MODEL)

    pair = []

    def exchange(g):
        def per_chip(name, p):
            a = g[name]
            if name in ("w_mem_kv", "w_out"):
                return a[256 * p:256 * (p + 1)]
            return a[:, 256 * p:256 * (p + 1)]

        in4 = jnp.stack(g["w_in"])
        rest4 = jnp.stack([_rest_pack({n: per_chip(n, p) for n, _, _ in REST_SPLITS}) for p in range(N_CHIPS)])
        halves = [in4.reshape(N_CHIPS, 2, D_MODEL // 2, SHARD_COLS),
                  rest4.reshape(N_CHIPS, 2, REST_ROWS // 2, D_MODEL)]
        got = _swap_with_sibling(halves)
        pair.extend(_add_pair("add_pair_%d" % i, halves[i], got[i], c) for i in range(2))
        return _scatter_to_owners(pair)

    loss_lanes, grad_x, g, landed = _local_step(
        x[0], mem[0], positions.reshape(s, 1), loss_target[0], norm_pre_g, norm_post_g, norm_mem_g,
        w_main, w_fb, b_forget, b_merge, w_kv_f, w_br_f[0], w_br_f[1], w_br_f[2], w_out_f, exchange)
    loss = lax.psum(jnp.sum(loss_lanes), ("x", "y", "c"))
    half = [_add_chips("add_chips_%d" % i, landed[i], pair[i], chip) for i in range(2)]
    red_in, red_rest = [put(a, o[0], c) for a, o in zip(_share_with_sibling(half), half)]
    gs = {"w_in": red_in.reshape(D_MODEL, SHARD_COLS)}
    gs.update(_rest_unpack(red_rest.reshape(REST_ROWS, D_MODEL), {n: w[n].shape for n, _, _ in REST_SPLITS}))
    gs.update(_small_unpack(_sum_small(_small_pack(
        g["norm_pre_g"], g["norm_post_g"], g["norm_mem_g"], g["b_forget"], g["b_merge"]))))

    delta, new_m, new_v = {}, {}, {}
    for n, tm in (("w_in", 128), ("w_mem_kv", 256), ("w_branch_a", 512), ("w_branch_b", 512),
                  ("w_branch_m", 512), ("w_out", 256)):
        d_, m_, v_ = _adamw("adamw_" + n, w[n], gs[n], mo[n], vo[n], tm)
        delta[n], new_m[n], new_v[n] = d_[None], m_[None], v_[None]
        gs[n] = gs[n][None]
    packs = [_small_pack(*[t[n] for n in SMALL])
             for t in (w, gs, mo, vo)]
    for res, store in zip(_adamw("adamw_small", *packs, 8), (delta, new_m, new_v)):
        store.update(_small_unpack(res))

    return (loss, grad_x[None], *[gs[n] for n in WEIGHTS], *[delta[n] for n in WEIGHTS],
            *[new_m[n] for n in WEIGHTS], *[new_v[n] for n in WEIGHTS])
```

```python
import functools

import jax
import jax.numpy as jnp
from jax import lax
from jax.experimental import pallas as pl
from jax.experimental.pallas import tpu as pltpu

F32 = jnp.float32
BF16 = jnp.bfloat16
MESH = pl.DeviceIdType.MESH

D_MODEL = 1024
N_MEM = 256
EPS = 1e-6
NEG = -1e30
ROPE_THETA = 500000.0
ROT_DIM = 32
HD = 128
A_GROUP = 512
DILATIONS = (1, 4, 16)
BAND = 128
B_HEADS = 8
B_HD = 64
N_CHIPS = 4
N_DEV = 8

C_QA, C_KA, C_VA, C_ZA = 0, 1536, 3072, 4608
C_QB, C_KB, C_VB, C_ZB = 5120, 5632, 6144, 6656
C_QM, C_ZM, C_GL = 7168, 7680, 8192
FB_ORIG = 6656
IN_COLS = 11272
SHARD_COLS = IN_COLS // N_CHIPS

ADAM_LR, ADAM_B1, ADAM_B2, ADAM_EPS, ADAM_WD, ADAM_STEP = 0.001, 0.9, 0.999, 1e-08, 0.01, 10

VMEM_LIMIT_V7X = 56 * 1024 * 1024

NT = (((1,), (1,)), ((), ()))
NN = (((1,), (0,)), ((), ()))
TN = (((0,), (0,)), ((), ()))


def _params(sem):
    return pltpu.CompilerParams(dimension_semantics=sem, vmem_limit_bytes=VMEM_LIMIT_V7X)


def _dot(a, b, dn=NN):
    return lax.dot_general(a, b, dn, preferred_element_type=F32)


def _sig(z):
    return 1.0 / (1.0 + jnp.exp(-z))


def _rows(name, fn, row_ins, bc_ins, outs, reds=(), tm=512, scratch=(), into=None):
    arrs, specs = [], []
    s = None
    for r in row_ins:
        arr, w, cb, d = (tuple(r) + (1,))[:4] if isinstance(r, tuple) else (r, r.shape[1], 0, 1)
        s = arr.shape[0] * d if s is None else s
        arrs.append(arr)
        specs.append((w, cb, d))
    tm = min(tm, s)
    specs = [pl.BlockSpec((tm // d, w), functools.partial(lambda i, cb: (i, cb), cb=cb)) for w, cb, d in specs]
    for b in bc_ins:
        arrs.append(b)
        specs.append(pl.BlockSpec(b.shape, lambda i: (0, 0)))
    outs = [(tuple(o) + (1,))[:3] for o in outs]
    n_in, n_out = len(arrs), len(outs)
    o0 = n_in + (0 if into is None else 1)

    def body(*refs):
        n_ref = o0 + n_out + len(reds)
        vals = fn(*[r[...] for r in refs[:n_in]], *refs[n_ref:])
        if not isinstance(vals, (tuple, list)):
            vals = (vals,)
        for r, v in zip(refs[o0:o0 + n_out], vals[:n_out]):
            r[...] = v.astype(r.dtype)
        if reds:
            red_refs = refs[o0 + n_out:n_ref]

            @pl.when(pl.program_id(0) == 0)
            def _():
                for r in red_refs:
                    r[...] = jnp.zeros_like(r)

            for r, v in zip(red_refs, vals[n_out:]):
                r[...] += v

    out_shape = [jax.ShapeDtypeStruct((s // d, c), dt) for c, dt, d in outs]
    out_shape += [jax.ShapeDtypeStruct((1, c), F32) for c in reds]
    out_specs = [pl.BlockSpec((tm // d, c), lambda i: (i, 0)) for c, _, d in outs]
    out_specs += [pl.BlockSpec((1, c), lambda i: (0, 0)) for c in reds]
    aliases = {}
    if into is not None:
        whole, k, cb = into
        out_shape[k] = jax.ShapeDtypeStruct(whole.shape, whole.dtype)
        out_specs[k] = pl.BlockSpec((tm, outs[k][0]), functools.partial(lambda i, cb: (i, cb), cb=cb))
        aliases = {n_in: k}
        arrs.append(whole)
        specs.append(pl.BlockSpec(memory_space=pl.ANY))
    res = pl.pallas_call(
        body, name=name, grid=(s // tm,), in_specs=specs, out_specs=out_specs, out_shape=out_shape,
        scratch_shapes=list(scratch), input_output_aliases=aliases,
        compiler_params=_params(("arbitrary",) if reds else ("parallel",)),
    )(*arrs)
    return res


def _to_class(x, scr, d):
    if d == 1:
        return x.astype(F32)
    tm, c = x.shape
    for g in range(c // 128):
        scr[g][...] = x[:, g * 128:(g + 1) * 128].astype(F32)
    return jnp.concatenate([scr[g][pl.ds(r, tm // d, stride=d), :] for r in range(d) for g in range(c // 128)],
                           axis=1)


def _from_class(x, scr, d):
    if d == 1:
        return x.astype(F32)
    n, dc = x.shape
    c = dc // d
    for r in range(d):
        for g in range(c // 128):
            scr[g][pl.ds(r, n, stride=d), :] = x[:, r * c + g * 128:r * c + (g + 1) * 128].astype(F32)
    return jnp.concatenate([scr[g][...] for g in range(c // 128)], axis=1)


def _mm(name, a, b, mode, out_dtype, tm=1024, tn=1024, tk=1024, side=None):
    if mode == "nn":
        (m, k), (_, n) = a.shape, b.shape
    elif mode == "nt":
        (m, k), (n, _) = a.shape, b.shape
    else:
        (k, m), (_, n) = a.shape, b.shape
    tm, tn, tk = min(tm, m), min(tn, n), min(tk, k)
    nk = k // tk
    grid = (m // tm, n // tn, nk)
    dn = {"nn": NN, "nt": NT, "tn": TN}[mode]
    n_si = len(side["ins"]) if side else 0
    n_so = len(side["outs"]) if side else 0
    n_acc = 1 if nk > 1 else 0

    def body(*refs):
        a_ref, b_ref = refs[:2]
        side_in, o_ref = refs[2:2 + n_si], refs[2 + n_si]
        side_out = refs[3 + n_si:3 + n_si + n_so]
        acc = refs[3 + n_si + n_so:3 + n_si + n_so + n_acc]
        side_scratch = refs[3 + n_si + n_so + n_acc:]
        step = (pl.program_id(0) * grid[1] + pl.program_id(1)) * grid[2] + pl.program_id(2)
        if side:
            @pl.when(step == 0)
            def _():
                side["start"](side_in, side_out, side_scratch)

        part = _dot(a_ref[...].astype(BF16), b_ref[...].astype(BF16), dn)
        if nk == 1:
            o_ref[...] = part.astype(o_ref.dtype)
        else:
            kk = pl.program_id(2)

            @pl.when(kk == 0)
            def _():
                acc[0][...] = part

            @pl.when(kk > 0)
            def _():
                acc[0][...] += part

            @pl.when(kk == nk - 1)
            def _():
                o_ref[...] = acc[0][...].astype(o_ref.dtype)

        if side:
            @pl.when(step == grid[0] * grid[1] * grid[2] - 1)
            def _():
                side["wait"](side_in, side_out, side_scratch)

    a_spec = (pl.BlockSpec((tk, tm), lambda i, j, kk: (kk, i)) if mode == "tn"
              else pl.BlockSpec((tm, tk), lambda i, j, kk: (i, kk)))
    b_spec = (pl.BlockSpec((tn, tk), lambda i, j, kk: (j, kk)) if mode == "nt"
              else pl.BlockSpec((tk, tn), lambda i, j, kk: (kk, j)))
    o_spec = pl.BlockSpec((tm, tn), lambda i, j, kk: (i, j))
    o_shape = jax.ShapeDtypeStruct((m, n), out_dtype)
    acc_scratch = [pltpu.VMEM((tm, tn), F32)] * n_acc
    if not side:
        return pl.pallas_call(
            body, name=name, grid=grid, in_specs=[a_spec, b_spec], out_specs=o_spec, out_shape=o_shape,
            scratch_shapes=acc_scratch, compiler_params=_params(("parallel", "parallel", "arbitrary")),
        )(a, b)
    return pl.pallas_call(
        body, name=name, grid=grid, in_specs=[a_spec, b_spec] + [HBM_SPEC] * n_si,
        out_specs=[o_spec] + [HBM_SPEC] * n_so, out_shape=[o_shape] + side["outs"],
        scratch_shapes=acc_scratch + side["scratch"],
        compiler_params=_params(("arbitrary", "arbitrary", "arbitrary")),
    )(a, b, *side["ins"])


def _rms_fwd(name, x, g):
    def fn(xv, gv):
        r = lax.rsqrt(jnp.mean(xv * xv, axis=-1, keepdims=True) + EPS)
        return (xv * r * gv,)
    return _rows(name, fn, [x], [g], [(x.shape[1], BF16)], tm=min(512, x.shape[0]))[0]


def _rms_fwd_both(name, x, g):
    s, dm = x.shape
    tm = min(512, s)

    def body(x_ref, g_ref, h_ref, ht_ref):
        xv = x_ref[...]
        hv = xv * lax.rsqrt(jnp.mean(xv * xv, axis=-1, keepdims=True) + EPS) * g_ref[...]
        h_ref[...] = hv.astype(BF16)
        ht_ref[...] = hv.T.astype(BF16)

    return pl.pallas_call(
        body, name=name, grid=(s // tm,),
        in_specs=[pl.BlockSpec((tm, dm), lambda i: (i, 0)), pl.BlockSpec((1, dm), lambda i: (0, 0))],
        out_specs=[pl.BlockSpec((tm, dm), lambda i: (i, 0)), pl.BlockSpec((dm, tm), lambda i: (0, i))],
        out_shape=[jax.ShapeDtypeStruct((s, dm), BF16), jax.ShapeDtypeStruct((dm, s), BF16)],
        compiler_params=_params(("parallel",)),
    )(x, g)


def _rope_tables(pos, inv):
    ang = pos.astype(F32) * inv
    lane = lax.broadcasted_iota(jnp.int32, ang.shape, 1)
    c = jnp.where(lane < ROT_DIM, jnp.cos(ang), 1.0)
    sn = jnp.sin(ang)
    sg = jnp.where(lane < ROT_DIM // 2, -sn, jnp.where(lane < ROT_DIM, sn, 0.0))
    return c, sg, lane


def _rope_apply(x, c, sg, lane):
    outs = []
    for h in range(x.shape[1] // HD):
        xh = x[:, h * HD:(h + 1) * HD].astype(F32)
        swap = jnp.where(lane < ROT_DIM // 2, pltpu.roll(xh, HD - ROT_DIM // 2, 1),
                         pltpu.roll(xh, ROT_DIM // 2, 1))
        outs.append(xh * c + swap * sg)
    return jnp.concatenate(outs, axis=1)


ROPE_TM = 256


def _class_scratch(tm):
    return [pltpu.VMEM((tm, 128), F32) for _ in range(A_GROUP // 128)]


def _rope_fwd(u, pos, inv):
    def fn(q, k, v, p, iv, *scr):
        c, sg, lane = _rope_tables(p, iv)
        qr, kr = _rope_apply(q, c, sg, lane), _rope_apply(k, c, sg, lane)
        outs = []
        for g, d in enumerate(DILATIONS):
            gs = slice(g * A_GROUP, (g + 1) * A_GROUP)
            outs += [_to_class(qr[:, gs], scr, d), _to_class(kr[:, gs], scr, d), _to_class(v[:, gs], scr, d)]
        return tuple(outs)

    outs = [(d * A_GROUP, BF16, d) for d in DILATIONS for _ in range(3)]
    qkv = [(u, 3 * A_GROUP, c0 // (3 * A_GROUP)) for c0 in (C_QA, C_KA, C_VA)]
    return _rows("rope_fwd", fn, qkv + [pos], [inv], outs, tm=ROPE_TM,
                 scratch=_class_scratch(ROPE_TM))


def _rope_bwd(dqs, dks, dvs, pos, inv, du):
    def fn(*args):
        grads, p, iv, scr = args[:9], args[9], args[10], args[11:]
        c, sg, lane = _rope_tables(p, iv)
        tok = [jnp.concatenate([_from_class(grads[3 * k + g], scr, d) for g, d in enumerate(DILATIONS)], axis=1)
               for k in range(3)]
        return (jnp.concatenate([_rope_apply(tok[0], c, -sg, lane), _rope_apply(tok[1], c, -sg, lane), tok[2]],
                                axis=1),)

    ins = [(a, a.shape[1], 0, d) for grp in (dqs, dks, dvs) for a, d in zip(grp, DILATIONS)]
    return _rows("rope_bwd", fn, ins + [pos], [inv], [(9 * A_GROUP, BF16)], tm=ROPE_TM,
                 scratch=_class_scratch(ROPE_TM), into=(du, 0, 0))[0]


def _lane_pack(cols, like):
    lane = lax.broadcasted_iota(jnp.int32, like, 1)
    out = jnp.zeros(like, F32)
    for h, cvec in enumerate(cols):
        out = jnp.where(lane == h, cvec, out)
    return out


def _band_specs(l, d, tq):
    nsb = tq // BAND
    nblk = l // BAND
    cur = pl.BlockSpec((tq, A_GROUP), lambda r, i: (i, r))
    prev = pl.BlockSpec((BAND, A_GROUP), lambda r, i: (jnp.maximum(i * nsb - 1, 0), r))
    nxt = pl.BlockSpec((BAND, A_GROUP), lambda r, i: (jnp.minimum((i + 1) * nsb, nblk - 1), r))
    st_cur = pl.BlockSpec((tq, HD), lambda r, i: (i, r))
    st_nxt = pl.BlockSpec((BAND, HD), lambda r, i: (jnp.minimum((i + 1) * nsb, nblk - 1), r))
    return nsb, cur, prev, nxt, st_cur, st_nxt


def _band_mask_q(i, first_tile):
    qr = lax.broadcasted_iota(jnp.int32, (BAND, 2 * BAND), 0)
    kc = lax.broadcasted_iota(jnp.int32, (BAND, 2 * BAND), 1)
    in_prev = (kc < BAND) & (kc >= qr)
    in_cur = (kc >= BAND) & (kc - BAND <= qr)
    if i == 0:
        in_prev = in_prev & jnp.logical_not(first_tile)
    return in_prev | in_cur


def _band_mask_k(j, nsb, last_tile):
    kc = lax.broadcasted_iota(jnp.int32, (BAND, 2 * BAND), 0)
    qr = lax.broadcasted_iota(jnp.int32, (BAND, 2 * BAND), 1)
    same = (qr < BAND) & (kc <= qr)
    nxt = (qr >= BAND) & (kc >= qr - BAND)
    if j == nsb - 1:
        nxt = nxt & jnp.logical_not(last_tile)
    return same | nxt


def _band_fwd(name, q, k, v, d):
    l = q.shape[0]
    tq = min(512, l)
    nsb, cur, prev, _, st_cur, _ = _band_specs(l, d, tq)
    scale = HD ** -0.5

    def body(q_ref, kc_ref, kp_ref, vc_ref, vp_ref, o_ref, lse_ref):
        first = pl.program_id(1) == 0
        for i in range(nsb):
            lses = []
            mask = _band_mask_q(i, first)
            for h in range(4):
                cs = slice(h * HD, (h + 1) * HD)
                qv = q_ref[i * BAND:(i + 1) * BAND, cs]
                if i == 0:
                    kk = jnp.concatenate([kp_ref[:, cs], kc_ref[0:BAND, cs]], axis=0)
                    vv = jnp.concatenate([vp_ref[:, cs], vc_ref[0:BAND, cs]], axis=0)
                else:
                    kk = kc_ref[(i - 1) * BAND:(i + 1) * BAND, cs]
                    vv = vc_ref[(i - 1) * BAND:(i + 1) * BAND, cs]
                s = jnp.where(mask, _dot(qv, kk, NT) * scale, NEG)
                m = jnp.max(s, axis=-1, keepdims=True)
                p = jnp.exp(s - m)
                den = jnp.sum(p, axis=-1, keepdims=True)
                o_ref[i * BAND:(i + 1) * BAND, cs] = _dot(p.astype(BF16), vv) / den
                lses.append(m + jnp.log(den))
            lse_ref[i * BAND:(i + 1) * BAND, :] = _lane_pack(lses, (BAND, HD))

    return pl.pallas_call(
        body, name=name, grid=(d, l // tq), in_specs=[cur, cur, prev, cur, prev],
        out_specs=[cur, st_cur],
        out_shape=[jax.ShapeDtypeStruct((l, d * A_GROUP), F32), jax.ShapeDtypeStruct((l, d * HD), F32)],
        compiler_params=_params(("parallel", "parallel")),
    )(q, k, k, v, v)


def _band_dq(name, q, k, v, dy, lse, delta, d):
    l = q.shape[0]
    tq = min(512, l)
    nsb, cur, prev, _, st_cur, _ = _band_specs(l, d, tq)
    scale = HD ** -0.5

    def body(q_ref, kc_ref, kp_ref, vc_ref, vp_ref, dy_ref, lse_ref, dl_ref, dq_ref):
        first = pl.program_id(1) == 0
        for i in range(nsb):
            mask = _band_mask_q(i, first)
            rs = slice(i * BAND, (i + 1) * BAND)
            for h in range(4):
                cs = slice(h * HD, (h + 1) * HD)
                if i == 0:
                    kk = jnp.concatenate([kp_ref[:, cs], kc_ref[0:BAND, cs]], axis=0)
                    vv = jnp.concatenate([vp_ref[:, cs], vc_ref[0:BAND, cs]], axis=0)
                else:
                    kk = kc_ref[(i - 1) * BAND:(i + 1) * BAND, cs]
                    vv = vc_ref[(i - 1) * BAND:(i + 1) * BAND, cs]
                s = jnp.where(mask, _dot(q_ref[rs, cs], kk, NT) * scale, NEG)
                p = jnp.exp(s - lse_ref[rs, h:h + 1])
                dp = _dot(dy_ref[rs, cs], vv, NT)
                ds = p * (dp - dl_ref[rs, h:h + 1])
                dq_ref[rs, cs] = (_dot(ds.astype(BF16), kk) * scale).astype(dq_ref.dtype)

    return pl.pallas_call(
        body, name=name, grid=(d, l // tq),
        in_specs=[cur, cur, prev, cur, prev, cur, st_cur, st_cur], out_specs=cur,
        out_shape=jax.ShapeDtypeStruct((l, d * A_GROUP), BF16),
        compiler_params=_params(("parallel", "parallel")),
    )(q, k, k, v, v, dy, lse, delta)


def _band_dkv(name, q, k, v, dy, lse, delta, d):
    l = q.shape[0]
    tq = min(512, l)
    nsb, cur, _, nxt, st_cur, st_nxt = _band_specs(l, d, tq)
    scale = HD ** -0.5
    ntile = l // tq

    def body(k_ref, v_ref, qc_ref, qn_ref, dyc_ref, dyn_ref, lc_ref, ln_ref, dc_ref, dn_ref,
             dk_ref, dv_ref):
        last = pl.program_id(1) == ntile - 1

        def win(c_ref, n_ref, j, cs):
            if j == nsb - 1:
                return jnp.concatenate([c_ref[j * BAND:(j + 1) * BAND, cs], n_ref[:, cs]], axis=0)
            return c_ref[j * BAND:(j + 2) * BAND, cs]

        allh = slice(0, HD)
        for j in range(nsb):
            mask = _band_mask_k(j, nsb, last)
            rs = slice(j * BAND, (j + 1) * BAND)
            lse_t = win(lc_ref, ln_ref, j, allh).T
            delta_t = win(dc_ref, dn_ref, j, allh).T
            for h in range(4):
                cs = slice(h * HD, (h + 1) * HD)
                qw = win(qc_ref, qn_ref, j, cs)
                dyw = win(dyc_ref, dyn_ref, j, cs)
                st = jnp.where(mask, _dot(k_ref[rs, cs], qw, NT) * scale, NEG)
                pt = jnp.exp(st - lse_t[h:h + 1, :])
                dst = pt * (_dot(v_ref[rs, cs], dyw, NT) - delta_t[h:h + 1, :])
                dv_ref[rs, cs] = _dot(pt.astype(BF16), dyw).astype(dv_ref.dtype)
                dk_ref[rs, cs] = (_dot(dst.astype(BF16), qw) * scale).astype(dk_ref.dtype)

    shp = jax.ShapeDtypeStruct((l, d * A_GROUP), BF16)
    return pl.pallas_call(
        body, name=name, grid=(d, ntile),
        in_specs=[cur, cur, cur, nxt, cur, nxt, st_cur, st_nxt, st_cur, st_nxt],
        out_specs=[cur, cur], out_shape=[shp, shp],
        compiler_params=_params(("parallel", "parallel")),
    )(k, v, q, q, dy, dy, lse, lse, delta, delta)


def _split3(x):
    hi = x.astype(BF16)
    r1 = x - hi.astype(F32)
    mid = r1.astype(BF16)
    lo = (r1 - mid.astype(F32)).astype(BF16)
    return hi, mid, lo


def _fox_prep(z, b):
    h, s = z.shape
    blk = min(512, s)

    def body(z_ref, b_ref, c_ref):
        r = lax.broadcasted_iota(jnp.int32, (blk, blk), 0)
        cidx = lax.broadcasted_iota(jnp.int32, (blk, blk), 1)
        tri = (r <= cidx).astype(BF16)
        carry = jnp.zeros((h, 1), F32)
        for t in range(s // blk):
            zz = z_ref[:, t * blk:(t + 1) * blk] + b_ref[...]
            lf = jnp.minimum(zz, 0.0) - jnp.log(1.0 + jnp.exp(-jnp.abs(zz)))
            hi, mid, lo = _split3(lf)
            cs = _dot(hi, tri) + _dot(mid, tri) + _dot(lo, tri) + carry
            c_ref[:, t * blk:(t + 1) * blk] = cs
            carry = cs[:, blk - 1:blk]

    return pl.pallas_call(body, name="fox_prep", out_shape=jax.ShapeDtypeStruct((h, s), F32))(z, b)


def _fox_prep_bwd(dc, z, b):
    h, s = z.shape
    blk = min(512, s)

    def body(dc_ref, z_ref, b_ref, dz_ref, db_ref):
        r = lax.broadcasted_iota(jnp.int32, (blk, blk), 0)
        cidx = lax.broadcasted_iota(jnp.int32, (blk, blk), 1)
        tri = (r >= cidx).astype(BF16)
        carry = jnp.zeros((h, 1), F32)
        tot = jnp.zeros((h, 1), F32)
        for t in reversed(range(s // blk)):
            hi, mid, lo = _split3(dc_ref[:, t * blk:(t + 1) * blk])
            rc = _dot(hi, tri) + _dot(mid, tri) + _dot(lo, tri) + carry
            carry = rc[:, 0:1]
            zz = z_ref[:, t * blk:(t + 1) * blk] + b_ref[...]
            dz = rc * _sig(-zz)
            dz_ref[:, t * blk:(t + 1) * blk] = dz
            tot = tot + jnp.sum(dz, axis=-1, keepdims=True)
        db_ref[...] = tot

    return pl.pallas_call(
        body, name="fox_prep_bwd",
        out_shape=[jax.ShapeDtypeStruct((h, s), F32), jax.ShapeDtypeStruct((h, 1), F32)])(dc, z, b)


FOX_W = 128
FOX_C = B_HD
FOX_ONE = B_HD + 3
FOX_SUB = 256
FOX_SUB_FWD = 128
FOX_HEADS_PER_STEP = 2


def _head_of_pair(x, hh):
    return x if hh == 0 else pltpu.roll(x, B_HD, 1)


def _fox_pack(u, c_col, t):
    s = u.shape[0]
    nt = s // t
    scale = B_HD ** -0.5

    def body(q_ref, k_ref, v_ref, c_ref, qf_ref, kb_ref, ks_ref, vb_ref, vt_ref):
        lane = lax.broadcasted_iota(jnp.int32, (t, FOX_W), 1)
        qv, kv, vv = [r[...].astype(F32) for r in (q_ref, k_ref, v_ref)]
        for hh in range(2):
            qf_ref[hh] = jnp.where(lane < B_HD, _head_of_pair(qv, hh), B_HD ** 0.5).astype(BF16)
            neg = c_ref[hh] * (-scale)
            hi = neg.astype(BF16).astype(F32)
            mid = (neg - hi).astype(BF16).astype(F32)
            lo = neg - hi - mid
            aux = jnp.where(lane == FOX_C, hi,
                            jnp.where(lane == FOX_C + 1, mid, jnp.where(lane == FOX_C + 2, lo, 0.0)))
            kb = jnp.where(lane < B_HD, _head_of_pair(kv, hh) * scale, aux)
            kb_ref[hh] = kb.astype(BF16)
            ks_ref[hh] = jnp.where(lane == FOX_ONE, 1.0, kb).T.astype(BF16)
            vb = jnp.where(lane < B_HD, _head_of_pair(vv, hh), 1.0)
            vb_ref[hh] = vb.astype(BF16)
            vt_ref[hh] = vb.T.astype(BF16)

    def tok(col0):
        return pl.BlockSpec((t, FOX_W), functools.partial(lambda hp, i, cb: (i, cb + hp), cb=col0 // FOX_W))

    rows = pl.BlockSpec((2, t, FOX_W), lambda hp, i: (hp, i, 0))
    tiles = pl.BlockSpec((2, None, FOX_W, t), lambda hp, i: (hp, i, 0, 0))
    hm = jax.ShapeDtypeStruct((B_HEADS, s, FOX_W), BF16)
    tt = jax.ShapeDtypeStruct((B_HEADS, nt, FOX_W, t), BF16)
    return pl.pallas_call(
        body, name="fox_pack", grid=(B_HEADS // 2, nt),
        in_specs=[tok(C_QB), tok(C_KB), tok(C_VB), pl.BlockSpec((2, t, 1), lambda hp, i: (hp, i, 0))],
        out_specs=[rows, rows, tiles, rows, tiles], out_shape=[hm, hm, tt, hm, tt],
        compiler_params=_params(("parallel", "parallel")),
    )(u, u, u, c_col)


def _fox_pack_bwd(dy, y, t):
    s = dy.shape[0]
    nt = s // t

    def body(do_ref, o_ref, dow_ref, dl_ref):
        lane = lax.broadcasted_iota(jnp.int32, (t, FOX_W), 1)
        lane8 = lax.broadcasted_iota(jnp.int32, (8, FOX_W), 1)
        dov = do_ref[...].astype(F32)
        parts = _split3(dov * o_ref[...].astype(F32))
        for hh in range(2):
            dow_ref[hh] = jnp.where(lane < B_HD, _head_of_pair(dov, hh), 0.0).astype(BF16)
            mask = ((lane8 >= hh * B_HD) & (lane8 < (hh + 1) * B_HD)).astype(BF16)
            row = _dot(mask, parts[0], NT) + _dot(mask, parts[1], NT) + _dot(mask, parts[2], NT)
            dl_ref[hh] = row[0:1, :]

    tok = pl.BlockSpec((t, FOX_W), lambda hp, i: (i, hp))
    return pl.pallas_call(
        body, name="fox_pack_bwd", grid=(B_HEADS // 2, nt), in_specs=[tok, tok],
        out_specs=[pl.BlockSpec((2, t, FOX_W), lambda hp, i: (hp, i, 0)),
                   pl.BlockSpec((2, None, 1, t), lambda hp, i: (hp, i, 0, 0))],
        out_shape=[jax.ShapeDtypeStruct((B_HEADS, s, FOX_W), BF16), jax.ShapeDtypeStruct((B_HEADS, nt, 1, t), F32)],
        compiler_params=_params(("parallel", "parallel")),
    )(dy, y)


def _fox_unpack(dqt, dkw, dvw, t):
    h, nt = dqt.shape[:2]
    s = nt * t

    def body(dq_ref, dk_ref, dv_ref, dqo_ref, dko_ref, dvo_ref, dc_ref):
        lane = lax.broadcasted_iota(jnp.int32, (t, FOX_W), 1)

        def join(a0, a1):
            return jnp.where(lane < B_HD, a0, pltpu.roll(a1, B_HD, 1))

        for hh in range(2):
            dc_ref[hh] = dq_ref[hh][FOX_ONE:FOX_ONE + 1, :] - dk_ref[hh].T[B_HD:B_HD + 1, :]
        dqo_ref[...] = join(dq_ref[0].T, dq_ref[1].T).astype(BF16)
        dko_ref[...] = join(dk_ref[0], dk_ref[1]).astype(BF16)
        dvo_ref[...] = join(dv_ref[0], dv_ref[1]).astype(BF16)

    tok = pl.BlockSpec((t, FOX_W), lambda hp, i: (i, hp))
    rows = pl.BlockSpec((2, t, FOX_W), lambda hp, i: (hp, i, 0))
    shp = jax.ShapeDtypeStruct((s, h * B_HD), BF16)
    return pl.pallas_call(
        body, name="fox_unpack", grid=(h // 2, nt),
        in_specs=[pl.BlockSpec((2, None, FOX_W, t), lambda hp, i: (hp, i, 0, 0)), rows, rows],
        out_specs=[tok, tok, tok, pl.BlockSpec((2, None, 1, t), lambda hp, i: (hp, i, 0, 0))],
        out_shape=[shp, shp, shp, jax.ShapeDtypeStruct((h, nt, 1, t), F32)],
        compiler_params=_params(("parallel", "parallel")),
    )(dqt, dkw, dvw)


FOX_DEAD = -110.0


def _fox_norm2(qf, kb):
    h, s, w = qf.shape
    tm = min(2048, s)

    def body(q_ref, k_ref, qo_ref, ko_ref):
        row = lax.broadcasted_iota(jnp.int32, (w, w), 0)
        ones = (row < B_HD).astype(BF16)
        for x_ref, o_ref in ((q_ref, qo_ref), (k_ref, ko_ref)):
            xv = x_ref[...].astype(F32)
            n2 = _dot((xv * xv).astype(BF16), ones)
            o_ref[...] = jnp.broadcast_to(jnp.max(n2, axis=0, keepdims=True)[:, :1], o_ref.shape)

    spec = pl.BlockSpec((None, tm, w), lambda hh, i: (hh, i, 0))
    ospec = pl.BlockSpec((None, None, 8, 128), lambda hh, i: (hh, i, 0, 0))
    shp = jax.ShapeDtypeStruct((h, s // tm, 8, 128), F32)
    return pl.pallas_call(
        body, name="fox_norm2", grid=(h, s // tm), in_specs=[spec, spec], out_specs=[ospec, ospec],
        out_shape=[shp, shp], compiler_params=_params(("parallel", "parallel")),
    )(qf, kb)


def _fox_bounds(qf, kb, c, t):
    q2, k2 = _fox_norm2(qf, kb)
    g = 2.0 * jnp.sqrt(1.02 * jnp.max(q2[:, :, 0, 0], axis=1) * 1.02 * jnp.max(k2[:, :, 0, 0], axis=1))
    return jnp.concatenate([c[:, ::t], c[:, t - 1::t], g[:, None]], axis=1)


SMEM_SPEC = pl.BlockSpec(memory_space=pltpu.SMEM)


def _fox_fwd(qf, kb, vt4, bounds, t):
    h, s, w = qf.shape
    nt = s // t
    sub = FOX_SUB_FWD
    nsub = t // sub
    nh = FOX_HEADS_PER_STEP

    def body(b_ref, q_ref, k_ref, v_ref, o_ref, lse_ref):
        i = pl.program_id(1)
        krow = lax.broadcasted_iota(jnp.int32, (sub, t), 0)
        qcol = lax.broadcasted_iota(jnp.int32, (sub, t), 1)

        def dead_before(hh):
            head = pl.program_id(0) * nh + hh
            top = b_ref[head, 2 * nt] + b_ref[head, i]
            return lax.fori_loop(
                0, i, lambda jj, n: n + (top - b_ref[head, nt + jj] < FOX_DEAD).astype(jnp.int32), 0)

        j_lo = functools.reduce(jnp.minimum, [dead_before(hh) for hh in range(nh)])

        def tile(j, carry, diag):
            out = []
            for hh in range(nh):
                m, acc = carry[hh]
                qv, vj = q_ref[hh], v_ref[hh, j]
                los = [b * sub if diag else 0 for b in range(nsub)]
                sts = [_dot(k_ref[hh, pl.ds(pl.multiple_of(j * t + b * sub, sub), sub), :], qv[lo:, :], NT)
                       for b, lo in enumerate(los)]
                for b, lo in enumerate(los):
                    st = sts[b]
                    if diag:
                        st = jnp.where(krow[:, :t - lo] <= qcol[:, :t - lo], st, NEG)
                    m_old, acc_old = m[:, lo:], acc[:, lo:]
                    m2 = jnp.maximum(m_old, jnp.max(st, axis=0, keepdims=True))
                    p = jnp.exp(st - m2).astype(BF16)
                    acc2 = jnp.exp(m_old - m2) * acc_old + _dot(vj[:, b * sub:(b + 1) * sub], p)
                    m = m2 if lo == 0 else jnp.concatenate([m[:, :lo], m2], axis=1)
                    acc = acc2 if lo == 0 else jnp.concatenate([acc[:, :lo], acc2], axis=1)
                out.append((m, acc))
            return tuple(out)

        init = tuple((jnp.full((1, t), NEG, F32), jnp.zeros((w, t), F32)) for _ in range(nh))
        carry = lax.fori_loop(j_lo, i, lambda j, c: tile(j, c, False), init)
        outs = []
        for hh, (m, acc) in enumerate(tile(i, carry, True)):
            den = acc[B_HD:B_HD + 1, :]
            outs.append(acc[0:B_HD, :] / den)
            lse_ref[hh] = m + jnp.log(den)
        o_ref[...] = jnp.concatenate(outs, axis=0).T.astype(o_ref.dtype)

    return pl.pallas_call(
        body, name="fox_fwd", grid=(h // nh, nt),
        in_specs=[SMEM_SPEC,
                  pl.BlockSpec((nh, t, w), lambda hh, i: (hh, i, 0)),
                  pl.BlockSpec((nh, s, w), lambda hh, i: (hh, 0, 0)),
                  pl.BlockSpec((nh, nt, w, t), lambda hh, i: (hh, 0, 0, 0))],
        out_specs=[pl.BlockSpec((t, nh * B_HD), lambda hh, i: (i, hh)),
                   pl.BlockSpec((nh, 1, t), lambda hh, i: (hh, 0, i))],
        out_shape=[jax.ShapeDtypeStruct((s, h * B_HD), BF16), jax.ShapeDtypeStruct((h, 1, s), F32)],
        compiler_params=_params(("parallel", "parallel")),
    )(bounds, qf, kb, vt4)


def _fox_bwd(qf, dow, lse_row, delta_row, kb, kst4, vb, bounds, t):
    h, s, w = qf.shape
    nt = s // t
    nsub = t // FOX_SUB
    nh = FOX_HEADS_PER_STEP

    def body(b_ref, q_ref, do_ref, lse_ref, dl_ref, k_ref, kt_ref, v_ref, dqt_ref, dk_ref, dv_ref, dk_acc, dv_acc):
        j = pl.program_id(1)

        def alive_after(hh):
            head = pl.program_id(0) * nh + hh
            top = b_ref[head, 2 * nt] - b_ref[head, nt + j]
            return lax.fori_loop(
                j + 1, nt, lambda ii, n: n + (top + b_ref[head, ii] >= FOX_DEAD).astype(jnp.int32), 0)

        i_hi = j + 1 + functools.reduce(jnp.maximum, [alive_after(hh) for hh in range(nh)])

        @pl.when(j == 0)
        def _():
            dqt_ref[...] = jnp.zeros_like(dqt_ref)

        dk_acc[...] = jnp.zeros_like(dk_acc)
        dv_acc[...] = jnp.zeros_like(dv_acc)
        krow = lax.broadcasted_iota(jnp.int32, (FOX_SUB, t), 0)
        qcol = lax.broadcasted_iota(jnp.int32, (FOX_SUB, t), 1)
        subs = [slice(b * FOX_SUB, (b + 1) * FOX_SUB) for b in range(nsub)]

        def tile(i, diag):
            i0 = pl.multiple_of(i * t, t)
            for hh in range(nh):
                qi, doi = q_ref[hh, pl.ds(i0, t), :], do_ref[hh, pl.ds(i0, t), :]
                lse, dl = lse_ref[hh, i], dl_ref[hh, i]
                los = [b * FOX_SUB if diag else 0 for b in range(nsub)]
                sts = [_dot(k_ref[hh, rs, :], qi[lo:, :], NT) for rs, lo in zip(subs, los)]
                dps = [_dot(v_ref[hh, rs, :], doi[lo:, :], NT) for rs, lo in zip(subs, los)]
                dq = None
                for b, (rs, lo) in enumerate(zip(subs, los)):
                    st = sts[b] - lse[:, lo:]
                    if diag:
                        st = jnp.where(krow[:, :t - lo] <= qcol[:, :t - lo], st, NEG)
                    pt = jnp.exp(st)
                    dsb = (pt * (dps[b] - dl[:, lo:])).astype(BF16)
                    dv_acc[hh, rs, :] += _dot(pt.astype(BF16), doi[lo:, :])
                    dk_acc[hh, rs, :] += _dot(dsb, qi[lo:, :])
                    part = _dot(kt_ref[hh, :, rs], dsb)
                    if lo:
                        part = jnp.concatenate([jnp.zeros((w, lo), F32), part], axis=1)
                    dq = part if dq is None else dq + part
                dqt_ref[hh, i] += dq

        def step(i, carry):
            tile(i, False)
            return carry

        tile(j, True)
        lax.fori_loop(j + 1, i_hi, step, 0)
        dk_ref[...] = dk_acc[...] * (B_HD ** -0.5)
        dv_ref[...] = dv_acc[...]

    full = pl.BlockSpec((nh, s, w), lambda hh, j: (hh, 0, 0))
    rowst = pl.BlockSpec((nh, nt, 1, t), lambda hh, j: (hh, 0, 0, 0))
    tl = pl.BlockSpec((nh, t, w), lambda hh, j: (hh, j, 0))
    return pl.pallas_call(
        body, name="fox_bwd", grid=(h // nh, nt),
        in_specs=[SMEM_SPEC, full, full, rowst, rowst, tl,
                  pl.BlockSpec((nh, None, w, t), lambda hh, j: (hh, j, 0, 0)), tl],
        out_specs=[pl.BlockSpec((nh, nt, w, t), lambda hh, j: (hh, 0, 0, 0)), tl, tl],
        out_shape=[jax.ShapeDtypeStruct((h, nt, w, t), F32), jax.ShapeDtypeStruct((h, s, w), F32),
                   jax.ShapeDtypeStruct((h, s, w), F32)],
        scratch_shapes=[pltpu.VMEM((nh, t, w), F32), pltpu.VMEM((nh, t, w), F32)],
        compiler_params=_params(("parallel", "arbitrary")),
    )(bounds, qf, dow, lse_row, delta_row, kb, kst4, vb)


def _mem_fwd(u, mkv, tq=512):
    s = u.shape[0]
    scale = HD ** -0.5

    def body(q_ref, mk_ref, mv_ref, o_ref, lse_ref):
        lses = []
        for h in range(4):
            cs = slice(h * HD, (h + 1) * HD)
            sc = _dot(q_ref[:, cs], mk_ref[:, cs], NT) * scale
            m = jnp.max(sc, axis=-1, keepdims=True)
            p = jnp.exp(sc - m)
            den = jnp.sum(p, axis=-1, keepdims=True)
            o_ref[:, cs] = (_dot(p.astype(BF16), mv_ref[:, cs]) / den).astype(o_ref.dtype)
            lses.append(m + jnp.log(den))
        lse_ref[...] = _lane_pack(lses, (tq, HD))

    return pl.pallas_call(
        body, name="mem_fwd", grid=(s // tq,),
        in_specs=[pl.BlockSpec((tq, 512), lambda i: (i, C_QM // 512)),
                  pl.BlockSpec((N_MEM, 512), lambda i: (0, 0)),
                  pl.BlockSpec((N_MEM, 512), lambda i: (0, 1))],
        out_specs=[pl.BlockSpec((tq, 512), lambda i: (i, 0)), pl.BlockSpec((tq, HD), lambda i: (i, 0))],
        out_shape=[jax.ShapeDtypeStruct((s, 512), BF16), jax.ShapeDtypeStruct((s, HD), F32)],
        compiler_params=_params(("parallel",)),
    )(u, mkv, mkv)


def _mem_bwd(u, mkv, o, do, lse, du, tq=512):
    s = u.shape[0]
    scale = HD ** -0.5

    def body(q_ref, mk_ref, mv_ref, o_ref, do_ref, lse_ref, _, dq_ref, dmk_ref, dmv_ref):
        @pl.when(pl.program_id(0) == 0)
        def _():
            dmk_ref[...] = jnp.zeros_like(dmk_ref)
            dmv_ref[...] = jnp.zeros_like(dmv_ref)

        for h in range(4):
            cs = slice(h * HD, (h + 1) * HD)
            qv, dov = q_ref[:, cs], do_ref[:, cs]
            sc = _dot(qv, mk_ref[:, cs], NT) * scale
            p = jnp.exp(sc - lse_ref[:, h:h + 1])
            delta = jnp.sum(dov.astype(F32) * o_ref[:, cs].astype(F32), axis=-1, keepdims=True)
            ds = p * (_dot(dov, mv_ref[:, cs], NT) - delta)
            dsb = ds.astype(BF16)
            dq_ref[:, cs] = (_dot(dsb, mk_ref[:, cs]) * scale).astype(dq_ref.dtype)
            dmk_ref[:, cs] += _dot(dsb, qv, TN) * scale
            dmv_ref[:, cs] += _dot(p.astype(BF16), dov, TN)

    row = pl.BlockSpec((tq, 512), lambda i: (i, 0))
    acc = pl.BlockSpec((N_MEM, 512), lambda i: (0, 0))
    return pl.pallas_call(
        body, name="mem_bwd", grid=(s // tq,),
        in_specs=[pl.BlockSpec((tq, 512), lambda i: (i, C_QM // 512)),
                  pl.BlockSpec((N_MEM, 512), lambda i: (0, 0)),
                  pl.BlockSpec((N_MEM, 512), lambda i: (0, 1)),
                  row, row, pl.BlockSpec((tq, HD), lambda i: (i, 0)), pl.BlockSpec(memory_space=pl.ANY)],
        out_specs=[pl.BlockSpec((tq, 512), lambda i: (i, C_QM // 512)), acc, acc],
        out_shape=[jax.ShapeDtypeStruct(du.shape, du.dtype), jax.ShapeDtypeStruct((N_MEM, 512), F32),
                   jax.ShapeDtypeStruct((N_MEM, 512), F32)],
        input_output_aliases={6: 0},
        compiler_params=_params(("arbitrary",)),
    )(u, mkv, mkv, o, do, lse, du)


FB_CHIP = FB_ORIG // SHARD_COLS
FB_AT = FB_ORIG - FB_CHIP * SHARD_COLS


def _chip_slabs(main, fb):
    cuts = [SHARD_COLS * p - (B_HEADS if p > FB_CHIP else 0) for p in range(N_CHIPS + 1)]
    slabs = [main[:, a:b] for a, b in zip(cuts[:-1], cuts[1:])]
    own = slabs[FB_CHIP]
    slabs[FB_CHIP] = jnp.concatenate([own[:, :FB_AT], fb, own[:, FB_AT:]], axis=1)
    return slabs


def _split_forget(slabs):
    own = slabs[FB_CHIP]
    parts = list(slabs[:FB_CHIP]) + [own[:, :FB_AT], own[:, FB_AT + B_HEADS:]] + list(slabs[FB_CHIP + 1:])
    return jnp.concatenate(parts, axis=1), own[:, FB_AT:FB_AT + B_HEADS]


def _local_step(x, mem, pos, target, g_pre, g_post, g_mem, w_main, w_fb, b_forget, b_merge,
                w_mem_kv, w_ba, w_bb, w_bm, w_out, exchange=None):
    s = x.shape[0]
    t_fox = min(512, s)
    nt = s // t_fox
    half = ROT_DIM // 2
    inv = ROPE_THETA ** (-jnp.arange(half, dtype=F32) / half)
    inv128 = jnp.concatenate([inv, inv, jnp.zeros((HD - ROT_DIM,), F32)]).reshape(1, HD)

    h, h_t = _rms_fwd_both("norm_pre", x, g_pre)
    u = _mm("proj_in", h, w_main, "nn", BF16, tm=4096)
    ufb = _mm("proj_fb", h, w_fb, "nn", F32)
    memn = _rms_fwd("norm_mem", mem, g_mem)
    mkv = _mm("proj_mem", memn, w_mem_kv, "nn", BF16)

    qkv = _rope_fwd(u, pos, inv128)
    views = [tuple(qkv[3 * g:3 * g + 3]) for g in range(3)]
    os_, lses = [], []
    for g, d in enumerate(DILATIONS):
        o_g, lse_g = _band_fwd("band_fwd%d" % g, *views[g], d)
        os_.append((o_g, d * A_GROUP, 0, d))
        lses.append((lse_g, d * HD, 0, d))

    def merge_a(o1, o2, o3, l1, l2, l3, za, *scr):
        o1, o2, o3 = [_from_class(o, scr, d) for o, d in zip((o1, o2, o3), DILATIONS)]
        l1, l2, l3 = [_from_class(lv, scr, d) for lv, d in zip((l1, l2, l3), DILATIONS)]
        ys, tots = [], []
        for hh in range(4):
            cs, hs = slice(hh * HD, (hh + 1) * HD), slice(hh, hh + 1)
            mx = jnp.maximum(jnp.maximum(l1[:, hs], l2[:, hs]), l3[:, hs])
            e1, e2, e3 = jnp.exp(l1[:, hs] - mx), jnp.exp(l2[:, hs] - mx), jnp.exp(l3[:, hs] - mx)
            den = e1 + e2 + e3
            ys.append((e1 * o1[:, cs] + e2 * o2[:, cs] + e3 * o3[:, cs]) / den)
            tots.append(mx + jnp.log(den))
        y = jnp.concatenate(ys, axis=1)
        zf = za.astype(F32)
        tot = _lane_pack(tots, l1.shape)
        return (y, y * (zf * _sig(zf))) + tuple(_to_class(tot, scr, d) for d in DILATIONS)

    res = _rows("merge_a", merge_a, os_ + lses + [(u, 512, C_ZA // 512)], [],
                [(512, BF16), (512, BF16)] + [(d * HD, F32, d) for d in DILATIONS], tm=ROPE_TM,
                scratch=_class_scratch(ROPE_TM))
    y_a, yg_a, lse_a = res[0], res[1], res[2:5]

    zrow = ufb[:, :B_HEADS].T
    c = _fox_prep(zrow, b_forget.reshape(B_HEADS, 1))
    qf, kb, kst4, vb, vt4 = _fox_pack(u, c.reshape(B_HEADS, s, 1), t_fox)
    bounds = _fox_bounds(qf, kb, c, t_fox)
    y_b, lse_b = _fox_fwd(qf, kb, vt4, bounds, t_fox)

    y_m, lse_m = _mem_fwd(u, mkv)

    def gate(y, z):
        zf = z.astype(F32)
        return (y.astype(F32) * (zf * _sig(zf)),)

    yg_b = _rows("gate_b", gate, [y_b, (u, 512, C_ZB // 512)], [], [(512, BF16)])[0]
    yg_m = _rows("gate_m", gate, [y_m, (u, 512, C_ZM // 512)], [], [(512, BF16)])[0]

    br_a = _mm("branch_a", yg_a, w_ba, "nn", BF16)
    br_b = _mm("branch_b", yg_b, w_bb, "nn", BF16)
    br_m = _mm("branch_m", yg_m, w_bm, "nn", BF16)
    gl = [(u, 1024, C_GL // 1024 + i) for i in range(3)]
    bm3 = b_merge.reshape(3, D_MODEL)

    def merge(g0, g1, g2, b0, b1, b2, bm):
        tot = 0.0
        for i, (gv, bv) in enumerate(((g0, b0), (g1, b1), (g2, b2))):
            tot = tot + _sig(gv.astype(F32) + bm[i:i + 1, :]) * bv.astype(F32)
        return (tot,)

    merged = _rows("merge_gates", merge, gl + [br_a, br_b, br_m], [bm3], [(D_MODEL, BF16)])[0]
    out = _mm("proj_out", merged, w_out, "nn", F32)

    def tail(xv, ov, tv, gv):
        r = lax.rsqrt(jnp.mean(ov * ov, axis=-1, keepdims=True) + EPS)
        n = ov * r
        err = xv + n * gv - tv
        dy = err * (1.0 / D_MODEL)
        dn = dy * gv
        dout = r * (dn - n * jnp.mean(dn * n, axis=-1, keepdims=True))
        return (dy, dout, jnp.sum(0.5 * err * err * (1.0 / D_MODEL), axis=0, keepdims=True),
                jnp.sum(dy * n, axis=0, keepdims=True))

    dy, dout, loss_lanes, g_post_grad = _rows(
        "tail", tail, [x, out, target], [g_post], [(D_MODEL, F32), (D_MODEL, BF16)],
        reds=[D_MODEL, D_MODEL], tm=256)

    dmerged = _mm("d_merged", dout, w_out, "nt", BF16)
    gw_out = _mm("g_w_out", merged, dout, "tn", F32)

    def merge_bwd(dm, g0, g1, g2, b0, b1, b2, bm):
        dmf = dm.astype(F32)
        dbs, dgs, sums = [], [], []
        for i, (gv, bv) in enumerate(((g0, b0), (g1, b1), (g2, b2))):
            sg = _sig(gv.astype(F32) + bm[i:i + 1, :])
            dbs.append(dmf * sg)
            dg = dmf * bv.astype(F32) * sg * (1.0 - sg)
            dgs.append(dg)
            sums.append(jnp.sum(dg, axis=0, keepdims=True))
        return tuple(dbs + dgs + sums)

    res = _rows("merge_bwd", merge_bwd, [dmerged] + gl + [br_a, br_b, br_m], [bm3],
                [(D_MODEL, BF16)] * 6, reds=[D_MODEL] * 3, tm=256)
    dbr, dgl, g_bmerge = res[0:3], res[3:6], jnp.concatenate(res[6:9], axis=1)

    dyg, gw_branch = [], []
    for nm, dbv, wv, ygv in (("a", dbr[0], w_ba, yg_a), ("b", dbr[1], w_bb, yg_b), ("m", dbr[2], w_bm, yg_m)):
        dyg.append(_mm("d_yg_" + nm, dbv, wv, "nt", BF16))
        gw_branch.append(_mm("g_w_branch_" + nm, ygv, dbv, "tn", F32))

    def gate_bwd(dg, y, z):
        dgf, yf, zf = dg.astype(F32), y.astype(F32), z.astype(F32)
        sg = _sig(zf)
        return dgf * (zf * sg), dgf * yf * (sg * (1.0 + zf * (1.0 - sg)))

    def gate_bwd_a(dg, y, z, *scr):
        dyv, dz = gate_bwd(dg, y, z)
        prod = dyv * y.astype(F32)
        dl = [jnp.sum(prod[:, hh * HD:(hh + 1) * HD], axis=-1, keepdims=True) for hh in range(4)]
        delta = _lane_pack(dl, (dg.shape[0], HD))
        return ((dz,) + tuple(_to_class(dyv, scr, d) for d in DILATIONS)
                + tuple(_to_class(delta, scr, d) for d in DILATIONS))

    du = lax.empty(u.shape, BF16)
    res = _rows("gate_bwd_a", gate_bwd_a, [dyg[0], y_a, (u, 512, C_ZA // 512)], [],
                [(512, BF16)] + [(d * A_GROUP, BF16, d) for d in DILATIONS] + [(d * HD, F32, d) for d in DILATIONS],
                tm=ROPE_TM, scratch=_class_scratch(ROPE_TM), into=(du, 0, C_ZA // 512))
    du, dy_a, delta_a = res[0], res[1:4], res[4:7]
    dy_b, du = _rows("gate_bwd_b", gate_bwd, [dyg[1], y_b, (u, 512, C_ZB // 512)], [],
                     [(512, BF16), (512, BF16)], into=(du, 1, C_ZB // 512))
    dy_m, du = _rows("gate_bwd_m", gate_bwd, [dyg[2], y_m, (u, 512, C_ZM // 512)], [],
                     [(512, BF16), (512, BF16)], into=(du, 1, C_ZM // 512))

    du, dmk, dmv = _mem_bwd(u, mkv, y_m, dy_m, lse_m, du)
    dmkv = jnp.concatenate([dmk, dmv], axis=1)
    gw_mem_kv = _mm("g_w_mem_kv", memn, dmkv, "tn", F32)
    dmemn = _mm("d_memn", dmkv, w_mem_kv, "nt", F32)

    def mem_gain_grad(mv, dv):
        r = lax.rsqrt(jnp.mean(mv * mv, axis=-1, keepdims=True) + EPS)
        return (jnp.sum(dv * mv * r, axis=0, keepdims=True),)

    g_mem_grad = _rows("g_norm_mem", mem_gain_grad, [mem, dmemn], [], [], reds=[D_MODEL], tm=N_MEM)[0]

    dow, delta_b = _fox_pack_bwd(dy_b, y_b, t_fox)
    dqt, dkw, dvw = _fox_bwd(qf, dow, lse_b.reshape(B_HEADS, nt, 1, t_fox), delta_b, kb, kst4, vb, bounds, t_fox)
    dqb, dkb, dvb, dc = _fox_unpack(dqt, dkw, dvw, t_fox)
    dzrow, g_bforget = _fox_prep_bwd(dc.reshape(B_HEADS, s), zrow, b_forget.reshape(B_HEADS, 1))
    dfb = jnp.zeros((s, HD), BF16).at[:, :B_HEADS].set(dzrow.T.astype(BF16))

    dqs, dks, dvs = [], [], []
    for g, d in enumerate(DILATIONS):
        qv, kv, vv = views[g]
        dqs.append(_band_dq("band_dq%d" % g, qv, kv, vv, dy_a[g], lse_a[g], delta_a[g], d))
        dk_g, dv_g = _band_dkv("band_dkv%d" % g, qv, kv, vv, dy_a[g], lse_a[g], delta_a[g], d)
        dks.append(dk_g)
        dvs.append(dv_g)
    for col, piece in ((C_QB, dqb), (C_KB, dkb), (C_VB, dvb), (C_GL, dgl[0]), (C_GL + D_MODEL, dgl[1]),
                       (C_GL + 2 * D_MODEL, dgl[2])):
        du = lax.dynamic_update_slice(du, piece, (0, col))
    du = _rope_bwd(dqs, dks, dvs, pos, inv128, du)

    gw_main = _mm("g_w_main", h_t, du, "nn", F32, tk=2048)
    gw_fb = _mm("g_w_fb", h, dfb, "tn", F32)
    grads = dict(norm_post_g=g_post_grad, norm_mem_g=g_mem_grad, w_in=_chip_slabs(gw_main, gw_fb[:, :B_HEADS]),
                 b_forget=g_bforget.reshape(1, B_HEADS), b_merge=g_bmerge, w_mem_kv=gw_mem_kv,
                 w_branch_a=gw_branch[0], w_branch_b=gw_branch[1], w_branch_m=gw_branch[2], w_out=gw_out)
    side = exchange(grads) if exchange else None
    dh_main = _mm("d_h", du, w_main, "nt", F32, tk=2816, side=side)
    landed = None
    if side:
        dh_main, landed = dh_main[0], dh_main[1:]
    dh_fb = _mm("d_h_fb", dfb, w_fb, "nt", F32)

    def pre_bwd(xv, d1, d2, dyv, gv):
        r = lax.rsqrt(jnp.mean(xv * xv, axis=-1, keepdims=True) + EPS)
        n = xv * r
        dhv = d1 + d2
        dn = dhv * gv
        dx = r * (dn - n * jnp.mean(dn * n, axis=-1, keepdims=True))
        return dyv + dx, jnp.sum(dhv * n, axis=0, keepdims=True)

    grad_x, g_pre_grad = _rows("norm_pre_bwd", pre_bwd, [x, dh_main, dh_fb, dy], [g_pre],
                               [(D_MODEL, F32)], reds=[D_MODEL], tm=256)

    grads["norm_pre_g"] = g_pre_grad
    return loss_lanes, grad_x, grads, landed


HBM_SPEC = pl.BlockSpec(memory_space=pltpu.HBM)


def _place():
    x, y, c = lax.axis_index("x"), lax.axis_index("y"), lax.axis_index("c")
    chips = [(1 - x, y), (x, 1 - y), (1 - x, 1 - y)]
    return x, y, c, 2 * x + y, chips


N_CHUNKS = 4


def _units(parts, row_axis):
    units = []
    for i, a in enumerate(parts):
        ch = a.shape[row_axis] // N_CHUNKS
        units += [(i, pl.ds(k * ch, ch)) for k in range(N_CHUNKS)]
    return units


def _gather_weights(parts):
    n = len(parts)
    units = _units(parts, 1)
    nu = len(units)
    via_y = [(u % N_CHUNKS) < N_CHUNKS // 2 for u in range(nu)]

    def body(*refs):
        srcs, outs = refs[:n], refs[n:2 * n]
        send_sems, recv_sems = refs[2 * n:]
        x, y, c, p, _ = _place()
        me, sib = (x, y, c), (x, y, 1 - c)
        xn, yn, dg = (1 - x, y), (x, 1 - y), (1 - x, 1 - y)

        def cp(u, k, chip, half, to, from_src=False):
            i, rs = units[u]
            dst = outs[i].at[2 * chip[0] + chip[1], half, rs]
            return pltpu.make_async_remote_copy(
                src_ref=srcs[i].at[half, rs] if from_src else dst, dst_ref=dst, send_sem=send_sems.at[u, k],
                recv_sem=recv_sems.at[u, k], device_id=to, device_id_type=MESH)

        sent = []

        def go(copy):
            copy.start()
            sent.append(copy)

        for u in range(nu):
            go(cp(u, 0, (x, y), c, (*xn, c), from_src=True))
            go(cp(u, 1, (x, y), c, (*yn, c), from_src=True))
        for u in range(nu):
            cp(u, 0, xn, c, me).wait_recv()
            go(cp(u, 4, xn, c, sib))
            if via_y[u]:
                go(cp(u, 2, xn, c, (*yn, c)))
            cp(u, 1, yn, c, me).wait_recv()
            go(cp(u, 5, yn, c, sib))
            if not via_y[u]:
                go(cp(u, 3, yn, c, (*xn, c)))
        for u in range(nu):
            cp(u, 2 if via_y[u] else 3, dg, c, me).wait_recv()
            go(cp(u, 6, dg, c, sib))
        for u in range(nu):
            for k, chip in ((4, xn), (5, yn), (6, dg)):
                cp(u, k, chip, 1 - c, me).wait_recv()
        for copy in sent:
            copy.wait_send()

    return pl.pallas_call(
        body, name="gather_weights", in_specs=[HBM_SPEC] * n, out_specs=[HBM_SPEC] * n,
        out_shape=[jax.ShapeDtypeStruct((N_CHIPS,) + a.shape, a.dtype) for a in parts],
        scratch_shapes=[pltpu.SemaphoreType.DMA((nu, 7)), pltpu.SemaphoreType.DMA((nu, 7))],
    )(*parts)


def _swap_with_sibling(parts):
    n = len(parts)
    units = _units(parts, 2)

    def body(*refs):
        srcs, outs = refs[:n], refs[n:2 * n]
        send_sems, recv_sems = refs[2 * n:]
        x, y, c, _, _ = _place()
        cps = [pltpu.make_async_remote_copy(
            src_ref=srcs[i].at[q, 1 - c, rs], dst_ref=outs[i].at[q, rs], send_sem=send_sems.at[u, q],
            recv_sem=recv_sems.at[u, q], device_id=(x, y, 1 - c), device_id_type=MESH)
            for q in range(N_CHIPS) for u, (i, rs) in enumerate(units)]
        for cpy in cps:
            cpy.start()
        for cpy in cps:
            cpy.wait()

    return pl.pallas_call(
        body, name="swap_with_sibling", in_specs=[HBM_SPEC] * n, out_specs=[HBM_SPEC] * n,
        out_shape=[jax.ShapeDtypeStruct(a.shape[:1] + a.shape[2:], a.dtype) for a in parts],
        scratch_shapes=[pltpu.SemaphoreType.DMA((len(units), N_CHIPS)),
                        pltpu.SemaphoreType.DMA((len(units), N_CHIPS))],
    )(*parts)


def _scatter_to_owners(parts):
    n = len(parts)
    units = _units(parts, 1)

    def copies(srcs, outs, send_sems, recv_sems, incoming):
        x, y, c, p, chips = _place()
        return [pltpu.make_async_remote_copy(
            src_ref=srcs[i].at[2 * cx + cy, rs], dst_ref=outs[i].at[(2 * cx + cy) if incoming else p, rs],
            send_sem=send_sems.at[u, j], recv_sem=recv_sems.at[u, j], device_id=(cx, cy, c), device_id_type=MESH)
            for u, (i, rs) in enumerate(units) for j, (cx, cy) in enumerate(chips)]

    def start(ins, outs, scratch):
        for cpy in copies(ins, outs, *scratch, incoming=False):
            cpy.start()

    def wait(ins, outs, scratch):
        for cpy in copies(ins, outs, *scratch, incoming=True):
            cpy.wait_recv()
        for cpy in copies(ins, outs, *scratch, incoming=False):
            cpy.wait_send()

    return dict(ins=list(parts), outs=[jax.ShapeDtypeStruct(a.shape, a.dtype) for a in parts],
                scratch=[pltpu.SemaphoreType.DMA((len(units), 3)), pltpu.SemaphoreType.DMA((len(units), 3))],
                start=start, wait=wait)


def _share_with_sibling(parts):
    n = len(parts)
    units = _units(parts, 1)

    def body(*refs):
        srcs, outs = refs[:n], refs[n:2 * n]
        send_sems, recv_sems = refs[2 * n:]
        x, y, c, _, _ = _place()
        sends = [pltpu.make_async_remote_copy(
            src_ref=srcs[i].at[0, rs], dst_ref=outs[i].at[c, rs], send_sem=send_sems.at[u],
            recv_sem=recv_sems.at[u], device_id=(x, y, 1 - c), device_id_type=MESH)
            for u, (i, rs) in enumerate(units)]
        for cpy in sends:
            cpy.start()
        for u, (i, rs) in enumerate(units):
            pltpu.make_async_remote_copy(
                src_ref=srcs[i].at[0, rs], dst_ref=outs[i].at[1 - c, rs], send_sem=send_sems.at[u],
                recv_sem=recv_sems.at[u], device_id=(x, y, 1 - c), device_id_type=MESH).wait_recv()
        for cpy in sends:
            cpy.wait_send()

    return pl.pallas_call(
        body, name="share_with_sibling", in_specs=[HBM_SPEC] * n, out_specs=[HBM_SPEC] * n,
        out_shape=[jax.ShapeDtypeStruct((2,) + a.shape[1:], a.dtype) for a in parts],
        scratch_shapes=[pltpu.SemaphoreType.DMA((len(units),)), pltpu.SemaphoreType.DMA((len(units),))],
    )(*parts)


def _sum_small(v):
    def body(v_ref, out_ref, buf, send_sems, recv_sems):
        x, y, c, _, _ = _place()
        me = 4 * x + 2 * y + c
        buf[me] = v_ref[...]
        flips = [(dx, dy, dc) for dx in (0, 1) for dy in (0, 1) for dc in (0, 1)][1:]
        sends = []
        for k, (dx, dy, dc) in enumerate(flips):
            cpy = pltpu.make_async_remote_copy(
                src_ref=v_ref, dst_ref=buf.at[me], send_sem=send_sems.at[k], recv_sem=recv_sems.at[k],
                device_id=((x + dx) % 2, (y + dy) % 2, (c + dc) % 2), device_id_type=MESH)
            cpy.start()
            sends.append(cpy)
        for k, (dx, dy, dc) in enumerate(flips):
            px, py, pc = (x + dx) % 2, (y + dy) % 2, (c + dc) % 2
            pltpu.make_async_remote_copy(
                src_ref=v_ref, dst_ref=buf.at[4 * px + 2 * py + pc], send_sem=send_sems.at[k],
                recv_sem=recv_sems.at[k], device_id=(px, py, pc), device_id_type=MESH).wait_recv()
        for cpy in sends:
            cpy.wait_send()
        tot = buf[0]
        for i in range(1, N_DEV):
            tot = tot + buf[i]
        out_ref[...] = tot

    return pl.pallas_call(
        body, name="sum_small", out_shape=jax.ShapeDtypeStruct(v.shape, v.dtype),
        in_specs=[pl.BlockSpec(memory_space=pltpu.VMEM)], out_specs=pl.BlockSpec(memory_space=pltpu.VMEM),
        scratch_shapes=[pltpu.VMEM((N_DEV,) + v.shape, v.dtype), pltpu.SemaphoreType.DMA((N_DEV - 1,)),
                        pltpu.SemaphoreType.DMA((N_DEV - 1,))],
    )(v)


def _add_chips(name, landed, pair, chip):
    nq, r, w = landed.shape
    tr = 64

    def body(chip_ref, *refs):
        own = refs[nq][...].astype(F32)
        tot = None
        for q in range(nq):
            term = jnp.where(chip_ref[0] == q, own, refs[q][...].astype(F32))
            tot = term if tot is None else tot + term
        refs[nq + 1][...] = tot

    specs = [pl.BlockSpec((None, tr, w), functools.partial(lambda j, chip_ref, q: (q, j, 0), q=q)) for q in range(nq)]
    specs.append(pl.BlockSpec((None, tr, w), lambda j, chip_ref: (chip_ref[0], j, 0)))
    grid_spec = pltpu.PrefetchScalarGridSpec(
        num_scalar_prefetch=1, grid=(r // tr,), in_specs=specs,
        out_specs=pl.BlockSpec((None, tr, w), lambda j, chip_ref: (0, j, 0)))
    return pl.pallas_call(
        body, name=name, grid_spec=grid_spec, out_shape=jax.ShapeDtypeStruct((1, r, w), F32),
        compiler_params=_params(("parallel",)),
    )(jnp.reshape(chip, (1,)).astype(jnp.int32), *([landed] * nq), pair)


def _add_pair(name, halves, got, c):
    nq, _, r, w = halves.shape
    tr = 64

    def body(c_ref, a_ref, b_ref, o_ref):
        o_ref[...] = (a_ref[...] + b_ref[...]).astype(o_ref.dtype)

    grid_spec = pltpu.PrefetchScalarGridSpec(
        num_scalar_prefetch=1, grid=(nq, r // tr),
        in_specs=[pl.BlockSpec((None, None, tr, w), lambda i, j, c_ref: (i, c_ref[0], j, 0)),
                  pl.BlockSpec((None, tr, w), lambda i, j, c_ref: (i, j, 0))],
        out_specs=pl.BlockSpec((None, tr, w), lambda i, j, c_ref: (i, j, 0)))
    return pl.pallas_call(
        body, name=name, grid_spec=grid_spec, out_shape=jax.ShapeDtypeStruct((nq, r, w), BF16),
        compiler_params=_params(("parallel", "parallel")),
    )(jnp.reshape(c, (1,)).astype(jnp.int32), halves, got)


def _adamw(name, w, g, m, v, tm):
    def fn(wv, gv, mv, vv):
        m2 = ADAM_B1 * mv + (1.0 - ADAM_B1) * gv
        v2 = ADAM_B2 * vv + (1.0 - ADAM_B2) * (gv * gv)
        m_hat = m2 / (1.0 - ADAM_B1 ** ADAM_STEP)
        v_hat = v2 / (1.0 - ADAM_B2 ** ADAM_STEP)
        return -ADAM_LR * (m_hat / (jnp.sqrt(v_hat) + ADAM_EPS) + ADAM_WD * wv), m2, v2
    c = w.shape[1]
    return _rows(name, fn, [w, g, m, v], [], [(c, F32)] * 3, tm=tm)


REST_ROWS = 256 + 3 * 128 + 256
REST_SPLITS = (("w_mem_kv", 0, 256), ("w_branch_a", 256, 128), ("w_branch_b", 384, 128),
               ("w_branch_m", 512, 128), ("w_out", 640, 256))


def _rest_pack(t):
    return jnp.concatenate([t[n].reshape(rows, D_MODEL) for n, _, rows in REST_SPLITS], axis=0)


def _rest_unpack(a, shapes):
    return {n: a[r0:r0 + rows].reshape(shapes[n]) for n, r0, rows in REST_SPLITS}


def _small_pack(pre, post, memg, bforget, bmerge):
    pad = jnp.zeros((1, D_MODEL - B_HEADS), F32)
    return jnp.concatenate([pre, post, memg, bmerge.reshape(3, D_MODEL),
                            jnp.concatenate([bforget, pad], axis=1), jnp.zeros((1, D_MODEL), F32)], axis=0)


def _small_unpack(s8):
    return dict(norm_pre_g=s8[0:1], norm_post_g=s8[1:2], norm_mem_g=s8[2:3],
                b_merge=s8[3:6].reshape(1, 3 * D_MODEL), b_forget=s8[6:7, :B_HEADS])


WEIGHTS = ("norm_pre_g", "norm_post_g", "norm_mem_g", "w_in", "b_forget", "b_merge", "w_mem_kv",
           "w_branch_a", "w_branch_b", "w_branch_m", "w_out")
SMALL = ("norm_pre_g", "norm_post_g", "norm_mem_g", "b_forget", "b_merge")


def kernel(x, mem, positions, norm_pre_g, norm_post_g, norm_mem_g, w_in, b_forget, b_merge, w_mem_kv, w_branch_a, w_branch_b, w_branch_m, w_out, loss_target, m_norm_pre_g, m_norm_post_g, m_norm_mem_g, m_w_in, m_b_forget, m_b_merge, m_w_mem_kv, m_w_branch_a, m_w_branch_b, m_w_branch_m, m_w_out, v_norm_pre_g, v_norm_post_g, v_norm_mem_g, v_w_in, v_b_forget, v_b_merge, v_w_mem_kv, v_w_branch_a, v_w_branch_b, v_w_branch_m, v_w_out):
    w = dict(norm_pre_g=norm_pre_g, norm_post_g=norm_post_g, norm_mem_g=norm_mem_g, w_in=w_in[0],
             b_forget=b_forget, b_merge=b_merge, w_mem_kv=w_mem_kv[0], w_branch_a=w_branch_a[0],
             w_branch_b=w_branch_b[0], w_branch_m=w_branch_m[0], w_out=w_out[0])
    mo = dict(norm_pre_g=m_norm_pre_g, norm_post_g=m_norm_post_g, norm_mem_g=m_norm_mem_g, w_in=m_w_in[0],
              b_forget=m_b_forget, b_merge=m_b_merge, w_mem_kv=m_w_mem_kv[0], w_branch_a=m_w_branch_a[0],
              w_branch_b=m_w_branch_b[0], w_branch_m=m_w_branch_m[0], w_out=m_w_out[0])
    vo = dict(norm_pre_g=v_norm_pre_g, norm_post_g=v_norm_post_g, norm_mem_g=v_norm_mem_g, w_in=v_w_in[0],
              b_forget=v_b_forget, b_merge=v_b_merge, w_mem_kv=v_w_mem_kv[0], w_branch_a=v_w_branch_a[0],
              w_branch_b=v_w_branch_b[0], w_branch_m=v_w_branch_m[0], w_out=v_w_out[0])
    s = x.shape[1]
    c = lax.axis_index("c")

    chip = 2 * lax.axis_index("x") + lax.axis_index("y")

    def put(whole, own, slot):
        return lax.dynamic_update_index_in_dim(whole, own.astype(whole.dtype), slot, 0)

    own_w = [w["w_in"].astype(BF16).reshape(2, D_MODEL // 2, SHARD_COLS),
             _rest_pack(w).astype(BF16).reshape(2, REST_ROWS // 2, D_MODEL)]
    all_in, all_rest = _gather_weights(own_w)
    all_in = all_in.reshape(N_CHIPS, D_MODEL, SHARD_COLS)
    own_in, own_rest = own_w[0].reshape(D_MODEL, SHARD_COLS), own_w[1].reshape(REST_ROWS, D_MODEL)
    w_main, w_fb = _split_forget([jnp.where(chip == p, own_in, all_in[p]) for p in range(N_CHIPS)])
    w_fb = jnp.concatenate([w_fb, jnp.zeros((D_MODEL, HD - B_HEADS), BF16)], axis=1)
    all_rest = all_rest.reshape(N_CHIPS, REST_ROWS, D_MODEL)
    all_rest = jnp.stack([jnp.where(chip == p, own_rest, all_rest[p]) for p in range(N_CHIPS)])
    w_kv_f = all_rest[:, 0:256].reshape(D_MODEL, D_MODEL)
    w_br_f = [all_rest[:, 256 + 128 * i:384 + 128 * i].reshape(N_CHIPS, 512, 256).transpose(1, 0, 2)
              .reshape(512, D_MODEL) for i in range(3)]
    w_out_f = all_rest[:, 640:896].reshape(D_MODEL, D_MODEL)

    pair = []

    def exchange(g):
        def per_chip(name, p):
            a = g[name]
            if name in ("w_mem_kv", "w_out"):
                return a[256 * p:256 * (p + 1)]
            return a[:, 256 * p:256 * (p + 1)]

        in4 = jnp.stack(g["w_in"])
        rest4 = jnp.stack([_rest_pack({n: per_chip(n, p) for n, _, _ in REST_SPLITS}) for p in range(N_CHIPS)])
        halves = [in4.reshape(N_CHIPS, 2, D_MODEL // 2, SHARD_COLS),
                  rest4.reshape(N_CHIPS, 2, REST_ROWS // 2, D_MODEL)]
        got = _swap_with_sibling(halves)
        pair.extend(_add_pair("add_pair_%d" % i, halves[i], got[i], c) for i in range(2))
        return _scatter_to_owners(pair)

    loss_lanes, grad_x, g, landed = _local_step(
        x[0], mem[0], positions.reshape(s, 1), loss_target[0], norm_pre_g, norm_post_g, norm_mem_g,
        w_main, w_fb, b_forget, b_merge, w_kv_f, w_br_f[0], w_br_f[1], w_br_f[2], w_out_f, exchange)
    loss = lax.psum(jnp.sum(loss_lanes), ("x", "y", "c"))
    half = [_add_chips("add_chips_%d" % i, landed[i], pair[i], chip) for i in range(2)]
    red_in, red_rest = [put(a, o[0], c) for a, o in zip(_share_with_sibling(half), half)]
    gs = {"w_in": red_in.reshape(D_MODEL, SHARD_COLS)}
    gs.update(_rest_unpack(red_rest.reshape(REST_ROWS, D_MODEL), {n: w[n].shape for n, _, _ in REST_SPLITS}))
    gs.update(_small_unpack(_sum_small(_small_pack(
        g["norm_pre_g"], g["norm_post_g"], g["norm_mem_g"], g["b_forget"], g["b_merge"]))))

    delta, new_m, new_v = {}, {}, {}
    for n, tm in (("w_in", 128), ("w_mem_kv", 256), ("w_branch_a", 512), ("w_branch_b", 512),
                  ("w_branch_m", 512), ("w_out", 256)):
        d_, m_, v_ = _adamw("adamw_" + n, w[n], gs[n], mo[n], vo[n], tm)
        delta[n], new_m[n], new_v[n] = d_[None], m_[None], v_[None]
        gs[n] = gs[n][None]
    packs = [_small_pack(*[t[n] for n in SMALL])
             for t in (w, gs, mo, vo)]
    for res, store in zip(_adamw("adamw_small", *packs, 8), (delta, new_m, new_v)):
        store.update(_small_unpack(res))

    return (loss, grad_x[None], *[gs[n] for n in WEIGHTS], *[delta[n] for n in WEIGHTS],
            *[new_m[n] for n in WEIGHTS], *[new_v[n] for n in WEIGHTS])
```

```python
import functools

import jax
import jax.numpy as jnp
from jax import lax
from jax.experimental import pallas as pl
from jax.experimental.pallas import tpu as pltpu

F32 = jnp.float32
BF16 = jnp.bfloat16
MESH = pl.DeviceIdType.MESH

D_MODEL = 1024
N_MEM = 256
EPS = 1e-6
NEG = -1e30
ROPE_THETA = 500000.0
ROT_DIM = 32
HD = 128
A_GROUP = 512
DILATIONS = (1, 4, 16)
BAND = 128
B_HEADS = 8
B_HD = 64
N_CHIPS = 4
N_DEV = 8

C_QA, C_KA, C_VA, C_ZA = 0, 1536, 3072, 4608
C_QB, C_KB, C_VB, C_ZB = 5120, 5632, 6144, 6656
C_QM, C_ZM, C_GL = 7168, 7680, 8192
FB_ORIG = 6656
IN_COLS = 11272
SHARD_COLS = IN_COLS // N_CHIPS

ADAM_LR, ADAM_B1, ADAM_B2, ADAM_EPS, ADAM_WD, ADAM_STEP = 0.001, 0.9, 0.999, 1e-08, 0.01, 10

VMEM_LIMIT_V7X = 56 * 1024 * 1024

NT = (((1,), (1,)), ((), ()))
NN = (((1,), (0,)), ((), ()))
TN = (((0,), (0,)), ((), ()))


def _params(sem):
    return pltpu.CompilerParams(dimension_semantics=sem, vmem_limit_bytes=VMEM_LIMIT_V7X)


def _dot(a, b, dn=NN):
    return lax.dot_general(a, b, dn, preferred_element_type=F32)


def _sig(z):
    return 1.0 / (1.0 + jnp.exp(-z))


def _rows(name, fn, row_ins, bc_ins, outs, reds=(), tm=512, scratch=(), into=None):
    arrs, specs = [], []
    s = None
    for r in row_ins:
        arr, w, cb, d = (tuple(r) + (1,))[:4] if isinstance(r, tuple) else (r, r.shape[1], 0, 1)
        s = arr.shape[0] * d if s is None else s
        arrs.append(arr)
        specs.append((w, cb, d))
    tm = min(tm, s)
    specs = [pl.BlockSpec((tm // d, w), functools.partial(lambda i, cb: (i, cb), cb=cb)) for w, cb, d in specs]
    for b in bc_ins:
        arrs.append(b)
        specs.append(pl.BlockSpec(b.shape, lambda i: (0, 0)))
    outs = [(tuple(o) + (1,))[:3] for o in outs]
    n_in, n_out = len(arrs), len(outs)
    o0 = n_in + (0 if into is None else 1)

    def body(*refs):
        n_ref = o0 + n_out + len(reds)
        vals = fn(*[r[...] for r in refs[:n_in]], *refs[n_ref:])
        if not isinstance(vals, (tuple, list)):
            vals = (vals,)
        for r, v in zip(refs[o0:o0 + n_out], vals[:n_out]):
            r[...] = v.astype(r.dtype)
        if reds:
            red_refs = refs[o0 + n_out:n_ref]

            @pl.when(pl.program_id(0) == 0)
            def _():
                for r in red_refs:
                    r[...] = jnp.zeros_like(r)

            for r, v in zip(red_refs, vals[n_out:]):
                r[...] += v

    out_shape = [jax.ShapeDtypeStruct((s // d, c), dt) for c, dt, d in outs]
    out_shape += [jax.ShapeDtypeStruct((1, c), F32) for c in reds]
    out_specs = [pl.BlockSpec((tm // d, c), lambda i: (i, 0)) for c, _, d in outs]
    out_specs += [pl.BlockSpec((1, c), lambda i: (0, 0)) for c in reds]
    aliases = {}
    if into is not None:
        whole, k, cb = into
        out_shape[k] = jax.ShapeDtypeStruct(whole.shape, whole.dtype)
        if isinstance(cb, tuple):
            out_specs[k] = pl.BlockSpec((pl.Element(tm), pl.Element(outs[k][0])),
                                        functools.partial(lambda i, c0: (i * tm, c0), c0=cb[1]))
        else:
            out_specs[k] = pl.BlockSpec((tm, outs[k][0]), functools.partial(lambda i, cb: (i, cb), cb=cb))
        aliases = {n_in: k}
        arrs.append(whole)
        specs.append(pl.BlockSpec(memory_space=pl.ANY))
    res = pl.pallas_call(
        body, name=name, grid=(s // tm,), in_specs=specs, out_specs=out_specs, out_shape=out_shape,
        scratch_shapes=list(scratch), input_output_aliases=aliases,
        compiler_params=_params(("arbitrary",) if reds else ("parallel",)),
    )(*arrs)
    return res


def _to_class(x, scr, d):
    if d == 1:
        return x.astype(F32)
    tm, c = x.shape
    for g in range(c // 128):
        scr[g][...] = x[:, g * 128:(g + 1) * 128].astype(F32)
    return jnp.concatenate([scr[g][pl.ds(r, tm // d, stride=d), :] for r in range(d) for g in range(c // 128)],
                           axis=1)


def _from_class(x, scr, d):
    if d == 1:
        return x.astype(F32)
    n, dc = x.shape
    c = dc // d
    for r in range(d):
        for g in range(c // 128):
            scr[g][pl.ds(r, n, stride=d), :] = x[:, r * c + g * 128:r * c + (g + 1) * 128].astype(F32)
    return jnp.concatenate([scr[g][...] for g in range(c // 128)], axis=1)


def _mm(name, a, b, mode, out_dtype, tm=1024, tn=1024, tk=1024, side=None):
    if mode == "nn":
        (m, k), (_, n) = a.shape, b.shape
    elif mode == "nt":
        (m, k), (n, _) = a.shape, b.shape
    else:
        (k, m), (_, n) = a.shape, b.shape
    tm, tn, tk = min(tm, m), min(tn, n), min(tk, k)
    nk = k // tk
    grid = (m // tm, n // tn, nk)
    dn = {"nn": NN, "nt": NT, "tn": TN}[mode]
    n_si = len(side["ins"]) if side else 0
    n_so = len(side["outs"]) if side else 0
    n_acc = 1 if nk > 1 else 0

    def body(*refs):
        a_ref, b_ref = refs[:2]
        side_in, o_ref = refs[2:2 + n_si], refs[2 + n_si]
        side_out = refs[3 + n_si:3 + n_si + n_so]
        acc = refs[3 + n_si + n_so:3 + n_si + n_so + n_acc]
        side_scratch = refs[3 + n_si + n_so + n_acc:]
        step = (pl.program_id(0) * grid[1] + pl.program_id(1)) * grid[2] + pl.program_id(2)
        if side:
            @pl.when(step == 0)
            def _():
                side["start"](side_in, side_out, side_scratch)

        part = _dot(a_ref[...].astype(BF16), b_ref[...].astype(BF16), dn)
        if nk == 1:
            o_ref[...] = part.astype(o_ref.dtype)
        else:
            kk = pl.program_id(2)

            @pl.when(kk == 0)
            def _():
                acc[0][...] = part

            @pl.when(kk > 0)
            def _():
                acc[0][...] += part

            @pl.when(kk == nk - 1)
            def _():
                o_ref[...] = acc[0][...].astype(o_ref.dtype)

        if side:
            @pl.when(step == grid[0] * grid[1] * grid[2] - 1)
            def _():
                side["wait"](side_in, side_out, side_scratch)

    a_spec = (pl.BlockSpec((tk, tm), lambda i, j, kk: (kk, i)) if mode == "tn"
              else pl.BlockSpec((tm, tk), lambda i, j, kk: (i, kk)))
    b_spec = (pl.BlockSpec((tn, tk), lambda i, j, kk: (j, kk)) if mode == "nt"
              else pl.BlockSpec((tk, tn), lambda i, j, kk: (kk, j)))
    o_spec = pl.BlockSpec((tm, tn), lambda i, j, kk: (i, j))
    o_shape = jax.ShapeDtypeStruct((m, n), out_dtype)
    acc_scratch = [pltpu.VMEM((tm, tn), F32)] * n_acc
    if not side:
        return pl.pallas_call(
            body, name=name, grid=grid, in_specs=[a_spec, b_spec], out_specs=o_spec, out_shape=o_shape,
            scratch_shapes=acc_scratch, compiler_params=_params(("parallel", "parallel", "arbitrary")),
        )(a, b)
    return pl.pallas_call(
        body, name=name, grid=grid, in_specs=[a_spec, b_spec] + [HBM_SPEC] * n_si,
        out_specs=[o_spec] + [HBM_SPEC] * n_so, out_shape=[o_shape] + side["outs"],
        scratch_shapes=acc_scratch + side["scratch"],
        compiler_params=_params(("arbitrary", "arbitrary", "arbitrary")),
    )(a, b, *side["ins"])


def _rms_fwd(name, x, g):
    def fn(xv, gv):
        r = lax.rsqrt(jnp.mean(xv * xv, axis=-1, keepdims=True) + EPS)
        return (xv * r * gv,)
    return _rows(name, fn, [x], [g], [(x.shape[1], BF16)], tm=min(512, x.shape[0]))[0]


def _rms_fwd_both(name, x, g):
    s, dm = x.shape
    tm = min(512, s)

    def body(x_ref, g_ref, h_ref, ht_ref):
        xv = x_ref[...]
        hv = xv * lax.rsqrt(jnp.mean(xv * xv, axis=-1, keepdims=True) + EPS) * g_ref[...]
        h_ref[...] = hv.astype(BF16)
        ht_ref[...] = hv.T.astype(BF16)

    return pl.pallas_call(
        body, name=name, grid=(s // tm,),
        in_specs=[pl.BlockSpec((tm, dm), lambda i: (i, 0)), pl.BlockSpec((1, dm), lambda i: (0, 0))],
        out_specs=[pl.BlockSpec((tm, dm), lambda i: (i, 0)), pl.BlockSpec((dm, tm), lambda i: (0, i))],
        out_shape=[jax.ShapeDtypeStruct((s, dm), BF16), jax.ShapeDtypeStruct((dm, s), BF16)],
        compiler_params=_params(("parallel",)),
    )(x, g)


def _rope_tables(pos, inv):
    ang = pos.astype(F32) * inv
    lane = lax.broadcasted_iota(jnp.int32, ang.shape, 1)
    c = jnp.where(lane < ROT_DIM, jnp.cos(ang), 1.0)
    sn = jnp.sin(ang)
    sg = jnp.where(lane < ROT_DIM // 2, -sn, jnp.where(lane < ROT_DIM, sn, 0.0))
    return c, sg, lane


def _rope_apply(x, c, sg, lane):
    outs = []
    for h in range(x.shape[1] // HD):
        xh = x[:, h * HD:(h + 1) * HD].astype(F32)
        swap = jnp.where(lane < ROT_DIM // 2, pltpu.roll(xh, HD - ROT_DIM // 2, 1),
                         pltpu.roll(xh, ROT_DIM // 2, 1))
        outs.append(xh * c + swap * sg)
    return jnp.concatenate(outs, axis=1)


ROPE_TM = 256


def _class_scratch(tm):
    return [pltpu.VMEM((tm, 128), F32) for _ in range(A_GROUP // 128)]


def _rope_fwd(u, pos, inv):
    def fn(q, k, v, p, iv, *scr):
        c, sg, lane = _rope_tables(p, iv)
        qr, kr = _rope_apply(q, c, sg, lane), _rope_apply(k, c, sg, lane)
        outs = []
        for g, d in enumerate(DILATIONS):
            gs = slice(g * A_GROUP, (g + 1) * A_GROUP)
            outs += [_to_class(qr[:, gs], scr, d), _to_class(kr[:, gs], scr, d), _to_class(v[:, gs], scr, d)]
        return tuple(outs)

    outs = [(d * A_GROUP, BF16, d) for d in DILATIONS for _ in range(3)]
    qkv = [(u, 3 * A_GROUP, c0 // (3 * A_GROUP)) for c0 in (C_QA, C_KA, C_VA)]
    return _rows("rope_fwd", fn, qkv + [pos], [inv], outs, tm=ROPE_TM,
                 scratch=_class_scratch(ROPE_TM))


def _rope_bwd(dqs, dks, dvs, pos, inv, du):
    def fn(*args):
        grads, p, iv, scr = args[:9], args[9], args[10], args[11:]
        c, sg, lane = _rope_tables(p, iv)
        tok = [jnp.concatenate([_from_class(grads[3 * k + g], scr, d) for g, d in enumerate(DILATIONS)], axis=1)
               for k in range(3)]
        return (jnp.concatenate([_rope_apply(tok[0], c, -sg, lane), _rope_apply(tok[1], c, -sg, lane), tok[2]],
                                axis=1),)

    ins = [(a, a.shape[1], 0, d) for grp in (dqs, dks, dvs) for a, d in zip(grp, DILATIONS)]
    return _rows("rope_bwd", fn, ins + [pos], [inv], [(9 * A_GROUP, BF16)], tm=ROPE_TM,
                 scratch=_class_scratch(ROPE_TM), into=(du, 0, 0))[0]


def _lane_pack(cols, like):
    lane = lax.broadcasted_iota(jnp.int32, like, 1)
    out = jnp.zeros(like, F32)
    for h, cvec in enumerate(cols):
        out = jnp.where(lane == h, cvec, out)
    return out


def _band_specs(l, d, tq):
    nsb = tq // BAND
    nblk = l // BAND
    cur = pl.BlockSpec((tq, A_GROUP), lambda r, i: (i, r))
    prev = pl.BlockSpec((BAND, A_GROUP), lambda r, i: (jnp.maximum(i * nsb - 1, 0), r))
    nxt = pl.BlockSpec((BAND, A_GROUP), lambda r, i: (jnp.minimum((i + 1) * nsb, nblk - 1), r))
    st_cur = pl.BlockSpec((tq, HD), lambda r, i: (i, r))
    st_nxt = pl.BlockSpec((BAND, HD), lambda r, i: (jnp.minimum((i + 1) * nsb, nblk - 1), r))
    return nsb, cur, prev, nxt, st_cur, st_nxt


def _band_mask_q(i, first_tile):
    qr = lax.broadcasted_iota(jnp.int32, (BAND, 2 * BAND), 0)
    kc = lax.broadcasted_iota(jnp.int32, (BAND, 2 * BAND), 1)
    in_prev = (kc < BAND) & (kc >= qr)
    in_cur = (kc >= BAND) & (kc - BAND <= qr)
    if i == 0:
        in_prev = in_prev & jnp.logical_not(first_tile)
    return in_prev | in_cur


def _band_mask_k(j, nsb, last_tile):
    kc = lax.broadcasted_iota(jnp.int32, (BAND, 2 * BAND), 0)
    qr = lax.broadcasted_iota(jnp.int32, (BAND, 2 * BAND), 1)
    same = (qr < BAND) & (kc <= qr)
    nxt = (qr >= BAND) & (kc >= qr - BAND)
    if j == nsb - 1:
        nxt = nxt & jnp.logical_not(last_tile)
    return same | nxt


def _band_fwd(name, q, k, v, d):
    l = q.shape[0]
    tq = min(512, l)
    nsb, cur, prev, _, st_cur, _ = _band_specs(l, d, tq)
    scale = HD ** -0.5

    def body(q_ref, kc_ref, kp_ref, vc_ref, vp_ref, o_ref, lse_ref):
        first = pl.program_id(1) == 0
        for i in range(nsb):
            lses = []
            mask = _band_mask_q(i, first)
            for h in range(4):
                cs = slice(h * HD, (h + 1) * HD)
                qv = q_ref[i * BAND:(i + 1) * BAND, cs]
                if i == 0:
                    kk = jnp.concatenate([kp_ref[:, cs], kc_ref[0:BAND, cs]], axis=0)
                    vv = jnp.concatenate([vp_ref[:, cs], vc_ref[0:BAND, cs]], axis=0)
                else:
                    kk = kc_ref[(i - 1) * BAND:(i + 1) * BAND, cs]
                    vv = vc_ref[(i - 1) * BAND:(i + 1) * BAND, cs]
                s = jnp.where(mask, _dot(qv, kk, NT) * scale, NEG)
                m = jnp.max(s, axis=-1, keepdims=True)
                p = jnp.exp(s - m)
                den = jnp.sum(p, axis=-1, keepdims=True)
                o_ref[i * BAND:(i + 1) * BAND, cs] = _dot(p.astype(BF16), vv) / den
                lses.append(m + jnp.log(den))
            lse_ref[i * BAND:(i + 1) * BAND, :] = _lane_pack(lses, (BAND, HD))

    return pl.pallas_call(
        body, name=name, grid=(d, l // tq), in_specs=[cur, cur, prev, cur, prev],
        out_specs=[cur, st_cur],
        out_shape=[jax.ShapeDtypeStruct((l, d * A_GROUP), F32), jax.ShapeDtypeStruct((l, d * HD), F32)],
        compiler_params=_params(("parallel", "parallel")),
    )(q, k, k, v, v)


def _band_dq(name, q, k, v, dy, lse, delta, d):
    l = q.shape[0]
    tq = min(512, l)
    nsb, cur, prev, _, st_cur, _ = _band_specs(l, d, tq)
    scale = HD ** -0.5

    def body(q_ref, kc_ref, kp_ref, vc_ref, vp_ref, dy_ref, lse_ref, dl_ref, dq_ref):
        first = pl.program_id(1) == 0
        for i in range(nsb):
            mask = _band_mask_q(i, first)
            rs = slice(i * BAND, (i + 1) * BAND)
            for h in range(4):
                cs = slice(h * HD, (h + 1) * HD)
                if i == 0:
                    kk = jnp.concatenate([kp_ref[:, cs], kc_ref[0:BAND, cs]], axis=0)
                    vv = jnp.concatenate([vp_ref[:, cs], vc_ref[0:BAND, cs]], axis=0)
                else:
                    kk = kc_ref[(i - 1) * BAND:(i + 1) * BAND, cs]
                    vv = vc_ref[(i - 1) * BAND:(i + 1) * BAND, cs]
                s = jnp.where(mask, _dot(q_ref[rs, cs], kk, NT) * scale, NEG)
                p = jnp.exp(s - lse_ref[rs, h:h + 1])
                dp = _dot(dy_ref[rs, cs], vv, NT)
                ds = p * (dp - dl_ref[rs, h:h + 1])
                dq_ref[rs, cs] = (_dot(ds.astype(BF16), kk) * scale).astype(dq_ref.dtype)

    return pl.pallas_call(
        body, name=name, grid=(d, l // tq),
        in_specs=[cur, cur, prev, cur, prev, cur, st_cur, st_cur], out_specs=cur,
        out_shape=jax.ShapeDtypeStruct((l, d * A_GROUP), BF16),
        compiler_params=_params(("parallel", "parallel")),
    )(q, k, k, v, v, dy, lse, delta)


def _band_dkv(name, q, k, v, dy, lse, delta, d):
    l = q.shape[0]
    tq = min(512, l)
    nsb, cur, _, nxt, st_cur, st_nxt = _band_specs(l, d, tq)
    scale = HD ** -0.5
    ntile = l // tq

    def body(k_ref, v_ref, qc_ref, qn_ref, dyc_ref, dyn_ref, lc_ref, ln_ref, dc_ref, dn_ref,
             dk_ref, dv_ref):
        last = pl.program_id(1) == ntile - 1

        def win(c_ref, n_ref, j, cs):
            if j == nsb - 1:
                return jnp.concatenate([c_ref[j * BAND:(j + 1) * BAND, cs], n_ref[:, cs]], axis=0)
            return c_ref[j * BAND:(j + 2) * BAND, cs]

        allh = slice(0, HD)
        for j in range(nsb):
            mask = _band_mask_k(j, nsb, last)
            rs = slice(j * BAND, (j + 1) * BAND)
            lse_t = win(lc_ref, ln_ref, j, allh).T
            delta_t = win(dc_ref, dn_ref, j, allh).T
            for h in range(4):
                cs = slice(h * HD, (h + 1) * HD)
                qw = win(qc_ref, qn_ref, j, cs)
                dyw = win(dyc_ref, dyn_ref, j, cs)
                st = jnp.where(mask, _dot(k_ref[rs, cs], qw, NT) * scale, NEG)
                pt = jnp.exp(st - lse_t[h:h + 1, :])
                dst = pt * (_dot(v_ref[rs, cs], dyw, NT) - delta_t[h:h + 1, :])
                dv_ref[rs, cs] = _dot(pt.astype(BF16), dyw).astype(dv_ref.dtype)
                dk_ref[rs, cs] = (_dot(dst.astype(BF16), qw) * scale).astype(dk_ref.dtype)

    shp = jax.ShapeDtypeStruct((l, d * A_GROUP), BF16)
    return pl.pallas_call(
        body, name=name, grid=(d, ntile),
        in_specs=[cur, cur, cur, nxt, cur, nxt, st_cur, st_nxt, st_cur, st_nxt],
        out_specs=[cur, cur], out_shape=[shp, shp],
        compiler_params=_params(("parallel", "parallel")),
    )(k, v, q, q, dy, dy, lse, lse, delta, delta)


def _split3(x):
    hi = x.astype(BF16)
    r1 = x - hi.astype(F32)
    mid = r1.astype(BF16)
    lo = (r1 - mid.astype(F32)).astype(BF16)
    return hi, mid, lo


def _fox_prep(z, b):
    h, s = z.shape
    blk = min(512, s)

    def body(z_ref, b_ref, c_ref):
        r = lax.broadcasted_iota(jnp.int32, (blk, blk), 0)
        cidx = lax.broadcasted_iota(jnp.int32, (blk, blk), 1)
        tri = (r <= cidx).astype(BF16)
        carry = jnp.zeros((h, 1), F32)
        for t in range(s // blk):
            zz = z_ref[:, t * blk:(t + 1) * blk] + b_ref[...]
            lf = jnp.minimum(zz, 0.0) - jnp.log(1.0 + jnp.exp(-jnp.abs(zz)))
            hi, mid, lo = _split3(lf)
            cs = _dot(hi, tri) + _dot(mid, tri) + _dot(lo, tri) + carry
            c_ref[:, t * blk:(t + 1) * blk] = cs
            carry = cs[:, blk - 1:blk]

    return pl.pallas_call(body, name="fox_prep", out_shape=jax.ShapeDtypeStruct((h, s), F32))(z, b)


def _fox_prep_bwd(dc, z, b):
    h, s = z.shape
    blk = min(512, s)

    def body(dc_ref, z_ref, b_ref, dz_ref, db_ref):
        r = lax.broadcasted_iota(jnp.int32, (blk, blk), 0)
        cidx = lax.broadcasted_iota(jnp.int32, (blk, blk), 1)
        tri = (r >= cidx).astype(BF16)
        carry = jnp.zeros((h, 1), F32)
        tot = jnp.zeros((h, 1), F32)
        for t in reversed(range(s // blk)):
            hi, mid, lo = _split3(dc_ref[:, t * blk:(t + 1) * blk])
            rc = _dot(hi, tri) + _dot(mid, tri) + _dot(lo, tri) + carry
            carry = rc[:, 0:1]
            zz = z_ref[:, t * blk:(t + 1) * blk] + b_ref[...]
            dz = rc * _sig(-zz)
            dz_ref[:, t * blk:(t + 1) * blk] = dz
            tot = tot + jnp.sum(dz, axis=-1, keepdims=True)
        db_ref[...] = tot

    return pl.pallas_call(
        body, name="fox_prep_bwd",
        out_shape=[jax.ShapeDtypeStruct((h, s), F32), jax.ShapeDtypeStruct((h, 1), F32)])(dc, z, b)


FOX_W = 128
FOX_C = B_HD
FOX_ONE = B_HD + 3
FOX_SUB = 256
FOX_SUB_FWD = 128
FOX_HEADS_PER_STEP = 2


def _head_of_pair(x, hh):
    return x if hh == 0 else pltpu.roll(x, B_HD, 1)


def _fox_pack(u, c_col, t):
    s = u.shape[0]
    nt = s // t
    scale = B_HD ** -0.5

    def body(q_ref, k_ref, v_ref, c_ref, qf_ref, kb_ref, ks_ref, vb_ref, vt_ref):
        lane = lax.broadcasted_iota(jnp.int32, (t, FOX_W), 1)
        qv, kv, vv = [r[...].astype(F32) for r in (q_ref, k_ref, v_ref)]
        for hh in range(2):
            qf_ref[hh] = jnp.where(lane < B_HD, _head_of_pair(qv, hh), B_HD ** 0.5).astype(BF16)
            neg = c_ref[hh] * (-scale)
            hi = neg.astype(BF16).astype(F32)
            mid = (neg - hi).astype(BF16).astype(F32)
            lo = neg - hi - mid
            aux = jnp.where(lane == FOX_C, hi,
                            jnp.where(lane == FOX_C + 1, mid, jnp.where(lane == FOX_C + 2, lo, 0.0)))
            kb = jnp.where(lane < B_HD, _head_of_pair(kv, hh) * scale, aux)
            kb_ref[hh] = kb.astype(BF16)
            ks_ref[hh] = jnp.where(lane == FOX_ONE, 1.0, kb).T.astype(BF16)
            vb = jnp.where(lane < B_HD, _head_of_pair(vv, hh), 1.0)
            vb_ref[hh] = vb.astype(BF16)
            vt_ref[hh] = vb.T.astype(BF16)

    def tok(col0):
        return pl.BlockSpec((t, FOX_W), functools.partial(lambda hp, i, cb: (i, cb + hp), cb=col0 // FOX_W))

    rows = pl.BlockSpec((2, t, FOX_W), lambda hp, i: (hp, i, 0))
    tiles = pl.BlockSpec((2, None, FOX_W, t), lambda hp, i: (hp, i, 0, 0))
    hm = jax.ShapeDtypeStruct((B_HEADS, s, FOX_W), BF16)
    tt = jax.ShapeDtypeStruct((B_HEADS, nt, FOX_W, t), BF16)
    return pl.pallas_call(
        body, name="fox_pack", grid=(B_HEADS // 2, nt),
        in_specs=[tok(C_QB), tok(C_KB), tok(C_VB), pl.BlockSpec((2, t, 1), lambda hp, i: (hp, i, 0))],
        out_specs=[rows, rows, tiles, rows, tiles], out_shape=[hm, hm, tt, hm, tt],
        compiler_params=_params(("parallel", "parallel")),
    )(u, u, u, c_col)


def _fox_pack_bwd(dy, y, t):
    s = dy.shape[0]
    nt = s // t

    def body(do_ref, o_ref, dow_ref, dl_ref):
        lane = lax.broadcasted_iota(jnp.int32, (t, FOX_W), 1)
        lane8 = lax.broadcasted_iota(jnp.int32, (8, FOX_W), 1)
        dov = do_ref[...].astype(F32)
        parts = _split3(dov * o_ref[...].astype(F32))
        for hh in range(2):
            dow_ref[hh] = jnp.where(lane < B_HD, _head_of_pair(dov, hh), 0.0).astype(BF16)
            mask = ((lane8 >= hh * B_HD) & (lane8 < (hh + 1) * B_HD)).astype(BF16)
            row = _dot(mask, parts[0], NT) + _dot(mask, parts[1], NT) + _dot(mask, parts[2], NT)
            dl_ref[hh] = row[0:1, :]

    tok = pl.BlockSpec((t, FOX_W), lambda hp, i: (i, hp))
    return pl.pallas_call(
        body, name="fox_pack_bwd", grid=(B_HEADS // 2, nt), in_specs=[tok, tok],
        out_specs=[pl.BlockSpec((2, t, FOX_W), lambda hp, i: (hp, i, 0)),
                   pl.BlockSpec((2, None, 1, t), lambda hp, i: (hp, i, 0, 0))],
        out_shape=[jax.ShapeDtypeStruct((B_HEADS, s, FOX_W), BF16), jax.ShapeDtypeStruct((B_HEADS, nt, 1, t), F32)],
        compiler_params=_params(("parallel", "parallel")),
    )(dy, y)


def _fox_unpack(dqt, dkw, dvw, du, t):
    h, nt = dqt.shape[:2]

    def body(dq_ref, dk_ref, dv_ref, _, o_ref, dc_ref):
        lane = lax.broadcasted_iota(jnp.int32, (t, FOX_W), 1)

        def join(a0, a1):
            return jnp.where(lane < B_HD, a0, pltpu.roll(a1, B_HD, 1))

        for hh in range(h):
            dc_ref[hh] = dq_ref[hh][FOX_ONE:FOX_ONE + 1, :] - dk_ref[hh].T[B_HD:B_HD + 1, :]
        pairs = range(0, h, 2)
        cols = ([join(dq_ref[a].T, dq_ref[a + 1].T) for a in pairs] + [join(dk_ref[a], dk_ref[a + 1]) for a in pairs]
                + [join(dv_ref[a], dv_ref[a + 1]) for a in pairs])
        o_ref[...] = jnp.concatenate(cols, axis=1).astype(o_ref.dtype)

    rows = pl.BlockSpec((h, t, FOX_W), lambda i: (0, i, 0))
    return pl.pallas_call(
        body, name="fox_unpack", grid=(nt,),
        in_specs=[pl.BlockSpec((h, None, FOX_W, t), lambda i: (0, i, 0, 0)), rows, rows,
                  pl.BlockSpec(memory_space=pl.ANY)],
        out_specs=[pl.BlockSpec((pl.Element(t), pl.Element(3 * h * B_HD)), lambda i: (i * t, C_QB)),
                   pl.BlockSpec((h, None, 1, t), lambda i: (0, i, 0, 0))],
        out_shape=[jax.ShapeDtypeStruct(du.shape, du.dtype), jax.ShapeDtypeStruct((h, nt, 1, t), F32)],
        input_output_aliases={3: 0},
        compiler_params=_params(("parallel",)),
    )(dqt, dkw, dvw, du)


FOX_DEAD = -110.0


def _fox_norm2(qf, kb):
    h, s, w = qf.shape
    tm = min(2048, s)

    def body(q_ref, k_ref, qo_ref, ko_ref):
        row = lax.broadcasted_iota(jnp.int32, (w, w), 0)
        ones = (row < B_HD).astype(BF16)
        for x_ref, o_ref in ((q_ref, qo_ref), (k_ref, ko_ref)):
            xv = x_ref[...].astype(F32)
            n2 = _dot((xv * xv).astype(BF16), ones)
            o_ref[...] = jnp.broadcast_to(jnp.max(n2, axis=0, keepdims=True)[:, :1], o_ref.shape)

    spec = pl.BlockSpec((None, tm, w), lambda hh, i: (hh, i, 0))
    ospec = pl.BlockSpec((None, None, 8, 128), lambda hh, i: (hh, i, 0, 0))
    shp = jax.ShapeDtypeStruct((h, s // tm, 8, 128), F32)
    return pl.pallas_call(
        body, name="fox_norm2", grid=(h, s // tm), in_specs=[spec, spec], out_specs=[ospec, ospec],
        out_shape=[shp, shp], compiler_params=_params(("parallel", "parallel")),
    )(qf, kb)


def _fox_bounds(qf, kb, c, t):
    q2, k2 = _fox_norm2(qf, kb)
    g = 2.0 * jnp.sqrt(1.02 * jnp.max(q2[:, :, 0, 0], axis=1) * 1.02 * jnp.max(k2[:, :, 0, 0], axis=1))
    return jnp.concatenate([c[:, ::t], c[:, t - 1::t], g[:, None]], axis=1)


SMEM_SPEC = pl.BlockSpec(memory_space=pltpu.SMEM)


def _fox_fwd(qf, kb, vt4, bounds, t):
    h, s, w = qf.shape
    nt = s // t
    sub = FOX_SUB_FWD
    nsub = t // sub
    nh = FOX_HEADS_PER_STEP

    def body(b_ref, q_ref, k_ref, v_ref, o_ref, lse_ref):
        i = pl.program_id(1)
        krow = lax.broadcasted_iota(jnp.int32, (sub, t), 0)
        qcol = lax.broadcasted_iota(jnp.int32, (sub, t), 1)

        def dead_before(hh):
            head = pl.program_id(0) * nh + hh
            top = b_ref[head, 2 * nt] + b_ref[head, i]
            return lax.fori_loop(
                0, i, lambda jj, n: n + (top - b_ref[head, nt + jj] < FOX_DEAD).astype(jnp.int32), 0)

        j_lo = functools.reduce(jnp.minimum, [dead_before(hh) for hh in range(nh)])

        def tile(j, carry, diag):
            out = []
            for hh in range(nh):
                m, acc = carry[hh]
                qv, vj = q_ref[hh], v_ref[hh, j]
                los = [b * sub if diag else 0 for b in range(nsub)]
                sts = [_dot(k_ref[hh, pl.ds(pl.multiple_of(j * t + b * sub, sub), sub), :], qv[lo:, :], NT)
                       for b, lo in enumerate(los)]
                for b, lo in enumerate(los):
                    st = sts[b]
                    if diag:
                        st = jnp.where(krow[:, :t - lo] <= qcol[:, :t - lo], st, NEG)
                    m_old, acc_old = m[:, lo:], acc[:, lo:]
                    m2 = jnp.maximum(m_old, jnp.max(st, axis=0, keepdims=True))
                    p = jnp.exp(st - m2).astype(BF16)
                    acc2 = jnp.exp(m_old - m2) * acc_old + _dot(vj[:, b * sub:(b + 1) * sub], p)
                    m = m2 if lo == 0 else jnp.concatenate([m[:, :lo], m2], axis=1)
                    acc = acc2 if lo == 0 else jnp.concatenate([acc[:, :lo], acc2], axis=1)
                out.append((m, acc))
            return tuple(out)

        init = tuple((jnp.full((1, t), NEG, F32), jnp.zeros((w, t), F32)) for _ in range(nh))
        carry = lax.fori_loop(j_lo, i, lambda j, c: tile(j, c, False), init)
        outs = []
        for hh, (m, acc) in enumerate(tile(i, carry, True)):
            den = acc[B_HD:B_HD + 1, :]
            outs.append(acc[0:B_HD, :] / den)
            lse_ref[hh] = m + jnp.log(den)
        o_ref[...] = jnp.concatenate(outs, axis=0).T.astype(o_ref.dtype)

    return pl.pallas_call(
        body, name="fox_fwd", grid=(h // nh, nt),
        in_specs=[SMEM_SPEC,
                  pl.BlockSpec((nh, t, w), lambda hh, i: (hh, i, 0)),
                  pl.BlockSpec((nh, s, w), lambda hh, i: (hh, 0, 0)),
                  pl.BlockSpec((nh, nt, w, t), lambda hh, i: (hh, 0, 0, 0))],
        out_specs=[pl.BlockSpec((t, nh * B_HD), lambda hh, i: (i, hh)),
                   pl.BlockSpec((nh, 1, t), lambda hh, i: (hh, 0, i))],
        out_shape=[jax.ShapeDtypeStruct((s, h * B_HD), BF16), jax.ShapeDtypeStruct((h, 1, s), F32)],
        compiler_params=_params(("parallel", "parallel")),
    )(bounds, qf, kb, vt4)


def _fox_bwd(qf, dow, lse_row, delta_row, kb, kst4, vb, bounds, t):
    h, s, w = qf.shape
    nt = s // t
    nsub = t // FOX_SUB
    nh = FOX_HEADS_PER_STEP

    def body(b_ref, q_ref, do_ref, lse_ref, dl_ref, k_ref, kt_ref, v_ref, dqt_ref, dk_ref, dv_ref, dk_acc, dv_acc):
        j = pl.program_id(1)

        def alive_after(hh):
            head = pl.program_id(0) * nh + hh
            top = b_ref[head, 2 * nt] - b_ref[head, nt + j]
            return lax.fori_loop(
                j + 1, nt, lambda ii, n: n + (top + b_ref[head, ii] >= FOX_DEAD).astype(jnp.int32), 0)

        i_hi = j + 1 + functools.reduce(jnp.maximum, [alive_after(hh) for hh in range(nh)])

        @pl.when(j == 0)
        def _():
            dqt_ref[...] = jnp.zeros_like(dqt_ref)

        dk_acc[...] = jnp.zeros_like(dk_acc)
        dv_acc[...] = jnp.zeros_like(dv_acc)
        krow = lax.broadcasted_iota(jnp.int32, (FOX_SUB, t), 0)
        qcol = lax.broadcasted_iota(jnp.int32, (FOX_SUB, t), 1)
        subs = [slice(b * FOX_SUB, (b + 1) * FOX_SUB) for b in range(nsub)]

        def tile(i, diag):
            i0 = pl.multiple_of(i * t, t)
            for hh in range(nh):
                qi, doi = q_ref[hh, pl.ds(i0, t), :], do_ref[hh, pl.ds(i0, t), :]
                lse, dl = lse_ref[hh, i], dl_ref[hh, i]
                los = [b * FOX_SUB if diag else 0 for b in range(nsub)]
                sts = [_dot(k_ref[hh, rs, :], qi[lo:, :], NT) for rs, lo in zip(subs, los)]
                dps = [_dot(v_ref[hh, rs, :], doi[lo:, :], NT) for rs, lo in zip(subs, los)]
                dq = None
                for b, (rs, lo) in enumerate(zip(subs, los)):
                    st = sts[b] - lse[:, lo:]
                    if diag:
                        st = jnp.where(krow[:, :t - lo] <= qcol[:, :t - lo], st, NEG)
                    pt = jnp.exp(st)
                    dsb = (pt * (dps[b] - dl[:, lo:])).astype(BF16)
                    dv_acc[hh, rs, :] += _dot(pt.astype(BF16), doi[lo:, :])
                    dk_acc[hh, rs, :] += _dot(dsb, qi[lo:, :])
                    part = _dot(kt_ref[hh, :, rs], dsb)
                    if lo:
                        part = jnp.concatenate([jnp.zeros((w, lo), F32), part], axis=1)
                    dq = part if dq is None else dq + part
                dqt_ref[hh, i] += dq

        def step(i, carry):
            tile(i, False)
            return carry

        tile(j, True)
        lax.fori_loop(j + 1, i_hi, step, 0)
        dk_ref[...] = dk_acc[...] * (B_HD ** -0.5)
        dv_ref[...] = dv_acc[...]

    full = pl.BlockSpec((nh, s, w), lambda hh, j: (hh, 0, 0))
    rowst = pl.BlockSpec((nh, nt, 1, t), lambda hh, j: (hh, 0, 0, 0))
    tl = pl.BlockSpec((nh, t, w), lambda hh, j: (hh, j, 0))
    return pl.pallas_call(
        body, name="fox_bwd", grid=(h // nh, nt),
        in_specs=[SMEM_SPEC, full, full, rowst, rowst, tl,
                  pl.BlockSpec((nh, None, w, t), lambda hh, j: (hh, j, 0, 0)), tl],
        out_specs=[pl.BlockSpec((nh, nt, w, t), lambda hh, j: (hh, 0, 0, 0)), tl, tl],
        out_shape=[jax.ShapeDtypeStruct((h, nt, w, t), F32), jax.ShapeDtypeStruct((h, s, w), F32),
                   jax.ShapeDtypeStruct((h, s, w), F32)],
        scratch_shapes=[pltpu.VMEM((nh, t, w), F32), pltpu.VMEM((nh, t, w), F32)],
        compiler_params=_params(("parallel", "arbitrary")),
    )(bounds, qf, dow, lse_row, delta_row, kb, kst4, vb)


def _mem_fwd(u, mkv, tq=512):
    s = u.shape[0]
    scale = HD ** -0.5

    def body(q_ref, mk_ref, mv_ref, o_ref, lse_ref):
        lses = []
        for h in range(4):
            cs = slice(h * HD, (h + 1) * HD)
            sc = _dot(q_ref[:, cs], mk_ref[:, cs], NT) * scale
            m = jnp.max(sc, axis=-1, keepdims=True)
            p = jnp.exp(sc - m)
            den = jnp.sum(p, axis=-1, keepdims=True)
            o_ref[:, cs] = (_dot(p.astype(BF16), mv_ref[:, cs]) / den).astype(o_ref.dtype)
            lses.append(m + jnp.log(den))
        lse_ref[...] = _lane_pack(lses, (tq, HD))

    return pl.pallas_call(
        body, name="mem_fwd", grid=(s // tq,),
        in_specs=[pl.BlockSpec((tq, 512), lambda i: (i, C_QM // 512)),
                  pl.BlockSpec((N_MEM, 512), lambda i: (0, 0)),
                  pl.BlockSpec((N_MEM, 512), lambda i: (0, 1))],
        out_specs=[pl.BlockSpec((tq, 512), lambda i: (i, 0)), pl.BlockSpec((tq, HD), lambda i: (i, 0))],
        out_shape=[jax.ShapeDtypeStruct((s, 512), BF16), jax.ShapeDtypeStruct((s, HD), F32)],
        compiler_params=_params(("parallel",)),
    )(u, mkv, mkv)


def _mem_bwd(u, mkv, o, do, lse, du, tq=512):
    s = u.shape[0]
    scale = HD ** -0.5

    def body(q_ref, mk_ref, mv_ref, o_ref, do_ref, lse_ref, _, dq_ref, dmk_ref, dmv_ref):
        @pl.when(pl.program_id(0) == 0)
        def _():
            dmk_ref[...] = jnp.zeros_like(dmk_ref)
            dmv_ref[...] = jnp.zeros_like(dmv_ref)

        for h in range(4):
            cs = slice(h * HD, (h + 1) * HD)
            qv, dov = q_ref[:, cs], do_ref[:, cs]
            sc = _dot(qv, mk_ref[:, cs], NT) * scale
            p = jnp.exp(sc - lse_ref[:, h:h + 1])
            delta = jnp.sum(dov.astype(F32) * o_ref[:, cs].astype(F32), axis=-1, keepdims=True)
            ds = p * (_dot(dov, mv_ref[:, cs], NT) - delta)
            dsb = ds.astype(BF16)
            dq_ref[:, cs] = (_dot(dsb, mk_ref[:, cs]) * scale).astype(dq_ref.dtype)
            dmk_ref[:, cs] += _dot(dsb, qv, TN) * scale
            dmv_ref[:, cs] += _dot(p.astype(BF16), dov, TN)

    row = pl.BlockSpec((tq, 512), lambda i: (i, 0))
    acc = pl.BlockSpec((N_MEM, 512), lambda i: (0, 0))
    return pl.pallas_call(
        body, name="mem_bwd", grid=(s // tq,),
        in_specs=[pl.BlockSpec((tq, 512), lambda i: (i, C_QM // 512)),
                  pl.BlockSpec((N_MEM, 512), lambda i: (0, 0)),
                  pl.BlockSpec((N_MEM, 512), lambda i: (0, 1)),
                  row, row, pl.BlockSpec((tq, HD), lambda i: (i, 0)), pl.BlockSpec(memory_space=pl.ANY)],
        out_specs=[pl.BlockSpec((tq, 512), lambda i: (i, C_QM // 512)), acc, acc],
        out_shape=[jax.ShapeDtypeStruct(du.shape, du.dtype), jax.ShapeDtypeStruct((N_MEM, 512), F32),
                   jax.ShapeDtypeStruct((N_MEM, 512), F32)],
        input_output_aliases={6: 0},
        compiler_params=_params(("arbitrary",)),
    )(u, mkv, mkv, o, do, lse, du)


FB_CHIP = FB_ORIG // SHARD_COLS
FB_AT = FB_ORIG - FB_CHIP * SHARD_COLS


def _chip_slabs(main, fb):
    cuts = [SHARD_COLS * p - (B_HEADS if p > FB_CHIP else 0) for p in range(N_CHIPS + 1)]
    slabs = [main[:, a:b] for a, b in zip(cuts[:-1], cuts[1:])]
    own = slabs[FB_CHIP]
    slabs[FB_CHIP] = jnp.concatenate([own[:, :FB_AT], fb, own[:, FB_AT:]], axis=1)
    return slabs


def _split_forget(slabs):
    own = slabs[FB_CHIP]
    parts = list(slabs[:FB_CHIP]) + [own[:, :FB_AT], own[:, FB_AT + B_HEADS:]] + list(slabs[FB_CHIP + 1:])
    return jnp.concatenate(parts, axis=1), own[:, FB_AT:FB_AT + B_HEADS]


def _local_step(x, mem, pos, target, g_pre, g_post, g_mem, w_main, w_fb, b_forget, b_merge,
                w_mem_kv, w_ba, w_bb, w_bm, w_out, exchange=None):
    s = x.shape[0]
    t_fox = min(512, s)
    nt = s // t_fox
    half = ROT_DIM // 2
    inv = ROPE_THETA ** (-jnp.arange(half, dtype=F32) / half)
    inv128 = jnp.concatenate([inv, inv, jnp.zeros((HD - ROT_DIM,), F32)]).reshape(1, HD)

    h, h_t = _rms_fwd_both("norm_pre", x, g_pre)
    u = _mm("proj_in", h, w_main, "nn", BF16, tm=4096)
    ufb = _mm("proj_fb", h, w_fb, "nn", F32)
    memn = _rms_fwd("norm_mem", mem, g_mem)
    mkv = _mm("proj_mem", memn, w_mem_kv, "nn", BF16)

    qkv = _rope_fwd(u, pos, inv128)
    views = [tuple(qkv[3 * g:3 * g + 3]) for g in range(3)]
    os_, lses = [], []
    for g, d in enumerate(DILATIONS):
        o_g, lse_g = _band_fwd("band_fwd%d" % g, *views[g], d)
        os_.append((o_g, d * A_GROUP, 0, d))
        lses.append((lse_g, d * HD, 0, d))

    def merge_a(o1, o2, o3, l1, l2, l3, za, *scr):
        o1, o2, o3 = [_from_class(o, scr, d) for o, d in zip((o1, o2, o3), DILATIONS)]
        l1, l2, l3 = [_from_class(lv, scr, d) for lv, d in zip((l1, l2, l3), DILATIONS)]
        ys, tots = [], []
        for hh in range(4):
            cs, hs = slice(hh * HD, (hh + 1) * HD), slice(hh, hh + 1)
            mx = jnp.maximum(jnp.maximum(l1[:, hs], l2[:, hs]), l3[:, hs])
            e1, e2, e3 = jnp.exp(l1[:, hs] - mx), jnp.exp(l2[:, hs] - mx), jnp.exp(l3[:, hs] - mx)
            den = e1 + e2 + e3
            ys.append((e1 * o1[:, cs] + e2 * o2[:, cs] + e3 * o3[:, cs]) / den)
            tots.append(mx + jnp.log(den))
        y = jnp.concatenate(ys, axis=1)
        zf = za.astype(F32)
        tot = _lane_pack(tots, l1.shape)
        return (y, y * (zf * _sig(zf))) + tuple(_to_class(tot, scr, d) for d in DILATIONS)

    res = _rows("merge_a", merge_a, os_ + lses + [(u, 512, C_ZA // 512)], [],
                [(512, BF16), (512, BF16)] + [(d * HD, F32, d) for d in DILATIONS], tm=ROPE_TM,
                scratch=_class_scratch(ROPE_TM))
    y_a, yg_a, lse_a = res[0], res[1], res[2:5]

    zrow = ufb[:, :B_HEADS].T
    c = _fox_prep(zrow, b_forget.reshape(B_HEADS, 1))
    qf, kb, kst4, vb, vt4 = _fox_pack(u, c.reshape(B_HEADS, s, 1), t_fox)
    bounds = _fox_bounds(qf, kb, c, t_fox)
    y_b, lse_b = _fox_fwd(qf, kb, vt4, bounds, t_fox)

    y_m, lse_m = _mem_fwd(u, mkv)

    def gate(y, z):
        zf = z.astype(F32)
        return (y.astype(F32) * (zf * _sig(zf)),)

    yg_b = _rows("gate_b", gate, [y_b, (u, 512, C_ZB // 512)], [], [(512, BF16)])[0]
    yg_m = _rows("gate_m", gate, [y_m, (u, 512, C_ZM // 512)], [], [(512, BF16)])[0]

    br_a = _mm("branch_a", yg_a, w_ba, "nn", BF16)
    br_b = _mm("branch_b", yg_b, w_bb, "nn", BF16)
    br_m = _mm("branch_m", yg_m, w_bm, "nn", BF16)
    gl = [(u, 1024, C_GL // 1024 + i) for i in range(3)]
    bm3 = b_merge.reshape(3, D_MODEL)

    def merge(g0, g1, g2, b0, b1, b2, bm):
        tot = 0.0
        for i, (gv, bv) in enumerate(((g0, b0), (g1, b1), (g2, b2))):
            tot = tot + _sig(gv.astype(F32) + bm[i:i + 1, :]) * bv.astype(F32)
        return (tot,)

    merged = _rows("merge_gates", merge, gl + [br_a, br_b, br_m], [bm3], [(D_MODEL, BF16)])[0]
    out = _mm("proj_out", merged, w_out, "nn", F32)

    def tail(xv, ov, tv, gv):
        r = lax.rsqrt(jnp.mean(ov * ov, axis=-1, keepdims=True) + EPS)
        n = ov * r
        err = xv + n * gv - tv
        dy = err * (1.0 / D_MODEL)
        dn = dy * gv
        dout = r * (dn - n * jnp.mean(dn * n, axis=-1, keepdims=True))
        return (dy, dout, jnp.sum(0.5 * err * err * (1.0 / D_MODEL), axis=0, keepdims=True),
                jnp.sum(dy * n, axis=0, keepdims=True))

    dy, dout, loss_lanes, g_post_grad = _rows(
        "tail", tail, [x, out, target], [g_post], [(D_MODEL, F32), (D_MODEL, BF16)],
        reds=[D_MODEL, D_MODEL], tm=256)

    dmerged = _mm("d_merged", dout, w_out, "nt", BF16)
    gw_out = _mm("g_w_out", merged, dout, "tn", F32)

    def merge_bwd(dm, g0, g1, g2, b0, b1, b2, bm):
        dmf = dm.astype(F32)
        dbs, dgs, sums = [], [], []
        for i, (gv, bv) in enumerate(((g0, b0), (g1, b1), (g2, b2))):
            sg = _sig(gv.astype(F32) + bm[i:i + 1, :])
            dbs.append(dmf * sg)
            dg = dmf * bv.astype(F32) * sg * (1.0 - sg)
            dgs.append(dg)
            sums.append(jnp.sum(dg, axis=0, keepdims=True))
        return tuple(dbs + [jnp.concatenate(dgs, axis=1)] + sums)

    du = lax.empty(u.shape, BF16)
    res = _rows("merge_bwd", merge_bwd, [dmerged] + gl + [br_a, br_b, br_m], [bm3],
                [(D_MODEL, BF16)] * 3 + [(3 * D_MODEL, BF16)], reds=[D_MODEL] * 3, tm=256,
                into=(du, 3, ("column", C_GL)))
    dbr, du, g_bmerge = res[0:3], res[3], jnp.concatenate(res[4:7], axis=1)

    dyg, gw_branch = [], []
    for nm, dbv, wv, ygv in (("a", dbr[0], w_ba, yg_a), ("b", dbr[1], w_bb, yg_b), ("m", dbr[2], w_bm, yg_m)):
        dyg.append(_mm("d_yg_" + nm, dbv, wv, "nt", BF16))
        gw_branch.append(_mm("g_w_branch_" + nm, ygv, dbv, "tn", F32))

    def gate_bwd(dg, y, z):
        dgf, yf, zf = dg.astype(F32), y.astype(F32), z.astype(F32)
        sg = _sig(zf)
        return dgf * (zf * sg), dgf * yf * (sg * (1.0 + zf * (1.0 - sg)))

    def gate_bwd_a(dg, y, z, *scr):
        dyv, dz = gate_bwd(dg, y, z)
        prod = dyv * y.astype(F32)
        dl = [jnp.sum(prod[:, hh * HD:(hh + 1) * HD], axis=-1, keepdims=True) for hh in range(4)]
        delta = _lane_pack(dl, (dg.shape[0], HD))
        return ((dz,) + tuple(_to_class(dyv, scr, d) for d in DILATIONS)
                + tuple(_to_class(delta, scr, d) for d in DILATIONS))

    res = _rows("gate_bwd_a", gate_bwd_a, [dyg[0], y_a, (u, 512, C_ZA // 512)], [],
                [(512, BF16)] + [(d * A_GROUP, BF16, d) for d in DILATIONS] + [(d * HD, F32, d) for d in DILATIONS],
                tm=ROPE_TM, scratch=_class_scratch(ROPE_TM), into=(du, 0, C_ZA // 512))
    du, dy_a, delta_a = res[0], res[1:4], res[4:7]
    dy_b, du = _rows("gate_bwd_b", gate_bwd, [dyg[1], y_b, (u, 512, C_ZB // 512)], [],
                     [(512, BF16), (512, BF16)], into=(du, 1, C_ZB // 512))
    dy_m, du = _rows("gate_bwd_m", gate_bwd, [dyg[2], y_m, (u, 512, C_ZM // 512)], [],
                     [(512, BF16), (512, BF16)], into=(du, 1, C_ZM // 512))

    du, dmk, dmv = _mem_bwd(u, mkv, y_m, dy_m, lse_m, du)
    dmkv = jnp.concatenate([dmk, dmv], axis=1)
    gw_mem_kv = _mm("g_w_mem_kv", memn, dmkv, "tn", F32)
    dmemn = _mm("d_memn", dmkv, w_mem_kv, "nt", F32)

    def mem_gain_grad(mv, dv):
        r = lax.rsqrt(jnp.mean(mv * mv, axis=-1, keepdims=True) + EPS)
        return (jnp.sum(dv * mv * r, axis=0, keepdims=True),)

    g_mem_grad = _rows("g_norm_mem", mem_gain_grad, [mem, dmemn], [], [], reds=[D_MODEL], tm=N_MEM)[0]

    dow, delta_b = _fox_pack_bwd(dy_b, y_b, t_fox)
    dqt, dkw, dvw = _fox_bwd(qf, dow, lse_b.reshape(B_HEADS, nt, 1, t_fox), delta_b, kb, kst4, vb, bounds, t_fox)
    du, dc = _fox_unpack(dqt, dkw, dvw, du, t_fox)
    dzrow, g_bforget = _fox_prep_bwd(dc.reshape(B_HEADS, s), zrow, b_forget.reshape(B_HEADS, 1))
    dfb = jnp.zeros((s, HD), BF16).at[:, :B_HEADS].set(dzrow.T.astype(BF16))

    dqs, dks, dvs = [], [], []
    for g, d in enumerate(DILATIONS):
        qv, kv, vv = views[g]
        dqs.append(_band_dq("band_dq%d" % g, qv, kv, vv, dy_a[g], lse_a[g], delta_a[g], d))
        dk_g, dv_g = _band_dkv("band_dkv%d" % g, qv, kv, vv, dy_a[g], lse_a[g], delta_a[g], d)
        dks.append(dk_g)
        dvs.append(dv_g)
    du = _rope_bwd(dqs, dks, dvs, pos, inv128, du)

    gw_main = _mm("g_w_main", h_t, du, "nn", F32, tk=2048)
    gw_fb = _mm("g_w_fb", h, dfb, "tn", F32)
    grads = dict(norm_post_g=g_post_grad, norm_mem_g=g_mem_grad, w_in=_chip_slabs(gw_main, gw_fb[:, :B_HEADS]),
                 b_forget=g_bforget.reshape(1, B_HEADS), b_merge=g_bmerge, w_mem_kv=gw_mem_kv,
                 w_branch_a=gw_branch[0], w_branch_b=gw_branch[1], w_branch_m=gw_branch[2], w_out=gw_out)
    side = exchange(grads) if exchange else None
    dh_main = _mm("d_h", du, w_main, "nt", F32, tk=2816, side=side)
    landed = None
    if side:
        dh_main, landed = dh_main[0], dh_main[1:]
    dh_fb = _mm("d_h_fb", dfb, w_fb, "nt", F32)

    def pre_bwd(xv, d1, d2, dyv, gv):
        r = lax.rsqrt(jnp.mean(xv * xv, axis=-1, keepdims=True) + EPS)
        n = xv * r
        dhv = d1 + d2
        dn = dhv * gv
        dx = r * (dn - n * jnp.mean(dn * n, axis=-1, keepdims=True))
        return dyv + dx, jnp.sum(dhv * n, axis=0, keepdims=True)

    grad_x, g_pre_grad = _rows("norm_pre_bwd", pre_bwd, [x, dh_main, dh_fb, dy], [g_pre],
                               [(D_MODEL, F32)], reds=[D_MODEL], tm=256)

    grads["norm_pre_g"] = g_pre_grad
    return loss_lanes, grad_x, grads, landed


HBM_SPEC = pl.BlockSpec(memory_space=pltpu.HBM)


def _place():
    x, y, c = lax.axis_index("x"), lax.axis_index("y"), lax.axis_index("c")
    chips = [(1 - x, y), (x, 1 - y), (1 - x, 1 - y)]
    return x, y, c, 2 * x + y, chips


N_CHUNKS = 4


def _units(parts, row_axis):
    units = []
    for i, a in enumerate(parts):
        ch = a.shape[row_axis] // N_CHUNKS
        units += [(i, pl.ds(k * ch, ch)) for k in range(N_CHUNKS)]
    return units


def _gather_weights(parts):
    n = len(parts)
    units = _units(parts, 1)
    nu = len(units)
    via_y = [(u % N_CHUNKS) < N_CHUNKS // 2 for u in range(nu)]

    def body(*refs):
        srcs, outs = refs[:n], refs[n:2 * n]
        send_sems, recv_sems = refs[2 * n:]
        x, y, c, p, _ = _place()
        me, sib = (x, y, c), (x, y, 1 - c)
        xn, yn, dg = (1 - x, y), (x, 1 - y), (1 - x, 1 - y)

        def cp(u, k, chip, half, to, from_src=False):
            i, rs = units[u]
            dst = outs[i].at[2 * chip[0] + chip[1], half, rs]
            return pltpu.make_async_remote_copy(
                src_ref=srcs[i].at[half, rs] if from_src else dst, dst_ref=dst, send_sem=send_sems.at[u, k],
                recv_sem=recv_sems.at[u, k], device_id=to, device_id_type=MESH)

        sent = []

        def go(copy):
            copy.start()
            sent.append(copy)

        for u in range(nu):
            go(cp(u, 0, (x, y), c, (*xn, c), from_src=True))
            go(cp(u, 1, (x, y), c, (*yn, c), from_src=True))
        for u in range(nu):
            cp(u, 0, xn, c, me).wait_recv()
            go(cp(u, 4, xn, c, sib))
            if via_y[u]:
                go(cp(u, 2, xn, c, (*yn, c)))
            cp(u, 1, yn, c, me).wait_recv()
            go(cp(u, 5, yn, c, sib))
            if not via_y[u]:
                go(cp(u, 3, yn, c, (*xn, c)))
        for u in range(nu):
            cp(u, 2 if via_y[u] else 3, dg, c, me).wait_recv()
            go(cp(u, 6, dg, c, sib))
        for u in range(nu):
            for k, chip in ((4, xn), (5, yn), (6, dg)):
                cp(u, k, chip, 1 - c, me).wait_recv()
        for copy in sent:
            copy.wait_send()

    return pl.pallas_call(
        body, name="gather_weights", in_specs=[HBM_SPEC] * n, out_specs=[HBM_SPEC] * n,
        out_shape=[jax.ShapeDtypeStruct((N_CHIPS,) + a.shape, a.dtype) for a in parts],
        scratch_shapes=[pltpu.SemaphoreType.DMA((nu, 7)), pltpu.SemaphoreType.DMA((nu, 7))],
    )(*parts)


def _swap_with_sibling(parts):
    n = len(parts)
    units = _units(parts, 2)

    def body(*refs):
        srcs, outs = refs[:n], refs[n:2 * n]
        send_sems, recv_sems = refs[2 * n:]
        x, y, c, _, _ = _place()
        cps = [pltpu.make_async_remote_copy(
            src_ref=srcs[i].at[q, 1 - c, rs], dst_ref=outs[i].at[q, rs], send_sem=send_sems.at[u, q],
            recv_sem=recv_sems.at[u, q], device_id=(x, y, 1 - c), device_id_type=MESH)
            for q in range(N_CHIPS) for u, (i, rs) in enumerate(units)]
        for cpy in cps:
            cpy.start()
        for cpy in cps:
            cpy.wait()

    return pl.pallas_call(
        body, name="swap_with_sibling", in_specs=[HBM_SPEC] * n, out_specs=[HBM_SPEC] * n,
        out_shape=[jax.ShapeDtypeStruct(a.shape[:1] + a.shape[2:], a.dtype) for a in parts],
        scratch_shapes=[pltpu.SemaphoreType.DMA((len(units), N_CHIPS)),
                        pltpu.SemaphoreType.DMA((len(units), N_CHIPS))],
    )(*parts)


def _scatter_to_owners(parts):
    n = len(parts)
    units = _units(parts, 1)

    def copies(srcs, outs, send_sems, recv_sems, incoming):
        x, y, c, p, chips = _place()
        return [pltpu.make_async_remote_copy(
            src_ref=srcs[i].at[2 * cx + cy, rs], dst_ref=outs[i].at[(2 * cx + cy) if incoming else p, rs],
            send_sem=send_sems.at[u, j], recv_sem=recv_sems.at[u, j], device_id=(cx, cy, c), device_id_type=MESH)
            for u, (i, rs) in enumerate(units) for j, (cx, cy) in enumerate(chips)]

    def start(ins, outs, scratch):
        for cpy in copies(ins, outs, *scratch, incoming=False):
            cpy.start()

    def wait(ins, outs, scratch):
        for cpy in copies(ins, outs, *scratch, incoming=True):
            cpy.wait_recv()
        for cpy in copies(ins, outs, *scratch, incoming=False):
            cpy.wait_send()

    return dict(ins=list(parts), outs=[jax.ShapeDtypeStruct(a.shape, a.dtype) for a in parts],
                scratch=[pltpu.SemaphoreType.DMA((len(units), 3)), pltpu.SemaphoreType.DMA((len(units), 3))],
                start=start, wait=wait)


def _share_with_sibling(parts):
    n = len(parts)
    units = _units(parts, 1)

    def body(*refs):
        srcs, outs = refs[:n], refs[n:2 * n]
        send_sems, recv_sems = refs[2 * n:]
        x, y, c, _, _ = _place()
        sends = [pltpu.make_async_remote_copy(
            src_ref=srcs[i].at[0, rs], dst_ref=outs[i].at[c, rs], send_sem=send_sems.at[u],
            recv_sem=recv_sems.at[u], device_id=(x, y, 1 - c), device_id_type=MESH)
            for u, (i, rs) in enumerate(units)]
        for cpy in sends:
            cpy.start()
        for u, (i, rs) in enumerate(units):
            pltpu.make_async_remote_copy(
                src_ref=srcs[i].at[0, rs], dst_ref=outs[i].at[1 - c, rs], send_sem=send_sems.at[u],
                recv_sem=recv_sems.at[u], device_id=(x, y, 1 - c), device_id_type=MESH).wait_recv()
        for cpy in sends:
            cpy.wait_send()

    return pl.pallas_call(
        body, name="share_with_sibling", in_specs=[HBM_SPEC] * n, out_specs=[HBM_SPEC] * n,
        out_shape=[jax.ShapeDtypeStruct((2,) + a.shape[1:], a.dtype) for a in parts],
        scratch_shapes=[pltpu.SemaphoreType.DMA((len(units),)), pltpu.SemaphoreType.DMA((len(units),))],
    )(*parts)


def _sum_small(v):
    def body(v_ref, out_ref, buf, send_sems, recv_sems):
        x, y, c, _, _ = _place()
        me = 4 * x + 2 * y + c
        buf[me] = v_ref[...]
        flips = [(dx, dy, dc) for dx in (0, 1) for dy in (0, 1) for dc in (0, 1)][1:]
        sends = []
        for k, (dx, dy, dc) in enumerate(flips):
            cpy = pltpu.make_async_remote_copy(
                src_ref=v_ref, dst_ref=buf.at[me], send_sem=send_sems.at[k], recv_sem=recv_sems.at[k],
                device_id=((x + dx) % 2, (y + dy) % 2, (c + dc) % 2), device_id_type=MESH)
            cpy.start()
            sends.append(cpy)
        for k, (dx, dy, dc) in enumerate(flips):
            px, py, pc = (x + dx) % 2, (y + dy) % 2, (c + dc) % 2
            pltpu.make_async_remote_copy(
                src_ref=v_ref, dst_ref=buf.at[4 * px + 2 * py + pc], send_sem=send_sems.at[k],
                recv_sem=recv_sems.at[k], device_id=(px, py, pc), device_id_type=MESH).wait_recv()
        for cpy in sends:
            cpy.wait_send()
        tot = buf[0]
        for i in range(1, N_DEV):
            tot = tot + buf[i]
        out_ref[...] = tot

    return pl.pallas_call(
        body, name="sum_small", out_shape=jax.ShapeDtypeStruct(v.shape, v.dtype),
        in_specs=[pl.BlockSpec(memory_space=pltpu.VMEM)], out_specs=pl.BlockSpec(memory_space=pltpu.VMEM),
        scratch_shapes=[pltpu.VMEM((N_DEV,) + v.shape, v.dtype), pltpu.SemaphoreType.DMA((N_DEV - 1,)),
                        pltpu.SemaphoreType.DMA((N_DEV - 1,))],
    )(v)


def _add_chips(name, landed, pair, chip):
    nq, r, w = landed.shape
    tr = 64

    def body(chip_ref, *refs):
        own = refs[nq][...].astype(F32)
        tot = None
        for q in range(nq):
            term = jnp.where(chip_ref[0] == q, own, refs[q][...].astype(F32))
            tot = term if tot is None else tot + term
        refs[nq + 1][...] = tot

    specs = [pl.BlockSpec((None, tr, w), functools.partial(lambda j, chip_ref, q: (q, j, 0), q=q)) for q in range(nq)]
    specs.append(pl.BlockSpec((None, tr, w), lambda j, chip_ref: (chip_ref[0], j, 0)))
    grid_spec = pltpu.PrefetchScalarGridSpec(
        num_scalar_prefetch=1, grid=(r // tr,), in_specs=specs,
        out_specs=pl.BlockSpec((None, tr, w), lambda j, chip_ref: (0, j, 0)))
    return pl.pallas_call(
        body, name=name, grid_spec=grid_spec, out_shape=jax.ShapeDtypeStruct((1, r, w), F32),
        compiler_params=_params(("parallel",)),
    )(jnp.reshape(chip, (1,)).astype(jnp.int32), *([landed] * nq), pair)


def _add_pair(name, halves, got, c):
    nq, _, r, w = halves.shape
    tr = 64

    def body(c_ref, a_ref, b_ref, o_ref):
        o_ref[...] = (a_ref[...] + b_ref[...]).astype(o_ref.dtype)

    grid_spec = pltpu.PrefetchScalarGridSpec(
        num_scalar_prefetch=1, grid=(nq, r // tr),
        in_specs=[pl.BlockSpec((None, None, tr, w), lambda i, j, c_ref: (i, c_ref[0], j, 0)),
                  pl.BlockSpec((None, tr, w), lambda i, j, c_ref: (i, j, 0))],
        out_specs=pl.BlockSpec((None, tr, w), lambda i, j, c_ref: (i, j, 0)))
    return pl.pallas_call(
        body, name=name, grid_spec=grid_spec, out_shape=jax.ShapeDtypeStruct((nq, r, w), BF16),
        compiler_params=_params(("parallel", "parallel")),
    )(jnp.reshape(c, (1,)).astype(jnp.int32), halves, got)


def _adamw(name, w, g, m, v, tm):
    def fn(wv, gv, mv, vv):
        m2 = ADAM_B1 * mv + (1.0 - ADAM_B1) * gv
        v2 = ADAM_B2 * vv + (1.0 - ADAM_B2) * (gv * gv)
        m_hat = m2 / (1.0 - ADAM_B1 ** ADAM_STEP)
        v_hat = v2 / (1.0 - ADAM_B2 ** ADAM_STEP)
        return -ADAM_LR * (m_hat / (jnp.sqrt(v_hat) + ADAM_EPS) + ADAM_WD * wv), m2, v2
    c = w.shape[1]
    return _rows(name, fn, [w, g, m, v], [], [(c, F32)] * 3, tm=tm)


REST_ROWS = 256 + 3 * 128 + 256
REST_SPLITS = (("w_mem_kv", 0, 256), ("w_branch_a", 256, 128), ("w_branch_b", 384, 128),
               ("w_branch_m", 512, 128), ("w_out", 640, 256))


def _rest_pack(t):
    return jnp.concatenate([t[n].reshape(rows, D_MODEL) for n, _, rows in REST_SPLITS], axis=0)


def _rest_unpack(a, shapes):
    return {n: a[r0:r0 + rows].reshape(shapes[n]) for n, r0, rows in REST_SPLITS}


def _small_pack(pre, post, memg, bforget, bmerge):
    pad = jnp.zeros((1, D_MODEL - B_HEADS), F32)
    return jnp.concatenate([pre, post, memg, bmerge.reshape(3, D_MODEL),
                            jnp.concatenate([bforget, pad], axis=1), jnp.zeros((1, D_MODEL), F32)], axis=0)


def _small_unpack(s8):
    return dict(norm_pre_g=s8[0:1], norm_post_g=s8[1:2], norm_mem_g=s8[2:3],
                b_merge=s8[3:6].reshape(1, 3 * D_MODEL), b_forget=s8[6:7, :B_HEADS])


WEIGHTS = ("norm_pre_g", "norm_post_g", "norm_mem_g", "w_in", "b_forget", "b_merge", "w_mem_kv",
           "w_branch_a", "w_branch_b", "w_branch_m", "w_out")
SMALL = ("norm_pre_g", "norm_post_g", "norm_mem_g", "b_forget", "b_merge")


def kernel(x, mem, positions, norm_pre_g, norm_post_g, norm_mem_g, w_in, b_forget, b_merge, w_mem_kv, w_branch_a, w_branch_b, w_branch_m, w_out, loss_target, m_norm_pre_g, m_norm_post_g, m_norm_mem_g, m_w_in, m_b_forget, m_b_merge, m_w_mem_kv, m_w_branch_a, m_w_branch_b, m_w_branch_m, m_w_out, v_norm_pre_g, v_norm_post_g, v_norm_mem_g, v_w_in, v_b_forget, v_b_merge, v_w_mem_kv, v_w_branch_a, v_w_branch_b, v_w_branch_m, v_w_out):
    w = dict(norm_pre_g=norm_pre_g, norm_post_g=norm_post_g, norm_mem_g=norm_mem_g, w_in=w_in[0],
             b_forget=b_forget, b_merge=b_merge, w_mem_kv=w_mem_kv[0], w_branch_a=w_branch_a[0],
             w_branch_b=w_branch_b[0], w_branch_m=w_branch_m[0], w_out=w_out[0])
    mo = dict(norm_pre_g=m_norm_pre_g, norm_post_g=m_norm_post_g, norm_mem_g=m_norm_mem_g, w_in=m_w_in[0],
              b_forget=m_b_forget, b_merge=m_b_merge, w_mem_kv=m_w_mem_kv[0], w_branch_a=m_w_branch_a[0],
              w_branch_b=m_w_branch_b[0], w_branch_m=m_w_branch_m[0], w_out=m_w_out[0])
    vo = dict(norm_pre_g=v_norm_pre_g, norm_post_g=v_norm_post_g, norm_mem_g=v_norm_mem_g, w_in=v_w_in[0],
              b_forget=v_b_forget, b_merge=v_b_merge, w_mem_kv=v_w_mem_kv[0], w_branch_a=v_w_branch_a[0],
              w_branch_b=v_w_branch_b[0], w_branch_m=v_w_branch_m[0], w_out=v_w_out[0])
    s = x.shape[1]
    c = lax.axis_index("c")

    chip = 2 * lax.axis_index("x") + lax.axis_index("y")

    def put(whole, own, slot):
        return lax.dynamic_update_index_in_dim(whole, own.astype(whole.dtype), slot, 0)

    own_w = [w["w_in"].astype(BF16).reshape(2, D_MODEL // 2, SHARD_COLS),
             _rest_pack(w).astype(BF16).reshape(2, REST_ROWS // 2, D_MODEL)]
    all_in, all_rest = _gather_weights(own_w)
    all_in = all_in.reshape(N_CHIPS, D_MODEL, SHARD_COLS)
    own_in, own_rest = own_w[0].reshape(D_MODEL, SHARD_COLS), own_w[1].reshape(REST_ROWS, D_MODEL)
    w_main, w_fb = _split_forget([jnp.where(chip == p, own_in, all_in[p]) for p in range(N_CHIPS)])
    w_fb = jnp.concatenate([w_fb, jnp.zeros((D_MODEL, HD - B_HEADS), BF16)], axis=1)
    all_rest = all_rest.reshape(N_CHIPS, REST_ROWS, D_MODEL)
    all_rest = jnp.stack([jnp.where(chip == p, own_rest, all_rest[p]) for p in range(N_CHIPS)])
    w_kv_f = all_rest[:, 0:256].reshape(D_MODEL, D_MODEL)
    w_br_f = [all_rest[:, 256 + 128 * i:384 + 128 * i].reshape(N_CHIPS, 512, 256).transpose(1, 0, 2)
              .reshape(512, D_MODEL) for i in range(3)]
    w_out_f = all_rest[:, 640:896].reshape(D_MODEL, D_MODEL)

    pair = []

    def exchange(g):
        def per_chip(name, p):
            a = g[name]
            if name in ("w_mem_kv", "w_out"):
                return a[256 * p:256 * (p + 1)]
            return a[:, 256 * p:256 * (p + 1)]

        in4 = jnp.stack(g["w_in"])
        rest4 = jnp.stack([_rest_pack({n: per_chip(n, p) for n, _, _ in REST_SPLITS}) for p in range(N_CHIPS)])
        halves = [in4.reshape(N_CHIPS, 2, D_MODEL // 2, SHARD_COLS),
                  rest4.reshape(N_CHIPS, 2, REST_ROWS // 2, D_MODEL)]
        got = _swap_with_sibling(halves)
        pair.extend(_add_pair("add_pair_%d" % i, halves[i], got[i], c) for i in range(2))
        return _scatter_to_owners(pair)

    loss_lanes, grad_x, g, landed = _local_step(
        x[0], mem[0], positions.reshape(s, 1), loss_target[0], norm_pre_g, norm_post_g, norm_mem_g,
        w_main, w_fb, b_forget, b_merge, w_kv_f, w_br_f[0], w_br_f[1], w_br_f[2], w_out_f, exchange)
    loss = lax.psum(jnp.sum(loss_lanes), ("x", "y", "c"))
    half = [_add_chips("add_chips_%d" % i, landed[i], pair[i], chip) for i in range(2)]
    red_in, red_rest = [put(a, o[0], c) for a, o in zip(_share_with_sibling(half), half)]
    gs = {"w_in": red_in.reshape(D_MODEL, SHARD_COLS)}
    gs.update(_rest_unpack(red_rest.reshape(REST_ROWS, D_MODEL), {n: w[n].shape for n, _, _ in REST_SPLITS}))
    gs.update(_small_unpack(_sum_small(_small_pack(
        g["norm_pre_g"], g["norm_post_g"], g["norm_mem_g"], g["b_forget"], g["b_merge"]))))

    delta, new_m, new_v = {}, {}, {}
    for n, tm in (("w_in", 128), ("w_mem_kv", 256), ("w_branch_a", 512), ("w_branch_b", 512),
                  ("w_branch_m", 512), ("w_out", 256)):
        d_, m_, v_ = _adamw("adamw_" + n, w[n], gs[n], mo[n], vo[n], tm)
        delta[n], new_m[n], new_v[n] = d_[None], m_[None], v_[None]
        gs[n] = gs[n][None]
    packs = [_small_pack(*[t[n] for n in SMALL])
             for t in (w, gs, mo, vo)]
    for res, store in zip(_adamw("adamw_small", *packs, 8), (delta, new_m, new_v)):
        store.update(_small_unpack(res))

    return (loss, grad_x[None], *[gs[n] for n in WEIGHTS], *[delta[n] for n in WEIGHTS],
            *[new_m[n] for n in WEIGHTS], *[new_v[n] for n in WEIGHTS])
```

```python
import functools

import jax
import jax.numpy as jnp
from jax import lax
from jax.experimental import pallas as pl
from jax.experimental.pallas import tpu as pltpu

F32 = jnp.float32
BF16 = jnp.bfloat16
MESH = pl.DeviceIdType.MESH

D_MODEL = 1024
N_MEM = 256
EPS = 1e-6
NEG = -1e30
ROPE_THETA = 500000.0
ROT_DIM = 32
HD = 128
A_GROUP = 512
DILATIONS = (1, 4, 16)
BAND = 128
B_HEADS = 8
B_HD = 64
N_CHIPS = 4
N_DEV = 8

C_QA, C_KA, C_VA, C_ZA = 0, 1536, 3072, 4608
C_QB, C_KB, C_VB, C_ZB = 5120, 5632, 6144, 6656
C_QM, C_ZM, C_GL = 7168, 7680, 8192
FB_ORIG = 6656
IN_COLS = 11272
SHARD_COLS = IN_COLS // N_CHIPS

ADAM_LR, ADAM_B1, ADAM_B2, ADAM_EPS, ADAM_WD, ADAM_STEP = 0.001, 0.9, 0.999, 1e-08, 0.01, 10

VMEM_LIMIT_V7X = 56 * 1024 * 1024

NT = (((1,), (1,)), ((), ()))
NN = (((1,), (0,)), ((), ()))
TN = (((0,), (0,)), ((), ()))


def _params(sem):
    return pltpu.CompilerParams(dimension_semantics=sem, vmem_limit_bytes=VMEM_LIMIT_V7X)


def _dot(a, b, dn=NN):
    return lax.dot_general(a, b, dn, preferred_element_type=F32)


def _sig(z):
    return 1.0 / (1.0 + jnp.exp(-z))


def _rows(name, fn, row_ins, bc_ins, outs, reds=(), tm=512, scratch=(), into=None):
    arrs, specs = [], []
    s = None
    for r in row_ins:
        arr, w, cb, d = (tuple(r) + (1,))[:4] if isinstance(r, tuple) else (r, r.shape[1], 0, 1)
        s = arr.shape[0] * d if s is None else s
        arrs.append(arr)
        specs.append((w, cb, d))
    tm = min(tm, s)
    specs = [pl.BlockSpec((tm // d, w), functools.partial(lambda i, cb: (i, cb), cb=cb)) for w, cb, d in specs]
    for b in bc_ins:
        arrs.append(b)
        specs.append(pl.BlockSpec(b.shape, lambda i: (0, 0)))
    outs = [(tuple(o) + (1,))[:3] for o in outs]
    n_in, n_out = len(arrs), len(outs)
    o0 = n_in + (0 if into is None else 1)

    def body(*refs):
        n_ref = o0 + n_out + len(reds)
        vals = fn(*[r[...] for r in refs[:n_in]], *refs[n_ref:])
        if not isinstance(vals, (tuple, list)):
            vals = (vals,)
        for r, v in zip(refs[o0:o0 + n_out], vals[:n_out]):
            r[...] = v.astype(r.dtype)
        if reds:
            red_refs = refs[o0 + n_out:n_ref]

            @pl.when(pl.program_id(0) == 0)
            def _():
                for r in red_refs:
                    r[...] = jnp.zeros_like(r)

            for r, v in zip(red_refs, vals[n_out:]):
                r[...] += v

    out_shape = [jax.ShapeDtypeStruct((s // d, c), dt) for c, dt, d in outs]
    out_shape += [jax.ShapeDtypeStruct((1, c), F32) for c in reds]
    out_specs = [pl.BlockSpec((tm // d, c), lambda i: (i, 0)) for c, _, d in outs]
    out_specs += [pl.BlockSpec((1, c), lambda i: (0, 0)) for c in reds]
    aliases = {}
    if into is not None:
        whole, k, cb = into
        out_shape[k] = jax.ShapeDtypeStruct(whole.shape, whole.dtype)
        if isinstance(cb, tuple):
            out_specs[k] = pl.BlockSpec((pl.Element(tm), pl.Element(outs[k][0])),
                                        functools.partial(lambda i, c0: (i * tm, c0), c0=cb[1]))
        else:
            out_specs[k] = pl.BlockSpec((tm, outs[k][0]), functools.partial(lambda i, cb: (i, cb), cb=cb))
        aliases = {n_in: k}
        arrs.append(whole)
        specs.append(pl.BlockSpec(memory_space=pl.ANY))
    res = pl.pallas_call(
        body, name=name, grid=(s // tm,), in_specs=specs, out_specs=out_specs, out_shape=out_shape,
        scratch_shapes=list(scratch), input_output_aliases=aliases,
        compiler_params=_params(("arbitrary",) if reds else ("parallel",)),
    )(*arrs)
    return res


def _to_class(x, scr, d):
    if d == 1:
        return x.astype(F32)
    tm, c = x.shape
    for g in range(c // 128):
        scr[g][...] = x[:, g * 128:(g + 1) * 128].astype(F32)
    return jnp.concatenate([scr[g][pl.ds(r, tm // d, stride=d), :] for r in range(d) for g in range(c // 128)],
                           axis=1)


def _from_class(x, scr, d):
    if d == 1:
        return x.astype(F32)
    n, dc = x.shape
    c = dc // d
    for r in range(d):
        for g in range(c // 128):
            scr[g][pl.ds(r, n, stride=d), :] = x[:, r * c + g * 128:r * c + (g + 1) * 128].astype(F32)
    return jnp.concatenate([scr[g][...] for g in range(c // 128)], axis=1)


def _mm(name, a, b, mode, out_dtype, tm=1024, tn=1024, tk=1024, side=None):
    if mode == "nn":
        (m, k), (_, n) = a.shape, b.shape
    elif mode == "nt":
        (m, k), (n, _) = a.shape, b.shape
    else:
        (k, m), (_, n) = a.shape, b.shape
    tm, tn, tk = min(tm, m), min(tn, n), min(tk, k)
    nk = k // tk
    grid = (m // tm, n // tn, nk)
    dn = {"nn": NN, "nt": NT, "tn": TN}[mode]
    n_si = len(side["ins"]) if side else 0
    n_so = len(side["outs"]) if side else 0
    n_acc = 1 if nk > 1 else 0

    def body(*refs):
        a_ref, b_ref = refs[:2]
        side_in, o_ref = refs[2:2 + n_si], refs[2 + n_si]
        side_out = refs[3 + n_si:3 + n_si + n_so]
        acc = refs[3 + n_si + n_so:3 + n_si + n_so + n_acc]
        side_scratch = refs[3 + n_si + n_so + n_acc:]
        step = (pl.program_id(0) * grid[1] + pl.program_id(1)) * grid[2] + pl.program_id(2)
        if side:
            @pl.when(step == 0)
            def _():
                side["start"](side_in, side_out, side_scratch)

        part = _dot(a_ref[...].astype(BF16), b_ref[...].astype(BF16), dn)
        if nk == 1:
            o_ref[...] = part.astype(o_ref.dtype)
        else:
            kk = pl.program_id(2)

            @pl.when(kk == 0)
            def _():
                acc[0][...] = part

            @pl.when(kk > 0)
            def _():
                acc[0][...] += part

            @pl.when(kk == nk - 1)
            def _():
                o_ref[...] = acc[0][...].astype(o_ref.dtype)

        if side:
            @pl.when(step == grid[0] * grid[1] * grid[2] - 1)
            def _():
                side["wait"](side_in, side_out, side_scratch)

    a_spec = (pl.BlockSpec((tk, tm), lambda i, j, kk: (kk, i)) if mode == "tn"
              else pl.BlockSpec((tm, tk), lambda i, j, kk: (i, kk)))
    b_spec = (pl.BlockSpec((tn, tk), lambda i, j, kk: (j, kk)) if mode == "nt"
              else pl.BlockSpec((tk, tn), lambda i, j, kk: (kk, j)))
    o_spec = pl.BlockSpec((tm, tn), lambda i, j, kk: (i, j))
    o_shape = jax.ShapeDtypeStruct((m, n), out_dtype)
    acc_scratch = [pltpu.VMEM((tm, tn), F32)] * n_acc
    if not side:
        return pl.pallas_call(
            body, name=name, grid=grid, in_specs=[a_spec, b_spec], out_specs=o_spec, out_shape=o_shape,
            scratch_shapes=acc_scratch, compiler_params=_params(("parallel", "parallel", "arbitrary")),
        )(a, b)
    return pl.pallas_call(
        body, name=name, grid=grid, in_specs=[a_spec, b_spec] + [HBM_SPEC] * n_si,
        out_specs=[o_spec] + [HBM_SPEC] * n_so, out_shape=[o_shape] + side["outs"],
        scratch_shapes=acc_scratch + side["scratch"],
        compiler_params=_params(("arbitrary", "arbitrary", "arbitrary")),
    )(a, b, *side["ins"])


def _rms_fwd(name, x, g):
    def fn(xv, gv):
        r = lax.rsqrt(jnp.mean(xv * xv, axis=-1, keepdims=True) + EPS)
        return (xv * r * gv,)
    return _rows(name, fn, [x], [g], [(x.shape[1], BF16)], tm=min(512, x.shape[0]))[0]


def _rms_fwd_both(name, x, g):
    s, dm = x.shape
    tm = min(512, s)

    def body(x_ref, g_ref, h_ref, ht_ref):
        xv = x_ref[...]
        hv = xv * lax.rsqrt(jnp.mean(xv * xv, axis=-1, keepdims=True) + EPS) * g_ref[...]
        h_ref[...] = hv.astype(BF16)
        ht_ref[...] = hv.T.astype(BF16)

    return pl.pallas_call(
        body, name=name, grid=(s // tm,),
        in_specs=[pl.BlockSpec((tm, dm), lambda i: (i, 0)), pl.BlockSpec((1, dm), lambda i: (0, 0))],
        out_specs=[pl.BlockSpec((tm, dm), lambda i: (i, 0)), pl.BlockSpec((dm, tm), lambda i: (0, i))],
        out_shape=[jax.ShapeDtypeStruct((s, dm), BF16), jax.ShapeDtypeStruct((dm, s), BF16)],
        compiler_params=_params(("parallel",)),
    )(x, g)


def _rope_tables(pos, inv):
    ang = pos.astype(F32) * inv
    lane = lax.broadcasted_iota(jnp.int32, ang.shape, 1)
    c = jnp.where(lane < ROT_DIM, jnp.cos(ang), 1.0)
    sn = jnp.sin(ang)
    sg = jnp.where(lane < ROT_DIM // 2, -sn, jnp.where(lane < ROT_DIM, sn, 0.0))
    return c, sg, lane


def _rope_apply(x, c, sg, lane):
    outs = []
    for h in range(x.shape[1] // HD):
        xh = x[:, h * HD:(h + 1) * HD].astype(F32)
        swap = jnp.where(lane < ROT_DIM // 2, pltpu.roll(xh, HD - ROT_DIM // 2, 1),
                         pltpu.roll(xh, ROT_DIM // 2, 1))
        outs.append(xh * c + swap * sg)
    return jnp.concatenate(outs, axis=1)


ROPE_TM = 256


def _class_scratch(tm):
    return [pltpu.VMEM((tm, 128), F32) for _ in range(A_GROUP // 128)]


def _rope_fwd(u, pos, inv):
    def fn(q, k, v, p, iv, *scr):
        c, sg, lane = _rope_tables(p, iv)
        qr, kr = _rope_apply(q, c, sg, lane), _rope_apply(k, c, sg, lane)
        outs = []
        for g, d in enumerate(DILATIONS):
            gs = slice(g * A_GROUP, (g + 1) * A_GROUP)
            outs += [_to_class(qr[:, gs], scr, d), _to_class(kr[:, gs], scr, d), _to_class(v[:, gs], scr, d)]
        return tuple(outs)

    outs = [(d * A_GROUP, BF16, d) for d in DILATIONS for _ in range(3)]
    qkv = [(u, 3 * A_GROUP, c0 // (3 * A_GROUP)) for c0 in (C_QA, C_KA, C_VA)]
    return _rows("rope_fwd", fn, qkv + [pos], [inv], outs, tm=ROPE_TM,
                 scratch=_class_scratch(ROPE_TM))


def _rope_bwd(dqs, dks, dvs, pos, inv, du):
    def fn(*args):
        grads, p, iv, scr = args[:9], args[9], args[10], args[11:]
        c, sg, lane = _rope_tables(p, iv)
        tok = [jnp.concatenate([_from_class(grads[3 * k + g], scr, d) for g, d in enumerate(DILATIONS)], axis=1)
               for k in range(3)]
        return (jnp.concatenate([_rope_apply(tok[0], c, -sg, lane), _rope_apply(tok[1], c, -sg, lane), tok[2]],
                                axis=1),)

    ins = [(a, a.shape[1], 0, d) for grp in (dqs, dks, dvs) for a, d in zip(grp, DILATIONS)]
    return _rows("rope_bwd", fn, ins + [pos], [inv], [(9 * A_GROUP, BF16)], tm=ROPE_TM,
                 scratch=_class_scratch(ROPE_TM), into=(du, 0, 0))[0]


def _lane_pack(cols, like):
    lane = lax.broadcasted_iota(jnp.int32, like, 1)
    out = jnp.zeros(like, F32)
    for h, cvec in enumerate(cols):
        out = jnp.where(lane == h, cvec, out)
    return out


def _band_specs(l, d, tq):
    nsb = tq // BAND
    nblk = l // BAND
    cur = pl.BlockSpec((tq, A_GROUP), lambda r, i: (i, r))
    prev = pl.BlockSpec((BAND, A_GROUP), lambda r, i: (jnp.maximum(i * nsb - 1, 0), r))
    nxt = pl.BlockSpec((BAND, A_GROUP), lambda r, i: (jnp.minimum((i + 1) * nsb, nblk - 1), r))
    st_cur = pl.BlockSpec((tq, HD), lambda r, i: (i, r))
    st_nxt = pl.BlockSpec((BAND, HD), lambda r, i: (jnp.minimum((i + 1) * nsb, nblk - 1), r))
    return nsb, cur, prev, nxt, st_cur, st_nxt


def _band_mask_q(i, first_tile):
    qr = lax.broadcasted_iota(jnp.int32, (BAND, 2 * BAND), 0)
    kc = lax.broadcasted_iota(jnp.int32, (BAND, 2 * BAND), 1)
    in_prev = (kc < BAND) & (kc >= qr)
    in_cur = (kc >= BAND) & (kc - BAND <= qr)
    if i == 0:
        in_prev = in_prev & jnp.logical_not(first_tile)
    return in_prev | in_cur


def _band_mask_k(j, nsb, last_tile):
    kc = lax.broadcasted_iota(jnp.int32, (BAND, 2 * BAND), 0)
    qr = lax.broadcasted_iota(jnp.int32, (BAND, 2 * BAND), 1)
    same = (qr < BAND) & (kc <= qr)
    nxt = (qr >= BAND) & (kc >= qr - BAND)
    if j == nsb - 1:
        nxt = nxt & jnp.logical_not(last_tile)
    return same | nxt


def _band_fwd(name, q, k, v, d):
    l = q.shape[0]
    tq = min(512, l)
    nsb, cur, prev, _, st_cur, _ = _band_specs(l, d, tq)
    scale = HD ** -0.5

    def body(q_ref, kc_ref, kp_ref, vc_ref, vp_ref, o_ref, lse_ref):
        first = pl.program_id(1) == 0
        for i in range(nsb):
            lses = []
            mask = _band_mask_q(i, first)
            for h in range(4):
                cs = slice(h * HD, (h + 1) * HD)
                qv = q_ref[i * BAND:(i + 1) * BAND, cs]
                if i == 0:
                    kk = jnp.concatenate([kp_ref[:, cs], kc_ref[0:BAND, cs]], axis=0)
                    vv = jnp.concatenate([vp_ref[:, cs], vc_ref[0:BAND, cs]], axis=0)
                else:
                    kk = kc_ref[(i - 1) * BAND:(i + 1) * BAND, cs]
                    vv = vc_ref[(i - 1) * BAND:(i + 1) * BAND, cs]
                s = jnp.where(mask, _dot(qv, kk, NT) * scale, NEG)
                m = jnp.max(s, axis=-1, keepdims=True)
                p = jnp.exp(s - m)
                den = jnp.sum(p, axis=-1, keepdims=True)
                o_ref[i * BAND:(i + 1) * BAND, cs] = _dot(p.astype(BF16), vv) / den
                lses.append(m + jnp.log(den))
            lse_ref[i * BAND:(i + 1) * BAND, :] = _lane_pack(lses, (BAND, HD))

    return pl.pallas_call(
        body, name=name, grid=(d, l // tq), in_specs=[cur, cur, prev, cur, prev],
        out_specs=[cur, st_cur],
        out_shape=[jax.ShapeDtypeStruct((l, d * A_GROUP), F32), jax.ShapeDtypeStruct((l, d * HD), F32)],
        compiler_params=_params(("parallel", "parallel")),
    )(q, k, k, v, v)


def _band_dq(name, q, k, v, dy, lse, delta, d):
    l = q.shape[0]
    tq = min(512, l)
    nsb, cur, prev, _, st_cur, _ = _band_specs(l, d, tq)
    scale = HD ** -0.5

    def body(q_ref, kc_ref, kp_ref, vc_ref, vp_ref, dy_ref, lse_ref, dl_ref, dq_ref):
        first = pl.program_id(1) == 0
        for i in range(nsb):
            mask = _band_mask_q(i, first)
            rs = slice(i * BAND, (i + 1) * BAND)
            for h in range(4):
                cs = slice(h * HD, (h + 1) * HD)
                if i == 0:
                    kk = jnp.concatenate([kp_ref[:, cs], kc_ref[0:BAND, cs]], axis=0)
                    vv = jnp.concatenate([vp_ref[:, cs], vc_ref[0:BAND, cs]], axis=0)
                else:
                    kk = kc_ref[(i - 1) * BAND:(i + 1) * BAND, cs]
                    vv = vc_ref[(i - 1) * BAND:(i + 1) * BAND, cs]
                s = jnp.where(mask, _dot(q_ref[rs, cs], kk, NT) * scale, NEG)
                p = jnp.exp(s - lse_ref[rs, h:h + 1])
                dp = _dot(dy_ref[rs, cs], vv, NT)
                ds = p * (dp - dl_ref[rs, h:h + 1])
                dq_ref[rs, cs] = (_dot(ds.astype(BF16), kk) * scale).astype(dq_ref.dtype)

    return pl.pallas_call(
        body, name=name, grid=(d, l // tq),
        in_specs=[cur, cur, prev, cur, prev, cur, st_cur, st_cur], out_specs=cur,
        out_shape=jax.ShapeDtypeStruct((l, d * A_GROUP), BF16),
        compiler_params=_params(("parallel", "parallel")),
    )(q, k, k, v, v, dy, lse, delta)


def _band_dkv(name, q, k, v, dy, lse, delta, d):
    l = q.shape[0]
    tq = min(512, l)
    nsb, cur, _, nxt, st_cur, st_nxt = _band_specs(l, d, tq)
    scale = HD ** -0.5
    ntile = l // tq

    def body(k_ref, v_ref, qc_ref, qn_ref, dyc_ref, dyn_ref, lc_ref, ln_ref, dc_ref, dn_ref,
             dk_ref, dv_ref):
        last = pl.program_id(1) == ntile - 1

        def win(c_ref, n_ref, j, cs):
            if j == nsb - 1:
                return jnp.concatenate([c_ref[j * BAND:(j + 1) * BAND, cs], n_ref[:, cs]], axis=0)
            return c_ref[j * BAND:(j + 2) * BAND, cs]

        allh = slice(0, HD)
        for j in range(nsb):
            mask = _band_mask_k(j, nsb, last)
            rs = slice(j * BAND, (j + 1) * BAND)
            lse_t = win(lc_ref, ln_ref, j, allh).T
            delta_t = win(dc_ref, dn_ref, j, allh).T
            for h in range(4):
                cs = slice(h * HD, (h + 1) * HD)
                qw = win(qc_ref, qn_ref, j, cs)
                dyw = win(dyc_ref, dyn_ref, j, cs)
                st = jnp.where(mask, _dot(k_ref[rs, cs], qw, NT) * scale, NEG)
                pt = jnp.exp(st - lse_t[h:h + 1, :])
                dst = pt * (_dot(v_ref[rs, cs], dyw, NT) - delta_t[h:h + 1, :])
                dv_ref[rs, cs] = _dot(pt.astype(BF16), dyw).astype(dv_ref.dtype)
                dk_ref[rs, cs] = (_dot(dst.astype(BF16), qw) * scale).astype(dk_ref.dtype)

    shp = jax.ShapeDtypeStruct((l, d * A_GROUP), BF16)
    return pl.pallas_call(
        body, name=name, grid=(d, ntile),
        in_specs=[cur, cur, cur, nxt, cur, nxt, st_cur, st_nxt, st_cur, st_nxt],
        out_specs=[cur, cur], out_shape=[shp, shp],
        compiler_params=_params(("parallel", "parallel")),
    )(k, v, q, q, dy, dy, lse, lse, delta, delta)


def _split3(x):
    hi = x.astype(BF16)
    r1 = x - hi.astype(F32)
    mid = r1.astype(BF16)
    lo = (r1 - mid.astype(F32)).astype(BF16)
    return hi, mid, lo


def _fox_prep(z, b):
    h, s = z.shape
    blk = min(512, s)

    def body(z_ref, b_ref, c_ref):
        r = lax.broadcasted_iota(jnp.int32, (blk, blk), 0)
        cidx = lax.broadcasted_iota(jnp.int32, (blk, blk), 1)
        tri = (r <= cidx).astype(BF16)
        carry = jnp.zeros((h, 1), F32)
        for t in range(s // blk):
            zz = z_ref[:, t * blk:(t + 1) * blk] + b_ref[...]
            lf = jnp.minimum(zz, 0.0) - jnp.log(1.0 + jnp.exp(-jnp.abs(zz)))
            hi, mid, lo = _split3(lf)
            cs = _dot(hi, tri) + _dot(mid, tri) + _dot(lo, tri) + carry
            c_ref[:, t * blk:(t + 1) * blk] = cs
            carry = cs[:, blk - 1:blk]

    return pl.pallas_call(body, name="fox_prep", out_shape=jax.ShapeDtypeStruct((h, s), F32))(z, b)


def _fox_prep_bwd(dc, z, b):
    h, s = z.shape
    blk = min(512, s)

    def body(dc_ref, z_ref, b_ref, dz_ref, db_ref):
        r = lax.broadcasted_iota(jnp.int32, (blk, blk), 0)
        cidx = lax.broadcasted_iota(jnp.int32, (blk, blk), 1)
        tri = (r >= cidx).astype(BF16)
        carry = jnp.zeros((h, 1), F32)
        tot = jnp.zeros((h, 1), F32)
        for t in reversed(range(s // blk)):
            hi, mid, lo = _split3(dc_ref[:, t * blk:(t + 1) * blk])
            rc = _dot(hi, tri) + _dot(mid, tri) + _dot(lo, tri) + carry
            carry = rc[:, 0:1]
            zz = z_ref[:, t * blk:(t + 1) * blk] + b_ref[...]
            dz = rc * _sig(-zz)
            dz_ref[:, t * blk:(t + 1) * blk] = dz
            tot = tot + jnp.sum(dz, axis=-1, keepdims=True)
        db_ref[...] = tot

    return pl.pallas_call(
        body, name="fox_prep_bwd",
        out_shape=[jax.ShapeDtypeStruct((h, s), F32), jax.ShapeDtypeStruct((h, 1), F32)])(dc, z, b)


FOX_W = 128
FOX_C = B_HD
FOX_ONE = B_HD + 3
FOX_SUB = 256
FOX_SUB_FWD = 128
FOX_HEADS_PER_STEP = 2


def _head_of_pair(x, hh):
    return x if hh == 0 else pltpu.roll(x, B_HD, 1)


def _fox_pack(u, c_col, t):
    s = u.shape[0]
    nt = s // t
    scale = B_HD ** -0.5

    def body(q_ref, k_ref, v_ref, c_ref, qf_ref, kb_ref, ks_ref, vb_ref, vt_ref):
        lane = lax.broadcasted_iota(jnp.int32, (t, FOX_W), 1)
        for hd in range(B_HEADS):
            pair, hh = slice(hd // 2 * FOX_W, (hd // 2 + 1) * FOX_W), hd % 2
            qv, kv, vv = [r[:, pair].astype(F32) for r in (q_ref, k_ref, v_ref)]
            qf_ref[hd] = jnp.where(lane < B_HD, _head_of_pair(qv, hh), B_HD ** 0.5).astype(BF16)
            neg = c_ref[hd] * (-scale)
            hi = neg.astype(BF16).astype(F32)
            mid = (neg - hi).astype(BF16).astype(F32)
            lo = neg - hi - mid
            aux = jnp.where(lane == FOX_C, hi,
                            jnp.where(lane == FOX_C + 1, mid, jnp.where(lane == FOX_C + 2, lo, 0.0)))
            kb = jnp.where(lane < B_HD, _head_of_pair(kv, hh) * scale, aux)
            kb_ref[hd] = kb.astype(BF16)
            ks_ref[hd] = jnp.where(lane == FOX_ONE, 1.0, kb).T.astype(BF16)
            vb = jnp.where(lane < B_HD, _head_of_pair(vv, hh), 1.0)
            vb_ref[hd] = vb.astype(BF16)
            vt_ref[hd] = vb.T.astype(BF16)

    def tok(col0):
        return pl.BlockSpec((t, B_HEADS * B_HD), functools.partial(lambda i, cb: (i, cb), cb=col0 // (B_HEADS * B_HD)))

    rows = pl.BlockSpec((B_HEADS, t, FOX_W), lambda i: (0, i, 0))
    tiles = pl.BlockSpec((B_HEADS, None, FOX_W, t), lambda i: (0, i, 0, 0))
    hm = jax.ShapeDtypeStruct((B_HEADS, s, FOX_W), BF16)
    tt = jax.ShapeDtypeStruct((B_HEADS, nt, FOX_W, t), BF16)
    return pl.pallas_call(
        body, name="fox_pack", grid=(nt,),
        in_specs=[tok(C_QB), tok(C_KB), tok(C_VB), pl.BlockSpec((B_HEADS, t, 1), lambda i: (0, i, 0))],
        out_specs=[rows, rows, tiles, rows, tiles], out_shape=[hm, hm, tt, hm, tt],
        compiler_params=_params(("parallel",)),
    )(u, u, u, c_col)


def _fox_pack_bwd(dy, y, t):
    s = dy.shape[0]
    nt = s // t

    def body(do_ref, o_ref, dow_ref, dl_ref):
        lane = lax.broadcasted_iota(jnp.int32, (t, FOX_W), 1)
        lane8 = lax.broadcasted_iota(jnp.int32, (8, FOX_W), 1)
        for pr in range(B_HEADS // 2):
            pair = slice(pr * FOX_W, (pr + 1) * FOX_W)
            dov = do_ref[:, pair].astype(F32)
            parts = _split3(dov * o_ref[:, pair].astype(F32))
            for hh in range(2):
                dow_ref[2 * pr + hh] = jnp.where(lane < B_HD, _head_of_pair(dov, hh), 0.0).astype(BF16)
                mask = ((lane8 >= hh * B_HD) & (lane8 < (hh + 1) * B_HD)).astype(BF16)
                row = _dot(mask, parts[0], NT) + _dot(mask, parts[1], NT) + _dot(mask, parts[2], NT)
                dl_ref[2 * pr + hh] = row[0:1, :]

    tok = pl.BlockSpec((t, B_HEADS * B_HD), lambda i: (i, 0))
    return pl.pallas_call(
        body, name="fox_pack_bwd", grid=(nt,), in_specs=[tok, tok],
        out_specs=[pl.BlockSpec((B_HEADS, t, FOX_W), lambda i: (0, i, 0)),
                   pl.BlockSpec((B_HEADS, None, 1, t), lambda i: (0, i, 0, 0))],
        out_shape=[jax.ShapeDtypeStruct((B_HEADS, s, FOX_W), BF16), jax.ShapeDtypeStruct((B_HEADS, nt, 1, t), F32)],
        compiler_params=_params(("parallel",)),
    )(dy, y)


def _fox_unpack(dqt, dkw, dvw, du, t):
    h, nt = dqt.shape[:2]

    def body(dq_ref, dk_ref, dv_ref, _, o_ref, dc_ref):
        lane = lax.broadcasted_iota(jnp.int32, (t, FOX_W), 1)

        def join(a0, a1):
            return jnp.where(lane < B_HD, a0, pltpu.roll(a1, B_HD, 1))

        for hh in range(h):
            dc_ref[hh] = dq_ref[hh][FOX_ONE:FOX_ONE + 1, :] - dk_ref[hh].T[B_HD:B_HD + 1, :]
        pairs = range(0, h, 2)
        cols = ([join(dq_ref[a].T, dq_ref[a + 1].T) for a in pairs] + [join(dk_ref[a], dk_ref[a + 1]) for a in pairs]
                + [join(dv_ref[a], dv_ref[a + 1]) for a in pairs])
        o_ref[...] = jnp.concatenate(cols, axis=1).astype(o_ref.dtype)

    rows = pl.BlockSpec((h, t, FOX_W), lambda i: (0, i, 0))
    return pl.pallas_call(
        body, name="fox_unpack", grid=(nt,),
        in_specs=[pl.BlockSpec((h, None, FOX_W, t), lambda i: (0, i, 0, 0)), rows, rows,
                  pl.BlockSpec(memory_space=pl.ANY)],
        out_specs=[pl.BlockSpec((pl.Element(t), pl.Element(3 * h * B_HD)), lambda i: (i * t, C_QB)),
                   pl.BlockSpec((h, None, 1, t), lambda i: (0, i, 0, 0))],
        out_shape=[jax.ShapeDtypeStruct(du.shape, du.dtype), jax.ShapeDtypeStruct((h, nt, 1, t), F32)],
        input_output_aliases={3: 0},
        compiler_params=_params(("parallel",)),
    )(dqt, dkw, dvw, du)


FOX_DEAD = -110.0


def _fox_norm2(qf, kb):
    h, s, w = qf.shape
    tm = min(2048, s)

    def body(q_ref, k_ref, qo_ref, ko_ref):
        row = lax.broadcasted_iota(jnp.int32, (w, w), 0)
        ones = (row < B_HD).astype(BF16)
        for x_ref, o_ref in ((q_ref, qo_ref), (k_ref, ko_ref)):
            xv = x_ref[...].astype(F32)
            n2 = _dot((xv * xv).astype(BF16), ones)
            o_ref[...] = jnp.broadcast_to(jnp.max(n2, axis=0, keepdims=True)[:, :1], o_ref.shape)

    spec = pl.BlockSpec((None, tm, w), lambda hh, i: (hh, i, 0))
    ospec = pl.BlockSpec((None, None, 8, 128), lambda hh, i: (hh, i, 0, 0))
    shp = jax.ShapeDtypeStruct((h, s // tm, 8, 128), F32)
    return pl.pallas_call(
        body, name="fox_norm2", grid=(h, s // tm), in_specs=[spec, spec], out_specs=[ospec, ospec],
        out_shape=[shp, shp], compiler_params=_params(("parallel", "parallel")),
    )(qf, kb)


def _fox_bounds(qf, kb, c, t):
    q2, k2 = _fox_norm2(qf, kb)
    g = 2.0 * jnp.sqrt(1.02 * jnp.max(q2[:, :, 0, 0], axis=1) * 1.02 * jnp.max(k2[:, :, 0, 0], axis=1))
    return jnp.concatenate([c[:, ::t], c[:, t - 1::t], g[:, None]], axis=1)


SMEM_SPEC = pl.BlockSpec(memory_space=pltpu.SMEM)


def _fox_fwd(qf, kb, vt4, bounds, t):
    h, s, w = qf.shape
    nt = s // t
    sub = FOX_SUB_FWD
    nsub = t // sub
    nh = FOX_HEADS_PER_STEP

    def body(b_ref, q_ref, k_ref, v_ref, o_ref, lse_ref):
        i = pl.program_id(1)
        krow = lax.broadcasted_iota(jnp.int32, (sub, t), 0)
        qcol = lax.broadcasted_iota(jnp.int32, (sub, t), 1)

        def dead_before(hh):
            head = pl.program_id(0) * nh + hh
            top = b_ref[head, 2 * nt] + b_ref[head, i]
            return lax.fori_loop(
                0, i, lambda jj, n: n + (top - b_ref[head, nt + jj] < FOX_DEAD).astype(jnp.int32), 0)

        j_lo = functools.reduce(jnp.minimum, [dead_before(hh) for hh in range(nh)])

        def tile(j, carry, diag):
            out = []
            for hh in range(nh):
                m, acc = carry[hh]
                qv, vj = q_ref[hh], v_ref[hh, j]
                los = [b * sub if diag else 0 for b in range(nsub)]
                sts = [_dot(k_ref[hh, pl.ds(pl.multiple_of(j * t + b * sub, sub), sub), :], qv[lo:, :], NT)
                       for b, lo in enumerate(los)]
                for b, lo in enumerate(los):
                    st = sts[b]
                    if diag:
                        st = jnp.where(krow[:, :t - lo] <= qcol[:, :t - lo], st, NEG)
                    m_old, acc_old = m[:, lo:], acc[:, lo:]
                    m2 = jnp.maximum(m_old, jnp.max(st, axis=0, keepdims=True))
                    p = jnp.exp(st - m2).astype(BF16)
                    acc2 = jnp.exp(m_old - m2) * acc_old + _dot(vj[:, b * sub:(b + 1) * sub], p)
                    m = m2 if lo == 0 else jnp.concatenate([m[:, :lo], m2], axis=1)
                    acc = acc2 if lo == 0 else jnp.concatenate([acc[:, :lo], acc2], axis=1)
                out.append((m, acc))
            return tuple(out)

        init = tuple((jnp.full((1, t), NEG, F32), jnp.zeros((w, t), F32)) for _ in range(nh))
        carry = lax.fori_loop(j_lo, i, lambda j, c: tile(j, c, False), init)
        outs = []
        for hh, (m, acc) in enumerate(tile(i, carry, True)):
            den = acc[B_HD:B_HD + 1, :]
            outs.append(acc[0:B_HD, :] / den)
            lse_ref[hh] = m + jnp.log(den)
        o_ref[...] = jnp.concatenate(outs, axis=0).T.astype(o_ref.dtype)

    return pl.pallas_call(
        body, name="fox_fwd", grid=(h // nh, nt),
        in_specs=[SMEM_SPEC,
                  pl.BlockSpec((nh, t, w), lambda hh, i: (hh, i, 0)),
                  pl.BlockSpec((nh, s, w), lambda hh, i: (hh, 0, 0)),
                  pl.BlockSpec((nh, nt, w, t), lambda hh, i: (hh, 0, 0, 0))],
        out_specs=[pl.BlockSpec((t, nh * B_HD), lambda hh, i: (i, hh)),
                   pl.BlockSpec((nh, 1, t), lambda hh, i: (hh, 0, i))],
        out_shape=[jax.ShapeDtypeStruct((s, h * B_HD), BF16), jax.ShapeDtypeStruct((h, 1, s), F32)],
        compiler_params=_params(("parallel", "parallel")),
    )(bounds, qf, kb, vt4)


def _fox_bwd(qf, dow, lse_row, delta_row, kb, kst4, vb, bounds, t):
    h, s, w = qf.shape
    nt = s // t
    nsub = t // FOX_SUB
    nh = FOX_HEADS_PER_STEP

    def body(b_ref, q_ref, do_ref, lse_ref, dl_ref, k_ref, kt_ref, v_ref, dqt_ref, dk_ref, dv_ref, dk_acc, dv_acc):
        j = pl.program_id(1)

        def alive_after(hh):
            head = pl.program_id(0) * nh + hh
            top = b_ref[head, 2 * nt] - b_ref[head, nt + j]
            return lax.fori_loop(
                j + 1, nt, lambda ii, n: n + (top + b_ref[head, ii] >= FOX_DEAD).astype(jnp.int32), 0)

        i_hi = j + 1 + functools.reduce(jnp.maximum, [alive_after(hh) for hh in range(nh)])

        @pl.when(j == 0)
        def _():
            dqt_ref[...] = jnp.zeros_like(dqt_ref)

        dk_acc[...] = jnp.zeros_like(dk_acc)
        dv_acc[...] = jnp.zeros_like(dv_acc)
        krow = lax.broadcasted_iota(jnp.int32, (FOX_SUB, t), 0)
        qcol = lax.broadcasted_iota(jnp.int32, (FOX_SUB, t), 1)
        subs = [slice(b * FOX_SUB, (b + 1) * FOX_SUB) for b in range(nsub)]

        def tile(i, diag):
            i0 = pl.multiple_of(i * t, t)
            for hh in range(nh):
                qi, doi = q_ref[hh, pl.ds(i0, t), :], do_ref[hh, pl.ds(i0, t), :]
                lse, dl = lse_ref[hh, i], dl_ref[hh, i]
                los = [b * FOX_SUB if diag else 0 for b in range(nsub)]
                sts = [_dot(k_ref[hh, rs, :], qi[lo:, :], NT) for rs, lo in zip(subs, los)]
                dps = [_dot(v_ref[hh, rs, :], doi[lo:, :], NT) for rs, lo in zip(subs, los)]
                dq = None
                for b, (rs, lo) in enumerate(zip(subs, los)):
                    st = sts[b] - lse[:, lo:]
                    if diag:
                        st = jnp.where(krow[:, :t - lo] <= qcol[:, :t - lo], st, NEG)
                    pt = jnp.exp(st)
                    dsb = (pt * (dps[b] - dl[:, lo:])).astype(BF16)
                    dv_acc[hh, rs, :] += _dot(pt.astype(BF16), doi[lo:, :])
                    dk_acc[hh, rs, :] += _dot(dsb, qi[lo:, :])
                    part = _dot(kt_ref[hh, :, rs], dsb)
                    if lo:
                        part = jnp.concatenate([jnp.zeros((w, lo), F32), part], axis=1)
                    dq = part if dq is None else dq + part
                dqt_ref[hh, i] += dq

        def step(i, carry):
            tile(i, False)
            return carry

        tile(j, True)
        lax.fori_loop(j + 1, i_hi, step, 0)
        dk_ref[...] = dk_acc[...] * (B_HD ** -0.5)
        dv_ref[...] = dv_acc[...]

    full = pl.BlockSpec((nh, s, w), lambda hh, j: (hh, 0, 0))
    rowst = pl.BlockSpec((nh, nt, 1, t), lambda hh, j: (hh, 0, 0, 0))
    tl = pl.BlockSpec((nh, t, w), lambda hh, j: (hh, j, 0))
    return pl.pallas_call(
        body, name="fox_bwd", grid=(h // nh, nt),
        in_specs=[SMEM_SPEC, full, full, rowst, rowst, tl,
                  pl.BlockSpec((nh, None, w, t), lambda hh, j: (hh, j, 0, 0)), tl],
        out_specs=[pl.BlockSpec((nh, nt, w, t), lambda hh, j: (hh, 0, 0, 0)), tl, tl],
        out_shape=[jax.ShapeDtypeStruct((h, nt, w, t), F32), jax.ShapeDtypeStruct((h, s, w), F32),
                   jax.ShapeDtypeStruct((h, s, w), F32)],
        scratch_shapes=[pltpu.VMEM((nh, t, w), F32), pltpu.VMEM((nh, t, w), F32)],
        compiler_params=_params(("parallel", "arbitrary")),
    )(bounds, qf, dow, lse_row, delta_row, kb, kst4, vb)


def _mem_fwd(u, mkv, tq=512):
    s = u.shape[0]
    scale = HD ** -0.5

    def body(q_ref, mk_ref, mv_ref, o_ref, lse_ref):
        lses = []
        for h in range(4):
            cs = slice(h * HD, (h + 1) * HD)
            sc = _dot(q_ref[:, cs], mk_ref[:, cs], NT) * scale
            m = jnp.max(sc, axis=-1, keepdims=True)
            p = jnp.exp(sc - m)
            den = jnp.sum(p, axis=-1, keepdims=True)
            o_ref[:, cs] = (_dot(p.astype(BF16), mv_ref[:, cs]) / den).astype(o_ref.dtype)
            lses.append(m + jnp.log(den))
        lse_ref[...] = _lane_pack(lses, (tq, HD))

    return pl.pallas_call(
        body, name="mem_fwd", grid=(s // tq,),
        in_specs=[pl.BlockSpec((tq, 512), lambda i: (i, C_QM // 512)),
                  pl.BlockSpec((N_MEM, 512), lambda i: (0, 0)),
                  pl.BlockSpec((N_MEM, 512), lambda i: (0, 1))],
        out_specs=[pl.BlockSpec((tq, 512), lambda i: (i, 0)), pl.BlockSpec((tq, HD), lambda i: (i, 0))],
        out_shape=[jax.ShapeDtypeStruct((s, 512), BF16), jax.ShapeDtypeStruct((s, HD), F32)],
        compiler_params=_params(("parallel",)),
    )(u, mkv, mkv)


def _mem_bwd(u, mkv, o, do, lse, du, tq=512):
    s = u.shape[0]
    scale = HD ** -0.5

    def body(q_ref, mk_ref, mv_ref, o_ref, do_ref, lse_ref, _, dq_ref, dmk_ref, dmv_ref):
        @pl.when(pl.program_id(0) == 0)
        def _():
            dmk_ref[...] = jnp.zeros_like(dmk_ref)
            dmv_ref[...] = jnp.zeros_like(dmv_ref)

        for h in range(4):
            cs = slice(h * HD, (h + 1) * HD)
            qv, dov = q_ref[:, cs], do_ref[:, cs]
            sc = _dot(qv, mk_ref[:, cs], NT) * scale
            p = jnp.exp(sc - lse_ref[:, h:h + 1])
            delta = jnp.sum(dov.astype(F32) * o_ref[:, cs].astype(F32), axis=-1, keepdims=True)
            ds = p * (_dot(dov, mv_ref[:, cs], NT) - delta)
            dsb = ds.astype(BF16)
            dq_ref[:, cs] = (_dot(dsb, mk_ref[:, cs]) * scale).astype(dq_ref.dtype)
            dmk_ref[:, cs] += _dot(dsb, qv, TN) * scale
            dmv_ref[:, cs] += _dot(p.astype(BF16), dov, TN)

    row = pl.BlockSpec((tq, 512), lambda i: (i, 0))
    acc = pl.BlockSpec((N_MEM, 512), lambda i: (0, 0))
    return pl.pallas_call(
        body, name="mem_bwd", grid=(s // tq,),
        in_specs=[pl.BlockSpec((tq, 512), lambda i: (i, C_QM // 512)),
                  pl.BlockSpec((N_MEM, 512), lambda i: (0, 0)),
                  pl.BlockSpec((N_MEM, 512), lambda i: (0, 1)),
                  row, row, pl.BlockSpec((tq, HD), lambda i: (i, 0)), pl.BlockSpec(memory_space=pl.ANY)],
        out_specs=[pl.BlockSpec((tq, 512), lambda i: (i, C_QM // 512)), acc, acc],
        out_shape=[jax.ShapeDtypeStruct(du.shape, du.dtype), jax.ShapeDtypeStruct((N_MEM, 512), F32),
                   jax.ShapeDtypeStruct((N_MEM, 512), F32)],
        input_output_aliases={6: 0},
        compiler_params=_params(("arbitrary",)),
    )(u, mkv, mkv, o, do, lse, du)


FB_CHIP = FB_ORIG // SHARD_COLS
FB_AT = FB_ORIG - FB_CHIP * SHARD_COLS


def _chip_slabs(main, fb):
    cuts = [SHARD_COLS * p - (B_HEADS if p > FB_CHIP else 0) for p in range(N_CHIPS + 1)]
    slabs = [main[:, a:b] for a, b in zip(cuts[:-1], cuts[1:])]
    own = slabs[FB_CHIP]
    slabs[FB_CHIP] = jnp.concatenate([own[:, :FB_AT], fb, own[:, FB_AT:]], axis=1)
    return slabs


def _split_forget(slabs):
    own = slabs[FB_CHIP]
    parts = list(slabs[:FB_CHIP]) + [own[:, :FB_AT], own[:, FB_AT + B_HEADS:]] + list(slabs[FB_CHIP + 1:])
    return jnp.concatenate(parts, axis=1), own[:, FB_AT:FB_AT + B_HEADS]


def _local_step(x, mem, pos, target, g_pre, g_post, g_mem, w_main, w_fb, b_forget, b_merge,
                w_mem_kv, w_ba, w_bb, w_bm, w_out, exchange=None):
    s = x.shape[0]
    t_fox = min(512, s)
    nt = s // t_fox
    half = ROT_DIM // 2
    inv = ROPE_THETA ** (-jnp.arange(half, dtype=F32) / half)
    inv128 = jnp.concatenate([inv, inv, jnp.zeros((HD - ROT_DIM,), F32)]).reshape(1, HD)

    h, h_t = _rms_fwd_both("norm_pre", x, g_pre)
    u = _mm("proj_in", h, w_main, "nn", BF16, tm=4096)
    ufb = _mm("proj_fb", h, w_fb, "nn", F32)
    memn = _rms_fwd("norm_mem", mem, g_mem)
    mkv = _mm("proj_mem", memn, w_mem_kv, "nn", BF16)

    qkv = _rope_fwd(u, pos, inv128)
    views = [tuple(qkv[3 * g:3 * g + 3]) for g in range(3)]
    os_, lses = [], []
    for g, d in enumerate(DILATIONS):
        o_g, lse_g = _band_fwd("band_fwd%d" % g, *views[g], d)
        os_.append((o_g, d * A_GROUP, 0, d))
        lses.append((lse_g, d * HD, 0, d))

    def merge_a(o1, o2, o3, l1, l2, l3, za, *scr):
        o1, o2, o3 = [_from_class(o, scr, d) for o, d in zip((o1, o2, o3), DILATIONS)]
        l1, l2, l3 = [_from_class(lv, scr, d) for lv, d in zip((l1, l2, l3), DILATIONS)]
        ys, tots = [], []
        for hh in range(4):
            cs, hs = slice(hh * HD, (hh + 1) * HD), slice(hh, hh + 1)
            mx = jnp.maximum(jnp.maximum(l1[:, hs], l2[:, hs]), l3[:, hs])
            e1, e2, e3 = jnp.exp(l1[:, hs] - mx), jnp.exp(l2[:, hs] - mx), jnp.exp(l3[:, hs] - mx)
            den = e1 + e2 + e3
            ys.append((e1 * o1[:, cs] + e2 * o2[:, cs] + e3 * o3[:, cs]) / den)
            tots.append(mx + jnp.log(den))
        y = jnp.concatenate(ys, axis=1)
        zf = za.astype(F32)
        tot = _lane_pack(tots, l1.shape)
        return (y, y * (zf * _sig(zf))) + tuple(_to_class(tot, scr, d) for d in DILATIONS)

    res = _rows("merge_a", merge_a, os_ + lses + [(u, 512, C_ZA // 512)], [],
                [(512, BF16), (512, BF16)] + [(d * HD, F32, d) for d in DILATIONS], tm=ROPE_TM,
                scratch=_class_scratch(ROPE_TM))
    y_a, yg_a, lse_a = res[0], res[1], res[2:5]

    zrow = ufb[:, :B_HEADS].T
    c = _fox_prep(zrow, b_forget.reshape(B_HEADS, 1))
    qf, kb, kst4, vb, vt4 = _fox_pack(u, c.reshape(B_HEADS, s, 1), t_fox)
    bounds = _fox_bounds(qf, kb, c, t_fox)
    y_b, lse_b = _fox_fwd(qf, kb, vt4, bounds, t_fox)

    y_m, lse_m = _mem_fwd(u, mkv)

    def gate(y, z):
        zf = z.astype(F32)
        return (y.astype(F32) * (zf * _sig(zf)),)

    yg_b = _rows("gate_b", gate, [y_b, (u, 512, C_ZB // 512)], [], [(512, BF16)])[0]
    yg_m = _rows("gate_m", gate, [y_m, (u, 512, C_ZM // 512)], [], [(512, BF16)])[0]

    br_a = _mm("branch_a", yg_a, w_ba, "nn", BF16)
    br_b = _mm("branch_b", yg_b, w_bb, "nn", BF16)
    br_m = _mm("branch_m", yg_m, w_bm, "nn", BF16)
    gl = [(u, 1024, C_GL // 1024 + i) for i in range(3)]
    bm3 = b_merge.reshape(3, D_MODEL)

    def merge(g0, g1, g2, b0, b1, b2, bm):
        tot = 0.0
        for i, (gv, bv) in enumerate(((g0, b0), (g1, b1), (g2, b2))):
            tot = tot + _sig(gv.astype(F32) + bm[i:i + 1, :]) * bv.astype(F32)
        return (tot,)

    merged = _rows("merge_gates", merge, gl + [br_a, br_b, br_m], [bm3], [(D_MODEL, BF16)])[0]
    out = _mm("proj_out", merged, w_out, "nn", F32)

    def tail(xv, ov, tv, gv):
        r = lax.rsqrt(jnp.mean(ov * ov, axis=-1, keepdims=True) + EPS)
        n = ov * r
        err = xv + n * gv - tv
        dy = err * (1.0 / D_MODEL)
        dn = dy * gv
        dout = r * (dn - n * jnp.mean(dn * n, axis=-1, keepdims=True))
        return (dy, dout, jnp.sum(0.5 * err * err * (1.0 / D_MODEL), axis=0, keepdims=True),
                jnp.sum(dy * n, axis=0, keepdims=True))

    dy, dout, loss_lanes, g_post_grad = _rows(
        "tail", tail, [x, out, target], [g_post], [(D_MODEL, F32), (D_MODEL, BF16)],
        reds=[D_MODEL, D_MODEL], tm=256)

    dmerged = _mm("d_merged", dout, w_out, "nt", BF16)
    gw_out = _mm("g_w_out", merged, dout, "tn", F32)

    def merge_bwd(dm, g0, g1, g2, b0, b1, b2, bm):
        dmf = dm.astype(F32)
        dbs, dgs, sums = [], [], []
        for i, (gv, bv) in enumerate(((g0, b0), (g1, b1), (g2, b2))):
            sg = _sig(gv.astype(F32) + bm[i:i + 1, :])
            dbs.append(dmf * sg)
            dg = dmf * bv.astype(F32) * sg * (1.0 - sg)
            dgs.append(dg)
            sums.append(jnp.sum(dg, axis=0, keepdims=True))
        return tuple(dbs + [jnp.concatenate(dgs, axis=1)] + sums)

    du = lax.empty(u.shape, BF16)
    res = _rows("merge_bwd", merge_bwd, [dmerged] + gl + [br_a, br_b, br_m], [bm3],
                [(D_MODEL, BF16)] * 3 + [(3 * D_MODEL, BF16)], reds=[D_MODEL] * 3, tm=256,
                into=(du, 3, ("column", C_GL)))
    dbr, du, g_bmerge = res[0:3], res[3], jnp.concatenate(res[4:7], axis=1)

    dyg, gw_branch = [], []
    for nm, dbv, wv, ygv in (("a", dbr[0], w_ba, yg_a), ("b", dbr[1], w_bb, yg_b), ("m", dbr[2], w_bm, yg_m)):
        dyg.append(_mm("d_yg_" + nm, dbv, wv, "nt", BF16))
        gw_branch.append(_mm("g_w_branch_" + nm, ygv, dbv, "tn", F32))

    def gate_bwd(dg, y, z):
        dgf, yf, zf = dg.astype(F32), y.astype(F32), z.astype(F32)
        sg = _sig(zf)
        return dgf * (zf * sg), dgf * yf * (sg * (1.0 + zf * (1.0 - sg)))

    def gate_bwd_a(dg, y, z, *scr):
        dyv, dz = gate_bwd(dg, y, z)
        prod = dyv * y.astype(F32)
        dl = [jnp.sum(prod[:, hh * HD:(hh + 1) * HD], axis=-1, keepdims=True) for hh in range(4)]
        delta = _lane_pack(dl, (dg.shape[0], HD))
        return ((dz,) + tuple(_to_class(dyv, scr, d) for d in DILATIONS)
                + tuple(_to_class(delta, scr, d) for d in DILATIONS))

    res = _rows("gate_bwd_a", gate_bwd_a, [dyg[0], y_a, (u, 512, C_ZA // 512)], [],
                [(512, BF16)] + [(d * A_GROUP, BF16, d) for d in DILATIONS] + [(d * HD, F32, d) for d in DILATIONS],
                tm=ROPE_TM, scratch=_class_scratch(ROPE_TM), into=(du, 0, C_ZA // 512))
    du, dy_a, delta_a = res[0], res[1:4], res[4:7]
    dy_b, du = _rows("gate_bwd_b", gate_bwd, [dyg[1], y_b, (u, 512, C_ZB // 512)], [],
                     [(512, BF16), (512, BF16)], into=(du, 1, C_ZB // 512))
    dy_m, du = _rows("gate_bwd_m", gate_bwd, [dyg[2], y_m, (u, 512, C_ZM // 512)], [],
                     [(512, BF16), (512, BF16)], into=(du, 1, C_ZM // 512))

    du, dmk, dmv = _mem_bwd(u, mkv, y_m, dy_m, lse_m, du)
    dmkv = jnp.concatenate([dmk, dmv], axis=1)
    gw_mem_kv = _mm("g_w_mem_kv", memn, dmkv, "tn", F32)
    dmemn = _mm("d_memn", dmkv, w_mem_kv, "nt", F32)

    def mem_gain_grad(mv, dv):
        r = lax.rsqrt(jnp.mean(mv * mv, axis=-1, keepdims=True) + EPS)
        return (jnp.sum(dv * mv * r, axis=0, keepdims=True),)

    g_mem_grad = _rows("g_norm_mem", mem_gain_grad, [mem, dmemn], [], [], reds=[D_MODEL], tm=N_MEM)[0]

    dow, delta_b = _fox_pack_bwd(dy_b, y_b, t_fox)
    dqt, dkw, dvw = _fox_bwd(qf, dow, lse_b.reshape(B_HEADS, nt, 1, t_fox), delta_b, kb, kst4, vb, bounds, t_fox)
    du, dc = _fox_unpack(dqt, dkw, dvw, du, t_fox)
    dzrow, g_bforget = _fox_prep_bwd(dc.reshape(B_HEADS, s), zrow, b_forget.reshape(B_HEADS, 1))
    dfb = jnp.zeros((s, HD), BF16).at[:, :B_HEADS].set(dzrow.T.astype(BF16))

    dqs, dks, dvs = [], [], []
    for g, d in enumerate(DILATIONS):
        qv, kv, vv = views[g]
        dqs.append(_band_dq("band_dq%d" % g, qv, kv, vv, dy_a[g], lse_a[g], delta_a[g], d))
        dk_g, dv_g = _band_dkv("band_dkv%d" % g, qv, kv, vv, dy_a[g], lse_a[g], delta_a[g], d)
        dks.append(dk_g)
        dvs.append(dv_g)
    du = _rope_bwd(dqs, dks, dvs, pos, inv128, du)

    gw_main = _mm("g_w_main", h_t, du, "nn", F32, tk=2048)
    gw_fb = _mm("g_w_fb", h, dfb, "tn", F32)
    grads = dict(norm_post_g=g_post_grad, norm_mem_g=g_mem_grad, w_in=_chip_slabs(gw_main, gw_fb[:, :B_HEADS]),
                 b_forget=g_bforget.reshape(1, B_HEADS), b_merge=g_bmerge, w_mem_kv=gw_mem_kv,
                 w_branch_a=gw_branch[0], w_branch_b=gw_branch[1], w_branch_m=gw_branch[2], w_out=gw_out)
    side = exchange(grads) if exchange else None
    dh_main = _mm("d_h", du, w_main, "nt", F32, tk=2816, side=side)
    landed = None
    if side:
        dh_main, landed = dh_main[0], dh_main[1:]
    dh_fb = _mm("d_h_fb", dfb, w_fb, "nt", F32)

    def pre_bwd(xv, d1, d2, dyv, gv):
        r = lax.rsqrt(jnp.mean(xv * xv, axis=-1, keepdims=True) + EPS)
        n = xv * r
        dhv = d1 + d2
        dn = dhv * gv
        dx = r * (dn - n * jnp.mean(dn * n, axis=-1, keepdims=True))
        return dyv + dx, jnp.sum(dhv * n, axis=0, keepdims=True)

    grad_x, g_pre_grad = _rows("norm_pre_bwd", pre_bwd, [x, dh_main, dh_fb, dy], [g_pre],
                               [(D_MODEL, F32)], reds=[D_MODEL], tm=256)

    grads["norm_pre_g"] = g_pre_grad
    return loss_lanes, grad_x, grads, landed


HBM_SPEC = pl.BlockSpec(memory_space=pltpu.HBM)


def _place():
    x, y, c = lax.axis_index("x"), lax.axis_index("y"), lax.axis_index("c")
    chips = [(1 - x, y), (x, 1 - y), (1 - x, 1 - y)]
    return x, y, c, 2 * x + y, chips


N_CHUNKS = 4


def _units(parts, row_axis):
    units = []
    for i, a in enumerate(parts):
        ch = a.shape[row_axis] // N_CHUNKS
        units += [(i, pl.ds(k * ch, ch)) for k in range(N_CHUNKS)]
    return units


def _gather_weights(parts):
    n = len(parts)
    units = _units(parts, 1)
    nu = len(units)
    via_y = [(u % N_CHUNKS) < N_CHUNKS // 2 for u in range(nu)]

    def body(*refs):
        srcs, outs = refs[:n], refs[n:2 * n]
        send_sems, recv_sems = refs[2 * n:]
        x, y, c, p, _ = _place()
        me, sib = (x, y, c), (x, y, 1 - c)
        xn, yn, dg = (1 - x, y), (x, 1 - y), (1 - x, 1 - y)

        def cp(u, k, chip, half, to, from_src=False):
            i, rs = units[u]
            dst = outs[i].at[2 * chip[0] + chip[1], half, rs]
            return pltpu.make_async_remote_copy(
                src_ref=srcs[i].at[half, rs] if from_src else dst, dst_ref=dst, send_sem=send_sems.at[u, k],
                recv_sem=recv_sems.at[u, k], device_id=to, device_id_type=MESH)

        sent = []

        def go(copy):
            copy.start()
            sent.append(copy)

        for u in range(nu):
            go(cp(u, 0, (x, y), c, (*xn, c), from_src=True))
            go(cp(u, 1, (x, y), c, (*yn, c), from_src=True))
        for u in range(nu):
            cp(u, 0, xn, c, me).wait_recv()
            go(cp(u, 4, xn, c, sib))
            if via_y[u]:
                go(cp(u, 2, xn, c, (*yn, c)))
            cp(u, 1, yn, c, me).wait_recv()
            go(cp(u, 5, yn, c, sib))
            if not via_y[u]:
                go(cp(u, 3, yn, c, (*xn, c)))
        for u in range(nu):
            cp(u, 2 if via_y[u] else 3, dg, c, me).wait_recv()
            go(cp(u, 6, dg, c, sib))
        for u in range(nu):
            for k, chip in ((4, xn), (5, yn), (6, dg)):
                cp(u, k, chip, 1 - c, me).wait_recv()
        for copy in sent:
            copy.wait_send()

    return pl.pallas_call(
        body, name="gather_weights", in_specs=[HBM_SPEC] * n, out_specs=[HBM_SPEC] * n,
        out_shape=[jax.ShapeDtypeStruct((N_CHIPS,) + a.shape, a.dtype) for a in parts],
        scratch_shapes=[pltpu.SemaphoreType.DMA((nu, 7)), pltpu.SemaphoreType.DMA((nu, 7))],
    )(*parts)


def _swap_with_sibling(parts):
    n = len(parts)
    units = _units(parts, 2)

    def body(*refs):
        srcs, outs = refs[:n], refs[n:2 * n]
        send_sems, recv_sems = refs[2 * n:]
        x, y, c, _, _ = _place()
        cps = [pltpu.make_async_remote_copy(
            src_ref=srcs[i].at[q, 1 - c, rs], dst_ref=outs[i].at[q, rs], send_sem=send_sems.at[u, q],
            recv_sem=recv_sems.at[u, q], device_id=(x, y, 1 - c), device_id_type=MESH)
            for q in range(N_CHIPS) for u, (i, rs) in enumerate(units)]
        for cpy in cps:
            cpy.start()
        for cpy in cps:
            cpy.wait()

    return pl.pallas_call(
        body, name="swap_with_sibling", in_specs=[HBM_SPEC] * n, out_specs=[HBM_SPEC] * n,
        out_shape=[jax.ShapeDtypeStruct(a.shape[:1] + a.shape[2:], a.dtype) for a in parts],
        scratch_shapes=[pltpu.SemaphoreType.DMA((len(units), N_CHIPS)),
                        pltpu.SemaphoreType.DMA((len(units), N_CHIPS))],
    )(*parts)


def _scatter_to_owners(parts):
    n = len(parts)
    units = _units(parts, 1)

    def copies(srcs, outs, send_sems, recv_sems, incoming):
        x, y, c, p, chips = _place()
        return [pltpu.make_async_remote_copy(
            src_ref=srcs[i].at[2 * cx + cy, rs], dst_ref=outs[i].at[(2 * cx + cy) if incoming else p, rs],
            send_sem=send_sems.at[u, j], recv_sem=recv_sems.at[u, j], device_id=(cx, cy, c), device_id_type=MESH)
            for u, (i, rs) in enumerate(units) for j, (cx, cy) in enumerate(chips)]

    def start(ins, outs, scratch):
        for cpy in copies(ins, outs, *scratch, incoming=False):
            cpy.start()

    def wait(ins, outs, scratch):
        for cpy in copies(ins, outs, *scratch, incoming=True):
            cpy.wait_recv()
        for cpy in copies(ins, outs, *scratch, incoming=False):
            cpy.wait_send()

    return dict(ins=list(parts), outs=[jax.ShapeDtypeStruct(a.shape, a.dtype) for a in parts],
                scratch=[pltpu.SemaphoreType.DMA((len(units), 3)), pltpu.SemaphoreType.DMA((len(units), 3))],
                start=start, wait=wait)


def _share_with_sibling(parts):
    n = len(parts)
    units = _units(parts, 1)

    def body(*refs):
        srcs, outs = refs[:n], refs[n:2 * n]
        send_sems, recv_sems = refs[2 * n:]
        x, y, c, _, _ = _place()
        sends = [pltpu.make_async_remote_copy(
            src_ref=srcs[i].at[0, rs], dst_ref=outs[i].at[c, rs], send_sem=send_sems.at[u],
            recv_sem=recv_sems.at[u], device_id=(x, y, 1 - c), device_id_type=MESH)
            for u, (i, rs) in enumerate(units)]
        for cpy in sends:
            cpy.start()
        for u, (i, rs) in enumerate(units):
            pltpu.make_async_remote_copy(
                src_ref=srcs[i].at[0, rs], dst_ref=outs[i].at[1 - c, rs], send_sem=send_sems.at[u],
                recv_sem=recv_sems.at[u], device_id=(x, y, 1 - c), device_id_type=MESH).wait_recv()
        for cpy in sends:
            cpy.wait_send()

    return pl.pallas_call(
        body, name="share_with_sibling", in_specs=[HBM_SPEC] * n, out_specs=[HBM_SPEC] * n,
        out_shape=[jax.ShapeDtypeStruct((2,) + a.shape[1:], a.dtype) for a in parts],
        scratch_shapes=[pltpu.SemaphoreType.DMA((len(units),)), pltpu.SemaphoreType.DMA((len(units),))],
    )(*parts)


def _sum_small(v):
    def body(v_ref, out_ref, buf, send_sems, recv_sems):
        x, y, c, _, _ = _place()
        me = 4 * x + 2 * y + c
        buf[me] = v_ref[...]
        flips = [(dx, dy, dc) for dx in (0, 1) for dy in (0, 1) for dc in (0, 1)][1:]
        sends = []
        for k, (dx, dy, dc) in enumerate(flips):
            cpy = pltpu.make_async_remote_copy(
                src_ref=v_ref, dst_ref=buf.at[me], send_sem=send_sems.at[k], recv_sem=recv_sems.at[k],
                device_id=((x + dx) % 2, (y + dy) % 2, (c + dc) % 2), device_id_type=MESH)
            cpy.start()
            sends.append(cpy)
        for k, (dx, dy, dc) in enumerate(flips):
            px, py, pc = (x + dx) % 2, (y + dy) % 2, (c + dc) % 2
            pltpu.make_async_remote_copy(
                src_ref=v_ref, dst_ref=buf.at[4 * px + 2 * py + pc], send_sem=send_sems.at[k],
                recv_sem=recv_sems.at[k], device_id=(px, py, pc), device_id_type=MESH).wait_recv()
        for cpy in sends:
            cpy.wait_send()
        tot = buf[0]
        for i in range(1, N_DEV):
            tot = tot + buf[i]
        out_ref[...] = tot

    return pl.pallas_call(
        body, name="sum_small", out_shape=jax.ShapeDtypeStruct(v.shape, v.dtype),
        in_specs=[pl.BlockSpec(memory_space=pltpu.VMEM)], out_specs=pl.BlockSpec(memory_space=pltpu.VMEM),
        scratch_shapes=[pltpu.VMEM((N_DEV,) + v.shape, v.dtype), pltpu.SemaphoreType.DMA((N_DEV - 1,)),
                        pltpu.SemaphoreType.DMA((N_DEV - 1,))],
    )(v)


def _add_chips(name, landed, pair, chip):
    nq, r, w = landed.shape
    tr = 64

    def body(chip_ref, *refs):
        own = refs[nq][...].astype(F32)
        tot = None
        for q in range(nq):
            term = jnp.where(chip_ref[0] == q, own, refs[q][...].astype(F32))
            tot = term if tot is None else tot + term
        refs[nq + 1][...] = tot

    specs = [pl.BlockSpec((None, tr, w), functools.partial(lambda j, chip_ref, q: (q, j, 0), q=q)) for q in range(nq)]
    specs.append(pl.BlockSpec((None, tr, w), lambda j, chip_ref: (chip_ref[0], j, 0)))
    grid_spec = pltpu.PrefetchScalarGridSpec(
        num_scalar_prefetch=1, grid=(r // tr,), in_specs=specs,
        out_specs=pl.BlockSpec((None, tr, w), lambda j, chip_ref: (0, j, 0)))
    return pl.pallas_call(
        body, name=name, grid_spec=grid_spec, out_shape=jax.ShapeDtypeStruct((1, r, w), F32),
        compiler_params=_params(("parallel",)),
    )(jnp.reshape(chip, (1,)).astype(jnp.int32), *([landed] * nq), pair)


def _add_pair(name, halves, got, c):
    nq, _, r, w = halves.shape
    tr = 64

    def body(c_ref, a_ref, b_ref, o_ref):
        o_ref[...] = (a_ref[...] + b_ref[...]).astype(o_ref.dtype)

    grid_spec = pltpu.PrefetchScalarGridSpec(
        num_scalar_prefetch=1, grid=(nq, r // tr),
        in_specs=[pl.BlockSpec((None, None, tr, w), lambda i, j, c_ref: (i, c_ref[0], j, 0)),
                  pl.BlockSpec((None, tr, w), lambda i, j, c_ref: (i, j, 0))],
        out_specs=pl.BlockSpec((None, tr, w), lambda i, j, c_ref: (i, j, 0)))
    return pl.pallas_call(
        body, name=name, grid_spec=grid_spec, out_shape=jax.ShapeDtypeStruct((nq, r, w), BF16),
        compiler_params=_params(("parallel", "parallel")),
    )(jnp.reshape(c, (1,)).astype(jnp.int32), halves, got)


def _adamw(name, w, g, m, v, tm):
    def fn(wv, gv, mv, vv):
        m2 = ADAM_B1 * mv + (1.0 - ADAM_B1) * gv
        v2 = ADAM_B2 * vv + (1.0 - ADAM_B2) * (gv * gv)
        m_hat = m2 / (1.0 - ADAM_B1 ** ADAM_STEP)
        v_hat = v2 / (1.0 - ADAM_B2 ** ADAM_STEP)
        return -ADAM_LR * (m_hat / (jnp.sqrt(v_hat) + ADAM_EPS) + ADAM_WD * wv), m2, v2
    c = w.shape[1]
    return _rows(name, fn, [w, g, m, v], [], [(c, F32)] * 3, tm=tm)


REST_ROWS = 256 + 3 * 128 + 256
REST_SPLITS = (("w_mem_kv", 0, 256), ("w_branch_a", 256, 128), ("w_branch_b", 384, 128),
               ("w_branch_m", 512, 128), ("w_out", 640, 256))


def _rest_pack(t):
    return jnp.concatenate([t[n].reshape(rows, D_MODEL) for n, _, rows in REST_SPLITS], axis=0)


def _rest_unpack(a, shapes):
    return {n: a[r0:r0 + rows].reshape(shapes[n]) for n, r0, rows in REST_SPLITS}


def _small_pack(pre, post, memg, bforget, bmerge):
    pad = jnp.zeros((1, D_MODEL - B_HEADS), F32)
    return jnp.concatenate([pre, post, memg, bmerge.reshape(3, D_MODEL),
                            jnp.concatenate([bforget, pad], axis=1), jnp.zeros((1, D_MODEL), F32)], axis=0)


def _small_unpack(s8):
    return dict(norm_pre_g=s8[0:1], norm_post_g=s8[1:2], norm_mem_g=s8[2:3],
                b_merge=s8[3:6].reshape(1, 3 * D_MODEL), b_forget=s8[6:7, :B_HEADS])


WEIGHTS = ("norm_pre_g", "norm_post_g", "norm_mem_g", "w_in", "b_forget", "b_merge", "w_mem_kv",
           "w_branch_a", "w_branch_b", "w_branch_m", "w_out")
SMALL = ("norm_pre_g", "norm_post_g", "norm_mem_g", "b_forget", "b_merge")


def kernel(x, mem, positions, norm_pre_g, norm_post_g, norm_mem_g, w_in, b_forget, b_merge, w_mem_kv, w_branch_a, w_branch_b, w_branch_m, w_out, loss_target, m_norm_pre_g, m_norm_post_g, m_norm_mem_g, m_w_in, m_b_forget, m_b_merge, m_w_mem_kv, m_w_branch_a, m_w_branch_b, m_w_branch_m, m_w_out, v_norm_pre_g, v_norm_post_g, v_norm_mem_g, v_w_in, v_b_forget, v_b_merge, v_w_mem_kv, v_w_branch_a, v_w_branch_b, v_w_branch_m, v_w_out):
    w = dict(norm_pre_g=norm_pre_g, norm_post_g=norm_post_g, norm_mem_g=norm_mem_g, w_in=w_in[0],
             b_forget=b_forget, b_merge=b_merge, w_mem_kv=w_mem_kv[0], w_branch_a=w_branch_a[0],
             w_branch_b=w_branch_b[0], w_branch_m=w_branch_m[0], w_out=w_out[0])
    mo = dict(norm_pre_g=m_norm_pre_g, norm_post_g=m_norm_post_g, norm_mem_g=m_norm_mem_g, w_in=m_w_in[0],
              b_forget=m_b_forget, b_merge=m_b_merge, w_mem_kv=m_w_mem_kv[0], w_branch_a=m_w_branch_a[0],
              w_branch_b=m_w_branch_b[0], w_branch_m=m_w_branch_m[0], w_out=m_w_out[0])
    vo = dict(norm_pre_g=v_norm_pre_g, norm_post_g=v_norm_post_g, norm_mem_g=v_norm_mem_g, w_in=v_w_in[0],
              b_forget=v_b_forget, b_merge=v_b_merge, w_mem_kv=v_w_mem_kv[0], w_branch_a=v_w_branch_a[0],
              w_branch_b=v_w_branch_b[0], w_branch_m=v_w_branch_m[0], w_out=v_w_out[0])
    s = x.shape[1]
    c = lax.axis_index("c")

    chip = 2 * lax.axis_index("x") + lax.axis_index("y")

    def put(whole, own, slot):
        return lax.dynamic_update_index_in_dim(whole, own.astype(whole.dtype), slot, 0)

    own_w = [w["w_in"].astype(BF16).reshape(2, D_MODEL // 2, SHARD_COLS),
             _rest_pack(w).astype(BF16).reshape(2, REST_ROWS // 2, D_MODEL)]
    all_in, all_rest = _gather_weights(own_w)
    all_in = all_in.reshape(N_CHIPS, D_MODEL, SHARD_COLS)
    own_in, own_rest = own_w[0].reshape(D_MODEL, SHARD_COLS), own_w[1].reshape(REST_ROWS, D_MODEL)
    w_main, w_fb = _split_forget([jnp.where(chip == p, own_in, all_in[p]) for p in range(N_CHIPS)])
    w_fb = jnp.concatenate([w_fb, jnp.zeros((D_MODEL, HD - B_HEADS), BF16)], axis=1)
    all_rest = all_rest.reshape(N_CHIPS, REST_ROWS, D_MODEL)
    all_rest = jnp.stack([jnp.where(chip == p, own_rest, all_rest[p]) for p in range(N_CHIPS)])
    w_kv_f = all_rest[:, 0:256].reshape(D_MODEL, D_MODEL)
    w_br_f = [all_rest[:, 256 + 128 * i:384 + 128 * i].reshape(N_CHIPS, 512, 256).transpose(1, 0, 2)
              .reshape(512, D_MODEL) for i in range(3)]
    w_out_f = all_rest[:, 640:896].reshape(D_MODEL, D_MODEL)

    pair = []

    def exchange(g):
        def per_chip(name, p):
            a = g[name]
            if name in ("w_mem_kv", "w_out"):
                return a[256 * p:256 * (p + 1)]
            return a[:, 256 * p:256 * (p + 1)]

        in4 = jnp.stack(g["w_in"])
        rest4 = jnp.stack([_rest_pack({n: per_chip(n, p) for n, _, _ in REST_SPLITS}) for p in range(N_CHIPS)])
        halves = [in4.reshape(N_CHIPS, 2, D_MODEL // 2, SHARD_COLS),
                  rest4.reshape(N_CHIPS, 2, REST_ROWS // 2, D_MODEL)]
        got = _swap_with_sibling(halves)
        pair.extend(_add_pair("add_pair_%d" % i, halves[i], got[i], c) for i in range(2))
        return _scatter_to_owners(pair)

    loss_lanes, grad_x, g, landed = _local_step(
        x[0], mem[0], positions.reshape(s, 1), loss_target[0], norm_pre_g, norm_post_g, norm_mem_g,
        w_main, w_fb, b_forget, b_merge, w_kv_f, w_br_f[0], w_br_f[1], w_br_f[2], w_out_f, exchange)
    loss = lax.psum(jnp.sum(loss_lanes), ("x", "y", "c"))
    half = [_add_chips("add_chips_%d" % i, landed[i], pair[i], chip) for i in range(2)]
    red_in, red_rest = [put(a, o[0], c) for a, o in zip(_share_with_sibling(half), half)]
    gs = {"w_in": red_in.reshape(D_MODEL, SHARD_COLS)}
    gs.update(_rest_unpack(red_rest.reshape(REST_ROWS, D_MODEL), {n: w[n].shape for n, _, _ in REST_SPLITS}))
    gs.update(_small_unpack(_sum_small(_small_pack(
        g["norm_pre_g"], g["norm_post_g"], g["norm_mem_g"], g["b_forget"], g["b_merge"]))))

    delta, new_m, new_v = {}, {}, {}
    for n, tm in (("w_in", 128), ("w_mem_kv", 256), ("w_branch_a", 512), ("w_branch_b", 512),
                  ("w_branch_m", 512), ("w_out", 256)):
        d_, m_, v_ = _adamw("adamw_" + n, w[n], gs[n], mo[n], vo[n], tm)
        delta[n], new_m[n], new_v[n] = d_[None], m_[None], v_[None]
        gs[n] = gs[n][None]
    packs = [_small_pack(*[t[n] for n in SMALL])
             for t in (w, gs, mo, vo)]
    for res, store in zip(_adamw("adamw_small", *packs, 8), (delta, new_m, new_v)):
        store.update(_small_unpack(res))

    return (loss, grad_x[None], *[gs[n] for n in WEIGHTS], *[delta[n] for n in WEIGHTS],
            *[new_m[n] for n in WEIGHTS], *[new_v[n] for n in WEIGHTS])
```

```python
import functools

import jax
import jax.numpy as jnp
from jax import lax
from jax.experimental import pallas as pl
from jax.experimental.pallas import tpu as pltpu

F32 = jnp.float32
BF16 = jnp.bfloat16
MESH = pl.DeviceIdType.MESH

D_MODEL = 1024
N_MEM = 256
EPS = 1e-6
NEG = -1e30
ROPE_THETA = 500000.0
ROT_DIM = 32
HD = 128
A_GROUP = 512
DILATIONS = (1, 4, 16)
BAND = 128
B_HEADS = 8
B_HD = 64
N_CHIPS = 4
N_DEV = 8

C_QA, C_KA, C_VA, C_ZA = 0, 1536, 3072, 4608
C_QB, C_KB, C_VB, C_ZB = 5120, 5632, 6144, 6656
C_QM, C_ZM, C_GL = 7168, 7680, 8192
FB_ORIG = 6656
IN_COLS = 11272
SHARD_COLS = IN_COLS // N_CHIPS

ADAM_LR, ADAM_B1, ADAM_B2, ADAM_EPS, ADAM_WD, ADAM_STEP = 0.001, 0.9, 0.999, 1e-08, 0.01, 10

VMEM_LIMIT_V7X = 56 * 1024 * 1024

NT = (((1,), (1,)), ((), ()))
NN = (((1,), (0,)), ((), ()))
TN = (((0,), (0,)), ((), ()))


def _params(sem):
    return pltpu.CompilerParams(dimension_semantics=sem, vmem_limit_bytes=VMEM_LIMIT_V7X)


def _dot(a, b, dn=NN):
    return lax.dot_general(a, b, dn, preferred_element_type=F32)


def _sig(z):
    return 1.0 / (1.0 + jnp.exp(-z))


def _rows(name, fn, row_ins, bc_ins, outs, reds=(), tm=512, scratch=(), into=None):
    arrs, specs = [], []
    s = None
    for r in row_ins:
        arr, w, cb, d = (tuple(r) + (1,))[:4] if isinstance(r, tuple) else (r, r.shape[1], 0, 1)
        s = arr.shape[0] * d if s is None else s
        arrs.append(arr)
        specs.append((w, cb, d))
    tm = min(tm, s)
    specs = [pl.BlockSpec((tm // d, w), functools.partial(lambda i, cb: (i, cb), cb=cb)) for w, cb, d in specs]
    for b in bc_ins:
        arrs.append(b)
        specs.append(pl.BlockSpec(b.shape, lambda i: (0, 0)))
    outs = [(tuple(o) + (1,))[:3] for o in outs]
    n_in, n_out = len(arrs), len(outs)
    o0 = n_in + (0 if into is None else 1)

    def body(*refs):
        n_ref = o0 + n_out + len(reds)
        vals = fn(*[r[...] for r in refs[:n_in]], *refs[n_ref:])
        if not isinstance(vals, (tuple, list)):
            vals = (vals,)
        for r, v in zip(refs[o0:o0 + n_out], vals[:n_out]):
            r[...] = v.astype(r.dtype)
        if reds:
            red_refs = refs[o0 + n_out:n_ref]

            @pl.when(pl.program_id(0) == 0)
            def _():
                for r in red_refs:
                    r[...] = jnp.zeros_like(r)

            for r, v in zip(red_refs, vals[n_out:]):
                r[...] += v

    out_shape = [jax.ShapeDtypeStruct((s // d, c), dt) for c, dt, d in outs]
    out_shape += [jax.ShapeDtypeStruct((1, c), F32) for c in reds]
    out_specs = [pl.BlockSpec((tm // d, c), lambda i: (i, 0)) for c, _, d in outs]
    out_specs += [pl.BlockSpec((1, c), lambda i: (0, 0)) for c in reds]
    aliases = {}
    if into is not None:
        whole, k, cb = into
        out_shape[k] = jax.ShapeDtypeStruct(whole.shape, whole.dtype)
        if isinstance(cb, tuple):
            out_specs[k] = pl.BlockSpec((pl.Element(tm), pl.Element(outs[k][0])),
                                        functools.partial(lambda i, c0: (i * tm, c0), c0=cb[1]))
        else:
            out_specs[k] = pl.BlockSpec((tm, outs[k][0]), functools.partial(lambda i, cb: (i, cb), cb=cb))
        aliases = {n_in: k}
        arrs.append(whole)
        specs.append(pl.BlockSpec(memory_space=pl.ANY))
    res = pl.pallas_call(
        body, name=name, grid=(s // tm,), in_specs=specs, out_specs=out_specs, out_shape=out_shape,
        scratch_shapes=list(scratch), input_output_aliases=aliases,
        compiler_params=_params(("arbitrary",) if reds else ("parallel",)),
    )(*arrs)
    return res


def _to_class(x, scr, d):
    if d == 1:
        return x.astype(F32)
    tm, c = x.shape
    for g in range(c // 128):
        scr[g][...] = x[:, g * 128:(g + 1) * 128].astype(F32)
    return jnp.concatenate([scr[g][pl.ds(r, tm // d, stride=d), :] for r in range(d) for g in range(c // 128)],
                           axis=1)


def _from_class(x, scr, d):
    if d == 1:
        return x.astype(F32)
    n, dc = x.shape
    c = dc // d
    for r in range(d):
        for g in range(c // 128):
            scr[g][pl.ds(r, n, stride=d), :] = x[:, r * c + g * 128:r * c + (g + 1) * 128].astype(F32)
    return jnp.concatenate([scr[g][...] for g in range(c // 128)], axis=1)


def _mm(name, a, b, mode, out_dtype, tm=1024, tn=1024, tk=1024, side=None):
    if mode == "nn":
        (m, k), (_, n) = a.shape, b.shape
    elif mode == "nt":
        (m, k), (n, _) = a.shape, b.shape
    else:
        (k, m), (_, n) = a.shape, b.shape
    tm, tn, tk = min(tm, m), min(tn, n), min(tk, k)
    nk = k // tk
    grid = (m // tm, n // tn, nk)
    dn = {"nn": NN, "nt": NT, "tn": TN}[mode]
    n_si = len(side["ins"]) if side else 0
    n_so = len(side["outs"]) if side else 0
    n_acc = 1 if nk > 1 else 0

    def body(*refs):
        a_ref, b_ref = refs[:2]
        side_in, o_ref = refs[2:2 + n_si], refs[2 + n_si]
        side_out = refs[3 + n_si:3 + n_si + n_so]
        acc = refs[3 + n_si + n_so:3 + n_si + n_so + n_acc]
        side_scratch = refs[3 + n_si + n_so + n_acc:]
        step = (pl.program_id(0) * grid[1] + pl.program_id(1)) * grid[2] + pl.program_id(2)
        if side:
            @pl.when(step == 0)
            def _():
                side["start"](side_in, side_out, side_scratch)

        part = _dot(a_ref[...].astype(BF16), b_ref[...].astype(BF16), dn)
        if nk == 1:
            o_ref[...] = part.astype(o_ref.dtype)
        else:
            kk = pl.program_id(2)

            @pl.when(kk == 0)
            def _():
                acc[0][...] = part

            @pl.when(kk > 0)
            def _():
                acc[0][...] += part

            @pl.when(kk == nk - 1)
            def _():
                o_ref[...] = acc[0][...].astype(o_ref.dtype)

        if side:
            @pl.when(step == grid[0] * grid[1] * grid[2] - 1)
            def _():
                side["wait"](side_in, side_out, side_scratch)

    a_spec = (pl.BlockSpec((tk, tm), lambda i, j, kk: (kk, i)) if mode == "tn"
              else pl.BlockSpec((tm, tk), lambda i, j, kk: (i, kk)))
    b_spec = (pl.BlockSpec((tn, tk), lambda i, j, kk: (j, kk)) if mode == "nt"
              else pl.BlockSpec((tk, tn), lambda i, j, kk: (kk, j)))
    o_spec = pl.BlockSpec((tm, tn), lambda i, j, kk: (i, j))
    o_shape = jax.ShapeDtypeStruct((m, n), out_dtype)
    acc_scratch = [pltpu.VMEM((tm, tn), F32)] * n_acc
    if not side:
        return pl.pallas_call(
            body, name=name, grid=grid, in_specs=[a_spec, b_spec], out_specs=o_spec, out_shape=o_shape,
            scratch_shapes=acc_scratch, compiler_params=_params(("parallel", "parallel", "arbitrary")),
        )(a, b)
    return pl.pallas_call(
        body, name=name, grid=grid, in_specs=[a_spec, b_spec] + [HBM_SPEC] * n_si,
        out_specs=[o_spec] + [HBM_SPEC] * n_so, out_shape=[o_shape] + side["outs"],
        scratch_shapes=acc_scratch + side["scratch"],
        compiler_params=_params(("arbitrary", "arbitrary", "arbitrary")),
    )(a, b, *side["ins"])


def _rms_fwd(name, x, g):
    def fn(xv, gv):
        r = lax.rsqrt(jnp.mean(xv * xv, axis=-1, keepdims=True) + EPS)
        return (xv * r * gv,)
    return _rows(name, fn, [x], [g], [(x.shape[1], BF16)], tm=min(512, x.shape[0]))[0]


def _rms_fwd_both(name, x, g):
    s, dm = x.shape
    tm = min(512, s)

    def body(x_ref, g_ref, h_ref, ht_ref):
        xv = x_ref[...]
        hv = xv * lax.rsqrt(jnp.mean(xv * xv, axis=-1, keepdims=True) + EPS) * g_ref[...]
        h_ref[...] = hv.astype(BF16)
        ht_ref[...] = hv.T.astype(BF16)

    return pl.pallas_call(
        body, name=name, grid=(s // tm,),
        in_specs=[pl.BlockSpec((tm, dm), lambda i: (i, 0)), pl.BlockSpec((1, dm), lambda i: (0, 0))],
        out_specs=[pl.BlockSpec((tm, dm), lambda i: (i, 0)), pl.BlockSpec((dm, tm), lambda i: (0, i))],
        out_shape=[jax.ShapeDtypeStruct((s, dm), BF16), jax.ShapeDtypeStruct((dm, s), BF16)],
        compiler_params=_params(("parallel",)),
    )(x, g)


def _rope_tables(pos, inv):
    ang = pos.astype(F32) * inv
    lane = lax.broadcasted_iota(jnp.int32, ang.shape, 1)
    c = jnp.where(lane < ROT_DIM, jnp.cos(ang), 1.0)
    sn = jnp.sin(ang)
    sg = jnp.where(lane < ROT_DIM // 2, -sn, jnp.where(lane < ROT_DIM, sn, 0.0))
    return c, sg, lane


def _rope_apply(x, c, sg, lane):
    outs = []
    for h in range(x.shape[1] // HD):
        xh = x[:, h * HD:(h + 1) * HD].astype(F32)
        swap = jnp.where(lane < ROT_DIM // 2, pltpu.roll(xh, HD - ROT_DIM // 2, 1),
                         pltpu.roll(xh, ROT_DIM // 2, 1))
        outs.append(xh * c + swap * sg)
    return jnp.concatenate(outs, axis=1)


ROPE_TM = 512


def _class_scratch(tm):
    return [pltpu.VMEM((tm, 128), F32) for _ in range(A_GROUP // 128)]


def _rope_fwd(u, pos, inv):
    def fn(q, k, v, p, iv, *scr):
        c, sg, lane = _rope_tables(p, iv)
        qr, kr = _rope_apply(q, c, sg, lane), _rope_apply(k, c, sg, lane)
        outs = []
        for g, d in enumerate(DILATIONS):
            gs = slice(g * A_GROUP, (g + 1) * A_GROUP)
            outs += [_to_class(qr[:, gs], scr, d), _to_class(kr[:, gs], scr, d), _to_class(v[:, gs], scr, d)]
        return tuple(outs)

    outs = [(d * A_GROUP, BF16, d) for d in DILATIONS for _ in range(3)]
    qkv = [(u, 3 * A_GROUP, c0 // (3 * A_GROUP)) for c0 in (C_QA, C_KA, C_VA)]
    return _rows("rope_fwd", fn, qkv + [pos], [inv], outs, tm=ROPE_TM,
                 scratch=_class_scratch(ROPE_TM))


def _rope_bwd(dqs, dks, dvs, pos, inv, du):
    def fn(*args):
        grads, p, iv, scr = args[:9], args[9], args[10], args[11:]
        c, sg, lane = _rope_tables(p, iv)
        tok = [jnp.concatenate([_from_class(grads[3 * k + g], scr, d) for g, d in enumerate(DILATIONS)], axis=1)
               for k in range(3)]
        return (jnp.concatenate([_rope_apply(tok[0], c, -sg, lane), _rope_apply(tok[1], c, -sg, lane), tok[2]],
                                axis=1),)

    ins = [(a, a.shape[1], 0, d) for grp in (dqs, dks, dvs) for a, d in zip(grp, DILATIONS)]
    return _rows("rope_bwd", fn, ins + [pos], [inv], [(9 * A_GROUP, BF16)], tm=ROPE_TM,
                 scratch=_class_scratch(ROPE_TM), into=(du, 0, 0))[0]


def _lane_pack(cols, like):
    lane = lax.broadcasted_iota(jnp.int32, like, 1)
    out = jnp.zeros(like, F32)
    for h, cvec in enumerate(cols):
        out = jnp.where(lane == h, cvec, out)
    return out


def _band_specs(l, d, tq):
    nsb = tq // BAND
    nblk = l // BAND
    cur = pl.BlockSpec((tq, A_GROUP), lambda r, i: (i, r))
    prev = pl.BlockSpec((BAND, A_GROUP), lambda r, i: (jnp.maximum(i * nsb - 1, 0), r))
    nxt = pl.BlockSpec((BAND, A_GROUP), lambda r, i: (jnp.minimum((i + 1) * nsb, nblk - 1), r))
    st_cur = pl.BlockSpec((tq, HD), lambda r, i: (i, r))
    st_nxt = pl.BlockSpec((BAND, HD), lambda r, i: (jnp.minimum((i + 1) * nsb, nblk - 1), r))
    return nsb, cur, prev, nxt, st_cur, st_nxt


def _band_mask_q(i, first_tile):
    qr = lax.broadcasted_iota(jnp.int32, (BAND, 2 * BAND), 0)
    kc = lax.broadcasted_iota(jnp.int32, (BAND, 2 * BAND), 1)
    in_prev = (kc < BAND) & (kc >= qr)
    in_cur = (kc >= BAND) & (kc - BAND <= qr)
    if i == 0:
        in_prev = in_prev & jnp.logical_not(first_tile)
    return in_prev | in_cur


def _band_mask_k(j, nsb, last_tile):
    kc = lax.broadcasted_iota(jnp.int32, (BAND, 2 * BAND), 0)
    qr = lax.broadcasted_iota(jnp.int32, (BAND, 2 * BAND), 1)
    same = (qr < BAND) & (kc <= qr)
    nxt = (qr >= BAND) & (kc >= qr - BAND)
    if j == nsb - 1:
        nxt = nxt & jnp.logical_not(last_tile)
    return same | nxt


def _band_fwd(name, q, k, v, d):
    l = q.shape[0]
    tq = min(512, l)
    nsb, cur, prev, _, st_cur, _ = _band_specs(l, d, tq)
    scale = HD ** -0.5

    def body(q_ref, kc_ref, kp_ref, vc_ref, vp_ref, o_ref, lse_ref):
        first = pl.program_id(1) == 0
        for i in range(nsb):
            lses = []
            mask = _band_mask_q(i, first)
            for h in range(4):
                cs = slice(h * HD, (h + 1) * HD)
                qv = q_ref[i * BAND:(i + 1) * BAND, cs]
                if i == 0:
                    kk = jnp.concatenate([kp_ref[:, cs], kc_ref[0:BAND, cs]], axis=0)
                    vv = jnp.concatenate([vp_ref[:, cs], vc_ref[0:BAND, cs]], axis=0)
                else:
                    kk = kc_ref[(i - 1) * BAND:(i + 1) * BAND, cs]
                    vv = vc_ref[(i - 1) * BAND:(i + 1) * BAND, cs]
                s = jnp.where(mask, _dot(qv, kk, NT) * scale, NEG)
                m = jnp.max(s, axis=-1, keepdims=True)
                p = jnp.exp(s - m)
                den = jnp.sum(p, axis=-1, keepdims=True)
                o_ref[i * BAND:(i + 1) * BAND, cs] = _dot(p.astype(BF16), vv) / den
                lses.append(m + jnp.log(den))
            lse_ref[i * BAND:(i + 1) * BAND, :] = _lane_pack(lses, (BAND, HD))

    return pl.pallas_call(
        body, name=name, grid=(d, l // tq), in_specs=[cur, cur, prev, cur, prev],
        out_specs=[cur, st_cur],
        out_shape=[jax.ShapeDtypeStruct((l, d * A_GROUP), F32), jax.ShapeDtypeStruct((l, d * HD), F32)],
        compiler_params=_params(("parallel", "parallel")),
    )(q, k, k, v, v)


def _band_dq(name, q, k, v, dy, lse, delta, d):
    l = q.shape[0]
    tq = min(512, l)
    nsb, cur, prev, _, st_cur, _ = _band_specs(l, d, tq)
    scale = HD ** -0.5

    def body(q_ref, kc_ref, kp_ref, vc_ref, vp_ref, dy_ref, lse_ref, dl_ref, dq_ref):
        first = pl.program_id(1) == 0
        for i in range(nsb):
            mask = _band_mask_q(i, first)
            rs = slice(i * BAND, (i + 1) * BAND)
            for h in range(4):
                cs = slice(h * HD, (h + 1) * HD)
                if i == 0:
                    kk = jnp.concatenate([kp_ref[:, cs], kc_ref[0:BAND, cs]], axis=0)
                    vv = jnp.concatenate([vp_ref[:, cs], vc_ref[0:BAND, cs]], axis=0)
                else:
                    kk = kc_ref[(i - 1) * BAND:(i + 1) * BAND, cs]
                    vv = vc_ref[(i - 1) * BAND:(i + 1) * BAND, cs]
                s = jnp.where(mask, _dot(q_ref[rs, cs], kk, NT) * scale, NEG)
                p = jnp.exp(s - lse_ref[rs, h:h + 1])
                dp = _dot(dy_ref[rs, cs], vv, NT)
                ds = p * (dp - dl_ref[rs, h:h + 1])
                dq_ref[rs, cs] = (_dot(ds.astype(BF16), kk) * scale).astype(dq_ref.dtype)

    return pl.pallas_call(
        body, name=name, grid=(d, l // tq),
        in_specs=[cur, cur, prev, cur, prev, cur, st_cur, st_cur], out_specs=cur,
        out_shape=jax.ShapeDtypeStruct((l, d * A_GROUP), BF16),
        compiler_params=_params(("parallel", "parallel")),
    )(q, k, k, v, v, dy, lse, delta)


def _band_dkv(name, q, k, v, dy, lse, delta, d):
    l = q.shape[0]
    tq = min(512, l)
    nsb, cur, _, nxt, st_cur, st_nxt = _band_specs(l, d, tq)
    scale = HD ** -0.5
    ntile = l // tq

    def body(k_ref, v_ref, qc_ref, qn_ref, dyc_ref, dyn_ref, lc_ref, ln_ref, dc_ref, dn_ref,
             dk_ref, dv_ref):
        last = pl.program_id(1) == ntile - 1

        def win(c_ref, n_ref, j, cs):
            if j == nsb - 1:
                return jnp.concatenate([c_ref[j * BAND:(j + 1) * BAND, cs], n_ref[:, cs]], axis=0)
            return c_ref[j * BAND:(j + 2) * BAND, cs]

        allh = slice(0, HD)
        for j in range(nsb):
            mask = _band_mask_k(j, nsb, last)
            rs = slice(j * BAND, (j + 1) * BAND)
            lse_t = win(lc_ref, ln_ref, j, allh).T
            delta_t = win(dc_ref, dn_ref, j, allh).T
            for h in range(4):
                cs = slice(h * HD, (h + 1) * HD)
                qw = win(qc_ref, qn_ref, j, cs)
                dyw = win(dyc_ref, dyn_ref, j, cs)
                st = jnp.where(mask, _dot(k_ref[rs, cs], qw, NT) * scale, NEG)
                pt = jnp.exp(st - lse_t[h:h + 1, :])
                dst = pt * (_dot(v_ref[rs, cs], dyw, NT) - delta_t[h:h + 1, :])
                dv_ref[rs, cs] = _dot(pt.astype(BF16), dyw).astype(dv_ref.dtype)
                dk_ref[rs, cs] = (_dot(dst.astype(BF16), qw) * scale).astype(dk_ref.dtype)

    shp = jax.ShapeDtypeStruct((l, d * A_GROUP), BF16)
    return pl.pallas_call(
        body, name=name, grid=(d, ntile),
        in_specs=[cur, cur, cur, nxt, cur, nxt, st_cur, st_nxt, st_cur, st_nxt],
        out_specs=[cur, cur], out_shape=[shp, shp],
        compiler_params=_params(("parallel", "parallel")),
    )(k, v, q, q, dy, dy, lse, lse, delta, delta)


def _split3(x):
    hi = x.astype(BF16)
    r1 = x - hi.astype(F32)
    mid = r1.astype(BF16)
    lo = (r1 - mid.astype(F32)).astype(BF16)
    return hi, mid, lo


def _fox_prep(z, b):
    h, s = z.shape
    blk = min(512, s)

    def body(z_ref, b_ref, c_ref):
        r = lax.broadcasted_iota(jnp.int32, (blk, blk), 0)
        cidx = lax.broadcasted_iota(jnp.int32, (blk, blk), 1)
        tri = (r <= cidx).astype(BF16)
        carry = jnp.zeros((h, 1), F32)
        for t in range(s // blk):
            zz = z_ref[:, t * blk:(t + 1) * blk] + b_ref[...]
            lf = jnp.minimum(zz, 0.0) - jnp.log(1.0 + jnp.exp(-jnp.abs(zz)))
            hi, mid, lo = _split3(lf)
            cs = _dot(hi, tri) + _dot(mid, tri) + _dot(lo, tri) + carry
            c_ref[:, t * blk:(t + 1) * blk] = cs
            carry = cs[:, blk - 1:blk]

    return pl.pallas_call(body, name="fox_prep", out_shape=jax.ShapeDtypeStruct((h, s), F32))(z, b)


def _fox_prep_bwd(dc, z, b):
    h, s = z.shape
    blk = min(512, s)

    def body(dc_ref, z_ref, b_ref, dz_ref, db_ref):
        r = lax.broadcasted_iota(jnp.int32, (blk, blk), 0)
        cidx = lax.broadcasted_iota(jnp.int32, (blk, blk), 1)
        tri = (r >= cidx).astype(BF16)
        carry = jnp.zeros((h, 1), F32)
        tot = jnp.zeros((h, 1), F32)
        for t in reversed(range(s // blk)):
            hi, mid, lo = _split3(dc_ref[:, t * blk:(t + 1) * blk])
            rc = _dot(hi, tri) + _dot(mid, tri) + _dot(lo, tri) + carry
            carry = rc[:, 0:1]
            zz = z_ref[:, t * blk:(t + 1) * blk] + b_ref[...]
            dz = rc * _sig(-zz)
            dz_ref[:, t * blk:(t + 1) * blk] = dz
            tot = tot + jnp.sum(dz, axis=-1, keepdims=True)
        db_ref[...] = tot

    return pl.pallas_call(
        body, name="fox_prep_bwd",
        out_shape=[jax.ShapeDtypeStruct((h, s), F32), jax.ShapeDtypeStruct((h, 1), F32)])(dc, z, b)


FOX_W = 128
FOX_C = B_HD
FOX_ONE = B_HD + 3
FOX_SUB = 256
FOX_SUB_FWD = 128
FOX_HEADS_PER_STEP = 2


def _head_of_pair(x, hh):
    return x if hh == 0 else pltpu.roll(x, B_HD, 1)


def _fox_pack(u, c_col, t):
    s = u.shape[0]
    nt = s // t
    scale = B_HD ** -0.5

    def body(q_ref, k_ref, v_ref, c_ref, qf_ref, kb_ref, ks_ref, vb_ref, vt_ref):
        lane = lax.broadcasted_iota(jnp.int32, (t, FOX_W), 1)
        for hd in range(B_HEADS):
            pair, hh = slice(hd // 2 * FOX_W, (hd // 2 + 1) * FOX_W), hd % 2
            qv, kv, vv = [r[:, pair].astype(F32) for r in (q_ref, k_ref, v_ref)]
            qf_ref[hd] = jnp.where(lane < B_HD, _head_of_pair(qv, hh), B_HD ** 0.5).astype(BF16)
            neg = c_ref[hd] * (-scale)
            hi = neg.astype(BF16).astype(F32)
            mid = (neg - hi).astype(BF16).astype(F32)
            lo = neg - hi - mid
            aux = jnp.where(lane == FOX_C, hi,
                            jnp.where(lane == FOX_C + 1, mid, jnp.where(lane == FOX_C + 2, lo, 0.0)))
            kb = jnp.where(lane < B_HD, _head_of_pair(kv, hh) * scale, aux)
            kb_ref[hd] = kb.astype(BF16)
            ks_ref[hd] = jnp.where(lane == FOX_ONE, 1.0, kb).T.astype(BF16)
            vb = jnp.where(lane < B_HD, _head_of_pair(vv, hh), 1.0)
            vb_ref[hd] = vb.astype(BF16)
            vt_ref[hd] = vb.T.astype(BF16)

    def tok(col0):
        return pl.BlockSpec((t, B_HEADS * B_HD), functools.partial(lambda i, cb: (i, cb), cb=col0 // (B_HEADS * B_HD)))

    rows = pl.BlockSpec((B_HEADS, t, FOX_W), lambda i: (0, i, 0))
    tiles = pl.BlockSpec((B_HEADS, None, FOX_W, t), lambda i: (0, i, 0, 0))
    hm = jax.ShapeDtypeStruct((B_HEADS, s, FOX_W), BF16)
    tt = jax.ShapeDtypeStruct((B_HEADS, nt, FOX_W, t), BF16)
    return pl.pallas_call(
        body, name="fox_pack", grid=(nt,),
        in_specs=[tok(C_QB), tok(C_KB), tok(C_VB), pl.BlockSpec((B_HEADS, t, 1), lambda i: (0, i, 0))],
        out_specs=[rows, rows, tiles, rows, tiles], out_shape=[hm, hm, tt, hm, tt],
        compiler_params=_params(("parallel",)),
    )(u, u, u, c_col)


def _fox_pack_bwd(dy, y, t):
    s = dy.shape[0]
    nt = s // t

    def body(do_ref, o_ref, dow_ref, dl_ref):
        lane = lax.broadcasted_iota(jnp.int32, (t, FOX_W), 1)
        lane8 = lax.broadcasted_iota(jnp.int32, (8, FOX_W), 1)
        for pr in range(B_HEADS // 2):
            pair = slice(pr * FOX_W, (pr + 1) * FOX_W)
            dov = do_ref[:, pair].astype(F32)
            parts = _split3(dov * o_ref[:, pair].astype(F32))
            for hh in range(2):
                dow_ref[2 * pr + hh] = jnp.where(lane < B_HD, _head_of_pair(dov, hh), 0.0).astype(BF16)
                mask = ((lane8 >= hh * B_HD) & (lane8 < (hh + 1) * B_HD)).astype(BF16)
                row = _dot(mask, parts[0], NT) + _dot(mask, parts[1], NT) + _dot(mask, parts[2], NT)
                dl_ref[2 * pr + hh] = row[0:1, :]

    tok = pl.BlockSpec((t, B_HEADS * B_HD), lambda i: (i, 0))
    return pl.pallas_call(
        body, name="fox_pack_bwd", grid=(nt,), in_specs=[tok, tok],
        out_specs=[pl.BlockSpec((B_HEADS, t, FOX_W), lambda i: (0, i, 0)),
                   pl.BlockSpec((B_HEADS, None, 1, t), lambda i: (0, i, 0, 0))],
        out_shape=[jax.ShapeDtypeStruct((B_HEADS, s, FOX_W), BF16), jax.ShapeDtypeStruct((B_HEADS, nt, 1, t), F32)],
        compiler_params=_params(("parallel",)),
    )(dy, y)


def _fox_unpack(dqt, dkw, dvw, du, t):
    h, nt = dqt.shape[:2]

    def body(dq_ref, dk_ref, dv_ref, _, o_ref, dc_ref):
        lane = lax.broadcasted_iota(jnp.int32, (t, FOX_W), 1)

        def join(a0, a1):
            return jnp.where(lane < B_HD, a0, pltpu.roll(a1, B_HD, 1))

        for hh in range(h):
            dc_ref[hh] = dq_ref[hh][FOX_ONE:FOX_ONE + 1, :] - dk_ref[hh].T[B_HD:B_HD + 1, :]
        pairs = range(0, h, 2)
        cols = ([join(dq_ref[a].T, dq_ref[a + 1].T) for a in pairs] + [join(dk_ref[a], dk_ref[a + 1]) for a in pairs]
                + [join(dv_ref[a], dv_ref[a + 1]) for a in pairs])
        o_ref[...] = jnp.concatenate(cols, axis=1).astype(o_ref.dtype)

    rows = pl.BlockSpec((h, t, FOX_W), lambda i: (0, i, 0))
    return pl.pallas_call(
        body, name="fox_unpack", grid=(nt,),
        in_specs=[pl.BlockSpec((h, None, FOX_W, t), lambda i: (0, i, 0, 0)), rows, rows,
                  pl.BlockSpec(memory_space=pl.ANY)],
        out_specs=[pl.BlockSpec((pl.Element(t), pl.Element(3 * h * B_HD)), lambda i: (i * t, C_QB)),
                   pl.BlockSpec((h, None, 1, t), lambda i: (0, i, 0, 0))],
        out_shape=[jax.ShapeDtypeStruct(du.shape, du.dtype), jax.ShapeDtypeStruct((h, nt, 1, t), F32)],
        input_output_aliases={3: 0},
        compiler_params=_params(("parallel",)),
    )(dqt, dkw, dvw, du)


FOX_DEAD = -110.0


def _fox_norm2(qf, kb):
    h, s, w = qf.shape
    tm = min(2048, s)

    def body(q_ref, k_ref, qo_ref, ko_ref):
        row = lax.broadcasted_iota(jnp.int32, (w, w), 0)
        ones = (row < B_HD).astype(BF16)
        for x_ref, o_ref in ((q_ref, qo_ref), (k_ref, ko_ref)):
            xv = x_ref[...].astype(F32)
            n2 = _dot((xv * xv).astype(BF16), ones)
            o_ref[...] = jnp.broadcast_to(jnp.max(n2, axis=0, keepdims=True)[:, :1], o_ref.shape)

    spec = pl.BlockSpec((None, tm, w), lambda hh, i: (hh, i, 0))
    ospec = pl.BlockSpec((None, None, 8, 128), lambda hh, i: (hh, i, 0, 0))
    shp = jax.ShapeDtypeStruct((h, s // tm, 8, 128), F32)
    return pl.pallas_call(
        body, name="fox_norm2", grid=(h, s // tm), in_specs=[spec, spec], out_specs=[ospec, ospec],
        out_shape=[shp, shp], compiler_params=_params(("parallel", "parallel")),
    )(qf, kb)


def _fox_bounds(qf, kb, c, t):
    q2, k2 = _fox_norm2(qf, kb)
    g = 2.0 * jnp.sqrt(1.02 * jnp.max(q2[:, :, 0, 0], axis=1) * 1.02 * jnp.max(k2[:, :, 0, 0], axis=1))
    return jnp.concatenate([c[:, ::t], c[:, t - 1::t], g[:, None]], axis=1)


SMEM_SPEC = pl.BlockSpec(memory_space=pltpu.SMEM)


def _fox_fwd(qf, kb, vt4, bounds, t):
    h, s, w = qf.shape
    nt = s // t
    sub = FOX_SUB_FWD
    nsub = t // sub
    nh = FOX_HEADS_PER_STEP

    def body(b_ref, q_ref, k_ref, v_ref, o_ref, lse_ref):
        i = pl.program_id(1)
        krow = lax.broadcasted_iota(jnp.int32, (sub, t), 0)
        qcol = lax.broadcasted_iota(jnp.int32, (sub, t), 1)

        def dead_before(hh):
            head = pl.program_id(0) * nh + hh
            top = b_ref[head, 2 * nt] + b_ref[head, i]
            return lax.fori_loop(
                0, i, lambda jj, n: n + (top - b_ref[head, nt + jj] < FOX_DEAD).astype(jnp.int32), 0)

        j_lo = functools.reduce(jnp.minimum, [dead_before(hh) for hh in range(nh)])

        def tile(j, carry, diag):
            out = []
            for hh in range(nh):
                m, acc = carry[hh]
                qv, vj = q_ref[hh], v_ref[hh, j]
                los = [b * sub if diag else 0 for b in range(nsub)]
                sts = [_dot(k_ref[hh, pl.ds(pl.multiple_of(j * t + b * sub, sub), sub), :], qv[lo:, :], NT)
                       for b, lo in enumerate(los)]
                for b, lo in enumerate(los):
                    st = sts[b]
                    if diag:
                        st = jnp.where(krow[:, :t - lo] <= qcol[:, :t - lo], st, NEG)
                    m_old, acc_old = m[:, lo:], acc[:, lo:]
                    m2 = jnp.maximum(m_old, jnp.max(st, axis=0, keepdims=True))
                    p = jnp.exp(st - m2).astype(BF16)
                    acc2 = jnp.exp(m_old - m2) * acc_old + _dot(vj[:, b * sub:(b + 1) * sub], p)
                    m = m2 if lo == 0 else jnp.concatenate([m[:, :lo], m2], axis=1)
                    acc = acc2 if lo == 0 else jnp.concatenate([acc[:, :lo], acc2], axis=1)
                out.append((m, acc))
            return tuple(out)

        init = tuple((jnp.full((1, t), NEG, F32), jnp.zeros((w, t), F32)) for _ in range(nh))
        carry = lax.fori_loop(j_lo, i, lambda j, c: tile(j, c, False), init)
        outs = []
        for hh, (m, acc) in enumerate(tile(i, carry, True)):
            den = acc[B_HD:B_HD + 1, :]
            outs.append(acc[0:B_HD, :] / den)
            lse_ref[hh] = m + jnp.log(den)
        o_ref[...] = jnp.concatenate(outs, axis=0).T.astype(o_ref.dtype)

    return pl.pallas_call(
        body, name="fox_fwd", grid=(h // nh, nt),
        in_specs=[SMEM_SPEC,
                  pl.BlockSpec((nh, t, w), lambda hh, i: (hh, i, 0)),
                  pl.BlockSpec((nh, s, w), lambda hh, i: (hh, 0, 0)),
                  pl.BlockSpec((nh, nt, w, t), lambda hh, i: (hh, 0, 0, 0))],
        out_specs=[pl.BlockSpec((t, nh * B_HD), lambda hh, i: (i, hh)),
                   pl.BlockSpec((nh, 1, t), lambda hh, i: (hh, 0, i))],
        out_shape=[jax.ShapeDtypeStruct((s, h * B_HD), BF16), jax.ShapeDtypeStruct((h, 1, s), F32)],
        compiler_params=_params(("parallel", "parallel")),
    )(bounds, qf, kb, vt4)


def _fox_bwd(qf, dow, lse_row, delta_row, kb, kst4, vb, bounds, t):
    h, s, w = qf.shape
    nt = s // t
    nsub = t // FOX_SUB
    nh = FOX_HEADS_PER_STEP

    def body(b_ref, q_ref, do_ref, lse_ref, dl_ref, k_ref, kt_ref, v_ref, dqt_ref, dk_ref, dv_ref, dk_acc, dv_acc):
        j = pl.program_id(1)

        def alive_after(hh):
            head = pl.program_id(0) * nh + hh
            top = b_ref[head, 2 * nt] - b_ref[head, nt + j]
            return lax.fori_loop(
                j + 1, nt, lambda ii, n: n + (top + b_ref[head, ii] >= FOX_DEAD).astype(jnp.int32), 0)

        i_hi = j + 1 + functools.reduce(jnp.maximum, [alive_after(hh) for hh in range(nh)])

        @pl.when(j == 0)
        def _():
            dqt_ref[...] = jnp.zeros_like(dqt_ref)

        dk_acc[...] = jnp.zeros_like(dk_acc)
        dv_acc[...] = jnp.zeros_like(dv_acc)
        krow = lax.broadcasted_iota(jnp.int32, (FOX_SUB, t), 0)
        qcol = lax.broadcasted_iota(jnp.int32, (FOX_SUB, t), 1)
        subs = [slice(b * FOX_SUB, (b + 1) * FOX_SUB) for b in range(nsub)]

        def tile(i, diag):
            i0 = pl.multiple_of(i * t, t)
            for hh in range(nh):
                qi, doi = q_ref[hh, pl.ds(i0, t), :], do_ref[hh, pl.ds(i0, t), :]
                lse, dl = lse_ref[hh, i], dl_ref[hh, i]
                los = [b * FOX_SUB if diag else 0 for b in range(nsub)]
                sts = [_dot(k_ref[hh, rs, :], qi[lo:, :], NT) for rs, lo in zip(subs, los)]
                dps = [_dot(v_ref[hh, rs, :], doi[lo:, :], NT) for rs, lo in zip(subs, los)]
                dq = None
                for b, (rs, lo) in enumerate(zip(subs, los)):
                    st = sts[b] - lse[:, lo:]
                    if diag:
                        st = jnp.where(krow[:, :t - lo] <= qcol[:, :t - lo], st, NEG)
                    pt = jnp.exp(st)
                    dsb = (pt * (dps[b] - dl[:, lo:])).astype(BF16)
                    dv_acc[hh, rs, :] += _dot(pt.astype(BF16), doi[lo:, :])
                    dk_acc[hh, rs, :] += _dot(dsb, qi[lo:, :])
                    part = _dot(kt_ref[hh, :, rs], dsb)
                    if lo:
                        part = jnp.concatenate([jnp.zeros((w, lo), F32), part], axis=1)
                    dq = part if dq is None else dq + part
                dqt_ref[hh, i] += dq

        def step(i, carry):
            tile(i, False)
            return carry

        tile(j, True)
        lax.fori_loop(j + 1, i_hi, step, 0)
        dk_ref[...] = dk_acc[...] * (B_HD ** -0.5)
        dv_ref[...] = dv_acc[...]

    full = pl.BlockSpec((nh, s, w), lambda hh, j: (hh, 0, 0))
    rowst = pl.BlockSpec((nh, nt, 1, t), lambda hh, j: (hh, 0, 0, 0))
    tl = pl.BlockSpec((nh, t, w), lambda hh, j: (hh, j, 0))
    return pl.pallas_call(
        body, name="fox_bwd", grid=(h // nh, nt),
        in_specs=[SMEM_SPEC, full, full, rowst, rowst, tl,
                  pl.BlockSpec((nh, None, w, t), lambda hh, j: (hh, j, 0, 0)), tl],
        out_specs=[pl.BlockSpec((nh, nt, w, t), lambda hh, j: (hh, 0, 0, 0)), tl, tl],
        out_shape=[jax.ShapeDtypeStruct((h, nt, w, t), F32), jax.ShapeDtypeStruct((h, s, w), F32),
                   jax.ShapeDtypeStruct((h, s, w), F32)],
        scratch_shapes=[pltpu.VMEM((nh, t, w), F32), pltpu.VMEM((nh, t, w), F32)],
        compiler_params=_params(("parallel", "arbitrary")),
    )(bounds, qf, dow, lse_row, delta_row, kb, kst4, vb)


def _mem_fwd(u, mkv, tq=512):
    s = u.shape[0]
    scale = HD ** -0.5

    def body(q_ref, mk_ref, mv_ref, o_ref, lse_ref):
        lses = []
        for h in range(4):
            cs = slice(h * HD, (h + 1) * HD)
            sc = _dot(q_ref[:, cs], mk_ref[:, cs], NT) * scale
            m = jnp.max(sc, axis=-1, keepdims=True)
            p = jnp.exp(sc - m)
            den = jnp.sum(p, axis=-1, keepdims=True)
            o_ref[:, cs] = (_dot(p.astype(BF16), mv_ref[:, cs]) / den).astype(o_ref.dtype)
            lses.append(m + jnp.log(den))
        lse_ref[...] = _lane_pack(lses, (tq, HD))

    return pl.pallas_call(
        body, name="mem_fwd", grid=(s // tq,),
        in_specs=[pl.BlockSpec((tq, 512), lambda i: (i, C_QM // 512)),
                  pl.BlockSpec((N_MEM, 512), lambda i: (0, 0)),
                  pl.BlockSpec((N_MEM, 512), lambda i: (0, 1))],
        out_specs=[pl.BlockSpec((tq, 512), lambda i: (i, 0)), pl.BlockSpec((tq, HD), lambda i: (i, 0))],
        out_shape=[jax.ShapeDtypeStruct((s, 512), BF16), jax.ShapeDtypeStruct((s, HD), F32)],
        compiler_params=_params(("parallel",)),
    )(u, mkv, mkv)


def _mem_bwd(u, mkv, o, do, lse, du, tq=512):
    s = u.shape[0]
    scale = HD ** -0.5

    def body(q_ref, mk_ref, mv_ref, o_ref, do_ref, lse_ref, _, dq_ref, dmk_ref, dmv_ref):
        @pl.when(pl.program_id(0) == 0)
        def _():
            dmk_ref[...] = jnp.zeros_like(dmk_ref)
            dmv_ref[...] = jnp.zeros_like(dmv_ref)

        for h in range(4):
            cs = slice(h * HD, (h + 1) * HD)
            qv, dov = q_ref[:, cs], do_ref[:, cs]
            sc = _dot(qv, mk_ref[:, cs], NT) * scale
            p = jnp.exp(sc - lse_ref[:, h:h + 1])
            delta = jnp.sum(dov.astype(F32) * o_ref[:, cs].astype(F32), axis=-1, keepdims=True)
            ds = p * (_dot(dov, mv_ref[:, cs], NT) - delta)
            dsb = ds.astype(BF16)
            dq_ref[:, cs] = (_dot(dsb, mk_ref[:, cs]) * scale).astype(dq_ref.dtype)
            dmk_ref[:, cs] += _dot(dsb, qv, TN) * scale
            dmv_ref[:, cs] += _dot(p.astype(BF16), dov, TN)

    row = pl.BlockSpec((tq, 512), lambda i: (i, 0))
    acc = pl.BlockSpec((N_MEM, 512), lambda i: (0, 0))
    return pl.pallas_call(
        body, name="mem_bwd", grid=(s // tq,),
        in_specs=[pl.BlockSpec((tq, 512), lambda i: (i, C_QM // 512)),
                  pl.BlockSpec((N_MEM, 512), lambda i: (0, 0)),
                  pl.BlockSpec((N_MEM, 512), lambda i: (0, 1)),
                  row, row, pl.BlockSpec((tq, HD), lambda i: (i, 0)), pl.BlockSpec(memory_space=pl.ANY)],
        out_specs=[pl.BlockSpec((tq, 512), lambda i: (i, C_QM // 512)), acc, acc],
        out_shape=[jax.ShapeDtypeStruct(du.shape, du.dtype), jax.ShapeDtypeStruct((N_MEM, 512), F32),
                   jax.ShapeDtypeStruct((N_MEM, 512), F32)],
        input_output_aliases={6: 0},
        compiler_params=_params(("arbitrary",)),
    )(u, mkv, mkv, o, do, lse, du)


FB_CHIP = FB_ORIG // SHARD_COLS
FB_AT = FB_ORIG - FB_CHIP * SHARD_COLS


def _chip_slabs(main, fb):
    cuts = [SHARD_COLS * p - (B_HEADS if p > FB_CHIP else 0) for p in range(N_CHIPS + 1)]
    slabs = [main[:, a:b] for a, b in zip(cuts[:-1], cuts[1:])]
    own = slabs[FB_CHIP]
    slabs[FB_CHIP] = jnp.concatenate([own[:, :FB_AT], fb, own[:, FB_AT:]], axis=1)
    return slabs


def _split_forget(slabs):
    own = slabs[FB_CHIP]
    parts = list(slabs[:FB_CHIP]) + [own[:, :FB_AT], own[:, FB_AT + B_HEADS:]] + list(slabs[FB_CHIP + 1:])
    return jnp.concatenate(parts, axis=1), own[:, FB_AT:FB_AT + B_HEADS]


def _local_step(x, mem, pos, target, g_pre, g_post, g_mem, w_main, w_fb, b_forget, b_merge,
                w_mem_kv, w_ba, w_bb, w_bm, w_out, exchange=None):
    s = x.shape[0]
    t_fox = min(512, s)
    nt = s // t_fox
    half = ROT_DIM // 2
    inv = ROPE_THETA ** (-jnp.arange(half, dtype=F32) / half)
    inv128 = jnp.concatenate([inv, inv, jnp.zeros((HD - ROT_DIM,), F32)]).reshape(1, HD)

    h, h_t = _rms_fwd_both("norm_pre", x, g_pre)
    u = _mm("proj_in", h, w_main, "nn", BF16, tm=4096)
    ufb = _mm("proj_fb", h, w_fb, "nn", F32)
    memn = _rms_fwd("norm_mem", mem, g_mem)
    mkv = _mm("proj_mem", memn, w_mem_kv, "nn", BF16)

    qkv = _rope_fwd(u, pos, inv128)
    views = [tuple(qkv[3 * g:3 * g + 3]) for g in range(3)]
    os_, lses = [], []
    for g, d in enumerate(DILATIONS):
        o_g, lse_g = _band_fwd("band_fwd%d" % g, *views[g], d)
        os_.append((o_g, d * A_GROUP, 0, d))
        lses.append((lse_g, d * HD, 0, d))

    def merge_a(o1, o2, o3, l1, l2, l3, za, *scr):
        o1, o2, o3 = [_from_class(o, scr, d) for o, d in zip((o1, o2, o3), DILATIONS)]
        l1, l2, l3 = [_from_class(lv, scr, d) for lv, d in zip((l1, l2, l3), DILATIONS)]
        ys, tots = [], []
        for hh in range(4):
            cs, hs = slice(hh * HD, (hh + 1) * HD), slice(hh, hh + 1)
            mx = jnp.maximum(jnp.maximum(l1[:, hs], l2[:, hs]), l3[:, hs])
            e1, e2, e3 = jnp.exp(l1[:, hs] - mx), jnp.exp(l2[:, hs] - mx), jnp.exp(l3[:, hs] - mx)
            den = e1 + e2 + e3
            ys.append((e1 * o1[:, cs] + e2 * o2[:, cs] + e3 * o3[:, cs]) / den)
            tots.append(mx + jnp.log(den))
        y = jnp.concatenate(ys, axis=1)
        zf = za.astype(F32)
        tot = _lane_pack(tots, l1.shape)
        return (y, y * (zf * _sig(zf))) + tuple(_to_class(tot, scr, d) for d in DILATIONS)

    res = _rows("merge_a", merge_a, os_ + lses + [(u, 512, C_ZA // 512)], [],
                [(512, BF16), (512, BF16)] + [(d * HD, F32, d) for d in DILATIONS], tm=ROPE_TM,
                scratch=_class_scratch(ROPE_TM))
    y_a, yg_a, lse_a = res[0], res[1], res[2:5]

    zrow = ufb[:, :B_HEADS].T
    c = _fox_prep(zrow, b_forget.reshape(B_HEADS, 1))
    qf, kb, kst4, vb, vt4 = _fox_pack(u, c.reshape(B_HEADS, s, 1), t_fox)
    bounds = _fox_bounds(qf, kb, c, t_fox)
    y_b, lse_b = _fox_fwd(qf, kb, vt4, bounds, t_fox)

    y_m, lse_m = _mem_fwd(u, mkv)

    def gate(y, z):
        zf = z.astype(F32)
        return (y.astype(F32) * (zf * _sig(zf)),)

    yg_b = _rows("gate_b", gate, [y_b, (u, 512, C_ZB // 512)], [], [(512, BF16)])[0]
    yg_m = _rows("gate_m", gate, [y_m, (u, 512, C_ZM // 512)], [], [(512, BF16)])[0]

    br_a = _mm("branch_a", yg_a, w_ba, "nn", BF16)
    br_b = _mm("branch_b", yg_b, w_bb, "nn", BF16)
    br_m = _mm("branch_m", yg_m, w_bm, "nn", BF16)
    gl = [(u, 1024, C_GL // 1024 + i) for i in range(3)]
    bm3 = b_merge.reshape(3, D_MODEL)

    def merge(g0, g1, g2, b0, b1, b2, bm):
        tot = 0.0
        for i, (gv, bv) in enumerate(((g0, b0), (g1, b1), (g2, b2))):
            tot = tot + _sig(gv.astype(F32) + bm[i:i + 1, :]) * bv.astype(F32)
        return (tot,)

    merged = _rows("merge_gates", merge, gl + [br_a, br_b, br_m], [bm3], [(D_MODEL, BF16)])[0]
    out = _mm("proj_out", merged, w_out, "nn", F32)

    def tail(xv, ov, tv, gv):
        r = lax.rsqrt(jnp.mean(ov * ov, axis=-1, keepdims=True) + EPS)
        n = ov * r
        err = xv + n * gv - tv
        dy = err * (1.0 / D_MODEL)
        dn = dy * gv
        dout = r * (dn - n * jnp.mean(dn * n, axis=-1, keepdims=True))
        return (dy, dout, jnp.sum(0.5 * err * err * (1.0 / D_MODEL), axis=0, keepdims=True),
                jnp.sum(dy * n, axis=0, keepdims=True))

    dy, dout, loss_lanes, g_post_grad = _rows(
        "tail", tail, [x, out, target], [g_post], [(D_MODEL, F32), (D_MODEL, BF16)],
        reds=[D_MODEL, D_MODEL], tm=512)

    dmerged = _mm("d_merged", dout, w_out, "nt", BF16)
    gw_out = _mm("g_w_out", merged, dout, "tn", F32)

    def merge_bwd(dm, g0, g1, g2, b0, b1, b2, bm):
        dmf = dm.astype(F32)
        dbs, dgs, sums = [], [], []
        for i, (gv, bv) in enumerate(((g0, b0), (g1, b1), (g2, b2))):
            sg = _sig(gv.astype(F32) + bm[i:i + 1, :])
            dbs.append(dmf * sg)
            dg = dmf * bv.astype(F32) * sg * (1.0 - sg)
            dgs.append(dg)
            sums.append(jnp.sum(dg, axis=0, keepdims=True))
        return tuple(dbs + [jnp.concatenate(dgs, axis=1)] + sums)

    du = lax.empty(u.shape, BF16)
    res = _rows("merge_bwd", merge_bwd, [dmerged] + gl + [br_a, br_b, br_m], [bm3],
                [(D_MODEL, BF16)] * 3 + [(3 * D_MODEL, BF16)], reds=[D_MODEL] * 3, tm=512,
                into=(du, 3, ("column", C_GL)))
    dbr, du, g_bmerge = res[0:3], res[3], jnp.concatenate(res[4:7], axis=1)

    dyg, gw_branch = [], []
    for nm, dbv, wv, ygv in (("a", dbr[0], w_ba, yg_a), ("b", dbr[1], w_bb, yg_b), ("m", dbr[2], w_bm, yg_m)):
        dyg.append(_mm("d_yg_" + nm, dbv, wv, "nt", BF16))
        gw_branch.append(_mm("g_w_branch_" + nm, ygv, dbv, "tn", F32))

    def gate_bwd(dg, y, z):
        dgf, yf, zf = dg.astype(F32), y.astype(F32), z.astype(F32)
        sg = _sig(zf)
        return dgf * (zf * sg), dgf * yf * (sg * (1.0 + zf * (1.0 - sg)))

    def gate_bwd_a(dg, y, z, *scr):
        dyv, dz = gate_bwd(dg, y, z)
        prod = dyv * y.astype(F32)
        dl = [jnp.sum(prod[:, hh * HD:(hh + 1) * HD], axis=-1, keepdims=True) for hh in range(4)]
        delta = _lane_pack(dl, (dg.shape[0], HD))
        return ((dz,) + tuple(_to_class(dyv, scr, d) for d in DILATIONS)
                + tuple(_to_class(delta, scr, d) for d in DILATIONS))

    res = _rows("gate_bwd_a", gate_bwd_a, [dyg[0], y_a, (u, 512, C_ZA // 512)], [],
                [(512, BF16)] + [(d * A_GROUP, BF16, d) for d in DILATIONS] + [(d * HD, F32, d) for d in DILATIONS],
                tm=ROPE_TM, scratch=_class_scratch(ROPE_TM), into=(du, 0, C_ZA // 512))
    du, dy_a, delta_a = res[0], res[1:4], res[4:7]
    dy_b, du = _rows("gate_bwd_b", gate_bwd, [dyg[1], y_b, (u, 512, C_ZB // 512)], [],
                     [(512, BF16), (512, BF16)], into=(du, 1, C_ZB // 512))
    dy_m, du = _rows("gate_bwd_m", gate_bwd, [dyg[2], y_m, (u, 512, C_ZM // 512)], [],
                     [(512, BF16), (512, BF16)], into=(du, 1, C_ZM // 512))

    du, dmk, dmv = _mem_bwd(u, mkv, y_m, dy_m, lse_m, du)
    dmkv = jnp.concatenate([dmk, dmv], axis=1)
    gw_mem_kv = _mm("g_w_mem_kv", memn, dmkv, "tn", F32)
    dmemn = _mm("d_memn", dmkv, w_mem_kv, "nt", F32)

    def mem_gain_grad(mv, dv):
        r = lax.rsqrt(jnp.mean(mv * mv, axis=-1, keepdims=True) + EPS)
        return (jnp.sum(dv * mv * r, axis=0, keepdims=True),)

    g_mem_grad = _rows("g_norm_mem", mem_gain_grad, [mem, dmemn], [], [], reds=[D_MODEL], tm=N_MEM)[0]

    dow, delta_b = _fox_pack_bwd(dy_b, y_b, t_fox)
    dqt, dkw, dvw = _fox_bwd(qf, dow, lse_b.reshape(B_HEADS, nt, 1, t_fox), delta_b, kb, kst4, vb, bounds, t_fox)
    du, dc = _fox_unpack(dqt, dkw, dvw, du, t_fox)
    dzrow, g_bforget = _fox_prep_bwd(dc.reshape(B_HEADS, s), zrow, b_forget.reshape(B_HEADS, 1))
    dfb = jnp.zeros((s, HD), BF16).at[:, :B_HEADS].set(dzrow.T.astype(BF16))

    dqs, dks, dvs = [], [], []
    for g, d in enumerate(DILATIONS):
        qv, kv, vv = views[g]
        dqs.append(_band_dq("band_dq%d" % g, qv, kv, vv, dy_a[g], lse_a[g], delta_a[g], d))
        dk_g, dv_g = _band_dkv("band_dkv%d" % g, qv, kv, vv, dy_a[g], lse_a[g], delta_a[g], d)
        dks.append(dk_g)
        dvs.append(dv_g)
    du = _rope_bwd(dqs, dks, dvs, pos, inv128, du)

    gw_main = _mm("g_w_main", h_t, du, "nn", F32, tk=2048)
    gw_fb = _mm("g_w_fb", h, dfb, "tn", F32)
    grads = dict(norm_post_g=g_post_grad, norm_mem_g=g_mem_grad, w_in=_chip_slabs(gw_main, gw_fb[:, :B_HEADS]),
                 b_forget=g_bforget.reshape(1, B_HEADS), b_merge=g_bmerge, w_mem_kv=gw_mem_kv,
                 w_branch_a=gw_branch[0], w_branch_b=gw_branch[1], w_branch_m=gw_branch[2], w_out=gw_out)
    side = exchange(grads) if exchange else None
    dh_main = _mm("d_h", du, w_main, "nt", F32, tk=2816, side=side)
    landed = None
    if side:
        dh_main, landed = dh_main[0], dh_main[1:]
    dh_fb = _mm("d_h_fb", dfb, w_fb, "nt", F32)

    def pre_bwd(xv, d1, d2, dyv, gv):
        r = lax.rsqrt(jnp.mean(xv * xv, axis=-1, keepdims=True) + EPS)
        n = xv * r
        dhv = d1 + d2
        dn = dhv * gv
        dx = r * (dn - n * jnp.mean(dn * n, axis=-1, keepdims=True))
        return dyv + dx, jnp.sum(dhv * n, axis=0, keepdims=True)

    grad_x, g_pre_grad = _rows("norm_pre_bwd", pre_bwd, [x, dh_main, dh_fb, dy], [g_pre],
                               [(D_MODEL, F32)], reds=[D_MODEL], tm=512)

    grads["norm_pre_g"] = g_pre_grad
    return loss_lanes, grad_x, grads, landed


HBM_SPEC = pl.BlockSpec(memory_space=pltpu.HBM)


def _place():
    x, y, c = lax.axis_index("x"), lax.axis_index("y"), lax.axis_index("c")
    chips = [(1 - x, y), (x, 1 - y), (1 - x, 1 - y)]
    return x, y, c, 2 * x + y, chips


N_CHUNKS = 4


def _units(parts, row_axis):
    units = []
    for i, a in enumerate(parts):
        ch = a.shape[row_axis] // N_CHUNKS
        units += [(i, pl.ds(k * ch, ch)) for k in range(N_CHUNKS)]
    return units


def _gather_weights(parts):
    n = len(parts)
    units = _units(parts, 1)
    nu = len(units)
    via_y = [(u % N_CHUNKS) < N_CHUNKS // 2 for u in range(nu)]

    def body(*refs):
        srcs, outs = refs[:n], refs[n:2 * n]
        send_sems, recv_sems = refs[2 * n:]
        x, y, c, p, _ = _place()
        me, sib = (x, y, c), (x, y, 1 - c)
        xn, yn, dg = (1 - x, y), (x, 1 - y), (1 - x, 1 - y)

        def cp(u, k, chip, half, to, from_src=False):
            i, rs = units[u]
            dst = outs[i].at[2 * chip[0] + chip[1], half, rs]
            return pltpu.make_async_remote_copy(
                src_ref=srcs[i].at[half, rs] if from_src else dst, dst_ref=dst, send_sem=send_sems.at[u, k],
                recv_sem=recv_sems.at[u, k], device_id=to, device_id_type=MESH)

        sent = []

        def go(copy):
            copy.start()
            sent.append(copy)

        for u in range(nu):
            go(cp(u, 0, (x, y), c, (*xn, c), from_src=True))
            go(cp(u, 1, (x, y), c, (*yn, c), from_src=True))
        for u in range(nu):
            cp(u, 0, xn, c, me).wait_recv()
            go(cp(u, 4, xn, c, sib))
            if via_y[u]:
                go(cp(u, 2, xn, c, (*yn, c)))
            cp(u, 1, yn, c, me).wait_recv()
            go(cp(u, 5, yn, c, sib))
            if not via_y[u]:
                go(cp(u, 3, yn, c, (*xn, c)))
        for u in range(nu):
            cp(u, 2 if via_y[u] else 3, dg, c, me).wait_recv()
            go(cp(u, 6, dg, c, sib))
        for u in range(nu):
            for k, chip in ((4, xn), (5, yn), (6, dg)):
                cp(u, k, chip, 1 - c, me).wait_recv()
        for copy in sent:
            copy.wait_send()

    return pl.pallas_call(
        body, name="gather_weights", in_specs=[HBM_SPEC] * n, out_specs=[HBM_SPEC] * n,
        out_shape=[jax.ShapeDtypeStruct((N_CHIPS,) + a.shape, a.dtype) for a in parts],
        scratch_shapes=[pltpu.SemaphoreType.DMA((nu, 7)), pltpu.SemaphoreType.DMA((nu, 7))],
    )(*parts)


def _swap_with_sibling(parts):
    n = len(parts)
    units = _units(parts, 2)

    def body(*refs):
        srcs, outs = refs[:n], refs[n:2 * n]
        send_sems, recv_sems = refs[2 * n:]
        x, y, c, _, _ = _place()
        cps = [pltpu.make_async_remote_copy(
            src_ref=srcs[i].at[q, 1 - c, rs], dst_ref=outs[i].at[q, rs], send_sem=send_sems.at[u, q],
            recv_sem=recv_sems.at[u, q], device_id=(x, y, 1 - c), device_id_type=MESH)
            for q in range(N_CHIPS) for u, (i, rs) in enumerate(units)]
        for cpy in cps:
            cpy.start()
        for cpy in cps:
            cpy.wait()

    return pl.pallas_call(
        body, name="swap_with_sibling", in_specs=[HBM_SPEC] * n, out_specs=[HBM_SPEC] * n,
        out_shape=[jax.ShapeDtypeStruct(a.shape[:1] + a.shape[2:], a.dtype) for a in parts],
        scratch_shapes=[pltpu.SemaphoreType.DMA((len(units), N_CHIPS)),
                        pltpu.SemaphoreType.DMA((len(units), N_CHIPS))],
    )(*parts)


def _scatter_to_owners(parts):
    n = len(parts)
    units = _units(parts, 1)

    def copies(srcs, outs, send_sems, recv_sems, incoming):
        x, y, c, p, chips = _place()
        return [pltpu.make_async_remote_copy(
            src_ref=srcs[i].at[2 * cx + cy, rs], dst_ref=outs[i].at[(2 * cx + cy) if incoming else p, rs],
            send_sem=send_sems.at[u, j], recv_sem=recv_sems.at[u, j], device_id=(cx, cy, c), device_id_type=MESH)
            for u, (i, rs) in enumerate(units) for j, (cx, cy) in enumerate(chips)]

    def start(ins, outs, scratch):
        for cpy in copies(ins, outs, *scratch, incoming=False):
            cpy.start()

    def wait(ins, outs, scratch):
        for cpy in copies(ins, outs, *scratch, incoming=True):
            cpy.wait_recv()
        for cpy in copies(ins, outs, *scratch, incoming=False):
            cpy.wait_send()

    return dict(ins=list(parts), outs=[jax.ShapeDtypeStruct(a.shape, a.dtype) for a in parts],
                scratch=[pltpu.SemaphoreType.DMA((len(units), 3)), pltpu.SemaphoreType.DMA((len(units), 3))],
                start=start, wait=wait)


def _share_with_sibling(parts):
    n = len(parts)
    units = _units(parts, 1)

    def body(*refs):
        srcs, outs = refs[:n], refs[n:2 * n]
        send_sems, recv_sems = refs[2 * n:]
        x, y, c, _, _ = _place()
        sends = [pltpu.make_async_remote_copy(
            src_ref=srcs[i].at[0, rs], dst_ref=outs[i].at[c, rs], send_sem=send_sems.at[u],
            recv_sem=recv_sems.at[u], device_id=(x, y, 1 - c), device_id_type=MESH)
            for u, (i, rs) in enumerate(units)]
        for cpy in sends:
            cpy.start()
        for u, (i, rs) in enumerate(units):
            pltpu.make_async_remote_copy(
                src_ref=srcs[i].at[0, rs], dst_ref=outs[i].at[1 - c, rs], send_sem=send_sems.at[u],
                recv_sem=recv_sems.at[u], device_id=(x, y, 1 - c), device_id_type=MESH).wait_recv()
        for cpy in sends:
            cpy.wait_send()

    return pl.pallas_call(
        body, name="share_with_sibling", in_specs=[HBM_SPEC] * n, out_specs=[HBM_SPEC] * n,
        out_shape=[jax.ShapeDtypeStruct((2,) + a.shape[1:], a.dtype) for a in parts],
        scratch_shapes=[pltpu.SemaphoreType.DMA((len(units),)), pltpu.SemaphoreType.DMA((len(units),))],
    )(*parts)


def _sum_small(v):
    def body(v_ref, out_ref, buf, send_sems, recv_sems):
        x, y, c, _, _ = _place()
        me = 4 * x + 2 * y + c
        buf[me] = v_ref[...]
        flips = [(dx, dy, dc) for dx in (0, 1) for dy in (0, 1) for dc in (0, 1)][1:]
        sends = []
        for k, (dx, dy, dc) in enumerate(flips):
            cpy = pltpu.make_async_remote_copy(
                src_ref=v_ref, dst_ref=buf.at[me], send_sem=send_sems.at[k], recv_sem=recv_sems.at[k],
                device_id=((x + dx) % 2, (y + dy) % 2, (c + dc) % 2), device_id_type=MESH)
            cpy.start()
            sends.append(cpy)
        for k, (dx, dy, dc) in enumerate(flips):
            px, py, pc = (x + dx) % 2, (y + dy) % 2, (c + dc) % 2
            pltpu.make_async_remote_copy(
                src_ref=v_ref, dst_ref=buf.at[4 * px + 2 * py + pc], send_sem=send_sems.at[k],
                recv_sem=recv_sems.at[k], device_id=(px, py, pc), device_id_type=MESH).wait_recv()
        for cpy in sends:
            cpy.wait_send()
        tot = buf[0]
        for i in range(1, N_DEV):
            tot = tot + buf[i]
        out_ref[...] = tot

    return pl.pallas_call(
        body, name="sum_small", out_shape=jax.ShapeDtypeStruct(v.shape, v.dtype),
        in_specs=[pl.BlockSpec(memory_space=pltpu.VMEM)], out_specs=pl.BlockSpec(memory_space=pltpu.VMEM),
        scratch_shapes=[pltpu.VMEM((N_DEV,) + v.shape, v.dtype), pltpu.SemaphoreType.DMA((N_DEV - 1,)),
                        pltpu.SemaphoreType.DMA((N_DEV - 1,))],
    )(v)


def _add_chips(name, landed, pair, chip):
    nq, r, w = landed.shape
    tr = 64

    def body(chip_ref, *refs):
        own = refs[nq][...].astype(F32)
        tot = None
        for q in range(nq):
            term = jnp.where(chip_ref[0] == q, own, refs[q][...].astype(F32))
            tot = term if tot is None else tot + term
        refs[nq + 1][...] = tot

    specs = [pl.BlockSpec((None, tr, w), functools.partial(lambda j, chip_ref, q: (q, j, 0), q=q)) for q in range(nq)]
    specs.append(pl.BlockSpec((None, tr, w), lambda j, chip_ref: (chip_ref[0], j, 0)))
    grid_spec = pltpu.PrefetchScalarGridSpec(
        num_scalar_prefetch=1, grid=(r // tr,), in_specs=specs,
        out_specs=pl.BlockSpec((None, tr, w), lambda j, chip_ref: (0, j, 0)))
    return pl.pallas_call(
        body, name=name, grid_spec=grid_spec, out_shape=jax.ShapeDtypeStruct((1, r, w), F32),
        compiler_params=_params(("parallel",)),
    )(jnp.reshape(chip, (1,)).astype(jnp.int32), *([landed] * nq), pair)


def _add_pair(name, halves, got, c):
    nq, _, r, w = halves.shape
    tr = 64

    def body(c_ref, a_ref, b_ref, o_ref):
        o_ref[...] = (a_ref[...] + b_ref[...]).astype(o_ref.dtype)

    grid_spec = pltpu.PrefetchScalarGridSpec(
        num_scalar_prefetch=1, grid=(nq, r // tr),
        in_specs=[pl.BlockSpec((None, None, tr, w), lambda i, j, c_ref: (i, c_ref[0], j, 0)),
                  pl.BlockSpec((None, tr, w), lambda i, j, c_ref: (i, j, 0))],
        out_specs=pl.BlockSpec((None, tr, w), lambda i, j, c_ref: (i, j, 0)))
    return pl.pallas_call(
        body, name=name, grid_spec=grid_spec, out_shape=jax.ShapeDtypeStruct((nq, r, w), BF16),
        compiler_params=_params(("parallel", "parallel")),
    )(jnp.reshape(c, (1,)).astype(jnp.int32), halves, got)


def _adamw(name, w, g, m, v, tm):
    def fn(wv, gv, mv, vv):
        m2 = ADAM_B1 * mv + (1.0 - ADAM_B1) * gv
        v2 = ADAM_B2 * vv + (1.0 - ADAM_B2) * (gv * gv)
        m_hat = m2 / (1.0 - ADAM_B1 ** ADAM_STEP)
        v_hat = v2 / (1.0 - ADAM_B2 ** ADAM_STEP)
        return -ADAM_LR * (m_hat / (jnp.sqrt(v_hat) + ADAM_EPS) + ADAM_WD * wv), m2, v2
    c = w.shape[1]
    return _rows(name, fn, [w, g, m, v], [], [(c, F32)] * 3, tm=tm)


REST_ROWS = 256 + 3 * 128 + 256
REST_SPLITS = (("w_mem_kv", 0, 256), ("w_branch_a", 256, 128), ("w_branch_b", 384, 128),
               ("w_branch_m", 512, 128), ("w_out", 640, 256))


def _rest_pack(t):
    return jnp.concatenate([t[n].reshape(rows, D_MODEL) for n, _, rows in REST_SPLITS], axis=0)


def _rest_unpack(a, shapes):
    return {n: a[r0:r0 + rows].reshape(shapes[n]) for n, r0, rows in REST_SPLITS}


def _small_pack(pre, post, memg, bforget, bmerge):
    pad = jnp.zeros((1, D_MODEL - B_HEADS), F32)
    return jnp.concatenate([pre, post, memg, bmerge.reshape(3, D_MODEL),
                            jnp.concatenate([bforget, pad], axis=1), jnp.zeros((1, D_MODEL), F32)], axis=0)


def _small_unpack(s8):
    return dict(norm_pre_g=s8[0:1], norm_post_g=s8[1:2], norm_mem_g=s8[2:3],
                b_merge=s8[3:6].reshape(1, 3 * D_MODEL), b_forget=s8[6:7, :B_HEADS])


WEIGHTS = ("norm_pre_g", "norm_post_g", "norm_mem_g", "w_in", "b_forget", "b_merge", "w_mem_kv",
           "w_branch_a", "w_branch_b", "w_branch_m", "w_out")
SMALL = ("norm_pre_g", "norm_post_g", "norm_mem_g", "b_forget", "b_merge")


def kernel(x, mem, positions, norm_pre_g, norm_post_g, norm_mem_g, w_in, b_forget, b_merge, w_mem_kv, w_branch_a, w_branch_b, w_branch_m, w_out, loss_target, m_norm_pre_g, m_norm_post_g, m_norm_mem_g, m_w_in, m_b_forget, m_b_merge, m_w_mem_kv, m_w_branch_a, m_w_branch_b, m_w_branch_m, m_w_out, v_norm_pre_g, v_norm_post_g, v_norm_mem_g, v_w_in, v_b_forget, v_b_merge, v_w_mem_kv, v_w_branch_a, v_w_branch_b, v_w_branch_m, v_w_out):
    w = dict(norm_pre_g=norm_pre_g, norm_post_g=norm_post_g, norm_mem_g=norm_mem_g, w_in=w_in[0],
             b_forget=b_forget, b_merge=b_merge, w_mem_kv=w_mem_kv[0], w_branch_a=w_branch_a[0],
             w_branch_b=w_branch_b[0], w_branch_m=w_branch_m[0], w_out=w_out[0])
    mo = dict(norm_pre_g=m_norm_pre_g, norm_post_g=m_norm_post_g, norm_mem_g=m_norm_mem_g, w_in=m_w_in[0],
              b_forget=m_b_forget, b_merge=m_b_merge, w_mem_kv=m_w_mem_kv[0], w_branch_a=m_w_branch_a[0],
              w_branch_b=m_w_branch_b[0], w_branch_m=m_w_branch_m[0], w_out=m_w_out[0])
    vo = dict(norm_pre_g=v_norm_pre_g, norm_post_g=v_norm_post_g, norm_mem_g=v_norm_mem_g, w_in=v_w_in[0],
              b_forget=v_b_forget, b_merge=v_b_merge, w_mem_kv=v_w_mem_kv[0], w_branch_a=v_w_branch_a[0],
              w_branch_b=v_w_branch_b[0], w_branch_m=v_w_branch_m[0], w_out=v_w_out[0])
    s = x.shape[1]
    c = lax.axis_index("c")

    chip = 2 * lax.axis_index("x") + lax.axis_index("y")

    def put(whole, own, slot):
        return lax.dynamic_update_index_in_dim(whole, own.astype(whole.dtype), slot, 0)

    own_w = [w["w_in"].astype(BF16).reshape(2, D_MODEL // 2, SHARD_COLS),
             _rest_pack(w).astype(BF16).reshape(2, REST_ROWS // 2, D_MODEL)]
    all_in, all_rest = _gather_weights(own_w)
    all_in = all_in.reshape(N_CHIPS, D_MODEL, SHARD_COLS)
    own_in, own_rest = own_w[0].reshape(D_MODEL, SHARD_COLS), own_w[1].reshape(REST_ROWS, D_MODEL)
    w_main, w_fb = _split_forget([jnp.where(chip == p, own_in, all_in[p]) for p in range(N_CHIPS)])
    w_fb = jnp.concatenate([w_fb, jnp.zeros((D_MODEL, HD - B_HEADS), BF16)], axis=1)
    all_rest = all_rest.reshape(N_CHIPS, REST_ROWS, D_MODEL)
    all_rest = jnp.stack([jnp.where(chip == p, own_rest, all_rest[p]) for p in range(N_CHIPS)])
    w_kv_f = all_rest[:, 0:256].reshape(D_MODEL, D_MODEL)
    w_br_f = [all_rest[:, 256 + 128 * i:384 + 128 * i].reshape(N_CHIPS, 512, 256).transpose(1, 0, 2)
              .reshape(512, D_MODEL) for i in range(3)]
    w_out_f = all_rest[:, 640:896].reshape(D_MODEL, D_MODEL)

    pair = []

    def exchange(g):
        def per_chip(name, p):
            a = g[name]
            if name in ("w_mem_kv", "w_out"):
                return a[256 * p:256 * (p + 1)]
            return a[:, 256 * p:256 * (p + 1)]

        in4 = jnp.stack(g["w_in"])
        rest4 = jnp.stack([_rest_pack({n: per_chip(n, p) for n, _, _ in REST_SPLITS}) for p in range(N_CHIPS)])
        halves = [in4.reshape(N_CHIPS, 2, D_MODEL // 2, SHARD_COLS),
                  rest4.reshape(N_CHIPS, 2, REST_ROWS // 2, D_MODEL)]
        got = _swap_with_sibling(halves)
        pair.extend(_add_pair("add_pair_%d" % i, halves[i], got[i], c) for i in range(2))
        return _scatter_to_owners(pair)

    loss_lanes, grad_x, g, landed = _local_step(
        x[0], mem[0], positions.reshape(s, 1), loss_target[0], norm_pre_g, norm_post_g, norm_mem_g,
        w_main, w_fb, b_forget, b_merge, w_kv_f, w_br_f[0], w_br_f[1], w_br_f[2], w_out_f, exchange)
    loss = lax.psum(jnp.sum(loss_lanes), ("x", "y", "c"))
    half = [_add_chips("add_chips_%d" % i, landed[i], pair[i], chip) for i in range(2)]
    red_in, red_rest = [put(a, o[0], c) for a, o in zip(_share_with_sibling(half), half)]
    gs = {"w_in": red_in.reshape(D_MODEL, SHARD_COLS)}
    gs.update(_rest_unpack(red_rest.reshape(REST_ROWS, D_MODEL), {n: w[n].shape for n, _, _ in REST_SPLITS}))
    gs.update(_small_unpack(_sum_small(_small_pack(
        g["norm_pre_g"], g["norm_post_g"], g["norm_mem_g"], g["b_forget"], g["b_merge"]))))

    delta, new_m, new_v = {}, {}, {}
    for n, tm in (("w_in", 128), ("w_mem_kv", 256), ("w_branch_a", 512), ("w_branch_b", 512),
                  ("w_branch_m", 512), ("w_out", 256)):
        d_, m_, v_ = _adamw("adamw_" + n, w[n], gs[n], mo[n], vo[n], tm)
        delta[n], new_m[n], new_v[n] = d_[None], m_[None], v_[None]
        gs[n] = gs[n][None]
    packs = [_small_pack(*[t[n] for n in SMALL])
             for t in (w, gs, mo, vo)]
    for res, store in zip(_adamw("adamw_small", *packs, 8), (delta, new_m, new_v)):
        store.update(_small_unpack(res))

    return (loss, grad_x[None], *[gs[n] for n in WEIGHTS], *[delta[n] for n in WEIGHTS],
            *[new_m[n] for n in WEIGHTS], *[new_v[n] for n in WEIGHTS])
```

```python
import functools

import jax
import jax.numpy as jnp
from jax import lax
from jax.experimental import pallas as pl
from jax.experimental.pallas import tpu as pltpu

F32 = jnp.float32
BF16 = jnp.bfloat16
MESH = pl.DeviceIdType.MESH

D_MODEL = 1024
N_MEM = 256
EPS = 1e-6
NEG = -1e30
ROPE_THETA = 500000.0
ROT_DIM = 32
HD = 128
A_GROUP = 512
DILATIONS = (1, 4, 16)
BAND = 128
B_HEADS = 8
B_HD = 64
N_CHIPS = 4
N_DEV = 8

C_QA, C_KA, C_VA, C_ZA = 0, 1536, 3072, 4608
C_QB, C_KB, C_VB, C_ZB = 5120, 5632, 6144, 6656
C_QM, C_ZM, C_GL = 7168, 7680, 8192
FB_ORIG = 6656
IN_COLS = 11272
SHARD_COLS = IN_COLS // N_CHIPS

ADAM_LR, ADAM_B1, ADAM_B2, ADAM_EPS, ADAM_WD, ADAM_STEP = 0.001, 0.9, 0.999, 1e-08, 0.01, 10

VMEM_LIMIT_V7X = 56 * 1024 * 1024

NT = (((1,), (1,)), ((), ()))
NN = (((1,), (0,)), ((), ()))
TN = (((0,), (0,)), ((), ()))


def _params(sem):
    return pltpu.CompilerParams(dimension_semantics=sem, vmem_limit_bytes=VMEM_LIMIT_V7X)


def _dot(a, b, dn=NN):
    return lax.dot_general(a, b, dn, preferred_element_type=F32)


def _sig(z):
    return 1.0 / (1.0 + jnp.exp(-z))


def _rows(name, fn, row_ins, bc_ins, outs, reds=(), tm=512, scratch=(), into=None):
    arrs, specs = [], []
    s = None
    for r in row_ins:
        arr, w, cb, d = (tuple(r) + (1,))[:4] if isinstance(r, tuple) else (r, r.shape[1], 0, 1)
        s = arr.shape[0] * d if s is None else s
        arrs.append(arr)
        specs.append((w, cb, d))
    tm = min(tm, s)
    specs = [pl.BlockSpec((tm // d, w), functools.partial(lambda i, cb: (i, cb), cb=cb)) for w, cb, d in specs]
    for b in bc_ins:
        arrs.append(b)
        specs.append(pl.BlockSpec(b.shape, lambda i: (0, 0)))
    outs = [(tuple(o) + (1,))[:3] for o in outs]
    n_in, n_out = len(arrs), len(outs)
    o0 = n_in + (0 if into is None else 1)

    def body(*refs):
        n_ref = o0 + n_out + len(reds)
        vals = fn(*[r[...] for r in refs[:n_in]], *refs[n_ref:])
        if not isinstance(vals, (tuple, list)):
            vals = (vals,)
        for r, v in zip(refs[o0:o0 + n_out], vals[:n_out]):
            r[...] = v.astype(r.dtype)
        if reds:
            red_refs = refs[o0 + n_out:n_ref]

            @pl.when(pl.program_id(0) == 0)
            def _():
                for r in red_refs:
                    r[...] = jnp.zeros_like(r)

            for r, v in zip(red_refs, vals[n_out:]):
                r[...] += v

    out_shape = [jax.ShapeDtypeStruct((s // d, c), dt) for c, dt, d in outs]
    out_shape += [jax.ShapeDtypeStruct((1, c), F32) for c in reds]
    out_specs = [pl.BlockSpec((tm // d, c), lambda i: (i, 0)) for c, _, d in outs]
    out_specs += [pl.BlockSpec((1, c), lambda i: (0, 0)) for c in reds]
    aliases = {}
    if into is not None:
        whole, k, cb = into
        out_shape[k] = jax.ShapeDtypeStruct(whole.shape, whole.dtype)
        if isinstance(cb, tuple):
            out_specs[k] = pl.BlockSpec((pl.Element(tm), pl.Element(outs[k][0])),
                                        functools.partial(lambda i, c0: (i * tm, c0), c0=cb[1]))
        else:
            out_specs[k] = pl.BlockSpec((tm, outs[k][0]), functools.partial(lambda i, cb: (i, cb), cb=cb))
        aliases = {n_in: k}
        arrs.append(whole)
        specs.append(pl.BlockSpec(memory_space=pl.ANY))
    res = pl.pallas_call(
        body, name=name, grid=(s // tm,), in_specs=specs, out_specs=out_specs, out_shape=out_shape,
        scratch_shapes=list(scratch), input_output_aliases=aliases,
        compiler_params=_params(("arbitrary",) if reds else ("parallel",)),
    )(*arrs)
    return res


def _to_class(x, scr, d):
    if d == 1:
        return x.astype(F32)
    tm, c = x.shape
    for g in range(c // 128):
        scr[g][...] = x[:, g * 128:(g + 1) * 128].astype(F32)
    return jnp.concatenate([scr[g][pl.ds(r, tm // d, stride=d), :] for r in range(d) for g in range(c // 128)],
                           axis=1)


def _from_class(x, scr, d):
    if d == 1:
        return x.astype(F32)
    n, dc = x.shape
    c = dc // d
    for r in range(d):
        for g in range(c // 128):
            scr[g][pl.ds(r, n, stride=d), :] = x[:, r * c + g * 128:r * c + (g + 1) * 128].astype(F32)
    return jnp.concatenate([scr[g][...] for g in range(c // 128)], axis=1)


def _mm(name, a, b, mode, out_dtype, tm=2048, tn=1024, tk=1024, side=None):
    if mode == "nn":
        (m, k), (_, n) = a.shape, b.shape
    elif mode == "nt":
        (m, k), (n, _) = a.shape, b.shape
    else:
        (k, m), (_, n) = a.shape, b.shape
    tm, tn, tk = min(tm, m), min(tn, n), min(tk, k)
    nk = k // tk
    grid = (m // tm, n // tn, nk)
    dn = {"nn": NN, "nt": NT, "tn": TN}[mode]
    n_si = len(side["ins"]) if side else 0
    n_so = len(side["outs"]) if side else 0
    n_acc = 1 if nk > 1 else 0

    def body(*refs):
        a_ref, b_ref = refs[:2]
        side_in, o_ref = refs[2:2 + n_si], refs[2 + n_si]
        side_out = refs[3 + n_si:3 + n_si + n_so]
        acc = refs[3 + n_si + n_so:3 + n_si + n_so + n_acc]
        side_scratch = refs[3 + n_si + n_so + n_acc:]
        step = (pl.program_id(0) * grid[1] + pl.program_id(1)) * grid[2] + pl.program_id(2)
        if side:
            @pl.when(step == 0)
            def _():
                side["start"](side_in, side_out, side_scratch)

        part = _dot(a_ref[...].astype(BF16), b_ref[...].astype(BF16), dn)
        if nk == 1:
            o_ref[...] = part.astype(o_ref.dtype)
        else:
            kk = pl.program_id(2)

            @pl.when(kk == 0)
            def _():
                acc[0][...] = part

            @pl.when(kk > 0)
            def _():
                acc[0][...] += part

            @pl.when(kk == nk - 1)
            def _():
                o_ref[...] = acc[0][...].astype(o_ref.dtype)

        if side:
            @pl.when(step == grid[0] * grid[1] * grid[2] - 1)
            def _():
                side["wait"](side_in, side_out, side_scratch)

    a_spec = (pl.BlockSpec((tk, tm), lambda i, j, kk: (kk, i)) if mode == "tn"
              else pl.BlockSpec((tm, tk), lambda i, j, kk: (i, kk)))
    b_spec = (pl.BlockSpec((tn, tk), lambda i, j, kk: (j, kk)) if mode == "nt"
              else pl.BlockSpec((tk, tn), lambda i, j, kk: (kk, j)))
    o_spec = pl.BlockSpec((tm, tn), lambda i, j, kk: (i, j))
    o_shape = jax.ShapeDtypeStruct((m, n), out_dtype)
    acc_scratch = [pltpu.VMEM((tm, tn), F32)] * n_acc
    if not side:
        return pl.pallas_call(
            body, name=name, grid=grid, in_specs=[a_spec, b_spec], out_specs=o_spec, out_shape=o_shape,
            scratch_shapes=acc_scratch, compiler_params=_params(("parallel", "parallel", "arbitrary")),
        )(a, b)
    return pl.pallas_call(
        body, name=name, grid=grid, in_specs=[a_spec, b_spec] + [HBM_SPEC] * n_si,
        out_specs=[o_spec] + [HBM_SPEC] * n_so, out_shape=[o_shape] + side["outs"],
        scratch_shapes=acc_scratch + side["scratch"],
        compiler_params=_params(("arbitrary", "arbitrary", "arbitrary")),
    )(a, b, *side["ins"])


def _rms_fwd(name, x, g):
    def fn(xv, gv):
        r = lax.rsqrt(jnp.mean(xv * xv, axis=-1, keepdims=True) + EPS)
        return (xv * r * gv,)
    return _rows(name, fn, [x], [g], [(x.shape[1], BF16)], tm=min(512, x.shape[0]))[0]


def _rms_fwd_both(name, x, g):
    s, dm = x.shape
    tm = min(512, s)

    def body(x_ref, g_ref, h_ref, ht_ref):
        xv = x_ref[...]
        hv = xv * lax.rsqrt(jnp.mean(xv * xv, axis=-1, keepdims=True) + EPS) * g_ref[...]
        h_ref[...] = hv.astype(BF16)
        ht_ref[...] = hv.T.astype(BF16)

    return pl.pallas_call(
        body, name=name, grid=(s // tm,),
        in_specs=[pl.BlockSpec((tm, dm), lambda i: (i, 0)), pl.BlockSpec((1, dm), lambda i: (0, 0))],
        out_specs=[pl.BlockSpec((tm, dm), lambda i: (i, 0)), pl.BlockSpec((dm, tm), lambda i: (0, i))],
        out_shape=[jax.ShapeDtypeStruct((s, dm), BF16), jax.ShapeDtypeStruct((dm, s), BF16)],
        compiler_params=_params(("parallel",)),
    )(x, g)


def _rope_tables(pos, inv):
    ang = pos.astype(F32) * inv
    lane = lax.broadcasted_iota(jnp.int32, ang.shape, 1)
    c = jnp.where(lane < ROT_DIM, jnp.cos(ang), 1.0)
    sn = jnp.sin(ang)
    sg = jnp.where(lane < ROT_DIM // 2, -sn, jnp.where(lane < ROT_DIM, sn, 0.0))
    return c, sg, lane


def _rope_apply(x, c, sg, lane):
    outs = []
    for h in range(x.shape[1] // HD):
        xh = x[:, h * HD:(h + 1) * HD].astype(F32)
        swap = jnp.where(lane < ROT_DIM // 2, pltpu.roll(xh, HD - ROT_DIM // 2, 1),
                         pltpu.roll(xh, ROT_DIM // 2, 1))
        outs.append(xh * c + swap * sg)
    return jnp.concatenate(outs, axis=1)


ROPE_TM = 512


def _class_scratch(tm):
    return [pltpu.VMEM((tm, 128), F32) for _ in range(A_GROUP // 128)]


def _rope_fwd(u, pos, inv):
    def fn(q, k, v, p, iv, *scr):
        c, sg, lane = _rope_tables(p, iv)
        qr, kr = _rope_apply(q, c, sg, lane), _rope_apply(k, c, sg, lane)
        outs = []
        for g, d in enumerate(DILATIONS):
            gs = slice(g * A_GROUP, (g + 1) * A_GROUP)
            outs += [_to_class(qr[:, gs], scr, d), _to_class(kr[:, gs], scr, d), _to_class(v[:, gs], scr, d)]
        return tuple(outs)

    outs = [(d * A_GROUP, BF16, d) for d in DILATIONS for _ in range(3)]
    qkv = [(u, 3 * A_GROUP, c0 // (3 * A_GROUP)) for c0 in (C_QA, C_KA, C_VA)]
    return _rows("rope_fwd", fn, qkv + [pos], [inv], outs, tm=ROPE_TM,
                 scratch=_class_scratch(ROPE_TM))


def _rope_bwd(dqs, dks, dvs, pos, inv, du):
    def fn(*args):
        grads, p, iv, scr = args[:9], args[9], args[10], args[11:]
        c, sg, lane = _rope_tables(p, iv)
        tok = [jnp.concatenate([_from_class(grads[3 * k + g], scr, d) for g, d in enumerate(DILATIONS)], axis=1)
               for k in range(3)]
        return (jnp.concatenate([_rope_apply(tok[0], c, -sg, lane), _rope_apply(tok[1], c, -sg, lane), tok[2]],
                                axis=1),)

    ins = [(a, a.shape[1], 0, d) for grp in (dqs, dks, dvs) for a, d in zip(grp, DILATIONS)]
    return _rows("rope_bwd", fn, ins + [pos], [inv], [(9 * A_GROUP, BF16)], tm=ROPE_TM,
                 scratch=_class_scratch(ROPE_TM), into=(du, 0, 0))[0]


def _lane_pack(cols, like):
    lane = lax.broadcasted_iota(jnp.int32, like, 1)
    out = jnp.zeros(like, F32)
    for h, cvec in enumerate(cols):
        out = jnp.where(lane == h, cvec, out)
    return out


def _band_specs(l, d, tq):
    nsb = tq // BAND
    nblk = l // BAND
    cur = pl.BlockSpec((tq, A_GROUP), lambda r, i: (i, r))
    prev = pl.BlockSpec((BAND, A_GROUP), lambda r, i: (jnp.maximum(i * nsb - 1, 0), r))
    nxt = pl.BlockSpec((BAND, A_GROUP), lambda r, i: (jnp.minimum((i + 1) * nsb, nblk - 1), r))
    st_cur = pl.BlockSpec((tq, HD), lambda r, i: (i, r))
    st_nxt = pl.BlockSpec((BAND, HD), lambda r, i: (jnp.minimum((i + 1) * nsb, nblk - 1), r))
    return nsb, cur, prev, nxt, st_cur, st_nxt


def _band_mask_q(i, first_tile):
    qr = lax.broadcasted_iota(jnp.int32, (BAND, 2 * BAND), 0)
    kc = lax.broadcasted_iota(jnp.int32, (BAND, 2 * BAND), 1)
    in_prev = (kc < BAND) & (kc >= qr)
    in_cur = (kc >= BAND) & (kc - BAND <= qr)
    if i == 0:
        in_prev = in_prev & jnp.logical_not(first_tile)
    return in_prev | in_cur


def _band_mask_k(j, nsb, last_tile):
    kc = lax.broadcasted_iota(jnp.int32, (BAND, 2 * BAND), 0)
    qr = lax.broadcasted_iota(jnp.int32, (BAND, 2 * BAND), 1)
    same = (qr < BAND) & (kc <= qr)
    nxt = (qr >= BAND) & (kc >= qr - BAND)
    if j == nsb - 1:
        nxt = nxt & jnp.logical_not(last_tile)
    return same | nxt


def _band_fwd(name, q, k, v, d):
    l = q.shape[0]
    tq = min(512, l)
    nsb, cur, prev, _, st_cur, _ = _band_specs(l, d, tq)
    scale = HD ** -0.5

    def body(q_ref, kc_ref, kp_ref, vc_ref, vp_ref, o_ref, lse_ref):
        first = pl.program_id(1) == 0
        for i in range(nsb):
            lses = []
            mask = _band_mask_q(i, first)
            for h in range(4):
                cs = slice(h * HD, (h + 1) * HD)
                qv = q_ref[i * BAND:(i + 1) * BAND, cs]
                if i == 0:
                    kk = jnp.concatenate([kp_ref[:, cs], kc_ref[0:BAND, cs]], axis=0)
                    vv = jnp.concatenate([vp_ref[:, cs], vc_ref[0:BAND, cs]], axis=0)
                else:
                    kk = kc_ref[(i - 1) * BAND:(i + 1) * BAND, cs]
                    vv = vc_ref[(i - 1) * BAND:(i + 1) * BAND, cs]
                s = jnp.where(mask, _dot(qv, kk, NT) * scale, NEG)
                m = jnp.max(s, axis=-1, keepdims=True)
                p = jnp.exp(s - m)
                den = jnp.sum(p, axis=-1, keepdims=True)
                o_ref[i * BAND:(i + 1) * BAND, cs] = _dot(p.astype(BF16), vv) / den
                lses.append(m + jnp.log(den))
            lse_ref[i * BAND:(i + 1) * BAND, :] = _lane_pack(lses, (BAND, HD))

    return pl.pallas_call(
        body, name=name, grid=(d, l // tq), in_specs=[cur, cur, prev, cur, prev],
        out_specs=[cur, st_cur],
        out_shape=[jax.ShapeDtypeStruct((l, d * A_GROUP), F32), jax.ShapeDtypeStruct((l, d * HD), F32)],
        compiler_params=_params(("parallel", "parallel")),
    )(q, k, k, v, v)


def _band_dq(name, q, k, v, dy, lse, delta, d):
    l = q.shape[0]
    tq = min(512, l)
    nsb, cur, prev, _, st_cur, _ = _band_specs(l, d, tq)
    scale = HD ** -0.5

    def body(q_ref, kc_ref, kp_ref, vc_ref, vp_ref, dy_ref, lse_ref, dl_ref, dq_ref):
        first = pl.program_id(1) == 0
        for i in range(nsb):
            mask = _band_mask_q(i, first)
            rs = slice(i * BAND, (i + 1) * BAND)
            for h in range(4):
                cs = slice(h * HD, (h + 1) * HD)
                if i == 0:
                    kk = jnp.concatenate([kp_ref[:, cs], kc_ref[0:BAND, cs]], axis=0)
                    vv = jnp.concatenate([vp_ref[:, cs], vc_ref[0:BAND, cs]], axis=0)
                else:
                    kk = kc_ref[(i - 1) * BAND:(i + 1) * BAND, cs]
                    vv = vc_ref[(i - 1) * BAND:(i + 1) * BAND, cs]
                s = jnp.where(mask, _dot(q_ref[rs, cs], kk, NT) * scale, NEG)
                p = jnp.exp(s - lse_ref[rs, h:h + 1])
                dp = _dot(dy_ref[rs, cs], vv, NT)
                ds = p * (dp - dl_ref[rs, h:h + 1])
                dq_ref[rs, cs] = (_dot(ds.astype(BF16), kk) * scale).astype(dq_ref.dtype)

    return pl.pallas_call(
        body, name=name, grid=(d, l // tq),
        in_specs=[cur, cur, prev, cur, prev, cur, st_cur, st_cur], out_specs=cur,
        out_shape=jax.ShapeDtypeStruct((l, d * A_GROUP), BF16),
        compiler_params=_params(("parallel", "parallel")),
    )(q, k, k, v, v, dy, lse, delta)


def _band_dkv(name, q, k, v, dy, lse, delta, d):
    l = q.shape[0]
    tq = min(512, l)
    nsb, cur, _, nxt, st_cur, st_nxt = _band_specs(l, d, tq)
    scale = HD ** -0.5
    ntile = l // tq

    def body(k_ref, v_ref, qc_ref, qn_ref, dyc_ref, dyn_ref, lc_ref, ln_ref, dc_ref, dn_ref,
             dk_ref, dv_ref):
        last = pl.program_id(1) == ntile - 1

        def win(c_ref, n_ref, j, cs):
            if j == nsb - 1:
                return jnp.concatenate([c_ref[j * BAND:(j + 1) * BAND, cs], n_ref[:, cs]], axis=0)
            return c_ref[j * BAND:(j + 2) * BAND, cs]

        allh = slice(0, HD)
        for j in range(nsb):
            mask = _band_mask_k(j, nsb, last)
            rs = slice(j * BAND, (j + 1) * BAND)
            lse_t = win(lc_ref, ln_ref, j, allh).T
            delta_t = win(dc_ref, dn_ref, j, allh).T
            for h in range(4):
                cs = slice(h * HD, (h + 1) * HD)
                qw = win(qc_ref, qn_ref, j, cs)
                dyw = win(dyc_ref, dyn_ref, j, cs)
                st = jnp.where(mask, _dot(k_ref[rs, cs], qw, NT) * scale, NEG)
                pt = jnp.exp(st - lse_t[h:h + 1, :])
                dst = pt * (_dot(v_ref[rs, cs], dyw, NT) - delta_t[h:h + 1, :])
                dv_ref[rs, cs] = _dot(pt.astype(BF16), dyw).astype(dv_ref.dtype)
                dk_ref[rs, cs] = (_dot(dst.astype(BF16), qw) * scale).astype(dk_ref.dtype)

    shp = jax.ShapeDtypeStruct((l, d * A_GROUP), BF16)
    return pl.pallas_call(
        body, name=name, grid=(d, ntile),
        in_specs=[cur, cur, cur, nxt, cur, nxt, st_cur, st_nxt, st_cur, st_nxt],
        out_specs=[cur, cur], out_shape=[shp, shp],
        compiler_params=_params(("parallel", "parallel")),
    )(k, v, q, q, dy, dy, lse, lse, delta, delta)


def _split3(x):
    hi = x.astype(BF16)
    r1 = x - hi.astype(F32)
    mid = r1.astype(BF16)
    lo = (r1 - mid.astype(F32)).astype(BF16)
    return hi, mid, lo


def _fox_prep(z, b):
    h, s = z.shape
    blk = min(512, s)

    def body(z_ref, b_ref, c_ref):
        r = lax.broadcasted_iota(jnp.int32, (blk, blk), 0)
        cidx = lax.broadcasted_iota(jnp.int32, (blk, blk), 1)
        tri = (r <= cidx).astype(BF16)
        carry = jnp.zeros((h, 1), F32)
        for t in range(s // blk):
            zz = z_ref[:, t * blk:(t + 1) * blk] + b_ref[...]
            lf = jnp.minimum(zz, 0.0) - jnp.log(1.0 + jnp.exp(-jnp.abs(zz)))
            hi, mid, lo = _split3(lf)
            cs = _dot(hi, tri) + _dot(mid, tri) + _dot(lo, tri) + carry
            c_ref[:, t * blk:(t + 1) * blk] = cs
            carry = cs[:, blk - 1:blk]

    return pl.pallas_call(body, name="fox_prep", out_shape=jax.ShapeDtypeStruct((h, s), F32))(z, b)


def _fox_prep_bwd(dc, z, b):
    h, s = z.shape
    blk = min(512, s)

    def body(dc_ref, z_ref, b_ref, dz_ref, db_ref):
        r = lax.broadcasted_iota(jnp.int32, (blk, blk), 0)
        cidx = lax.broadcasted_iota(jnp.int32, (blk, blk), 1)
        tri = (r >= cidx).astype(BF16)
        carry = jnp.zeros((h, 1), F32)
        tot = jnp.zeros((h, 1), F32)
        for t in reversed(range(s // blk)):
            hi, mid, lo = _split3(dc_ref[:, t * blk:(t + 1) * blk])
            rc = _dot(hi, tri) + _dot(mid, tri) + _dot(lo, tri) + carry
            carry = rc[:, 0:1]
            zz = z_ref[:, t * blk:(t + 1) * blk] + b_ref[...]
            dz = rc * _sig(-zz)
            dz_ref[:, t * blk:(t + 1) * blk] = dz
            tot = tot + jnp.sum(dz, axis=-1, keepdims=True)
        db_ref[...] = tot

    return pl.pallas_call(
        body, name="fox_prep_bwd",
        out_shape=[jax.ShapeDtypeStruct((h, s), F32), jax.ShapeDtypeStruct((h, 1), F32)])(dc, z, b)


FOX_W = 128
FOX_C = B_HD
FOX_ONE = B_HD + 3
FOX_SUB = 256
FOX_SUB_FWD = 128
FOX_HEADS_PER_STEP = 2


def _head_of_pair(x, hh):
    return x if hh == 0 else pltpu.roll(x, B_HD, 1)


def _fox_pack(u, c_col, t):
    s = u.shape[0]
    nt = s // t
    scale = B_HD ** -0.5

    def body(q_ref, k_ref, v_ref, c_ref, qf_ref, kb_ref, ks_ref, vb_ref, vt_ref):
        lane = lax.broadcasted_iota(jnp.int32, (t, FOX_W), 1)
        for hd in range(B_HEADS):
            pair, hh = slice(hd // 2 * FOX_W, (hd // 2 + 1) * FOX_W), hd % 2
            qv, kv, vv = [r[:, pair].astype(F32) for r in (q_ref, k_ref, v_ref)]
            qf_ref[hd] = jnp.where(lane < B_HD, _head_of_pair(qv, hh), B_HD ** 0.5).astype(BF16)
            neg = c_ref[hd] * (-scale)
            hi = neg.astype(BF16).astype(F32)
            mid = (neg - hi).astype(BF16).astype(F32)
            lo = neg - hi - mid
            aux = jnp.where(lane == FOX_C, hi,
                            jnp.where(lane == FOX_C + 1, mid, jnp.where(lane == FOX_C + 2, lo, 0.0)))
            kb = jnp.where(lane < B_HD, _head_of_pair(kv, hh) * scale, aux)
            kb_ref[hd] = kb.astype(BF16)
            ks_ref[hd] = jnp.where(lane == FOX_ONE, 1.0, kb).T.astype(BF16)
            vb = jnp.where(lane < B_HD, _head_of_pair(vv, hh), 1.0)
            vb_ref[hd] = vb.astype(BF16)
            vt_ref[hd] = vb.T.astype(BF16)

    def tok(col0):
        return pl.BlockSpec((t, B_HEADS * B_HD), functools.partial(lambda i, cb: (i, cb), cb=col0 // (B_HEADS * B_HD)))

    rows = pl.BlockSpec((B_HEADS, t, FOX_W), lambda i: (0, i, 0))
    tiles = pl.BlockSpec((B_HEADS, None, FOX_W, t), lambda i: (0, i, 0, 0))
    hm = jax.ShapeDtypeStruct((B_HEADS, s, FOX_W), BF16)
    tt = jax.ShapeDtypeStruct((B_HEADS, nt, FOX_W, t), BF16)
    return pl.pallas_call(
        body, name="fox_pack", grid=(nt,),
        in_specs=[tok(C_QB), tok(C_KB), tok(C_VB), pl.BlockSpec((B_HEADS, t, 1), lambda i: (0, i, 0))],
        out_specs=[rows, rows, tiles, rows, tiles], out_shape=[hm, hm, tt, hm, tt],
        compiler_params=_params(("parallel",)),
    )(u, u, u, c_col)


def _fox_pack_bwd(dy, y, t):
    s = dy.shape[0]
    nt = s // t

    def body(do_ref, o_ref, dow_ref, dl_ref):
        lane = lax.broadcasted_iota(jnp.int32, (t, FOX_W), 1)
        lane8 = lax.broadcasted_iota(jnp.int32, (8, FOX_W), 1)
        for pr in range(B_HEADS // 2):
            pair = slice(pr * FOX_W, (pr + 1) * FOX_W)
            dov = do_ref[:, pair].astype(F32)
            parts = _split3(dov * o_ref[:, pair].astype(F32))
            for hh in range(2):
                dow_ref[2 * pr + hh] = jnp.where(lane < B_HD, _head_of_pair(dov, hh), 0.0).astype(BF16)
                mask = ((lane8 >= hh * B_HD) & (lane8 < (hh + 1) * B_HD)).astype(BF16)
                row = _dot(mask, parts[0], NT) + _dot(mask, parts[1], NT) + _dot(mask, parts[2], NT)
                dl_ref[2 * pr + hh] = row[0:1, :]

    tok = pl.BlockSpec((t, B_HEADS * B_HD), lambda i: (i, 0))
    return pl.pallas_call(
        body, name="fox_pack_bwd", grid=(nt,), in_specs=[tok, tok],
        out_specs=[pl.BlockSpec((B_HEADS, t, FOX_W), lambda i: (0, i, 0)),
                   pl.BlockSpec((B_HEADS, None, 1, t), lambda i: (0, i, 0, 0))],
        out_shape=[jax.ShapeDtypeStruct((B_HEADS, s, FOX_W), BF16), jax.ShapeDtypeStruct((B_HEADS, nt, 1, t), F32)],
        compiler_params=_params(("parallel",)),
    )(dy, y)


def _fox_unpack(dqt, dkw, dvw, du, t):
    h, nt = dqt.shape[:2]

    def body(dq_ref, dk_ref, dv_ref, _, o_ref, dc_ref):
        lane = lax.broadcasted_iota(jnp.int32, (t, FOX_W), 1)

        def join(a0, a1):
            return jnp.where(lane < B_HD, a0, pltpu.roll(a1, B_HD, 1))

        for hh in range(h):
            dc_ref[hh] = dq_ref[hh][FOX_ONE:FOX_ONE + 1, :] - dk_ref[hh].T[B_HD:B_HD + 1, :]
        pairs = range(0, h, 2)
        cols = ([join(dq_ref[a].T, dq_ref[a + 1].T) for a in pairs] + [join(dk_ref[a], dk_ref[a + 1]) for a in pairs]
                + [join(dv_ref[a], dv_ref[a + 1]) for a in pairs])
        o_ref[...] = jnp.concatenate(cols, axis=1).astype(o_ref.dtype)

    rows = pl.BlockSpec((h, t, FOX_W), lambda i: (0, i, 0))
    return pl.pallas_call(
        body, name="fox_unpack", grid=(nt,),
        in_specs=[pl.BlockSpec((h, None, FOX_W, t), lambda i: (0, i, 0, 0)), rows, rows,
                  pl.BlockSpec(memory_space=pl.ANY)],
        out_specs=[pl.BlockSpec((pl.Element(t), pl.Element(3 * h * B_HD)), lambda i: (i * t, C_QB)),
                   pl.BlockSpec((h, None, 1, t), lambda i: (0, i, 0, 0))],
        out_shape=[jax.ShapeDtypeStruct(du.shape, du.dtype), jax.ShapeDtypeStruct((h, nt, 1, t), F32)],
        input_output_aliases={3: 0},
        compiler_params=_params(("parallel",)),
    )(dqt, dkw, dvw, du)


FOX_DEAD = -110.0


def _fox_norm2(qf, kb):
    h, s, w = qf.shape
    tm = min(2048, s)

    def body(q_ref, k_ref, qo_ref, ko_ref):
        row = lax.broadcasted_iota(jnp.int32, (w, w), 0)
        ones = (row < B_HD).astype(BF16)
        for x_ref, o_ref in ((q_ref, qo_ref), (k_ref, ko_ref)):
            xv = x_ref[...].astype(F32)
            n2 = _dot((xv * xv).astype(BF16), ones)
            o_ref[...] = jnp.broadcast_to(jnp.max(n2, axis=0, keepdims=True)[:, :1], o_ref.shape)

    spec = pl.BlockSpec((None, tm, w), lambda hh, i: (hh, i, 0))
    ospec = pl.BlockSpec((None, None, 8, 128), lambda hh, i: (hh, i, 0, 0))
    shp = jax.ShapeDtypeStruct((h, s // tm, 8, 128), F32)
    return pl.pallas_call(
        body, name="fox_norm2", grid=(h, s // tm), in_specs=[spec, spec], out_specs=[ospec, ospec],
        out_shape=[shp, shp], compiler_params=_params(("parallel", "parallel")),
    )(qf, kb)


def _fox_bounds(qf, kb, c, t):
    q2, k2 = _fox_norm2(qf, kb)
    g = 2.0 * jnp.sqrt(1.02 * jnp.max(q2[:, :, 0, 0], axis=1) * 1.02 * jnp.max(k2[:, :, 0, 0], axis=1))
    return jnp.concatenate([c[:, ::t], c[:, t - 1::t], g[:, None]], axis=1)


SMEM_SPEC = pl.BlockSpec(memory_space=pltpu.SMEM)


def _fox_fwd(qf, kb, vt4, bounds, t):
    h, s, w = qf.shape
    nt = s // t
    sub = FOX_SUB_FWD
    nsub = t // sub
    nh = FOX_HEADS_PER_STEP

    def body(b_ref, q_ref, k_ref, v_ref, o_ref, lse_ref):
        i = pl.program_id(1)
        krow = lax.broadcasted_iota(jnp.int32, (sub, t), 0)
        qcol = lax.broadcasted_iota(jnp.int32, (sub, t), 1)

        def dead_before(hh):
            head = pl.program_id(0) * nh + hh
            top = b_ref[head, 2 * nt] + b_ref[head, i]
            return lax.fori_loop(
                0, i, lambda jj, n: n + (top - b_ref[head, nt + jj] < FOX_DEAD).astype(jnp.int32), 0)

        j_lo = functools.reduce(jnp.minimum, [dead_before(hh) for hh in range(nh)])

        def tile(j, carry, diag):
            out = []
            for hh in range(nh):
                m, acc = carry[hh]
                qv, vj = q_ref[hh], v_ref[hh, j]
                los = [b * sub if diag else 0 for b in range(nsub)]
                sts = [_dot(k_ref[hh, pl.ds(pl.multiple_of(j * t + b * sub, sub), sub), :], qv[lo:, :], NT)
                       for b, lo in enumerate(los)]
                for b, lo in enumerate(los):
                    st = sts[b]
                    if diag:
                        st = jnp.where(krow[:, :t - lo] <= qcol[:, :t - lo], st, NEG)
                    m_old, acc_old = m[:, lo:], acc[:, lo:]
                    m2 = jnp.maximum(m_old, jnp.max(st, axis=0, keepdims=True))
                    p = jnp.exp(st - m2).astype(BF16)
                    acc2 = jnp.exp(m_old - m2) * acc_old + _dot(vj[:, b * sub:(b + 1) * sub], p)
                    m = m2 if lo == 0 else jnp.concatenate([m[:, :lo], m2], axis=1)
                    acc = acc2 if lo == 0 else jnp.concatenate([acc[:, :lo], acc2], axis=1)
                out.append((m, acc))
            return tuple(out)

        init = tuple((jnp.full((1, t), NEG, F32), jnp.zeros((w, t), F32)) for _ in range(nh))
        carry = lax.fori_loop(j_lo, i, lambda j, c: tile(j, c, False), init)
        outs = []
        for hh, (m, acc) in enumerate(tile(i, carry, True)):
            den = acc[B_HD:B_HD + 1, :]
            outs.append(acc[0:B_HD, :] / den)
            lse_ref[hh] = m + jnp.log(den)
        o_ref[...] = jnp.concatenate(outs, axis=0).T.astype(o_ref.dtype)

    return pl.pallas_call(
        body, name="fox_fwd", grid=(h // nh, nt),
        in_specs=[SMEM_SPEC,
                  pl.BlockSpec((nh, t, w), lambda hh, i: (hh, i, 0)),
                  pl.BlockSpec((nh, s, w), lambda hh, i: (hh, 0, 0)),
                  pl.BlockSpec((nh, nt, w, t), lambda hh, i: (hh, 0, 0, 0))],
        out_specs=[pl.BlockSpec((t, nh * B_HD), lambda hh, i: (i, hh)),
                   pl.BlockSpec((nh, 1, t), lambda hh, i: (hh, 0, i))],
        out_shape=[jax.ShapeDtypeStruct((s, h * B_HD), BF16), jax.ShapeDtypeStruct((h, 1, s), F32)],
        compiler_params=_params(("parallel", "parallel")),
    )(bounds, qf, kb, vt4)


def _fox_bwd(qf, dow, lse_row, delta_row, kb, kst4, vb, bounds, t):
    h, s, w = qf.shape
    nt = s // t
    nsub = t // FOX_SUB
    nh = FOX_HEADS_PER_STEP

    def body(b_ref, q_ref, do_ref, lse_ref, dl_ref, k_ref, kt_ref, v_ref, dqt_ref, dk_ref, dv_ref, dk_acc, dv_acc):
        j = pl.program_id(1)

        def alive_after(hh):
            head = pl.program_id(0) * nh + hh
            top = b_ref[head, 2 * nt] - b_ref[head, nt + j]
            return lax.fori_loop(
                j + 1, nt, lambda ii, n: n + (top + b_ref[head, ii] >= FOX_DEAD).astype(jnp.int32), 0)

        i_hi = j + 1 + functools.reduce(jnp.maximum, [alive_after(hh) for hh in range(nh)])

        @pl.when(j == 0)
        def _():
            dqt_ref[...] = jnp.zeros_like(dqt_ref)

        dk_acc[...] = jnp.zeros_like(dk_acc)
        dv_acc[...] = jnp.zeros_like(dv_acc)
        krow = lax.broadcasted_iota(jnp.int32, (FOX_SUB, t), 0)
        qcol = lax.broadcasted_iota(jnp.int32, (FOX_SUB, t), 1)
        subs = [slice(b * FOX_SUB, (b + 1) * FOX_SUB) for b in range(nsub)]

        def tile(i, diag):
            i0 = pl.multiple_of(i * t, t)
            for hh in range(nh):
                qi, doi = q_ref[hh, pl.ds(i0, t), :], do_ref[hh, pl.ds(i0, t), :]
                lse, dl = lse_ref[hh, i], dl_ref[hh, i]
                los = [b * FOX_SUB if diag else 0 for b in range(nsub)]
                sts = [_dot(k_ref[hh, rs, :], qi[lo:, :], NT) for rs, lo in zip(subs, los)]
                dps = [_dot(v_ref[hh, rs, :], doi[lo:, :], NT) for rs, lo in zip(subs, los)]
                dq = None
                for b, (rs, lo) in enumerate(zip(subs, los)):
                    st = sts[b] - lse[:, lo:]
                    if diag:
                        st = jnp.where(krow[:, :t - lo] <= qcol[:, :t - lo], st, NEG)
                    pt = jnp.exp(st)
                    dsb = (pt * (dps[b] - dl[:, lo:])).astype(BF16)
                    dv_acc[hh, rs, :] += _dot(pt.astype(BF16), doi[lo:, :])
                    dk_acc[hh, rs, :] += _dot(dsb, qi[lo:, :])
                    part = _dot(kt_ref[hh, :, rs], dsb)
                    if lo:
                        part = jnp.concatenate([jnp.zeros((w, lo), F32), part], axis=1)
                    dq = part if dq is None else dq + part
                dqt_ref[hh, i] += dq

        def step(i, carry):
            tile(i, False)
            return carry

        tile(j, True)
        lax.fori_loop(j + 1, i_hi, step, 0)
        dk_ref[...] = dk_acc[...] * (B_HD ** -0.5)
        dv_ref[...] = dv_acc[...]

    full = pl.BlockSpec((nh, s, w), lambda hh, j: (hh, 0, 0))
    rowst = pl.BlockSpec((nh, nt, 1, t), lambda hh, j: (hh, 0, 0, 0))
    tl = pl.BlockSpec((nh, t, w), lambda hh, j: (hh, j, 0))
    return pl.pallas_call(
        body, name="fox_bwd", grid=(h // nh, nt),
        in_specs=[SMEM_SPEC, full, full, rowst, rowst, tl,
                  pl.BlockSpec((nh, None, w, t), lambda hh, j: (hh, j, 0, 0)), tl],
        out_specs=[pl.BlockSpec((nh, nt, w, t), lambda hh, j: (hh, 0, 0, 0)), tl, tl],
        out_shape=[jax.ShapeDtypeStruct((h, nt, w, t), F32), jax.ShapeDtypeStruct((h, s, w), F32),
                   jax.ShapeDtypeStruct((h, s, w), F32)],
        scratch_shapes=[pltpu.VMEM((nh, t, w), F32), pltpu.VMEM((nh, t, w), F32)],
        compiler_params=_params(("parallel", "arbitrary")),
    )(bounds, qf, dow, lse_row, delta_row, kb, kst4, vb)


def _mem_fwd(u, mkv, tq=512):
    s = u.shape[0]
    scale = HD ** -0.5

    def body(q_ref, mk_ref, mv_ref, o_ref, lse_ref):
        lses = []
        for h in range(4):
            cs = slice(h * HD, (h + 1) * HD)
            sc = _dot(q_ref[:, cs], mk_ref[:, cs], NT) * scale
            m = jnp.max(sc, axis=-1, keepdims=True)
            p = jnp.exp(sc - m)
            den = jnp.sum(p, axis=-1, keepdims=True)
            o_ref[:, cs] = (_dot(p.astype(BF16), mv_ref[:, cs]) / den).astype(o_ref.dtype)
            lses.append(m + jnp.log(den))
        lse_ref[...] = _lane_pack(lses, (tq, HD))

    return pl.pallas_call(
        body, name="mem_fwd", grid=(s // tq,),
        in_specs=[pl.BlockSpec((tq, 512), lambda i: (i, C_QM // 512)),
                  pl.BlockSpec((N_MEM, 512), lambda i: (0, 0)),
                  pl.BlockSpec((N_MEM, 512), lambda i: (0, 1))],
        out_specs=[pl.BlockSpec((tq, 512), lambda i: (i, 0)), pl.BlockSpec((tq, HD), lambda i: (i, 0))],
        out_shape=[jax.ShapeDtypeStruct((s, 512), BF16), jax.ShapeDtypeStruct((s, HD), F32)],
        compiler_params=_params(("parallel",)),
    )(u, mkv, mkv)


def _mem_bwd(u, mkv, o, do, lse, du, tq=512):
    s = u.shape[0]
    scale = HD ** -0.5

    def body(q_ref, mk_ref, mv_ref, o_ref, do_ref, lse_ref, _, dq_ref, dmk_ref, dmv_ref):
        @pl.when(pl.program_id(0) == 0)
        def _():
            dmk_ref[...] = jnp.zeros_like(dmk_ref)
            dmv_ref[...] = jnp.zeros_like(dmv_ref)

        for h in range(4):
            cs = slice(h * HD, (h + 1) * HD)
            qv, dov = q_ref[:, cs], do_ref[:, cs]
            sc = _dot(qv, mk_ref[:, cs], NT) * scale
            p = jnp.exp(sc - lse_ref[:, h:h + 1])
            delta = jnp.sum(dov.astype(F32) * o_ref[:, cs].astype(F32), axis=-1, keepdims=True)
            ds = p * (_dot(dov, mv_ref[:, cs], NT) - delta)
            dsb = ds.astype(BF16)
            dq_ref[:, cs] = (_dot(dsb, mk_ref[:, cs]) * scale).astype(dq_ref.dtype)
            dmk_ref[:, cs] += _dot(dsb, qv, TN) * scale
            dmv_ref[:, cs] += _dot(p.astype(BF16), dov, TN)

    row = pl.BlockSpec((tq, 512), lambda i: (i, 0))
    acc = pl.BlockSpec((N_MEM, 512), lambda i: (0, 0))
    return pl.pallas_call(
        body, name="mem_bwd", grid=(s // tq,),
        in_specs=[pl.BlockSpec((tq, 512), lambda i: (i, C_QM // 512)),
                  pl.BlockSpec((N_MEM, 512), lambda i: (0, 0)),
                  pl.BlockSpec((N_MEM, 512), lambda i: (0, 1)),
                  row, row, pl.BlockSpec((tq, HD), lambda i: (i, 0)), pl.BlockSpec(memory_space=pl.ANY)],
        out_specs=[pl.BlockSpec((tq, 512), lambda i: (i, C_QM // 512)), acc, acc],
        out_shape=[jax.ShapeDtypeStruct(du.shape, du.dtype), jax.ShapeDtypeStruct((N_MEM, 512), F32),
                   jax.ShapeDtypeStruct((N_MEM, 512), F32)],
        input_output_aliases={6: 0},
        compiler_params=_params(("arbitrary",)),
    )(u, mkv, mkv, o, do, lse, du)


FB_CHIP = FB_ORIG // SHARD_COLS
FB_AT = FB_ORIG - FB_CHIP * SHARD_COLS


def _chip_slabs(main, fb):
    cuts = [SHARD_COLS * p - (B_HEADS if p > FB_CHIP else 0) for p in range(N_CHIPS + 1)]
    slabs = [main[:, a:b] for a, b in zip(cuts[:-1], cuts[1:])]
    own = slabs[FB_CHIP]
    slabs[FB_CHIP] = jnp.concatenate([own[:, :FB_AT], fb, own[:, FB_AT:]], axis=1)
    return slabs


def _split_forget(slabs):
    own = slabs[FB_CHIP]
    parts = list(slabs[:FB_CHIP]) + [own[:, :FB_AT], own[:, FB_AT + B_HEADS:]] + list(slabs[FB_CHIP + 1:])
    return jnp.concatenate(parts, axis=1), own[:, FB_AT:FB_AT + B_HEADS]


def _local_step(x, mem, pos, target, g_pre, g_post, g_mem, w_main, w_fb, b_forget, b_merge,
                w_mem_kv, w_ba, w_bb, w_bm, w_out, exchange=None):
    s = x.shape[0]
    t_fox = min(512, s)
    nt = s // t_fox
    half = ROT_DIM // 2
    inv = ROPE_THETA ** (-jnp.arange(half, dtype=F32) / half)
    inv128 = jnp.concatenate([inv, inv, jnp.zeros((HD - ROT_DIM,), F32)]).reshape(1, HD)

    h, h_t = _rms_fwd_both("norm_pre", x, g_pre)
    u = _mm("proj_in", h, w_main, "nn", BF16, tm=4096)
    ufb = _mm("proj_fb", h, w_fb, "nn", F32)
    memn = _rms_fwd("norm_mem", mem, g_mem)
    mkv = _mm("proj_mem", memn, w_mem_kv, "nn", BF16)

    qkv = _rope_fwd(u, pos, inv128)
    views = [tuple(qkv[3 * g:3 * g + 3]) for g in range(3)]
    os_, lses = [], []
    for g, d in enumerate(DILATIONS):
        o_g, lse_g = _band_fwd("band_fwd%d" % g, *views[g], d)
        os_.append((o_g, d * A_GROUP, 0, d))
        lses.append((lse_g, d * HD, 0, d))

    def merge_a(o1, o2, o3, l1, l2, l3, za, *scr):
        o1, o2, o3 = [_from_class(o, scr, d) for o, d in zip((o1, o2, o3), DILATIONS)]
        l1, l2, l3 = [_from_class(lv, scr, d) for lv, d in zip((l1, l2, l3), DILATIONS)]
        ys, tots = [], []
        for hh in range(4):
            cs, hs = slice(hh * HD, (hh + 1) * HD), slice(hh, hh + 1)
            mx = jnp.maximum(jnp.maximum(l1[:, hs], l2[:, hs]), l3[:, hs])
            e1, e2, e3 = jnp.exp(l1[:, hs] - mx), jnp.exp(l2[:, hs] - mx), jnp.exp(l3[:, hs] - mx)
            den = e1 + e2 + e3
            ys.append((e1 * o1[:, cs] + e2 * o2[:, cs] + e3 * o3[:, cs]) / den)
            tots.append(mx + jnp.log(den))
        y = jnp.concatenate(ys, axis=1)
        zf = za.astype(F32)
        tot = _lane_pack(tots, l1.shape)
        return (y, y * (zf * _sig(zf))) + tuple(_to_class(tot, scr, d) for d in DILATIONS)

    res = _rows("merge_a", merge_a, os_ + lses + [(u, 512, C_ZA // 512)], [],
                [(512, BF16), (512, BF16)] + [(d * HD, F32, d) for d in DILATIONS], tm=ROPE_TM,
                scratch=_class_scratch(ROPE_TM))
    y_a, yg_a, lse_a = res[0], res[1], res[2:5]

    zrow = ufb[:, :B_HEADS].T
    c = _fox_prep(zrow, b_forget.reshape(B_HEADS, 1))
    qf, kb, kst4, vb, vt4 = _fox_pack(u, c.reshape(B_HEADS, s, 1), t_fox)
    bounds = _fox_bounds(qf, kb, c, t_fox)
    y_b, lse_b = _fox_fwd(qf, kb, vt4, bounds, t_fox)

    y_m, lse_m = _mem_fwd(u, mkv)

    def gate(y, z):
        zf = z.astype(F32)
        return (y.astype(F32) * (zf * _sig(zf)),)

    yg_b = _rows("gate_b", gate, [y_b, (u, 512, C_ZB // 512)], [], [(512, BF16)])[0]
    yg_m = _rows("gate_m", gate, [y_m, (u, 512, C_ZM // 512)], [], [(512, BF16)])[0]

    br_a = _mm("branch_a", yg_a, w_ba, "nn", BF16)
    br_b = _mm("branch_b", yg_b, w_bb, "nn", BF16)
    br_m = _mm("branch_m", yg_m, w_bm, "nn", BF16)
    gl = [(u, 1024, C_GL // 1024 + i) for i in range(3)]
    bm3 = b_merge.reshape(3, D_MODEL)

    def merge(g0, g1, g2, b0, b1, b2, bm):
        tot = 0.0
        for i, (gv, bv) in enumerate(((g0, b0), (g1, b1), (g2, b2))):
            tot = tot + _sig(gv.astype(F32) + bm[i:i + 1, :]) * bv.astype(F32)
        return (tot,)

    merged = _rows("merge_gates", merge, gl + [br_a, br_b, br_m], [bm3], [(D_MODEL, BF16)])[0]
    out = _mm("proj_out", merged, w_out, "nn", F32)

    def tail(xv, ov, tv, gv):
        r = lax.rsqrt(jnp.mean(ov * ov, axis=-1, keepdims=True) + EPS)
        n = ov * r
        err = xv + n * gv - tv
        dy = err * (1.0 / D_MODEL)
        dn = dy * gv
        dout = r * (dn - n * jnp.mean(dn * n, axis=-1, keepdims=True))
        return (dy, dout, jnp.sum(0.5 * err * err * (1.0 / D_MODEL), axis=0, keepdims=True),
                jnp.sum(dy * n, axis=0, keepdims=True))

    dy, dout, loss_lanes, g_post_grad = _rows(
        "tail", tail, [x, out, target], [g_post], [(D_MODEL, F32), (D_MODEL, BF16)],
        reds=[D_MODEL, D_MODEL], tm=512)

    dmerged = _mm("d_merged", dout, w_out, "nt", BF16)
    gw_out = _mm("g_w_out", merged, dout, "tn", F32, tk=2048)

    def merge_bwd(dm, g0, g1, g2, b0, b1, b2, bm):
        dmf = dm.astype(F32)
        dbs, dgs, sums = [], [], []
        for i, (gv, bv) in enumerate(((g0, b0), (g1, b1), (g2, b2))):
            sg = _sig(gv.astype(F32) + bm[i:i + 1, :])
            dbs.append(dmf * sg)
            dg = dmf * bv.astype(F32) * sg * (1.0 - sg)
            dgs.append(dg)
            sums.append(jnp.sum(dg, axis=0, keepdims=True))
        return tuple(dbs + [jnp.concatenate(dgs, axis=1)] + sums)

    du = lax.empty(u.shape, BF16)
    res = _rows("merge_bwd", merge_bwd, [dmerged] + gl + [br_a, br_b, br_m], [bm3],
                [(D_MODEL, BF16)] * 3 + [(3 * D_MODEL, BF16)], reds=[D_MODEL] * 3, tm=512,
                into=(du, 3, ("column", C_GL)))
    dbr, du, g_bmerge = res[0:3], res[3], jnp.concatenate(res[4:7], axis=1)

    dyg, gw_branch = [], []
    for nm, dbv, wv, ygv in (("a", dbr[0], w_ba, yg_a), ("b", dbr[1], w_bb, yg_b), ("m", dbr[2], w_bm, yg_m)):
        dyg.append(_mm("d_yg_" + nm, dbv, wv, "nt", BF16))
        gw_branch.append(_mm("g_w_branch_" + nm, ygv, dbv, "tn", F32, tk=2048))

    def gate_bwd(dg, y, z):
        dgf, yf, zf = dg.astype(F32), y.astype(F32), z.astype(F32)
        sg = _sig(zf)
        return dgf * (zf * sg), dgf * yf * (sg * (1.0 + zf * (1.0 - sg)))

    def gate_bwd_a(dg, y, z, *scr):
        dyv, dz = gate_bwd(dg, y, z)
        prod = dyv * y.astype(F32)
        dl = [jnp.sum(prod[:, hh * HD:(hh + 1) * HD], axis=-1, keepdims=True) for hh in range(4)]
        delta = _lane_pack(dl, (dg.shape[0], HD))
        return ((dz,) + tuple(_to_class(dyv, scr, d) for d in DILATIONS)
                + tuple(_to_class(delta, scr, d) for d in DILATIONS))

    res = _rows("gate_bwd_a", gate_bwd_a, [dyg[0], y_a, (u, 512, C_ZA // 512)], [],
                [(512, BF16)] + [(d * A_GROUP, BF16, d) for d in DILATIONS] + [(d * HD, F32, d) for d in DILATIONS],
                tm=ROPE_TM, scratch=_class_scratch(ROPE_TM), into=(du, 0, C_ZA // 512))
    du, dy_a, delta_a = res[0], res[1:4], res[4:7]
    dy_b, du = _rows("gate_bwd_b", gate_bwd, [dyg[1], y_b, (u, 512, C_ZB // 512)], [],
                     [(512, BF16), (512, BF16)], into=(du, 1, C_ZB // 512))
    dy_m, du = _rows("gate_bwd_m", gate_bwd, [dyg[2], y_m, (u, 512, C_ZM // 512)], [],
                     [(512, BF16), (512, BF16)], into=(du, 1, C_ZM // 512))

    du, dmk, dmv = _mem_bwd(u, mkv, y_m, dy_m, lse_m, du)
    dmkv = jnp.concatenate([dmk, dmv], axis=1)
    gw_mem_kv = _mm("g_w_mem_kv", memn, dmkv, "tn", F32)
    dmemn = _mm("d_memn", dmkv, w_mem_kv, "nt", F32)

    def mem_gain_grad(mv, dv):
        r = lax.rsqrt(jnp.mean(mv * mv, axis=-1, keepdims=True) + EPS)
        return (jnp.sum(dv * mv * r, axis=0, keepdims=True),)

    g_mem_grad = _rows("g_norm_mem", mem_gain_grad, [mem, dmemn], [], [], reds=[D_MODEL], tm=N_MEM)[0]

    dow, delta_b = _fox_pack_bwd(dy_b, y_b, t_fox)
    dqt, dkw, dvw = _fox_bwd(qf, dow, lse_b.reshape(B_HEADS, nt, 1, t_fox), delta_b, kb, kst4, vb, bounds, t_fox)
    du, dc = _fox_unpack(dqt, dkw, dvw, du, t_fox)
    dzrow, g_bforget = _fox_prep_bwd(dc.reshape(B_HEADS, s), zrow, b_forget.reshape(B_HEADS, 1))
    dfb = jnp.zeros((s, HD), BF16).at[:, :B_HEADS].set(dzrow.T.astype(BF16))

    dqs, dks, dvs = [], [], []
    for g, d in enumerate(DILATIONS):
        qv, kv, vv = views[g]
        dqs.append(_band_dq("band_dq%d" % g, qv, kv, vv, dy_a[g], lse_a[g], delta_a[g], d))
        dk_g, dv_g = _band_dkv("band_dkv%d" % g, qv, kv, vv, dy_a[g], lse_a[g], delta_a[g], d)
        dks.append(dk_g)
        dvs.append(dv_g)
    du = _rope_bwd(dqs, dks, dvs, pos, inv128, du)

    gw_main = _mm("g_w_main", h_t, du, "nn", F32, tk=2048)
    gw_fb = _mm("g_w_fb", h, dfb, "tn", F32)
    grads = dict(norm_post_g=g_post_grad, norm_mem_g=g_mem_grad, w_in=_chip_slabs(gw_main, gw_fb[:, :B_HEADS]),
                 b_forget=g_bforget.reshape(1, B_HEADS), b_merge=g_bmerge, w_mem_kv=gw_mem_kv,
                 w_branch_a=gw_branch[0], w_branch_b=gw_branch[1], w_branch_m=gw_branch[2], w_out=gw_out)
    side = exchange(grads) if exchange else None
    dh_main = _mm("d_h", du, w_main, "nt", F32, tm=1024, tk=2816, side=side)
    landed = None
    if side:
        dh_main, landed = dh_main[0], dh_main[1:]
    dh_fb = _mm("d_h_fb", dfb, w_fb, "nt", F32)

    def pre_bwd(xv, d1, d2, dyv, gv):
        r = lax.rsqrt(jnp.mean(xv * xv, axis=-1, keepdims=True) + EPS)
        n = xv * r
        dhv = d1 + d2
        dn = dhv * gv
        dx = r * (dn - n * jnp.mean(dn * n, axis=-1, keepdims=True))
        return dyv + dx, jnp.sum(dhv * n, axis=0, keepdims=True)

    grad_x, g_pre_grad = _rows("norm_pre_bwd", pre_bwd, [x, dh_main, dh_fb, dy], [g_pre],
                               [(D_MODEL, F32)], reds=[D_MODEL], tm=512)

    grads["norm_pre_g"] = g_pre_grad
    return loss_lanes, grad_x, grads, landed


HBM_SPEC = pl.BlockSpec(memory_space=pltpu.HBM)


def _place():
    x, y, c = lax.axis_index("x"), lax.axis_index("y"), lax.axis_index("c")
    chips = [(1 - x, y), (x, 1 - y), (1 - x, 1 - y)]
    return x, y, c, 2 * x + y, chips


N_CHUNKS = 4


def _units(parts, row_axis):
    units = []
    for i, a in enumerate(parts):
        ch = a.shape[row_axis] // N_CHUNKS
        units += [(i, pl.ds(k * ch, ch)) for k in range(N_CHUNKS)]
    return units


def _gather_weights(parts):
    n = len(parts)
    units = _units(parts, 1)
    nu = len(units)
    via_y = [(u % N_CHUNKS) < N_CHUNKS // 2 for u in range(nu)]

    def body(*refs):
        srcs, outs = refs[:n], refs[n:2 * n]
        send_sems, recv_sems = refs[2 * n:]
        x, y, c, p, _ = _place()
        me, sib = (x, y, c), (x, y, 1 - c)
        xn, yn, dg = (1 - x, y), (x, 1 - y), (1 - x, 1 - y)

        def cp(u, k, chip, half, to, from_src=False):
            i, rs = units[u]
            dst = outs[i].at[2 * chip[0] + chip[1], half, rs]
            return pltpu.make_async_remote_copy(
                src_ref=srcs[i].at[half, rs] if from_src else dst, dst_ref=dst, send_sem=send_sems.at[u, k],
                recv_sem=recv_sems.at[u, k], device_id=to, device_id_type=MESH)

        sent = []

        def go(copy):
            copy.start()
            sent.append(copy)

        for u in range(nu):
            go(cp(u, 0, (x, y), c, (*xn, c), from_src=True))
            go(cp(u, 1, (x, y), c, (*yn, c), from_src=True))
        for u in range(nu):
            cp(u, 0, xn, c, me).wait_recv()
            go(cp(u, 4, xn, c, sib))
            if via_y[u]:
                go(cp(u, 2, xn, c, (*yn, c)))
            cp(u, 1, yn, c, me).wait_recv()
            go(cp(u, 5, yn, c, sib))
            if not via_y[u]:
                go(cp(u, 3, yn, c, (*xn, c)))
        for u in range(nu):
            cp(u, 2 if via_y[u] else 3, dg, c, me).wait_recv()
            go(cp(u, 6, dg, c, sib))
        for u in range(nu):
            for k, chip in ((4, xn), (5, yn), (6, dg)):
                cp(u, k, chip, 1 - c, me).wait_recv()
        for copy in sent:
            copy.wait_send()

    return pl.pallas_call(
        body, name="gather_weights", in_specs=[HBM_SPEC] * n, out_specs=[HBM_SPEC] * n,
        out_shape=[jax.ShapeDtypeStruct((N_CHIPS,) + a.shape, a.dtype) for a in parts],
        scratch_shapes=[pltpu.SemaphoreType.DMA((nu, 7)), pltpu.SemaphoreType.DMA((nu, 7))],
    )(*parts)


def _swap_with_sibling(parts):
    n = len(parts)
    units = _units(parts, 2)

    def body(*refs):
        srcs, outs = refs[:n], refs[n:2 * n]
        send_sems, recv_sems = refs[2 * n:]
        x, y, c, _, _ = _place()
        cps = [pltpu.make_async_remote_copy(
            src_ref=srcs[i].at[q, 1 - c, rs], dst_ref=outs[i].at[q, rs], send_sem=send_sems.at[u, q],
            recv_sem=recv_sems.at[u, q], device_id=(x, y, 1 - c), device_id_type=MESH)
            for q in range(N_CHIPS) for u, (i, rs) in enumerate(units)]
        for cpy in cps:
            cpy.start()
        for cpy in cps:
            cpy.wait()

    return pl.pallas_call(
        body, name="swap_with_sibling", in_specs=[HBM_SPEC] * n, out_specs=[HBM_SPEC] * n,
        out_shape=[jax.ShapeDtypeStruct(a.shape[:1] + a.shape[2:], a.dtype) for a in parts],
        scratch_shapes=[pltpu.SemaphoreType.DMA((len(units), N_CHIPS)),
                        pltpu.SemaphoreType.DMA((len(units), N_CHIPS))],
    )(*parts)


def _scatter_to_owners(parts):
    n = len(parts)
    units = _units(parts, 1)

    def copies(srcs, outs, send_sems, recv_sems, incoming):
        x, y, c, p, chips = _place()
        return [pltpu.make_async_remote_copy(
            src_ref=srcs[i].at[2 * cx + cy, rs], dst_ref=outs[i].at[(2 * cx + cy) if incoming else p, rs],
            send_sem=send_sems.at[u, j], recv_sem=recv_sems.at[u, j], device_id=(cx, cy, c), device_id_type=MESH)
            for u, (i, rs) in enumerate(units) for j, (cx, cy) in enumerate(chips)]

    def start(ins, outs, scratch):
        for cpy in copies(ins, outs, *scratch, incoming=False):
            cpy.start()

    def wait(ins, outs, scratch):
        for cpy in copies(ins, outs, *scratch, incoming=True):
            cpy.wait_recv()
        for cpy in copies(ins, outs, *scratch, incoming=False):
            cpy.wait_send()

    return dict(ins=list(parts), outs=[jax.ShapeDtypeStruct(a.shape, a.dtype) for a in parts],
                scratch=[pltpu.SemaphoreType.DMA((len(units), 3)), pltpu.SemaphoreType.DMA((len(units), 3))],
                start=start, wait=wait)


def _share_with_sibling(parts):
    n = len(parts)
    units = _units(parts, 1)

    def body(*refs):
        srcs, outs = refs[:n], refs[n:2 * n]
        send_sems, recv_sems = refs[2 * n:]
        x, y, c, _, _ = _place()
        sends = [pltpu.make_async_remote_copy(
            src_ref=srcs[i].at[0, rs], dst_ref=outs[i].at[c, rs], send_sem=send_sems.at[u],
            recv_sem=recv_sems.at[u], device_id=(x, y, 1 - c), device_id_type=MESH)
            for u, (i, rs) in enumerate(units)]
        for cpy in sends:
            cpy.start()
        for u, (i, rs) in enumerate(units):
            pltpu.make_async_remote_copy(
                src_ref=srcs[i].at[0, rs], dst_ref=outs[i].at[1 - c, rs], send_sem=send_sems.at[u],
                recv_sem=recv_sems.at[u], device_id=(x, y, 1 - c), device_id_type=MESH).wait_recv()
        for cpy in sends:
            cpy.wait_send()

    return pl.pallas_call(
        body, name="share_with_sibling", in_specs=[HBM_SPEC] * n, out_specs=[HBM_SPEC] * n,
        out_shape=[jax.ShapeDtypeStruct((2,) + a.shape[1:], a.dtype) for a in parts],
        scratch_shapes=[pltpu.SemaphoreType.DMA((len(units),)), pltpu.SemaphoreType.DMA((len(units),))],
    )(*parts)


def _sum_small(v):
    def body(v_ref, out_ref, buf, send_sems, recv_sems):
        x, y, c, _, _ = _place()
        me = 4 * x + 2 * y + c
        buf[me] = v_ref[...]
        flips = [(dx, dy, dc) for dx in (0, 1) for dy in (0, 1) for dc in (0, 1)][1:]
        sends = []
        for k, (dx, dy, dc) in enumerate(flips):
            cpy = pltpu.make_async_remote_copy(
                src_ref=v_ref, dst_ref=buf.at[me], send_sem=send_sems.at[k], recv_sem=recv_sems.at[k],
                device_id=((x + dx) % 2, (y + dy) % 2, (c + dc) % 2), device_id_type=MESH)
            cpy.start()
            sends.append(cpy)
        for k, (dx, dy, dc) in enumerate(flips):
            px, py, pc = (x + dx) % 2, (y + dy) % 2, (c + dc) % 2
            pltpu.make_async_remote_copy(
                src_ref=v_ref, dst_ref=buf.at[4 * px + 2 * py + pc], send_sem=send_sems.at[k],
                recv_sem=recv_sems.at[k], device_id=(px, py, pc), device_id_type=MESH).wait_recv()
        for cpy in sends:
            cpy.wait_send()
        tot = buf[0]
        for i in range(1, N_DEV):
            tot = tot + buf[i]
        out_ref[...] = tot

    return pl.pallas_call(
        body, name="sum_small", out_shape=jax.ShapeDtypeStruct(v.shape, v.dtype),
        in_specs=[pl.BlockSpec(memory_space=pltpu.VMEM)], out_specs=pl.BlockSpec(memory_space=pltpu.VMEM),
        scratch_shapes=[pltpu.VMEM((N_DEV,) + v.shape, v.dtype), pltpu.SemaphoreType.DMA((N_DEV - 1,)),
                        pltpu.SemaphoreType.DMA((N_DEV - 1,))],
    )(v)


def _add_chips(name, landed, pair, chip):
    nq, r, w = landed.shape
    tr = 64

    def body(chip_ref, *refs):
        own = refs[nq][...].astype(F32)
        tot = None
        for q in range(nq):
            term = jnp.where(chip_ref[0] == q, own, refs[q][...].astype(F32))
            tot = term if tot is None else tot + term
        refs[nq + 1][...] = tot

    specs = [pl.BlockSpec((None, tr, w), functools.partial(lambda j, chip_ref, q: (q, j, 0), q=q)) for q in range(nq)]
    specs.append(pl.BlockSpec((None, tr, w), lambda j, chip_ref: (chip_ref[0], j, 0)))
    grid_spec = pltpu.PrefetchScalarGridSpec(
        num_scalar_prefetch=1, grid=(r // tr,), in_specs=specs,
        out_specs=pl.BlockSpec((None, tr, w), lambda j, chip_ref: (0, j, 0)))
    return pl.pallas_call(
        body, name=name, grid_spec=grid_spec, out_shape=jax.ShapeDtypeStruct((1, r, w), F32),
        compiler_params=_params(("parallel",)),
    )(jnp.reshape(chip, (1,)).astype(jnp.int32), *([landed] * nq), pair)


def _add_pair(name, halves, got, c):
    nq, _, r, w = halves.shape
    tr = 64

    def body(c_ref, a_ref, b_ref, o_ref):
        o_ref[...] = (a_ref[...] + b_ref[...]).astype(o_ref.dtype)

    grid_spec = pltpu.PrefetchScalarGridSpec(
        num_scalar_prefetch=1, grid=(nq, r // tr),
        in_specs=[pl.BlockSpec((None, None, tr, w), lambda i, j, c_ref: (i, c_ref[0], j, 0)),
                  pl.BlockSpec((None, tr, w), lambda i, j, c_ref: (i, j, 0))],
        out_specs=pl.BlockSpec((None, tr, w), lambda i, j, c_ref: (i, j, 0)))
    return pl.pallas_call(
        body, name=name, grid_spec=grid_spec, out_shape=jax.ShapeDtypeStruct((nq, r, w), BF16),
        compiler_params=_params(("parallel", "parallel")),
    )(jnp.reshape(c, (1,)).astype(jnp.int32), halves, got)


def _adamw(name, w, g, m, v, tm):
    def fn(wv, gv, mv, vv):
        m2 = ADAM_B1 * mv + (1.0 - ADAM_B1) * gv
        v2 = ADAM_B2 * vv + (1.0 - ADAM_B2) * (gv * gv)
        m_hat = m2 / (1.0 - ADAM_B1 ** ADAM_STEP)
        v_hat = v2 / (1.0 - ADAM_B2 ** ADAM_STEP)
        return -ADAM_LR * (m_hat / (jnp.sqrt(v_hat) + ADAM_EPS) + ADAM_WD * wv), m2, v2
    c = w.shape[1]
    return _rows(name, fn, [w, g, m, v], [], [(c, F32)] * 3, tm=tm)


REST_ROWS = 256 + 3 * 128 + 256
REST_SPLITS = (("w_mem_kv", 0, 256), ("w_branch_a", 256, 128), ("w_branch_b", 384, 128),
               ("w_branch_m", 512, 128), ("w_out", 640, 256))


def _rest_pack(t):
    return jnp.concatenate([t[n].reshape(rows, D_MODEL) for n, _, rows in REST_SPLITS], axis=0)


def _rest_unpack(a, shapes):
    return {n: a[r0:r0 + rows].reshape(shapes[n]) for n, r0, rows in REST_SPLITS}


def _small_pack(pre, post, memg, bforget, bmerge):
    pad = jnp.zeros((1, D_MODEL - B_HEADS), F32)
    return jnp.concatenate([pre, post, memg, bmerge.reshape(3, D_MODEL),
                            jnp.concatenate([bforget, pad], axis=1), jnp.zeros((1, D_MODEL), F32)], axis=0)


def _small_unpack(s8):
    return dict(norm_pre_g=s8[0:1], norm_post_g=s8[1:2], norm_mem_g=s8[2:3],
                b_merge=s8[3:6].reshape(1, 3 * D_MODEL), b_forget=s8[6:7, :B_HEADS])


WEIGHTS = ("norm_pre_g", "norm_post_g", "norm_mem_g", "w_in", "b_forget", "b_merge", "w_mem_kv",
           "w_branch_a", "w_branch_b", "w_branch_m", "w_out")
SMALL = ("norm_pre_g", "norm_post_g", "norm_mem_g", "b_forget", "b_merge")


def kernel(x, mem, positions, norm_pre_g, norm_post_g, norm_mem_g, w_in, b_forget, b_merge, w_mem_kv, w_branch_a, w_branch_b, w_branch_m, w_out, loss_target, m_norm_pre_g, m_norm_post_g, m_norm_mem_g, m_w_in, m_b_forget, m_b_merge, m_w_mem_kv, m_w_branch_a, m_w_branch_b, m_w_branch_m, m_w_out, v_norm_pre_g, v_norm_post_g, v_norm_mem_g, v_w_in, v_b_forget, v_b_merge, v_w_mem_kv, v_w_branch_a, v_w_branch_b, v_w_branch_m, v_w_out):
    w = dict(norm_pre_g=norm_pre_g, norm_post_g=norm_post_g, norm_mem_g=norm_mem_g, w_in=w_in[0],
             b_forget=b_forget, b_merge=b_merge, w_mem_kv=w_mem_kv[0], w_branch_a=w_branch_a[0],
             w_branch_b=w_branch_b[0], w_branch_m=w_branch_m[0], w_out=w_out[0])
    mo = dict(norm_pre_g=m_norm_pre_g, norm_post_g=m_norm_post_g, norm_mem_g=m_norm_mem_g, w_in=m_w_in[0],
              b_forget=m_b_forget, b_merge=m_b_merge, w_mem_kv=m_w_mem_kv[0], w_branch_a=m_w_branch_a[0],
              w_branch_b=m_w_branch_b[0], w_branch_m=m_w_branch_m[0], w_out=m_w_out[0])
    vo = dict(norm_pre_g=v_norm_pre_g, norm_post_g=v_norm_post_g, norm_mem_g=v_norm_mem_g, w_in=v_w_in[0],
              b_forget=v_b_forget, b_merge=v_b_merge, w_mem_kv=v_w_mem_kv[0], w_branch_a=v_w_branch_a[0],
              w_branch_b=v_w_branch_b[0], w_branch_m=v_w_branch_m[0], w_out=v_w_out[0])
    s = x.shape[1]
    c = lax.axis_index("c")

    chip = 2 * lax.axis_index("x") + lax.axis_index("y")

    def put(whole, own, slot):
        return lax.dynamic_update_index_in_dim(whole, own.astype(whole.dtype), slot, 0)

    own_w = [w["w_in"].astype(BF16).reshape(2, D_MODEL // 2, SHARD_COLS),
             _rest_pack(w).astype(BF16).reshape(2, REST_ROWS // 2, D_MODEL)]
    all_in, all_rest = _gather_weights(own_w)
    all_in = all_in.reshape(N_CHIPS, D_MODEL, SHARD_COLS)
    own_in, own_rest = own_w[0].reshape(D_MODEL, SHARD_COLS), own_w[1].reshape(REST_ROWS, D_MODEL)
    w_main, w_fb = _split_forget([jnp.where(chip == p, own_in, all_in[p]) for p in range(N_CHIPS)])
    w_fb = jnp.concatenate([w_fb, jnp.zeros((D_MODEL, HD - B_HEADS), BF16)], axis=1)
    all_rest = all_rest.reshape(N_CHIPS, REST_ROWS, D_MODEL)
    all_rest = jnp.stack([jnp.where(chip == p, own_rest, all_rest[p]) for p in range(N_CHIPS)])
    w_kv_f = all_rest[:, 0:256].reshape(D_MODEL, D_MODEL)
    w_br_f = [all_rest[:, 256 + 128 * i:384 + 128 * i].reshape(N_CHIPS, 512, 256).transpose(1, 0, 2)
              .reshape(512, D_MODEL) for i in range(3)]
    w_out_f = all_rest[:, 640:896].reshape(D_MODEL, D_MODEL)

    pair = []

    def exchange(g):
        def per_chip(name, p):
            a = g[name]
            if name in ("w_mem_kv", "w_out"):
                return a[256 * p:256 * (p + 1)]
            return a[:, 256 * p:256 * (p + 1)]

        in4 = jnp.stack(g["w_in"])
        rest4 = jnp.stack([_rest_pack({n: per_chip(n, p) for n, _, _ in REST_SPLITS}) for p in range(N_CHIPS)])
        halves = [in4.reshape(N_CHIPS, 2, D_MODEL // 2, SHARD_COLS),
                  rest4.reshape(N_CHIPS, 2, REST_ROWS // 2, D_MODEL)]
        got = _swap_with_sibling(halves)
        pair.extend(_add_pair("add_pair_%d" % i, halves[i], got[i], c) for i in range(2))
        return _scatter_to_owners(pair)

    loss_lanes, grad_x, g, landed = _local_step(
        x[0], mem[0], positions.reshape(s, 1), loss_target[0], norm_pre_g, norm_post_g, norm_mem_g,
        w_main, w_fb, b_forget, b_merge, w_kv_f, w_br_f[0], w_br_f[1], w_br_f[2], w_out_f, exchange)
    loss = lax.psum(jnp.sum(loss_lanes), ("x", "y", "c"))
    half = [_add_chips("add_chips_%d" % i, landed[i], pair[i], chip) for i in range(2)]
    red_in, red_rest = [put(a, o[0], c) for a, o in zip(_share_with_sibling(half), half)]
    gs = {"w_in": red_in.reshape(D_MODEL, SHARD_COLS)}
    gs.update(_rest_unpack(red_rest.reshape(REST_ROWS, D_MODEL), {n: w[n].shape for n, _, _ in REST_SPLITS}))
    gs.update(_small_unpack(_sum_small(_small_pack(
        g["norm_pre_g"], g["norm_post_g"], g["norm_mem_g"], g["b_forget"], g["b_merge"]))))

    delta, new_m, new_v = {}, {}, {}
    for n, tm in (("w_in", 128), ("w_mem_kv", 256), ("w_branch_a", 512), ("w_branch_b", 512),
                  ("w_branch_m", 512), ("w_out", 256)):
        d_, m_, v_ = _adamw("adamw_" + n, w[n], gs[n], mo[n], vo[n], tm)
        delta[n], new_m[n], new_v[n] = d_[None], m_[None], v_[None]
        gs[n] = gs[n][None]
    packs = [_small_pack(*[t[n] for n in SMALL])
             for t in (w, gs, mo, vo)]
    for res, store in zip(_adamw("adamw_small", *packs, 8), (delta, new_m, new_v)):
        store.update(_small_unpack(res))

    return (loss, grad_x[None], *[gs[n] for n in WEIGHTS], *[delta[n] for n in WEIGHTS],
            *[new_m[n] for n in WEIGHTS], *[new_v[n] for n in WEIGHTS])
```

```python
import functools

import jax
import jax.numpy as jnp
from jax import lax
from jax.experimental import pallas as pl
from jax.experimental.pallas import tpu as pltpu

F32 = jnp.float32
BF16 = jnp.bfloat16
MESH = pl.DeviceIdType.MESH

D_MODEL = 1024
N_MEM = 256
EPS = 1e-6
NEG = -1e30
ROPE_THETA = 500000.0
ROT_DIM = 32
HD = 128
A_GROUP = 512
DILATIONS = (1, 4, 16)
BAND = 128
B_HEADS = 8
B_HD = 64
N_CHIPS = 4
N_DEV = 8

C_QA, C_KA, C_VA, C_ZA = 0, 1536, 3072, 4608
C_QB, C_KB, C_VB, C_ZB = 5120, 5632, 6144, 6656
C_QM, C_ZM, C_GL = 7168, 7680, 8192
FB_ORIG = 6656
IN_COLS = 11272
SHARD_COLS = IN_COLS // N_CHIPS

ADAM_LR, ADAM_B1, ADAM_B2, ADAM_EPS, ADAM_WD, ADAM_STEP = 0.001, 0.9, 0.999, 1e-08, 0.01, 10

VMEM_LIMIT_V7X = 56 * 1024 * 1024

NT = (((1,), (1,)), ((), ()))
NN = (((1,), (0,)), ((), ()))
TN = (((0,), (0,)), ((), ()))


def _params(sem):
    return pltpu.CompilerParams(dimension_semantics=sem, vmem_limit_bytes=VMEM_LIMIT_V7X)


def _dot(a, b, dn=NN):
    return lax.dot_general(a, b, dn, preferred_element_type=F32)


def _sig(z):
    return 1.0 / (1.0 + jnp.exp(-z))


def _rows(name, fn, row_ins, bc_ins, outs, reds=(), tm=512, scratch=(), into=None):
    arrs, specs = [], []
    s = None
    for r in row_ins:
        arr, w, cb, d = (tuple(r) + (1,))[:4] if isinstance(r, tuple) else (r, r.shape[1], 0, 1)
        s = arr.shape[0] * d if s is None else s
        arrs.append(arr)
        specs.append((w, cb, d))
    tm = min(tm, s)
    specs = [pl.BlockSpec((tm // d, w), functools.partial(lambda i, cb: (i, cb), cb=cb)) for w, cb, d in specs]
    for b in bc_ins:
        arrs.append(b)
        specs.append(pl.BlockSpec(b.shape, lambda i: (0, 0)))
    outs = [(tuple(o) + (1,))[:3] for o in outs]
    n_in, n_out = len(arrs), len(outs)
    o0 = n_in + (0 if into is None else 1)

    def body(*refs):
        n_ref = o0 + n_out + len(reds)
        vals = fn(*[r[...] for r in refs[:n_in]], *refs[n_ref:])
        if not isinstance(vals, (tuple, list)):
            vals = (vals,)
        for r, v in zip(refs[o0:o0 + n_out], vals[:n_out]):
            r[...] = v.astype(r.dtype)
        if reds:
            red_refs = refs[o0 + n_out:n_ref]

            @pl.when(pl.program_id(0) == 0)
            def _():
                for r in red_refs:
                    r[...] = jnp.zeros_like(r)

            for r, v in zip(red_refs, vals[n_out:]):
                r[...] += v

    out_shape = [jax.ShapeDtypeStruct((s // d, c), dt) for c, dt, d in outs]
    out_shape += [jax.ShapeDtypeStruct((1, c), F32) for c in reds]
    out_specs = [pl.BlockSpec((tm // d, c), lambda i: (i, 0)) for c, _, d in outs]
    out_specs += [pl.BlockSpec((1, c), lambda i: (0, 0)) for c in reds]
    aliases = {}
    if into is not None:
        whole, k, cb = into
        out_shape[k] = jax.ShapeDtypeStruct(whole.shape, whole.dtype)
        if isinstance(cb, tuple):
            out_specs[k] = pl.BlockSpec((pl.Element(tm), pl.Element(outs[k][0])),
                                        functools.partial(lambda i, c0: (i * tm, c0), c0=cb[1]))
        else:
            out_specs[k] = pl.BlockSpec((tm, outs[k][0]), functools.partial(lambda i, cb: (i, cb), cb=cb))
        aliases = {n_in: k}
        arrs.append(whole)
        specs.append(pl.BlockSpec(memory_space=pl.ANY))
    res = pl.pallas_call(
        body, name=name, grid=(s // tm,), in_specs=specs, out_specs=out_specs, out_shape=out_shape,
        scratch_shapes=list(scratch), input_output_aliases=aliases,
        compiler_params=_params(("arbitrary",) if reds else ("parallel",)),
    )(*arrs)
    return res


def _to_class(x, scr, d):
    if d == 1:
        return x.astype(F32)
    tm, c = x.shape
    for g in range(c // 128):
        scr[g][...] = x[:, g * 128:(g + 1) * 128].astype(F32)
    return jnp.concatenate([scr[g][pl.ds(r, tm // d, stride=d), :] for r in range(d) for g in range(c // 128)],
                           axis=1)


def _from_class(x, scr, d):
    if d == 1:
        return x.astype(F32)
    n, dc = x.shape
    c = dc // d
    for r in range(d):
        for g in range(c // 128):
            scr[g][pl.ds(r, n, stride=d), :] = x[:, r * c + g * 128:r * c + (g + 1) * 128].astype(F32)
    return jnp.concatenate([scr[g][...] for g in range(c // 128)], axis=1)


def _mm(name, a, b, mode, out_dtype, tm=2048, tn=1024, tk=1024, side=None):
    if mode == "nn":
        (m, k), (_, n) = a.shape, b.shape
    elif mode == "nt":
        (m, k), (n, _) = a.shape, b.shape
    else:
        (k, m), (_, n) = a.shape, b.shape
    tm, tn, tk = min(tm, m), min(tn, n), min(tk, k)
    nk = k // tk
    grid = (m // tm, n // tn, nk)
    dn = {"nn": NN, "nt": NT, "tn": TN}[mode]
    n_si = len(side["ins"]) if side else 0
    n_so = len(side["outs"]) if side else 0
    n_acc = 1 if nk > 1 else 0

    def body(*refs):
        a_ref, b_ref = refs[:2]
        side_in, o_ref = refs[2:2 + n_si], refs[2 + n_si]
        side_out = refs[3 + n_si:3 + n_si + n_so]
        acc = refs[3 + n_si + n_so:3 + n_si + n_so + n_acc]
        side_scratch = refs[3 + n_si + n_so + n_acc:]
        step = (pl.program_id(0) * grid[1] + pl.program_id(1)) * grid[2] + pl.program_id(2)
        if side:
            @pl.when(step == 0)
            def _():
                side["start"](side_in, side_out, side_scratch)

        part = _dot(a_ref[...].astype(BF16), b_ref[...].astype(BF16), dn)
        if nk == 1:
            o_ref[...] = part.astype(o_ref.dtype)
        else:
            kk = pl.program_id(2)

            @pl.when(kk == 0)
            def _():
                acc[0][...] = part

            @pl.when(kk > 0)
            def _():
                acc[0][...] += part

            @pl.when(kk == nk - 1)
            def _():
                o_ref[...] = acc[0][...].astype(o_ref.dtype)

        if side:
            @pl.when(step == grid[0] * grid[1] * grid[2] - 1)
            def _():
                side["wait"](side_in, side_out, side_scratch)

    a_spec = (pl.BlockSpec((tk, tm), lambda i, j, kk: (kk, i)) if mode == "tn"
              else pl.BlockSpec((tm, tk), lambda i, j, kk: (i, kk)))
    b_spec = (pl.BlockSpec((tn, tk), lambda i, j, kk: (j, kk)) if mode == "nt"
              else pl.BlockSpec((tk, tn), lambda i, j, kk: (kk, j)))
    o_spec = pl.BlockSpec((tm, tn), lambda i, j, kk: (i, j))
    o_shape = jax.ShapeDtypeStruct((m, n), out_dtype)
    acc_scratch = [pltpu.VMEM((tm, tn), F32)] * n_acc
    if not side:
        return pl.pallas_call(
            body, name=name, grid=grid, in_specs=[a_spec, b_spec], out_specs=o_spec, out_shape=o_shape,
            scratch_shapes=acc_scratch, compiler_params=_params(("parallel", "parallel", "arbitrary")),
        )(a, b)
    return pl.pallas_call(
        body, name=name, grid=grid, in_specs=[a_spec, b_spec] + [HBM_SPEC] * n_si,
        out_specs=[o_spec] + [HBM_SPEC] * n_so, out_shape=[o_shape] + side["outs"],
        scratch_shapes=acc_scratch + side["scratch"],
        compiler_params=_params(("arbitrary", "arbitrary", "arbitrary")),
    )(a, b, *side["ins"])


def _rms_fwd(name, x, g):
    def fn(xv, gv):
        r = lax.rsqrt(jnp.mean(xv * xv, axis=-1, keepdims=True) + EPS)
        return (xv * r * gv,)
    return _rows(name, fn, [x], [g], [(x.shape[1], BF16)], tm=min(512, x.shape[0]))[0]


def _rms_fwd_both(name, x, g):
    s, dm = x.shape
    tm = min(512, s)

    def body(x_ref, g_ref, h_ref, ht_ref):
        xv = x_ref[...]
        hv = xv * lax.rsqrt(jnp.mean(xv * xv, axis=-1, keepdims=True) + EPS) * g_ref[...]
        h_ref[...] = hv.astype(BF16)
        ht_ref[...] = hv.T.astype(BF16)

    return pl.pallas_call(
        body, name=name, grid=(s // tm,),
        in_specs=[pl.BlockSpec((tm, dm), lambda i: (i, 0)), pl.BlockSpec((1, dm), lambda i: (0, 0))],
        out_specs=[pl.BlockSpec((tm, dm), lambda i: (i, 0)), pl.BlockSpec((dm, tm), lambda i: (0, i))],
        out_shape=[jax.ShapeDtypeStruct((s, dm), BF16), jax.ShapeDtypeStruct((dm, s), BF16)],
        compiler_params=_params(("parallel",)),
    )(x, g)


def _rope_tables(pos, inv):
    ang = pos.astype(F32) * inv
    lane = lax.broadcasted_iota(jnp.int32, ang.shape, 1)
    c = jnp.where(lane < ROT_DIM, jnp.cos(ang), 1.0)
    sn = jnp.sin(ang)
    sg = jnp.where(lane < ROT_DIM // 2, -sn, jnp.where(lane < ROT_DIM, sn, 0.0))
    return c, sg, lane


def _rope_apply(x, c, sg, lane):
    outs = []
    for h in range(x.shape[1] // HD):
        xh = x[:, h * HD:(h + 1) * HD].astype(F32)
        swap = jnp.where(lane < ROT_DIM // 2, pltpu.roll(xh, HD - ROT_DIM // 2, 1),
                         pltpu.roll(xh, ROT_DIM // 2, 1))
        outs.append(xh * c + swap * sg)
    return jnp.concatenate(outs, axis=1)


ROPE_TM = 512


def _class_scratch(tm):
    return [pltpu.VMEM((tm, 128), F32) for _ in range(A_GROUP // 128)]


def _rope_fwd(u, pos, inv):
    def fn(q, k, v, p, iv, *scr):
        c, sg, lane = _rope_tables(p, iv)
        qr, kr = _rope_apply(q, c, sg, lane), _rope_apply(k, c, sg, lane)
        outs = []
        for g, d in enumerate(DILATIONS):
            gs = slice(g * A_GROUP, (g + 1) * A_GROUP)
            outs += [_to_class(qr[:, gs], scr, d), _to_class(kr[:, gs], scr, d), _to_class(v[:, gs], scr, d)]
        return tuple(outs)

    outs = [(d * A_GROUP, BF16, d) for d in DILATIONS for _ in range(3)]
    qkv = [(u, 3 * A_GROUP, c0 // (3 * A_GROUP)) for c0 in (C_QA, C_KA, C_VA)]
    return _rows("rope_fwd", fn, qkv + [pos], [inv], outs, tm=ROPE_TM,
                 scratch=_class_scratch(ROPE_TM))


def _rope_bwd(dqs, dks, dvs, pos, inv, du):
    def fn(*args):
        grads, p, iv, scr = args[:9], args[9], args[10], args[11:]
        c, sg, lane = _rope_tables(p, iv)
        tok = [jnp.concatenate([_from_class(grads[3 * k + g], scr, d) for g, d in enumerate(DILATIONS)], axis=1)
               for k in range(3)]
        return (jnp.concatenate([_rope_apply(tok[0], c, -sg, lane), _rope_apply(tok[1], c, -sg, lane), tok[2]],
                                axis=1),)

    ins = [(a, a.shape[1], 0, d) for grp in (dqs, dks, dvs) for a, d in zip(grp, DILATIONS)]
    return _rows("rope_bwd", fn, ins + [pos], [inv], [(9 * A_GROUP, BF16)], tm=ROPE_TM,
                 scratch=_class_scratch(ROPE_TM), into=(du, 0, 0))[0]


def _lane_pack(cols, like):
    lane = lax.broadcasted_iota(jnp.int32, like, 1)
    out = jnp.zeros(like, F32)
    for h, cvec in enumerate(cols):
        out = jnp.where(lane == h, cvec, out)
    return out


def _band_specs(l, d, tq):
    nsb = tq // BAND
    nblk = l // BAND
    cur = pl.BlockSpec((tq, A_GROUP), lambda r, i: (i, r))
    prev = pl.BlockSpec((BAND, A_GROUP), lambda r, i: (jnp.maximum(i * nsb - 1, 0), r))
    nxt = pl.BlockSpec((BAND, A_GROUP), lambda r, i: (jnp.minimum((i + 1) * nsb, nblk - 1), r))
    st_cur = pl.BlockSpec((tq, HD), lambda r, i: (i, r))
    st_nxt = pl.BlockSpec((BAND, HD), lambda r, i: (jnp.minimum((i + 1) * nsb, nblk - 1), r))
    return nsb, cur, prev, nxt, st_cur, st_nxt


def _band_mask_q(i, first_tile):
    qr = lax.broadcasted_iota(jnp.int32, (BAND, 2 * BAND), 0)
    kc = lax.broadcasted_iota(jnp.int32, (BAND, 2 * BAND), 1)
    in_prev = (kc < BAND) & (kc >= qr)
    in_cur = (kc >= BAND) & (kc - BAND <= qr)
    if i == 0:
        in_prev = in_prev & jnp.logical_not(first_tile)
    return in_prev | in_cur


def _band_mask_k(j, nsb, last_tile):
    kc = lax.broadcasted_iota(jnp.int32, (BAND, 2 * BAND), 0)
    qr = lax.broadcasted_iota(jnp.int32, (BAND, 2 * BAND), 1)
    same = (qr < BAND) & (kc <= qr)
    nxt = (qr >= BAND) & (kc >= qr - BAND)
    if j == nsb - 1:
        nxt = nxt & jnp.logical_not(last_tile)
    return same | nxt


def _band_fwd(name, q, k, v, d):
    l = q.shape[0]
    tq = min(512, l)
    nsb, cur, prev, _, st_cur, _ = _band_specs(l, d, tq)
    scale = HD ** -0.5

    def body(q_ref, kc_ref, kp_ref, vc_ref, vp_ref, o_ref, lse_ref):
        first = pl.program_id(1) == 0
        for i in range(nsb):
            lses = []
            mask = _band_mask_q(i, first)
            for h in range(4):
                cs = slice(h * HD, (h + 1) * HD)
                qv = q_ref[i * BAND:(i + 1) * BAND, cs]
                if i == 0:
                    kk = jnp.concatenate([kp_ref[:, cs], kc_ref[0:BAND, cs]], axis=0)
                    vv = jnp.concatenate([vp_ref[:, cs], vc_ref[0:BAND, cs]], axis=0)
                else:
                    kk = kc_ref[(i - 1) * BAND:(i + 1) * BAND, cs]
                    vv = vc_ref[(i - 1) * BAND:(i + 1) * BAND, cs]
                s = jnp.where(mask, _dot(qv, kk, NT) * scale, NEG)
                m = jnp.max(s, axis=-1, keepdims=True)
                p = jnp.exp(s - m)
                den = jnp.sum(p, axis=-1, keepdims=True)
                o_ref[i * BAND:(i + 1) * BAND, cs] = _dot(p.astype(BF16), vv) / den
                lses.append(m + jnp.log(den))
            lse_ref[i * BAND:(i + 1) * BAND, :] = _lane_pack(lses, (BAND, HD))

    return pl.pallas_call(
        body, name=name, grid=(d, l // tq), in_specs=[cur, cur, prev, cur, prev],
        out_specs=[cur, st_cur],
        out_shape=[jax.ShapeDtypeStruct((l, d * A_GROUP), F32), jax.ShapeDtypeStruct((l, d * HD), F32)],
        compiler_params=_params(("parallel", "parallel")),
    )(q, k, k, v, v)


def _band_dq(name, q, k, v, dy, lse, delta, d):
    l = q.shape[0]
    tq = min(512, l)
    nsb, cur, prev, _, st_cur, _ = _band_specs(l, d, tq)
    scale = HD ** -0.5

    def body(q_ref, kc_ref, kp_ref, vc_ref, vp_ref, dy_ref, lse_ref, dl_ref, dq_ref):
        first = pl.program_id(1) == 0
        for i in range(nsb):
            mask = _band_mask_q(i, first)
            rs = slice(i * BAND, (i + 1) * BAND)
            for h in range(4):
                cs = slice(h * HD, (h + 1) * HD)
                if i == 0:
                    kk = jnp.concatenate([kp_ref[:, cs], kc_ref[0:BAND, cs]], axis=0)
                    vv = jnp.concatenate([vp_ref[:, cs], vc_ref[0:BAND, cs]], axis=0)
                else:
                    kk = kc_ref[(i - 1) * BAND:(i + 1) * BAND, cs]
                    vv = vc_ref[(i - 1) * BAND:(i + 1) * BAND, cs]
                s = jnp.where(mask, _dot(q_ref[rs, cs], kk, NT) * scale, NEG)
                p = jnp.exp(s - lse_ref[rs, h:h + 1])
                dp = _dot(dy_ref[rs, cs], vv, NT)
                ds = p * (dp - dl_ref[rs, h:h + 1])
                dq_ref[rs, cs] = (_dot(ds.astype(BF16), kk) * scale).astype(dq_ref.dtype)

    return pl.pallas_call(
        body, name=name, grid=(d, l // tq),
        in_specs=[cur, cur, prev, cur, prev, cur, st_cur, st_cur], out_specs=cur,
        out_shape=jax.ShapeDtypeStruct((l, d * A_GROUP), BF16),
        compiler_params=_params(("parallel", "parallel")),
    )(q, k, k, v, v, dy, lse, delta)


def _band_dkv(name, q, k, v, dy, lse, delta, d):
    l = q.shape[0]
    tq = min(512, l)
    nsb, cur, _, nxt, st_cur, st_nxt = _band_specs(l, d, tq)
    scale = HD ** -0.5
    ntile = l // tq

    def body(k_ref, v_ref, qc_ref, qn_ref, dyc_ref, dyn_ref, lc_ref, ln_ref, dc_ref, dn_ref,
             dk_ref, dv_ref):
        last = pl.program_id(1) == ntile - 1

        def win(c_ref, n_ref, j, cs):
            if j == nsb - 1:
                return jnp.concatenate([c_ref[j * BAND:(j + 1) * BAND, cs], n_ref[:, cs]], axis=0)
            return c_ref[j * BAND:(j + 2) * BAND, cs]

        allh = slice(0, HD)
        for j in range(nsb):
            mask = _band_mask_k(j, nsb, last)
            rs = slice(j * BAND, (j + 1) * BAND)
            lse_t = win(lc_ref, ln_ref, j, allh).T
            delta_t = win(dc_ref, dn_ref, j, allh).T
            for h in range(4):
                cs = slice(h * HD, (h + 1) * HD)
                qw = win(qc_ref, qn_ref, j, cs)
                dyw = win(dyc_ref, dyn_ref, j, cs)
                st = jnp.where(mask, _dot(k_ref[rs, cs], qw, NT) * scale, NEG)
                pt = jnp.exp(st - lse_t[h:h + 1, :])
                dst = pt * (_dot(v_ref[rs, cs], dyw, NT) - delta_t[h:h + 1, :])
                dv_ref[rs, cs] = _dot(pt.astype(BF16), dyw).astype(dv_ref.dtype)
                dk_ref[rs, cs] = (_dot(dst.astype(BF16), qw) * scale).astype(dk_ref.dtype)

    shp = jax.ShapeDtypeStruct((l, d * A_GROUP), BF16)
    return pl.pallas_call(
        body, name=name, grid=(d, ntile),
        in_specs=[cur, cur, cur, nxt, cur, nxt, st_cur, st_nxt, st_cur, st_nxt],
        out_specs=[cur, cur], out_shape=[shp, shp],
        compiler_params=_params(("parallel", "parallel")),
    )(k, v, q, q, dy, dy, lse, lse, delta, delta)


def _split3(x):
    hi = x.astype(BF16)
    r1 = x - hi.astype(F32)
    mid = r1.astype(BF16)
    lo = (r1 - mid.astype(F32)).astype(BF16)
    return hi, mid, lo


def _fox_prep(z, b):
    h, s = z.shape
    blk = min(512, s)

    def body(z_ref, b_ref, c_ref):
        r = lax.broadcasted_iota(jnp.int32, (blk, blk), 0)
        cidx = lax.broadcasted_iota(jnp.int32, (blk, blk), 1)
        tri = (r <= cidx).astype(BF16)
        carry = jnp.zeros((h, 1), F32)
        for t in range(s // blk):
            zz = z_ref[:, t * blk:(t + 1) * blk] + b_ref[...]
            lf = jnp.minimum(zz, 0.0) - jnp.log(1.0 + jnp.exp(-jnp.abs(zz)))
            hi, mid, lo = _split3(lf)
            cs = _dot(hi, tri) + _dot(mid, tri) + _dot(lo, tri) + carry
            c_ref[:, t * blk:(t + 1) * blk] = cs
            carry = cs[:, blk - 1:blk]

    return pl.pallas_call(body, name="fox_prep", out_shape=jax.ShapeDtypeStruct((h, s), F32))(z, b)


def _fox_prep_bwd(dc, z, b):
    h, s = z.shape
    blk = min(512, s)

    def body(dc_ref, z_ref, b_ref, dz_ref, db_ref):
        r = lax.broadcasted_iota(jnp.int32, (blk, blk), 0)
        cidx = lax.broadcasted_iota(jnp.int32, (blk, blk), 1)
        tri = (r >= cidx).astype(BF16)
        carry = jnp.zeros((h, 1), F32)
        tot = jnp.zeros((h, 1), F32)
        for t in reversed(range(s // blk)):
            hi, mid, lo = _split3(dc_ref[:, t * blk:(t + 1) * blk])
            rc = _dot(hi, tri) + _dot(mid, tri) + _dot(lo, tri) + carry
            carry = rc[:, 0:1]
            zz = z_ref[:, t * blk:(t + 1) * blk] + b_ref[...]
            dz = rc * _sig(-zz)
            dz_ref[:, t * blk:(t + 1) * blk] = dz
            tot = tot + jnp.sum(dz, axis=-1, keepdims=True)
        db_ref[...] = tot

    return pl.pallas_call(
        body, name="fox_prep_bwd",
        out_shape=[jax.ShapeDtypeStruct((h, s), F32), jax.ShapeDtypeStruct((h, 1), F32)])(dc, z, b)


FOX_W = 128
FOX_C = B_HD
FOX_ONE = B_HD + 3
FOX_SUB = 256
FOX_SUB_FWD = 128
FOX_HEADS_PER_STEP = 2


def _head_of_pair(x, hh):
    return x if hh == 0 else pltpu.roll(x, B_HD, 1)


def _fox_pack(u, c_col, t):
    s = u.shape[0]
    nt = s // t
    scale = B_HD ** -0.5

    def body(q_ref, k_ref, v_ref, c_ref, qf_ref, kb_ref, ks_ref, vb_ref, vt_ref):
        lane = lax.broadcasted_iota(jnp.int32, (t, FOX_W), 1)
        for hd in range(B_HEADS):
            pair, hh = slice(hd // 2 * FOX_W, (hd // 2 + 1) * FOX_W), hd % 2
            qv, kv, vv = [r[:, pair].astype(F32) for r in (q_ref, k_ref, v_ref)]
            qf_ref[hd] = jnp.where(lane < B_HD, _head_of_pair(qv, hh), B_HD ** 0.5).astype(BF16)
            neg = c_ref[hd] * (-scale)
            hi = neg.astype(BF16).astype(F32)
            mid = (neg - hi).astype(BF16).astype(F32)
            lo = neg - hi - mid
            aux = jnp.where(lane == FOX_C, hi,
                            jnp.where(lane == FOX_C + 1, mid, jnp.where(lane == FOX_C + 2, lo, 0.0)))
            kb = jnp.where(lane < B_HD, _head_of_pair(kv, hh) * scale, aux)
            kb_ref[hd] = kb.astype(BF16)
            ks_ref[hd] = jnp.where(lane == FOX_ONE, 1.0, kb).T.astype(BF16)
            vb = jnp.where(lane < B_HD, _head_of_pair(vv, hh), 1.0)
            vb_ref[hd] = vb.astype(BF16)
            vt_ref[hd] = vb.T.astype(BF16)

    def tok(col0):
        return pl.BlockSpec((t, B_HEADS * B_HD), functools.partial(lambda i, cb: (i, cb), cb=col0 // (B_HEADS * B_HD)))

    rows = pl.BlockSpec((B_HEADS, t, FOX_W), lambda i: (0, i, 0))
    tiles = pl.BlockSpec((B_HEADS, None, FOX_W, t), lambda i: (0, i, 0, 0))
    hm = jax.ShapeDtypeStruct((B_HEADS, s, FOX_W), BF16)
    tt = jax.ShapeDtypeStruct((B_HEADS, nt, FOX_W, t), BF16)
    return pl.pallas_call(
        body, name="fox_pack", grid=(nt,),
        in_specs=[tok(C_QB), tok(C_KB), tok(C_VB), pl.BlockSpec((B_HEADS, t, 1), lambda i: (0, i, 0))],
        out_specs=[rows, rows, tiles, rows, tiles], out_shape=[hm, hm, tt, hm, tt],
        compiler_params=_params(("parallel",)),
    )(u, u, u, c_col)


def _fox_pack_bwd(dy, y, t):
    s = dy.shape[0]
    nt = s // t

    def body(do_ref, o_ref, dow_ref, dl_ref):
        lane = lax.broadcasted_iota(jnp.int32, (t, FOX_W), 1)
        lane8 = lax.broadcasted_iota(jnp.int32, (8, FOX_W), 1)
        for pr in range(B_HEADS // 2):
            pair = slice(pr * FOX_W, (pr + 1) * FOX_W)
            dov = do_ref[:, pair].astype(F32)
            parts = _split3(dov * o_ref[:, pair].astype(F32))
            for hh in range(2):
                dow_ref[2 * pr + hh] = jnp.where(lane < B_HD, _head_of_pair(dov, hh), 0.0).astype(BF16)
                mask = ((lane8 >= hh * B_HD) & (lane8 < (hh + 1) * B_HD)).astype(BF16)
                row = _dot(mask, parts[0], NT) + _dot(mask, parts[1], NT) + _dot(mask, parts[2], NT)
                dl_ref[2 * pr + hh] = row[0:1, :]

    tok = pl.BlockSpec((t, B_HEADS * B_HD), lambda i: (i, 0))
    return pl.pallas_call(
        body, name="fox_pack_bwd", grid=(nt,), in_specs=[tok, tok],
        out_specs=[pl.BlockSpec((B_HEADS, t, FOX_W), lambda i: (0, i, 0)),
                   pl.BlockSpec((B_HEADS, None, 1, t), lambda i: (0, i, 0, 0))],
        out_shape=[jax.ShapeDtypeStruct((B_HEADS, s, FOX_W), BF16), jax.ShapeDtypeStruct((B_HEADS, nt, 1, t), F32)],
        compiler_params=_params(("parallel",)),
    )(dy, y)


def _fox_unpack(dqt, dkw, dvw, du, t):
    h, nt = dqt.shape[:2]

    def body(dq_ref, dk_ref, dv_ref, _, o_ref, dc_ref):
        lane = lax.broadcasted_iota(jnp.int32, (t, FOX_W), 1)

        def join(a0, a1):
            return jnp.where(lane < B_HD, a0, pltpu.roll(a1, B_HD, 1))

        for hh in range(h):
            dc_ref[hh] = dq_ref[hh][FOX_ONE:FOX_ONE + 1, :] - dk_ref[hh].T[B_HD:B_HD + 1, :]
        pairs = range(0, h, 2)
        cols = ([join(dq_ref[a].T, dq_ref[a + 1].T) for a in pairs] + [join(dk_ref[a], dk_ref[a + 1]) for a in pairs]
                + [join(dv_ref[a], dv_ref[a + 1]) for a in pairs])
        o_ref[...] = jnp.concatenate(cols, axis=1).astype(o_ref.dtype)

    rows = pl.BlockSpec((h, t, FOX_W), lambda i: (0, i, 0))
    return pl.pallas_call(
        body, name="fox_unpack", grid=(nt,),
        in_specs=[pl.BlockSpec((h, None, FOX_W, t), lambda i: (0, i, 0, 0)), rows, rows,
                  pl.BlockSpec(memory_space=pl.ANY)],
        out_specs=[pl.BlockSpec((pl.Element(t), pl.Element(3 * h * B_HD)), lambda i: (i * t, C_QB)),
                   pl.BlockSpec((h, None, 1, t), lambda i: (0, i, 0, 0))],
        out_shape=[jax.ShapeDtypeStruct(du.shape, du.dtype), jax.ShapeDtypeStruct((h, nt, 1, t), F32)],
        input_output_aliases={3: 0},
        compiler_params=_params(("parallel",)),
    )(dqt, dkw, dvw, du)


FOX_DEAD = -110.0


def _fox_norm2(qf, kb):
    h, s, w = qf.shape
    tm = min(2048, s)

    def body(q_ref, k_ref, qo_ref, ko_ref):
        row = lax.broadcasted_iota(jnp.int32, (w, w), 0)
        ones = (row < B_HD).astype(BF16)
        for x_ref, o_ref in ((q_ref, qo_ref), (k_ref, ko_ref)):
            xv = x_ref[...].astype(F32)
            n2 = _dot((xv * xv).astype(BF16), ones)
            o_ref[...] = jnp.broadcast_to(jnp.max(n2, axis=0, keepdims=True)[:, :1], o_ref.shape)

    spec = pl.BlockSpec((None, tm, w), lambda hh, i: (hh, i, 0))
    ospec = pl.BlockSpec((None, None, 8, 128), lambda hh, i: (hh, i, 0, 0))
    shp = jax.ShapeDtypeStruct((h, s // tm, 8, 128), F32)
    return pl.pallas_call(
        body, name="fox_norm2", grid=(h, s // tm), in_specs=[spec, spec], out_specs=[ospec, ospec],
        out_shape=[shp, shp], compiler_params=_params(("parallel", "parallel")),
    )(qf, kb)


def _fox_bounds(qf, kb, c, t):
    q2, k2 = _fox_norm2(qf, kb)
    g = 2.0 * jnp.sqrt(1.02 * jnp.max(q2[:, :, 0, 0], axis=1) * 1.02 * jnp.max(k2[:, :, 0, 0], axis=1))
    return jnp.concatenate([c[:, ::t], c[:, t - 1::t], g[:, None]], axis=1)


SMEM_SPEC = pl.BlockSpec(memory_space=pltpu.SMEM)


def _fox_fwd(qf, kb, vt4, bounds, t):
    h, s, w = qf.shape
    nt = s // t
    sub = FOX_SUB_FWD
    nsub = t // sub
    nh = FOX_HEADS_PER_STEP

    def body(b_ref, q_ref, k_ref, v_ref, o_ref, lse_ref):
        i = pl.program_id(1)
        krow = lax.broadcasted_iota(jnp.int32, (sub, t), 0)
        qcol = lax.broadcasted_iota(jnp.int32, (sub, t), 1)

        def dead_before(hh):
            head = pl.program_id(0) * nh + hh
            top = b_ref[head, 2 * nt] + b_ref[head, i]
            return lax.fori_loop(
                0, i, lambda jj, n: n + (top - b_ref[head, nt + jj] < FOX_DEAD).astype(jnp.int32), 0)

        j_lo = functools.reduce(jnp.minimum, [dead_before(hh) for hh in range(nh)])

        def tile(j, carry, diag):
            out = []
            for hh in range(nh):
                m, acc = carry[hh]
                qv, vj = q_ref[hh], v_ref[hh, j]
                los = [b * sub if diag else 0 for b in range(nsub)]
                sts = [_dot(k_ref[hh, pl.ds(pl.multiple_of(j * t + b * sub, sub), sub), :], qv[lo:, :], NT)
                       for b, lo in enumerate(los)]
                for b, lo in enumerate(los):
                    st = sts[b]
                    if diag:
                        st = jnp.where(krow[:, :t - lo] <= qcol[:, :t - lo], st, NEG)
                    m_old, acc_old = m[:, lo:], acc[:, lo:]
                    m2 = jnp.maximum(m_old, jnp.max(st, axis=0, keepdims=True))
                    p = jnp.exp(st - m2).astype(BF16)
                    acc2 = jnp.exp(m_old - m2) * acc_old + _dot(vj[:, b * sub:(b + 1) * sub], p)
                    m = m2 if lo == 0 else jnp.concatenate([m[:, :lo], m2], axis=1)
                    acc = acc2 if lo == 0 else jnp.concatenate([acc[:, :lo], acc2], axis=1)
                out.append((m, acc))
            return tuple(out)

        init = tuple((jnp.full((1, t), NEG, F32), jnp.zeros((w, t), F32)) for _ in range(nh))
        carry = lax.fori_loop(j_lo, i, lambda j, c: tile(j, c, False), init)
        outs = []
        for hh, (m, acc) in enumerate(tile(i, carry, True)):
            den = acc[B_HD:B_HD + 1, :]
            outs.append(acc[0:B_HD, :] / den)
            lse_ref[hh] = m + jnp.log(den)
        o_ref[...] = jnp.concatenate(outs, axis=0).T.astype(o_ref.dtype)

    return pl.pallas_call(
        body, name="fox_fwd", grid=(h // nh, nt),
        in_specs=[SMEM_SPEC,
                  pl.BlockSpec((nh, t, w), lambda hh, i: (hh, i, 0)),
                  pl.BlockSpec((nh, s, w), lambda hh, i: (hh, 0, 0)),
                  pl.BlockSpec((nh, nt, w, t), lambda hh, i: (hh, 0, 0, 0))],
        out_specs=[pl.BlockSpec((t, nh * B_HD), lambda hh, i: (i, hh)),
                   pl.BlockSpec((nh, 1, t), lambda hh, i: (hh, 0, i))],
        out_shape=[jax.ShapeDtypeStruct((s, h * B_HD), BF16), jax.ShapeDtypeStruct((h, 1, s), F32)],
        compiler_params=_params(("parallel", "parallel")),
    )(bounds, qf, kb, vt4)


def _fox_bwd(qf, dow, lse_row, delta_row, kb, kst4, vb, bounds, t):
    h, s, w = qf.shape
    nt = s // t
    nsub = t // FOX_SUB
    nh = FOX_HEADS_PER_STEP

    def body(b_ref, q_ref, do_ref, lse_ref, dl_ref, k_ref, kt_ref, v_ref, dqt_ref, dk_ref, dv_ref, dk_acc, dv_acc):
        j = pl.program_id(1)

        def alive_after(hh):
            head = pl.program_id(0) * nh + hh
            top = b_ref[head, 2 * nt] - b_ref[head, nt + j]
            return lax.fori_loop(
                j + 1, nt, lambda ii, n: n + (top + b_ref[head, ii] >= FOX_DEAD).astype(jnp.int32), 0)

        i_hi = j + 1 + functools.reduce(jnp.maximum, [alive_after(hh) for hh in range(nh)])

        @pl.when(j == 0)
        def _():
            dqt_ref[...] = jnp.zeros_like(dqt_ref)

        dk_acc[...] = jnp.zeros_like(dk_acc)
        dv_acc[...] = jnp.zeros_like(dv_acc)
        krow = lax.broadcasted_iota(jnp.int32, (FOX_SUB, t), 0)
        qcol = lax.broadcasted_iota(jnp.int32, (FOX_SUB, t), 1)
        subs = [slice(b * FOX_SUB, (b + 1) * FOX_SUB) for b in range(nsub)]

        def tile(i, diag):
            i0 = pl.multiple_of(i * t, t)
            for hh in range(nh):
                qi, doi = q_ref[hh, pl.ds(i0, t), :], do_ref[hh, pl.ds(i0, t), :]
                lse, dl = lse_ref[hh, i], dl_ref[hh, i]
                los = [b * FOX_SUB if diag else 0 for b in range(nsub)]
                sts = [_dot(k_ref[hh, rs, :], qi[lo:, :], NT) for rs, lo in zip(subs, los)]
                dps = [_dot(v_ref[hh, rs, :], doi[lo:, :], NT) for rs, lo in zip(subs, los)]
                dq = None
                for b, (rs, lo) in enumerate(zip(subs, los)):
                    st = sts[b] - lse[:, lo:]
                    if diag:
                        st = jnp.where(krow[:, :t - lo] <= qcol[:, :t - lo], st, NEG)
                    pt = jnp.exp(st)
                    dsb = (pt * (dps[b] - dl[:, lo:])).astype(BF16)
                    dv_acc[hh, rs, :] += _dot(pt.astype(BF16), doi[lo:, :])
                    dk_acc[hh, rs, :] += _dot(dsb, qi[lo:, :])
                    part = _dot(kt_ref[hh, :, rs], dsb)
                    if lo:
                        part = jnp.concatenate([jnp.zeros((w, lo), F32), part], axis=1)
                    dq = part if dq is None else dq + part
                dqt_ref[hh, i] += dq

        def step(i, carry):
            tile(i, False)
            return carry

        tile(j, True)
        lax.fori_loop(j + 1, i_hi, step, 0)
        dk_ref[...] = dk_acc[...] * (B_HD ** -0.5)
        dv_ref[...] = dv_acc[...]

    full = pl.BlockSpec((nh, s, w), lambda hh, j: (hh, 0, 0))
    rowst = pl.BlockSpec((nh, nt, 1, t), lambda hh, j: (hh, 0, 0, 0))
    tl = pl.BlockSpec((nh, t, w), lambda hh, j: (hh, j, 0))
    return pl.pallas_call(
        body, name="fox_bwd", grid=(h // nh, nt),
        in_specs=[SMEM_SPEC, full, full, rowst, rowst, tl,
                  pl.BlockSpec((nh, None, w, t), lambda hh, j: (hh, j, 0, 0)), tl],
        out_specs=[pl.BlockSpec((nh, nt, w, t), lambda hh, j: (hh, 0, 0, 0)), tl, tl],
        out_shape=[jax.ShapeDtypeStruct((h, nt, w, t), F32), jax.ShapeDtypeStruct((h, s, w), F32),
                   jax.ShapeDtypeStruct((h, s, w), F32)],
        scratch_shapes=[pltpu.VMEM((nh, t, w), F32), pltpu.VMEM((nh, t, w), F32)],
        compiler_params=_params(("parallel", "arbitrary")),
    )(bounds, qf, dow, lse_row, delta_row, kb, kst4, vb)


def _mem_fwd(u, mkv, tq=1024):
    s = u.shape[0]
    scale = HD ** -0.5

    def body(q_ref, mk_ref, mv_ref, o_ref, lse_ref):
        lses = []
        for h in range(4):
            cs = slice(h * HD, (h + 1) * HD)
            sc = _dot(q_ref[:, cs], mk_ref[:, cs], NT) * scale
            m = jnp.max(sc, axis=-1, keepdims=True)
            p = jnp.exp(sc - m)
            den = jnp.sum(p, axis=-1, keepdims=True)
            o_ref[:, cs] = (_dot(p.astype(BF16), mv_ref[:, cs]) / den).astype(o_ref.dtype)
            lses.append(m + jnp.log(den))
        lse_ref[...] = _lane_pack(lses, (tq, HD))

    return pl.pallas_call(
        body, name="mem_fwd", grid=(s // tq,),
        in_specs=[pl.BlockSpec((tq, 512), lambda i: (i, C_QM // 512)),
                  pl.BlockSpec((N_MEM, 512), lambda i: (0, 0)),
                  pl.BlockSpec((N_MEM, 512), lambda i: (0, 1))],
        out_specs=[pl.BlockSpec((tq, 512), lambda i: (i, 0)), pl.BlockSpec((tq, HD), lambda i: (i, 0))],
        out_shape=[jax.ShapeDtypeStruct((s, 512), BF16), jax.ShapeDtypeStruct((s, HD), F32)],
        compiler_params=_params(("parallel",)),
    )(u, mkv, mkv)


def _mem_bwd(u, mkv, o, do, lse, du, tq=1024):
    s = u.shape[0]
    scale = HD ** -0.5

    def body(q_ref, mk_ref, mv_ref, o_ref, do_ref, lse_ref, _, dq_ref, dmk_ref, dmv_ref):
        @pl.when(pl.program_id(0) == 0)
        def _():
            dmk_ref[...] = jnp.zeros_like(dmk_ref)
            dmv_ref[...] = jnp.zeros_like(dmv_ref)

        for h in range(4):
            cs = slice(h * HD, (h + 1) * HD)
            qv, dov = q_ref[:, cs], do_ref[:, cs]
            sc = _dot(qv, mk_ref[:, cs], NT) * scale
            p = jnp.exp(sc - lse_ref[:, h:h + 1])
            delta = jnp.sum(dov.astype(F32) * o_ref[:, cs].astype(F32), axis=-1, keepdims=True)
            ds = p * (_dot(dov, mv_ref[:, cs], NT) - delta)
            dsb = ds.astype(BF16)
            dq_ref[:, cs] = (_dot(dsb, mk_ref[:, cs]) * scale).astype(dq_ref.dtype)
            dmk_ref[:, cs] += _dot(dsb, qv, TN) * scale
            dmv_ref[:, cs] += _dot(p.astype(BF16), dov, TN)

    row = pl.BlockSpec((tq, 512), lambda i: (i, 0))
    acc = pl.BlockSpec((N_MEM, 512), lambda i: (0, 0))
    return pl.pallas_call(
        body, name="mem_bwd", grid=(s // tq,),
        in_specs=[pl.BlockSpec((tq, 512), lambda i: (i, C_QM // 512)),
                  pl.BlockSpec((N_MEM, 512), lambda i: (0, 0)),
                  pl.BlockSpec((N_MEM, 512), lambda i: (0, 1)),
                  row, row, pl.BlockSpec((tq, HD), lambda i: (i, 0)), pl.BlockSpec(memory_space=pl.ANY)],
        out_specs=[pl.BlockSpec((tq, 512), lambda i: (i, C_QM // 512)), acc, acc],
        out_shape=[jax.ShapeDtypeStruct(du.shape, du.dtype), jax.ShapeDtypeStruct((N_MEM, 512), F32),
                   jax.ShapeDtypeStruct((N_MEM, 512), F32)],
        input_output_aliases={6: 0},
        compiler_params=_params(("arbitrary",)),
    )(u, mkv, mkv, o, do, lse, du)


FB_CHIP = FB_ORIG // SHARD_COLS
FB_AT = FB_ORIG - FB_CHIP * SHARD_COLS


def _chip_slabs(main, fb):
    cuts = [SHARD_COLS * p - (B_HEADS if p > FB_CHIP else 0) for p in range(N_CHIPS + 1)]
    slabs = [main[:, a:b] for a, b in zip(cuts[:-1], cuts[1:])]
    own = slabs[FB_CHIP]
    slabs[FB_CHIP] = jnp.concatenate([own[:, :FB_AT], fb, own[:, FB_AT:]], axis=1)
    return slabs


def _split_forget(slabs):
    own = slabs[FB_CHIP]
    parts = list(slabs[:FB_CHIP]) + [own[:, :FB_AT], own[:, FB_AT + B_HEADS:]] + list(slabs[FB_CHIP + 1:])
    return jnp.concatenate(parts, axis=1), own[:, FB_AT:FB_AT + B_HEADS]


def _local_step(x, mem, pos, target, g_pre, g_post, g_mem, w_main, w_fb, b_forget, b_merge,
                w_mem_kv, w_ba, w_bb, w_bm, w_out, exchange=None):
    s = x.shape[0]
    t_fox = min(512, s)
    nt = s // t_fox
    half = ROT_DIM // 2
    inv = ROPE_THETA ** (-jnp.arange(half, dtype=F32) / half)
    inv128 = jnp.concatenate([inv, inv, jnp.zeros((HD - ROT_DIM,), F32)]).reshape(1, HD)

    h, h_t = _rms_fwd_both("norm_pre", x, g_pre)
    u = _mm("proj_in", h, w_main, "nn", BF16, tm=4096)
    ufb = _mm("proj_fb", h, w_fb, "nn", F32)
    memn = _rms_fwd("norm_mem", mem, g_mem)
    mkv = _mm("proj_mem", memn, w_mem_kv, "nn", BF16)

    qkv = _rope_fwd(u, pos, inv128)
    views = [tuple(qkv[3 * g:3 * g + 3]) for g in range(3)]
    os_, lses = [], []
    for g, d in enumerate(DILATIONS):
        o_g, lse_g = _band_fwd("band_fwd%d" % g, *views[g], d)
        os_.append((o_g, d * A_GROUP, 0, d))
        lses.append((lse_g, d * HD, 0, d))

    def merge_a(o1, o2, o3, l1, l2, l3, za, *scr):
        o1, o2, o3 = [_from_class(o, scr, d) for o, d in zip((o1, o2, o3), DILATIONS)]
        l1, l2, l3 = [_from_class(lv, scr, d) for lv, d in zip((l1, l2, l3), DILATIONS)]
        ys, tots = [], []
        for hh in range(4):
            cs, hs = slice(hh * HD, (hh + 1) * HD), slice(hh, hh + 1)
            mx = jnp.maximum(jnp.maximum(l1[:, hs], l2[:, hs]), l3[:, hs])
            e1, e2, e3 = jnp.exp(l1[:, hs] - mx), jnp.exp(l2[:, hs] - mx), jnp.exp(l3[:, hs] - mx)
            den = e1 + e2 + e3
            ys.append((e1 * o1[:, cs] + e2 * o2[:, cs] + e3 * o3[:, cs]) / den)
            tots.append(mx + jnp.log(den))
        y = jnp.concatenate(ys, axis=1)
        zf = za.astype(F32)
        tot = _lane_pack(tots, l1.shape)
        return (y, y * (zf * _sig(zf))) + tuple(_to_class(tot, scr, d) for d in DILATIONS)

    res = _rows("merge_a", merge_a, os_ + lses + [(u, 512, C_ZA // 512)], [],
                [(512, BF16), (512, BF16)] + [(d * HD, F32, d) for d in DILATIONS], tm=ROPE_TM,
                scratch=_class_scratch(ROPE_TM))
    y_a, yg_a, lse_a = res[0], res[1], res[2:5]

    zrow = ufb[:, :B_HEADS].T
    c = _fox_prep(zrow, b_forget.reshape(B_HEADS, 1))
    qf, kb, kst4, vb, vt4 = _fox_pack(u, c.reshape(B_HEADS, s, 1), t_fox)
    bounds = _fox_bounds(qf, kb, c, t_fox)
    y_b, lse_b = _fox_fwd(qf, kb, vt4, bounds, t_fox)

    y_m, lse_m = _mem_fwd(u, mkv)

    def gate(y, z):
        zf = z.astype(F32)
        return (y.astype(F32) * (zf * _sig(zf)),)

    yg_b = _rows("gate_b", gate, [y_b, (u, 512, C_ZB // 512)], [], [(512, BF16)])[0]
    yg_m = _rows("gate_m", gate, [y_m, (u, 512, C_ZM // 512)], [], [(512, BF16)])[0]

    br_a = _mm("branch_a", yg_a, w_ba, "nn", BF16)
    br_b = _mm("branch_b", yg_b, w_bb, "nn", BF16)
    br_m = _mm("branch_m", yg_m, w_bm, "nn", BF16)
    gl = [(u, 1024, C_GL // 1024 + i) for i in range(3)]
    bm3 = b_merge.reshape(3, D_MODEL)

    def merge(g0, g1, g2, b0, b1, b2, bm):
        tot = 0.0
        for i, (gv, bv) in enumerate(((g0, b0), (g1, b1), (g2, b2))):
            tot = tot + _sig(gv.astype(F32) + bm[i:i + 1, :]) * bv.astype(F32)
        return (tot,)

    merged = _rows("merge_gates", merge, gl + [br_a, br_b, br_m], [bm3], [(D_MODEL, BF16)])[0]
    out = _mm("proj_out", merged, w_out, "nn", F32)

    def tail(xv, ov, tv, gv):
        r = lax.rsqrt(jnp.mean(ov * ov, axis=-1, keepdims=True) + EPS)
        n = ov * r
        err = xv + n * gv - tv
        dy = err * (1.0 / D_MODEL)
        dn = dy * gv
        dout = r * (dn - n * jnp.mean(dn * n, axis=-1, keepdims=True))
        return (dy, dout, jnp.sum(0.5 * err * err * (1.0 / D_MODEL), axis=0, keepdims=True),
                jnp.sum(dy * n, axis=0, keepdims=True))

    dy, dout, loss_lanes, g_post_grad = _rows(
        "tail", tail, [x, out, target], [g_post], [(D_MODEL, F32), (D_MODEL, BF16)],
        reds=[D_MODEL, D_MODEL], tm=512)

    dmerged = _mm("d_merged", dout, w_out, "nt", BF16)
    gw_out = _mm("g_w_out", merged, dout, "tn", F32, tk=2048)

    def merge_bwd(dm, g0, g1, g2, b0, b1, b2, bm):
        dmf = dm.astype(F32)
        dbs, dgs, sums = [], [], []
        for i, (gv, bv) in enumerate(((g0, b0), (g1, b1), (g2, b2))):
            sg = _sig(gv.astype(F32) + bm[i:i + 1, :])
            dbs.append(dmf * sg)
            dg = dmf * bv.astype(F32) * sg * (1.0 - sg)
            dgs.append(dg)
            sums.append(jnp.sum(dg, axis=0, keepdims=True))
        return tuple(dbs + [jnp.concatenate(dgs, axis=1)] + sums)

    du = lax.empty(u.shape, BF16)
    res = _rows("merge_bwd", merge_bwd, [dmerged] + gl + [br_a, br_b, br_m], [bm3],
                [(D_MODEL, BF16)] * 3 + [(3 * D_MODEL, BF16)], reds=[D_MODEL] * 3, tm=512,
                into=(du, 3, ("column", C_GL)))
    dbr, du, g_bmerge = res[0:3], res[3], jnp.concatenate(res[4:7], axis=1)

    dyg, gw_branch = [], []
    for nm, dbv, wv, ygv in (("a", dbr[0], w_ba, yg_a), ("b", dbr[1], w_bb, yg_b), ("m", dbr[2], w_bm, yg_m)):
        dyg.append(_mm("d_yg_" + nm, dbv, wv, "nt", BF16))
        gw_branch.append(_mm("g_w_branch_" + nm, ygv, dbv, "tn", F32, tk=2048))

    def gate_bwd(dg, y, z):
        dgf, yf, zf = dg.astype(F32), y.astype(F32), z.astype(F32)
        sg = _sig(zf)
        return dgf * (zf * sg), dgf * yf * (sg * (1.0 + zf * (1.0 - sg)))

    def gate_bwd_a(dg, y, z, *scr):
        dyv, dz = gate_bwd(dg, y, z)
        prod = dyv * y.astype(F32)
        dl = [jnp.sum(prod[:, hh * HD:(hh + 1) * HD], axis=-1, keepdims=True) for hh in range(4)]
        delta = _lane_pack(dl, (dg.shape[0], HD))
        return ((dz,) + tuple(_to_class(dyv, scr, d) for d in DILATIONS)
                + tuple(_to_class(delta, scr, d) for d in DILATIONS))

    res = _rows("gate_bwd_a", gate_bwd_a, [dyg[0], y_a, (u, 512, C_ZA // 512)], [],
                [(512, BF16)] + [(d * A_GROUP, BF16, d) for d in DILATIONS] + [(d * HD, F32, d) for d in DILATIONS],
                tm=ROPE_TM, scratch=_class_scratch(ROPE_TM), into=(du, 0, C_ZA // 512))
    du, dy_a, delta_a = res[0], res[1:4], res[4:7]
    dy_b, du = _rows("gate_bwd_b", gate_bwd, [dyg[1], y_b, (u, 512, C_ZB // 512)], [],
                     [(512, BF16), (512, BF16)], into=(du, 1, C_ZB // 512))
    dy_m, du = _rows("gate_bwd_m", gate_bwd, [dyg[2], y_m, (u, 512, C_ZM // 512)], [],
                     [(512, BF16), (512, BF16)], into=(du, 1, C_ZM // 512))

    du, dmk, dmv = _mem_bwd(u, mkv, y_m, dy_m, lse_m, du)
    dmkv = jnp.concatenate([dmk, dmv], axis=1)
    gw_mem_kv = _mm("g_w_mem_kv", memn, dmkv, "tn", F32)
    dmemn = _mm("d_memn", dmkv, w_mem_kv, "nt", F32)

    def mem_gain_grad(mv, dv):
        r = lax.rsqrt(jnp.mean(mv * mv, axis=-1, keepdims=True) + EPS)
        return (jnp.sum(dv * mv * r, axis=0, keepdims=True),)

    g_mem_grad = _rows("g_norm_mem", mem_gain_grad, [mem, dmemn], [], [], reds=[D_MODEL], tm=N_MEM)[0]

    dow, delta_b = _fox_pack_bwd(dy_b, y_b, t_fox)
    dqt, dkw, dvw = _fox_bwd(qf, dow, lse_b.reshape(B_HEADS, nt, 1, t_fox), delta_b, kb, kst4, vb, bounds, t_fox)
    du, dc = _fox_unpack(dqt, dkw, dvw, du, t_fox)
    dzrow, g_bforget = _fox_prep_bwd(dc.reshape(B_HEADS, s), zrow, b_forget.reshape(B_HEADS, 1))
    dfb = jnp.zeros((s, HD), BF16).at[:, :B_HEADS].set(dzrow.T.astype(BF16))

    dqs, dks, dvs = [], [], []
    for g, d in enumerate(DILATIONS):
        qv, kv, vv = views[g]
        dqs.append(_band_dq("band_dq%d" % g, qv, kv, vv, dy_a[g], lse_a[g], delta_a[g], d))
        dk_g, dv_g = _band_dkv("band_dkv%d" % g, qv, kv, vv, dy_a[g], lse_a[g], delta_a[g], d)
        dks.append(dk_g)
        dvs.append(dv_g)
    du = _rope_bwd(dqs, dks, dvs, pos, inv128, du)

    gw_main = _mm("g_w_main", h_t, du, "nn", F32, tk=2048)
    gw_fb = _mm("g_w_fb", h, dfb, "tn", F32)
    grads = dict(norm_post_g=g_post_grad, norm_mem_g=g_mem_grad, w_in=_chip_slabs(gw_main, gw_fb[:, :B_HEADS]),
                 b_forget=g_bforget.reshape(1, B_HEADS), b_merge=g_bmerge, w_mem_kv=gw_mem_kv,
                 w_branch_a=gw_branch[0], w_branch_b=gw_branch[1], w_branch_m=gw_branch[2], w_out=gw_out)
    side = exchange(grads) if exchange else None
    dh_main = _mm("d_h", du, w_main, "nt", F32, tm=1024, tk=2816, side=side)
    landed = None
    if side:
        dh_main, landed = dh_main[0], dh_main[1:]
    dh_fb = _mm("d_h_fb", dfb, w_fb, "nt", F32)

    def pre_bwd(xv, d1, d2, dyv, gv):
        r = lax.rsqrt(jnp.mean(xv * xv, axis=-1, keepdims=True) + EPS)
        n = xv * r
        dhv = d1 + d2
        dn = dhv * gv
        dx = r * (dn - n * jnp.mean(dn * n, axis=-1, keepdims=True))
        return dyv + dx, jnp.sum(dhv * n, axis=0, keepdims=True)

    grad_x, g_pre_grad = _rows("norm_pre_bwd", pre_bwd, [x, dh_main, dh_fb, dy], [g_pre],
                               [(D_MODEL, F32)], reds=[D_MODEL], tm=512)

    grads["norm_pre_g"] = g_pre_grad
    return loss_lanes, grad_x, grads, landed


HBM_SPEC = pl.BlockSpec(memory_space=pltpu.HBM)


def _place():
    x, y, c = lax.axis_index("x"), lax.axis_index("y"), lax.axis_index("c")
    chips = [(1 - x, y), (x, 1 - y), (1 - x, 1 - y)]
    return x, y, c, 2 * x + y, chips


N_CHUNKS = 4


def _units(parts, row_axis):
    units = []
    for i, a in enumerate(parts):
        ch = a.shape[row_axis] // N_CHUNKS
        units += [(i, pl.ds(k * ch, ch)) for k in range(N_CHUNKS)]
    return units


def _gather_weights(parts):
    n = len(parts)
    units = _units(parts, 1)
    nu = len(units)
    via_y = [(u % N_CHUNKS) < N_CHUNKS // 2 for u in range(nu)]

    def body(*refs):
        srcs, outs = refs[:n], refs[n:2 * n]
        send_sems, recv_sems = refs[2 * n:]
        x, y, c, p, _ = _place()
        me, sib = (x, y, c), (x, y, 1 - c)
        xn, yn, dg = (1 - x, y), (x, 1 - y), (1 - x, 1 - y)

        def cp(u, k, chip, half, to, from_src=False):
            i, rs = units[u]
            dst = outs[i].at[2 * chip[0] + chip[1], half, rs]
            return pltpu.make_async_remote_copy(
                src_ref=srcs[i].at[half, rs] if from_src else dst, dst_ref=dst, send_sem=send_sems.at[u, k],
                recv_sem=recv_sems.at[u, k], device_id=to, device_id_type=MESH)

        sent = []

        def go(copy):
            copy.start()
            sent.append(copy)

        for u in range(nu):
            go(cp(u, 0, (x, y), c, (*xn, c), from_src=True))
            go(cp(u, 1, (x, y), c, (*yn, c), from_src=True))
        for u in range(nu):
            cp(u, 0, xn, c, me).wait_recv()
            go(cp(u, 4, xn, c, sib))
            if via_y[u]:
                go(cp(u, 2, xn, c, (*yn, c)))
            cp(u, 1, yn, c, me).wait_recv()
            go(cp(u, 5, yn, c, sib))
            if not via_y[u]:
                go(cp(u, 3, yn, c, (*xn, c)))
        for u in range(nu):
            cp(u, 2 if via_y[u] else 3, dg, c, me).wait_recv()
            go(cp(u, 6, dg, c, sib))
        for u in range(nu):
            for k, chip in ((4, xn), (5, yn), (6, dg)):
                cp(u, k, chip, 1 - c, me).wait_recv()
        for copy in sent:
            copy.wait_send()

    return pl.pallas_call(
        body, name="gather_weights", in_specs=[HBM_SPEC] * n, out_specs=[HBM_SPEC] * n,
        out_shape=[jax.ShapeDtypeStruct((N_CHIPS,) + a.shape, a.dtype) for a in parts],
        scratch_shapes=[pltpu.SemaphoreType.DMA((nu, 7)), pltpu.SemaphoreType.DMA((nu, 7))],
    )(*parts)


def _swap_with_sibling(parts):
    n = len(parts)
    units = _units(parts, 2)

    def body(*refs):
        srcs, outs = refs[:n], refs[n:2 * n]
        send_sems, recv_sems = refs[2 * n:]
        x, y, c, _, _ = _place()
        cps = [pltpu.make_async_remote_copy(
            src_ref=srcs[i].at[q, 1 - c, rs], dst_ref=outs[i].at[q, rs], send_sem=send_sems.at[u, q],
            recv_sem=recv_sems.at[u, q], device_id=(x, y, 1 - c), device_id_type=MESH)
            for q in range(N_CHIPS) for u, (i, rs) in enumerate(units)]
        for cpy in cps:
            cpy.start()
        for cpy in cps:
            cpy.wait()

    return pl.pallas_call(
        body, name="swap_with_sibling", in_specs=[HBM_SPEC] * n, out_specs=[HBM_SPEC] * n,
        out_shape=[jax.ShapeDtypeStruct(a.shape[:1] + a.shape[2:], a.dtype) for a in parts],
        scratch_shapes=[pltpu.SemaphoreType.DMA((len(units), N_CHIPS)),
                        pltpu.SemaphoreType.DMA((len(units), N_CHIPS))],
    )(*parts)


def _scatter_to_owners(parts):
    n = len(parts)
    units = _units(parts, 1)

    def copies(srcs, outs, send_sems, recv_sems, incoming):
        x, y, c, p, chips = _place()
        return [pltpu.make_async_remote_copy(
            src_ref=srcs[i].at[2 * cx + cy, rs], dst_ref=outs[i].at[(2 * cx + cy) if incoming else p, rs],
            send_sem=send_sems.at[u, j], recv_sem=recv_sems.at[u, j], device_id=(cx, cy, c), device_id_type=MESH)
            for u, (i, rs) in enumerate(units) for j, (cx, cy) in enumerate(chips)]

    def start(ins, outs, scratch):
        for cpy in copies(ins, outs, *scratch, incoming=False):
            cpy.start()

    def wait(ins, outs, scratch):
        for cpy in copies(ins, outs, *scratch, incoming=True):
            cpy.wait_recv()
        for cpy in copies(ins, outs, *scratch, incoming=False):
            cpy.wait_send()

    return dict(ins=list(parts), outs=[jax.ShapeDtypeStruct(a.shape, a.dtype) for a in parts],
                scratch=[pltpu.SemaphoreType.DMA((len(units), 3)), pltpu.SemaphoreType.DMA((len(units), 3))],
                start=start, wait=wait)


def _share_with_sibling(parts):
    n = len(parts)
    units = _units(parts, 1)

    def body(*refs):
        srcs, outs = refs[:n], refs[n:2 * n]
        send_sems, recv_sems = refs[2 * n:]
        x, y, c, _, _ = _place()
        sends = [pltpu.make_async_remote_copy(
            src_ref=srcs[i].at[0, rs], dst_ref=outs[i].at[c, rs], send_sem=send_sems.at[u],
            recv_sem=recv_sems.at[u], device_id=(x, y, 1 - c), device_id_type=MESH)
            for u, (i, rs) in enumerate(units)]
        for cpy in sends:
            cpy.start()
        for u, (i, rs) in enumerate(units):
            pltpu.make_async_remote_copy(
                src_ref=srcs[i].at[0, rs], dst_ref=outs[i].at[1 - c, rs], send_sem=send_sems.at[u],
                recv_sem=recv_sems.at[u], device_id=(x, y, 1 - c), device_id_type=MESH).wait_recv()
        for cpy in sends:
            cpy.wait_send()

    return pl.pallas_call(
        body, name="share_with_sibling", in_specs=[HBM_SPEC] * n, out_specs=[HBM_SPEC] * n,
        out_shape=[jax.ShapeDtypeStruct((2,) + a.shape[1:], a.dtype) for a in parts],
        scratch_shapes=[pltpu.SemaphoreType.DMA((len(units),)), pltpu.SemaphoreType.DMA((len(units),))],
    )(*parts)


def _sum_small(v):
    def body(v_ref, out_ref, buf, send_sems, recv_sems):
        x, y, c, _, _ = _place()
        me = 4 * x + 2 * y + c
        buf[me] = v_ref[...]
        flips = [(dx, dy, dc) for dx in (0, 1) for dy in (0, 1) for dc in (0, 1)][1:]
        sends = []
        for k, (dx, dy, dc) in enumerate(flips):
            cpy = pltpu.make_async_remote_copy(
                src_ref=v_ref, dst_ref=buf.at[me], send_sem=send_sems.at[k], recv_sem=recv_sems.at[k],
                device_id=((x + dx) % 2, (y + dy) % 2, (c + dc) % 2), device_id_type=MESH)
            cpy.start()
            sends.append(cpy)
        for k, (dx, dy, dc) in enumerate(flips):
            px, py, pc = (x + dx) % 2, (y + dy) % 2, (c + dc) % 2
            pltpu.make_async_remote_copy(
                src_ref=v_ref, dst_ref=buf.at[4 * px + 2 * py + pc], send_sem=send_sems.at[k],
                recv_sem=recv_sems.at[k], device_id=(px, py, pc), device_id_type=MESH).wait_recv()
        for cpy in sends:
            cpy.wait_send()
        tot = buf[0]
        for i in range(1, N_DEV):
            tot = tot + buf[i]
        out_ref[...] = tot

    return pl.pallas_call(
        body, name="sum_small", out_shape=jax.ShapeDtypeStruct(v.shape, v.dtype),
        in_specs=[pl.BlockSpec(memory_space=pltpu.VMEM)], out_specs=pl.BlockSpec(memory_space=pltpu.VMEM),
        scratch_shapes=[pltpu.VMEM((N_DEV,) + v.shape, v.dtype), pltpu.SemaphoreType.DMA((N_DEV - 1,)),
                        pltpu.SemaphoreType.DMA((N_DEV - 1,))],
    )(v)


def _add_chips(name, landed, pair, chip):
    nq, r, w = landed.shape
    tr = 128 if r % 128 == 0 else 64

    def body(chip_ref, *refs):
        own = refs[nq][...].astype(F32)
        tot = None
        for q in range(nq):
            term = jnp.where(chip_ref[0] == q, own, refs[q][...].astype(F32))
            tot = term if tot is None else tot + term
        refs[nq + 1][...] = tot

    specs = [pl.BlockSpec((None, tr, w), functools.partial(lambda j, chip_ref, q: (q, j, 0), q=q)) for q in range(nq)]
    specs.append(pl.BlockSpec((None, tr, w), lambda j, chip_ref: (chip_ref[0], j, 0)))
    grid_spec = pltpu.PrefetchScalarGridSpec(
        num_scalar_prefetch=1, grid=(r // tr,), in_specs=specs,
        out_specs=pl.BlockSpec((None, tr, w), lambda j, chip_ref: (0, j, 0)))
    return pl.pallas_call(
        body, name=name, grid_spec=grid_spec, out_shape=jax.ShapeDtypeStruct((1, r, w), F32),
        compiler_params=_params(("parallel",)),
    )(jnp.reshape(chip, (1,)).astype(jnp.int32), *([landed] * nq), pair)


def _add_pair(name, halves, got, c):
    nq, _, r, w = halves.shape
    tr = 128 if r % 128 == 0 else 64

    def body(c_ref, a_ref, b_ref, o_ref):
        o_ref[...] = (a_ref[...] + b_ref[...]).astype(o_ref.dtype)

    grid_spec = pltpu.PrefetchScalarGridSpec(
        num_scalar_prefetch=1, grid=(nq, r // tr),
        in_specs=[pl.BlockSpec((None, None, tr, w), lambda i, j, c_ref: (i, c_ref[0], j, 0)),
                  pl.BlockSpec((None, tr, w), lambda i, j, c_ref: (i, j, 0))],
        out_specs=pl.BlockSpec((None, tr, w), lambda i, j, c_ref: (i, j, 0)))
    return pl.pallas_call(
        body, name=name, grid_spec=grid_spec, out_shape=jax.ShapeDtypeStruct((nq, r, w), BF16),
        compiler_params=_params(("parallel", "parallel")),
    )(jnp.reshape(c, (1,)).astype(jnp.int32), halves, got)


def _adamw(name, w, g, m, v, tm):
    def fn(wv, gv, mv, vv):
        m2 = ADAM_B1 * mv + (1.0 - ADAM_B1) * gv
        v2 = ADAM_B2 * vv + (1.0 - ADAM_B2) * (gv * gv)
        m_hat = m2 / (1.0 - ADAM_B1 ** ADAM_STEP)
        v_hat = v2 / (1.0 - ADAM_B2 ** ADAM_STEP)
        return -ADAM_LR * (m_hat / (jnp.sqrt(v_hat) + ADAM_EPS) + ADAM_WD * wv), m2, v2
    c = w.shape[1]
    return _rows(name, fn, [w, g, m, v], [], [(c, F32)] * 3, tm=tm)


REST_ROWS = 256 + 3 * 128 + 256
REST_SPLITS = (("w_mem_kv", 0, 256), ("w_branch_a", 256, 128), ("w_branch_b", 384, 128),
               ("w_branch_m", 512, 128), ("w_out", 640, 256))


def _rest_pack(t):
    return jnp.concatenate([t[n].reshape(rows, D_MODEL) for n, _, rows in REST_SPLITS], axis=0)


def _rest_unpack(a, shapes):
    return {n: a[r0:r0 + rows].reshape(shapes[n]) for n, r0, rows in REST_SPLITS}


def _small_pack(pre, post, memg, bforget, bmerge):
    pad = jnp.zeros((1, D_MODEL - B_HEADS), F32)
    return jnp.concatenate([pre, post, memg, bmerge.reshape(3, D_MODEL),
                            jnp.concatenate([bforget, pad], axis=1), jnp.zeros((1, D_MODEL), F32)], axis=0)


def _small_unpack(s8):
    return dict(norm_pre_g=s8[0:1], norm_post_g=s8[1:2], norm_mem_g=s8[2:3],
                b_merge=s8[3:6].reshape(1, 3 * D_MODEL), b_forget=s8[6:7, :B_HEADS])


WEIGHTS = ("norm_pre_g", "norm_post_g", "norm_mem_g", "w_in", "b_forget", "b_merge", "w_mem_kv",
           "w_branch_a", "w_branch_b", "w_branch_m", "w_out")
SMALL = ("norm_pre_g", "norm_post_g", "norm_mem_g", "b_forget", "b_merge")


def kernel(x, mem, positions, norm_pre_g, norm_post_g, norm_mem_g, w_in, b_forget, b_merge, w_mem_kv, w_branch_a, w_branch_b, w_branch_m, w_out, loss_target, m_norm_pre_g, m_norm_post_g, m_norm_mem_g, m_w_in, m_b_forget, m_b_merge, m_w_mem_kv, m_w_branch_a, m_w_branch_b, m_w_branch_m, m_w_out, v_norm_pre_g, v_norm_post_g, v_norm_mem_g, v_w_in, v_b_forget, v_b_merge, v_w_mem_kv, v_w_branch_a, v_w_branch_b, v_w_branch_m, v_w_out):
    w = dict(norm_pre_g=norm_pre_g, norm_post_g=norm_post_g, norm_mem_g=norm_mem_g, w_in=w_in[0],
             b_forget=b_forget, b_merge=b_merge, w_mem_kv=w_mem_kv[0], w_branch_a=w_branch_a[0],
             w_branch_b=w_branch_b[0], w_branch_m=w_branch_m[0], w_out=w_out[0])
    mo = dict(norm_pre_g=m_norm_pre_g, norm_post_g=m_norm_post_g, norm_mem_g=m_norm_mem_g, w_in=m_w_in[0],
              b_forget=m_b_forget, b_merge=m_b_merge, w_mem_kv=m_w_mem_kv[0], w_branch_a=m_w_branch_a[0],
              w_branch_b=m_w_branch_b[0], w_branch_m=m_w_branch_m[0], w_out=m_w_out[0])
    vo = dict(norm_pre_g=v_norm_pre_g, norm_post_g=v_norm_post_g, norm_mem_g=v_norm_mem_g, w_in=v_w_in[0],
              b_forget=v_b_forget, b_merge=v_b_merge, w_mem_kv=v_w_mem_kv[0], w_branch_a=v_w_branch_a[0],
              w_branch_b=v_w_branch_b[0], w_branch_m=v_w_branch_m[0], w_out=v_w_out[0])
    s = x.shape[1]
    c = lax.axis_index("c")

    chip = 2 * lax.axis_index("x") + lax.axis_index("y")

    def put(whole, own, slot):
        return lax.dynamic_update_index_in_dim(whole, own.astype(whole.dtype), slot, 0)

    own_w = [w["w_in"].astype(BF16).reshape(2, D_MODEL // 2, SHARD_COLS),
             _rest_pack(w).astype(BF16).reshape(2, REST_ROWS // 2, D_MODEL)]
    all_in, all_rest = _gather_weights(own_w)
    all_in = all_in.reshape(N_CHIPS, D_MODEL, SHARD_COLS)
    own_in, own_rest = own_w[0].reshape(D_MODEL, SHARD_COLS), own_w[1].reshape(REST_ROWS, D_MODEL)
    w_main, w_fb = _split_forget([jnp.where(chip == p, own_in, all_in[p]) for p in range(N_CHIPS)])
    w_fb = jnp.concatenate([w_fb, jnp.zeros((D_MODEL, HD - B_HEADS), BF16)], axis=1)
    all_rest = all_rest.reshape(N_CHIPS, REST_ROWS, D_MODEL)
    all_rest = jnp.stack([jnp.where(chip == p, own_rest, all_rest[p]) for p in range(N_CHIPS)])
    w_kv_f = all_rest[:, 0:256].reshape(D_MODEL, D_MODEL)
    w_br_f = [all_rest[:, 256 + 128 * i:384 + 128 * i].reshape(N_CHIPS, 512, 256).transpose(1, 0, 2)
              .reshape(512, D_MODEL) for i in range(3)]
    w_out_f = all_rest[:, 640:896].reshape(D_MODEL, D_MODEL)

    pair = []

    def exchange(g):
        def per_chip(name, p):
            a = g[name]
            if name in ("w_mem_kv", "w_out"):
                return a[256 * p:256 * (p + 1)]
            return a[:, 256 * p:256 * (p + 1)]

        in4 = jnp.stack(g["w_in"])
        rest4 = jnp.stack([_rest_pack({n: per_chip(n, p) for n, _, _ in REST_SPLITS}) for p in range(N_CHIPS)])
        halves = [in4.reshape(N_CHIPS, 2, D_MODEL // 2, SHARD_COLS),
                  rest4.reshape(N_CHIPS, 2, REST_ROWS // 2, D_MODEL)]
        got = _swap_with_sibling(halves)
        pair.extend(_add_pair("add_pair_%d" % i, halves[i], got[i], c) for i in range(2))
        return _scatter_to_owners(pair)

    loss_lanes, grad_x, g, landed = _local_step(
        x[0], mem[0], positions.reshape(s, 1), loss_target[0], norm_pre_g, norm_post_g, norm_mem_g,
        w_main, w_fb, b_forget, b_merge, w_kv_f, w_br_f[0], w_br_f[1], w_br_f[2], w_out_f, exchange)
    loss = lax.psum(jnp.sum(loss_lanes), ("x", "y", "c"))
    half = [_add_chips("add_chips_%d" % i, landed[i], pair[i], chip) for i in range(2)]
    red_in, red_rest = [put(a, o[0], c) for a, o in zip(_share_with_sibling(half), half)]
    gs = {"w_in": red_in.reshape(D_MODEL, SHARD_COLS)}
    gs.update(_rest_unpack(red_rest.reshape(REST_ROWS, D_MODEL), {n: w[n].shape for n, _, _ in REST_SPLITS}))
    gs.update(_small_unpack(_sum_small(_small_pack(
        g["norm_pre_g"], g["norm_post_g"], g["norm_mem_g"], g["b_forget"], g["b_merge"]))))

    delta, new_m, new_v = {}, {}, {}
    for n, tm in (("w_in", 128), ("w_mem_kv", 256), ("w_branch_a", 512), ("w_branch_b", 512),
                  ("w_branch_m", 512), ("w_out", 256)):
        d_, m_, v_ = _adamw("adamw_" + n, w[n], gs[n], mo[n], vo[n], tm)
        delta[n], new_m[n], new_v[n] = d_[None], m_[None], v_[None]
        gs[n] = gs[n][None]
    packs = [_small_pack(*[t[n] for n in SMALL])
             for t in (w, gs, mo, vo)]
    for res, store in zip(_adamw("adamw_small", *packs, 8), (delta, new_m, new_v)):
        store.update(_small_unpack(res))

    return (loss, grad_x[None], *[gs[n] for n in WEIGHTS], *[delta[n] for n in WEIGHTS],
            *[new_m[n] for n in WEIGHTS], *[new_v[n] for n in WEIGHTS])
```

```python
import functools

import jax
import jax.numpy as jnp
from jax import lax
from jax.experimental import pallas as pl
from jax.experimental.pallas import tpu as pltpu

F32 = jnp.float32
BF16 = jnp.bfloat16
MESH = pl.DeviceIdType.MESH

D_MODEL = 1024
N_MEM = 256
EPS = 1e-6
NEG = -1e30
ROPE_THETA = 500000.0
ROT_DIM = 32
HD = 128
A_GROUP = 512
DILATIONS = (1, 4, 16)
BAND = 128
B_HEADS = 8
B_HD = 64
N_CHIPS = 4
N_DEV = 8

C_QA, C_KA, C_VA, C_ZA = 0, 1536, 3072, 4608
C_QB, C_KB, C_VB, C_ZB = 5120, 5632, 6144, 6656
C_QM, C_ZM, C_GL = 7168, 7680, 8192
FB_ORIG = 6656
IN_COLS = 11272
SHARD_COLS = IN_COLS // N_CHIPS

ADAM_LR, ADAM_B1, ADAM_B2, ADAM_EPS, ADAM_WD, ADAM_STEP = 0.001, 0.9, 0.999, 1e-08, 0.01, 10

VMEM_LIMIT_V7X = 56 * 1024 * 1024

NT = (((1,), (1,)), ((), ()))
NN = (((1,), (0,)), ((), ()))
TN = (((0,), (0,)), ((), ()))


def _params(sem):
    return pltpu.CompilerParams(dimension_semantics=sem, vmem_limit_bytes=VMEM_LIMIT_V7X)


def _dot(a, b, dn=NN):
    return lax.dot_general(a, b, dn, preferred_element_type=F32)


def _sig(z):
    return 1.0 / (1.0 + jnp.exp(-z))


def _rows(name, fn, row_ins, bc_ins, outs, reds=(), tm=512, scratch=(), into=None):
    arrs, specs = [], []
    s = None
    for r in row_ins:
        arr, w, cb, d = (tuple(r) + (1,))[:4] if isinstance(r, tuple) else (r, r.shape[1], 0, 1)
        s = arr.shape[0] * d if s is None else s
        arrs.append(arr)
        specs.append((w, cb, d))
    tm = min(tm, s)
    specs = [pl.BlockSpec((tm // d, w), functools.partial(lambda i, cb: (i, cb), cb=cb)) for w, cb, d in specs]
    for b in bc_ins:
        arrs.append(b)
        specs.append(pl.BlockSpec(b.shape, lambda i: (0, 0)))
    outs = [(tuple(o) + (1,))[:3] for o in outs]
    n_in, n_out = len(arrs), len(outs)
    o0 = n_in + (0 if into is None else 1)

    def body(*refs):
        n_ref = o0 + n_out + len(reds)
        vals = fn(*[r[...] for r in refs[:n_in]], *refs[n_ref:])
        if not isinstance(vals, (tuple, list)):
            vals = (vals,)
        for r, v in zip(refs[o0:o0 + n_out], vals[:n_out]):
            r[...] = v.astype(r.dtype)
        if reds:
            red_refs = refs[o0 + n_out:n_ref]

            @pl.when(pl.program_id(0) == 0)
            def _():
                for r in red_refs:
                    r[...] = jnp.zeros_like(r)

            for r, v in zip(red_refs, vals[n_out:]):
                r[...] += v

    out_shape = [jax.ShapeDtypeStruct((s // d, c), dt) for c, dt, d in outs]
    out_shape += [jax.ShapeDtypeStruct((1, c), F32) for c in reds]
    out_specs = [pl.BlockSpec((tm // d, c), lambda i: (i, 0)) for c, _, d in outs]
    out_specs += [pl.BlockSpec((1, c), lambda i: (0, 0)) for c in reds]
    aliases = {}
    if into is not None:
        whole, k, cb = into
        out_shape[k] = jax.ShapeDtypeStruct(whole.shape, whole.dtype)
        if isinstance(cb, tuple):
            out_specs[k] = pl.BlockSpec((pl.Element(tm), pl.Element(outs[k][0])),
                                        functools.partial(lambda i, c0: (i * tm, c0), c0=cb[1]))
        else:
            out_specs[k] = pl.BlockSpec((tm, outs[k][0]), functools.partial(lambda i, cb: (i, cb), cb=cb))
        aliases = {n_in: k}
        arrs.append(whole)
        specs.append(pl.BlockSpec(memory_space=pl.ANY))
    res = pl.pallas_call(
        body, name=name, grid=(s // tm,), in_specs=specs, out_specs=out_specs, out_shape=out_shape,
        scratch_shapes=list(scratch), input_output_aliases=aliases,
        compiler_params=_params(("arbitrary",) if reds else ("parallel",)),
    )(*arrs)
    return res


def _to_class(x, scr, d):
    if d == 1:
        return x.astype(F32)
    tm, c = x.shape
    for g in range(c // 128):
        scr[g][...] = x[:, g * 128:(g + 1) * 128].astype(F32)
    return jnp.concatenate([scr[g][pl.ds(r, tm // d, stride=d), :] for r in range(d) for g in range(c // 128)],
                           axis=1)


def _from_class(x, scr, d):
    if d == 1:
        return x.astype(F32)
    n, dc = x.shape
    c = dc // d
    for r in range(d):
        for g in range(c // 128):
            scr[g][pl.ds(r, n, stride=d), :] = x[:, r * c + g * 128:r * c + (g + 1) * 128].astype(F32)
    return jnp.concatenate([scr[g][...] for g in range(c // 128)], axis=1)


def _mm(name, a, b, mode, out_dtype, tm=2048, tn=1024, tk=1024, side=None):
    if mode == "nn":
        (m, k), (_, n) = a.shape, b.shape
    elif mode == "nt":
        (m, k), (n, _) = a.shape, b.shape
    else:
        (k, m), (_, n) = a.shape, b.shape
    tm, tn, tk = min(tm, m), min(tn, n), min(tk, k)
    nk = k // tk
    grid = (m // tm, n // tn, nk)
    dn = {"nn": NN, "nt": NT, "tn": TN}[mode]
    n_si = len(side["ins"]) if side else 0
    n_so = len(side["outs"]) if side else 0
    n_acc = 1 if nk > 1 else 0

    def body(*refs):
        a_ref, b_ref = refs[:2]
        side_in, o_ref = refs[2:2 + n_si], refs[2 + n_si]
        side_out = refs[3 + n_si:3 + n_si + n_so]
        acc = refs[3 + n_si + n_so:3 + n_si + n_so + n_acc]
        side_scratch = refs[3 + n_si + n_so + n_acc:]
        step = (pl.program_id(0) * grid[1] + pl.program_id(1)) * grid[2] + pl.program_id(2)
        if side:
            @pl.when(step == 0)
            def _():
                side["start"](side_in, side_out, side_scratch)

        part = _dot(a_ref[...].astype(BF16), b_ref[...].astype(BF16), dn)
        if nk == 1:
            o_ref[...] = part.astype(o_ref.dtype)
        else:
            kk = pl.program_id(2)

            @pl.when(kk == 0)
            def _():
                acc[0][...] = part

            @pl.when(kk > 0)
            def _():
                acc[0][...] += part

            @pl.when(kk == nk - 1)
            def _():
                o_ref[...] = acc[0][...].astype(o_ref.dtype)

        if side:
            @pl.when(step == grid[0] * grid[1] * grid[2] - 1)
            def _():
                side["wait"](side_in, side_out, side_scratch)

    a_spec = (pl.BlockSpec((tk, tm), lambda i, j, kk: (kk, i)) if mode == "tn"
              else pl.BlockSpec((tm, tk), lambda i, j, kk: (i, kk)))
    b_spec = (pl.BlockSpec((tn, tk), lambda i, j, kk: (j, kk)) if mode == "nt"
              else pl.BlockSpec((tk, tn), lambda i, j, kk: (kk, j)))
    o_spec = pl.BlockSpec((tm, tn), lambda i, j, kk: (i, j))
    o_shape = jax.ShapeDtypeStruct((m, n), out_dtype)
    acc_scratch = [pltpu.VMEM((tm, tn), F32)] * n_acc
    if not side:
        return pl.pallas_call(
            body, name=name, grid=grid, in_specs=[a_spec, b_spec], out_specs=o_spec, out_shape=o_shape,
            scratch_shapes=acc_scratch, compiler_params=_params(("parallel", "parallel", "arbitrary")),
        )(a, b)
    return pl.pallas_call(
        body, name=name, grid=grid, in_specs=[a_spec, b_spec] + [HBM_SPEC] * n_si,
        out_specs=[o_spec] + [HBM_SPEC] * n_so, out_shape=[o_shape] + side["outs"],
        scratch_shapes=acc_scratch + side["scratch"],
        compiler_params=_params(("arbitrary", "arbitrary", "arbitrary")),
    )(a, b, *side["ins"])


def _rms_fwd(name, x, g):
    def fn(xv, gv):
        r = lax.rsqrt(jnp.mean(xv * xv, axis=-1, keepdims=True) + EPS)
        return (xv * r * gv,)
    return _rows(name, fn, [x], [g], [(x.shape[1], BF16)], tm=min(512, x.shape[0]))[0]


def _rms_fwd_both(name, x, g):
    s, dm = x.shape
    tm = min(512, s)

    def body(x_ref, g_ref, h_ref, ht_ref):
        xv = x_ref[...]
        hv = xv * lax.rsqrt(jnp.mean(xv * xv, axis=-1, keepdims=True) + EPS) * g_ref[...]
        h_ref[...] = hv.astype(BF16)
        ht_ref[...] = hv.T.astype(BF16)

    return pl.pallas_call(
        body, name=name, grid=(s // tm,),
        in_specs=[pl.BlockSpec((tm, dm), lambda i: (i, 0)), pl.BlockSpec((1, dm), lambda i: (0, 0))],
        out_specs=[pl.BlockSpec((tm, dm), lambda i: (i, 0)), pl.BlockSpec((dm, tm), lambda i: (0, i))],
        out_shape=[jax.ShapeDtypeStruct((s, dm), BF16), jax.ShapeDtypeStruct((dm, s), BF16)],
        compiler_params=_params(("parallel",)),
    )(x, g)


def _rope_tables(pos, inv):
    ang = pos.astype(F32) * inv
    lane = lax.broadcasted_iota(jnp.int32, ang.shape, 1)
    c = jnp.where(lane < ROT_DIM, jnp.cos(ang), 1.0)
    sn = jnp.sin(ang)
    sg = jnp.where(lane < ROT_DIM // 2, -sn, jnp.where(lane < ROT_DIM, sn, 0.0))
    return c, sg, lane


def _rope_apply(x, c, sg, lane):
    outs = []
    for h in range(x.shape[1] // HD):
        xh = x[:, h * HD:(h + 1) * HD].astype(F32)
        swap = jnp.where(lane < ROT_DIM // 2, pltpu.roll(xh, HD - ROT_DIM // 2, 1),
                         pltpu.roll(xh, ROT_DIM // 2, 1))
        outs.append(xh * c + swap * sg)
    return jnp.concatenate(outs, axis=1)


ROPE_TM = 512


def _class_scratch(tm):
    return [pltpu.VMEM((tm, 128), F32) for _ in range(A_GROUP // 128)]


def _rope_fwd(u, pos, inv):
    def fn(q, k, v, p, iv, *scr):
        c, sg, lane = _rope_tables(p, iv)
        qr, kr = _rope_apply(q, c, sg, lane), _rope_apply(k, c, sg, lane)
        outs = []
        for g, d in enumerate(DILATIONS):
            gs = slice(g * A_GROUP, (g + 1) * A_GROUP)
            outs += [_to_class(qr[:, gs], scr, d), _to_class(kr[:, gs], scr, d), _to_class(v[:, gs], scr, d)]
        return tuple(outs)

    outs = [(d * A_GROUP, BF16, d) for d in DILATIONS for _ in range(3)]
    qkv = [(u, 3 * A_GROUP, c0 // (3 * A_GROUP)) for c0 in (C_QA, C_KA, C_VA)]
    return _rows("rope_fwd", fn, qkv + [pos], [inv], outs, tm=ROPE_TM,
                 scratch=_class_scratch(ROPE_TM))


def _rope_bwd(dqs, dks, dvs, pos, inv, du):
    def fn(*args):
        grads, p, iv, scr = args[:9], args[9], args[10], args[11:]
        c, sg, lane = _rope_tables(p, iv)
        tok = [jnp.concatenate([_from_class(grads[3 * k + g], scr, d) for g, d in enumerate(DILATIONS)], axis=1)
               for k in range(3)]
        return (jnp.concatenate([_rope_apply(tok[0], c, -sg, lane), _rope_apply(tok[1], c, -sg, lane), tok[2]],
                                axis=1),)

    ins = [(a, a.shape[1], 0, d) for grp in (dqs, dks, dvs) for a, d in zip(grp, DILATIONS)]
    return _rows("rope_bwd", fn, ins + [pos], [inv], [(9 * A_GROUP, BF16)], tm=ROPE_TM,
                 scratch=_class_scratch(ROPE_TM), into=(du, 0, 0))[0]


def _lane_pack(cols, like):
    lane = lax.broadcasted_iota(jnp.int32, like, 1)
    out = jnp.zeros(like, F32)
    for h, cvec in enumerate(cols):
        out = jnp.where(lane == h, cvec, out)
    return out


def _band_specs(l, d, tq):
    nsb = tq // BAND
    nblk = l // BAND
    cur = pl.BlockSpec((tq, A_GROUP), lambda r, i: (i, r))
    prev = pl.BlockSpec((BAND, A_GROUP), lambda r, i: (jnp.maximum(i * nsb - 1, 0), r))
    nxt = pl.BlockSpec((BAND, A_GROUP), lambda r, i: (jnp.minimum((i + 1) * nsb, nblk - 1), r))
    st_cur = pl.BlockSpec((tq, HD), lambda r, i: (i, r))
    st_nxt = pl.BlockSpec((BAND, HD), lambda r, i: (jnp.minimum((i + 1) * nsb, nblk - 1), r))
    return nsb, cur, prev, nxt, st_cur, st_nxt


def _band_mask_q(i, first_tile):
    qr = lax.broadcasted_iota(jnp.int32, (BAND, 2 * BAND), 0)
    kc = lax.broadcasted_iota(jnp.int32, (BAND, 2 * BAND), 1)
    in_prev = (kc < BAND) & (kc >= qr)
    in_cur = (kc >= BAND) & (kc - BAND <= qr)
    if i == 0:
        in_prev = in_prev & jnp.logical_not(first_tile)
    return in_prev | in_cur


def _band_mask_k(j, nsb, last_tile):
    kc = lax.broadcasted_iota(jnp.int32, (BAND, 2 * BAND), 0)
    qr = lax.broadcasted_iota(jnp.int32, (BAND, 2 * BAND), 1)
    same = (qr < BAND) & (kc <= qr)
    nxt = (qr >= BAND) & (kc >= qr - BAND)
    if j == nsb - 1:
        nxt = nxt & jnp.logical_not(last_tile)
    return same | nxt


def _band_fwd(name, q, k, v, d):
    l = q.shape[0]
    tq = min(512, l)
    nsb, cur, prev, _, st_cur, _ = _band_specs(l, d, tq)
    scale = HD ** -0.5

    def body(q_ref, kc_ref, kp_ref, vc_ref, vp_ref, o_ref, lse_ref):
        first = pl.program_id(1) == 0
        for i in range(nsb):
            lses = []
            mask = _band_mask_q(i, first)
            for h in range(4):
                cs = slice(h * HD, (h + 1) * HD)
                qv = q_ref[i * BAND:(i + 1) * BAND, cs]
                if i == 0:
                    kk = jnp.concatenate([kp_ref[:, cs], kc_ref[0:BAND, cs]], axis=0)
                    vv = jnp.concatenate([vp_ref[:, cs], vc_ref[0:BAND, cs]], axis=0)
                else:
                    kk = kc_ref[(i - 1) * BAND:(i + 1) * BAND, cs]
                    vv = vc_ref[(i - 1) * BAND:(i + 1) * BAND, cs]
                s = jnp.where(mask, _dot(qv, kk, NT) * scale, NEG)
                m = jnp.max(s, axis=-1, keepdims=True)
                p = jnp.exp(s - m)
                den = jnp.sum(p, axis=-1, keepdims=True)
                o_ref[i * BAND:(i + 1) * BAND, cs] = _dot(p.astype(BF16), vv) / den
                lses.append(m + jnp.log(den))
            lse_ref[i * BAND:(i + 1) * BAND, :] = _lane_pack(lses, (BAND, HD))

    return pl.pallas_call(
        body, name=name, grid=(d, l // tq), in_specs=[cur, cur, prev, cur, prev],
        out_specs=[cur, st_cur],
        out_shape=[jax.ShapeDtypeStruct((l, d * A_GROUP), F32), jax.ShapeDtypeStruct((l, d * HD), F32)],
        compiler_params=_params(("parallel", "parallel")),
    )(q, k, k, v, v)


def _band_dq(name, q, k, v, dy, lse, delta, d):
    l = q.shape[0]
    tq = min(512, l)
    nsb, cur, prev, _, st_cur, _ = _band_specs(l, d, tq)
    scale = HD ** -0.5

    def body(q_ref, kc_ref, kp_ref, vc_ref, vp_ref, dy_ref, lse_ref, dl_ref, dq_ref):
        first = pl.program_id(1) == 0
        for i in range(nsb):
            mask = _band_mask_q(i, first)
            rs = slice(i * BAND, (i + 1) * BAND)
            for h in range(4):
                cs = slice(h * HD, (h + 1) * HD)
                if i == 0:
                    kk = jnp.concatenate([kp_ref[:, cs], kc_ref[0:BAND, cs]], axis=0)
                    vv = jnp.concatenate([vp_ref[:, cs], vc_ref[0:BAND, cs]], axis=0)
                else:
                    kk = kc_ref[(i - 1) * BAND:(i + 1) * BAND, cs]
                    vv = vc_ref[(i - 1) * BAND:(i + 1) * BAND, cs]
                s = jnp.where(mask, _dot(q_ref[rs, cs], kk, NT) * scale, NEG)
                p = jnp.exp(s - lse_ref[rs, h:h + 1])
                dp = _dot(dy_ref[rs, cs], vv, NT)
                ds = p * (dp - dl_ref[rs, h:h + 1])
                dq_ref[rs, cs] = (_dot(ds.astype(BF16), kk) * scale).astype(dq_ref.dtype)

    return pl.pallas_call(
        body, name=name, grid=(d, l // tq),
        in_specs=[cur, cur, prev, cur, prev, cur, st_cur, st_cur], out_specs=cur,
        out_shape=jax.ShapeDtypeStruct((l, d * A_GROUP), BF16),
        compiler_params=_params(("parallel", "parallel")),
    )(q, k, k, v, v, dy, lse, delta)


def _band_dkv(name, q, k, v, dy, lse, delta, d):
    l = q.shape[0]
    tq = min(512, l)
    nsb, cur, _, nxt, st_cur, st_nxt = _band_specs(l, d, tq)
    scale = HD ** -0.5
    ntile = l // tq

    def body(k_ref, v_ref, qc_ref, qn_ref, dyc_ref, dyn_ref, lc_ref, ln_ref, dc_ref, dn_ref,
             dk_ref, dv_ref):
        last = pl.program_id(1) == ntile - 1

        def win(c_ref, n_ref, j, cs):
            if j == nsb - 1:
                return jnp.concatenate([c_ref[j * BAND:(j + 1) * BAND, cs], n_ref[:, cs]], axis=0)
            return c_ref[j * BAND:(j + 2) * BAND, cs]

        allh = slice(0, HD)
        for j in range(nsb):
            mask = _band_mask_k(j, nsb, last)
            rs = slice(j * BAND, (j + 1) * BAND)
            lse_t = win(lc_ref, ln_ref, j, allh).T
            delta_t = win(dc_ref, dn_ref, j, allh).T
            for h in range(4):
                cs = slice(h * HD, (h + 1) * HD)
                qw = win(qc_ref, qn_ref, j, cs)
                dyw = win(dyc_ref, dyn_ref, j, cs)
                st = jnp.where(mask, _dot(k_ref[rs, cs], qw, NT) * scale, NEG)
                pt = jnp.exp(st - lse_t[h:h + 1, :])
                dst = pt * (_dot(v_ref[rs, cs], dyw, NT) - delta_t[h:h + 1, :])
                dv_ref[rs, cs] = _dot(pt.astype(BF16), dyw).astype(dv_ref.dtype)
                dk_ref[rs, cs] = (_dot(dst.astype(BF16), qw) * scale).astype(dk_ref.dtype)

    shp = jax.ShapeDtypeStruct((l, d * A_GROUP), BF16)
    return pl.pallas_call(
        body, name=name, grid=(d, ntile),
        in_specs=[cur, cur, cur, nxt, cur, nxt, st_cur, st_nxt, st_cur, st_nxt],
        out_specs=[cur, cur], out_shape=[shp, shp],
        compiler_params=_params(("parallel", "parallel")),
    )(k, v, q, q, dy, dy, lse, lse, delta, delta)


def _split3(x):
    hi = x.astype(BF16)
    r1 = x - hi.astype(F32)
    mid = r1.astype(BF16)
    lo = (r1 - mid.astype(F32)).astype(BF16)
    return hi, mid, lo


def _fox_prep(z, b):
    h, s = z.shape
    blk = min(512, s)

    def body(z_ref, b_ref, c_ref):
        r = lax.broadcasted_iota(jnp.int32, (blk, blk), 0)
        cidx = lax.broadcasted_iota(jnp.int32, (blk, blk), 1)
        tri = (r <= cidx).astype(BF16)
        carry = jnp.zeros((h, 1), F32)
        for t in range(s // blk):
            zz = z_ref[:, t * blk:(t + 1) * blk] + b_ref[...]
            lf = jnp.minimum(zz, 0.0) - jnp.log(1.0 + jnp.exp(-jnp.abs(zz)))
            hi, mid, lo = _split3(lf)
            cs = _dot(hi, tri) + _dot(mid, tri) + _dot(lo, tri) + carry
            c_ref[:, t * blk:(t + 1) * blk] = cs
            carry = cs[:, blk - 1:blk]

    return pl.pallas_call(body, name="fox_prep", out_shape=jax.ShapeDtypeStruct((h, s), F32))(z, b)


def _fox_prep_bwd(dc, z, b):
    h, s = z.shape
    blk = min(512, s)

    def body(dc_ref, z_ref, b_ref, dz_ref, db_ref):
        r = lax.broadcasted_iota(jnp.int32, (blk, blk), 0)
        cidx = lax.broadcasted_iota(jnp.int32, (blk, blk), 1)
        tri = (r >= cidx).astype(BF16)
        carry = jnp.zeros((h, 1), F32)
        tot = jnp.zeros((h, 1), F32)
        for t in reversed(range(s // blk)):
            hi, mid, lo = _split3(dc_ref[:, t * blk:(t + 1) * blk])
            rc = _dot(hi, tri) + _dot(mid, tri) + _dot(lo, tri) + carry
            carry = rc[:, 0:1]
            zz = z_ref[:, t * blk:(t + 1) * blk] + b_ref[...]
            dz = rc * _sig(-zz)
            dz_ref[:, t * blk:(t + 1) * blk] = dz
            tot = tot + jnp.sum(dz, axis=-1, keepdims=True)
        db_ref[...] = tot

    return pl.pallas_call(
        body, name="fox_prep_bwd",
        out_shape=[jax.ShapeDtypeStruct((h, s), F32), jax.ShapeDtypeStruct((h, 1), F32)])(dc, z, b)


FOX_W = 128
FOX_C = B_HD
FOX_ONE = B_HD + 3
FOX_SUB = 256
FOX_SUB_FWD = 128
FOX_HEADS_PER_STEP = 2


def _head_of_pair(x, hh):
    return x if hh == 0 else pltpu.roll(x, B_HD, 1)


def _fox_pack(u, c_col, t):
    s = u.shape[0]
    nt = s // t
    scale = B_HD ** -0.5

    def body(q_ref, k_ref, v_ref, c_ref, qf_ref, kb_ref, ks_ref, vb_ref, vt_ref, q2_ref, k2_ref):
        lane = lax.broadcasted_iota(jnp.int32, (t, FOX_W), 1)
        head_lanes = (lax.broadcasted_iota(jnp.int32, (FOX_W, FOX_W), 0) < B_HD).astype(BF16)

        def top_norm2(xv):
            n2 = _dot((xv * xv).astype(BF16), head_lanes)
            return jnp.broadcast_to(jnp.max(n2, axis=0, keepdims=True)[:, :1], (8, 128))

        for hd in range(B_HEADS):
            pair, hh = slice(hd // 2 * FOX_W, (hd // 2 + 1) * FOX_W), hd % 2
            qv, kv, vv = [r[:, pair].astype(F32) for r in (q_ref, k_ref, v_ref)]
            qh = _head_of_pair(qv, hh)
            q2_ref[hd] = top_norm2(qh)
            qf_ref[hd] = jnp.where(lane < B_HD, qh, B_HD ** 0.5).astype(BF16)
            neg = c_ref[hd] * (-scale)
            hi = neg.astype(BF16).astype(F32)
            mid = (neg - hi).astype(BF16).astype(F32)
            lo = neg - hi - mid
            aux = jnp.where(lane == FOX_C, hi,
                            jnp.where(lane == FOX_C + 1, mid, jnp.where(lane == FOX_C + 2, lo, 0.0)))
            kb = jnp.where(lane < B_HD, _head_of_pair(kv, hh) * scale, aux)
            k2_ref[hd] = top_norm2(jnp.where(lane < B_HD, kb, 0.0))
            kb_ref[hd] = kb.astype(BF16)
            ks_ref[hd] = jnp.where(lane == FOX_ONE, 1.0, kb).T.astype(BF16)
            vb = jnp.where(lane < B_HD, _head_of_pair(vv, hh), 1.0)
            vb_ref[hd] = vb.astype(BF16)
            vt_ref[hd] = vb.T.astype(BF16)

    def tok(col0):
        return pl.BlockSpec((t, B_HEADS * B_HD), functools.partial(lambda i, cb: (i, cb), cb=col0 // (B_HEADS * B_HD)))

    rows = pl.BlockSpec((B_HEADS, t, FOX_W), lambda i: (0, i, 0))
    tiles = pl.BlockSpec((B_HEADS, None, FOX_W, t), lambda i: (0, i, 0, 0))
    hm = jax.ShapeDtypeStruct((B_HEADS, s, FOX_W), BF16)
    tt = jax.ShapeDtypeStruct((B_HEADS, nt, FOX_W, t), BF16)
    return pl.pallas_call(
        body, name="fox_pack", grid=(nt,),
        in_specs=[tok(C_QB), tok(C_KB), tok(C_VB), pl.BlockSpec((B_HEADS, t, 1), lambda i: (0, i, 0))],
        out_specs=[rows, rows, tiles, rows, tiles] + [pl.BlockSpec((B_HEADS, None, 8, 128), lambda i: (0, i, 0, 0))] * 2,
        out_shape=[hm, hm, tt, hm, tt] + [jax.ShapeDtypeStruct((B_HEADS, nt, 8, 128), F32)] * 2,
        compiler_params=_params(("parallel",)),
    )(u, u, u, c_col)


def _fox_pack_bwd(dy, y, t):
    s = dy.shape[0]
    nt = s // t

    def body(do_ref, o_ref, dow_ref, dl_ref):
        lane = lax.broadcasted_iota(jnp.int32, (t, FOX_W), 1)
        lane8 = lax.broadcasted_iota(jnp.int32, (8, FOX_W), 1)
        for pr in range(B_HEADS // 2):
            pair = slice(pr * FOX_W, (pr + 1) * FOX_W)
            dov = do_ref[:, pair].astype(F32)
            parts = _split3(dov * o_ref[:, pair].astype(F32))
            for hh in range(2):
                dow_ref[2 * pr + hh] = jnp.where(lane < B_HD, _head_of_pair(dov, hh), 0.0).astype(BF16)
                mask = ((lane8 >= hh * B_HD) & (lane8 < (hh + 1) * B_HD)).astype(BF16)
                row = _dot(mask, parts[0], NT) + _dot(mask, parts[1], NT) + _dot(mask, parts[2], NT)
                dl_ref[2 * pr + hh] = row[0:1, :]

    tok = pl.BlockSpec((t, B_HEADS * B_HD), lambda i: (i, 0))
    return pl.pallas_call(
        body, name="fox_pack_bwd", grid=(nt,), in_specs=[tok, tok],
        out_specs=[pl.BlockSpec((B_HEADS, t, FOX_W), lambda i: (0, i, 0)),
                   pl.BlockSpec((B_HEADS, None, 1, t), lambda i: (0, i, 0, 0))],
        out_shape=[jax.ShapeDtypeStruct((B_HEADS, s, FOX_W), BF16), jax.ShapeDtypeStruct((B_HEADS, nt, 1, t), F32)],
        compiler_params=_params(("parallel",)),
    )(dy, y)


def _fox_unpack(dqt, dkw, dvw, du, t):
    h, nt = dqt.shape[:2]

    def body(dq_ref, dk_ref, dv_ref, _, o_ref, dc_ref):
        lane = lax.broadcasted_iota(jnp.int32, (t, FOX_W), 1)

        def join(a0, a1):
            return jnp.where(lane < B_HD, a0, pltpu.roll(a1, B_HD, 1))

        for hh in range(h):
            dc_ref[hh] = dq_ref[hh][FOX_ONE:FOX_ONE + 1, :] - dk_ref[hh].T[B_HD:B_HD + 1, :]
        pairs = range(0, h, 2)
        cols = ([join(dq_ref[a].T, dq_ref[a + 1].T) for a in pairs] + [join(dk_ref[a], dk_ref[a + 1]) for a in pairs]
                + [join(dv_ref[a], dv_ref[a + 1]) for a in pairs])
        o_ref[...] = jnp.concatenate(cols, axis=1).astype(o_ref.dtype)

    rows = pl.BlockSpec((h, t, FOX_W), lambda i: (0, i, 0))
    return pl.pallas_call(
        body, name="fox_unpack", grid=(nt,),
        in_specs=[pl.BlockSpec((h, None, FOX_W, t), lambda i: (0, i, 0, 0)), rows, rows,
                  pl.BlockSpec(memory_space=pl.ANY)],
        out_specs=[pl.BlockSpec((pl.Element(t), pl.Element(3 * h * B_HD)), lambda i: (i * t, C_QB)),
                   pl.BlockSpec((h, None, 1, t), lambda i: (0, i, 0, 0))],
        out_shape=[jax.ShapeDtypeStruct(du.shape, du.dtype), jax.ShapeDtypeStruct((h, nt, 1, t), F32)],
        input_output_aliases={3: 0},
        compiler_params=_params(("parallel",)),
    )(dqt, dkw, dvw, du)


FOX_DEAD = -110.0


def _fox_bounds(q2, k2, c, t):
    g = 2.0 * jnp.sqrt(1.02 * jnp.max(q2[:, :, 0, 0], axis=1) * 1.02 * jnp.max(k2[:, :, 0, 0], axis=1))
    return jnp.concatenate([c[:, ::t], c[:, t - 1::t], g[:, None]], axis=1)


SMEM_SPEC = pl.BlockSpec(memory_space=pltpu.SMEM)


def _fox_fwd(qf, kb, vt4, bounds, t):
    h, s, w = qf.shape
    nt = s // t
    sub = FOX_SUB_FWD
    nsub = t // sub
    nh = FOX_HEADS_PER_STEP

    def body(b_ref, q_ref, k_ref, v_ref, o_ref, lse_ref):
        i = pl.program_id(1)
        krow = lax.broadcasted_iota(jnp.int32, (sub, t), 0)
        qcol = lax.broadcasted_iota(jnp.int32, (sub, t), 1)

        def dead_before(hh):
            head = pl.program_id(0) * nh + hh
            top = b_ref[head, 2 * nt] + b_ref[head, i]
            return lax.fori_loop(
                0, i, lambda jj, n: n + (top - b_ref[head, nt + jj] < FOX_DEAD).astype(jnp.int32), 0)

        j_lo = functools.reduce(jnp.minimum, [dead_before(hh) for hh in range(nh)])

        def tile(j, carry, diag):
            out = []
            for hh in range(nh):
                m, acc = carry[hh]
                qv, vj = q_ref[hh], v_ref[hh, j]
                los = [b * sub if diag else 0 for b in range(nsub)]
                sts = [_dot(k_ref[hh, pl.ds(pl.multiple_of(j * t + b * sub, sub), sub), :], qv[lo:, :], NT)
                       for b, lo in enumerate(los)]
                for b, lo in enumerate(los):
                    st = sts[b]
                    if diag:
                        st = jnp.where(krow[:, :t - lo] <= qcol[:, :t - lo], st, NEG)
                    m_old, acc_old = m[:, lo:], acc[:, lo:]
                    m2 = jnp.maximum(m_old, jnp.max(st, axis=0, keepdims=True))
                    p = jnp.exp(st - m2).astype(BF16)
                    acc2 = jnp.exp(m_old - m2) * acc_old + _dot(vj[:, b * sub:(b + 1) * sub], p)
                    m = m2 if lo == 0 else jnp.concatenate([m[:, :lo], m2], axis=1)
                    acc = acc2 if lo == 0 else jnp.concatenate([acc[:, :lo], acc2], axis=1)
                out.append((m, acc))
            return tuple(out)

        init = tuple((jnp.full((1, t), NEG, F32), jnp.zeros((w, t), F32)) for _ in range(nh))
        carry = lax.fori_loop(j_lo, i, lambda j, c: tile(j, c, False), init)
        outs = []
        for hh, (m, acc) in enumerate(tile(i, carry, True)):
            den = acc[B_HD:B_HD + 1, :]
            outs.append(acc[0:B_HD, :] / den)
            lse_ref[hh] = m + jnp.log(den)
        o_ref[...] = jnp.concatenate(outs, axis=0).T.astype(o_ref.dtype)

    return pl.pallas_call(
        body, name="fox_fwd", grid=(h // nh, nt),
        in_specs=[SMEM_SPEC,
                  pl.BlockSpec((nh, t, w), lambda hh, i: (hh, i, 0)),
                  pl.BlockSpec((nh, s, w), lambda hh, i: (hh, 0, 0)),
                  pl.BlockSpec((nh, nt, w, t), lambda hh, i: (hh, 0, 0, 0))],
        out_specs=[pl.BlockSpec((t, nh * B_HD), lambda hh, i: (i, hh)),
                   pl.BlockSpec((nh, 1, t), lambda hh, i: (hh, 0, i))],
        out_shape=[jax.ShapeDtypeStruct((s, h * B_HD), BF16), jax.ShapeDtypeStruct((h, 1, s), F32)],
        compiler_params=_params(("parallel", "parallel")),
    )(bounds, qf, kb, vt4)


def _fox_bwd(qf, dow, lse_row, delta_row, kb, kst4, vb, bounds, t):
    h, s, w = qf.shape
    nt = s // t
    nsub = t // FOX_SUB
    nh = FOX_HEADS_PER_STEP

    def body(b_ref, q_ref, do_ref, lse_ref, dl_ref, k_ref, kt_ref, v_ref, dqt_ref, dk_ref, dv_ref, dk_acc, dv_acc):
        j = pl.program_id(1)

        def alive_after(hh):
            head = pl.program_id(0) * nh + hh
            top = b_ref[head, 2 * nt] - b_ref[head, nt + j]
            return lax.fori_loop(
                j + 1, nt, lambda ii, n: n + (top + b_ref[head, ii] >= FOX_DEAD).astype(jnp.int32), 0)

        i_hi = j + 1 + functools.reduce(jnp.maximum, [alive_after(hh) for hh in range(nh)])

        @pl.when(j == 0)
        def _():
            dqt_ref[...] = jnp.zeros_like(dqt_ref)

        dk_acc[...] = jnp.zeros_like(dk_acc)
        dv_acc[...] = jnp.zeros_like(dv_acc)
        krow = lax.broadcasted_iota(jnp.int32, (FOX_SUB, t), 0)
        qcol = lax.broadcasted_iota(jnp.int32, (FOX_SUB, t), 1)
        subs = [slice(b * FOX_SUB, (b + 1) * FOX_SUB) for b in range(nsub)]

        def tile(i, diag):
            i0 = pl.multiple_of(i * t, t)
            for hh in range(nh):
                qi, doi = q_ref[hh, pl.ds(i0, t), :], do_ref[hh, pl.ds(i0, t), :]
                lse, dl = lse_ref[hh, i], dl_ref[hh, i]
                los = [b * FOX_SUB if diag else 0 for b in range(nsub)]
                sts = [_dot(k_ref[hh, rs, :], qi[lo:, :], NT) for rs, lo in zip(subs, los)]
                dps = [_dot(v_ref[hh, rs, :], doi[lo:, :], NT) for rs, lo in zip(subs, los)]
                dq = None
                for b, (rs, lo) in enumerate(zip(subs, los)):
                    st = sts[b] - lse[:, lo:]
                    if diag:
                        st = jnp.where(krow[:, :t - lo] <= qcol[:, :t - lo], st, NEG)
                    pt = jnp.exp(st)
                    dsb = (pt * (dps[b] - dl[:, lo:])).astype(BF16)
                    dv_acc[hh, rs, :] += _dot(pt.astype(BF16), doi[lo:, :])
                    dk_acc[hh, rs, :] += _dot(dsb, qi[lo:, :])
                    part = _dot(kt_ref[hh, :, rs], dsb)
                    if lo:
                        part = jnp.concatenate([jnp.zeros((w, lo), F32), part], axis=1)
                    dq = part if dq is None else dq + part
                dqt_ref[hh, i] += dq

        def step(i, carry):
            tile(i, False)
            return carry

        tile(j, True)
        lax.fori_loop(j + 1, i_hi, step, 0)
        dk_ref[...] = dk_acc[...] * (B_HD ** -0.5)
        dv_ref[...] = dv_acc[...]

    full = pl.BlockSpec((nh, s, w), lambda hh, j: (hh, 0, 0))
    rowst = pl.BlockSpec((nh, nt, 1, t), lambda hh, j: (hh, 0, 0, 0))
    tl = pl.BlockSpec((nh, t, w), lambda hh, j: (hh, j, 0))
    return pl.pallas_call(
        body, name="fox_bwd", grid=(h // nh, nt),
        in_specs=[SMEM_SPEC, full, full, rowst, rowst, tl,
                  pl.BlockSpec((nh, None, w, t), lambda hh, j: (hh, j, 0, 0)), tl],
        out_specs=[pl.BlockSpec((nh, nt, w, t), lambda hh, j: (hh, 0, 0, 0)), tl, tl],
        out_shape=[jax.ShapeDtypeStruct((h, nt, w, t), F32), jax.ShapeDtypeStruct((h, s, w), F32),
                   jax.ShapeDtypeStruct((h, s, w), F32)],
        scratch_shapes=[pltpu.VMEM((nh, t, w), F32), pltpu.VMEM((nh, t, w), F32)],
        compiler_params=_params(("parallel", "arbitrary")),
    )(bounds, qf, dow, lse_row, delta_row, kb, kst4, vb)


def _mem_fwd(u, mkv, tq=1024):
    s = u.shape[0]
    scale = HD ** -0.5

    def body(q_ref, mk_ref, mv_ref, o_ref, lse_ref):
        lses = []
        for h in range(4):
            cs = slice(h * HD, (h + 1) * HD)
            sc = _dot(q_ref[:, cs], mk_ref[:, cs], NT) * scale
            m = jnp.max(sc, axis=-1, keepdims=True)
            p = jnp.exp(sc - m)
            den = jnp.sum(p, axis=-1, keepdims=True)
            o_ref[:, cs] = (_dot(p.astype(BF16), mv_ref[:, cs]) / den).astype(o_ref.dtype)
            lses.append(m + jnp.log(den))
        lse_ref[...] = _lane_pack(lses, (tq, HD))

    return pl.pallas_call(
        body, name="mem_fwd", grid=(s // tq,),
        in_specs=[pl.BlockSpec((tq, 512), lambda i: (i, C_QM // 512)),
                  pl.BlockSpec((N_MEM, 512), lambda i: (0, 0)),
                  pl.BlockSpec((N_MEM, 512), lambda i: (0, 1))],
        out_specs=[pl.BlockSpec((tq, 512), lambda i: (i, 0)), pl.BlockSpec((tq, HD), lambda i: (i, 0))],
        out_shape=[jax.ShapeDtypeStruct((s, 512), BF16), jax.ShapeDtypeStruct((s, HD), F32)],
        compiler_params=_params(("parallel",)),
    )(u, mkv, mkv)


def _mem_bwd(u, mkv, o, do, lse, du, tq=1024):
    s = u.shape[0]
    scale = HD ** -0.5

    def body(q_ref, mk_ref, mv_ref, o_ref, do_ref, lse_ref, _, dq_ref, dmk_ref, dmv_ref):
        @pl.when(pl.program_id(0) == 0)
        def _():
            dmk_ref[...] = jnp.zeros_like(dmk_ref)
            dmv_ref[...] = jnp.zeros_like(dmv_ref)

        for h in range(4):
            cs = slice(h * HD, (h + 1) * HD)
            qv, dov = q_ref[:, cs], do_ref[:, cs]
            sc = _dot(qv, mk_ref[:, cs], NT) * scale
            p = jnp.exp(sc - lse_ref[:, h:h + 1])
            delta = jnp.sum(dov.astype(F32) * o_ref[:, cs].astype(F32), axis=-1, keepdims=True)
            ds = p * (_dot(dov, mv_ref[:, cs], NT) - delta)
            dsb = ds.astype(BF16)
            dq_ref[:, cs] = (_dot(dsb, mk_ref[:, cs]) * scale).astype(dq_ref.dtype)
            dmk_ref[:, cs] += _dot(dsb, qv, TN) * scale
            dmv_ref[:, cs] += _dot(p.astype(BF16), dov, TN)

    row = pl.BlockSpec((tq, 512), lambda i: (i, 0))
    acc = pl.BlockSpec((N_MEM, 512), lambda i: (0, 0))
    return pl.pallas_call(
        body, name="mem_bwd", grid=(s // tq,),
        in_specs=[pl.BlockSpec((tq, 512), lambda i: (i, C_QM // 512)),
                  pl.BlockSpec((N_MEM, 512), lambda i: (0, 0)),
                  pl.BlockSpec((N_MEM, 512), lambda i: (0, 1)),
                  row, row, pl.BlockSpec((tq, HD), lambda i: (i, 0)), pl.BlockSpec(memory_space=pl.ANY)],
        out_specs=[pl.BlockSpec((tq, 512), lambda i: (i, C_QM // 512)), acc, acc],
        out_shape=[jax.ShapeDtypeStruct(du.shape, du.dtype), jax.ShapeDtypeStruct((N_MEM, 512), F32),
                   jax.ShapeDtypeStruct((N_MEM, 512), F32)],
        input_output_aliases={6: 0},
        compiler_params=_params(("arbitrary",)),
    )(u, mkv, mkv, o, do, lse, du)


FB_CHIP = FB_ORIG // SHARD_COLS
FB_AT = FB_ORIG - FB_CHIP * SHARD_COLS


def _chip_slabs(main, fb):
    cuts = [SHARD_COLS * p - (B_HEADS if p > FB_CHIP else 0) for p in range(N_CHIPS + 1)]
    slabs = [main[:, a:b] for a, b in zip(cuts[:-1], cuts[1:])]
    own = slabs[FB_CHIP]
    slabs[FB_CHIP] = jnp.concatenate([own[:, :FB_AT], fb, own[:, FB_AT:]], axis=1)
    return slabs


def _split_forget(slabs):
    own = slabs[FB_CHIP]
    parts = list(slabs[:FB_CHIP]) + [own[:, :FB_AT], own[:, FB_AT + B_HEADS:]] + list(slabs[FB_CHIP + 1:])
    return jnp.concatenate(parts, axis=1), own[:, FB_AT:FB_AT + B_HEADS]


def _local_step(x, mem, pos, target, g_pre, g_post, g_mem, w_main, w_fb, b_forget, b_merge,
                w_mem_kv, w_ba, w_bb, w_bm, w_out, exchange=None):
    s = x.shape[0]
    t_fox = min(512, s)
    nt = s // t_fox
    half = ROT_DIM // 2
    inv = ROPE_THETA ** (-jnp.arange(half, dtype=F32) / half)
    inv128 = jnp.concatenate([inv, inv, jnp.zeros((HD - ROT_DIM,), F32)]).reshape(1, HD)

    h, h_t = _rms_fwd_both("norm_pre", x, g_pre)
    u = _mm("proj_in", h, w_main, "nn", BF16, tm=4096)
    ufb = _mm("proj_fb", h, w_fb, "nn", F32)
    memn = _rms_fwd("norm_mem", mem, g_mem)
    mkv = _mm("proj_mem", memn, w_mem_kv, "nn", BF16)

    qkv = _rope_fwd(u, pos, inv128)
    views = [tuple(qkv[3 * g:3 * g + 3]) for g in range(3)]
    os_, lses = [], []
    for g, d in enumerate(DILATIONS):
        o_g, lse_g = _band_fwd("band_fwd%d" % g, *views[g], d)
        os_.append((o_g, d * A_GROUP, 0, d))
        lses.append((lse_g, d * HD, 0, d))

    def merge_a(o1, o2, o3, l1, l2, l3, za, *scr):
        o1, o2, o3 = [_from_class(o, scr, d) for o, d in zip((o1, o2, o3), DILATIONS)]
        l1, l2, l3 = [_from_class(lv, scr, d) for lv, d in zip((l1, l2, l3), DILATIONS)]
        ys, tots = [], []
        for hh in range(4):
            cs, hs = slice(hh * HD, (hh + 1) * HD), slice(hh, hh + 1)
            mx = jnp.maximum(jnp.maximum(l1[:, hs], l2[:, hs]), l3[:, hs])
            e1, e2, e3 = jnp.exp(l1[:, hs] - mx), jnp.exp(l2[:, hs] - mx), jnp.exp(l3[:, hs] - mx)
            den = e1 + e2 + e3
            ys.append((e1 * o1[:, cs] + e2 * o2[:, cs] + e3 * o3[:, cs]) / den)
            tots.append(mx + jnp.log(den))
        y = jnp.concatenate(ys, axis=1)
        zf = za.astype(F32)
        tot = _lane_pack(tots, l1.shape)
        return (y, y * (zf * _sig(zf))) + tuple(_to_class(tot, scr, d) for d in DILATIONS)

    res = _rows("merge_a", merge_a, os_ + lses + [(u, 512, C_ZA // 512)], [],
                [(512, BF16), (512, BF16)] + [(d * HD, F32, d) for d in DILATIONS], tm=ROPE_TM,
                scratch=_class_scratch(ROPE_TM))
    y_a, yg_a, lse_a = res[0], res[1], res[2:5]

    zrow = ufb[:, :B_HEADS].T
    c = _fox_prep(zrow, b_forget.reshape(B_HEADS, 1))
    qf, kb, kst4, vb, vt4, q2, k2 = _fox_pack(u, c.reshape(B_HEADS, s, 1), t_fox)
    bounds = _fox_bounds(q2, k2, c, t_fox)
    y_b, lse_b = _fox_fwd(qf, kb, vt4, bounds, t_fox)

    y_m, lse_m = _mem_fwd(u, mkv)

    def gate(y, z):
        zf = z.astype(F32)
        return (y.astype(F32) * (zf * _sig(zf)),)

    yg_b = _rows("gate_b", gate, [y_b, (u, 512, C_ZB // 512)], [], [(512, BF16)])[0]
    yg_m = _rows("gate_m", gate, [y_m, (u, 512, C_ZM // 512)], [], [(512, BF16)])[0]

    br_a = _mm("branch_a", yg_a, w_ba, "nn", BF16)
    br_b = _mm("branch_b", yg_b, w_bb, "nn", BF16)
    br_m = _mm("branch_m", yg_m, w_bm, "nn", BF16)
    gl = [(u, 1024, C_GL // 1024 + i) for i in range(3)]
    bm3 = b_merge.reshape(3, D_MODEL)

    def merge(g0, g1, g2, b0, b1, b2, bm):
        tot = 0.0
        for i, (gv, bv) in enumerate(((g0, b0), (g1, b1), (g2, b2))):
            tot = tot + _sig(gv.astype(F32) + bm[i:i + 1, :]) * bv.astype(F32)
        return (tot,)

    merged = _rows("merge_gates", merge, gl + [br_a, br_b, br_m], [bm3], [(D_MODEL, BF16)])[0]
    out = _mm("proj_out", merged, w_out, "nn", F32)

    def tail(xv, ov, tv, gv):
        r = lax.rsqrt(jnp.mean(ov * ov, axis=-1, keepdims=True) + EPS)
        n = ov * r
        err = xv + n * gv - tv
        dy = err * (1.0 / D_MODEL)
        dn = dy * gv
        dout = r * (dn - n * jnp.mean(dn * n, axis=-1, keepdims=True))
        return (dy, dout, jnp.sum(0.5 * err * err * (1.0 / D_MODEL), axis=0, keepdims=True),
                jnp.sum(dy * n, axis=0, keepdims=True))

    dy, dout, loss_lanes, g_post_grad = _rows(
        "tail", tail, [x, out, target], [g_post], [(D_MODEL, F32), (D_MODEL, BF16)],
        reds=[D_MODEL, D_MODEL], tm=512)

    dmerged = _mm("d_merged", dout, w_out, "nt", BF16)
    gw_out = _mm("g_w_out", merged, dout, "tn", F32, tk=2048)

    def merge_bwd(dm, g0, g1, g2, b0, b1, b2, bm):
        dmf = dm.astype(F32)
        dbs, dgs, sums = [], [], []
        for i, (gv, bv) in enumerate(((g0, b0), (g1, b1), (g2, b2))):
            sg = _sig(gv.astype(F32) + bm[i:i + 1, :])
            dbs.append(dmf * sg)
            dg = dmf * bv.astype(F32) * sg * (1.0 - sg)
            dgs.append(dg)
            sums.append(jnp.sum(dg, axis=0, keepdims=True))
        return tuple(dbs + [jnp.concatenate(dgs, axis=1)] + sums)

    du = lax.empty(u.shape, BF16)
    res = _rows("merge_bwd", merge_bwd, [dmerged] + gl + [br_a, br_b, br_m], [bm3],
                [(D_MODEL, BF16)] * 3 + [(3 * D_MODEL, BF16)], reds=[D_MODEL] * 3, tm=512,
                into=(du, 3, ("column", C_GL)))
    dbr, du, g_bmerge = res[0:3], res[3], jnp.concatenate(res[4:7], axis=1)

    dyg, gw_branch = [], []
    for nm, dbv, wv, ygv in (("a", dbr[0], w_ba, yg_a), ("b", dbr[1], w_bb, yg_b), ("m", dbr[2], w_bm, yg_m)):
        dyg.append(_mm("d_yg_" + nm, dbv, wv, "nt", BF16))
        gw_branch.append(_mm("g_w_branch_" + nm, ygv, dbv, "tn", F32, tk=2048))

    def gate_bwd(dg, y, z):
        dgf, yf, zf = dg.astype(F32), y.astype(F32), z.astype(F32)
        sg = _sig(zf)
        return dgf * (zf * sg), dgf * yf * (sg * (1.0 + zf * (1.0 - sg)))

    def gate_bwd_a(dg, y, z, *scr):
        dyv, dz = gate_bwd(dg, y, z)
        prod = dyv * y.astype(F32)
        dl = [jnp.sum(prod[:, hh * HD:(hh + 1) * HD], axis=-1, keepdims=True) for hh in range(4)]
        delta = _lane_pack(dl, (dg.shape[0], HD))
        return ((dz,) + tuple(_to_class(dyv, scr, d) for d in DILATIONS)
                + tuple(_to_class(delta, scr, d) for d in DILATIONS))

    res = _rows("gate_bwd_a", gate_bwd_a, [dyg[0], y_a, (u, 512, C_ZA // 512)], [],
                [(512, BF16)] + [(d * A_GROUP, BF16, d) for d in DILATIONS] + [(d * HD, F32, d) for d in DILATIONS],
                tm=ROPE_TM, scratch=_class_scratch(ROPE_TM), into=(du, 0, C_ZA // 512))
    du, dy_a, delta_a = res[0], res[1:4], res[4:7]
    dy_b, du = _rows("gate_bwd_b", gate_bwd, [dyg[1], y_b, (u, 512, C_ZB // 512)], [],
                     [(512, BF16), (512, BF16)], into=(du, 1, C_ZB // 512))
    dy_m, du = _rows("gate_bwd_m", gate_bwd, [dyg[2], y_m, (u, 512, C_ZM // 512)], [],
                     [(512, BF16), (512, BF16)], into=(du, 1, C_ZM // 512))

    du, dmk, dmv = _mem_bwd(u, mkv, y_m, dy_m, lse_m, du)
    dmkv = jnp.concatenate([dmk, dmv], axis=1)
    gw_mem_kv = _mm("g_w_mem_kv", memn, dmkv, "tn", F32)
    dmemn = _mm("d_memn", dmkv, w_mem_kv, "nt", F32)

    def mem_gain_grad(mv, dv):
        r = lax.rsqrt(jnp.mean(mv * mv, axis=-1, keepdims=True) + EPS)
        return (jnp.sum(dv * mv * r, axis=0, keepdims=True),)

    g_mem_grad = _rows("g_norm_mem", mem_gain_grad, [mem, dmemn], [], [], reds=[D_MODEL], tm=N_MEM)[0]

    dow, delta_b = _fox_pack_bwd(dy_b, y_b, t_fox)
    dqt, dkw, dvw = _fox_bwd(qf, dow, lse_b.reshape(B_HEADS, nt, 1, t_fox), delta_b, kb, kst4, vb, bounds, t_fox)
    du, dc = _fox_unpack(dqt, dkw, dvw, du, t_fox)
    dzrow, g_bforget = _fox_prep_bwd(dc.reshape(B_HEADS, s), zrow, b_forget.reshape(B_HEADS, 1))
    dfb = jnp.zeros((s, HD), BF16).at[:, :B_HEADS].set(dzrow.T.astype(BF16))

    dqs, dks, dvs = [], [], []
    for g, d in enumerate(DILATIONS):
        qv, kv, vv = views[g]
        dqs.append(_band_dq("band_dq%d" % g, qv, kv, vv, dy_a[g], lse_a[g], delta_a[g], d))
        dk_g, dv_g = _band_dkv("band_dkv%d" % g, qv, kv, vv, dy_a[g], lse_a[g], delta_a[g], d)
        dks.append(dk_g)
        dvs.append(dv_g)
    du = _rope_bwd(dqs, dks, dvs, pos, inv128, du)

    gw_main = _mm("g_w_main", h_t, du, "nn", F32, tk=2048)
    gw_fb = _mm("g_w_fb", h, dfb, "tn", F32)
    grads = dict(norm_post_g=g_post_grad, norm_mem_g=g_mem_grad, w_in=_chip_slabs(gw_main, gw_fb[:, :B_HEADS]),
                 b_forget=g_bforget.reshape(1, B_HEADS), b_merge=g_bmerge, w_mem_kv=gw_mem_kv,
                 w_branch_a=gw_branch[0], w_branch_b=gw_branch[1], w_branch_m=gw_branch[2], w_out=gw_out)
    side = exchange(grads) if exchange else None
    dh_main = _mm("d_h", du, w_main, "nt", F32, tm=1024, tk=2816, side=side)
    landed = None
    if side:
        dh_main, landed = dh_main[0], dh_main[1:]
    dh_fb = _mm("d_h_fb", dfb, w_fb, "nt", F32)

    def pre_bwd(xv, d1, d2, dyv, gv):
        r = lax.rsqrt(jnp.mean(xv * xv, axis=-1, keepdims=True) + EPS)
        n = xv * r
        dhv = d1 + d2
        dn = dhv * gv
        dx = r * (dn - n * jnp.mean(dn * n, axis=-1, keepdims=True))
        return dyv + dx, jnp.sum(dhv * n, axis=0, keepdims=True)

    grad_x, g_pre_grad = _rows("norm_pre_bwd", pre_bwd, [x, dh_main, dh_fb, dy], [g_pre],
                               [(D_MODEL, F32)], reds=[D_MODEL], tm=512)

    grads["norm_pre_g"] = g_pre_grad
    return loss_lanes, grad_x, grads, landed


HBM_SPEC = pl.BlockSpec(memory_space=pltpu.HBM)


def _place():
    x, y, c = lax.axis_index("x"), lax.axis_index("y"), lax.axis_index("c")
    chips = [(1 - x, y), (x, 1 - y), (1 - x, 1 - y)]
    return x, y, c, 2 * x + y, chips


N_CHUNKS = 4


def _units(parts, row_axis):
    units = []
    for i, a in enumerate(parts):
        ch = a.shape[row_axis] // N_CHUNKS
        units += [(i, pl.ds(k * ch, ch)) for k in range(N_CHUNKS)]
    return units


def _gather_weights(parts):
    n = len(parts)
    units = _units(parts, 1)
    nu = len(units)
    via_y = [(u % N_CHUNKS) < N_CHUNKS // 2 for u in range(nu)]

    def body(*refs):
        srcs, outs = refs[:n], refs[n:2 * n]
        send_sems, recv_sems = refs[2 * n:]
        x, y, c, p, _ = _place()
        me, sib = (x, y, c), (x, y, 1 - c)
        xn, yn, dg = (1 - x, y), (x, 1 - y), (1 - x, 1 - y)

        def cp(u, k, chip, half, to, from_src=False):
            i, rs = units[u]
            dst = outs[i].at[2 * chip[0] + chip[1], half, rs]
            return pltpu.make_async_remote_copy(
                src_ref=srcs[i].at[half, rs] if from_src else dst, dst_ref=dst, send_sem=send_sems.at[u, k],
                recv_sem=recv_sems.at[u, k], device_id=to, device_id_type=MESH)

        sent = []

        def go(copy):
            copy.start()
            sent.append(copy)

        for u in range(nu):
            go(cp(u, 0, (x, y), c, (*xn, c), from_src=True))
            go(cp(u, 1, (x, y), c, (*yn, c), from_src=True))
        for u in range(nu):
            cp(u, 0, xn, c, me).wait_recv()
            go(cp(u, 4, xn, c, sib))
            if via_y[u]:
                go(cp(u, 2, xn, c, (*yn, c)))
            cp(u, 1, yn, c, me).wait_recv()
            go(cp(u, 5, yn, c, sib))
            if not via_y[u]:
                go(cp(u, 3, yn, c, (*xn, c)))
        for u in range(nu):
            cp(u, 2 if via_y[u] else 3, dg, c, me).wait_recv()
            go(cp(u, 6, dg, c, sib))
        for u in range(nu):
            for k, chip in ((4, xn), (5, yn), (6, dg)):
                cp(u, k, chip, 1 - c, me).wait_recv()
        for copy in sent:
            copy.wait_send()

    return pl.pallas_call(
        body, name="gather_weights", in_specs=[HBM_SPEC] * n, out_specs=[HBM_SPEC] * n,
        out_shape=[jax.ShapeDtypeStruct((N_CHIPS,) + a.shape, a.dtype) for a in parts],
        scratch_shapes=[pltpu.SemaphoreType.DMA((nu, 7)), pltpu.SemaphoreType.DMA((nu, 7))],
    )(*parts)


def _swap_with_sibling(parts):
    n = len(parts)
    units = _units(parts, 2)

    def body(*refs):
        srcs, outs = refs[:n], refs[n:2 * n]
        send_sems, recv_sems = refs[2 * n:]
        x, y, c, _, _ = _place()
        cps = [pltpu.make_async_remote_copy(
            src_ref=srcs[i].at[q, 1 - c, rs], dst_ref=outs[i].at[q, rs], send_sem=send_sems.at[u, q],
            recv_sem=recv_sems.at[u, q], device_id=(x, y, 1 - c), device_id_type=MESH)
            for q in range(N_CHIPS) for u, (i, rs) in enumerate(units)]
        for cpy in cps:
            cpy.start()
        for cpy in cps:
            cpy.wait()

    return pl.pallas_call(
        body, name="swap_with_sibling", in_specs=[HBM_SPEC] * n, out_specs=[HBM_SPEC] * n,
        out_shape=[jax.ShapeDtypeStruct(a.shape[:1] + a.shape[2:], a.dtype) for a in parts],
        scratch_shapes=[pltpu.SemaphoreType.DMA((len(units), N_CHIPS)),
                        pltpu.SemaphoreType.DMA((len(units), N_CHIPS))],
    )(*parts)


def _scatter_to_owners(parts):
    n = len(parts)
    units = _units(parts, 1)

    def copies(srcs, outs, send_sems, recv_sems, incoming):
        x, y, c, p, chips = _place()
        return [pltpu.make_async_remote_copy(
            src_ref=srcs[i].at[2 * cx + cy, rs], dst_ref=outs[i].at[(2 * cx + cy) if incoming else p, rs],
            send_sem=send_sems.at[u, j], recv_sem=recv_sems.at[u, j], device_id=(cx, cy, c), device_id_type=MESH)
            for u, (i, rs) in enumerate(units) for j, (cx, cy) in enumerate(chips)]

    def start(ins, outs, scratch):
        for cpy in copies(ins, outs, *scratch, incoming=False):
            cpy.start()

    def wait(ins, outs, scratch):
        for cpy in copies(ins, outs, *scratch, incoming=True):
            cpy.wait_recv()
        for cpy in copies(ins, outs, *scratch, incoming=False):
            cpy.wait_send()

    return dict(ins=list(parts), outs=[jax.ShapeDtypeStruct(a.shape, a.dtype) for a in parts],
                scratch=[pltpu.SemaphoreType.DMA((len(units), 3)), pltpu.SemaphoreType.DMA((len(units), 3))],
                start=start, wait=wait)


def _share_with_sibling(parts):
    n = len(parts)
    units = _units(parts, 1)

    def body(*refs):
        srcs, outs = refs[:n], refs[n:2 * n]
        send_sems, recv_sems = refs[2 * n:]
        x, y, c, _, _ = _place()
        sends = [pltpu.make_async_remote_copy(
            src_ref=srcs[i].at[0, rs], dst_ref=outs[i].at[c, rs], send_sem=send_sems.at[u],
            recv_sem=recv_sems.at[u], device_id=(x, y, 1 - c), device_id_type=MESH)
            for u, (i, rs) in enumerate(units)]
        for cpy in sends:
            cpy.start()
        for u, (i, rs) in enumerate(units):
            pltpu.make_async_remote_copy(
                src_ref=srcs[i].at[0, rs], dst_ref=outs[i].at[1 - c, rs], send_sem=send_sems.at[u],
                recv_sem=recv_sems.at[u], device_id=(x, y, 1 - c), device_id_type=MESH).wait_recv()
        for cpy in sends:
            cpy.wait_send()

    return pl.pallas_call(
        body, name="share_with_sibling", in_specs=[HBM_SPEC] * n, out_specs=[HBM_SPEC] * n,
        out_shape=[jax.ShapeDtypeStruct((2,) + a.shape[1:], a.dtype) for a in parts],
        scratch_shapes=[pltpu.SemaphoreType.DMA((len(units),)), pltpu.SemaphoreType.DMA((len(units),))],
    )(*parts)


def _sum_small(v):
    def body(v_ref, out_ref, buf, send_sems, recv_sems):
        x, y, c, _, _ = _place()
        me = 4 * x + 2 * y + c
        buf[me] = v_ref[...]
        flips = [(dx, dy, dc) for dx in (0, 1) for dy in (0, 1) for dc in (0, 1)][1:]
        sends = []
        for k, (dx, dy, dc) in enumerate(flips):
            cpy = pltpu.make_async_remote_copy(
                src_ref=v_ref, dst_ref=buf.at[me], send_sem=send_sems.at[k], recv_sem=recv_sems.at[k],
                device_id=((x + dx) % 2, (y + dy) % 2, (c + dc) % 2), device_id_type=MESH)
            cpy.start()
            sends.append(cpy)
        for k, (dx, dy, dc) in enumerate(flips):
            px, py, pc = (x + dx) % 2, (y + dy) % 2, (c + dc) % 2
            pltpu.make_async_remote_copy(
                src_ref=v_ref, dst_ref=buf.at[4 * px + 2 * py + pc], send_sem=send_sems.at[k],
                recv_sem=recv_sems.at[k], device_id=(px, py, pc), device_id_type=MESH).wait_recv()
        for cpy in sends:
            cpy.wait_send()
        tot = buf[0]
        for i in range(1, N_DEV):
            tot = tot + buf[i]
        out_ref[...] = tot

    return pl.pallas_call(
        body, name="sum_small", out_shape=jax.ShapeDtypeStruct(v.shape, v.dtype),
        in_specs=[pl.BlockSpec(memory_space=pltpu.VMEM)], out_specs=pl.BlockSpec(memory_space=pltpu.VMEM),
        scratch_shapes=[pltpu.VMEM((N_DEV,) + v.shape, v.dtype), pltpu.SemaphoreType.DMA((N_DEV - 1,)),
                        pltpu.SemaphoreType.DMA((N_DEV - 1,))],
    )(v)


def _add_chips(name, landed, pair, chip):
    nq, r, w = landed.shape
    tr = 128 if r % 128 == 0 else 64

    def body(chip_ref, *refs):
        own = refs[nq][...].astype(F32)
        tot = None
        for q in range(nq):
            term = jnp.where(chip_ref[0] == q, own, refs[q][...].astype(F32))
            tot = term if tot is None else tot + term
        refs[nq + 1][...] = tot

    specs = [pl.BlockSpec((None, tr, w), functools.partial(lambda j, chip_ref, q: (q, j, 0), q=q)) for q in range(nq)]
    specs.append(pl.BlockSpec((None, tr, w), lambda j, chip_ref: (chip_ref[0], j, 0)))
    grid_spec = pltpu.PrefetchScalarGridSpec(
        num_scalar_prefetch=1, grid=(r // tr,), in_specs=specs,
        out_specs=pl.BlockSpec((None, tr, w), lambda j, chip_ref: (0, j, 0)))
    return pl.pallas_call(
        body, name=name, grid_spec=grid_spec, out_shape=jax.ShapeDtypeStruct((1, r, w), F32),
        compiler_params=_params(("parallel",)),
    )(jnp.reshape(chip, (1,)).astype(jnp.int32), *([landed] * nq), pair)


def _add_pair(name, halves, got, c):
    nq, _, r, w = halves.shape
    tr = 128 if r % 128 == 0 else 64

    def body(c_ref, a_ref, b_ref, o_ref):
        o_ref[...] = (a_ref[...] + b_ref[...]).astype(o_ref.dtype)

    grid_spec = pltpu.PrefetchScalarGridSpec(
        num_scalar_prefetch=1, grid=(nq, r // tr),
        in_specs=[pl.BlockSpec((None, None, tr, w), lambda i, j, c_ref: (i, c_ref[0], j, 0)),
                  pl.BlockSpec((None, tr, w), lambda i, j, c_ref: (i, j, 0))],
        out_specs=pl.BlockSpec((None, tr, w), lambda i, j, c_ref: (i, j, 0)))
    return pl.pallas_call(
        body, name=name, grid_spec=grid_spec, out_shape=jax.ShapeDtypeStruct((nq, r, w), BF16),
        compiler_params=_params(("parallel", "parallel")),
    )(jnp.reshape(c, (1,)).astype(jnp.int32), halves, got)


def _adamw(name, w, g, m, v, tm):
    def fn(wv, gv, mv, vv):
        m2 = ADAM_B1 * mv + (1.0 - ADAM_B1) * gv
        v2 = ADAM_B2 * vv + (1.0 - ADAM_B2) * (gv * gv)
        m_hat = m2 / (1.0 - ADAM_B1 ** ADAM_STEP)
        v_hat = v2 / (1.0 - ADAM_B2 ** ADAM_STEP)
        return -ADAM_LR * (m_hat / (jnp.sqrt(v_hat) + ADAM_EPS) + ADAM_WD * wv), m2, v2
    c = w.shape[1]
    return _rows(name, fn, [w, g, m, v], [], [(c, F32)] * 3, tm=tm)


REST_ROWS = 256 + 3 * 128 + 256
REST_SPLITS = (("w_mem_kv", 0, 256), ("w_branch_a", 256, 128), ("w_branch_b", 384, 128),
               ("w_branch_m", 512, 128), ("w_out", 640, 256))


def _rest_pack(t):
    return jnp.concatenate([t[n].reshape(rows, D_MODEL) for n, _, rows in REST_SPLITS], axis=0)


def _rest_unpack(a, shapes):
    return {n: a[r0:r0 + rows].reshape(shapes[n]) for n, r0, rows in REST_SPLITS}


def _small_pack(pre, post, memg, bforget, bmerge):
    pad = jnp.zeros((1, D_MODEL - B_HEADS), F32)
    return jnp.concatenate([pre, post, memg, bmerge.reshape(3, D_MODEL),
                            jnp.concatenate([bforget, pad], axis=1), jnp.zeros((1, D_MODEL), F32)], axis=0)


def _small_unpack(s8):
    return dict(norm_pre_g=s8[0:1], norm_post_g=s8[1:2], norm_mem_g=s8[2:3],
                b_merge=s8[3:6].reshape(1, 3 * D_MODEL), b_forget=s8[6:7, :B_HEADS])


WEIGHTS = ("norm_pre_g", "norm_post_g", "norm_mem_g", "w_in", "b_forget", "b_merge", "w_mem_kv",
           "w_branch_a", "w_branch_b", "w_branch_m", "w_out")
SMALL = ("norm_pre_g", "norm_post_g", "norm_mem_g", "b_forget", "b_merge")


def kernel(x, mem, positions, norm_pre_g, norm_post_g, norm_mem_g, w_in, b_forget, b_merge, w_mem_kv, w_branch_a, w_branch_b, w_branch_m, w_out, loss_target, m_norm_pre_g, m_norm_post_g, m_norm_mem_g, m_w_in, m_b_forget, m_b_merge, m_w_mem_kv, m_w_branch_a, m_w_branch_b, m_w_branch_m, m_w_out, v_norm_pre_g, v_norm_post_g, v_norm_mem_g, v_w_in, v_b_forget, v_b_merge, v_w_mem_kv, v_w_branch_a, v_w_branch_b, v_w_branch_m, v_w_out):
    w = dict(norm_pre_g=norm_pre_g, norm_post_g=norm_post_g, norm_mem_g=norm_mem_g, w_in=w_in[0],
             b_forget=b_forget, b_merge=b_merge, w_mem_kv=w_mem_kv[0], w_branch_a=w_branch_a[0],
             w_branch_b=w_branch_b[0], w_branch_m=w_branch_m[0], w_out=w_out[0])
    mo = dict(norm_pre_g=m_norm_pre_g, norm_post_g=m_norm_post_g, norm_mem_g=m_norm_mem_g, w_in=m_w_in[0],
              b_forget=m_b_forget, b_merge=m_b_merge, w_mem_kv=m_w_mem_kv[0], w_branch_a=m_w_branch_a[0],
              w_branch_b=m_w_branch_b[0], w_branch_m=m_w_branch_m[0], w_out=m_w_out[0])
    vo = dict(norm_pre_g=v_norm_pre_g, norm_post_g=v_norm_post_g, norm_mem_g=v_norm_mem_g, w_in=v_w_in[0],
              b_forget=v_b_forget, b_merge=v_b_merge, w_mem_kv=v_w_mem_kv[0], w_branch_a=v_w_branch_a[0],
              w_branch_b=v_w_branch_b[0], w_branch_m=v_w_branch_m[0], w_out=v_w_out[0])
    s = x.shape[1]
    c = lax.axis_index("c")

    chip = 2 * lax.axis_index("x") + lax.axis_index("y")

    def put(whole, own, slot):
        return lax.dynamic_update_index_in_dim(whole, own.astype(whole.dtype), slot, 0)

    own_w = [w["w_in"].astype(BF16).reshape(2, D_MODEL // 2, SHARD_COLS),
             _rest_pack(w).astype(BF16).reshape(2, REST_ROWS // 2, D_MODEL)]
    all_in, all_rest = _gather_weights(own_w)
    all_in = all_in.reshape(N_CHIPS, D_MODEL, SHARD_COLS)
    own_in, own_rest = own_w[0].reshape(D_MODEL, SHARD_COLS), own_w[1].reshape(REST_ROWS, D_MODEL)
    w_main, w_fb = _split_forget([jnp.where(chip == p, own_in, all_in[p]) for p in range(N_CHIPS)])
    w_fb = jnp.concatenate([w_fb, jnp.zeros((D_MODEL, HD - B_HEADS), BF16)], axis=1)
    all_rest = all_rest.reshape(N_CHIPS, REST_ROWS, D_MODEL)
    all_rest = jnp.stack([jnp.where(chip == p, own_rest, all_rest[p]) for p in range(N_CHIPS)])
    w_kv_f = all_rest[:, 0:256].reshape(D_MODEL, D_MODEL)
    w_br_f = [all_rest[:, 256 + 128 * i:384 + 128 * i].reshape(N_CHIPS, 512, 256).transpose(1, 0, 2)
              .reshape(512, D_MODEL) for i in range(3)]
    w_out_f = all_rest[:, 640:896].reshape(D_MODEL, D_MODEL)

    pair = []

    def exchange(g):
        def per_chip(name, p):
            a = g[name]
            if name in ("w_mem_kv", "w_out"):
                return a[256 * p:256 * (p + 1)]
            return a[:, 256 * p:256 * (p + 1)]

        in4 = jnp.stack(g["w_in"])
        rest4 = jnp.stack([_rest_pack({n: per_chip(n, p) for n, _, _ in REST_SPLITS}) for p in range(N_CHIPS)])
        halves = [in4.reshape(N_CHIPS, 2, D_MODEL // 2, SHARD_COLS),
                  rest4.reshape(N_CHIPS, 2, REST_ROWS // 2, D_MODEL)]
        got = _swap_with_sibling(halves)
        pair.extend(_add_pair("add_pair_%d" % i, halves[i], got[i], c) for i in range(2))
        return _scatter_to_owners(pair)

    loss_lanes, grad_x, g, landed = _local_step(
        x[0], mem[0], positions.reshape(s, 1), loss_target[0], norm_pre_g, norm_post_g, norm_mem_g,
        w_main, w_fb, b_forget, b_merge, w_kv_f, w_br_f[0], w_br_f[1], w_br_f[2], w_out_f, exchange)
    loss = lax.psum(jnp.sum(loss_lanes), ("x", "y", "c"))
    half = [_add_chips("add_chips_%d" % i, landed[i], pair[i], chip) for i in range(2)]
    red_in, red_rest = [put(a, o[0], c) for a, o in zip(_share_with_sibling(half), half)]
    gs = {"w_in": red_in.reshape(D_MODEL, SHARD_COLS)}
    gs.update(_rest_unpack(red_rest.reshape(REST_ROWS, D_MODEL), {n: w[n].shape for n, _, _ in REST_SPLITS}))
    gs.update(_small_unpack(_sum_small(_small_pack(
        g["norm_pre_g"], g["norm_post_g"], g["norm_mem_g"], g["b_forget"], g["b_merge"]))))

    delta, new_m, new_v = {}, {}, {}
    for n, tm in (("w_in", 128), ("w_mem_kv", 256), ("w_branch_a", 512), ("w_branch_b", 512),
                  ("w_branch_m", 512), ("w_out", 256)):
        d_, m_, v_ = _adamw("adamw_" + n, w[n], gs[n], mo[n], vo[n], tm)
        delta[n], new_m[n], new_v[n] = d_[None], m_[None], v_[None]
        gs[n] = gs[n][None]
    packs = [_small_pack(*[t[n] for n in SMALL])
             for t in (w, gs, mo, vo)]
    for res, store in zip(_adamw("adamw_small", *packs, 8), (delta, new_m, new_v)):
        store.update(_small_unpack(res))

    return (loss, grad_x[None], *[gs[n] for n in WEIGHTS], *[delta[n] for n in WEIGHTS],
            *[new_m[n] for n in WEIGHTS], *[new_v[n] for n in WEIGHTS])
```

```python
import functools

import jax
import jax.numpy as jnp
from jax import lax
from jax.experimental import pallas as pl
from jax.experimental.pallas import tpu as pltpu

F32 = jnp.float32
BF16 = jnp.bfloat16
MESH = pl.DeviceIdType.MESH

D_MODEL = 1024
N_MEM = 256
EPS = 1e-6
NEG = -1e30
ROPE_THETA = 500000.0
ROT_DIM = 32
HD = 128
A_GROUP = 512
DILATIONS = (1, 4, 16)
BAND = 128
B_HEADS = 8
B_HD = 64
N_CHIPS = 4
N_DEV = 8

C_QA, C_KA, C_VA, C_ZA = 0, 1536, 3072, 4608
C_QB, C_KB, C_VB, C_ZB = 5120, 5632, 6144, 6656
C_QM, C_ZM, C_GL = 7168, 7680, 8192
FB_ORIG = 6656
IN_COLS = 11272
SHARD_COLS = IN_COLS // N_CHIPS

ADAM_LR, ADAM_B1, ADAM_B2, ADAM_EPS, ADAM_WD, ADAM_STEP = 0.001, 0.9, 0.999, 1e-08, 0.01, 10

VMEM_LIMIT_V7X = 56 * 1024 * 1024

NT = (((1,), (1,)), ((), ()))
NN = (((1,), (0,)), ((), ()))
TN = (((0,), (0,)), ((), ()))


def _params(sem):
    return pltpu.CompilerParams(dimension_semantics=sem, vmem_limit_bytes=VMEM_LIMIT_V7X)


def _dot(a, b, dn=NN):
    return lax.dot_general(a, b, dn, preferred_element_type=F32)


def _sig(z):
    return 1.0 / (1.0 + jnp.exp(-z))


def _rows(name, fn, row_ins, bc_ins, outs, reds=(), tm=512, scratch=(), into=None):
    arrs, specs = [], []
    s = None
    for r in row_ins:
        arr, w, cb, d = (tuple(r) + (1,))[:4] if isinstance(r, tuple) else (r, r.shape[1], 0, 1)
        s = arr.shape[0] * d if s is None else s
        arrs.append(arr)
        specs.append((w, cb, d))
    tm = min(tm, s)
    specs = [pl.BlockSpec((tm // d, w), functools.partial(lambda i, cb: (i, cb), cb=cb)) for w, cb, d in specs]
    for b in bc_ins:
        arrs.append(b)
        specs.append(pl.BlockSpec(b.shape, lambda i: (0, 0)))
    outs = [(tuple(o) + (1,))[:3] for o in outs]
    n_in, n_out = len(arrs), len(outs)
    o0 = n_in + (0 if into is None else 1)

    def body(*refs):
        n_ref = o0 + n_out + len(reds)
        vals = fn(*[r[...] for r in refs[:n_in]], *refs[n_ref:])
        if not isinstance(vals, (tuple, list)):
            vals = (vals,)
        for r, v in zip(refs[o0:o0 + n_out], vals[:n_out]):
            r[...] = v.astype(r.dtype)
        if reds:
            red_refs = refs[o0 + n_out:n_ref]

            @pl.when(pl.program_id(0) == 0)
            def _():
                for r in red_refs:
                    r[...] = jnp.zeros_like(r)

            for r, v in zip(red_refs, vals[n_out:]):
                r[...] += v

    out_shape = [jax.ShapeDtypeStruct((s // d, c), dt) for c, dt, d in outs]
    out_shape += [jax.ShapeDtypeStruct((1, c), F32) for c in reds]
    out_specs = [pl.BlockSpec((tm // d, c), lambda i: (i, 0)) for c, _, d in outs]
    out_specs += [pl.BlockSpec((1, c), lambda i: (0, 0)) for c in reds]
    aliases = {}
    if into is not None:
        whole, k, cb = into
        out_shape[k] = jax.ShapeDtypeStruct(whole.shape, whole.dtype)
        if isinstance(cb, tuple):
            out_specs[k] = pl.BlockSpec((pl.Element(tm), pl.Element(outs[k][0])),
                                        functools.partial(lambda i, c0: (i * tm, c0), c0=cb[1]))
        else:
            out_specs[k] = pl.BlockSpec((tm, outs[k][0]), functools.partial(lambda i, cb: (i, cb), cb=cb))
        aliases = {n_in: k}
        arrs.append(whole)
        specs.append(pl.BlockSpec(memory_space=pl.ANY))
    res = pl.pallas_call(
        body, name=name, grid=(s // tm,), in_specs=specs, out_specs=out_specs, out_shape=out_shape,
        scratch_shapes=list(scratch), input_output_aliases=aliases,
        compiler_params=_params(("arbitrary",) if reds else ("parallel",)),
    )(*arrs)
    return res


def _to_class(x, scr, d):
    if d == 1:
        return x.astype(F32)
    tm, c = x.shape
    for g in range(c // 128):
        scr[g][...] = x[:, g * 128:(g + 1) * 128].astype(F32)
    return jnp.concatenate([scr[g][pl.ds(r, tm // d, stride=d), :] for r in range(d) for g in range(c // 128)],
                           axis=1)


def _from_class(x, scr, d):
    if d == 1:
        return x.astype(F32)
    n, dc = x.shape
    c = dc // d
    for r in range(d):
        for g in range(c // 128):
            scr[g][pl.ds(r, n, stride=d), :] = x[:, r * c + g * 128:r * c + (g + 1) * 128].astype(F32)
    return jnp.concatenate([scr[g][...] for g in range(c // 128)], axis=1)


def _mm(name, a, b, mode, out_dtype, tm=2048, tn=1024, tk=1024, side=None):
    if mode == "nn":
        (m, k), (_, n) = a.shape, b.shape
    elif mode == "nt":
        (m, k), (n, _) = a.shape, b.shape
    else:
        (k, m), (_, n) = a.shape, b.shape
    tm, tn, tk = min(tm, m), min(tn, n), min(tk, k)
    nk = k // tk
    grid = (m // tm, n // tn, nk)
    dn = {"nn": NN, "nt": NT, "tn": TN}[mode]
    n_si = len(side["ins"]) if side else 0
    n_so = len(side["outs"]) if side else 0
    n_acc = 1 if nk > 1 else 0

    def body(*refs):
        a_ref, b_ref = refs[:2]
        side_in, o_ref = refs[2:2 + n_si], refs[2 + n_si]
        side_out = refs[3 + n_si:3 + n_si + n_so]
        acc = refs[3 + n_si + n_so:3 + n_si + n_so + n_acc]
        side_scratch = refs[3 + n_si + n_so + n_acc:]
        step = (pl.program_id(0) * grid[1] + pl.program_id(1)) * grid[2] + pl.program_id(2)
        if side:
            @pl.when(step == 0)
            def _():
                side["start"](side_in, side_out, side_scratch)

        part = _dot(a_ref[...].astype(BF16), b_ref[...].astype(BF16), dn)
        if nk == 1:
            o_ref[...] = part.astype(o_ref.dtype)
        else:
            kk = pl.program_id(2)

            @pl.when(kk == 0)
            def _():
                acc[0][...] = part

            @pl.when(kk > 0)
            def _():
                acc[0][...] += part

            @pl.when(kk == nk - 1)
            def _():
                o_ref[...] = acc[0][...].astype(o_ref.dtype)

        if side:
            @pl.when(step == grid[0] * grid[1] * grid[2] - 1)
            def _():
                side["wait"](side_in, side_out, side_scratch)

    a_spec = (pl.BlockSpec((tk, tm), lambda i, j, kk: (kk, i)) if mode == "tn"
              else pl.BlockSpec((tm, tk), lambda i, j, kk: (i, kk)))
    b_spec = (pl.BlockSpec((tn, tk), lambda i, j, kk: (j, kk)) if mode == "nt"
              else pl.BlockSpec((tk, tn), lambda i, j, kk: (kk, j)))
    o_spec = pl.BlockSpec((tm, tn), lambda i, j, kk: (i, j))
    o_shape = jax.ShapeDtypeStruct((m, n), out_dtype)
    acc_scratch = [pltpu.VMEM((tm, tn), F32)] * n_acc
    if not side:
        return pl.pallas_call(
            body, name=name, grid=grid, in_specs=[a_spec, b_spec], out_specs=o_spec, out_shape=o_shape,
            scratch_shapes=acc_scratch, compiler_params=_params(("parallel", "parallel", "arbitrary")),
        )(a, b)
    return pl.pallas_call(
        body, name=name, grid=grid, in_specs=[a_spec, b_spec] + [HBM_SPEC] * n_si,
        out_specs=[o_spec] + [HBM_SPEC] * n_so, out_shape=[o_shape] + side["outs"],
        scratch_shapes=acc_scratch + side["scratch"],
        compiler_params=_params(("arbitrary", "arbitrary", "arbitrary")),
    )(a, b, *side["ins"])


def _rms_fwd(name, x, g):
    def fn(xv, gv):
        r = lax.rsqrt(jnp.mean(xv * xv, axis=-1, keepdims=True) + EPS)
        return (xv * r * gv,)
    return _rows(name, fn, [x], [g], [(x.shape[1], BF16)], tm=min(512, x.shape[0]))[0]


def _rms_fwd_both(name, x, g):
    s, dm = x.shape
    tm = min(512, s)

    def body(x_ref, g_ref, h_ref, ht_ref):
        xv = x_ref[...]
        hv = xv * lax.rsqrt(jnp.mean(xv * xv, axis=-1, keepdims=True) + EPS) * g_ref[...]
        h_ref[...] = hv.astype(BF16)
        ht_ref[...] = hv.T.astype(BF16)

    return pl.pallas_call(
        body, name=name, grid=(s // tm,),
        in_specs=[pl.BlockSpec((tm, dm), lambda i: (i, 0)), pl.BlockSpec((1, dm), lambda i: (0, 0))],
        out_specs=[pl.BlockSpec((tm, dm), lambda i: (i, 0)), pl.BlockSpec((dm, tm), lambda i: (0, i))],
        out_shape=[jax.ShapeDtypeStruct((s, dm), BF16), jax.ShapeDtypeStruct((dm, s), BF16)],
        compiler_params=_params(("parallel",)),
    )(x, g)


def _rope_tables(pos, inv):
    ang = pos.astype(F32) * inv
    lane = lax.broadcasted_iota(jnp.int32, ang.shape, 1)
    c = jnp.where(lane < ROT_DIM, jnp.cos(ang), 1.0)
    sn = jnp.sin(ang)
    sg = jnp.where(lane < ROT_DIM // 2, -sn, jnp.where(lane < ROT_DIM, sn, 0.0))
    return c, sg, lane


def _rope_apply(x, c, sg, lane):
    outs = []
    for h in range(x.shape[1] // HD):
        xh = x[:, h * HD:(h + 1) * HD].astype(F32)
        swap = jnp.where(lane < ROT_DIM // 2, pltpu.roll(xh, HD - ROT_DIM // 2, 1),
                         pltpu.roll(xh, ROT_DIM // 2, 1))
        outs.append(xh * c + swap * sg)
    return jnp.concatenate(outs, axis=1)


ROPE_TM = 512


def _class_scratch(tm):
    return [pltpu.VMEM((tm, 128), F32) for _ in range(A_GROUP // 128)]


def _rope_fwd(u, pos, inv):
    def fn(q, k, v, p, iv, *scr):
        c, sg, lane = _rope_tables(p, iv)
        qr, kr = _rope_apply(q, c, sg, lane), _rope_apply(k, c, sg, lane)
        outs = []
        for g, d in enumerate(DILATIONS):
            gs = slice(g * A_GROUP, (g + 1) * A_GROUP)
            outs += [_to_class(qr[:, gs], scr, d), _to_class(kr[:, gs], scr, d), _to_class(v[:, gs], scr, d)]
        return tuple(outs)

    outs = [(d * A_GROUP, BF16, d) for d in DILATIONS for _ in range(3)]
    qkv = [(u, 3 * A_GROUP, c0 // (3 * A_GROUP)) for c0 in (C_QA, C_KA, C_VA)]
    return _rows("rope_fwd", fn, qkv + [pos], [inv], outs, tm=ROPE_TM,
                 scratch=_class_scratch(ROPE_TM))


def _rope_bwd(dqs, dks, dvs, pos, inv, du):
    def fn(*args):
        grads, p, iv, scr = args[:9], args[9], args[10], args[11:]
        c, sg, lane = _rope_tables(p, iv)
        tok = [jnp.concatenate([_from_class(grads[3 * k + g], scr, d) for g, d in enumerate(DILATIONS)], axis=1)
               for k in range(3)]
        return (jnp.concatenate([_rope_apply(tok[0], c, -sg, lane), _rope_apply(tok[1], c, -sg, lane), tok[2]],
                                axis=1),)

    ins = [(a, a.shape[1], 0, d) for grp in (dqs, dks, dvs) for a, d in zip(grp, DILATIONS)]
    return _rows("rope_bwd", fn, ins + [pos], [inv], [(9 * A_GROUP, BF16)], tm=ROPE_TM,
                 scratch=_class_scratch(ROPE_TM), into=(du, 0, 0))[0]


def _lane_pack(cols, like):
    lane = lax.broadcasted_iota(jnp.int32, like, 1)
    out = jnp.zeros(like, F32)
    for h, cvec in enumerate(cols):
        out = jnp.where(lane == h, cvec, out)
    return out


def _band_specs(l, d, tq):
    nsb = tq // BAND
    nblk = l // BAND
    cur = pl.BlockSpec((tq, A_GROUP), lambda r, i: (i, r))
    prev = pl.BlockSpec((BAND, A_GROUP), lambda r, i: (jnp.maximum(i * nsb - 1, 0), r))
    nxt = pl.BlockSpec((BAND, A_GROUP), lambda r, i: (jnp.minimum((i + 1) * nsb, nblk - 1), r))
    st_cur = pl.BlockSpec((tq, HD), lambda r, i: (i, r))
    st_nxt = pl.BlockSpec((BAND, HD), lambda r, i: (jnp.minimum((i + 1) * nsb, nblk - 1), r))
    return nsb, cur, prev, nxt, st_cur, st_nxt


def _band_mask_q(i, first_tile):
    qr = lax.broadcasted_iota(jnp.int32, (BAND, 2 * BAND), 0)
    kc = lax.broadcasted_iota(jnp.int32, (BAND, 2 * BAND), 1)
    in_prev = (kc < BAND) & (kc >= qr)
    in_cur = (kc >= BAND) & (kc - BAND <= qr)
    if i == 0:
        in_prev = in_prev & jnp.logical_not(first_tile)
    return in_prev | in_cur


def _band_mask_k(j, nsb, last_tile):
    kc = lax.broadcasted_iota(jnp.int32, (BAND, 2 * BAND), 0)
    qr = lax.broadcasted_iota(jnp.int32, (BAND, 2 * BAND), 1)
    same = (qr < BAND) & (kc <= qr)
    nxt = (qr >= BAND) & (kc >= qr - BAND)
    if j == nsb - 1:
        nxt = nxt & jnp.logical_not(last_tile)
    return same | nxt


def _band_fwd(name, q, k, v, d):
    l = q.shape[0]
    tq = min(512, l)
    nsb, cur, prev, _, st_cur, _ = _band_specs(l, d, tq)
    scale = HD ** -0.5

    def body(q_ref, kc_ref, kp_ref, vc_ref, vp_ref, o_ref, lse_ref):
        first = pl.program_id(1) == 0
        for i in range(nsb):
            lses = []
            mask = _band_mask_q(i, first)
            for h in range(4):
                cs = slice(h * HD, (h + 1) * HD)
                qv = q_ref[i * BAND:(i + 1) * BAND, cs]
                if i == 0:
                    kk = jnp.concatenate([kp_ref[:, cs], kc_ref[0:BAND, cs]], axis=0)
                    vv = jnp.concatenate([vp_ref[:, cs], vc_ref[0:BAND, cs]], axis=0)
                else:
                    kk = kc_ref[(i - 1) * BAND:(i + 1) * BAND, cs]
                    vv = vc_ref[(i - 1) * BAND:(i + 1) * BAND, cs]
                s = jnp.where(mask, _dot(qv, kk, NT) * scale, NEG)
                m = jnp.max(s, axis=-1, keepdims=True)
                p = jnp.exp(s - m)
                den = jnp.sum(p, axis=-1, keepdims=True)
                o_ref[i * BAND:(i + 1) * BAND, cs] = _dot(p.astype(BF16), vv) / den
                lses.append(m + jnp.log(den))
            lse_ref[i * BAND:(i + 1) * BAND, :] = _lane_pack(lses, (BAND, HD))

    return pl.pallas_call(
        body, name=name, grid=(d, l // tq), in_specs=[cur, cur, prev, cur, prev],
        out_specs=[cur, st_cur],
        out_shape=[jax.ShapeDtypeStruct((l, d * A_GROUP), F32), jax.ShapeDtypeStruct((l, d * HD), F32)],
        compiler_params=_params(("parallel", "parallel")),
    )(q, k, k, v, v)


def _band_dq(name, q, k, v, dy, lse, delta, d):
    l = q.shape[0]
    tq = min(512, l)
    nsb, cur, prev, _, st_cur, _ = _band_specs(l, d, tq)
    scale = HD ** -0.5

    def body(q_ref, kc_ref, kp_ref, vc_ref, vp_ref, dy_ref, lse_ref, dl_ref, dq_ref):
        first = pl.program_id(1) == 0
        for i in range(nsb):
            mask = _band_mask_q(i, first)
            rs = slice(i * BAND, (i + 1) * BAND)
            for h in range(4):
                cs = slice(h * HD, (h + 1) * HD)
                if i == 0:
                    kk = jnp.concatenate([kp_ref[:, cs], kc_ref[0:BAND, cs]], axis=0)
                    vv = jnp.concatenate([vp_ref[:, cs], vc_ref[0:BAND, cs]], axis=0)
                else:
                    kk = kc_ref[(i - 1) * BAND:(i + 1) * BAND, cs]
                    vv = vc_ref[(i - 1) * BAND:(i + 1) * BAND, cs]
                s = jnp.where(mask, _dot(q_ref[rs, cs], kk, NT) * scale, NEG)
                p = jnp.exp(s - lse_ref[rs, h:h + 1])
                dp = _dot(dy_ref[rs, cs], vv, NT)
                ds = p * (dp - dl_ref[rs, h:h + 1])
                dq_ref[rs, cs] = (_dot(ds.astype(BF16), kk) * scale).astype(dq_ref.dtype)

    return pl.pallas_call(
        body, name=name, grid=(d, l // tq),
        in_specs=[cur, cur, prev, cur, prev, cur, st_cur, st_cur], out_specs=cur,
        out_shape=jax.ShapeDtypeStruct((l, d * A_GROUP), BF16),
        compiler_params=_params(("parallel", "parallel")),
    )(q, k, k, v, v, dy, lse, delta)


def _band_dkv(name, q, k, v, dy, lse, delta, d):
    l = q.shape[0]
    tq = min(512, l)
    nsb, cur, _, nxt, st_cur, st_nxt = _band_specs(l, d, tq)
    scale = HD ** -0.5
    ntile = l // tq

    def body(k_ref, v_ref, qc_ref, qn_ref, dyc_ref, dyn_ref, lc_ref, ln_ref, dc_ref, dn_ref,
             dk_ref, dv_ref):
        last = pl.program_id(1) == ntile - 1

        def win(c_ref, n_ref, j, cs):
            if j == nsb - 1:
                return jnp.concatenate([c_ref[j * BAND:(j + 1) * BAND, cs], n_ref[:, cs]], axis=0)
            return c_ref[j * BAND:(j + 2) * BAND, cs]

        allh = slice(0, HD)
        for j in range(nsb):
            mask = _band_mask_k(j, nsb, last)
            rs = slice(j * BAND, (j + 1) * BAND)
            lse_t = win(lc_ref, ln_ref, j, allh).T
            delta_t = win(dc_ref, dn_ref, j, allh).T
            for h in range(4):
                cs = slice(h * HD, (h + 1) * HD)
                qw = win(qc_ref, qn_ref, j, cs)
                dyw = win(dyc_ref, dyn_ref, j, cs)
                st = jnp.where(mask, _dot(k_ref[rs, cs], qw, NT) * scale, NEG)
                pt = jnp.exp(st - lse_t[h:h + 1, :])
                dst = pt * (_dot(v_ref[rs, cs], dyw, NT) - delta_t[h:h + 1, :])
                dv_ref[rs, cs] = _dot(pt.astype(BF16), dyw).astype(dv_ref.dtype)
                dk_ref[rs, cs] = (_dot(dst.astype(BF16), qw) * scale).astype(dk_ref.dtype)

    shp = jax.ShapeDtypeStruct((l, d * A_GROUP), BF16)
    return pl.pallas_call(
        body, name=name, grid=(d, ntile),
        in_specs=[cur, cur, cur, nxt, cur, nxt, st_cur, st_nxt, st_cur, st_nxt],
        out_specs=[cur, cur], out_shape=[shp, shp],
        compiler_params=_params(("parallel", "parallel")),
    )(k, v, q, q, dy, dy, lse, lse, delta, delta)


def _split3(x):
    hi = x.astype(BF16)
    r1 = x - hi.astype(F32)
    mid = r1.astype(BF16)
    lo = (r1 - mid.astype(F32)).astype(BF16)
    return hi, mid, lo


def _fox_prep(z, b):
    h, s = z.shape
    blk = min(512, s)

    def body(z_ref, b_ref, c_ref):
        r = lax.broadcasted_iota(jnp.int32, (blk, blk), 0)
        cidx = lax.broadcasted_iota(jnp.int32, (blk, blk), 1)
        tri = (r <= cidx).astype(BF16)
        carry = jnp.zeros((h, 1), F32)
        for t in range(s // blk):
            zz = z_ref[:, t * blk:(t + 1) * blk] + b_ref[...]
            lf = jnp.minimum(zz, 0.0) - jnp.log(1.0 + jnp.exp(-jnp.abs(zz)))
            hi, mid, lo = _split3(lf)
            cs = _dot(hi, tri) + _dot(mid, tri) + _dot(lo, tri) + carry
            c_ref[:, t * blk:(t + 1) * blk] = cs
            carry = cs[:, blk - 1:blk]

    return pl.pallas_call(body, name="fox_prep", out_shape=jax.ShapeDtypeStruct((h, s), F32))(z, b)


def _fox_prep_bwd(dc, z, b):
    h, s = z.shape
    blk = min(512, s)

    def body(dc_ref, z_ref, b_ref, dz_ref, db_ref):
        r = lax.broadcasted_iota(jnp.int32, (blk, blk), 0)
        cidx = lax.broadcasted_iota(jnp.int32, (blk, blk), 1)
        tri = (r >= cidx).astype(BF16)
        carry = jnp.zeros((h, 1), F32)
        tot = jnp.zeros((h, 1), F32)
        for t in reversed(range(s // blk)):
            hi, mid, lo = _split3(dc_ref[:, t * blk:(t + 1) * blk])
            rc = _dot(hi, tri) + _dot(mid, tri) + _dot(lo, tri) + carry
            carry = rc[:, 0:1]
            zz = z_ref[:, t * blk:(t + 1) * blk] + b_ref[...]
            dz = rc * _sig(-zz)
            dz_ref[:, t * blk:(t + 1) * blk] = dz
            tot = tot + jnp.sum(dz, axis=-1, keepdims=True)
        db_ref[...] = tot

    return pl.pallas_call(
        body, name="fox_prep_bwd",
        out_shape=[jax.ShapeDtypeStruct((h, s), F32), jax.ShapeDtypeStruct((h, 1), F32)])(dc, z, b)


FOX_W = 128
FOX_C = B_HD
FOX_ONE = B_HD + 3
FOX_SUB = 256
FOX_SUB_FWD = 128
FOX_HEADS_PER_STEP = 2


def _head_of_pair(x, hh):
    return x if hh == 0 else pltpu.roll(x, B_HD, 1)


def _fox_pack(u, c_col, t):
    s = u.shape[0]
    nt = s // t
    scale = B_HD ** -0.5

    def body(q_ref, k_ref, v_ref, c_ref, qf_ref, kb_ref, ks_ref, vb_ref, vt_ref, q2_ref, k2_ref):
        lane = lax.broadcasted_iota(jnp.int32, (t, FOX_W), 1)
        head_lanes = (lax.broadcasted_iota(jnp.int32, (FOX_W, FOX_W), 0) < B_HD).astype(BF16)

        def top_norm2(xv):
            n2 = _dot((xv * xv).astype(BF16), head_lanes)
            return jnp.broadcast_to(jnp.max(n2, axis=0, keepdims=True)[:, :1], (8, 128))

        for hd in range(B_HEADS):
            pair, hh = slice(hd // 2 * FOX_W, (hd // 2 + 1) * FOX_W), hd % 2
            qv, kv, vv = [r[:, pair].astype(F32) for r in (q_ref, k_ref, v_ref)]
            qh = _head_of_pair(qv, hh)
            q2_ref[hd] = top_norm2(qh)
            qf_ref[hd] = jnp.where(lane < B_HD, qh, B_HD ** 0.5).astype(BF16)
            neg = c_ref[hd] * (-scale)
            hi = neg.astype(BF16).astype(F32)
            mid = (neg - hi).astype(BF16).astype(F32)
            lo = neg - hi - mid
            aux = jnp.where(lane == FOX_C, hi,
                            jnp.where(lane == FOX_C + 1, mid, jnp.where(lane == FOX_C + 2, lo, 0.0)))
            kb = jnp.where(lane < B_HD, _head_of_pair(kv, hh) * scale, aux)
            k2_ref[hd] = top_norm2(jnp.where(lane < B_HD, kb, 0.0))
            kb_ref[hd] = kb.astype(BF16)
            ks_ref[hd] = jnp.where(lane == FOX_ONE, 1.0, kb).T.astype(BF16)
            vb = jnp.where(lane < B_HD, _head_of_pair(vv, hh), 1.0)
            vb_ref[hd] = vb.astype(BF16)
            vt_ref[hd] = vb.T.astype(BF16)

    def tok(col0):
        return pl.BlockSpec((t, B_HEADS * B_HD), functools.partial(lambda i, cb: (i, cb), cb=col0 // (B_HEADS * B_HD)))

    rows = pl.BlockSpec((B_HEADS, t, FOX_W), lambda i: (0, i, 0))
    tiles = pl.BlockSpec((B_HEADS, None, FOX_W, t), lambda i: (0, i, 0, 0))
    hm = jax.ShapeDtypeStruct((B_HEADS, s, FOX_W), BF16)
    tt = jax.ShapeDtypeStruct((B_HEADS, nt, FOX_W, t), BF16)
    return pl.pallas_call(
        body, name="fox_pack", grid=(nt,),
        in_specs=[tok(C_QB), tok(C_KB), tok(C_VB), pl.BlockSpec((B_HEADS, t, 1), lambda i: (0, i, 0))],
        out_specs=[rows, rows, tiles, rows, tiles] + [pl.BlockSpec((B_HEADS, None, 8, 128), lambda i: (0, i, 0, 0))] * 2,
        out_shape=[hm, hm, tt, hm, tt] + [jax.ShapeDtypeStruct((B_HEADS, nt, 8, 128), F32)] * 2,
        compiler_params=_params(("parallel",)),
    )(u, u, u, c_col)


def _fox_pack_bwd(dy, y, t):
    s = dy.shape[0]
    nt = s // t

    def body(do_ref, o_ref, dow_ref, dl_ref):
        lane = lax.broadcasted_iota(jnp.int32, (t, FOX_W), 1)
        lane8 = lax.broadcasted_iota(jnp.int32, (8, FOX_W), 1)
        for pr in range(B_HEADS // 2):
            pair = slice(pr * FOX_W, (pr + 1) * FOX_W)
            dov = do_ref[:, pair].astype(F32)
            parts = _split3(dov * o_ref[:, pair].astype(F32))
            for hh in range(2):
                dow_ref[2 * pr + hh] = jnp.where(lane < B_HD, _head_of_pair(dov, hh), 0.0).astype(BF16)
                mask = ((lane8 >= hh * B_HD) & (lane8 < (hh + 1) * B_HD)).astype(BF16)
                row = _dot(mask, parts[0], NT) + _dot(mask, parts[1], NT) + _dot(mask, parts[2], NT)
                dl_ref[2 * pr + hh] = row[0:1, :]

    tok = pl.BlockSpec((t, B_HEADS * B_HD), lambda i: (i, 0))
    return pl.pallas_call(
        body, name="fox_pack_bwd", grid=(nt,), in_specs=[tok, tok],
        out_specs=[pl.BlockSpec((B_HEADS, t, FOX_W), lambda i: (0, i, 0)),
                   pl.BlockSpec((B_HEADS, None, 1, t), lambda i: (0, i, 0, 0))],
        out_shape=[jax.ShapeDtypeStruct((B_HEADS, s, FOX_W), BF16), jax.ShapeDtypeStruct((B_HEADS, nt, 1, t), F32)],
        compiler_params=_params(("parallel",)),
    )(dy, y)


def _fox_unpack(dqt, dkw, dvw, du, t):
    h, nt = dqt.shape[:2]

    def body(dq_ref, dk_ref, dv_ref, _, o_ref, dc_ref):
        lane = lax.broadcasted_iota(jnp.int32, (t, FOX_W), 1)

        def join(a0, a1):
            return jnp.where(lane < B_HD, a0, pltpu.roll(a1, B_HD, 1))

        for hh in range(h):
            dc_ref[hh] = dq_ref[hh][FOX_ONE:FOX_ONE + 1, :] - dk_ref[hh].T[B_HD:B_HD + 1, :]
        pairs = range(0, h, 2)
        cols = ([join(dq_ref[a].T, dq_ref[a + 1].T) for a in pairs] + [join(dk_ref[a], dk_ref[a + 1]) for a in pairs]
                + [join(dv_ref[a], dv_ref[a + 1]) for a in pairs])
        o_ref[...] = jnp.concatenate(cols, axis=1).astype(o_ref.dtype)

    rows = pl.BlockSpec((h, t, FOX_W), lambda i: (0, i, 0))
    return pl.pallas_call(
        body, name="fox_unpack", grid=(nt,),
        in_specs=[pl.BlockSpec((h, None, FOX_W, t), lambda i: (0, i, 0, 0)), rows, rows,
                  pl.BlockSpec(memory_space=pl.ANY)],
        out_specs=[pl.BlockSpec((pl.Element(t), pl.Element(3 * h * B_HD)), lambda i: (i * t, C_QB)),
                   pl.BlockSpec((h, None, 1, t), lambda i: (0, i, 0, 0))],
        out_shape=[jax.ShapeDtypeStruct(du.shape, du.dtype), jax.ShapeDtypeStruct((h, nt, 1, t), F32)],
        input_output_aliases={3: 0},
        compiler_params=_params(("parallel",)),
    )(dqt, dkw, dvw, du)


FOX_DEAD = -110.0


def _fox_bounds(q2, k2, c, t):
    g = 2.0 * jnp.sqrt(1.02 * jnp.max(q2[:, :, 0, 0], axis=1) * 1.02 * jnp.max(k2[:, :, 0, 0], axis=1))
    return jnp.concatenate([c[:, ::t], c[:, t - 1::t], g[:, None]], axis=1)


SMEM_SPEC = pl.BlockSpec(memory_space=pltpu.SMEM)


def _fox_fwd(qf, kb, vt4, bounds, t):
    h, s, w = qf.shape
    nt = s // t
    sub = FOX_SUB_FWD
    nsub = t // sub
    nh = FOX_HEADS_PER_STEP

    def body(b_ref, q_ref, k_ref, v_ref, o_ref, lse_ref):
        i = pl.program_id(1)
        krow = lax.broadcasted_iota(jnp.int32, (sub, t), 0)
        qcol = lax.broadcasted_iota(jnp.int32, (sub, t), 1)

        def dead_before(hh):
            head = pl.program_id(0) * nh + hh
            top = b_ref[head, 2 * nt] + b_ref[head, i]
            return lax.fori_loop(
                0, i, lambda jj, n: n + (top - b_ref[head, nt + jj] < FOX_DEAD).astype(jnp.int32), 0)

        j_lo = functools.reduce(jnp.minimum, [dead_before(hh) for hh in range(nh)])

        def tile(j, carry, diag):
            out = []
            for hh in range(nh):
                m, acc = carry[hh]
                qv, vj = q_ref[hh], v_ref[hh, j]
                los = [b * sub if diag else 0 for b in range(nsub)]
                sts = [_dot(k_ref[hh, pl.ds(pl.multiple_of(j * t + b * sub, sub), sub), :], qv[lo:, :], NT)
                       for b, lo in enumerate(los)]
                for b, lo in enumerate(los):
                    st = sts[b]
                    if diag:
                        st = jnp.where(krow[:, :t - lo] <= qcol[:, :t - lo], st, NEG)
                    m_old, acc_old = m[:, lo:], acc[:, lo:]
                    m2 = jnp.maximum(m_old, jnp.max(st, axis=0, keepdims=True))
                    p = jnp.exp(st - m2).astype(BF16)
                    acc2 = jnp.exp(m_old - m2) * acc_old + _dot(vj[:, b * sub:(b + 1) * sub], p)
                    m = m2 if lo == 0 else jnp.concatenate([m[:, :lo], m2], axis=1)
                    acc = acc2 if lo == 0 else jnp.concatenate([acc[:, :lo], acc2], axis=1)
                out.append((m, acc))
            return tuple(out)

        init = tuple((jnp.full((1, t), NEG, F32), jnp.zeros((w, t), F32)) for _ in range(nh))
        carry = lax.fori_loop(j_lo, i, lambda j, c: tile(j, c, False), init)
        outs = []
        for hh, (m, acc) in enumerate(tile(i, carry, True)):
            den = acc[B_HD:B_HD + 1, :]
            outs.append(acc[0:B_HD, :] / den)
            lse_ref[hh] = m + jnp.log(den)
        o_ref[...] = jnp.concatenate(outs, axis=0).T.astype(o_ref.dtype)

    return pl.pallas_call(
        body, name="fox_fwd", grid=(h // nh, nt),
        in_specs=[SMEM_SPEC,
                  pl.BlockSpec((nh, t, w), lambda hh, i: (hh, i, 0)),
                  pl.BlockSpec((nh, s, w), lambda hh, i: (hh, 0, 0)),
                  pl.BlockSpec((nh, nt, w, t), lambda hh, i: (hh, 0, 0, 0))],
        out_specs=[pl.BlockSpec((t, nh * B_HD), lambda hh, i: (i, hh)),
                   pl.BlockSpec((nh, 1, t), lambda hh, i: (hh, 0, i))],
        out_shape=[jax.ShapeDtypeStruct((s, h * B_HD), BF16), jax.ShapeDtypeStruct((h, 1, s), F32)],
        compiler_params=_params(("parallel", "parallel")),
    )(bounds, qf, kb, vt4)


def _fox_bwd(qf, dow, lse_row, delta_row, kb, kst4, vb, bounds, t):
    h, s, w = qf.shape
    nt = s // t
    nsub = t // FOX_SUB
    nh = FOX_HEADS_PER_STEP

    def body(b_ref, q_ref, do_ref, lse_ref, dl_ref, k_ref, kt_ref, v_ref, dqt_ref, dk_ref, dv_ref, dk_acc, dv_acc):
        j = pl.program_id(1)

        def alive_after(hh):
            head = pl.program_id(0) * nh + hh
            top = b_ref[head, 2 * nt] - b_ref[head, nt + j]
            return lax.fori_loop(
                j + 1, nt, lambda ii, n: n + (top + b_ref[head, ii] >= FOX_DEAD).astype(jnp.int32), 0)

        i_hi = j + 1 + functools.reduce(jnp.maximum, [alive_after(hh) for hh in range(nh)])

        @pl.when(j == 0)
        def _():
            dqt_ref[...] = jnp.zeros_like(dqt_ref)

        dk_acc[...] = jnp.zeros_like(dk_acc)
        dv_acc[...] = jnp.zeros_like(dv_acc)
        krow = lax.broadcasted_iota(jnp.int32, (FOX_SUB, t), 0)
        qcol = lax.broadcasted_iota(jnp.int32, (FOX_SUB, t), 1)
        subs = [slice(b * FOX_SUB, (b + 1) * FOX_SUB) for b in range(nsub)]

        def tile(i, diag):
            i0 = pl.multiple_of(i * t, t)
            for hh in range(nh):
                qi, doi = q_ref[hh, pl.ds(i0, t), :], do_ref[hh, pl.ds(i0, t), :]
                lse, dl = lse_ref[hh, i], dl_ref[hh, i]
                los = [b * FOX_SUB if diag else 0 for b in range(nsub)]
                sts = [_dot(k_ref[hh, rs, :], qi[lo:, :], NT) for rs, lo in zip(subs, los)]
                dps = [_dot(v_ref[hh, rs, :], doi[lo:, :], NT) for rs, lo in zip(subs, los)]
                dq = None
                for b, (rs, lo) in enumerate(zip(subs, los)):
                    st = sts[b] - lse[:, lo:]
                    if diag:
                        st = jnp.where(krow[:, :t - lo] <= qcol[:, :t - lo], st, NEG)
                    pt = jnp.exp(st)
                    dsb = (pt * (dps[b] - dl[:, lo:])).astype(BF16)
                    dv_acc[hh, rs, :] += _dot(pt.astype(BF16), doi[lo:, :])
                    dk_acc[hh, rs, :] += _dot(dsb, qi[lo:, :])
                    part = _dot(kt_ref[hh, :, rs], dsb)
                    if lo:
                        part = jnp.concatenate([jnp.zeros((w, lo), F32), part], axis=1)
                    dq = part if dq is None else dq + part
                dqt_ref[hh, i] += dq

        def step(i, carry):
            tile(i, False)
            return carry

        tile(j, True)
        lax.fori_loop(j + 1, i_hi, step, 0)
        dk_ref[...] = dk_acc[...] * (B_HD ** -0.5)
        dv_ref[...] = dv_acc[...]

    full = pl.BlockSpec((nh, s, w), lambda hh, j: (hh, 0, 0))
    rowst = pl.BlockSpec((nh, nt, 1, t), lambda hh, j: (hh, 0, 0, 0))
    tl = pl.BlockSpec((nh, t, w), lambda hh, j: (hh, j, 0))
    return pl.pallas_call(
        body, name="fox_bwd", grid=(h // nh, nt),
        in_specs=[SMEM_SPEC, full, full, rowst, rowst, tl,
                  pl.BlockSpec((nh, None, w, t), lambda hh, j: (hh, j, 0, 0)), tl],
        out_specs=[pl.BlockSpec((nh, nt, w, t), lambda hh, j: (hh, 0, 0, 0)), tl, tl],
        out_shape=[jax.ShapeDtypeStruct((h, nt, w, t), F32), jax.ShapeDtypeStruct((h, s, w), F32),
                   jax.ShapeDtypeStruct((h, s, w), F32)],
        scratch_shapes=[pltpu.VMEM((nh, t, w), F32), pltpu.VMEM((nh, t, w), F32)],
        compiler_params=_params(("parallel", "arbitrary")),
    )(bounds, qf, dow, lse_row, delta_row, kb, kst4, vb)


def _mem_fwd(u, mkv, tq=1024):
    s = u.shape[0]
    scale = HD ** -0.5

    def body(q_ref, mk_ref, mv_ref, o_ref, lse_ref):
        lses = []
        for h in range(4):
            cs = slice(h * HD, (h + 1) * HD)
            sc = _dot(q_ref[:, cs], mk_ref[:, cs], NT) * scale
            m = jnp.max(sc, axis=-1, keepdims=True)
            p = jnp.exp(sc - m)
            den = jnp.sum(p, axis=-1, keepdims=True)
            o_ref[:, cs] = (_dot(p.astype(BF16), mv_ref[:, cs]) / den).astype(o_ref.dtype)
            lses.append(m + jnp.log(den))
        lse_ref[...] = _lane_pack(lses, (tq, HD))

    return pl.pallas_call(
        body, name="mem_fwd", grid=(s // tq,),
        in_specs=[pl.BlockSpec((tq, 512), lambda i: (i, C_QM // 512)),
                  pl.BlockSpec((N_MEM, 512), lambda i: (0, 0)),
                  pl.BlockSpec((N_MEM, 512), lambda i: (0, 1))],
        out_specs=[pl.BlockSpec((tq, 512), lambda i: (i, 0)), pl.BlockSpec((tq, HD), lambda i: (i, 0))],
        out_shape=[jax.ShapeDtypeStruct((s, 512), BF16), jax.ShapeDtypeStruct((s, HD), F32)],
        compiler_params=_params(("parallel",)),
    )(u, mkv, mkv)


def _mem_bwd(u, mkv, o, do, lse, du, tq=1024):
    s = u.shape[0]
    scale = HD ** -0.5

    def body(q_ref, mk_ref, mv_ref, o_ref, do_ref, lse_ref, _, dq_ref, dmk_ref, dmv_ref):
        @pl.when(pl.program_id(0) == 0)
        def _():
            dmk_ref[...] = jnp.zeros_like(dmk_ref)
            dmv_ref[...] = jnp.zeros_like(dmv_ref)

        for h in range(4):
            cs = slice(h * HD, (h + 1) * HD)
            qv, dov = q_ref[:, cs], do_ref[:, cs]
            sc = _dot(qv, mk_ref[:, cs], NT) * scale
            p = jnp.exp(sc - lse_ref[:, h:h + 1])
            delta = jnp.sum(dov.astype(F32) * o_ref[:, cs].astype(F32), axis=-1, keepdims=True)
            ds = p * (_dot(dov, mv_ref[:, cs], NT) - delta)
            dsb = ds.astype(BF16)
            dq_ref[:, cs] = (_dot(dsb, mk_ref[:, cs]) * scale).astype(dq_ref.dtype)
            dmk_ref[:, cs] += _dot(dsb, qv, TN) * scale
            dmv_ref[:, cs] += _dot(p.astype(BF16), dov, TN)

    row = pl.BlockSpec((tq, 512), lambda i: (i, 0))
    acc = pl.BlockSpec((N_MEM, 512), lambda i: (0, 0))
    return pl.pallas_call(
        body, name="mem_bwd", grid=(s // tq,),
        in_specs=[pl.BlockSpec((tq, 512), lambda i: (i, C_QM // 512)),
                  pl.BlockSpec((N_MEM, 512), lambda i: (0, 0)),
                  pl.BlockSpec((N_MEM, 512), lambda i: (0, 1)),
                  row, row, pl.BlockSpec((tq, HD), lambda i: (i, 0)), pl.BlockSpec(memory_space=pl.ANY)],
        out_specs=[pl.BlockSpec((tq, 512), lambda i: (i, C_QM // 512)), acc, acc],
        out_shape=[jax.ShapeDtypeStruct(du.shape, du.dtype), jax.ShapeDtypeStruct((N_MEM, 512), F32),
                   jax.ShapeDtypeStruct((N_MEM, 512), F32)],
        input_output_aliases={6: 0},
        compiler_params=_params(("arbitrary",)),
    )(u, mkv, mkv, o, do, lse, du)


FB_CHIP = FB_ORIG // SHARD_COLS
FB_AT = FB_ORIG - FB_CHIP * SHARD_COLS


def _chip_slabs(main, fb):
    cuts = [SHARD_COLS * p - (B_HEADS if p > FB_CHIP else 0) for p in range(N_CHIPS + 1)]
    slabs = [main[:, a:b] for a, b in zip(cuts[:-1], cuts[1:])]
    own = slabs[FB_CHIP]
    slabs[FB_CHIP] = jnp.concatenate([own[:, :FB_AT], fb, own[:, FB_AT:]], axis=1)
    return slabs


def _split_forget(slabs):
    own = slabs[FB_CHIP]
    parts = list(slabs[:FB_CHIP]) + [own[:, :FB_AT], own[:, FB_AT + B_HEADS:]] + list(slabs[FB_CHIP + 1:])
    return jnp.concatenate(parts, axis=1), own[:, FB_AT:FB_AT + B_HEADS]


def _local_step(x, mem, pos, target, g_pre, g_post, g_mem, w_main, w_fb, b_forget, b_merge,
                w_mem_kv, w_ba, w_bb, w_bm, w_out, exchange=None):
    s = x.shape[0]
    t_fox = min(512, s)
    nt = s // t_fox
    half = ROT_DIM // 2
    inv = ROPE_THETA ** (-jnp.arange(half, dtype=F32) / half)
    inv128 = jnp.concatenate([inv, inv, jnp.zeros((HD - ROT_DIM,), F32)]).reshape(1, HD)

    h, h_t = _rms_fwd_both("norm_pre", x, g_pre)
    u = _mm("proj_in", h, w_main, "nn", BF16, tm=4096)
    ufb = _mm("proj_fb", h, w_fb, "nn", F32)
    memn = _rms_fwd("norm_mem", mem, g_mem)
    mkv = _mm("proj_mem", memn, w_mem_kv, "nn", BF16)

    qkv = _rope_fwd(u, pos, inv128)
    views = [tuple(qkv[3 * g:3 * g + 3]) for g in range(3)]
    os_, lses = [], []
    for g, d in enumerate(DILATIONS):
        o_g, lse_g = _band_fwd("band_fwd%d" % g, *views[g], d)
        os_.append((o_g, d * A_GROUP, 0, d))
        lses.append((lse_g, d * HD, 0, d))

    def merge_a(o1, o2, o3, l1, l2, l3, za, *scr):
        o1, o2, o3 = [_from_class(o, scr, d) for o, d in zip((o1, o2, o3), DILATIONS)]
        l1, l2, l3 = [_from_class(lv, scr, d) for lv, d in zip((l1, l2, l3), DILATIONS)]
        ys, tots = [], []
        for hh in range(4):
            cs, hs = slice(hh * HD, (hh + 1) * HD), slice(hh, hh + 1)
            mx = jnp.maximum(jnp.maximum(l1[:, hs], l2[:, hs]), l3[:, hs])
            e1, e2, e3 = jnp.exp(l1[:, hs] - mx), jnp.exp(l2[:, hs] - mx), jnp.exp(l3[:, hs] - mx)
            den = e1 + e2 + e3
            ys.append((e1 * o1[:, cs] + e2 * o2[:, cs] + e3 * o3[:, cs]) / den)
            tots.append(mx + jnp.log(den))
        y = jnp.concatenate(ys, axis=1)
        zf = za.astype(F32)
        tot = _lane_pack(tots, l1.shape)
        return (y, y * (zf * _sig(zf))) + tuple(_to_class(tot, scr, d) for d in DILATIONS)

    res = _rows("merge_a", merge_a, os_ + lses + [(u, 512, C_ZA // 512)], [],
                [(512, BF16), (512, BF16)] + [(d * HD, F32, d) for d in DILATIONS], tm=ROPE_TM,
                scratch=_class_scratch(ROPE_TM))
    y_a, yg_a, lse_a = res[0], res[1], res[2:5]

    zrow = ufb[:, :B_HEADS].T
    c = _fox_prep(zrow, b_forget.reshape(B_HEADS, 1))
    qf, kb, kst4, vb, vt4, q2, k2 = _fox_pack(u, c.reshape(B_HEADS, s, 1), t_fox)
    bounds = _fox_bounds(q2, k2, c, t_fox)
    y_b, lse_b = _fox_fwd(qf, kb, vt4, bounds, t_fox)

    y_m, lse_m = _mem_fwd(u, mkv)

    def gate(y, z):
        zf = z.astype(F32)
        return (y.astype(F32) * (zf * _sig(zf)),)

    yg_b = _rows("gate_b", gate, [y_b, (u, 512, C_ZB // 512)], [], [(512, BF16)])[0]
    yg_m = _rows("gate_m", gate, [y_m, (u, 512, C_ZM // 512)], [], [(512, BF16)])[0]

    br_a = _mm("branch_a", yg_a, w_ba, "nn", BF16)
    br_b = _mm("branch_b", yg_b, w_bb, "nn", BF16)
    br_m = _mm("branch_m", yg_m, w_bm, "nn", BF16)
    gl = [(u, 1024, C_GL // 1024 + i) for i in range(3)]
    bm3 = b_merge.reshape(3, D_MODEL)

    def merge(g0, g1, g2, b0, b1, b2, bm):
        tot = 0.0
        for i, (gv, bv) in enumerate(((g0, b0), (g1, b1), (g2, b2))):
            tot = tot + _sig(gv.astype(F32) + bm[i:i + 1, :]) * bv.astype(F32)
        return (tot,)

    merged = _rows("merge_gates", merge, gl + [br_a, br_b, br_m], [bm3], [(D_MODEL, BF16)])[0]
    out = _mm("proj_out", merged, w_out, "nn", F32)

    def tail(xv, ov, tv, gv):
        r = lax.rsqrt(jnp.mean(ov * ov, axis=-1, keepdims=True) + EPS)
        n = ov * r
        err = xv + n * gv - tv
        dy = err * (1.0 / D_MODEL)
        dn = dy * gv
        dout = r * (dn - n * jnp.mean(dn * n, axis=-1, keepdims=True))
        return (dy, dout, jnp.sum(0.5 * err * err * (1.0 / D_MODEL), axis=0, keepdims=True),
                jnp.sum(dy * n, axis=0, keepdims=True))

    dy, dout, loss_lanes, g_post_grad = _rows(
        "tail", tail, [x, out, target], [g_post], [(D_MODEL, F32), (D_MODEL, BF16)],
        reds=[D_MODEL, D_MODEL], tm=512)

    dmerged = _mm("d_merged", dout, w_out, "nt", BF16)
    gw_out = _mm("g_w_out", merged, dout, "tn", F32, tk=2048)

    def merge_bwd(dm, g0, g1, g2, b0, b1, b2, bm):
        dmf = dm.astype(F32)
        dbs, dgs, sums = [], [], []
        for i, (gv, bv) in enumerate(((g0, b0), (g1, b1), (g2, b2))):
            sg = _sig(gv.astype(F32) + bm[i:i + 1, :])
            dbs.append(dmf * sg)
            dg = dmf * bv.astype(F32) * sg * (1.0 - sg)
            dgs.append(dg)
            sums.append(jnp.sum(dg, axis=0, keepdims=True))
        return tuple(dbs + [jnp.concatenate(dgs, axis=1)] + sums)

    du = lax.empty(u.shape, BF16)
    res = _rows("merge_bwd", merge_bwd, [dmerged] + gl + [br_a, br_b, br_m], [bm3],
                [(D_MODEL, BF16)] * 3 + [(3 * D_MODEL, BF16)], reds=[D_MODEL] * 3, tm=512,
                into=(du, 3, ("column", C_GL)))
    dbr, du, g_bmerge = res[0:3], res[3], jnp.concatenate(res[4:7], axis=1)

    dyg, gw_branch = [], []
    for nm, dbv, wv, ygv in (("a", dbr[0], w_ba, yg_a), ("b", dbr[1], w_bb, yg_b), ("m", dbr[2], w_bm, yg_m)):
        dyg.append(_mm("d_yg_" + nm, dbv, wv, "nt", BF16))
        gw_branch.append(_mm("g_w_branch_" + nm, ygv, dbv, "tn", F32, tk=2048))

    def gate_bwd(dg, y, z):
        dgf, yf, zf = dg.astype(F32), y.astype(F32), z.astype(F32)
        sg = _sig(zf)
        return dgf * (zf * sg), dgf * yf * (sg * (1.0 + zf * (1.0 - sg)))

    def gate_bwd_a(dg, y, z, *scr):
        dyv, dz = gate_bwd(dg, y, z)
        prod = dyv * y.astype(F32)
        dl = [jnp.sum(prod[:, hh * HD:(hh + 1) * HD], axis=-1, keepdims=True) for hh in range(4)]
        delta = _lane_pack(dl, (dg.shape[0], HD))
        return ((dz,) + tuple(_to_class(dyv, scr, d) for d in DILATIONS)
                + tuple(_to_class(delta, scr, d) for d in DILATIONS))

    res = _rows("gate_bwd_a", gate_bwd_a, [dyg[0], y_a, (u, 512, C_ZA // 512)], [],
                [(512, BF16)] + [(d * A_GROUP, BF16, d) for d in DILATIONS] + [(d * HD, F32, d) for d in DILATIONS],
                tm=ROPE_TM, scratch=_class_scratch(ROPE_TM), into=(du, 0, C_ZA // 512))
    du, dy_a, delta_a = res[0], res[1:4], res[4:7]
    dy_b, du = _rows("gate_bwd_b", gate_bwd, [dyg[1], y_b, (u, 512, C_ZB // 512)], [],
                     [(512, BF16), (512, BF16)], into=(du, 1, C_ZB // 512))
    dy_m, du = _rows("gate_bwd_m", gate_bwd, [dyg[2], y_m, (u, 512, C_ZM // 512)], [],
                     [(512, BF16), (512, BF16)], into=(du, 1, C_ZM // 512))

    du, dmk, dmv = _mem_bwd(u, mkv, y_m, dy_m, lse_m, du)
    dmkv = jnp.concatenate([dmk, dmv], axis=1)
    gw_mem_kv = _mm("g_w_mem_kv", memn, dmkv, "tn", F32)
    dmemn = _mm("d_memn", dmkv, w_mem_kv, "nt", F32)

    def mem_gain_grad(mv, dv):
        r = lax.rsqrt(jnp.mean(mv * mv, axis=-1, keepdims=True) + EPS)
        return (jnp.sum(dv * mv * r, axis=0, keepdims=True),)

    g_mem_grad = _rows("g_norm_mem", mem_gain_grad, [mem, dmemn], [], [], reds=[D_MODEL], tm=N_MEM)[0]

    dow, delta_b = _fox_pack_bwd(dy_b, y_b, t_fox)
    dqt, dkw, dvw = _fox_bwd(qf, dow, lse_b.reshape(B_HEADS, nt, 1, t_fox), delta_b, kb, kst4, vb, bounds, t_fox)
    du, dc = _fox_unpack(dqt, dkw, dvw, du, t_fox)
    dzrow, g_bforget = _fox_prep_bwd(dc.reshape(B_HEADS, s), zrow, b_forget.reshape(B_HEADS, 1))
    dfb = jnp.zeros((s, HD), BF16).at[:, :B_HEADS].set(dzrow.T.astype(BF16))

    dqs, dks, dvs = [], [], []
    for g, d in enumerate(DILATIONS):
        qv, kv, vv = views[g]
        dqs.append(_band_dq("band_dq%d" % g, qv, kv, vv, dy_a[g], lse_a[g], delta_a[g], d))
        dk_g, dv_g = _band_dkv("band_dkv%d" % g, qv, kv, vv, dy_a[g], lse_a[g], delta_a[g], d)
        dks.append(dk_g)
        dvs.append(dv_g)
    du = _rope_bwd(dqs, dks, dvs, pos, inv128, du)

    gw_main = _mm("g_w_main", h_t, du, "nn", F32, tk=2048)
    gw_fb = _mm("g_w_fb", h, dfb, "tn", F32)
    grads = dict(norm_post_g=g_post_grad, norm_mem_g=g_mem_grad, w_in=_chip_slabs(gw_main, gw_fb[:, :B_HEADS]),
                 b_forget=g_bforget.reshape(1, B_HEADS), b_merge=g_bmerge, w_mem_kv=gw_mem_kv,
                 w_branch_a=gw_branch[0], w_branch_b=gw_branch[1], w_branch_m=gw_branch[2], w_out=gw_out)
    side = exchange(grads) if exchange else None
    dh_main = _mm("d_h", du, w_main, "nt", F32, tm=1024, tk=2816, side=side)
    landed = None
    if side:
        dh_main, landed = dh_main[0], dh_main[1:]
    def pre_bwd(xv, d1, dfbv, dyv, gv, wfb):
        r = lax.rsqrt(jnp.mean(xv * xv, axis=-1, keepdims=True) + EPS)
        n = xv * r
        dhv = d1 + _dot(dfbv, wfb, NT)
        dn = dhv * gv
        dx = r * (dn - n * jnp.mean(dn * n, axis=-1, keepdims=True))
        return dyv + dx, jnp.sum(dhv * n, axis=0, keepdims=True)

    grad_x, g_pre_grad = _rows("norm_pre_bwd", pre_bwd, [x, dh_main, dfb, dy], [g_pre, w_fb],
                               [(D_MODEL, F32)], reds=[D_MODEL], tm=512)

    grads["norm_pre_g"] = g_pre_grad
    return loss_lanes, grad_x, grads, landed


HBM_SPEC = pl.BlockSpec(memory_space=pltpu.HBM)


def _place():
    x, y, c = lax.axis_index("x"), lax.axis_index("y"), lax.axis_index("c")
    chips = [(1 - x, y), (x, 1 - y), (1 - x, 1 - y)]
    return x, y, c, 2 * x + y, chips


N_CHUNKS = 4


def _units(parts, row_axis):
    units = []
    for i, a in enumerate(parts):
        ch = a.shape[row_axis] // N_CHUNKS
        units += [(i, pl.ds(k * ch, ch)) for k in range(N_CHUNKS)]
    return units


def _gather_weights(parts):
    n = len(parts)
    units = _units(parts, 1)
    nu = len(units)
    via_y = [(u % N_CHUNKS) < N_CHUNKS // 2 for u in range(nu)]

    def body(*refs):
        srcs, outs = refs[:n], refs[n:2 * n]
        send_sems, recv_sems = refs[2 * n:]
        x, y, c, p, _ = _place()
        me, sib = (x, y, c), (x, y, 1 - c)
        xn, yn, dg = (1 - x, y), (x, 1 - y), (1 - x, 1 - y)

        def cp(u, k, chip, half, to, from_src=False):
            i, rs = units[u]
            dst = outs[i].at[2 * chip[0] + chip[1], half, rs]
            return pltpu.make_async_remote_copy(
                src_ref=srcs[i].at[half, rs] if from_src else dst, dst_ref=dst, send_sem=send_sems.at[u, k],
                recv_sem=recv_sems.at[u, k], device_id=to, device_id_type=MESH)

        sent = []

        def go(copy):
            copy.start()
            sent.append(copy)

        for u in range(nu):
            go(cp(u, 0, (x, y), c, (*xn, c), from_src=True))
            go(cp(u, 1, (x, y), c, (*yn, c), from_src=True))
        for u in range(nu):
            cp(u, 0, xn, c, me).wait_recv()
            go(cp(u, 4, xn, c, sib))
            if via_y[u]:
                go(cp(u, 2, xn, c, (*yn, c)))
            cp(u, 1, yn, c, me).wait_recv()
            go(cp(u, 5, yn, c, sib))
            if not via_y[u]:
                go(cp(u, 3, yn, c, (*xn, c)))
        for u in range(nu):
            cp(u, 2 if via_y[u] else 3, dg, c, me).wait_recv()
            go(cp(u, 6, dg, c, sib))
        for u in range(nu):
            for k, chip in ((4, xn), (5, yn), (6, dg)):
                cp(u, k, chip, 1 - c, me).wait_recv()
        for copy in sent:
            copy.wait_send()

    return pl.pallas_call(
        body, name="gather_weights", in_specs=[HBM_SPEC] * n, out_specs=[HBM_SPEC] * n,
        out_shape=[jax.ShapeDtypeStruct((N_CHIPS,) + a.shape, a.dtype) for a in parts],
        scratch_shapes=[pltpu.SemaphoreType.DMA((nu, 7)), pltpu.SemaphoreType.DMA((nu, 7))],
    )(*parts)


def _swap_with_sibling(parts):
    n = len(parts)
    units = _units(parts, 2)

    def body(*refs):
        srcs, outs = refs[:n], refs[n:2 * n]
        send_sems, recv_sems = refs[2 * n:]
        x, y, c, _, _ = _place()
        cps = [pltpu.make_async_remote_copy(
            src_ref=srcs[i].at[q, 1 - c, rs], dst_ref=outs[i].at[q, rs], send_sem=send_sems.at[u, q],
            recv_sem=recv_sems.at[u, q], device_id=(x, y, 1 - c), device_id_type=MESH)
            for q in range(N_CHIPS) for u, (i, rs) in enumerate(units)]
        for cpy in cps:
            cpy.start()
        for cpy in cps:
            cpy.wait()

    return pl.pallas_call(
        body, name="swap_with_sibling", in_specs=[HBM_SPEC] * n, out_specs=[HBM_SPEC] * n,
        out_shape=[jax.ShapeDtypeStruct(a.shape[:1] + a.shape[2:], a.dtype) for a in parts],
        scratch_shapes=[pltpu.SemaphoreType.DMA((len(units), N_CHIPS)),
                        pltpu.SemaphoreType.DMA((len(units), N_CHIPS))],
    )(*parts)


def _scatter_to_owners(parts):
    n = len(parts)
    units = _units(parts, 1)

    def copies(srcs, outs, send_sems, recv_sems, incoming):
        x, y, c, p, chips = _place()
        return [pltpu.make_async_remote_copy(
            src_ref=srcs[i].at[2 * cx + cy, rs], dst_ref=outs[i].at[(2 * cx + cy) if incoming else p, rs],
            send_sem=send_sems.at[u, j], recv_sem=recv_sems.at[u, j], device_id=(cx, cy, c), device_id_type=MESH)
            for u, (i, rs) in enumerate(units) for j, (cx, cy) in enumerate(chips)]

    def start(ins, outs, scratch):
        for cpy in copies(ins, outs, *scratch, incoming=False):
            cpy.start()

    def wait(ins, outs, scratch):
        for cpy in copies(ins, outs, *scratch, incoming=True):
            cpy.wait_recv()
        for cpy in copies(ins, outs, *scratch, incoming=False):
            cpy.wait_send()

    return dict(ins=list(parts), outs=[jax.ShapeDtypeStruct(a.shape, a.dtype) for a in parts],
                scratch=[pltpu.SemaphoreType.DMA((len(units), 3)), pltpu.SemaphoreType.DMA((len(units), 3))],
                start=start, wait=wait)


def _share_with_sibling(parts):
    n = len(parts)
    units = _units(parts, 1)

    def body(*refs):
        srcs, outs = refs[:n], refs[n:2 * n]
        send_sems, recv_sems = refs[2 * n:]
        x, y, c, _, _ = _place()
        sends = [pltpu.make_async_remote_copy(
            src_ref=srcs[i].at[0, rs], dst_ref=outs[i].at[c, rs], send_sem=send_sems.at[u],
            recv_sem=recv_sems.at[u], device_id=(x, y, 1 - c), device_id_type=MESH)
            for u, (i, rs) in enumerate(units)]
        for cpy in sends:
            cpy.start()
        for u, (i, rs) in enumerate(units):
            pltpu.make_async_remote_copy(
                src_ref=srcs[i].at[0, rs], dst_ref=outs[i].at[1 - c, rs], send_sem=send_sems.at[u],
                recv_sem=recv_sems.at[u], device_id=(x, y, 1 - c), device_id_type=MESH).wait_recv()
        for cpy in sends:
            cpy.wait_send()

    return pl.pallas_call(
        body, name="share_with_sibling", in_specs=[HBM_SPEC] * n, out_specs=[HBM_SPEC] * n,
        out_shape=[jax.ShapeDtypeStruct((2,) + a.shape[1:], a.dtype) for a in parts],
        scratch_shapes=[pltpu.SemaphoreType.DMA((len(units),)), pltpu.SemaphoreType.DMA((len(units),))],
    )(*parts)


def _sum_small(v):
    def body(v_ref, out_ref, buf, send_sems, recv_sems):
        x, y, c, _, _ = _place()
        me = 4 * x + 2 * y + c
        buf[me] = v_ref[...]
        flips = [(dx, dy, dc) for dx in (0, 1) for dy in (0, 1) for dc in (0, 1)][1:]
        sends = []
        for k, (dx, dy, dc) in enumerate(flips):
            cpy = pltpu.make_async_remote_copy(
                src_ref=v_ref, dst_ref=buf.at[me], send_sem=send_sems.at[k], recv_sem=recv_sems.at[k],
                device_id=((x + dx) % 2, (y + dy) % 2, (c + dc) % 2), device_id_type=MESH)
            cpy.start()
            sends.append(cpy)
        for k, (dx, dy, dc) in enumerate(flips):
            px, py, pc = (x + dx) % 2, (y + dy) % 2, (c + dc) % 2
            pltpu.make_async_remote_copy(
                src_ref=v_ref, dst_ref=buf.at[4 * px + 2 * py + pc], send_sem=send_sems.at[k],
                recv_sem=recv_sems.at[k], device_id=(px, py, pc), device_id_type=MESH).wait_recv()
        for cpy in sends:
            cpy.wait_send()
        tot = buf[0]
        for i in range(1, N_DEV):
            tot = tot + buf[i]
        out_ref[...] = tot

    return pl.pallas_call(
        body, name="sum_small", out_shape=jax.ShapeDtypeStruct(v.shape, v.dtype),
        in_specs=[pl.BlockSpec(memory_space=pltpu.VMEM)], out_specs=pl.BlockSpec(memory_space=pltpu.VMEM),
        scratch_shapes=[pltpu.VMEM((N_DEV,) + v.shape, v.dtype), pltpu.SemaphoreType.DMA((N_DEV - 1,)),
                        pltpu.SemaphoreType.DMA((N_DEV - 1,))],
    )(v)


def _add_chips(name, landed, pair, chip):
    nq, r, w = landed.shape
    tr = 128 if r % 128 == 0 else 64

    def body(chip_ref, *refs):
        own = refs[nq][...].astype(F32)
        tot = None
        for q in range(nq):
            term = jnp.where(chip_ref[0] == q, own, refs[q][...].astype(F32))
            tot = term if tot is None else tot + term
        refs[nq + 1][...] = tot

    specs = [pl.BlockSpec((None, tr, w), functools.partial(lambda j, chip_ref, q: (q, j, 0), q=q)) for q in range(nq)]
    specs.append(pl.BlockSpec((None, tr, w), lambda j, chip_ref: (chip_ref[0], j, 0)))
    grid_spec = pltpu.PrefetchScalarGridSpec(
        num_scalar_prefetch=1, grid=(r // tr,), in_specs=specs,
        out_specs=pl.BlockSpec((None, tr, w), lambda j, chip_ref: (0, j, 0)))
    return pl.pallas_call(
        body, name=name, grid_spec=grid_spec, out_shape=jax.ShapeDtypeStruct((1, r, w), F32),
        compiler_params=_params(("parallel",)),
    )(jnp.reshape(chip, (1,)).astype(jnp.int32), *([landed] * nq), pair)


def _add_pair(name, halves, got, c):
    nq, _, r, w = halves.shape
    tr = 128 if r % 128 == 0 else 64

    def body(c_ref, a_ref, b_ref, o_ref):
        o_ref[...] = (a_ref[...] + b_ref[...]).astype(o_ref.dtype)

    grid_spec = pltpu.PrefetchScalarGridSpec(
        num_scalar_prefetch=1, grid=(nq, r // tr),
        in_specs=[pl.BlockSpec((None, None, tr, w), lambda i, j, c_ref: (i, c_ref[0], j, 0)),
                  pl.BlockSpec((None, tr, w), lambda i, j, c_ref: (i, j, 0))],
        out_specs=pl.BlockSpec((None, tr, w), lambda i, j, c_ref: (i, j, 0)))
    return pl.pallas_call(
        body, name=name, grid_spec=grid_spec, out_shape=jax.ShapeDtypeStruct((nq, r, w), BF16),
        compiler_params=_params(("parallel", "parallel")),
    )(jnp.reshape(c, (1,)).astype(jnp.int32), halves, got)


def _adamw(name, w, g, m, v, tm):
    def fn(wv, gv, mv, vv):
        m2 = ADAM_B1 * mv + (1.0 - ADAM_B1) * gv
        v2 = ADAM_B2 * vv + (1.0 - ADAM_B2) * (gv * gv)
        m_hat = m2 / (1.0 - ADAM_B1 ** ADAM_STEP)
        v_hat = v2 / (1.0 - ADAM_B2 ** ADAM_STEP)
        return -ADAM_LR * (m_hat / (jnp.sqrt(v_hat) + ADAM_EPS) + ADAM_WD * wv), m2, v2
    c = w.shape[1]
    return _rows(name, fn, [w, g, m, v], [], [(c, F32)] * 3, tm=tm)


REST_ROWS = 256 + 3 * 128 + 256
REST_SPLITS = (("w_mem_kv", 0, 256), ("w_branch_a", 256, 128), ("w_branch_b", 384, 128),
               ("w_branch_m", 512, 128), ("w_out", 640, 256))


def _rest_pack(t):
    return jnp.concatenate([t[n].reshape(rows, D_MODEL) for n, _, rows in REST_SPLITS], axis=0)


def _rest_unpack(a, shapes):
    return {n: a[r0:r0 + rows].reshape(shapes[n]) for n, r0, rows in REST_SPLITS}


def _small_pack(pre, post, memg, bforget, bmerge):
    pad = jnp.zeros((1, D_MODEL - B_HEADS), F32)
    return jnp.concatenate([pre, post, memg, bmerge.reshape(3, D_MODEL),
                            jnp.concatenate([bforget, pad], axis=1), jnp.zeros((1, D_MODEL), F32)], axis=0)


def _small_unpack(s8):
    return dict(norm_pre_g=s8[0:1], norm_post_g=s8[1:2], norm_mem_g=s8[2:3],
                b_merge=s8[3:6].reshape(1, 3 * D_MODEL), b_forget=s8[6:7, :B_HEADS])


WEIGHTS = ("norm_pre_g", "norm_post_g", "norm_mem_g", "w_in", "b_forget", "b_merge", "w_mem_kv",
           "w_branch_a", "w_branch_b", "w_branch_m", "w_out")
SMALL = ("norm_pre_g", "norm_post_g", "norm_mem_g", "b_forget", "b_merge")


def kernel(x, mem, positions, norm_pre_g, norm_post_g, norm_mem_g, w_in, b_forget, b_merge, w_mem_kv, w_branch_a, w_branch_b, w_branch_m, w_out, loss_target, m_norm_pre_g, m_norm_post_g, m_norm_mem_g, m_w_in, m_b_forget, m_b_merge, m_w_mem_kv, m_w_branch_a, m_w_branch_b, m_w_branch_m, m_w_out, v_norm_pre_g, v_norm_post_g, v_norm_mem_g, v_w_in, v_b_forget, v_b_merge, v_w_mem_kv, v_w_branch_a, v_w_branch_b, v_w_branch_m, v_w_out):
    w = dict(norm_pre_g=norm_pre_g, norm_post_g=norm_post_g, norm_mem_g=norm_mem_g, w_in=w_in[0],
             b_forget=b_forget, b_merge=b_merge, w_mem_kv=w_mem_kv[0], w_branch_a=w_branch_a[0],
             w_branch_b=w_branch_b[0], w_branch_m=w_branch_m[0], w_out=w_out[0])
    mo = dict(norm_pre_g=m_norm_pre_g, norm_post_g=m_norm_post_g, norm_mem_g=m_norm_mem_g, w_in=m_w_in[0],
              b_forget=m_b_forget, b_merge=m_b_merge, w_mem_kv=m_w_mem_kv[0], w_branch_a=m_w_branch_a[0],
              w_branch_b=m_w_branch_b[0], w_branch_m=m_w_branch_m[0], w_out=m_w_out[0])
    vo = dict(norm_pre_g=v_norm_pre_g, norm_post_g=v_norm_post_g, norm_mem_g=v_norm_mem_g, w_in=v_w_in[0],
              b_forget=v_b_forget, b_merge=v_b_merge, w_mem_kv=v_w_mem_kv[0], w_branch_a=v_w_branch_a[0],
              w_branch_b=v_w_branch_b[0], w_branch_m=v_w_branch_m[0], w_out=v_w_out[0])
    s = x.shape[1]
    c = lax.axis_index("c")

    chip = 2 * lax.axis_index("x") + lax.axis_index("y")

    def put(whole, own, slot):
        return lax.dynamic_update_index_in_dim(whole, own.astype(whole.dtype), slot, 0)

    own_w = [w["w_in"].astype(BF16).reshape(2, D_MODEL // 2, SHARD_COLS),
             _rest_pack(w).astype(BF16).reshape(2, REST_ROWS // 2, D_MODEL)]
    all_in, all_rest = _gather_weights(own_w)
    all_in = all_in.reshape(N_CHIPS, D_MODEL, SHARD_COLS)
    own_in, own_rest = own_w[0].reshape(D_MODEL, SHARD_COLS), own_w[1].reshape(REST_ROWS, D_MODEL)
    w_main, w_fb = _split_forget([jnp.where(chip == p, own_in, all_in[p]) for p in range(N_CHIPS)])
    w_fb = jnp.concatenate([w_fb, jnp.zeros((D_MODEL, HD - B_HEADS), BF16)], axis=1)
    all_rest = all_rest.reshape(N_CHIPS, REST_ROWS, D_MODEL)
    all_rest = jnp.stack([jnp.where(chip == p, own_rest, all_rest[p]) for p in range(N_CHIPS)])
    w_kv_f = all_rest[:, 0:256].reshape(D_MODEL, D_MODEL)
    w_br_f = [all_rest[:, 256 + 128 * i:384 + 128 * i].reshape(N_CHIPS, 512, 256).transpose(1, 0, 2)
              .reshape(512, D_MODEL) for i in range(3)]
    w_out_f = all_rest[:, 640:896].reshape(D_MODEL, D_MODEL)

    pair = []

    def exchange(g):
        def per_chip(name, p):
            a = g[name]
            if name in ("w_mem_kv", "w_out"):
                return a[256 * p:256 * (p + 1)]
            return a[:, 256 * p:256 * (p + 1)]

        in4 = jnp.stack(g["w_in"])
        rest4 = jnp.stack([_rest_pack({n: per_chip(n, p) for n, _, _ in REST_SPLITS}) for p in range(N_CHIPS)])
        halves = [in4.reshape(N_CHIPS, 2, D_MODEL // 2, SHARD_COLS),
                  rest4.reshape(N_CHIPS, 2, REST_ROWS // 2, D_MODEL)]
        got = _swap_with_sibling(halves)
        pair.extend(_add_pair("add_pair_%d" % i, halves[i], got[i], c) for i in range(2))
        return _scatter_to_owners(pair)

    loss_lanes, grad_x, g, landed = _local_step(
        x[0], mem[0], positions.reshape(s, 1), loss_target[0], norm_pre_g, norm_post_g, norm_mem_g,
        w_main, w_fb, b_forget, b_merge, w_kv_f, w_br_f[0], w_br_f[1], w_br_f[2], w_out_f, exchange)
    loss = lax.psum(jnp.sum(loss_lanes), ("x", "y", "c"))
    half = [_add_chips("add_chips_%d" % i, landed[i], pair[i], chip) for i in range(2)]
    red_in, red_rest = [put(a, o[0], c) for a, o in zip(_share_with_sibling(half), half)]
    gs = {"w_in": red_in.reshape(D_MODEL, SHARD_COLS)}
    gs.update(_rest_unpack(red_rest.reshape(REST_ROWS, D_MODEL), {n: w[n].shape for n, _, _ in REST_SPLITS}))
    gs.update(_small_unpack(_sum_small(_small_pack(
        g["norm_pre_g"], g["norm_post_g"], g["norm_mem_g"], g["b_forget"], g["b_merge"]))))

    delta, new_m, new_v = {}, {}, {}
    for n, tm in (("w_in", 128), ("w_mem_kv", 256), ("w_branch_a", 512), ("w_branch_b", 512),
                  ("w_branch_m", 512), ("w_out", 256)):
        d_, m_, v_ = _adamw("adamw_" + n, w[n], gs[n], mo[n], vo[n], tm)
        delta[n], new_m[n], new_v[n] = d_[None], m_[None], v_[None]
        gs[n] = gs[n][None]
    packs = [_small_pack(*[t[n] for n in SMALL])
             for t in (w, gs, mo, vo)]
    for res, store in zip(_adamw("adamw_small", *packs, 8), (delta, new_m, new_v)):
        store.update(_small_unpack(res))

    return (loss, grad_x[None], *[gs[n] for n in WEIGHTS], *[delta[n] for n in WEIGHTS],
            *[new_m[n] for n in WEIGHTS], *[new_v[n] for n in WEIGHTS])
```

```python
import functools

import jax
import jax.numpy as jnp
from jax import lax
from jax.experimental import pallas as pl
from jax.experimental.pallas import tpu as pltpu

F32 = jnp.float32
BF16 = jnp.bfloat16
MESH = pl.DeviceIdType.MESH

D_MODEL = 1024
N_MEM = 256
EPS = 1e-6
NEG = -1e30
ROPE_THETA = 500000.0
ROT_DIM = 32
HD = 128
A_GROUP = 512
DILATIONS = (1, 4, 16)
BAND = 128
B_HEADS = 8
B_HD = 64
N_CHIPS = 4
N_DEV = 8

C_QA, C_KA, C_VA, C_ZA = 0, 1536, 3072, 4608
C_QB, C_KB, C_VB, C_ZB = 5120, 5632, 6144, 6656
C_QM, C_ZM, C_GL = 7168, 7680, 8192
FB_ORIG = 6656
IN_COLS = 11272
SHARD_COLS = IN_COLS // N_CHIPS

ADAM_LR, ADAM_B1, ADAM_B2, ADAM_EPS, ADAM_WD, ADAM_STEP = 0.001, 0.9, 0.999, 1e-08, 0.01, 10

VMEM_LIMIT_V7X = 56 * 1024 * 1024

NT = (((1,), (1,)), ((), ()))
NN = (((1,), (0,)), ((), ()))
TN = (((0,), (0,)), ((), ()))


def _params(sem):
    return pltpu.CompilerParams(dimension_semantics=sem, vmem_limit_bytes=VMEM_LIMIT_V7X)


def _dot(a, b, dn=NN):
    return lax.dot_general(a, b, dn, preferred_element_type=F32)


def _sig(z):
    return 1.0 / (1.0 + jnp.exp(-z))


def _rows(name, fn, row_ins, bc_ins, outs, reds=(), tm=512, scratch=(), into=None):
    arrs, specs = [], []
    s = None
    for r in row_ins:
        arr, w, cb, d = (tuple(r) + (1,))[:4] if isinstance(r, tuple) else (r, r.shape[1], 0, 1)
        s = arr.shape[0] * d if s is None else s
        arrs.append(arr)
        specs.append((w, cb, d))
    tm = min(tm, s)
    specs = [pl.BlockSpec((tm // d, w), functools.partial(lambda i, cb: (i, cb), cb=cb)) for w, cb, d in specs]
    for b in bc_ins:
        arrs.append(b)
        specs.append(pl.BlockSpec(b.shape, lambda i: (0, 0)))
    outs = [(tuple(o) + (1,))[:3] for o in outs]
    n_in, n_out = len(arrs), len(outs)
    o0 = n_in + (0 if into is None else 1)

    def body(*refs):
        n_ref = o0 + n_out + len(reds)
        vals = fn(*[r[...] for r in refs[:n_in]], *refs[n_ref:])
        if not isinstance(vals, (tuple, list)):
            vals = (vals,)
        for r, v in zip(refs[o0:o0 + n_out], vals[:n_out]):
            r[...] = v.astype(r.dtype)
        if reds:
            red_refs = refs[o0 + n_out:n_ref]

            @pl.when(pl.program_id(0) == 0)
            def _():
                for r in red_refs:
                    r[...] = jnp.zeros_like(r)

            for r, v in zip(red_refs, vals[n_out:]):
                r[...] += v

    out_shape = [jax.ShapeDtypeStruct((s // d, c), dt) for c, dt, d in outs]
    out_shape += [jax.ShapeDtypeStruct((1, c), F32) for c in reds]
    out_specs = [pl.BlockSpec((tm // d, c), lambda i: (i, 0)) for c, _, d in outs]
    out_specs += [pl.BlockSpec((1, c), lambda i: (0, 0)) for c in reds]
    aliases = {}
    if into is not None:
        whole, k, cb = into
        out_shape[k] = jax.ShapeDtypeStruct(whole.shape, whole.dtype)
        if isinstance(cb, tuple):
            out_specs[k] = pl.BlockSpec((pl.Element(tm), pl.Element(outs[k][0])),
                                        functools.partial(lambda i, c0: (i * tm, c0), c0=cb[1]))
        else:
            out_specs[k] = pl.BlockSpec((tm, outs[k][0]), functools.partial(lambda i, cb: (i, cb), cb=cb))
        aliases = {n_in: k}
        arrs.append(whole)
        specs.append(pl.BlockSpec(memory_space=pl.ANY))
    res = pl.pallas_call(
        body, name=name, grid=(s // tm,), in_specs=specs, out_specs=out_specs, out_shape=out_shape,
        scratch_shapes=list(scratch), input_output_aliases=aliases,
        compiler_params=_params(("arbitrary",) if reds else ("parallel",)),
    )(*arrs)
    return res


def _to_class(x, scr, d):
    if d == 1:
        return x.astype(F32)
    tm, c = x.shape
    for g in range(c // 128):
        scr[g][...] = x[:, g * 128:(g + 1) * 128].astype(F32)
    return jnp.concatenate([scr[g][pl.ds(r, tm // d, stride=d), :] for r in range(d) for g in range(c // 128)],
                           axis=1)


def _from_class(x, scr, d):
    if d == 1:
        return x.astype(F32)
    n, dc = x.shape
    c = dc // d
    for r in range(d):
        for g in range(c // 128):
            scr[g][pl.ds(r, n, stride=d), :] = x[:, r * c + g * 128:r * c + (g + 1) * 128].astype(F32)
    return jnp.concatenate([scr[g][...] for g in range(c // 128)], axis=1)


def _mm(name, a, b, mode, out_dtype, tm=2048, tn=1024, tk=1024, side=None):
    if mode == "nn":
        (m, k), (_, n) = a.shape, b.shape
    elif mode == "nt":
        (m, k), (n, _) = a.shape, b.shape
    else:
        (k, m), (_, n) = a.shape, b.shape
    tm, tn, tk = min(tm, m), min(tn, n), min(tk, k)
    nk = k // tk
    grid = (m // tm, n // tn, nk)
    dn = {"nn": NN, "nt": NT, "tn": TN}[mode]
    n_si = len(side["ins"]) if side else 0
    n_so = len(side["outs"]) if side else 0
    n_acc = 1 if nk > 1 else 0

    def body(*refs):
        a_ref, b_ref = refs[:2]
        side_in, o_ref = refs[2:2 + n_si], refs[2 + n_si]
        side_out = refs[3 + n_si:3 + n_si + n_so]
        acc = refs[3 + n_si + n_so:3 + n_si + n_so + n_acc]
        side_scratch = refs[3 + n_si + n_so + n_acc:]
        step = (pl.program_id(0) * grid[1] + pl.program_id(1)) * grid[2] + pl.program_id(2)
        if side:
            @pl.when(step == 0)
            def _():
                side["start"](side_in, side_out, side_scratch)

        part = _dot(a_ref[...].astype(BF16), b_ref[...].astype(BF16), dn)
        if nk == 1:
            o_ref[...] = part.astype(o_ref.dtype)
        else:
            kk = pl.program_id(2)

            @pl.when(kk == 0)
            def _():
                acc[0][...] = part

            @pl.when(kk > 0)
            def _():
                acc[0][...] += part

            @pl.when(kk == nk - 1)
            def _():
                o_ref[...] = acc[0][...].astype(o_ref.dtype)

        if side:
            @pl.when(step == grid[0] * grid[1] * grid[2] - 1)
            def _():
                side["wait"](side_in, side_out, side_scratch)

    a_spec = (pl.BlockSpec((tk, tm), lambda i, j, kk: (kk, i)) if mode == "tn"
              else pl.BlockSpec((tm, tk), lambda i, j, kk: (i, kk)))
    b_spec = (pl.BlockSpec((tn, tk), lambda i, j, kk: (j, kk)) if mode == "nt"
              else pl.BlockSpec((tk, tn), lambda i, j, kk: (kk, j)))
    o_spec = pl.BlockSpec((tm, tn), lambda i, j, kk: (i, j))
    o_shape = jax.ShapeDtypeStruct((m, n), out_dtype)
    acc_scratch = [pltpu.VMEM((tm, tn), F32)] * n_acc
    if not side:
        return pl.pallas_call(
            body, name=name, grid=grid, in_specs=[a_spec, b_spec], out_specs=o_spec, out_shape=o_shape,
            scratch_shapes=acc_scratch, compiler_params=_params(("parallel", "parallel", "arbitrary")),
        )(a, b)
    return pl.pallas_call(
        body, name=name, grid=grid, in_specs=[a_spec, b_spec] + [HBM_SPEC] * n_si,
        out_specs=[o_spec] + [HBM_SPEC] * n_so, out_shape=[o_shape] + side["outs"],
        scratch_shapes=acc_scratch + side["scratch"],
        compiler_params=_params(("arbitrary", "arbitrary", "arbitrary")),
    )(a, b, *side["ins"])


def _rms_fwd(name, x, g):
    def fn(xv, gv):
        r = lax.rsqrt(jnp.mean(xv * xv, axis=-1, keepdims=True) + EPS)
        return (xv * r * gv,)
    return _rows(name, fn, [x], [g], [(x.shape[1], BF16)], tm=min(512, x.shape[0]))[0]


def _rms_fwd_both(name, x, g, w_fb):
    s, dm = x.shape
    tm = min(512, s)

    def body(x_ref, g_ref, w_ref, h_ref, ht_ref, fb_ref):
        xv = x_ref[...]
        hv = xv * lax.rsqrt(jnp.mean(xv * xv, axis=-1, keepdims=True) + EPS) * g_ref[...]
        hb = hv.astype(BF16)
        h_ref[...] = hb
        ht_ref[...] = hv.T.astype(BF16)
        fb_ref[...] = _dot(hb, w_ref[...])

    return pl.pallas_call(
        body, name=name, grid=(s // tm,),
        in_specs=[pl.BlockSpec((tm, dm), lambda i: (i, 0)), pl.BlockSpec((1, dm), lambda i: (0, 0)),
                  pl.BlockSpec(w_fb.shape, lambda i: (0, 0))],
        out_specs=[pl.BlockSpec((tm, dm), lambda i: (i, 0)), pl.BlockSpec((dm, tm), lambda i: (0, i)),
                   pl.BlockSpec((tm, w_fb.shape[1]), lambda i: (i, 0))],
        out_shape=[jax.ShapeDtypeStruct((s, dm), BF16), jax.ShapeDtypeStruct((dm, s), BF16),
                   jax.ShapeDtypeStruct((s, w_fb.shape[1]), F32)],
        compiler_params=_params(("parallel",)),
    )(x, g, w_fb)


def _rope_tables(pos, inv):
    ang = pos.astype(F32) * inv
    lane = lax.broadcasted_iota(jnp.int32, ang.shape, 1)
    c = jnp.where(lane < ROT_DIM, jnp.cos(ang), 1.0)
    sn = jnp.sin(ang)
    sg = jnp.where(lane < ROT_DIM // 2, -sn, jnp.where(lane < ROT_DIM, sn, 0.0))
    return c, sg, lane


def _rope_apply(x, c, sg, lane):
    outs = []
    for h in range(x.shape[1] // HD):
        xh = x[:, h * HD:(h + 1) * HD].astype(F32)
        swap = jnp.where(lane < ROT_DIM // 2, pltpu.roll(xh, HD - ROT_DIM // 2, 1),
                         pltpu.roll(xh, ROT_DIM // 2, 1))
        outs.append(xh * c + swap * sg)
    return jnp.concatenate(outs, axis=1)


ROPE_TM = 512


def _class_scratch(tm):
    return [pltpu.VMEM((tm, 128), F32) for _ in range(A_GROUP // 128)]


def _rope_fwd(u, pos, inv):
    def fn(q, k, v, p, iv, *scr):
        c, sg, lane = _rope_tables(p, iv)
        qr, kr = _rope_apply(q, c, sg, lane), _rope_apply(k, c, sg, lane)
        outs = []
        for g, d in enumerate(DILATIONS):
            gs = slice(g * A_GROUP, (g + 1) * A_GROUP)
            outs += [_to_class(qr[:, gs], scr, d), _to_class(kr[:, gs], scr, d), _to_class(v[:, gs], scr, d)]
        return tuple(outs)

    outs = [(d * A_GROUP, BF16, d) for d in DILATIONS for _ in range(3)]
    qkv = [(u, 3 * A_GROUP, c0 // (3 * A_GROUP)) for c0 in (C_QA, C_KA, C_VA)]
    return _rows("rope_fwd", fn, qkv + [pos], [inv], outs, tm=ROPE_TM,
                 scratch=_class_scratch(ROPE_TM))


def _rope_bwd(dqs, dks, dvs, pos, inv, du):
    def fn(*args):
        grads, p, iv, scr = args[:9], args[9], args[10], args[11:]
        c, sg, lane = _rope_tables(p, iv)
        tok = [jnp.concatenate([_from_class(grads[3 * k + g], scr, d) for g, d in enumerate(DILATIONS)], axis=1)
               for k in range(3)]
        return (jnp.concatenate([_rope_apply(tok[0], c, -sg, lane), _rope_apply(tok[1], c, -sg, lane), tok[2]],
                                axis=1),)

    ins = [(a, a.shape[1], 0, d) for grp in (dqs, dks, dvs) for a, d in zip(grp, DILATIONS)]
    return _rows("rope_bwd", fn, ins + [pos], [inv], [(9 * A_GROUP, BF16)], tm=ROPE_TM,
                 scratch=_class_scratch(ROPE_TM), into=(du, 0, 0))[0]


def _lane_pack(cols, like):
    lane = lax.broadcasted_iota(jnp.int32, like, 1)
    out = jnp.zeros(like, F32)
    for h, cvec in enumerate(cols):
        out = jnp.where(lane == h, cvec, out)
    return out


def _band_specs(l, d, tq):
    nsb = tq // BAND
    nblk = l // BAND
    cur = pl.BlockSpec((tq, A_GROUP), lambda r, i: (i, r))
    prev = pl.BlockSpec((BAND, A_GROUP), lambda r, i: (jnp.maximum(i * nsb - 1, 0), r))
    nxt = pl.BlockSpec((BAND, A_GROUP), lambda r, i: (jnp.minimum((i + 1) * nsb, nblk - 1), r))
    st_cur = pl.BlockSpec((tq, HD), lambda r, i: (i, r))
    st_nxt = pl.BlockSpec((BAND, HD), lambda r, i: (jnp.minimum((i + 1) * nsb, nblk - 1), r))
    return nsb, cur, prev, nxt, st_cur, st_nxt


def _band_mask_q(i, first_tile):
    qr = lax.broadcasted_iota(jnp.int32, (BAND, 2 * BAND), 0)
    kc = lax.broadcasted_iota(jnp.int32, (BAND, 2 * BAND), 1)
    in_prev = (kc < BAND) & (kc >= qr)
    in_cur = (kc >= BAND) & (kc - BAND <= qr)
    if i == 0:
        in_prev = in_prev & jnp.logical_not(first_tile)
    return in_prev | in_cur


def _band_mask_k(j, nsb, last_tile):
    kc = lax.broadcasted_iota(jnp.int32, (BAND, 2 * BAND), 0)
    qr = lax.broadcasted_iota(jnp.int32, (BAND, 2 * BAND), 1)
    same = (qr < BAND) & (kc <= qr)
    nxt = (qr >= BAND) & (kc >= qr - BAND)
    if j == nsb - 1:
        nxt = nxt & jnp.logical_not(last_tile)
    return same | nxt


def _band_fwd(name, q, k, v, d):
    l = q.shape[0]
    tq = min(512, l)
    nsb, cur, prev, _, st_cur, _ = _band_specs(l, d, tq)
    scale = HD ** -0.5

    def body(q_ref, kc_ref, kp_ref, vc_ref, vp_ref, o_ref, lse_ref):
        first = pl.program_id(1) == 0
        for i in range(nsb):
            lses = []
            mask = _band_mask_q(i, first)
            for h in range(4):
                cs = slice(h * HD, (h + 1) * HD)
                qv = q_ref[i * BAND:(i + 1) * BAND, cs]
                if i == 0:
                    kk = jnp.concatenate([kp_ref[:, cs], kc_ref[0:BAND, cs]], axis=0)
                    vv = jnp.concatenate([vp_ref[:, cs], vc_ref[0:BAND, cs]], axis=0)
                else:
                    kk = kc_ref[(i - 1) * BAND:(i + 1) * BAND, cs]
                    vv = vc_ref[(i - 1) * BAND:(i + 1) * BAND, cs]
                s = jnp.where(mask, _dot(qv, kk, NT) * scale, NEG)
                m = jnp.max(s, axis=-1, keepdims=True)
                p = jnp.exp(s - m)
                den = jnp.sum(p, axis=-1, keepdims=True)
                o_ref[i * BAND:(i + 1) * BAND, cs] = _dot(p.astype(BF16), vv) / den
                lses.append(m + jnp.log(den))
            lse_ref[i * BAND:(i + 1) * BAND, :] = _lane_pack(lses, (BAND, HD))

    return pl.pallas_call(
        body, name=name, grid=(d, l // tq), in_specs=[cur, cur, prev, cur, prev],
        out_specs=[cur, st_cur],
        out_shape=[jax.ShapeDtypeStruct((l, d * A_GROUP), F32), jax.ShapeDtypeStruct((l, d * HD), F32)],
        compiler_params=_params(("parallel", "parallel")),
    )(q, k, k, v, v)


def _band_dq(name, q, k, v, dy, lse, delta, d):
    l = q.shape[0]
    tq = min(512, l)
    nsb, cur, prev, _, st_cur, _ = _band_specs(l, d, tq)
    scale = HD ** -0.5

    def body(q_ref, kc_ref, kp_ref, vc_ref, vp_ref, dy_ref, lse_ref, dl_ref, dq_ref):
        first = pl.program_id(1) == 0
        for i in range(nsb):
            mask = _band_mask_q(i, first)
            rs = slice(i * BAND, (i + 1) * BAND)
            for h in range(4):
                cs = slice(h * HD, (h + 1) * HD)
                if i == 0:
                    kk = jnp.concatenate([kp_ref[:, cs], kc_ref[0:BAND, cs]], axis=0)
                    vv = jnp.concatenate([vp_ref[:, cs], vc_ref[0:BAND, cs]], axis=0)
                else:
                    kk = kc_ref[(i - 1) * BAND:(i + 1) * BAND, cs]
                    vv = vc_ref[(i - 1) * BAND:(i + 1) * BAND, cs]
                s = jnp.where(mask, _dot(q_ref[rs, cs], kk, NT) * scale, NEG)
                p = jnp.exp(s - lse_ref[rs, h:h + 1])
                dp = _dot(dy_ref[rs, cs], vv, NT)
                ds = p * (dp - dl_ref[rs, h:h + 1])
                dq_ref[rs, cs] = (_dot(ds.astype(BF16), kk) * scale).astype(dq_ref.dtype)

    return pl.pallas_call(
        body, name=name, grid=(d, l // tq),
        in_specs=[cur, cur, prev, cur, prev, cur, st_cur, st_cur], out_specs=cur,
        out_shape=jax.ShapeDtypeStruct((l, d * A_GROUP), BF16),
        compiler_params=_params(("parallel", "parallel")),
    )(q, k, k, v, v, dy, lse, delta)


def _band_dkv(name, q, k, v, dy, lse, delta, d):
    l = q.shape[0]
    tq = min(512, l)
    nsb, cur, _, nxt, st_cur, st_nxt = _band_specs(l, d, tq)
    scale = HD ** -0.5
    ntile = l // tq

    def body(k_ref, v_ref, qc_ref, qn_ref, dyc_ref, dyn_ref, lc_ref, ln_ref, dc_ref, dn_ref,
             dk_ref, dv_ref):
        last = pl.program_id(1) == ntile - 1

        def win(c_ref, n_ref, j, cs):
            if j == nsb - 1:
                return jnp.concatenate([c_ref[j * BAND:(j + 1) * BAND, cs], n_ref[:, cs]], axis=0)
            return c_ref[j * BAND:(j + 2) * BAND, cs]

        allh = slice(0, HD)
        for j in range(nsb):
            mask = _band_mask_k(j, nsb, last)
            rs = slice(j * BAND, (j + 1) * BAND)
            lse_t = win(lc_ref, ln_ref, j, allh).T
            delta_t = win(dc_ref, dn_ref, j, allh).T
            for h in range(4):
                cs = slice(h * HD, (h + 1) * HD)
                qw = win(qc_ref, qn_ref, j, cs)
                dyw = win(dyc_ref, dyn_ref, j, cs)
                st = jnp.where(mask, _dot(k_ref[rs, cs], qw, NT) * scale, NEG)
                pt = jnp.exp(st - lse_t[h:h + 1, :])
                dst = pt * (_dot(v_ref[rs, cs], dyw, NT) - delta_t[h:h + 1, :])
                dv_ref[rs, cs] = _dot(pt.astype(BF16), dyw).astype(dv_ref.dtype)
                dk_ref[rs, cs] = (_dot(dst.astype(BF16), qw) * scale).astype(dk_ref.dtype)

    shp = jax.ShapeDtypeStruct((l, d * A_GROUP), BF16)
    return pl.pallas_call(
        body, name=name, grid=(d, ntile),
        in_specs=[cur, cur, cur, nxt, cur, nxt, st_cur, st_nxt, st_cur, st_nxt],
        out_specs=[cur, cur], out_shape=[shp, shp],
        compiler_params=_params(("parallel", "parallel")),
    )(k, v, q, q, dy, dy, lse, lse, delta, delta)


def _split3(x):
    hi = x.astype(BF16)
    r1 = x - hi.astype(F32)
    mid = r1.astype(BF16)
    lo = (r1 - mid.astype(F32)).astype(BF16)
    return hi, mid, lo


def _fox_prep(z, b):
    h, s = z.shape
    blk = min(512, s)

    def body(z_ref, b_ref, c_ref):
        r = lax.broadcasted_iota(jnp.int32, (blk, blk), 0)
        cidx = lax.broadcasted_iota(jnp.int32, (blk, blk), 1)
        tri = (r <= cidx).astype(BF16)
        carry = jnp.zeros((h, 1), F32)
        for t in range(s // blk):
            zz = z_ref[:, t * blk:(t + 1) * blk] + b_ref[...]
            lf = jnp.minimum(zz, 0.0) - jnp.log(1.0 + jnp.exp(-jnp.abs(zz)))
            hi, mid, lo = _split3(lf)
            cs = _dot(hi, tri) + _dot(mid, tri) + _dot(lo, tri) + carry
            c_ref[:, t * blk:(t + 1) * blk] = cs
            carry = cs[:, blk - 1:blk]

    return pl.pallas_call(body, name="fox_prep", out_shape=jax.ShapeDtypeStruct((h, s), F32))(z, b)


def _fox_prep_bwd(dc, z, b):
    h, s = z.shape
    blk = min(512, s)

    def body(dc_ref, z_ref, b_ref, dz_ref, db_ref):
        r = lax.broadcasted_iota(jnp.int32, (blk, blk), 0)
        cidx = lax.broadcasted_iota(jnp.int32, (blk, blk), 1)
        tri = (r >= cidx).astype(BF16)
        carry = jnp.zeros((h, 1), F32)
        tot = jnp.zeros((h, 1), F32)
        for t in reversed(range(s // blk)):
            hi, mid, lo = _split3(dc_ref[:, t * blk:(t + 1) * blk])
            rc = _dot(hi, tri) + _dot(mid, tri) + _dot(lo, tri) + carry
            carry = rc[:, 0:1]
            zz = z_ref[:, t * blk:(t + 1) * blk] + b_ref[...]
            dz = rc * _sig(-zz)
            dz_ref[:, t * blk:(t + 1) * blk] = dz
            tot = tot + jnp.sum(dz, axis=-1, keepdims=True)
        db_ref[...] = tot

    return pl.pallas_call(
        body, name="fox_prep_bwd",
        out_shape=[jax.ShapeDtypeStruct((h, s), F32), jax.ShapeDtypeStruct((h, 1), F32)])(dc, z, b)


FOX_W = 128
FOX_C = B_HD
FOX_ONE = B_HD + 3
FOX_SUB = 256
FOX_SUB_FWD = 128
FOX_HEADS_PER_STEP = 2


def _head_of_pair(x, hh):
    return x if hh == 0 else pltpu.roll(x, B_HD, 1)


def _fox_pack(u, c_col, t):
    s = u.shape[0]
    nt = s // t
    scale = B_HD ** -0.5

    def body(q_ref, k_ref, v_ref, c_ref, qf_ref, kb_ref, ks_ref, vb_ref, vt_ref, q2_ref, k2_ref):
        lane = lax.broadcasted_iota(jnp.int32, (t, FOX_W), 1)
        head_lanes = (lax.broadcasted_iota(jnp.int32, (FOX_W, FOX_W), 0) < B_HD).astype(BF16)

        def top_norm2(xv):
            n2 = _dot((xv * xv).astype(BF16), head_lanes)
            return jnp.broadcast_to(jnp.max(n2, axis=0, keepdims=True)[:, :1], (8, 128))

        for hd in range(B_HEADS):
            pair, hh = slice(hd // 2 * FOX_W, (hd // 2 + 1) * FOX_W), hd % 2
            qv, kv, vv = [r[:, pair].astype(F32) for r in (q_ref, k_ref, v_ref)]
            qh = _head_of_pair(qv, hh)
            q2_ref[hd] = top_norm2(qh)
            qf_ref[hd] = jnp.where(lane < B_HD, qh, B_HD ** 0.5).astype(BF16)
            neg = c_ref[hd] * (-scale)
            hi = neg.astype(BF16).astype(F32)
            mid = (neg - hi).astype(BF16).astype(F32)
            lo = neg - hi - mid
            aux = jnp.where(lane == FOX_C, hi,
                            jnp.where(lane == FOX_C + 1, mid, jnp.where(lane == FOX_C + 2, lo, 0.0)))
            kb = jnp.where(lane < B_HD, _head_of_pair(kv, hh) * scale, aux)
            k2_ref[hd] = top_norm2(jnp.where(lane < B_HD, kb, 0.0))
            kb_ref[hd] = kb.astype(BF16)
            ks_ref[hd] = jnp.where(lane == FOX_ONE, 1.0, kb).T.astype(BF16)
            vb = jnp.where(lane < B_HD, _head_of_pair(vv, hh), 1.0)
            vb_ref[hd] = vb.astype(BF16)
            vt_ref[hd] = vb.T.astype(BF16)

    def tok(col0):
        return pl.BlockSpec((t, B_HEADS * B_HD), functools.partial(lambda i, cb: (i, cb), cb=col0 // (B_HEADS * B_HD)))

    rows = pl.BlockSpec((B_HEADS, t, FOX_W), lambda i: (0, i, 0))
    tiles = pl.BlockSpec((B_HEADS, None, FOX_W, t), lambda i: (0, i, 0, 0))
    hm = jax.ShapeDtypeStruct((B_HEADS, s, FOX_W), BF16)
    tt = jax.ShapeDtypeStruct((B_HEADS, nt, FOX_W, t), BF16)
    return pl.pallas_call(
        body, name="fox_pack", grid=(nt,),
        in_specs=[tok(C_QB), tok(C_KB), tok(C_VB), pl.BlockSpec((B_HEADS, t, 1), lambda i: (0, i, 0))],
        out_specs=[rows, rows, tiles, rows, tiles] + [pl.BlockSpec((B_HEADS, None, 8, 128), lambda i: (0, i, 0, 0))] * 2,
        out_shape=[hm, hm, tt, hm, tt] + [jax.ShapeDtypeStruct((B_HEADS, nt, 8, 128), F32)] * 2,
        compiler_params=_params(("parallel",)),
    )(u, u, u, c_col)


def _fox_pack_bwd(dy, y, t):
    s = dy.shape[0]
    nt = s // t

    def body(do_ref, o_ref, dow_ref, dl_ref):
        lane = lax.broadcasted_iota(jnp.int32, (t, FOX_W), 1)
        lane8 = lax.broadcasted_iota(jnp.int32, (8, FOX_W), 1)
        for pr in range(B_HEADS // 2):
            pair = slice(pr * FOX_W, (pr + 1) * FOX_W)
            dov = do_ref[:, pair].astype(F32)
            parts = _split3(dov * o_ref[:, pair].astype(F32))
            for hh in range(2):
                dow_ref[2 * pr + hh] = jnp.where(lane < B_HD, _head_of_pair(dov, hh), 0.0).astype(BF16)
                mask = ((lane8 >= hh * B_HD) & (lane8 < (hh + 1) * B_HD)).astype(BF16)
                row = _dot(mask, parts[0], NT) + _dot(mask, parts[1], NT) + _dot(mask, parts[2], NT)
                dl_ref[2 * pr + hh] = row[0:1, :]

    tok = pl.BlockSpec((t, B_HEADS * B_HD), lambda i: (i, 0))
    return pl.pallas_call(
        body, name="fox_pack_bwd", grid=(nt,), in_specs=[tok, tok],
        out_specs=[pl.BlockSpec((B_HEADS, t, FOX_W), lambda i: (0, i, 0)),
                   pl.BlockSpec((B_HEADS, None, 1, t), lambda i: (0, i, 0, 0))],
        out_shape=[jax.ShapeDtypeStruct((B_HEADS, s, FOX_W), BF16), jax.ShapeDtypeStruct((B_HEADS, nt, 1, t), F32)],
        compiler_params=_params(("parallel",)),
    )(dy, y)


def _fox_unpack(dqt, dkw, dvw, du, t):
    h, nt = dqt.shape[:2]

    def body(dq_ref, dk_ref, dv_ref, _, o_ref, dc_ref):
        lane = lax.broadcasted_iota(jnp.int32, (t, FOX_W), 1)

        def join(a0, a1):
            return jnp.where(lane < B_HD, a0, pltpu.roll(a1, B_HD, 1))

        for hh in range(h):
            dc_ref[hh] = dq_ref[hh][FOX_ONE:FOX_ONE + 1, :] - dk_ref[hh].T[B_HD:B_HD + 1, :]
        pairs = range(0, h, 2)
        cols = ([join(dq_ref[a].T, dq_ref[a + 1].T) for a in pairs] + [join(dk_ref[a], dk_ref[a + 1]) for a in pairs]
                + [join(dv_ref[a], dv_ref[a + 1]) for a in pairs])
        o_ref[...] = jnp.concatenate(cols, axis=1).astype(o_ref.dtype)

    rows = pl.BlockSpec((h, t, FOX_W), lambda i: (0, i, 0))
    return pl.pallas_call(
        body, name="fox_unpack", grid=(nt,),
        in_specs=[pl.BlockSpec((h, None, FOX_W, t), lambda i: (0, i, 0, 0)), rows, rows,
                  pl.BlockSpec(memory_space=pl.ANY)],
        out_specs=[pl.BlockSpec((pl.Element(t), pl.Element(3 * h * B_HD)), lambda i: (i * t, C_QB)),
                   pl.BlockSpec((h, None, 1, t), lambda i: (0, i, 0, 0))],
        out_shape=[jax.ShapeDtypeStruct(du.shape, du.dtype), jax.ShapeDtypeStruct((h, nt, 1, t), F32)],
        input_output_aliases={3: 0},
        compiler_params=_params(("parallel",)),
    )(dqt, dkw, dvw, du)


FOX_DEAD = -110.0


def _fox_bounds(q2, k2, c, t):
    g = 2.0 * jnp.sqrt(1.02 * jnp.max(q2[:, :, 0, 0], axis=1) * 1.02 * jnp.max(k2[:, :, 0, 0], axis=1))
    return jnp.concatenate([c[:, ::t], c[:, t - 1::t], g[:, None]], axis=1)


SMEM_SPEC = pl.BlockSpec(memory_space=pltpu.SMEM)


def _fox_fwd(qf, kb, vt4, bounds, t):
    h, s, w = qf.shape
    nt = s // t
    sub = FOX_SUB_FWD
    nsub = t // sub
    nh = FOX_HEADS_PER_STEP

    def body(b_ref, q_ref, k_ref, v_ref, o_ref, lse_ref):
        i = pl.program_id(1)
        krow = lax.broadcasted_iota(jnp.int32, (sub, t), 0)
        qcol = lax.broadcasted_iota(jnp.int32, (sub, t), 1)

        def dead_before(hh):
            head = pl.program_id(0) * nh + hh
            top = b_ref[head, 2 * nt] + b_ref[head, i]
            return lax.fori_loop(
                0, i, lambda jj, n: n + (top - b_ref[head, nt + jj] < FOX_DEAD).astype(jnp.int32), 0)

        j_lo = functools.reduce(jnp.minimum, [dead_before(hh) for hh in range(nh)])

        def tile(j, carry, diag):
            out = []
            for hh in range(nh):
                m, acc = carry[hh]
                qv, vj = q_ref[hh], v_ref[hh, j]
                los = [b * sub if diag else 0 for b in range(nsub)]
                sts = [_dot(k_ref[hh, pl.ds(pl.multiple_of(j * t + b * sub, sub), sub), :], qv[lo:, :], NT)
                       for b, lo in enumerate(los)]
                for b, lo in enumerate(los):
                    st = sts[b]
                    if diag:
                        st = jnp.where(krow[:, :t - lo] <= qcol[:, :t - lo], st, NEG)
                    m_old, acc_old = m[:, lo:], acc[:, lo:]
                    m2 = jnp.maximum(m_old, jnp.max(st, axis=0, keepdims=True))
                    p = jnp.exp(st - m2).astype(BF16)
                    acc2 = jnp.exp(m_old - m2) * acc_old + _dot(vj[:, b * sub:(b + 1) * sub], p)
                    m = m2 if lo == 0 else jnp.concatenate([m[:, :lo], m2], axis=1)
                    acc = acc2 if lo == 0 else jnp.concatenate([acc[:, :lo], acc2], axis=1)
                out.append((m, acc))
            return tuple(out)

        init = tuple((jnp.full((1, t), NEG, F32), jnp.zeros((w, t), F32)) for _ in range(nh))
        carry = lax.fori_loop(j_lo, i, lambda j, c: tile(j, c, False), init)
        outs = []
        for hh, (m, acc) in enumerate(tile(i, carry, True)):
            den = acc[B_HD:B_HD + 1, :]
            outs.append(acc[0:B_HD, :] / den)
            lse_ref[hh] = m + jnp.log(den)
        o_ref[...] = jnp.concatenate(outs, axis=0).T.astype(o_ref.dtype)

    return pl.pallas_call(
        body, name="fox_fwd", grid=(h // nh, nt),
        in_specs=[SMEM_SPEC,
                  pl.BlockSpec((nh, t, w), lambda hh, i: (hh, i, 0)),
                  pl.BlockSpec((nh, s, w), lambda hh, i: (hh, 0, 0)),
                  pl.BlockSpec((nh, nt, w, t), lambda hh, i: (hh, 0, 0, 0))],
        out_specs=[pl.BlockSpec((t, nh * B_HD), lambda hh, i: (i, hh)),
                   pl.BlockSpec((nh, 1, t), lambda hh, i: (hh, 0, i))],
        out_shape=[jax.ShapeDtypeStruct((s, h * B_HD), BF16), jax.ShapeDtypeStruct((h, 1, s), F32)],
        compiler_params=_params(("parallel", "parallel")),
    )(bounds, qf, kb, vt4)


def _fox_bwd(qf, dow, lse_row, delta_row, kb, kst4, vb, bounds, t):
    h, s, w = qf.shape
    nt = s // t
    nsub = t // FOX_SUB
    nh = FOX_HEADS_PER_STEP

    def body(b_ref, q_ref, do_ref, lse_ref, dl_ref, k_ref, kt_ref, v_ref, dqt_ref, dk_ref, dv_ref, dk_acc, dv_acc):
        j = pl.program_id(1)

        def alive_after(hh):
            head = pl.program_id(0) * nh + hh
            top = b_ref[head, 2 * nt] - b_ref[head, nt + j]
            return lax.fori_loop(
                j + 1, nt, lambda ii, n: n + (top + b_ref[head, ii] >= FOX_DEAD).astype(jnp.int32), 0)

        i_hi = j + 1 + functools.reduce(jnp.maximum, [alive_after(hh) for hh in range(nh)])

        @pl.when(j == 0)
        def _():
            dqt_ref[...] = jnp.zeros_like(dqt_ref)

        dk_acc[...] = jnp.zeros_like(dk_acc)
        dv_acc[...] = jnp.zeros_like(dv_acc)
        krow = lax.broadcasted_iota(jnp.int32, (FOX_SUB, t), 0)
        qcol = lax.broadcasted_iota(jnp.int32, (FOX_SUB, t), 1)
        subs = [slice(b * FOX_SUB, (b + 1) * FOX_SUB) for b in range(nsub)]

        def tile(i, diag):
            i0 = pl.multiple_of(i * t, t)
            for hh in range(nh):
                qi, doi = q_ref[hh, pl.ds(i0, t), :], do_ref[hh, pl.ds(i0, t), :]
                lse, dl = lse_ref[hh, i], dl_ref[hh, i]
                los = [b * FOX_SUB if diag else 0 for b in range(nsub)]
                sts = [_dot(k_ref[hh, rs, :], qi[lo:, :], NT) for rs, lo in zip(subs, los)]
                dps = [_dot(v_ref[hh, rs, :], doi[lo:, :], NT) for rs, lo in zip(subs, los)]
                dq = None
                for b, (rs, lo) in enumerate(zip(subs, los)):
                    st = sts[b] - lse[:, lo:]
                    if diag:
                        st = jnp.where(krow[:, :t - lo] <= qcol[:, :t - lo], st, NEG)
                    pt = jnp.exp(st)
                    dsb = (pt * (dps[b] - dl[:, lo:])).astype(BF16)
                    dv_acc[hh, rs, :] += _dot(pt.astype(BF16), doi[lo:, :])
                    dk_acc[hh, rs, :] += _dot(dsb, qi[lo:, :])
                    part = _dot(kt_ref[hh, :, rs], dsb)
                    if lo:
                        part = jnp.concatenate([jnp.zeros((w, lo), F32), part], axis=1)
                    dq = part if dq is None else dq + part
                dqt_ref[hh, i] += dq

        def step(i, carry):
            tile(i, False)
            return carry

        tile(j, True)
        lax.fori_loop(j + 1, i_hi, step, 0)
        dk_ref[...] = dk_acc[...] * (B_HD ** -0.5)
        dv_ref[...] = dv_acc[...]

    full = pl.BlockSpec((nh, s, w), lambda hh, j: (hh, 0, 0))
    rowst = pl.BlockSpec((nh, nt, 1, t), lambda hh, j: (hh, 0, 0, 0))
    tl = pl.BlockSpec((nh, t, w), lambda hh, j: (hh, j, 0))
    return pl.pallas_call(
        body, name="fox_bwd", grid=(h // nh, nt),
        in_specs=[SMEM_SPEC, full, full, rowst, rowst, tl,
                  pl.BlockSpec((nh, None, w, t), lambda hh, j: (hh, j, 0, 0)), tl],
        out_specs=[pl.BlockSpec((nh, nt, w, t), lambda hh, j: (hh, 0, 0, 0)), tl, tl],
        out_shape=[jax.ShapeDtypeStruct((h, nt, w, t), F32), jax.ShapeDtypeStruct((h, s, w), F32),
                   jax.ShapeDtypeStruct((h, s, w), F32)],
        scratch_shapes=[pltpu.VMEM((nh, t, w), F32), pltpu.VMEM((nh, t, w), F32)],
        compiler_params=_params(("parallel", "arbitrary")),
    )(bounds, qf, dow, lse_row, delta_row, kb, kst4, vb)


def _mem_fwd(u, mkv, tq=1024):
    s = u.shape[0]
    scale = HD ** -0.5

    def body(q_ref, mk_ref, mv_ref, o_ref, lse_ref):
        lses = []
        for h in range(4):
            cs = slice(h * HD, (h + 1) * HD)
            sc = _dot(q_ref[:, cs], mk_ref[:, cs], NT) * scale
            m = jnp.max(sc, axis=-1, keepdims=True)
            p = jnp.exp(sc - m)
            den = jnp.sum(p, axis=-1, keepdims=True)
            o_ref[:, cs] = (_dot(p.astype(BF16), mv_ref[:, cs]) / den).astype(o_ref.dtype)
            lses.append(m + jnp.log(den))
        lse_ref[...] = _lane_pack(lses, (tq, HD))

    return pl.pallas_call(
        body, name="mem_fwd", grid=(s // tq,),
        in_specs=[pl.BlockSpec((tq, 512), lambda i: (i, C_QM // 512)),
                  pl.BlockSpec((N_MEM, 512), lambda i: (0, 0)),
                  pl.BlockSpec((N_MEM, 512), lambda i: (0, 1))],
        out_specs=[pl.BlockSpec((tq, 512), lambda i: (i, 0)), pl.BlockSpec((tq, HD), lambda i: (i, 0))],
        out_shape=[jax.ShapeDtypeStruct((s, 512), BF16), jax.ShapeDtypeStruct((s, HD), F32)],
        compiler_params=_params(("parallel",)),
    )(u, mkv, mkv)


def _mem_bwd(u, mkv, o, do, lse, du, tq=1024):
    s = u.shape[0]
    scale = HD ** -0.5

    def body(q_ref, mk_ref, mv_ref, o_ref, do_ref, lse_ref, _, dq_ref, dmk_ref, dmv_ref):
        @pl.when(pl.program_id(0) == 0)
        def _():
            dmk_ref[...] = jnp.zeros_like(dmk_ref)
            dmv_ref[...] = jnp.zeros_like(dmv_ref)

        for h in range(4):
            cs = slice(h * HD, (h + 1) * HD)
            qv, dov = q_ref[:, cs], do_ref[:, cs]
            sc = _dot(qv, mk_ref[:, cs], NT) * scale
            p = jnp.exp(sc - lse_ref[:, h:h + 1])
            delta = jnp.sum(dov.astype(F32) * o_ref[:, cs].astype(F32), axis=-1, keepdims=True)
            ds = p * (_dot(dov, mv_ref[:, cs], NT) - delta)
            dsb = ds.astype(BF16)
            dq_ref[:, cs] = (_dot(dsb, mk_ref[:, cs]) * scale).astype(dq_ref.dtype)
            dmk_ref[:, cs] += _dot(dsb, qv, TN) * scale
            dmv_ref[:, cs] += _dot(p.astype(BF16), dov, TN)

    row = pl.BlockSpec((tq, 512), lambda i: (i, 0))
    acc = pl.BlockSpec((N_MEM, 512), lambda i: (0, 0))
    return pl.pallas_call(
        body, name="mem_bwd", grid=(s // tq,),
        in_specs=[pl.BlockSpec((tq, 512), lambda i: (i, C_QM // 512)),
                  pl.BlockSpec((N_MEM, 512), lambda i: (0, 0)),
                  pl.BlockSpec((N_MEM, 512), lambda i: (0, 1)),
                  row, row, pl.BlockSpec((tq, HD), lambda i: (i, 0)), pl.BlockSpec(memory_space=pl.ANY)],
        out_specs=[pl.BlockSpec((tq, 512), lambda i: (i, C_QM // 512)), acc, acc],
        out_shape=[jax.ShapeDtypeStruct(du.shape, du.dtype), jax.ShapeDtypeStruct((N_MEM, 512), F32),
                   jax.ShapeDtypeStruct((N_MEM, 512), F32)],
        input_output_aliases={6: 0},
        compiler_params=_params(("arbitrary",)),
    )(u, mkv, mkv, o, do, lse, du)


FB_CHIP = FB_ORIG // SHARD_COLS
FB_AT = FB_ORIG - FB_CHIP * SHARD_COLS


def _chip_slabs(main, fb):
    cuts = [SHARD_COLS * p - (B_HEADS if p > FB_CHIP else 0) for p in range(N_CHIPS + 1)]
    slabs = [main[:, a:b] for a, b in zip(cuts[:-1], cuts[1:])]
    own = slabs[FB_CHIP]
    slabs[FB_CHIP] = jnp.concatenate([own[:, :FB_AT], fb, own[:, FB_AT:]], axis=1)
    return slabs


def _split_forget(slabs):
    own = slabs[FB_CHIP]
    parts = list(slabs[:FB_CHIP]) + [own[:, :FB_AT], own[:, FB_AT + B_HEADS:]] + list(slabs[FB_CHIP + 1:])
    return jnp.concatenate(parts, axis=1), own[:, FB_AT:FB_AT + B_HEADS]


def _local_step(x, mem, pos, target, g_pre, g_post, g_mem, w_main, w_fb, b_forget, b_merge,
                w_mem_kv, w_ba, w_bb, w_bm, w_out, exchange=None):
    s = x.shape[0]
    t_fox = min(512, s)
    nt = s // t_fox
    half = ROT_DIM // 2
    inv = ROPE_THETA ** (-jnp.arange(half, dtype=F32) / half)
    inv128 = jnp.concatenate([inv, inv, jnp.zeros((HD - ROT_DIM,), F32)]).reshape(1, HD)

    h, h_t, ufb = _rms_fwd_both("norm_pre", x, g_pre, w_fb)
    u = _mm("proj_in", h, w_main, "nn", BF16, tm=4096)
    memn = _rms_fwd("norm_mem", mem, g_mem)
    mkv = _mm("proj_mem", memn, w_mem_kv, "nn", BF16)

    qkv = _rope_fwd(u, pos, inv128)
    views = [tuple(qkv[3 * g:3 * g + 3]) for g in range(3)]
    os_, lses = [], []
    for g, d in enumerate(DILATIONS):
        o_g, lse_g = _band_fwd("band_fwd%d" % g, *views[g], d)
        os_.append((o_g, d * A_GROUP, 0, d))
        lses.append((lse_g, d * HD, 0, d))

    def merge_a(o1, o2, o3, l1, l2, l3, za, *scr):
        o1, o2, o3 = [_from_class(o, scr, d) for o, d in zip((o1, o2, o3), DILATIONS)]
        l1, l2, l3 = [_from_class(lv, scr, d) for lv, d in zip((l1, l2, l3), DILATIONS)]
        ys, tots = [], []
        for hh in range(4):
            cs, hs = slice(hh * HD, (hh + 1) * HD), slice(hh, hh + 1)
            mx = jnp.maximum(jnp.maximum(l1[:, hs], l2[:, hs]), l3[:, hs])
            e1, e2, e3 = jnp.exp(l1[:, hs] - mx), jnp.exp(l2[:, hs] - mx), jnp.exp(l3[:, hs] - mx)
            den = e1 + e2 + e3
            ys.append((e1 * o1[:, cs] + e2 * o2[:, cs] + e3 * o3[:, cs]) / den)
            tots.append(mx + jnp.log(den))
        y = jnp.concatenate(ys, axis=1)
        zf = za.astype(F32)
        tot = _lane_pack(tots, l1.shape)
        return (y, y * (zf * _sig(zf))) + tuple(_to_class(tot, scr, d) for d in DILATIONS)

    res = _rows("merge_a", merge_a, os_ + lses + [(u, 512, C_ZA // 512)], [],
                [(512, BF16), (512, BF16)] + [(d * HD, F32, d) for d in DILATIONS], tm=ROPE_TM,
                scratch=_class_scratch(ROPE_TM))
    y_a, yg_a, lse_a = res[0], res[1], res[2:5]

    zrow = ufb[:, :B_HEADS].T
    c = _fox_prep(zrow, b_forget.reshape(B_HEADS, 1))
    qf, kb, kst4, vb, vt4, q2, k2 = _fox_pack(u, c.reshape(B_HEADS, s, 1), t_fox)
    bounds = _fox_bounds(q2, k2, c, t_fox)
    y_b, lse_b = _fox_fwd(qf, kb, vt4, bounds, t_fox)

    y_m, lse_m = _mem_fwd(u, mkv)

    def gate(y, z):
        zf = z.astype(F32)
        return (y.astype(F32) * (zf * _sig(zf)),)

    yg_b = _rows("gate_b", gate, [y_b, (u, 512, C_ZB // 512)], [], [(512, BF16)])[0]
    yg_m = _rows("gate_m", gate, [y_m, (u, 512, C_ZM // 512)], [], [(512, BF16)])[0]

    br_a = _mm("branch_a", yg_a, w_ba, "nn", BF16)
    br_b = _mm("branch_b", yg_b, w_bb, "nn", BF16)
    br_m = _mm("branch_m", yg_m, w_bm, "nn", BF16)
    gl = [(u, 1024, C_GL // 1024 + i) for i in range(3)]
    bm3 = b_merge.reshape(3, D_MODEL)

    def merge(g0, g1, g2, b0, b1, b2, bm):
        tot = 0.0
        for i, (gv, bv) in enumerate(((g0, b0), (g1, b1), (g2, b2))):
            tot = tot + _sig(gv.astype(F32) + bm[i:i + 1, :]) * bv.astype(F32)
        return (tot,)

    merged = _rows("merge_gates", merge, gl + [br_a, br_b, br_m], [bm3], [(D_MODEL, BF16)])[0]
    out = _mm("proj_out", merged, w_out, "nn", F32)

    def tail(xv, ov, tv, gv):
        r = lax.rsqrt(jnp.mean(ov * ov, axis=-1, keepdims=True) + EPS)
        n = ov * r
        err = xv + n * gv - tv
        dy = err * (1.0 / D_MODEL)
        dn = dy * gv
        dout = r * (dn - n * jnp.mean(dn * n, axis=-1, keepdims=True))
        return (dy, dout, jnp.sum(0.5 * err * err * (1.0 / D_MODEL), axis=0, keepdims=True),
                jnp.sum(dy * n, axis=0, keepdims=True))

    dy, dout, loss_lanes, g_post_grad = _rows(
        "tail", tail, [x, out, target], [g_post], [(D_MODEL, F32), (D_MODEL, BF16)],
        reds=[D_MODEL, D_MODEL], tm=512)

    dmerged = _mm("d_merged", dout, w_out, "nt", BF16)
    gw_out = _mm("g_w_out", merged, dout, "tn", F32, tk=2048)

    def merge_bwd(dm, g0, g1, g2, b0, b1, b2, bm):
        dmf = dm.astype(F32)
        dbs, dgs, sums = [], [], []
        for i, (gv, bv) in enumerate(((g0, b0), (g1, b1), (g2, b2))):
            sg = _sig(gv.astype(F32) + bm[i:i + 1, :])
            dbs.append(dmf * sg)
            dg = dmf * bv.astype(F32) * sg * (1.0 - sg)
            dgs.append(dg)
            sums.append(jnp.sum(dg, axis=0, keepdims=True))
        return tuple(dbs + [jnp.concatenate(dgs, axis=1)] + sums)

    du = lax.empty(u.shape, BF16)
    res = _rows("merge_bwd", merge_bwd, [dmerged] + gl + [br_a, br_b, br_m], [bm3],
                [(D_MODEL, BF16)] * 3 + [(3 * D_MODEL, BF16)], reds=[D_MODEL] * 3, tm=512,
                into=(du, 3, ("column", C_GL)))
    dbr, du, g_bmerge = res[0:3], res[3], jnp.concatenate(res[4:7], axis=1)

    dyg, gw_branch = [], []
    for nm, dbv, wv, ygv in (("a", dbr[0], w_ba, yg_a), ("b", dbr[1], w_bb, yg_b), ("m", dbr[2], w_bm, yg_m)):
        dyg.append(_mm("d_yg_" + nm, dbv, wv, "nt", BF16))
        gw_branch.append(_mm("g_w_branch_" + nm, ygv, dbv, "tn", F32, tk=2048))

    def gate_bwd(dg, y, z):
        dgf, yf, zf = dg.astype(F32), y.astype(F32), z.astype(F32)
        sg = _sig(zf)
        return dgf * (zf * sg), dgf * yf * (sg * (1.0 + zf * (1.0 - sg)))

    def gate_bwd_a(dg, y, z, *scr):
        dyv, dz = gate_bwd(dg, y, z)
        prod = dyv * y.astype(F32)
        dl = [jnp.sum(prod[:, hh * HD:(hh + 1) * HD], axis=-1, keepdims=True) for hh in range(4)]
        delta = _lane_pack(dl, (dg.shape[0], HD))
        return ((dz,) + tuple(_to_class(dyv, scr, d) for d in DILATIONS)
                + tuple(_to_class(delta, scr, d) for d in DILATIONS))

    res = _rows("gate_bwd_a", gate_bwd_a, [dyg[0], y_a, (u, 512, C_ZA // 512)], [],
                [(512, BF16)] + [(d * A_GROUP, BF16, d) for d in DILATIONS] + [(d * HD, F32, d) for d in DILATIONS],
                tm=ROPE_TM, scratch=_class_scratch(ROPE_TM), into=(du, 0, C_ZA // 512))
    du, dy_a, delta_a = res[0], res[1:4], res[4:7]
    dy_b, du = _rows("gate_bwd_b", gate_bwd, [dyg[1], y_b, (u, 512, C_ZB // 512)], [],
                     [(512, BF16), (512, BF16)], into=(du, 1, C_ZB // 512))
    dy_m, du = _rows("gate_bwd_m", gate_bwd, [dyg[2], y_m, (u, 512, C_ZM // 512)], [],
                     [(512, BF16), (512, BF16)], into=(du, 1, C_ZM // 512))

    du, dmk, dmv = _mem_bwd(u, mkv, y_m, dy_m, lse_m, du)
    dmkv = jnp.concatenate([dmk, dmv], axis=1)
    gw_mem_kv = _mm("g_w_mem_kv", memn, dmkv, "tn", F32)
    dmemn = _mm("d_memn", dmkv, w_mem_kv, "nt", F32)

    def mem_gain_grad(mv, dv):
        r = lax.rsqrt(jnp.mean(mv * mv, axis=-1, keepdims=True) + EPS)
        return (jnp.sum(dv * mv * r, axis=0, keepdims=True),)

    g_mem_grad = _rows("g_norm_mem", mem_gain_grad, [mem, dmemn], [], [], reds=[D_MODEL], tm=N_MEM)[0]

    dow, delta_b = _fox_pack_bwd(dy_b, y_b, t_fox)
    dqt, dkw, dvw = _fox_bwd(qf, dow, lse_b.reshape(B_HEADS, nt, 1, t_fox), delta_b, kb, kst4, vb, bounds, t_fox)
    du, dc = _fox_unpack(dqt, dkw, dvw, du, t_fox)
    dzrow, g_bforget = _fox_prep_bwd(dc.reshape(B_HEADS, s), zrow, b_forget.reshape(B_HEADS, 1))
    dfb = jnp.zeros((s, HD), BF16).at[:, :B_HEADS].set(dzrow.T.astype(BF16))

    dqs, dks, dvs = [], [], []
    for g, d in enumerate(DILATIONS):
        qv, kv, vv = views[g]
        dqs.append(_band_dq("band_dq%d" % g, qv, kv, vv, dy_a[g], lse_a[g], delta_a[g], d))
        dk_g, dv_g = _band_dkv("band_dkv%d" % g, qv, kv, vv, dy_a[g], lse_a[g], delta_a[g], d)
        dks.append(dk_g)
        dvs.append(dv_g)
    du = _rope_bwd(dqs, dks, dvs, pos, inv128, du)

    gw_main = _mm("g_w_main", h_t, du, "nn", F32, tk=2048)
    gw_fb = _mm("g_w_fb", h, dfb, "tn", F32)
    grads = dict(norm_post_g=g_post_grad, norm_mem_g=g_mem_grad, w_in=_chip_slabs(gw_main, gw_fb[:, :B_HEADS]),
                 b_forget=g_bforget.reshape(1, B_HEADS), b_merge=g_bmerge, w_mem_kv=gw_mem_kv,
                 w_branch_a=gw_branch[0], w_branch_b=gw_branch[1], w_branch_m=gw_branch[2], w_out=gw_out)
    side = exchange(grads) if exchange else None
    dh_main = _mm("d_h", du, w_main, "nt", F32, tm=1024, tk=2816, side=side)
    landed = None
    if side:
        dh_main, landed = dh_main[0], dh_main[1:]
    def pre_bwd(xv, d1, dfbv, dyv, gv, wfb):
        r = lax.rsqrt(jnp.mean(xv * xv, axis=-1, keepdims=True) + EPS)
        n = xv * r
        dhv = d1 + _dot(dfbv, wfb, NT)
        dn = dhv * gv
        dx = r * (dn - n * jnp.mean(dn * n, axis=-1, keepdims=True))
        return dyv + dx, jnp.sum(dhv * n, axis=0, keepdims=True)

    grad_x, g_pre_grad = _rows("norm_pre_bwd", pre_bwd, [x, dh_main, dfb, dy], [g_pre, w_fb],
                               [(D_MODEL, F32)], reds=[D_MODEL], tm=512)

    grads["norm_pre_g"] = g_pre_grad
    return loss_lanes, grad_x, grads, landed


HBM_SPEC = pl.BlockSpec(memory_space=pltpu.HBM)


def _place():
    x, y, c = lax.axis_index("x"), lax.axis_index("y"), lax.axis_index("c")
    chips = [(1 - x, y), (x, 1 - y), (1 - x, 1 - y)]
    return x, y, c, 2 * x + y, chips


N_CHUNKS = 4


def _units(parts, row_axis):
    units = []
    for i, a in enumerate(parts):
        ch = a.shape[row_axis] // N_CHUNKS
        units += [(i, pl.ds(k * ch, ch)) for k in range(N_CHUNKS)]
    return units


def _gather_weights(parts):
    n = len(parts)
    units = _units(parts, 1)
    nu = len(units)
    via_y = [(u % N_CHUNKS) < N_CHUNKS // 2 for u in range(nu)]

    def body(*refs):
        srcs, outs = refs[:n], refs[n:2 * n]
        send_sems, recv_sems = refs[2 * n:]
        x, y, c, p, _ = _place()
        me, sib = (x, y, c), (x, y, 1 - c)
        xn, yn, dg = (1 - x, y), (x, 1 - y), (1 - x, 1 - y)

        def cp(u, k, chip, half, to, from_src=False):
            i, rs = units[u]
            dst = outs[i].at[2 * chip[0] + chip[1], half, rs]
            return pltpu.make_async_remote_copy(
                src_ref=srcs[i].at[half, rs] if from_src else dst, dst_ref=dst, send_sem=send_sems.at[u, k],
                recv_sem=recv_sems.at[u, k], device_id=to, device_id_type=MESH)

        sent = []

        def go(copy):
            copy.start()
            sent.append(copy)

        for u in range(nu):
            go(cp(u, 0, (x, y), c, (*xn, c), from_src=True))
            go(cp(u, 1, (x, y), c, (*yn, c), from_src=True))
        for u in range(nu):
            cp(u, 0, xn, c, me).wait_recv()
            go(cp(u, 4, xn, c, sib))
            if via_y[u]:
                go(cp(u, 2, xn, c, (*yn, c)))
            cp(u, 1, yn, c, me).wait_recv()
            go(cp(u, 5, yn, c, sib))
            if not via_y[u]:
                go(cp(u, 3, yn, c, (*xn, c)))
        for u in range(nu):
            cp(u, 2 if via_y[u] else 3, dg, c, me).wait_recv()
            go(cp(u, 6, dg, c, sib))
        for u in range(nu):
            for k, chip in ((4, xn), (5, yn), (6, dg)):
                cp(u, k, chip, 1 - c, me).wait_recv()
        for copy in sent:
            copy.wait_send()

    return pl.pallas_call(
        body, name="gather_weights", in_specs=[HBM_SPEC] * n, out_specs=[HBM_SPEC] * n,
        out_shape=[jax.ShapeDtypeStruct((N_CHIPS,) + a.shape, a.dtype) for a in parts],
        scratch_shapes=[pltpu.SemaphoreType.DMA((nu, 7)), pltpu.SemaphoreType.DMA((nu, 7))],
    )(*parts)


def _swap_with_sibling(parts):
    n = len(parts)
    units = _units(parts, 2)

    def body(*refs):
        srcs, outs = refs[:n], refs[n:2 * n]
        send_sems, recv_sems = refs[2 * n:]
        x, y, c, _, _ = _place()
        cps = [pltpu.make_async_remote_copy(
            src_ref=srcs[i].at[q, 1 - c, rs], dst_ref=outs[i].at[q, rs], send_sem=send_sems.at[u, q],
            recv_sem=recv_sems.at[u, q], device_id=(x, y, 1 - c), device_id_type=MESH)
            for q in range(N_CHIPS) for u, (i, rs) in enumerate(units)]
        for cpy in cps:
            cpy.start()
        for cpy in cps:
            cpy.wait()

    return pl.pallas_call(
        body, name="swap_with_sibling", in_specs=[HBM_SPEC] * n, out_specs=[HBM_SPEC] * n,
        out_shape=[jax.ShapeDtypeStruct(a.shape[:1] + a.shape[2:], a.dtype) for a in parts],
        scratch_shapes=[pltpu.SemaphoreType.DMA((len(units), N_CHIPS)),
                        pltpu.SemaphoreType.DMA((len(units), N_CHIPS))],
    )(*parts)


def _scatter_to_owners(parts):
    n = len(parts)
    units = _units(parts, 1)

    def copies(srcs, outs, send_sems, recv_sems, incoming):
        x, y, c, p, chips = _place()
        return [pltpu.make_async_remote_copy(
            src_ref=srcs[i].at[2 * cx + cy, rs], dst_ref=outs[i].at[(2 * cx + cy) if incoming else p, rs],
            send_sem=send_sems.at[u, j], recv_sem=recv_sems.at[u, j], device_id=(cx, cy, c), device_id_type=MESH)
            for u, (i, rs) in enumerate(units) for j, (cx, cy) in enumerate(chips)]

    def start(ins, outs, scratch):
        for cpy in copies(ins, outs, *scratch, incoming=False):
            cpy.start()

    def wait(ins, outs, scratch):
        for cpy in copies(ins, outs, *scratch, incoming=True):
            cpy.wait_recv()
        for cpy in copies(ins, outs, *scratch, incoming=False):
            cpy.wait_send()

    return dict(ins=list(parts), outs=[jax.ShapeDtypeStruct(a.shape, a.dtype) for a in parts],
                scratch=[pltpu.SemaphoreType.DMA((len(units), 3)), pltpu.SemaphoreType.DMA((len(units), 3))],
                start=start, wait=wait)


def _share_with_sibling(parts):
    n = len(parts)
    units = _units(parts, 1)

    def body(*refs):
        srcs, outs = refs[:n], refs[n:2 * n]
        send_sems, recv_sems = refs[2 * n:]
        x, y, c, _, _ = _place()
        sends = [pltpu.make_async_remote_copy(
            src_ref=srcs[i].at[0, rs], dst_ref=outs[i].at[c, rs], send_sem=send_sems.at[u],
            recv_sem=recv_sems.at[u], device_id=(x, y, 1 - c), device_id_type=MESH)
            for u, (i, rs) in enumerate(units)]
        for cpy in sends:
            cpy.start()
        for u, (i, rs) in enumerate(units):
            pltpu.make_async_remote_copy(
                src_ref=srcs[i].at[0, rs], dst_ref=outs[i].at[1 - c, rs], send_sem=send_sems.at[u],
                recv_sem=recv_sems.at[u], device_id=(x, y, 1 - c), device_id_type=MESH).wait_recv()
        for cpy in sends:
            cpy.wait_send()

    return pl.pallas_call(
        body, name="share_with_sibling", in_specs=[HBM_SPEC] * n, out_specs=[HBM_SPEC] * n,
        out_shape=[jax.ShapeDtypeStruct((2,) + a.shape[1:], a.dtype) for a in parts],
        scratch_shapes=[pltpu.SemaphoreType.DMA((len(units),)), pltpu.SemaphoreType.DMA((len(units),))],
    )(*parts)


def _sum_small(v):
    def body(v_ref, out_ref, buf, send_sems, recv_sems):
        x, y, c, _, _ = _place()
        me = 4 * x + 2 * y + c
        buf[me] = v_ref[...]
        flips = [(dx, dy, dc) for dx in (0, 1) for dy in (0, 1) for dc in (0, 1)][1:]
        sends = []
        for k, (dx, dy, dc) in enumerate(flips):
            cpy = pltpu.make_async_remote_copy(
                src_ref=v_ref, dst_ref=buf.at[me], send_sem=send_sems.at[k], recv_sem=recv_sems.at[k],
                device_id=((x + dx) % 2, (y + dy) % 2, (c + dc) % 2), device_id_type=MESH)
            cpy.start()
            sends.append(cpy)
        for k, (dx, dy, dc) in enumerate(flips):
            px, py, pc = (x + dx) % 2, (y + dy) % 2, (c + dc) % 2
            pltpu.make_async_remote_copy(
                src_ref=v_ref, dst_ref=buf.at[4 * px + 2 * py + pc], send_sem=send_sems.at[k],
                recv_sem=recv_sems.at[k], device_id=(px, py, pc), device_id_type=MESH).wait_recv()
        for cpy in sends:
            cpy.wait_send()
        tot = buf[0]
        for i in range(1, N_DEV):
            tot = tot + buf[i]
        out_ref[...] = tot

    return pl.pallas_call(
        body, name="sum_small", out_shape=jax.ShapeDtypeStruct(v.shape, v.dtype),
        in_specs=[pl.BlockSpec(memory_space=pltpu.VMEM)], out_specs=pl.BlockSpec(memory_space=pltpu.VMEM),
        scratch_shapes=[pltpu.VMEM((N_DEV,) + v.shape, v.dtype), pltpu.SemaphoreType.DMA((N_DEV - 1,)),
                        pltpu.SemaphoreType.DMA((N_DEV - 1,))],
    )(v)


def _add_chips(name, landed, pair, chip):
    nq, r, w = landed.shape
    tr = 128 if r % 128 == 0 else 64

    def body(chip_ref, *refs):
        own = refs[nq][...].astype(F32)
        tot = None
        for q in range(nq):
            term = jnp.where(chip_ref[0] == q, own, refs[q][...].astype(F32))
            tot = term if tot is None else tot + term
        refs[nq + 1][...] = tot

    specs = [pl.BlockSpec((None, tr, w), functools.partial(lambda j, chip_ref, q: (q, j, 0), q=q)) for q in range(nq)]
    specs.append(pl.BlockSpec((None, tr, w), lambda j, chip_ref: (chip_ref[0], j, 0)))
    grid_spec = pltpu.PrefetchScalarGridSpec(
        num_scalar_prefetch=1, grid=(r // tr,), in_specs=specs,
        out_specs=pl.BlockSpec((None, tr, w), lambda j, chip_ref: (0, j, 0)))
    return pl.pallas_call(
        body, name=name, grid_spec=grid_spec, out_shape=jax.ShapeDtypeStruct((1, r, w), F32),
        compiler_params=_params(("parallel",)),
    )(jnp.reshape(chip, (1,)).astype(jnp.int32), *([landed] * nq), pair)


def _add_pair(name, halves, got, c):
    nq, _, r, w = halves.shape
    tr = 128 if r % 128 == 0 else 64

    def body(c_ref, a_ref, b_ref, o_ref):
        o_ref[...] = (a_ref[...] + b_ref[...]).astype(o_ref.dtype)

    grid_spec = pltpu.PrefetchScalarGridSpec(
        num_scalar_prefetch=1, grid=(nq, r // tr),
        in_specs=[pl.BlockSpec((None, None, tr, w), lambda i, j, c_ref: (i, c_ref[0], j, 0)),
                  pl.BlockSpec((None, tr, w), lambda i, j, c_ref: (i, j, 0))],
        out_specs=pl.BlockSpec((None, tr, w), lambda i, j, c_ref: (i, j, 0)))
    return pl.pallas_call(
        body, name=name, grid_spec=grid_spec, out_shape=jax.ShapeDtypeStruct((nq, r, w), BF16),
        compiler_params=_params(("parallel", "parallel")),
    )(jnp.reshape(c, (1,)).astype(jnp.int32), halves, got)


def _adamw(name, w, g, m, v, tm):
    def fn(wv, gv, mv, vv):
        m2 = ADAM_B1 * mv + (1.0 - ADAM_B1) * gv
        v2 = ADAM_B2 * vv + (1.0 - ADAM_B2) * (gv * gv)
        m_hat = m2 / (1.0 - ADAM_B1 ** ADAM_STEP)
        v_hat = v2 / (1.0 - ADAM_B2 ** ADAM_STEP)
        return -ADAM_LR * (m_hat / (jnp.sqrt(v_hat) + ADAM_EPS) + ADAM_WD * wv), m2, v2
    c = w.shape[1]
    return _rows(name, fn, [w, g, m, v], [], [(c, F32)] * 3, tm=tm)


REST_ROWS = 256 + 3 * 128 + 256
REST_SPLITS = (("w_mem_kv", 0, 256), ("w_branch_a", 256, 128), ("w_branch_b", 384, 128),
               ("w_branch_m", 512, 128), ("w_out", 640, 256))


def _rest_pack(t):
    return jnp.concatenate([t[n].reshape(rows, D_MODEL) for n, _, rows in REST_SPLITS], axis=0)


def _rest_unpack(a, shapes):
    return {n: a[r0:r0 + rows].reshape(shapes[n]) for n, r0, rows in REST_SPLITS}


def _small_pack(pre, post, memg, bforget, bmerge):
    pad = jnp.zeros((1, D_MODEL - B_HEADS), F32)
    return jnp.concatenate([pre, post, memg, bmerge.reshape(3, D_MODEL),
                            jnp.concatenate([bforget, pad], axis=1), jnp.zeros((1, D_MODEL), F32)], axis=0)


def _small_unpack(s8):
    return dict(norm_pre_g=s8[0:1], norm_post_g=s8[1:2], norm_mem_g=s8[2:3],
                b_merge=s8[3:6].reshape(1, 3 * D_MODEL), b_forget=s8[6:7, :B_HEADS])


WEIGHTS = ("norm_pre_g", "norm_post_g", "norm_mem_g", "w_in", "b_forget", "b_merge", "w_mem_kv",
           "w_branch_a", "w_branch_b", "w_branch_m", "w_out")
SMALL = ("norm_pre_g", "norm_post_g", "norm_mem_g", "b_forget", "b_merge")


def kernel(x, mem, positions, norm_pre_g, norm_post_g, norm_mem_g, w_in, b_forget, b_merge, w_mem_kv, w_branch_a, w_branch_b, w_branch_m, w_out, loss_target, m_norm_pre_g, m_norm_post_g, m_norm_mem_g, m_w_in, m_b_forget, m_b_merge, m_w_mem_kv, m_w_branch_a, m_w_branch_b, m_w_branch_m, m_w_out, v_norm_pre_g, v_norm_post_g, v_norm_mem_g, v_w_in, v_b_forget, v_b_merge, v_w_mem_kv, v_w_branch_a, v_w_branch_b, v_w_branch_m, v_w_out):
    w = dict(norm_pre_g=norm_pre_g, norm_post_g=norm_post_g, norm_mem_g=norm_mem_g, w_in=w_in[0],
             b_forget=b_forget, b_merge=b_merge, w_mem_kv=w_mem_kv[0], w_branch_a=w_branch_a[0],
             w_branch_b=w_branch_b[0], w_branch_m=w_branch_m[0], w_out=w_out[0])
    mo = dict(norm_pre_g=m_norm_pre_g, norm_post_g=m_norm_post_g, norm_mem_g=m_norm_mem_g, w_in=m_w_in[0],
              b_forget=m_b_forget, b_merge=m_b_merge, w_mem_kv=m_w_mem_kv[0], w_branch_a=m_w_branch_a[0],
              w_branch_b=m_w_branch_b[0], w_branch_m=m_w_branch_m[0], w_out=m_w_out[0])
    vo = dict(norm_pre_g=v_norm_pre_g, norm_post_g=v_norm_post_g, norm_mem_g=v_norm_mem_g, w_in=v_w_in[0],
              b_forget=v_b_forget, b_merge=v_b_merge, w_mem_kv=v_w_mem_kv[0], w_branch_a=v_w_branch_a[0],
              w_branch_b=v_w_branch_b[0], w_branch_m=v_w_branch_m[0], w_out=v_w_out[0])
    s = x.shape[1]
    c = lax.axis_index("c")

    chip = 2 * lax.axis_index("x") + lax.axis_index("y")

    def put(whole, own, slot):
        return lax.dynamic_update_index_in_dim(whole, own.astype(whole.dtype), slot, 0)

    own_w = [w["w_in"].astype(BF16).reshape(2, D_MODEL // 2, SHARD_COLS),
             _rest_pack(w).astype(BF16).reshape(2, REST_ROWS // 2, D_MODEL)]
    all_in, all_rest = _gather_weights(own_w)
    all_in = all_in.reshape(N_CHIPS, D_MODEL, SHARD_COLS)
    own_in, own_rest = own_w[0].reshape(D_MODEL, SHARD_COLS), own_w[1].reshape(REST_ROWS, D_MODEL)
    w_main, w_fb = _split_forget([jnp.where(chip == p, own_in, all_in[p]) for p in range(N_CHIPS)])
    w_fb = jnp.concatenate([w_fb, jnp.zeros((D_MODEL, HD - B_HEADS), BF16)], axis=1)
    all_rest = all_rest.reshape(N_CHIPS, REST_ROWS, D_MODEL)
    all_rest = jnp.stack([jnp.where(chip == p, own_rest, all_rest[p]) for p in range(N_CHIPS)])
    w_kv_f = all_rest[:, 0:256].reshape(D_MODEL, D_MODEL)
    w_br_f = [all_rest[:, 256 + 128 * i:384 + 128 * i].reshape(N_CHIPS, 512, 256).transpose(1, 0, 2)
              .reshape(512, D_MODEL) for i in range(3)]
    w_out_f = all_rest[:, 640:896].reshape(D_MODEL, D_MODEL)

    pair = []

    def exchange(g):
        def per_chip(name, p):
            a = g[name]
            if name in ("w_mem_kv", "w_out"):
                return a[256 * p:256 * (p + 1)]
            return a[:, 256 * p:256 * (p + 1)]

        in4 = jnp.stack(g["w_in"])
        rest4 = jnp.stack([_rest_pack({n: per_chip(n, p) for n, _, _ in REST_SPLITS}) for p in range(N_CHIPS)])
        halves = [in4.reshape(N_CHIPS, 2, D_MODEL // 2, SHARD_COLS),
                  rest4.reshape(N_CHIPS, 2, REST_ROWS // 2, D_MODEL)]
        got = _swap_with_sibling(halves)
        pair.extend(_add_pair("add_pair_%d" % i, halves[i], got[i], c) for i in range(2))
        return _scatter_to_owners(pair)

    loss_lanes, grad_x, g, landed = _local_step(
        x[0], mem[0], positions.reshape(s, 1), loss_target[0], norm_pre_g, norm_post_g, norm_mem_g,
        w_main, w_fb, b_forget, b_merge, w_kv_f, w_br_f[0], w_br_f[1], w_br_f[2], w_out_f, exchange)
    loss = lax.psum(jnp.sum(loss_lanes), ("x", "y", "c"))
    half = [_add_chips("add_chips_%d" % i, landed[i], pair[i], chip) for i in range(2)]
    red_in, red_rest = [put(a, o[0], c) for a, o in zip(_share_with_sibling(half), half)]
    gs = {"w_in": red_in.reshape(D_MODEL, SHARD_COLS)}
    gs.update(_rest_unpack(red_rest.reshape(REST_ROWS, D_MODEL), {n: w[n].shape for n, _, _ in REST_SPLITS}))
    gs.update(_small_unpack(_sum_small(_small_pack(
        g["norm_pre_g"], g["norm_post_g"], g["norm_mem_g"], g["b_forget"], g["b_merge"]))))

    delta, new_m, new_v = {}, {}, {}
    for n, tm in (("w_in", 128), ("w_mem_kv", 256), ("w_branch_a", 512), ("w_branch_b", 512),
                  ("w_branch_m", 512), ("w_out", 256)):
        d_, m_, v_ = _adamw("adamw_" + n, w[n], gs[n], mo[n], vo[n], tm)
        delta[n], new_m[n], new_v[n] = d_[None], m_[None], v_[None]
        gs[n] = gs[n][None]
    packs = [_small_pack(*[t[n] for n in SMALL])
             for t in (w, gs, mo, vo)]
    for res, store in zip(_adamw("adamw_small", *packs, 8), (delta, new_m, new_v)):
        store.update(_small_unpack(res))

    return (loss, grad_x[None], *[gs[n] for n in WEIGHTS], *[delta[n] for n in WEIGHTS],
            *[new_m[n] for n in WEIGHTS], *[new_v[n] for n in WEIGHTS])
```

```python
import functools

import jax
import jax.numpy as jnp
from jax import lax
from jax.experimental import pallas as pl
from jax.experimental.pallas import tpu as pltpu

F32 = jnp.float32
BF16 = jnp.bfloat16
MESH = pl.DeviceIdType.MESH

D_MODEL = 1024
N_MEM = 256
EPS = 1e-6
NEG = -1e30
ROPE_THETA = 500000.0
ROT_DIM = 32
HD = 128
A_GROUP = 512
DILATIONS = (1, 4, 16)
BAND = 128
B_HEADS = 8
B_HD = 64
N_CHIPS = 4
N_DEV = 8

C_QA, C_KA, C_VA, C_ZA = 0, 1536, 3072, 4608
C_QB, C_KB, C_VB, C_ZB = 5120, 5632, 6144, 6656
C_QM, C_ZM, C_GL = 7168, 7680, 8192
FB_ORIG = 6656
IN_COLS = 11272
SHARD_COLS = IN_COLS // N_CHIPS

ADAM_LR, ADAM_B1, ADAM_B2, ADAM_EPS, ADAM_WD, ADAM_STEP = 0.001, 0.9, 0.999, 1e-08, 0.01, 10

VMEM_LIMIT_V7X = 56 * 1024 * 1024

NT = (((1,), (1,)), ((), ()))
NN = (((1,), (0,)), ((), ()))
TN = (((0,), (0,)), ((), ()))


def _params(sem):
    return pltpu.CompilerParams(dimension_semantics=sem, vmem_limit_bytes=VMEM_LIMIT_V7X)


def _dot(a, b, dn=NN):
    return lax.dot_general(a, b, dn, preferred_element_type=F32)


def _sig(z):
    return 1.0 / (1.0 + jnp.exp(-z))


def _rows(name, fn, row_ins, bc_ins, outs, reds=(), tm=512, scratch=(), into=None):
    arrs, specs = [], []
    s = None
    for r in row_ins:
        arr, w, cb, d = (tuple(r) + (1,))[:4] if isinstance(r, tuple) else (r, r.shape[1], 0, 1)
        s = arr.shape[0] * d if s is None else s
        arrs.append(arr)
        specs.append((w, cb, d))
    tm = min(tm, s)
    specs = [pl.BlockSpec((tm // d, w), functools.partial(lambda i, cb: (i, cb), cb=cb)) for w, cb, d in specs]
    for b in bc_ins:
        arrs.append(b)
        specs.append(pl.BlockSpec(b.shape, lambda i: (0, 0)))
    outs = [(tuple(o) + (1,))[:3] for o in outs]
    n_in, n_out = len(arrs), len(outs)
    o0 = n_in + (0 if into is None else 1)

    def body(*refs):
        n_ref = o0 + n_out + len(reds)
        vals = fn(*[r[...] for r in refs[:n_in]], *refs[n_ref:])
        if not isinstance(vals, (tuple, list)):
            vals = (vals,)
        for r, v in zip(refs[o0:o0 + n_out], vals[:n_out]):
            r[...] = v.astype(r.dtype)
        if reds:
            red_refs = refs[o0 + n_out:n_ref]

            @pl.when(pl.program_id(0) == 0)
            def _():
                for r in red_refs:
                    r[...] = jnp.zeros_like(r)

            for r, v in zip(red_refs, vals[n_out:]):
                r[...] += v

    out_shape = [jax.ShapeDtypeStruct((s // d, c), dt) for c, dt, d in outs]
    out_shape += [jax.ShapeDtypeStruct((1, c), F32) for c in reds]
    out_specs = [pl.BlockSpec((tm // d, c), lambda i: (i, 0)) for c, _, d in outs]
    out_specs += [pl.BlockSpec((1, c), lambda i: (0, 0)) for c in reds]
    aliases = {}
    if into is not None:
        whole, k, cb = into
        out_shape[k] = jax.ShapeDtypeStruct(whole.shape, whole.dtype)
        if isinstance(cb, tuple):
            out_specs[k] = pl.BlockSpec((pl.Element(tm), pl.Element(outs[k][0])),
                                        functools.partial(lambda i, c0: (i * tm, c0), c0=cb[1]))
        else:
            out_specs[k] = pl.BlockSpec((tm, outs[k][0]), functools.partial(lambda i, cb: (i, cb), cb=cb))
        aliases = {n_in: k}
        arrs.append(whole)
        specs.append(pl.BlockSpec(memory_space=pl.ANY))
    res = pl.pallas_call(
        body, name=name, grid=(s // tm,), in_specs=specs, out_specs=out_specs, out_shape=out_shape,
        scratch_shapes=list(scratch), input_output_aliases=aliases,
        compiler_params=_params(("arbitrary",) if reds else ("parallel",)),
    )(*arrs)
    return res


def _to_class(x, scr, d):
    if d == 1:
        return x.astype(F32)
    tm, c = x.shape
    for g in range(c // 128):
        scr[g][...] = x[:, g * 128:(g + 1) * 128].astype(F32)
    return jnp.concatenate([scr[g][pl.ds(r, tm // d, stride=d), :] for r in range(d) for g in range(c // 128)],
                           axis=1)


def _from_class(x, scr, d):
    if d == 1:
        return x.astype(F32)
    n, dc = x.shape
    c = dc // d
    for r in range(d):
        for g in range(c // 128):
            scr[g][pl.ds(r, n, stride=d), :] = x[:, r * c + g * 128:r * c + (g + 1) * 128].astype(F32)
    return jnp.concatenate([scr[g][...] for g in range(c // 128)], axis=1)


def _mm(name, a, b, mode, out_dtype, tm=2048, tn=1024, tk=1024, side=None):
    if mode == "nn":
        (m, k), (_, n) = a.shape, b.shape
    elif mode == "nt":
        (m, k), (n, _) = a.shape, b.shape
    else:
        (k, m), (_, n) = a.shape, b.shape
    tm, tn, tk = min(tm, m), min(tn, n), min(tk, k)
    nk = k // tk
    grid = (m // tm, n // tn, nk)
    dn = {"nn": NN, "nt": NT, "tn": TN}[mode]
    n_si = len(side["ins"]) if side else 0
    n_so = len(side["outs"]) if side else 0
    n_acc = 1 if nk > 1 else 0

    def body(*refs):
        a_ref, b_ref = refs[:2]
        side_in, o_ref = refs[2:2 + n_si], refs[2 + n_si]
        side_out = refs[3 + n_si:3 + n_si + n_so]
        acc = refs[3 + n_si + n_so:3 + n_si + n_so + n_acc]
        side_scratch = refs[3 + n_si + n_so + n_acc:]
        step = (pl.program_id(0) * grid[1] + pl.program_id(1)) * grid[2] + pl.program_id(2)
        if side:
            @pl.when(step == 0)
            def _():
                side["start"](side_in, side_out, side_scratch)

        part = _dot(a_ref[...].astype(BF16), b_ref[...].astype(BF16), dn)
        if nk == 1:
            o_ref[...] = part.astype(o_ref.dtype)
        else:
            kk = pl.program_id(2)

            @pl.when(kk == 0)
            def _():
                acc[0][...] = part

            @pl.when(kk > 0)
            def _():
                acc[0][...] += part

            @pl.when(kk == nk - 1)
            def _():
                o_ref[...] = acc[0][...].astype(o_ref.dtype)

        if side:
            @pl.when(step == grid[0] * grid[1] * grid[2] - 1)
            def _():
                side["wait"](side_in, side_out, side_scratch)

    a_spec = (pl.BlockSpec((tk, tm), lambda i, j, kk: (kk, i)) if mode == "tn"
              else pl.BlockSpec((tm, tk), lambda i, j, kk: (i, kk)))
    b_spec = (pl.BlockSpec((tn, tk), lambda i, j, kk: (j, kk)) if mode == "nt"
              else pl.BlockSpec((tk, tn), lambda i, j, kk: (kk, j)))
    o_spec = pl.BlockSpec((tm, tn), lambda i, j, kk: (i, j))
    o_shape = jax.ShapeDtypeStruct((m, n), out_dtype)
    acc_scratch = [pltpu.VMEM((tm, tn), F32)] * n_acc
    if not side:
        return pl.pallas_call(
            body, name=name, grid=grid, in_specs=[a_spec, b_spec], out_specs=o_spec, out_shape=o_shape,
            scratch_shapes=acc_scratch, compiler_params=_params(("parallel", "parallel", "arbitrary")),
        )(a, b)
    return pl.pallas_call(
        body, name=name, grid=grid, in_specs=[a_spec, b_spec] + [HBM_SPEC] * n_si,
        out_specs=[o_spec] + [HBM_SPEC] * n_so, out_shape=[o_shape] + side["outs"],
        scratch_shapes=acc_scratch + side["scratch"],
        compiler_params=_params(("arbitrary", "arbitrary", "arbitrary")),
    )(a, b, *side["ins"])


def _rms_fwd(name, x, g):
    def fn(xv, gv):
        r = lax.rsqrt(jnp.mean(xv * xv, axis=-1, keepdims=True) + EPS)
        return (xv * r * gv,)
    return _rows(name, fn, [x], [g], [(x.shape[1], BF16)], tm=min(512, x.shape[0]))[0]


def _rms_fwd_both(name, x, g, w_fb):
    s, dm = x.shape
    tm = min(512, s)

    def body(x_ref, g_ref, w_ref, h_ref, ht_ref, fb_ref):
        xv = x_ref[...]
        hv = xv * lax.rsqrt(jnp.mean(xv * xv, axis=-1, keepdims=True) + EPS) * g_ref[...]
        hb = hv.astype(BF16)
        h_ref[...] = hb
        ht_ref[...] = hv.T.astype(BF16)
        fb_ref[...] = _dot(hb, w_ref[...])

    return pl.pallas_call(
        body, name=name, grid=(s // tm,),
        in_specs=[pl.BlockSpec((tm, dm), lambda i: (i, 0)), pl.BlockSpec((1, dm), lambda i: (0, 0)),
                  pl.BlockSpec(w_fb.shape, lambda i: (0, 0))],
        out_specs=[pl.BlockSpec((tm, dm), lambda i: (i, 0)), pl.BlockSpec((dm, tm), lambda i: (0, i)),
                   pl.BlockSpec((tm, w_fb.shape[1]), lambda i: (i, 0))],
        out_shape=[jax.ShapeDtypeStruct((s, dm), BF16), jax.ShapeDtypeStruct((dm, s), BF16),
                   jax.ShapeDtypeStruct((s, w_fb.shape[1]), F32)],
        compiler_params=_params(("parallel",)),
    )(x, g, w_fb)


def _rope_tables(pos, inv):
    ang = pos.astype(F32) * inv
    lane = lax.broadcasted_iota(jnp.int32, ang.shape, 1)
    c = jnp.where(lane < ROT_DIM, jnp.cos(ang), 1.0)
    sn = jnp.sin(ang)
    sg = jnp.where(lane < ROT_DIM // 2, -sn, jnp.where(lane < ROT_DIM, sn, 0.0))
    return c, sg, lane


def _rope_apply(x, c, sg, lane):
    outs = []
    for h in range(x.shape[1] // HD):
        xh = x[:, h * HD:(h + 1) * HD].astype(F32)
        swap = jnp.where(lane < ROT_DIM // 2, pltpu.roll(xh, HD - ROT_DIM // 2, 1),
                         pltpu.roll(xh, ROT_DIM // 2, 1))
        outs.append(xh * c + swap * sg)
    return jnp.concatenate(outs, axis=1)


ROPE_TM = 512


def _class_scratch(tm):
    return [pltpu.VMEM((tm, 128), F32) for _ in range(A_GROUP // 128)]


def _rope_fwd(u, pos, inv):
    def fn(q, k, v, p, iv, *scr):
        c, sg, lane = _rope_tables(p, iv)
        qr, kr = _rope_apply(q, c, sg, lane), _rope_apply(k, c, sg, lane)
        outs = []
        for g, d in enumerate(DILATIONS):
            gs = slice(g * A_GROUP, (g + 1) * A_GROUP)
            outs += [_to_class(qr[:, gs], scr, d), _to_class(kr[:, gs], scr, d), _to_class(v[:, gs], scr, d)]
        return tuple(outs)

    outs = [(d * A_GROUP, BF16, d) for d in DILATIONS for _ in range(3)]
    qkv = [(u, 3 * A_GROUP, c0 // (3 * A_GROUP)) for c0 in (C_QA, C_KA, C_VA)]
    return _rows("rope_fwd", fn, qkv + [pos], [inv], outs, tm=ROPE_TM,
                 scratch=_class_scratch(ROPE_TM))


def _rope_bwd(dqs, dks, dvs, pos, inv, du):
    def fn(*args):
        grads, p, iv, scr = args[:9], args[9], args[10], args[11:]
        c, sg, lane = _rope_tables(p, iv)
        tok = [jnp.concatenate([_from_class(grads[3 * k + g], scr, d) for g, d in enumerate(DILATIONS)], axis=1)
               for k in range(3)]
        return (jnp.concatenate([_rope_apply(tok[0], c, -sg, lane), _rope_apply(tok[1], c, -sg, lane), tok[2]],
                                axis=1),)

    ins = [(a, a.shape[1], 0, d) for grp in (dqs, dks, dvs) for a, d in zip(grp, DILATIONS)]
    return _rows("rope_bwd", fn, ins + [pos], [inv], [(9 * A_GROUP, BF16)], tm=ROPE_TM,
                 scratch=_class_scratch(ROPE_TM), into=(du, 0, 0))[0]


def _lane_pack(cols, like):
    lane = lax.broadcasted_iota(jnp.int32, like, 1)
    out = jnp.zeros(like, F32)
    for h, cvec in enumerate(cols):
        out = jnp.where(lane == h, cvec, out)
    return out


def _band_specs(l, d, tq):
    nsb = tq // BAND
    nblk = l // BAND
    cur = pl.BlockSpec((tq, A_GROUP), lambda r, i: (i, r))
    prev = pl.BlockSpec((BAND, A_GROUP), lambda r, i: (jnp.maximum(i * nsb - 1, 0), r))
    nxt = pl.BlockSpec((BAND, A_GROUP), lambda r, i: (jnp.minimum((i + 1) * nsb, nblk - 1), r))
    st_cur = pl.BlockSpec((tq, HD), lambda r, i: (i, r))
    st_nxt = pl.BlockSpec((BAND, HD), lambda r, i: (jnp.minimum((i + 1) * nsb, nblk - 1), r))
    return nsb, cur, prev, nxt, st_cur, st_nxt


def _band_mask_q(i, first_tile):
    qr = lax.broadcasted_iota(jnp.int32, (BAND, 2 * BAND), 0)
    kc = lax.broadcasted_iota(jnp.int32, (BAND, 2 * BAND), 1)
    in_prev = (kc < BAND) & (kc >= qr)
    in_cur = (kc >= BAND) & (kc - BAND <= qr)
    if i == 0:
        in_prev = in_prev & jnp.logical_not(first_tile)
    return in_prev | in_cur


def _band_mask_k(j, nsb, last_tile):
    kc = lax.broadcasted_iota(jnp.int32, (BAND, 2 * BAND), 0)
    qr = lax.broadcasted_iota(jnp.int32, (BAND, 2 * BAND), 1)
    same = (qr < BAND) & (kc <= qr)
    nxt = (qr >= BAND) & (kc >= qr - BAND)
    if j == nsb - 1:
        nxt = nxt & jnp.logical_not(last_tile)
    return same | nxt


def _band_fwd(name, q, k, v, d):
    l = q.shape[0]
    tq = min(512, l)
    nsb, cur, prev, _, st_cur, _ = _band_specs(l, d, tq)
    scale = HD ** -0.5

    def body(q_ref, kc_ref, kp_ref, vc_ref, vp_ref, o_ref, lse_ref):
        first = pl.program_id(1) == 0
        for i in range(nsb):
            lses = []
            mask = _band_mask_q(i, first)
            for h in range(4):
                cs = slice(h * HD, (h + 1) * HD)
                qv = q_ref[i * BAND:(i + 1) * BAND, cs]
                if i == 0:
                    kk = jnp.concatenate([kp_ref[:, cs], kc_ref[0:BAND, cs]], axis=0)
                    vv = jnp.concatenate([vp_ref[:, cs], vc_ref[0:BAND, cs]], axis=0)
                else:
                    kk = kc_ref[(i - 1) * BAND:(i + 1) * BAND, cs]
                    vv = vc_ref[(i - 1) * BAND:(i + 1) * BAND, cs]
                s = jnp.where(mask, _dot(qv, kk, NT) * scale, NEG)
                m = jnp.max(s, axis=-1, keepdims=True)
                p = jnp.exp(s - m)
                den = jnp.sum(p, axis=-1, keepdims=True)
                o_ref[i * BAND:(i + 1) * BAND, cs] = _dot(p.astype(BF16), vv) / den
                lses.append(m + jnp.log(den))
            lse_ref[i * BAND:(i + 1) * BAND, :] = _lane_pack(lses, (BAND, HD))

    return pl.pallas_call(
        body, name=name, grid=(d, l // tq), in_specs=[cur, cur, prev, cur, prev],
        out_specs=[cur, st_cur],
        out_shape=[jax.ShapeDtypeStruct((l, d * A_GROUP), F32), jax.ShapeDtypeStruct((l, d * HD), F32)],
        compiler_params=_params(("parallel", "parallel")),
    )(q, k, k, v, v)


def _band_dq(name, q, k, v, dy, lse, delta, d):
    l = q.shape[0]
    tq = min(512, l)
    nsb, cur, prev, _, st_cur, _ = _band_specs(l, d, tq)
    scale = HD ** -0.5

    def body(q_ref, kc_ref, kp_ref, vc_ref, vp_ref, dy_ref, lse_ref, dl_ref, dq_ref):
        first = pl.program_id(1) == 0
        for i in range(nsb):
            mask = _band_mask_q(i, first)
            rs = slice(i * BAND, (i + 1) * BAND)
            for h in range(4):
                cs = slice(h * HD, (h + 1) * HD)
                if i == 0:
                    kk = jnp.concatenate([kp_ref[:, cs], kc_ref[0:BAND, cs]], axis=0)
                    vv = jnp.concatenate([vp_ref[:, cs], vc_ref[0:BAND, cs]], axis=0)
                else:
                    kk = kc_ref[(i - 1) * BAND:(i + 1) * BAND, cs]
                    vv = vc_ref[(i - 1) * BAND:(i + 1) * BAND, cs]
                s = jnp.where(mask, _dot(q_ref[rs, cs], kk, NT) * scale, NEG)
                p = jnp.exp(s - lse_ref[rs, h:h + 1])
                dp = _dot(dy_ref[rs, cs], vv, NT)
                ds = p * (dp - dl_ref[rs, h:h + 1])
                dq_ref[rs, cs] = (_dot(ds.astype(BF16), kk) * scale).astype(dq_ref.dtype)

    return pl.pallas_call(
        body, name=name, grid=(d, l // tq),
        in_specs=[cur, cur, prev, cur, prev, cur, st_cur, st_cur], out_specs=cur,
        out_shape=jax.ShapeDtypeStruct((l, d * A_GROUP), BF16),
        compiler_params=_params(("parallel", "parallel")),
    )(q, k, k, v, v, dy, lse, delta)


def _band_dkv(name, q, k, v, dy, lse, delta, d):
    l = q.shape[0]
    tq = min(512, l)
    nsb, cur, _, nxt, st_cur, st_nxt = _band_specs(l, d, tq)
    scale = HD ** -0.5
    ntile = l // tq

    def body(k_ref, v_ref, qc_ref, qn_ref, dyc_ref, dyn_ref, lc_ref, ln_ref, dc_ref, dn_ref,
             dk_ref, dv_ref):
        last = pl.program_id(1) == ntile - 1

        def win(c_ref, n_ref, j, cs):
            if j == nsb - 1:
                return jnp.concatenate([c_ref[j * BAND:(j + 1) * BAND, cs], n_ref[:, cs]], axis=0)
            return c_ref[j * BAND:(j + 2) * BAND, cs]

        allh = slice(0, HD)
        for j in range(nsb):
            mask = _band_mask_k(j, nsb, last)
            rs = slice(j * BAND, (j + 1) * BAND)
            lse_t = win(lc_ref, ln_ref, j, allh).T
            delta_t = win(dc_ref, dn_ref, j, allh).T
            for h in range(4):
                cs = slice(h * HD, (h + 1) * HD)
                qw = win(qc_ref, qn_ref, j, cs)
                dyw = win(dyc_ref, dyn_ref, j, cs)
                st = jnp.where(mask, _dot(k_ref[rs, cs], qw, NT) * scale, NEG)
                pt = jnp.exp(st - lse_t[h:h + 1, :])
                dst = pt * (_dot(v_ref[rs, cs], dyw, NT) - delta_t[h:h + 1, :])
                dv_ref[rs, cs] = _dot(pt.astype(BF16), dyw).astype(dv_ref.dtype)
                dk_ref[rs, cs] = (_dot(dst.astype(BF16), qw) * scale).astype(dk_ref.dtype)

    shp = jax.ShapeDtypeStruct((l, d * A_GROUP), BF16)
    return pl.pallas_call(
        body, name=name, grid=(d, ntile),
        in_specs=[cur, cur, cur, nxt, cur, nxt, st_cur, st_nxt, st_cur, st_nxt],
        out_specs=[cur, cur], out_shape=[shp, shp],
        compiler_params=_params(("parallel", "parallel")),
    )(k, v, q, q, dy, dy, lse, lse, delta, delta)


def _split3(x):
    hi = x.astype(BF16)
    r1 = x - hi.astype(F32)
    mid = r1.astype(BF16)
    lo = (r1 - mid.astype(F32)).astype(BF16)
    return hi, mid, lo


def _fox_prep(z, b):
    h, s = z.shape
    blk = min(512, s)

    def body(z_ref, b_ref, c_ref):
        r = lax.broadcasted_iota(jnp.int32, (blk, blk), 0)
        cidx = lax.broadcasted_iota(jnp.int32, (blk, blk), 1)
        tri = (r <= cidx).astype(BF16)
        carry = jnp.zeros((h, 1), F32)
        for t in range(s // blk):
            zz = z_ref[:, t * blk:(t + 1) * blk] + b_ref[...]
            lf = jnp.minimum(zz, 0.0) - jnp.log(1.0 + jnp.exp(-jnp.abs(zz)))
            hi, mid, lo = _split3(lf)
            cs = _dot(hi, tri) + _dot(mid, tri) + _dot(lo, tri) + carry
            c_ref[:, t * blk:(t + 1) * blk] = cs
            carry = cs[:, blk - 1:blk]

    return pl.pallas_call(body, name="fox_prep", out_shape=jax.ShapeDtypeStruct((h, s), F32))(z, b)


def _fox_prep_bwd(dc, z, b):
    h, s = z.shape
    blk = min(512, s)

    def body(dc_ref, z_ref, b_ref, dz_ref, db_ref):
        r = lax.broadcasted_iota(jnp.int32, (blk, blk), 0)
        cidx = lax.broadcasted_iota(jnp.int32, (blk, blk), 1)
        tri = (r >= cidx).astype(BF16)
        carry = jnp.zeros((h, 1), F32)
        tot = jnp.zeros((h, 1), F32)
        for t in reversed(range(s // blk)):
            hi, mid, lo = _split3(dc_ref[:, t * blk:(t + 1) * blk])
            rc = _dot(hi, tri) + _dot(mid, tri) + _dot(lo, tri) + carry
            carry = rc[:, 0:1]
            zz = z_ref[:, t * blk:(t + 1) * blk] + b_ref[...]
            dz = rc * _sig(-zz)
            dz_ref[:, t * blk:(t + 1) * blk] = dz
            tot = tot + jnp.sum(dz, axis=-1, keepdims=True)
        db_ref[...] = tot

    return pl.pallas_call(
        body, name="fox_prep_bwd",
        out_shape=[jax.ShapeDtypeStruct((h, s), F32), jax.ShapeDtypeStruct((h, 1), F32)])(dc, z, b)


FOX_W = 128
FOX_C = B_HD
FOX_ONE = B_HD + 3
FOX_SUB = 256
FOX_SUB_FWD = 128
FOX_HEADS_PER_STEP = 2


def _head_of_pair(x, hh):
    return x if hh == 0 else pltpu.roll(x, B_HD, 1)


def _fox_pack(u, c_col, t):
    s = u.shape[0]
    nt = s // t
    scale = B_HD ** -0.5

    def body(q_ref, k_ref, v_ref, c_ref, qf_ref, kb_ref, ks_ref, vb_ref, vt_ref, q2_ref, k2_ref):
        lane = lax.broadcasted_iota(jnp.int32, (t, FOX_W), 1)
        head_lanes = (lax.broadcasted_iota(jnp.int32, (FOX_W, FOX_W), 0) < B_HD).astype(BF16)

        def top_norm2(xv):
            n2 = _dot((xv * xv).astype(BF16), head_lanes)
            return jnp.broadcast_to(jnp.max(n2, axis=0, keepdims=True)[:, :1], (8, 128))

        for hd in range(B_HEADS):
            pair, hh = slice(hd // 2 * FOX_W, (hd // 2 + 1) * FOX_W), hd % 2
            qv, kv, vv = [r[:, pair].astype(F32) for r in (q_ref, k_ref, v_ref)]
            qh = _head_of_pair(qv, hh)
            q2_ref[hd] = top_norm2(qh)
            qf_ref[hd] = jnp.where(lane < B_HD, qh, B_HD ** 0.5).astype(BF16)
            neg = c_ref[hd] * (-scale)
            hi = neg.astype(BF16).astype(F32)
            mid = (neg - hi).astype(BF16).astype(F32)
            lo = neg - hi - mid
            aux = jnp.where(lane == FOX_C, hi,
                            jnp.where(lane == FOX_C + 1, mid, jnp.where(lane == FOX_C + 2, lo, 0.0)))
            kb = jnp.where(lane < B_HD, _head_of_pair(kv, hh) * scale, aux)
            k2_ref[hd] = top_norm2(jnp.where(lane < B_HD, kb, 0.0))
            kb_ref[hd] = kb.astype(BF16)
            ks_ref[hd] = jnp.where(lane == FOX_ONE, 1.0, kb).T.astype(BF16)
            vb = jnp.where(lane < B_HD, _head_of_pair(vv, hh), 1.0)
            vb_ref[hd] = vb.astype(BF16)
            vt_ref[hd] = vb.T.astype(BF16)

    def tok(col0):
        return pl.BlockSpec((t, B_HEADS * B_HD), functools.partial(lambda i, cb: (i, cb), cb=col0 // (B_HEADS * B_HD)))

    rows = pl.BlockSpec((B_HEADS, t, FOX_W), lambda i: (0, i, 0))
    tiles = pl.BlockSpec((B_HEADS, None, FOX_W, t), lambda i: (0, i, 0, 0))
    hm = jax.ShapeDtypeStruct((B_HEADS, s, FOX_W), BF16)
    tt = jax.ShapeDtypeStruct((B_HEADS, nt, FOX_W, t), BF16)
    return pl.pallas_call(
        body, name="fox_pack", grid=(nt,),
        in_specs=[tok(C_QB), tok(C_KB), tok(C_VB), pl.BlockSpec((B_HEADS, t, 1), lambda i: (0, i, 0))],
        out_specs=[rows, rows, tiles, rows, tiles] + [pl.BlockSpec((B_HEADS, None, 8, 128), lambda i: (0, i, 0, 0))] * 2,
        out_shape=[hm, hm, tt, hm, tt] + [jax.ShapeDtypeStruct((B_HEADS, nt, 8, 128), F32)] * 2,
        compiler_params=_params(("parallel",)),
    )(u, u, u, c_col)


def _fox_pack_bwd(dy, y, t):
    s = dy.shape[0]
    nt = s // t

    def body(do_ref, o_ref, dow_ref, dl_ref):
        lane = lax.broadcasted_iota(jnp.int32, (t, FOX_W), 1)
        lane8 = lax.broadcasted_iota(jnp.int32, (8, FOX_W), 1)
        for pr in range(B_HEADS // 2):
            pair = slice(pr * FOX_W, (pr + 1) * FOX_W)
            dov = do_ref[:, pair].astype(F32)
            parts = _split3(dov * o_ref[:, pair].astype(F32))
            for hh in range(2):
                dow_ref[2 * pr + hh] = jnp.where(lane < B_HD, _head_of_pair(dov, hh), 0.0).astype(BF16)
                mask = ((lane8 >= hh * B_HD) & (lane8 < (hh + 1) * B_HD)).astype(BF16)
                row = _dot(mask, parts[0], NT) + _dot(mask, parts[1], NT) + _dot(mask, parts[2], NT)
                dl_ref[2 * pr + hh] = row[0:1, :]

    tok = pl.BlockSpec((t, B_HEADS * B_HD), lambda i: (i, 0))
    return pl.pallas_call(
        body, name="fox_pack_bwd", grid=(nt,), in_specs=[tok, tok],
        out_specs=[pl.BlockSpec((B_HEADS, t, FOX_W), lambda i: (0, i, 0)),
                   pl.BlockSpec((B_HEADS, None, 1, t), lambda i: (0, i, 0, 0))],
        out_shape=[jax.ShapeDtypeStruct((B_HEADS, s, FOX_W), BF16), jax.ShapeDtypeStruct((B_HEADS, nt, 1, t), F32)],
        compiler_params=_params(("parallel",)),
    )(dy, y)


def _fox_unpack(dqt, dkw, dvw, du, t):
    h, nt = dqt.shape[:2]

    def body(dq_ref, dk_ref, dv_ref, _, o_ref, dc_ref):
        lane = lax.broadcasted_iota(jnp.int32, (t, FOX_W), 1)

        def join(a0, a1):
            return jnp.where(lane < B_HD, a0, pltpu.roll(a1, B_HD, 1))

        for hh in range(h):
            dc_ref[hh] = dq_ref[hh][FOX_ONE:FOX_ONE + 1, :] - dk_ref[hh].T[B_HD:B_HD + 1, :]
        pairs = range(0, h, 2)
        cols = ([join(dq_ref[a].T, dq_ref[a + 1].T) for a in pairs] + [join(dk_ref[a], dk_ref[a + 1]) for a in pairs]
                + [join(dv_ref[a], dv_ref[a + 1]) for a in pairs])
        o_ref[...] = jnp.concatenate(cols, axis=1).astype(o_ref.dtype)

    rows = pl.BlockSpec((h, t, FOX_W), lambda i: (0, i, 0))
    return pl.pallas_call(
        body, name="fox_unpack", grid=(nt,),
        in_specs=[pl.BlockSpec((h, None, FOX_W, t), lambda i: (0, i, 0, 0)), rows, rows,
                  pl.BlockSpec(memory_space=pl.ANY)],
        out_specs=[pl.BlockSpec((pl.Element(t), pl.Element(3 * h * B_HD)), lambda i: (i * t, C_QB)),
                   pl.BlockSpec((h, None, 1, t), lambda i: (0, i, 0, 0))],
        out_shape=[jax.ShapeDtypeStruct(du.shape, du.dtype), jax.ShapeDtypeStruct((h, nt, 1, t), F32)],
        input_output_aliases={3: 0},
        compiler_params=_params(("parallel",)),
    )(dqt, dkw, dvw, du)


FOX_DEAD = -110.0


def _fox_bounds(q2, k2, c, t):
    g = 2.0 * jnp.sqrt(1.02 * jnp.max(q2[:, :, 0, 0], axis=1) * 1.02 * jnp.max(k2[:, :, 0, 0], axis=1))
    return jnp.concatenate([c[:, ::t], c[:, t - 1::t], g[:, None]], axis=1)


SMEM_SPEC = pl.BlockSpec(memory_space=pltpu.SMEM)


def _fox_fwd(qf, kb, vt4, bounds, t):
    h, s, w = qf.shape
    nt = s // t
    sub = FOX_SUB_FWD
    nsub = t // sub
    nh = FOX_HEADS_PER_STEP

    def body(b_ref, q_ref, k_ref, v_ref, o_ref, lse_ref):
        i = pl.program_id(1)
        krow = lax.broadcasted_iota(jnp.int32, (sub, t), 0)
        qcol = lax.broadcasted_iota(jnp.int32, (sub, t), 1)

        def dead_before(hh):
            head = pl.program_id(0) * nh + hh
            top = b_ref[head, 2 * nt] + b_ref[head, i]
            return lax.fori_loop(
                0, i, lambda jj, n: n + (top - b_ref[head, nt + jj] < FOX_DEAD).astype(jnp.int32), 0)

        j_lo = functools.reduce(jnp.minimum, [dead_before(hh) for hh in range(nh)])

        def tile(j, carry, diag):
            out = []
            for hh in range(nh):
                m, acc = carry[hh]
                qv, vj = q_ref[hh], v_ref[hh, j]
                los = [b * sub if diag else 0 for b in range(nsub)]
                sts = [_dot(k_ref[hh, pl.ds(pl.multiple_of(j * t + b * sub, sub), sub), :], qv[lo:, :], NT)
                       for b, lo in enumerate(los)]
                for b, lo in enumerate(los):
                    st = sts[b]
                    if diag:
                        st = jnp.where(krow[:, :t - lo] <= qcol[:, :t - lo], st, NEG)
                    m_old, acc_old = m[:, lo:], acc[:, lo:]
                    m2 = jnp.maximum(m_old, jnp.max(st, axis=0, keepdims=True))
                    p = jnp.exp(st - m2).astype(BF16)
                    acc2 = jnp.exp(m_old - m2) * acc_old + _dot(vj[:, b * sub:(b + 1) * sub], p)
                    m = m2 if lo == 0 else jnp.concatenate([m[:, :lo], m2], axis=1)
                    acc = acc2 if lo == 0 else jnp.concatenate([acc[:, :lo], acc2], axis=1)
                out.append((m, acc))
            return tuple(out)

        init = tuple((jnp.full((1, t), NEG, F32), jnp.zeros((w, t), F32)) for _ in range(nh))
        carry = lax.fori_loop(j_lo, i, lambda j, c: tile(j, c, False), init)
        outs = []
        for hh, (m, acc) in enumerate(tile(i, carry, True)):
            den = acc[B_HD:B_HD + 1, :]
            outs.append(acc[0:B_HD, :] / den)
            lse_ref[hh] = m + jnp.log(den)
        o_ref[...] = jnp.concatenate(outs, axis=0).T.astype(o_ref.dtype)

    return pl.pallas_call(
        body, name="fox_fwd", grid=(h // nh, nt),
        in_specs=[SMEM_SPEC,
                  pl.BlockSpec((nh, t, w), lambda hh, i: (hh, i, 0)),
                  pl.BlockSpec((nh, s, w), lambda hh, i: (hh, 0, 0)),
                  pl.BlockSpec((nh, nt, w, t), lambda hh, i: (hh, 0, 0, 0))],
        out_specs=[pl.BlockSpec((t, nh * B_HD), lambda hh, i: (i, hh)),
                   pl.BlockSpec((nh, 1, t), lambda hh, i: (hh, 0, i))],
        out_shape=[jax.ShapeDtypeStruct((s, h * B_HD), BF16), jax.ShapeDtypeStruct((h, 1, s), F32)],
        compiler_params=_params(("parallel", "parallel")),
    )(bounds, qf, kb, vt4)


def _fox_bwd(qf, dow, lse_row, delta_row, kb, kst4, vb, bounds, t):
    h, s, w = qf.shape
    nt = s // t
    nsub = t // FOX_SUB
    nh = FOX_HEADS_PER_STEP

    def body(b_ref, q_ref, do_ref, lse_ref, dl_ref, k_ref, kt_ref, v_ref, dqt_ref, dk_ref, dv_ref, dk_acc, dv_acc):
        j = pl.program_id(1)

        def alive_after(hh):
            head = pl.program_id(0) * nh + hh
            top = b_ref[head, 2 * nt] - b_ref[head, nt + j]
            return lax.fori_loop(
                j + 1, nt, lambda ii, n: n + (top + b_ref[head, ii] >= FOX_DEAD).astype(jnp.int32), 0)

        i_hi = j + 1 + functools.reduce(jnp.maximum, [alive_after(hh) for hh in range(nh)])

        @pl.when(j == 0)
        def _():
            dqt_ref[...] = jnp.zeros_like(dqt_ref)

        dk_acc[...] = jnp.zeros_like(dk_acc)
        dv_acc[...] = jnp.zeros_like(dv_acc)
        krow = lax.broadcasted_iota(jnp.int32, (FOX_SUB, t), 0)
        qcol = lax.broadcasted_iota(jnp.int32, (FOX_SUB, t), 1)
        subs = [slice(b * FOX_SUB, (b + 1) * FOX_SUB) for b in range(nsub)]

        def tile(i, diag):
            i0 = pl.multiple_of(i * t, t)
            for hh in range(nh):
                qi, doi = q_ref[hh, pl.ds(i0, t), :], do_ref[hh, pl.ds(i0, t), :]
                lse, dl = lse_ref[hh, i], dl_ref[hh, i]
                los = [b * FOX_SUB if diag else 0 for b in range(nsub)]
                sts = [_dot(k_ref[hh, rs, :], qi[lo:, :], NT) for rs, lo in zip(subs, los)]
                dps = [_dot(v_ref[hh, rs, :], doi[lo:, :], NT) for rs, lo in zip(subs, los)]
                dq = None
                for b, (rs, lo) in enumerate(zip(subs, los)):
                    st = sts[b] - lse[:, lo:]
                    if diag:
                        st = jnp.where(krow[:, :t - lo] <= qcol[:, :t - lo], st, NEG)
                    pt = jnp.exp(st)
                    dsb = (pt * (dps[b] - dl[:, lo:])).astype(BF16)
                    dv_acc[hh, rs, :] += _dot(pt.astype(BF16), doi[lo:, :])
                    dk_acc[hh, rs, :] += _dot(dsb, qi[lo:, :])
                    part = _dot(kt_ref[hh, :, rs], dsb)
                    if lo:
                        part = jnp.concatenate([jnp.zeros((w, lo), F32), part], axis=1)
                    dq = part if dq is None else dq + part
                dqt_ref[hh, i] += dq

        def step(i, carry):
            tile(i, False)
            return carry

        tile(j, True)
        lax.fori_loop(j + 1, i_hi, step, 0)
        dk_ref[...] = dk_acc[...] * (B_HD ** -0.5)
        dv_ref[...] = dv_acc[...]

    full = pl.BlockSpec((nh, s, w), lambda hh, j: (hh, 0, 0))
    rowst = pl.BlockSpec((nh, nt, 1, t), lambda hh, j: (hh, 0, 0, 0))
    tl = pl.BlockSpec((nh, t, w), lambda hh, j: (hh, j, 0))
    return pl.pallas_call(
        body, name="fox_bwd", grid=(h // nh, nt),
        in_specs=[SMEM_SPEC, full, full, rowst, rowst, tl,
                  pl.BlockSpec((nh, None, w, t), lambda hh, j: (hh, j, 0, 0)), tl],
        out_specs=[pl.BlockSpec((nh, nt, w, t), lambda hh, j: (hh, 0, 0, 0)), tl, tl],
        out_shape=[jax.ShapeDtypeStruct((h, nt, w, t), F32), jax.ShapeDtypeStruct((h, s, w), F32),
                   jax.ShapeDtypeStruct((h, s, w), F32)],
        scratch_shapes=[pltpu.VMEM((nh, t, w), F32), pltpu.VMEM((nh, t, w), F32)],
        compiler_params=_params(("parallel", "arbitrary")),
    )(bounds, qf, dow, lse_row, delta_row, kb, kst4, vb)


def _mem_fwd(u, mkv, tq=1024):
    s = u.shape[0]
    scale = HD ** -0.5

    def body(q_ref, mk_ref, mv_ref, o_ref, lse_ref):
        lses = []
        for h in range(4):
            cs = slice(h * HD, (h + 1) * HD)
            sc = _dot(q_ref[:, cs], mk_ref[:, cs], NT) * scale
            m = jnp.max(sc, axis=-1, keepdims=True)
            p = jnp.exp(sc - m)
            den = jnp.sum(p, axis=-1, keepdims=True)
            o_ref[:, cs] = (_dot(p.astype(BF16), mv_ref[:, cs]) / den).astype(o_ref.dtype)
            lses.append(m + jnp.log(den))
        lse_ref[...] = _lane_pack(lses, (tq, HD))

    return pl.pallas_call(
        body, name="mem_fwd", grid=(s // tq,),
        in_specs=[pl.BlockSpec((tq, 512), lambda i: (i, C_QM // 512)),
                  pl.BlockSpec((N_MEM, 512), lambda i: (0, 0)),
                  pl.BlockSpec((N_MEM, 512), lambda i: (0, 1))],
        out_specs=[pl.BlockSpec((tq, 512), lambda i: (i, 0)), pl.BlockSpec((tq, HD), lambda i: (i, 0))],
        out_shape=[jax.ShapeDtypeStruct((s, 512), BF16), jax.ShapeDtypeStruct((s, HD), F32)],
        compiler_params=_params(("parallel",)),
    )(u, mkv, mkv)


def _mem_bwd(u, mkv, o, do, lse, du, tq=1024):
    s = u.shape[0]
    scale = HD ** -0.5

    def body(q_ref, mk_ref, mv_ref, o_ref, do_ref, lse_ref, _, dq_ref, dmk_ref, dmv_ref):
        @pl.when(pl.program_id(0) == 0)
        def _():
            dmk_ref[...] = jnp.zeros_like(dmk_ref)
            dmv_ref[...] = jnp.zeros_like(dmv_ref)

        for h in range(4):
            cs = slice(h * HD, (h + 1) * HD)
            qv, dov = q_ref[:, cs], do_ref[:, cs]
            sc = _dot(qv, mk_ref[:, cs], NT) * scale
            p = jnp.exp(sc - lse_ref[:, h:h + 1])
            delta = jnp.sum(dov.astype(F32) * o_ref[:, cs].astype(F32), axis=-1, keepdims=True)
            ds = p * (_dot(dov, mv_ref[:, cs], NT) - delta)
            dsb = ds.astype(BF16)
            dq_ref[:, cs] = (_dot(dsb, mk_ref[:, cs]) * scale).astype(dq_ref.dtype)
            dmk_ref[:, cs] += _dot(dsb, qv, TN) * scale
            dmv_ref[:, cs] += _dot(p.astype(BF16), dov, TN)

    row = pl.BlockSpec((tq, 512), lambda i: (i, 0))
    acc = pl.BlockSpec((N_MEM, 512), lambda i: (0, 0))
    return pl.pallas_call(
        body, name="mem_bwd", grid=(s // tq,),
        in_specs=[pl.BlockSpec((tq, 512), lambda i: (i, C_QM // 512)),
                  pl.BlockSpec((N_MEM, 512), lambda i: (0, 0)),
                  pl.BlockSpec((N_MEM, 512), lambda i: (0, 1)),
                  row, row, pl.BlockSpec((tq, HD), lambda i: (i, 0)), pl.BlockSpec(memory_space=pl.ANY)],
        out_specs=[pl.BlockSpec((tq, 512), lambda i: (i, C_QM // 512)), acc, acc],
        out_shape=[jax.ShapeDtypeStruct(du.shape, du.dtype), jax.ShapeDtypeStruct((N_MEM, 512), F32),
                   jax.ShapeDtypeStruct((N_MEM, 512), F32)],
        input_output_aliases={6: 0},
        compiler_params=_params(("arbitrary",)),
    )(u, mkv, mkv, o, do, lse, du)


def _branch_merge(u, yg_a, y_b, y_m, w_ba, w_bb, w_bm, bm3, tm=512):
    s = u.shape[0]
    tm = min(tm, s)

    def body(yga_ref, yb_ref, ym_ref, zb_ref, zm_ref, g0_ref, g1_ref, g2_ref, wa_ref, wb_ref, wm_ref, bm_ref,
             ygb_ref, ygm_ref, bra_ref, brb_ref, brm_ref, mg_ref):
        def gated(y_ref, z_ref):
            zf = z_ref[...].astype(F32)
            return (y_ref[...].astype(F32) * (zf * _sig(zf))).astype(BF16)

        ygb, ygm = gated(yb_ref, zb_ref), gated(ym_ref, zm_ref)
        ygb_ref[...] = ygb
        ygm_ref[...] = ygm
        tot = None
        for i, (yg, w_ref, g_ref, br_ref) in enumerate(((yga_ref[...], wa_ref, g0_ref, bra_ref),
                                                        (ygb, wb_ref, g1_ref, brb_ref),
                                                        (ygm, wm_ref, g2_ref, brm_ref))):
            br = _dot(yg, w_ref[...]).astype(BF16)
            br_ref[...] = br
            term = _sig(g_ref[...].astype(F32) + bm_ref[i:i + 1, :]) * br.astype(F32)
            tot = term if tot is None else tot + term
        mg_ref[...] = tot.astype(BF16)

    def cols(width, col0):
        return pl.BlockSpec((tm, width), functools.partial(lambda i, cb: (i, cb), cb=col0 // width))

    half, full = cols(512, 0), cols(D_MODEL, 0)
    whole = pl.BlockSpec(w_ba.shape, lambda i: (0, 0))
    return pl.pallas_call(
        body, name="branch_merge", grid=(s // tm,),
        in_specs=[half, half, half, cols(512, C_ZB), cols(512, C_ZM)]
        + [cols(D_MODEL, C_GL + k * D_MODEL) for k in range(3)]
        + [whole, whole, whole, pl.BlockSpec(bm3.shape, lambda i: (0, 0))],
        out_specs=[half, half, full, full, full, full],
        out_shape=[jax.ShapeDtypeStruct((s, 512), BF16)] * 2 + [jax.ShapeDtypeStruct((s, D_MODEL), BF16)] * 4,
        compiler_params=_params(("parallel",)),
    )(yg_a, y_b, y_m, u, u, u, u, u, w_ba, w_bb, w_bm, bm3)


FB_CHIP = FB_ORIG // SHARD_COLS
FB_AT = FB_ORIG - FB_CHIP * SHARD_COLS


def _chip_slabs(main, fb):
    cuts = [SHARD_COLS * p - (B_HEADS if p > FB_CHIP else 0) for p in range(N_CHIPS + 1)]
    slabs = [main[:, a:b] for a, b in zip(cuts[:-1], cuts[1:])]
    own = slabs[FB_CHIP]
    slabs[FB_CHIP] = jnp.concatenate([own[:, :FB_AT], fb, own[:, FB_AT:]], axis=1)
    return slabs


def _split_forget(slabs):
    own = slabs[FB_CHIP]
    parts = list(slabs[:FB_CHIP]) + [own[:, :FB_AT], own[:, FB_AT + B_HEADS:]] + list(slabs[FB_CHIP + 1:])
    return jnp.concatenate(parts, axis=1), own[:, FB_AT:FB_AT + B_HEADS]


def _local_step(x, mem, pos, target, g_pre, g_post, g_mem, w_main, w_fb, b_forget, b_merge,
                w_mem_kv, w_ba, w_bb, w_bm, w_out, exchange=None):
    s = x.shape[0]
    t_fox = min(512, s)
    nt = s // t_fox
    half = ROT_DIM // 2
    inv = ROPE_THETA ** (-jnp.arange(half, dtype=F32) / half)
    inv128 = jnp.concatenate([inv, inv, jnp.zeros((HD - ROT_DIM,), F32)]).reshape(1, HD)

    h, h_t, ufb = _rms_fwd_both("norm_pre", x, g_pre, w_fb)
    u = _mm("proj_in", h, w_main, "nn", BF16, tm=4096)
    memn = _rms_fwd("norm_mem", mem, g_mem)
    mkv = _mm("proj_mem", memn, w_mem_kv, "nn", BF16)

    qkv = _rope_fwd(u, pos, inv128)
    views = [tuple(qkv[3 * g:3 * g + 3]) for g in range(3)]
    os_, lses = [], []
    for g, d in enumerate(DILATIONS):
        o_g, lse_g = _band_fwd("band_fwd%d" % g, *views[g], d)
        os_.append((o_g, d * A_GROUP, 0, d))
        lses.append((lse_g, d * HD, 0, d))

    def merge_a(o1, o2, o3, l1, l2, l3, za, *scr):
        o1, o2, o3 = [_from_class(o, scr, d) for o, d in zip((o1, o2, o3), DILATIONS)]
        l1, l2, l3 = [_from_class(lv, scr, d) for lv, d in zip((l1, l2, l3), DILATIONS)]
        ys, tots = [], []
        for hh in range(4):
            cs, hs = slice(hh * HD, (hh + 1) * HD), slice(hh, hh + 1)
            mx = jnp.maximum(jnp.maximum(l1[:, hs], l2[:, hs]), l3[:, hs])
            e1, e2, e3 = jnp.exp(l1[:, hs] - mx), jnp.exp(l2[:, hs] - mx), jnp.exp(l3[:, hs] - mx)
            den = e1 + e2 + e3
            ys.append((e1 * o1[:, cs] + e2 * o2[:, cs] + e3 * o3[:, cs]) / den)
            tots.append(mx + jnp.log(den))
        y = jnp.concatenate(ys, axis=1)
        zf = za.astype(F32)
        tot = _lane_pack(tots, l1.shape)
        return (y, y * (zf * _sig(zf))) + tuple(_to_class(tot, scr, d) for d in DILATIONS)

    res = _rows("merge_a", merge_a, os_ + lses + [(u, 512, C_ZA // 512)], [],
                [(512, BF16), (512, BF16)] + [(d * HD, F32, d) for d in DILATIONS], tm=ROPE_TM,
                scratch=_class_scratch(ROPE_TM))
    y_a, yg_a, lse_a = res[0], res[1], res[2:5]

    zrow = ufb[:, :B_HEADS].T
    c = _fox_prep(zrow, b_forget.reshape(B_HEADS, 1))
    qf, kb, kst4, vb, vt4, q2, k2 = _fox_pack(u, c.reshape(B_HEADS, s, 1), t_fox)
    bounds = _fox_bounds(q2, k2, c, t_fox)
    y_b, lse_b = _fox_fwd(qf, kb, vt4, bounds, t_fox)

    y_m, lse_m = _mem_fwd(u, mkv)

    gl = [(u, 1024, C_GL // 1024 + i) for i in range(3)]
    bm3 = b_merge.reshape(3, D_MODEL)
    yg_b, yg_m, br_a, br_b, br_m, merged = _branch_merge(u, yg_a, y_b, y_m, w_ba, w_bb, w_bm, bm3)
    out = _mm("proj_out", merged, w_out, "nn", F32)

    def tail(xv, ov, tv, gv):
        r = lax.rsqrt(jnp.mean(ov * ov, axis=-1, keepdims=True) + EPS)
        n = ov * r
        err = xv + n * gv - tv
        dy = err * (1.0 / D_MODEL)
        dn = dy * gv
        dout = r * (dn - n * jnp.mean(dn * n, axis=-1, keepdims=True))
        return (dy, dout, jnp.sum(0.5 * err * err * (1.0 / D_MODEL), axis=0, keepdims=True),
                jnp.sum(dy * n, axis=0, keepdims=True))

    dy, dout, loss_lanes, g_post_grad = _rows(
        "tail", tail, [x, out, target], [g_post], [(D_MODEL, F32), (D_MODEL, BF16)],
        reds=[D_MODEL, D_MODEL], tm=512)

    dmerged = _mm("d_merged", dout, w_out, "nt", BF16)
    gw_out = _mm("g_w_out", merged, dout, "tn", F32, tk=2048)

    def merge_bwd(dm, g0, g1, g2, b0, b1, b2, bm):
        dmf = dm.astype(F32)
        dbs, dgs, sums = [], [], []
        for i, (gv, bv) in enumerate(((g0, b0), (g1, b1), (g2, b2))):
            sg = _sig(gv.astype(F32) + bm[i:i + 1, :])
            dbs.append(dmf * sg)
            dg = dmf * bv.astype(F32) * sg * (1.0 - sg)
            dgs.append(dg)
            sums.append(jnp.sum(dg, axis=0, keepdims=True))
        return tuple(dbs + [jnp.concatenate(dgs, axis=1)] + sums)

    du = lax.empty(u.shape, BF16)
    res = _rows("merge_bwd", merge_bwd, [dmerged] + gl + [br_a, br_b, br_m], [bm3],
                [(D_MODEL, BF16)] * 3 + [(3 * D_MODEL, BF16)], reds=[D_MODEL] * 3, tm=512,
                into=(du, 3, ("column", C_GL)))
    dbr, du, g_bmerge = res[0:3], res[3], jnp.concatenate(res[4:7], axis=1)

    dyg, gw_branch = [], []
    for nm, dbv, wv, ygv in (("a", dbr[0], w_ba, yg_a), ("b", dbr[1], w_bb, yg_b), ("m", dbr[2], w_bm, yg_m)):
        dyg.append(_mm("d_yg_" + nm, dbv, wv, "nt", BF16))
        gw_branch.append(_mm("g_w_branch_" + nm, ygv, dbv, "tn", F32, tk=2048))

    def gate_bwd(dg, y, z):
        dgf, yf, zf = dg.astype(F32), y.astype(F32), z.astype(F32)
        sg = _sig(zf)
        return dgf * (zf * sg), dgf * yf * (sg * (1.0 + zf * (1.0 - sg)))

    def gate_bwd_a(dg, y, z, *scr):
        dyv, dz = gate_bwd(dg, y, z)
        prod = dyv * y.astype(F32)
        dl = [jnp.sum(prod[:, hh * HD:(hh + 1) * HD], axis=-1, keepdims=True) for hh in range(4)]
        delta = _lane_pack(dl, (dg.shape[0], HD))
        return ((dz,) + tuple(_to_class(dyv, scr, d) for d in DILATIONS)
                + tuple(_to_class(delta, scr, d) for d in DILATIONS))

    res = _rows("gate_bwd_a", gate_bwd_a, [dyg[0], y_a, (u, 512, C_ZA // 512)], [],
                [(512, BF16)] + [(d * A_GROUP, BF16, d) for d in DILATIONS] + [(d * HD, F32, d) for d in DILATIONS],
                tm=ROPE_TM, scratch=_class_scratch(ROPE_TM), into=(du, 0, C_ZA // 512))
    du, dy_a, delta_a = res[0], res[1:4], res[4:7]
    dy_b, du = _rows("gate_bwd_b", gate_bwd, [dyg[1], y_b, (u, 512, C_ZB // 512)], [],
                     [(512, BF16), (512, BF16)], into=(du, 1, C_ZB // 512))
    dy_m, du = _rows("gate_bwd_m", gate_bwd, [dyg[2], y_m, (u, 512, C_ZM // 512)], [],
                     [(512, BF16), (512, BF16)], into=(du, 1, C_ZM // 512))

    du, dmk, dmv = _mem_bwd(u, mkv, y_m, dy_m, lse_m, du)
    dmkv = jnp.concatenate([dmk, dmv], axis=1)
    gw_mem_kv = _mm("g_w_mem_kv", memn, dmkv, "tn", F32)
    dmemn = _mm("d_memn", dmkv, w_mem_kv, "nt", F32)

    def mem_gain_grad(mv, dv):
        r = lax.rsqrt(jnp.mean(mv * mv, axis=-1, keepdims=True) + EPS)
        return (jnp.sum(dv * mv * r, axis=0, keepdims=True),)

    g_mem_grad = _rows("g_norm_mem", mem_gain_grad, [mem, dmemn], [], [], reds=[D_MODEL], tm=N_MEM)[0]

    dow, delta_b = _fox_pack_bwd(dy_b, y_b, t_fox)
    dqt, dkw, dvw = _fox_bwd(qf, dow, lse_b.reshape(B_HEADS, nt, 1, t_fox), delta_b, kb, kst4, vb, bounds, t_fox)
    du, dc = _fox_unpack(dqt, dkw, dvw, du, t_fox)
    dzrow, g_bforget = _fox_prep_bwd(dc.reshape(B_HEADS, s), zrow, b_forget.reshape(B_HEADS, 1))
    dfb = jnp.zeros((s, HD), BF16).at[:, :B_HEADS].set(dzrow.T.astype(BF16))

    dqs, dks, dvs = [], [], []
    for g, d in enumerate(DILATIONS):
        qv, kv, vv = views[g]
        dqs.append(_band_dq("band_dq%d" % g, qv, kv, vv, dy_a[g], lse_a[g], delta_a[g], d))
        dk_g, dv_g = _band_dkv("band_dkv%d" % g, qv, kv, vv, dy_a[g], lse_a[g], delta_a[g], d)
        dks.append(dk_g)
        dvs.append(dv_g)
    du = _rope_bwd(dqs, dks, dvs, pos, inv128, du)

    gw_main = _mm("g_w_main", h_t, du, "nn", F32, tk=2048)
    gw_fb = _mm("g_w_fb", h, dfb, "tn", F32)
    grads = dict(norm_post_g=g_post_grad, norm_mem_g=g_mem_grad, w_in=_chip_slabs(gw_main, gw_fb[:, :B_HEADS]),
                 b_forget=g_bforget.reshape(1, B_HEADS), b_merge=g_bmerge, w_mem_kv=gw_mem_kv,
                 w_branch_a=gw_branch[0], w_branch_b=gw_branch[1], w_branch_m=gw_branch[2], w_out=gw_out)
    side = exchange(grads) if exchange else None
    dh_main = _mm("d_h", du, w_main, "nt", F32, tm=1024, tk=2816, side=side)
    landed = None
    if side:
        dh_main, landed = dh_main[0], dh_main[1:]
    def pre_bwd(xv, d1, dfbv, dyv, gv, wfb):
        r = lax.rsqrt(jnp.mean(xv * xv, axis=-1, keepdims=True) + EPS)
        n = xv * r
        dhv = d1 + _dot(dfbv, wfb, NT)
        dn = dhv * gv
        dx = r * (dn - n * jnp.mean(dn * n, axis=-1, keepdims=True))
        return dyv + dx, jnp.sum(dhv * n, axis=0, keepdims=True)

    grad_x, g_pre_grad = _rows("norm_pre_bwd", pre_bwd, [x, dh_main, dfb, dy], [g_pre, w_fb],
                               [(D_MODEL, F32)], reds=[D_MODEL], tm=512)

    grads["norm_pre_g"] = g_pre_grad
    return loss_lanes, grad_x, grads, landed


HBM_SPEC = pl.BlockSpec(memory_space=pltpu.HBM)


def _place():
    x, y, c = lax.axis_index("x"), lax.axis_index("y"), lax.axis_index("c")
    chips = [(1 - x, y), (x, 1 - y), (1 - x, 1 - y)]
    return x, y, c, 2 * x + y, chips


N_CHUNKS = 4


def _units(parts, row_axis):
    units = []
    for i, a in enumerate(parts):
        ch = a.shape[row_axis] // N_CHUNKS
        units += [(i, pl.ds(k * ch, ch)) for k in range(N_CHUNKS)]
    return units


def _gather_weights(parts):
    n = len(parts)
    units = _units(parts, 1)
    nu = len(units)
    via_y = [(u % N_CHUNKS) < N_CHUNKS // 2 for u in range(nu)]

    def body(*refs):
        srcs, outs = refs[:n], refs[n:2 * n]
        send_sems, recv_sems = refs[2 * n:]
        x, y, c, p, _ = _place()
        me, sib = (x, y, c), (x, y, 1 - c)
        xn, yn, dg = (1 - x, y), (x, 1 - y), (1 - x, 1 - y)

        def cp(u, k, chip, half, to, from_src=False):
            i, rs = units[u]
            dst = outs[i].at[2 * chip[0] + chip[1], half, rs]
            return pltpu.make_async_remote_copy(
                src_ref=srcs[i].at[half, rs] if from_src else dst, dst_ref=dst, send_sem=send_sems.at[u, k],
                recv_sem=recv_sems.at[u, k], device_id=to, device_id_type=MESH)

        sent = []

        def go(copy):
            copy.start()
            sent.append(copy)

        for u in range(nu):
            go(cp(u, 0, (x, y), c, (*xn, c), from_src=True))
            go(cp(u, 1, (x, y), c, (*yn, c), from_src=True))
        for u in range(nu):
            cp(u, 0, xn, c, me).wait_recv()
            go(cp(u, 4, xn, c, sib))
            if via_y[u]:
                go(cp(u, 2, xn, c, (*yn, c)))
            cp(u, 1, yn, c, me).wait_recv()
            go(cp(u, 5, yn, c, sib))
            if not via_y[u]:
                go(cp(u, 3, yn, c, (*xn, c)))
        for u in range(nu):
            cp(u, 2 if via_y[u] else 3, dg, c, me).wait_recv()
            go(cp(u, 6, dg, c, sib))
        for u in range(nu):
            for k, chip in ((4, xn), (5, yn), (6, dg)):
                cp(u, k, chip, 1 - c, me).wait_recv()
        for copy in sent:
            copy.wait_send()

    return pl.pallas_call(
        body, name="gather_weights", in_specs=[HBM_SPEC] * n, out_specs=[HBM_SPEC] * n,
        out_shape=[jax.ShapeDtypeStruct((N_CHIPS,) + a.shape, a.dtype) for a in parts],
        scratch_shapes=[pltpu.SemaphoreType.DMA((nu, 7)), pltpu.SemaphoreType.DMA((nu, 7))],
    )(*parts)


def _swap_with_sibling(parts):
    n = len(parts)
    units = _units(parts, 2)

    def body(*refs):
        srcs, outs = refs[:n], refs[n:2 * n]
        send_sems, recv_sems = refs[2 * n:]
        x, y, c, _, _ = _place()
        cps = [pltpu.make_async_remote_copy(
            src_ref=srcs[i].at[q, 1 - c, rs], dst_ref=outs[i].at[q, rs], send_sem=send_sems.at[u, q],
            recv_sem=recv_sems.at[u, q], device_id=(x, y, 1 - c), device_id_type=MESH)
            for q in range(N_CHIPS) for u, (i, rs) in enumerate(units)]
        for cpy in cps:
            cpy.start()
        for cpy in cps:
            cpy.wait()

    return pl.pallas_call(
        body, name="swap_with_sibling", in_specs=[HBM_SPEC] * n, out_specs=[HBM_SPEC] * n,
        out_shape=[jax.ShapeDtypeStruct(a.shape[:1] + a.shape[2:], a.dtype) for a in parts],
        scratch_shapes=[pltpu.SemaphoreType.DMA((len(units), N_CHIPS)),
                        pltpu.SemaphoreType.DMA((len(units), N_CHIPS))],
    )(*parts)


def _scatter_to_owners(parts):
    n = len(parts)
    units = _units(parts, 1)

    def copies(srcs, outs, send_sems, recv_sems, incoming):
        x, y, c, p, chips = _place()
        return [pltpu.make_async_remote_copy(
            src_ref=srcs[i].at[2 * cx + cy, rs], dst_ref=outs[i].at[(2 * cx + cy) if incoming else p, rs],
            send_sem=send_sems.at[u, j], recv_sem=recv_sems.at[u, j], device_id=(cx, cy, c), device_id_type=MESH)
            for u, (i, rs) in enumerate(units) for j, (cx, cy) in enumerate(chips)]

    def start(ins, outs, scratch):
        for cpy in copies(ins, outs, *scratch, incoming=False):
            cpy.start()

    def wait(ins, outs, scratch):
        for cpy in copies(ins, outs, *scratch, incoming=True):
            cpy.wait_recv()
        for cpy in copies(ins, outs, *scratch, incoming=False):
            cpy.wait_send()

    return dict(ins=list(parts), outs=[jax.ShapeDtypeStruct(a.shape, a.dtype) for a in parts],
                scratch=[pltpu.SemaphoreType.DMA((len(units), 3)), pltpu.SemaphoreType.DMA((len(units), 3))],
                start=start, wait=wait)


def _share_with_sibling(parts):
    n = len(parts)
    units = _units(parts, 1)

    def body(*refs):
        srcs, outs = refs[:n], refs[n:2 * n]
        send_sems, recv_sems = refs[2 * n:]
        x, y, c, _, _ = _place()
        sends = [pltpu.make_async_remote_copy(
            src_ref=srcs[i].at[0, rs], dst_ref=outs[i].at[c, rs], send_sem=send_sems.at[u],
            recv_sem=recv_sems.at[u], device_id=(x, y, 1 - c), device_id_type=MESH)
            for u, (i, rs) in enumerate(units)]
        for cpy in sends:
            cpy.start()
        for u, (i, rs) in enumerate(units):
            pltpu.make_async_remote_copy(
                src_ref=srcs[i].at[0, rs], dst_ref=outs[i].at[1 - c, rs], send_sem=send_sems.at[u],
                recv_sem=recv_sems.at[u], device_id=(x, y, 1 - c), device_id_type=MESH).wait_recv()
        for cpy in sends:
            cpy.wait_send()

    return pl.pallas_call(
        body, name="share_with_sibling", in_specs=[HBM_SPEC] * n, out_specs=[HBM_SPEC] * n,
        out_shape=[jax.ShapeDtypeStruct((2,) + a.shape[1:], a.dtype) for a in parts],
        scratch_shapes=[pltpu.SemaphoreType.DMA((len(units),)), pltpu.SemaphoreType.DMA((len(units),))],
    )(*parts)


def _sum_small(v):
    def body(v_ref, out_ref, buf, send_sems, recv_sems):
        x, y, c, _, _ = _place()
        me = 4 * x + 2 * y + c
        buf[me] = v_ref[...]
        flips = [(dx, dy, dc) for dx in (0, 1) for dy in (0, 1) for dc in (0, 1)][1:]
        sends = []
        for k, (dx, dy, dc) in enumerate(flips):
            cpy = pltpu.make_async_remote_copy(
                src_ref=v_ref, dst_ref=buf.at[me], send_sem=send_sems.at[k], recv_sem=recv_sems.at[k],
                device_id=((x + dx) % 2, (y + dy) % 2, (c + dc) % 2), device_id_type=MESH)
            cpy.start()
            sends.append(cpy)
        for k, (dx, dy, dc) in enumerate(flips):
            px, py, pc = (x + dx) % 2, (y + dy) % 2, (c + dc) % 2
            pltpu.make_async_remote_copy(
                src_ref=v_ref, dst_ref=buf.at[4 * px + 2 * py + pc], send_sem=send_sems.at[k],
                recv_sem=recv_sems.at[k], device_id=(px, py, pc), device_id_type=MESH).wait_recv()
        for cpy in sends:
            cpy.wait_send()
        tot = buf[0]
        for i in range(1, N_DEV):
            tot = tot + buf[i]
        out_ref[...] = tot

    return pl.pallas_call(
        body, name="sum_small", out_shape=jax.ShapeDtypeStruct(v.shape, v.dtype),
        in_specs=[pl.BlockSpec(memory_space=pltpu.VMEM)], out_specs=pl.BlockSpec(memory_space=pltpu.VMEM),
        scratch_shapes=[pltpu.VMEM((N_DEV,) + v.shape, v.dtype), pltpu.SemaphoreType.DMA((N_DEV - 1,)),
                        pltpu.SemaphoreType.DMA((N_DEV - 1,))],
    )(v)


def _add_chips(name, landed, pair, chip):
    nq, r, w = landed.shape
    tr = 128 if r % 128 == 0 else 64

    def body(chip_ref, *refs):
        own = refs[nq][...].astype(F32)
        tot = None
        for q in range(nq):
            term = jnp.where(chip_ref[0] == q, own, refs[q][...].astype(F32))
            tot = term if tot is None else tot + term
        refs[nq + 1][...] = tot

    specs = [pl.BlockSpec((None, tr, w), functools.partial(lambda j, chip_ref, q: (q, j, 0), q=q)) for q in range(nq)]
    specs.append(pl.BlockSpec((None, tr, w), lambda j, chip_ref: (chip_ref[0], j, 0)))
    grid_spec = pltpu.PrefetchScalarGridSpec(
        num_scalar_prefetch=1, grid=(r // tr,), in_specs=specs,
        out_specs=pl.BlockSpec((None, tr, w), lambda j, chip_ref: (0, j, 0)))
    return pl.pallas_call(
        body, name=name, grid_spec=grid_spec, out_shape=jax.ShapeDtypeStruct((1, r, w), F32),
        compiler_params=_params(("parallel",)),
    )(jnp.reshape(chip, (1,)).astype(jnp.int32), *([landed] * nq), pair)


def _add_pair(name, halves, got, c):
    nq, _, r, w = halves.shape
    tr = 128 if r % 128 == 0 else 64

    def body(c_ref, a_ref, b_ref, o_ref):
        o_ref[...] = (a_ref[...] + b_ref[...]).astype(o_ref.dtype)

    grid_spec = pltpu.PrefetchScalarGridSpec(
        num_scalar_prefetch=1, grid=(nq, r // tr),
        in_specs=[pl.BlockSpec((None, None, tr, w), lambda i, j, c_ref: (i, c_ref[0], j, 0)),
                  pl.BlockSpec((None, tr, w), lambda i, j, c_ref: (i, j, 0))],
        out_specs=pl.BlockSpec((None, tr, w), lambda i, j, c_ref: (i, j, 0)))
    return pl.pallas_call(
        body, name=name, grid_spec=grid_spec, out_shape=jax.ShapeDtypeStruct((nq, r, w), BF16),
        compiler_params=_params(("parallel", "parallel")),
    )(jnp.reshape(c, (1,)).astype(jnp.int32), halves, got)


def _adamw(name, w, g, m, v, tm):
    def fn(wv, gv, mv, vv):
        m2 = ADAM_B1 * mv + (1.0 - ADAM_B1) * gv
        v2 = ADAM_B2 * vv + (1.0 - ADAM_B2) * (gv * gv)
        m_hat = m2 / (1.0 - ADAM_B1 ** ADAM_STEP)
        v_hat = v2 / (1.0 - ADAM_B2 ** ADAM_STEP)
        return -ADAM_LR * (m_hat / (jnp.sqrt(v_hat) + ADAM_EPS) + ADAM_WD * wv), m2, v2
    c = w.shape[1]
    return _rows(name, fn, [w, g, m, v], [], [(c, F32)] * 3, tm=tm)


REST_ROWS = 256 + 3 * 128 + 256
REST_SPLITS = (("w_mem_kv", 0, 256), ("w_branch_a", 256, 128), ("w_branch_b", 384, 128),
               ("w_branch_m", 512, 128), ("w_out", 640, 256))


def _rest_pack(t):
    return jnp.concatenate([t[n].reshape(rows, D_MODEL) for n, _, rows in REST_SPLITS], axis=0)


def _rest_unpack(a, shapes):
    return {n: a[r0:r0 + rows].reshape(shapes[n]) for n, r0, rows in REST_SPLITS}


def _small_pack(pre, post, memg, bforget, bmerge):
    pad = jnp.zeros((1, D_MODEL - B_HEADS), F32)
    return jnp.concatenate([pre, post, memg, bmerge.reshape(3, D_MODEL),
                            jnp.concatenate([bforget, pad], axis=1), jnp.zeros((1, D_MODEL), F32)], axis=0)


def _small_unpack(s8):
    return dict(norm_pre_g=s8[0:1], norm_post_g=s8[1:2], norm_mem_g=s8[2:3],
                b_merge=s8[3:6].reshape(1, 3 * D_MODEL), b_forget=s8[6:7, :B_HEADS])


WEIGHTS = ("norm_pre_g", "norm_post_g", "norm_mem_g", "w_in", "b_forget", "b_merge", "w_mem_kv",
           "w_branch_a", "w_branch_b", "w_branch_m", "w_out")
SMALL = ("norm_pre_g", "norm_post_g", "norm_mem_g", "b_forget", "b_merge")


def kernel(x, mem, positions, norm_pre_g, norm_post_g, norm_mem_g, w_in, b_forget, b_merge, w_mem_kv, w_branch_a, w_branch_b, w_branch_m, w_out, loss_target, m_norm_pre_g, m_norm_post_g, m_norm_mem_g, m_w_in, m_b_forget, m_b_merge, m_w_mem_kv, m_w_branch_a, m_w_branch_b, m_w_branch_m, m_w_out, v_norm_pre_g, v_norm_post_g, v_norm_mem_g, v_w_in, v_b_forget, v_b_merge, v_w_mem_kv, v_w_branch_a, v_w_branch_b, v_w_branch_m, v_w_out):
    w = dict(norm_pre_g=norm_pre_g, norm_post_g=norm_post_g, norm_mem_g=norm_mem_g, w_in=w_in[0],
             b_forget=b_forget, b_merge=b_merge, w_mem_kv=w_mem_kv[0], w_branch_a=w_branch_a[0],
             w_branch_b=w_branch_b[0], w_branch_m=w_branch_m[0], w_out=w_out[0])
    mo = dict(norm_pre_g=m_norm_pre_g, norm_post_g=m_norm_post_g, norm_mem_g=m_norm_mem_g, w_in=m_w_in[0],
              b_forget=m_b_forget, b_merge=m_b_merge, w_mem_kv=m_w_mem_kv[0], w_branch_a=m_w_branch_a[0],
              w_branch_b=m_w_branch_b[0], w_branch_m=m_w_branch_m[0], w_out=m_w_out[0])
    vo = dict(norm_pre_g=v_norm_pre_g, norm_post_g=v_norm_post_g, norm_mem_g=v_norm_mem_g, w_in=v_w_in[0],
              b_forget=v_b_forget, b_merge=v_b_merge, w_mem_kv=v_w_mem_kv[0], w_branch_a=v_w_branch_a[0],
              w_branch_b=v_w_branch_b[0], w_branch_m=v_w_branch_m[0], w_out=v_w_out[0])
    s = x.shape[1]
    c = lax.axis_index("c")

    chip = 2 * lax.axis_index("x") + lax.axis_index("y")

    def put(whole, own, slot):
        return lax.dynamic_update_index_in_dim(whole, own.astype(whole.dtype), slot, 0)

    own_w = [w["w_in"].astype(BF16).reshape(2, D_MODEL // 2, SHARD_COLS),
             _rest_pack(w).astype(BF16).reshape(2, REST_ROWS // 2, D_MODEL)]
    all_in, all_rest = _gather_weights(own_w)
    all_in = all_in.reshape(N_CHIPS, D_MODEL, SHARD_COLS)
    own_in, own_rest = own_w[0].reshape(D_MODEL, SHARD_COLS), own_w[1].reshape(REST_ROWS, D_MODEL)
    w_main, w_fb = _split_forget([jnp.where(chip == p, own_in, all_in[p]) for p in range(N_CHIPS)])
    w_fb = jnp.concatenate([w_fb, jnp.zeros((D_MODEL, HD - B_HEADS), BF16)], axis=1)
    all_rest = all_rest.reshape(N_CHIPS, REST_ROWS, D_MODEL)
    all_rest = jnp.stack([jnp.where(chip == p, own_rest, all_rest[p]) for p in range(N_CHIPS)])
    w_kv_f = all_rest[:, 0:256].reshape(D_MODEL, D_MODEL)
    w_br_f = [all_rest[:, 256 + 128 * i:384 + 128 * i].reshape(N_CHIPS, 512, 256).transpose(1, 0, 2)
              .reshape(512, D_MODEL) for i in range(3)]
    w_out_f = all_rest[:, 640:896].reshape(D_MODEL, D_MODEL)

    pair = []

    def exchange(g):
        def per_chip(name, p):
            a = g[name]
            if name in ("w_mem_kv", "w_out"):
                return a[256 * p:256 * (p + 1)]
            return a[:, 256 * p:256 * (p + 1)]

        in4 = jnp.stack(g["w_in"])
        rest4 = jnp.stack([_rest_pack({n: per_chip(n, p) for n, _, _ in REST_SPLITS}) for p in range(N_CHIPS)])
        halves = [in4.reshape(N_CHIPS, 2, D_MODEL // 2, SHARD_COLS),
                  rest4.reshape(N_CHIPS, 2, REST_ROWS // 2, D_MODEL)]
        got = _swap_with_sibling(halves)
        pair.extend(_add_pair("add_pair_%d" % i, halves[i], got[i], c) for i in range(2))
        return _scatter_to_owners(pair)

    loss_lanes, grad_x, g, landed = _local_step(
        x[0], mem[0], positions.reshape(s, 1), loss_target[0], norm_pre_g, norm_post_g, norm_mem_g,
        w_main, w_fb, b_forget, b_merge, w_kv_f, w_br_f[0], w_br_f[1], w_br_f[2], w_out_f, exchange)
    loss = lax.psum(jnp.sum(loss_lanes), ("x", "y", "c"))
    half = [_add_chips("add_chips_%d" % i, landed[i], pair[i], chip) for i in range(2)]
    red_in, red_rest = [put(a, o[0], c) for a, o in zip(_share_with_sibling(half), half)]
    gs = {"w_in": red_in.reshape(D_MODEL, SHARD_COLS)}
    gs.update(_rest_unpack(red_rest.reshape(REST_ROWS, D_MODEL), {n: w[n].shape for n, _, _ in REST_SPLITS}))
    gs.update(_small_unpack(_sum_small(_small_pack(
        g["norm_pre_g"], g["norm_post_g"], g["norm_mem_g"], g["b_forget"], g["b_merge"]))))

    delta, new_m, new_v = {}, {}, {}
    for n, tm in (("w_in", 128), ("w_mem_kv", 256), ("w_branch_a", 512), ("w_branch_b", 512),
                  ("w_branch_m", 512), ("w_out", 256)):
        d_, m_, v_ = _adamw("adamw_" + n, w[n], gs[n], mo[n], vo[n], tm)
        delta[n], new_m[n], new_v[n] = d_[None], m_[None], v_[None]
        gs[n] = gs[n][None]
    packs = [_small_pack(*[t[n] for n in SMALL])
             for t in (w, gs, mo, vo)]
    for res, store in zip(_adamw("adamw_small", *packs, 8), (delta, new_m, new_v)):
        store.update(_small_unpack(res))

    return (loss, grad_x[None], *[gs[n] for n in WEIGHTS], *[delta[n] for n in WEIGHTS],
            *[new_m[n] for n in WEIGHTS], *[new_v[n] for n in WEIGHTS])
```
